```python
import jax, jax.numpy as jnp
from jax import lax
import numpy as np

D_MODEL = 1024
BATCH = 8
SEQ = 8192
DEPTH = 1

CHUNK = 64
D_MIX = D_MODEL
HEAD_DIM = 64
N_HEADS_A = 8
N_HEADS_B = 8
D_A = N_HEADS_A * HEAD_DIM
D_B = N_HEADS_B * HEAD_DIM
SG_BLOCK = 128
Q_BLOCK = 128
D_FF = 2816
CONV_W = 3
EPS = 1e-6
D_IN = 2 * D_A + 3 * D_B + N_HEADS_B

kernel_name = "hymba_gmlp_fox_convffn"


def rms_norm(x, g):
    xf = x.astype(jnp.float32)
    y = xf * lax.rsqrt(jnp.mean(xf * xf, axis=-1, keepdims=True) + EPS)
    return (y * g.astype(jnp.float32)).astype(x.dtype)


def spatial_gating(u, v, ln_g, w_s, b_s):
    B, S, _ = u.shape
    n = S // SG_BLOCK
    v = v.reshape(B, n, SG_BLOCK, N_HEADS_A, HEAD_DIM)
    vf = v.astype(jnp.float32)
    mu = jnp.mean(vf, axis=-1, keepdims=True)
    var = jnp.mean(jnp.square(vf - mu), axis=-1, keepdims=True)
    vn = ((vf - mu) * lax.rsqrt(var + EPS) * ln_g.astype(jnp.float32)).astype(u.dtype)
    pos_chunk = jnp.arange(SG_BLOCK) // CHUNK
    mask = pos_chunk[:, None] >= pos_chunk[None, :]
    w = jnp.where(mask[None], w_s, 0).astype(u.dtype)
    mixed = jnp.einsum('hts,bnshd->bnthd', w, vn) + b_s.T.astype(u.dtype)[None, None, :, :, None]
    out = u.reshape(B, n, SG_BLOCK, N_HEADS_A, HEAD_DIM) * mixed
    return out.reshape(B, S, D_A)


def forgetting_attention(q, k, v, f_logit):
    B, S, H, Dh = q.shape
    n = S // Q_BLOCK
    scale = Dh ** -0.5
    c = jnp.cumsum(jax.nn.log_sigmoid(f_logit.astype(jnp.float32)), axis=1)
    c = c.transpose(0, 2, 1)
    qb = q.reshape(B, n, Q_BLOCK, H, Dh).transpose(1, 0, 3, 2, 4)
    cb = c.reshape(B, H, n, Q_BLOCK).transpose(2, 0, 1, 3)
    kpos = jnp.arange(S)

    def block(args):
        i, qi, ci = args
        s = jnp.einsum('bhtd,bshd->bhts', qi, k, preferred_element_type=jnp.float32) * scale
        s = s + ci[..., :, None] - c[:, :, None, :]
        qpos = i * Q_BLOCK + jnp.arange(Q_BLOCK)
        s = jnp.where(kpos[None, :] <= qpos[:, None], s, -jnp.inf)
        p = jax.nn.softmax(s, axis=-1)
        return jnp.einsum('bhts,bshd->bthd', p.astype(v.dtype), v)

    out = lax.map(block, (jnp.arange(n), qb, cb))
    return out.transpose(1, 0, 2, 3, 4).reshape(B, S, H * Dh)


def conv_ffn(h, w_up, w_conv, b_conv, w_down):
    a = h @ w_up
    C = a.shape[-1]
    a = lax.conv_general_dilated(
        a, w_conv[:, None, :].astype(a.dtype), window_strides=(1,),
        padding=[(CONV_W - 1, 0)], dimension_numbers=('NWC', 'WIO', 'NWC'),
        feature_group_count=C) + b_conv.astype(a.dtype)
    g, val = jnp.split(a, 2, axis=-1)
    return (jax.nn.silu(g) * val) @ w_down


def _fwd_setup_inputs(seed: int = 0) -> dict:
    key = jax.random.key(seed)
    ks = jax.random.split(key, 14)
    L = DEPTH
    nrm = jax.random.normal
    return {
        "x": nrm(ks[0], (BATCH, SEQ, D_MODEL), jnp.float32),
        "norm_mix_g": 1.0 + 0.02 * nrm(ks[1], (L, D_MODEL), jnp.float32),
        "w_in": nrm(ks[2], (L, D_MODEL, D_IN), jnp.float32) * D_MODEL ** -0.5,
        "f_bias": 3.0 + 0.5 * nrm(ks[3], (L, N_HEADS_B), jnp.float32),
        "sg_ln_g": 1.0 + 0.02 * nrm(ks[4], (L, N_HEADS_A, HEAD_DIM), jnp.float32),
        "sg_w": nrm(ks[5], (L, N_HEADS_A, SG_BLOCK, SG_BLOCK), jnp.float32) * SG_BLOCK ** -0.5,
        "sg_b": 1.0 + 0.1 * nrm(ks[6], (L, N_HEADS_A, SG_BLOCK), jnp.float32),
        "w_out": nrm(ks[7], (L, D_MIX, D_MODEL), jnp.float32) * D_MIX ** -0.5,
        "norm_ffn_g": 1.0 + 0.02 * nrm(ks[8], (L, D_MODEL), jnp.float32),
        "w_up": nrm(ks[9], (L, D_MODEL, 2 * D_FF), jnp.float32) * D_MODEL ** -0.5,
        "w_conv": nrm(ks[10], (L, CONV_W, 2 * D_FF), jnp.float32) * CONV_W ** -0.5,
        "b_conv": 0.02 * nrm(ks[11], (L, 2 * D_FF), jnp.float32),
        "w_down": nrm(ks[12], (L, D_FF, D_MODEL), jnp.float32) * D_FF ** -0.5,
        "norm_final_g": 1.0 + 0.02 * nrm(ks[13], (D_MODEL,), jnp.float32),
    }


def _fwd_reference(x, norm_mix_g, w_in, f_bias, sg_ln_g, sg_w, sg_b, w_out,
              norm_ffn_g, w_up, w_conv, b_conv, w_down, norm_final_g):
    B, S, _ = x.shape
    for l in range(DEPTH):
        h = rms_norm(x, norm_mix_g[l])
        z = h @ w_in[l]
        o = 0
        u_a = jax.nn.gelu(z[..., o:o + D_A], approximate=False); o += D_A
        v_a = jax.nn.gelu(z[..., o:o + D_A], approximate=False); o += D_A
        q_b = z[..., o:o + D_B].reshape(B, S, N_HEADS_B, HEAD_DIM); o += D_B
        k_b = z[..., o:o + D_B].reshape(B, S, N_HEADS_B, HEAD_DIM); o += D_B
        v_b = z[..., o:o + D_B].reshape(B, S, N_HEADS_B, HEAD_DIM); o += D_B
        f_logit = z[..., o:o + N_HEADS_B] + f_bias[l].astype(z.dtype)
        out_a = spatial_gating(u_a, v_a, sg_ln_g[l], sg_w[l], sg_b[l])
        out_b = forgetting_attention(q_b, k_b, v_b, f_logit)
        x = x + jnp.concatenate([out_a, out_b], axis=-1) @ w_out[l]
        x = x + conv_ffn(rms_norm(x, norm_ffn_g[l]), w_up[l], w_conv[l], b_conv[l], w_down[l])
    return rms_norm(x, norm_final_g)


import jax as _jax
import jax.numpy as _jnp

TWIN_FORMAT = 'train_step'
FWD_PARAMS = ['x', 'norm_mix_g', 'w_in', 'f_bias', 'sg_ln_g', 'sg_w', 'sg_b', 'w_out', 'norm_ffn_g', 'w_up', 'w_conv', 'b_conv', 'w_down', 'norm_final_g']
TWIN_WEIGHTS = ['norm_mix_g', 'w_in', 'f_bias', 'sg_ln_g', 'sg_w', 'sg_b', 'w_out', 'norm_ffn_g', 'w_up', 'w_conv', 'b_conv', 'w_down', 'norm_final_g']
TWIN_DIFF_INPUT = 'x'
TWIN_INPUTS = ['x', 'norm_mix_g', 'w_in', 'f_bias', 'sg_ln_g', 'sg_w', 'sg_b', 'w_out', 'norm_ffn_g', 'w_up', 'w_conv', 'b_conv', 'w_down', 'norm_final_g', 'loss_target', 'm_norm_mix_g', 'm_w_in', 'm_f_bias', 'm_sg_ln_g', 'm_sg_w', 'm_sg_b', 'm_w_out', 'm_norm_ffn_g', 'm_w_up', 'm_w_conv', 'm_b_conv', 'm_w_down', 'm_norm_final_g', 'v_norm_mix_g', 'v_w_in', 'v_f_bias', 'v_sg_ln_g', 'v_sg_w', 'v_sg_b', 'v_w_out', 'v_norm_ffn_g', 'v_w_up', 'v_w_conv', 'v_b_conv', 'v_w_down', 'v_norm_final_g']
TWIN_OUTPUTS = ['loss', 'grad_x', 'grad_norm_mix_g', 'grad_w_in', 'grad_f_bias', 'grad_sg_ln_g', 'grad_sg_w', 'grad_sg_b', 'grad_w_out', 'grad_norm_ffn_g', 'grad_w_up', 'grad_w_conv', 'grad_b_conv', 'grad_w_down', 'grad_norm_final_g', 'delta_norm_mix_g', 'delta_w_in', 'delta_f_bias', 'delta_sg_ln_g', 'delta_sg_w', 'delta_sg_b', 'delta_w_out', 'delta_norm_ffn_g', 'delta_w_up', 'delta_w_conv', 'delta_b_conv', 'delta_w_down', 'delta_norm_final_g', 'new_m_norm_mix_g', 'new_m_w_in', 'new_m_f_bias', 'new_m_sg_ln_g', 'new_m_sg_w', 'new_m_sg_b', 'new_m_w_out', 'new_m_norm_ffn_g', 'new_m_w_up', 'new_m_w_conv', 'new_m_b_conv', 'new_m_w_down', 'new_m_norm_final_g', 'new_v_norm_mix_g', 'new_v_w_in', 'new_v_f_bias', 'new_v_sg_ln_g', 'new_v_sg_w', 'new_v_sg_b', 'new_v_w_out', 'new_v_norm_ffn_g', 'new_v_w_up', 'new_v_w_conv', 'new_v_b_conv', 'new_v_w_down', 'new_v_norm_final_g']
TWIN_LEAF_KINDS = {'loss': 'loss', 'grad_x': 'grad_x', 'grad_norm_mix_g': 'grad_w', 'grad_w_in': 'grad_w', 'grad_f_bias': 'grad_w', 'grad_sg_ln_g': 'grad_w', 'grad_sg_w': 'grad_w', 'grad_sg_b': 'grad_w', 'grad_w_out': 'grad_w', 'grad_norm_ffn_g': 'grad_w', 'grad_w_up': 'grad_w', 'grad_w_conv': 'grad_w', 'grad_b_conv': 'grad_w', 'grad_w_down': 'grad_w', 'grad_norm_final_g': 'grad_w', 'delta_norm_mix_g': 'delta_w', 'delta_w_in': 'delta_w', 'delta_f_bias': 'delta_w', 'delta_sg_ln_g': 'delta_w', 'delta_sg_w': 'delta_w', 'delta_sg_b': 'delta_w', 'delta_w_out': 'delta_w', 'delta_norm_ffn_g': 'delta_w', 'delta_w_up': 'delta_w', 'delta_w_conv': 'delta_w', 'delta_b_conv': 'delta_w', 'delta_w_down': 'delta_w', 'delta_norm_final_g': 'delta_w', 'new_m_norm_mix_g': 'new_m', 'new_m_w_in': 'new_m', 'new_m_f_bias': 'new_m', 'new_m_sg_ln_g': 'new_m', 'new_m_sg_w': 'new_m', 'new_m_sg_b': 'new_m', 'new_m_w_out': 'new_m', 'new_m_norm_ffn_g': 'new_m', 'new_m_w_up': 'new_m', 'new_m_w_conv': 'new_m', 'new_m_b_conv': 'new_m', 'new_m_w_down': 'new_m', 'new_m_norm_final_g': 'new_m', 'new_v_norm_mix_g': 'new_v', 'new_v_w_in': 'new_v', 'new_v_f_bias': 'new_v', 'new_v_sg_ln_g': 'new_v', 'new_v_sg_w': 'new_v', 'new_v_sg_b': 'new_v', 'new_v_w_out': 'new_v', 'new_v_norm_ffn_g': 'new_v', 'new_v_w_up': 'new_v', 'new_v_w_conv': 'new_v', 'new_v_b_conv': 'new_v', 'new_v_w_down': 'new_v', 'new_v_norm_final_g': 'new_v'}


def _forward(args):
    return _fwd_reference(*[args[k] for k in FWD_PARAMS])


def _output_shape():
    def fwd():
        inp = _fwd_setup_inputs(0)
        return _fwd_reference(*[inp[k] for k in FWD_PARAMS])
    out = _jax.eval_shape(fwd)
    return out.shape, out.dtype

N_MICROBATCH = 1
ADAM_LR = 0.001
ADAM_B1 = 0.9
ADAM_B2 = 0.999
ADAM_EPS = 1e-08
ADAM_WD = 0.01
ADAM_STEP = 10
PER_EXAMPLE_BATCH_AXIS = {'x': 0, 'loss_target': 0}
SHARED_INPUTS = []
_WEIGHT_DTYPES = {'norm_mix_g': _jnp.float32, 'w_in': _jnp.float32, 'f_bias': _jnp.float32, 'sg_ln_g': _jnp.float32, 'sg_w': _jnp.float32, 'sg_b': _jnp.float32, 'w_out': _jnp.float32, 'norm_ffn_g': _jnp.float32, 'w_up': _jnp.float32, 'w_conv': _jnp.float32, 'b_conv': _jnp.float32, 'w_down': _jnp.float32, 'norm_final_g': _jnp.float32}
MOMENT_SCALE = {'norm_mix_g': 2.059088e-01, 'w_in': 1.253334e-01, 'f_bias': 3.613044e-01, 'sg_ln_g': 1.265416e-01, 'sg_w': 9.195336e-02, 'sg_b': 1.111766e-01, 'w_out': 1.550333e-01, 'norm_ffn_g': 1.787076e-01, 'w_up': 7.076837e-02, 'w_conv': 6.980763e-02, 'b_conv': 7.478864e-02, 'w_down': 1.157204e-01, 'norm_final_g': 6.408136e+01}


def _to_microbatches(a, axis):
    t = _jnp.moveaxis(a, axis, 0)
    t = t.reshape((N_MICROBATCH, t.shape[0] // N_MICROBATCH) + t.shape[1:])
    return _jnp.moveaxis(t, 1, axis + 1)


def setup_inputs(seed: int = 0) -> dict:
    inp = _fwd_setup_inputs(seed)
    key = _jax.random.fold_in(_jax.random.key(seed), 7919)
    shape, _ = _output_shape()
    out = dict(inp)
    out["loss_target"] = _jax.random.normal(_jax.random.fold_in(key, 0), shape, _jnp.float32)
    for i, name in enumerate(TWIN_WEIGHTS):
        w = inp[name].astype(_jnp.float32)
        if MOMENT_SCALE is None:
            s = _jnp.sqrt(_jnp.mean(_jnp.square(w)) + 1e-30)
        else:
            s = MOMENT_SCALE[name]
        km, kv = _jax.random.split(_jax.random.fold_in(key, i + 1))
        out[name] = w
        out["m_" + name] = s * _jax.random.normal(km, w.shape, _jnp.float32)
        out["v_" + name] = (s * s) * _jax.random.uniform(kv, w.shape, _jnp.float32, 0.5, 1.5)
    if N_MICROBATCH > 1:
        for name, axis in PER_EXAMPLE_BATCH_AXIS.items():
            out[name] = _to_microbatches(out[name], axis)
    return {'x': out['x'], 'norm_mix_g': out['norm_mix_g'], 'w_in': out['w_in'], 'f_bias': out['f_bias'], 'sg_ln_g': out['sg_ln_g'], 'sg_w': out['sg_w'], 'sg_b': out['sg_b'], 'w_out': out['w_out'], 'norm_ffn_g': out['norm_ffn_g'], 'w_up': out['w_up'], 'w_conv': out['w_conv'], 'b_conv': out['b_conv'], 'w_down': out['w_down'], 'norm_final_g': out['norm_final_g'], 'loss_target': out['loss_target'], 'm_norm_mix_g': out['m_norm_mix_g'], 'm_w_in': out['m_w_in'], 'm_f_bias': out['m_f_bias'], 'm_sg_ln_g': out['m_sg_ln_g'], 'm_sg_w': out['m_sg_w'], 'm_sg_b': out['m_sg_b'], 'm_w_out': out['m_w_out'], 'm_norm_ffn_g': out['m_norm_ffn_g'], 'm_w_up': out['m_w_up'], 'm_w_conv': out['m_w_conv'], 'm_b_conv': out['m_b_conv'], 'm_w_down': out['m_w_down'], 'm_norm_final_g': out['m_norm_final_g'], 'v_norm_mix_g': out['v_norm_mix_g'], 'v_w_in': out['v_w_in'], 'v_f_bias': out['v_f_bias'], 'v_sg_ln_g': out['v_sg_ln_g'], 'v_sg_w': out['v_sg_w'], 'v_sg_b': out['v_sg_b'], 'v_w_out': out['v_w_out'], 'v_norm_ffn_g': out['v_norm_ffn_g'], 'v_w_up': out['v_w_up'], 'v_w_conv': out['v_w_conv'], 'v_b_conv': out['v_b_conv'], 'v_w_down': out['v_w_down'], 'v_norm_final_g': out['v_norm_final_g']}


def _loss(weights, diff, rest, loss_target):
    with _jax.named_scope("forward"):
        args = {**rest, TWIN_DIFF_INPUT: diff, **{k: w.astype(_WEIGHT_DTYPES[k]) for k, w in weights.items()}}
        y = _forward(args)
    with _jax.named_scope("loss_head"):
        err = _jnp.square(y.astype(_jnp.float32) - loss_target)
        return 0.5 * _jnp.sum(_jnp.mean(err, axis=-1)) if err.ndim else 0.5 * err


def _adamw(w, g, m, v):
    m = ADAM_B1 * m + (1.0 - ADAM_B1) * g
    v = ADAM_B2 * v + (1.0 - ADAM_B2) * _jnp.square(g)
    m_hat = m / (1.0 - ADAM_B1 ** ADAM_STEP)
    v_hat = v / (1.0 - ADAM_B2 ** ADAM_STEP)
    delta = -ADAM_LR * (m_hat / (_jnp.sqrt(v_hat) + ADAM_EPS) + ADAM_WD * w)
    return delta, m, v


def reference(x, norm_mix_g, w_in, f_bias, sg_ln_g, sg_w, sg_b, w_out, norm_ffn_g, w_up, w_conv, b_conv, w_down, norm_final_g, loss_target, m_norm_mix_g, m_w_in, m_f_bias, m_sg_ln_g, m_sg_w, m_sg_b, m_w_out, m_norm_ffn_g, m_w_up, m_w_conv, m_b_conv, m_w_down, m_norm_final_g, v_norm_mix_g, v_w_in, v_f_bias, v_sg_ln_g, v_sg_w, v_sg_b, v_w_out, v_norm_ffn_g, v_w_up, v_w_conv, v_b_conv, v_w_down, v_norm_final_g):
    given = dict(x=x, norm_mix_g=norm_mix_g, w_in=w_in, f_bias=f_bias, sg_ln_g=sg_ln_g, sg_w=sg_w, sg_b=sg_b, w_out=w_out, norm_ffn_g=norm_ffn_g, w_up=w_up, w_conv=w_conv, b_conv=b_conv, w_down=w_down, norm_final_g=norm_final_g, loss_target=loss_target, m_norm_mix_g=m_norm_mix_g, m_w_in=m_w_in, m_f_bias=m_f_bias, m_sg_ln_g=m_sg_ln_g, m_sg_w=m_sg_w, m_sg_b=m_sg_b, m_w_out=m_w_out, m_norm_ffn_g=m_norm_ffn_g, m_w_up=m_w_up, m_w_conv=m_w_conv, m_b_conv=m_b_conv, m_w_down=m_w_down, m_norm_final_g=m_norm_final_g, v_norm_mix_g=v_norm_mix_g, v_w_in=v_w_in, v_f_bias=v_f_bias, v_sg_ln_g=v_sg_ln_g, v_sg_w=v_sg_w, v_sg_b=v_sg_b, v_w_out=v_w_out, v_norm_ffn_g=v_norm_ffn_g, v_w_up=v_w_up, v_w_conv=v_w_conv, v_b_conv=v_b_conv, v_w_down=v_w_down, v_norm_final_g=v_norm_final_g)
    weights = {n: given[n] for n in TWIN_WEIGHTS}
    shared = {n: given[n] for n in SHARED_INPUTS}
    per_example = {n: given[n] for n in ['x']}
    grad_fn = _jax.value_and_grad(_loss, argnums=(0, 1))

    def one_microbatch(ex, loss_target):
        ex = dict(ex)
        diff = ex.pop(TWIN_DIFF_INPUT)
        return grad_fn(weights, diff, {**shared, **ex}, loss_target)

    if N_MICROBATCH == 1:
        loss, (grad_w, grad_x) = one_microbatch(per_example, given["loss_target"])
    else:
        def body(carry, xs):
            loss_sum, grad_sum = carry
            l_k, (gw_k, gx_k) = one_microbatch(xs[0], xs[1])
            with _jax.named_scope("update"):
                return (loss_sum + l_k, _jax.tree.map(_jnp.add, grad_sum, gw_k)), gx_k

        init = (_jnp.zeros((), _jnp.float32), _jax.tree.map(_jnp.zeros_like, weights))
        (loss, grad_w), grad_x = _jax.lax.scan(body, init, (per_example, given["loss_target"]))
    with _jax.named_scope("update"):
        delta_w, new_m, new_v = {}, {}, {}
        for n in TWIN_WEIGHTS:
            delta_w[n], new_m[n], new_v[n] = _adamw(weights[n], grad_w[n], given["m_" + n], given["v_" + n])
    return (loss, grad_x, *[grad_w[n] for n in TWIN_WEIGHTS], *[delta_w[n] for n in TWIN_WEIGHTS],
            *[new_m[n] for n in TWIN_WEIGHTS], *[new_v[n] for n in TWIN_WEIGHTS])
```

```python
import functools
import math

import jax
import jax.numpy as jnp
from jax import lax
from jax.experimental import pallas as pl
from jax.experimental.pallas import tpu as pltpu

F32 = jnp.float32
BF16 = jnp.bfloat16
MESH = pl.DeviceIdType.MESH

D_MODEL = 1024
N_HEADS = 8
HEAD_DIM = 64
D_HEADS = N_HEADS * HEAD_DIM
SG_BLOCK = 128
CHUNK = 64
D_FF = 2816
D_IN = 2 * D_HEADS + 3 * D_HEADS + N_HEADS
LANES = 128
SUBLANES = 8
D_IN_PAD = 5 * D_HEADS + LANES
EPS = 1e-6
SCALE = HEAD_DIM ** -0.5
NEG = -1e30

ADAM_LR = 0.001
ADAM_B1 = 0.9
ADAM_B2 = 0.999
ADAM_EPS = 1e-08
ADAM_WD = 0.01
ADAM_STEP = 10

VMEM_LIMIT = 56 * 1024 * 1024

NT = (((1,), (1,)), ((), ()))
TN = (((0,), (0,)), ((), ()))


def _params(sem):
    return pltpu.CompilerParams(dimension_semantics=sem, vmem_limit_bytes=VMEM_LIMIT)


def _full(shape):
    nd = len(shape)
    return pl.BlockSpec(shape, lambda *_: (0,) * nd)


def _row_tile(rows, target):
    best = None
    for t in range(SUBLANES, min(rows, target) + 1, SUBLANES):
        if rows % t == 0:
            best = t
    assert best is not None, rows
    return best


def _gelu(z):
    return 0.5 * z * (1.0 + lax.erf(z * (2.0 ** -0.5)))


def _gelu_grad(z):
    cdf = 0.5 * (1.0 + lax.erf(z * (2.0 ** -0.5)))
    pdf = jnp.exp(-0.5 * z * z) * (1.0 / math.sqrt(2.0 * math.pi))
    return cdf + z * pdf


def _split_dot(x, m):
    hi = x.astype(BF16)
    lo = (x - hi.astype(F32)).astype(BF16)
    return jnp.dot(hi, m, preferred_element_type=F32) + jnp.dot(lo, m, preferred_element_type=F32)


def _head_mask(h, rows):
    lane = lax.broadcasted_iota(jnp.int32, (rows, D_HEADS), 1)
    return (lane >= h * HEAD_DIM) & (lane < (h + 1) * HEAD_DIM)


def _rms_bwd(dh, x, g):
    r = lax.rsqrt(jnp.mean(x * x, axis=-1, keepdims=True) + EPS)
    xhat = x * r
    dg = jnp.sum(dh * xhat, axis=0, keepdims=True)
    dxhat = dh * g
    dx = r * (dxhat - xhat * jnp.mean(dxhat * xhat, axis=-1, keepdims=True))
    return dx, dg


def _in_proj(x, g1, w_in, tm):
    S = x.shape[0]
    nz = D_IN_PAD - LANES

    def body(x_ref, g_ref, w_ref, z_ref, f_ref, h_ref):
        xf = x_ref[...]
        r = lax.rsqrt(jnp.mean(xf * xf, axis=-1, keepdims=True) + EPS)
        h = (xf * r * g_ref[...]).astype(BF16)
        h_ref[...] = h
        zz = jnp.dot(h, w_ref[...], preferred_element_type=F32)
        z_ref[...] = zz[:, :nz].astype(BF16)
        f_ref[...] = zz[:, nz:]

    return pl.pallas_call(
        body, name="in_proj", grid=(S // tm,),
        in_specs=[pl.BlockSpec((tm, D_MODEL), lambda i: (i, 0)), _full((1, D_MODEL)), _full((D_MODEL, D_IN_PAD))],
        out_specs=[pl.BlockSpec((tm, nz), lambda i: (i, 0)), pl.BlockSpec((tm, LANES), lambda i: (i, 0)),
                   pl.BlockSpec((tm, D_MODEL), lambda i: (i, 0))],
        out_shape=[jax.ShapeDtypeStruct((S, nz), BF16), jax.ShapeDtypeStruct((S, LANES), F32),
                   jax.ShapeDtypeStruct((S, D_MODEL), BF16)],
        compiler_params=_params(("parallel",)),
    )(x, g1, w_in)


def _fox_prep(f, bias_row, tb):
    S = f.shape[0]

    def body(f_ref, b_ref, c_ref, ct_ref, carry):
        @pl.when(pl.program_id(0) == 0)
        def _():
            carry[...] = jnp.zeros_like(carry)

        xv = f_ref[...] + b_ref[...]
        lf = jnp.minimum(xv, 0.0) - jnp.log(1.0 + jnp.exp(-jnp.abs(xv)))
        r = lax.broadcasted_iota(jnp.int32, (tb, tb), 0)
        s = lax.broadcasted_iota(jnp.int32, (tb, tb), 1)
        tri = (r >= s).astype(F32)
        cs = jnp.dot(tri, lf, precision=lax.Precision.HIGHEST, preferred_element_type=F32) + carry[0:1, :]
        c_ref[...] = cs
        ct_ref[...] = cs.T[:N_HEADS, :]
        carry[...] = jnp.broadcast_to(cs[tb - 1:tb, :], carry.shape)

    return pl.pallas_call(
        body, name="fox_prep", grid=(S // tb,),
        in_specs=[pl.BlockSpec((tb, LANES), lambda i: (i, 0)), _full((1, LANES))],
        out_specs=[pl.BlockSpec((tb, LANES), lambda i: (i, 0)), pl.BlockSpec((N_HEADS, tb), lambda i: (0, i))],
        out_shape=[jax.ShapeDtypeStruct((S, LANES), F32), jax.ShapeDtypeStruct((N_HEADS, S), F32)],
        scratch_shapes=[pltpu.VMEM((SUBLANES, LANES), F32)],
        compiler_params=_params(("arbitrary",)),
    )(f, bias_row)


def _attn_fwd(z, c, ct, tq):
    S = z.shape[0]
    n = S // tq

    def body(q_ref, k_ref, v_ref, c_ref, ct_ref, o_ref, lse_ref, m_s, l_s, acc_s):
        qi, ki = pl.program_id(0), pl.program_id(1)

        @pl.when(ki == 0)
        def _():
            m_s[...] = jnp.full_like(m_s, NEG)
            l_s[...] = jnp.zeros_like(l_s)
            acc_s[...] = jnp.zeros_like(acc_s)

        def step(diagonal):
            if diagonal:
                row = lax.broadcasted_iota(jnp.int32, (tq, tq), 0)
                col = lax.broadcasted_iota(jnp.int32, (tq, tq), 1)
                keep = row >= col
            for h in range(N_HEADS):
                sl = slice(h * HEAD_DIM, (h + 1) * HEAD_DIM)
                s = lax.dot_general(q_ref[:, sl], k_ref[:, sl], NT, preferred_element_type=F32) * SCALE
                s = s + (c_ref[:, h:h + 1] - ct_ref[h:h + 1, :])
                if diagonal:
                    s = jnp.where(keep, s, NEG)
                m_prev = m_s[h]
                m_new = jnp.maximum(m_prev, jnp.max(s, axis=1, keepdims=True))
                alpha = jnp.exp(m_prev - m_new)
                p = jnp.exp(s - m_new)
                l_s[h] = alpha * l_s[h] + jnp.sum(p, axis=1, keepdims=True)
                acc_s[h] = alpha * acc_s[h] + jnp.dot(p.astype(BF16), v_ref[:, sl], preferred_element_type=F32)
                m_s[h] = m_new

        @pl.when(ki < qi)
        def _():
            step(False)

        @pl.when(ki == qi)
        def _():
            step(True)
            lse_ref[...] = jnp.zeros_like(lse_ref)
            for h in range(N_HEADS):
                sl = slice(h * HEAD_DIM, (h + 1) * HEAD_DIM)
                o_ref[:, sl] = (acc_s[h] / l_s[h]).astype(BF16)
                lse_ref[:, h:h + 1] = m_s[h] + jnp.log(l_s[h])

    kv = lambda col: pl.BlockSpec((tq, D_HEADS), lambda qi, ki: (jnp.minimum(ki, qi), col))
    return pl.pallas_call(
        body, name="attn_fwd", grid=(n, n),
        in_specs=[pl.BlockSpec((tq, D_HEADS), lambda qi, ki: (qi, 2)), kv(3), kv(4),
                  pl.BlockSpec((tq, LANES), lambda qi, ki: (qi, 0)),
                  pl.BlockSpec((N_HEADS, tq), lambda qi, ki: (0, jnp.minimum(ki, qi)))],
        out_specs=[pl.BlockSpec((tq, D_HEADS), lambda qi, ki: (qi, 0)), pl.BlockSpec((tq, LANES), lambda qi, ki: (qi, 0))],
        out_shape=[jax.ShapeDtypeStruct((S, D_HEADS), BF16), jax.ShapeDtypeStruct((S, LANES), F32)],
        scratch_shapes=[pltpu.VMEM((N_HEADS, tq, 1), F32), pltpu.VMEM((N_HEADS, tq, 1), F32),
                        pltpu.VMEM((N_HEADS, tq, HEAD_DIM), F32)],
        compiler_params=_params(("parallel", "arbitrary")),
    )(z, z, z, c, ct)


def _layer_norm_heads(v, seg_avg):
    mu = _split_dot(v, seg_avg)
    d = v - mu
    var = _split_dot(d * d, seg_avg)
    rstd = lax.rsqrt(var + EPS)
    return d * rstd, rstd


def _gate_mix(vn_blk, w_ref, bias):
    acc = bias
    for h in range(N_HEADS):
        vh = jnp.where(_head_mask(h, SG_BLOCK), vn_blk, 0.0).astype(BF16)
        acc = acc + jnp.dot(w_ref[h], vh, preferred_element_type=F32)
    return acc


def _gate_fwd(z, w_mask, ln_row, b_full, seg_avg, tm):
    S = z.shape[0]

    def body(zu_ref, zv_ref, w_ref, ln_ref, b_ref, avg_ref, o_ref):
        u = _gelu(zu_ref[...].astype(F32))
        v = _gelu(zv_ref[...].astype(F32))
        vhat, _ = _layer_norm_heads(v, avg_ref[...])
        vn = vhat * ln_ref[...]
        for b in range(tm // SG_BLOCK):
            rows = slice(b * SG_BLOCK, (b + 1) * SG_BLOCK)
            mixed = _gate_mix(vn[rows], w_ref, b_ref[...])
            o_ref[rows, :] = (u[rows] * mixed).astype(BF16)

    return pl.pallas_call(
        body, name="gate_fwd", grid=(S // tm,),
        in_specs=[pl.BlockSpec((tm, D_HEADS), lambda i: (i, 0)), pl.BlockSpec((tm, D_HEADS), lambda i: (i, 1)),
                  _full((N_HEADS, SG_BLOCK, SG_BLOCK)), _full((1, D_HEADS)), _full((SG_BLOCK, D_HEADS)),
                  _full((D_HEADS, D_HEADS))],
        out_specs=pl.BlockSpec((tm, D_HEADS), lambda i: (i, 0)),
        out_shape=jax.ShapeDtypeStruct((S, D_HEADS), BF16),
        compiler_params=_params(("parallel",)),
    )(z, z, w_mask, ln_row, b_full, seg_avg)


def _mix_out(x, out_a, out_b, w_out, g2, tm):
    S = x.shape[0]

    def body(x_ref, a_ref, b_ref, w_ref, g_ref, x1_ref, h_ref):
        y = jnp.dot(a_ref[...], w_ref[:D_HEADS, :], preferred_element_type=F32)
        y = y + jnp.dot(b_ref[...], w_ref[D_HEADS:, :], preferred_element_type=F32)
        x1 = x_ref[...] + y
        x1_ref[...] = x1
        r = lax.rsqrt(jnp.mean(x1 * x1, axis=-1, keepdims=True) + EPS)
        h_ref[...] = (x1 * r * g_ref[...]).astype(BF16)

    row = lambda w: pl.BlockSpec((tm, w), lambda i: (i, 0))
    return pl.pallas_call(
        body, name="mix_out", grid=(S // tm,),
        in_specs=[row(D_MODEL), row(D_HEADS), row(D_HEADS), _full((D_MODEL, D_MODEL)), _full((1, D_MODEL))],
        out_specs=[row(D_MODEL), row(D_MODEL)],
        out_shape=[jax.ShapeDtypeStruct((S, D_MODEL), F32), jax.ShapeDtypeStruct((S, D_MODEL), BF16)],
        compiler_params=_params(("parallel",)),
    )(x, out_a, out_b, w_out, g2)


def _up_proj(h2, w_up_q, tm):
    S = h2.shape[0]
    nq, _, wq = w_up_q.shape

    def body(h_ref, w_ref, a_ref):
        a_ref[...] = jnp.dot(h_ref[...], w_ref[...], preferred_element_type=F32).astype(BF16)

    return pl.pallas_call(
        body, name="up_proj", grid=(nq, S // tm),
        in_specs=[pl.BlockSpec((tm, D_MODEL), lambda j, i: (i, 0)), pl.BlockSpec((None, D_MODEL, wq), lambda j, i: (j, 0, 0))],
        out_specs=pl.BlockSpec((tm, wq), lambda j, i: (i, j)),
        out_shape=jax.ShapeDtypeStruct((S, nq * wq), BF16),
        compiler_params=_params(("parallel", "parallel")),
    )(h2, w_up_q)


def _shift_down(a, halo, k):
    tm = a.shape[0]
    ra = pltpu.roll(a, k, 0)
    rh = pltpu.roll(halo, k, 0)
    row = lax.broadcasted_iota(jnp.int32, halo.shape, 0)
    top = jnp.where(row < k, rh, ra[0:SUBLANES])
    return jnp.concatenate([top, ra[SUBLANES:tm]], axis=0)


def _shift_up(a, halo, k):
    tm = a.shape[0]
    ra = pltpu.roll(a, tm - k, 0)
    rh = pltpu.roll(halo, SUBLANES - k, 0)
    row = lax.broadcasted_iota(jnp.int32, halo.shape, 0)
    bottom = jnp.where(row >= SUBLANES - k, rh, ra[tm - SUBLANES:tm])
    return jnp.concatenate([ra[0:tm - SUBLANES], bottom], axis=0)


def _conv_taps(a_ref, halo_ref, first):
    a = a_ref[...].astype(F32)
    halo = halo_ref[...].astype(F32) * jnp.where(first, 0.0, 1.0)
    return a, _shift_down(a, halo, 1), _shift_down(a, halo, 2)


def _conv_specs(tm):
    step = tm // SUBLANES
    prev = lambda i: jnp.maximum(i * step - 1, 0)
    return [pl.BlockSpec((tm, D_FF), lambda i: (i, 0)), pl.BlockSpec((tm, D_FF), lambda i: (i, 1)),
            pl.BlockSpec((SUBLANES, D_FF), lambda i: (prev(i), 0)), pl.BlockSpec((SUBLANES, D_FF), lambda i: (prev(i), 1))]


def _ffn_fwd_loss(a, w_conv, b_conv, w_down, x1, g3, target, tm):
    S = x1.shape[0]

    def body(ag_ref, av_ref, hg_ref, hv_ref, wg_ref, wv_ref, bg_ref, bv_ref, wd_ref, x1_ref, g_ref, t_ref,
             dx2_ref, loss_ref, dg_ref):
        i = pl.program_id(0)

        @pl.when(i == 0)
        def _():
            loss_ref[...] = jnp.zeros_like(loss_ref)
            dg_ref[...] = jnp.zeros_like(dg_ref)

        g0, g1, g2 = _conv_taps(ag_ref, hg_ref, i == 0)
        gate = wg_ref[2:3, :] * g0 + wg_ref[1:2, :] * g1 + wg_ref[0:1, :] * g2 + bg_ref[...]
        v0, v1, v2 = _conv_taps(av_ref, hv_ref, i == 0)
        val = wv_ref[2:3, :] * v0 + wv_ref[1:2, :] * v1 + wv_ref[0:1, :] * v2 + bv_ref[...]
        y = (gate * jax.nn.sigmoid(gate) * val).astype(BF16)
        x2 = x1_ref[...] + jnp.dot(y, wd_ref[...], preferred_element_type=F32)
        r = lax.rsqrt(jnp.mean(x2 * x2, axis=-1, keepdims=True) + EPS)
        xhat = x2 * r
        gg = g_ref[...]
        err = xhat * gg - t_ref[...]
        loss_ref[...] += jnp.sum(err * err, axis=0, keepdims=True)
        dy = err * (1.0 / D_MODEL)
        dg_ref[...] += jnp.sum(dy * xhat, axis=0, keepdims=True)
        dxhat = dy * gg
        dx2_ref[...] = r * (dxhat - xhat * jnp.mean(dxhat * xhat, axis=-1, keepdims=True))

    row = lambda w: pl.BlockSpec((tm, w), lambda i: (i, 0))
    half = lambda r: [pl.BlockSpec((r, D_FF), lambda i: (0, 0)), pl.BlockSpec((r, D_FF), lambda i: (0, 1))]
    return pl.pallas_call(
        body, name="ffn_fwd_loss", grid=(S // tm,),
        in_specs=_conv_specs(tm) + half(3) + half(1) + [_full((D_FF, D_MODEL)), row(D_MODEL), _full((1, D_MODEL)), row(D_MODEL)],
        out_specs=[row(D_MODEL), _full((1, D_MODEL)), _full((1, D_MODEL))],
        out_shape=[jax.ShapeDtypeStruct((S, D_MODEL), F32), jax.ShapeDtypeStruct((1, D_MODEL), F32),
                   jax.ShapeDtypeStruct((1, D_MODEL), F32)],
        compiler_params=_params(("arbitrary",)),
    )(a, a, a, a, w_conv, w_conv, b_conv, b_conv, w_down, x1, g3, target)


def _ffn_bwd_gate(dx2, a, w_conv, b_conv, w_down, tm):
    S = dx2.shape[0]

    def body(dx_ref, ag_ref, av_ref, hg_ref, hv_ref, wg_ref, wv_ref, bg_ref, bv_ref, wd_ref,
             dc_ref, y_ref, dw_ref, db_ref):
        i = pl.program_id(0)

        @pl.when(i == 0)
        def _():
            dw_ref[...] = jnp.zeros_like(dw_ref)
            db_ref[...] = jnp.zeros_like(db_ref)

        g0, g1, g2 = _conv_taps(ag_ref, hg_ref, i == 0)
        gate = wg_ref[2:3, :] * g0 + wg_ref[1:2, :] * g1 + wg_ref[0:1, :] * g2 + bg_ref[...]
        v0, v1, v2 = _conv_taps(av_ref, hv_ref, i == 0)
        val = wv_ref[2:3, :] * v0 + wv_ref[1:2, :] * v1 + wv_ref[0:1, :] * v2 + bv_ref[...]
        sg = jax.nn.sigmoid(gate)
        act = gate * sg
        y_ref[...] = (act * val).astype(BF16)
        dy = lax.dot_general(dx_ref[...].astype(BF16), wd_ref[...], NT, preferred_element_type=F32)
        dgate = dy * val * (sg * (1.0 + gate * (1.0 - sg)))
        dval = dy * act
        dc_ref[:, :D_FF] = dgate.astype(BF16)
        dc_ref[:, D_FF:] = dval.astype(BF16)
        for half, (d, taps) in enumerate(((dgate, (g2, g1, g0)), (dval, (v2, v1, v0)))):
            cols = slice(half * D_FF, (half + 1) * D_FF)
            db_ref[0:1, cols] += jnp.sum(d, axis=0, keepdims=True)
            for j in range(3):
                dw_ref[j:j + 1, cols] += jnp.sum(d * taps[j], axis=0, keepdims=True)

    row = lambda w: pl.BlockSpec((tm, w), lambda i: (i, 0))
    half = lambda r: [pl.BlockSpec((r, D_FF), lambda i: (0, 0)), pl.BlockSpec((r, D_FF), lambda i: (0, 1))]
    return pl.pallas_call(
        body, name="ffn_bwd_gate", grid=(S // tm,),
        in_specs=[row(D_MODEL)] + _conv_specs(tm) + half(3) + half(1) + [_full((D_FF, D_MODEL))],
        out_specs=[row(2 * D_FF), row(D_FF), _full((SUBLANES, 2 * D_FF)), _full((1, 2 * D_FF))],
        out_shape=[jax.ShapeDtypeStruct((S, 2 * D_FF), BF16), jax.ShapeDtypeStruct((S, D_FF), BF16),
                   jax.ShapeDtypeStruct((SUBLANES, 2 * D_FF), F32), jax.ShapeDtypeStruct((1, 2 * D_FF), F32)],
        compiler_params=_params(("arbitrary",)),
    )(dx2, a, a, a, a, w_conv, w_conv, b_conv, b_conv, w_down)


def _conv_bwd(dc, w_conv, tm, tn):
    S, C = dc.shape
    step = tm // SUBLANES
    last_blk = S // SUBLANES - 1

    def body(d_ref, nx_ref, w_ref, o_ref):
        last = pl.program_id(0) == pl.num_programs(0) - 1
        d = d_ref[...].astype(F32)
        nx = nx_ref[...].astype(F32) * jnp.where(last, 0.0, 1.0)
        out = w_ref[2:3, :] * d + w_ref[1:2, :] * _shift_up(d, nx, 1) + w_ref[0:1, :] * _shift_up(d, nx, 2)
        o_ref[...] = out.astype(BF16)

    return pl.pallas_call(
        body, name="conv_bwd", grid=(S // tm, C // tn),
        in_specs=[pl.BlockSpec((tm, tn), lambda i, j: (i, j)),
                  pl.BlockSpec((SUBLANES, tn), lambda i, j: (jnp.minimum((i + 1) * step, last_blk), j)),
                  pl.BlockSpec((3, tn), lambda i, j: (0, j))],
        out_specs=pl.BlockSpec((tm, tn), lambda i, j: (i, j)),
        out_shape=jax.ShapeDtypeStruct((S, C), BF16),
        compiler_params=_params(("parallel", "parallel")),
    )(dc, dc, w_conv)


def _matmul_tn(a, b, name, bm, bn, tk, col_a=0, col_b=0, quarters=None):
    S = a.shape[0]
    gm, gn = quarters if quarters else (1, 1)
    nk = S // tk

    def body(a_ref, b_ref, o_ref):
        @pl.when(pl.program_id(2) == 0)
        def _():
            o_ref[...] = jnp.zeros_like(o_ref)

        o_ref[...] += lax.dot_general(a_ref[...].astype(BF16), b_ref[...].astype(BF16), TN, preferred_element_type=F32)

    if quarters and gn > 1:
        out_spec = pl.BlockSpec((None, bm, bn), lambda i, j, k: (j, i, 0))
        out_shape = jax.ShapeDtypeStruct((gn, gm * bm, bn), F32)
    else:
        out_spec = pl.BlockSpec((bm, bn), lambda i, j, k: (i, j))
        out_shape = jax.ShapeDtypeStruct((gm * bm, gn * bn), F32)
    return pl.pallas_call(
        body, name=name, grid=(gm, gn, nk),
        in_specs=[pl.BlockSpec((tk, bm), lambda i, j, k: (k, col_a * gm + i)),
                  pl.BlockSpec((tk, bn), lambda i, j, k: (k, col_b * gn + j))],
        out_specs=out_spec, out_shape=out_shape,
        compiler_params=_params(("parallel", "parallel", "arbitrary")),
    )(a, b)


def _up_bwd(dact, w_up_q, x1, g2, dx2, tm):
    S = x1.shape[0]
    nq, _, wq = w_up_q.shape

    def body(d_ref, w_ref, x_ref, g_ref, dx2_ref, dx1_ref, dg_ref):
        @pl.when(pl.program_id(0) == 0)
        def _():
            dg_ref[...] = jnp.zeros_like(dg_ref)

        dh = jnp.zeros((tm, D_MODEL), F32)
        for j in range(nq):
            dh = dh + lax.dot_general(d_ref[:, j * wq:(j + 1) * wq], w_ref[j], NT, preferred_element_type=F32)
        dx, dg = _rms_bwd(dh, x_ref[...], g_ref[...])
        dg_ref[...] += dg
        dx1_ref[...] = dx2_ref[...] + dx

    row = lambda w: pl.BlockSpec((tm, w), lambda i: (i, 0))
    return pl.pallas_call(
        body, name="up_bwd", grid=(S // tm,),
        in_specs=[row(nq * wq), pl.BlockSpec((nq, D_MODEL, wq), lambda i: (0, 0, 0), pipeline_mode=pl.Buffered(1)),
                  row(D_MODEL), _full((1, D_MODEL)), row(D_MODEL)],
        out_specs=[row(D_MODEL), _full((1, D_MODEL))],
        out_shape=[jax.ShapeDtypeStruct((S, D_MODEL), F32), jax.ShapeDtypeStruct((1, D_MODEL), F32)],
        compiler_params=_params(("arbitrary",)),
    )(dact, w_up_q, x1, g2, dx2)


def _out_bwd(dx1, w_out, tm):
    S = dx1.shape[0]

    def body(d_ref, w_ref, o_ref):
        o_ref[...] = lax.dot_general(d_ref[...].astype(BF16), w_ref[...], NT, preferred_element_type=F32).astype(BF16)

    return pl.pallas_call(
        body, name="out_bwd", grid=(S // tm,),
        in_specs=[pl.BlockSpec((tm, D_MODEL), lambda i: (i, 0)), _full((D_MODEL, D_MODEL))],
        out_specs=pl.BlockSpec((tm, D_MODEL), lambda i: (i, 0)),
        out_shape=jax.ShapeDtypeStruct((S, D_MODEL), BF16),
        compiler_params=_params(("parallel",)),
    )(dx1, w_out)


def _gate_bwd(z, dcat, w_mask, w_mask_t, ln_row, b_full, seg_avg, head_ind, tm):
    S = z.shape[0]
    nb = tm // SG_BLOCK

    def body(zu_ref, zv_ref, do_ref, w_ref, wt_ref, ln_ref, b_ref, avg_ref, ind_ref,
             dzu_ref, dzv_ref, dw_ref, db_ref, dln_ref, dvn_s, dbf_s):
        i = pl.program_id(0)

        @pl.when(i == 0)
        def _():
            dw_ref[...] = jnp.zeros_like(dw_ref)
            dln_ref[...] = jnp.zeros_like(dln_ref)
            dbf_s[...] = jnp.zeros_like(dbf_s)

        zu = zu_ref[...].astype(F32)
        zv = zv_ref[...].astype(F32)
        u = _gelu(zu)
        v = _gelu(zv)
        avg = avg_ref[...]
        vhat, rstd = _layer_norm_heads(v, avg)
        ln = ln_ref[...]
        vn = vhat * ln
        for b in range(nb):
            rows = slice(b * SG_BLOCK, (b + 1) * SG_BLOCK)
            vn_b = vn[rows]
            mixed = _gate_mix(vn_b, w_ref, b_ref[...])
            do = do_ref[rows, :].astype(F32)
            dzu_ref[rows, :] = (do * mixed * _gelu_grad(zu[rows])).astype(BF16)
            dmix = do * u[rows]
            dbf_s[...] += dmix
            vn_bf = vn_b.astype(BF16)
            dvn = jnp.zeros((SG_BLOCK, D_HEADS), F32)
            for h in range(N_HEADS):
                dmh = jnp.where(_head_mask(h, SG_BLOCK), dmix, 0.0).astype(BF16)
                dw_ref[h] += lax.dot_general(dmh, vn_bf, NT, preferred_element_type=F32)
                dvn = dvn + jnp.dot(wt_ref[h], dmh, preferred_element_type=F32)
            dvn_s[rows, :] = dvn
        dvn = dvn_s[...]
        dln_ref[...] += jnp.sum(dvn * vhat, axis=0, keepdims=True)
        dvhat = dvn * ln
        dv = rstd * (dvhat - _split_dot(dvhat, avg) - vhat * _split_dot(dvhat * vhat, avg))
        dzv_ref[...] = (dv * _gelu_grad(zv)).astype(BF16)

        @pl.when(i == pl.num_programs(0) - 1)
        def _():
            r = lax.broadcasted_iota(jnp.int32, (SG_BLOCK, SG_BLOCK), 0) // CHUNK
            s = lax.broadcasted_iota(jnp.int32, (SG_BLOCK, SG_BLOCK), 1) // CHUNK
            for h in range(N_HEADS):
                dw_ref[h] = jnp.where(r >= s, dw_ref[h], 0.0)
            db_ref[...] = _split_dot(dbf_s[...], ind_ref[...])

    row = lambda col: pl.BlockSpec((tm, D_HEADS), lambda i: (i, col))
    wspec = _full((N_HEADS, SG_BLOCK, SG_BLOCK))
    return pl.pallas_call(
        body, name="gate_bwd", grid=(S // tm,),
        in_specs=[row(0), row(1), row(0), wspec, wspec, _full((1, D_HEADS)), _full((SG_BLOCK, D_HEADS)),
                  _full((D_HEADS, D_HEADS)), _full((D_HEADS, LANES))],
        out_specs=[row(0), row(0), wspec, _full((SG_BLOCK, LANES)), _full((1, D_HEADS))],
        out_shape=[jax.ShapeDtypeStruct((S, D_HEADS), BF16), jax.ShapeDtypeStruct((S, D_HEADS), BF16),
                   jax.ShapeDtypeStruct((N_HEADS, SG_BLOCK, SG_BLOCK), F32), jax.ShapeDtypeStruct((SG_BLOCK, LANES), F32),
                   jax.ShapeDtypeStruct((1, D_HEADS), F32)],
        scratch_shapes=[pltpu.VMEM((tm, D_HEADS), F32), pltpu.VMEM((SG_BLOCK, D_HEADS), F32)],
        compiler_params=_params(("arbitrary",)),
    )(z, z, dcat, w_mask, w_mask_t, ln_row, b_full, seg_avg, head_ind)


def _attn_delta(o, dcat, head_ind, tm):
    S = o.shape[0]

    def body(o_ref, do_ref, ind_ref, d_ref):
        d_ref[...] = _split_dot(o_ref[...].astype(F32) * do_ref[...].astype(F32), ind_ref[...])

    return pl.pallas_call(
        body, name="attn_delta", grid=(S // tm,),
        in_specs=[pl.BlockSpec((tm, D_HEADS), lambda i: (i, 0)), pl.BlockSpec((tm, D_HEADS), lambda i: (i, 1)),
                  _full((D_HEADS, LANES))],
        out_specs=pl.BlockSpec((tm, LANES), lambda i: (i, 0)),
        out_shape=jax.ShapeDtypeStruct((S, LANES), F32),
        compiler_params=_params(("parallel",)),
    )(o, dcat, head_ind)


def _attn_bwd(z, dcat, c, ct, lse, delta, tq):
    S = z.shape[0]
    n = S // tq

    def body(q_ref, k_ref, v_ref, do_ref, c_ref, ct_ref, lse_ref, dl_ref,
             dq_hbm, dcr_hbm, dk_ref, dv_ref, dct_ref, dq_s, dcr_s, dk_s, dv_s, dc_s, sems):
        ki, qi = pl.program_id(0), pl.program_id(1)

        @pl.when((ki == 0) & (qi == 0))
        def _():
            dq_s[...] = jnp.zeros_like(dq_s)
            dcr_s[...] = jnp.zeros_like(dcr_s)

        @pl.when(qi == ki)
        def _():
            dk_s[...] = jnp.zeros_like(dk_s)
            dv_s[...] = jnp.zeros_like(dv_s)
            dc_s[...] = jnp.zeros_like(dc_s)

        def step(diagonal):
            rows = pl.ds(pl.multiple_of(qi * tq, tq), tq)
            if diagonal:
                row = lax.broadcasted_iota(jnp.int32, (tq, tq), 0)
                col = lax.broadcasted_iota(jnp.int32, (tq, tq), 1)
                keep = row >= col
            for h in range(N_HEADS):
                sl = slice(h * HEAD_DIM, (h + 1) * HEAD_DIM)
                q, k, v, do = q_ref[:, sl], k_ref[:, sl], v_ref[:, sl], do_ref[:, sl]
                s = lax.dot_general(q, k, NT, preferred_element_type=F32) * SCALE
                s = s + (c_ref[:, h:h + 1] - ct_ref[h:h + 1, :])
                p = jnp.exp(s - lse_ref[:, h:h + 1])
                if diagonal:
                    p = jnp.where(keep, p, 0.0)
                dv_s[:, sl] += lax.dot_general(p.astype(BF16), do, TN, preferred_element_type=F32)
                dp = lax.dot_general(do, v, NT, preferred_element_type=F32)
                ds = p * (dp - dl_ref[:, h:h + 1])
                dc_s[h:h + 1, :] += jnp.sum(ds, axis=0, keepdims=True)
                dcr_s[rows, h:h + 1] += jnp.sum(ds, axis=1, keepdims=True)
                dsb = ds.astype(BF16)
                dk_s[:, sl] += lax.dot_general(dsb, q, TN, preferred_element_type=F32) * SCALE
                dq_s[rows, sl] += jnp.dot(dsb, k, preferred_element_type=F32) * SCALE

        @pl.when(qi > ki)
        def _():
            step(False)

        @pl.when(qi == ki)
        def _():
            step(True)

        @pl.when(qi == n - 1)
        def _():
            dk_ref[...] = dk_s[...].astype(BF16)
            dv_ref[...] = dv_s[...].astype(BF16)
            dct_ref[...] = dc_s[...]

        @pl.when((ki == n - 1) & (qi == n - 1))
        def _():
            copies = [pltpu.make_async_copy(dq_s, dq_hbm, sems.at[0]), pltpu.make_async_copy(dcr_s, dcr_hbm, sems.at[1])]
            for cp in copies:
                cp.start()
            for cp in copies:
                cp.wait()

    qspec = lambda w, col: pl.BlockSpec((tq, w), lambda ki, qi: (jnp.maximum(qi, ki), col))
    kspec = lambda col: pl.BlockSpec((tq, D_HEADS), lambda ki, qi: (ki, col))
    return pl.pallas_call(
        body, name="attn_bwd", grid=(n, n),
        in_specs=[qspec(D_HEADS, 2), kspec(3), kspec(4), qspec(D_HEADS, 1), qspec(LANES, 0),
                  pl.BlockSpec((N_HEADS, tq), lambda ki, qi: (0, ki)), qspec(LANES, 0), qspec(LANES, 0)],
        out_specs=[_ANY, _ANY, kspec(0), kspec(0), pl.BlockSpec((N_HEADS, tq), lambda ki, qi: (0, ki))],
        out_shape=[jax.ShapeDtypeStruct((S, D_HEADS), F32), jax.ShapeDtypeStruct((S, LANES), F32), jax.ShapeDtypeStruct((S, D_HEADS), BF16),
                   jax.ShapeDtypeStruct((S, D_HEADS), BF16), jax.ShapeDtypeStruct((N_HEADS, S), F32)],
        scratch_shapes=[pltpu.VMEM((S, D_HEADS), F32), pltpu.VMEM((S, LANES), F32), pltpu.VMEM((tq, D_HEADS), F32),
                        pltpu.VMEM((tq, D_HEADS), F32), pltpu.VMEM((N_HEADS, tq), F32), pltpu.SemaphoreType.DMA((2,))],
        compiler_params=_params(("arbitrary", "arbitrary")),
    )(z, z, z, dcat, c, ct, lse, delta)


def _fox_bwd(dc, f, bias_row, tb):
    S = f.shape[0]
    nb = S // tb

    def body(dc_ref, f_ref, b_ref, df_ref, dbias_ref, carry):
        @pl.when(pl.program_id(0) == 0)
        def _():
            carry[...] = jnp.zeros_like(carry)
            dbias_ref[...] = jnp.zeros_like(dbias_ref)

        r = lax.broadcasted_iota(jnp.int32, (tb, tb), 0)
        s = lax.broadcasted_iota(jnp.int32, (tb, tb), 1)
        tri = (s >= r).astype(F32)
        rc = jnp.dot(tri, dc_ref[...], precision=lax.Precision.HIGHEST, preferred_element_type=F32) + carry[0:1, :]
        carry[...] = jnp.broadcast_to(rc[0:1, :], carry.shape)
        lane = lax.broadcasted_iota(jnp.int32, (tb, LANES), 1)
        df = jnp.where(lane < N_HEADS, rc * jax.nn.sigmoid(-(f_ref[...] + b_ref[...])), 0.0)
        df_ref[...] = df.astype(BF16)
        dbias_ref[...] += jnp.sum(df, axis=0, keepdims=True)

    rev = pl.BlockSpec((tb, LANES), lambda i: (nb - 1 - i, 0))
    return pl.pallas_call(
        body, name="fox_bwd", grid=(nb,),
        in_specs=[rev, rev, _full((1, LANES))],
        out_specs=[rev, _full((1, LANES))],
        out_shape=[jax.ShapeDtypeStruct((S, LANES), BF16), jax.ShapeDtypeStruct((1, LANES), F32)],
        scratch_shapes=[pltpu.VMEM((SUBLANES, LANES), F32)],
        compiler_params=_params(("arbitrary",)),
    )(dc, f, bias_row)


_DZ_WIDTHS = (D_HEADS,) * 5 + (LANES,)


def _in_bwd(pieces, w_in, x, g1, dx1, tm):
    S = x.shape[0]

    def body(*refs):
        p_refs, (w_ref, x_ref, g_ref, dx1_ref, dx_ref, dg_ref) = refs[:6], refs[6:]

        @pl.when(pl.program_id(0) == 0)
        def _():
            dg_ref[...] = jnp.zeros_like(dg_ref)

        dh = jnp.zeros((tm, D_MODEL), F32)
        off = 0
        for p_ref, w in zip(p_refs, _DZ_WIDTHS):
            dh = dh + lax.dot_general(p_ref[...].astype(BF16), w_ref[:, off:off + w], NT, preferred_element_type=F32)
            off += w
        dx, dg = _rms_bwd(dh, x_ref[...], g_ref[...])
        dg_ref[...] += dg
        dx_ref[...] = dx1_ref[...] + dx

    row = lambda w: pl.BlockSpec((tm, w), lambda i: (i, 0))
    return pl.pallas_call(
        body, name="in_bwd", grid=(S // tm,),
        in_specs=[row(w) for w in _DZ_WIDTHS] + [_full((D_MODEL, D_IN_PAD)), row(D_MODEL), _full((1, D_MODEL)), row(D_MODEL)],
        out_specs=[row(D_MODEL), _full((1, D_MODEL))],
        out_shape=[jax.ShapeDtypeStruct((S, D_MODEL), F32), jax.ShapeDtypeStruct((1, D_MODEL), F32)],
        compiler_params=_params(("arbitrary",)),
    )(*pieces, w_in, x, g1, dx1)


def _dw_in(h1, pieces, tk):
    S = h1.shape[0]

    def body(*refs):
        h_ref, p_refs, o_ref = refs[0], refs[1:7], refs[7]

        @pl.when(pl.program_id(0) == 0)
        def _():
            o_ref[...] = jnp.zeros_like(o_ref)

        off = 0
        for p_ref, w in zip(p_refs, _DZ_WIDTHS):
            o_ref[:, off:off + w] += lax.dot_general(h_ref[...], p_ref[...].astype(BF16), TN, preferred_element_type=F32)
            off += w

    row = lambda w: pl.BlockSpec((tk, w), lambda k: (k, 0))
    return pl.pallas_call(
        body, name="dw_in", grid=(S // tk,),
        in_specs=[row(D_MODEL)] + [row(w) for w in _DZ_WIDTHS],
        out_specs=_full((D_MODEL, D_IN_PAD)),
        out_shape=jax.ShapeDtypeStruct((D_MODEL, D_IN_PAD), F32),
        compiler_params=_params(("arbitrary",)),
    )(h1, *pieces)


def _adamw_math(w, g, m, v):
    m = ADAM_B1 * m + (1.0 - ADAM_B1) * g
    v = ADAM_B2 * v + (1.0 - ADAM_B2) * (g * g)
    m_hat = m / (1.0 - ADAM_B1 ** ADAM_STEP)
    v_hat = v / (1.0 - ADAM_B2 ** ADAM_STEP)
    delta = -ADAM_LR * (m_hat / (jnp.sqrt(v_hat) + ADAM_EPS) + ADAM_WD * w)
    return delta, m, v


def _adamw(name, w, g, m, v):
    R, C = w.shape
    tr = _row_tile(R, 256)

    def body(w_ref, g_ref, m_ref, v_ref, d_ref, nm_ref, nv_ref):
        d, nm, nv = _adamw_math(w_ref[...], g_ref[...], m_ref[...], v_ref[...])
        d_ref[...] = d
        nm_ref[...] = nm
        nv_ref[...] = nv

    spec = pl.BlockSpec((tr, C), lambda i: (i, 0))
    return pl.pallas_call(
        body, name=name, grid=(R // tr,), in_specs=[spec] * 4, out_specs=[spec] * 3,
        out_shape=[jax.ShapeDtypeStruct((R, C), F32)] * 3,
        compiler_params=_params(("parallel",)),
    )(w, g, m, v)


def _pair_sum(name, mine, theirs):
    R, C = mine.shape
    tr = _row_tile(R, 256)

    def body(a_ref, b_ref, s_ref, sb_ref):
        s = a_ref[...] + b_ref[...]
        s_ref[...] = s
        sb_ref[...] = s.astype(BF16)

    spec = pl.BlockSpec((tr, C), lambda i: (i, 0))
    return pl.pallas_call(
        body, name=name, grid=(R // tr,), in_specs=[spec] * 2, out_specs=[spec] * 2,
        out_shape=[jax.ShapeDtypeStruct((R, C), F32), jax.ShapeDtypeStruct((R, C), BF16)],
        compiler_params=_params(("parallel",)),
    )(mine, theirs)


def _chip_sum(name, own, others):
    R, C = own.shape
    tr = _row_tile(R, 256)

    def body(a_ref, o_ref, s_ref):
        s = a_ref[...]
        for j in range(3):
            s = s + o_ref[j].astype(F32)
        s_ref[...] = s

    return pl.pallas_call(
        body, name=name, grid=(R // tr,),
        in_specs=[pl.BlockSpec((tr, C), lambda i: (i, 0)), pl.BlockSpec((3, tr, C), lambda i: (0, i, 0))],
        out_specs=pl.BlockSpec((tr, C), lambda i: (i, 0)),
        out_shape=jax.ShapeDtypeStruct((R, C), F32),
        compiler_params=_params(("parallel",)),
    )(own, others)


def _place():
    return lax.axis_index("x"), lax.axis_index("y"), lax.axis_index("c")


def _other_chips(x, y):
    return [(1 - x, y), (x, 1 - y), (1 - x, 1 - y)]


_ANY = pl.BlockSpec(memory_space=pl.ANY)


def _gather_quarters(shards):
    n = len(shards)

    def body(*refs):
        ins, outs = refs[:n], refs[n:2 * n]
        send_sems, recv_sems, local_sems = refs[2 * n:]
        x, y, c = _place()
        mine = 2 * x + y
        chips = _other_chips(x, y)
        local, sends = [], []
        for a in range(n):
            cp = pltpu.make_async_copy(ins[a], outs[a].at[mine], local_sems.at[a])
            cp.start()
            local.append(cp)
            for j, (px, py) in enumerate(chips):
                cp = pltpu.make_async_remote_copy(src_ref=ins[a], dst_ref=outs[a].at[mine], send_sem=send_sems.at[a, j],
                                                  recv_sem=recv_sems.at[a, j], device_id=(px, py, c), device_id_type=MESH)
                cp.start()
                sends.append(cp)
        for a in range(n):
            for j, (px, py) in enumerate(chips):
                pltpu.make_async_remote_copy(src_ref=ins[a], dst_ref=outs[a].at[2 * px + py], send_sem=send_sems.at[a, j],
                                             recv_sem=recv_sems.at[a, j], device_id=(px, py, c), device_id_type=MESH).wait_recv()
        for cp in sends:
            cp.wait_send()
        for cp in local:
            cp.wait()

    return pl.pallas_call(
        body, name="gather_weights",
        in_specs=[_ANY] * n, out_specs=[_ANY] * n,
        out_shape=[jax.ShapeDtypeStruct((4,) + s.shape, s.dtype) for s in shards],
        scratch_shapes=[pltpu.SemaphoreType.DMA((n, 3)), pltpu.SemaphoreType.DMA((n, 3)), pltpu.SemaphoreType.DMA((n,))],
    )(*shards)


def _swap_halves(grads):
    n = len(grads)

    def body(*refs):
        ins, mine_out, theirs_out = refs[:n], refs[n:2 * n], refs[2 * n:3 * n]
        send_sems, recv_sems, local_sems = refs[3 * n:]
        x, y, c = _place()
        started = []
        for a in range(n):
            half = ins[a].shape[1] // 2
            cp = pltpu.make_async_copy(ins[a].at[:, pl.ds(c * half, half), :], mine_out[a], local_sems.at[a])
            cp.start()
            started.append(cp)
            cp = pltpu.make_async_remote_copy(src_ref=ins[a].at[:, pl.ds((1 - c) * half, half), :], dst_ref=theirs_out[a],
                                              send_sem=send_sems.at[a], recv_sem=recv_sems.at[a],
                                              device_id=(x, y, 1 - c), device_id_type=MESH)
            cp.start()
            started.append(cp)
        for cp in started:
            cp.wait()

    halves = [jax.ShapeDtypeStruct((4, g.shape[1] // 2, g.shape[2]), F32) for g in grads]
    return pl.pallas_call(
        body, name="swap_halves",
        in_specs=[_ANY] * n, out_specs=[_ANY] * (2 * n), out_shape=halves + halves,
        scratch_shapes=[pltpu.SemaphoreType.DMA((n,)), pltpu.SemaphoreType.DMA((n,)), pltpu.SemaphoreType.DMA((n,))],
    )(*grads)


def _scatter_quarters(sums32, sums16):
    n = len(sums32)

    def body(*refs):
        in32, in16, own_out, got_out = refs[:n], refs[n:2 * n], refs[2 * n:3 * n], refs[3 * n:4 * n]
        send_sems, recv_sems, local_sems = refs[4 * n:]
        x, y, c = _place()
        mine = 2 * x + y
        chips = _other_chips(x, y)
        local, sends = [], []
        for a in range(n):
            cp = pltpu.make_async_copy(in32[a].at[mine], own_out[a], local_sems.at[a])
            cp.start()
            local.append(cp)
            for j, (px, py) in enumerate(chips):
                cp = pltpu.make_async_remote_copy(src_ref=in16[a].at[2 * px + py], dst_ref=got_out[a].at[j],
                                                  send_sem=send_sems.at[a, j], recv_sem=recv_sems.at[a, j],
                                                  device_id=(px, py, c), device_id_type=MESH)
                cp.start()
                sends.append(cp)
        for cp in sends:
            cp.wait()
        for cp in local:
            cp.wait()

    own = [jax.ShapeDtypeStruct(s.shape[1:], F32) for s in sums32]
    got = [jax.ShapeDtypeStruct((3,) + s.shape[1:], BF16) for s in sums16]
    return pl.pallas_call(
        body, name="scatter_quarters",
        in_specs=[_ANY] * (2 * n), out_specs=[_ANY] * (2 * n), out_shape=own + got,
        scratch_shapes=[pltpu.SemaphoreType.DMA((n, 3)), pltpu.SemaphoreType.DMA((n, 3)), pltpu.SemaphoreType.DMA((n,))],
    )(*sums32, *sums16)


def _join_halves(halves):
    n = len(halves)

    def body(*refs):
        ins, outs = refs[:n], refs[n:2 * n]
        send_sems, recv_sems, local_sems = refs[2 * n:]
        x, y, c = _place()
        started = []
        for a in range(n):
            half = ins[a].shape[0]
            rows = outs[a].at[pl.ds(c * half, half), :]
            cp = pltpu.make_async_copy(ins[a], rows, local_sems.at[a])
            cp.start()
            started.append(cp)
            cp = pltpu.make_async_remote_copy(src_ref=ins[a], dst_ref=rows, send_sem=send_sems.at[a], recv_sem=recv_sems.at[a],
                                              device_id=(x, y, 1 - c), device_id_type=MESH)
            cp.start()
            started.append(cp)
        for cp in started:
            cp.wait()

    return pl.pallas_call(
        body, name="join_halves",
        in_specs=[_ANY] * n, out_specs=[_ANY] * n,
        out_shape=[jax.ShapeDtypeStruct((2 * h.shape[0], h.shape[1]), F32) for h in halves],
        scratch_shapes=[pltpu.SemaphoreType.DMA((n,)), pltpu.SemaphoreType.DMA((n,)), pltpu.SemaphoreType.DMA((n,))],
    )(*halves)


def _small_allreduce_adamw(g, w, m, v):
    R = g.shape[0]

    def body(g_ref, w_ref, m_ref, v_ref, gs_ref, d_ref, nm_ref, nv_ref, all_s, send_sems, recv_sems):
        x, y, c = _place()
        me = 4 * x + 2 * y + c
        all_s[me] = g_ref[...]
        sends = []
        for k in range(1, 8):
            peer = (x ^ (k >> 2), y ^ ((k >> 1) & 1), c ^ (k & 1))
            cp = pltpu.make_async_remote_copy(src_ref=g_ref, dst_ref=all_s.at[me], send_sem=send_sems.at[k - 1],
                                              recv_sem=recv_sems.at[k - 1], device_id=peer, device_id_type=MESH)
            cp.start()
            sends.append(cp)
        for cp in sends:
            cp.wait()
        total = all_s[0]
        for d in range(1, 8):
            total = total + all_s[d]
        gs_ref[...] = total
        delta, nm, nv = _adamw_math(w_ref[...], total, m_ref[...], v_ref[...])
        d_ref[...] = delta
        nm_ref[...] = nm
        nv_ref[...] = nv

    vm = pl.BlockSpec(memory_space=pltpu.VMEM)
    return pl.pallas_call(
        body, name="small_allreduce_adamw",
        in_specs=[vm] * 4, out_specs=[vm] * 4, out_shape=[jax.ShapeDtypeStruct((R, LANES), F32)] * 4,
        scratch_shapes=[pltpu.VMEM((8, R, LANES), F32), pltpu.SemaphoreType.DMA((7,)), pltpu.SemaphoreType.DMA((7,))],
        compiler_params=pltpu.CompilerParams(vmem_limit_bytes=VMEM_LIMIT),
    )(g, w, m, v)


_SMALL = (("norm_mix_g", D_MODEL), ("f_bias", N_HEADS), ("sg_ln_g", D_HEADS), ("sg_w", N_HEADS * SG_BLOCK * SG_BLOCK),
          ("sg_b", N_HEADS * SG_BLOCK), ("norm_ffn_g", D_MODEL), ("w_conv", 3 * 2 * D_FF), ("b_conv", 2 * D_FF),
          ("norm_final_g", D_MODEL))


def _pack_small(parts):
    rows = []
    for name, size in _SMALL:
        flat = parts[name].reshape(-1).astype(F32)
        pad = (-size) % (SUBLANES * LANES)
        rows.append(jnp.pad(flat, (0, pad)).reshape(-1, LANES))
    return jnp.concatenate(rows, axis=0)


def _unpack_small(packed, shapes):
    out, r = {}, 0
    for name, size in _SMALL:
        nrows = (size + SUBLANES * LANES - 1) // (SUBLANES * LANES) * SUBLANES
        out[name] = packed[r:r + nrows].reshape(-1)[:size].reshape(shapes[name])
        r += nrows
    return out


def _local_step(x, target, g1, w_in, f_bias, sg_ln_g, sg_w, sg_b, w_out, g2, w_up_q, w_conv, b_conv, w_down, g3):
    S = x.shape[0]
    tm = _row_tile(S, 512)
    tms = _row_tile(S, 256)
    tq = _row_tile(S, 512)

    lane = jnp.arange(D_HEADS)
    seg_avg = jnp.where(lane[:, None] // HEAD_DIM == lane[None, :] // HEAD_DIM, 1.0 / HEAD_DIM, 0.0).astype(BF16)
    head_ind = (lane[:, None] // HEAD_DIM == jnp.arange(LANES)[None, :]).astype(BF16)
    pos_chunk = jnp.arange(SG_BLOCK) // CHUNK
    w_mask32 = jnp.where(pos_chunk[:, None] >= pos_chunk[None, :], sg_w, 0.0)
    w_mask = w_mask32.astype(BF16)
    w_mask_t = jnp.swapaxes(w_mask32, 1, 2).astype(BF16)
    ln_row = sg_ln_g.reshape(1, D_HEADS)
    b_full = jnp.repeat(sg_b.T, HEAD_DIM, axis=1)
    bias_row = jnp.pad(f_bias.reshape(1, N_HEADS), ((0, 0), (0, LANES - N_HEADS)))
    b_conv_row = b_conv.reshape(1, 2 * D_FF)

    z, f, h1 = _in_proj(x, g1, w_in, tm)
    c, ct = _fox_prep(f, bias_row, _row_tile(S, 256))
    out_b, lse = _attn_fwd(z, c, ct, tq)
    out_a = _gate_fwd(z, w_mask, ln_row, b_full, seg_avg, tm)
    x1, h2 = _mix_out(x, out_a, out_b, w_out, g2, tm)
    a = _up_proj(h2, w_up_q, tm)
    dx2, sq_err, dg3 = _ffn_fwd_loss(a, w_conv, b_conv_row, w_down, x1, g3, target, tms)

    dconv, y, dw_conv8, db_conv = _ffn_bwd_gate(dx2, a, w_conv, b_conv_row, w_down, tms)
    dact = _conv_bwd(dconv, w_conv, tm, 2 * D_FF // 4)
    dw_down = _matmul_tn(y, dx2, "dw_down", D_FF // 2, D_MODEL, tm, quarters=(2, 1))
    dx1, dg2 = _up_bwd(dact, w_up_q, x1, g2, dx2, tms)
    dw_up_q = _matmul_tn(h2, dact, "dw_up", D_MODEL, 2 * D_FF // 4, tm, quarters=(1, 4))
    dcat = _out_bwd(dx1, w_out, tm)
    dw_out_a = _matmul_tn(out_a, dx1, "dw_out_a", D_HEADS, D_MODEL, tm)
    dw_out_b = _matmul_tn(out_b, dx1, "dw_out_b", D_HEADS, D_MODEL, tm)
    dzu, dzv, dsg_w, dsg_b_t, dln = _gate_bwd(z, dcat, w_mask, w_mask_t, ln_row, b_full, seg_avg, head_ind, tm)
    delta = _attn_delta(out_b, dcat, head_ind, tm)
    dq, dc_rows, dk, dv, dc_cols = _attn_bwd(z, dcat, c, ct, lse, delta, tq)
    dc = dc_rows - jnp.pad(dc_cols.T, ((0, 0), (0, LANES - N_HEADS)))
    df, dbias = _fox_bwd(dc, f, bias_row, _row_tile(S, 256))
    pieces = (dzu, dzv, dq, dk, dv, df)
    dx, dg1 = _in_bwd(pieces, w_in, x, g1, dx1, tms)
    dw_in = _dw_in(h1, pieces, tm)

    grads = {
        "norm_mix_g": dg1, "f_bias": dbias[:, :N_HEADS], "sg_ln_g": dln, "sg_w": dsg_w, "sg_b": dsg_b_t[:, :N_HEADS].T,
        "norm_ffn_g": dg2, "w_conv": dw_conv8[:3], "b_conv": db_conv, "norm_final_g": dg3,
        "w_in": dw_in, "w_out": jnp.concatenate([dw_out_a, dw_out_b], axis=0), "w_up_q": dw_up_q, "w_down": dw_down,
    }
    return sq_err, dx, grads


def _reduce_big(grads_q):
    names = list(grads_q)
    mine, theirs = _split(_swap_halves([grads_q[k] for k in names]))
    sums32, sums16 = [], []
    for k, a, b in zip(names, mine, theirs):
        q, r, cc = a.shape
        s32, s16 = _pair_sum("pair_sum_" + k, a.reshape(q * r, cc), b.reshape(q * r, cc))
        sums32.append(s32.reshape(q, r, cc))
        sums16.append(s16.reshape(q, r, cc))
    own, got = _split(_scatter_quarters(sums32, sums16))
    halves = [_chip_sum("chip_sum_" + k, o, g) for k, o, g in zip(names, own, got)]
    return dict(zip(names, _join_halves(halves)))


def _split(seq):
    seq = list(seq)
    return seq[:len(seq) // 2], seq[len(seq) // 2:]


def kernel(x, norm_mix_g, w_in, f_bias, sg_ln_g, sg_w, sg_b, w_out, norm_ffn_g, w_up, w_conv, b_conv, w_down, norm_final_g, loss_target, m_norm_mix_g, m_w_in, m_f_bias, m_sg_ln_g, m_sg_w, m_sg_b, m_w_out, m_norm_ffn_g, m_w_up, m_w_conv, m_b_conv, m_w_down, m_norm_final_g, v_norm_mix_g, v_w_in, v_f_bias, v_sg_ln_g, v_sg_w, v_sg_b, v_w_out, v_norm_ffn_g, v_w_up, v_w_conv, v_b_conv, v_w_down, v_norm_final_g):
    args = dict(locals())
    quarter = 2 * lax.axis_index("x") + lax.axis_index("y")
    wq_conv = w_conv.shape[-1]

    g_in, g_out, g_up, g_down, g_conv = _gather_quarters(
        [w_in[0].astype(BF16), w_out[0].astype(BF16), w_up[0].astype(BF16), w_down[0].astype(BF16), w_conv[0]])
    w_in_full = jnp.pad(jnp.concatenate([g_in[q] for q in range(4)], axis=1), ((0, 0), (0, D_IN_PAD - D_IN)))
    w_out_full = g_out.reshape(D_MODEL, D_MODEL)
    w_down_full = g_down.reshape(D_FF, D_MODEL)
    w_conv_full = jnp.concatenate([g_conv[q] for q in range(4)], axis=1)

    sq_err, dx, grads = _local_step(
        x[0], loss_target[0], norm_mix_g, w_in_full, f_bias[0], sg_ln_g[0], sg_w[0], sg_b[0], w_out_full, norm_ffn_g,
        g_up, w_conv_full, b_conv[0], w_down_full, norm_final_g.reshape(1, D_MODEL))
    loss = lax.psum(0.5 * jnp.sum(sq_err) / D_MODEL, ("x", "y", "c"))

    dw_in = grads["w_in"][:, :D_IN].reshape(D_MODEL, 4, D_IN // 4).transpose(1, 0, 2)
    big = _reduce_big({"w_in": dw_in, "w_out": grads["w_out"].reshape(4, D_MODEL // 4, D_MODEL),
                       "w_up": grads["w_up_q"], "w_down": grads["w_down"].reshape(4, D_FF // 4, D_MODEL)})

    out = {"loss": loss, "grad_x": dx[None]}
    for k in ("w_in", "w_out", "w_up", "w_down"):
        g = big[k]
        d, nm, nv = _adamw("adamw_" + k, args[k][0], g, args["m_" + k][0], args["v_" + k][0])
        out["grad_" + k], out["delta_" + k], out["new_m_" + k], out["new_v_" + k] = g[None], d[None], nm[None], nv[None]

    def padded_conv(t):
        return lax.dynamic_update_slice(jnp.zeros((3, 4 * wq_conv), F32), t[0], (0, quarter * wq_conv))

    small_names = [n for n, _ in _SMALL]
    shapes = {n: (3, 4 * wq_conv) if n == "w_conv" else args[n].shape for n in small_names}
    pack = lambda prefix: _pack_small({n: padded_conv(args[prefix + n]) if n == "w_conv" else args[prefix + n] for n in small_names})
    packed = _small_allreduce_adamw(_pack_small({n: grads[n] for n in small_names}), pack(""), pack("m_"), pack("v_"))
    for prefix, arr in zip(("grad_", "delta_", "new_m_", "new_v_"), packed):
        for n, t in _unpack_small(arr, shapes).items():
            if n == "w_conv":
                t = lax.dynamic_slice(t, (0, quarter * wq_conv), (3, wq_conv))[None]
            out[prefix + n] = t

    weights = ["norm_mix_g", "w_in", "f_bias", "sg_ln_g", "sg_w", "sg_b", "w_out", "norm_ffn_g", "w_up", "w_conv", "b_conv",
               "w_down", "norm_final_g"]
    return (out["loss"], out["grad_x"], *[out[p + n] for p in ("grad_", "delta_", "new_m_", "new_v_") for n in weights])
```

```python
import functools
import math

import jax
import jax.numpy as jnp
from jax import lax
from jax.experimental import pallas as pl
from jax.experimental.pallas import tpu as pltpu

F32 = jnp.float32
BF16 = jnp.bfloat16
MESH = pl.DeviceIdType.MESH

D_MODEL = 1024
N_HEADS = 8
HEAD_DIM = 64
D_HEADS = N_HEADS * HEAD_DIM
SG_BLOCK = 128
CHUNK = 64
D_FF = 2816
D_IN = 2 * D_HEADS + 3 * D_HEADS + N_HEADS
LANES = 128
SUBLANES = 8
D_IN_PAD = 5 * D_HEADS + LANES
EPS = 1e-6
SCALE = HEAD_DIM ** -0.5
NEG = -1e30

ADAM_LR = 0.001
ADAM_B1 = 0.9
ADAM_B2 = 0.999
ADAM_EPS = 1e-08
ADAM_WD = 0.01
ADAM_STEP = 10

VMEM_LIMIT = 56 * 1024 * 1024

NT = (((1,), (1,)), ((), ()))
TN = (((0,), (0,)), ((), ()))


def _params(sem):
    return pltpu.CompilerParams(dimension_semantics=sem, vmem_limit_bytes=VMEM_LIMIT)


def _full(shape):
    nd = len(shape)
    return pl.BlockSpec(shape, lambda *_: (0,) * nd)


def _row_tile(rows, target):
    best = None
    for t in range(SUBLANES, min(rows, target) + 1, SUBLANES):
        if rows % t == 0:
            best = t
    assert best is not None, rows
    return best


def _gelu(z):
    return 0.5 * z * (1.0 + lax.erf(z * (2.0 ** -0.5)))


def _gelu_grad(z):
    cdf = 0.5 * (1.0 + lax.erf(z * (2.0 ** -0.5)))
    pdf = jnp.exp(-0.5 * z * z) * (1.0 / math.sqrt(2.0 * math.pi))
    return cdf + z * pdf


def _split_dot(x, m):
    hi = x.astype(BF16)
    lo = (x - hi.astype(F32)).astype(BF16)
    return jnp.dot(hi, m, preferred_element_type=F32) + jnp.dot(lo, m, preferred_element_type=F32)


def _head_mask(h, rows):
    lane = lax.broadcasted_iota(jnp.int32, (rows, D_HEADS), 1)
    return (lane >= h * HEAD_DIM) & (lane < (h + 1) * HEAD_DIM)


def _rms_bwd(dh, x, g):
    r = lax.rsqrt(jnp.mean(x * x, axis=-1, keepdims=True) + EPS)
    xhat = x * r
    dg = jnp.sum(dh * xhat, axis=0, keepdims=True)
    dxhat = dh * g
    dx = r * (dxhat - xhat * jnp.mean(dxhat * xhat, axis=-1, keepdims=True))
    return dx, dg


def _in_proj(x, g1, w_in, tm):
    S = x.shape[0]
    nz = D_IN_PAD - LANES

    def body(x_ref, g_ref, w_ref, z_ref, f_ref, h_ref):
        xf = x_ref[...]
        r = lax.rsqrt(jnp.mean(xf * xf, axis=-1, keepdims=True) + EPS)
        h = (xf * r * g_ref[...]).astype(BF16)
        h_ref[...] = h
        zz = jnp.dot(h, w_ref[...], preferred_element_type=F32)
        z_ref[...] = zz[:, :nz].astype(BF16)
        f_ref[...] = zz[:, nz:]

    return pl.pallas_call(
        body, name="in_proj", grid=(S // tm,),
        in_specs=[pl.BlockSpec((tm, D_MODEL), lambda i: (i, 0)), _full((1, D_MODEL)), _full((D_MODEL, D_IN_PAD))],
        out_specs=[pl.BlockSpec((tm, nz), lambda i: (i, 0)), pl.BlockSpec((tm, LANES), lambda i: (i, 0)),
                   pl.BlockSpec((tm, D_MODEL), lambda i: (i, 0))],
        out_shape=[jax.ShapeDtypeStruct((S, nz), BF16), jax.ShapeDtypeStruct((S, LANES), F32),
                   jax.ShapeDtypeStruct((S, D_MODEL), BF16)],
        compiler_params=_params(("parallel",)),
    )(x, g1, w_in)


def _fox_prep(f, bias_row, tb):
    S = f.shape[0]

    def body(f_ref, b_ref, c_ref, ct_ref, carry):
        @pl.when(pl.program_id(0) == 0)
        def _():
            carry[...] = jnp.zeros_like(carry)

        xv = f_ref[...] + b_ref[...]
        lf = jnp.minimum(xv, 0.0) - jnp.log(1.0 + jnp.exp(-jnp.abs(xv)))
        r = lax.broadcasted_iota(jnp.int32, (tb, tb), 0)
        s = lax.broadcasted_iota(jnp.int32, (tb, tb), 1)
        tri = (r >= s).astype(F32)
        cs = jnp.dot(tri, lf, precision=lax.Precision.HIGHEST, preferred_element_type=F32) + carry[0:1, :]
        c_ref[...] = cs
        ct_ref[...] = cs.T[:N_HEADS, :]
        carry[...] = jnp.broadcast_to(cs[tb - 1:tb, :], carry.shape)

    return pl.pallas_call(
        body, name="fox_prep", grid=(S // tb,),
        in_specs=[pl.BlockSpec((tb, LANES), lambda i: (i, 0)), _full((1, LANES))],
        out_specs=[pl.BlockSpec((tb, LANES), lambda i: (i, 0)), pl.BlockSpec((N_HEADS, tb), lambda i: (0, i))],
        out_shape=[jax.ShapeDtypeStruct((S, LANES), F32), jax.ShapeDtypeStruct((N_HEADS, S), F32)],
        scratch_shapes=[pltpu.VMEM((SUBLANES, LANES), F32)],
        compiler_params=_params(("arbitrary",)),
    )(f, bias_row)


def _attn_fwd(z, c, ct, tq):
    S = z.shape[0]
    n = S // tq

    def body(q_ref, k_ref, v_ref, c_ref, ct_ref, o_ref, lse_ref, m_s, l_s, acc_s):
        qi, ki = pl.program_id(0), pl.program_id(1)

        @pl.when(ki == 0)
        def _():
            m_s[...] = jnp.full_like(m_s, NEG)
            l_s[...] = jnp.zeros_like(l_s)
            acc_s[...] = jnp.zeros_like(acc_s)

        def step(diagonal):
            if diagonal:
                row = lax.broadcasted_iota(jnp.int32, (tq, tq), 0)
                col = lax.broadcasted_iota(jnp.int32, (tq, tq), 1)
                keep = row >= col
            for h in range(N_HEADS):
                sl = slice(h * HEAD_DIM, (h + 1) * HEAD_DIM)
                s = lax.dot_general(q_ref[:, sl], k_ref[:, sl], NT, preferred_element_type=F32) * SCALE
                s = s + (c_ref[:, h:h + 1] - ct_ref[h:h + 1, :])
                if diagonal:
                    s = jnp.where(keep, s, NEG)
                m_prev = m_s[h]
                m_new = jnp.maximum(m_prev, jnp.max(s, axis=1, keepdims=True))
                alpha = jnp.exp(m_prev - m_new)
                p = jnp.exp(s - m_new)
                l_s[h] = alpha * l_s[h] + jnp.sum(p, axis=1, keepdims=True)
                acc_s[h] = alpha * acc_s[h] + jnp.dot(p.astype(BF16), v_ref[:, sl], preferred_element_type=F32)
                m_s[h] = m_new

        @pl.when(ki < qi)
        def _():
            step(False)

        @pl.when(ki == qi)
        def _():
            step(True)
            lse_ref[...] = jnp.zeros_like(lse_ref)
            for h in range(N_HEADS):
                sl = slice(h * HEAD_DIM, (h + 1) * HEAD_DIM)
                o_ref[:, sl] = (acc_s[h] / l_s[h]).astype(BF16)
                lse_ref[:, h:h + 1] = m_s[h] + jnp.log(l_s[h])

    kv = lambda col: pl.BlockSpec((tq, D_HEADS), lambda qi, ki: (jnp.minimum(ki, qi), col))
    return pl.pallas_call(
        body, name="attn_fwd", grid=(n, n),
        in_specs=[pl.BlockSpec((tq, D_HEADS), lambda qi, ki: (qi, 2)), kv(3), kv(4),
                  pl.BlockSpec((tq, LANES), lambda qi, ki: (qi, 0)),
                  pl.BlockSpec((N_HEADS, tq), lambda qi, ki: (0, jnp.minimum(ki, qi)))],
        out_specs=[pl.BlockSpec((tq, D_HEADS), lambda qi, ki: (qi, 0)), pl.BlockSpec((tq, LANES), lambda qi, ki: (qi, 0))],
        out_shape=[jax.ShapeDtypeStruct((S, D_HEADS), BF16), jax.ShapeDtypeStruct((S, LANES), F32)],
        scratch_shapes=[pltpu.VMEM((N_HEADS, tq, 1), F32), pltpu.VMEM((N_HEADS, tq, 1), F32),
                        pltpu.VMEM((N_HEADS, tq, HEAD_DIM), F32)],
        compiler_params=_params(("parallel", "arbitrary")),
    )(z, z, z, c, ct)


def _layer_norm_heads(v, seg_avg):
    mu = _split_dot(v, seg_avg)
    d = v - mu
    var = _split_dot(d * d, seg_avg)
    rstd = lax.rsqrt(var + EPS)
    return d * rstd, rstd


def _gate_mix(vn_blk, w_ref, bias):
    acc = bias
    for h in range(N_HEADS):
        vh = jnp.where(_head_mask(h, SG_BLOCK), vn_blk, 0.0).astype(BF16)
        acc = acc + jnp.dot(w_ref[h], vh, preferred_element_type=F32)
    return acc


def _gate_fwd(z, w_mask, ln_row, b_full, seg_avg, tm):
    S = z.shape[0]

    def body(zu_ref, zv_ref, w_ref, ln_ref, b_ref, avg_ref, o_ref):
        u = _gelu(zu_ref[...].astype(F32))
        v = _gelu(zv_ref[...].astype(F32))
        vhat, _ = _layer_norm_heads(v, avg_ref[...])
        vn = vhat * ln_ref[...]
        for b in range(tm // SG_BLOCK):
            rows = slice(b * SG_BLOCK, (b + 1) * SG_BLOCK)
            mixed = _gate_mix(vn[rows], w_ref, b_ref[...])
            o_ref[rows, :] = (u[rows] * mixed).astype(BF16)

    return pl.pallas_call(
        body, name="gate_fwd", grid=(S // tm,),
        in_specs=[pl.BlockSpec((tm, D_HEADS), lambda i: (i, 0)), pl.BlockSpec((tm, D_HEADS), lambda i: (i, 1)),
                  _full((N_HEADS, SG_BLOCK, SG_BLOCK)), _full((1, D_HEADS)), _full((SG_BLOCK, D_HEADS)),
                  _full((D_HEADS, D_HEADS))],
        out_specs=pl.BlockSpec((tm, D_HEADS), lambda i: (i, 0)),
        out_shape=jax.ShapeDtypeStruct((S, D_HEADS), BF16),
        compiler_params=_params(("parallel",)),
    )(z, z, w_mask, ln_row, b_full, seg_avg)


def _mix_out(x, out_a, out_b, w_out, g2, tm):
    S = x.shape[0]

    def body(x_ref, a_ref, b_ref, w_ref, g_ref, x1_ref, h_ref):
        y = jnp.dot(a_ref[...], w_ref[:D_HEADS, :], preferred_element_type=F32)
        y = y + jnp.dot(b_ref[...], w_ref[D_HEADS:, :], preferred_element_type=F32)
        x1 = x_ref[...] + y
        x1_ref[...] = x1
        r = lax.rsqrt(jnp.mean(x1 * x1, axis=-1, keepdims=True) + EPS)
        h_ref[...] = (x1 * r * g_ref[...]).astype(BF16)

    row = lambda w: pl.BlockSpec((tm, w), lambda i: (i, 0))
    return pl.pallas_call(
        body, name="mix_out", grid=(S // tm,),
        in_specs=[row(D_MODEL), row(D_HEADS), row(D_HEADS), _full((D_MODEL, D_MODEL)), _full((1, D_MODEL))],
        out_specs=[row(D_MODEL), row(D_MODEL)],
        out_shape=[jax.ShapeDtypeStruct((S, D_MODEL), F32), jax.ShapeDtypeStruct((S, D_MODEL), BF16)],
        compiler_params=_params(("parallel",)),
    )(x, out_a, out_b, w_out, g2)


def _up_proj(h2, w_up_q, tm):
    S = h2.shape[0]
    nq, _, wq = w_up_q.shape

    def body(h_ref, w_ref, a_ref):
        a_ref[...] = jnp.dot(h_ref[...], w_ref[...], preferred_element_type=F32).astype(BF16)

    return pl.pallas_call(
        body, name="up_proj", grid=(nq, S // tm),
        in_specs=[pl.BlockSpec((tm, D_MODEL), lambda j, i: (i, 0)), pl.BlockSpec((None, D_MODEL, wq), lambda j, i: (j, 0, 0))],
        out_specs=pl.BlockSpec((tm, wq), lambda j, i: (i, j)),
        out_shape=jax.ShapeDtypeStruct((S, nq * wq), BF16),
        compiler_params=_params(("parallel", "parallel")),
    )(h2, w_up_q)


def _shift_down(a, halo, k):
    tm = a.shape[0]
    ra = pltpu.roll(a, k, 0)
    rh = pltpu.roll(halo, k, 0)
    row = lax.broadcasted_iota(jnp.int32, halo.shape, 0)
    top = jnp.where(row < k, rh, ra[0:SUBLANES])
    return jnp.concatenate([top, ra[SUBLANES:tm]], axis=0)


def _shift_up(a, halo, k):
    tm = a.shape[0]
    ra = pltpu.roll(a, tm - k, 0)
    rh = pltpu.roll(halo, SUBLANES - k, 0)
    row = lax.broadcasted_iota(jnp.int32, halo.shape, 0)
    bottom = jnp.where(row >= SUBLANES - k, rh, ra[tm - SUBLANES:tm])
    return jnp.concatenate([ra[0:tm - SUBLANES], bottom], axis=0)


def _conv_taps(a_ref, halo_ref, first):
    a = a_ref[...].astype(F32)
    halo = halo_ref[...].astype(F32) * jnp.where(first, 0.0, 1.0)
    return a, _shift_down(a, halo, 1), _shift_down(a, halo, 2)


def _conv_specs(tm):
    step = tm // SUBLANES
    prev = lambda i: jnp.maximum(i * step - 1, 0)
    return [pl.BlockSpec((tm, D_FF), lambda i: (i, 0)), pl.BlockSpec((tm, D_FF), lambda i: (i, 1)),
            pl.BlockSpec((SUBLANES, D_FF), lambda i: (prev(i), 0)), pl.BlockSpec((SUBLANES, D_FF), lambda i: (prev(i), 1))]


def _ffn_fwd_loss(a, w_conv, b_conv, w_down, x1, g3, target, tm):
    S = x1.shape[0]

    def body(ag_ref, av_ref, hg_ref, hv_ref, wg_ref, wv_ref, bg_ref, bv_ref, wd_ref, x1_ref, g_ref, t_ref,
             dx2_ref, loss_ref, dg_ref):
        i = pl.program_id(0)

        @pl.when(i == 0)
        def _():
            loss_ref[...] = jnp.zeros_like(loss_ref)
            dg_ref[...] = jnp.zeros_like(dg_ref)

        g0, g1, g2 = _conv_taps(ag_ref, hg_ref, i == 0)
        gate = wg_ref[2:3, :] * g0 + wg_ref[1:2, :] * g1 + wg_ref[0:1, :] * g2 + bg_ref[...]
        v0, v1, v2 = _conv_taps(av_ref, hv_ref, i == 0)
        val = wv_ref[2:3, :] * v0 + wv_ref[1:2, :] * v1 + wv_ref[0:1, :] * v2 + bv_ref[...]
        y = (gate * jax.nn.sigmoid(gate) * val).astype(BF16)
        x2 = x1_ref[...] + jnp.dot(y, wd_ref[...], preferred_element_type=F32)
        r = lax.rsqrt(jnp.mean(x2 * x2, axis=-1, keepdims=True) + EPS)
        xhat = x2 * r
        gg = g_ref[...]
        err = xhat * gg - t_ref[...]
        loss_ref[...] += jnp.sum(err * err, axis=0, keepdims=True)
        dy = err * (1.0 / D_MODEL)
        dg_ref[...] += jnp.sum(dy * xhat, axis=0, keepdims=True)
        dxhat = dy * gg
        dx2_ref[...] = r * (dxhat - xhat * jnp.mean(dxhat * xhat, axis=-1, keepdims=True))

    row = lambda w: pl.BlockSpec((tm, w), lambda i: (i, 0))
    half = lambda r: [pl.BlockSpec((r, D_FF), lambda i: (0, 0)), pl.BlockSpec((r, D_FF), lambda i: (0, 1))]
    return pl.pallas_call(
        body, name="ffn_fwd_loss", grid=(S // tm,),
        in_specs=_conv_specs(tm) + half(3) + half(1) + [_full((D_FF, D_MODEL)), row(D_MODEL), _full((1, D_MODEL)), row(D_MODEL)],
        out_specs=[row(D_MODEL), _full((1, D_MODEL)), _full((1, D_MODEL))],
        out_shape=[jax.ShapeDtypeStruct((S, D_MODEL), F32), jax.ShapeDtypeStruct((1, D_MODEL), F32),
                   jax.ShapeDtypeStruct((1, D_MODEL), F32)],
        compiler_params=_params(("arbitrary",)),
    )(a, a, a, a, w_conv, w_conv, b_conv, b_conv, w_down, x1, g3, target)


def _ffn_bwd_gate(dx2, a, w_conv, b_conv, w_down, tm):
    S = dx2.shape[0]

    def body(dx_ref, ag_ref, av_ref, hg_ref, hv_ref, wg_ref, wv_ref, bg_ref, bv_ref, wd_ref,
             dc_ref, y_ref, dw_ref, db_ref):
        i = pl.program_id(0)

        @pl.when(i == 0)
        def _():
            dw_ref[...] = jnp.zeros_like(dw_ref)
            db_ref[...] = jnp.zeros_like(db_ref)

        g0, g1, g2 = _conv_taps(ag_ref, hg_ref, i == 0)
        gate = wg_ref[2:3, :] * g0 + wg_ref[1:2, :] * g1 + wg_ref[0:1, :] * g2 + bg_ref[...]
        v0, v1, v2 = _conv_taps(av_ref, hv_ref, i == 0)
        val = wv_ref[2:3, :] * v0 + wv_ref[1:2, :] * v1 + wv_ref[0:1, :] * v2 + bv_ref[...]
        sg = jax.nn.sigmoid(gate)
        act = gate * sg
        y_ref[...] = (act * val).astype(BF16)
        dy = lax.dot_general(dx_ref[...].astype(BF16), wd_ref[...], NT, preferred_element_type=F32)
        dgate = dy * val * (sg * (1.0 + gate * (1.0 - sg)))
        dval = dy * act
        dc_ref[:, :D_FF] = dgate.astype(BF16)
        dc_ref[:, D_FF:] = dval.astype(BF16)
        for half, (d, taps) in enumerate(((dgate, (g2, g1, g0)), (dval, (v2, v1, v0)))):
            cols = slice(half * D_FF, (half + 1) * D_FF)
            db_ref[0:1, cols] += jnp.sum(d, axis=0, keepdims=True)
            for j in range(3):
                dw_ref[j:j + 1, cols] += jnp.sum(d * taps[j], axis=0, keepdims=True)

    row = lambda w: pl.BlockSpec((tm, w), lambda i: (i, 0))
    half = lambda r: [pl.BlockSpec((r, D_FF), lambda i: (0, 0)), pl.BlockSpec((r, D_FF), lambda i: (0, 1))]
    return pl.pallas_call(
        body, name="ffn_bwd_gate", grid=(S // tm,),
        in_specs=[row(D_MODEL)] + _conv_specs(tm) + half(3) + half(1) + [_full((D_FF, D_MODEL))],
        out_specs=[row(2 * D_FF), row(D_FF), _full((SUBLANES, 2 * D_FF)), _full((1, 2 * D_FF))],
        out_shape=[jax.ShapeDtypeStruct((S, 2 * D_FF), BF16), jax.ShapeDtypeStruct((S, D_FF), BF16),
                   jax.ShapeDtypeStruct((SUBLANES, 2 * D_FF), F32), jax.ShapeDtypeStruct((1, 2 * D_FF), F32)],
        compiler_params=_params(("arbitrary",)),
    )(dx2, a, a, a, a, w_conv, w_conv, b_conv, b_conv, w_down)


def _conv_bwd(dc, w_conv, tm, tn):
    S, C = dc.shape
    step = tm // SUBLANES
    last_blk = S // SUBLANES - 1

    def body(d_ref, nx_ref, w_ref, o_ref):
        last = pl.program_id(0) == pl.num_programs(0) - 1
        d = d_ref[...].astype(F32)
        nx = nx_ref[...].astype(F32) * jnp.where(last, 0.0, 1.0)
        out = w_ref[2:3, :] * d + w_ref[1:2, :] * _shift_up(d, nx, 1) + w_ref[0:1, :] * _shift_up(d, nx, 2)
        o_ref[...] = out.astype(BF16)

    return pl.pallas_call(
        body, name="conv_bwd", grid=(S // tm, C // tn),
        in_specs=[pl.BlockSpec((tm, tn), lambda i, j: (i, j)),
                  pl.BlockSpec((SUBLANES, tn), lambda i, j: (jnp.minimum((i + 1) * step, last_blk), j)),
                  pl.BlockSpec((3, tn), lambda i, j: (0, j))],
        out_specs=pl.BlockSpec((tm, tn), lambda i, j: (i, j)),
        out_shape=jax.ShapeDtypeStruct((S, C), BF16),
        compiler_params=_params(("parallel", "parallel")),
    )(dc, dc, w_conv)


def _matmul_tn(a, b, name, bm, bn, tk, col_a=0, col_b=0, quarters=None):
    S = a.shape[0]
    gm, gn = quarters if quarters else (1, 1)
    nk = S // tk

    def body(a_ref, b_ref, o_ref):
        @pl.when(pl.program_id(2) == 0)
        def _():
            o_ref[...] = jnp.zeros_like(o_ref)

        o_ref[...] += lax.dot_general(a_ref[...].astype(BF16), b_ref[...].astype(BF16), TN, preferred_element_type=F32)

    if quarters and gn > 1:
        out_spec = pl.BlockSpec((None, bm, bn), lambda i, j, k: (j, i, 0))
        out_shape = jax.ShapeDtypeStruct((gn, gm * bm, bn), F32)
    else:
        out_spec = pl.BlockSpec((bm, bn), lambda i, j, k: (i, j))
        out_shape = jax.ShapeDtypeStruct((gm * bm, gn * bn), F32)
    return pl.pallas_call(
        body, name=name, grid=(gm, gn, nk),
        in_specs=[pl.BlockSpec((tk, bm), lambda i, j, k: (k, col_a * gm + i)),
                  pl.BlockSpec((tk, bn), lambda i, j, k: (k, col_b * gn + j))],
        out_specs=out_spec, out_shape=out_shape,
        compiler_params=_params(("parallel", "parallel", "arbitrary")),
    )(a, b)


def _up_bwd(dact, w_up_q, x1, g2, dx2, tm):
    S = x1.shape[0]
    nq, _, wq = w_up_q.shape

    def body(d_ref, w_ref, x_ref, g_ref, dx2_ref, dx1_ref, dg_ref):
        @pl.when(pl.program_id(0) == 0)
        def _():
            dg_ref[...] = jnp.zeros_like(dg_ref)

        dh = jnp.zeros((tm, D_MODEL), F32)
        for j in range(nq):
            dh = dh + lax.dot_general(d_ref[:, j * wq:(j + 1) * wq], w_ref[j], NT, preferred_element_type=F32)
        dx, dg = _rms_bwd(dh, x_ref[...], g_ref[...])
        dg_ref[...] += dg
        dx1_ref[...] = dx2_ref[...] + dx

    row = lambda w: pl.BlockSpec((tm, w), lambda i: (i, 0))
    return pl.pallas_call(
        body, name="up_bwd", grid=(S // tm,),
        in_specs=[row(nq * wq), pl.BlockSpec((nq, D_MODEL, wq), lambda i: (0, 0, 0), pipeline_mode=pl.Buffered(1)),
                  row(D_MODEL), _full((1, D_MODEL)), row(D_MODEL)],
        out_specs=[row(D_MODEL), _full((1, D_MODEL))],
        out_shape=[jax.ShapeDtypeStruct((S, D_MODEL), F32), jax.ShapeDtypeStruct((1, D_MODEL), F32)],
        compiler_params=_params(("arbitrary",)),
    )(dact, w_up_q, x1, g2, dx2)


def _out_bwd(dx1, w_out, tm):
    S = dx1.shape[0]

    def body(d_ref, w_ref, o_ref):
        o_ref[...] = lax.dot_general(d_ref[...].astype(BF16), w_ref[...], NT, preferred_element_type=F32).astype(BF16)

    return pl.pallas_call(
        body, name="out_bwd", grid=(S // tm,),
        in_specs=[pl.BlockSpec((tm, D_MODEL), lambda i: (i, 0)), _full((D_MODEL, D_MODEL))],
        out_specs=pl.BlockSpec((tm, D_MODEL), lambda i: (i, 0)),
        out_shape=jax.ShapeDtypeStruct((S, D_MODEL), BF16),
        compiler_params=_params(("parallel",)),
    )(dx1, w_out)


def _gate_bwd(z, dcat, w_mask, w_mask_t, ln_row, b_full, seg_avg, head_ind, tm):
    S = z.shape[0]
    nb = tm // SG_BLOCK

    def body(zu_ref, zv_ref, do_ref, w_ref, wt_ref, ln_ref, b_ref, avg_ref, ind_ref,
             dzu_ref, dzv_ref, dw_ref, db_ref, dln_ref, dvn_s, dbf_s):
        i = pl.program_id(0)

        @pl.when(i == 0)
        def _():
            dw_ref[...] = jnp.zeros_like(dw_ref)
            dln_ref[...] = jnp.zeros_like(dln_ref)
            dbf_s[...] = jnp.zeros_like(dbf_s)

        zu = zu_ref[...].astype(F32)
        zv = zv_ref[...].astype(F32)
        u = _gelu(zu)
        v = _gelu(zv)
        avg = avg_ref[...]
        vhat, rstd = _layer_norm_heads(v, avg)
        ln = ln_ref[...]
        vn = vhat * ln
        for b in range(nb):
            rows = slice(b * SG_BLOCK, (b + 1) * SG_BLOCK)
            vn_b = vn[rows]
            mixed = _gate_mix(vn_b, w_ref, b_ref[...])
            do = do_ref[rows, :].astype(F32)
            dzu_ref[rows, :] = (do * mixed * _gelu_grad(zu[rows])).astype(BF16)
            dmix = do * u[rows]
            dbf_s[...] += dmix
            vn_bf = vn_b.astype(BF16)
            dvn = jnp.zeros((SG_BLOCK, D_HEADS), F32)
            for h in range(N_HEADS):
                dmh = jnp.where(_head_mask(h, SG_BLOCK), dmix, 0.0).astype(BF16)
                dw_ref[h] += lax.dot_general(dmh, vn_bf, NT, preferred_element_type=F32)
                dvn = dvn + jnp.dot(wt_ref[h], dmh, preferred_element_type=F32)
            dvn_s[rows, :] = dvn
        dvn = dvn_s[...]
        dln_ref[...] += jnp.sum(dvn * vhat, axis=0, keepdims=True)
        dvhat = dvn * ln
        dv = rstd * (dvhat - _split_dot(dvhat, avg) - vhat * _split_dot(dvhat * vhat, avg))
        dzv_ref[...] = (dv * _gelu_grad(zv)).astype(BF16)

        @pl.when(i == pl.num_programs(0) - 1)
        def _():
            r = lax.broadcasted_iota(jnp.int32, (SG_BLOCK, SG_BLOCK), 0) // CHUNK
            s = lax.broadcasted_iota(jnp.int32, (SG_BLOCK, SG_BLOCK), 1) // CHUNK
            for h in range(N_HEADS):
                dw_ref[h] = jnp.where(r >= s, dw_ref[h], 0.0)
            db_ref[...] = _split_dot(dbf_s[...], ind_ref[...])

    row = lambda col: pl.BlockSpec((tm, D_HEADS), lambda i: (i, col))
    wspec = _full((N_HEADS, SG_BLOCK, SG_BLOCK))
    return pl.pallas_call(
        body, name="gate_bwd", grid=(S // tm,),
        in_specs=[row(0), row(1), row(0), wspec, wspec, _full((1, D_HEADS)), _full((SG_BLOCK, D_HEADS)),
                  _full((D_HEADS, D_HEADS)), _full((D_HEADS, LANES))],
        out_specs=[row(0), row(0), wspec, _full((SG_BLOCK, LANES)), _full((1, D_HEADS))],
        out_shape=[jax.ShapeDtypeStruct((S, D_HEADS), BF16), jax.ShapeDtypeStruct((S, D_HEADS), BF16),
                   jax.ShapeDtypeStruct((N_HEADS, SG_BLOCK, SG_BLOCK), F32), jax.ShapeDtypeStruct((SG_BLOCK, LANES), F32),
                   jax.ShapeDtypeStruct((1, D_HEADS), F32)],
        scratch_shapes=[pltpu.VMEM((tm, D_HEADS), F32), pltpu.VMEM((SG_BLOCK, D_HEADS), F32)],
        compiler_params=_params(("arbitrary",)),
    )(z, z, dcat, w_mask, w_mask_t, ln_row, b_full, seg_avg, head_ind)


def _attn_delta(o, dcat, head_ind, tm):
    S = o.shape[0]

    def body(o_ref, do_ref, ind_ref, d_ref):
        d_ref[...] = _split_dot(o_ref[...].astype(F32) * do_ref[...].astype(F32), ind_ref[...])

    return pl.pallas_call(
        body, name="attn_delta", grid=(S // tm,),
        in_specs=[pl.BlockSpec((tm, D_HEADS), lambda i: (i, 0)), pl.BlockSpec((tm, D_HEADS), lambda i: (i, 1)),
                  _full((D_HEADS, LANES))],
        out_specs=pl.BlockSpec((tm, LANES), lambda i: (i, 0)),
        out_shape=jax.ShapeDtypeStruct((S, LANES), F32),
        compiler_params=_params(("parallel",)),
    )(o, dcat, head_ind)


def _attn_bwd(z, dcat, c, ct, lse, delta, tq):
    S = z.shape[0]
    n = S // tq

    def body(q_ref, k_ref, v_ref, do_ref, c_ref, ct_ref, lse_ref, dl_ref,
             dq_hbm, dcr_hbm, dk_ref, dv_ref, dct_ref, dq_s, dcr_s, dk_s, dv_s, dc_s, sems):
        ki, qi = pl.program_id(0), pl.program_id(1)

        @pl.when((ki == 0) & (qi == 0))
        def _():
            dq_s[...] = jnp.zeros_like(dq_s)
            dcr_s[...] = jnp.zeros_like(dcr_s)

        @pl.when(qi == ki)
        def _():
            dk_s[...] = jnp.zeros_like(dk_s)
            dv_s[...] = jnp.zeros_like(dv_s)
            dc_s[...] = jnp.zeros_like(dc_s)

        def step(diagonal):
            rows = pl.ds(pl.multiple_of(qi * tq, tq), tq)
            if diagonal:
                row = lax.broadcasted_iota(jnp.int32, (tq, tq), 0)
                col = lax.broadcasted_iota(jnp.int32, (tq, tq), 1)
                keep = row >= col
            for h in range(N_HEADS):
                sl = slice(h * HEAD_DIM, (h + 1) * HEAD_DIM)
                q, k, v, do = q_ref[:, sl], k_ref[:, sl], v_ref[:, sl], do_ref[:, sl]
                s = lax.dot_general(q, k, NT, preferred_element_type=F32) * SCALE
                s = s + (c_ref[:, h:h + 1] - ct_ref[h:h + 1, :])
                p = jnp.exp(s - lse_ref[:, h:h + 1])
                if diagonal:
                    p = jnp.where(keep, p, 0.0)
                dv_s[:, sl] += lax.dot_general(p.astype(BF16), do, TN, preferred_element_type=F32)
                dp = lax.dot_general(do, v, NT, preferred_element_type=F32)
                ds = p * (dp - dl_ref[:, h:h + 1])
                dc_s[h:h + 1, :] += jnp.sum(ds, axis=0, keepdims=True)
                dcr_s[rows, h:h + 1] += jnp.sum(ds, axis=1, keepdims=True)
                dsb = ds.astype(BF16)
                dk_s[:, sl] += lax.dot_general(dsb, q, TN, preferred_element_type=F32) * SCALE
                dq_s[rows, sl] += jnp.dot(dsb, k, preferred_element_type=F32) * SCALE

        @pl.when(qi > ki)
        def _():
            step(False)

        @pl.when(qi == ki)
        def _():
            step(True)

        @pl.when(qi == n - 1)
        def _():
            dk_ref[...] = dk_s[...].astype(BF16)
            dv_ref[...] = dv_s[...].astype(BF16)
            dct_ref[...] = dc_s[...]

        @pl.when((ki == n - 1) & (qi == n - 1))
        def _():
            copies = [pltpu.make_async_copy(dq_s, dq_hbm, sems.at[0]), pltpu.make_async_copy(dcr_s, dcr_hbm, sems.at[1])]
            for cp in copies:
                cp.start()
            for cp in copies:
                cp.wait()

    qspec = lambda w, col: pl.BlockSpec((tq, w), lambda ki, qi: (jnp.maximum(qi, ki), col))
    kspec = lambda col: pl.BlockSpec((tq, D_HEADS), lambda ki, qi: (ki, col))
    return pl.pallas_call(
        body, name="attn_bwd", grid=(n, n),
        in_specs=[qspec(D_HEADS, 2), kspec(3), kspec(4), qspec(D_HEADS, 1), qspec(LANES, 0),
                  pl.BlockSpec((N_HEADS, tq), lambda ki, qi: (0, ki)), qspec(LANES, 0), qspec(LANES, 0)],
        out_specs=[_ANY, _ANY, kspec(0), kspec(0), pl.BlockSpec((N_HEADS, tq), lambda ki, qi: (0, ki))],
        out_shape=[jax.ShapeDtypeStruct((S, D_HEADS), F32), jax.ShapeDtypeStruct((S, LANES), F32), jax.ShapeDtypeStruct((S, D_HEADS), BF16),
                   jax.ShapeDtypeStruct((S, D_HEADS), BF16), jax.ShapeDtypeStruct((N_HEADS, S), F32)],
        scratch_shapes=[pltpu.VMEM((S, D_HEADS), F32), pltpu.VMEM((S, LANES), F32), pltpu.VMEM((tq, D_HEADS), F32),
                        pltpu.VMEM((tq, D_HEADS), F32), pltpu.VMEM((N_HEADS, tq), F32), pltpu.SemaphoreType.DMA((2,))],
        compiler_params=_params(("arbitrary", "arbitrary")),
    )(z, z, z, dcat, c, ct, lse, delta)


def _fox_bwd(dc, f, bias_row, tb):
    S = f.shape[0]
    nb = S // tb

    def body(dc_ref, f_ref, b_ref, df_ref, dbias_ref, carry):
        @pl.when(pl.program_id(0) == 0)
        def _():
            carry[...] = jnp.zeros_like(carry)
            dbias_ref[...] = jnp.zeros_like(dbias_ref)

        r = lax.broadcasted_iota(jnp.int32, (tb, tb), 0)
        s = lax.broadcasted_iota(jnp.int32, (tb, tb), 1)
        tri = (s >= r).astype(F32)
        rc = jnp.dot(tri, dc_ref[...], precision=lax.Precision.HIGHEST, preferred_element_type=F32) + carry[0:1, :]
        carry[...] = jnp.broadcast_to(rc[0:1, :], carry.shape)
        lane = lax.broadcasted_iota(jnp.int32, (tb, LANES), 1)
        df = jnp.where(lane < N_HEADS, rc * jax.nn.sigmoid(-(f_ref[...] + b_ref[...])), 0.0)
        df_ref[...] = df.astype(BF16)
        dbias_ref[...] += jnp.sum(df, axis=0, keepdims=True)

    rev = pl.BlockSpec((tb, LANES), lambda i: (nb - 1 - i, 0))
    return pl.pallas_call(
        body, name="fox_bwd", grid=(nb,),
        in_specs=[rev, rev, _full((1, LANES))],
        out_specs=[rev, _full((1, LANES))],
        out_shape=[jax.ShapeDtypeStruct((S, LANES), BF16), jax.ShapeDtypeStruct((1, LANES), F32)],
        scratch_shapes=[pltpu.VMEM((SUBLANES, LANES), F32)],
        compiler_params=_params(("arbitrary",)),
    )(dc, f, bias_row)


_DZ_WIDTHS = (D_HEADS,) * 5 + (LANES,)


def _in_bwd(pieces, w_in, x, g1, dx1, tm):
    S = x.shape[0]

    def body(*refs):
        p_refs, (w_ref, x_ref, g_ref, dx1_ref, dx_ref, dg_ref) = refs[:6], refs[6:]

        @pl.when(pl.program_id(0) == 0)
        def _():
            dg_ref[...] = jnp.zeros_like(dg_ref)

        dh = jnp.zeros((tm, D_MODEL), F32)
        off = 0
        for p_ref, w in zip(p_refs, _DZ_WIDTHS):
            dh = dh + lax.dot_general(p_ref[...].astype(BF16), w_ref[:, off:off + w], NT, preferred_element_type=F32)
            off += w
        dx, dg = _rms_bwd(dh, x_ref[...], g_ref[...])
        dg_ref[...] += dg
        dx_ref[...] = dx1_ref[...] + dx

    row = lambda w: pl.BlockSpec((tm, w), lambda i: (i, 0))
    return pl.pallas_call(
        body, name="in_bwd", grid=(S // tm,),
        in_specs=[row(w) for w in _DZ_WIDTHS] + [_full((D_MODEL, D_IN_PAD)), row(D_MODEL), _full((1, D_MODEL)), row(D_MODEL)],
        out_specs=[row(D_MODEL), _full((1, D_MODEL))],
        out_shape=[jax.ShapeDtypeStruct((S, D_MODEL), F32), jax.ShapeDtypeStruct((1, D_MODEL), F32)],
        compiler_params=_params(("arbitrary",)),
    )(*pieces, w_in, x, g1, dx1)


def _dw_in(h1, pieces, tk):
    S = h1.shape[0]

    def body(*refs):
        h_ref, p_refs, o_ref = refs[0], refs[1:7], refs[7]

        @pl.when(pl.program_id(0) == 0)
        def _():
            o_ref[...] = jnp.zeros_like(o_ref)

        off = 0
        for p_ref, w in zip(p_refs, _DZ_WIDTHS):
            o_ref[:, off:off + w] += lax.dot_general(h_ref[...], p_ref[...].astype(BF16), TN, preferred_element_type=F32)
            off += w

    row = lambda w: pl.BlockSpec((tk, w), lambda k: (k, 0))
    return pl.pallas_call(
        body, name="dw_in", grid=(S // tk,),
        in_specs=[row(D_MODEL)] + [row(w) for w in _DZ_WIDTHS],
        out_specs=_full((D_MODEL, D_IN_PAD)),
        out_shape=jax.ShapeDtypeStruct((D_MODEL, D_IN_PAD), F32),
        compiler_params=_params(("arbitrary",)),
    )(h1, *pieces)


def _adamw_math(w, g, m, v):
    m = ADAM_B1 * m + (1.0 - ADAM_B1) * g
    v = ADAM_B2 * v + (1.0 - ADAM_B2) * (g * g)
    m_hat = m / (1.0 - ADAM_B1 ** ADAM_STEP)
    v_hat = v / (1.0 - ADAM_B2 ** ADAM_STEP)
    delta = -ADAM_LR * (m_hat / (jnp.sqrt(v_hat) + ADAM_EPS) + ADAM_WD * w)
    return delta, m, v


def _adamw(name, w, g, m, v):
    R, C = w.shape
    tr = _row_tile(R, 256)

    def body(w_ref, g_ref, m_ref, v_ref, d_ref, nm_ref, nv_ref):
        d, nm, nv = _adamw_math(w_ref[...], g_ref[...], m_ref[...], v_ref[...])
        d_ref[...] = d
        nm_ref[...] = nm
        nv_ref[...] = nv

    spec = pl.BlockSpec((tr, C), lambda i: (i, 0))
    return pl.pallas_call(
        body, name=name, grid=(R // tr,), in_specs=[spec] * 4, out_specs=[spec] * 3,
        out_shape=[jax.ShapeDtypeStruct((R, C), F32)] * 3,
        compiler_params=_params(("parallel",)),
    )(w, g, m, v)


def _pair_sum(name, grad, theirs, ids):
    q, half, C = theirs.shape
    tr = _row_tile(half, 256)
    nb = half // tr

    def body(ids_ref, a_ref, b_ref, s_ref, sb_ref):
        s = a_ref[...] + b_ref[...]
        s_ref[...] = s
        sb_ref[...] = s.astype(BF16)

    here = pl.BlockSpec((None, tr, C), lambda j, i, ids: (j, i, 0))
    return pl.pallas_call(
        body, name=name,
        grid_spec=pltpu.PrefetchScalarGridSpec(
            num_scalar_prefetch=1, grid=(q, nb),
            in_specs=[pl.BlockSpec((None, tr, C), lambda j, i, ids: (j, ids[1] * nb + i, 0)), here],
            out_specs=[here, here]),
        out_shape=[jax.ShapeDtypeStruct((q, half, C), F32), jax.ShapeDtypeStruct((q, half, C), BF16)],
        compiler_params=_params(("parallel", "parallel")),
    )(ids, grad, theirs)


def _chip_sum(name, sums32, others, ids):
    _, half, C = sums32.shape
    tr = _row_tile(half, 256)
    nb = half // tr

    def body(ids_ref, a_ref, o_ref, s_ref):
        s = a_ref[...]
        for j in range(3):
            s = s + o_ref[j].astype(F32)
        s_ref[...] = s

    return pl.pallas_call(
        body, name=name,
        grid_spec=pltpu.PrefetchScalarGridSpec(
            num_scalar_prefetch=1, grid=(nb,),
            in_specs=[pl.BlockSpec((None, tr, C), lambda i, ids: (ids[0], i, 0)),
                      pl.BlockSpec((3, tr, C), lambda i, ids: (0, i, 0))],
            out_specs=pl.BlockSpec((tr, C), lambda i, ids: (ids[1] * nb + i, 0))),
        out_shape=jax.ShapeDtypeStruct((2 * half, C), F32),
        compiler_params=_params(("parallel",)),
    )(ids, sums32, others)


def _place():
    return lax.axis_index("x"), lax.axis_index("y"), lax.axis_index("c")


def _other_chips(x, y):
    return [(1 - x, y), (x, 1 - y), (1 - x, 1 - y)]


_ANY = pl.BlockSpec(memory_space=pl.ANY)


def _gather_quarters(shards):
    n = len(shards)
    halved = [s.shape[0] % 32 == 0 for s in shards]

    def body(*refs):
        ins, outs = refs[:n], refs[n:2 * n]
        send_sems, recv_sems, pass_send_sems, pass_recv_sems = refs[2 * n:]
        x, y, c = _place()
        mine = 2 * x + y
        chips = _other_chips(x, y)

        def part(a, quarter, core):
            if not halved[a]:
                return outs[a].at[quarter]
            half = ins[a].shape[0] // 2
            return outs[a].at[quarter, pl.ds(core * half, half), :]

        def source(a):
            if not halved[a]:
                return ins[a]
            half = ins[a].shape[0] // 2
            return ins[a].at[pl.ds(c * half, half), :]

        sends = []
        for a in range(n):
            for j, (px, py) in enumerate(chips):
                cp = pltpu.make_async_remote_copy(src_ref=source(a), dst_ref=part(a, mine, c), send_sem=send_sems.at[a, j],
                                                  recv_sem=recv_sems.at[a, j], device_id=(px, py, c), device_id_type=MESH)
                cp.start()
                sends.append(cp)
        for a in range(n):
            for j, (px, py) in enumerate(chips):
                landed = part(a, 2 * px + py, c)
                pltpu.make_async_remote_copy(src_ref=source(a), dst_ref=landed, send_sem=send_sems.at[a, j],
                                             recv_sem=recv_sems.at[a, j], device_id=(px, py, c), device_id_type=MESH).wait_recv()
                if halved[a]:
                    cp = pltpu.make_async_remote_copy(src_ref=landed, dst_ref=landed, send_sem=pass_send_sems.at[a, j],
                                                      recv_sem=pass_recv_sems.at[a, j], device_id=(x, y, 1 - c), device_id_type=MESH)
                    cp.start()
                    sends.append(cp)
        for a in range(n):
            if halved[a]:
                for j, (px, py) in enumerate(chips):
                    other = part(a, 2 * px + py, 1 - c)
                    pltpu.make_async_remote_copy(src_ref=other, dst_ref=other, send_sem=pass_send_sems.at[a, j],
                                                 recv_sem=pass_recv_sems.at[a, j], device_id=(x, y, 1 - c),
                                                 device_id_type=MESH).wait_recv()
        for cp in sends:
            cp.wait_send()

    sems = pltpu.SemaphoreType.DMA((n, 3))
    return pl.pallas_call(
        body, name="gather_weights",
        in_specs=[_ANY] * n, out_specs=[_ANY] * n,
        out_shape=[jax.ShapeDtypeStruct((4,) + s.shape, s.dtype) for s in shards],
        scratch_shapes=[sems, sems, sems, sems],
    )(*shards)


def _swap_halves(grads):
    n = len(grads)

    def body(*refs):
        ins, outs = refs[:n], refs[n:2 * n]
        send_sems, recv_sems = refs[2 * n:]
        x, y, c = _place()
        started = []
        for a in range(n):
            half = ins[a].shape[1] // 2
            cp = pltpu.make_async_remote_copy(src_ref=ins[a].at[:, pl.ds((1 - c) * half, half), :], dst_ref=outs[a],
                                              send_sem=send_sems.at[a], recv_sem=recv_sems.at[a],
                                              device_id=(x, y, 1 - c), device_id_type=MESH)
            cp.start()
            started.append(cp)
        for cp in started:
            cp.wait()

    return pl.pallas_call(
        body, name="swap_halves",
        in_specs=[_ANY] * n, out_specs=[_ANY] * n,
        out_shape=[jax.ShapeDtypeStruct((4, g.shape[1] // 2, g.shape[2]), F32) for g in grads],
        scratch_shapes=[pltpu.SemaphoreType.DMA((n,)), pltpu.SemaphoreType.DMA((n,))],
    )(*grads)


def _scatter_quarters(sums16):
    n = len(sums16)

    def body(*refs):
        ins, outs = refs[:n], refs[n:2 * n]
        send_sems, recv_sems = refs[2 * n:]
        x, y, c = _place()
        sends = []
        for a in range(n):
            for j, (px, py) in enumerate(_other_chips(x, y)):
                cp = pltpu.make_async_remote_copy(src_ref=ins[a].at[2 * px + py], dst_ref=outs[a].at[j],
                                                  send_sem=send_sems.at[a, j], recv_sem=recv_sems.at[a, j],
                                                  device_id=(px, py, c), device_id_type=MESH)
                cp.start()
                sends.append(cp)
        for cp in sends:
            cp.wait()

    return pl.pallas_call(
        body, name="scatter_quarters",
        in_specs=[_ANY] * n, out_specs=[_ANY] * n,
        out_shape=[jax.ShapeDtypeStruct((3,) + s.shape[1:], BF16) for s in sums16],
        scratch_shapes=[pltpu.SemaphoreType.DMA((n, 3)), pltpu.SemaphoreType.DMA((n, 3))],
    )(*sums16)


def _join_halves(fulls):
    n = len(fulls)

    def body(*refs):
        ins, outs = refs[:n], refs[n:2 * n]
        send_sems, recv_sems = refs[2 * n:]
        x, y, c = _place()
        started = []
        for a in range(n):
            half = ins[a].shape[0] // 2
            rows = pl.ds(c * half, half)
            cp = pltpu.make_async_remote_copy(src_ref=ins[a].at[rows, :], dst_ref=outs[a].at[rows, :], send_sem=send_sems.at[a],
                                              recv_sem=recv_sems.at[a], device_id=(x, y, 1 - c), device_id_type=MESH)
            cp.start()
            started.append(cp)
        for cp in started:
            cp.wait()

    return pl.pallas_call(
        body, name="join_halves",
        in_specs=[_ANY] * n, out_specs=[_ANY] * n,
        out_shape=[jax.ShapeDtypeStruct(f.shape, F32) for f in fulls],
        input_output_aliases={a: a for a in range(n)},
        scratch_shapes=[pltpu.SemaphoreType.DMA((n,)), pltpu.SemaphoreType.DMA((n,))],
    )(*fulls)


def _small_allreduce_adamw(g, w, m, v):
    R = g.shape[0]

    def body(g_ref, w_ref, m_ref, v_ref, gs_ref, d_ref, nm_ref, nv_ref, all_s, send_sems, recv_sems):
        x, y, c = _place()
        me = 4 * x + 2 * y + c
        all_s[me] = g_ref[...]
        sends = []
        for k in range(1, 8):
            peer = (x ^ (k >> 2), y ^ ((k >> 1) & 1), c ^ (k & 1))
            cp = pltpu.make_async_remote_copy(src_ref=g_ref, dst_ref=all_s.at[me], send_sem=send_sems.at[k - 1],
                                              recv_sem=recv_sems.at[k - 1], device_id=peer, device_id_type=MESH)
            cp.start()
            sends.append(cp)
        for cp in sends:
            cp.wait()
        total = all_s[0]
        for d in range(1, 8):
            total = total + all_s[d]
        gs_ref[...] = total
        delta, nm, nv = _adamw_math(w_ref[...], total, m_ref[...], v_ref[...])
        d_ref[...] = delta
        nm_ref[...] = nm
        nv_ref[...] = nv

    vm = pl.BlockSpec(memory_space=pltpu.VMEM)
    return pl.pallas_call(
        body, name="small_allreduce_adamw",
        in_specs=[vm] * 4, out_specs=[vm] * 4, out_shape=[jax.ShapeDtypeStruct((R, LANES), F32)] * 4,
        scratch_shapes=[pltpu.VMEM((8, R, LANES), F32), pltpu.SemaphoreType.DMA((7,)), pltpu.SemaphoreType.DMA((7,))],
        compiler_params=pltpu.CompilerParams(vmem_limit_bytes=VMEM_LIMIT),
    )(g, w, m, v)


_SMALL = (("norm_mix_g", D_MODEL), ("f_bias", N_HEADS), ("sg_ln_g", D_HEADS), ("sg_w", N_HEADS * SG_BLOCK * SG_BLOCK),
          ("sg_b", N_HEADS * SG_BLOCK), ("norm_ffn_g", D_MODEL), ("w_conv", 3 * 2 * D_FF), ("b_conv", 2 * D_FF),
          ("norm_final_g", D_MODEL))


def _pack_small(parts):
    rows = []
    for name, size in _SMALL:
        flat = parts[name].reshape(-1).astype(F32)
        pad = (-size) % (SUBLANES * LANES)
        rows.append(jnp.pad(flat, (0, pad)).reshape(-1, LANES))
    return jnp.concatenate(rows, axis=0)


def _unpack_small(packed, shapes):
    out, r = {}, 0
    for name, size in _SMALL:
        nrows = (size + SUBLANES * LANES - 1) // (SUBLANES * LANES) * SUBLANES
        out[name] = packed[r:r + nrows].reshape(-1)[:size].reshape(shapes[name])
        r += nrows
    return out


def _local_step(x, target, g1, w_in, f_bias, sg_ln_g, sg_w, sg_b, w_out, g2, w_up_q, w_conv, b_conv, w_down, g3):
    S = x.shape[0]
    tm = _row_tile(S, 512)
    tms = _row_tile(S, 256)
    tq = _row_tile(S, 512)

    lane = jnp.arange(D_HEADS)
    seg_avg = jnp.where(lane[:, None] // HEAD_DIM == lane[None, :] // HEAD_DIM, 1.0 / HEAD_DIM, 0.0).astype(BF16)
    head_ind = (lane[:, None] // HEAD_DIM == jnp.arange(LANES)[None, :]).astype(BF16)
    pos_chunk = jnp.arange(SG_BLOCK) // CHUNK
    w_mask32 = jnp.where(pos_chunk[:, None] >= pos_chunk[None, :], sg_w, 0.0)
    w_mask = w_mask32.astype(BF16)
    w_mask_t = jnp.swapaxes(w_mask32, 1, 2).astype(BF16)
    ln_row = sg_ln_g.reshape(1, D_HEADS)
    b_full = jnp.repeat(sg_b.T, HEAD_DIM, axis=1)
    bias_row = jnp.pad(f_bias.reshape(1, N_HEADS), ((0, 0), (0, LANES - N_HEADS)))
    b_conv_row = b_conv.reshape(1, 2 * D_FF)

    z, f, h1 = _in_proj(x, g1, w_in, tm)
    c, ct = _fox_prep(f, bias_row, _row_tile(S, 256))
    out_b, lse = _attn_fwd(z, c, ct, tq)
    out_a = _gate_fwd(z, w_mask, ln_row, b_full, seg_avg, tm)
    x1, h2 = _mix_out(x, out_a, out_b, w_out, g2, tm)
    a = _up_proj(h2, w_up_q, tm)
    dx2, sq_err, dg3 = _ffn_fwd_loss(a, w_conv, b_conv_row, w_down, x1, g3, target, tms)

    dconv, y, dw_conv8, db_conv = _ffn_bwd_gate(dx2, a, w_conv, b_conv_row, w_down, tms)
    dact = _conv_bwd(dconv, w_conv, tm, 2 * D_FF // 4)
    dw_down = _matmul_tn(y, dx2, "dw_down", D_FF // 2, D_MODEL, tm, quarters=(2, 1))
    dx1, dg2 = _up_bwd(dact, w_up_q, x1, g2, dx2, tms)
    dw_up_q = _matmul_tn(h2, dact, "dw_up", D_MODEL, 2 * D_FF // 4, tm, quarters=(1, 4))
    dcat = _out_bwd(dx1, w_out, tm)
    dw_out_a = _matmul_tn(out_a, dx1, "dw_out_a", D_HEADS, D_MODEL, tm)
    dw_out_b = _matmul_tn(out_b, dx1, "dw_out_b", D_HEADS, D_MODEL, tm)
    dzu, dzv, dsg_w, dsg_b_t, dln = _gate_bwd(z, dcat, w_mask, w_mask_t, ln_row, b_full, seg_avg, head_ind, tm)
    delta = _attn_delta(out_b, dcat, head_ind, tm)
    dq, dc_rows, dk, dv, dc_cols = _attn_bwd(z, dcat, c, ct, lse, delta, tq)
    dc = dc_rows - jnp.pad(dc_cols.T, ((0, 0), (0, LANES - N_HEADS)))
    df, dbias = _fox_bwd(dc, f, bias_row, _row_tile(S, 256))
    pieces = (dzu, dzv, dq, dk, dv, df)
    dx, dg1 = _in_bwd(pieces, w_in, x, g1, dx1, tms)
    dw_in = _dw_in(h1, pieces, tm)

    grads = {
        "norm_mix_g": dg1, "f_bias": dbias[:, :N_HEADS], "sg_ln_g": dln, "sg_w": dsg_w, "sg_b": dsg_b_t[:, :N_HEADS].T,
        "norm_ffn_g": dg2, "w_conv": dw_conv8[:3], "b_conv": db_conv, "norm_final_g": dg3,
        "w_in": dw_in, "w_out": jnp.concatenate([dw_out_a, dw_out_b], axis=0), "w_up_q": dw_up_q, "w_down": dw_down,
    }
    return sq_err, dx, grads


def _reduce_big(grads_q, ids):
    names = list(grads_q)
    theirs = _swap_halves([grads_q[k] for k in names])
    sums = [_pair_sum("pair_sum_" + k, grads_q[k], t, ids) for k, t in zip(names, theirs)]
    got = _scatter_quarters([s16 for _, s16 in sums])
    fulls = [_chip_sum("chip_sum_" + k, s32, g, ids) for k, (s32, _), g in zip(names, sums, got)]
    return dict(zip(names, _join_halves(fulls)))


def kernel(x, norm_mix_g, w_in, f_bias, sg_ln_g, sg_w, sg_b, w_out, norm_ffn_g, w_up, w_conv, b_conv, w_down, norm_final_g, loss_target, m_norm_mix_g, m_w_in, m_f_bias, m_sg_ln_g, m_sg_w, m_sg_b, m_w_out, m_norm_ffn_g, m_w_up, m_w_conv, m_b_conv, m_w_down, m_norm_final_g, v_norm_mix_g, v_w_in, v_f_bias, v_sg_ln_g, v_sg_w, v_sg_b, v_w_out, v_norm_ffn_g, v_w_up, v_w_conv, v_b_conv, v_w_down, v_norm_final_g):
    args = dict(locals())
    quarter = 2 * lax.axis_index("x") + lax.axis_index("y")
    ids = jnp.stack([quarter, lax.axis_index("c")]).astype(jnp.int32)
    wq_conv = w_conv.shape[-1]

    shards = [w_in[0].astype(BF16), w_out[0].astype(BF16), w_up[0].astype(BF16), w_down[0].astype(BF16), w_conv[0]]
    g_in, g_out, g_up, g_down, g_conv = [lax.dynamic_update_index_in_dim(g, s, quarter, 0)
                                         for g, s in zip(_gather_quarters(shards), shards)]
    w_in_full = jnp.pad(jnp.concatenate([g_in[q] for q in range(4)], axis=1), ((0, 0), (0, D_IN_PAD - D_IN)))
    w_out_full = g_out.reshape(D_MODEL, D_MODEL)
    w_down_full = g_down.reshape(D_FF, D_MODEL)
    w_conv_full = jnp.concatenate([g_conv[q] for q in range(4)], axis=1)

    sq_err, dx, grads = _local_step(
        x[0], loss_target[0], norm_mix_g, w_in_full, f_bias[0], sg_ln_g[0], sg_w[0], sg_b[0], w_out_full, norm_ffn_g,
        g_up, w_conv_full, b_conv[0], w_down_full, norm_final_g.reshape(1, D_MODEL))
    loss = lax.psum(0.5 * jnp.sum(sq_err) / D_MODEL, ("x", "y", "c"))

    dw_in = grads["w_in"][:, :D_IN].reshape(D_MODEL, 4, D_IN // 4).transpose(1, 0, 2)
    big = _reduce_big({"w_in": dw_in, "w_out": grads["w_out"].reshape(4, D_MODEL // 4, D_MODEL),
                       "w_up": grads["w_up_q"], "w_down": grads["w_down"].reshape(4, D_FF // 4, D_MODEL)}, ids)

    out = {"loss": loss, "grad_x": dx[None]}
    for k in ("w_in", "w_out", "w_up", "w_down"):
        g = big[k]
        d, nm, nv = _adamw("adamw_" + k, args[k][0], g, args["m_" + k][0], args["v_" + k][0])
        out["grad_" + k], out["delta_" + k], out["new_m_" + k], out["new_v_" + k] = g[None], d[None], nm[None], nv[None]

    def padded_conv(t):
        return lax.dynamic_update_slice(jnp.zeros((3, 4 * wq_conv), F32), t[0], (0, quarter * wq_conv))

    small_names = [n for n, _ in _SMALL]
    shapes = {n: (3, 4 * wq_conv) if n == "w_conv" else args[n].shape for n in small_names}
    pack = lambda prefix: _pack_small({n: padded_conv(args[prefix + n]) if n == "w_conv" else args[prefix + n] for n in small_names})
    packed = _small_allreduce_adamw(_pack_small({n: grads[n] for n in small_names}), pack(""), pack("m_"), pack("v_"))
    for prefix, arr in zip(("grad_", "delta_", "new_m_", "new_v_"), packed):
        for n, t in _unpack_small(arr, shapes).items():
            if n == "w_conv":
                t = lax.dynamic_slice(t, (0, quarter * wq_conv), (3, wq_conv))[None]
            out[prefix + n] = t

    weights = ["norm_mix_g", "w_in", "f_bias", "sg_ln_g", "sg_w", "sg_b", "w_out", "norm_ffn_g", "w_up", "w_conv", "b_conv",
               "w_down", "norm_final_g"]
    return (out["loss"], out["grad_x"], *[out[p + n] for p in ("grad_", "delta_", "new_m_", "new_v_") for n in weights])
```

```python
import functools
import math

import jax
import jax.numpy as jnp
from jax import lax
from jax.experimental import pallas as pl
from jax.experimental.pallas import tpu as pltpu

F32 = jnp.float32
BF16 = jnp.bfloat16
MESH = pl.DeviceIdType.MESH

D_MODEL = 1024
N_HEADS = 8
HEAD_DIM = 64
D_HEADS = N_HEADS * HEAD_DIM
SG_BLOCK = 128
CHUNK = 64
D_FF = 2816
D_IN = 2 * D_HEADS + 3 * D_HEADS + N_HEADS
LANES = 128
SUBLANES = 8
D_IN_PAD = 5 * D_HEADS + LANES
EPS = 1e-6
SCALE = HEAD_DIM ** -0.5
NEG = -1e30
LOG2E = 1.4426950408889634
HEAD_PAD = LANES
D_PAD = N_HEADS * HEAD_PAD
Q_STAT = HEAD_DIM
K_STAT = HEAD_DIM + 3
GROUPS = 2
GROUP_HEADS = N_HEADS // GROUPS
GROUP_PAD = GROUP_HEADS * HEAD_PAD

ADAM_LR = 0.001
ADAM_B1 = 0.9
ADAM_B2 = 0.999
ADAM_EPS = 1e-08
ADAM_WD = 0.01
ADAM_STEP = 10

VMEM_LIMIT = 56 * 1024 * 1024

NT = (((1,), (1,)), ((), ()))
TN = (((0,), (0,)), ((), ()))


def _params(sem):
    return pltpu.CompilerParams(dimension_semantics=sem, vmem_limit_bytes=VMEM_LIMIT)


def _full(shape):
    nd = len(shape)
    return pl.BlockSpec(shape, lambda *_: (0,) * nd)


def _row_tile(rows, target):
    best = None
    for t in range(SUBLANES, min(rows, target) + 1, SUBLANES):
        if rows % t == 0:
            best = t
    assert best is not None, rows
    return best


def _gelu(z):
    return 0.5 * z * (1.0 + lax.erf(z * (2.0 ** -0.5)))


def _gelu_grad(z):
    cdf = 0.5 * (1.0 + lax.erf(z * (2.0 ** -0.5)))
    pdf = jnp.exp(-0.5 * z * z) * (1.0 / math.sqrt(2.0 * math.pi))
    return cdf + z * pdf


def _split_dot(x, m):
    hi = x.astype(BF16)
    lo = (x - hi.astype(F32)).astype(BF16)
    return jnp.dot(hi, m, preferred_element_type=F32) + jnp.dot(lo, m, preferred_element_type=F32)


def _head_mask(h, rows):
    lane = lax.broadcasted_iota(jnp.int32, (rows, D_HEADS), 1)
    return (lane >= h * HEAD_DIM) & (lane < (h + 1) * HEAD_DIM)


def _rms_bwd(dh, x, g):
    r = lax.rsqrt(jnp.mean(x * x, axis=-1, keepdims=True) + EPS)
    xhat = x * r
    dg = jnp.sum(dh * xhat, axis=0, keepdims=True)
    dxhat = dh * g
    dx = r * (dxhat - xhat * jnp.mean(dxhat * xhat, axis=-1, keepdims=True))
    return dx, dg


def _in_proj(x, g1, w_in, tm):
    S = x.shape[0]
    nz = D_IN_PAD - LANES

    def body(x_ref, g_ref, w_ref, z_ref, f_ref, h_ref):
        xf = x_ref[...]
        r = lax.rsqrt(jnp.mean(xf * xf, axis=-1, keepdims=True) + EPS)
        h = (xf * r * g_ref[...]).astype(BF16)
        h_ref[...] = h
        zz = jnp.dot(h, w_ref[...], preferred_element_type=F32)
        z_ref[...] = zz[:, :nz].astype(BF16)
        f_ref[...] = zz[:, nz:]

    return pl.pallas_call(
        body, name="in_proj", grid=(S // tm,),
        in_specs=[pl.BlockSpec((tm, D_MODEL), lambda i: (i, 0)), _full((1, D_MODEL)), _full((D_MODEL, D_IN_PAD))],
        out_specs=[pl.BlockSpec((tm, nz), lambda i: (i, 0)), pl.BlockSpec((tm, LANES), lambda i: (i, 0)),
                   pl.BlockSpec((tm, D_MODEL), lambda i: (i, 0))],
        out_shape=[jax.ShapeDtypeStruct((S, nz), BF16), jax.ShapeDtypeStruct((S, LANES), F32),
                   jax.ShapeDtypeStruct((S, D_MODEL), BF16)],
        compiler_params=_params(("parallel",)),
    )(x, g1, w_in)


def _fox_prep(f, bias_row, tb):
    S = f.shape[0]

    def body(f_ref, b_ref, c_ref, ct_ref, carry):
        @pl.when(pl.program_id(0) == 0)
        def _():
            carry[...] = jnp.zeros_like(carry)

        xv = f_ref[...] + b_ref[...]
        lf = jnp.minimum(xv, 0.0) - jnp.log(1.0 + jnp.exp(-jnp.abs(xv)))
        r = lax.broadcasted_iota(jnp.int32, (tb, tb), 0)
        s = lax.broadcasted_iota(jnp.int32, (tb, tb), 1)
        tri = (r >= s).astype(F32)
        cs = jnp.dot(tri, lf, precision=lax.Precision.HIGHEST, preferred_element_type=F32) + carry[0:1, :]
        c_ref[...] = cs
        ct_ref[...] = cs.T[:N_HEADS, :]
        carry[...] = jnp.broadcast_to(cs[tb - 1:tb, :], carry.shape)

    return pl.pallas_call(
        body, name="fox_prep", grid=(S // tb,),
        in_specs=[pl.BlockSpec((tb, LANES), lambda i: (i, 0)), _full((1, LANES))],
        out_specs=[pl.BlockSpec((tb, LANES), lambda i: (i, 0)), pl.BlockSpec((N_HEADS, tb), lambda i: (0, i))],
        out_shape=[jax.ShapeDtypeStruct((S, LANES), F32), jax.ShapeDtypeStruct((N_HEADS, S), F32)],
        scratch_shapes=[pltpu.VMEM((SUBLANES, LANES), F32)],
        compiler_params=_params(("arbitrary",)),
    )(f, bias_row)


def _attn_consts():
    col = jnp.arange(D_PAD)
    row = jnp.arange(D_HEADS)
    head = jnp.arange(LANES)
    place = (row[:, None] // HEAD_DIM == col[None, :] // HEAD_PAD) & (row[:, None] % HEAD_DIM == col[None, :] % HEAD_PAD)

    def stat(offset):
        return ((head[:, None] < N_HEADS) & (col[None, :] == head[:, None] * HEAD_PAD + offset)).astype(BF16)

    def ones(offsets):
        return sum((col % HEAD_PAD == o) for o in offsets).astype(F32).reshape(1, D_PAD)

    def pick(offset):
        gcol = jnp.arange(GROUP_PAD)
        return jnp.stack([((gcol[:, None] % HEAD_PAD == offset) & (head[None, :] == g * GROUP_HEADS + gcol[:, None] // HEAD_PAD))
                          for g in range(GROUPS)]).astype(BF16)

    place = place.astype(BF16)
    return {
        "place": place, "place_t": place.T, "place_t_group": place.T[:GROUP_PAD, :GROUP_HEADS * HEAD_DIM],
        "q_stat": jnp.stack([stat(Q_STAT + j) for j in range(3)]), "k_stat": jnp.stack([stat(K_STAT + j) for j in range(3)]),
        "d_stat": jnp.stack([stat(Q_STAT + j) for j in range(2)]),
        "q_ones": ones(range(K_STAT, K_STAT + 3)), "k_ones": ones(range(Q_STAT, Q_STAT + 3)), "v_ones": ones(range(Q_STAT, Q_STAT + 2)),
        "pick_rows": pick(Q_STAT), "pick_cols": pick(K_STAT),
    }


def _split3(x):
    hi = x.astype(BF16)
    r = x - hi.astype(F32)
    mid = r.astype(BF16)
    return hi, mid, (r - mid.astype(F32)).astype(BF16)


def _split3_dot(x, m):
    return sum(jnp.dot(part, m, preferred_element_type=F32) for part in _split3(x))


def _attn_pack(z, c, k, tm):
    S = z.shape[0]

    def body(q_ref, k_ref, v_ref, c_ref, pl_ref, qs_ref, ks_ref, qo_ref, ko_ref, vo_ref, qa_ref, ka_ref, va_ref):
        place = pl_ref[...]
        q = (q_ref[...].astype(F32) * (SCALE * LOG2E)).astype(BF16)
        qa = jnp.dot(q, place, preferred_element_type=F32) + qo_ref[...]
        ka = jnp.dot(k_ref[...], place, preferred_element_type=F32) + ko_ref[...]
        for j, part in enumerate(_split3(c_ref[...] * LOG2E)):
            qa = qa + jnp.dot(part, qs_ref[j], preferred_element_type=F32)
            ka = ka - jnp.dot(part, ks_ref[j], preferred_element_type=F32)
        qa_ref[...] = qa.astype(BF16)
        ka_ref[...] = ka.astype(BF16)
        va_ref[...] = (jnp.dot(v_ref[...], place, preferred_element_type=F32) + vo_ref[...]).astype(BF16)

    blk = lambda col: pl.BlockSpec((tm, D_HEADS), lambda i: (i, col))
    out = pl.BlockSpec((tm, D_PAD), lambda i: (i, 0))
    return pl.pallas_call(
        body, name="attn_pack", grid=(S // tm,),
        in_specs=[blk(2), blk(3), blk(4), pl.BlockSpec((tm, LANES), lambda i: (i, 0)), _full((D_HEADS, D_PAD)),
                  _full((3, LANES, D_PAD)), _full((3, LANES, D_PAD)), _full((1, D_PAD)), _full((1, D_PAD)), _full((1, D_PAD))],
        out_specs=[out, out, out], out_shape=[jax.ShapeDtypeStruct((S, D_PAD), BF16)] * 3,
        compiler_params=_params(("parallel",)),
    )(z, z, z, c, k["place"], k["q_stat"], k["k_stat"], k["q_ones"], k["k_ones"], k["v_ones"])


def _attn_fwd(qa, ka, va, place_t, tq):
    S = qa.shape[0]
    n = S // tq

    def body(q_ref, k_ref, v_ref, pt_ref, o_ref, lse_ref, m_s, acc_s, opad_s):
        qi, ki = pl.program_id(0), pl.program_id(1)

        @pl.when(ki == 0)
        def _():
            m_s[...] = jnp.full_like(m_s, NEG)
            acc_s[...] = jnp.zeros_like(acc_s)

        def step(diagonal):
            if diagonal:
                row = lax.broadcasted_iota(jnp.int32, (tq, tq), 0)
                col = lax.broadcasted_iota(jnp.int32, (tq, tq), 1)
                keep = row >= col
            for h in range(N_HEADS):
                sl = slice(h * HEAD_PAD, (h + 1) * HEAD_PAD)
                s = lax.dot_general(q_ref[:, sl], k_ref[:, sl], NT, preferred_element_type=F32)
                if diagonal:
                    s = jnp.where(keep, s, NEG)
                m_prev = m_s[h]
                m_new = jnp.maximum(m_prev, jnp.max(s, axis=1, keepdims=True))
                p = jnp.exp2(s - m_new)
                acc_s[h] = jnp.exp2(m_prev - m_new) * acc_s[h] + jnp.dot(p.astype(BF16), v_ref[:, sl], preferred_element_type=F32)
                m_s[h] = m_new

        @pl.when(ki < qi)
        def _():
            step(False)

        @pl.when(ki == qi)
        def _():
            step(True)
            lse_ref[...] = jnp.zeros_like(lse_ref)
            for h in range(N_HEADS):
                acc = acc_s[h]
                denom = acc[:, Q_STAT:Q_STAT + 1]
                opad_s[:, h * HEAD_PAD:(h + 1) * HEAD_PAD] = (acc / denom).astype(BF16)
                lse_ref[h // GROUP_HEADS, :, pl.ds(h % GROUP_HEADS, 1)] = m_s[h] + jnp.log(denom) * LOG2E
            o_ref[...] = jnp.dot(opad_s[...], pt_ref[...], preferred_element_type=F32).astype(BF16)

    kv = pl.BlockSpec((tq, D_PAD), lambda qi, ki: (jnp.minimum(ki, qi), 0))
    return pl.pallas_call(
        body, name="attn_fwd", grid=(n, n),
        in_specs=[pl.BlockSpec((tq, D_PAD), lambda qi, ki: (qi, 0)), kv, kv, _full((D_PAD, D_HEADS))],
        out_specs=[pl.BlockSpec((tq, D_HEADS), lambda qi, ki: (qi, 0)), pl.BlockSpec((GROUPS, tq, LANES), lambda qi, ki: (0, qi, 0))],
        out_shape=[jax.ShapeDtypeStruct((S, D_HEADS), BF16), jax.ShapeDtypeStruct((GROUPS, S, LANES), F32)],
        scratch_shapes=[pltpu.VMEM((N_HEADS, tq, 1), F32), pltpu.VMEM((N_HEADS, tq, HEAD_PAD), F32), pltpu.VMEM((tq, D_PAD), BF16)],
        compiler_params=_params(("parallel", "arbitrary")),
    )(qa, ka, va, place_t)


def _layer_norm_heads(v, seg_avg):
    mu = _split_dot(v, seg_avg)
    d = v - mu
    var = _split_dot(d * d, seg_avg)
    rstd = lax.rsqrt(var + EPS)
    return d * rstd, rstd


def _gate_mix(vn_blk, w_ref, bias):
    acc = bias
    for h in range(N_HEADS):
        vh = jnp.where(_head_mask(h, SG_BLOCK), vn_blk, 0.0).astype(BF16)
        acc = acc + jnp.dot(w_ref[h], vh, preferred_element_type=F32)
    return acc


def _gate_fwd(z, w_mask, ln_row, b_full, seg_avg, tm):
    S = z.shape[0]

    def body(zu_ref, zv_ref, w_ref, ln_ref, b_ref, avg_ref, o_ref):
        u = _gelu(zu_ref[...].astype(F32))
        v = _gelu(zv_ref[...].astype(F32))
        vhat, _ = _layer_norm_heads(v, avg_ref[...])
        vn = vhat * ln_ref[...]
        for b in range(tm // SG_BLOCK):
            rows = slice(b * SG_BLOCK, (b + 1) * SG_BLOCK)
            mixed = _gate_mix(vn[rows], w_ref, b_ref[...])
            o_ref[rows, :] = (u[rows] * mixed).astype(BF16)

    return pl.pallas_call(
        body, name="gate_fwd", grid=(S // tm,),
        in_specs=[pl.BlockSpec((tm, D_HEADS), lambda i: (i, 0)), pl.BlockSpec((tm, D_HEADS), lambda i: (i, 1)),
                  _full((N_HEADS, SG_BLOCK, SG_BLOCK)), _full((1, D_HEADS)), _full((SG_BLOCK, D_HEADS)),
                  _full((D_HEADS, D_HEADS))],
        out_specs=pl.BlockSpec((tm, D_HEADS), lambda i: (i, 0)),
        out_shape=jax.ShapeDtypeStruct((S, D_HEADS), BF16),
        compiler_params=_params(("parallel",)),
    )(z, z, w_mask, ln_row, b_full, seg_avg)


def _mix_out(x, out_a, out_b, w_out, g2, tm):
    S = x.shape[0]

    def body(x_ref, a_ref, b_ref, w_ref, g_ref, x1_ref, h_ref):
        y = jnp.dot(a_ref[...], w_ref[:D_HEADS, :], preferred_element_type=F32)
        y = y + jnp.dot(b_ref[...], w_ref[D_HEADS:, :], preferred_element_type=F32)
        x1 = x_ref[...] + y
        x1_ref[...] = x1
        r = lax.rsqrt(jnp.mean(x1 * x1, axis=-1, keepdims=True) + EPS)
        h_ref[...] = (x1 * r * g_ref[...]).astype(BF16)

    row = lambda w: pl.BlockSpec((tm, w), lambda i: (i, 0))
    return pl.pallas_call(
        body, name="mix_out", grid=(S // tm,),
        in_specs=[row(D_MODEL), row(D_HEADS), row(D_HEADS), _full((D_MODEL, D_MODEL)), _full((1, D_MODEL))],
        out_specs=[row(D_MODEL), row(D_MODEL)],
        out_shape=[jax.ShapeDtypeStruct((S, D_MODEL), F32), jax.ShapeDtypeStruct((S, D_MODEL), BF16)],
        compiler_params=_params(("parallel",)),
    )(x, out_a, out_b, w_out, g2)


def _up_proj(h2, w_up_q, tm):
    S = h2.shape[0]
    nq, _, wq = w_up_q.shape

    def body(h_ref, w_ref, a_ref):
        a_ref[...] = jnp.dot(h_ref[...], w_ref[...], preferred_element_type=F32).astype(BF16)

    return pl.pallas_call(
        body, name="up_proj", grid=(nq, S // tm),
        in_specs=[pl.BlockSpec((tm, D_MODEL), lambda j, i: (i, 0)), pl.BlockSpec((None, D_MODEL, wq), lambda j, i: (j, 0, 0))],
        out_specs=pl.BlockSpec((tm, wq), lambda j, i: (i, j)),
        out_shape=jax.ShapeDtypeStruct((S, nq * wq), BF16),
        compiler_params=_params(("parallel", "parallel")),
    )(h2, w_up_q)


def _shift_down(a, halo, k):
    tm = a.shape[0]
    ra = pltpu.roll(a, k, 0)
    rh = pltpu.roll(halo, k, 0)
    row = lax.broadcasted_iota(jnp.int32, halo.shape, 0)
    top = jnp.where(row < k, rh, ra[0:SUBLANES])
    return jnp.concatenate([top, ra[SUBLANES:tm]], axis=0)


def _shift_up(a, halo, k):
    tm = a.shape[0]
    ra = pltpu.roll(a, tm - k, 0)
    rh = pltpu.roll(halo, SUBLANES - k, 0)
    row = lax.broadcasted_iota(jnp.int32, halo.shape, 0)
    bottom = jnp.where(row >= SUBLANES - k, rh, ra[tm - SUBLANES:tm])
    return jnp.concatenate([ra[0:tm - SUBLANES], bottom], axis=0)


def _conv_taps(a_ref, halo_ref, first):
    a = a_ref[...].astype(F32)
    halo = halo_ref[...].astype(F32) * jnp.where(first, 0.0, 1.0)
    return a, _shift_down(a, halo, 1), _shift_down(a, halo, 2)


def _conv_specs(tm):
    step = tm // SUBLANES
    prev = lambda i: jnp.maximum(i * step - 1, 0)
    return [pl.BlockSpec((tm, D_FF), lambda i: (i, 0)), pl.BlockSpec((tm, D_FF), lambda i: (i, 1)),
            pl.BlockSpec((SUBLANES, D_FF), lambda i: (prev(i), 0)), pl.BlockSpec((SUBLANES, D_FF), lambda i: (prev(i), 1))]


def _ffn_fwd_loss(a, w_conv, b_conv, w_down, x1, g3, target, tm):
    S = x1.shape[0]

    def body(ag_ref, av_ref, hg_ref, hv_ref, wg_ref, wv_ref, bg_ref, bv_ref, wd_ref, x1_ref, g_ref, t_ref,
             dx2_ref, loss_ref, dg_ref):
        i = pl.program_id(0)

        @pl.when(i == 0)
        def _():
            loss_ref[...] = jnp.zeros_like(loss_ref)
            dg_ref[...] = jnp.zeros_like(dg_ref)

        g0, g1, g2 = _conv_taps(ag_ref, hg_ref, i == 0)
        gate = wg_ref[2:3, :] * g0 + wg_ref[1:2, :] * g1 + wg_ref[0:1, :] * g2 + bg_ref[...]
        v0, v1, v2 = _conv_taps(av_ref, hv_ref, i == 0)
        val = wv_ref[2:3, :] * v0 + wv_ref[1:2, :] * v1 + wv_ref[0:1, :] * v2 + bv_ref[...]
        y = (gate * jax.nn.sigmoid(gate) * val).astype(BF16)
        x2 = x1_ref[...] + jnp.dot(y, wd_ref[...], preferred_element_type=F32)
        r = lax.rsqrt(jnp.mean(x2 * x2, axis=-1, keepdims=True) + EPS)
        xhat = x2 * r
        gg = g_ref[...]
        err = xhat * gg - t_ref[...]
        loss_ref[...] += jnp.sum(err * err, axis=0, keepdims=True)
        dy = err * (1.0 / D_MODEL)
        dg_ref[...] += jnp.sum(dy * xhat, axis=0, keepdims=True)
        dxhat = dy * gg
        dx2_ref[...] = r * (dxhat - xhat * jnp.mean(dxhat * xhat, axis=-1, keepdims=True))

    row = lambda w: pl.BlockSpec((tm, w), lambda i: (i, 0))
    half = lambda r: [pl.BlockSpec((r, D_FF), lambda i: (0, 0)), pl.BlockSpec((r, D_FF), lambda i: (0, 1))]
    return pl.pallas_call(
        body, name="ffn_fwd_loss", grid=(S // tm,),
        in_specs=_conv_specs(tm) + half(3) + half(1) + [_full((D_FF, D_MODEL)), row(D_MODEL), _full((1, D_MODEL)), row(D_MODEL)],
        out_specs=[row(D_MODEL), _full((1, D_MODEL)), _full((1, D_MODEL))],
        out_shape=[jax.ShapeDtypeStruct((S, D_MODEL), F32), jax.ShapeDtypeStruct((1, D_MODEL), F32),
                   jax.ShapeDtypeStruct((1, D_MODEL), F32)],
        compiler_params=_params(("arbitrary",)),
    )(a, a, a, a, w_conv, w_conv, b_conv, b_conv, w_down, x1, g3, target)


def _ffn_bwd_gate(dx2, a, w_conv, b_conv, w_down, tm):
    S = dx2.shape[0]

    def body(dx_ref, ag_ref, av_ref, hg_ref, hv_ref, wg_ref, wv_ref, bg_ref, bv_ref, wd_ref,
             dc_ref, y_ref, dw_ref, db_ref):
        i = pl.program_id(0)

        @pl.when(i == 0)
        def _():
            dw_ref[...] = jnp.zeros_like(dw_ref)
            db_ref[...] = jnp.zeros_like(db_ref)

        g0, g1, g2 = _conv_taps(ag_ref, hg_ref, i == 0)
        gate = wg_ref[2:3, :] * g0 + wg_ref[1:2, :] * g1 + wg_ref[0:1, :] * g2 + bg_ref[...]
        v0, v1, v2 = _conv_taps(av_ref, hv_ref, i == 0)
        val = wv_ref[2:3, :] * v0 + wv_ref[1:2, :] * v1 + wv_ref[0:1, :] * v2 + bv_ref[...]
        sg = jax.nn.sigmoid(gate)
        act = gate * sg
        y_ref[...] = (act * val).astype(BF16)
        dy = lax.dot_general(dx_ref[...].astype(BF16), wd_ref[...], NT, preferred_element_type=F32)
        dgate = dy * val * (sg * (1.0 + gate * (1.0 - sg)))
        dval = dy * act
        dc_ref[:, :D_FF] = dgate.astype(BF16)
        dc_ref[:, D_FF:] = dval.astype(BF16)
        for half, (d, taps) in enumerate(((dgate, (g2, g1, g0)), (dval, (v2, v1, v0)))):
            cols = slice(half * D_FF, (half + 1) * D_FF)
            db_ref[0:1, cols] += jnp.sum(d, axis=0, keepdims=True)
            for j in range(3):
                dw_ref[j:j + 1, cols] += jnp.sum(d * taps[j], axis=0, keepdims=True)

    row = lambda w: pl.BlockSpec((tm, w), lambda i: (i, 0))
    half = lambda r: [pl.BlockSpec((r, D_FF), lambda i: (0, 0)), pl.BlockSpec((r, D_FF), lambda i: (0, 1))]
    return pl.pallas_call(
        body, name="ffn_bwd_gate", grid=(S // tm,),
        in_specs=[row(D_MODEL)] + _conv_specs(tm) + half(3) + half(1) + [_full((D_FF, D_MODEL))],
        out_specs=[row(2 * D_FF), row(D_FF), _full((SUBLANES, 2 * D_FF)), _full((1, 2 * D_FF))],
        out_shape=[jax.ShapeDtypeStruct((S, 2 * D_FF), BF16), jax.ShapeDtypeStruct((S, D_FF), BF16),
                   jax.ShapeDtypeStruct((SUBLANES, 2 * D_FF), F32), jax.ShapeDtypeStruct((1, 2 * D_FF), F32)],
        compiler_params=_params(("arbitrary",)),
    )(dx2, a, a, a, a, w_conv, w_conv, b_conv, b_conv, w_down)


def _conv_bwd(dc, w_conv, tm, tn):
    S, C = dc.shape
    step = tm // SUBLANES
    last_blk = S // SUBLANES - 1

    def body(d_ref, nx_ref, w_ref, o_ref):
        last = pl.program_id(0) == pl.num_programs(0) - 1
        d = d_ref[...].astype(F32)
        nx = nx_ref[...].astype(F32) * jnp.where(last, 0.0, 1.0)
        out = w_ref[2:3, :] * d + w_ref[1:2, :] * _shift_up(d, nx, 1) + w_ref[0:1, :] * _shift_up(d, nx, 2)
        o_ref[...] = out.astype(BF16)

    return pl.pallas_call(
        body, name="conv_bwd", grid=(S // tm, C // tn),
        in_specs=[pl.BlockSpec((tm, tn), lambda i, j: (i, j)),
                  pl.BlockSpec((SUBLANES, tn), lambda i, j: (jnp.minimum((i + 1) * step, last_blk), j)),
                  pl.BlockSpec((3, tn), lambda i, j: (0, j))],
        out_specs=pl.BlockSpec((tm, tn), lambda i, j: (i, j)),
        out_shape=jax.ShapeDtypeStruct((S, C), BF16),
        compiler_params=_params(("parallel", "parallel")),
    )(dc, dc, w_conv)


def _matmul_tn(a, b, name, bm, bn, tk, col_a=0, col_b=0, quarters=None):
    S = a.shape[0]
    gm, gn = quarters if quarters else (1, 1)
    nk = S // tk

    def body(a_ref, b_ref, o_ref):
        @pl.when(pl.program_id(2) == 0)
        def _():
            o_ref[...] = jnp.zeros_like(o_ref)

        o_ref[...] += lax.dot_general(a_ref[...].astype(BF16), b_ref[...].astype(BF16), TN, preferred_element_type=F32)

    if quarters and gn > 1:
        out_spec = pl.BlockSpec((None, bm, bn), lambda i, j, k: (j, i, 0))
        out_shape = jax.ShapeDtypeStruct((gn, gm * bm, bn), F32)
    else:
        out_spec = pl.BlockSpec((bm, bn), lambda i, j, k: (i, j))
        out_shape = jax.ShapeDtypeStruct((gm * bm, gn * bn), F32)
    return pl.pallas_call(
        body, name=name, grid=(gm, gn, nk),
        in_specs=[pl.BlockSpec((tk, bm), lambda i, j, k: (k, col_a * gm + i)),
                  pl.BlockSpec((tk, bn), lambda i, j, k: (k, col_b * gn + j))],
        out_specs=out_spec, out_shape=out_shape,
        compiler_params=_params(("parallel", "parallel", "arbitrary")),
    )(a, b)


def _up_bwd(dact, w_up_q, x1, g2, dx2, tm):
    S = x1.shape[0]
    nq, _, wq = w_up_q.shape

    def body(d_ref, w_ref, x_ref, g_ref, dx2_ref, dx1_ref, dg_ref):
        @pl.when(pl.program_id(0) == 0)
        def _():
            dg_ref[...] = jnp.zeros_like(dg_ref)

        dh = jnp.zeros((tm, D_MODEL), F32)
        for j in range(nq):
            dh = dh + lax.dot_general(d_ref[:, j * wq:(j + 1) * wq], w_ref[j], NT, preferred_element_type=F32)
        dx, dg = _rms_bwd(dh, x_ref[...], g_ref[...])
        dg_ref[...] += dg
        dx1_ref[...] = dx2_ref[...] + dx

    row = lambda w: pl.BlockSpec((tm, w), lambda i: (i, 0))
    return pl.pallas_call(
        body, name="up_bwd", grid=(S // tm,),
        in_specs=[row(nq * wq), pl.BlockSpec((nq, D_MODEL, wq), lambda i: (0, 0, 0), pipeline_mode=pl.Buffered(1)),
                  row(D_MODEL), _full((1, D_MODEL)), row(D_MODEL)],
        out_specs=[row(D_MODEL), _full((1, D_MODEL))],
        out_shape=[jax.ShapeDtypeStruct((S, D_MODEL), F32), jax.ShapeDtypeStruct((1, D_MODEL), F32)],
        compiler_params=_params(("arbitrary",)),
    )(dact, w_up_q, x1, g2, dx2)


def _out_bwd(dx1, w_out, tm):
    S = dx1.shape[0]

    def body(d_ref, w_ref, o_ref):
        o_ref[...] = lax.dot_general(d_ref[...].astype(BF16), w_ref[...], NT, preferred_element_type=F32).astype(BF16)

    return pl.pallas_call(
        body, name="out_bwd", grid=(S // tm,),
        in_specs=[pl.BlockSpec((tm, D_MODEL), lambda i: (i, 0)), _full((D_MODEL, D_MODEL))],
        out_specs=pl.BlockSpec((tm, D_MODEL), lambda i: (i, 0)),
        out_shape=jax.ShapeDtypeStruct((S, D_MODEL), BF16),
        compiler_params=_params(("parallel",)),
    )(dx1, w_out)


def _gate_bwd(z, dcat, w_mask, w_mask_t, ln_row, b_full, seg_avg, head_ind, tm):
    S = z.shape[0]
    nb = tm // SG_BLOCK

    def body(zu_ref, zv_ref, do_ref, w_ref, wt_ref, ln_ref, b_ref, avg_ref, ind_ref,
             dzu_ref, dzv_ref, dw_ref, db_ref, dln_ref, dvn_s, dbf_s):
        i = pl.program_id(0)

        @pl.when(i == 0)
        def _():
            dw_ref[...] = jnp.zeros_like(dw_ref)
            dln_ref[...] = jnp.zeros_like(dln_ref)
            dbf_s[...] = jnp.zeros_like(dbf_s)

        zu = zu_ref[...].astype(F32)
        zv = zv_ref[...].astype(F32)
        u = _gelu(zu)
        v = _gelu(zv)
        avg = avg_ref[...]
        vhat, rstd = _layer_norm_heads(v, avg)
        ln = ln_ref[...]
        vn = vhat * ln
        for b in range(nb):
            rows = slice(b * SG_BLOCK, (b + 1) * SG_BLOCK)
            vn_b = vn[rows]
            mixed = _gate_mix(vn_b, w_ref, b_ref[...])
            do = do_ref[rows, :].astype(F32)
            dzu_ref[rows, :] = (do * mixed * _gelu_grad(zu[rows])).astype(BF16)
            dmix = do * u[rows]
            dbf_s[...] += dmix
            vn_bf = vn_b.astype(BF16)
            dvn = jnp.zeros((SG_BLOCK, D_HEADS), F32)
            for h in range(N_HEADS):
                dmh = jnp.where(_head_mask(h, SG_BLOCK), dmix, 0.0).astype(BF16)
                dw_ref[h] += lax.dot_general(dmh, vn_bf, NT, preferred_element_type=F32)
                dvn = dvn + jnp.dot(wt_ref[h], dmh, preferred_element_type=F32)
            dvn_s[rows, :] = dvn
        dvn = dvn_s[...]
        dln_ref[...] += jnp.sum(dvn * vhat, axis=0, keepdims=True)
        dvhat = dvn * ln
        dv = rstd * (dvhat - _split_dot(dvhat, avg) - vhat * _split_dot(dvhat * vhat, avg))
        dzv_ref[...] = (dv * _gelu_grad(zv)).astype(BF16)

        @pl.when(i == pl.num_programs(0) - 1)
        def _():
            r = lax.broadcasted_iota(jnp.int32, (SG_BLOCK, SG_BLOCK), 0) // CHUNK
            s = lax.broadcasted_iota(jnp.int32, (SG_BLOCK, SG_BLOCK), 1) // CHUNK
            for h in range(N_HEADS):
                dw_ref[h] = jnp.where(r >= s, dw_ref[h], 0.0)
            db_ref[...] = _split_dot(dbf_s[...], ind_ref[...])

    row = lambda col: pl.BlockSpec((tm, D_HEADS), lambda i: (i, col))
    wspec = _full((N_HEADS, SG_BLOCK, SG_BLOCK))
    return pl.pallas_call(
        body, name="gate_bwd", grid=(S // tm,),
        in_specs=[row(0), row(1), row(0), wspec, wspec, _full((1, D_HEADS)), _full((SG_BLOCK, D_HEADS)),
                  _full((D_HEADS, D_HEADS)), _full((D_HEADS, LANES))],
        out_specs=[row(0), row(0), wspec, _full((SG_BLOCK, LANES)), _full((1, D_HEADS))],
        out_shape=[jax.ShapeDtypeStruct((S, D_HEADS), BF16), jax.ShapeDtypeStruct((S, D_HEADS), BF16),
                   jax.ShapeDtypeStruct((N_HEADS, SG_BLOCK, SG_BLOCK), F32), jax.ShapeDtypeStruct((SG_BLOCK, LANES), F32),
                   jax.ShapeDtypeStruct((1, D_HEADS), F32)],
        scratch_shapes=[pltpu.VMEM((tm, D_HEADS), F32), pltpu.VMEM((SG_BLOCK, D_HEADS), F32)],
        compiler_params=_params(("arbitrary",)),
    )(z, z, dcat, w_mask, w_mask_t, ln_row, b_full, seg_avg, head_ind)


def _attn_pack_grad(o, dcat, head_ind, k, tm):
    S = o.shape[0]

    def body(o_ref, do_ref, ind_ref, pl_ref, ds_ref, dop_ref):
        do = do_ref[...]
        delta = _split_dot(o_ref[...].astype(F32) * do.astype(F32), ind_ref[...])
        hi = delta.astype(BF16)
        lo = (delta - hi.astype(F32)).astype(BF16)
        dop = jnp.dot(do, pl_ref[...], preferred_element_type=F32)
        dop = dop - jnp.dot(hi, ds_ref[0], preferred_element_type=F32) - jnp.dot(lo, ds_ref[1], preferred_element_type=F32)
        dop_ref[...] = dop.astype(BF16)

    return pl.pallas_call(
        body, name="attn_pack_grad", grid=(S // tm,),
        in_specs=[pl.BlockSpec((tm, D_HEADS), lambda i: (i, 0)), pl.BlockSpec((tm, D_HEADS), lambda i: (i, 1)),
                  _full((D_HEADS, LANES)), _full((D_HEADS, D_PAD)), _full((2, LANES, D_PAD))],
        out_specs=pl.BlockSpec((tm, D_PAD), lambda i: (i, 0)),
        out_shape=jax.ShapeDtypeStruct((S, D_PAD), BF16),
        compiler_params=_params(("parallel",)),
    )(o, dcat, head_ind, k["place"], k["d_stat"])


def _attn_bwd(qa, ka, va, dop, lse, k, tq):
    S = qa.shape[0]
    n = S // tq

    def body(q_ref, k_ref, v_ref, do_ref, lse_ref, pt_ref, pick_ref, dq_hbm, dk_ref, dv_ref, dcc_ref, dq_s, dk_s, dv_s, sem):
        g, ki, qi = pl.program_id(0), pl.program_id(1), pl.program_id(2)

        @pl.when((ki == 0) & (qi == 0))
        def _():
            dq_s[...] = jnp.zeros_like(dq_s)

        @pl.when(qi == ki)
        def _():
            dk_s[...] = jnp.zeros_like(dk_s)
            dv_s[...] = jnp.zeros_like(dv_s)

        def step(diagonal):
            rows = pl.ds(pl.multiple_of(qi * tq, tq), tq)
            if diagonal:
                row = lax.broadcasted_iota(jnp.int32, (tq, tq), 0)
                col = lax.broadcasted_iota(jnp.int32, (tq, tq), 1)
                keep = row >= col
            for hh in range(GROUP_HEADS):
                sl = slice(hh * HEAD_PAD, (hh + 1) * HEAD_PAD)
                q, kk, v, do = q_ref[:, sl], k_ref[:, sl], v_ref[:, sl], do_ref[:, sl]
                s = lax.dot_general(q, kk, NT, preferred_element_type=F32)
                p = jnp.exp2(s - lse_ref[:, hh:hh + 1])
                if diagonal:
                    p = jnp.where(keep, p, 0.0)
                dv_s[:, sl] += lax.dot_general(p.astype(BF16), do, TN, preferred_element_type=F32)
                ds = (p * lax.dot_general(do, v, NT, preferred_element_type=F32)).astype(BF16)
                dk_s[:, sl] += lax.dot_general(ds, q, TN, preferred_element_type=F32)
                dq_s[rows, sl] += jnp.dot(ds, kk, preferred_element_type=F32)

        @pl.when(qi > ki)
        def _():
            step(False)

        @pl.when(qi == ki)
        def _():
            step(True)

        @pl.when(qi == n - 1)
        def _():
            dk = dk_s[...]
            dk_ref[...] = jnp.dot((dk * (1.0 / LOG2E)).astype(BF16), pt_ref[...], preferred_element_type=F32).astype(BF16)
            dv_ref[...] = jnp.dot(dv_s[...].astype(BF16), pt_ref[...], preferred_element_type=F32).astype(BF16)
            dcc_ref[...] = _split3_dot(dk, pick_ref[...])

        @pl.when((ki == n - 1) & (qi == n - 1))
        def _():
            cp = pltpu.make_async_copy(dq_s, dq_hbm.at[g], sem)
            cp.start()
            cp.wait()

    gw = GROUP_HEADS * HEAD_DIM
    qspec = pl.BlockSpec((tq, GROUP_PAD), lambda g, ki, qi: (jnp.maximum(qi, ki), g))
    kspec = pl.BlockSpec((tq, GROUP_PAD), lambda g, ki, qi: (ki, g))
    kout = pl.BlockSpec((tq, gw), lambda g, ki, qi: (ki, g))
    return pl.pallas_call(
        body, name="attn_bwd", grid=(GROUPS, n, n),
        in_specs=[qspec, kspec, kspec, qspec, pl.BlockSpec((None, tq, LANES), lambda g, ki, qi: (g, jnp.maximum(qi, ki), 0)),
                  _full((GROUP_PAD, gw)), pl.BlockSpec((None, GROUP_PAD, LANES), lambda g, ki, qi: (g, 0, 0))],
        out_specs=[_ANY, kout, kout, pl.BlockSpec((None, tq, LANES), lambda g, ki, qi: (g, ki, 0))],
        out_shape=[jax.ShapeDtypeStruct((GROUPS, S, GROUP_PAD), F32), jax.ShapeDtypeStruct((S, D_HEADS), BF16),
                   jax.ShapeDtypeStruct((S, D_HEADS), BF16), jax.ShapeDtypeStruct((GROUPS, S, LANES), F32)],
        scratch_shapes=[pltpu.VMEM((S, GROUP_PAD), F32), pltpu.VMEM((tq, GROUP_PAD), F32), pltpu.VMEM((tq, GROUP_PAD), F32),
                        pltpu.SemaphoreType.DMA],
        compiler_params=_params(("arbitrary", "arbitrary", "arbitrary")),
    )(qa, ka, va, dop, lse, k["place_t_group"], k["pick_cols"])


def _attn_unpack(dqp, dcc, k, tm):
    S = dqp.shape[1]
    gw = GROUP_HEADS * HEAD_DIM

    def body(dqp_ref, dcc_ref, pt_ref, pick_ref, dq_ref, dc_ref):
        dc = jnp.zeros((tm, LANES), F32)
        for g in range(GROUPS):
            x = dqp_ref[g]
            dq_ref[:, g * gw:(g + 1) * gw] = jnp.dot((x * SCALE).astype(BF16), pt_ref[...], preferred_element_type=F32).astype(BF16)
            dc = dc + _split3_dot(x, pick_ref[g]) - dcc_ref[g]
        dc_ref[...] = dc

    return pl.pallas_call(
        body, name="attn_unpack", grid=(S // tm,),
        in_specs=[pl.BlockSpec((GROUPS, tm, GROUP_PAD), lambda i: (0, i, 0)), pl.BlockSpec((GROUPS, tm, LANES), lambda i: (0, i, 0)),
                  _full((GROUP_PAD, gw)), _full((GROUPS, GROUP_PAD, LANES))],
        out_specs=[pl.BlockSpec((tm, D_HEADS), lambda i: (i, 0)), pl.BlockSpec((tm, LANES), lambda i: (i, 0))],
        out_shape=[jax.ShapeDtypeStruct((S, D_HEADS), BF16), jax.ShapeDtypeStruct((S, LANES), F32)],
        compiler_params=_params(("parallel",)),
    )(dqp, dcc, k["place_t_group"], k["pick_rows"])


def _fox_bwd(dc, f, bias_row, tb):
    S = f.shape[0]
    nb = S // tb

    def body(dc_ref, f_ref, b_ref, df_ref, dbias_ref, carry):
        @pl.when(pl.program_id(0) == 0)
        def _():
            carry[...] = jnp.zeros_like(carry)
            dbias_ref[...] = jnp.zeros_like(dbias_ref)

        r = lax.broadcasted_iota(jnp.int32, (tb, tb), 0)
        s = lax.broadcasted_iota(jnp.int32, (tb, tb), 1)
        tri = (s >= r).astype(F32)
        rc = jnp.dot(tri, dc_ref[...], precision=lax.Precision.HIGHEST, preferred_element_type=F32) + carry[0:1, :]
        carry[...] = jnp.broadcast_to(rc[0:1, :], carry.shape)
        lane = lax.broadcasted_iota(jnp.int32, (tb, LANES), 1)
        df = jnp.where(lane < N_HEADS, rc * jax.nn.sigmoid(-(f_ref[...] + b_ref[...])), 0.0)
        df_ref[...] = df.astype(BF16)
        dbias_ref[...] += jnp.sum(df, axis=0, keepdims=True)

    rev = pl.BlockSpec((tb, LANES), lambda i: (nb - 1 - i, 0))
    return pl.pallas_call(
        body, name="fox_bwd", grid=(nb,),
        in_specs=[rev, rev, _full((1, LANES))],
        out_specs=[rev, _full((1, LANES))],
        out_shape=[jax.ShapeDtypeStruct((S, LANES), BF16), jax.ShapeDtypeStruct((1, LANES), F32)],
        scratch_shapes=[pltpu.VMEM((SUBLANES, LANES), F32)],
        compiler_params=_params(("arbitrary",)),
    )(dc, f, bias_row)


_DZ_WIDTHS = (D_HEADS,) * 5 + (LANES,)


def _in_bwd(pieces, w_in, x, g1, dx1, tm):
    S = x.shape[0]

    def body(*refs):
        p_refs, (w_ref, x_ref, g_ref, dx1_ref, dx_ref, dg_ref) = refs[:6], refs[6:]

        @pl.when(pl.program_id(0) == 0)
        def _():
            dg_ref[...] = jnp.zeros_like(dg_ref)

        dh = jnp.zeros((tm, D_MODEL), F32)
        off = 0
        for p_ref, w in zip(p_refs, _DZ_WIDTHS):
            dh = dh + lax.dot_general(p_ref[...].astype(BF16), w_ref[:, off:off + w], NT, preferred_element_type=F32)
            off += w
        dx, dg = _rms_bwd(dh, x_ref[...], g_ref[...])
        dg_ref[...] += dg
        dx_ref[...] = dx1_ref[...] + dx

    row = lambda w: pl.BlockSpec((tm, w), lambda i: (i, 0))
    return pl.pallas_call(
        body, name="in_bwd", grid=(S // tm,),
        in_specs=[row(w) for w in _DZ_WIDTHS] + [_full((D_MODEL, D_IN_PAD)), row(D_MODEL), _full((1, D_MODEL)), row(D_MODEL)],
        out_specs=[row(D_MODEL), _full((1, D_MODEL))],
        out_shape=[jax.ShapeDtypeStruct((S, D_MODEL), F32), jax.ShapeDtypeStruct((1, D_MODEL), F32)],
        compiler_params=_params(("arbitrary",)),
    )(*pieces, w_in, x, g1, dx1)


def _dw_in(h1, pieces, tk):
    S = h1.shape[0]

    def body(*refs):
        h_ref, p_refs, o_ref = refs[0], refs[1:7], refs[7]

        @pl.when(pl.program_id(0) == 0)
        def _():
            o_ref[...] = jnp.zeros_like(o_ref)

        off = 0
        for p_ref, w in zip(p_refs, _DZ_WIDTHS):
            o_ref[:, off:off + w] += lax.dot_general(h_ref[...], p_ref[...].astype(BF16), TN, preferred_element_type=F32)
            off += w

    row = lambda w: pl.BlockSpec((tk, w), lambda k: (k, 0))
    return pl.pallas_call(
        body, name="dw_in", grid=(S // tk,),
        in_specs=[row(D_MODEL)] + [row(w) for w in _DZ_WIDTHS],
        out_specs=_full((D_MODEL, D_IN_PAD)),
        out_shape=jax.ShapeDtypeStruct((D_MODEL, D_IN_PAD), F32),
        compiler_params=_params(("arbitrary",)),
    )(h1, *pieces)


def _adamw_math(w, g, m, v):
    m = ADAM_B1 * m + (1.0 - ADAM_B1) * g
    v = ADAM_B2 * v + (1.0 - ADAM_B2) * (g * g)
    m_hat = m / (1.0 - ADAM_B1 ** ADAM_STEP)
    v_hat = v / (1.0 - ADAM_B2 ** ADAM_STEP)
    delta = -ADAM_LR * (m_hat / (jnp.sqrt(v_hat) + ADAM_EPS) + ADAM_WD * w)
    return delta, m, v


def _adamw(name, w, g, m, v):
    R, C = w.shape
    tr = _row_tile(R, 256)

    def body(w_ref, g_ref, m_ref, v_ref, d_ref, nm_ref, nv_ref):
        d, nm, nv = _adamw_math(w_ref[...], g_ref[...], m_ref[...], v_ref[...])
        d_ref[...] = d
        nm_ref[...] = nm
        nv_ref[...] = nv

    spec = pl.BlockSpec((tr, C), lambda i: (i, 0))
    return pl.pallas_call(
        body, name=name, grid=(R // tr,), in_specs=[spec] * 4, out_specs=[spec] * 3,
        out_shape=[jax.ShapeDtypeStruct((R, C), F32)] * 3,
        compiler_params=_params(("parallel",)),
    )(w, g, m, v)


def _pair_sum(name, grad, theirs, ids):
    q, half, C = theirs.shape
    tr = _row_tile(half, 256)
    nb = half // tr

    def body(ids_ref, a_ref, b_ref, s_ref, sb_ref):
        s = a_ref[...] + b_ref[...]
        s_ref[...] = s
        sb_ref[...] = s.astype(BF16)

    here = pl.BlockSpec((None, tr, C), lambda j, i, ids: (j, i, 0))
    return pl.pallas_call(
        body, name=name,
        grid_spec=pltpu.PrefetchScalarGridSpec(
            num_scalar_prefetch=1, grid=(q, nb),
            in_specs=[pl.BlockSpec((None, tr, C), lambda j, i, ids: (j, ids[1] * nb + i, 0)), here],
            out_specs=[here, here]),
        out_shape=[jax.ShapeDtypeStruct((q, half, C), F32), jax.ShapeDtypeStruct((q, half, C), BF16)],
        compiler_params=_params(("parallel", "parallel")),
    )(ids, grad, theirs)


def _chip_sum(name, sums32, others, ids):
    _, half, C = sums32.shape
    tr = _row_tile(half, 256)
    nb = half // tr

    def body(ids_ref, a_ref, o_ref, s_ref):
        s = a_ref[...]
        for j in range(3):
            s = s + o_ref[j].astype(F32)
        s_ref[...] = s

    return pl.pallas_call(
        body, name=name,
        grid_spec=pltpu.PrefetchScalarGridSpec(
            num_scalar_prefetch=1, grid=(nb,),
            in_specs=[pl.BlockSpec((None, tr, C), lambda i, ids: (ids[0], i, 0)),
                      pl.BlockSpec((3, tr, C), lambda i, ids: (0, i, 0))],
            out_specs=pl.BlockSpec((tr, C), lambda i, ids: (ids[1] * nb + i, 0))),
        out_shape=jax.ShapeDtypeStruct((2 * half, C), F32),
        compiler_params=_params(("parallel",)),
    )(ids, sums32, others)


def _place():
    return lax.axis_index("x"), lax.axis_index("y"), lax.axis_index("c")


def _other_chips(x, y):
    return [(1 - x, y), (x, 1 - y), (1 - x, 1 - y)]


_ANY = pl.BlockSpec(memory_space=pl.ANY)


def _gather_quarters(shards):
    n = len(shards)
    halved = [s.shape[0] % 32 == 0 for s in shards]

    def body(*refs):
        ins, outs = refs[:n], refs[n:2 * n]
        send_sems, recv_sems, pass_send_sems, pass_recv_sems = refs[2 * n:]
        x, y, c = _place()
        mine = 2 * x + y
        chips = _other_chips(x, y)

        def part(a, quarter, core):
            if not halved[a]:
                return outs[a].at[quarter]
            half = ins[a].shape[0] // 2
            return outs[a].at[quarter, pl.ds(core * half, half), :]

        def source(a):
            if not halved[a]:
                return ins[a]
            half = ins[a].shape[0] // 2
            return ins[a].at[pl.ds(c * half, half), :]

        sends = []
        for a in range(n):
            for j, (px, py) in enumerate(chips):
                cp = pltpu.make_async_remote_copy(src_ref=source(a), dst_ref=part(a, mine, c), send_sem=send_sems.at[a, j],
                                                  recv_sem=recv_sems.at[a, j], device_id=(px, py, c), device_id_type=MESH)
                cp.start()
                sends.append(cp)
        for a in range(n):
            for j, (px, py) in enumerate(chips):
                landed = part(a, 2 * px + py, c)
                pltpu.make_async_remote_copy(src_ref=source(a), dst_ref=landed, send_sem=send_sems.at[a, j],
                                             recv_sem=recv_sems.at[a, j], device_id=(px, py, c), device_id_type=MESH).wait_recv()
                if halved[a]:
                    cp = pltpu.make_async_remote_copy(src_ref=landed, dst_ref=landed, send_sem=pass_send_sems.at[a, j],
                                                      recv_sem=pass_recv_sems.at[a, j], device_id=(x, y, 1 - c), device_id_type=MESH)
                    cp.start()
                    sends.append(cp)
        for a in range(n):
            if halved[a]:
                for j, (px, py) in enumerate(chips):
                    other = part(a, 2 * px + py, 1 - c)
                    pltpu.make_async_remote_copy(src_ref=other, dst_ref=other, send_sem=pass_send_sems.at[a, j],
                                                 recv_sem=pass_recv_sems.at[a, j], device_id=(x, y, 1 - c),
                                                 device_id_type=MESH).wait_recv()
        for cp in sends:
            cp.wait_send()

    sems = pltpu.SemaphoreType.DMA((n, 3))
    return pl.pallas_call(
        body, name="gather_weights",
        in_specs=[_ANY] * n, out_specs=[_ANY] * n,
        out_shape=[jax.ShapeDtypeStruct((4,) + s.shape, s.dtype) for s in shards],
        scratch_shapes=[sems, sems, sems, sems],
    )(*shards)


def _swap_halves(grads):
    n = len(grads)

    def body(*refs):
        ins, outs = refs[:n], refs[n:2 * n]
        send_sems, recv_sems = refs[2 * n:]
        x, y, c = _place()
        started = []
        for a in range(n):
            half = ins[a].shape[1] // 2
            cp = pltpu.make_async_remote_copy(src_ref=ins[a].at[:, pl.ds((1 - c) * half, half), :], dst_ref=outs[a],
                                              send_sem=send_sems.at[a], recv_sem=recv_sems.at[a],
                                              device_id=(x, y, 1 - c), device_id_type=MESH)
            cp.start()
            started.append(cp)
        for cp in started:
            cp.wait()

    return pl.pallas_call(
        body, name="swap_halves",
        in_specs=[_ANY] * n, out_specs=[_ANY] * n,
        out_shape=[jax.ShapeDtypeStruct((4, g.shape[1] // 2, g.shape[2]), F32) for g in grads],
        scratch_shapes=[pltpu.SemaphoreType.DMA((n,)), pltpu.SemaphoreType.DMA((n,))],
    )(*grads)


def _scatter_quarters(sums16):
    n = len(sums16)

    def body(*refs):
        ins, outs = refs[:n], refs[n:2 * n]
        send_sems, recv_sems = refs[2 * n:]
        x, y, c = _place()
        sends = []
        for a in range(n):
            for j, (px, py) in enumerate(_other_chips(x, y)):
                cp = pltpu.make_async_remote_copy(src_ref=ins[a].at[2 * px + py], dst_ref=outs[a].at[j],
                                                  send_sem=send_sems.at[a, j], recv_sem=recv_sems.at[a, j],
                                                  device_id=(px, py, c), device_id_type=MESH)
                cp.start()
                sends.append(cp)
        for cp in sends:
            cp.wait()

    return pl.pallas_call(
        body, name="scatter_quarters",
        in_specs=[_ANY] * n, out_specs=[_ANY] * n,
        out_shape=[jax.ShapeDtypeStruct((3,) + s.shape[1:], BF16) for s in sums16],
        scratch_shapes=[pltpu.SemaphoreType.DMA((n, 3)), pltpu.SemaphoreType.DMA((n, 3))],
    )(*sums16)


def _join_halves(fulls):
    n = len(fulls)

    def body(*refs):
        ins, outs = refs[:n], refs[n:2 * n]
        send_sems, recv_sems = refs[2 * n:]
        x, y, c = _place()
        started = []
        for a in range(n):
            half = ins[a].shape[0] // 2
            rows = pl.ds(c * half, half)
            cp = pltpu.make_async_remote_copy(src_ref=ins[a].at[rows, :], dst_ref=outs[a].at[rows, :], send_sem=send_sems.at[a],
                                              recv_sem=recv_sems.at[a], device_id=(x, y, 1 - c), device_id_type=MESH)
            cp.start()
            started.append(cp)
        for cp in started:
            cp.wait()

    return pl.pallas_call(
        body, name="join_halves",
        in_specs=[_ANY] * n, out_specs=[_ANY] * n,
        out_shape=[jax.ShapeDtypeStruct(f.shape, F32) for f in fulls],
        input_output_aliases={a: a for a in range(n)},
        scratch_shapes=[pltpu.SemaphoreType.DMA((n,)), pltpu.SemaphoreType.DMA((n,))],
    )(*fulls)


def _small_allreduce_adamw(g, w, m, v):
    R = g.shape[0]

    def body(g_ref, w_ref, m_ref, v_ref, gs_ref, d_ref, nm_ref, nv_ref, all_s, send_sems, recv_sems):
        x, y, c = _place()
        me = 4 * x + 2 * y + c
        all_s[me] = g_ref[...]
        sends = []
        for k in range(1, 8):
            peer = (x ^ (k >> 2), y ^ ((k >> 1) & 1), c ^ (k & 1))
            cp = pltpu.make_async_remote_copy(src_ref=g_ref, dst_ref=all_s.at[me], send_sem=send_sems.at[k - 1],
                                              recv_sem=recv_sems.at[k - 1], device_id=peer, device_id_type=MESH)
            cp.start()
            sends.append(cp)
        for cp in sends:
            cp.wait()
        total = all_s[0]
        for d in range(1, 8):
            total = total + all_s[d]
        gs_ref[...] = total
        delta, nm, nv = _adamw_math(w_ref[...], total, m_ref[...], v_ref[...])
        d_ref[...] = delta
        nm_ref[...] = nm
        nv_ref[...] = nv

    vm = pl.BlockSpec(memory_space=pltpu.VMEM)
    return pl.pallas_call(
        body, name="small_allreduce_adamw",
        in_specs=[vm] * 4, out_specs=[vm] * 4, out_shape=[jax.ShapeDtypeStruct((R, LANES), F32)] * 4,
        scratch_shapes=[pltpu.VMEM((8, R, LANES), F32), pltpu.SemaphoreType.DMA((7,)), pltpu.SemaphoreType.DMA((7,))],
        compiler_params=pltpu.CompilerParams(vmem_limit_bytes=VMEM_LIMIT),
    )(g, w, m, v)


_SMALL = (("norm_mix_g", D_MODEL), ("f_bias", N_HEADS), ("sg_ln_g", D_HEADS), ("sg_w", N_HEADS * SG_BLOCK * SG_BLOCK),
          ("sg_b", N_HEADS * SG_BLOCK), ("norm_ffn_g", D_MODEL), ("w_conv", 3 * 2 * D_FF), ("b_conv", 2 * D_FF),
          ("norm_final_g", D_MODEL))


def _pack_small(parts):
    rows = []
    for name, size in _SMALL:
        flat = parts[name].reshape(-1).astype(F32)
        pad = (-size) % (SUBLANES * LANES)
        rows.append(jnp.pad(flat, (0, pad)).reshape(-1, LANES))
    return jnp.concatenate(rows, axis=0)


def _unpack_small(packed, shapes):
    out, r = {}, 0
    for name, size in _SMALL:
        nrows = (size + SUBLANES * LANES - 1) // (SUBLANES * LANES) * SUBLANES
        out[name] = packed[r:r + nrows].reshape(-1)[:size].reshape(shapes[name])
        r += nrows
    return out


def _local_step(x, target, g1, w_in, f_bias, sg_ln_g, sg_w, sg_b, w_out, g2, w_up_q, w_conv, b_conv, w_down, g3):
    S = x.shape[0]
    tm = _row_tile(S, 512)
    tms = _row_tile(S, 256)
    tq = _row_tile(S, 512)

    lane = jnp.arange(D_HEADS)
    seg_avg = jnp.where(lane[:, None] // HEAD_DIM == lane[None, :] // HEAD_DIM, 1.0 / HEAD_DIM, 0.0).astype(BF16)
    head_ind = (lane[:, None] // HEAD_DIM == jnp.arange(LANES)[None, :]).astype(BF16)
    pos_chunk = jnp.arange(SG_BLOCK) // CHUNK
    w_mask32 = jnp.where(pos_chunk[:, None] >= pos_chunk[None, :], sg_w, 0.0)
    w_mask = w_mask32.astype(BF16)
    w_mask_t = jnp.swapaxes(w_mask32, 1, 2).astype(BF16)
    ln_row = sg_ln_g.reshape(1, D_HEADS)
    b_full = jnp.repeat(sg_b.T, HEAD_DIM, axis=1)
    bias_row = jnp.pad(f_bias.reshape(1, N_HEADS), ((0, 0), (0, LANES - N_HEADS)))
    b_conv_row = b_conv.reshape(1, 2 * D_FF)

    z, f, h1 = _in_proj(x, g1, w_in, tm)
    c, ct = _fox_prep(f, bias_row, _row_tile(S, 256))
    consts = _attn_consts()
    qa, ka, va = _attn_pack(z, c, consts, tm)
    out_b, lse = _attn_fwd(qa, ka, va, consts["place_t"], tq)
    out_a = _gate_fwd(z, w_mask, ln_row, b_full, seg_avg, tm)
    x1, h2 = _mix_out(x, out_a, out_b, w_out, g2, tm)
    a = _up_proj(h2, w_up_q, tm)
    dx2, sq_err, dg3 = _ffn_fwd_loss(a, w_conv, b_conv_row, w_down, x1, g3, target, tms)

    dconv, y, dw_conv8, db_conv = _ffn_bwd_gate(dx2, a, w_conv, b_conv_row, w_down, tms)
    dact = _conv_bwd(dconv, w_conv, tm, 2 * D_FF // 4)
    dw_down = _matmul_tn(y, dx2, "dw_down", D_FF // 2, D_MODEL, tm, quarters=(2, 1))
    dx1, dg2 = _up_bwd(dact, w_up_q, x1, g2, dx2, tms)
    dw_up_q = _matmul_tn(h2, dact, "dw_up", D_MODEL, 2 * D_FF // 4, tm, quarters=(1, 4))
    dcat = _out_bwd(dx1, w_out, tm)
    dw_out_a = _matmul_tn(out_a, dx1, "dw_out_a", D_HEADS, D_MODEL, tm)
    dw_out_b = _matmul_tn(out_b, dx1, "dw_out_b", D_HEADS, D_MODEL, tm)
    dzu, dzv, dsg_w, dsg_b_t, dln = _gate_bwd(z, dcat, w_mask, w_mask_t, ln_row, b_full, seg_avg, head_ind, tm)
    dop = _attn_pack_grad(out_b, dcat, head_ind, consts, tm)
    dqp, dk, dv, dcc = _attn_bwd(qa, ka, va, dop, lse, consts, tq)
    dq, dc = _attn_unpack(dqp, dcc, consts, tm)
    df, dbias = _fox_bwd(dc, f, bias_row, _row_tile(S, 256))
    pieces = (dzu, dzv, dq, dk, dv, df)
    dx, dg1 = _in_bwd(pieces, w_in, x, g1, dx1, tms)
    dw_in = _dw_in(h1, pieces, tm)

    grads = {
        "norm_mix_g": dg1, "f_bias": dbias[:, :N_HEADS], "sg_ln_g": dln, "sg_w": dsg_w, "sg_b": dsg_b_t[:, :N_HEADS].T,
        "norm_ffn_g": dg2, "w_conv": dw_conv8[:3], "b_conv": db_conv, "norm_final_g": dg3,
        "w_in": dw_in, "w_out": jnp.concatenate([dw_out_a, dw_out_b], axis=0), "w_up_q": dw_up_q, "w_down": dw_down,
    }
    return sq_err, dx, grads


def _reduce_big(grads_q, ids):
    names = list(grads_q)
    theirs = _swap_halves([grads_q[k] for k in names])
    sums = [_pair_sum("pair_sum_" + k, grads_q[k], t, ids) for k, t in zip(names, theirs)]
    got = _scatter_quarters([s16 for _, s16 in sums])
    fulls = [_chip_sum("chip_sum_" + k, s32, g, ids) for k, (s32, _), g in zip(names, sums, got)]
    return dict(zip(names, _join_halves(fulls)))


def kernel(x, norm_mix_g, w_in, f_bias, sg_ln_g, sg_w, sg_b, w_out, norm_ffn_g, w_up, w_conv, b_conv, w_down, norm_final_g, loss_target, m_norm_mix_g, m_w_in, m_f_bias, m_sg_ln_g, m_sg_w, m_sg_b, m_w_out, m_norm_ffn_g, m_w_up, m_w_conv, m_b_conv, m_w_down, m_norm_final_g, v_norm_mix_g, v_w_in, v_f_bias, v_sg_ln_g, v_sg_w, v_sg_b, v_w_out, v_norm_ffn_g, v_w_up, v_w_conv, v_b_conv, v_w_down, v_norm_final_g):
    args = dict(locals())
    quarter = 2 * lax.axis_index("x") + lax.axis_index("y")
    ids = jnp.stack([quarter, lax.axis_index("c")]).astype(jnp.int32)
    wq_conv = w_conv.shape[-1]

    shards = [w_in[0].astype(BF16), w_out[0].astype(BF16), w_up[0].astype(BF16), w_down[0].astype(BF16), w_conv[0]]
    g_in, g_out, g_up, g_down, g_conv = [lax.dynamic_update_index_in_dim(g, s, quarter, 0)
                                         for g, s in zip(_gather_quarters(shards), shards)]
    w_in_full = jnp.pad(jnp.concatenate([g_in[q] for q in range(4)], axis=1), ((0, 0), (0, D_IN_PAD - D_IN)))
    w_out_full = g_out.reshape(D_MODEL, D_MODEL)
    w_down_full = g_down.reshape(D_FF, D_MODEL)
    w_conv_full = jnp.concatenate([g_conv[q] for q in range(4)], axis=1)

    sq_err, dx, grads = _local_step(
        x[0], loss_target[0], norm_mix_g, w_in_full, f_bias[0], sg_ln_g[0], sg_w[0], sg_b[0], w_out_full, norm_ffn_g,
        g_up, w_conv_full, b_conv[0], w_down_full, norm_final_g.reshape(1, D_MODEL))
    loss = lax.psum(0.5 * jnp.sum(sq_err) / D_MODEL, ("x", "y", "c"))

    dw_in = grads["w_in"][:, :D_IN].reshape(D_MODEL, 4, D_IN // 4).transpose(1, 0, 2)
    big = _reduce_big({"w_in": dw_in, "w_out": grads["w_out"].reshape(4, D_MODEL // 4, D_MODEL),
                       "w_up": grads["w_up_q"], "w_down": grads["w_down"].reshape(4, D_FF // 4, D_MODEL)}, ids)

    out = {"loss": loss, "grad_x": dx[None]}
    for k in ("w_in", "w_out", "w_up", "w_down"):
        g = big[k]
        d, nm, nv = _adamw("adamw_" + k, args[k][0], g, args["m_" + k][0], args["v_" + k][0])
        out["grad_" + k], out["delta_" + k], out["new_m_" + k], out["new_v_" + k] = g[None], d[None], nm[None], nv[None]

    def padded_conv(t):
        return lax.dynamic_update_slice(jnp.zeros((3, 4 * wq_conv), F32), t[0], (0, quarter * wq_conv))

    small_names = [n for n, _ in _SMALL]
    shapes = {n: (3, 4 * wq_conv) if n == "w_conv" else args[n].shape for n in small_names}
    pack = lambda prefix: _pack_small({n: padded_conv(args[prefix + n]) if n == "w_conv" else args[prefix + n] for n in small_names})
    packed = _small_allreduce_adamw(_pack_small({n: grads[n] for n in small_names}), pack(""), pack("m_"), pack("v_"))
    for prefix, arr in zip(("grad_", "delta_", "new_m_", "new_v_"), packed):
        for n, t in _unpack_small(arr, shapes).items():
            if n == "w_conv":
                t = lax.dynamic_slice(t, (0, quarter * wq_conv), (3, wq_conv))[None]
            out[prefix + n] = t

    weights = ["norm_mix_g", "w_in", "f_bias", "sg_ln_g", "sg_w", "sg_b", "w_out", "norm_ffn_g", "w_up", "w_conv", "b_conv",
               "w_down", "norm_final_g"]
    return (out["loss"], out["grad_x"], *[out[p + n] for p in ("grad_", "delta_", "new_m_", "new_v_") for n in weights])
```

```python
import functools
import math

import jax
import jax.numpy as jnp
from jax import lax
from jax.experimental import pallas as pl
from jax.experimental.pallas import tpu as pltpu

F32 = jnp.float32
BF16 = jnp.bfloat16
MESH = pl.DeviceIdType.MESH

D_MODEL = 1024
N_HEADS = 8
HEAD_DIM = 64
D_HEADS = N_HEADS * HEAD_DIM
SG_BLOCK = 128
CHUNK = 64
D_FF = 2816
D_IN = 2 * D_HEADS + 3 * D_HEADS + N_HEADS
LANES = 128
SUBLANES = 8
D_IN_PAD = 5 * D_HEADS + LANES
EPS = 1e-6
SCALE = HEAD_DIM ** -0.5
NEG = -1e30
LOG2E = 1.4426950408889634
HEAD_PAD = LANES
D_PAD = N_HEADS * HEAD_PAD
Q_STAT = HEAD_DIM
K_STAT = HEAD_DIM + 3
L_STAT = HEAD_DIM + 6
GROUPS = 2
GROUP_HEADS = N_HEADS // GROUPS
GROUP_PAD = GROUP_HEADS * HEAD_PAD

ADAM_LR = 0.001
ADAM_B1 = 0.9
ADAM_B2 = 0.999
ADAM_EPS = 1e-08
ADAM_WD = 0.01
ADAM_STEP = 10

VMEM_LIMIT = 56 * 1024 * 1024

NT = (((1,), (1,)), ((), ()))
TN = (((0,), (0,)), ((), ()))


def _params(sem):
    return pltpu.CompilerParams(dimension_semantics=sem, vmem_limit_bytes=VMEM_LIMIT)


def _full(shape):
    nd = len(shape)
    return pl.BlockSpec(shape, lambda *_: (0,) * nd)


def _row_tile(rows, target):
    best = None
    for t in range(SUBLANES, min(rows, target) + 1, SUBLANES):
        if rows % t == 0:
            best = t
    assert best is not None, rows
    return best


def _gelu(z):
    return 0.5 * z * (1.0 + lax.erf(z * (2.0 ** -0.5)))


def _gelu_grad(z):
    cdf = 0.5 * (1.0 + lax.erf(z * (2.0 ** -0.5)))
    pdf = jnp.exp(-0.5 * z * z) * (1.0 / math.sqrt(2.0 * math.pi))
    return cdf + z * pdf


def _split_dot(x, m):
    hi = x.astype(BF16)
    lo = (x - hi.astype(F32)).astype(BF16)
    return jnp.dot(hi, m, preferred_element_type=F32) + jnp.dot(lo, m, preferred_element_type=F32)


def _head_mask(h, rows):
    lane = lax.broadcasted_iota(jnp.int32, (rows, D_HEADS), 1)
    return (lane >= h * HEAD_DIM) & (lane < (h + 1) * HEAD_DIM)


def _rms_bwd(dh, x, g):
    r = lax.rsqrt(jnp.mean(x * x, axis=-1, keepdims=True) + EPS)
    xhat = x * r
    dg = jnp.sum(dh * xhat, axis=0, keepdims=True)
    dxhat = dh * g
    dx = r * (dxhat - xhat * jnp.mean(dxhat * xhat, axis=-1, keepdims=True))
    return dx, dg


def _in_proj(x, g1, w_in, tm):
    S = x.shape[0]
    nz = D_IN_PAD - LANES

    def body(x_ref, g_ref, w_ref, z_ref, f_ref, h_ref):
        xf = x_ref[...]
        r = lax.rsqrt(jnp.mean(xf * xf, axis=-1, keepdims=True) + EPS)
        h = (xf * r * g_ref[...]).astype(BF16)
        h_ref[...] = h
        zz = jnp.dot(h, w_ref[...], preferred_element_type=F32)
        z_ref[...] = zz[:, :nz].astype(BF16)
        f_ref[...] = zz[:, nz:]

    return pl.pallas_call(
        body, name="in_proj", grid=(S // tm,),
        in_specs=[pl.BlockSpec((tm, D_MODEL), lambda i: (i, 0)), _full((1, D_MODEL)), _full((D_MODEL, D_IN_PAD))],
        out_specs=[pl.BlockSpec((tm, nz), lambda i: (i, 0)), pl.BlockSpec((tm, LANES), lambda i: (i, 0)),
                   pl.BlockSpec((tm, D_MODEL), lambda i: (i, 0))],
        out_shape=[jax.ShapeDtypeStruct((S, nz), BF16), jax.ShapeDtypeStruct((S, LANES), F32),
                   jax.ShapeDtypeStruct((S, D_MODEL), BF16)],
        compiler_params=_params(("parallel",)),
    )(x, g1, w_in)


def _fox_prep(f, bias_row, tb):
    S = f.shape[0]

    def body(f_ref, b_ref, c_ref, ct_ref, carry):
        @pl.when(pl.program_id(0) == 0)
        def _():
            carry[...] = jnp.zeros_like(carry)

        xv = f_ref[...] + b_ref[...]
        lf = jnp.minimum(xv, 0.0) - jnp.log(1.0 + jnp.exp(-jnp.abs(xv)))
        r = lax.broadcasted_iota(jnp.int32, (tb, tb), 0)
        s = lax.broadcasted_iota(jnp.int32, (tb, tb), 1)
        tri = (r >= s).astype(F32)
        cs = jnp.dot(tri, lf, precision=lax.Precision.HIGHEST, preferred_element_type=F32) + carry[0:1, :]
        c_ref[...] = cs
        ct_ref[...] = cs.T[:N_HEADS, :]
        carry[...] = jnp.broadcast_to(cs[tb - 1:tb, :], carry.shape)

    return pl.pallas_call(
        body, name="fox_prep", grid=(S // tb,),
        in_specs=[pl.BlockSpec((tb, LANES), lambda i: (i, 0)), _full((1, LANES))],
        out_specs=[pl.BlockSpec((tb, LANES), lambda i: (i, 0)), pl.BlockSpec((N_HEADS, tb), lambda i: (0, i))],
        out_shape=[jax.ShapeDtypeStruct((S, LANES), F32), jax.ShapeDtypeStruct((N_HEADS, S), F32)],
        scratch_shapes=[pltpu.VMEM((SUBLANES, LANES), F32)],
        compiler_params=_params(("arbitrary",)),
    )(f, bias_row)


def _attn_consts():
    col = jnp.arange(D_PAD)
    row = jnp.arange(D_HEADS)
    head = jnp.arange(LANES)
    place = (row[:, None] // HEAD_DIM == col[None, :] // HEAD_PAD) & (row[:, None] % HEAD_DIM == col[None, :] % HEAD_PAD)

    def stat(offset):
        return ((head[:, None] < N_HEADS) & (col[None, :] == head[:, None] * HEAD_PAD + offset)).astype(BF16)

    def ones(offsets):
        return sum((col % HEAD_PAD == o) for o in offsets).astype(F32).reshape(1, D_PAD)

    def pick(offset):
        gcol = jnp.arange(GROUP_PAD)
        return jnp.stack([((gcol[:, None] % HEAD_PAD == offset) & (head[None, :] == g * GROUP_HEADS + gcol[:, None] // HEAD_PAD))
                          for g in range(GROUPS)]).astype(BF16)

    place = place.astype(BF16)
    return {
        "place": place, "place_t": place.T, "place_t_group": place.T[:GROUP_PAD, :GROUP_HEADS * HEAD_DIM],
        "q_stat": jnp.stack([stat(Q_STAT + j) for j in range(3)]), "k_stat": jnp.stack([stat(K_STAT + j) for j in range(3)]),
        "d_stat": jnp.stack([stat(Q_STAT + j) for j in range(2)]), "l_stat": jnp.stack([stat(L_STAT + j) for j in range(3)]),
        "q_ones": ones(range(K_STAT, K_STAT + 3)), "k_ones": ones(list(range(Q_STAT, Q_STAT + 3)) + list(range(L_STAT, L_STAT + 3))),
        "v_ones": ones(range(Q_STAT, Q_STAT + 2)),
        "pick_rows": pick(Q_STAT), "pick_cols": pick(K_STAT),
    }


def _split3(x):
    hi = x.astype(BF16)
    r = x - hi.astype(F32)
    mid = r.astype(BF16)
    return hi, mid, (r - mid.astype(F32)).astype(BF16)


def _split3_dot(x, m):
    return sum(jnp.dot(part, m, preferred_element_type=F32) for part in _split3(x))


def _attn_pack(z, c, k, tm):
    S = z.shape[0]

    def body(q_ref, k_ref, v_ref, c_ref, pl_ref, qs_ref, ks_ref, qo_ref, ko_ref, vo_ref, qa_ref, ka_ref, va_ref):
        place = pl_ref[...]
        q = (q_ref[...].astype(F32) * (SCALE * LOG2E)).astype(BF16)
        qa = jnp.dot(q, place, preferred_element_type=F32) + qo_ref[...]
        ka = jnp.dot(k_ref[...], place, preferred_element_type=F32) + ko_ref[...]
        for j, part in enumerate(_split3(c_ref[...] * LOG2E)):
            qa = qa + jnp.dot(part, qs_ref[j], preferred_element_type=F32)
            ka = ka - jnp.dot(part, ks_ref[j], preferred_element_type=F32)
        qa_ref[...] = qa.astype(BF16)
        ka_ref[...] = ka.astype(BF16)
        va_ref[...] = (jnp.dot(v_ref[...], place, preferred_element_type=F32) + vo_ref[...]).astype(BF16)

    blk = lambda col: pl.BlockSpec((tm, D_HEADS), lambda i: (i, col))
    out = pl.BlockSpec((tm, D_PAD), lambda i: (i, 0))
    return pl.pallas_call(
        body, name="attn_pack", grid=(S // tm,),
        in_specs=[blk(2), blk(3), blk(4), pl.BlockSpec((tm, LANES), lambda i: (i, 0)), _full((D_HEADS, D_PAD)),
                  _full((3, LANES, D_PAD)), _full((3, LANES, D_PAD)), _full((1, D_PAD)), _full((1, D_PAD)), _full((1, D_PAD))],
        out_specs=[out, out, out], out_shape=[jax.ShapeDtypeStruct((S, D_PAD), BF16)] * 3,
        compiler_params=_params(("parallel",)),
    )(z, z, z, c, k["place"], k["q_stat"], k["k_stat"], k["q_ones"], k["k_ones"], k["v_ones"])


def _attn_fwd(qa, ka, va, place_t, tq):
    S = qa.shape[0]
    n = S // tq

    def body(q_ref, k_ref, v_ref, pt_ref, o_ref, lse_ref, m_s, acc_s, opad_s):
        qi, ki = pl.program_id(0), pl.program_id(1)

        @pl.when(ki == 0)
        def _():
            m_s[...] = jnp.full_like(m_s, NEG)
            acc_s[...] = jnp.zeros_like(acc_s)

        def step(diagonal):
            if diagonal:
                row = lax.broadcasted_iota(jnp.int32, (tq, tq), 0)
                col = lax.broadcasted_iota(jnp.int32, (tq, tq), 1)
                keep = row >= col
            for h in range(N_HEADS):
                sl = slice(h * HEAD_PAD, (h + 1) * HEAD_PAD)
                s = lax.dot_general(q_ref[:, sl], k_ref[:, sl], NT, preferred_element_type=F32)
                if diagonal:
                    s = jnp.where(keep, s, NEG)
                m_prev = m_s[h]
                m_new = jnp.maximum(m_prev, jnp.max(s, axis=1, keepdims=True))
                p = jnp.concatenate([jnp.exp2(s[:, j * LANES:(j + 1) * LANES] - m_new) for j in range(tq // LANES)], axis=1)
                acc_s[h] = jnp.exp2(m_prev - m_new) * acc_s[h] + jnp.dot(p.astype(BF16), v_ref[:, sl], preferred_element_type=F32)
                m_s[h] = m_new

        @pl.when(ki < qi)
        def _():
            step(False)

        @pl.when(ki == qi)
        def _():
            step(True)
            lse_ref[...] = jnp.zeros_like(lse_ref)
            for h in range(N_HEADS):
                acc = acc_s[h]
                denom = acc[:, Q_STAT:Q_STAT + 1]
                opad_s[:, h * HEAD_PAD:(h + 1) * HEAD_PAD] = (acc / denom).astype(BF16)
                lse_ref[:, h:h + 1] = m_s[h][:, 0:1] + jnp.log(denom) * LOG2E
            o_ref[...] = jnp.dot(opad_s[...], pt_ref[...], preferred_element_type=F32).astype(BF16)

    kv = pl.BlockSpec((tq, D_PAD), lambda qi, ki: (jnp.minimum(ki, qi), 0))
    return pl.pallas_call(
        body, name="attn_fwd", grid=(n, n),
        in_specs=[pl.BlockSpec((tq, D_PAD), lambda qi, ki: (qi, 0)), kv, kv, _full((D_PAD, D_HEADS))],
        out_specs=[pl.BlockSpec((tq, D_HEADS), lambda qi, ki: (qi, 0)), pl.BlockSpec((tq, LANES), lambda qi, ki: (qi, 0))],
        out_shape=[jax.ShapeDtypeStruct((S, D_HEADS), BF16), jax.ShapeDtypeStruct((S, LANES), F32)],
        scratch_shapes=[pltpu.VMEM((N_HEADS, tq, LANES), F32), pltpu.VMEM((N_HEADS, tq, HEAD_PAD), F32), pltpu.VMEM((tq, D_PAD), BF16)],
        compiler_params=_params(("parallel", "arbitrary")),
    )(qa, ka, va, place_t)


def _layer_norm_heads(v, seg_avg):
    mu = _split_dot(v, seg_avg)
    d = v - mu
    var = _split_dot(d * d, seg_avg)
    rstd = lax.rsqrt(var + EPS)
    return d * rstd, rstd


def _gate_mix(vn_blk, w_ref, bias):
    acc = bias
    for h in range(N_HEADS):
        vh = jnp.where(_head_mask(h, SG_BLOCK), vn_blk, 0.0).astype(BF16)
        acc = acc + jnp.dot(w_ref[h], vh, preferred_element_type=F32)
    return acc


def _gate_fwd(z, w_mask, ln_row, b_full, seg_avg, tm):
    S = z.shape[0]

    def body(zu_ref, zv_ref, w_ref, ln_ref, b_ref, avg_ref, o_ref):
        u = _gelu(zu_ref[...].astype(F32))
        v = _gelu(zv_ref[...].astype(F32))
        vhat, _ = _layer_norm_heads(v, avg_ref[...])
        vn = vhat * ln_ref[...]
        for b in range(tm // SG_BLOCK):
            rows = slice(b * SG_BLOCK, (b + 1) * SG_BLOCK)
            mixed = _gate_mix(vn[rows], w_ref, b_ref[...])
            o_ref[rows, :] = (u[rows] * mixed).astype(BF16)

    return pl.pallas_call(
        body, name="gate_fwd", grid=(S // tm,),
        in_specs=[pl.BlockSpec((tm, D_HEADS), lambda i: (i, 0)), pl.BlockSpec((tm, D_HEADS), lambda i: (i, 1)),
                  _full((N_HEADS, SG_BLOCK, SG_BLOCK)), _full((1, D_HEADS)), _full((SG_BLOCK, D_HEADS)),
                  _full((D_HEADS, D_HEADS))],
        out_specs=pl.BlockSpec((tm, D_HEADS), lambda i: (i, 0)),
        out_shape=jax.ShapeDtypeStruct((S, D_HEADS), BF16),
        compiler_params=_params(("parallel",)),
    )(z, z, w_mask, ln_row, b_full, seg_avg)


def _mix_out(x, out_a, out_b, w_out, g2, tm):
    S = x.shape[0]

    def body(x_ref, a_ref, b_ref, w_ref, g_ref, x1_ref, h_ref):
        y = jnp.dot(a_ref[...], w_ref[:D_HEADS, :], preferred_element_type=F32)
        y = y + jnp.dot(b_ref[...], w_ref[D_HEADS:, :], preferred_element_type=F32)
        x1 = x_ref[...] + y
        x1_ref[...] = x1
        r = lax.rsqrt(jnp.mean(x1 * x1, axis=-1, keepdims=True) + EPS)
        h_ref[...] = (x1 * r * g_ref[...]).astype(BF16)

    row = lambda w: pl.BlockSpec((tm, w), lambda i: (i, 0))
    return pl.pallas_call(
        body, name="mix_out", grid=(S // tm,),
        in_specs=[row(D_MODEL), row(D_HEADS), row(D_HEADS), _full((D_MODEL, D_MODEL)), _full((1, D_MODEL))],
        out_specs=[row(D_MODEL), row(D_MODEL)],
        out_shape=[jax.ShapeDtypeStruct((S, D_MODEL), F32), jax.ShapeDtypeStruct((S, D_MODEL), BF16)],
        compiler_params=_params(("parallel",)),
    )(x, out_a, out_b, w_out, g2)


def _up_proj(h2, w_up_q, tm):
    S = h2.shape[0]
    nq, _, wq = w_up_q.shape

    def body(h_ref, w_ref, a_ref):
        a_ref[...] = jnp.dot(h_ref[...], w_ref[...], preferred_element_type=F32).astype(BF16)

    return pl.pallas_call(
        body, name="up_proj", grid=(nq, S // tm),
        in_specs=[pl.BlockSpec((tm, D_MODEL), lambda j, i: (i, 0)), pl.BlockSpec((None, D_MODEL, wq), lambda j, i: (j, 0, 0))],
        out_specs=pl.BlockSpec((tm, wq), lambda j, i: (i, j)),
        out_shape=jax.ShapeDtypeStruct((S, nq * wq), BF16),
        compiler_params=_params(("parallel", "parallel")),
    )(h2, w_up_q)


def _shift_down(a, halo, k):
    tm = a.shape[0]
    ra = pltpu.roll(a, k, 0)
    rh = pltpu.roll(halo, k, 0)
    row = lax.broadcasted_iota(jnp.int32, halo.shape, 0)
    top = jnp.where(row < k, rh, ra[0:SUBLANES])
    return jnp.concatenate([top, ra[SUBLANES:tm]], axis=0)


def _shift_up(a, halo, k):
    tm = a.shape[0]
    ra = pltpu.roll(a, tm - k, 0)
    rh = pltpu.roll(halo, SUBLANES - k, 0)
    row = lax.broadcasted_iota(jnp.int32, halo.shape, 0)
    bottom = jnp.where(row >= SUBLANES - k, rh, ra[tm - SUBLANES:tm])
    return jnp.concatenate([ra[0:tm - SUBLANES], bottom], axis=0)


def _conv_taps(a_ref, halo_ref, first):
    a = a_ref[...].astype(F32)
    halo = halo_ref[...].astype(F32) * jnp.where(first, 0.0, 1.0)
    return a, _shift_down(a, halo, 1), _shift_down(a, halo, 2)


def _conv_specs(tm):
    step = tm // SUBLANES
    prev = lambda i: jnp.maximum(i * step - 1, 0)
    return [pl.BlockSpec((tm, D_FF), lambda i: (i, 0)), pl.BlockSpec((tm, D_FF), lambda i: (i, 1)),
            pl.BlockSpec((SUBLANES, D_FF), lambda i: (prev(i), 0)), pl.BlockSpec((SUBLANES, D_FF), lambda i: (prev(i), 1))]


def _ffn_fwd_loss(a, w_conv, b_conv, w_down, x1, g3, target, tm):
    S = x1.shape[0]

    def body(ag_ref, av_ref, hg_ref, hv_ref, wg_ref, wv_ref, bg_ref, bv_ref, wd_ref, x1_ref, g_ref, t_ref,
             dx2_ref, loss_ref, dg_ref):
        i = pl.program_id(0)

        @pl.when(i == 0)
        def _():
            loss_ref[...] = jnp.zeros_like(loss_ref)
            dg_ref[...] = jnp.zeros_like(dg_ref)

        g0, g1, g2 = _conv_taps(ag_ref, hg_ref, i == 0)
        gate = wg_ref[2:3, :] * g0 + wg_ref[1:2, :] * g1 + wg_ref[0:1, :] * g2 + bg_ref[...]
        v0, v1, v2 = _conv_taps(av_ref, hv_ref, i == 0)
        val = wv_ref[2:3, :] * v0 + wv_ref[1:2, :] * v1 + wv_ref[0:1, :] * v2 + bv_ref[...]
        y = (gate * jax.nn.sigmoid(gate) * val).astype(BF16)
        x2 = x1_ref[...] + jnp.dot(y, wd_ref[...], preferred_element_type=F32)
        r = lax.rsqrt(jnp.mean(x2 * x2, axis=-1, keepdims=True) + EPS)
        xhat = x2 * r
        gg = g_ref[...]
        err = xhat * gg - t_ref[...]
        loss_ref[...] += jnp.sum(err * err, axis=0, keepdims=True)
        dy = err * (1.0 / D_MODEL)
        dg_ref[...] += jnp.sum(dy * xhat, axis=0, keepdims=True)
        dxhat = dy * gg
        dx2_ref[...] = r * (dxhat - xhat * jnp.mean(dxhat * xhat, axis=-1, keepdims=True))

    row = lambda w: pl.BlockSpec((tm, w), lambda i: (i, 0))
    half = lambda r: [pl.BlockSpec((r, D_FF), lambda i: (0, 0)), pl.BlockSpec((r, D_FF), lambda i: (0, 1))]
    return pl.pallas_call(
        body, name="ffn_fwd_loss", grid=(S // tm,),
        in_specs=_conv_specs(tm) + half(3) + half(1) + [_full((D_FF, D_MODEL)), row(D_MODEL), _full((1, D_MODEL)), row(D_MODEL)],
        out_specs=[row(D_MODEL), _full((1, D_MODEL)), _full((1, D_MODEL))],
        out_shape=[jax.ShapeDtypeStruct((S, D_MODEL), F32), jax.ShapeDtypeStruct((1, D_MODEL), F32),
                   jax.ShapeDtypeStruct((1, D_MODEL), F32)],
        compiler_params=_params(("arbitrary",)),
    )(a, a, a, a, w_conv, w_conv, b_conv, b_conv, w_down, x1, g3, target)


def _ffn_bwd_gate(dx2, a, w_conv, b_conv, w_down, tm):
    S = dx2.shape[0]

    def body(dx_ref, ag_ref, av_ref, hg_ref, hv_ref, wg_ref, wv_ref, bg_ref, bv_ref, wd_ref,
             dc_ref, y_ref, dw_ref, db_ref):
        i = pl.program_id(0)

        @pl.when(i == 0)
        def _():
            dw_ref[...] = jnp.zeros_like(dw_ref)
            db_ref[...] = jnp.zeros_like(db_ref)

        g0, g1, g2 = _conv_taps(ag_ref, hg_ref, i == 0)
        gate = wg_ref[2:3, :] * g0 + wg_ref[1:2, :] * g1 + wg_ref[0:1, :] * g2 + bg_ref[...]
        v0, v1, v2 = _conv_taps(av_ref, hv_ref, i == 0)
        val = wv_ref[2:3, :] * v0 + wv_ref[1:2, :] * v1 + wv_ref[0:1, :] * v2 + bv_ref[...]
        sg = jax.nn.sigmoid(gate)
        act = gate * sg
        y_ref[...] = (act * val).astype(BF16)
        dy = lax.dot_general(dx_ref[...].astype(BF16), wd_ref[...], NT, preferred_element_type=F32)
        dgate = dy * val * (sg * (1.0 + gate * (1.0 - sg)))
        dval = dy * act
        dc_ref[:, :D_FF] = dgate.astype(BF16)
        dc_ref[:, D_FF:] = dval.astype(BF16)
        for half, (d, taps) in enumerate(((dgate, (g2, g1, g0)), (dval, (v2, v1, v0)))):
            cols = slice(half * D_FF, (half + 1) * D_FF)
            db_ref[0:1, cols] += jnp.sum(d, axis=0, keepdims=True)
            for j in range(3):
                dw_ref[j:j + 1, cols] += jnp.sum(d * taps[j], axis=0, keepdims=True)

    row = lambda w: pl.BlockSpec((tm, w), lambda i: (i, 0))
    half = lambda r: [pl.BlockSpec((r, D_FF), lambda i: (0, 0)), pl.BlockSpec((r, D_FF), lambda i: (0, 1))]
    return pl.pallas_call(
        body, name="ffn_bwd_gate", grid=(S // tm,),
        in_specs=[row(D_MODEL)] + _conv_specs(tm) + half(3) + half(1) + [_full((D_FF, D_MODEL))],
        out_specs=[row(2 * D_FF), row(D_FF), _full((SUBLANES, 2 * D_FF)), _full((1, 2 * D_FF))],
        out_shape=[jax.ShapeDtypeStruct((S, 2 * D_FF), BF16), jax.ShapeDtypeStruct((S, D_FF), BF16),
                   jax.ShapeDtypeStruct((SUBLANES, 2 * D_FF), F32), jax.ShapeDtypeStruct((1, 2 * D_FF), F32)],
        compiler_params=_params(("arbitrary",)),
    )(dx2, a, a, a, a, w_conv, w_conv, b_conv, b_conv, w_down)


def _conv_bwd(dc, w_conv, tm, tn):
    S, C = dc.shape
    step = tm // SUBLANES
    last_blk = S // SUBLANES - 1

    def body(d_ref, nx_ref, w_ref, o_ref):
        last = pl.program_id(0) == pl.num_programs(0) - 1
        d = d_ref[...].astype(F32)
        nx = nx_ref[...].astype(F32) * jnp.where(last, 0.0, 1.0)
        out = w_ref[2:3, :] * d + w_ref[1:2, :] * _shift_up(d, nx, 1) + w_ref[0:1, :] * _shift_up(d, nx, 2)
        o_ref[...] = out.astype(BF16)

    return pl.pallas_call(
        body, name="conv_bwd", grid=(S // tm, C // tn),
        in_specs=[pl.BlockSpec((tm, tn), lambda i, j: (i, j)),
                  pl.BlockSpec((SUBLANES, tn), lambda i, j: (jnp.minimum((i + 1) * step, last_blk), j)),
                  pl.BlockSpec((3, tn), lambda i, j: (0, j))],
        out_specs=pl.BlockSpec((tm, tn), lambda i, j: (i, j)),
        out_shape=jax.ShapeDtypeStruct((S, C), BF16),
        compiler_params=_params(("parallel", "parallel")),
    )(dc, dc, w_conv)


def _matmul_tn(a, b, name, bm, bn, tk, col_a=0, col_b=0, quarters=None):
    S = a.shape[0]
    gm, gn = quarters if quarters else (1, 1)
    nk = S // tk

    def body(a_ref, b_ref, o_ref):
        @pl.when(pl.program_id(2) == 0)
        def _():
            o_ref[...] = jnp.zeros_like(o_ref)

        o_ref[...] += lax.dot_general(a_ref[...].astype(BF16), b_ref[...].astype(BF16), TN, preferred_element_type=F32)

    if quarters and gn > 1:
        out_spec = pl.BlockSpec((None, bm, bn), lambda i, j, k: (j, i, 0))
        out_shape = jax.ShapeDtypeStruct((gn, gm * bm, bn), F32)
    else:
        out_spec = pl.BlockSpec((bm, bn), lambda i, j, k: (i, j))
        out_shape = jax.ShapeDtypeStruct((gm * bm, gn * bn), F32)
    return pl.pallas_call(
        body, name=name, grid=(gm, gn, nk),
        in_specs=[pl.BlockSpec((tk, bm), lambda i, j, k: (k, col_a * gm + i)),
                  pl.BlockSpec((tk, bn), lambda i, j, k: (k, col_b * gn + j))],
        out_specs=out_spec, out_shape=out_shape,
        compiler_params=_params(("parallel", "parallel", "arbitrary")),
    )(a, b)


def _up_bwd(dact, w_up_q, x1, g2, dx2, tm):
    S = x1.shape[0]
    nq, _, wq = w_up_q.shape

    def body(d_ref, w_ref, x_ref, g_ref, dx2_ref, dx1_ref, dg_ref):
        @pl.when(pl.program_id(0) == 0)
        def _():
            dg_ref[...] = jnp.zeros_like(dg_ref)

        dh = jnp.zeros((tm, D_MODEL), F32)
        for j in range(nq):
            dh = dh + lax.dot_general(d_ref[:, j * wq:(j + 1) * wq], w_ref[j], NT, preferred_element_type=F32)
        dx, dg = _rms_bwd(dh, x_ref[...], g_ref[...])
        dg_ref[...] += dg
        dx1_ref[...] = dx2_ref[...] + dx

    row = lambda w: pl.BlockSpec((tm, w), lambda i: (i, 0))
    return pl.pallas_call(
        body, name="up_bwd", grid=(S // tm,),
        in_specs=[row(nq * wq), pl.BlockSpec((nq, D_MODEL, wq), lambda i: (0, 0, 0), pipeline_mode=pl.Buffered(1)),
                  row(D_MODEL), _full((1, D_MODEL)), row(D_MODEL)],
        out_specs=[row(D_MODEL), _full((1, D_MODEL))],
        out_shape=[jax.ShapeDtypeStruct((S, D_MODEL), F32), jax.ShapeDtypeStruct((1, D_MODEL), F32)],
        compiler_params=_params(("arbitrary",)),
    )(dact, w_up_q, x1, g2, dx2)


def _out_bwd(dx1, w_out, tm):
    S = dx1.shape[0]

    def body(d_ref, w_ref, o_ref):
        o_ref[...] = lax.dot_general(d_ref[...].astype(BF16), w_ref[...], NT, preferred_element_type=F32).astype(BF16)

    return pl.pallas_call(
        body, name="out_bwd", grid=(S // tm,),
        in_specs=[pl.BlockSpec((tm, D_MODEL), lambda i: (i, 0)), _full((D_MODEL, D_MODEL))],
        out_specs=pl.BlockSpec((tm, D_MODEL), lambda i: (i, 0)),
        out_shape=jax.ShapeDtypeStruct((S, D_MODEL), BF16),
        compiler_params=_params(("parallel",)),
    )(dx1, w_out)


def _gate_bwd(z, dcat, w_mask, w_mask_t, ln_row, b_full, seg_avg, head_ind, tm):
    S = z.shape[0]
    nb = tm // SG_BLOCK

    def body(zu_ref, zv_ref, do_ref, w_ref, wt_ref, ln_ref, b_ref, avg_ref, ind_ref,
             dzu_ref, dzv_ref, dw_ref, db_ref, dln_ref, dvn_s, dbf_s):
        i = pl.program_id(0)

        @pl.when(i == 0)
        def _():
            dw_ref[...] = jnp.zeros_like(dw_ref)
            dln_ref[...] = jnp.zeros_like(dln_ref)
            dbf_s[...] = jnp.zeros_like(dbf_s)

        zu = zu_ref[...].astype(F32)
        zv = zv_ref[...].astype(F32)
        u = _gelu(zu)
        v = _gelu(zv)
        avg = avg_ref[...]
        vhat, rstd = _layer_norm_heads(v, avg)
        ln = ln_ref[...]
        vn = vhat * ln
        for b in range(nb):
            rows = slice(b * SG_BLOCK, (b + 1) * SG_BLOCK)
            vn_b = vn[rows]
            mixed = _gate_mix(vn_b, w_ref, b_ref[...])
            do = do_ref[rows, :].astype(F32)
            dzu_ref[rows, :] = (do * mixed * _gelu_grad(zu[rows])).astype(BF16)
            dmix = do * u[rows]
            dbf_s[...] += dmix
            vn_bf = vn_b.astype(BF16)
            dvn = jnp.zeros((SG_BLOCK, D_HEADS), F32)
            for h in range(N_HEADS):
                dmh = jnp.where(_head_mask(h, SG_BLOCK), dmix, 0.0).astype(BF16)
                dw_ref[h] += lax.dot_general(dmh, vn_bf, NT, preferred_element_type=F32)
                dvn = dvn + jnp.dot(wt_ref[h], dmh, preferred_element_type=F32)
            dvn_s[rows, :] = dvn
        dvn = dvn_s[...]
        dln_ref[...] += jnp.sum(dvn * vhat, axis=0, keepdims=True)
        dvhat = dvn * ln
        dv = rstd * (dvhat - _split_dot(dvhat, avg) - vhat * _split_dot(dvhat * vhat, avg))
        dzv_ref[...] = (dv * _gelu_grad(zv)).astype(BF16)

        @pl.when(i == pl.num_programs(0) - 1)
        def _():
            r = lax.broadcasted_iota(jnp.int32, (SG_BLOCK, SG_BLOCK), 0) // CHUNK
            s = lax.broadcasted_iota(jnp.int32, (SG_BLOCK, SG_BLOCK), 1) // CHUNK
            for h in range(N_HEADS):
                dw_ref[h] = jnp.where(r >= s, dw_ref[h], 0.0)
            db_ref[...] = _split_dot(dbf_s[...], ind_ref[...])

    row = lambda col: pl.BlockSpec((tm, D_HEADS), lambda i: (i, col))
    wspec = _full((N_HEADS, SG_BLOCK, SG_BLOCK))
    return pl.pallas_call(
        body, name="gate_bwd", grid=(S // tm,),
        in_specs=[row(0), row(1), row(0), wspec, wspec, _full((1, D_HEADS)), _full((SG_BLOCK, D_HEADS)),
                  _full((D_HEADS, D_HEADS)), _full((D_HEADS, LANES))],
        out_specs=[row(0), row(0), wspec, _full((SG_BLOCK, LANES)), _full((1, D_HEADS))],
        out_shape=[jax.ShapeDtypeStruct((S, D_HEADS), BF16), jax.ShapeDtypeStruct((S, D_HEADS), BF16),
                   jax.ShapeDtypeStruct((N_HEADS, SG_BLOCK, SG_BLOCK), F32), jax.ShapeDtypeStruct((SG_BLOCK, LANES), F32),
                   jax.ShapeDtypeStruct((1, D_HEADS), F32)],
        scratch_shapes=[pltpu.VMEM((tm, D_HEADS), F32), pltpu.VMEM((SG_BLOCK, D_HEADS), F32)],
        compiler_params=_params(("arbitrary",)),
    )(z, z, dcat, w_mask, w_mask_t, ln_row, b_full, seg_avg, head_ind)


def _attn_pack_grad(o, dcat, qa, lse, head_ind, k, tm):
    S = o.shape[0]

    def body(o_ref, do_ref, qa_ref, lse_ref, ind_ref, pl_ref, ds_ref, ls_ref, dop_ref, qb_ref):
        do = do_ref[...]
        delta = _split_dot(o_ref[...].astype(F32) * do.astype(F32), ind_ref[...])
        hi = delta.astype(BF16)
        lo = (delta - hi.astype(F32)).astype(BF16)
        dop = jnp.dot(do, pl_ref[...], preferred_element_type=F32)
        dop = dop - jnp.dot(hi, ds_ref[0], preferred_element_type=F32) - jnp.dot(lo, ds_ref[1], preferred_element_type=F32)
        dop_ref[...] = dop.astype(BF16)
        qb = qa_ref[...].astype(F32)
        for j, part in enumerate(_split3(lse_ref[...])):
            qb = qb - jnp.dot(part, ls_ref[j], preferred_element_type=F32)
        qb_ref[...] = qb.astype(BF16)

    pad = pl.BlockSpec((tm, D_PAD), lambda i: (i, 0))
    return pl.pallas_call(
        body, name="attn_pack_grad", grid=(S // tm,),
        in_specs=[pl.BlockSpec((tm, D_HEADS), lambda i: (i, 0)), pl.BlockSpec((tm, D_HEADS), lambda i: (i, 1)), pad,
                  pl.BlockSpec((tm, LANES), lambda i: (i, 0)), _full((D_HEADS, LANES)), _full((D_HEADS, D_PAD)),
                  _full((2, LANES, D_PAD)), _full((3, LANES, D_PAD))],
        out_specs=[pad, pad], out_shape=[jax.ShapeDtypeStruct((S, D_PAD), BF16)] * 2,
        compiler_params=_params(("parallel",)),
    )(o, dcat, qa, lse, head_ind, k["place"], k["d_stat"], k["l_stat"])


def _attn_bwd(qb, ka, va, dop, k, tq):
    S = qb.shape[0]
    n = S // tq

    def body(q_ref, k_ref, v_ref, do_ref, pt_ref, pick_ref, dq_hbm, dk_ref, dv_ref, dcc_ref, dq_s, dk_s, dv_s, sem):
        g, ki, qi = pl.program_id(0), pl.program_id(1), pl.program_id(2)

        @pl.when((ki == 0) & (qi == 0))
        def _():
            dq_s[...] = jnp.zeros_like(dq_s)

        @pl.when(qi == ki)
        def _():
            dk_s[...] = jnp.zeros_like(dk_s)
            dv_s[...] = jnp.zeros_like(dv_s)

        def step(diagonal):
            rows = pl.ds(pl.multiple_of(qi * tq, tq), tq)
            if diagonal:
                row = lax.broadcasted_iota(jnp.int32, (tq, tq), 0)
                col = lax.broadcasted_iota(jnp.int32, (tq, tq), 1)
                keep = row >= col
            for hh in range(GROUP_HEADS):
                sl = slice(hh * HEAD_PAD, (hh + 1) * HEAD_PAD)
                q, kk, v, do = q_ref[:, sl], k_ref[:, sl], v_ref[:, sl], do_ref[:, sl]
                s = lax.dot_general(q, kk, NT, preferred_element_type=F32)
                p = jnp.exp2(s)
                if diagonal:
                    p = jnp.where(keep, p, 0.0)
                dv_s[:, sl] += lax.dot_general(p.astype(BF16), do, TN, preferred_element_type=F32)
                ds = (p * lax.dot_general(do, v, NT, preferred_element_type=F32)).astype(BF16)
                dk_s[:, sl] += lax.dot_general(ds, q, TN, preferred_element_type=F32)
                dq_s[rows, sl] += jnp.dot(ds, kk, preferred_element_type=F32)

        @pl.when(qi > ki)
        def _():
            step(False)

        @pl.when(qi == ki)
        def _():
            step(True)

        @pl.when(qi == n - 1)
        def _():
            dk = dk_s[...]
            dk_ref[...] = jnp.dot((dk * (1.0 / LOG2E)).astype(BF16), pt_ref[...], preferred_element_type=F32).astype(BF16)
            dv_ref[...] = jnp.dot(dv_s[...].astype(BF16), pt_ref[...], preferred_element_type=F32).astype(BF16)
            dcc_ref[...] = _split3_dot(dk, pick_ref[...])

        @pl.when((ki == n - 1) & (qi == n - 1))
        def _():
            cp = pltpu.make_async_copy(dq_s, dq_hbm.at[g], sem)
            cp.start()
            cp.wait()

    gw = GROUP_HEADS * HEAD_DIM
    qspec = pl.BlockSpec((tq, GROUP_PAD), lambda g, ki, qi: (jnp.maximum(qi, ki), g))
    kspec = pl.BlockSpec((tq, GROUP_PAD), lambda g, ki, qi: (ki, g))
    kout = pl.BlockSpec((tq, gw), lambda g, ki, qi: (ki, g))
    return pl.pallas_call(
        body, name="attn_bwd", grid=(GROUPS, n, n),
        in_specs=[qspec, kspec, kspec, qspec,
                  _full((GROUP_PAD, gw)), pl.BlockSpec((None, GROUP_PAD, LANES), lambda g, ki, qi: (g, 0, 0))],
        out_specs=[_ANY, kout, kout, pl.BlockSpec((None, tq, LANES), lambda g, ki, qi: (g, ki, 0))],
        out_shape=[jax.ShapeDtypeStruct((GROUPS, S, GROUP_PAD), F32), jax.ShapeDtypeStruct((S, D_HEADS), BF16),
                   jax.ShapeDtypeStruct((S, D_HEADS), BF16), jax.ShapeDtypeStruct((GROUPS, S, LANES), F32)],
        scratch_shapes=[pltpu.VMEM((S, GROUP_PAD), F32), pltpu.VMEM((tq, GROUP_PAD), F32), pltpu.VMEM((tq, GROUP_PAD), F32),
                        pltpu.SemaphoreType.DMA],
        compiler_params=_params(("arbitrary", "arbitrary", "arbitrary")),
    )(qb, ka, va, dop, k["place_t_group"], k["pick_cols"])


def _attn_unpack(dqp, dcc, k, tm):
    S = dqp.shape[1]
    gw = GROUP_HEADS * HEAD_DIM

    def body(dqp_ref, dcc_ref, pt_ref, pick_ref, dq_ref, dc_ref):
        dc = jnp.zeros((tm, LANES), F32)
        for g in range(GROUPS):
            x = dqp_ref[g]
            dq_ref[:, g * gw:(g + 1) * gw] = jnp.dot((x * SCALE).astype(BF16), pt_ref[...], preferred_element_type=F32).astype(BF16)
            dc = dc + _split3_dot(x, pick_ref[g]) - dcc_ref[g]
        dc_ref[...] = dc

    return pl.pallas_call(
        body, name="attn_unpack", grid=(S // tm,),
        in_specs=[pl.BlockSpec((GROUPS, tm, GROUP_PAD), lambda i: (0, i, 0)), pl.BlockSpec((GROUPS, tm, LANES), lambda i: (0, i, 0)),
                  _full((GROUP_PAD, gw)), _full((GROUPS, GROUP_PAD, LANES))],
        out_specs=[pl.BlockSpec((tm, D_HEADS), lambda i: (i, 0)), pl.BlockSpec((tm, LANES), lambda i: (i, 0))],
        out_shape=[jax.ShapeDtypeStruct((S, D_HEADS), BF16), jax.ShapeDtypeStruct((S, LANES), F32)],
        compiler_params=_params(("parallel",)),
    )(dqp, dcc, k["place_t_group"], k["pick_rows"])


def _fox_bwd(dc, f, bias_row, tb):
    S = f.shape[0]
    nb = S // tb

    def body(dc_ref, f_ref, b_ref, df_ref, dbias_ref, carry):
        @pl.when(pl.program_id(0) == 0)
        def _():
            carry[...] = jnp.zeros_like(carry)
            dbias_ref[...] = jnp.zeros_like(dbias_ref)

        r = lax.broadcasted_iota(jnp.int32, (tb, tb), 0)
        s = lax.broadcasted_iota(jnp.int32, (tb, tb), 1)
        tri = (s >= r).astype(F32)
        rc = jnp.dot(tri, dc_ref[...], precision=lax.Precision.HIGHEST, preferred_element_type=F32) + carry[0:1, :]
        carry[...] = jnp.broadcast_to(rc[0:1, :], carry.shape)
        lane = lax.broadcasted_iota(jnp.int32, (tb, LANES), 1)
        df = jnp.where(lane < N_HEADS, rc * jax.nn.sigmoid(-(f_ref[...] + b_ref[...])), 0.0)
        df_ref[...] = df.astype(BF16)
        dbias_ref[...] += jnp.sum(df, axis=0, keepdims=True)

    rev = pl.BlockSpec((tb, LANES), lambda i: (nb - 1 - i, 0))
    return pl.pallas_call(
        body, name="fox_bwd", grid=(nb,),
        in_specs=[rev, rev, _full((1, LANES))],
        out_specs=[rev, _full((1, LANES))],
        out_shape=[jax.ShapeDtypeStruct((S, LANES), BF16), jax.ShapeDtypeStruct((1, LANES), F32)],
        scratch_shapes=[pltpu.VMEM((SUBLANES, LANES), F32)],
        compiler_params=_params(("arbitrary",)),
    )(dc, f, bias_row)


_DZ_WIDTHS = (D_HEADS,) * 5 + (LANES,)


def _in_bwd(pieces, w_in, x, g1, dx1, tm):
    S = x.shape[0]

    def body(*refs):
        p_refs, (w_ref, x_ref, g_ref, dx1_ref, dx_ref, dg_ref) = refs[:6], refs[6:]

        @pl.when(pl.program_id(0) == 0)
        def _():
            dg_ref[...] = jnp.zeros_like(dg_ref)

        dh = jnp.zeros((tm, D_MODEL), F32)
        off = 0
        for p_ref, w in zip(p_refs, _DZ_WIDTHS):
            dh = dh + lax.dot_general(p_ref[...].astype(BF16), w_ref[:, off:off + w], NT, preferred_element_type=F32)
            off += w
        dx, dg = _rms_bwd(dh, x_ref[...], g_ref[...])
        dg_ref[...] += dg
        dx_ref[...] = dx1_ref[...] + dx

    row = lambda w: pl.BlockSpec((tm, w), lambda i: (i, 0))
    return pl.pallas_call(
        body, name="in_bwd", grid=(S // tm,),
        in_specs=[row(w) for w in _DZ_WIDTHS] + [_full((D_MODEL, D_IN_PAD)), row(D_MODEL), _full((1, D_MODEL)), row(D_MODEL)],
        out_specs=[row(D_MODEL), _full((1, D_MODEL))],
        out_shape=[jax.ShapeDtypeStruct((S, D_MODEL), F32), jax.ShapeDtypeStruct((1, D_MODEL), F32)],
        compiler_params=_params(("arbitrary",)),
    )(*pieces, w_in, x, g1, dx1)


def _dw_in(h1, pieces, tk):
    S = h1.shape[0]

    def body(*refs):
        h_ref, p_refs, o_ref = refs[0], refs[1:7], refs[7]

        @pl.when(pl.program_id(0) == 0)
        def _():
            o_ref[...] = jnp.zeros_like(o_ref)

        off = 0
        for p_ref, w in zip(p_refs, _DZ_WIDTHS):
            o_ref[:, off:off + w] += lax.dot_general(h_ref[...], p_ref[...].astype(BF16), TN, preferred_element_type=F32)
            off += w

    row = lambda w: pl.BlockSpec((tk, w), lambda k: (k, 0))
    return pl.pallas_call(
        body, name="dw_in", grid=(S // tk,),
        in_specs=[row(D_MODEL)] + [row(w) for w in _DZ_WIDTHS],
        out_specs=_full((D_MODEL, D_IN_PAD)),
        out_shape=jax.ShapeDtypeStruct((D_MODEL, D_IN_PAD), F32),
        compiler_params=_params(("arbitrary",)),
    )(h1, *pieces)


def _adamw_math(w, g, m, v):
    m = ADAM_B1 * m + (1.0 - ADAM_B1) * g
    v = ADAM_B2 * v + (1.0 - ADAM_B2) * (g * g)
    m_hat = m / (1.0 - ADAM_B1 ** ADAM_STEP)
    v_hat = v / (1.0 - ADAM_B2 ** ADAM_STEP)
    delta = -ADAM_LR * (m_hat / (jnp.sqrt(v_hat) + ADAM_EPS) + ADAM_WD * w)
    return delta, m, v


def _adamw(name, w, g, m, v):
    R, C = w.shape
    tr = _row_tile(R, 256)

    def body(w_ref, g_ref, m_ref, v_ref, d_ref, nm_ref, nv_ref):
        d, nm, nv = _adamw_math(w_ref[...], g_ref[...], m_ref[...], v_ref[...])
        d_ref[...] = d
        nm_ref[...] = nm
        nv_ref[...] = nv

    spec = pl.BlockSpec((tr, C), lambda i: (i, 0))
    return pl.pallas_call(
        body, name=name, grid=(R // tr,), in_specs=[spec] * 4, out_specs=[spec] * 3,
        out_shape=[jax.ShapeDtypeStruct((R, C), F32)] * 3,
        compiler_params=_params(("parallel",)),
    )(w, g, m, v)


def _pair_sum(name, grad, theirs, ids):
    q, half, C = theirs.shape
    tr = _row_tile(half, 256)
    nb = half // tr

    def body(ids_ref, a_ref, b_ref, s_ref, sb_ref):
        s = a_ref[...] + b_ref[...]
        s_ref[...] = s
        sb_ref[...] = s.astype(BF16)

    here = pl.BlockSpec((None, tr, C), lambda j, i, ids: (j, i, 0))
    return pl.pallas_call(
        body, name=name,
        grid_spec=pltpu.PrefetchScalarGridSpec(
            num_scalar_prefetch=1, grid=(q, nb),
            in_specs=[pl.BlockSpec((None, tr, C), lambda j, i, ids: (j, ids[1] * nb + i, 0)), here],
            out_specs=[here, here]),
        out_shape=[jax.ShapeDtypeStruct((q, half, C), F32), jax.ShapeDtypeStruct((q, half, C), BF16)],
        compiler_params=_params(("parallel", "parallel")),
    )(ids, grad, theirs)


def _chip_sum(name, sums32, others, ids):
    _, half, C = sums32.shape
    tr = _row_tile(half, 256)
    nb = half // tr

    def body(ids_ref, a_ref, o_ref, s_ref):
        s = a_ref[...]
        for j in range(3):
            s = s + o_ref[j].astype(F32)
        s_ref[...] = s

    return pl.pallas_call(
        body, name=name,
        grid_spec=pltpu.PrefetchScalarGridSpec(
            num_scalar_prefetch=1, grid=(nb,),
            in_specs=[pl.BlockSpec((None, tr, C), lambda i, ids: (ids[0], i, 0)),
                      pl.BlockSpec((3, tr, C), lambda i, ids: (0, i, 0))],
            out_specs=pl.BlockSpec((tr, C), lambda i, ids: (ids[1] * nb + i, 0))),
        out_shape=jax.ShapeDtypeStruct((2 * half, C), F32),
        compiler_params=_params(("parallel",)),
    )(ids, sums32, others)


def _place():
    return lax.axis_index("x"), lax.axis_index("y"), lax.axis_index("c")


def _other_chips(x, y):
    return [(1 - x, y), (x, 1 - y), (1 - x, 1 - y)]


_ANY = pl.BlockSpec(memory_space=pl.ANY)


def _gather_quarters(shards):
    n = len(shards)
    halved = [s.shape[0] % 32 == 0 for s in shards]

    def body(*refs):
        ins, outs = refs[:n], refs[n:2 * n]
        send_sems, recv_sems, pass_send_sems, pass_recv_sems = refs[2 * n:]
        x, y, c = _place()
        mine = 2 * x + y
        chips = _other_chips(x, y)

        def part(a, quarter, core):
            if not halved[a]:
                return outs[a].at[quarter]
            half = ins[a].shape[0] // 2
            return outs[a].at[quarter, pl.ds(core * half, half), :]

        def source(a):
            if not halved[a]:
                return ins[a]
            half = ins[a].shape[0] // 2
            return ins[a].at[pl.ds(c * half, half), :]

        sends = []
        for a in range(n):
            for j, (px, py) in enumerate(chips):
                cp = pltpu.make_async_remote_copy(src_ref=source(a), dst_ref=part(a, mine, c), send_sem=send_sems.at[a, j],
                                                  recv_sem=recv_sems.at[a, j], device_id=(px, py, c), device_id_type=MESH)
                cp.start()
                sends.append(cp)
        for a in range(n):
            for j, (px, py) in enumerate(chips):
                landed = part(a, 2 * px + py, c)
                pltpu.make_async_remote_copy(src_ref=source(a), dst_ref=landed, send_sem=send_sems.at[a, j],
                                             recv_sem=recv_sems.at[a, j], device_id=(px, py, c), device_id_type=MESH).wait_recv()
                if halved[a]:
                    cp = pltpu.make_async_remote_copy(src_ref=landed, dst_ref=landed, send_sem=pass_send_sems.at[a, j],
                                                      recv_sem=pass_recv_sems.at[a, j], device_id=(x, y, 1 - c), device_id_type=MESH)
                    cp.start()
                    sends.append(cp)
        for a in range(n):
            if halved[a]:
                for j, (px, py) in enumerate(chips):
                    other = part(a, 2 * px + py, 1 - c)
                    pltpu.make_async_remote_copy(src_ref=other, dst_ref=other, send_sem=pass_send_sems.at[a, j],
                                                 recv_sem=pass_recv_sems.at[a, j], device_id=(x, y, 1 - c),
                                                 device_id_type=MESH).wait_recv()
        for cp in sends:
            cp.wait_send()

    sems = pltpu.SemaphoreType.DMA((n, 3))
    return pl.pallas_call(
        body, name="gather_weights",
        in_specs=[_ANY] * n, out_specs=[_ANY] * n,
        out_shape=[jax.ShapeDtypeStruct((4,) + s.shape, s.dtype) for s in shards],
        scratch_shapes=[sems, sems, sems, sems],
    )(*shards)


def _swap_halves(grads):
    n = len(grads)

    def body(*refs):
        ins, outs = refs[:n], refs[n:2 * n]
        send_sems, recv_sems = refs[2 * n:]
        x, y, c = _place()
        started = []
        for a in range(n):
            half = ins[a].shape[1] // 2
            cp = pltpu.make_async_remote_copy(src_ref=ins[a].at[:, pl.ds((1 - c) * half, half), :], dst_ref=outs[a],
                                              send_sem=send_sems.at[a], recv_sem=recv_sems.at[a],
                                              device_id=(x, y, 1 - c), device_id_type=MESH)
            cp.start()
            started.append(cp)
        for cp in started:
            cp.wait()

    return pl.pallas_call(
        body, name="swap_halves",
        in_specs=[_ANY] * n, out_specs=[_ANY] * n,
        out_shape=[jax.ShapeDtypeStruct((4, g.shape[1] // 2, g.shape[2]), F32) for g in grads],
        scratch_shapes=[pltpu.SemaphoreType.DMA((n,)), pltpu.SemaphoreType.DMA((n,))],
    )(*grads)


def _scatter_quarters(sums16):
    n = len(sums16)

    def body(*refs):
        ins, outs = refs[:n], refs[n:2 * n]
        send_sems, recv_sems = refs[2 * n:]
        x, y, c = _place()
        sends = []
        for a in range(n):
            for j, (px, py) in enumerate(_other_chips(x, y)):
                cp = pltpu.make_async_remote_copy(src_ref=ins[a].at[2 * px + py], dst_ref=outs[a].at[j],
                                                  send_sem=send_sems.at[a, j], recv_sem=recv_sems.at[a, j],
                                                  device_id=(px, py, c), device_id_type=MESH)
                cp.start()
                sends.append(cp)
        for cp in sends:
            cp.wait()

    return pl.pallas_call(
        body, name="scatter_quarters",
        in_specs=[_ANY] * n, out_specs=[_ANY] * n,
        out_shape=[jax.ShapeDtypeStruct((3,) + s.shape[1:], BF16) for s in sums16],
        scratch_shapes=[pltpu.SemaphoreType.DMA((n, 3)), pltpu.SemaphoreType.DMA((n, 3))],
    )(*sums16)


def _join_halves(fulls):
    n = len(fulls)

    def body(*refs):
        ins, outs = refs[:n], refs[n:2 * n]
        send_sems, recv_sems = refs[2 * n:]
        x, y, c = _place()
        started = []
        for a in range(n):
            half = ins[a].shape[0] // 2
            rows = pl.ds(c * half, half)
            cp = pltpu.make_async_remote_copy(src_ref=ins[a].at[rows, :], dst_ref=outs[a].at[rows, :], send_sem=send_sems.at[a],
                                              recv_sem=recv_sems.at[a], device_id=(x, y, 1 - c), device_id_type=MESH)
            cp.start()
            started.append(cp)
        for cp in started:
            cp.wait()

    return pl.pallas_call(
        body, name="join_halves",
        in_specs=[_ANY] * n, out_specs=[_ANY] * n,
        out_shape=[jax.ShapeDtypeStruct(f.shape, F32) for f in fulls],
        input_output_aliases={a: a for a in range(n)},
        scratch_shapes=[pltpu.SemaphoreType.DMA((n,)), pltpu.SemaphoreType.DMA((n,))],
    )(*fulls)


def _small_allreduce_adamw(g, w, m, v):
    R = g.shape[0]

    def body(g_ref, w_ref, m_ref, v_ref, gs_ref, d_ref, nm_ref, nv_ref, all_s, send_sems, recv_sems):
        x, y, c = _place()
        me = 4 * x + 2 * y + c
        all_s[me] = g_ref[...]
        sends = []
        for k in range(1, 8):
            peer = (x ^ (k >> 2), y ^ ((k >> 1) & 1), c ^ (k & 1))
            cp = pltpu.make_async_remote_copy(src_ref=g_ref, dst_ref=all_s.at[me], send_sem=send_sems.at[k - 1],
                                              recv_sem=recv_sems.at[k - 1], device_id=peer, device_id_type=MESH)
            cp.start()
            sends.append(cp)
        for cp in sends:
            cp.wait()
        total = all_s[0]
        for d in range(1, 8):
            total = total + all_s[d]
        gs_ref[...] = total
        delta, nm, nv = _adamw_math(w_ref[...], total, m_ref[...], v_ref[...])
        d_ref[...] = delta
        nm_ref[...] = nm
        nv_ref[...] = nv

    vm = pl.BlockSpec(memory_space=pltpu.VMEM)
    return pl.pallas_call(
        body, name="small_allreduce_adamw",
        in_specs=[vm] * 4, out_specs=[vm] * 4, out_shape=[jax.ShapeDtypeStruct((R, LANES), F32)] * 4,
        scratch_shapes=[pltpu.VMEM((8, R, LANES), F32), pltpu.SemaphoreType.DMA((7,)), pltpu.SemaphoreType.DMA((7,))],
        compiler_params=pltpu.CompilerParams(vmem_limit_bytes=VMEM_LIMIT),
    )(g, w, m, v)


_SMALL = (("norm_mix_g", D_MODEL), ("f_bias", N_HEADS), ("sg_ln_g", D_HEADS), ("sg_w", N_HEADS * SG_BLOCK * SG_BLOCK),
          ("sg_b", N_HEADS * SG_BLOCK), ("norm_ffn_g", D_MODEL), ("w_conv", 3 * 2 * D_FF), ("b_conv", 2 * D_FF),
          ("norm_final_g", D_MODEL))


def _pack_small(parts):
    rows = []
    for name, size in _SMALL:
        flat = parts[name].reshape(-1).astype(F32)
        pad = (-size) % (SUBLANES * LANES)
        rows.append(jnp.pad(flat, (0, pad)).reshape(-1, LANES))
    return jnp.concatenate(rows, axis=0)


def _unpack_small(packed, shapes):
    out, r = {}, 0
    for name, size in _SMALL:
        nrows = (size + SUBLANES * LANES - 1) // (SUBLANES * LANES) * SUBLANES
        out[name] = packed[r:r + nrows].reshape(-1)[:size].reshape(shapes[name])
        r += nrows
    return out


def _local_step(x, target, g1, w_in, f_bias, sg_ln_g, sg_w, sg_b, w_out, g2, w_up_q, w_conv, b_conv, w_down, g3):
    S = x.shape[0]
    tm = _row_tile(S, 512)
    tms = _row_tile(S, 256)
    tq = _row_tile(S, 512)

    lane = jnp.arange(D_HEADS)
    seg_avg = jnp.where(lane[:, None] // HEAD_DIM == lane[None, :] // HEAD_DIM, 1.0 / HEAD_DIM, 0.0).astype(BF16)
    head_ind = (lane[:, None] // HEAD_DIM == jnp.arange(LANES)[None, :]).astype(BF16)
    pos_chunk = jnp.arange(SG_BLOCK) // CHUNK
    w_mask32 = jnp.where(pos_chunk[:, None] >= pos_chunk[None, :], sg_w, 0.0)
    w_mask = w_mask32.astype(BF16)
    w_mask_t = jnp.swapaxes(w_mask32, 1, 2).astype(BF16)
    ln_row = sg_ln_g.reshape(1, D_HEADS)
    b_full = jnp.repeat(sg_b.T, HEAD_DIM, axis=1)
    bias_row = jnp.pad(f_bias.reshape(1, N_HEADS), ((0, 0), (0, LANES - N_HEADS)))
    b_conv_row = b_conv.reshape(1, 2 * D_FF)

    z, f, h1 = _in_proj(x, g1, w_in, tm)
    c, ct = _fox_prep(f, bias_row, _row_tile(S, 256))
    consts = _attn_consts()
    qa, ka, va = _attn_pack(z, c, consts, tm)
    out_b, lse = _attn_fwd(qa, ka, va, consts["place_t"], tq)
    out_a = _gate_fwd(z, w_mask, ln_row, b_full, seg_avg, tm)
    x1, h2 = _mix_out(x, out_a, out_b, w_out, g2, tm)
    a = _up_proj(h2, w_up_q, tm)
    dx2, sq_err, dg3 = _ffn_fwd_loss(a, w_conv, b_conv_row, w_down, x1, g3, target, tms)

    dconv, y, dw_conv8, db_conv = _ffn_bwd_gate(dx2, a, w_conv, b_conv_row, w_down, tms)
    dact = _conv_bwd(dconv, w_conv, tm, 2 * D_FF // 4)
    dw_down = _matmul_tn(y, dx2, "dw_down", D_FF // 2, D_MODEL, tm, quarters=(2, 1))
    dx1, dg2 = _up_bwd(dact, w_up_q, x1, g2, dx2, tms)
    dw_up_q = _matmul_tn(h2, dact, "dw_up", D_MODEL, 2 * D_FF // 4, tm, quarters=(1, 4))
    dcat = _out_bwd(dx1, w_out, tm)
    dw_out_a = _matmul_tn(out_a, dx1, "dw_out_a", D_HEADS, D_MODEL, tm)
    dw_out_b = _matmul_tn(out_b, dx1, "dw_out_b", D_HEADS, D_MODEL, tm)
    dzu, dzv, dsg_w, dsg_b_t, dln = _gate_bwd(z, dcat, w_mask, w_mask_t, ln_row, b_full, seg_avg, head_ind, tm)
    dop, qb = _attn_pack_grad(out_b, dcat, qa, lse, head_ind, consts, tm)
    dqp, dk, dv, dcc = _attn_bwd(qb, ka, va, dop, consts, tq)
    dq, dc = _attn_unpack(dqp, dcc, consts, tm)
    df, dbias = _fox_bwd(dc, f, bias_row, _row_tile(S, 256))
    pieces = (dzu, dzv, dq, dk, dv, df)
    dx, dg1 = _in_bwd(pieces, w_in, x, g1, dx1, tms)
    dw_in = _dw_in(h1, pieces, tm)

    grads = {
        "norm_mix_g": dg1, "f_bias": dbias[:, :N_HEADS], "sg_ln_g": dln, "sg_w": dsg_w, "sg_b": dsg_b_t[:, :N_HEADS].T,
        "norm_ffn_g": dg2, "w_conv": dw_conv8[:3], "b_conv": db_conv, "norm_final_g": dg3,
        "w_in": dw_in, "w_out": jnp.concatenate([dw_out_a, dw_out_b], axis=0), "w_up_q": dw_up_q, "w_down": dw_down,
    }
    return sq_err, dx, grads


def _reduce_big(grads_q, ids):
    names = list(grads_q)
    theirs = _swap_halves([grads_q[k] for k in names])
    sums = [_pair_sum("pair_sum_" + k, grads_q[k], t, ids) for k, t in zip(names, theirs)]
    got = _scatter_quarters([s16 for _, s16 in sums])
    fulls = [_chip_sum("chip_sum_" + k, s32, g, ids) for k, (s32, _), g in zip(names, sums, got)]
    return dict(zip(names, _join_halves(fulls)))


def kernel(x, norm_mix_g, w_in, f_bias, sg_ln_g, sg_w, sg_b, w_out, norm_ffn_g, w_up, w_conv, b_conv, w_down, norm_final_g, loss_target, m_norm_mix_g, m_w_in, m_f_bias, m_sg_ln_g, m_sg_w, m_sg_b, m_w_out, m_norm_ffn_g, m_w_up, m_w_conv, m_b_conv, m_w_down, m_norm_final_g, v_norm_mix_g, v_w_in, v_f_bias, v_sg_ln_g, v_sg_w, v_sg_b, v_w_out, v_norm_ffn_g, v_w_up, v_w_conv, v_b_conv, v_w_down, v_norm_final_g):
    args = dict(locals())
    quarter = 2 * lax.axis_index("x") + lax.axis_index("y")
    ids = jnp.stack([quarter, lax.axis_index("c")]).astype(jnp.int32)
    wq_conv = w_conv.shape[-1]

    shards = [w_in[0].astype(BF16), w_out[0].astype(BF16), w_up[0].astype(BF16), w_down[0].astype(BF16), w_conv[0]]
    g_in, g_out, g_up, g_down, g_conv = [lax.dynamic_update_index_in_dim(g, s, quarter, 0)
                                         for g, s in zip(_gather_quarters(shards), shards)]
    w_in_full = jnp.pad(jnp.concatenate([g_in[q] for q in range(4)], axis=1), ((0, 0), (0, D_IN_PAD - D_IN)))
    w_out_full = g_out.reshape(D_MODEL, D_MODEL)
    w_down_full = g_down.reshape(D_FF, D_MODEL)
    w_conv_full = jnp.concatenate([g_conv[q] for q in range(4)], axis=1)

    sq_err, dx, grads = _local_step(
        x[0], loss_target[0], norm_mix_g, w_in_full, f_bias[0], sg_ln_g[0], sg_w[0], sg_b[0], w_out_full, norm_ffn_g,
        g_up, w_conv_full, b_conv[0], w_down_full, norm_final_g.reshape(1, D_MODEL))
    loss = lax.psum(0.5 * jnp.sum(sq_err) / D_MODEL, ("x", "y", "c"))

    dw_in = grads["w_in"][:, :D_IN].reshape(D_MODEL, 4, D_IN // 4).transpose(1, 0, 2)
    big = _reduce_big({"w_in": dw_in, "w_out": grads["w_out"].reshape(4, D_MODEL // 4, D_MODEL),
                       "w_up": grads["w_up_q"], "w_down": grads["w_down"].reshape(4, D_FF // 4, D_MODEL)}, ids)

    out = {"loss": loss, "grad_x": dx[None]}
    for k in ("w_in", "w_out", "w_up", "w_down"):
        g = big[k]
        d, nm, nv = _adamw("adamw_" + k, args[k][0], g, args["m_" + k][0], args["v_" + k][0])
        out["grad_" + k], out["delta_" + k], out["new_m_" + k], out["new_v_" + k] = g[None], d[None], nm[None], nv[None]

    def padded_conv(t):
        return lax.dynamic_update_slice(jnp.zeros((3, 4 * wq_conv), F32), t[0], (0, quarter * wq_conv))

    small_names = [n for n, _ in _SMALL]
    shapes = {n: (3, 4 * wq_conv) if n == "w_conv" else args[n].shape for n in small_names}
    pack = lambda prefix: _pack_small({n: padded_conv(args[prefix + n]) if n == "w_conv" else args[prefix + n] for n in small_names})
    packed = _small_allreduce_adamw(_pack_small({n: grads[n] for n in small_names}), pack(""), pack("m_"), pack("v_"))
    for prefix, arr in zip(("grad_", "delta_", "new_m_", "new_v_"), packed):
        for n, t in _unpack_small(arr, shapes).items():
            if n == "w_conv":
                t = lax.dynamic_slice(t, (0, quarter * wq_conv), (3, wq_conv))[None]
            out[prefix + n] = t

    weights = ["norm_mix_g", "w_in", "f_bias", "sg_ln_g", "sg_w", "sg_b", "w_out", "norm_ffn_g", "w_up", "w_conv", "b_conv",
               "w_down", "norm_final_g"]
    return (out["loss"], out["grad_x"], *[out[p + n] for p in ("grad_", "delta_", "new_m_", "new_v_") for n in weights])
```

```python
import functools
import math

import jax
import jax.numpy as jnp
from jax import lax
from jax.experimental import pallas as pl
from jax.experimental.pallas import tpu as pltpu

F32 = jnp.float32
BF16 = jnp.bfloat16
MESH = pl.DeviceIdType.MESH

D_MODEL = 1024
N_HEADS = 8
HEAD_DIM = 64
D_HEADS = N_HEADS * HEAD_DIM
SG_BLOCK = 128
CHUNK = 64
D_FF = 2816
D_IN = 2 * D_HEADS + 3 * D_HEADS + N_HEADS
LANES = 128
SUBLANES = 8
D_IN_PAD = 5 * D_HEADS + LANES
EPS = 1e-6
SCALE = HEAD_DIM ** -0.5
NEG = -1e30
LOG2E = 1.4426950408889634
HEAD_PAD = LANES
D_PAD = N_HEADS * HEAD_PAD
Q_STAT = HEAD_DIM
K_STAT = HEAD_DIM + 3
L_STAT = HEAD_DIM + 6
GROUPS = 2
GROUP_HEADS = N_HEADS // GROUPS
GROUP_PAD = GROUP_HEADS * HEAD_PAD
KEY_CHUNK = 256

ADAM_LR = 0.001
ADAM_B1 = 0.9
ADAM_B2 = 0.999
ADAM_EPS = 1e-08
ADAM_WD = 0.01
ADAM_STEP = 10

VMEM_LIMIT = 56 * 1024 * 1024

NT = (((1,), (1,)), ((), ()))
TN = (((0,), (0,)), ((), ()))


def _params(sem):
    return pltpu.CompilerParams(dimension_semantics=sem, vmem_limit_bytes=VMEM_LIMIT)


def _full(shape):
    nd = len(shape)
    return pl.BlockSpec(shape, lambda *_: (0,) * nd)


def _row_tile(rows, target):
    best = None
    for t in range(SUBLANES, min(rows, target) + 1, SUBLANES):
        if rows % t == 0:
            best = t
    assert best is not None, rows
    return best


def _gelu(z):
    return 0.5 * z * (1.0 + lax.erf(z * (2.0 ** -0.5)))


def _gelu_grad(z):
    cdf = 0.5 * (1.0 + lax.erf(z * (2.0 ** -0.5)))
    pdf = jnp.exp(-0.5 * z * z) * (1.0 / math.sqrt(2.0 * math.pi))
    return cdf + z * pdf


def _split_dot(x, m):
    hi = x.astype(BF16)
    lo = (x - hi.astype(F32)).astype(BF16)
    return jnp.dot(hi, m, preferred_element_type=F32) + jnp.dot(lo, m, preferred_element_type=F32)


def _head_mask(h, rows):
    lane = lax.broadcasted_iota(jnp.int32, (rows, D_HEADS), 1)
    return (lane >= h * HEAD_DIM) & (lane < (h + 1) * HEAD_DIM)


def _rms_bwd(dh, x, g):
    r = lax.rsqrt(jnp.mean(x * x, axis=-1, keepdims=True) + EPS)
    xhat = x * r
    dg = jnp.sum(dh * xhat, axis=0, keepdims=True)
    dxhat = dh * g
    dx = r * (dxhat - xhat * jnp.mean(dxhat * xhat, axis=-1, keepdims=True))
    return dx, dg


def _in_proj(x, g1, w_in, tm):
    S = x.shape[0]
    nz = D_IN_PAD - LANES

    def body(x_ref, g_ref, w_ref, z_ref, f_ref, h_ref):
        xf = x_ref[...]
        r = lax.rsqrt(jnp.mean(xf * xf, axis=-1, keepdims=True) + EPS)
        h = (xf * r * g_ref[...]).astype(BF16)
        h_ref[...] = h
        zz = jnp.dot(h, w_ref[...], preferred_element_type=F32)
        z_ref[...] = zz[:, :nz].astype(BF16)
        f_ref[...] = zz[:, nz:]

    return pl.pallas_call(
        body, name="in_proj", grid=(S // tm,),
        in_specs=[pl.BlockSpec((tm, D_MODEL), lambda i: (i, 0)), _full((1, D_MODEL)), _full((D_MODEL, D_IN_PAD))],
        out_specs=[pl.BlockSpec((tm, nz), lambda i: (i, 0)), pl.BlockSpec((tm, LANES), lambda i: (i, 0)),
                   pl.BlockSpec((tm, D_MODEL), lambda i: (i, 0))],
        out_shape=[jax.ShapeDtypeStruct((S, nz), BF16), jax.ShapeDtypeStruct((S, LANES), F32),
                   jax.ShapeDtypeStruct((S, D_MODEL), BF16)],
        compiler_params=_params(("parallel",)),
    )(x, g1, w_in)


def _fox_prep(f, bias_row, tb):
    S = f.shape[0]

    def body(f_ref, b_ref, c_ref, ct_ref, carry):
        @pl.when(pl.program_id(0) == 0)
        def _():
            carry[...] = jnp.zeros_like(carry)

        xv = f_ref[...] + b_ref[...]
        lf = jnp.minimum(xv, 0.0) - jnp.log(1.0 + jnp.exp(-jnp.abs(xv)))
        r = lax.broadcasted_iota(jnp.int32, (tb, tb), 0)
        s = lax.broadcasted_iota(jnp.int32, (tb, tb), 1)
        tri = (r >= s).astype(F32)
        cs = jnp.dot(tri, lf, precision=lax.Precision.HIGHEST, preferred_element_type=F32) + carry[0:1, :]
        c_ref[...] = cs
        ct_ref[...] = cs.T[:N_HEADS, :]
        carry[...] = jnp.broadcast_to(cs[tb - 1:tb, :], carry.shape)

    return pl.pallas_call(
        body, name="fox_prep", grid=(S // tb,),
        in_specs=[pl.BlockSpec((tb, LANES), lambda i: (i, 0)), _full((1, LANES))],
        out_specs=[pl.BlockSpec((tb, LANES), lambda i: (i, 0)), pl.BlockSpec((N_HEADS, tb), lambda i: (0, i))],
        out_shape=[jax.ShapeDtypeStruct((S, LANES), F32), jax.ShapeDtypeStruct((N_HEADS, S), F32)],
        scratch_shapes=[pltpu.VMEM((SUBLANES, LANES), F32)],
        compiler_params=_params(("arbitrary",)),
    )(f, bias_row)


def _attn_consts():
    col = jnp.arange(D_PAD)
    row = jnp.arange(D_HEADS)
    head = jnp.arange(LANES)
    place = (row[:, None] // HEAD_DIM == col[None, :] // HEAD_PAD) & (row[:, None] % HEAD_DIM == col[None, :] % HEAD_PAD)

    def stat(offset):
        return ((head[:, None] < N_HEADS) & (col[None, :] == head[:, None] * HEAD_PAD + offset)).astype(BF16)

    def ones(offsets):
        return sum((col % HEAD_PAD == o) for o in offsets).astype(F32).reshape(1, D_PAD)

    def pick(offset):
        gcol = jnp.arange(GROUP_PAD)
        return jnp.stack([((gcol[:, None] % HEAD_PAD == offset) & (head[None, :] == g * GROUP_HEADS + gcol[:, None] // HEAD_PAD))
                          for g in range(GROUPS)]).astype(BF16)

    place = place.astype(BF16)
    return {
        "place": place, "place_t": place.T, "place_t_group": place.T[:GROUP_PAD, :GROUP_HEADS * HEAD_DIM],
        "q_stat": jnp.stack([stat(Q_STAT + j) for j in range(3)]), "k_stat": jnp.stack([stat(K_STAT + j) for j in range(3)]),
        "d_stat": jnp.stack([stat(Q_STAT + j) for j in range(2)]), "l_stat": jnp.stack([stat(L_STAT + j)[:N_HEADS] for j in range(3)]).astype(F32),
        "q_ones": ones(range(K_STAT, K_STAT + 3)), "k_ones": ones(list(range(Q_STAT, Q_STAT + 3)) + list(range(L_STAT, L_STAT + 3))),
        "v_ones": ones(range(Q_STAT, Q_STAT + 2)),
        "pick_rows": pick(Q_STAT), "pick_cols": pick(K_STAT),
    }


def _split3(x):
    hi = x.astype(BF16)
    r = x - hi.astype(F32)
    mid = r.astype(BF16)
    return hi, mid, (r - mid.astype(F32)).astype(BF16)


def _split3_dot(x, m):
    return sum(jnp.dot(part, m, preferred_element_type=F32) for part in _split3(x))


def _attn_pack(z, c, k, tm):
    S = z.shape[0]

    def body(q_ref, k_ref, v_ref, c_ref, pl_ref, pt_ref, qs_ref, ks_ref, qo_ref, ko_ref, vo_ref, voc_ref,
             qa_ref, ka_ref, va_ref, vt_ref):
        place = pl_ref[...]
        q = (q_ref[...].astype(F32) * (SCALE * LOG2E)).astype(BF16)
        qa = jnp.dot(q, place, preferred_element_type=F32) + qo_ref[...]
        ka = jnp.dot(k_ref[...], place, preferred_element_type=F32) + ko_ref[...]
        for j, part in enumerate(_split3(c_ref[...] * LOG2E)):
            qa = qa + jnp.dot(part, qs_ref[j], preferred_element_type=F32)
            ka = ka - jnp.dot(part, ks_ref[j], preferred_element_type=F32)
        qa_ref[...] = qa.astype(BF16)
        ka_ref[...] = ka.astype(BF16)
        v = v_ref[...]
        va_ref[...] = (jnp.dot(v, place, preferred_element_type=F32) + vo_ref[...]).astype(BF16)
        vt_ref[...] = (lax.dot_general(pt_ref[...], v, NT, preferred_element_type=F32) + voc_ref[...]).astype(BF16)

    blk = lambda col: pl.BlockSpec((tm, D_HEADS), lambda i: (i, col))
    out = pl.BlockSpec((tm, D_PAD), lambda i: (i, 0))
    pad = jax.ShapeDtypeStruct((S, D_PAD), BF16)
    return pl.pallas_call(
        body, name="attn_pack", grid=(S // tm,),
        in_specs=[blk(2), blk(3), blk(4), pl.BlockSpec((tm, LANES), lambda i: (i, 0)), _full((D_HEADS, D_PAD)), _full((D_PAD, D_HEADS)),
                  _full((3, LANES, D_PAD)), _full((3, LANES, D_PAD)), _full((1, D_PAD)), _full((1, D_PAD)), _full((1, D_PAD)),
                  _full((D_PAD, 1))],
        out_specs=[out, out, out, pl.BlockSpec((D_PAD, tm), lambda i: (0, i))],
        out_shape=[pad, pad, pad, jax.ShapeDtypeStruct((D_PAD, S), BF16)],
        compiler_params=_params(("parallel",)),
    )(z, z, z, c, k["place"], k["place_t"], k["q_stat"], k["k_stat"], k["q_ones"], k["k_ones"], k["v_ones"], k["v_ones"].T)


def _attn_fwd(qa, ka, vat, place_t, tq):
    S = qa.shape[0]
    n = S // tq

    def body(q_ref, k_ref, vt_ref, pt_ref, o_ref, lse_ref, m_s, acc_s, ot_s, s_s):
        qi, ki = pl.program_id(0), pl.program_id(1)

        @pl.when(ki == 0)
        def _():
            m_s[...] = jnp.full_like(m_s, NEG)
            acc_s[...] = jnp.zeros_like(acc_s)

        def step(diagonal):
            chunks = [slice(c * KEY_CHUNK, (c + 1) * KEY_CHUNK) for c in range(tq // KEY_CHUNK)]

            def scores(h, rows, slot):
                sl = slice(h * HEAD_PAD, (h + 1) * HEAD_PAD)
                st = lax.dot_general(k_ref[rows, sl], q_ref[:, sl], NT, preferred_element_type=F32)
                if diagonal:
                    key = rows.start + lax.broadcasted_iota(jnp.int32, (KEY_CHUNK, tq), 0)
                    query = lax.broadcasted_iota(jnp.int32, (KEY_CHUNK, tq), 1)
                    st = jnp.where(query >= key, st, NEG)
                s_s[slot, rows, :] = st
                return jnp.max(st, axis=0, keepdims=True)

            m_cur = functools.reduce(jnp.maximum, [scores(0, rows, 0) for rows in chunks])
            for h in range(N_HEADS):
                sl = slice(h * HEAD_PAD, (h + 1) * HEAD_PAD)
                slot = h % 2
                m_prev = m_s[h][0:1, :]
                m_new = jnp.maximum(m_prev, m_cur)
                acc = jnp.exp2(m_prev - m_new) * acc_s[h]
                m_next = []
                for rows in chunks:
                    if h + 1 < N_HEADS:
                        m_next.append(scores(h + 1, rows, 1 - slot))
                    pt = jnp.exp2(s_s[slot, rows, :] - m_new).astype(BF16)
                    acc = acc + jnp.dot(vt_ref[sl, rows], pt, preferred_element_type=F32)
                acc_s[h] = acc
                m_s[h] = jnp.broadcast_to(m_new, (SUBLANES, tq))
                if m_next:
                    m_cur = functools.reduce(jnp.maximum, m_next)

        @pl.when(ki < qi)
        def _():
            step(False)

        @pl.when(ki == qi)
        def _():
            step(True)
            for h in range(N_HEADS):
                acc = acc_s[h]
                denom = acc[Q_STAT:Q_STAT + 1, :]
                ot_s[h * HEAD_PAD:(h + 1) * HEAD_PAD, :] = (acc / denom).astype(BF16)
                lse_ref[h:h + 1, :] = m_s[h][0:1, :] + jnp.log(denom) * LOG2E
            o_ref[...] = lax.dot_general(ot_s[...], pt_ref[...], TN, preferred_element_type=F32).astype(BF16)

    kmin = lambda qi, ki: jnp.minimum(ki, qi)
    return pl.pallas_call(
        body, name="attn_fwd", grid=(n, n),
        in_specs=[pl.BlockSpec((tq, D_PAD), lambda qi, ki: (qi, 0)), pl.BlockSpec((tq, D_PAD), lambda qi, ki: (kmin(qi, ki), 0)),
                  pl.BlockSpec((D_PAD, tq), lambda qi, ki: (0, kmin(qi, ki))), _full((D_PAD, D_HEADS))],
        out_specs=[pl.BlockSpec((tq, D_HEADS), lambda qi, ki: (qi, 0)), pl.BlockSpec((N_HEADS, tq), lambda qi, ki: (0, qi))],
        out_shape=[jax.ShapeDtypeStruct((S, D_HEADS), BF16), jax.ShapeDtypeStruct((N_HEADS, S), F32)],
        scratch_shapes=[pltpu.VMEM((N_HEADS, SUBLANES, tq), F32), pltpu.VMEM((N_HEADS, HEAD_PAD, tq), F32),
                        pltpu.VMEM((D_PAD, tq), BF16), pltpu.VMEM((2, tq, tq), F32)],
        compiler_params=_params(("parallel", "arbitrary")),
    )(qa, ka, vat, place_t)


def _layer_norm_heads(v, seg_avg):
    mu = _split_dot(v, seg_avg)
    d = v - mu
    var = _split_dot(d * d, seg_avg)
    rstd = lax.rsqrt(var + EPS)
    return d * rstd, rstd


def _gate_mix(vn_blk, w_ref, bias):
    acc = bias
    for h in range(N_HEADS):
        vh = jnp.where(_head_mask(h, SG_BLOCK), vn_blk, 0.0).astype(BF16)
        acc = acc + jnp.dot(w_ref[h], vh, preferred_element_type=F32)
    return acc


def _gate_fwd(z, w_mask, ln_row, b_full, seg_avg, tm):
    S = z.shape[0]

    def body(zu_ref, zv_ref, w_ref, ln_ref, b_ref, avg_ref, o_ref):
        u = _gelu(zu_ref[...].astype(F32))
        v = _gelu(zv_ref[...].astype(F32))
        vhat, _ = _layer_norm_heads(v, avg_ref[...])
        vn = vhat * ln_ref[...]
        for b in range(tm // SG_BLOCK):
            rows = slice(b * SG_BLOCK, (b + 1) * SG_BLOCK)
            mixed = _gate_mix(vn[rows], w_ref, b_ref[...])
            o_ref[rows, :] = (u[rows] * mixed).astype(BF16)

    return pl.pallas_call(
        body, name="gate_fwd", grid=(S // tm,),
        in_specs=[pl.BlockSpec((tm, D_HEADS), lambda i: (i, 0)), pl.BlockSpec((tm, D_HEADS), lambda i: (i, 1)),
                  _full((N_HEADS, SG_BLOCK, SG_BLOCK)), _full((1, D_HEADS)), _full((SG_BLOCK, D_HEADS)),
                  _full((D_HEADS, D_HEADS))],
        out_specs=pl.BlockSpec((tm, D_HEADS), lambda i: (i, 0)),
        out_shape=jax.ShapeDtypeStruct((S, D_HEADS), BF16),
        compiler_params=_params(("parallel",)),
    )(z, z, w_mask, ln_row, b_full, seg_avg)


def _mix_out(x, out_a, out_b, w_out, g2, tm):
    S = x.shape[0]

    def body(x_ref, a_ref, b_ref, w_ref, g_ref, x1_ref, h_ref):
        y = jnp.dot(a_ref[...], w_ref[:D_HEADS, :], preferred_element_type=F32)
        y = y + jnp.dot(b_ref[...], w_ref[D_HEADS:, :], preferred_element_type=F32)
        x1 = x_ref[...] + y
        x1_ref[...] = x1
        r = lax.rsqrt(jnp.mean(x1 * x1, axis=-1, keepdims=True) + EPS)
        h_ref[...] = (x1 * r * g_ref[...]).astype(BF16)

    row = lambda w: pl.BlockSpec((tm, w), lambda i: (i, 0))
    return pl.pallas_call(
        body, name="mix_out", grid=(S // tm,),
        in_specs=[row(D_MODEL), row(D_HEADS), row(D_HEADS), _full((D_MODEL, D_MODEL)), _full((1, D_MODEL))],
        out_specs=[row(D_MODEL), row(D_MODEL)],
        out_shape=[jax.ShapeDtypeStruct((S, D_MODEL), F32), jax.ShapeDtypeStruct((S, D_MODEL), BF16)],
        compiler_params=_params(("parallel",)),
    )(x, out_a, out_b, w_out, g2)


def _up_proj(h2, w_up_q, tm):
    S = h2.shape[0]
    nq, _, wq = w_up_q.shape

    def body(h_ref, w_ref, a_ref):
        a_ref[...] = jnp.dot(h_ref[...], w_ref[...], preferred_element_type=F32).astype(BF16)

    return pl.pallas_call(
        body, name="up_proj", grid=(nq, S // tm),
        in_specs=[pl.BlockSpec((tm, D_MODEL), lambda j, i: (i, 0)), pl.BlockSpec((None, D_MODEL, wq), lambda j, i: (j, 0, 0))],
        out_specs=pl.BlockSpec((tm, wq), lambda j, i: (i, j)),
        out_shape=jax.ShapeDtypeStruct((S, nq * wq), BF16),
        compiler_params=_params(("parallel", "parallel")),
    )(h2, w_up_q)


def _shift_down(a, halo, k):
    tm = a.shape[0]
    ra = pltpu.roll(a, k, 0)
    rh = pltpu.roll(halo, k, 0)
    row = lax.broadcasted_iota(jnp.int32, halo.shape, 0)
    top = jnp.where(row < k, rh, ra[0:SUBLANES])
    return jnp.concatenate([top, ra[SUBLANES:tm]], axis=0)


def _shift_up(a, halo, k):
    tm = a.shape[0]
    ra = pltpu.roll(a, tm - k, 0)
    rh = pltpu.roll(halo, SUBLANES - k, 0)
    row = lax.broadcasted_iota(jnp.int32, halo.shape, 0)
    bottom = jnp.where(row >= SUBLANES - k, rh, ra[tm - SUBLANES:tm])
    return jnp.concatenate([ra[0:tm - SUBLANES], bottom], axis=0)


def _conv_taps(a_ref, halo_ref, first):
    a = a_ref[...].astype(F32)
    halo = halo_ref[...].astype(F32) * jnp.where(first, 0.0, 1.0)
    return a, _shift_down(a, halo, 1), _shift_down(a, halo, 2)


def _conv_specs(tm):
    step = tm // SUBLANES
    prev = lambda i: jnp.maximum(i * step - 1, 0)
    return [pl.BlockSpec((tm, D_FF), lambda i: (i, 0)), pl.BlockSpec((tm, D_FF), lambda i: (i, 1)),
            pl.BlockSpec((SUBLANES, D_FF), lambda i: (prev(i), 0)), pl.BlockSpec((SUBLANES, D_FF), lambda i: (prev(i), 1))]


def _ffn_fwd_loss(a, w_conv, b_conv, w_down, x1, g3, target, tm):
    S = x1.shape[0]

    def body(ag_ref, av_ref, hg_ref, hv_ref, wg_ref, wv_ref, bg_ref, bv_ref, wd_ref, x1_ref, g_ref, t_ref,
             dx2_ref, loss_ref, dg_ref):
        i = pl.program_id(0)

        @pl.when(i == 0)
        def _():
            loss_ref[...] = jnp.zeros_like(loss_ref)
            dg_ref[...] = jnp.zeros_like(dg_ref)

        g0, g1, g2 = _conv_taps(ag_ref, hg_ref, i == 0)
        gate = wg_ref[2:3, :] * g0 + wg_ref[1:2, :] * g1 + wg_ref[0:1, :] * g2 + bg_ref[...]
        v0, v1, v2 = _conv_taps(av_ref, hv_ref, i == 0)
        val = wv_ref[2:3, :] * v0 + wv_ref[1:2, :] * v1 + wv_ref[0:1, :] * v2 + bv_ref[...]
        y = (gate * jax.nn.sigmoid(gate) * val).astype(BF16)
        x2 = x1_ref[...] + jnp.dot(y, wd_ref[...], preferred_element_type=F32)
        r = lax.rsqrt(jnp.mean(x2 * x2, axis=-1, keepdims=True) + EPS)
        xhat = x2 * r
        gg = g_ref[...]
        err = xhat * gg - t_ref[...]
        loss_ref[...] += jnp.sum(err * err, axis=0, keepdims=True)
        dy = err * (1.0 / D_MODEL)
        dg_ref[...] += jnp.sum(dy * xhat, axis=0, keepdims=True)
        dxhat = dy * gg
        dx2_ref[...] = r * (dxhat - xhat * jnp.mean(dxhat * xhat, axis=-1, keepdims=True))

    row = lambda w: pl.BlockSpec((tm, w), lambda i: (i, 0))
    half = lambda r: [pl.BlockSpec((r, D_FF), lambda i: (0, 0)), pl.BlockSpec((r, D_FF), lambda i: (0, 1))]
    return pl.pallas_call(
        body, name="ffn_fwd_loss", grid=(S // tm,),
        in_specs=_conv_specs(tm) + half(3) + half(1) + [_full((D_FF, D_MODEL)), row(D_MODEL), _full((1, D_MODEL)), row(D_MODEL)],
        out_specs=[row(D_MODEL), _full((1, D_MODEL)), _full((1, D_MODEL))],
        out_shape=[jax.ShapeDtypeStruct((S, D_MODEL), F32), jax.ShapeDtypeStruct((1, D_MODEL), F32),
                   jax.ShapeDtypeStruct((1, D_MODEL), F32)],
        compiler_params=_params(("arbitrary",)),
    )(a, a, a, a, w_conv, w_conv, b_conv, b_conv, w_down, x1, g3, target)


def _ffn_bwd_gate(dx2, a, w_conv, b_conv, w_down, tm):
    S = dx2.shape[0]

    def body(dx_ref, ag_ref, av_ref, hg_ref, hv_ref, wg_ref, wv_ref, bg_ref, bv_ref, wd_ref,
             dc_ref, y_ref, dw_ref, db_ref):
        i = pl.program_id(0)

        @pl.when(i == 0)
        def _():
            dw_ref[...] = jnp.zeros_like(dw_ref)
            db_ref[...] = jnp.zeros_like(db_ref)

        g0, g1, g2 = _conv_taps(ag_ref, hg_ref, i == 0)
        gate = wg_ref[2:3, :] * g0 + wg_ref[1:2, :] * g1 + wg_ref[0:1, :] * g2 + bg_ref[...]
        v0, v1, v2 = _conv_taps(av_ref, hv_ref, i == 0)
        val = wv_ref[2:3, :] * v0 + wv_ref[1:2, :] * v1 + wv_ref[0:1, :] * v2 + bv_ref[...]
        sg = jax.nn.sigmoid(gate)
        act = gate * sg
        y_ref[...] = (act * val).astype(BF16)
        dy = lax.dot_general(dx_ref[...].astype(BF16), wd_ref[...], NT, preferred_element_type=F32)
        dgate = dy * val * (sg * (1.0 + gate * (1.0 - sg)))
        dval = dy * act
        dc_ref[:, :D_FF] = dgate.astype(BF16)
        dc_ref[:, D_FF:] = dval.astype(BF16)
        for half, (d, taps) in enumerate(((dgate, (g2, g1, g0)), (dval, (v2, v1, v0)))):
            cols = slice(half * D_FF, (half + 1) * D_FF)
            db_ref[0:1, cols] += jnp.sum(d, axis=0, keepdims=True)
            for j in range(3):
                dw_ref[j:j + 1, cols] += jnp.sum(d * taps[j], axis=0, keepdims=True)

    row = lambda w: pl.BlockSpec((tm, w), lambda i: (i, 0))
    half = lambda r: [pl.BlockSpec((r, D_FF), lambda i: (0, 0)), pl.BlockSpec((r, D_FF), lambda i: (0, 1))]
    return pl.pallas_call(
        body, name="ffn_bwd_gate", grid=(S // tm,),
        in_specs=[row(D_MODEL)] + _conv_specs(tm) + half(3) + half(1) + [_full((D_FF, D_MODEL))],
        out_specs=[row(2 * D_FF), row(D_FF), _full((SUBLANES, 2 * D_FF)), _full((1, 2 * D_FF))],
        out_shape=[jax.ShapeDtypeStruct((S, 2 * D_FF), BF16), jax.ShapeDtypeStruct((S, D_FF), BF16),
                   jax.ShapeDtypeStruct((SUBLANES, 2 * D_FF), F32), jax.ShapeDtypeStruct((1, 2 * D_FF), F32)],
        compiler_params=_params(("arbitrary",)),
    )(dx2, a, a, a, a, w_conv, w_conv, b_conv, b_conv, w_down)


def _conv_bwd(dc, w_conv, tm, tn):
    S, C = dc.shape
    step = tm // SUBLANES
    last_blk = S // SUBLANES - 1

    def body(d_ref, nx_ref, w_ref, o_ref):
        last = pl.program_id(0) == pl.num_programs(0) - 1
        d = d_ref[...].astype(F32)
        nx = nx_ref[...].astype(F32) * jnp.where(last, 0.0, 1.0)
        out = w_ref[2:3, :] * d + w_ref[1:2, :] * _shift_up(d, nx, 1) + w_ref[0:1, :] * _shift_up(d, nx, 2)
        o_ref[...] = out.astype(BF16)

    return pl.pallas_call(
        body, name="conv_bwd", grid=(S // tm, C // tn),
        in_specs=[pl.BlockSpec((tm, tn), lambda i, j: (i, j)),
                  pl.BlockSpec((SUBLANES, tn), lambda i, j: (jnp.minimum((i + 1) * step, last_blk), j)),
                  pl.BlockSpec((3, tn), lambda i, j: (0, j))],
        out_specs=pl.BlockSpec((tm, tn), lambda i, j: (i, j)),
        out_shape=jax.ShapeDtypeStruct((S, C), BF16),
        compiler_params=_params(("parallel", "parallel")),
    )(dc, dc, w_conv)


def _matmul_tn(a, b, name, bm, bn, tk, col_a=0, col_b=0, quarters=None):
    S = a.shape[0]
    gm, gn = quarters if quarters else (1, 1)
    nk = S // tk

    def body(a_ref, b_ref, o_ref):
        @pl.when(pl.program_id(2) == 0)
        def _():
            o_ref[...] = jnp.zeros_like(o_ref)

        o_ref[...] += lax.dot_general(a_ref[...].astype(BF16), b_ref[...].astype(BF16), TN, preferred_element_type=F32)

    if quarters and gn > 1:
        out_spec = pl.BlockSpec((None, bm, bn), lambda i, j, k: (j, i, 0))
        out_shape = jax.ShapeDtypeStruct((gn, gm * bm, bn), F32)
    else:
        out_spec = pl.BlockSpec((bm, bn), lambda i, j, k: (i, j))
        out_shape = jax.ShapeDtypeStruct((gm * bm, gn * bn), F32)
    return pl.pallas_call(
        body, name=name, grid=(gm, gn, nk),
        in_specs=[pl.BlockSpec((tk, bm), lambda i, j, k: (k, col_a * gm + i)),
                  pl.BlockSpec((tk, bn), lambda i, j, k: (k, col_b * gn + j))],
        out_specs=out_spec, out_shape=out_shape,
        compiler_params=_params(("parallel", "parallel", "arbitrary")),
    )(a, b)


def _up_bwd(dact, w_up_q, x1, g2, dx2, tm):
    S = x1.shape[0]
    nq, _, wq = w_up_q.shape

    def body(d_ref, w_ref, x_ref, g_ref, dx2_ref, dx1_ref, dg_ref):
        @pl.when(pl.program_id(0) == 0)
        def _():
            dg_ref[...] = jnp.zeros_like(dg_ref)

        dh = jnp.zeros((tm, D_MODEL), F32)
        for j in range(nq):
            dh = dh + lax.dot_general(d_ref[:, j * wq:(j + 1) * wq], w_ref[j], NT, preferred_element_type=F32)
        dx, dg = _rms_bwd(dh, x_ref[...], g_ref[...])
        dg_ref[...] += dg
        dx1_ref[...] = dx2_ref[...] + dx

    row = lambda w: pl.BlockSpec((tm, w), lambda i: (i, 0))
    return pl.pallas_call(
        body, name="up_bwd", grid=(S // tm,),
        in_specs=[row(nq * wq), pl.BlockSpec((nq, D_MODEL, wq), lambda i: (0, 0, 0), pipeline_mode=pl.Buffered(1)),
                  row(D_MODEL), _full((1, D_MODEL)), row(D_MODEL)],
        out_specs=[row(D_MODEL), _full((1, D_MODEL))],
        out_shape=[jax.ShapeDtypeStruct((S, D_MODEL), F32), jax.ShapeDtypeStruct((1, D_MODEL), F32)],
        compiler_params=_params(("arbitrary",)),
    )(dact, w_up_q, x1, g2, dx2)


def _out_bwd(dx1, w_out, tm):
    S = dx1.shape[0]

    def body(d_ref, w_ref, o_ref):
        o_ref[...] = lax.dot_general(d_ref[...].astype(BF16), w_ref[...], NT, preferred_element_type=F32).astype(BF16)

    return pl.pallas_call(
        body, name="out_bwd", grid=(S // tm,),
        in_specs=[pl.BlockSpec((tm, D_MODEL), lambda i: (i, 0)), _full((D_MODEL, D_MODEL))],
        out_specs=pl.BlockSpec((tm, D_MODEL), lambda i: (i, 0)),
        out_shape=jax.ShapeDtypeStruct((S, D_MODEL), BF16),
        compiler_params=_params(("parallel",)),
    )(dx1, w_out)


def _gate_bwd(z, dcat, w_mask, w_mask_t, ln_row, b_full, seg_avg, head_ind, tm):
    S = z.shape[0]
    nb = tm // SG_BLOCK

    def body(zu_ref, zv_ref, do_ref, w_ref, wt_ref, ln_ref, b_ref, avg_ref, ind_ref,
             dzu_ref, dzv_ref, dw_ref, db_ref, dln_ref, dvn_s, dbf_s):
        i = pl.program_id(0)

        @pl.when(i == 0)
        def _():
            dw_ref[...] = jnp.zeros_like(dw_ref)
            dln_ref[...] = jnp.zeros_like(dln_ref)
            dbf_s[...] = jnp.zeros_like(dbf_s)

        zu = zu_ref[...].astype(F32)
        zv = zv_ref[...].astype(F32)
        u = _gelu(zu)
        v = _gelu(zv)
        avg = avg_ref[...]
        vhat, rstd = _layer_norm_heads(v, avg)
        ln = ln_ref[...]
        vn = vhat * ln
        for b in range(nb):
            rows = slice(b * SG_BLOCK, (b + 1) * SG_BLOCK)
            vn_b = vn[rows]
            mixed = _gate_mix(vn_b, w_ref, b_ref[...])
            do = do_ref[rows, :].astype(F32)
            dzu_ref[rows, :] = (do * mixed * _gelu_grad(zu[rows])).astype(BF16)
            dmix = do * u[rows]
            dbf_s[...] += dmix
            vn_bf = vn_b.astype(BF16)
            dvn = jnp.zeros((SG_BLOCK, D_HEADS), F32)
            for h in range(N_HEADS):
                dmh = jnp.where(_head_mask(h, SG_BLOCK), dmix, 0.0).astype(BF16)
                dw_ref[h] += lax.dot_general(dmh, vn_bf, NT, preferred_element_type=F32)
                dvn = dvn + jnp.dot(wt_ref[h], dmh, preferred_element_type=F32)
            dvn_s[rows, :] = dvn
        dvn = dvn_s[...]
        dln_ref[...] += jnp.sum(dvn * vhat, axis=0, keepdims=True)
        dvhat = dvn * ln
        dv = rstd * (dvhat - _split_dot(dvhat, avg) - vhat * _split_dot(dvhat * vhat, avg))
        dzv_ref[...] = (dv * _gelu_grad(zv)).astype(BF16)

        @pl.when(i == pl.num_programs(0) - 1)
        def _():
            r = lax.broadcasted_iota(jnp.int32, (SG_BLOCK, SG_BLOCK), 0) // CHUNK
            s = lax.broadcasted_iota(jnp.int32, (SG_BLOCK, SG_BLOCK), 1) // CHUNK
            for h in range(N_HEADS):
                dw_ref[h] = jnp.where(r >= s, dw_ref[h], 0.0)
            db_ref[...] = _split_dot(dbf_s[...], ind_ref[...])

    row = lambda col: pl.BlockSpec((tm, D_HEADS), lambda i: (i, col))
    wspec = _full((N_HEADS, SG_BLOCK, SG_BLOCK))
    return pl.pallas_call(
        body, name="gate_bwd", grid=(S // tm,),
        in_specs=[row(0), row(1), row(0), wspec, wspec, _full((1, D_HEADS)), _full((SG_BLOCK, D_HEADS)),
                  _full((D_HEADS, D_HEADS)), _full((D_HEADS, LANES))],
        out_specs=[row(0), row(0), wspec, _full((SG_BLOCK, LANES)), _full((1, D_HEADS))],
        out_shape=[jax.ShapeDtypeStruct((S, D_HEADS), BF16), jax.ShapeDtypeStruct((S, D_HEADS), BF16),
                   jax.ShapeDtypeStruct((N_HEADS, SG_BLOCK, SG_BLOCK), F32), jax.ShapeDtypeStruct((SG_BLOCK, LANES), F32),
                   jax.ShapeDtypeStruct((1, D_HEADS), F32)],
        scratch_shapes=[pltpu.VMEM((tm, D_HEADS), F32), pltpu.VMEM((SG_BLOCK, D_HEADS), F32)],
        compiler_params=_params(("arbitrary",)),
    )(z, z, dcat, w_mask, w_mask_t, ln_row, b_full, seg_avg, head_ind)


def _attn_pack_grad(o, dcat, qa, lse, head_ind, k, tm):
    S = o.shape[0]

    def body(o_ref, do_ref, qa_ref, lse_ref, ind_ref, pl_ref, pt_ref, eye_ref, ds_ref, dst_ref, ls_ref, lst_ref,
             dop_ref, qb_ref, dot_ref, qbt_ref):
        do = do_ref[...]
        delta = _split_dot(o_ref[...].astype(F32) * do.astype(F32), ind_ref[...])
        hi = delta.astype(BF16)
        lo = (delta - hi.astype(F32)).astype(BF16)
        dop = jnp.dot(do, pl_ref[...], preferred_element_type=F32)
        dop = dop - jnp.dot(hi, ds_ref[0], preferred_element_type=F32) - jnp.dot(lo, ds_ref[1], preferred_element_type=F32)
        dop_ref[...] = dop.astype(BF16)
        dot = lax.dot_general(pt_ref[...], do, NT, preferred_element_type=F32)
        dot = dot - lax.dot_general(dst_ref[0], hi, NT, preferred_element_type=F32)
        dot = dot - lax.dot_general(dst_ref[1], lo, NT, preferred_element_type=F32)
        dot_ref[...] = dot.astype(BF16)
        qa = qa_ref[...]
        qb = qa.astype(F32)
        qbt = lax.dot_general(eye_ref[...], qa, NT, preferred_element_type=F32)
        for j, part in enumerate(_split3(lse_ref[...])):
            part = part.astype(F32)
            qb = qb - lax.dot_general(part, ls_ref[j], TN, precision=lax.Precision.HIGHEST, preferred_element_type=F32)
            qbt = qbt - jnp.dot(lst_ref[j], part, precision=lax.Precision.HIGHEST, preferred_element_type=F32)
        qb_ref[...] = qb.astype(BF16)
        qbt_ref[...] = qbt.astype(BF16)

    pad = pl.BlockSpec((tm, D_PAD), lambda i: (i, 0))
    padt = pl.BlockSpec((D_PAD, tm), lambda i: (0, i))
    return pl.pallas_call(
        body, name="attn_pack_grad", grid=(S // tm,),
        in_specs=[pl.BlockSpec((tm, D_HEADS), lambda i: (i, 0)), pl.BlockSpec((tm, D_HEADS), lambda i: (i, 1)), pad,
                  pl.BlockSpec((N_HEADS, tm), lambda i: (0, i)), _full((D_HEADS, LANES)), _full((D_HEADS, D_PAD)),
                  _full((D_PAD, D_HEADS)), _full((D_PAD, D_PAD)), _full((2, LANES, D_PAD)), _full((2, D_PAD, LANES)),
                  _full((3, N_HEADS, D_PAD)), _full((3, D_PAD, N_HEADS))],
        out_specs=[pad, pad, padt, padt],
        out_shape=[jax.ShapeDtypeStruct((S, D_PAD), BF16)] * 2 + [jax.ShapeDtypeStruct((D_PAD, S), BF16)] * 2,
        compiler_params=_params(("parallel",)),
    )(o, dcat, qa, lse, head_ind, k["place"], k["place_t"], jnp.eye(D_PAD, dtype=BF16), k["d_stat"],
      jnp.swapaxes(k["d_stat"], 1, 2), k["l_stat"], jnp.swapaxes(k["l_stat"], 1, 2))


def _attn_bwd(qb, qbt, ka, va, dop, dopt, k, tq):
    S = qb.shape[0]
    n = S // tq

    def body(q_ref, qt_ref, k_ref, v_ref, do_ref, dot_ref, pt_ref, pick_ref, dq_hbm, dk_ref, dv_ref, dcc_ref, dq_s, dk_s, dv_s, s_s, d_s, sem):
        g, ki, qi = pl.program_id(0), pl.program_id(1), pl.program_id(2)

        @pl.when((ki == 0) & (qi == 0))
        def _():
            dq_s[...] = jnp.zeros_like(dq_s)

        @pl.when(qi == ki)
        def _():
            dk_s[...] = jnp.zeros_like(dk_s)
            dv_s[...] = jnp.zeros_like(dv_s)

        def step(diagonal):
            chunks = [slice(c * KEY_CHUNK, (c + 1) * KEY_CHUNK) for c in range(tq // KEY_CHUNK)]

            def scores(hh, rows, slot):
                sl = slice(hh * HEAD_PAD, (hh + 1) * HEAD_PAD)
                s_s[slot, rows, :] = lax.dot_general(q_ref[rows, sl], k_ref[:, sl], NT, preferred_element_type=F32)
                d_s[slot, rows, :] = lax.dot_general(do_ref[rows, sl], v_ref[:, sl], NT, preferred_element_type=F32)

            for rows in chunks:
                scores(0, rows, 0)
            for hh in range(GROUP_HEADS):
                sl = slice(hh * HEAD_PAD, (hh + 1) * HEAD_PAD)
                slot = hh % 2
                dv, dk = dv_s[sl, :], dk_s[sl, :]
                for rows in chunks:
                    if hh + 1 < GROUP_HEADS:
                        scores(hh + 1, rows, 1 - slot)
                    p = jnp.exp2(s_s[slot, rows, :])
                    if diagonal:
                        row = rows.start + lax.broadcasted_iota(jnp.int32, (KEY_CHUNK, tq), 0)
                        col = lax.broadcasted_iota(jnp.int32, (KEY_CHUNK, tq), 1)
                        p = jnp.where(row >= col, p, 0.0)
                    ds = (p * d_s[slot, rows, :]).astype(BF16)
                    dv = dv + jnp.dot(dot_ref[sl, rows], p.astype(BF16), preferred_element_type=F32)
                    dk = dk + jnp.dot(qt_ref[sl, rows], ds, preferred_element_type=F32)
                    qrows = pl.ds(pl.multiple_of(qi * tq + rows.start, KEY_CHUNK), KEY_CHUNK)
                    dq_s[qrows, sl] += jnp.dot(ds, k_ref[:, sl], preferred_element_type=F32)
                dv_s[sl, :] = dv
                dk_s[sl, :] = dk

        @pl.when(qi > ki)
        def _():
            step(False)

        @pl.when(qi == ki)
        def _():
            step(True)

        @pl.when(qi == n - 1)
        def _():
            dk = dk_s[...]
            pt = pt_ref[...]
            dk_ref[...] = lax.dot_general((dk * (1.0 / LOG2E)).astype(BF16), pt, TN, preferred_element_type=F32).astype(BF16)
            dv_ref[...] = lax.dot_general(dv_s[...].astype(BF16), pt, TN, preferred_element_type=F32).astype(BF16)
            dcc_ref[...] = sum(lax.dot_general(part, pick_ref[...], TN, preferred_element_type=F32) for part in _split3(dk))

        @pl.when((ki == n - 1) & (qi == n - 1))
        def _():
            cp = pltpu.make_async_copy(dq_s, dq_hbm.at[g], sem)
            cp.start()
            cp.wait()

    gw = GROUP_HEADS * HEAD_DIM
    qmax = lambda ki, qi: jnp.maximum(qi, ki)
    qspec = pl.BlockSpec((tq, GROUP_PAD), lambda g, ki, qi: (qmax(ki, qi), g))
    qtspec = pl.BlockSpec((GROUP_PAD, tq), lambda g, ki, qi: (g, qmax(ki, qi)))
    kspec = pl.BlockSpec((tq, GROUP_PAD), lambda g, ki, qi: (ki, g))
    kout = pl.BlockSpec((tq, gw), lambda g, ki, qi: (ki, g))
    return pl.pallas_call(
        body, name="attn_bwd", grid=(GROUPS, n, n),
        in_specs=[qspec, qtspec, kspec, kspec, qspec, qtspec,
                  _full((GROUP_PAD, gw)), pl.BlockSpec((None, GROUP_PAD, LANES), lambda g, ki, qi: (g, 0, 0))],
        out_specs=[_ANY, kout, kout, pl.BlockSpec((None, tq, LANES), lambda g, ki, qi: (g, ki, 0))],
        out_shape=[jax.ShapeDtypeStruct((GROUPS, S, GROUP_PAD), F32), jax.ShapeDtypeStruct((S, D_HEADS), BF16),
                   jax.ShapeDtypeStruct((S, D_HEADS), BF16), jax.ShapeDtypeStruct((GROUPS, S, LANES), F32)],
        scratch_shapes=[pltpu.VMEM((S, GROUP_PAD), F32), pltpu.VMEM((GROUP_PAD, tq), F32), pltpu.VMEM((GROUP_PAD, tq), F32),
                        pltpu.VMEM((2, tq, tq), F32), pltpu.VMEM((2, tq, tq), F32), pltpu.SemaphoreType.DMA],
        compiler_params=_params(("arbitrary", "arbitrary", "arbitrary")),
    )(qb, qbt, ka, va, dop, dopt, k["place_t_group"], k["pick_cols"])


def _attn_unpack(dqp, dcc, k, tm):
    S = dqp.shape[1]
    gw = GROUP_HEADS * HEAD_DIM

    def body(dqp_ref, dcc_ref, pt_ref, pick_ref, dq_ref, dc_ref):
        dc = jnp.zeros((tm, LANES), F32)
        for g in range(GROUPS):
            x = dqp_ref[g]
            dq_ref[:, g * gw:(g + 1) * gw] = jnp.dot((x * SCALE).astype(BF16), pt_ref[...], preferred_element_type=F32).astype(BF16)
            dc = dc + _split3_dot(x, pick_ref[g]) - dcc_ref[g]
        dc_ref[...] = dc

    return pl.pallas_call(
        body, name="attn_unpack", grid=(S // tm,),
        in_specs=[pl.BlockSpec((GROUPS, tm, GROUP_PAD), lambda i: (0, i, 0)), pl.BlockSpec((GROUPS, tm, LANES), lambda i: (0, i, 0)),
                  _full((GROUP_PAD, gw)), _full((GROUPS, GROUP_PAD, LANES))],
        out_specs=[pl.BlockSpec((tm, D_HEADS), lambda i: (i, 0)), pl.BlockSpec((tm, LANES), lambda i: (i, 0))],
        out_shape=[jax.ShapeDtypeStruct((S, D_HEADS), BF16), jax.ShapeDtypeStruct((S, LANES), F32)],
        compiler_params=_params(("parallel",)),
    )(dqp, dcc, k["place_t_group"], k["pick_rows"])


def _fox_bwd(dc, f, bias_row, tb):
    S = f.shape[0]
    nb = S // tb

    def body(dc_ref, f_ref, b_ref, df_ref, dbias_ref, carry):
        @pl.when(pl.program_id(0) == 0)
        def _():
            carry[...] = jnp.zeros_like(carry)
            dbias_ref[...] = jnp.zeros_like(dbias_ref)

        r = lax.broadcasted_iota(jnp.int32, (tb, tb), 0)
        s = lax.broadcasted_iota(jnp.int32, (tb, tb), 1)
        tri = (s >= r).astype(F32)
        rc = jnp.dot(tri, dc_ref[...], precision=lax.Precision.HIGHEST, preferred_element_type=F32) + carry[0:1, :]
        carry[...] = jnp.broadcast_to(rc[0:1, :], carry.shape)
        lane = lax.broadcasted_iota(jnp.int32, (tb, LANES), 1)
        df = jnp.where(lane < N_HEADS, rc * jax.nn.sigmoid(-(f_ref[...] + b_ref[...])), 0.0)
        df_ref[...] = df.astype(BF16)
        dbias_ref[...] += jnp.sum(df, axis=0, keepdims=True)

    rev = pl.BlockSpec((tb, LANES), lambda i: (nb - 1 - i, 0))
    return pl.pallas_call(
        body, name="fox_bwd", grid=(nb,),
        in_specs=[rev, rev, _full((1, LANES))],
        out_specs=[rev, _full((1, LANES))],
        out_shape=[jax.ShapeDtypeStruct((S, LANES), BF16), jax.ShapeDtypeStruct((1, LANES), F32)],
        scratch_shapes=[pltpu.VMEM((SUBLANES, LANES), F32)],
        compiler_params=_params(("arbitrary",)),
    )(dc, f, bias_row)


_DZ_WIDTHS = (D_HEADS,) * 5 + (LANES,)


def _in_bwd(pieces, w_in, x, g1, dx1, tm):
    S = x.shape[0]

    def body(*refs):
        p_refs, (w_ref, x_ref, g_ref, dx1_ref, dx_ref, dg_ref) = refs[:6], refs[6:]

        @pl.when(pl.program_id(0) == 0)
        def _():
            dg_ref[...] = jnp.zeros_like(dg_ref)

        dh = jnp.zeros((tm, D_MODEL), F32)
        off = 0
        for p_ref, w in zip(p_refs, _DZ_WIDTHS):
            dh = dh + lax.dot_general(p_ref[...].astype(BF16), w_ref[:, off:off + w], NT, preferred_element_type=F32)
            off += w
        dx, dg = _rms_bwd(dh, x_ref[...], g_ref[...])
        dg_ref[...] += dg
        dx_ref[...] = dx1_ref[...] + dx

    row = lambda w: pl.BlockSpec((tm, w), lambda i: (i, 0))
    return pl.pallas_call(
        body, name="in_bwd", grid=(S // tm,),
        in_specs=[row(w) for w in _DZ_WIDTHS] + [_full((D_MODEL, D_IN_PAD)), row(D_MODEL), _full((1, D_MODEL)), row(D_MODEL)],
        out_specs=[row(D_MODEL), _full((1, D_MODEL))],
        out_shape=[jax.ShapeDtypeStruct((S, D_MODEL), F32), jax.ShapeDtypeStruct((1, D_MODEL), F32)],
        compiler_params=_params(("arbitrary",)),
    )(*pieces, w_in, x, g1, dx1)


def _dw_in(h1, pieces, tk):
    S = h1.shape[0]

    def body(*refs):
        h_ref, p_refs, o_ref = refs[0], refs[1:7], refs[7]

        @pl.when(pl.program_id(0) == 0)
        def _():
            o_ref[...] = jnp.zeros_like(o_ref)

        off = 0
        for p_ref, w in zip(p_refs, _DZ_WIDTHS):
            o_ref[:, off:off + w] += lax.dot_general(h_ref[...], p_ref[...].astype(BF16), TN, preferred_element_type=F32)
            off += w

    row = lambda w: pl.BlockSpec((tk, w), lambda k: (k, 0))
    return pl.pallas_call(
        body, name="dw_in", grid=(S // tk,),
        in_specs=[row(D_MODEL)] + [row(w) for w in _DZ_WIDTHS],
        out_specs=_full((D_MODEL, D_IN_PAD)),
        out_shape=jax.ShapeDtypeStruct((D_MODEL, D_IN_PAD), F32),
        compiler_params=_params(("arbitrary",)),
    )(h1, *pieces)


def _adamw_math(w, g, m, v):
    m = ADAM_B1 * m + (1.0 - ADAM_B1) * g
    v = ADAM_B2 * v + (1.0 - ADAM_B2) * (g * g)
    m_hat = m / (1.0 - ADAM_B1 ** ADAM_STEP)
    v_hat = v / (1.0 - ADAM_B2 ** ADAM_STEP)
    delta = -ADAM_LR * (m_hat / (jnp.sqrt(v_hat) + ADAM_EPS) + ADAM_WD * w)
    return delta, m, v


def _adamw(name, w, g, m, v):
    R, C = w.shape
    tr = _row_tile(R, 256)

    def body(w_ref, g_ref, m_ref, v_ref, d_ref, nm_ref, nv_ref):
        d, nm, nv = _adamw_math(w_ref[...], g_ref[...], m_ref[...], v_ref[...])
        d_ref[...] = d
        nm_ref[...] = nm
        nv_ref[...] = nv

    spec = pl.BlockSpec((tr, C), lambda i: (i, 0))
    return pl.pallas_call(
        body, name=name, grid=(R // tr,), in_specs=[spec] * 4, out_specs=[spec] * 3,
        out_shape=[jax.ShapeDtypeStruct((R, C), F32)] * 3,
        compiler_params=_params(("parallel",)),
    )(w, g, m, v)


def _pair_sum(name, grad, theirs, ids):
    q, half, C = theirs.shape
    tr = _row_tile(half, 256)
    nb = half // tr

    def body(ids_ref, a_ref, b_ref, s_ref, sb_ref):
        s = a_ref[...] + b_ref[...]
        s_ref[...] = s
        sb_ref[...] = s.astype(BF16)

    here = pl.BlockSpec((None, tr, C), lambda j, i, ids: (j, i, 0))
    return pl.pallas_call(
        body, name=name,
        grid_spec=pltpu.PrefetchScalarGridSpec(
            num_scalar_prefetch=1, grid=(q, nb),
            in_specs=[pl.BlockSpec((None, tr, C), lambda j, i, ids: (j, ids[1] * nb + i, 0)), here],
            out_specs=[here, here]),
        out_shape=[jax.ShapeDtypeStruct((q, half, C), F32), jax.ShapeDtypeStruct((q, half, C), BF16)],
        compiler_params=_params(("parallel", "parallel")),
    )(ids, grad, theirs)


def _chip_sum(name, sums32, others, ids):
    _, half, C = sums32.shape
    tr = _row_tile(half, 256)
    nb = half // tr

    def body(ids_ref, a_ref, o_ref, s_ref):
        s = a_ref[...]
        for j in range(3):
            s = s + o_ref[j].astype(F32)
        s_ref[...] = s

    return pl.pallas_call(
        body, name=name,
        grid_spec=pltpu.PrefetchScalarGridSpec(
            num_scalar_prefetch=1, grid=(nb,),
            in_specs=[pl.BlockSpec((None, tr, C), lambda i, ids: (ids[0], i, 0)),
                      pl.BlockSpec((3, tr, C), lambda i, ids: (0, i, 0))],
            out_specs=pl.BlockSpec((tr, C), lambda i, ids: (ids[1] * nb + i, 0))),
        out_shape=jax.ShapeDtypeStruct((2 * half, C), F32),
        compiler_params=_params(("parallel",)),
    )(ids, sums32, others)


def _place():
    return lax.axis_index("x"), lax.axis_index("y"), lax.axis_index("c")


def _other_chips(x, y):
    return [(1 - x, y), (x, 1 - y), (1 - x, 1 - y)]


_ANY = pl.BlockSpec(memory_space=pl.ANY)


def _gather_quarters(shards):
    n = len(shards)
    halved = [s.shape[0] % 32 == 0 for s in shards]

    def body(*refs):
        ins, outs = refs[:n], refs[n:2 * n]
        send_sems, recv_sems, pass_send_sems, pass_recv_sems = refs[2 * n:]
        x, y, c = _place()
        mine = 2 * x + y
        chips = _other_chips(x, y)

        def part(a, quarter, core):
            if not halved[a]:
                return outs[a].at[quarter]
            half = ins[a].shape[0] // 2
            return outs[a].at[quarter, pl.ds(core * half, half), :]

        def source(a):
            if not halved[a]:
                return ins[a]
            half = ins[a].shape[0] // 2
            return ins[a].at[pl.ds(c * half, half), :]

        sends = []
        for a in range(n):
            for j, (px, py) in enumerate(chips):
                cp = pltpu.make_async_remote_copy(src_ref=source(a), dst_ref=part(a, mine, c), send_sem=send_sems.at[a, j],
                                                  recv_sem=recv_sems.at[a, j], device_id=(px, py, c), device_id_type=MESH)
                cp.start()
                sends.append(cp)
        for a in range(n):
            for j, (px, py) in enumerate(chips):
                landed = part(a, 2 * px + py, c)
                pltpu.make_async_remote_copy(src_ref=source(a), dst_ref=landed, send_sem=send_sems.at[a, j],
                                             recv_sem=recv_sems.at[a, j], device_id=(px, py, c), device_id_type=MESH).wait_recv()
                if halved[a]:
                    cp = pltpu.make_async_remote_copy(src_ref=landed, dst_ref=landed, send_sem=pass_send_sems.at[a, j],
                                                      recv_sem=pass_recv_sems.at[a, j], device_id=(x, y, 1 - c), device_id_type=MESH)
                    cp.start()
                    sends.append(cp)
        for a in range(n):
            if halved[a]:
                for j, (px, py) in enumerate(chips):
                    other = part(a, 2 * px + py, 1 - c)
                    pltpu.make_async_remote_copy(src_ref=other, dst_ref=other, send_sem=pass_send_sems.at[a, j],
                                                 recv_sem=pass_recv_sems.at[a, j], device_id=(x, y, 1 - c),
                                                 device_id_type=MESH).wait_recv()
        for cp in sends:
            cp.wait_send()

    sems = pltpu.SemaphoreType.DMA((n, 3))
    return pl.pallas_call(
        body, name="gather_weights",
        in_specs=[_ANY] * n, out_specs=[_ANY] * n,
        out_shape=[jax.ShapeDtypeStruct((4,) + s.shape, s.dtype) for s in shards],
        scratch_shapes=[sems, sems, sems, sems],
    )(*shards)


def _swap_halves(grads):
    n = len(grads)

    def body(*refs):
        ins, outs = refs[:n], refs[n:2 * n]
        send_sems, recv_sems = refs[2 * n:]
        x, y, c = _place()
        started = []
        for a in range(n):
            half = ins[a].shape[1] // 2
            cp = pltpu.make_async_remote_copy(src_ref=ins[a].at[:, pl.ds((1 - c) * half, half), :], dst_ref=outs[a],
                                              send_sem=send_sems.at[a], recv_sem=recv_sems.at[a],
                                              device_id=(x, y, 1 - c), device_id_type=MESH)
            cp.start()
            started.append(cp)
        for cp in started:
            cp.wait()

    return pl.pallas_call(
        body, name="swap_halves",
        in_specs=[_ANY] * n, out_specs=[_ANY] * n,
        out_shape=[jax.ShapeDtypeStruct((4, g.shape[1] // 2, g.shape[2]), F32) for g in grads],
        scratch_shapes=[pltpu.SemaphoreType.DMA((n,)), pltpu.SemaphoreType.DMA((n,))],
    )(*grads)


def _scatter_quarters(sums16):
    n = len(sums16)

    def body(*refs):
        ins, outs = refs[:n], refs[n:2 * n]
        send_sems, recv_sems = refs[2 * n:]
        x, y, c = _place()
        sends = []
        for a in range(n):
            for j, (px, py) in enumerate(_other_chips(x, y)):
                cp = pltpu.make_async_remote_copy(src_ref=ins[a].at[2 * px + py], dst_ref=outs[a].at[j],
                                                  send_sem=send_sems.at[a, j], recv_sem=recv_sems.at[a, j],
                                                  device_id=(px, py, c), device_id_type=MESH)
                cp.start()
                sends.append(cp)
        for cp in sends:
            cp.wait()

    return pl.pallas_call(
        body, name="scatter_quarters",
        in_specs=[_ANY] * n, out_specs=[_ANY] * n,
        out_shape=[jax.ShapeDtypeStruct((3,) + s.shape[1:], BF16) for s in sums16],
        scratch_shapes=[pltpu.SemaphoreType.DMA((n, 3)), pltpu.SemaphoreType.DMA((n, 3))],
    )(*sums16)


def _join_halves(fulls):
    n = len(fulls)

    def body(*refs):
        ins, outs = refs[:n], refs[n:2 * n]
        send_sems, recv_sems = refs[2 * n:]
        x, y, c = _place()
        started = []
        for a in range(n):
            half = ins[a].shape[0] // 2
            rows = pl.ds(c * half, half)
            cp = pltpu.make_async_remote_copy(src_ref=ins[a].at[rows, :], dst_ref=outs[a].at[rows, :], send_sem=send_sems.at[a],
                                              recv_sem=recv_sems.at[a], device_id=(x, y, 1 - c), device_id_type=MESH)
            cp.start()
            started.append(cp)
        for cp in started:
            cp.wait()

    return pl.pallas_call(
        body, name="join_halves",
        in_specs=[_ANY] * n, out_specs=[_ANY] * n,
        out_shape=[jax.ShapeDtypeStruct(f.shape, F32) for f in fulls],
        input_output_aliases={a: a for a in range(n)},
        scratch_shapes=[pltpu.SemaphoreType.DMA((n,)), pltpu.SemaphoreType.DMA((n,))],
    )(*fulls)


def _small_allreduce_adamw(g, w, m, v):
    R = g.shape[0]

    def body(g_ref, w_ref, m_ref, v_ref, gs_ref, d_ref, nm_ref, nv_ref, all_s, send_sems, recv_sems):
        x, y, c = _place()
        me = 4 * x + 2 * y + c
        all_s[me] = g_ref[...]
        sends = []
        for k in range(1, 8):
            peer = (x ^ (k >> 2), y ^ ((k >> 1) & 1), c ^ (k & 1))
            cp = pltpu.make_async_remote_copy(src_ref=g_ref, dst_ref=all_s.at[me], send_sem=send_sems.at[k - 1],
                                              recv_sem=recv_sems.at[k - 1], device_id=peer, device_id_type=MESH)
            cp.start()
            sends.append(cp)
        for cp in sends:
            cp.wait()
        total = all_s[0]
        for d in range(1, 8):
            total = total + all_s[d]
        gs_ref[...] = total
        delta, nm, nv = _adamw_math(w_ref[...], total, m_ref[...], v_ref[...])
        d_ref[...] = delta
        nm_ref[...] = nm
        nv_ref[...] = nv

    vm = pl.BlockSpec(memory_space=pltpu.VMEM)
    return pl.pallas_call(
        body, name="small_allreduce_adamw",
        in_specs=[vm] * 4, out_specs=[vm] * 4, out_shape=[jax.ShapeDtypeStruct((R, LANES), F32)] * 4,
        scratch_shapes=[pltpu.VMEM((8, R, LANES), F32), pltpu.SemaphoreType.DMA((7,)), pltpu.SemaphoreType.DMA((7,))],
        compiler_params=pltpu.CompilerParams(vmem_limit_bytes=VMEM_LIMIT),
    )(g, w, m, v)


_SMALL = (("norm_mix_g", D_MODEL), ("f_bias", N_HEADS), ("sg_ln_g", D_HEADS), ("sg_w", N_HEADS * SG_BLOCK * SG_BLOCK),
          ("sg_b", N_HEADS * SG_BLOCK), ("norm_ffn_g", D_MODEL), ("w_conv", 3 * 2 * D_FF), ("b_conv", 2 * D_FF),
          ("norm_final_g", D_MODEL))


def _pack_small(parts):
    rows = []
    for name, size in _SMALL:
        flat = parts[name].reshape(-1).astype(F32)
        pad = (-size) % (SUBLANES * LANES)
        rows.append(jnp.pad(flat, (0, pad)).reshape(-1, LANES))
    return jnp.concatenate(rows, axis=0)


def _unpack_small(packed, shapes):
    out, r = {}, 0
    for name, size in _SMALL:
        nrows = (size + SUBLANES * LANES - 1) // (SUBLANES * LANES) * SUBLANES
        out[name] = packed[r:r + nrows].reshape(-1)[:size].reshape(shapes[name])
        r += nrows
    return out


def _local_step(x, target, g1, w_in, f_bias, sg_ln_g, sg_w, sg_b, w_out, g2, w_up_q, w_conv, b_conv, w_down, g3):
    S = x.shape[0]
    tm = _row_tile(S, 512)
    tms = _row_tile(S, 256)
    tq = _row_tile(S, 512)

    lane = jnp.arange(D_HEADS)
    seg_avg = jnp.where(lane[:, None] // HEAD_DIM == lane[None, :] // HEAD_DIM, 1.0 / HEAD_DIM, 0.0).astype(BF16)
    head_ind = (lane[:, None] // HEAD_DIM == jnp.arange(LANES)[None, :]).astype(BF16)
    pos_chunk = jnp.arange(SG_BLOCK) // CHUNK
    w_mask32 = jnp.where(pos_chunk[:, None] >= pos_chunk[None, :], sg_w, 0.0)
    w_mask = w_mask32.astype(BF16)
    w_mask_t = jnp.swapaxes(w_mask32, 1, 2).astype(BF16)
    ln_row = sg_ln_g.reshape(1, D_HEADS)
    b_full = jnp.repeat(sg_b.T, HEAD_DIM, axis=1)
    bias_row = jnp.pad(f_bias.reshape(1, N_HEADS), ((0, 0), (0, LANES - N_HEADS)))
    b_conv_row = b_conv.reshape(1, 2 * D_FF)

    z, f, h1 = _in_proj(x, g1, w_in, tm)
    c, ct = _fox_prep(f, bias_row, _row_tile(S, 256))
    consts = _attn_consts()
    qa, ka, va, vat = _attn_pack(z, c, consts, tm)
    out_b, lse = _attn_fwd(qa, ka, vat, consts["place_t"], tq)
    out_a = _gate_fwd(z, w_mask, ln_row, b_full, seg_avg, tm)
    x1, h2 = _mix_out(x, out_a, out_b, w_out, g2, tm)
    a = _up_proj(h2, w_up_q, tm)
    dx2, sq_err, dg3 = _ffn_fwd_loss(a, w_conv, b_conv_row, w_down, x1, g3, target, tms)

    dconv, y, dw_conv8, db_conv = _ffn_bwd_gate(dx2, a, w_conv, b_conv_row, w_down, tms)
    dact = _conv_bwd(dconv, w_conv, tm, 2 * D_FF // 4)
    dw_down = _matmul_tn(y, dx2, "dw_down", D_FF // 2, D_MODEL, tm, quarters=(2, 1))
    dx1, dg2 = _up_bwd(dact, w_up_q, x1, g2, dx2, tms)
    dw_up_q = _matmul_tn(h2, dact, "dw_up", D_MODEL, 2 * D_FF // 4, tm, quarters=(1, 4))
    dcat = _out_bwd(dx1, w_out, tm)
    dw_out_a = _matmul_tn(out_a, dx1, "dw_out_a", D_HEADS, D_MODEL, tm)
    dw_out_b = _matmul_tn(out_b, dx1, "dw_out_b", D_HEADS, D_MODEL, tm)
    dzu, dzv, dsg_w, dsg_b_t, dln = _gate_bwd(z, dcat, w_mask, w_mask_t, ln_row, b_full, seg_avg, head_ind, tm)
    dop, qb, dopt, qbt = _attn_pack_grad(out_b, dcat, qa, lse, head_ind, consts, tm)
    dqp, dk, dv, dcc = _attn_bwd(qb, qbt, ka, va, dop, dopt, consts, tq)
    dq, dc = _attn_unpack(dqp, dcc, consts, tm)
    df, dbias = _fox_bwd(dc, f, bias_row, _row_tile(S, 256))
    pieces = (dzu, dzv, dq, dk, dv, df)
    dx, dg1 = _in_bwd(pieces, w_in, x, g1, dx1, tms)
    dw_in = _dw_in(h1, pieces, tm)

    grads = {
        "norm_mix_g": dg1, "f_bias": dbias[:, :N_HEADS], "sg_ln_g": dln, "sg_w": dsg_w, "sg_b": dsg_b_t[:, :N_HEADS].T,
        "norm_ffn_g": dg2, "w_conv": dw_conv8[:3], "b_conv": db_conv, "norm_final_g": dg3,
        "w_in": dw_in, "w_out": jnp.concatenate([dw_out_a, dw_out_b], axis=0), "w_up_q": dw_up_q, "w_down": dw_down,
    }
    return sq_err, dx, grads


def _reduce_big(grads_q, ids):
    names = list(grads_q)
    theirs = _swap_halves([grads_q[k] for k in names])
    sums = [_pair_sum("pair_sum_" + k, grads_q[k], t, ids) for k, t in zip(names, theirs)]
    got = _scatter_quarters([s16 for _, s16 in sums])
    fulls = [_chip_sum("chip_sum_" + k, s32, g, ids) for k, (s32, _), g in zip(names, sums, got)]
    return dict(zip(names, _join_halves(fulls)))


def kernel(x, norm_mix_g, w_in, f_bias, sg_ln_g, sg_w, sg_b, w_out, norm_ffn_g, w_up, w_conv, b_conv, w_down, norm_final_g, loss_target, m_norm_mix_g, m_w_in, m_f_bias, m_sg_ln_g, m_sg_w, m_sg_b, m_w_out, m_norm_ffn_g, m_w_up, m_w_conv, m_b_conv, m_w_down, m_norm_final_g, v_norm_mix_g, v_w_in, v_f_bias, v_sg_ln_g, v_sg_w, v_sg_b, v_w_out, v_norm_ffn_g, v_w_up, v_w_conv, v_b_conv, v_w_down, v_norm_final_g):
    args = dict(locals())
    quarter = 2 * lax.axis_index("x") + lax.axis_index("y")
    ids = jnp.stack([quarter, lax.axis_index("c")]).astype(jnp.int32)
    wq_conv = w_conv.shape[-1]

    shards = [w_in[0].astype(BF16), w_out[0].astype(BF16), w_up[0].astype(BF16), w_down[0].astype(BF16), w_conv[0]]
    g_in, g_out, g_up, g_down, g_conv = [lax.dynamic_update_index_in_dim(g, s, quarter, 0)
                                         for g, s in zip(_gather_quarters(shards), shards)]
    w_in_full = jnp.pad(jnp.concatenate([g_in[q] for q in range(4)], axis=1), ((0, 0), (0, D_IN_PAD - D_IN)))
    w_out_full = g_out.reshape(D_MODEL, D_MODEL)
    w_down_full = g_down.reshape(D_FF, D_MODEL)
    w_conv_full = jnp.concatenate([g_conv[q] for q in range(4)], axis=1)

    sq_err, dx, grads = _local_step(
        x[0], loss_target[0], norm_mix_g, w_in_full, f_bias[0], sg_ln_g[0], sg_w[0], sg_b[0], w_out_full, norm_ffn_g,
        g_up, w_conv_full, b_conv[0], w_down_full, norm_final_g.reshape(1, D_MODEL))
    loss = lax.psum(0.5 * jnp.sum(sq_err) / D_MODEL, ("x", "y", "c"))

    dw_in = grads["w_in"][:, :D_IN].reshape(D_MODEL, 4, D_IN // 4).transpose(1, 0, 2)
    big = _reduce_big({"w_in": dw_in, "w_out": grads["w_out"].reshape(4, D_MODEL // 4, D_MODEL),
                       "w_up": grads["w_up_q"], "w_down": grads["w_down"].reshape(4, D_FF // 4, D_MODEL)}, ids)

    out = {"loss": loss, "grad_x": dx[None]}
    for k in ("w_in", "w_out", "w_up", "w_down"):
        g = big[k]
        d, nm, nv = _adamw("adamw_" + k, args[k][0], g, args["m_" + k][0], args["v_" + k][0])
        out["grad_" + k], out["delta_" + k], out["new_m_" + k], out["new_v_" + k] = g[None], d[None], nm[None], nv[None]

    def padded_conv(t):
        return lax.dynamic_update_slice(jnp.zeros((3, 4 * wq_conv), F32), t[0], (0, quarter * wq_conv))

    small_names = [n for n, _ in _SMALL]
    shapes = {n: (3, 4 * wq_conv) if n == "w_conv" else args[n].shape for n in small_names}
    pack = lambda prefix: _pack_small({n: padded_conv(args[prefix + n]) if n == "w_conv" else args[prefix + n] for n in small_names})
    packed = _small_allreduce_adamw(_pack_small({n: grads[n] for n in small_names}), pack(""), pack("m_"), pack("v_"))
    for prefix, arr in zip(("grad_", "delta_", "new_m_", "new_v_"), packed):
        for n, t in _unpack_small(arr, shapes).items():
            if n == "w_conv":
                t = lax.dynamic_slice(t, (0, quarter * wq_conv), (3, wq_conv))[None]
            out[prefix + n] = t

    weights = ["norm_mix_g", "w_in", "f_bias", "sg_ln_g", "sg_w", "sg_b", "w_out", "norm_ffn_g", "w_up", "w_conv", "b_conv",
               "w_down", "norm_final_g"]
    return (out["loss"], out["grad_x"], *[out[p + n] for p in ("grad_", "delta_", "new_m_", "new_v_") for n in weights])
```

```python
import functools
import math

import jax
import jax.numpy as jnp
from jax import lax
from jax.experimental import pallas as pl
from jax.experimental.pallas import tpu as pltpu

F32 = jnp.float32
BF16 = jnp.bfloat16
MESH = pl.DeviceIdType.MESH

D_MODEL = 1024
N_HEADS = 8
HEAD_DIM = 64
D_HEADS = N_HEADS * HEAD_DIM
SG_BLOCK = 128
CHUNK = 64
D_FF = 2816
D_IN = 2 * D_HEADS + 3 * D_HEADS + N_HEADS
LANES = 128
SUBLANES = 8
D_IN_PAD = 5 * D_HEADS + LANES
EPS = 1e-6
SCALE = HEAD_DIM ** -0.5
NEG = -1e30
LOG2E = 1.4426950408889634
HEAD_PAD = LANES
D_PAD = N_HEADS * HEAD_PAD
Q_STAT = HEAD_DIM
K_STAT = HEAD_DIM + 3
L_STAT = HEAD_DIM + 6
GROUPS = 2
GROUP_HEADS = N_HEADS // GROUPS
GROUP_PAD = GROUP_HEADS * HEAD_PAD
KEY_CHUNK = 256
STAT_ROWS = 16

ADAM_LR = 0.001
ADAM_B1 = 0.9
ADAM_B2 = 0.999
ADAM_EPS = 1e-08
ADAM_WD = 0.01
ADAM_STEP = 10

VMEM_LIMIT = 56 * 1024 * 1024

NT = (((1,), (1,)), ((), ()))
TN = (((0,), (0,)), ((), ()))


def _params(sem):
    return pltpu.CompilerParams(dimension_semantics=sem, vmem_limit_bytes=VMEM_LIMIT)


def _full(shape):
    nd = len(shape)
    return pl.BlockSpec(shape, lambda *_: (0,) * nd)


def _row_tile(rows, target):
    best = None
    for t in range(SUBLANES, min(rows, target) + 1, SUBLANES):
        if rows % t == 0:
            best = t
    assert best is not None, rows
    return best


def _sigmoid(x):
    return 0.5 * jnp.tanh(0.5 * x) + 0.5


def _gelu(z):
    return 0.5 * z * (1.0 + lax.erf(z * (2.0 ** -0.5)))


def _gelu_grad(z):
    cdf = 0.5 * (1.0 + lax.erf(z * (2.0 ** -0.5)))
    pdf = jnp.exp(-0.5 * z * z) * (1.0 / math.sqrt(2.0 * math.pi))
    return cdf + z * pdf


def _split_dot(x, m):
    hi = x.astype(BF16)
    lo = (x - hi.astype(F32)).astype(BF16)
    return jnp.dot(hi, m, preferred_element_type=F32) + jnp.dot(lo, m, preferred_element_type=F32)


def _head_mask(h, rows):
    lane = lax.broadcasted_iota(jnp.int32, (rows, D_HEADS), 1)
    return (lane >= h * HEAD_DIM) & (lane < (h + 1) * HEAD_DIM)


def _rms_bwd(dh, x, g):
    r = lax.rsqrt(jnp.mean(x * x, axis=-1, keepdims=True) + EPS)
    xhat = x * r
    dg = jnp.sum(dh * xhat, axis=0, keepdims=True)
    dxhat = dh * g
    dx = r * (dxhat - xhat * jnp.mean(dxhat * xhat, axis=-1, keepdims=True))
    return dx, dg


def _in_proj(x, g1, w_in, tm):
    S = x.shape[0]
    nz = D_IN_PAD - LANES

    def body(x_ref, g_ref, w_ref, z_ref, f_ref, h_ref):
        xf = x_ref[...]
        r = lax.rsqrt(jnp.mean(xf * xf, axis=-1, keepdims=True) + EPS)
        h = (xf * r * g_ref[...]).astype(BF16)
        h_ref[...] = h
        zz = jnp.dot(h, w_ref[...], preferred_element_type=F32)
        z_ref[...] = zz[:, :nz].astype(BF16)
        f_ref[...] = zz[:, nz:]

    return pl.pallas_call(
        body, name="in_proj", grid=(S // tm,),
        in_specs=[pl.BlockSpec((tm, D_MODEL), lambda i: (i, 0)), _full((1, D_MODEL)), _full((D_MODEL, D_IN_PAD))],
        out_specs=[pl.BlockSpec((tm, nz), lambda i: (i, 0)), pl.BlockSpec((tm, LANES), lambda i: (i, 0)),
                   pl.BlockSpec((tm, D_MODEL), lambda i: (i, 0))],
        out_shape=[jax.ShapeDtypeStruct((S, nz), BF16), jax.ShapeDtypeStruct((S, LANES), F32),
                   jax.ShapeDtypeStruct((S, D_MODEL), BF16)],
        compiler_params=_params(("parallel",)),
    )(x, g1, w_in)


def _fox_prep(f, bias_row, tb):
    S = f.shape[0]

    def body(f_ref, b_ref, c_ref, carry):
        @pl.when(pl.program_id(0) == 0)
        def _():
            carry[...] = jnp.zeros_like(carry)

        xv = f_ref[...] + b_ref[...]
        lf = jnp.minimum(xv, 0.0) - jnp.log(1.0 + jnp.exp(-jnp.abs(xv)))
        r = lax.broadcasted_iota(jnp.int32, (tb, tb), 0)
        s = lax.broadcasted_iota(jnp.int32, (tb, tb), 1)
        tri = (r >= s).astype(F32)
        cs = jnp.dot(tri, lf, precision=lax.Precision.HIGHEST, preferred_element_type=F32) + carry[0:1, :]
        c_ref[...] = cs
        carry[...] = jnp.broadcast_to(cs[tb - 1:tb, :], carry.shape)

    return pl.pallas_call(
        body, name="fox_prep", grid=(S // tb,),
        in_specs=[pl.BlockSpec((tb, LANES), lambda i: (i, 0)), _full((1, LANES))],
        out_specs=pl.BlockSpec((tb, LANES), lambda i: (i, 0)),
        out_shape=jax.ShapeDtypeStruct((S, LANES), F32),
        scratch_shapes=[pltpu.VMEM((SUBLANES, LANES), F32)],
        compiler_params=_params(("arbitrary",)),
    )(f, bias_row)


def _attn_consts():
    col = jnp.arange(D_PAD)
    row = jnp.arange(D_HEADS)
    head = jnp.arange(LANES)
    place = (row[:, None] // HEAD_DIM == col[None, :] // HEAD_PAD) & (row[:, None] % HEAD_DIM == col[None, :] % HEAD_PAD)

    def stat(offset):
        return ((head[:, None] < N_HEADS) & (col[None, :] == head[:, None] * HEAD_PAD + offset)).astype(BF16)

    def ones(offsets):
        return sum((col % HEAD_PAD == o) for o in offsets).astype(F32).reshape(1, D_PAD)

    def pick(offset):
        gcol = jnp.arange(GROUP_PAD)
        return jnp.stack([((gcol[:, None] % HEAD_PAD == offset) & (head[None, :] == g * GROUP_HEADS + gcol[:, None] // HEAD_PAD))
                          for g in range(GROUPS)]).astype(BF16)

    place = place.astype(BF16)
    return {
        "place": place, "place_t": place.T, "place_t_group": place.T[:GROUP_PAD, :GROUP_HEADS * HEAD_DIM],
        "q_stat": jnp.stack([stat(Q_STAT + j) for j in range(3)]), "k_stat": jnp.stack([stat(K_STAT + j) for j in range(3)]),
        "d_stat": jnp.stack([stat(Q_STAT + j) for j in range(2)]), "l_stat": jnp.stack([stat(L_STAT + j)[:STAT_ROWS] for j in range(3)]),
        "q_ones": ones(range(K_STAT, K_STAT + 3)), "k_ones": ones(list(range(Q_STAT, Q_STAT + 3)) + list(range(L_STAT, L_STAT + 3))),
        "v_ones": ones(range(Q_STAT, Q_STAT + 2)),
        "pick_rows": pick(Q_STAT), "pick_cols": pick(K_STAT),
    }


def _split3(x):
    hi = x.astype(BF16)
    r = x - hi.astype(F32)
    mid = r.astype(BF16)
    return hi, mid, (r - mid.astype(F32)).astype(BF16)


def _split3_dot(x, m):
    return sum(jnp.dot(part, m, preferred_element_type=F32) for part in _split3(x))


def _attn_pack(z, c, k, tm):
    S = z.shape[0]

    def body(q_ref, k_ref, v_ref, c_ref, pl_ref, pt_ref, qs_ref, ks_ref, qo_ref, ko_ref, vo_ref, voc_ref,
             qa_ref, ka_ref, va_ref, vt_ref):
        place = pl_ref[...]
        q = (q_ref[...].astype(F32) * (SCALE * LOG2E)).astype(BF16)
        qa = jnp.dot(q, place, preferred_element_type=F32) + qo_ref[...]
        ka = jnp.dot(k_ref[...], place, preferred_element_type=F32) + ko_ref[...]
        for j, part in enumerate(_split3(c_ref[...] * LOG2E)):
            qa = qa + jnp.dot(part, qs_ref[j], preferred_element_type=F32)
            ka = ka - jnp.dot(part, ks_ref[j], preferred_element_type=F32)
        qa_ref[...] = qa.astype(BF16)
        ka_ref[...] = ka.astype(BF16)
        v = v_ref[...]
        va_ref[...] = (jnp.dot(v, place, preferred_element_type=F32) + vo_ref[...]).astype(BF16)
        vt_ref[...] = (lax.dot_general(pt_ref[...], v, NT, preferred_element_type=F32) + voc_ref[...]).astype(BF16)

    blk = lambda col: pl.BlockSpec((tm, D_HEADS), lambda i: (i, col))
    out = pl.BlockSpec((tm, D_PAD), lambda i: (i, 0))
    pad = jax.ShapeDtypeStruct((S, D_PAD), BF16)
    return pl.pallas_call(
        body, name="attn_pack", grid=(S // tm,),
        in_specs=[blk(2), blk(3), blk(4), pl.BlockSpec((tm, LANES), lambda i: (i, 0)), _full((D_HEADS, D_PAD)), _full((D_PAD, D_HEADS)),
                  _full((3, LANES, D_PAD)), _full((3, LANES, D_PAD)), _full((1, D_PAD)), _full((1, D_PAD)), _full((1, D_PAD)),
                  _full((D_PAD, 1))],
        out_specs=[out, out, out, pl.BlockSpec((D_PAD, tm), lambda i: (0, i))],
        out_shape=[pad, pad, pad, jax.ShapeDtypeStruct((D_PAD, S), BF16)],
        compiler_params=_params(("parallel",)),
    )(z, z, z, c, k["place"], k["place_t"], k["q_stat"], k["k_stat"], k["q_ones"], k["k_ones"], k["v_ones"], k["v_ones"].T)


def _attn_fwd(qa, ka, vat, place_t, tq):
    S = qa.shape[0]
    n = S // tq

    def body(q_ref, k_ref, vt_ref, pt_ref, o_ref, lse_ref, m_s, acc_s, ot_s, s_s):
        qi, ki = pl.program_id(0), pl.program_id(1)

        @pl.when(ki == 0)
        def _():
            m_s[...] = jnp.full_like(m_s, NEG)
            acc_s[...] = jnp.zeros_like(acc_s)

        def step(diagonal):
            chunks = [slice(c * KEY_CHUNK, (c + 1) * KEY_CHUNK) for c in range(tq // KEY_CHUNK)]

            def scores(h, rows, slot):
                sl = slice(h * HEAD_PAD, (h + 1) * HEAD_PAD)
                st = lax.dot_general(k_ref[rows, sl], q_ref[:, sl], NT, preferred_element_type=F32)
                if diagonal:
                    key = rows.start + lax.broadcasted_iota(jnp.int32, (KEY_CHUNK, tq), 0)
                    query = lax.broadcasted_iota(jnp.int32, (KEY_CHUNK, tq), 1)
                    st = jnp.where(query >= key, st, NEG)
                s_s[slot, rows, :] = st
                return jnp.max(st, axis=0, keepdims=True)

            m_cur = functools.reduce(jnp.maximum, [scores(0, rows, 0) for rows in chunks])
            for h in range(N_HEADS):
                sl = slice(h * HEAD_PAD, (h + 1) * HEAD_PAD)
                slot = h % 2
                m_prev = m_s[h][0:1, :]
                m_new = jnp.maximum(m_prev, m_cur)
                acc = jnp.exp2(m_prev - m_new) * acc_s[h]
                m_next = []
                for rows in chunks:
                    if h + 1 < N_HEADS:
                        m_next.append(scores(h + 1, rows, 1 - slot))
                    pt = jnp.exp2(s_s[slot, rows, :] - m_new).astype(BF16)
                    acc = acc + jnp.dot(vt_ref[sl, rows], pt, preferred_element_type=F32)
                acc_s[h] = acc
                m_s[h] = jnp.broadcast_to(m_new, (SUBLANES, tq))
                if m_next:
                    m_cur = functools.reduce(jnp.maximum, m_next)

        @pl.when(ki < qi)
        def _():
            step(False)

        @pl.when(ki == qi)
        def _():
            step(True)
            lse_ref[...] = jnp.zeros_like(lse_ref)
            for h in range(N_HEADS):
                acc = acc_s[h]
                denom = acc[Q_STAT:Q_STAT + 1, :]
                ot_s[h * HEAD_PAD:(h + 1) * HEAD_PAD, :] = (acc / denom).astype(BF16)
                lse_ref[h:h + 1, :] = m_s[h][0:1, :] + jnp.log(denom) * LOG2E
            o_ref[...] = lax.dot_general(ot_s[...], pt_ref[...], TN, preferred_element_type=F32).astype(BF16)

    kmin = lambda qi, ki: jnp.minimum(ki, qi)
    return pl.pallas_call(
        body, name="attn_fwd", grid=(n, n),
        in_specs=[pl.BlockSpec((tq, D_PAD), lambda qi, ki: (qi, 0)), pl.BlockSpec((tq, D_PAD), lambda qi, ki: (kmin(qi, ki), 0)),
                  pl.BlockSpec((D_PAD, tq), lambda qi, ki: (0, kmin(qi, ki))), _full((D_PAD, D_HEADS))],
        out_specs=[pl.BlockSpec((tq, D_HEADS), lambda qi, ki: (qi, 0)), pl.BlockSpec((STAT_ROWS, tq), lambda qi, ki: (0, qi))],
        out_shape=[jax.ShapeDtypeStruct((S, D_HEADS), BF16), jax.ShapeDtypeStruct((STAT_ROWS, S), F32)],
        scratch_shapes=[pltpu.VMEM((N_HEADS, SUBLANES, tq), F32), pltpu.VMEM((N_HEADS, HEAD_PAD, tq), F32),
                        pltpu.VMEM((D_PAD, tq), BF16), pltpu.VMEM((2, tq, tq), F32)],
        compiler_params=_params(("parallel", "arbitrary")),
    )(qa, ka, vat, place_t)


def _layer_norm_heads(v, seg_avg):
    mu = _split_dot(v, seg_avg)
    d = v - mu
    var = _split_dot(d * d, seg_avg)
    rstd = lax.rsqrt(var + EPS)
    return d * rstd, rstd


def _gate_mix(vn_blk, w_ref, bias):
    acc = bias
    for h in range(N_HEADS):
        vh = jnp.where(_head_mask(h, SG_BLOCK), vn_blk, 0.0).astype(BF16)
        acc = acc + jnp.dot(w_ref[h], vh, preferred_element_type=F32)
    return acc


def _gate_fwd(z, w_mask, ln_row, b_full, seg_avg, tm):
    S = z.shape[0]

    def body(zu_ref, zv_ref, w_ref, ln_ref, b_ref, avg_ref, o_ref):
        u = _gelu(zu_ref[...].astype(F32))
        v = _gelu(zv_ref[...].astype(F32))
        vhat, _ = _layer_norm_heads(v, avg_ref[...])
        vn = vhat * ln_ref[...]
        for b in range(tm // SG_BLOCK):
            rows = slice(b * SG_BLOCK, (b + 1) * SG_BLOCK)
            mixed = _gate_mix(vn[rows], w_ref, b_ref[...])
            o_ref[rows, :] = (u[rows] * mixed).astype(BF16)

    return pl.pallas_call(
        body, name="gate_fwd", grid=(S // tm,),
        in_specs=[pl.BlockSpec((tm, D_HEADS), lambda i: (i, 0)), pl.BlockSpec((tm, D_HEADS), lambda i: (i, 1)),
                  _full((N_HEADS, SG_BLOCK, SG_BLOCK)), _full((1, D_HEADS)), _full((SG_BLOCK, D_HEADS)),
                  _full((D_HEADS, D_HEADS))],
        out_specs=pl.BlockSpec((tm, D_HEADS), lambda i: (i, 0)),
        out_shape=jax.ShapeDtypeStruct((S, D_HEADS), BF16),
        compiler_params=_params(("parallel",)),
    )(z, z, w_mask, ln_row, b_full, seg_avg)


def _mix_out(x, out_a, out_b, w_out, g2, tm):
    S = x.shape[0]

    def body(x_ref, a_ref, b_ref, w_ref, g_ref, x1_ref, h_ref):
        y = jnp.dot(a_ref[...], w_ref[:D_HEADS, :], preferred_element_type=F32)
        y = y + jnp.dot(b_ref[...], w_ref[D_HEADS:, :], preferred_element_type=F32)
        x1 = x_ref[...] + y
        x1_ref[...] = x1
        r = lax.rsqrt(jnp.mean(x1 * x1, axis=-1, keepdims=True) + EPS)
        h_ref[...] = (x1 * r * g_ref[...]).astype(BF16)

    row = lambda w: pl.BlockSpec((tm, w), lambda i: (i, 0))
    return pl.pallas_call(
        body, name="mix_out", grid=(S // tm,),
        in_specs=[row(D_MODEL), row(D_HEADS), row(D_HEADS), _full((D_MODEL, D_MODEL)), _full((1, D_MODEL))],
        out_specs=[row(D_MODEL), row(D_MODEL)],
        out_shape=[jax.ShapeDtypeStruct((S, D_MODEL), F32), jax.ShapeDtypeStruct((S, D_MODEL), BF16)],
        compiler_params=_params(("parallel",)),
    )(x, out_a, out_b, w_out, g2)


def _up_proj(h2, w_up_q, tm):
    S = h2.shape[0]
    nq, _, wq = w_up_q.shape

    def body(h_ref, w_ref, a_ref):
        a_ref[...] = jnp.dot(h_ref[...], w_ref[...], preferred_element_type=F32).astype(BF16)

    return pl.pallas_call(
        body, name="up_proj", grid=(nq, S // tm),
        in_specs=[pl.BlockSpec((tm, D_MODEL), lambda j, i: (i, 0)), pl.BlockSpec((None, D_MODEL, wq), lambda j, i: (j, 0, 0))],
        out_specs=pl.BlockSpec((tm, wq), lambda j, i: (i, j)),
        out_shape=jax.ShapeDtypeStruct((S, nq * wq), BF16),
        compiler_params=_params(("parallel", "parallel")),
    )(h2, w_up_q)


def _shifted(a, halo, k):
    tm = a.shape[0]
    row = lax.broadcasted_iota(jnp.int32, (tm, tm), 0)
    col = lax.broadcasted_iota(jnp.int32, (tm, tm), 1)
    row8 = lax.broadcasted_iota(jnp.int32, halo.shape, 0)
    out = jnp.dot((row == col + k).astype(BF16), a, preferred_element_type=F32)
    if k > 0:
        top = out[0:SUBLANES] + jnp.where(row8 < k, pltpu.roll(halo, k, 0), 0.0)
        return jnp.concatenate([top, out[SUBLANES:tm]], axis=0)
    bottom = out[tm - SUBLANES:tm] + jnp.where(row8 >= SUBLANES + k, pltpu.roll(halo, SUBLANES + k, 0), 0.0)
    return jnp.concatenate([out[0:tm - SUBLANES], bottom], axis=0)


def _conv_taps(a_ref, halo_ref, first):
    a = a_ref[...]
    halo = halo_ref[...].astype(F32) * jnp.where(first, 0.0, 1.0)
    return a.astype(F32), _shifted(a, halo, 1), _shifted(a, halo, 2)


def _conv_specs(tm):
    step = tm // SUBLANES
    prev = lambda i: jnp.maximum(i * step - 1, 0)
    return [pl.BlockSpec((tm, D_FF), lambda i: (i, 0)), pl.BlockSpec((tm, D_FF), lambda i: (i, 1)),
            pl.BlockSpec((SUBLANES, D_FF), lambda i: (prev(i), 0)), pl.BlockSpec((SUBLANES, D_FF), lambda i: (prev(i), 1))]


def _ffn_fwd_loss(a, w_conv, b_conv, w_down, x1, g3, target, tm):
    S = x1.shape[0]

    def body(ag_ref, av_ref, hg_ref, hv_ref, wg_ref, wv_ref, bg_ref, bv_ref, wd_ref, x1_ref, g_ref, t_ref,
             dx2_ref, loss_ref, dg_ref):
        i = pl.program_id(0)

        @pl.when(i == 0)
        def _():
            loss_ref[...] = jnp.zeros_like(loss_ref)
            dg_ref[...] = jnp.zeros_like(dg_ref)

        g0, g1, g2 = _conv_taps(ag_ref, hg_ref, i == 0)
        gate = wg_ref[2:3, :] * g0 + wg_ref[1:2, :] * g1 + wg_ref[0:1, :] * g2 + bg_ref[...]
        v0, v1, v2 = _conv_taps(av_ref, hv_ref, i == 0)
        val = wv_ref[2:3, :] * v0 + wv_ref[1:2, :] * v1 + wv_ref[0:1, :] * v2 + bv_ref[...]
        y = (gate * _sigmoid(gate) * val).astype(BF16)
        x2 = x1_ref[...] + jnp.dot(y, wd_ref[...], preferred_element_type=F32)
        r = lax.rsqrt(jnp.mean(x2 * x2, axis=-1, keepdims=True) + EPS)
        xhat = x2 * r
        gg = g_ref[...]
        err = xhat * gg - t_ref[...]
        loss_ref[...] += jnp.sum(err * err, axis=0, keepdims=True)
        dy = err * (1.0 / D_MODEL)
        dg_ref[...] += jnp.sum(dy * xhat, axis=0, keepdims=True)
        dxhat = dy * gg
        dx2_ref[...] = r * (dxhat - xhat * jnp.mean(dxhat * xhat, axis=-1, keepdims=True))

    row = lambda w: pl.BlockSpec((tm, w), lambda i: (i, 0))
    half = lambda r: [pl.BlockSpec((r, D_FF), lambda i: (0, 0)), pl.BlockSpec((r, D_FF), lambda i: (0, 1))]
    return pl.pallas_call(
        body, name="ffn_fwd_loss", grid=(S // tm,),
        in_specs=_conv_specs(tm) + half(3) + half(1) + [_full((D_FF, D_MODEL)), row(D_MODEL), _full((1, D_MODEL)), row(D_MODEL)],
        out_specs=[row(D_MODEL), _full((1, D_MODEL)), _full((1, D_MODEL))],
        out_shape=[jax.ShapeDtypeStruct((S, D_MODEL), F32), jax.ShapeDtypeStruct((1, D_MODEL), F32),
                   jax.ShapeDtypeStruct((1, D_MODEL), F32)],
        compiler_params=_params(("arbitrary",)),
    )(a, a, a, a, w_conv, w_conv, b_conv, b_conv, w_down, x1, g3, target)


def _ffn_bwd_gate(dx2, a, w_conv, b_conv, w_down, tm):
    S = dx2.shape[0]

    def body(dx_ref, ag_ref, av_ref, hg_ref, hv_ref, wg_ref, wv_ref, bg_ref, bv_ref, wd_ref,
             dc_ref, y_ref, dw_ref, db_ref):
        i = pl.program_id(0)

        @pl.when(i == 0)
        def _():
            dw_ref[...] = jnp.zeros_like(dw_ref)
            db_ref[...] = jnp.zeros_like(db_ref)

        g0, g1, g2 = _conv_taps(ag_ref, hg_ref, i == 0)
        gate = wg_ref[2:3, :] * g0 + wg_ref[1:2, :] * g1 + wg_ref[0:1, :] * g2 + bg_ref[...]
        v0, v1, v2 = _conv_taps(av_ref, hv_ref, i == 0)
        val = wv_ref[2:3, :] * v0 + wv_ref[1:2, :] * v1 + wv_ref[0:1, :] * v2 + bv_ref[...]
        sg = _sigmoid(gate)
        act = gate * sg
        y_ref[...] = (act * val).astype(BF16)
        dy = lax.dot_general(dx_ref[...].astype(BF16), wd_ref[...], NT, preferred_element_type=F32)
        dgate = dy * val * (sg + act - act * sg)
        dval = dy * act
        dc_ref[:, :D_FF] = dgate.astype(BF16)
        dc_ref[:, D_FF:] = dval.astype(BF16)
        for half, (d, taps) in enumerate(((dgate, (g2, g1, g0)), (dval, (v2, v1, v0)))):
            cols = slice(half * D_FF, (half + 1) * D_FF)
            db_ref[0:1, cols] += jnp.sum(d, axis=0, keepdims=True)
            for j in range(3):
                dw_ref[j:j + 1, cols] += jnp.sum(d * taps[j], axis=0, keepdims=True)

    row = lambda w: pl.BlockSpec((tm, w), lambda i: (i, 0))
    half = lambda r: [pl.BlockSpec((r, D_FF), lambda i: (0, 0)), pl.BlockSpec((r, D_FF), lambda i: (0, 1))]
    return pl.pallas_call(
        body, name="ffn_bwd_gate", grid=(S // tm,),
        in_specs=[row(D_MODEL)] + _conv_specs(tm) + half(3) + half(1) + [_full((D_FF, D_MODEL))],
        out_specs=[row(2 * D_FF), row(D_FF), _full((SUBLANES, 2 * D_FF)), _full((1, 2 * D_FF))],
        out_shape=[jax.ShapeDtypeStruct((S, 2 * D_FF), BF16), jax.ShapeDtypeStruct((S, D_FF), BF16),
                   jax.ShapeDtypeStruct((SUBLANES, 2 * D_FF), F32), jax.ShapeDtypeStruct((1, 2 * D_FF), F32)],
        compiler_params=_params(("arbitrary",)),
    )(dx2, a, a, a, a, w_conv, w_conv, b_conv, b_conv, w_down)


def _conv_bwd(dc, w_conv, tm, tn):
    S, C = dc.shape
    step = tm // SUBLANES
    last_blk = S // SUBLANES - 1

    def body(d_ref, nx_ref, w_ref, o_ref):
        last = pl.program_id(0) == pl.num_programs(0) - 1
        d = d_ref[...]
        nx = nx_ref[...].astype(F32) * jnp.where(last, 0.0, 1.0)
        out = w_ref[2:3, :] * d.astype(F32) + w_ref[1:2, :] * _shifted(d, nx, -1) + w_ref[0:1, :] * _shifted(d, nx, -2)
        o_ref[...] = out.astype(BF16)

    return pl.pallas_call(
        body, name="conv_bwd", grid=(S // tm, C // tn),
        in_specs=[pl.BlockSpec((tm, tn), lambda i, j: (i, j)),
                  pl.BlockSpec((SUBLANES, tn), lambda i, j: (jnp.minimum((i + 1) * step, last_blk), j)),
                  pl.BlockSpec((3, tn), lambda i, j: (0, j))],
        out_specs=pl.BlockSpec((tm, tn), lambda i, j: (i, j)),
        out_shape=jax.ShapeDtypeStruct((S, C), BF16),
        compiler_params=_params(("parallel", "parallel")),
    )(dc, dc, w_conv)


def _matmul_tn(a, b, name, bm, bn, tk, col_a=0, col_b=0, quarters=None):
    S = a.shape[0]
    gm, gn = quarters if quarters else (1, 1)
    nk = S // tk

    def body(a_ref, b_ref, o_ref):
        @pl.when(pl.program_id(2) == 0)
        def _():
            o_ref[...] = jnp.zeros_like(o_ref)

        o_ref[...] += lax.dot_general(a_ref[...].astype(BF16), b_ref[...].astype(BF16), TN, preferred_element_type=F32)

    if quarters and gn > 1:
        out_spec = pl.BlockSpec((None, bm, bn), lambda i, j, k: (j, i, 0))
        out_shape = jax.ShapeDtypeStruct((gn, gm * bm, bn), F32)
    else:
        out_spec = pl.BlockSpec((bm, bn), lambda i, j, k: (i, j))
        out_shape = jax.ShapeDtypeStruct((gm * bm, gn * bn), F32)
    return pl.pallas_call(
        body, name=name, grid=(gm, gn, nk),
        in_specs=[pl.BlockSpec((tk, bm), lambda i, j, k: (k, col_a * gm + i)),
                  pl.BlockSpec((tk, bn), lambda i, j, k: (k, col_b * gn + j))],
        out_specs=out_spec, out_shape=out_shape,
        compiler_params=_params(("parallel", "parallel", "arbitrary")),
    )(a, b)


def _up_bwd(dact, w_up_q, x1, g2, dx2, tm):
    S = x1.shape[0]
    nq, _, wq = w_up_q.shape

    def body(d_ref, w_ref, x_ref, g_ref, dx2_ref, dx1_ref, dg_ref):
        @pl.when(pl.program_id(0) == 0)
        def _():
            dg_ref[...] = jnp.zeros_like(dg_ref)

        dh = jnp.zeros((tm, D_MODEL), F32)
        for j in range(nq):
            dh = dh + lax.dot_general(d_ref[:, j * wq:(j + 1) * wq], w_ref[j], NT, preferred_element_type=F32)
        dx, dg = _rms_bwd(dh, x_ref[...], g_ref[...])
        dg_ref[...] += dg
        dx1_ref[...] = dx2_ref[...] + dx

    row = lambda w: pl.BlockSpec((tm, w), lambda i: (i, 0))
    return pl.pallas_call(
        body, name="up_bwd", grid=(S // tm,),
        in_specs=[row(nq * wq), pl.BlockSpec((nq, D_MODEL, wq), lambda i: (0, 0, 0), pipeline_mode=pl.Buffered(1)),
                  row(D_MODEL), _full((1, D_MODEL)), row(D_MODEL)],
        out_specs=[row(D_MODEL), _full((1, D_MODEL))],
        out_shape=[jax.ShapeDtypeStruct((S, D_MODEL), F32), jax.ShapeDtypeStruct((1, D_MODEL), F32)],
        compiler_params=_params(("arbitrary",)),
    )(dact, w_up_q, x1, g2, dx2)


def _out_bwd(dx1, w_out, tm):
    S = dx1.shape[0]

    def body(d_ref, w_ref, o_ref):
        o_ref[...] = lax.dot_general(d_ref[...].astype(BF16), w_ref[...], NT, preferred_element_type=F32).astype(BF16)

    return pl.pallas_call(
        body, name="out_bwd", grid=(S // tm,),
        in_specs=[pl.BlockSpec((tm, D_MODEL), lambda i: (i, 0)), _full((D_MODEL, D_MODEL))],
        out_specs=pl.BlockSpec((tm, D_MODEL), lambda i: (i, 0)),
        out_shape=jax.ShapeDtypeStruct((S, D_MODEL), BF16),
        compiler_params=_params(("parallel",)),
    )(dx1, w_out)


def _gate_bwd(z, dcat, w_mask, w_mask_t, ln_row, b_full, seg_avg, head_ind, tm):
    S = z.shape[0]
    nb = tm // SG_BLOCK

    def body(zu_ref, zv_ref, do_ref, w_ref, wt_ref, ln_ref, b_ref, avg_ref, ind_ref,
             dzu_ref, dzv_ref, dw_ref, db_ref, dln_ref, dvn_s, dbf_s):
        i = pl.program_id(0)

        @pl.when(i == 0)
        def _():
            dw_ref[...] = jnp.zeros_like(dw_ref)
            dln_ref[...] = jnp.zeros_like(dln_ref)
            dbf_s[...] = jnp.zeros_like(dbf_s)

        zu = zu_ref[...].astype(F32)
        zv = zv_ref[...].astype(F32)
        u = _gelu(zu)
        v = _gelu(zv)
        avg = avg_ref[...]
        vhat, rstd = _layer_norm_heads(v, avg)
        ln = ln_ref[...]
        vn = vhat * ln
        for b in range(nb):
            rows = slice(b * SG_BLOCK, (b + 1) * SG_BLOCK)
            vn_b = vn[rows]
            mixed = _gate_mix(vn_b, w_ref, b_ref[...])
            do = do_ref[rows, :].astype(F32)
            dzu_ref[rows, :] = (do * mixed * _gelu_grad(zu[rows])).astype(BF16)
            dmix = do * u[rows]
            dbf_s[...] += dmix
            vn_bf = vn_b.astype(BF16)
            dvn = jnp.zeros((SG_BLOCK, D_HEADS), F32)
            for h in range(N_HEADS):
                dmh = jnp.where(_head_mask(h, SG_BLOCK), dmix, 0.0).astype(BF16)
                dw_ref[h] += lax.dot_general(dmh, vn_bf, NT, preferred_element_type=F32)
                dvn = dvn + jnp.dot(wt_ref[h], dmh, preferred_element_type=F32)
            dvn_s[rows, :] = dvn
        dvn = dvn_s[...]
        dln_ref[...] += jnp.sum(dvn * vhat, axis=0, keepdims=True)
        dvhat = dvn * ln
        dv = rstd * (dvhat - _split_dot(dvhat, avg) - vhat * _split_dot(dvhat * vhat, avg))
        dzv_ref[...] = (dv * _gelu_grad(zv)).astype(BF16)

        @pl.when(i == pl.num_programs(0) - 1)
        def _():
            r = lax.broadcasted_iota(jnp.int32, (SG_BLOCK, SG_BLOCK), 0) // CHUNK
            s = lax.broadcasted_iota(jnp.int32, (SG_BLOCK, SG_BLOCK), 1) // CHUNK
            for h in range(N_HEADS):
                dw_ref[h] = jnp.where(r >= s, dw_ref[h], 0.0)
            db_ref[...] = _split_dot(dbf_s[...], ind_ref[...])

    row = lambda col: pl.BlockSpec((tm, D_HEADS), lambda i: (i, col))
    wspec = _full((N_HEADS, SG_BLOCK, SG_BLOCK))
    return pl.pallas_call(
        body, name="gate_bwd", grid=(S // tm,),
        in_specs=[row(0), row(1), row(0), wspec, wspec, _full((1, D_HEADS)), _full((SG_BLOCK, D_HEADS)),
                  _full((D_HEADS, D_HEADS)), _full((D_HEADS, LANES))],
        out_specs=[row(0), row(0), wspec, _full((SG_BLOCK, LANES)), _full((1, D_HEADS))],
        out_shape=[jax.ShapeDtypeStruct((S, D_HEADS), BF16), jax.ShapeDtypeStruct((S, D_HEADS), BF16),
                   jax.ShapeDtypeStruct((N_HEADS, SG_BLOCK, SG_BLOCK), F32), jax.ShapeDtypeStruct((SG_BLOCK, LANES), F32),
                   jax.ShapeDtypeStruct((1, D_HEADS), F32)],
        scratch_shapes=[pltpu.VMEM((tm, D_HEADS), F32), pltpu.VMEM((SG_BLOCK, D_HEADS), F32)],
        compiler_params=_params(("arbitrary",)),
    )(z, z, dcat, w_mask, w_mask_t, ln_row, b_full, seg_avg, head_ind)


def _attn_pack_grad(o, dcat, qa, lse, head_ind, k, tm):
    S = o.shape[0]

    def body(o_ref, do_ref, qa_ref, lse_ref, ind_ref, pl_ref, pt_ref, eye_ref, ds_ref, dst_ref, ls_ref, lst_ref,
             dop_ref, qb_ref, dot_ref, qbt_ref):
        do = do_ref[...]
        delta = _split_dot(o_ref[...].astype(F32) * do.astype(F32), ind_ref[...])
        hi = delta.astype(BF16)
        lo = (delta - hi.astype(F32)).astype(BF16)
        dop = jnp.dot(do, pl_ref[...], preferred_element_type=F32)
        dop = dop - jnp.dot(hi, ds_ref[0], preferred_element_type=F32) - jnp.dot(lo, ds_ref[1], preferred_element_type=F32)
        dop_ref[...] = dop.astype(BF16)
        dot = lax.dot_general(pt_ref[...], do, NT, preferred_element_type=F32)
        dot = dot - lax.dot_general(dst_ref[0], hi, NT, preferred_element_type=F32)
        dot = dot - lax.dot_general(dst_ref[1], lo, NT, preferred_element_type=F32)
        dot_ref[...] = dot.astype(BF16)
        qa = qa_ref[...]
        qb = qa.astype(F32)
        qbt = lax.dot_general(eye_ref[...], qa, NT, preferred_element_type=F32)
        for j, part in enumerate(_split3(lse_ref[...])):
            qb = qb - lax.dot_general(part, ls_ref[j], TN, preferred_element_type=F32)
            qbt = qbt - jnp.dot(lst_ref[j], part, preferred_element_type=F32)
        qb_ref[...] = qb.astype(BF16)
        qbt_ref[...] = qbt.astype(BF16)

    pad = pl.BlockSpec((tm, D_PAD), lambda i: (i, 0))
    padt = pl.BlockSpec((D_PAD, tm), lambda i: (0, i))
    return pl.pallas_call(
        body, name="attn_pack_grad", grid=(S // tm,),
        in_specs=[pl.BlockSpec((tm, D_HEADS), lambda i: (i, 0)), pl.BlockSpec((tm, D_HEADS), lambda i: (i, 1)), pad,
                  pl.BlockSpec((STAT_ROWS, tm), lambda i: (0, i)), _full((D_HEADS, LANES)), _full((D_HEADS, D_PAD)),
                  _full((D_PAD, D_HEADS)), _full((D_PAD, D_PAD)), _full((2, LANES, D_PAD)), _full((2, D_PAD, LANES)),
                  _full((3, STAT_ROWS, D_PAD)), _full((3, D_PAD, STAT_ROWS))],
        out_specs=[pad, pad, padt, padt],
        out_shape=[jax.ShapeDtypeStruct((S, D_PAD), BF16)] * 2 + [jax.ShapeDtypeStruct((D_PAD, S), BF16)] * 2,
        compiler_params=_params(("parallel",)),
    )(o, dcat, qa, lse, head_ind, k["place"], k["place_t"], jnp.eye(D_PAD, dtype=BF16), k["d_stat"],
      jnp.swapaxes(k["d_stat"], 1, 2), k["l_stat"], jnp.swapaxes(k["l_stat"], 1, 2))


def _attn_bwd(qb, qbt, ka, va, dop, dopt, k, tq):
    S = qb.shape[0]
    n = S // tq

    def body(q_ref, qt_ref, k_ref, v_ref, do_ref, dot_ref, pt_ref, pick_ref, dq_hbm, dk_ref, dv_ref, dcc_ref, dq_s, dk_s, dv_s, s_s, d_s, sem):
        g, ki, qi = pl.program_id(0), pl.program_id(1), pl.program_id(2)

        @pl.when((ki == 0) & (qi == 0))
        def _():
            dq_s[...] = jnp.zeros_like(dq_s)

        @pl.when(qi == ki)
        def _():
            dk_s[...] = jnp.zeros_like(dk_s)
            dv_s[...] = jnp.zeros_like(dv_s)

        def step(diagonal):
            chunks = [slice(c * KEY_CHUNK, (c + 1) * KEY_CHUNK) for c in range(tq // KEY_CHUNK)]

            def scores(hh, rows, slot):
                sl = slice(hh * HEAD_PAD, (hh + 1) * HEAD_PAD)
                s_s[slot, rows, :] = lax.dot_general(q_ref[rows, sl], k_ref[:, sl], NT, preferred_element_type=F32)
                d_s[slot, rows, :] = lax.dot_general(do_ref[rows, sl], v_ref[:, sl], NT, preferred_element_type=F32)

            for rows in chunks:
                scores(0, rows, 0)
            for hh in range(GROUP_HEADS):
                sl = slice(hh * HEAD_PAD, (hh + 1) * HEAD_PAD)
                slot = hh % 2
                dv, dk = dv_s[sl, :], dk_s[sl, :]
                for rows in chunks:
                    if hh + 1 < GROUP_HEADS:
                        scores(hh + 1, rows, 1 - slot)
                    p = jnp.exp2(s_s[slot, rows, :])
                    if diagonal:
                        row = rows.start + lax.broadcasted_iota(jnp.int32, (KEY_CHUNK, tq), 0)
                        col = lax.broadcasted_iota(jnp.int32, (KEY_CHUNK, tq), 1)
                        p = jnp.where(row >= col, p, 0.0)
                    ds = (p * d_s[slot, rows, :]).astype(BF16)
                    dv = dv + jnp.dot(dot_ref[sl, rows], p.astype(BF16), preferred_element_type=F32)
                    dk = dk + jnp.dot(qt_ref[sl, rows], ds, preferred_element_type=F32)
                    qrows = pl.ds(pl.multiple_of(qi * tq + rows.start, KEY_CHUNK), KEY_CHUNK)
                    dq_s[qrows, sl] += jnp.dot(ds, k_ref[:, sl], preferred_element_type=F32)
                dv_s[sl, :] = dv
                dk_s[sl, :] = dk

        @pl.when(qi > ki)
        def _():
            step(False)

        @pl.when(qi == ki)
        def _():
            step(True)

        @pl.when(qi == n - 1)
        def _():
            dk = dk_s[...]
            pt = pt_ref[...]
            dk_ref[...] = lax.dot_general((dk * (1.0 / LOG2E)).astype(BF16), pt, TN, preferred_element_type=F32).astype(BF16)
            dv_ref[...] = lax.dot_general(dv_s[...].astype(BF16), pt, TN, preferred_element_type=F32).astype(BF16)
            dcc_ref[...] = sum(lax.dot_general(part, pick_ref[...], TN, preferred_element_type=F32) for part in _split3(dk))

        @pl.when((ki == n - 1) & (qi == n - 1))
        def _():
            cp = pltpu.make_async_copy(dq_s, dq_hbm.at[g], sem)
            cp.start()
            cp.wait()

    gw = GROUP_HEADS * HEAD_DIM
    qmax = lambda ki, qi: jnp.maximum(qi, ki)
    qspec = pl.BlockSpec((tq, GROUP_PAD), lambda g, ki, qi: (qmax(ki, qi), g))
    qtspec = pl.BlockSpec((GROUP_PAD, tq), lambda g, ki, qi: (g, qmax(ki, qi)))
    kspec = pl.BlockSpec((tq, GROUP_PAD), lambda g, ki, qi: (ki, g))
    kout = pl.BlockSpec((tq, gw), lambda g, ki, qi: (ki, g))
    return pl.pallas_call(
        body, name="attn_bwd", grid=(GROUPS, n, n),
        in_specs=[qspec, qtspec, kspec, kspec, qspec, qtspec,
                  _full((GROUP_PAD, gw)), pl.BlockSpec((None, GROUP_PAD, LANES), lambda g, ki, qi: (g, 0, 0))],
        out_specs=[_ANY, kout, kout, pl.BlockSpec((None, tq, LANES), lambda g, ki, qi: (g, ki, 0))],
        out_shape=[jax.ShapeDtypeStruct((GROUPS, S, GROUP_PAD), F32), jax.ShapeDtypeStruct((S, D_HEADS), BF16),
                   jax.ShapeDtypeStruct((S, D_HEADS), BF16), jax.ShapeDtypeStruct((GROUPS, S, LANES), F32)],
        scratch_shapes=[pltpu.VMEM((S, GROUP_PAD), F32), pltpu.VMEM((GROUP_PAD, tq), F32), pltpu.VMEM((GROUP_PAD, tq), F32),
                        pltpu.VMEM((2, tq, tq), F32), pltpu.VMEM((2, tq, tq), F32), pltpu.SemaphoreType.DMA],
        compiler_params=_params(("arbitrary", "arbitrary", "arbitrary")),
    )(qb, qbt, ka, va, dop, dopt, k["place_t_group"], k["pick_cols"])


def _attn_unpack(dqp, dcc, k, tm):
    S = dqp.shape[1]
    gw = GROUP_HEADS * HEAD_DIM

    def body(dqp_ref, dcc_ref, pt_ref, pick_ref, dq_ref, dc_ref):
        dc = jnp.zeros((tm, LANES), F32)
        for g in range(GROUPS):
            x = dqp_ref[g]
            dq_ref[:, g * gw:(g + 1) * gw] = jnp.dot((x * SCALE).astype(BF16), pt_ref[...], preferred_element_type=F32).astype(BF16)
            dc = dc + _split3_dot(x, pick_ref[g]) - dcc_ref[g]
        dc_ref[...] = dc

    return pl.pallas_call(
        body, name="attn_unpack", grid=(S // tm,),
        in_specs=[pl.BlockSpec((GROUPS, tm, GROUP_PAD), lambda i: (0, i, 0)), pl.BlockSpec((GROUPS, tm, LANES), lambda i: (0, i, 0)),
                  _full((GROUP_PAD, gw)), _full((GROUPS, GROUP_PAD, LANES))],
        out_specs=[pl.BlockSpec((tm, D_HEADS), lambda i: (i, 0)), pl.BlockSpec((tm, LANES), lambda i: (i, 0))],
        out_shape=[jax.ShapeDtypeStruct((S, D_HEADS), BF16), jax.ShapeDtypeStruct((S, LANES), F32)],
        compiler_params=_params(("parallel",)),
    )(dqp, dcc, k["place_t_group"], k["pick_rows"])


def _fox_bwd(dc, f, bias_row, tb):
    S = f.shape[0]
    nb = S // tb

    def body(dc_ref, f_ref, b_ref, df_ref, dbias_ref, carry):
        @pl.when(pl.program_id(0) == 0)
        def _():
            carry[...] = jnp.zeros_like(carry)
            dbias_ref[...] = jnp.zeros_like(dbias_ref)

        r = lax.broadcasted_iota(jnp.int32, (tb, tb), 0)
        s = lax.broadcasted_iota(jnp.int32, (tb, tb), 1)
        tri = (s >= r).astype(F32)
        rc = jnp.dot(tri, dc_ref[...], precision=lax.Precision.HIGHEST, preferred_element_type=F32) + carry[0:1, :]
        carry[...] = jnp.broadcast_to(rc[0:1, :], carry.shape)
        lane = lax.broadcasted_iota(jnp.int32, (tb, LANES), 1)
        df = jnp.where(lane < N_HEADS, rc * jax.nn.sigmoid(-(f_ref[...] + b_ref[...])), 0.0)
        df_ref[...] = df.astype(BF16)
        dbias_ref[...] += jnp.sum(df, axis=0, keepdims=True)

    rev = pl.BlockSpec((tb, LANES), lambda i: (nb - 1 - i, 0))
    return pl.pallas_call(
        body, name="fox_bwd", grid=(nb,),
        in_specs=[rev, rev, _full((1, LANES))],
        out_specs=[rev, _full((1, LANES))],
        out_shape=[jax.ShapeDtypeStruct((S, LANES), BF16), jax.ShapeDtypeStruct((1, LANES), F32)],
        scratch_shapes=[pltpu.VMEM((SUBLANES, LANES), F32)],
        compiler_params=_params(("arbitrary",)),
    )(dc, f, bias_row)


_DZ_WIDTHS = (D_HEADS,) * 5 + (LANES,)


def _in_bwd(pieces, w_in, x, g1, dx1, tm):
    S = x.shape[0]

    def body(*refs):
        p_refs, (w_ref, x_ref, g_ref, dx1_ref, dx_ref, dg_ref) = refs[:6], refs[6:]

        @pl.when(pl.program_id(0) == 0)
        def _():
            dg_ref[...] = jnp.zeros_like(dg_ref)

        dh = jnp.zeros((tm, D_MODEL), F32)
        off = 0
        for p_ref, w in zip(p_refs, _DZ_WIDTHS):
            dh = dh + lax.dot_general(p_ref[...].astype(BF16), w_ref[:, off:off + w], NT, preferred_element_type=F32)
            off += w
        dx, dg = _rms_bwd(dh, x_ref[...], g_ref[...])
        dg_ref[...] += dg
        dx_ref[...] = dx1_ref[...] + dx

    row = lambda w: pl.BlockSpec((tm, w), lambda i: (i, 0))
    return pl.pallas_call(
        body, name="in_bwd", grid=(S // tm,),
        in_specs=[row(w) for w in _DZ_WIDTHS] + [_full((D_MODEL, D_IN_PAD)), row(D_MODEL), _full((1, D_MODEL)), row(D_MODEL)],
        out_specs=[row(D_MODEL), _full((1, D_MODEL))],
        out_shape=[jax.ShapeDtypeStruct((S, D_MODEL), F32), jax.ShapeDtypeStruct((1, D_MODEL), F32)],
        compiler_params=_params(("arbitrary",)),
    )(*pieces, w_in, x, g1, dx1)


def _dw_in(h1, pieces, tk):
    S = h1.shape[0]

    def body(*refs):
        h_ref, p_refs, o_ref = refs[0], refs[1:7], refs[7]

        @pl.when(pl.program_id(0) == 0)
        def _():
            o_ref[...] = jnp.zeros_like(o_ref)

        off = 0
        for p_ref, w in zip(p_refs, _DZ_WIDTHS):
            o_ref[:, off:off + w] += lax.dot_general(h_ref[...], p_ref[...].astype(BF16), TN, preferred_element_type=F32)
            off += w

    row = lambda w: pl.BlockSpec((tk, w), lambda k: (k, 0))
    return pl.pallas_call(
        body, name="dw_in", grid=(S // tk,),
        in_specs=[row(D_MODEL)] + [row(w) for w in _DZ_WIDTHS],
        out_specs=_full((D_MODEL, D_IN_PAD)),
        out_shape=jax.ShapeDtypeStruct((D_MODEL, D_IN_PAD), F32),
        compiler_params=_params(("arbitrary",)),
    )(h1, *pieces)


def _adamw_math(w, g, m, v):
    m = ADAM_B1 * m + (1.0 - ADAM_B1) * g
    v = ADAM_B2 * v + (1.0 - ADAM_B2) * (g * g)
    m_hat = m / (1.0 - ADAM_B1 ** ADAM_STEP)
    v_hat = v / (1.0 - ADAM_B2 ** ADAM_STEP)
    delta = -ADAM_LR * (m_hat / (jnp.sqrt(v_hat) + ADAM_EPS) + ADAM_WD * w)
    return delta, m, v


def _adamw(name, w, g, m, v):
    R, C = w.shape
    tr = _row_tile(R, 256)

    def body(w_ref, g_ref, m_ref, v_ref, d_ref, nm_ref, nv_ref):
        d, nm, nv = _adamw_math(w_ref[...], g_ref[...], m_ref[...], v_ref[...])
        d_ref[...] = d
        nm_ref[...] = nm
        nv_ref[...] = nv

    spec = pl.BlockSpec((tr, C), lambda i: (i, 0))
    return pl.pallas_call(
        body, name=name, grid=(R // tr,), in_specs=[spec] * 4, out_specs=[spec] * 3,
        out_shape=[jax.ShapeDtypeStruct((R, C), F32)] * 3,
        compiler_params=_params(("parallel",)),
    )(w, g, m, v)


def _pair_sum(name, grad, theirs, ids):
    q, half, C = theirs.shape
    tr = _row_tile(half, 256)
    nb = half // tr

    def body(ids_ref, a_ref, b_ref, s_ref, sb_ref):
        s = a_ref[...] + b_ref[...]
        s_ref[...] = s
        sb_ref[...] = s.astype(BF16)

    here = pl.BlockSpec((None, tr, C), lambda j, i, ids: (j, i, 0))
    return pl.pallas_call(
        body, name=name,
        grid_spec=pltpu.PrefetchScalarGridSpec(
            num_scalar_prefetch=1, grid=(q, nb),
            in_specs=[pl.BlockSpec((None, tr, C), lambda j, i, ids: (j, ids[1] * nb + i, 0)), here],
            out_specs=[here, here]),
        out_shape=[jax.ShapeDtypeStruct((q, half, C), F32), jax.ShapeDtypeStruct((q, half, C), BF16)],
        compiler_params=_params(("parallel", "parallel")),
    )(ids, grad, theirs)


def _chip_sum(name, sums32, others, ids):
    _, half, C = sums32.shape
    tr = _row_tile(half, 256)
    nb = half // tr

    def body(ids_ref, a_ref, o_ref, s_ref):
        s = a_ref[...]
        for j in range(3):
            s = s + o_ref[j].astype(F32)
        s_ref[...] = s

    return pl.pallas_call(
        body, name=name,
        grid_spec=pltpu.PrefetchScalarGridSpec(
            num_scalar_prefetch=1, grid=(nb,),
            in_specs=[pl.BlockSpec((None, tr, C), lambda i, ids: (ids[0], i, 0)),
                      pl.BlockSpec((3, tr, C), lambda i, ids: (0, i, 0))],
            out_specs=pl.BlockSpec((tr, C), lambda i, ids: (ids[1] * nb + i, 0))),
        out_shape=jax.ShapeDtypeStruct((2 * half, C), F32),
        compiler_params=_params(("parallel",)),
    )(ids, sums32, others)


def _place():
    return lax.axis_index("x"), lax.axis_index("y"), lax.axis_index("c")


def _other_chips(x, y):
    return [(1 - x, y), (x, 1 - y), (1 - x, 1 - y)]


_ANY = pl.BlockSpec(memory_space=pl.ANY)


def _gather_quarters(shards):
    n = len(shards)
    halved = [s.shape[0] % 32 == 0 for s in shards]

    def body(*refs):
        ins, outs = refs[:n], refs[n:2 * n]
        send_sems, recv_sems, pass_send_sems, pass_recv_sems = refs[2 * n:]
        x, y, c = _place()
        mine = 2 * x + y
        chips = _other_chips(x, y)

        def part(a, quarter, core):
            if not halved[a]:
                return outs[a].at[quarter]
            half = ins[a].shape[0] // 2
            return outs[a].at[quarter, pl.ds(core * half, half), :]

        def source(a):
            if not halved[a]:
                return ins[a]
            half = ins[a].shape[0] // 2
            return ins[a].at[pl.ds(c * half, half), :]

        sends = []
        for a in range(n):
            for j, (px, py) in enumerate(chips):
                cp = pltpu.make_async_remote_copy(src_ref=source(a), dst_ref=part(a, mine, c), send_sem=send_sems.at[a, j],
                                                  recv_sem=recv_sems.at[a, j], device_id=(px, py, c), device_id_type=MESH)
                cp.start()
                sends.append(cp)
        for a in range(n):
            for j, (px, py) in enumerate(chips):
                landed = part(a, 2 * px + py, c)
                pltpu.make_async_remote_copy(src_ref=source(a), dst_ref=landed, send_sem=send_sems.at[a, j],
                                             recv_sem=recv_sems.at[a, j], device_id=(px, py, c), device_id_type=MESH).wait_recv()
                if halved[a]:
                    cp = pltpu.make_async_remote_copy(src_ref=landed, dst_ref=landed, send_sem=pass_send_sems.at[a, j],
                                                      recv_sem=pass_recv_sems.at[a, j], device_id=(x, y, 1 - c), device_id_type=MESH)
                    cp.start()
                    sends.append(cp)
        for a in range(n):
            if halved[a]:
                for j, (px, py) in enumerate(chips):
                    other = part(a, 2 * px + py, 1 - c)
                    pltpu.make_async_remote_copy(src_ref=other, dst_ref=other, send_sem=pass_send_sems.at[a, j],
                                                 recv_sem=pass_recv_sems.at[a, j], device_id=(x, y, 1 - c),
                                                 device_id_type=MESH).wait_recv()
        for cp in sends:
            cp.wait_send()

    sems = pltpu.SemaphoreType.DMA((n, 3))
    return pl.pallas_call(
        body, name="gather_weights",
        in_specs=[_ANY] * n, out_specs=[_ANY] * n,
        out_shape=[jax.ShapeDtypeStruct((4,) + s.shape, s.dtype) for s in shards],
        scratch_shapes=[sems, sems, sems, sems],
    )(*shards)


def _swap_halves(grads):
    n = len(grads)

    def body(*refs):
        ins, outs = refs[:n], refs[n:2 * n]
        send_sems, recv_sems = refs[2 * n:]
        x, y, c = _place()
        started = []
        for a in range(n):
            half = ins[a].shape[1] // 2
            cp = pltpu.make_async_remote_copy(src_ref=ins[a].at[:, pl.ds((1 - c) * half, half), :], dst_ref=outs[a],
                                              send_sem=send_sems.at[a], recv_sem=recv_sems.at[a],
                                              device_id=(x, y, 1 - c), device_id_type=MESH)
            cp.start()
            started.append(cp)
        for cp in started:
            cp.wait()

    return pl.pallas_call(
        body, name="swap_halves",
        in_specs=[_ANY] * n, out_specs=[_ANY] * n,
        out_shape=[jax.ShapeDtypeStruct((4, g.shape[1] // 2, g.shape[2]), F32) for g in grads],
        scratch_shapes=[pltpu.SemaphoreType.DMA((n,)), pltpu.SemaphoreType.DMA((n,))],
    )(*grads)


def _scatter_quarters(sums16):
    n = len(sums16)

    def body(*refs):
        ins, outs = refs[:n], refs[n:2 * n]
        send_sems, recv_sems = refs[2 * n:]
        x, y, c = _place()
        sends = []
        for a in range(n):
            for j, (px, py) in enumerate(_other_chips(x, y)):
                cp = pltpu.make_async_remote_copy(src_ref=ins[a].at[2 * px + py], dst_ref=outs[a].at[j],
                                                  send_sem=send_sems.at[a, j], recv_sem=recv_sems.at[a, j],
                                                  device_id=(px, py, c), device_id_type=MESH)
                cp.start()
                sends.append(cp)
        for cp in sends:
            cp.wait()

    return pl.pallas_call(
        body, name="scatter_quarters",
        in_specs=[_ANY] * n, out_specs=[_ANY] * n,
        out_shape=[jax.ShapeDtypeStruct((3,) + s.shape[1:], BF16) for s in sums16],
        scratch_shapes=[pltpu.SemaphoreType.DMA((n, 3)), pltpu.SemaphoreType.DMA((n, 3))],
    )(*sums16)


def _join_halves(fulls):
    n = len(fulls)

    def body(*refs):
        ins, outs = refs[:n], refs[n:2 * n]
        send_sems, recv_sems = refs[2 * n:]
        x, y, c = _place()
        started = []
        for a in range(n):
            half = ins[a].shape[0] // 2
            rows = pl.ds(c * half, half)
            cp = pltpu.make_async_remote_copy(src_ref=ins[a].at[rows, :], dst_ref=outs[a].at[rows, :], send_sem=send_sems.at[a],
                                              recv_sem=recv_sems.at[a], device_id=(x, y, 1 - c), device_id_type=MESH)
            cp.start()
            started.append(cp)
        for cp in started:
            cp.wait()

    return pl.pallas_call(
        body, name="join_halves",
        in_specs=[_ANY] * n, out_specs=[_ANY] * n,
        out_shape=[jax.ShapeDtypeStruct(f.shape, F32) for f in fulls],
        input_output_aliases={a: a for a in range(n)},
        scratch_shapes=[pltpu.SemaphoreType.DMA((n,)), pltpu.SemaphoreType.DMA((n,))],
    )(*fulls)


def _small_allreduce_adamw(g, w, m, v):
    R = g.shape[0]

    def body(g_ref, w_ref, m_ref, v_ref, gs_ref, d_ref, nm_ref, nv_ref, all_s, send_sems, recv_sems):
        x, y, c = _place()
        me = 4 * x + 2 * y + c
        all_s[me] = g_ref[...]
        sends = []
        for k in range(1, 8):
            peer = (x ^ (k >> 2), y ^ ((k >> 1) & 1), c ^ (k & 1))
            cp = pltpu.make_async_remote_copy(src_ref=g_ref, dst_ref=all_s.at[me], send_sem=send_sems.at[k - 1],
                                              recv_sem=recv_sems.at[k - 1], device_id=peer, device_id_type=MESH)
            cp.start()
            sends.append(cp)
        for cp in sends:
            cp.wait()
        total = all_s[0]
        for d in range(1, 8):
            total = total + all_s[d]
        gs_ref[...] = total
        delta, nm, nv = _adamw_math(w_ref[...], total, m_ref[...], v_ref[...])
        d_ref[...] = delta
        nm_ref[...] = nm
        nv_ref[...] = nv

    vm = pl.BlockSpec(memory_space=pltpu.VMEM)
    return pl.pallas_call(
        body, name="small_allreduce_adamw",
        in_specs=[vm] * 4, out_specs=[vm] * 4, out_shape=[jax.ShapeDtypeStruct((R, LANES), F32)] * 4,
        scratch_shapes=[pltpu.VMEM((8, R, LANES), F32), pltpu.SemaphoreType.DMA((7,)), pltpu.SemaphoreType.DMA((7,))],
        compiler_params=pltpu.CompilerParams(vmem_limit_bytes=VMEM_LIMIT),
    )(g, w, m, v)


_SMALL = (("norm_mix_g", D_MODEL), ("f_bias", N_HEADS), ("sg_ln_g", D_HEADS), ("sg_w", N_HEADS * SG_BLOCK * SG_BLOCK),
          ("sg_b", N_HEADS * SG_BLOCK), ("norm_ffn_g", D_MODEL), ("w_conv", 3 * 2 * D_FF), ("b_conv", 2 * D_FF),
          ("norm_final_g", D_MODEL))


def _pack_small(parts):
    rows = []
    for name, size in _SMALL:
        flat = parts[name].reshape(-1).astype(F32)
        pad = (-size) % (SUBLANES * LANES)
        rows.append(jnp.pad(flat, (0, pad)).reshape(-1, LANES))
    return jnp.concatenate(rows, axis=0)


def _unpack_small(packed, shapes):
    out, r = {}, 0
    for name, size in _SMALL:
        nrows = (size + SUBLANES * LANES - 1) // (SUBLANES * LANES) * SUBLANES
        out[name] = packed[r:r + nrows].reshape(-1)[:size].reshape(shapes[name])
        r += nrows
    return out


def _local_step(x, target, g1, w_in, f_bias, sg_ln_g, sg_w, sg_b, w_out, g2, w_up_q, w_conv, b_conv, w_down, g3):
    S = x.shape[0]
    tm = _row_tile(S, 512)
    tms = _row_tile(S, 256)
    tq = _row_tile(S, 512)

    lane = jnp.arange(D_HEADS)
    seg_avg = jnp.where(lane[:, None] // HEAD_DIM == lane[None, :] // HEAD_DIM, 1.0 / HEAD_DIM, 0.0).astype(BF16)
    head_ind = (lane[:, None] // HEAD_DIM == jnp.arange(LANES)[None, :]).astype(BF16)
    pos_chunk = jnp.arange(SG_BLOCK) // CHUNK
    w_mask32 = jnp.where(pos_chunk[:, None] >= pos_chunk[None, :], sg_w, 0.0)
    w_mask = w_mask32.astype(BF16)
    w_mask_t = jnp.swapaxes(w_mask32, 1, 2).astype(BF16)
    ln_row = sg_ln_g.reshape(1, D_HEADS)
    b_full = jnp.repeat(sg_b.T, HEAD_DIM, axis=1)
    bias_row = jnp.pad(f_bias.reshape(1, N_HEADS), ((0, 0), (0, LANES - N_HEADS)))
    b_conv_row = b_conv.reshape(1, 2 * D_FF)

    z, f, h1 = _in_proj(x, g1, w_in, tm)
    c = _fox_prep(f, bias_row, _row_tile(S, 256))
    consts = _attn_consts()
    qa, ka, va, vat = _attn_pack(z, c, consts, tm)
    out_b, lse = _attn_fwd(qa, ka, vat, consts["place_t"], tq)
    out_a = _gate_fwd(z, w_mask, ln_row, b_full, seg_avg, tm)
    x1, h2 = _mix_out(x, out_a, out_b, w_out, g2, tm)
    a = _up_proj(h2, w_up_q, tm)
    dx2, sq_err, dg3 = _ffn_fwd_loss(a, w_conv, b_conv_row, w_down, x1, g3, target, tms)

    dconv, y, dw_conv8, db_conv = _ffn_bwd_gate(dx2, a, w_conv, b_conv_row, w_down, tms)
    dact = _conv_bwd(dconv, w_conv, tms, 2 * D_FF // 4)
    dw_down = _matmul_tn(y, dx2, "dw_down", D_FF // 2, D_MODEL, tm, quarters=(2, 1))
    dx1, dg2 = _up_bwd(dact, w_up_q, x1, g2, dx2, tms)
    dw_up_q = _matmul_tn(h2, dact, "dw_up", D_MODEL, 2 * D_FF // 4, tm, quarters=(1, 4))
    dcat = _out_bwd(dx1, w_out, tm)
    dw_out_a = _matmul_tn(out_a, dx1, "dw_out_a", D_HEADS, D_MODEL, tm)
    dw_out_b = _matmul_tn(out_b, dx1, "dw_out_b", D_HEADS, D_MODEL, tm)
    dzu, dzv, dsg_w, dsg_b_t, dln = _gate_bwd(z, dcat, w_mask, w_mask_t, ln_row, b_full, seg_avg, head_ind, tm)
    dop, qb, dopt, qbt = _attn_pack_grad(out_b, dcat, qa, lse, head_ind, consts, tm)
    dqp, dk, dv, dcc = _attn_bwd(qb, qbt, ka, va, dop, dopt, consts, tq)
    dq, dc = _attn_unpack(dqp, dcc, consts, tm)
    df, dbias = _fox_bwd(dc, f, bias_row, _row_tile(S, 256))
    pieces = (dzu, dzv, dq, dk, dv, df)
    dx, dg1 = _in_bwd(pieces, w_in, x, g1, dx1, tms)
    dw_in = _dw_in(h1, pieces, tm)

    grads = {
        "norm_mix_g": dg1, "f_bias": dbias[:, :N_HEADS], "sg_ln_g": dln, "sg_w": dsg_w, "sg_b": dsg_b_t[:, :N_HEADS].T,
        "norm_ffn_g": dg2, "w_conv": dw_conv8[:3], "b_conv": db_conv, "norm_final_g": dg3,
        "w_in": dw_in, "w_out": jnp.concatenate([dw_out_a, dw_out_b], axis=0), "w_up_q": dw_up_q, "w_down": dw_down,
    }
    return sq_err, dx, grads


def _reduce_big(grads_q, ids):
    names = list(grads_q)
    theirs = _swap_halves([grads_q[k] for k in names])
    sums = [_pair_sum("pair_sum_" + k, grads_q[k], t, ids) for k, t in zip(names, theirs)]
    got = _scatter_quarters([s16 for _, s16 in sums])
    fulls = [_chip_sum("chip_sum_" + k, s32, g, ids) for k, (s32, _), g in zip(names, sums, got)]
    return dict(zip(names, _join_halves(fulls)))


def kernel(x, norm_mix_g, w_in, f_bias, sg_ln_g, sg_w, sg_b, w_out, norm_ffn_g, w_up, w_conv, b_conv, w_down, norm_final_g, loss_target, m_norm_mix_g, m_w_in, m_f_bias, m_sg_ln_g, m_sg_w, m_sg_b, m_w_out, m_norm_ffn_g, m_w_up, m_w_conv, m_b_conv, m_w_down, m_norm_final_g, v_norm_mix_g, v_w_in, v_f_bias, v_sg_ln_g, v_sg_w, v_sg_b, v_w_out, v_norm_ffn_g, v_w_up, v_w_conv, v_b_conv, v_w_down, v_norm_final_g):
    args = dict(locals())
    quarter = 2 * lax.axis_index("x") + lax.axis_index("y")
    ids = jnp.stack([quarter, lax.axis_index("c")]).astype(jnp.int32)
    wq_conv = w_conv.shape[-1]

    shards = [w_in[0].astype(BF16), w_out[0].astype(BF16), w_up[0].astype(BF16), w_down[0].astype(BF16), w_conv[0]]
    g_in, g_out, g_up, g_down, g_conv = [lax.dynamic_update_index_in_dim(g, s, quarter, 0)
                                         for g, s in zip(_gather_quarters(shards), shards)]
    w_in_full = jnp.pad(jnp.concatenate([g_in[q] for q in range(4)], axis=1), ((0, 0), (0, D_IN_PAD - D_IN)))
    w_out_full = g_out.reshape(D_MODEL, D_MODEL)
    w_down_full = g_down.reshape(D_FF, D_MODEL)
    w_conv_full = jnp.concatenate([g_conv[q] for q in range(4)], axis=1)

    sq_err, dx, grads = _local_step(
        x[0], loss_target[0], norm_mix_g, w_in_full, f_bias[0], sg_ln_g[0], sg_w[0], sg_b[0], w_out_full, norm_ffn_g,
        g_up, w_conv_full, b_conv[0], w_down_full, norm_final_g.reshape(1, D_MODEL))
    loss = lax.psum(0.5 * jnp.sum(sq_err) / D_MODEL, ("x", "y", "c"))

    dw_in = grads["w_in"][:, :D_IN].reshape(D_MODEL, 4, D_IN // 4).transpose(1, 0, 2)
    big = _reduce_big({"w_in": dw_in, "w_out": grads["w_out"].reshape(4, D_MODEL // 4, D_MODEL),
                       "w_up": grads["w_up_q"], "w_down": grads["w_down"].reshape(4, D_FF // 4, D_MODEL)}, ids)

    out = {"loss": loss, "grad_x": dx[None]}
    for k in ("w_in", "w_out", "w_up", "w_down"):
        g = big[k]
        d, nm, nv = _adamw("adamw_" + k, args[k][0], g, args["m_" + k][0], args["v_" + k][0])
        out["grad_" + k], out["delta_" + k], out["new_m_" + k], out["new_v_" + k] = g[None], d[None], nm[None], nv[None]

    def padded_conv(t):
        return lax.dynamic_update_slice(jnp.zeros((3, 4 * wq_conv), F32), t[0], (0, quarter * wq_conv))

    small_names = [n for n, _ in _SMALL]
    shapes = {n: (3, 4 * wq_conv) if n == "w_conv" else args[n].shape for n in small_names}
    pack = lambda prefix: _pack_small({n: padded_conv(args[prefix + n]) if n == "w_conv" else args[prefix + n] for n in small_names})
    packed = _small_allreduce_adamw(_pack_small({n: grads[n] for n in small_names}), pack(""), pack("m_"), pack("v_"))
    for prefix, arr in zip(("grad_", "delta_", "new_m_", "new_v_"), packed):
        for n, t in _unpack_small(arr, shapes).items():
            if n == "w_conv":
                t = lax.dynamic_slice(t, (0, quarter * wq_conv), (3, wq_conv))[None]
            out[prefix + n] = t

    weights = ["norm_mix_g", "w_in", "f_bias", "sg_ln_g", "sg_w", "sg_b", "w_out", "norm_ffn_g", "w_up", "w_conv", "b_conv",
               "w_down", "norm_final_g"]
    return (out["loss"], out["grad_x"], *[out[p + n] for p in ("grad_", "delta_", "new_m_", "new_v_") for n in weights])
```

```python
import functools
import math

import jax
import jax.numpy as jnp
from jax import lax
from jax.experimental import pallas as pl
from jax.experimental.pallas import tpu as pltpu

F32 = jnp.float32
BF16 = jnp.bfloat16
MESH = pl.DeviceIdType.MESH

D_MODEL = 1024
N_HEADS = 8
HEAD_DIM = 64
D_HEADS = N_HEADS * HEAD_DIM
SG_BLOCK = 128
CHUNK = 64
D_FF = 2816
D_IN = 2 * D_HEADS + 3 * D_HEADS + N_HEADS
LANES = 128
SUBLANES = 8
D_IN_PAD = 5 * D_HEADS + LANES
EPS = 1e-6
SCALE = HEAD_DIM ** -0.5
NEG = -1e30
LOG2E = 1.4426950408889634
HEAD_PAD = LANES
D_PAD = N_HEADS * HEAD_PAD
Q_STAT = HEAD_DIM
K_STAT = HEAD_DIM + 3
L_STAT = HEAD_DIM + 6
GROUPS = 2
GROUP_HEADS = N_HEADS // GROUPS
GROUP_PAD = GROUP_HEADS * HEAD_PAD
KEY_CHUNK = 256
STAT_ROWS = 16

ADAM_LR = 0.001
ADAM_B1 = 0.9
ADAM_B2 = 0.999
ADAM_EPS = 1e-08
ADAM_WD = 0.01
ADAM_STEP = 10

VMEM_LIMIT = 56 * 1024 * 1024

NT = (((1,), (1,)), ((), ()))
TN = (((0,), (0,)), ((), ()))


def _params(sem):
    return pltpu.CompilerParams(dimension_semantics=sem, vmem_limit_bytes=VMEM_LIMIT)


def _full(shape):
    nd = len(shape)
    return pl.BlockSpec(shape, lambda *_: (0,) * nd)


def _row_tile(rows, target):
    best = None
    for t in range(SUBLANES, min(rows, target) + 1, SUBLANES):
        if rows % t == 0:
            best = t
    assert best is not None, rows
    return best


def _sigmoid(x):
    return 0.5 * jnp.tanh(0.5 * x) + 0.5


def _gelu(z):
    return 0.5 * z * (1.0 + lax.erf(z * (2.0 ** -0.5)))


def _gelu_grad(z):
    cdf = 0.5 * (1.0 + lax.erf(z * (2.0 ** -0.5)))
    pdf = jnp.exp(-0.5 * z * z) * (1.0 / math.sqrt(2.0 * math.pi))
    return cdf + z * pdf


def _split_dot(x, m):
    hi = x.astype(BF16)
    lo = (x - hi.astype(F32)).astype(BF16)
    return jnp.dot(hi, m, preferred_element_type=F32) + jnp.dot(lo, m, preferred_element_type=F32)


def _head_mask(h, rows):
    lane = lax.broadcasted_iota(jnp.int32, (rows, D_HEADS), 1)
    return (lane >= h * HEAD_DIM) & (lane < (h + 1) * HEAD_DIM)


def _rms_bwd(dh, x, g):
    r = lax.rsqrt(jnp.mean(x * x, axis=-1, keepdims=True) + EPS)
    xhat = x * r
    dg = jnp.sum(dh * xhat, axis=0, keepdims=True)
    dxhat = dh * g
    dx = r * (dxhat - xhat * jnp.mean(dxhat * xhat, axis=-1, keepdims=True))
    return dx, dg


def _in_proj(x, g1, w_in, tm):
    S = x.shape[0]
    nz = D_IN_PAD - LANES

    def body(x_ref, g_ref, w_ref, z_ref, f_ref, h_ref):
        xf = x_ref[...]
        r = lax.rsqrt(jnp.mean(xf * xf, axis=-1, keepdims=True) + EPS)
        h = (xf * r * g_ref[...]).astype(BF16)
        h_ref[...] = h
        zz = jnp.dot(h, w_ref[...], preferred_element_type=F32)
        z_ref[...] = zz[:, :nz].astype(BF16)
        f_ref[...] = zz[:, nz:]

    return pl.pallas_call(
        body, name="in_proj", grid=(S // tm,),
        in_specs=[pl.BlockSpec((tm, D_MODEL), lambda i: (i, 0)), _full((1, D_MODEL)), _full((D_MODEL, D_IN_PAD))],
        out_specs=[pl.BlockSpec((tm, nz), lambda i: (i, 0)), pl.BlockSpec((tm, LANES), lambda i: (i, 0)),
                   pl.BlockSpec((tm, D_MODEL), lambda i: (i, 0))],
        out_shape=[jax.ShapeDtypeStruct((S, nz), BF16), jax.ShapeDtypeStruct((S, LANES), F32),
                   jax.ShapeDtypeStruct((S, D_MODEL), BF16)],
        compiler_params=_params(("parallel",)),
    )(x, g1, w_in)


def _fox_prep(f, bias_row, tb):
    S = f.shape[0]

    def body(f_ref, b_ref, c_ref, carry):
        @pl.when(pl.program_id(0) == 0)
        def _():
            carry[...] = jnp.zeros_like(carry)

        xv = f_ref[...] + b_ref[...]
        lf = jnp.minimum(xv, 0.0) - jnp.log(1.0 + jnp.exp(-jnp.abs(xv)))
        r = lax.broadcasted_iota(jnp.int32, (tb, tb), 0)
        s = lax.broadcasted_iota(jnp.int32, (tb, tb), 1)
        tri = (r >= s).astype(F32)
        cs = jnp.dot(tri, lf, precision=lax.Precision.HIGHEST, preferred_element_type=F32) + carry[0:1, :]
        c_ref[...] = cs
        carry[...] = jnp.broadcast_to(cs[tb - 1:tb, :], carry.shape)

    return pl.pallas_call(
        body, name="fox_prep", grid=(S // tb,),
        in_specs=[pl.BlockSpec((tb, LANES), lambda i: (i, 0)), _full((1, LANES))],
        out_specs=pl.BlockSpec((tb, LANES), lambda i: (i, 0)),
        out_shape=jax.ShapeDtypeStruct((S, LANES), F32),
        scratch_shapes=[pltpu.VMEM((SUBLANES, LANES), F32)],
        compiler_params=_params(("arbitrary",)),
    )(f, bias_row)


def _attn_consts():
    col = jnp.arange(D_PAD)
    row = jnp.arange(D_HEADS)
    head = jnp.arange(LANES)
    place = (row[:, None] // HEAD_DIM == col[None, :] // HEAD_PAD) & (row[:, None] % HEAD_DIM == col[None, :] % HEAD_PAD)

    def stat(offset):
        return ((head[:, None] < N_HEADS) & (col[None, :] == head[:, None] * HEAD_PAD + offset)).astype(BF16)

    def ones(offsets):
        return sum((col % HEAD_PAD == o) for o in offsets).astype(F32).reshape(1, D_PAD)

    def pick(offset):
        gcol = jnp.arange(GROUP_PAD)
        return jnp.stack([((gcol[:, None] % HEAD_PAD == offset) & (head[None, :] == g * GROUP_HEADS + gcol[:, None] // HEAD_PAD))
                          for g in range(GROUPS)]).astype(BF16)

    place = place.astype(BF16)
    return {
        "place": place, "place_t": place.T, "place_t_group": place.T[:GROUP_PAD, :GROUP_HEADS * HEAD_DIM],
        "q_stat": jnp.stack([stat(Q_STAT + j) for j in range(3)]), "k_stat": jnp.stack([stat(K_STAT + j) for j in range(3)]),
        "d_stat": jnp.stack([stat(Q_STAT + j) for j in range(2)]), "l_stat": jnp.stack([stat(L_STAT + j)[:STAT_ROWS] for j in range(3)]),
        "q_ones": ones(range(K_STAT, K_STAT + 3)), "k_ones": ones(list(range(Q_STAT, Q_STAT + 3)) + list(range(L_STAT, L_STAT + 3))),
        "v_ones": ones(range(Q_STAT, Q_STAT + 2)),
        "pick_rows": pick(Q_STAT), "pick_cols": pick(K_STAT),
    }


def _split3(x):
    hi = x.astype(BF16)
    r = x - hi.astype(F32)
    mid = r.astype(BF16)
    return hi, mid, (r - mid.astype(F32)).astype(BF16)


def _split3_dot(x, m):
    return sum(jnp.dot(part, m, preferred_element_type=F32) for part in _split3(x))


def _attn_pack(z, c, k, tm):
    S = z.shape[0]

    def body(q_ref, k_ref, v_ref, c_ref, pl_ref, pt_ref, qs_ref, ks_ref, qo_ref, ko_ref, vo_ref, voc_ref,
             qa_ref, ka_ref, va_ref, vt_ref):
        place = pl_ref[...]
        q = (q_ref[...].astype(F32) * (SCALE * LOG2E)).astype(BF16)
        qa = jnp.dot(q, place, preferred_element_type=F32) + qo_ref[...]
        ka = jnp.dot(k_ref[...], place, preferred_element_type=F32) + ko_ref[...]
        for j, part in enumerate(_split3(c_ref[...] * LOG2E)):
            qa = qa + jnp.dot(part, qs_ref[j], preferred_element_type=F32)
            ka = ka - jnp.dot(part, ks_ref[j], preferred_element_type=F32)
        qa_ref[...] = qa.astype(BF16)
        ka_ref[...] = ka.astype(BF16)
        v = v_ref[...]
        va_ref[...] = (jnp.dot(v, place, preferred_element_type=F32) + vo_ref[...]).astype(BF16)
        vt_ref[...] = (lax.dot_general(pt_ref[...], v, NT, preferred_element_type=F32) + voc_ref[...]).astype(BF16)

    blk = lambda col: pl.BlockSpec((tm, D_HEADS), lambda i: (i, col))
    out = pl.BlockSpec((tm, D_PAD), lambda i: (i, 0))
    pad = jax.ShapeDtypeStruct((S, D_PAD), BF16)
    return pl.pallas_call(
        body, name="attn_pack", grid=(S // tm,),
        in_specs=[blk(2), blk(3), blk(4), pl.BlockSpec((tm, LANES), lambda i: (i, 0)), _full((D_HEADS, D_PAD)), _full((D_PAD, D_HEADS)),
                  _full((3, LANES, D_PAD)), _full((3, LANES, D_PAD)), _full((1, D_PAD)), _full((1, D_PAD)), _full((1, D_PAD)),
                  _full((D_PAD, 1))],
        out_specs=[out, out, out, pl.BlockSpec((D_PAD, tm), lambda i: (0, i))],
        out_shape=[pad, pad, pad, jax.ShapeDtypeStruct((D_PAD, S), BF16)],
        compiler_params=_params(("parallel",)),
    )(z, z, z, c, k["place"], k["place_t"], k["q_stat"], k["k_stat"], k["q_ones"], k["k_ones"], k["v_ones"], k["v_ones"].T)


def _attn_fwd(qa, ka, vat, place_t, tq, shards):
    S = qa.shape[0]
    n = S // tq
    ns = len(shards)
    hand_on_at = (2 * n) // 3

    def body(q_ref, k_ref, vt_ref, pt_ref, *rest):
        o_ref, lse_ref = rest[ns:ns + 2]
        m_s, acc_s, ot_s, s_s = rest[2 * ns + 2:2 * ns + 6]
        start, hand_on, finish = _gather_ops(rest[:ns], rest[ns + 2:2 * ns + 2], *rest[2 * ns + 6:])
        qi, ki = pl.program_id(0), pl.program_id(1)

        @pl.when((qi == 0) & (ki == 0))
        def _():
            start()

        @pl.when((qi == hand_on_at) & (ki == 0))
        def _():
            hand_on()

        @pl.when(ki == 0)
        def _():
            m_s[...] = jnp.full_like(m_s, NEG)
            acc_s[...] = jnp.zeros_like(acc_s)

        def step(diagonal):
            chunks = [slice(c * KEY_CHUNK, (c + 1) * KEY_CHUNK) for c in range(tq // KEY_CHUNK)]

            def scores(h, rows, slot):
                sl = slice(h * HEAD_PAD, (h + 1) * HEAD_PAD)
                st = lax.dot_general(k_ref[rows, sl], q_ref[:, sl], NT, preferred_element_type=F32)
                if diagonal:
                    key = rows.start + lax.broadcasted_iota(jnp.int32, (KEY_CHUNK, tq), 0)
                    query = lax.broadcasted_iota(jnp.int32, (KEY_CHUNK, tq), 1)
                    st = jnp.where(query >= key, st, NEG)
                s_s[slot, rows, :] = st
                return jnp.max(st, axis=0, keepdims=True)

            m_cur = functools.reduce(jnp.maximum, [scores(0, rows, 0) for rows in chunks])
            for h in range(N_HEADS):
                sl = slice(h * HEAD_PAD, (h + 1) * HEAD_PAD)
                slot = h % 2
                m_prev = m_s[h][0:1, :]
                m_new = jnp.maximum(m_prev, m_cur)
                acc = jnp.exp2(m_prev - m_new) * acc_s[h]
                m_next = []
                for rows in chunks:
                    if h + 1 < N_HEADS:
                        m_next.append(scores(h + 1, rows, 1 - slot))
                    pt = jnp.exp2(s_s[slot, rows, :] - m_new).astype(BF16)
                    acc = acc + jnp.dot(vt_ref[sl, rows], pt, preferred_element_type=F32)
                acc_s[h] = acc
                m_s[h] = jnp.broadcast_to(m_new, (SUBLANES, tq))
                if m_next:
                    m_cur = functools.reduce(jnp.maximum, m_next)

        @pl.when(ki < qi)
        def _():
            step(False)

        @pl.when(ki == qi)
        def _():
            step(True)
            lse_ref[...] = jnp.zeros_like(lse_ref)
            for h in range(N_HEADS):
                acc = acc_s[h]
                denom = acc[Q_STAT:Q_STAT + 1, :]
                ot_s[h * HEAD_PAD:(h + 1) * HEAD_PAD, :] = (acc / denom).astype(BF16)
                lse_ref[h:h + 1, :] = m_s[h][0:1, :] + jnp.log(denom) * LOG2E
            o_ref[...] = lax.dot_general(ot_s[...], pt_ref[...], TN, preferred_element_type=F32).astype(BF16)

        @pl.when((qi == n - 1) & (ki == n - 1))
        def _():
            finish()

    kmin = lambda qi, ki: jnp.minimum(ki, qi)
    out = pl.pallas_call(
        body, name="attn_fwd", grid=(n, n),
        in_specs=[pl.BlockSpec((tq, D_PAD), lambda qi, ki: (qi, 0)), pl.BlockSpec((tq, D_PAD), lambda qi, ki: (kmin(qi, ki), 0)),
                  pl.BlockSpec((D_PAD, tq), lambda qi, ki: (0, kmin(qi, ki))), _full((D_PAD, D_HEADS))] + [_ANY] * ns,
        out_specs=[pl.BlockSpec((tq, D_HEADS), lambda qi, ki: (qi, 0)), pl.BlockSpec((STAT_ROWS, tq), lambda qi, ki: (0, qi))]
        + [_ANY] * ns,
        out_shape=[jax.ShapeDtypeStruct((S, D_HEADS), BF16), jax.ShapeDtypeStruct((STAT_ROWS, S), F32)] + _gather_shapes(shards),
        scratch_shapes=[pltpu.VMEM((N_HEADS, SUBLANES, tq), F32), pltpu.VMEM((N_HEADS, HEAD_PAD, tq), F32),
                        pltpu.VMEM((D_PAD, tq), BF16), pltpu.VMEM((2, tq, tq), F32)] + _gather_sems(ns),
        compiler_params=_params(("arbitrary", "arbitrary")),
    )(qa, ka, vat, place_t, *shards)
    return out[0], out[1], out[2:]


def _layer_norm_heads(v, seg_avg):
    mu = _split_dot(v, seg_avg)
    d = v - mu
    var = _split_dot(d * d, seg_avg)
    rstd = lax.rsqrt(var + EPS)
    return d * rstd, rstd


def _gate_mix(vn_blk, w_ref, bias):
    acc = bias
    for h in range(N_HEADS):
        vh = jnp.where(_head_mask(h, SG_BLOCK), vn_blk, 0.0).astype(BF16)
        acc = acc + jnp.dot(w_ref[h], vh, preferred_element_type=F32)
    return acc


def _gate_fwd(z, w_mask, ln_row, b_full, seg_avg, tm):
    S = z.shape[0]

    def body(zu_ref, zv_ref, w_ref, ln_ref, b_ref, avg_ref, o_ref):
        u = _gelu(zu_ref[...].astype(F32))
        v = _gelu(zv_ref[...].astype(F32))
        vhat, _ = _layer_norm_heads(v, avg_ref[...])
        vn = vhat * ln_ref[...]
        for b in range(tm // SG_BLOCK):
            rows = slice(b * SG_BLOCK, (b + 1) * SG_BLOCK)
            mixed = _gate_mix(vn[rows], w_ref, b_ref[...])
            o_ref[rows, :] = (u[rows] * mixed).astype(BF16)

    return pl.pallas_call(
        body, name="gate_fwd", grid=(S // tm,),
        in_specs=[pl.BlockSpec((tm, D_HEADS), lambda i: (i, 0)), pl.BlockSpec((tm, D_HEADS), lambda i: (i, 1)),
                  _full((N_HEADS, SG_BLOCK, SG_BLOCK)), _full((1, D_HEADS)), _full((SG_BLOCK, D_HEADS)),
                  _full((D_HEADS, D_HEADS))],
        out_specs=pl.BlockSpec((tm, D_HEADS), lambda i: (i, 0)),
        out_shape=jax.ShapeDtypeStruct((S, D_HEADS), BF16),
        compiler_params=_params(("parallel",)),
    )(z, z, w_mask, ln_row, b_full, seg_avg)


def _mix_out(x, out_a, out_b, w_out, g2, tm):
    S = x.shape[0]

    def body(x_ref, a_ref, b_ref, w_ref, g_ref, x1_ref, h_ref):
        y = jnp.dot(a_ref[...], w_ref[:D_HEADS, :], preferred_element_type=F32)
        y = y + jnp.dot(b_ref[...], w_ref[D_HEADS:, :], preferred_element_type=F32)
        x1 = x_ref[...] + y
        x1_ref[...] = x1
        r = lax.rsqrt(jnp.mean(x1 * x1, axis=-1, keepdims=True) + EPS)
        h_ref[...] = (x1 * r * g_ref[...]).astype(BF16)

    row = lambda w: pl.BlockSpec((tm, w), lambda i: (i, 0))
    return pl.pallas_call(
        body, name="mix_out", grid=(S // tm,),
        in_specs=[row(D_MODEL), row(D_HEADS), row(D_HEADS), _full((D_MODEL, D_MODEL)), _full((1, D_MODEL))],
        out_specs=[row(D_MODEL), row(D_MODEL)],
        out_shape=[jax.ShapeDtypeStruct((S, D_MODEL), F32), jax.ShapeDtypeStruct((S, D_MODEL), BF16)],
        compiler_params=_params(("parallel",)),
    )(x, out_a, out_b, w_out, g2)


def _up_proj(h2, w_up_q, tm):
    S = h2.shape[0]
    nq, _, wq = w_up_q.shape

    def body(h_ref, w_ref, a_ref):
        a_ref[...] = jnp.dot(h_ref[...], w_ref[...], preferred_element_type=F32).astype(BF16)

    return pl.pallas_call(
        body, name="up_proj", grid=(nq, S // tm),
        in_specs=[pl.BlockSpec((tm, D_MODEL), lambda j, i: (i, 0)), pl.BlockSpec((None, D_MODEL, wq), lambda j, i: (j, 0, 0))],
        out_specs=pl.BlockSpec((tm, wq), lambda j, i: (i, j)),
        out_shape=jax.ShapeDtypeStruct((S, nq * wq), BF16),
        compiler_params=_params(("parallel", "parallel")),
    )(h2, w_up_q)


def _shift_down(a, halo, k):
    tm = a.shape[0]
    ra = pltpu.roll(a, k, 0)
    rh = pltpu.roll(halo, k, 0)
    row = lax.broadcasted_iota(jnp.int32, halo.shape, 0)
    top = jnp.where(row < k, rh, ra[0:SUBLANES])
    return jnp.concatenate([top, ra[SUBLANES:tm]], axis=0)


def _shift_up(a, halo, k):
    tm = a.shape[0]
    ra = pltpu.roll(a, tm - k, 0)
    rh = pltpu.roll(halo, SUBLANES - k, 0)
    row = lax.broadcasted_iota(jnp.int32, halo.shape, 0)
    bottom = jnp.where(row >= SUBLANES - k, rh, ra[tm - SUBLANES:tm])
    return jnp.concatenate([ra[0:tm - SUBLANES], bottom], axis=0)


def _conv_taps(a_ref, halo_ref, first):
    a = a_ref[...].astype(F32)
    halo = halo_ref[...].astype(F32) * jnp.where(first, 0.0, 1.0)
    return a, _shift_down(a, halo, 1), _shift_down(a, halo, 2)


def _conv_specs(tm):
    step = tm // SUBLANES
    prev = lambda i: jnp.maximum(i * step - 1, 0)
    return [pl.BlockSpec((tm, D_FF), lambda i: (i, 0)), pl.BlockSpec((tm, D_FF), lambda i: (i, 1)),
            pl.BlockSpec((SUBLANES, D_FF), lambda i: (prev(i), 0)), pl.BlockSpec((SUBLANES, D_FF), lambda i: (prev(i), 1))]


def _ffn_fwd_loss(a, w_conv, b_conv, w_down, x1, g3, target, tm):
    S = x1.shape[0]

    def body(ag_ref, av_ref, hg_ref, hv_ref, wg_ref, wv_ref, bg_ref, bv_ref, wd_ref, x1_ref, g_ref, t_ref,
             dx2_ref, loss_ref, dg_ref):
        i = pl.program_id(0)

        @pl.when(i == 0)
        def _():
            loss_ref[...] = jnp.zeros_like(loss_ref)
            dg_ref[...] = jnp.zeros_like(dg_ref)

        g0, g1, g2 = _conv_taps(ag_ref, hg_ref, i == 0)
        gate = wg_ref[2:3, :] * g0 + wg_ref[1:2, :] * g1 + wg_ref[0:1, :] * g2 + bg_ref[...]
        v0, v1, v2 = _conv_taps(av_ref, hv_ref, i == 0)
        val = wv_ref[2:3, :] * v0 + wv_ref[1:2, :] * v1 + wv_ref[0:1, :] * v2 + bv_ref[...]
        y = (gate * _sigmoid(gate) * val).astype(BF16)
        x2 = x1_ref[...] + jnp.dot(y, wd_ref[...], preferred_element_type=F32)
        r = lax.rsqrt(jnp.mean(x2 * x2, axis=-1, keepdims=True) + EPS)
        xhat = x2 * r
        gg = g_ref[...]
        err = xhat * gg - t_ref[...]
        loss_ref[...] += jnp.sum(err * err, axis=0, keepdims=True)
        dy = err * (1.0 / D_MODEL)
        dg_ref[...] += jnp.sum(dy * xhat, axis=0, keepdims=True)
        dxhat = dy * gg
        dx2_ref[...] = r * (dxhat - xhat * jnp.mean(dxhat * xhat, axis=-1, keepdims=True))

    row = lambda w: pl.BlockSpec((tm, w), lambda i: (i, 0))
    half = lambda r: [pl.BlockSpec((r, D_FF), lambda i: (0, 0)), pl.BlockSpec((r, D_FF), lambda i: (0, 1))]
    return pl.pallas_call(
        body, name="ffn_fwd_loss", grid=(S // tm,),
        in_specs=_conv_specs(tm) + half(3) + half(1) + [_full((D_FF, D_MODEL)), row(D_MODEL), _full((1, D_MODEL)), row(D_MODEL)],
        out_specs=[row(D_MODEL), _full((1, D_MODEL)), _full((1, D_MODEL))],
        out_shape=[jax.ShapeDtypeStruct((S, D_MODEL), F32), jax.ShapeDtypeStruct((1, D_MODEL), F32),
                   jax.ShapeDtypeStruct((1, D_MODEL), F32)],
        compiler_params=_params(("arbitrary",)),
    )(a, a, a, a, w_conv, w_conv, b_conv, b_conv, w_down, x1, g3, target)


def _ffn_bwd_gate(dx2, a, w_conv, b_conv, w_down, tm):
    S = dx2.shape[0]

    def body(dx_ref, ag_ref, av_ref, hg_ref, hv_ref, wg_ref, wv_ref, bg_ref, bv_ref, wd_ref,
             dc_ref, y_ref, dw_ref, db_ref):
        i = pl.program_id(0)

        @pl.when(i == 0)
        def _():
            dw_ref[...] = jnp.zeros_like(dw_ref)
            db_ref[...] = jnp.zeros_like(db_ref)

        g0, g1, g2 = _conv_taps(ag_ref, hg_ref, i == 0)
        gate = wg_ref[2:3, :] * g0 + wg_ref[1:2, :] * g1 + wg_ref[0:1, :] * g2 + bg_ref[...]
        v0, v1, v2 = _conv_taps(av_ref, hv_ref, i == 0)
        val = wv_ref[2:3, :] * v0 + wv_ref[1:2, :] * v1 + wv_ref[0:1, :] * v2 + bv_ref[...]
        sg = _sigmoid(gate)
        act = gate * sg
        y_ref[...] = (act * val).astype(BF16)
        dy = lax.dot_general(dx_ref[...].astype(BF16), wd_ref[...], NT, preferred_element_type=F32)
        dgate = dy * val * (sg + act - act * sg)
        dval = dy * act
        dc_ref[:, :D_FF] = dgate.astype(BF16)
        dc_ref[:, D_FF:] = dval.astype(BF16)
        for half, (d, taps) in enumerate(((dgate, (g2, g1, g0)), (dval, (v2, v1, v0)))):
            cols = slice(half * D_FF, (half + 1) * D_FF)
            db_ref[0:1, cols] += jnp.sum(d, axis=0, keepdims=True)
            for j in range(3):
                dw_ref[j:j + 1, cols] += jnp.sum(d * taps[j], axis=0, keepdims=True)

    row = lambda w: pl.BlockSpec((tm, w), lambda i: (i, 0))
    half = lambda r: [pl.BlockSpec((r, D_FF), lambda i: (0, 0)), pl.BlockSpec((r, D_FF), lambda i: (0, 1))]
    return pl.pallas_call(
        body, name="ffn_bwd_gate", grid=(S // tm,),
        in_specs=[row(D_MODEL)] + _conv_specs(tm) + half(3) + half(1) + [_full((D_FF, D_MODEL))],
        out_specs=[row(2 * D_FF), row(D_FF), _full((SUBLANES, 2 * D_FF)), _full((1, 2 * D_FF))],
        out_shape=[jax.ShapeDtypeStruct((S, 2 * D_FF), BF16), jax.ShapeDtypeStruct((S, D_FF), BF16),
                   jax.ShapeDtypeStruct((SUBLANES, 2 * D_FF), F32), jax.ShapeDtypeStruct((1, 2 * D_FF), F32)],
        compiler_params=_params(("arbitrary",)),
    )(dx2, a, a, a, a, w_conv, w_conv, b_conv, b_conv, w_down)


def _conv_bwd(dc, w_conv, tm, tn):
    S, C = dc.shape
    step = tm // SUBLANES
    last_blk = S // SUBLANES - 1

    def body(d_ref, nx_ref, w_ref, o_ref):
        last = pl.program_id(0) == pl.num_programs(0) - 1
        d = d_ref[...].astype(F32)
        nx = nx_ref[...].astype(F32) * jnp.where(last, 0.0, 1.0)
        out = w_ref[2:3, :] * d + w_ref[1:2, :] * _shift_up(d, nx, 1) + w_ref[0:1, :] * _shift_up(d, nx, 2)
        o_ref[...] = out.astype(BF16)

    return pl.pallas_call(
        body, name="conv_bwd", grid=(S // tm, C // tn),
        in_specs=[pl.BlockSpec((tm, tn), lambda i, j: (i, j)),
                  pl.BlockSpec((SUBLANES, tn), lambda i, j: (jnp.minimum((i + 1) * step, last_blk), j)),
                  pl.BlockSpec((3, tn), lambda i, j: (0, j))],
        out_specs=pl.BlockSpec((tm, tn), lambda i, j: (i, j)),
        out_shape=jax.ShapeDtypeStruct((S, C), BF16),
        compiler_params=_params(("parallel", "parallel")),
    )(dc, dc, w_conv)


def _matmul_tn(a, b, name, bm, bn, tk, col_a=0, col_b=0, quarters=None):
    S = a.shape[0]
    gm, gn = quarters if quarters else (1, 1)
    nk = S // tk

    def body(a_ref, b_ref, o_ref):
        @pl.when(pl.program_id(2) == 0)
        def _():
            o_ref[...] = jnp.zeros_like(o_ref)

        o_ref[...] += lax.dot_general(a_ref[...].astype(BF16), b_ref[...].astype(BF16), TN, preferred_element_type=F32)

    if quarters and gn > 1:
        out_spec = pl.BlockSpec((None, bm, bn), lambda i, j, k: (j, i, 0))
        out_shape = jax.ShapeDtypeStruct((gn, gm * bm, bn), F32)
    else:
        out_spec = pl.BlockSpec((bm, bn), lambda i, j, k: (i, j))
        out_shape = jax.ShapeDtypeStruct((gm * bm, gn * bn), F32)
    return pl.pallas_call(
        body, name=name, grid=(gm, gn, nk),
        in_specs=[pl.BlockSpec((tk, bm), lambda i, j, k: (k, col_a * gm + i)),
                  pl.BlockSpec((tk, bn), lambda i, j, k: (k, col_b * gn + j))],
        out_specs=out_spec, out_shape=out_shape,
        compiler_params=_params(("parallel", "parallel", "arbitrary")),
    )(a, b)


def _up_bwd(dact, w_up_q, x1, g2, dx2, tm):
    S = x1.shape[0]
    nq, _, wq = w_up_q.shape

    def body(d_ref, w_ref, x_ref, g_ref, dx2_ref, dx1_ref, dg_ref):
        @pl.when(pl.program_id(0) == 0)
        def _():
            dg_ref[...] = jnp.zeros_like(dg_ref)

        dh = jnp.zeros((tm, D_MODEL), F32)
        for j in range(nq):
            dh = dh + lax.dot_general(d_ref[:, j * wq:(j + 1) * wq], w_ref[j], NT, preferred_element_type=F32)
        dx, dg = _rms_bwd(dh, x_ref[...], g_ref[...])
        dg_ref[...] += dg
        dx1_ref[...] = dx2_ref[...] + dx

    row = lambda w: pl.BlockSpec((tm, w), lambda i: (i, 0))
    return pl.pallas_call(
        body, name="up_bwd", grid=(S // tm,),
        in_specs=[row(nq * wq), pl.BlockSpec((nq, D_MODEL, wq), lambda i: (0, 0, 0), pipeline_mode=pl.Buffered(1)),
                  row(D_MODEL), _full((1, D_MODEL)), row(D_MODEL)],
        out_specs=[row(D_MODEL), _full((1, D_MODEL))],
        out_shape=[jax.ShapeDtypeStruct((S, D_MODEL), F32), jax.ShapeDtypeStruct((1, D_MODEL), F32)],
        compiler_params=_params(("arbitrary",)),
    )(dact, w_up_q, x1, g2, dx2)


def _out_bwd(dx1, w_out, tm):
    S = dx1.shape[0]

    def body(d_ref, w_ref, o_ref):
        o_ref[...] = lax.dot_general(d_ref[...].astype(BF16), w_ref[...], NT, preferred_element_type=F32).astype(BF16)

    return pl.pallas_call(
        body, name="out_bwd", grid=(S // tm,),
        in_specs=[pl.BlockSpec((tm, D_MODEL), lambda i: (i, 0)), _full((D_MODEL, D_MODEL))],
        out_specs=pl.BlockSpec((tm, D_MODEL), lambda i: (i, 0)),
        out_shape=jax.ShapeDtypeStruct((S, D_MODEL), BF16),
        compiler_params=_params(("parallel",)),
    )(dx1, w_out)


def _gate_bwd(z, dcat, w_mask, w_mask_t, ln_row, b_full, seg_avg, head_ind, tm):
    S = z.shape[0]
    nb = tm // SG_BLOCK

    def body(zu_ref, zv_ref, do_ref, w_ref, wt_ref, ln_ref, b_ref, avg_ref, ind_ref,
             dzu_ref, dzv_ref, dw_ref, db_ref, dln_ref, dvn_s, dbf_s):
        i = pl.program_id(0)

        @pl.when(i == 0)
        def _():
            dw_ref[...] = jnp.zeros_like(dw_ref)
            dln_ref[...] = jnp.zeros_like(dln_ref)
            dbf_s[...] = jnp.zeros_like(dbf_s)

        zu = zu_ref[...].astype(F32)
        zv = zv_ref[...].astype(F32)
        u = _gelu(zu)
        v = _gelu(zv)
        avg = avg_ref[...]
        vhat, rstd = _layer_norm_heads(v, avg)
        ln = ln_ref[...]
        vn = vhat * ln
        for b in range(nb):
            rows = slice(b * SG_BLOCK, (b + 1) * SG_BLOCK)
            vn_b = vn[rows]
            mixed = _gate_mix(vn_b, w_ref, b_ref[...])
            do = do_ref[rows, :].astype(F32)
            dzu_ref[rows, :] = (do * mixed * _gelu_grad(zu[rows])).astype(BF16)
            dmix = do * u[rows]
            dbf_s[...] += dmix
            vn_bf = vn_b.astype(BF16)
            dvn = jnp.zeros((SG_BLOCK, D_HEADS), F32)
            for h in range(N_HEADS):
                dmh = jnp.where(_head_mask(h, SG_BLOCK), dmix, 0.0).astype(BF16)
                dw_ref[h] += lax.dot_general(dmh, vn_bf, NT, preferred_element_type=F32)
                dvn = dvn + jnp.dot(wt_ref[h], dmh, preferred_element_type=F32)
            dvn_s[rows, :] = dvn
        dvn = dvn_s[...]
        dln_ref[...] += jnp.sum(dvn * vhat, axis=0, keepdims=True)
        dvhat = dvn * ln
        dv = rstd * (dvhat - _split_dot(dvhat, avg) - vhat * _split_dot(dvhat * vhat, avg))
        dzv_ref[...] = (dv * _gelu_grad(zv)).astype(BF16)

        @pl.when(i == pl.num_programs(0) - 1)
        def _():
            r = lax.broadcasted_iota(jnp.int32, (SG_BLOCK, SG_BLOCK), 0) // CHUNK
            s = lax.broadcasted_iota(jnp.int32, (SG_BLOCK, SG_BLOCK), 1) // CHUNK
            for h in range(N_HEADS):
                dw_ref[h] = jnp.where(r >= s, dw_ref[h], 0.0)
            db_ref[...] = _split_dot(dbf_s[...], ind_ref[...])

    row = lambda col: pl.BlockSpec((tm, D_HEADS), lambda i: (i, col))
    wspec = _full((N_HEADS, SG_BLOCK, SG_BLOCK))
    return pl.pallas_call(
        body, name="gate_bwd", grid=(S // tm,),
        in_specs=[row(0), row(1), row(0), wspec, wspec, _full((1, D_HEADS)), _full((SG_BLOCK, D_HEADS)),
                  _full((D_HEADS, D_HEADS)), _full((D_HEADS, LANES))],
        out_specs=[row(0), row(0), wspec, _full((SG_BLOCK, LANES)), _full((1, D_HEADS))],
        out_shape=[jax.ShapeDtypeStruct((S, D_HEADS), BF16), jax.ShapeDtypeStruct((S, D_HEADS), BF16),
                   jax.ShapeDtypeStruct((N_HEADS, SG_BLOCK, SG_BLOCK), F32), jax.ShapeDtypeStruct((SG_BLOCK, LANES), F32),
                   jax.ShapeDtypeStruct((1, D_HEADS), F32)],
        scratch_shapes=[pltpu.VMEM((tm, D_HEADS), F32), pltpu.VMEM((SG_BLOCK, D_HEADS), F32)],
        compiler_params=_params(("arbitrary",)),
    )(z, z, dcat, w_mask, w_mask_t, ln_row, b_full, seg_avg, head_ind)


def _attn_pack_grad(o, dcat, qa, lse, head_ind, k, tm):
    S = o.shape[0]

    def body(o_ref, do_ref, qa_ref, lse_ref, ind_ref, pl_ref, pt_ref, eye_ref, ds_ref, dst_ref, ls_ref, lst_ref,
             dop_ref, qb_ref, dot_ref, qbt_ref):
        do = do_ref[...]
        delta = _split_dot(o_ref[...].astype(F32) * do.astype(F32), ind_ref[...])
        hi = delta.astype(BF16)
        lo = (delta - hi.astype(F32)).astype(BF16)
        dop = jnp.dot(do, pl_ref[...], preferred_element_type=F32)
        dop = dop - jnp.dot(hi, ds_ref[0], preferred_element_type=F32) - jnp.dot(lo, ds_ref[1], preferred_element_type=F32)
        dop_ref[...] = dop.astype(BF16)
        dot = lax.dot_general(pt_ref[...], do, NT, preferred_element_type=F32)
        dot = dot - lax.dot_general(dst_ref[0], hi, NT, preferred_element_type=F32)
        dot = dot - lax.dot_general(dst_ref[1], lo, NT, preferred_element_type=F32)
        dot_ref[...] = dot.astype(BF16)
        qa = qa_ref[...]
        qb = qa.astype(F32)
        qbt = lax.dot_general(eye_ref[...], qa, NT, preferred_element_type=F32)
        for j, part in enumerate(_split3(lse_ref[...])):
            qb = qb - lax.dot_general(part, ls_ref[j], TN, preferred_element_type=F32)
            qbt = qbt - jnp.dot(lst_ref[j], part, preferred_element_type=F32)
        qb_ref[...] = qb.astype(BF16)
        qbt_ref[...] = qbt.astype(BF16)

    pad = pl.BlockSpec((tm, D_PAD), lambda i: (i, 0))
    padt = pl.BlockSpec((D_PAD, tm), lambda i: (0, i))
    return pl.pallas_call(
        body, name="attn_pack_grad", grid=(S // tm,),
        in_specs=[pl.BlockSpec((tm, D_HEADS), lambda i: (i, 0)), pl.BlockSpec((tm, D_HEADS), lambda i: (i, 1)), pad,
                  pl.BlockSpec((STAT_ROWS, tm), lambda i: (0, i)), _full((D_HEADS, LANES)), _full((D_HEADS, D_PAD)),
                  _full((D_PAD, D_HEADS)), _full((D_PAD, D_PAD)), _full((2, LANES, D_PAD)), _full((2, D_PAD, LANES)),
                  _full((3, STAT_ROWS, D_PAD)), _full((3, D_PAD, STAT_ROWS))],
        out_specs=[pad, pad, padt, padt],
        out_shape=[jax.ShapeDtypeStruct((S, D_PAD), BF16)] * 2 + [jax.ShapeDtypeStruct((D_PAD, S), BF16)] * 2,
        compiler_params=_params(("parallel",)),
    )(o, dcat, qa, lse, head_ind, k["place"], k["place_t"], jnp.eye(D_PAD, dtype=BF16), k["d_stat"],
      jnp.swapaxes(k["d_stat"], 1, 2), k["l_stat"], jnp.swapaxes(k["l_stat"], 1, 2))


def _attn_bwd(qb, qbt, ka, va, dop, dopt, k, tq, sums16):
    S = qb.shape[0]
    n = S // tq
    ns = len(sums16)

    def body(q_ref, qt_ref, k_ref, v_ref, do_ref, dot_ref, pt_ref, pick_ref, *rest):
        dq_hbm, dk_ref, dv_ref, dcc_ref = rest[ns:ns + 4]
        dq_s, dk_s, dv_s, s_s, d_s, sem = rest[2 * ns + 4:2 * ns + 10]
        scatter_start, scatter_finish = _scatter_ops(rest[:ns], rest[ns + 4:2 * ns + 4], *rest[2 * ns + 10:])
        g, ki, qi = pl.program_id(0), pl.program_id(1), pl.program_id(2)

        @pl.when((g == 0) & (ki == 0) & (qi == 0))
        def _():
            scatter_start()

        @pl.when((ki == 0) & (qi == 0))
        def _():
            dq_s[...] = jnp.zeros_like(dq_s)

        @pl.when(qi == ki)
        def _():
            dk_s[...] = jnp.zeros_like(dk_s)
            dv_s[...] = jnp.zeros_like(dv_s)

        def step(diagonal):
            chunks = [slice(c * KEY_CHUNK, (c + 1) * KEY_CHUNK) for c in range(tq // KEY_CHUNK)]

            def scores(hh, rows, slot):
                sl = slice(hh * HEAD_PAD, (hh + 1) * HEAD_PAD)
                s_s[slot, rows, :] = lax.dot_general(q_ref[rows, sl], k_ref[:, sl], NT, preferred_element_type=F32)
                d_s[slot, rows, :] = lax.dot_general(do_ref[rows, sl], v_ref[:, sl], NT, preferred_element_type=F32)

            for rows in chunks:
                scores(0, rows, 0)
            for hh in range(GROUP_HEADS):
                sl = slice(hh * HEAD_PAD, (hh + 1) * HEAD_PAD)
                slot = hh % 2
                dv, dk = dv_s[sl, :], dk_s[sl, :]
                for rows in chunks:
                    if hh + 1 < GROUP_HEADS:
                        scores(hh + 1, rows, 1 - slot)
                    p = jnp.exp2(s_s[slot, rows, :])
                    if diagonal:
                        row = rows.start + lax.broadcasted_iota(jnp.int32, (KEY_CHUNK, tq), 0)
                        col = lax.broadcasted_iota(jnp.int32, (KEY_CHUNK, tq), 1)
                        p = jnp.where(row >= col, p, 0.0)
                    ds = (p * d_s[slot, rows, :]).astype(BF16)
                    dv = dv + jnp.dot(dot_ref[sl, rows], p.astype(BF16), preferred_element_type=F32)
                    dk = dk + jnp.dot(qt_ref[sl, rows], ds, preferred_element_type=F32)
                    qrows = pl.ds(pl.multiple_of(qi * tq + rows.start, KEY_CHUNK), KEY_CHUNK)
                    dq_s[qrows, sl] += jnp.dot(ds, k_ref[:, sl], preferred_element_type=F32)
                dv_s[sl, :] = dv
                dk_s[sl, :] = dk

        @pl.when(qi > ki)
        def _():
            step(False)

        @pl.when(qi == ki)
        def _():
            step(True)

        @pl.when(qi == n - 1)
        def _():
            dk = dk_s[...]
            pt = pt_ref[...]
            dk_ref[...] = lax.dot_general((dk * (1.0 / LOG2E)).astype(BF16), pt, TN, preferred_element_type=F32).astype(BF16)
            dv_ref[...] = lax.dot_general(dv_s[...].astype(BF16), pt, TN, preferred_element_type=F32).astype(BF16)
            dcc_ref[...] = sum(lax.dot_general(part, pick_ref[...], TN, preferred_element_type=F32) for part in _split3(dk))

        @pl.when((ki == n - 1) & (qi == n - 1))
        def _():
            cp = pltpu.make_async_copy(dq_s, dq_hbm.at[g], sem)
            cp.start()
            cp.wait()

        @pl.when((g == GROUPS - 1) & (ki == n - 1) & (qi == n - 1))
        def _():
            scatter_finish()

    gw = GROUP_HEADS * HEAD_DIM
    qmax = lambda ki, qi: jnp.maximum(qi, ki)
    qspec = pl.BlockSpec((tq, GROUP_PAD), lambda g, ki, qi: (qmax(ki, qi), g))
    qtspec = pl.BlockSpec((GROUP_PAD, tq), lambda g, ki, qi: (g, qmax(ki, qi)))
    kspec = pl.BlockSpec((tq, GROUP_PAD), lambda g, ki, qi: (ki, g))
    kout = pl.BlockSpec((tq, gw), lambda g, ki, qi: (ki, g))
    out = pl.pallas_call(
        body, name="attn_bwd", grid=(GROUPS, n, n),
        in_specs=[qspec, qtspec, kspec, kspec, qspec, qtspec,
                  _full((GROUP_PAD, gw)), pl.BlockSpec((None, GROUP_PAD, LANES), lambda g, ki, qi: (g, 0, 0))] + [_ANY] * ns,
        out_specs=[_ANY, kout, kout, pl.BlockSpec((None, tq, LANES), lambda g, ki, qi: (g, ki, 0))] + [_ANY] * ns,
        out_shape=[jax.ShapeDtypeStruct((GROUPS, S, GROUP_PAD), F32), jax.ShapeDtypeStruct((S, D_HEADS), BF16),
                   jax.ShapeDtypeStruct((S, D_HEADS), BF16), jax.ShapeDtypeStruct((GROUPS, S, LANES), F32)]
        + _scatter_shapes(sums16),
        scratch_shapes=[pltpu.VMEM((S, GROUP_PAD), F32), pltpu.VMEM((GROUP_PAD, tq), F32), pltpu.VMEM((GROUP_PAD, tq), F32),
                        pltpu.VMEM((2, tq, tq), F32), pltpu.VMEM((2, tq, tq), F32), pltpu.SemaphoreType.DMA] + _scatter_sems(ns),
        compiler_params=_params(("arbitrary", "arbitrary", "arbitrary")),
    )(qb, qbt, ka, va, dop, dopt, k["place_t_group"], k["pick_cols"], *sums16)
    return out[0], out[1], out[2], out[3], out[4:]


def _attn_unpack(dqp, dcc, k, tm):
    S = dqp.shape[1]
    gw = GROUP_HEADS * HEAD_DIM

    def body(dqp_ref, dcc_ref, pt_ref, pick_ref, dq_ref, dc_ref):
        dc = jnp.zeros((tm, LANES), F32)
        for g in range(GROUPS):
            x = dqp_ref[g]
            dq_ref[:, g * gw:(g + 1) * gw] = jnp.dot((x * SCALE).astype(BF16), pt_ref[...], preferred_element_type=F32).astype(BF16)
            dc = dc + _split3_dot(x, pick_ref[g]) - dcc_ref[g]
        dc_ref[...] = dc

    return pl.pallas_call(
        body, name="attn_unpack", grid=(S // tm,),
        in_specs=[pl.BlockSpec((GROUPS, tm, GROUP_PAD), lambda i: (0, i, 0)), pl.BlockSpec((GROUPS, tm, LANES), lambda i: (0, i, 0)),
                  _full((GROUP_PAD, gw)), _full((GROUPS, GROUP_PAD, LANES))],
        out_specs=[pl.BlockSpec((tm, D_HEADS), lambda i: (i, 0)), pl.BlockSpec((tm, LANES), lambda i: (i, 0))],
        out_shape=[jax.ShapeDtypeStruct((S, D_HEADS), BF16), jax.ShapeDtypeStruct((S, LANES), F32)],
        compiler_params=_params(("parallel",)),
    )(dqp, dcc, k["place_t_group"], k["pick_rows"])


def _fox_bwd(dc, f, bias_row, tb):
    S = f.shape[0]
    nb = S // tb

    def body(dc_ref, f_ref, b_ref, df_ref, dbias_ref, carry):
        @pl.when(pl.program_id(0) == 0)
        def _():
            carry[...] = jnp.zeros_like(carry)
            dbias_ref[...] = jnp.zeros_like(dbias_ref)

        r = lax.broadcasted_iota(jnp.int32, (tb, tb), 0)
        s = lax.broadcasted_iota(jnp.int32, (tb, tb), 1)
        tri = (s >= r).astype(F32)
        rc = jnp.dot(tri, dc_ref[...], precision=lax.Precision.HIGHEST, preferred_element_type=F32) + carry[0:1, :]
        carry[...] = jnp.broadcast_to(rc[0:1, :], carry.shape)
        lane = lax.broadcasted_iota(jnp.int32, (tb, LANES), 1)
        df = jnp.where(lane < N_HEADS, rc * jax.nn.sigmoid(-(f_ref[...] + b_ref[...])), 0.0)
        df_ref[...] = df.astype(BF16)
        dbias_ref[...] += jnp.sum(df, axis=0, keepdims=True)

    rev = pl.BlockSpec((tb, LANES), lambda i: (nb - 1 - i, 0))
    return pl.pallas_call(
        body, name="fox_bwd", grid=(nb,),
        in_specs=[rev, rev, _full((1, LANES))],
        out_specs=[rev, _full((1, LANES))],
        out_shape=[jax.ShapeDtypeStruct((S, LANES), BF16), jax.ShapeDtypeStruct((1, LANES), F32)],
        scratch_shapes=[pltpu.VMEM((SUBLANES, LANES), F32)],
        compiler_params=_params(("arbitrary",)),
    )(dc, f, bias_row)


_DZ_WIDTHS = (D_HEADS,) * 5 + (LANES,)


def _in_bwd(pieces, w_in, x, g1, dx1, tm):
    S = x.shape[0]

    def body(*refs):
        p_refs, (w_ref, x_ref, g_ref, dx1_ref, dx_ref, dg_ref) = refs[:6], refs[6:]

        @pl.when(pl.program_id(0) == 0)
        def _():
            dg_ref[...] = jnp.zeros_like(dg_ref)

        dh = jnp.zeros((tm, D_MODEL), F32)
        off = 0
        for p_ref, w in zip(p_refs, _DZ_WIDTHS):
            dh = dh + lax.dot_general(p_ref[...].astype(BF16), w_ref[:, off:off + w], NT, preferred_element_type=F32)
            off += w
        dx, dg = _rms_bwd(dh, x_ref[...], g_ref[...])
        dg_ref[...] += dg
        dx_ref[...] = dx1_ref[...] + dx

    row = lambda w: pl.BlockSpec((tm, w), lambda i: (i, 0))
    return pl.pallas_call(
        body, name="in_bwd", grid=(S // tm,),
        in_specs=[row(w) for w in _DZ_WIDTHS] + [_full((D_MODEL, D_IN_PAD)), row(D_MODEL), _full((1, D_MODEL)), row(D_MODEL)],
        out_specs=[row(D_MODEL), _full((1, D_MODEL))],
        out_shape=[jax.ShapeDtypeStruct((S, D_MODEL), F32), jax.ShapeDtypeStruct((1, D_MODEL), F32)],
        compiler_params=_params(("arbitrary",)),
    )(*pieces, w_in, x, g1, dx1)


def _dw_in(h1, pieces, tk):
    S = h1.shape[0]

    def body(*refs):
        h_ref, p_refs, o_ref = refs[0], refs[1:7], refs[7]

        @pl.when(pl.program_id(0) == 0)
        def _():
            o_ref[...] = jnp.zeros_like(o_ref)

        off = 0
        for p_ref, w in zip(p_refs, _DZ_WIDTHS):
            o_ref[:, off:off + w] += lax.dot_general(h_ref[...], p_ref[...].astype(BF16), TN, preferred_element_type=F32)
            off += w

    row = lambda w: pl.BlockSpec((tk, w), lambda k: (k, 0))
    return pl.pallas_call(
        body, name="dw_in", grid=(S // tk,),
        in_specs=[row(D_MODEL)] + [row(w) for w in _DZ_WIDTHS],
        out_specs=_full((D_MODEL, D_IN_PAD)),
        out_shape=jax.ShapeDtypeStruct((D_MODEL, D_IN_PAD), F32),
        compiler_params=_params(("arbitrary",)),
    )(h1, *pieces)


def _adamw_math(w, g, m, v):
    m = ADAM_B1 * m + (1.0 - ADAM_B1) * g
    v = ADAM_B2 * v + (1.0 - ADAM_B2) * (g * g)
    m_hat = m / (1.0 - ADAM_B1 ** ADAM_STEP)
    v_hat = v / (1.0 - ADAM_B2 ** ADAM_STEP)
    delta = -ADAM_LR * (m_hat / (jnp.sqrt(v_hat) + ADAM_EPS) + ADAM_WD * w)
    return delta, m, v


def _adamw(name, w, g, m, v):
    R, C = w.shape
    tr = _row_tile(R, 256)

    def body(w_ref, g_ref, m_ref, v_ref, d_ref, nm_ref, nv_ref):
        d, nm, nv = _adamw_math(w_ref[...], g_ref[...], m_ref[...], v_ref[...])
        d_ref[...] = d
        nm_ref[...] = nm
        nv_ref[...] = nv

    spec = pl.BlockSpec((tr, C), lambda i: (i, 0))
    return pl.pallas_call(
        body, name=name, grid=(R // tr,), in_specs=[spec] * 4, out_specs=[spec] * 3,
        out_shape=[jax.ShapeDtypeStruct((R, C), F32)] * 3,
        compiler_params=_params(("parallel",)),
    )(w, g, m, v)


def _pair_sum(name, grad, theirs, ids):
    q, half, C = theirs.shape
    tr = _row_tile(half, 256)
    nb = half // tr

    def body(ids_ref, a_ref, b_ref, s_ref, sb_ref):
        s = a_ref[...] + b_ref[...]
        s_ref[...] = s
        sb_ref[...] = s.astype(BF16)

    here = pl.BlockSpec((None, tr, C), lambda j, i, ids: (j, i, 0))
    return pl.pallas_call(
        body, name=name,
        grid_spec=pltpu.PrefetchScalarGridSpec(
            num_scalar_prefetch=1, grid=(q, nb),
            in_specs=[pl.BlockSpec((None, tr, C), lambda j, i, ids: (j, ids[1] * nb + i, 0)), here],
            out_specs=[here, here]),
        out_shape=[jax.ShapeDtypeStruct((q, half, C), F32), jax.ShapeDtypeStruct((q, half, C), BF16)],
        compiler_params=_params(("parallel", "parallel")),
    )(ids, grad, theirs)


def _chip_sum(name, sums32, others, ids):
    _, half, C = sums32.shape
    tr = _row_tile(half, 256)
    nb = half // tr

    def body(ids_ref, a_ref, o_ref, s_ref):
        s = a_ref[...]
        for j in range(3):
            s = s + o_ref[j].astype(F32)
        s_ref[...] = s

    return pl.pallas_call(
        body, name=name,
        grid_spec=pltpu.PrefetchScalarGridSpec(
            num_scalar_prefetch=1, grid=(nb,),
            in_specs=[pl.BlockSpec((None, tr, C), lambda i, ids: (ids[0], i, 0)),
                      pl.BlockSpec((3, tr, C), lambda i, ids: (0, i, 0))],
            out_specs=pl.BlockSpec((tr, C), lambda i, ids: (ids[1] * nb + i, 0))),
        out_shape=jax.ShapeDtypeStruct((2 * half, C), F32),
        compiler_params=_params(("parallel",)),
    )(ids, sums32, others)


def _place():
    return lax.axis_index("x"), lax.axis_index("y"), lax.axis_index("c")


def _other_chips(x, y):
    return [(1 - x, y), (x, 1 - y), (1 - x, 1 - y)]


_ANY = pl.BlockSpec(memory_space=pl.ANY)


def _gather_quarters(shards):
    n = len(shards)

    def body(*refs):
        start, hand_on, finish = _gather_ops(refs[:n], refs[n:2 * n], *refs[2 * n:])
        start()
        hand_on()
        finish()

    return pl.pallas_call(
        body, name="gather_weights",
        in_specs=[_ANY] * n, out_specs=[_ANY] * n,
        out_shape=_gather_shapes(shards), scratch_shapes=_gather_sems(n),
    )(*shards)


def _gather_shapes(shards):
    return [jax.ShapeDtypeStruct((4,) + s.shape, s.dtype) for s in shards]


def _gather_sems(n):
    return [pltpu.SemaphoreType.DMA((n, 3))] * 4


def _gather_ops(ins, outs, send_sems, recv_sems, pass_send_sems, pass_recv_sems):
    n = len(ins)
    halved = [r.shape[0] % 32 == 0 for r in ins]

    def part(a, quarter, core):
        if not halved[a]:
            return outs[a].at[quarter]
        half = ins[a].shape[0] // 2
        return outs[a].at[quarter, pl.ds(core * half, half), :]

    def ici(a, j, quarter):
        x, y, c = _place()
        px, py = _other_chips(x, y)[j]
        src = ins[a]
        if halved[a]:
            half = src.shape[0] // 2
            src = src.at[pl.ds(c * half, half), :]
        return pltpu.make_async_remote_copy(src_ref=src, dst_ref=part(a, quarter, c), send_sem=send_sems.at[a, j],
                                            recv_sem=recv_sems.at[a, j], device_id=(px, py, c), device_id_type=MESH)

    def passed(a, j, core):
        x, y, c = _place()
        px, py = _other_chips(x, y)[j]
        half = part(a, 2 * px + py, core)
        return pltpu.make_async_remote_copy(src_ref=half, dst_ref=half, send_sem=pass_send_sems.at[a, j],
                                            recv_sem=pass_recv_sems.at[a, j], device_id=(x, y, 1 - c), device_id_type=MESH)

    def start():
        x, y, _ = _place()
        for a in range(n):
            for j in range(3):
                ici(a, j, 2 * x + y).start()

    def hand_on():
        x, y, c = _place()
        for a in range(n):
            for j, (px, py) in enumerate(_other_chips(x, y)):
                ici(a, j, 2 * px + py).wait_recv()
                if halved[a]:
                    passed(a, j, c).start()

    def finish():
        x, y, c = _place()
        for a in range(n):
            for j in range(3):
                if halved[a]:
                    passed(a, j, 1 - c).wait_recv()
                    passed(a, j, c).wait_send()
                ici(a, j, 2 * x + y).wait_send()

    return start, hand_on, finish


def _swap_halves(grads, name):
    n = len(grads)

    def body(*refs):
        ins, outs = refs[:n], refs[n:2 * n]
        send_sems, recv_sems = refs[2 * n:]
        x, y, c = _place()
        started = []
        for a in range(n):
            half = ins[a].shape[1] // 2
            cp = pltpu.make_async_remote_copy(src_ref=ins[a].at[:, pl.ds((1 - c) * half, half), :], dst_ref=outs[a],
                                              send_sem=send_sems.at[a], recv_sem=recv_sems.at[a],
                                              device_id=(x, y, 1 - c), device_id_type=MESH)
            cp.start()
            started.append(cp)
        for cp in started:
            cp.wait()

    return pl.pallas_call(
        body, name=name,
        in_specs=[_ANY] * n, out_specs=[_ANY] * n,
        out_shape=[jax.ShapeDtypeStruct((4, g.shape[1] // 2, g.shape[2]), F32) for g in grads],
        scratch_shapes=[pltpu.SemaphoreType.DMA((n,)), pltpu.SemaphoreType.DMA((n,))],
    )(*grads)


def _scatter_quarters(sums16):
    n = len(sums16)

    def body(*refs):
        start, finish = _scatter_ops(refs[:n], refs[n:2 * n], *refs[2 * n:])
        start()
        finish()

    return pl.pallas_call(
        body, name="scatter_quarters",
        in_specs=[_ANY] * n, out_specs=[_ANY] * n,
        out_shape=_scatter_shapes(sums16), scratch_shapes=_scatter_sems(n),
    )(*sums16)


def _scatter_shapes(sums16):
    return [jax.ShapeDtypeStruct((3,) + s.shape[1:], BF16) for s in sums16]


def _scatter_sems(n):
    return [pltpu.SemaphoreType.DMA((n, 3))] * 2


def _scatter_ops(ins, outs, send_sems, recv_sems):
    n = len(ins)

    def copy(a, j):
        x, y, c = _place()
        px, py = _other_chips(x, y)[j]
        return pltpu.make_async_remote_copy(src_ref=ins[a].at[2 * px + py], dst_ref=outs[a].at[j], send_sem=send_sems.at[a, j],
                                            recv_sem=recv_sems.at[a, j], device_id=(px, py, c), device_id_type=MESH)

    def start():
        for a in range(n):
            for j in range(3):
                copy(a, j).start()

    def finish():
        for a in range(n):
            for j in range(3):
                copy(a, j).wait()

    return start, finish


def _join_halves(fulls):
    n = len(fulls)

    def body(*refs):
        ins, outs = refs[:n], refs[n:2 * n]
        send_sems, recv_sems = refs[2 * n:]
        x, y, c = _place()
        started = []
        for a in range(n):
            half = ins[a].shape[0] // 2
            rows = pl.ds(c * half, half)
            cp = pltpu.make_async_remote_copy(src_ref=ins[a].at[rows, :], dst_ref=outs[a].at[rows, :], send_sem=send_sems.at[a],
                                              recv_sem=recv_sems.at[a], device_id=(x, y, 1 - c), device_id_type=MESH)
            cp.start()
            started.append(cp)
        for cp in started:
            cp.wait()

    return pl.pallas_call(
        body, name="join_halves",
        in_specs=[_ANY] * n, out_specs=[_ANY] * n,
        out_shape=[jax.ShapeDtypeStruct(f.shape, F32) for f in fulls],
        input_output_aliases={a: a for a in range(n)},
        scratch_shapes=[pltpu.SemaphoreType.DMA((n,)), pltpu.SemaphoreType.DMA((n,))],
    )(*fulls)


def _small_allreduce_adamw(g, w, m, v):
    R = g.shape[0]

    def body(g_ref, w_ref, m_ref, v_ref, gs_ref, d_ref, nm_ref, nv_ref, all_s, send_sems, recv_sems):
        x, y, c = _place()
        me = 4 * x + 2 * y + c
        all_s[me] = g_ref[...]
        sends = []
        for k in range(1, 8):
            peer = (x ^ (k >> 2), y ^ ((k >> 1) & 1), c ^ (k & 1))
            cp = pltpu.make_async_remote_copy(src_ref=g_ref, dst_ref=all_s.at[me], send_sem=send_sems.at[k - 1],
                                              recv_sem=recv_sems.at[k - 1], device_id=peer, device_id_type=MESH)
            cp.start()
            sends.append(cp)
        for cp in sends:
            cp.wait()
        total = all_s[0]
        for d in range(1, 8):
            total = total + all_s[d]
        gs_ref[...] = total
        delta, nm, nv = _adamw_math(w_ref[...], total, m_ref[...], v_ref[...])
        d_ref[...] = delta
        nm_ref[...] = nm
        nv_ref[...] = nv

    vm = pl.BlockSpec(memory_space=pltpu.VMEM)
    return pl.pallas_call(
        body, name="small_allreduce_adamw",
        in_specs=[vm] * 4, out_specs=[vm] * 4, out_shape=[jax.ShapeDtypeStruct((R, LANES), F32)] * 4,
        scratch_shapes=[pltpu.VMEM((8, R, LANES), F32), pltpu.SemaphoreType.DMA((7,)), pltpu.SemaphoreType.DMA((7,))],
        compiler_params=pltpu.CompilerParams(vmem_limit_bytes=VMEM_LIMIT),
    )(g, w, m, v)


_SMALL = (("norm_mix_g", D_MODEL), ("f_bias", N_HEADS), ("sg_ln_g", D_HEADS), ("sg_w", N_HEADS * SG_BLOCK * SG_BLOCK),
          ("sg_b", N_HEADS * SG_BLOCK), ("norm_ffn_g", D_MODEL), ("w_conv", 3 * 2 * D_FF), ("b_conv", 2 * D_FF),
          ("norm_final_g", D_MODEL))


def _pack_small(parts):
    rows = []
    for name, size in _SMALL:
        flat = parts[name].reshape(-1).astype(F32)
        pad = (-size) % (SUBLANES * LANES)
        rows.append(jnp.pad(flat, (0, pad)).reshape(-1, LANES))
    return jnp.concatenate(rows, axis=0)


def _unpack_small(packed, shapes):
    out, r = {}, 0
    for name, size in _SMALL:
        nrows = (size + SUBLANES * LANES - 1) // (SUBLANES * LANES) * SUBLANES
        out[name] = packed[r:r + nrows].reshape(-1)[:size].reshape(shapes[name])
        r += nrows
    return out


def _local_step(x, target, g1, w_in, f_bias, sg_ln_g, sg_w, sg_b, g2, b_conv, g3, late_shards, quarter, ids):
    S = x.shape[0]
    tm = _row_tile(S, 512)
    tms = _row_tile(S, 256)
    tq = _row_tile(S, 512)

    lane = jnp.arange(D_HEADS)
    seg_avg = jnp.where(lane[:, None] // HEAD_DIM == lane[None, :] // HEAD_DIM, 1.0 / HEAD_DIM, 0.0).astype(BF16)
    head_ind = (lane[:, None] // HEAD_DIM == jnp.arange(LANES)[None, :]).astype(BF16)
    pos_chunk = jnp.arange(SG_BLOCK) // CHUNK
    w_mask32 = jnp.where(pos_chunk[:, None] >= pos_chunk[None, :], sg_w, 0.0)
    w_mask = w_mask32.astype(BF16)
    w_mask_t = jnp.swapaxes(w_mask32, 1, 2).astype(BF16)
    ln_row = sg_ln_g.reshape(1, D_HEADS)
    b_full = jnp.repeat(sg_b.T, HEAD_DIM, axis=1)
    bias_row = jnp.pad(f_bias.reshape(1, N_HEADS), ((0, 0), (0, LANES - N_HEADS)))
    b_conv_row = b_conv.reshape(1, 2 * D_FF)

    z, f, h1 = _in_proj(x, g1, w_in, tm)
    c = _fox_prep(f, bias_row, _row_tile(S, 256))
    consts = _attn_consts()
    qa, ka, va, vat = _attn_pack(z, c, consts, tm)
    out_b, lse, gathered = _attn_fwd(qa, ka, vat, consts["place_t"], tq, late_shards)
    g_out, w_up_q, g_down, g_conv = [lax.dynamic_update_index_in_dim(g, s, quarter, 0) for g, s in zip(gathered, late_shards)]
    w_out = g_out.reshape(D_MODEL, D_MODEL)
    w_down = g_down.reshape(D_FF, D_MODEL)
    w_conv = jnp.concatenate([g_conv[q] for q in range(4)], axis=1)
    out_a = _gate_fwd(z, w_mask, ln_row, b_full, seg_avg, tm)
    x1, h2 = _mix_out(x, out_a, out_b, w_out, g2, tm)
    a = _up_proj(h2, w_up_q, tm)
    dx2, sq_err, dg3 = _ffn_fwd_loss(a, w_conv, b_conv_row, w_down, x1, g3, target, tms)

    dconv, y, dw_conv8, db_conv = _ffn_bwd_gate(dx2, a, w_conv, b_conv_row, w_down, tms)
    dact = _conv_bwd(dconv, w_conv, tm, 2 * D_FF // 4)
    dw_down = _matmul_tn(y, dx2, "dw_down", D_FF // 2, D_MODEL, tm, quarters=(2, 1))
    dx1, dg2 = _up_bwd(dact, w_up_q, x1, g2, dx2, tms)
    dw_up_q = _matmul_tn(h2, dact, "dw_up", D_MODEL, 2 * D_FF // 4, tm, quarters=(1, 4))
    early = {"w_down": dw_down.reshape(4, D_FF // 4, D_MODEL), "w_up": dw_up_q}
    early_sums = _chip_sums(early, ids, "early")
    dcat = _out_bwd(dx1, w_out, tm)
    dw_out_a = _matmul_tn(out_a, dx1, "dw_out_a", D_HEADS, D_MODEL, tm)
    dw_out_b = _matmul_tn(out_b, dx1, "dw_out_b", D_HEADS, D_MODEL, tm)
    dzu, dzv, dsg_w, dsg_b_t, dln = _gate_bwd(z, dcat, w_mask, w_mask_t, ln_row, b_full, seg_avg, head_ind, tm)
    dop, qb, dopt, qbt = _attn_pack_grad(out_b, dcat, qa, lse, head_ind, consts, tm)
    dqp, dk, dv, dcc, landed = _attn_bwd(qb, qbt, ka, va, dop, dopt, consts, tq, [s16 for _, s16 in early_sums.values()])
    early_parts = {k: (s32, got) for (k, (s32, _)), got in zip(early_sums.items(), landed)}
    dq, dc = _attn_unpack(dqp, dcc, consts, tm)
    df, dbias = _fox_bwd(dc, f, bias_row, _row_tile(S, 256))
    pieces = (dzu, dzv, dq, dk, dv, df)
    dx, dg1 = _in_bwd(pieces, w_in, x, g1, dx1, tms)
    dw_in = _dw_in(h1, pieces, tm)

    grads = {
        "norm_mix_g": dg1, "f_bias": dbias[:, :N_HEADS], "sg_ln_g": dln, "sg_w": dsg_w, "sg_b": dsg_b_t[:, :N_HEADS].T,
        "norm_ffn_g": dg2, "w_conv": dw_conv8[:3], "b_conv": db_conv, "norm_final_g": dg3,
        "w_in": dw_in, "w_out": jnp.concatenate([dw_out_a, dw_out_b], axis=0),
    }
    return sq_err, dx, grads, early_parts


def _chip_sums(grads_q, ids, tag):
    names = list(grads_q)
    theirs = _swap_halves([grads_q[k] for k in names], "swap_halves_" + tag)
    return {k: _pair_sum("pair_sum_" + k, grads_q[k], t, ids) for k, t in zip(names, theirs)}


def _finish_reduction(parts, ids):
    names = list(parts)
    fulls = [_chip_sum("chip_sum_" + k, s32, got, ids) for k, (s32, got) in parts.items()]
    return dict(zip(names, _join_halves(fulls)))


def kernel(x, norm_mix_g, w_in, f_bias, sg_ln_g, sg_w, sg_b, w_out, norm_ffn_g, w_up, w_conv, b_conv, w_down, norm_final_g, loss_target, m_norm_mix_g, m_w_in, m_f_bias, m_sg_ln_g, m_sg_w, m_sg_b, m_w_out, m_norm_ffn_g, m_w_up, m_w_conv, m_b_conv, m_w_down, m_norm_final_g, v_norm_mix_g, v_w_in, v_f_bias, v_sg_ln_g, v_sg_w, v_sg_b, v_w_out, v_norm_ffn_g, v_w_up, v_w_conv, v_b_conv, v_w_down, v_norm_final_g):
    args = dict(locals())
    quarter = 2 * lax.axis_index("x") + lax.axis_index("y")
    ids = jnp.stack([quarter, lax.axis_index("c")]).astype(jnp.int32)
    wq_conv = w_conv.shape[-1]

    w_in16 = w_in[0].astype(BF16)
    g_in = lax.dynamic_update_index_in_dim(_gather_quarters([w_in16])[0], w_in16, quarter, 0)
    w_in_full = jnp.pad(jnp.concatenate([g_in[q] for q in range(4)], axis=1), ((0, 0), (0, D_IN_PAD - D_IN)))
    late_shards = [w_out[0].astype(BF16), w_up[0].astype(BF16), w_down[0].astype(BF16), w_conv[0]]

    sq_err, dx, grads, early_parts = _local_step(
        x[0], loss_target[0], norm_mix_g, w_in_full, f_bias[0], sg_ln_g[0], sg_w[0], sg_b[0], norm_ffn_g, b_conv[0],
        norm_final_g.reshape(1, D_MODEL), late_shards, quarter, ids)
    loss = lax.psum(0.5 * jnp.sum(sq_err) / D_MODEL, ("x", "y", "c"))

    dw_in = grads["w_in"][:, :D_IN].reshape(D_MODEL, 4, D_IN // 4).transpose(1, 0, 2)
    late_sums = _chip_sums({"w_in": dw_in, "w_out": grads["w_out"].reshape(4, D_MODEL // 4, D_MODEL)}, ids, "late")
    landed = _scatter_quarters([s16 for _, s16 in late_sums.values()])
    late_parts = {k: (s32, got) for (k, (s32, _)), got in zip(late_sums.items(), landed)}
    big = _finish_reduction({**early_parts, **late_parts}, ids)

    out = {"loss": loss, "grad_x": dx[None]}
    for k in ("w_in", "w_out", "w_up", "w_down"):
        g = big[k]
        d, nm, nv = _adamw("adamw_" + k, args[k][0], g, args["m_" + k][0], args["v_" + k][0])
        out["grad_" + k], out["delta_" + k], out["new_m_" + k], out["new_v_" + k] = g[None], d[None], nm[None], nv[None]

    def padded_conv(t):
        return lax.dynamic_update_slice(jnp.zeros((3, 4 * wq_conv), F32), t[0], (0, quarter * wq_conv))

    small_names = [n for n, _ in _SMALL]
    shapes = {n: (3, 4 * wq_conv) if n == "w_conv" else args[n].shape for n in small_names}
    pack = lambda prefix: _pack_small({n: padded_conv(args[prefix + n]) if n == "w_conv" else args[prefix + n] for n in small_names})
    packed = _small_allreduce_adamw(_pack_small({n: grads[n] for n in small_names}), pack(""), pack("m_"), pack("v_"))
    for prefix, arr in zip(("grad_", "delta_", "new_m_", "new_v_"), packed):
        for n, t in _unpack_small(arr, shapes).items():
            if n == "w_conv":
                t = lax.dynamic_slice(t, (0, quarter * wq_conv), (3, wq_conv))[None]
            out[prefix + n] = t

    weights = ["norm_mix_g", "w_in", "f_bias", "sg_ln_g", "sg_w", "sg_b", "w_out", "norm_ffn_g", "w_up", "w_conv", "b_conv",
               "w_down", "norm_final_g"]
    return (out["loss"], out["grad_x"], *[out[p + n] for p in ("grad_", "delta_", "new_m_", "new_v_") for n in weights])
```

```python
import functools
import math

import jax
import jax.numpy as jnp
from jax import lax
from jax.experimental import pallas as pl
from jax.experimental.pallas import tpu as pltpu

F32 = jnp.float32
BF16 = jnp.bfloat16
MESH = pl.DeviceIdType.MESH

D_MODEL = 1024
N_HEADS = 8
HEAD_DIM = 64
D_HEADS = N_HEADS * HEAD_DIM
SG_BLOCK = 128
CHUNK = 64
D_FF = 2816
D_IN = 2 * D_HEADS + 3 * D_HEADS + N_HEADS
LANES = 128
SUBLANES = 8
D_IN_PAD = 5 * D_HEADS + LANES
EPS = 1e-6
SCALE = HEAD_DIM ** -0.5
NEG = -1e30
LOG2E = 1.4426950408889634
HEAD_PAD = LANES
D_PAD = N_HEADS * HEAD_PAD
Q_STAT = HEAD_DIM
K_STAT = HEAD_DIM + 3
L_STAT = HEAD_DIM + 6
GROUPS = 2
GROUP_HEADS = N_HEADS // GROUPS
GROUP_PAD = GROUP_HEADS * HEAD_PAD
KEY_CHUNK = 256
STAT_ROWS = 16
FF_CHUNK = 256

ADAM_LR = 0.001
ADAM_B1 = 0.9
ADAM_B2 = 0.999
ADAM_EPS = 1e-08
ADAM_WD = 0.01
ADAM_STEP = 10

VMEM_LIMIT = 56 * 1024 * 1024

NT = (((1,), (1,)), ((), ()))
TN = (((0,), (0,)), ((), ()))


def _params(sem):
    return pltpu.CompilerParams(dimension_semantics=sem, vmem_limit_bytes=VMEM_LIMIT)


def _full(shape):
    nd = len(shape)
    return pl.BlockSpec(shape, lambda *_: (0,) * nd)


def _row_tile(rows, target):
    best = None
    for t in range(SUBLANES, min(rows, target) + 1, SUBLANES):
        if rows % t == 0:
            best = t
    assert best is not None, rows
    return best


def _sigmoid(x):
    return 0.5 * jnp.tanh(0.5 * x) + 0.5


def _gelu(z):
    return 0.5 * z * (1.0 + lax.erf(z * (2.0 ** -0.5)))


def _gelu_grad(z):
    cdf = 0.5 * (1.0 + lax.erf(z * (2.0 ** -0.5)))
    pdf = jnp.exp(-0.5 * z * z) * (1.0 / math.sqrt(2.0 * math.pi))
    return cdf + z * pdf


def _split_dot(x, m):
    hi = x.astype(BF16)
    lo = (x - hi.astype(F32)).astype(BF16)
    return jnp.dot(hi, m, preferred_element_type=F32) + jnp.dot(lo, m, preferred_element_type=F32)


def _head_mask(h, rows):
    lane = lax.broadcasted_iota(jnp.int32, (rows, D_HEADS), 1)
    return (lane >= h * HEAD_DIM) & (lane < (h + 1) * HEAD_DIM)


def _rms_bwd(dh, x, g):
    r = lax.rsqrt(jnp.mean(x * x, axis=-1, keepdims=True) + EPS)
    xhat = x * r
    dg = jnp.sum(dh * xhat, axis=0, keepdims=True)
    dxhat = dh * g
    dx = r * (dxhat - xhat * jnp.mean(dxhat * xhat, axis=-1, keepdims=True))
    return dx, dg


def _in_proj(x, g1, w_in, tm):
    S = x.shape[0]
    nz = D_IN_PAD - LANES

    def body(x_ref, g_ref, w_ref, z_ref, f_ref, h_ref):
        xf = x_ref[...]
        r = lax.rsqrt(jnp.mean(xf * xf, axis=-1, keepdims=True) + EPS)
        h = (xf * r * g_ref[...]).astype(BF16)
        h_ref[...] = h
        zz = jnp.dot(h, w_ref[...], preferred_element_type=F32)
        z_ref[...] = zz[:, :nz].astype(BF16)
        f_ref[...] = zz[:, nz:]

    return pl.pallas_call(
        body, name="in_proj", grid=(S // tm,),
        in_specs=[pl.BlockSpec((tm, D_MODEL), lambda i: (i, 0)), _full((1, D_MODEL)), _full((D_MODEL, D_IN_PAD))],
        out_specs=[pl.BlockSpec((tm, nz), lambda i: (i, 0)), pl.BlockSpec((tm, LANES), lambda i: (i, 0)),
                   pl.BlockSpec((tm, D_MODEL), lambda i: (i, 0))],
        out_shape=[jax.ShapeDtypeStruct((S, nz), BF16), jax.ShapeDtypeStruct((S, LANES), F32),
                   jax.ShapeDtypeStruct((S, D_MODEL), BF16)],
        compiler_params=_params(("parallel",)),
    )(x, g1, w_in)


def _fox_prep(f, bias_row, tb):
    S = f.shape[0]

    def body(f_ref, b_ref, c_ref, carry):
        @pl.when(pl.program_id(0) == 0)
        def _():
            carry[...] = jnp.zeros_like(carry)

        xv = f_ref[...] + b_ref[...]
        lf = jnp.minimum(xv, 0.0) - jnp.log(1.0 + jnp.exp(-jnp.abs(xv)))
        r = lax.broadcasted_iota(jnp.int32, (tb, tb), 0)
        s = lax.broadcasted_iota(jnp.int32, (tb, tb), 1)
        tri = (r >= s).astype(F32)
        cs = jnp.dot(tri, lf, precision=lax.Precision.HIGHEST, preferred_element_type=F32) + carry[0:1, :]
        c_ref[...] = cs
        carry[...] = jnp.broadcast_to(cs[tb - 1:tb, :], carry.shape)

    return pl.pallas_call(
        body, name="fox_prep", grid=(S // tb,),
        in_specs=[pl.BlockSpec((tb, LANES), lambda i: (i, 0)), _full((1, LANES))],
        out_specs=pl.BlockSpec((tb, LANES), lambda i: (i, 0)),
        out_shape=jax.ShapeDtypeStruct((S, LANES), F32),
        scratch_shapes=[pltpu.VMEM((SUBLANES, LANES), F32)],
        compiler_params=_params(("arbitrary",)),
    )(f, bias_row)


def _attn_consts():
    col = jnp.arange(D_PAD)
    row = jnp.arange(D_HEADS)
    head = jnp.arange(LANES)
    place = (row[:, None] // HEAD_DIM == col[None, :] // HEAD_PAD) & (row[:, None] % HEAD_DIM == col[None, :] % HEAD_PAD)

    def stat(offset):
        return ((head[:, None] < N_HEADS) & (col[None, :] == head[:, None] * HEAD_PAD + offset)).astype(BF16)

    def ones(offsets):
        return sum((col % HEAD_PAD == o) for o in offsets).astype(F32).reshape(1, D_PAD)

    def pick(offset):
        gcol = jnp.arange(GROUP_PAD)
        return jnp.stack([((gcol[:, None] % HEAD_PAD == offset) & (head[None, :] == g * GROUP_HEADS + gcol[:, None] // HEAD_PAD))
                          for g in range(GROUPS)]).astype(BF16)

    place = place.astype(BF16)
    return {
        "place": place, "place_t": place.T, "place_t_group": place.T[:GROUP_PAD, :GROUP_HEADS * HEAD_DIM],
        "q_stat": jnp.stack([stat(Q_STAT + j) for j in range(3)]), "k_stat": jnp.stack([stat(K_STAT + j) for j in range(3)]),
        "d_stat": jnp.stack([stat(Q_STAT + j) for j in range(2)]), "l_stat": jnp.stack([stat(L_STAT + j)[:STAT_ROWS] for j in range(3)]),
        "q_ones": ones(range(K_STAT, K_STAT + 3)), "k_ones": ones(list(range(Q_STAT, Q_STAT + 3)) + list(range(L_STAT, L_STAT + 3))),
        "v_ones": ones(range(Q_STAT, Q_STAT + 2)),
        "pick_rows": pick(Q_STAT), "pick_cols": pick(K_STAT),
    }


def _split3(x):
    hi = x.astype(BF16)
    r = x - hi.astype(F32)
    mid = r.astype(BF16)
    return hi, mid, (r - mid.astype(F32)).astype(BF16)


def _split3_dot(x, m):
    return sum(jnp.dot(part, m, preferred_element_type=F32) for part in _split3(x))


def _attn_pack(z, c, k, tm):
    S = z.shape[0]

    def body(q_ref, k_ref, v_ref, c_ref, pl_ref, pt_ref, qs_ref, ks_ref, qo_ref, ko_ref, vo_ref, voc_ref,
             qa_ref, ka_ref, va_ref, vt_ref):
        place = pl_ref[...]
        q = (q_ref[...].astype(F32) * (SCALE * LOG2E)).astype(BF16)
        qa = jnp.dot(q, place, preferred_element_type=F32) + qo_ref[...]
        ka = jnp.dot(k_ref[...], place, preferred_element_type=F32) + ko_ref[...]
        for j, part in enumerate(_split3(c_ref[...] * LOG2E)):
            qa = qa + jnp.dot(part, qs_ref[j], preferred_element_type=F32)
            ka = ka - jnp.dot(part, ks_ref[j], preferred_element_type=F32)
        qa_ref[...] = qa.astype(BF16)
        ka_ref[...] = ka.astype(BF16)
        v = v_ref[...]
        va_ref[...] = (jnp.dot(v, place, preferred_element_type=F32) + vo_ref[...]).astype(BF16)
        vt_ref[...] = (lax.dot_general(pt_ref[...], v, NT, preferred_element_type=F32) + voc_ref[...]).astype(BF16)

    blk = lambda col: pl.BlockSpec((tm, D_HEADS), lambda i: (i, col))
    out = pl.BlockSpec((tm, D_PAD), lambda i: (i, 0))
    pad = jax.ShapeDtypeStruct((S, D_PAD), BF16)
    return pl.pallas_call(
        body, name="attn_pack", grid=(S // tm,),
        in_specs=[blk(2), blk(3), blk(4), pl.BlockSpec((tm, LANES), lambda i: (i, 0)), _full((D_HEADS, D_PAD)), _full((D_PAD, D_HEADS)),
                  _full((3, LANES, D_PAD)), _full((3, LANES, D_PAD)), _full((1, D_PAD)), _full((1, D_PAD)), _full((1, D_PAD)),
                  _full((D_PAD, 1))],
        out_specs=[out, out, out, pl.BlockSpec((D_PAD, tm), lambda i: (0, i))],
        out_shape=[pad, pad, pad, jax.ShapeDtypeStruct((D_PAD, S), BF16)],
        compiler_params=_params(("parallel",)),
    )(z, z, z, c, k["place"], k["place_t"], k["q_stat"], k["k_stat"], k["q_ones"], k["k_ones"], k["v_ones"], k["v_ones"].T)


def _attn_fwd(qa, ka, vat, place_t, tq, shards):
    S = qa.shape[0]
    n = S // tq
    ns = len(shards)
    hand_on_at = (2 * n) // 3

    def body(q_ref, k_ref, vt_ref, pt_ref, *rest):
        o_ref, lse_ref = rest[ns:ns + 2]
        m_s, acc_s, ot_s, s_s = rest[2 * ns + 2:2 * ns + 6]
        start, hand_on, finish = _gather_ops(rest[:ns], rest[ns + 2:2 * ns + 2], *rest[2 * ns + 6:])
        qi, ki = pl.program_id(0), pl.program_id(1)

        @pl.when((qi == 0) & (ki == 0))
        def _():
            start()

        @pl.when((qi == hand_on_at) & (ki == 0))
        def _():
            hand_on()

        @pl.when(ki == 0)
        def _():
            m_s[...] = jnp.full_like(m_s, NEG)
            acc_s[...] = jnp.zeros_like(acc_s)

        def step(diagonal):
            chunks = [slice(c * KEY_CHUNK, (c + 1) * KEY_CHUNK) for c in range(tq // KEY_CHUNK)]

            def scores(h, rows, slot):
                sl = slice(h * HEAD_PAD, (h + 1) * HEAD_PAD)
                st = lax.dot_general(k_ref[rows, sl], q_ref[:, sl], NT, preferred_element_type=F32)
                if diagonal:
                    key = rows.start + lax.broadcasted_iota(jnp.int32, (KEY_CHUNK, tq), 0)
                    query = lax.broadcasted_iota(jnp.int32, (KEY_CHUNK, tq), 1)
                    st = jnp.where(query >= key, st, NEG)
                s_s[slot, rows, :] = st
                return jnp.max(st, axis=0, keepdims=True)

            m_cur = functools.reduce(jnp.maximum, [scores(0, rows, 0) for rows in chunks])
            for h in range(N_HEADS):
                sl = slice(h * HEAD_PAD, (h + 1) * HEAD_PAD)
                slot = h % 2
                m_prev = m_s[h][0:1, :]
                m_new = jnp.maximum(m_prev, m_cur)
                acc = jnp.exp2(m_prev - m_new) * acc_s[h]
                m_next = []
                for rows in chunks:
                    if h + 1 < N_HEADS:
                        m_next.append(scores(h + 1, rows, 1 - slot))
                    pt = jnp.exp2(s_s[slot, rows, :] - m_new).astype(BF16)
                    acc = acc + jnp.dot(vt_ref[sl, rows], pt, preferred_element_type=F32)
                acc_s[h] = acc
                m_s[h] = jnp.broadcast_to(m_new, (SUBLANES, tq))
                if m_next:
                    m_cur = functools.reduce(jnp.maximum, m_next)

        @pl.when(ki < qi)
        def _():
            step(False)

        @pl.when(ki == qi)
        def _():
            step(True)
            lse_ref[...] = jnp.zeros_like(lse_ref)
            for h in range(N_HEADS):
                acc = acc_s[h]
                denom = acc[Q_STAT:Q_STAT + 1, :]
                ot_s[h * HEAD_PAD:(h + 1) * HEAD_PAD, :] = (acc / denom).astype(BF16)
                lse_ref[h:h + 1, :] = m_s[h][0:1, :] + jnp.log(denom) * LOG2E
            o_ref[...] = lax.dot_general(ot_s[...], pt_ref[...], TN, preferred_element_type=F32).astype(BF16)

        @pl.when((qi == n - 1) & (ki == n - 1))
        def _():
            finish()

    kmin = lambda qi, ki: jnp.minimum(ki, qi)
    out = pl.pallas_call(
        body, name="attn_fwd", grid=(n, n),
        in_specs=[pl.BlockSpec((tq, D_PAD), lambda qi, ki: (qi, 0)), pl.BlockSpec((tq, D_PAD), lambda qi, ki: (kmin(qi, ki), 0)),
                  pl.BlockSpec((D_PAD, tq), lambda qi, ki: (0, kmin(qi, ki))), _full((D_PAD, D_HEADS))] + [_ANY] * ns,
        out_specs=[pl.BlockSpec((tq, D_HEADS), lambda qi, ki: (qi, 0)), pl.BlockSpec((STAT_ROWS, tq), lambda qi, ki: (0, qi))]
        + [_ANY] * ns,
        out_shape=[jax.ShapeDtypeStruct((S, D_HEADS), BF16), jax.ShapeDtypeStruct((STAT_ROWS, S), F32)] + _gather_shapes(shards),
        scratch_shapes=[pltpu.VMEM((N_HEADS, SUBLANES, tq), F32), pltpu.VMEM((N_HEADS, HEAD_PAD, tq), F32),
                        pltpu.VMEM((D_PAD, tq), BF16), pltpu.VMEM((2, tq, tq), F32)] + _gather_sems(ns),
        compiler_params=_params(("arbitrary", "arbitrary")),
    )(qa, ka, vat, place_t, *shards)
    return out[0], out[1], out[2:]


def _layer_norm_heads(v, seg_avg):
    mu = _split_dot(v, seg_avg)
    d = v - mu
    var = _split_dot(d * d, seg_avg)
    rstd = lax.rsqrt(var + EPS)
    return d * rstd, rstd


def _gate_mix(vn_blk, w_ref, bias):
    acc = bias
    for h in range(N_HEADS):
        vh = jnp.where(_head_mask(h, SG_BLOCK), vn_blk, 0.0).astype(BF16)
        acc = acc + jnp.dot(w_ref[h], vh, preferred_element_type=F32)
    return acc


def _gate_fwd(z, w_mask, ln_row, b_full, seg_avg, tm):
    S = z.shape[0]

    def body(zu_ref, zv_ref, w_ref, ln_ref, b_ref, avg_ref, o_ref):
        u = _gelu(zu_ref[...].astype(F32))
        v = _gelu(zv_ref[...].astype(F32))
        vhat, _ = _layer_norm_heads(v, avg_ref[...])
        vn = vhat * ln_ref[...]
        for b in range(tm // SG_BLOCK):
            rows = slice(b * SG_BLOCK, (b + 1) * SG_BLOCK)
            mixed = _gate_mix(vn[rows], w_ref, b_ref[...])
            o_ref[rows, :] = (u[rows] * mixed).astype(BF16)

    return pl.pallas_call(
        body, name="gate_fwd", grid=(S // tm,),
        in_specs=[pl.BlockSpec((tm, D_HEADS), lambda i: (i, 0)), pl.BlockSpec((tm, D_HEADS), lambda i: (i, 1)),
                  _full((N_HEADS, SG_BLOCK, SG_BLOCK)), _full((1, D_HEADS)), _full((SG_BLOCK, D_HEADS)),
                  _full((D_HEADS, D_HEADS))],
        out_specs=pl.BlockSpec((tm, D_HEADS), lambda i: (i, 0)),
        out_shape=jax.ShapeDtypeStruct((S, D_HEADS), BF16),
        compiler_params=_params(("parallel",)),
    )(z, z, w_mask, ln_row, b_full, seg_avg)


def _mix_out(x, out_a, out_b, w_out, g2, tm):
    S = x.shape[0]

    def body(x_ref, a_ref, b_ref, w_ref, g_ref, x1_ref, h_ref):
        y = jnp.dot(a_ref[...], w_ref[:D_HEADS, :], preferred_element_type=F32)
        y = y + jnp.dot(b_ref[...], w_ref[D_HEADS:, :], preferred_element_type=F32)
        x1 = x_ref[...] + y
        x1_ref[...] = x1
        r = lax.rsqrt(jnp.mean(x1 * x1, axis=-1, keepdims=True) + EPS)
        h_ref[...] = (x1 * r * g_ref[...]).astype(BF16)

    row = lambda w: pl.BlockSpec((tm, w), lambda i: (i, 0))
    return pl.pallas_call(
        body, name="mix_out", grid=(S // tm,),
        in_specs=[row(D_MODEL), row(D_HEADS), row(D_HEADS), _full((D_MODEL, D_MODEL)), _full((1, D_MODEL))],
        out_specs=[row(D_MODEL), row(D_MODEL)],
        out_shape=[jax.ShapeDtypeStruct((S, D_MODEL), F32), jax.ShapeDtypeStruct((S, D_MODEL), BF16)],
        compiler_params=_params(("parallel",)),
    )(x, out_a, out_b, w_out, g2)


def _up_proj(h2, w_up_q, tm):
    S = h2.shape[0]
    nq, _, wq = w_up_q.shape

    def body(h_ref, w_ref, a_ref):
        a_ref[...] = jnp.dot(h_ref[...], w_ref[...], preferred_element_type=F32).astype(BF16)

    return pl.pallas_call(
        body, name="up_proj", grid=(nq, S // tm),
        in_specs=[pl.BlockSpec((tm, D_MODEL), lambda j, i: (i, 0)), pl.BlockSpec((None, D_MODEL, wq), lambda j, i: (j, 0, 0))],
        out_specs=pl.BlockSpec((tm, wq), lambda j, i: (i, j)),
        out_shape=jax.ShapeDtypeStruct((S, nq * wq), BF16),
        compiler_params=_params(("parallel", "parallel")),
    )(h2, w_up_q)


def _shift_down(a, halo, k):
    tm = a.shape[0]
    ra = pltpu.roll(a, k, 0)
    rh = pltpu.roll(halo, k, 0)
    row = lax.broadcasted_iota(jnp.int32, halo.shape, 0)
    top = jnp.where(row < k, rh, ra[0:SUBLANES])
    return jnp.concatenate([top, ra[SUBLANES:tm]], axis=0)


def _shift_up(a, halo, k):
    tm = a.shape[0]
    ra = pltpu.roll(a, tm - k, 0)
    rh = pltpu.roll(halo, SUBLANES - k, 0)
    row = lax.broadcasted_iota(jnp.int32, halo.shape, 0)
    bottom = jnp.where(row >= SUBLANES - k, rh, ra[tm - SUBLANES:tm])
    return jnp.concatenate([ra[0:tm - SUBLANES], bottom], axis=0)


def _shift_matrices(tm):
    row = lax.broadcasted_iota(jnp.int32, (tm, tm), 0)
    col = lax.broadcasted_iota(jnp.int32, (tm, tm), 1)
    return [(row == col + k).astype(BF16) for k in (1, 2)]


def _conv_taps(a, halo, first, shifts):
    tm = a.shape[0]
    halo = halo.astype(F32) * jnp.where(first, 0.0, 1.0)
    if shifts is None:
        a = a.astype(F32)
        return a, _shift_down(a, halo, 1), _shift_down(a, halo, 2)
    row8 = lax.broadcasted_iota(jnp.int32, halo.shape, 0)
    taps = [a.astype(F32)]
    for k, shift in zip((1, 2), shifts):
        down = jnp.dot(shift, a, preferred_element_type=F32)
        top = down[0:SUBLANES] + jnp.where(row8 < k, pltpu.roll(halo, k, 0), 0.0)
        taps.append(jnp.concatenate([top, down[SUBLANES:tm]], axis=0))
    return taps


def _conv_gate_val(refs, shifts, cols, first):
    ag_ref, av_ref, hg_ref, hv_ref, wg_ref, wv_ref, bg_ref, bv_ref = refs
    g0, g1, g2 = _conv_taps(ag_ref[:, cols], hg_ref[:, cols], first, shifts)
    gate = wg_ref[2:3, cols] * g0 + wg_ref[1:2, cols] * g1 + wg_ref[0:1, cols] * g2 + bg_ref[:, cols]
    v0, v1, v2 = _conv_taps(av_ref[:, cols], hv_ref[:, cols], first, shifts)
    val = wv_ref[2:3, cols] * v0 + wv_ref[1:2, cols] * v1 + wv_ref[0:1, cols] * v2 + bv_ref[:, cols]
    return gate, val, (g2, g1, g0), (v2, v1, v0)


_FF_CHUNKS = [slice(j * FF_CHUNK, (j + 1) * FF_CHUNK) for j in range(D_FF // FF_CHUNK)]


def _conv_specs(tm):
    step = tm // SUBLANES
    prev = lambda i: jnp.maximum(i * step - 1, 0)
    return [pl.BlockSpec((tm, D_FF), lambda i: (i, 0)), pl.BlockSpec((tm, D_FF), lambda i: (i, 1)),
            pl.BlockSpec((SUBLANES, D_FF), lambda i: (prev(i), 0)), pl.BlockSpec((SUBLANES, D_FF), lambda i: (prev(i), 1))]


def _ffn_fwd_loss(a, w_conv, b_conv, w_down, x1, g3, target, tm):
    S = x1.shape[0]

    def body(ag_ref, av_ref, hg_ref, hv_ref, wg_ref, wv_ref, bg_ref, bv_ref, wd_ref, x1_ref, g_ref, t_ref,
             dx2_ref, loss_ref, dg_ref):
        i = pl.program_id(0)

        @pl.when(i == 0)
        def _():
            loss_ref[...] = jnp.zeros_like(loss_ref)
            dg_ref[...] = jnp.zeros_like(dg_ref)

        x2 = x1_ref[...]
        for cols in _FF_CHUNKS:
            gate, val, _, _ = _conv_gate_val((ag_ref, av_ref, hg_ref, hv_ref, wg_ref, wv_ref, bg_ref, bv_ref), None, cols, i == 0)
            y = (gate * _sigmoid(gate) * val).astype(BF16)
            x2 = x2 + jnp.dot(y, wd_ref[cols, :], preferred_element_type=F32)
        r = lax.rsqrt(jnp.mean(x2 * x2, axis=-1, keepdims=True) + EPS)
        xhat = x2 * r
        gg = g_ref[...]
        err = xhat * gg - t_ref[...]
        loss_ref[...] += jnp.sum(err * err, axis=0, keepdims=True)
        dy = err * (1.0 / D_MODEL)
        dg_ref[...] += jnp.sum(dy * xhat, axis=0, keepdims=True)
        dxhat = dy * gg
        dx2_ref[...] = r * (dxhat - xhat * jnp.mean(dxhat * xhat, axis=-1, keepdims=True))

    row = lambda w: pl.BlockSpec((tm, w), lambda i: (i, 0))
    half = lambda r: [pl.BlockSpec((r, D_FF), lambda i: (0, 0)), pl.BlockSpec((r, D_FF), lambda i: (0, 1))]
    return pl.pallas_call(
        body, name="ffn_fwd_loss", grid=(S // tm,),
        in_specs=_conv_specs(tm) + half(3) + half(1) + [_full((D_FF, D_MODEL)), row(D_MODEL), _full((1, D_MODEL)), row(D_MODEL)],
        out_specs=[row(D_MODEL), _full((1, D_MODEL)), _full((1, D_MODEL))],
        out_shape=[jax.ShapeDtypeStruct((S, D_MODEL), F32), jax.ShapeDtypeStruct((1, D_MODEL), F32),
                   jax.ShapeDtypeStruct((1, D_MODEL), F32)],
        compiler_params=_params(("arbitrary",)),
    )(a, a, a, a, w_conv, w_conv, b_conv, b_conv, w_down, x1, g3, target)


def _ffn_bwd_gate(dx2, a, w_conv, b_conv, w_down, tm):
    S = dx2.shape[0]

    def body(dx_ref, ag_ref, av_ref, hg_ref, hv_ref, wg_ref, wv_ref, bg_ref, bv_ref, wd_ref,
             dc_ref, y_ref, dw_ref, db_ref):
        i = pl.program_id(0)

        @pl.when(i == 0)
        def _():
            dw_ref[...] = jnp.zeros_like(dw_ref)
            db_ref[...] = jnp.zeros_like(db_ref)

        dx = dx_ref[...].astype(BF16)
        shifts = _shift_matrices(tm)
        for cols in _FF_CHUNKS:
            gate, val, gtaps, vtaps = _conv_gate_val((ag_ref, av_ref, hg_ref, hv_ref, wg_ref, wv_ref, bg_ref, bv_ref), shifts, cols, i == 0)
            sg = _sigmoid(gate)
            act = gate * sg
            y_ref[:, cols] = (act * val).astype(BF16)
            dy = lax.dot_general(dx, wd_ref[cols, :], NT, preferred_element_type=F32)
            dgate = dy * val * (sg + act - act * sg)
            dval = dy * act
            for d, taps, out in ((dgate, gtaps, cols), (dval, vtaps, slice(D_FF + cols.start, D_FF + cols.stop))):
                dc_ref[:, out] = d.astype(BF16)
                db_ref[0:1, out] += jnp.sum(d, axis=0, keepdims=True)
                for j in range(3):
                    dw_ref[j:j + 1, out] += jnp.sum(d * taps[j], axis=0, keepdims=True)

    row = lambda w: pl.BlockSpec((tm, w), lambda i: (i, 0))
    half = lambda r: [pl.BlockSpec((r, D_FF), lambda i: (0, 0)), pl.BlockSpec((r, D_FF), lambda i: (0, 1))]
    return pl.pallas_call(
        body, name="ffn_bwd_gate", grid=(S // tm,),
        in_specs=[row(D_MODEL)] + _conv_specs(tm) + half(3) + half(1) + [_full((D_FF, D_MODEL))],
        out_specs=[row(2 * D_FF), row(D_FF), _full((SUBLANES, 2 * D_FF)), _full((1, 2 * D_FF))],
        out_shape=[jax.ShapeDtypeStruct((S, 2 * D_FF), BF16), jax.ShapeDtypeStruct((S, D_FF), BF16),
                   jax.ShapeDtypeStruct((SUBLANES, 2 * D_FF), F32), jax.ShapeDtypeStruct((1, 2 * D_FF), F32)],
        compiler_params=_params(("arbitrary",)),
    )(dx2, a, a, a, a, w_conv, w_conv, b_conv, b_conv, w_down)


def _conv_bwd(dc, w_conv, tm, tn):
    S, C = dc.shape
    step = tm // SUBLANES
    last_blk = S // SUBLANES - 1

    def body(d_ref, nx_ref, w_ref, o_ref):
        last = pl.program_id(0) == pl.num_programs(0) - 1
        d = d_ref[...].astype(F32)
        nx = nx_ref[...].astype(F32) * jnp.where(last, 0.0, 1.0)
        out = w_ref[2:3, :] * d + w_ref[1:2, :] * _shift_up(d, nx, 1) + w_ref[0:1, :] * _shift_up(d, nx, 2)
        o_ref[...] = out.astype(BF16)

    return pl.pallas_call(
        body, name="conv_bwd", grid=(S // tm, C // tn),
        in_specs=[pl.BlockSpec((tm, tn), lambda i, j: (i, j)),
                  pl.BlockSpec((SUBLANES, tn), lambda i, j: (jnp.minimum((i + 1) * step, last_blk), j)),
                  pl.BlockSpec((3, tn), lambda i, j: (0, j))],
        out_specs=pl.BlockSpec((tm, tn), lambda i, j: (i, j)),
        out_shape=jax.ShapeDtypeStruct((S, C), BF16),
        compiler_params=_params(("parallel", "parallel")),
    )(dc, dc, w_conv)


def _matmul_tn(a, b, name, bm, bn, tk, col_a=0, col_b=0, quarters=None):
    S = a.shape[0]
    gm, gn = quarters if quarters else (1, 1)
    nk = S // tk

    def body(a_ref, b_ref, o_ref):
        @pl.when(pl.program_id(2) == 0)
        def _():
            o_ref[...] = jnp.zeros_like(o_ref)

        o_ref[...] += lax.dot_general(a_ref[...].astype(BF16), b_ref[...].astype(BF16), TN, preferred_element_type=F32)

    if quarters and gn > 1:
        out_spec = pl.BlockSpec((None, bm, bn), lambda i, j, k: (j, i, 0))
        out_shape = jax.ShapeDtypeStruct((gn, gm * bm, bn), F32)
    else:
        out_spec = pl.BlockSpec((bm, bn), lambda i, j, k: (i, j))
        out_shape = jax.ShapeDtypeStruct((gm * bm, gn * bn), F32)
    return pl.pallas_call(
        body, name=name, grid=(gm, gn, nk),
        in_specs=[pl.BlockSpec((tk, bm), lambda i, j, k: (k, col_a * gm + i)),
                  pl.BlockSpec((tk, bn), lambda i, j, k: (k, col_b * gn + j))],
        out_specs=out_spec, out_shape=out_shape,
        compiler_params=_params(("parallel", "parallel", "arbitrary")),
    )(a, b)


def _up_bwd(dact, w_up_q, x1, g2, dx2, tm):
    S = x1.shape[0]
    nq, _, wq = w_up_q.shape

    def body(d_ref, w_ref, x_ref, g_ref, dx2_ref, dx1_ref, dg_ref):
        @pl.when(pl.program_id(0) == 0)
        def _():
            dg_ref[...] = jnp.zeros_like(dg_ref)

        dh = jnp.zeros((tm, D_MODEL), F32)
        for j in range(nq):
            dh = dh + lax.dot_general(d_ref[:, j * wq:(j + 1) * wq], w_ref[j], NT, preferred_element_type=F32)
        dx, dg = _rms_bwd(dh, x_ref[...], g_ref[...])
        dg_ref[...] += dg
        dx1_ref[...] = dx2_ref[...] + dx

    row = lambda w: pl.BlockSpec((tm, w), lambda i: (i, 0))
    return pl.pallas_call(
        body, name="up_bwd", grid=(S // tm,),
        in_specs=[row(nq * wq), pl.BlockSpec((nq, D_MODEL, wq), lambda i: (0, 0, 0), pipeline_mode=pl.Buffered(1)),
                  row(D_MODEL), _full((1, D_MODEL)), row(D_MODEL)],
        out_specs=[row(D_MODEL), _full((1, D_MODEL))],
        out_shape=[jax.ShapeDtypeStruct((S, D_MODEL), F32), jax.ShapeDtypeStruct((1, D_MODEL), F32)],
        compiler_params=_params(("arbitrary",)),
    )(dact, w_up_q, x1, g2, dx2)


def _out_bwd(dx1, w_out, tm):
    S = dx1.shape[0]

    def body(d_ref, w_ref, o_ref):
        o_ref[...] = lax.dot_general(d_ref[...].astype(BF16), w_ref[...], NT, preferred_element_type=F32).astype(BF16)

    return pl.pallas_call(
        body, name="out_bwd", grid=(S // tm,),
        in_specs=[pl.BlockSpec((tm, D_MODEL), lambda i: (i, 0)), _full((D_MODEL, D_MODEL))],
        out_specs=pl.BlockSpec((tm, D_MODEL), lambda i: (i, 0)),
        out_shape=jax.ShapeDtypeStruct((S, D_MODEL), BF16),
        compiler_params=_params(("parallel",)),
    )(dx1, w_out)


def _gate_bwd(z, dcat, w_mask, w_mask_t, ln_row, b_full, seg_avg, head_ind, tm):
    S = z.shape[0]
    nb = tm // SG_BLOCK

    def body(zu_ref, zv_ref, do_ref, w_ref, wt_ref, ln_ref, b_ref, avg_ref, ind_ref,
             dzu_ref, dzv_ref, dw_ref, db_ref, dln_ref, dvn_s, dbf_s):
        i = pl.program_id(0)

        @pl.when(i == 0)
        def _():
            dw_ref[...] = jnp.zeros_like(dw_ref)
            dln_ref[...] = jnp.zeros_like(dln_ref)
            dbf_s[...] = jnp.zeros_like(dbf_s)

        zu = zu_ref[...].astype(F32)
        zv = zv_ref[...].astype(F32)
        u = _gelu(zu)
        v = _gelu(zv)
        avg = avg_ref[...]
        vhat, rstd = _layer_norm_heads(v, avg)
        ln = ln_ref[...]
        vn = vhat * ln
        for b in range(nb):
            rows = slice(b * SG_BLOCK, (b + 1) * SG_BLOCK)
            vn_b = vn[rows]
            mixed = _gate_mix(vn_b, w_ref, b_ref[...])
            do = do_ref[rows, :].astype(F32)
            dzu_ref[rows, :] = (do * mixed * _gelu_grad(zu[rows])).astype(BF16)
            dmix = do * u[rows]
            dbf_s[...] += dmix
            vn_bf = vn_b.astype(BF16)
            dvn = jnp.zeros((SG_BLOCK, D_HEADS), F32)
            for h in range(N_HEADS):
                dmh = jnp.where(_head_mask(h, SG_BLOCK), dmix, 0.0).astype(BF16)
                dw_ref[h] += lax.dot_general(dmh, vn_bf, NT, preferred_element_type=F32)
                dvn = dvn + jnp.dot(wt_ref[h], dmh, preferred_element_type=F32)
            dvn_s[rows, :] = dvn
        dvn = dvn_s[...]
        dln_ref[...] += jnp.sum(dvn * vhat, axis=0, keepdims=True)
        dvhat = dvn * ln
        dv = rstd * (dvhat - _split_dot(dvhat, avg) - vhat * _split_dot(dvhat * vhat, avg))
        dzv_ref[...] = (dv * _gelu_grad(zv)).astype(BF16)

        @pl.when(i == pl.num_programs(0) - 1)
        def _():
            r = lax.broadcasted_iota(jnp.int32, (SG_BLOCK, SG_BLOCK), 0) // CHUNK
            s = lax.broadcasted_iota(jnp.int32, (SG_BLOCK, SG_BLOCK), 1) // CHUNK
            for h in range(N_HEADS):
                dw_ref[h] = jnp.where(r >= s, dw_ref[h], 0.0)
            db_ref[...] = _split_dot(dbf_s[...], ind_ref[...])

    row = lambda col: pl.BlockSpec((tm, D_HEADS), lambda i: (i, col))
    wspec = _full((N_HEADS, SG_BLOCK, SG_BLOCK))
    return pl.pallas_call(
        body, name="gate_bwd", grid=(S // tm,),
        in_specs=[row(0), row(1), row(0), wspec, wspec, _full((1, D_HEADS)), _full((SG_BLOCK, D_HEADS)),
                  _full((D_HEADS, D_HEADS)), _full((D_HEADS, LANES))],
        out_specs=[row(0), row(0), wspec, _full((SG_BLOCK, LANES)), _full((1, D_HEADS))],
        out_shape=[jax.ShapeDtypeStruct((S, D_HEADS), BF16), jax.ShapeDtypeStruct((S, D_HEADS), BF16),
                   jax.ShapeDtypeStruct((N_HEADS, SG_BLOCK, SG_BLOCK), F32), jax.ShapeDtypeStruct((SG_BLOCK, LANES), F32),
                   jax.ShapeDtypeStruct((1, D_HEADS), F32)],
        scratch_shapes=[pltpu.VMEM((tm, D_HEADS), F32), pltpu.VMEM((SG_BLOCK, D_HEADS), F32)],
        compiler_params=_params(("arbitrary",)),
    )(z, z, dcat, w_mask, w_mask_t, ln_row, b_full, seg_avg, head_ind)


def _attn_pack_grad(o, dcat, qa, lse, head_ind, k, tm):
    S = o.shape[0]

    def body(o_ref, do_ref, qa_ref, lse_ref, ind_ref, pl_ref, pt_ref, eye_ref, ds_ref, dst_ref, ls_ref, lst_ref,
             dop_ref, qb_ref, dot_ref, qbt_ref):
        do = do_ref[...]
        delta = _split_dot(o_ref[...].astype(F32) * do.astype(F32), ind_ref[...])
        hi = delta.astype(BF16)
        lo = (delta - hi.astype(F32)).astype(BF16)
        dop = jnp.dot(do, pl_ref[...], preferred_element_type=F32)
        dop = dop - jnp.dot(hi, ds_ref[0], preferred_element_type=F32) - jnp.dot(lo, ds_ref[1], preferred_element_type=F32)
        dop_ref[...] = dop.astype(BF16)
        dot = lax.dot_general(pt_ref[...], do, NT, preferred_element_type=F32)
        dot = dot - lax.dot_general(dst_ref[0], hi, NT, preferred_element_type=F32)
        dot = dot - lax.dot_general(dst_ref[1], lo, NT, preferred_element_type=F32)
        dot_ref[...] = dot.astype(BF16)
        qa = qa_ref[...]
        qb = qa.astype(F32)
        qbt = lax.dot_general(eye_ref[...], qa, NT, preferred_element_type=F32)
        for j, part in enumerate(_split3(lse_ref[...])):
            qb = qb - lax.dot_general(part, ls_ref[j], TN, preferred_element_type=F32)
            qbt = qbt - jnp.dot(lst_ref[j], part, preferred_element_type=F32)
        qb_ref[...] = qb.astype(BF16)
        qbt_ref[...] = qbt.astype(BF16)

    pad = pl.BlockSpec((tm, D_PAD), lambda i: (i, 0))
    padt = pl.BlockSpec((D_PAD, tm), lambda i: (0, i))
    return pl.pallas_call(
        body, name="attn_pack_grad", grid=(S // tm,),
        in_specs=[pl.BlockSpec((tm, D_HEADS), lambda i: (i, 0)), pl.BlockSpec((tm, D_HEADS), lambda i: (i, 1)), pad,
                  pl.BlockSpec((STAT_ROWS, tm), lambda i: (0, i)), _full((D_HEADS, LANES)), _full((D_HEADS, D_PAD)),
                  _full((D_PAD, D_HEADS)), _full((D_PAD, D_PAD)), _full((2, LANES, D_PAD)), _full((2, D_PAD, LANES)),
                  _full((3, STAT_ROWS, D_PAD)), _full((3, D_PAD, STAT_ROWS))],
        out_specs=[pad, pad, padt, padt],
        out_shape=[jax.ShapeDtypeStruct((S, D_PAD), BF16)] * 2 + [jax.ShapeDtypeStruct((D_PAD, S), BF16)] * 2,
        compiler_params=_params(("parallel",)),
    )(o, dcat, qa, lse, head_ind, k["place"], k["place_t"], jnp.eye(D_PAD, dtype=BF16), k["d_stat"],
      jnp.swapaxes(k["d_stat"], 1, 2), k["l_stat"], jnp.swapaxes(k["l_stat"], 1, 2))


def _attn_bwd(qb, qbt, ka, va, dop, dopt, k, tq, sums16):
    S = qb.shape[0]
    n = S // tq
    ns = len(sums16)

    def body(q_ref, qt_ref, k_ref, v_ref, do_ref, dot_ref, pt_ref, pick_ref, *rest):
        dq_hbm, dk_ref, dv_ref, dcc_ref = rest[ns:ns + 4]
        dq_s, dk_s, dv_s, s_s, d_s, sem = rest[2 * ns + 4:2 * ns + 10]
        scatter_start, scatter_finish = _scatter_ops(rest[:ns], rest[ns + 4:2 * ns + 4], *rest[2 * ns + 10:])
        g, ki, qi = pl.program_id(0), pl.program_id(1), pl.program_id(2)

        @pl.when((g == 0) & (ki == 0) & (qi == 0))
        def _():
            scatter_start()

        @pl.when((ki == 0) & (qi == 0))
        def _():
            dq_s[...] = jnp.zeros_like(dq_s)

        @pl.when(qi == ki)
        def _():
            dk_s[...] = jnp.zeros_like(dk_s)
            dv_s[...] = jnp.zeros_like(dv_s)

        def step(diagonal):
            chunks = [slice(c * KEY_CHUNK, (c + 1) * KEY_CHUNK) for c in range(tq // KEY_CHUNK)]

            def scores(hh, rows, slot):
                sl = slice(hh * HEAD_PAD, (hh + 1) * HEAD_PAD)
                s_s[slot, rows, :] = lax.dot_general(q_ref[rows, sl], k_ref[:, sl], NT, preferred_element_type=F32)
                d_s[slot, rows, :] = lax.dot_general(do_ref[rows, sl], v_ref[:, sl], NT, preferred_element_type=F32)

            for rows in chunks:
                scores(0, rows, 0)
            for hh in range(GROUP_HEADS):
                sl = slice(hh * HEAD_PAD, (hh + 1) * HEAD_PAD)
                slot = hh % 2
                dv, dk = dv_s[sl, :], dk_s[sl, :]
                for rows in chunks:
                    if hh + 1 < GROUP_HEADS:
                        scores(hh + 1, rows, 1 - slot)
                    p = jnp.exp2(s_s[slot, rows, :])
                    if diagonal:
                        row = rows.start + lax.broadcasted_iota(jnp.int32, (KEY_CHUNK, tq), 0)
                        col = lax.broadcasted_iota(jnp.int32, (KEY_CHUNK, tq), 1)
                        p = jnp.where(row >= col, p, 0.0)
                    ds = (p * d_s[slot, rows, :]).astype(BF16)
                    dv = dv + jnp.dot(dot_ref[sl, rows], p.astype(BF16), preferred_element_type=F32)
                    dk = dk + jnp.dot(qt_ref[sl, rows], ds, preferred_element_type=F32)
                    qrows = pl.ds(pl.multiple_of(qi * tq + rows.start, KEY_CHUNK), KEY_CHUNK)
                    dq_s[qrows, sl] += jnp.dot(ds, k_ref[:, sl], preferred_element_type=F32)
                dv_s[sl, :] = dv
                dk_s[sl, :] = dk

        @pl.when(qi > ki)
        def _():
            step(False)

        @pl.when(qi == ki)
        def _():
            step(True)

        @pl.when(qi == n - 1)
        def _():
            dk = dk_s[...]
            pt = pt_ref[...]
            dk_ref[...] = lax.dot_general((dk * (1.0 / LOG2E)).astype(BF16), pt, TN, preferred_element_type=F32).astype(BF16)
            dv_ref[...] = lax.dot_general(dv_s[...].astype(BF16), pt, TN, preferred_element_type=F32).astype(BF16)
            dcc_ref[...] = sum(lax.dot_general(part, pick_ref[...], TN, preferred_element_type=F32) for part in _split3(dk))

        @pl.when((ki == n - 1) & (qi == n - 1))
        def _():
            cp = pltpu.make_async_copy(dq_s, dq_hbm.at[g], sem)
            cp.start()
            cp.wait()

        @pl.when((g == GROUPS - 1) & (ki == n - 1) & (qi == n - 1))
        def _():
            scatter_finish()

    gw = GROUP_HEADS * HEAD_DIM
    qmax = lambda ki, qi: jnp.maximum(qi, ki)
    qspec = pl.BlockSpec((tq, GROUP_PAD), lambda g, ki, qi: (qmax(ki, qi), g))
    qtspec = pl.BlockSpec((GROUP_PAD, tq), lambda g, ki, qi: (g, qmax(ki, qi)))
    kspec = pl.BlockSpec((tq, GROUP_PAD), lambda g, ki, qi: (ki, g))
    kout = pl.BlockSpec((tq, gw), lambda g, ki, qi: (ki, g))
    out = pl.pallas_call(
        body, name="attn_bwd", grid=(GROUPS, n, n),
        in_specs=[qspec, qtspec, kspec, kspec, qspec, qtspec,
                  _full((GROUP_PAD, gw)), pl.BlockSpec((None, GROUP_PAD, LANES), lambda g, ki, qi: (g, 0, 0))] + [_ANY] * ns,
        out_specs=[_ANY, kout, kout, pl.BlockSpec((None, tq, LANES), lambda g, ki, qi: (g, ki, 0))] + [_ANY] * ns,
        out_shape=[jax.ShapeDtypeStruct((GROUPS, S, GROUP_PAD), F32), jax.ShapeDtypeStruct((S, D_HEADS), BF16),
                   jax.ShapeDtypeStruct((S, D_HEADS), BF16), jax.ShapeDtypeStruct((GROUPS, S, LANES), F32)]
        + _scatter_shapes(sums16),
        scratch_shapes=[pltpu.VMEM((S, GROUP_PAD), F32), pltpu.VMEM((GROUP_PAD, tq), F32), pltpu.VMEM((GROUP_PAD, tq), F32),
                        pltpu.VMEM((2, tq, tq), F32), pltpu.VMEM((2, tq, tq), F32), pltpu.SemaphoreType.DMA] + _scatter_sems(ns),
        compiler_params=_params(("arbitrary", "arbitrary", "arbitrary")),
    )(qb, qbt, ka, va, dop, dopt, k["place_t_group"], k["pick_cols"], *sums16)
    return out[0], out[1], out[2], out[3], out[4:]


def _attn_unpack(dqp, dcc, k, tm):
    S = dqp.shape[1]
    gw = GROUP_HEADS * HEAD_DIM

    def body(dqp_ref, dcc_ref, pt_ref, pick_ref, dq_ref, dc_ref):
        dc = jnp.zeros((tm, LANES), F32)
        for g in range(GROUPS):
            x = dqp_ref[g]
            dq_ref[:, g * gw:(g + 1) * gw] = jnp.dot((x * SCALE).astype(BF16), pt_ref[...], preferred_element_type=F32).astype(BF16)
            dc = dc + _split3_dot(x, pick_ref[g]) - dcc_ref[g]
        dc_ref[...] = dc

    return pl.pallas_call(
        body, name="attn_unpack", grid=(S // tm,),
        in_specs=[pl.BlockSpec((GROUPS, tm, GROUP_PAD), lambda i: (0, i, 0)), pl.BlockSpec((GROUPS, tm, LANES), lambda i: (0, i, 0)),
                  _full((GROUP_PAD, gw)), _full((GROUPS, GROUP_PAD, LANES))],
        out_specs=[pl.BlockSpec((tm, D_HEADS), lambda i: (i, 0)), pl.BlockSpec((tm, LANES), lambda i: (i, 0))],
        out_shape=[jax.ShapeDtypeStruct((S, D_HEADS), BF16), jax.ShapeDtypeStruct((S, LANES), F32)],
        compiler_params=_params(("parallel",)),
    )(dqp, dcc, k["place_t_group"], k["pick_rows"])


def _fox_bwd(dc, f, bias_row, tb):
    S = f.shape[0]
    nb = S // tb

    def body(dc_ref, f_ref, b_ref, df_ref, dbias_ref, carry):
        @pl.when(pl.program_id(0) == 0)
        def _():
            carry[...] = jnp.zeros_like(carry)
            dbias_ref[...] = jnp.zeros_like(dbias_ref)

        r = lax.broadcasted_iota(jnp.int32, (tb, tb), 0)
        s = lax.broadcasted_iota(jnp.int32, (tb, tb), 1)
        tri = (s >= r).astype(F32)
        rc = jnp.dot(tri, dc_ref[...], precision=lax.Precision.HIGHEST, preferred_element_type=F32) + carry[0:1, :]
        carry[...] = jnp.broadcast_to(rc[0:1, :], carry.shape)
        lane = lax.broadcasted_iota(jnp.int32, (tb, LANES), 1)
        df = jnp.where(lane < N_HEADS, rc * jax.nn.sigmoid(-(f_ref[...] + b_ref[...])), 0.0)
        df_ref[...] = df.astype(BF16)
        dbias_ref[...] += jnp.sum(df, axis=0, keepdims=True)

    rev = pl.BlockSpec((tb, LANES), lambda i: (nb - 1 - i, 0))
    return pl.pallas_call(
        body, name="fox_bwd", grid=(nb,),
        in_specs=[rev, rev, _full((1, LANES))],
        out_specs=[rev, _full((1, LANES))],
        out_shape=[jax.ShapeDtypeStruct((S, LANES), BF16), jax.ShapeDtypeStruct((1, LANES), F32)],
        scratch_shapes=[pltpu.VMEM((SUBLANES, LANES), F32)],
        compiler_params=_params(("arbitrary",)),
    )(dc, f, bias_row)


_DZ_WIDTHS = (D_HEADS,) * 5 + (LANES,)


def _in_bwd(pieces, w_in, x, g1, dx1, tm):
    S = x.shape[0]

    def body(*refs):
        p_refs, (w_ref, x_ref, g_ref, dx1_ref, dx_ref, dg_ref) = refs[:6], refs[6:]

        @pl.when(pl.program_id(0) == 0)
        def _():
            dg_ref[...] = jnp.zeros_like(dg_ref)

        dh = jnp.zeros((tm, D_MODEL), F32)
        off = 0
        for p_ref, w in zip(p_refs, _DZ_WIDTHS):
            dh = dh + lax.dot_general(p_ref[...].astype(BF16), w_ref[:, off:off + w], NT, preferred_element_type=F32)
            off += w
        dx, dg = _rms_bwd(dh, x_ref[...], g_ref[...])
        dg_ref[...] += dg
        dx_ref[...] = dx1_ref[...] + dx

    row = lambda w: pl.BlockSpec((tm, w), lambda i: (i, 0))
    return pl.pallas_call(
        body, name="in_bwd", grid=(S // tm,),
        in_specs=[row(w) for w in _DZ_WIDTHS] + [_full((D_MODEL, D_IN_PAD)), row(D_MODEL), _full((1, D_MODEL)), row(D_MODEL)],
        out_specs=[row(D_MODEL), _full((1, D_MODEL))],
        out_shape=[jax.ShapeDtypeStruct((S, D_MODEL), F32), jax.ShapeDtypeStruct((1, D_MODEL), F32)],
        compiler_params=_params(("arbitrary",)),
    )(*pieces, w_in, x, g1, dx1)


def _dw_in(h1, pieces, tk):
    S = h1.shape[0]

    def body(*refs):
        h_ref, p_refs, o_ref = refs[0], refs[1:7], refs[7]

        @pl.when(pl.program_id(0) == 0)
        def _():
            o_ref[...] = jnp.zeros_like(o_ref)

        off = 0
        for p_ref, w in zip(p_refs, _DZ_WIDTHS):
            o_ref[:, off:off + w] += lax.dot_general(h_ref[...], p_ref[...].astype(BF16), TN, preferred_element_type=F32)
            off += w

    row = lambda w: pl.BlockSpec((tk, w), lambda k: (k, 0))
    return pl.pallas_call(
        body, name="dw_in", grid=(S // tk,),
        in_specs=[row(D_MODEL)] + [row(w) for w in _DZ_WIDTHS],
        out_specs=_full((D_MODEL, D_IN_PAD)),
        out_shape=jax.ShapeDtypeStruct((D_MODEL, D_IN_PAD), F32),
        compiler_params=_params(("arbitrary",)),
    )(h1, *pieces)


def _adamw_math(w, g, m, v):
    m = ADAM_B1 * m + (1.0 - ADAM_B1) * g
    v = ADAM_B2 * v + (1.0 - ADAM_B2) * (g * g)
    m_hat = m / (1.0 - ADAM_B1 ** ADAM_STEP)
    v_hat = v / (1.0 - ADAM_B2 ** ADAM_STEP)
    delta = -ADAM_LR * (m_hat / (jnp.sqrt(v_hat) + ADAM_EPS) + ADAM_WD * w)
    return delta, m, v


def _adamw(name, w, g, m, v):
    R, C = w.shape
    tr = _row_tile(R, 256)

    def body(w_ref, g_ref, m_ref, v_ref, d_ref, nm_ref, nv_ref):
        d, nm, nv = _adamw_math(w_ref[...], g_ref[...], m_ref[...], v_ref[...])
        d_ref[...] = d
        nm_ref[...] = nm
        nv_ref[...] = nv

    spec = pl.BlockSpec((tr, C), lambda i: (i, 0))
    return pl.pallas_call(
        body, name=name, grid=(R // tr,), in_specs=[spec] * 4, out_specs=[spec] * 3,
        out_shape=[jax.ShapeDtypeStruct((R, C), F32)] * 3,
        compiler_params=_params(("parallel",)),
    )(w, g, m, v)


def _pair_sum(name, grad, theirs, ids):
    q, half, C = theirs.shape
    tr = _row_tile(half, 256)
    nb = half // tr

    def body(ids_ref, a_ref, b_ref, s_ref, sb_ref):
        s = a_ref[...] + b_ref[...]
        s_ref[...] = s
        sb_ref[...] = s.astype(BF16)

    here = pl.BlockSpec((None, tr, C), lambda j, i, ids: (j, i, 0))
    return pl.pallas_call(
        body, name=name,
        grid_spec=pltpu.PrefetchScalarGridSpec(
            num_scalar_prefetch=1, grid=(q, nb),
            in_specs=[pl.BlockSpec((None, tr, C), lambda j, i, ids: (j, ids[1] * nb + i, 0)), here],
            out_specs=[here, here]),
        out_shape=[jax.ShapeDtypeStruct((q, half, C), F32), jax.ShapeDtypeStruct((q, half, C), BF16)],
        compiler_params=_params(("parallel", "parallel")),
    )(ids, grad, theirs)


def _chip_sum(name, sums32, others, ids):
    _, half, C = sums32.shape
    tr = _row_tile(half, 256)
    nb = half // tr

    def body(ids_ref, a_ref, o_ref, s_ref):
        s = a_ref[...]
        for j in range(3):
            s = s + o_ref[j].astype(F32)
        s_ref[...] = s

    return pl.pallas_call(
        body, name=name,
        grid_spec=pltpu.PrefetchScalarGridSpec(
            num_scalar_prefetch=1, grid=(nb,),
            in_specs=[pl.BlockSpec((None, tr, C), lambda i, ids: (ids[0], i, 0)),
                      pl.BlockSpec((3, tr, C), lambda i, ids: (0, i, 0))],
            out_specs=pl.BlockSpec((tr, C), lambda i, ids: (ids[1] * nb + i, 0))),
        out_shape=jax.ShapeDtypeStruct((2 * half, C), F32),
        compiler_params=_params(("parallel",)),
    )(ids, sums32, others)


def _place():
    return lax.axis_index("x"), lax.axis_index("y"), lax.axis_index("c")


def _other_chips(x, y):
    return [(1 - x, y), (x, 1 - y), (1 - x, 1 - y)]


_ANY = pl.BlockSpec(memory_space=pl.ANY)


def _gather_quarters(shards):
    n = len(shards)

    def body(*refs):
        start, hand_on, finish = _gather_ops(refs[:n], refs[n:2 * n], *refs[2 * n:])
        start()
        hand_on()
        finish()

    return pl.pallas_call(
        body, name="gather_weights",
        in_specs=[_ANY] * n, out_specs=[_ANY] * n,
        out_shape=_gather_shapes(shards), scratch_shapes=_gather_sems(n),
    )(*shards)


def _gather_shapes(shards):
    return [jax.ShapeDtypeStruct((4,) + s.shape, s.dtype) for s in shards]


def _gather_sems(n):
    return [pltpu.SemaphoreType.DMA((n, 3))] * 4 + [pltpu.SemaphoreType.DMA((n,))]


def _gather_ops(ins, outs, send_sems, recv_sems, pass_send_sems, pass_recv_sems, own_sems):
    n = len(ins)
    halved = [r.shape[0] % 32 == 0 for r in ins]

    def part(a, quarter, core):
        if not halved[a]:
            return outs[a].at[quarter]
        half = ins[a].shape[0] // 2
        return outs[a].at[quarter, pl.ds(core * half, half), :]

    def ici(a, j, quarter):
        x, y, c = _place()
        px, py = _other_chips(x, y)[j]
        src = ins[a]
        if halved[a]:
            half = src.shape[0] // 2
            src = src.at[pl.ds(c * half, half), :]
        return pltpu.make_async_remote_copy(src_ref=src, dst_ref=part(a, quarter, c), send_sem=send_sems.at[a, j],
                                            recv_sem=recv_sems.at[a, j], device_id=(px, py, c), device_id_type=MESH)

    def passed(a, j, core):
        x, y, c = _place()
        px, py = _other_chips(x, y)[j]
        half = part(a, 2 * px + py, core)
        return pltpu.make_async_remote_copy(src_ref=half, dst_ref=half, send_sem=pass_send_sems.at[a, j],
                                            recv_sem=pass_recv_sems.at[a, j], device_id=(x, y, 1 - c), device_id_type=MESH)

    def own(a):
        x, y, _ = _place()
        return pltpu.make_async_copy(ins[a], outs[a].at[2 * x + y], own_sems.at[a])

    def start():
        x, y, _ = _place()
        for a in range(n):
            for j in range(3):
                ici(a, j, 2 * x + y).start()
            own(a).start()

    def hand_on():
        x, y, c = _place()
        for a in range(n):
            for j, (px, py) in enumerate(_other_chips(x, y)):
                ici(a, j, 2 * px + py).wait_recv()
                if halved[a]:
                    passed(a, j, c).start()

    def finish():
        x, y, c = _place()
        for a in range(n):
            for j in range(3):
                if halved[a]:
                    passed(a, j, 1 - c).wait_recv()
                    passed(a, j, c).wait_send()
                ici(a, j, 2 * x + y).wait_send()
            own(a).wait()

    return start, hand_on, finish


def _swap_halves(grads, name):
    n = len(grads)

    def body(*refs):
        ins, outs = refs[:n], refs[n:2 * n]
        send_sems, recv_sems = refs[2 * n:]
        x, y, c = _place()
        started = []
        for a in range(n):
            half = ins[a].shape[1] // 2
            cp = pltpu.make_async_remote_copy(src_ref=ins[a].at[:, pl.ds((1 - c) * half, half), :], dst_ref=outs[a],
                                              send_sem=send_sems.at[a], recv_sem=recv_sems.at[a],
                                              device_id=(x, y, 1 - c), device_id_type=MESH)
            cp.start()
            started.append(cp)
        for cp in started:
            cp.wait()

    return pl.pallas_call(
        body, name=name,
        in_specs=[_ANY] * n, out_specs=[_ANY] * n,
        out_shape=[jax.ShapeDtypeStruct((4, g.shape[1] // 2, g.shape[2]), F32) for g in grads],
        scratch_shapes=[pltpu.SemaphoreType.DMA((n,)), pltpu.SemaphoreType.DMA((n,))],
    )(*grads)


def _scatter_quarters(sums16):
    n = len(sums16)

    def body(*refs):
        start, finish = _scatter_ops(refs[:n], refs[n:2 * n], *refs[2 * n:])
        start()
        finish()

    return pl.pallas_call(
        body, name="scatter_quarters",
        in_specs=[_ANY] * n, out_specs=[_ANY] * n,
        out_shape=_scatter_shapes(sums16), scratch_shapes=_scatter_sems(n),
    )(*sums16)


def _scatter_shapes(sums16):
    return [jax.ShapeDtypeStruct((3,) + s.shape[1:], BF16) for s in sums16]


def _scatter_sems(n):
    return [pltpu.SemaphoreType.DMA((n, 3))] * 2


def _scatter_ops(ins, outs, send_sems, recv_sems):
    n = len(ins)

    def copy(a, j):
        x, y, c = _place()
        px, py = _other_chips(x, y)[j]
        return pltpu.make_async_remote_copy(src_ref=ins[a].at[2 * px + py], dst_ref=outs[a].at[j], send_sem=send_sems.at[a, j],
                                            recv_sem=recv_sems.at[a, j], device_id=(px, py, c), device_id_type=MESH)

    def start():
        for a in range(n):
            for j in range(3):
                copy(a, j).start()

    def finish():
        for a in range(n):
            for j in range(3):
                copy(a, j).wait()

    return start, finish


def _join_halves(fulls):
    n = len(fulls)

    def body(*refs):
        ins, outs = refs[:n], refs[n:2 * n]
        send_sems, recv_sems = refs[2 * n:]
        x, y, c = _place()
        started = []
        for a in range(n):
            half = ins[a].shape[0] // 2
            rows = pl.ds(c * half, half)
            cp = pltpu.make_async_remote_copy(src_ref=ins[a].at[rows, :], dst_ref=outs[a].at[rows, :], send_sem=send_sems.at[a],
                                              recv_sem=recv_sems.at[a], device_id=(x, y, 1 - c), device_id_type=MESH)
            cp.start()
            started.append(cp)
        for cp in started:
            cp.wait()

    return pl.pallas_call(
        body, name="join_halves",
        in_specs=[_ANY] * n, out_specs=[_ANY] * n,
        out_shape=[jax.ShapeDtypeStruct(f.shape, F32) for f in fulls],
        input_output_aliases={a: a for a in range(n)},
        scratch_shapes=[pltpu.SemaphoreType.DMA((n,)), pltpu.SemaphoreType.DMA((n,))],
    )(*fulls)


def _small_allreduce_adamw(g, w, m, v):
    R = g.shape[0]

    def body(g_ref, w_ref, m_ref, v_ref, gs_ref, d_ref, nm_ref, nv_ref, all_s, send_sems, recv_sems):
        x, y, c = _place()
        me = 4 * x + 2 * y + c
        all_s[me] = g_ref[...]
        sends = []
        for k in range(1, 8):
            peer = (x ^ (k >> 2), y ^ ((k >> 1) & 1), c ^ (k & 1))
            cp = pltpu.make_async_remote_copy(src_ref=g_ref, dst_ref=all_s.at[me], send_sem=send_sems.at[k - 1],
                                              recv_sem=recv_sems.at[k - 1], device_id=peer, device_id_type=MESH)
            cp.start()
            sends.append(cp)
        for cp in sends:
            cp.wait()
        total = all_s[0]
        for d in range(1, 8):
            total = total + all_s[d]
        gs_ref[...] = total
        delta, nm, nv = _adamw_math(w_ref[...], total, m_ref[...], v_ref[...])
        d_ref[...] = delta
        nm_ref[...] = nm
        nv_ref[...] = nv

    vm = pl.BlockSpec(memory_space=pltpu.VMEM)
    return pl.pallas_call(
        body, name="small_allreduce_adamw",
        in_specs=[vm] * 4, out_specs=[vm] * 4, out_shape=[jax.ShapeDtypeStruct((R, LANES), F32)] * 4,
        scratch_shapes=[pltpu.VMEM((8, R, LANES), F32), pltpu.SemaphoreType.DMA((7,)), pltpu.SemaphoreType.DMA((7,))],
        compiler_params=pltpu.CompilerParams(vmem_limit_bytes=VMEM_LIMIT),
    )(g, w, m, v)


_SMALL = (("norm_mix_g", D_MODEL), ("f_bias", N_HEADS), ("sg_ln_g", D_HEADS), ("sg_w", N_HEADS * SG_BLOCK * SG_BLOCK),
          ("sg_b", N_HEADS * SG_BLOCK), ("norm_ffn_g", D_MODEL), ("w_conv", 3 * 2 * D_FF), ("b_conv", 2 * D_FF),
          ("norm_final_g", D_MODEL))


def _pack_small(parts):
    rows = []
    for name, size in _SMALL:
        flat = parts[name].reshape(-1).astype(F32)
        pad = (-size) % (SUBLANES * LANES)
        rows.append(jnp.pad(flat, (0, pad)).reshape(-1, LANES))
    return jnp.concatenate(rows, axis=0)


def _unpack_small(packed, shapes):
    out, r = {}, 0
    for name, size in _SMALL:
        nrows = (size + SUBLANES * LANES - 1) // (SUBLANES * LANES) * SUBLANES
        out[name] = packed[r:r + nrows].reshape(-1)[:size].reshape(shapes[name])
        r += nrows
    return out


def _local_step(x, target, g1, w_in, f_bias, sg_ln_g, sg_w, sg_b, g2, b_conv, g3, late_shards, ids):
    S = x.shape[0]
    tm = _row_tile(S, 512)
    tms = _row_tile(S, 256)
    tq = _row_tile(S, 512)

    lane = jnp.arange(D_HEADS)
    seg_avg = jnp.where(lane[:, None] // HEAD_DIM == lane[None, :] // HEAD_DIM, 1.0 / HEAD_DIM, 0.0).astype(BF16)
    head_ind = (lane[:, None] // HEAD_DIM == jnp.arange(LANES)[None, :]).astype(BF16)
    pos_chunk = jnp.arange(SG_BLOCK) // CHUNK
    w_mask32 = jnp.where(pos_chunk[:, None] >= pos_chunk[None, :], sg_w, 0.0)
    w_mask = w_mask32.astype(BF16)
    w_mask_t = jnp.swapaxes(w_mask32, 1, 2).astype(BF16)
    ln_row = sg_ln_g.reshape(1, D_HEADS)
    b_full = jnp.repeat(sg_b.T, HEAD_DIM, axis=1)
    bias_row = jnp.pad(f_bias.reshape(1, N_HEADS), ((0, 0), (0, LANES - N_HEADS)))
    b_conv_row = b_conv.reshape(1, 2 * D_FF)

    z, f, h1 = _in_proj(x, g1, w_in, tm)
    c = _fox_prep(f, bias_row, _row_tile(S, 256))
    consts = _attn_consts()
    qa, ka, va, vat = _attn_pack(z, c, consts, tm)
    out_b, lse, gathered = _attn_fwd(qa, ka, vat, consts["place_t"], tq, late_shards)
    g_out, w_up_q, g_down, g_conv = gathered
    w_out = g_out.reshape(D_MODEL, D_MODEL)
    w_down = g_down.reshape(D_FF, D_MODEL)
    w_conv = jnp.concatenate([g_conv[q] for q in range(4)], axis=1)
    out_a = _gate_fwd(z, w_mask, ln_row, b_full, seg_avg, tm)
    x1, h2 = _mix_out(x, out_a, out_b, w_out, g2, tm)
    a = _up_proj(h2, w_up_q, tm)
    dx2, sq_err, dg3 = _ffn_fwd_loss(a, w_conv, b_conv_row, w_down, x1, g3, target, tms)

    dconv, y, dw_conv8, db_conv = _ffn_bwd_gate(dx2, a, w_conv, b_conv_row, w_down, tms)
    dact = _conv_bwd(dconv, w_conv, tm, 2 * D_FF // 4)
    dw_down = _matmul_tn(y, dx2, "dw_down", D_FF // 2, D_MODEL, tm, quarters=(2, 1))
    dx1, dg2 = _up_bwd(dact, w_up_q, x1, g2, dx2, tms)
    dw_up_q = _matmul_tn(h2, dact, "dw_up", D_MODEL, 2 * D_FF // 4, tm, quarters=(1, 4))
    dcat = _out_bwd(dx1, w_out, tm)
    dw_out_a = _matmul_tn(out_a, dx1, "dw_out_a", D_HEADS, D_MODEL, tm)
    dw_out_b = _matmul_tn(out_b, dx1, "dw_out_b", D_HEADS, D_MODEL, tm)
    early = {"w_down": dw_down.reshape(4, D_FF // 4, D_MODEL), "w_up": dw_up_q,
             "w_out": jnp.concatenate([dw_out_a, dw_out_b], axis=0).reshape(4, D_MODEL // 4, D_MODEL)}
    early_sums = _chip_sums(early, ids, "early")
    dzu, dzv, dsg_w, dsg_b_t, dln = _gate_bwd(z, dcat, w_mask, w_mask_t, ln_row, b_full, seg_avg, head_ind, tm)
    dop, qb, dopt, qbt = _attn_pack_grad(out_b, dcat, qa, lse, head_ind, consts, tm)
    dqp, dk, dv, dcc, landed = _attn_bwd(qb, qbt, ka, va, dop, dopt, consts, tq, [s16 for _, s16 in early_sums.values()])
    early_parts = {k: (s32, got) for (k, (s32, _)), got in zip(early_sums.items(), landed)}
    dq, dc = _attn_unpack(dqp, dcc, consts, tm)
    df, dbias = _fox_bwd(dc, f, bias_row, _row_tile(S, 256))
    pieces = (dzu, dzv, dq, dk, dv, df)
    dx, dg1 = _in_bwd(pieces, w_in, x, g1, dx1, tms)
    dw_in = _dw_in(h1, pieces, tm)

    grads = {
        "norm_mix_g": dg1, "f_bias": dbias[:, :N_HEADS], "sg_ln_g": dln, "sg_w": dsg_w, "sg_b": dsg_b_t[:, :N_HEADS].T,
        "norm_ffn_g": dg2, "w_conv": dw_conv8[:3], "b_conv": db_conv, "norm_final_g": dg3,
        "w_in": dw_in,
    }
    return sq_err, dx, grads, early_parts


def _chip_sums(grads_q, ids, tag):
    names = list(grads_q)
    theirs = _swap_halves([grads_q[k] for k in names], "swap_halves_" + tag)
    return {k: _pair_sum("pair_sum_" + k, grads_q[k], t, ids) for k, t in zip(names, theirs)}


def _finish_reduction(parts, ids):
    names = list(parts)
    fulls = [_chip_sum("chip_sum_" + k, s32, got, ids) for k, (s32, got) in parts.items()]
    return dict(zip(names, _join_halves(fulls)))


def kernel(x, norm_mix_g, w_in, f_bias, sg_ln_g, sg_w, sg_b, w_out, norm_ffn_g, w_up, w_conv, b_conv, w_down, norm_final_g, loss_target, m_norm_mix_g, m_w_in, m_f_bias, m_sg_ln_g, m_sg_w, m_sg_b, m_w_out, m_norm_ffn_g, m_w_up, m_w_conv, m_b_conv, m_w_down, m_norm_final_g, v_norm_mix_g, v_w_in, v_f_bias, v_sg_ln_g, v_sg_w, v_sg_b, v_w_out, v_norm_ffn_g, v_w_up, v_w_conv, v_b_conv, v_w_down, v_norm_final_g):
    args = dict(locals())
    quarter = 2 * lax.axis_index("x") + lax.axis_index("y")
    ids = jnp.stack([quarter, lax.axis_index("c")]).astype(jnp.int32)
    wq_conv = w_conv.shape[-1]

    g_in = _gather_quarters([w_in[0].astype(BF16)])[0]
    w_in_full = jnp.pad(jnp.concatenate([g_in[q] for q in range(4)], axis=1), ((0, 0), (0, D_IN_PAD - D_IN)))
    late_shards = [w_out[0].astype(BF16), w_up[0].astype(BF16), w_down[0].astype(BF16), w_conv[0]]

    sq_err, dx, grads, early_parts = _local_step(
        x[0], loss_target[0], norm_mix_g, w_in_full, f_bias[0], sg_ln_g[0], sg_w[0], sg_b[0], norm_ffn_g, b_conv[0],
        norm_final_g.reshape(1, D_MODEL), late_shards, ids)
    loss = lax.psum(0.5 * jnp.sum(sq_err) / D_MODEL, ("x", "y", "c"))

    dw_in = grads["w_in"][:, :D_IN].reshape(D_MODEL, 4, D_IN // 4).transpose(1, 0, 2)
    late_sums = _chip_sums({"w_in": dw_in}, ids, "late")
    landed = _scatter_quarters([s16 for _, s16 in late_sums.values()])
    late_parts = {k: (s32, got) for (k, (s32, _)), got in zip(late_sums.items(), landed)}
    big = _finish_reduction({**early_parts, **late_parts}, ids)

    out = {"loss": loss, "grad_x": dx[None]}
    for k in ("w_in", "w_out", "w_up", "w_down"):
        g = big[k]
        d, nm, nv = _adamw("adamw_" + k, args[k][0], g, args["m_" + k][0], args["v_" + k][0])
        out["grad_" + k], out["delta_" + k], out["new_m_" + k], out["new_v_" + k] = g[None], d[None], nm[None], nv[None]

    def padded_conv(t):
        return lax.dynamic_update_slice(jnp.zeros((3, 4 * wq_conv), F32), t[0], (0, quarter * wq_conv))

    small_names = [n for n, _ in _SMALL]
    shapes = {n: (3, 4 * wq_conv) if n == "w_conv" else args[n].shape for n in small_names}
    pack = lambda prefix: _pack_small({n: padded_conv(args[prefix + n]) if n == "w_conv" else args[prefix + n] for n in small_names})
    packed = _small_allreduce_adamw(_pack_small({n: grads[n] for n in small_names}), pack(""), pack("m_"), pack("v_"))
    for prefix, arr in zip(("grad_", "delta_", "new_m_", "new_v_"), packed):
        for n, t in _unpack_small(arr, shapes).items():
            if n == "w_conv":
                t = lax.dynamic_slice(t, (0, quarter * wq_conv), (3, wq_conv))[None]
            out[prefix + n] = t

    weights = ["norm_mix_g", "w_in", "f_bias", "sg_ln_g", "sg_w", "sg_b", "w_out", "norm_ffn_g", "w_up", "w_conv", "b_conv",
               "w_down", "norm_final_g"]
    return (out["loss"], out["grad_x"], *[out[p + n] for p in ("grad_", "delta_", "new_m_", "new_v_") for n in weights])
```

```python
import functools
import math

import jax
import jax.numpy as jnp
from jax import lax
from jax.experimental import pallas as pl
from jax.experimental.pallas import tpu as pltpu

F32 = jnp.float32
BF16 = jnp.bfloat16
MESH = pl.DeviceIdType.MESH

D_MODEL = 1024
N_HEADS = 8
HEAD_DIM = 64
D_HEADS = N_HEADS * HEAD_DIM
SG_BLOCK = 128
CHUNK = 64
D_FF = 2816
D_IN = 2 * D_HEADS + 3 * D_HEADS + N_HEADS
LANES = 128
SUBLANES = 8
D_IN_PAD = 5 * D_HEADS + LANES
EPS = 1e-6
SCALE = HEAD_DIM ** -0.5
NEG = -1e30
LOG2E = 1.4426950408889634
HEAD_PAD = LANES
D_PAD = N_HEADS * HEAD_PAD
Q_STAT = HEAD_DIM
K_STAT = HEAD_DIM + 3
L_STAT = HEAD_DIM + 6
GROUPS = 2
GROUP_HEADS = N_HEADS // GROUPS
GROUP_PAD = GROUP_HEADS * HEAD_PAD
KEY_CHUNK = 256
STAT_ROWS = 16
FF_CHUNK = 256

ADAM_LR = 0.001
ADAM_B1 = 0.9
ADAM_B2 = 0.999
ADAM_EPS = 1e-08
ADAM_WD = 0.01
ADAM_STEP = 10

VMEM_LIMIT = 56 * 1024 * 1024

NT = (((1,), (1,)), ((), ()))
TN = (((0,), (0,)), ((), ()))


def _params(sem):
    return pltpu.CompilerParams(dimension_semantics=sem, vmem_limit_bytes=VMEM_LIMIT)


def _full(shape):
    nd = len(shape)
    return pl.BlockSpec(shape, lambda *_: (0,) * nd)


def _row_tile(rows, target):
    best = None
    for t in range(SUBLANES, min(rows, target) + 1, SUBLANES):
        if rows % t == 0:
            best = t
    assert best is not None, rows
    return best


def _sigmoid(x):
    return 0.5 * jnp.tanh(0.5 * x) + 0.5


def _gelu(z):
    return 0.5 * z * (1.0 + lax.erf(z * (2.0 ** -0.5)))


def _gelu_grad(z):
    cdf = 0.5 * (1.0 + lax.erf(z * (2.0 ** -0.5)))
    pdf = jnp.exp(-0.5 * z * z) * (1.0 / math.sqrt(2.0 * math.pi))
    return cdf + z * pdf


def _split_dot(x, m):
    hi = x.astype(BF16)
    lo = (x - hi.astype(F32)).astype(BF16)
    return jnp.dot(hi, m, preferred_element_type=F32) + jnp.dot(lo, m, preferred_element_type=F32)


def _head_mask(h, rows):
    lane = lax.broadcasted_iota(jnp.int32, (rows, D_HEADS), 1)
    return (lane >= h * HEAD_DIM) & (lane < (h + 1) * HEAD_DIM)


def _rms_bwd(dh, x, g):
    r = lax.rsqrt(jnp.mean(x * x, axis=-1, keepdims=True) + EPS)
    xhat = x * r
    dg = jnp.sum(dh * xhat, axis=0, keepdims=True)
    dxhat = dh * g
    dx = r * (dxhat - xhat * jnp.mean(dxhat * xhat, axis=-1, keepdims=True))
    return dx, dg


def _in_proj(x, g1, w_in, tm):
    S = x.shape[0]
    nz = D_IN_PAD - LANES

    def body(x_ref, g_ref, w_ref, z_ref, f_ref, h_ref):
        xf = x_ref[...]
        r = lax.rsqrt(jnp.mean(xf * xf, axis=-1, keepdims=True) + EPS)
        h = (xf * r * g_ref[...]).astype(BF16)
        h_ref[...] = h
        zz = jnp.dot(h, w_ref[...], preferred_element_type=F32)
        z_ref[...] = zz[:, :nz].astype(BF16)
        f_ref[...] = zz[:, nz:]

    return pl.pallas_call(
        body, name="in_proj", grid=(S // tm,),
        in_specs=[pl.BlockSpec((tm, D_MODEL), lambda i: (i, 0)), _full((1, D_MODEL)), _full((D_MODEL, D_IN_PAD))],
        out_specs=[pl.BlockSpec((tm, nz), lambda i: (i, 0)), pl.BlockSpec((tm, LANES), lambda i: (i, 0)),
                   pl.BlockSpec((tm, D_MODEL), lambda i: (i, 0))],
        out_shape=[jax.ShapeDtypeStruct((S, nz), BF16), jax.ShapeDtypeStruct((S, LANES), F32),
                   jax.ShapeDtypeStruct((S, D_MODEL), BF16)],
        compiler_params=_params(("parallel",)),
    )(x, g1, w_in)


def _fox_prep(f, bias_row, tb):
    S = f.shape[0]

    def body(f_ref, b_ref, c_ref, carry):
        @pl.when(pl.program_id(0) == 0)
        def _():
            carry[...] = jnp.zeros_like(carry)

        xv = f_ref[...] + b_ref[...]
        lf = jnp.minimum(xv, 0.0) - jnp.log(1.0 + jnp.exp(-jnp.abs(xv)))
        r = lax.broadcasted_iota(jnp.int32, (tb, tb), 0)
        s = lax.broadcasted_iota(jnp.int32, (tb, tb), 1)
        tri = (r >= s).astype(F32)
        cs = jnp.dot(tri, lf, precision=lax.Precision.HIGHEST, preferred_element_type=F32) + carry[0:1, :]
        c_ref[...] = cs
        carry[...] = jnp.broadcast_to(cs[tb - 1:tb, :], carry.shape)

    return pl.pallas_call(
        body, name="fox_prep", grid=(S // tb,),
        in_specs=[pl.BlockSpec((tb, LANES), lambda i: (i, 0)), _full((1, LANES))],
        out_specs=pl.BlockSpec((tb, LANES), lambda i: (i, 0)),
        out_shape=jax.ShapeDtypeStruct((S, LANES), F32),
        scratch_shapes=[pltpu.VMEM((SUBLANES, LANES), F32)],
        compiler_params=_params(("arbitrary",)),
    )(f, bias_row)


def _attn_consts():
    col = jnp.arange(D_PAD)
    row = jnp.arange(D_HEADS)
    head = jnp.arange(LANES)
    place = (row[:, None] // HEAD_DIM == col[None, :] // HEAD_PAD) & (row[:, None] % HEAD_DIM == col[None, :] % HEAD_PAD)

    def stat(offset):
        return ((head[:, None] < N_HEADS) & (col[None, :] == head[:, None] * HEAD_PAD + offset)).astype(BF16)

    def ones(offsets):
        return sum((col % HEAD_PAD == o) for o in offsets).astype(F32).reshape(1, D_PAD)

    def pick(offset):
        gcol = jnp.arange(GROUP_PAD)
        return jnp.stack([((gcol[:, None] % HEAD_PAD == offset) & (head[None, :] == g * GROUP_HEADS + gcol[:, None] // HEAD_PAD))
                          for g in range(GROUPS)]).astype(BF16)

    place = place.astype(BF16)
    return {
        "place": place, "place_t": place.T, "place_t_group": place.T[:GROUP_PAD, :GROUP_HEADS * HEAD_DIM],
        "q_stat": jnp.stack([stat(Q_STAT + j) for j in range(3)]), "k_stat": jnp.stack([stat(K_STAT + j) for j in range(3)]),
        "d_stat": jnp.stack([stat(Q_STAT + j) for j in range(2)]), "l_stat": jnp.stack([stat(L_STAT + j)[:STAT_ROWS] for j in range(3)]),
        "q_ones": ones(range(K_STAT, K_STAT + 3)), "k_ones": ones(list(range(Q_STAT, Q_STAT + 3)) + list(range(L_STAT, L_STAT + 3))),
        "v_ones": ones(range(Q_STAT, Q_STAT + 2)),
        "pick_rows": pick(Q_STAT), "pick_cols": pick(K_STAT),
    }


def _split3(x):
    hi = x.astype(BF16)
    r = x - hi.astype(F32)
    mid = r.astype(BF16)
    return hi, mid, (r - mid.astype(F32)).astype(BF16)


def _split3_dot(x, m):
    return sum(jnp.dot(part, m, preferred_element_type=F32) for part in _split3(x))


def _attn_pack(z, c, k, tm):
    S = z.shape[0]

    def body(q_ref, k_ref, v_ref, c_ref, pl_ref, pt_ref, qs_ref, ks_ref, qo_ref, ko_ref, vo_ref, voc_ref,
             qa_ref, ka_ref, va_ref, vt_ref):
        place = pl_ref[...]
        q = (q_ref[...].astype(F32) * (SCALE * LOG2E)).astype(BF16)
        qa = jnp.dot(q, place, preferred_element_type=F32) + qo_ref[...]
        ka = jnp.dot(k_ref[...], place, preferred_element_type=F32) + ko_ref[...]
        for j, part in enumerate(_split3(c_ref[...] * LOG2E)):
            qa = qa + jnp.dot(part, qs_ref[j], preferred_element_type=F32)
            ka = ka - jnp.dot(part, ks_ref[j], preferred_element_type=F32)
        qa_ref[...] = qa.astype(BF16)
        ka_ref[...] = ka.astype(BF16)
        v = v_ref[...]
        va_ref[...] = (jnp.dot(v, place, preferred_element_type=F32) + vo_ref[...]).astype(BF16)
        vt_ref[...] = (lax.dot_general(pt_ref[...], v, NT, preferred_element_type=F32) + voc_ref[...]).astype(BF16)

    blk = lambda col: pl.BlockSpec((tm, D_HEADS), lambda i: (i, col))
    out = pl.BlockSpec((tm, D_PAD), lambda i: (i, 0))
    pad = jax.ShapeDtypeStruct((S, D_PAD), BF16)
    return pl.pallas_call(
        body, name="attn_pack", grid=(S // tm,),
        in_specs=[blk(2), blk(3), blk(4), pl.BlockSpec((tm, LANES), lambda i: (i, 0)), _full((D_HEADS, D_PAD)), _full((D_PAD, D_HEADS)),
                  _full((3, LANES, D_PAD)), _full((3, LANES, D_PAD)), _full((1, D_PAD)), _full((1, D_PAD)), _full((1, D_PAD)),
                  _full((D_PAD, 1))],
        out_specs=[out, out, out, pl.BlockSpec((D_PAD, tm), lambda i: (0, i))],
        out_shape=[pad, pad, pad, jax.ShapeDtypeStruct((D_PAD, S), BF16)],
        compiler_params=_params(("parallel",)),
    )(z, z, z, c, k["place"], k["place_t"], k["q_stat"], k["k_stat"], k["q_ones"], k["k_ones"], k["v_ones"], k["v_ones"].T)


def _attn_fwd(qa, ka, vat, place_t, tq, shards):
    S = qa.shape[0]
    n = S // tq
    ns = len(shards)
    hand_on_at = (2 * n) // 3

    def body(q_ref, k_ref, vt_ref, pt_ref, *rest):
        o_ref, lse_ref = rest[ns:ns + 2]
        m_s, acc_s, ot_s, s_s = rest[2 * ns + 2:2 * ns + 6]
        start, hand_on, finish = _gather_ops(rest[:ns], rest[ns + 2:2 * ns + 2], *rest[2 * ns + 6:])
        qi, ki = pl.program_id(0), pl.program_id(1)

        @pl.when((qi == 0) & (ki == 0))
        def _():
            start()

        @pl.when((qi == hand_on_at) & (ki == 0))
        def _():
            hand_on()

        @pl.when(ki == 0)
        def _():
            m_s[...] = jnp.full_like(m_s, NEG)
            acc_s[...] = jnp.zeros_like(acc_s)

        def step(diagonal):
            chunks = [slice(c * KEY_CHUNK, (c + 1) * KEY_CHUNK) for c in range(tq // KEY_CHUNK)]

            def scores(h, rows, slot):
                sl = slice(h * HEAD_PAD, (h + 1) * HEAD_PAD)
                st = lax.dot_general(k_ref[rows, sl], q_ref[:, sl], NT, preferred_element_type=F32)
                if diagonal:
                    key = rows.start + lax.broadcasted_iota(jnp.int32, (KEY_CHUNK, tq), 0)
                    query = lax.broadcasted_iota(jnp.int32, (KEY_CHUNK, tq), 1)
                    st = jnp.where(query >= key, st, NEG)
                s_s[slot, rows, :] = st
                return jnp.max(st, axis=0, keepdims=True)

            m_cur = functools.reduce(jnp.maximum, [scores(0, rows, 0) for rows in chunks])
            for h in range(N_HEADS):
                sl = slice(h * HEAD_PAD, (h + 1) * HEAD_PAD)
                slot = h % 2
                m_prev = m_s[h][0:1, :]
                m_new = jnp.maximum(m_prev, m_cur)
                acc = jnp.exp2(m_prev - m_new) * acc_s[h]
                m_next = []
                for rows in chunks:
                    if h + 1 < N_HEADS:
                        m_next.append(scores(h + 1, rows, 1 - slot))
                    pt = jnp.exp2(s_s[slot, rows, :] - m_new).astype(BF16)
                    acc = acc + jnp.dot(vt_ref[sl, rows], pt, preferred_element_type=F32)
                acc_s[h] = acc
                m_s[h] = jnp.broadcast_to(m_new, (SUBLANES, tq))
                if m_next:
                    m_cur = functools.reduce(jnp.maximum, m_next)

        @pl.when(ki < qi)
        def _():
            step(False)

        @pl.when(ki == qi)
        def _():
            step(True)
            lse_ref[...] = jnp.zeros_like(lse_ref)
            for h in range(N_HEADS):
                acc = acc_s[h]
                denom = acc[Q_STAT:Q_STAT + 1, :]
                ot_s[h * HEAD_PAD:(h + 1) * HEAD_PAD, :] = (acc / denom).astype(BF16)
                lse_ref[h:h + 1, :] = m_s[h][0:1, :] + jnp.log(denom) * LOG2E
            o_ref[...] = lax.dot_general(ot_s[...], pt_ref[...], TN, preferred_element_type=F32).astype(BF16)

        @pl.when((qi == n - 1) & (ki == n - 1))
        def _():
            finish()

    kmin = lambda qi, ki: jnp.minimum(ki, qi)
    out = pl.pallas_call(
        body, name="attn_fwd", grid=(n, n),
        in_specs=[pl.BlockSpec((tq, D_PAD), lambda qi, ki: (qi, 0)), pl.BlockSpec((tq, D_PAD), lambda qi, ki: (kmin(qi, ki), 0)),
                  pl.BlockSpec((D_PAD, tq), lambda qi, ki: (0, kmin(qi, ki))), _full((D_PAD, D_HEADS))] + [_ANY] * ns,
        out_specs=[pl.BlockSpec((tq, D_HEADS), lambda qi, ki: (qi, 0)), pl.BlockSpec((STAT_ROWS, tq), lambda qi, ki: (0, qi))]
        + [_ANY] * ns,
        out_shape=[jax.ShapeDtypeStruct((S, D_HEADS), BF16), jax.ShapeDtypeStruct((STAT_ROWS, S), F32)] + _gather_shapes(shards),
        scratch_shapes=[pltpu.VMEM((N_HEADS, SUBLANES, tq), F32), pltpu.VMEM((N_HEADS, HEAD_PAD, tq), F32),
                        pltpu.VMEM((D_PAD, tq), BF16), pltpu.VMEM((2, tq, tq), F32)] + _gather_sems(ns),
        compiler_params=_params(("arbitrary", "arbitrary")),
    )(qa, ka, vat, place_t, *shards)
    return out[0], out[1], out[2:]


def _layer_norm_heads(v, seg_avg):
    mu = _split_dot(v, seg_avg)
    d = v - mu
    var = _split_dot(d * d, seg_avg)
    rstd = lax.rsqrt(var + EPS)
    return d * rstd, rstd


def _gate_mix(vn_blk, w_ref, bias):
    acc = bias
    for h in range(N_HEADS):
        vh = jnp.where(_head_mask(h, SG_BLOCK), vn_blk, 0.0).astype(BF16)
        acc = acc + jnp.dot(w_ref[h], vh, preferred_element_type=F32)
    return acc


def _gate_fwd(z, w_mask, ln_row, b_full, seg_avg, tm):
    S = z.shape[0]

    def body(zu_ref, zv_ref, w_ref, ln_ref, b_ref, avg_ref, o_ref):
        u = _gelu(zu_ref[...].astype(F32))
        v = _gelu(zv_ref[...].astype(F32))
        vhat, _ = _layer_norm_heads(v, avg_ref[...])
        vn = vhat * ln_ref[...]
        for b in range(tm // SG_BLOCK):
            rows = slice(b * SG_BLOCK, (b + 1) * SG_BLOCK)
            mixed = _gate_mix(vn[rows], w_ref, b_ref[...])
            o_ref[rows, :] = (u[rows] * mixed).astype(BF16)

    return pl.pallas_call(
        body, name="gate_fwd", grid=(S // tm,),
        in_specs=[pl.BlockSpec((tm, D_HEADS), lambda i: (i, 0)), pl.BlockSpec((tm, D_HEADS), lambda i: (i, 1)),
                  _full((N_HEADS, SG_BLOCK, SG_BLOCK)), _full((1, D_HEADS)), _full((SG_BLOCK, D_HEADS)),
                  _full((D_HEADS, D_HEADS))],
        out_specs=pl.BlockSpec((tm, D_HEADS), lambda i: (i, 0)),
        out_shape=jax.ShapeDtypeStruct((S, D_HEADS), BF16),
        compiler_params=_params(("parallel",)),
    )(z, z, w_mask, ln_row, b_full, seg_avg)


def _mix_out(x, out_a, out_b, w_out, g2, tm):
    S = x.shape[0]

    def body(x_ref, a_ref, b_ref, w_ref, g_ref, x1_ref, h_ref):
        y = jnp.dot(a_ref[...], w_ref[:D_HEADS, :], preferred_element_type=F32)
        y = y + jnp.dot(b_ref[...], w_ref[D_HEADS:, :], preferred_element_type=F32)
        x1 = x_ref[...] + y
        x1_ref[...] = x1
        r = lax.rsqrt(jnp.mean(x1 * x1, axis=-1, keepdims=True) + EPS)
        h_ref[...] = (x1 * r * g_ref[...]).astype(BF16)

    row = lambda w: pl.BlockSpec((tm, w), lambda i: (i, 0))
    return pl.pallas_call(
        body, name="mix_out", grid=(S // tm,),
        in_specs=[row(D_MODEL), row(D_HEADS), row(D_HEADS), _full((D_MODEL, D_MODEL)), _full((1, D_MODEL))],
        out_specs=[row(D_MODEL), row(D_MODEL)],
        out_shape=[jax.ShapeDtypeStruct((S, D_MODEL), F32), jax.ShapeDtypeStruct((S, D_MODEL), BF16)],
        compiler_params=_params(("parallel",)),
    )(x, out_a, out_b, w_out, g2)


def _up_proj(h2, w_up_q, tm):
    S = h2.shape[0]
    nq, _, wq = w_up_q.shape

    def body(h_ref, w_ref, a_ref):
        a_ref[...] = jnp.dot(h_ref[...], w_ref[...], preferred_element_type=F32).astype(BF16)

    return pl.pallas_call(
        body, name="up_proj", grid=(nq, S // tm),
        in_specs=[pl.BlockSpec((tm, D_MODEL), lambda j, i: (i, 0)), pl.BlockSpec((None, D_MODEL, wq), lambda j, i: (j, 0, 0))],
        out_specs=pl.BlockSpec((tm, wq), lambda j, i: (i, j)),
        out_shape=jax.ShapeDtypeStruct((S, nq * wq), BF16),
        compiler_params=_params(("parallel", "parallel")),
    )(h2, w_up_q)


def _shift_down(a, halo, k):
    tm = a.shape[0]
    ra = pltpu.roll(a, k, 0)
    rh = pltpu.roll(halo, k, 0)
    row = lax.broadcasted_iota(jnp.int32, halo.shape, 0)
    top = jnp.where(row < k, rh, ra[0:SUBLANES])
    return jnp.concatenate([top, ra[SUBLANES:tm]], axis=0)


def _shift_up(a, halo, k):
    tm = a.shape[0]
    ra = pltpu.roll(a, tm - k, 0)
    rh = pltpu.roll(halo, SUBLANES - k, 0)
    row = lax.broadcasted_iota(jnp.int32, halo.shape, 0)
    bottom = jnp.where(row >= SUBLANES - k, rh, ra[tm - SUBLANES:tm])
    return jnp.concatenate([ra[0:tm - SUBLANES], bottom], axis=0)


def _shift_matrices(tm):
    row = lax.broadcasted_iota(jnp.int32, (tm, tm), 0)
    col = lax.broadcasted_iota(jnp.int32, (tm, tm), 1)
    return [(row == col + k).astype(BF16) for k in (1, 2)]


def _conv_taps(a, halo, first, shifts):
    tm = a.shape[0]
    halo = halo.astype(F32) * jnp.where(first, 0.0, 1.0)
    if shifts is None:
        a = a.astype(F32)
        return a, _shift_down(a, halo, 1), _shift_down(a, halo, 2)
    row8 = lax.broadcasted_iota(jnp.int32, halo.shape, 0)
    taps = [a.astype(F32)]
    for k, shift in zip((1, 2), shifts):
        down = jnp.dot(shift, a, preferred_element_type=F32)
        top = down[0:SUBLANES] + jnp.where(row8 < k, pltpu.roll(halo, k, 0), 0.0)
        taps.append(jnp.concatenate([top, down[SUBLANES:tm]], axis=0))
    return taps


def _conv_gate_val(refs, shifts, cols, first):
    ag_ref, av_ref, hg_ref, hv_ref, wg_ref, wv_ref, bg_ref, bv_ref = refs
    g0, g1, g2 = _conv_taps(ag_ref[:, cols], hg_ref[:, cols], first, shifts)
    gate = wg_ref[2:3, cols] * g0 + wg_ref[1:2, cols] * g1 + wg_ref[0:1, cols] * g2 + bg_ref[:, cols]
    v0, v1, v2 = _conv_taps(av_ref[:, cols], hv_ref[:, cols], first, shifts)
    val = wv_ref[2:3, cols] * v0 + wv_ref[1:2, cols] * v1 + wv_ref[0:1, cols] * v2 + bv_ref[:, cols]
    return gate, val, (g2, g1, g0), (v2, v1, v0)


_FF_CHUNKS = [slice(j * FF_CHUNK, (j + 1) * FF_CHUNK) for j in range(D_FF // FF_CHUNK)]


def _conv_specs(tm):
    step = tm // SUBLANES
    prev = lambda i: jnp.maximum(i * step - 1, 0)
    return [pl.BlockSpec((tm, D_FF), lambda i: (i, 0)), pl.BlockSpec((tm, D_FF), lambda i: (i, 1)),
            pl.BlockSpec((SUBLANES, D_FF), lambda i: (prev(i), 0)), pl.BlockSpec((SUBLANES, D_FF), lambda i: (prev(i), 1))]


def _ffn_fwd_loss(a, w_conv, b_conv, w_down, x1, g3, target, tm):
    S = x1.shape[0]

    def body(ag_ref, av_ref, hg_ref, hv_ref, wg_ref, wv_ref, bg_ref, bv_ref, wd_ref, x1_ref, g_ref, t_ref,
             dx2_ref, loss_ref, dg_ref):
        i = pl.program_id(0)

        @pl.when(i == 0)
        def _():
            loss_ref[...] = jnp.zeros_like(loss_ref)
            dg_ref[...] = jnp.zeros_like(dg_ref)

        x2 = x1_ref[...]
        for cols in _FF_CHUNKS:
            gate, val, _, _ = _conv_gate_val((ag_ref, av_ref, hg_ref, hv_ref, wg_ref, wv_ref, bg_ref, bv_ref), None, cols, i == 0)
            y = (gate * _sigmoid(gate) * val).astype(BF16)
            x2 = x2 + jnp.dot(y, wd_ref[cols, :], preferred_element_type=F32)
        r = lax.rsqrt(jnp.mean(x2 * x2, axis=-1, keepdims=True) + EPS)
        xhat = x2 * r
        gg = g_ref[...]
        err = xhat * gg - t_ref[...]
        loss_ref[...] += jnp.sum(err * err, axis=0, keepdims=True)
        dy = err * (1.0 / D_MODEL)
        dg_ref[...] += jnp.sum(dy * xhat, axis=0, keepdims=True)
        dxhat = dy * gg
        dx2_ref[...] = r * (dxhat - xhat * jnp.mean(dxhat * xhat, axis=-1, keepdims=True))

    row = lambda w: pl.BlockSpec((tm, w), lambda i: (i, 0))
    half = lambda r: [pl.BlockSpec((r, D_FF), lambda i: (0, 0)), pl.BlockSpec((r, D_FF), lambda i: (0, 1))]
    return pl.pallas_call(
        body, name="ffn_fwd_loss", grid=(S // tm,),
        in_specs=_conv_specs(tm) + half(3) + half(1) + [_full((D_FF, D_MODEL)), row(D_MODEL), _full((1, D_MODEL)), row(D_MODEL)],
        out_specs=[row(D_MODEL), _full((1, D_MODEL)), _full((1, D_MODEL))],
        out_shape=[jax.ShapeDtypeStruct((S, D_MODEL), F32), jax.ShapeDtypeStruct((1, D_MODEL), F32),
                   jax.ShapeDtypeStruct((1, D_MODEL), F32)],
        compiler_params=_params(("arbitrary",)),
    )(a, a, a, a, w_conv, w_conv, b_conv, b_conv, w_down, x1, g3, target)


def _ffn_bwd_gate(dx2, a, w_conv, b_conv, w_down, tm):
    S = dx2.shape[0]

    def body(dx_ref, ag_ref, av_ref, hg_ref, hv_ref, wg_ref, wv_ref, bg_ref, bv_ref, wd_ref,
             dc_ref, y_ref, dw_ref, db_ref):
        i = pl.program_id(0)

        @pl.when(i == 0)
        def _():
            dw_ref[...] = jnp.zeros_like(dw_ref)
            db_ref[...] = jnp.zeros_like(db_ref)

        dx = dx_ref[...].astype(BF16)
        shifts = _shift_matrices(tm)
        for cols in _FF_CHUNKS:
            gate, val, gtaps, vtaps = _conv_gate_val((ag_ref, av_ref, hg_ref, hv_ref, wg_ref, wv_ref, bg_ref, bv_ref), shifts, cols, i == 0)
            sg = _sigmoid(gate)
            act = gate * sg
            y_ref[:, cols] = (act * val).astype(BF16)
            dy = lax.dot_general(dx, wd_ref[cols, :], NT, preferred_element_type=F32)
            dgate = dy * val * (sg + act - act * sg)
            dval = dy * act
            for d, taps, out in ((dgate, gtaps, cols), (dval, vtaps, slice(D_FF + cols.start, D_FF + cols.stop))):
                dc_ref[:, out] = d.astype(BF16)
                db_ref[0:1, out] += jnp.sum(d, axis=0, keepdims=True)
                for j in range(3):
                    dw_ref[j:j + 1, out] += jnp.sum(d * taps[j], axis=0, keepdims=True)

    row = lambda w: pl.BlockSpec((tm, w), lambda i: (i, 0))
    half = lambda r: [pl.BlockSpec((r, D_FF), lambda i: (0, 0)), pl.BlockSpec((r, D_FF), lambda i: (0, 1))]
    return pl.pallas_call(
        body, name="ffn_bwd_gate", grid=(S // tm,),
        in_specs=[row(D_MODEL)] + _conv_specs(tm) + half(3) + half(1) + [_full((D_FF, D_MODEL))],
        out_specs=[row(2 * D_FF), row(D_FF), _full((SUBLANES, 2 * D_FF)), _full((1, 2 * D_FF))],
        out_shape=[jax.ShapeDtypeStruct((S, 2 * D_FF), BF16), jax.ShapeDtypeStruct((S, D_FF), BF16),
                   jax.ShapeDtypeStruct((SUBLANES, 2 * D_FF), F32), jax.ShapeDtypeStruct((1, 2 * D_FF), F32)],
        compiler_params=_params(("arbitrary",)),
    )(dx2, a, a, a, a, w_conv, w_conv, b_conv, b_conv, w_down)


def _conv_bwd(dc, w_conv, tm, tn):
    S, C = dc.shape
    step = tm // SUBLANES
    last_blk = S // SUBLANES - 1

    def body(d_ref, nx_ref, w_ref, o_ref):
        last = pl.program_id(0) == pl.num_programs(0) - 1
        d = d_ref[...].astype(F32)
        nx = nx_ref[...].astype(F32) * jnp.where(last, 0.0, 1.0)
        out = w_ref[2:3, :] * d + w_ref[1:2, :] * _shift_up(d, nx, 1) + w_ref[0:1, :] * _shift_up(d, nx, 2)
        o_ref[...] = out.astype(BF16)

    return pl.pallas_call(
        body, name="conv_bwd", grid=(S // tm, C // tn),
        in_specs=[pl.BlockSpec((tm, tn), lambda i, j: (i, j)),
                  pl.BlockSpec((SUBLANES, tn), lambda i, j: (jnp.minimum((i + 1) * step, last_blk), j)),
                  pl.BlockSpec((3, tn), lambda i, j: (0, j))],
        out_specs=pl.BlockSpec((tm, tn), lambda i, j: (i, j)),
        out_shape=jax.ShapeDtypeStruct((S, C), BF16),
        compiler_params=_params(("parallel", "parallel")),
    )(dc, dc, w_conv)


def _matmul_tn(a, b, name, bm, bn, tk, col_a=0, col_b=0, quarters=None):
    S = a.shape[0]
    gm, gn = quarters if quarters else (1, 1)
    nk = S // tk

    def body(a_ref, b_ref, o_ref):
        @pl.when(pl.program_id(2) == 0)
        def _():
            o_ref[...] = jnp.zeros_like(o_ref)

        o_ref[...] += lax.dot_general(a_ref[...].astype(BF16), b_ref[...].astype(BF16), TN, preferred_element_type=F32)

    if quarters and gn > 1:
        out_spec = pl.BlockSpec((None, bm, bn), lambda i, j, k: (j, i, 0))
        out_shape = jax.ShapeDtypeStruct((gn, gm * bm, bn), F32)
    else:
        out_spec = pl.BlockSpec((bm, bn), lambda i, j, k: (i, j))
        out_shape = jax.ShapeDtypeStruct((gm * bm, gn * bn), F32)
    return pl.pallas_call(
        body, name=name, grid=(gm, gn, nk),
        in_specs=[pl.BlockSpec((tk, bm), lambda i, j, k: (k, col_a * gm + i)),
                  pl.BlockSpec((tk, bn), lambda i, j, k: (k, col_b * gn + j))],
        out_specs=out_spec, out_shape=out_shape,
        compiler_params=_params(("parallel", "parallel", "arbitrary")),
    )(a, b)


def _up_bwd(dact, w_up_q, x1, g2, dx2, tm):
    S = x1.shape[0]
    nq, _, wq = w_up_q.shape

    def body(d_ref, w_ref, x_ref, g_ref, dx2_ref, dx1_ref, dg_ref):
        @pl.when(pl.program_id(0) == 0)
        def _():
            dg_ref[...] = jnp.zeros_like(dg_ref)

        dh = jnp.zeros((tm, D_MODEL), F32)
        for j in range(nq):
            dh = dh + lax.dot_general(d_ref[:, j * wq:(j + 1) * wq], w_ref[j], NT, preferred_element_type=F32)
        dx, dg = _rms_bwd(dh, x_ref[...], g_ref[...])
        dg_ref[...] += dg
        dx1_ref[...] = dx2_ref[...] + dx

    row = lambda w: pl.BlockSpec((tm, w), lambda i: (i, 0))
    return pl.pallas_call(
        body, name="up_bwd", grid=(S // tm,),
        in_specs=[row(nq * wq), pl.BlockSpec((nq, D_MODEL, wq), lambda i: (0, 0, 0), pipeline_mode=pl.Buffered(1)),
                  row(D_MODEL), _full((1, D_MODEL)), row(D_MODEL)],
        out_specs=[row(D_MODEL), _full((1, D_MODEL))],
        out_shape=[jax.ShapeDtypeStruct((S, D_MODEL), F32), jax.ShapeDtypeStruct((1, D_MODEL), F32)],
        compiler_params=_params(("arbitrary",)),
    )(dact, w_up_q, x1, g2, dx2)


def _out_bwd(dx1, w_out, tm):
    S = dx1.shape[0]

    def body(d_ref, w_ref, o_ref):
        o_ref[...] = lax.dot_general(d_ref[...].astype(BF16), w_ref[...], NT, preferred_element_type=F32).astype(BF16)

    return pl.pallas_call(
        body, name="out_bwd", grid=(S // tm,),
        in_specs=[pl.BlockSpec((tm, D_MODEL), lambda i: (i, 0)), _full((D_MODEL, D_MODEL))],
        out_specs=pl.BlockSpec((tm, D_MODEL), lambda i: (i, 0)),
        out_shape=jax.ShapeDtypeStruct((S, D_MODEL), BF16),
        compiler_params=_params(("parallel",)),
    )(dx1, w_out)


def _gate_bwd(z, dcat, w_mask, w_mask_t, ln_row, b_full, seg_avg, head_ind, tm):
    S = z.shape[0]
    nb = tm // SG_BLOCK

    def body(zu_ref, zv_ref, do_ref, w_ref, wt_ref, ln_ref, b_ref, avg_ref, ind_ref,
             dzu_ref, dzv_ref, dw_ref, db_ref, dln_ref, dvn_s, dbf_s):
        i = pl.program_id(0)

        @pl.when(i == 0)
        def _():
            dw_ref[...] = jnp.zeros_like(dw_ref)
            dln_ref[...] = jnp.zeros_like(dln_ref)
            dbf_s[...] = jnp.zeros_like(dbf_s)

        zu = zu_ref[...].astype(F32)
        zv = zv_ref[...].astype(F32)
        u = _gelu(zu)
        v = _gelu(zv)
        avg = avg_ref[...]
        vhat, rstd = _layer_norm_heads(v, avg)
        ln = ln_ref[...]
        vn = vhat * ln
        for b in range(nb):
            rows = slice(b * SG_BLOCK, (b + 1) * SG_BLOCK)
            vn_b = vn[rows]
            mixed = _gate_mix(vn_b, w_ref, b_ref[...])
            do = do_ref[rows, :].astype(F32)
            dzu_ref[rows, :] = (do * mixed * _gelu_grad(zu[rows])).astype(BF16)
            dmix = do * u[rows]
            dbf_s[...] += dmix
            vn_bf = vn_b.astype(BF16)
            dvn = jnp.zeros((SG_BLOCK, D_HEADS), F32)
            for h in range(N_HEADS):
                dmh = jnp.where(_head_mask(h, SG_BLOCK), dmix, 0.0).astype(BF16)
                dw_ref[h] += lax.dot_general(dmh, vn_bf, NT, preferred_element_type=F32)
                dvn = dvn + jnp.dot(wt_ref[h], dmh, preferred_element_type=F32)
            dvn_s[rows, :] = dvn
        dvn = dvn_s[...]
        dln_ref[...] += jnp.sum(dvn * vhat, axis=0, keepdims=True)
        dvhat = dvn * ln
        dv = rstd * (dvhat - _split_dot(dvhat, avg) - vhat * _split_dot(dvhat * vhat, avg))
        dzv_ref[...] = (dv * _gelu_grad(zv)).astype(BF16)

        @pl.when(i == pl.num_programs(0) - 1)
        def _():
            r = lax.broadcasted_iota(jnp.int32, (SG_BLOCK, SG_BLOCK), 0) // CHUNK
            s = lax.broadcasted_iota(jnp.int32, (SG_BLOCK, SG_BLOCK), 1) // CHUNK
            for h in range(N_HEADS):
                dw_ref[h] = jnp.where(r >= s, dw_ref[h], 0.0)
            db_ref[...] = _split_dot(dbf_s[...], ind_ref[...])

    row = lambda col: pl.BlockSpec((tm, D_HEADS), lambda i: (i, col))
    wspec = _full((N_HEADS, SG_BLOCK, SG_BLOCK))
    return pl.pallas_call(
        body, name="gate_bwd", grid=(S // tm,),
        in_specs=[row(0), row(1), row(0), wspec, wspec, _full((1, D_HEADS)), _full((SG_BLOCK, D_HEADS)),
                  _full((D_HEADS, D_HEADS)), _full((D_HEADS, LANES))],
        out_specs=[row(0), row(0), wspec, _full((SG_BLOCK, LANES)), _full((1, D_HEADS))],
        out_shape=[jax.ShapeDtypeStruct((S, D_HEADS), BF16), jax.ShapeDtypeStruct((S, D_HEADS), BF16),
                   jax.ShapeDtypeStruct((N_HEADS, SG_BLOCK, SG_BLOCK), F32), jax.ShapeDtypeStruct((SG_BLOCK, LANES), F32),
                   jax.ShapeDtypeStruct((1, D_HEADS), F32)],
        scratch_shapes=[pltpu.VMEM((tm, D_HEADS), F32), pltpu.VMEM((SG_BLOCK, D_HEADS), F32)],
        compiler_params=_params(("arbitrary",)),
    )(z, z, dcat, w_mask, w_mask_t, ln_row, b_full, seg_avg, head_ind)


def _attn_pack_grad(o, dcat, qa, lse, head_ind, k, tm):
    S = o.shape[0]

    def body(o_ref, do_ref, qa_ref, lse_ref, ind_ref, pl_ref, pt_ref, eye_ref, ds_ref, dst_ref, ls_ref, lst_ref,
             dop_ref, qb_ref, dot_ref, qbt_ref):
        do = do_ref[...]
        delta = _split_dot(o_ref[...].astype(F32) * do.astype(F32), ind_ref[...])
        hi = delta.astype(BF16)
        lo = (delta - hi.astype(F32)).astype(BF16)
        dop = jnp.dot(do, pl_ref[...], preferred_element_type=F32)
        dop = dop - jnp.dot(hi, ds_ref[0], preferred_element_type=F32) - jnp.dot(lo, ds_ref[1], preferred_element_type=F32)
        dop_ref[...] = dop.astype(BF16)
        dot = lax.dot_general(pt_ref[...], do, NT, preferred_element_type=F32)
        dot = dot - lax.dot_general(dst_ref[0], hi, NT, preferred_element_type=F32)
        dot = dot - lax.dot_general(dst_ref[1], lo, NT, preferred_element_type=F32)
        dot_ref[...] = dot.astype(BF16)
        qa = qa_ref[...]
        qb = qa.astype(F32)
        qbt = lax.dot_general(eye_ref[...], qa, NT, preferred_element_type=F32)
        for j, part in enumerate(_split3(lse_ref[...])):
            qb = qb - lax.dot_general(part, ls_ref[j], TN, preferred_element_type=F32)
            qbt = qbt - jnp.dot(lst_ref[j], part, preferred_element_type=F32)
        qb_ref[...] = qb.astype(BF16)
        qbt_ref[...] = qbt.astype(BF16)

    pad = pl.BlockSpec((tm, D_PAD), lambda i: (i, 0))
    padt = pl.BlockSpec((D_PAD, tm), lambda i: (0, i))
    return pl.pallas_call(
        body, name="attn_pack_grad", grid=(S // tm,),
        in_specs=[pl.BlockSpec((tm, D_HEADS), lambda i: (i, 0)), pl.BlockSpec((tm, D_HEADS), lambda i: (i, 1)), pad,
                  pl.BlockSpec((STAT_ROWS, tm), lambda i: (0, i)), _full((D_HEADS, LANES)), _full((D_HEADS, D_PAD)),
                  _full((D_PAD, D_HEADS)), _full((D_PAD, D_PAD)), _full((2, LANES, D_PAD)), _full((2, D_PAD, LANES)),
                  _full((3, STAT_ROWS, D_PAD)), _full((3, D_PAD, STAT_ROWS))],
        out_specs=[pad, pad, padt, padt],
        out_shape=[jax.ShapeDtypeStruct((S, D_PAD), BF16)] * 2 + [jax.ShapeDtypeStruct((D_PAD, S), BF16)] * 2,
        compiler_params=_params(("parallel",)),
    )(o, dcat, qa, lse, head_ind, k["place"], k["place_t"], jnp.eye(D_PAD, dtype=BF16), k["d_stat"],
      jnp.swapaxes(k["d_stat"], 1, 2), k["l_stat"], jnp.swapaxes(k["l_stat"], 1, 2))


def _attn_bwd(qb, qbt, ka, va, dop, dopt, k, tq, sums16):
    S = qb.shape[0]
    n = S // tq
    ns = len(sums16)

    def body(q_ref, qt_ref, k_ref, v_ref, do_ref, dot_ref, pt_ref, pick_ref, *rest):
        dq_hbm, dk_ref, dv_ref, dcc_ref = rest[ns:ns + 4]
        dq_s, dk_s, dv_s, s_s, d_s, sem = rest[2 * ns + 4:2 * ns + 10]
        scatter_start, scatter_finish = _scatter_ops(rest[:ns], rest[ns + 4:2 * ns + 4], *rest[2 * ns + 10:])
        g, ki, qi = pl.program_id(0), pl.program_id(1), pl.program_id(2)

        @pl.when((g == 0) & (ki == 0) & (qi == 0))
        def _():
            scatter_start()

        @pl.when((ki == 0) & (qi == 0))
        def _():
            dq_s[...] = jnp.zeros_like(dq_s)

        @pl.when(qi == ki)
        def _():
            dk_s[...] = jnp.zeros_like(dk_s)
            dv_s[...] = jnp.zeros_like(dv_s)

        def step(diagonal):
            chunks = [slice(c * KEY_CHUNK, (c + 1) * KEY_CHUNK) for c in range(tq // KEY_CHUNK)]

            def scores(hh, rows, slot):
                sl = slice(hh * HEAD_PAD, (hh + 1) * HEAD_PAD)
                s_s[slot, rows, :] = lax.dot_general(q_ref[rows, sl], k_ref[:, sl], NT, preferred_element_type=F32)
                d_s[slot, rows, :] = lax.dot_general(do_ref[rows, sl], v_ref[:, sl], NT, preferred_element_type=F32)

            for rows in chunks:
                scores(0, rows, 0)
            for hh in range(GROUP_HEADS):
                sl = slice(hh * HEAD_PAD, (hh + 1) * HEAD_PAD)
                slot = hh % 2
                dv, dk = dv_s[sl, :], dk_s[sl, :]
                for rows in chunks:
                    if hh + 1 < GROUP_HEADS:
                        scores(hh + 1, rows, 1 - slot)
                    p = jnp.exp2(s_s[slot, rows, :])
                    if diagonal:
                        row = rows.start + lax.broadcasted_iota(jnp.int32, (KEY_CHUNK, tq), 0)
                        col = lax.broadcasted_iota(jnp.int32, (KEY_CHUNK, tq), 1)
                        p = jnp.where(row >= col, p, 0.0)
                    ds = (p * d_s[slot, rows, :]).astype(BF16)
                    dv = dv + jnp.dot(dot_ref[sl, rows], p.astype(BF16), preferred_element_type=F32)
                    dk = dk + jnp.dot(qt_ref[sl, rows], ds, preferred_element_type=F32)
                    qrows = pl.ds(pl.multiple_of(qi * tq + rows.start, KEY_CHUNK), KEY_CHUNK)
                    dq_s[qrows, sl] += jnp.dot(ds, k_ref[:, sl], preferred_element_type=F32)
                dv_s[sl, :] = dv
                dk_s[sl, :] = dk

        @pl.when(qi > ki)
        def _():
            step(False)

        @pl.when(qi == ki)
        def _():
            step(True)

        @pl.when(qi == n - 1)
        def _():
            dk = dk_s[...]
            pt = pt_ref[...]
            dk_ref[...] = lax.dot_general((dk * (1.0 / LOG2E)).astype(BF16), pt, TN, preferred_element_type=F32).astype(BF16)
            dv_ref[...] = lax.dot_general(dv_s[...].astype(BF16), pt, TN, preferred_element_type=F32).astype(BF16)
            dcc_ref[...] = sum(lax.dot_general(part, pick_ref[...], TN, preferred_element_type=F32) for part in _split3(dk))

        @pl.when((ki == n - 1) & (qi == n - 1))
        def _():
            cp = pltpu.make_async_copy(dq_s, dq_hbm.at[g], sem)
            cp.start()
            cp.wait()

        @pl.when((g == GROUPS - 1) & (ki == n - 1) & (qi == n - 1))
        def _():
            scatter_finish()

    gw = GROUP_HEADS * HEAD_DIM
    qmax = lambda ki, qi: jnp.maximum(qi, ki)
    qspec = pl.BlockSpec((tq, GROUP_PAD), lambda g, ki, qi: (qmax(ki, qi), g))
    qtspec = pl.BlockSpec((GROUP_PAD, tq), lambda g, ki, qi: (g, qmax(ki, qi)))
    kspec = pl.BlockSpec((tq, GROUP_PAD), lambda g, ki, qi: (ki, g))
    kout = pl.BlockSpec((tq, gw), lambda g, ki, qi: (ki, g))
    out = pl.pallas_call(
        body, name="attn_bwd", grid=(GROUPS, n, n),
        in_specs=[qspec, qtspec, kspec, kspec, qspec, qtspec,
                  _full((GROUP_PAD, gw)), pl.BlockSpec((None, GROUP_PAD, LANES), lambda g, ki, qi: (g, 0, 0))] + [_ANY] * ns,
        out_specs=[_ANY, kout, kout, pl.BlockSpec((None, tq, LANES), lambda g, ki, qi: (g, ki, 0))] + [_ANY] * ns,
        out_shape=[jax.ShapeDtypeStruct((GROUPS, S, GROUP_PAD), F32), jax.ShapeDtypeStruct((S, D_HEADS), BF16),
                   jax.ShapeDtypeStruct((S, D_HEADS), BF16), jax.ShapeDtypeStruct((GROUPS, S, LANES), F32)]
        + _scatter_shapes(sums16),
        scratch_shapes=[pltpu.VMEM((S, GROUP_PAD), F32), pltpu.VMEM((GROUP_PAD, tq), F32), pltpu.VMEM((GROUP_PAD, tq), F32),
                        pltpu.VMEM((2, tq, tq), F32), pltpu.VMEM((2, tq, tq), F32), pltpu.SemaphoreType.DMA] + _scatter_sems(ns),
        compiler_params=_params(("arbitrary", "arbitrary", "arbitrary")),
    )(qb, qbt, ka, va, dop, dopt, k["place_t_group"], k["pick_cols"], *sums16)
    return out[0], out[1], out[2], out[3], out[4:]


def _attn_unpack(dqp, dcc, k, tm):
    S = dqp.shape[1]
    gw = GROUP_HEADS * HEAD_DIM

    def body(dqp_ref, dcc_ref, pt_ref, pick_ref, dq_ref, dc_ref):
        dc = jnp.zeros((tm, LANES), F32)
        for g in range(GROUPS):
            x = dqp_ref[g]
            dq_ref[:, g * gw:(g + 1) * gw] = jnp.dot((x * SCALE).astype(BF16), pt_ref[...], preferred_element_type=F32).astype(BF16)
            dc = dc + _split3_dot(x, pick_ref[g]) - dcc_ref[g]
        dc_ref[...] = dc

    return pl.pallas_call(
        body, name="attn_unpack", grid=(S // tm,),
        in_specs=[pl.BlockSpec((GROUPS, tm, GROUP_PAD), lambda i: (0, i, 0)), pl.BlockSpec((GROUPS, tm, LANES), lambda i: (0, i, 0)),
                  _full((GROUP_PAD, gw)), _full((GROUPS, GROUP_PAD, LANES))],
        out_specs=[pl.BlockSpec((tm, D_HEADS), lambda i: (i, 0)), pl.BlockSpec((tm, LANES), lambda i: (i, 0))],
        out_shape=[jax.ShapeDtypeStruct((S, D_HEADS), BF16), jax.ShapeDtypeStruct((S, LANES), F32)],
        compiler_params=_params(("parallel",)),
    )(dqp, dcc, k["place_t_group"], k["pick_rows"])


def _fox_bwd(dc, f, bias_row, tb):
    S = f.shape[0]
    nb = S // tb

    def body(dc_ref, f_ref, b_ref, df_ref, dbias_ref, carry):
        @pl.when(pl.program_id(0) == 0)
        def _():
            carry[...] = jnp.zeros_like(carry)
            dbias_ref[...] = jnp.zeros_like(dbias_ref)

        r = lax.broadcasted_iota(jnp.int32, (tb, tb), 0)
        s = lax.broadcasted_iota(jnp.int32, (tb, tb), 1)
        tri = (s >= r).astype(F32)
        rc = jnp.dot(tri, dc_ref[...], precision=lax.Precision.HIGHEST, preferred_element_type=F32) + carry[0:1, :]
        carry[...] = jnp.broadcast_to(rc[0:1, :], carry.shape)
        lane = lax.broadcasted_iota(jnp.int32, (tb, LANES), 1)
        df = jnp.where(lane < N_HEADS, rc * jax.nn.sigmoid(-(f_ref[...] + b_ref[...])), 0.0)
        df_ref[...] = df.astype(BF16)
        dbias_ref[...] += jnp.sum(df, axis=0, keepdims=True)

    rev = pl.BlockSpec((tb, LANES), lambda i: (nb - 1 - i, 0))
    return pl.pallas_call(
        body, name="fox_bwd", grid=(nb,),
        in_specs=[rev, rev, _full((1, LANES))],
        out_specs=[rev, _full((1, LANES))],
        out_shape=[jax.ShapeDtypeStruct((S, LANES), BF16), jax.ShapeDtypeStruct((1, LANES), F32)],
        scratch_shapes=[pltpu.VMEM((SUBLANES, LANES), F32)],
        compiler_params=_params(("arbitrary",)),
    )(dc, f, bias_row)


_DZ_WIDTHS = (D_HEADS,) * 5 + (LANES,)


def _in_bwd(pieces, w_in, x, g1, dx1, tm):
    S = x.shape[0]

    def body(*refs):
        p_refs, (w_ref, x_ref, g_ref, dx1_ref, dx_ref, dg_ref) = refs[:6], refs[6:]

        @pl.when(pl.program_id(0) == 0)
        def _():
            dg_ref[...] = jnp.zeros_like(dg_ref)

        dh = jnp.zeros((tm, D_MODEL), F32)
        off = 0
        for p_ref, w in zip(p_refs, _DZ_WIDTHS):
            dh = dh + lax.dot_general(p_ref[...].astype(BF16), w_ref[:, off:off + w], NT, preferred_element_type=F32)
            off += w
        dx, dg = _rms_bwd(dh, x_ref[...], g_ref[...])
        dg_ref[...] += dg
        dx_ref[...] = dx1_ref[...] + dx

    row = lambda w: pl.BlockSpec((tm, w), lambda i: (i, 0))
    return pl.pallas_call(
        body, name="in_bwd", grid=(S // tm,),
        in_specs=[row(w) for w in _DZ_WIDTHS] + [_full((D_MODEL, D_IN_PAD)), row(D_MODEL), _full((1, D_MODEL)), row(D_MODEL)],
        out_specs=[row(D_MODEL), _full((1, D_MODEL))],
        out_shape=[jax.ShapeDtypeStruct((S, D_MODEL), F32), jax.ShapeDtypeStruct((1, D_MODEL), F32)],
        compiler_params=_params(("arbitrary",)),
    )(*pieces, w_in, x, g1, dx1)


def _dw_in(h1, pieces, tk):
    S = h1.shape[0]

    def body(*refs):
        h_ref, p_refs, o_ref = refs[0], refs[1:7], refs[7]

        @pl.when(pl.program_id(0) == 0)
        def _():
            o_ref[...] = jnp.zeros_like(o_ref)

        off = 0
        for p_ref, w in zip(p_refs, _DZ_WIDTHS):
            o_ref[:, off:off + w] += lax.dot_general(h_ref[...], p_ref[...].astype(BF16), TN, preferred_element_type=F32)
            off += w

    row = lambda w: pl.BlockSpec((tk, w), lambda k: (k, 0))
    return pl.pallas_call(
        body, name="dw_in", grid=(S // tk,),
        in_specs=[row(D_MODEL)] + [row(w) for w in _DZ_WIDTHS],
        out_specs=_full((D_MODEL, D_IN_PAD)),
        out_shape=jax.ShapeDtypeStruct((D_MODEL, D_IN_PAD), F32),
        compiler_params=_params(("arbitrary",)),
    )(h1, *pieces)


def _adamw_math(w, g, m, v):
    m = ADAM_B1 * m + (1.0 - ADAM_B1) * g
    v = ADAM_B2 * v + (1.0 - ADAM_B2) * (g * g)
    m_hat = m / (1.0 - ADAM_B1 ** ADAM_STEP)
    v_hat = v / (1.0 - ADAM_B2 ** ADAM_STEP)
    delta = -ADAM_LR * (m_hat / (jnp.sqrt(v_hat) + ADAM_EPS) + ADAM_WD * w)
    return delta, m, v


def _adamw(name, w, g, m, v):
    R, C = w.shape
    tr = _row_tile(R, 256)

    def body(w_ref, g_ref, m_ref, v_ref, d_ref, nm_ref, nv_ref):
        d, nm, nv = _adamw_math(w_ref[...], g_ref[...], m_ref[...], v_ref[...])
        d_ref[...] = d
        nm_ref[...] = nm
        nv_ref[...] = nv

    spec = pl.BlockSpec((tr, C), lambda i: (i, 0))
    return pl.pallas_call(
        body, name=name, grid=(R // tr,), in_specs=[spec] * 4, out_specs=[spec] * 3,
        out_shape=[jax.ShapeDtypeStruct((R, C), F32)] * 3,
        compiler_params=_params(("parallel",)),
    )(w, g, m, v)


def _pair_sum(name, grad, theirs, ids):
    q, half, C = theirs.shape
    tr = _row_tile(half, 256)
    nb = half // tr

    def body(ids_ref, a_ref, b_ref, s_ref, sb_ref):
        s = a_ref[...] + b_ref[...]
        s_ref[...] = s
        sb_ref[...] = s.astype(BF16)

    here = pl.BlockSpec((None, tr, C), lambda j, i, ids: (j, i, 0))
    return pl.pallas_call(
        body, name=name,
        grid_spec=pltpu.PrefetchScalarGridSpec(
            num_scalar_prefetch=1, grid=(q, nb),
            in_specs=[pl.BlockSpec((None, tr, C), lambda j, i, ids: (j, ids[1] * nb + i, 0)), here],
            out_specs=[here, here]),
        out_shape=[jax.ShapeDtypeStruct((q, half, C), F32), jax.ShapeDtypeStruct((q, half, C), BF16)],
        compiler_params=_params(("parallel", "parallel")),
    )(ids, grad, theirs)


def _chip_sum(name, sums32, others, ids):
    _, half, C = sums32.shape
    tr = _row_tile(half, 256)
    nb = half // tr

    def body(ids_ref, a_ref, o_ref, s_ref):
        s = a_ref[...]
        for j in range(3):
            s = s + o_ref[j].astype(F32)
        s_ref[...] = s

    return pl.pallas_call(
        body, name=name,
        grid_spec=pltpu.PrefetchScalarGridSpec(
            num_scalar_prefetch=1, grid=(nb,),
            in_specs=[pl.BlockSpec((None, tr, C), lambda i, ids: (ids[0], i, 0)),
                      pl.BlockSpec((3, tr, C), lambda i, ids: (0, i, 0))],
            out_specs=pl.BlockSpec((tr, C), lambda i, ids: (ids[1] * nb + i, 0))),
        out_shape=jax.ShapeDtypeStruct((2 * half, C), F32),
        compiler_params=_params(("parallel",)),
    )(ids, sums32, others)


def _place():
    return lax.axis_index("x"), lax.axis_index("y"), lax.axis_index("c")


def _other_chips(x, y):
    return [(1 - x, y), (x, 1 - y), (1 - x, 1 - y)]


_ANY = pl.BlockSpec(memory_space=pl.ANY)


def _gather_quarters(shards):
    n = len(shards)

    def body(*refs):
        start, hand_on, finish = _gather_ops(refs[:n], refs[n:2 * n], *refs[2 * n:])
        start()
        hand_on()
        finish()

    return pl.pallas_call(
        body, name="gather_weights",
        in_specs=[_ANY] * n, out_specs=[_ANY] * n,
        out_shape=_gather_shapes(shards), scratch_shapes=_gather_sems(n),
    )(*shards)


def _gather_shapes(shards):
    return [jax.ShapeDtypeStruct((4,) + s.shape, s.dtype) for s in shards]


def _gather_sems(n):
    return [pltpu.SemaphoreType.DMA((n, 3))] * 4 + [pltpu.SemaphoreType.DMA((n,))]


def _gather_ops(ins, outs, send_sems, recv_sems, pass_send_sems, pass_recv_sems, own_sems):
    n = len(ins)
    halved = [r.shape[0] % 32 == 0 for r in ins]

    def part(a, quarter, core):
        if not halved[a]:
            return outs[a].at[quarter]
        half = ins[a].shape[0] // 2
        return outs[a].at[quarter, pl.ds(core * half, half), :]

    def ici(a, j, quarter):
        x, y, c = _place()
        px, py = _other_chips(x, y)[j]
        src = ins[a]
        if halved[a]:
            half = src.shape[0] // 2
            src = src.at[pl.ds(c * half, half), :]
        return pltpu.make_async_remote_copy(src_ref=src, dst_ref=part(a, quarter, c), send_sem=send_sems.at[a, j],
                                            recv_sem=recv_sems.at[a, j], device_id=(px, py, c), device_id_type=MESH)

    def passed(a, j, core):
        x, y, c = _place()
        px, py = _other_chips(x, y)[j]
        half = part(a, 2 * px + py, core)
        return pltpu.make_async_remote_copy(src_ref=half, dst_ref=half, send_sem=pass_send_sems.at[a, j],
                                            recv_sem=pass_recv_sems.at[a, j], device_id=(x, y, 1 - c), device_id_type=MESH)

    def own(a):
        x, y, _ = _place()
        return pltpu.make_async_copy(ins[a], outs[a].at[2 * x + y], own_sems.at[a])

    def start():
        x, y, _ = _place()
        for a in range(n):
            for j in range(3):
                ici(a, j, 2 * x + y).start()
            own(a).start()

    def hand_on():
        x, y, c = _place()
        for a in range(n):
            for j, (px, py) in enumerate(_other_chips(x, y)):
                ici(a, j, 2 * px + py).wait_recv()
                if halved[a]:
                    passed(a, j, c).start()

    def finish():
        x, y, c = _place()
        for a in range(n):
            for j in range(3):
                if halved[a]:
                    passed(a, j, 1 - c).wait_recv()
                    passed(a, j, c).wait_send()
                ici(a, j, 2 * x + y).wait_send()
            own(a).wait()

    return start, hand_on, finish


def _swap_halves(grads, name):
    n = len(grads)

    def body(*refs):
        ins, outs = refs[:n], refs[n:2 * n]
        send_sems, recv_sems = refs[2 * n:]
        x, y, c = _place()
        started = []
        for a in range(n):
            half = ins[a].shape[1] // 2
            cp = pltpu.make_async_remote_copy(src_ref=ins[a].at[:, pl.ds((1 - c) * half, half), :], dst_ref=outs[a],
                                              send_sem=send_sems.at[a], recv_sem=recv_sems.at[a],
                                              device_id=(x, y, 1 - c), device_id_type=MESH)
            cp.start()
            started.append(cp)
        for cp in started:
            cp.wait()

    return pl.pallas_call(
        body, name=name,
        in_specs=[_ANY] * n, out_specs=[_ANY] * n,
        out_shape=[jax.ShapeDtypeStruct((4, g.shape[1] // 2, g.shape[2]), F32) for g in grads],
        scratch_shapes=[pltpu.SemaphoreType.DMA((n,)), pltpu.SemaphoreType.DMA((n,))],
    )(*grads)


def _scatter_quarters(sums16):
    n = len(sums16)

    def body(*refs):
        start, finish = _scatter_ops(refs[:n], refs[n:2 * n], *refs[2 * n:])
        start()
        finish()

    return pl.pallas_call(
        body, name="scatter_quarters",
        in_specs=[_ANY] * n, out_specs=[_ANY] * n,
        out_shape=_scatter_shapes(sums16), scratch_shapes=_scatter_sems(n),
    )(*sums16)


def _scatter_shapes(sums16):
    return [jax.ShapeDtypeStruct((3,) + s.shape[1:], BF16) for s in sums16]


def _scatter_sems(n):
    return [pltpu.SemaphoreType.DMA((n, 3))] * 2


def _scatter_ops(ins, outs, send_sems, recv_sems):
    n = len(ins)

    def copy(a, j):
        x, y, c = _place()
        px, py = _other_chips(x, y)[j]
        return pltpu.make_async_remote_copy(src_ref=ins[a].at[2 * px + py], dst_ref=outs[a].at[j], send_sem=send_sems.at[a, j],
                                            recv_sem=recv_sems.at[a, j], device_id=(px, py, c), device_id_type=MESH)

    def start():
        for a in range(n):
            for j in range(3):
                copy(a, j).start()

    def finish():
        for a in range(n):
            for j in range(3):
                copy(a, j).wait()

    return start, finish


def _join_halves(fulls):
    n = len(fulls)

    def body(*refs):
        ins, outs = refs[:n], refs[n:2 * n]
        send_sems, recv_sems = refs[2 * n:]
        x, y, c = _place()
        started = []
        for a in range(n):
            half = ins[a].shape[0] // 2
            rows = pl.ds(c * half, half)
            cp = pltpu.make_async_remote_copy(src_ref=ins[a].at[rows, :], dst_ref=outs[a].at[rows, :], send_sem=send_sems.at[a],
                                              recv_sem=recv_sems.at[a], device_id=(x, y, 1 - c), device_id_type=MESH)
            cp.start()
            started.append(cp)
        for cp in started:
            cp.wait()

    return pl.pallas_call(
        body, name="join_halves",
        in_specs=[_ANY] * n, out_specs=[_ANY] * n,
        out_shape=[jax.ShapeDtypeStruct(f.shape, F32) for f in fulls],
        input_output_aliases={a: a for a in range(n)},
        scratch_shapes=[pltpu.SemaphoreType.DMA((n,)), pltpu.SemaphoreType.DMA((n,))],
    )(*fulls)


def _small_allreduce(g):
    R = g.shape[0]
    half = R // 2

    def body(g_ref, out_ref, other_s, chip_s, parts_s, send_sems, recv_sems):
        x, y, c = _place()
        mine = 2 * x + y
        rows = pl.ds(pl.multiple_of(c * half, SUBLANES), half)

        def to_other_core(src, dst, k):
            return pltpu.make_async_remote_copy(src_ref=src, dst_ref=dst, send_sem=send_sems.at[k], recv_sem=recv_sems.at[k],
                                                device_id=(x, y, 1 - c), device_id_type=MESH)

        swap = to_other_core(g_ref, other_s, 0)
        swap.start()
        swap.wait()
        chip_s[...] = g_ref[...] + other_s[...]
        parts_s[mine] = chip_s[rows, :]
        sends = []
        for j, (px, py) in enumerate(_other_chips(x, y)):
            cp = pltpu.make_async_remote_copy(src_ref=chip_s.at[rows, :], dst_ref=parts_s.at[mine], send_sem=send_sems.at[1 + j],
                                              recv_sem=recv_sems.at[1 + j], device_id=(px, py, c), device_id_type=MESH)
            cp.start()
            sends.append(cp)
        for cp in sends:
            cp.wait()
        out_ref[rows, :] = (parts_s[0] + parts_s[1]) + (parts_s[2] + parts_s[3])
        join = to_other_core(out_ref.at[rows, :], out_ref.at[rows, :], 4)
        join.start()
        join.wait()

    vm = pl.BlockSpec(memory_space=pltpu.VMEM)
    return pl.pallas_call(
        body, name="small_allreduce",
        in_specs=[vm], out_specs=vm, out_shape=jax.ShapeDtypeStruct((R, LANES), F32),
        scratch_shapes=[pltpu.VMEM((R, LANES), F32), pltpu.VMEM((R, LANES), F32), pltpu.VMEM((4, half, LANES), F32),
                        pltpu.SemaphoreType.DMA((5,)), pltpu.SemaphoreType.DMA((5,))],
        compiler_params=pltpu.CompilerParams(vmem_limit_bytes=VMEM_LIMIT),
    )(g)


def _adamw_small(ws, gs, ms, vs):
    n = len(ws)

    def body(*refs):
        for k in range(n):
            w_ref, g_ref, m_ref, v_ref = (refs[j * n + k] for j in range(4))
            d, nm, nv = _adamw_math(w_ref[...], g_ref[...], m_ref[...], v_ref[...])
            refs[4 * n + k][...] = d
            refs[5 * n + k][...] = nm
            refs[6 * n + k][...] = nv

    vm = pl.BlockSpec(memory_space=pltpu.VMEM)
    out = pl.pallas_call(
        body, name="adamw_small",
        in_specs=[vm] * (4 * n), out_specs=[vm] * (3 * n),
        out_shape=[jax.ShapeDtypeStruct(w.shape, F32) for w in ws] * 3,
        compiler_params=pltpu.CompilerParams(vmem_limit_bytes=VMEM_LIMIT),
    )(*ws, *gs, *ms, *vs)
    return out[:n], out[n:2 * n], out[2 * n:]


_SMALL = (("norm_mix_g", D_MODEL), ("f_bias", N_HEADS), ("sg_ln_g", D_HEADS), ("sg_w", N_HEADS * SG_BLOCK * SG_BLOCK),
          ("sg_b", N_HEADS * SG_BLOCK), ("norm_ffn_g", D_MODEL), ("w_conv", 3 * 2 * D_FF), ("b_conv", 2 * D_FF),
          ("norm_final_g", D_MODEL))


def _pack_small(parts):
    rows = []
    for name, size in _SMALL:
        flat = parts[name].reshape(-1).astype(F32)
        pad = (-size) % (SUBLANES * LANES)
        rows.append(jnp.pad(flat, (0, pad)).reshape(-1, LANES))
    packed = jnp.concatenate(rows, axis=0)
    return jnp.pad(packed, ((0, (-packed.shape[0]) % (2 * SUBLANES)), (0, 0)))


def _unpack_small(packed, shapes):
    out, r = {}, 0
    for name, size in _SMALL:
        nrows = (size + SUBLANES * LANES - 1) // (SUBLANES * LANES) * SUBLANES
        out[name] = packed[r:r + nrows].reshape(-1)[:size].reshape(shapes[name])
        r += nrows
    return out


def _local_step(x, target, g1, w_in, f_bias, sg_ln_g, sg_w, sg_b, g2, b_conv, g3, late_shards, ids):
    S = x.shape[0]
    tm = _row_tile(S, 512)
    tms = _row_tile(S, 256)
    tq = _row_tile(S, 512)

    lane = jnp.arange(D_HEADS)
    seg_avg = jnp.where(lane[:, None] // HEAD_DIM == lane[None, :] // HEAD_DIM, 1.0 / HEAD_DIM, 0.0).astype(BF16)
    head_ind = (lane[:, None] // HEAD_DIM == jnp.arange(LANES)[None, :]).astype(BF16)
    pos_chunk = jnp.arange(SG_BLOCK) // CHUNK
    w_mask32 = jnp.where(pos_chunk[:, None] >= pos_chunk[None, :], sg_w, 0.0)
    w_mask = w_mask32.astype(BF16)
    w_mask_t = jnp.swapaxes(w_mask32, 1, 2).astype(BF16)
    ln_row = sg_ln_g.reshape(1, D_HEADS)
    b_full = jnp.repeat(sg_b.T, HEAD_DIM, axis=1)
    bias_row = jnp.pad(f_bias.reshape(1, N_HEADS), ((0, 0), (0, LANES - N_HEADS)))
    b_conv_row = b_conv.reshape(1, 2 * D_FF)

    z, f, h1 = _in_proj(x, g1, w_in, tm)
    c = _fox_prep(f, bias_row, _row_tile(S, 256))
    consts = _attn_consts()
    qa, ka, va, vat = _attn_pack(z, c, consts, tm)
    out_b, lse, gathered = _attn_fwd(qa, ka, vat, consts["place_t"], tq, late_shards)
    g_out, w_up_q, g_down, g_conv = gathered
    w_out = g_out.reshape(D_MODEL, D_MODEL)
    w_down = g_down.reshape(D_FF, D_MODEL)
    w_conv = jnp.concatenate([g_conv[q] for q in range(4)], axis=1)
    out_a = _gate_fwd(z, w_mask, ln_row, b_full, seg_avg, tm)
    x1, h2 = _mix_out(x, out_a, out_b, w_out, g2, tm)
    a = _up_proj(h2, w_up_q, tm)
    dx2, sq_err, dg3 = _ffn_fwd_loss(a, w_conv, b_conv_row, w_down, x1, g3, target, tms)

    dconv, y, dw_conv8, db_conv = _ffn_bwd_gate(dx2, a, w_conv, b_conv_row, w_down, tms)
    dact = _conv_bwd(dconv, w_conv, tm, 2 * D_FF // 4)
    dw_down = _matmul_tn(y, dx2, "dw_down", D_FF // 2, D_MODEL, tm, quarters=(2, 1))
    dx1, dg2 = _up_bwd(dact, w_up_q, x1, g2, dx2, tms)
    dw_up_q = _matmul_tn(h2, dact, "dw_up", D_MODEL, 2 * D_FF // 4, tm, quarters=(1, 4))
    dcat = _out_bwd(dx1, w_out, tm)
    dw_out_a = _matmul_tn(out_a, dx1, "dw_out_a", D_HEADS, D_MODEL, tm)
    dw_out_b = _matmul_tn(out_b, dx1, "dw_out_b", D_HEADS, D_MODEL, tm)
    early = {"w_down": dw_down.reshape(4, D_FF // 4, D_MODEL), "w_up": dw_up_q,
             "w_out": jnp.concatenate([dw_out_a, dw_out_b], axis=0).reshape(4, D_MODEL // 4, D_MODEL)}
    early_sums = _chip_sums(early, ids, "early")
    dzu, dzv, dsg_w, dsg_b_t, dln = _gate_bwd(z, dcat, w_mask, w_mask_t, ln_row, b_full, seg_avg, head_ind, tm)
    dop, qb, dopt, qbt = _attn_pack_grad(out_b, dcat, qa, lse, head_ind, consts, tm)
    dqp, dk, dv, dcc, landed = _attn_bwd(qb, qbt, ka, va, dop, dopt, consts, tq, [s16 for _, s16 in early_sums.values()])
    early_parts = {k: (s32, got) for (k, (s32, _)), got in zip(early_sums.items(), landed)}
    dq, dc = _attn_unpack(dqp, dcc, consts, tm)
    df, dbias = _fox_bwd(dc, f, bias_row, _row_tile(S, 256))
    pieces = (dzu, dzv, dq, dk, dv, df)
    dx, dg1 = _in_bwd(pieces, w_in, x, g1, dx1, tms)
    dw_in = _dw_in(h1, pieces, tm)

    grads = {
        "norm_mix_g": dg1, "f_bias": dbias[:, :N_HEADS], "sg_ln_g": dln, "sg_w": dsg_w, "sg_b": dsg_b_t[:, :N_HEADS].T,
        "norm_ffn_g": dg2, "w_conv": dw_conv8[:3], "b_conv": db_conv, "norm_final_g": dg3,
        "w_in": dw_in,
    }
    return sq_err, dx, grads, early_parts


def _chip_sums(grads_q, ids, tag):
    names = list(grads_q)
    theirs = _swap_halves([grads_q[k] for k in names], "swap_halves_" + tag)
    return {k: _pair_sum("pair_sum_" + k, grads_q[k], t, ids) for k, t in zip(names, theirs)}


def _finish_reduction(parts, ids):
    names = list(parts)
    fulls = [_chip_sum("chip_sum_" + k, s32, got, ids) for k, (s32, got) in parts.items()]
    return dict(zip(names, _join_halves(fulls)))


def kernel(x, norm_mix_g, w_in, f_bias, sg_ln_g, sg_w, sg_b, w_out, norm_ffn_g, w_up, w_conv, b_conv, w_down, norm_final_g, loss_target, m_norm_mix_g, m_w_in, m_f_bias, m_sg_ln_g, m_sg_w, m_sg_b, m_w_out, m_norm_ffn_g, m_w_up, m_w_conv, m_b_conv, m_w_down, m_norm_final_g, v_norm_mix_g, v_w_in, v_f_bias, v_sg_ln_g, v_sg_w, v_sg_b, v_w_out, v_norm_ffn_g, v_w_up, v_w_conv, v_b_conv, v_w_down, v_norm_final_g):
    args = dict(locals())
    quarter = 2 * lax.axis_index("x") + lax.axis_index("y")
    ids = jnp.stack([quarter, lax.axis_index("c")]).astype(jnp.int32)
    wq_conv = w_conv.shape[-1]

    g_in = _gather_quarters([w_in[0].astype(BF16)])[0]
    w_in_full = jnp.pad(jnp.concatenate([g_in[q] for q in range(4)], axis=1), ((0, 0), (0, D_IN_PAD - D_IN)))
    late_shards = [w_out[0].astype(BF16), w_up[0].astype(BF16), w_down[0].astype(BF16), w_conv[0]]

    sq_err, dx, grads, early_parts = _local_step(
        x[0], loss_target[0], norm_mix_g, w_in_full, f_bias[0], sg_ln_g[0], sg_w[0], sg_b[0], norm_ffn_g, b_conv[0],
        norm_final_g.reshape(1, D_MODEL), late_shards, ids)
    loss = lax.psum(0.5 * jnp.sum(sq_err) / D_MODEL, ("x", "y", "c"))

    dw_in = grads["w_in"][:, :D_IN].reshape(D_MODEL, 4, D_IN // 4).transpose(1, 0, 2)
    late_sums = _chip_sums({"w_in": dw_in}, ids, "late")
    landed = _scatter_quarters([s16 for _, s16 in late_sums.values()])
    late_parts = {k: (s32, got) for (k, (s32, _)), got in zip(late_sums.items(), landed)}
    big = _finish_reduction({**early_parts, **late_parts}, ids)

    out = {"loss": loss, "grad_x": dx[None]}
    for k in ("w_in", "w_out", "w_up", "w_down"):
        g = big[k]
        d, nm, nv = _adamw("adamw_" + k, args[k][0], g, args["m_" + k][0], args["v_" + k][0])
        out["grad_" + k], out["delta_" + k], out["new_m_" + k], out["new_v_" + k] = g[None], d[None], nm[None], nv[None]

    small_names = [n for n, _ in _SMALL]
    shapes = {n: (3, 4 * wq_conv) if n == "w_conv" else args[n].shape for n in small_names}
    g_small = _unpack_small(_small_allreduce(_pack_small({n: grads[n] for n in small_names})), shapes)
    g_small["w_conv"] = lax.dynamic_slice(g_small["w_conv"], (0, quarter * wq_conv), (3, wq_conv))[None]
    flat2d = lambda t: t.reshape(-1, t.shape[-1])
    updated = _adamw_small(*[[flat2d(src[p + n]) for n in small_names] for src, p in
                             ((args, ""), (g_small, ""), (args, "m_"), (args, "v_"))])
    for n, g in g_small.items():
        out["grad_" + n] = g
    for prefix, arrs in zip(("delta_", "new_m_", "new_v_"), updated):
        for n, t in zip(small_names, arrs):
            out[prefix + n] = t.reshape(args[n].shape)

    weights = ["norm_mix_g", "w_in", "f_bias", "sg_ln_g", "sg_w", "sg_b", "w_out", "norm_ffn_g", "w_up", "w_conv", "b_conv",
               "w_down", "norm_final_g"]
    return (out["loss"], out["grad_x"], *[out[p + n] for p in ("grad_", "delta_", "new_m_", "new_v_") for n in weights])
```

```python
import functools
import math

import jax
import jax.numpy as jnp
from jax import lax
from jax.experimental import pallas as pl
from jax.experimental.pallas import tpu as pltpu

F32 = jnp.float32
BF16 = jnp.bfloat16
MESH = pl.DeviceIdType.MESH

D_MODEL = 1024
N_HEADS = 8
HEAD_DIM = 64
D_HEADS = N_HEADS * HEAD_DIM
SG_BLOCK = 128
CHUNK = 64
D_FF = 2816
D_IN = 2 * D_HEADS + 3 * D_HEADS + N_HEADS
LANES = 128
SUBLANES = 8
D_IN_PAD = 5 * D_HEADS + LANES
EPS = 1e-6
SCALE = HEAD_DIM ** -0.5
NEG = -1e30
LOG2E = 1.4426950408889634
HEAD_PAD = LANES
D_PAD = N_HEADS * HEAD_PAD
Q_STAT = HEAD_DIM
K_STAT = HEAD_DIM + 3
L_STAT = HEAD_DIM + 6
GROUPS = 2
GROUP_HEADS = N_HEADS // GROUPS
GROUP_PAD = GROUP_HEADS * HEAD_PAD
KEY_CHUNK = 256
STAT_ROWS = 16
FF_CHUNK = 256

ADAM_LR = 0.001
ADAM_B1 = 0.9
ADAM_B2 = 0.999
ADAM_EPS = 1e-08
ADAM_WD = 0.01
ADAM_STEP = 10

VMEM_LIMIT = 56 * 1024 * 1024

NT = (((1,), (1,)), ((), ()))
TN = (((0,), (0,)), ((), ()))


def _params(sem):
    return pltpu.CompilerParams(dimension_semantics=sem, vmem_limit_bytes=VMEM_LIMIT)


def _full(shape):
    nd = len(shape)
    return pl.BlockSpec(shape, lambda *_: (0,) * nd)


def _row_tile(rows, target):
    best = None
    for t in range(SUBLANES, min(rows, target) + 1, SUBLANES):
        if rows % t == 0:
            best = t
    assert best is not None, rows
    return best


def _sigmoid(x):
    return 0.5 * jnp.tanh(0.5 * x) + 0.5


def _gelu(z):
    return 0.5 * z * (1.0 + lax.erf(z * (2.0 ** -0.5)))


def _gelu_grad(z):
    cdf = 0.5 * (1.0 + lax.erf(z * (2.0 ** -0.5)))
    pdf = jnp.exp(-0.5 * z * z) * (1.0 / math.sqrt(2.0 * math.pi))
    return cdf + z * pdf


def _split_dot(x, m):
    hi = x.astype(BF16)
    lo = (x - hi.astype(F32)).astype(BF16)
    return jnp.dot(hi, m, preferred_element_type=F32) + jnp.dot(lo, m, preferred_element_type=F32)


def _head_mask(h, rows):
    lane = lax.broadcasted_iota(jnp.int32, (rows, D_HEADS), 1)
    return (lane >= h * HEAD_DIM) & (lane < (h + 1) * HEAD_DIM)


def _rms_bwd(dh, x, g):
    r = lax.rsqrt(jnp.mean(x * x, axis=-1, keepdims=True) + EPS)
    xhat = x * r
    dg = jnp.sum(dh * xhat, axis=0, keepdims=True)
    dxhat = dh * g
    dx = r * (dxhat - xhat * jnp.mean(dxhat * xhat, axis=-1, keepdims=True))
    return dx, dg


def _in_proj(x, g1, w_in, tm):
    S = x.shape[0]
    nz = D_IN_PAD - LANES

    def body(x_ref, g_ref, w_ref, z_ref, f_ref, h_ref):
        xf = x_ref[...]
        r = lax.rsqrt(jnp.mean(xf * xf, axis=-1, keepdims=True) + EPS)
        h = (xf * r * g_ref[...]).astype(BF16)
        h_ref[...] = h
        zz = jnp.dot(h, w_ref[...], preferred_element_type=F32)
        z_ref[...] = zz[:, :nz].astype(BF16)
        f_ref[...] = zz[:, nz:]

    return pl.pallas_call(
        body, name="in_proj", grid=(S // tm,),
        in_specs=[pl.BlockSpec((tm, D_MODEL), lambda i: (i, 0)), _full((1, D_MODEL)), _full((D_MODEL, D_IN_PAD))],
        out_specs=[pl.BlockSpec((tm, nz), lambda i: (i, 0)), pl.BlockSpec((tm, LANES), lambda i: (i, 0)),
                   pl.BlockSpec((tm, D_MODEL), lambda i: (i, 0))],
        out_shape=[jax.ShapeDtypeStruct((S, nz), BF16), jax.ShapeDtypeStruct((S, LANES), F32),
                   jax.ShapeDtypeStruct((S, D_MODEL), BF16)],
        compiler_params=_params(("parallel",)),
    )(x, g1, w_in)


def _fox_prep(f, bias_row, tb):
    S = f.shape[0]

    def body(f_ref, b_ref, c_ref, carry):
        @pl.when(pl.program_id(0) == 0)
        def _():
            carry[...] = jnp.zeros_like(carry)

        xv = f_ref[...] + b_ref[...]
        lf = jnp.minimum(xv, 0.0) - jnp.log(1.0 + jnp.exp(-jnp.abs(xv)))
        r = lax.broadcasted_iota(jnp.int32, (tb, tb), 0)
        s = lax.broadcasted_iota(jnp.int32, (tb, tb), 1)
        tri = (r >= s).astype(F32)
        cs = jnp.dot(tri, lf, precision=lax.Precision.HIGHEST, preferred_element_type=F32) + carry[0:1, :]
        c_ref[...] = cs
        carry[...] = jnp.broadcast_to(cs[tb - 1:tb, :], carry.shape)

    return pl.pallas_call(
        body, name="fox_prep", grid=(S // tb,),
        in_specs=[pl.BlockSpec((tb, LANES), lambda i: (i, 0)), _full((1, LANES))],
        out_specs=pl.BlockSpec((tb, LANES), lambda i: (i, 0)),
        out_shape=jax.ShapeDtypeStruct((S, LANES), F32),
        scratch_shapes=[pltpu.VMEM((SUBLANES, LANES), F32)],
        compiler_params=_params(("arbitrary",)),
    )(f, bias_row)


def _attn_consts():
    col = jnp.arange(D_PAD)
    row = jnp.arange(D_HEADS)
    head = jnp.arange(LANES)
    place = (row[:, None] // HEAD_DIM == col[None, :] // HEAD_PAD) & (row[:, None] % HEAD_DIM == col[None, :] % HEAD_PAD)

    def stat(offset):
        return ((head[:, None] < N_HEADS) & (col[None, :] == head[:, None] * HEAD_PAD + offset)).astype(BF16)

    def ones(offsets):
        return sum((col % HEAD_PAD == o) for o in offsets).astype(F32).reshape(1, D_PAD)

    def pick(offset):
        gcol = jnp.arange(GROUP_PAD)
        return jnp.stack([((gcol[:, None] % HEAD_PAD == offset) & (head[None, :] == g * GROUP_HEADS + gcol[:, None] // HEAD_PAD))
                          for g in range(GROUPS)]).astype(BF16)

    place = place.astype(BF16)
    return {
        "place": place, "place_t": place.T, "place_t_group": place.T[:GROUP_PAD, :GROUP_HEADS * HEAD_DIM],
        "q_stat": jnp.stack([stat(Q_STAT + j) for j in range(3)]), "k_stat": jnp.stack([stat(K_STAT + j) for j in range(3)]),
        "d_stat": jnp.stack([stat(Q_STAT + j) for j in range(2)]), "l_stat": jnp.stack([stat(L_STAT + j)[:STAT_ROWS] for j in range(3)]),
        "q_ones": ones(range(K_STAT, K_STAT + 3)), "k_ones": ones(list(range(Q_STAT, Q_STAT + 3)) + list(range(L_STAT, L_STAT + 3))),
        "v_ones": ones(range(Q_STAT, Q_STAT + 2)),
        "pick_rows": pick(Q_STAT), "pick_cols": pick(K_STAT),
    }


def _split3(x):
    hi = x.astype(BF16)
    r = x - hi.astype(F32)
    mid = r.astype(BF16)
    return hi, mid, (r - mid.astype(F32)).astype(BF16)


def _split3_dot(x, m):
    return sum(jnp.dot(part, m, preferred_element_type=F32) for part in _split3(x))


def _attn_pack(z, c, k, tm):
    S = z.shape[0]

    def body(q_ref, k_ref, v_ref, c_ref, pl_ref, pt_ref, qs_ref, ks_ref, qo_ref, ko_ref, vo_ref, voc_ref,
             qa_ref, ka_ref, va_ref, vt_ref):
        place = pl_ref[...]
        q = (q_ref[...].astype(F32) * (SCALE * LOG2E)).astype(BF16)
        qa = jnp.dot(q, place, preferred_element_type=F32) + qo_ref[...]
        ka = jnp.dot(k_ref[...], place, preferred_element_type=F32) + ko_ref[...]
        for j, part in enumerate(_split3(c_ref[...] * LOG2E)):
            qa = qa + jnp.dot(part, qs_ref[j], preferred_element_type=F32)
            ka = ka - jnp.dot(part, ks_ref[j], preferred_element_type=F32)
        qa_ref[...] = qa.astype(BF16)
        ka_ref[...] = ka.astype(BF16)
        v = v_ref[...]
        va_ref[...] = (jnp.dot(v, place, preferred_element_type=F32) + vo_ref[...]).astype(BF16)
        vt_ref[...] = (lax.dot_general(pt_ref[...], v, NT, preferred_element_type=F32) + voc_ref[...]).astype(BF16)

    blk = lambda col: pl.BlockSpec((tm, D_HEADS), lambda i: (i, col))
    out = pl.BlockSpec((tm, D_PAD), lambda i: (i, 0))
    pad = jax.ShapeDtypeStruct((S, D_PAD), BF16)
    return pl.pallas_call(
        body, name="attn_pack", grid=(S // tm,),
        in_specs=[blk(2), blk(3), blk(4), pl.BlockSpec((tm, LANES), lambda i: (i, 0)), _full((D_HEADS, D_PAD)), _full((D_PAD, D_HEADS)),
                  _full((3, LANES, D_PAD)), _full((3, LANES, D_PAD)), _full((1, D_PAD)), _full((1, D_PAD)), _full((1, D_PAD)),
                  _full((D_PAD, 1))],
        out_specs=[out, out, out, pl.BlockSpec((D_PAD, tm), lambda i: (0, i))],
        out_shape=[pad, pad, pad, jax.ShapeDtypeStruct((D_PAD, S), BF16)],
        compiler_params=_params(("parallel",)),
    )(z, z, z, c, k["place"], k["place_t"], k["q_stat"], k["k_stat"], k["q_ones"], k["k_ones"], k["v_ones"], k["v_ones"].T)


def _attn_fwd(qa, ka, vat, place_t, tq, shards):
    S = qa.shape[0]
    n = S // tq
    ns = len(shards)
    hand_on_at = (2 * n) // 3

    pairs = [(q, k) for q in range(n) for k in range(q + 1)]
    q_of = jnp.asarray([q for q, _ in pairs], jnp.int32)
    k_of = jnp.asarray([k for _, k in pairs], jnp.int32)

    def body(q_of_ref, k_of_ref, q_ref, k_ref, vt_ref, pt_ref, *rest):
        o_ref, lse_ref = rest[ns:ns + 2]
        m_s, acc_s, ot_s, s_s = rest[2 * ns + 2:2 * ns + 6]
        start, hand_on, finish = _gather_ops(rest[:ns], rest[ns + 2:2 * ns + 2], *rest[2 * ns + 6:])
        qi, ki = q_of_ref[pl.program_id(0)], k_of_ref[pl.program_id(0)]

        @pl.when((qi == 0) & (ki == 0))
        def _():
            start()

        @pl.when((qi == hand_on_at) & (ki == 0))
        def _():
            hand_on()

        @pl.when(ki == 0)
        def _():
            m_s[...] = jnp.full_like(m_s, NEG)
            acc_s[...] = jnp.zeros_like(acc_s)

        def step(diagonal):
            chunks = [slice(c * KEY_CHUNK, (c + 1) * KEY_CHUNK) for c in range(tq // KEY_CHUNK)]

            def scores(h, rows, slot):
                sl = slice(h * HEAD_PAD, (h + 1) * HEAD_PAD)
                st = lax.dot_general(k_ref[rows, sl], q_ref[:, sl], NT, preferred_element_type=F32)
                if diagonal:
                    key = rows.start + lax.broadcasted_iota(jnp.int32, (KEY_CHUNK, tq), 0)
                    query = lax.broadcasted_iota(jnp.int32, (KEY_CHUNK, tq), 1)
                    st = jnp.where(query >= key, st, NEG)
                s_s[slot, rows, :] = st
                return jnp.max(st, axis=0, keepdims=True)

            m_cur = functools.reduce(jnp.maximum, [scores(0, rows, 0) for rows in chunks])
            for h in range(N_HEADS):
                sl = slice(h * HEAD_PAD, (h + 1) * HEAD_PAD)
                slot = h % 2
                m_prev = m_s[h][0:1, :]
                m_new = jnp.maximum(m_prev, m_cur)
                acc = jnp.exp2(m_prev - m_new) * acc_s[h]
                m_next = []
                for rows in chunks:
                    if h + 1 < N_HEADS:
                        m_next.append(scores(h + 1, rows, 1 - slot))
                    pt = jnp.exp2(s_s[slot, rows, :] - m_new).astype(BF16)
                    acc = acc + jnp.dot(vt_ref[sl, rows], pt, preferred_element_type=F32)
                acc_s[h] = acc
                m_s[h] = jnp.broadcast_to(m_new, (SUBLANES, tq))
                if m_next:
                    m_cur = functools.reduce(jnp.maximum, m_next)

        @pl.when(ki < qi)
        def _():
            step(False)

        @pl.when(ki == qi)
        def _():
            step(True)
            lse_ref[...] = jnp.zeros_like(lse_ref)
            for h in range(N_HEADS):
                acc = acc_s[h]
                denom = acc[Q_STAT:Q_STAT + 1, :]
                ot_s[h * HEAD_PAD:(h + 1) * HEAD_PAD, :] = (acc / denom).astype(BF16)
                lse_ref[h:h + 1, :] = m_s[h][0:1, :] + jnp.log(denom) * LOG2E
            o_ref[...] = lax.dot_general(ot_s[...], pt_ref[...], TN, preferred_element_type=F32).astype(BF16)

        @pl.when((qi == n - 1) & (ki == n - 1))
        def _():
            finish()

    out = pl.pallas_call(
        body, name="attn_fwd",
        grid_spec=pltpu.PrefetchScalarGridSpec(
            num_scalar_prefetch=2, grid=(len(pairs),),
            in_specs=[pl.BlockSpec((tq, D_PAD), lambda i, qs, ks: (qs[i], 0)), pl.BlockSpec((tq, D_PAD), lambda i, qs, ks: (ks[i], 0)),
                      pl.BlockSpec((D_PAD, tq), lambda i, qs, ks: (0, ks[i])), pl.BlockSpec((D_PAD, D_HEADS), lambda i, qs, ks: (0, 0))]
            + [_ANY] * ns,
            out_specs=[pl.BlockSpec((tq, D_HEADS), lambda i, qs, ks: (qs[i], 0)),
                       pl.BlockSpec((STAT_ROWS, tq), lambda i, qs, ks: (0, qs[i]))] + [_ANY] * ns,
            scratch_shapes=[pltpu.VMEM((N_HEADS, SUBLANES, tq), F32), pltpu.VMEM((N_HEADS, HEAD_PAD, tq), F32),
                            pltpu.VMEM((D_PAD, tq), BF16), pltpu.VMEM((2, tq, tq), F32)] + _gather_sems(ns)),
        out_shape=[jax.ShapeDtypeStruct((S, D_HEADS), BF16), jax.ShapeDtypeStruct((STAT_ROWS, S), F32)] + _gather_shapes(shards),
        compiler_params=_params(("arbitrary",)),
    )(q_of, k_of, qa, ka, vat, place_t, *shards)
    return out[0], out[1], out[2:]


def _layer_norm_heads(v, seg_avg):
    mu = _split_dot(v, seg_avg)
    d = v - mu
    var = _split_dot(d * d, seg_avg)
    rstd = lax.rsqrt(var + EPS)
    return d * rstd, rstd


def _gate_mix(vn_blk, w_ref, bias):
    acc = bias
    for h in range(N_HEADS):
        vh = jnp.where(_head_mask(h, SG_BLOCK), vn_blk, 0.0).astype(BF16)
        acc = acc + jnp.dot(w_ref[h], vh, preferred_element_type=F32)
    return acc


def _gate_fwd(z, w_mask, ln_row, b_full, seg_avg, tm):
    S = z.shape[0]

    def body(zu_ref, zv_ref, w_ref, ln_ref, b_ref, avg_ref, o_ref):
        u = _gelu(zu_ref[...].astype(F32))
        v = _gelu(zv_ref[...].astype(F32))
        vhat, _ = _layer_norm_heads(v, avg_ref[...])
        vn = vhat * ln_ref[...]
        for b in range(tm // SG_BLOCK):
            rows = slice(b * SG_BLOCK, (b + 1) * SG_BLOCK)
            mixed = _gate_mix(vn[rows], w_ref, b_ref[...])
            o_ref[rows, :] = (u[rows] * mixed).astype(BF16)

    return pl.pallas_call(
        body, name="gate_fwd", grid=(S // tm,),
        in_specs=[pl.BlockSpec((tm, D_HEADS), lambda i: (i, 0)), pl.BlockSpec((tm, D_HEADS), lambda i: (i, 1)),
                  _full((N_HEADS, SG_BLOCK, SG_BLOCK)), _full((1, D_HEADS)), _full((SG_BLOCK, D_HEADS)),
                  _full((D_HEADS, D_HEADS))],
        out_specs=pl.BlockSpec((tm, D_HEADS), lambda i: (i, 0)),
        out_shape=jax.ShapeDtypeStruct((S, D_HEADS), BF16),
        compiler_params=_params(("parallel",)),
    )(z, z, w_mask, ln_row, b_full, seg_avg)


def _mix_out(x, out_a, out_b, w_out, g2, tm):
    S = x.shape[0]

    def body(x_ref, a_ref, b_ref, w_ref, g_ref, x1_ref, h_ref):
        y = jnp.dot(a_ref[...], w_ref[:D_HEADS, :], preferred_element_type=F32)
        y = y + jnp.dot(b_ref[...], w_ref[D_HEADS:, :], preferred_element_type=F32)
        x1 = x_ref[...] + y
        x1_ref[...] = x1
        r = lax.rsqrt(jnp.mean(x1 * x1, axis=-1, keepdims=True) + EPS)
        h_ref[...] = (x1 * r * g_ref[...]).astype(BF16)

    row = lambda w: pl.BlockSpec((tm, w), lambda i: (i, 0))
    return pl.pallas_call(
        body, name="mix_out", grid=(S // tm,),
        in_specs=[row(D_MODEL), row(D_HEADS), row(D_HEADS), _full((D_MODEL, D_MODEL)), _full((1, D_MODEL))],
        out_specs=[row(D_MODEL), row(D_MODEL)],
        out_shape=[jax.ShapeDtypeStruct((S, D_MODEL), F32), jax.ShapeDtypeStruct((S, D_MODEL), BF16)],
        compiler_params=_params(("parallel",)),
    )(x, out_a, out_b, w_out, g2)


def _up_proj(h2, w_up_q, tm):
    S = h2.shape[0]
    nq, _, wq = w_up_q.shape

    def body(h_ref, w_ref, a_ref):
        a_ref[...] = jnp.dot(h_ref[...], w_ref[...], preferred_element_type=F32).astype(BF16)

    return pl.pallas_call(
        body, name="up_proj", grid=(nq, S // tm),
        in_specs=[pl.BlockSpec((tm, D_MODEL), lambda j, i: (i, 0)), pl.BlockSpec((None, D_MODEL, wq), lambda j, i: (j, 0, 0))],
        out_specs=pl.BlockSpec((tm, wq), lambda j, i: (i, j)),
        out_shape=jax.ShapeDtypeStruct((S, nq * wq), BF16),
        compiler_params=_params(("parallel", "parallel")),
    )(h2, w_up_q)


def _shift_down(a, halo, k):
    tm = a.shape[0]
    ra = pltpu.roll(a, k, 0)
    rh = pltpu.roll(halo, k, 0)
    row = lax.broadcasted_iota(jnp.int32, halo.shape, 0)
    top = jnp.where(row < k, rh, ra[0:SUBLANES])
    return jnp.concatenate([top, ra[SUBLANES:tm]], axis=0)


def _shift_up(a, halo, k):
    tm = a.shape[0]
    ra = pltpu.roll(a, tm - k, 0)
    rh = pltpu.roll(halo, SUBLANES - k, 0)
    row = lax.broadcasted_iota(jnp.int32, halo.shape, 0)
    bottom = jnp.where(row >= SUBLANES - k, rh, ra[tm - SUBLANES:tm])
    return jnp.concatenate([ra[0:tm - SUBLANES], bottom], axis=0)


def _shift_matrices(tm):
    row = lax.broadcasted_iota(jnp.int32, (tm, tm), 0)
    col = lax.broadcasted_iota(jnp.int32, (tm, tm), 1)
    return [(row == col + k).astype(BF16) for k in (1, 2)]


def _conv_taps(a, halo, first, shifts):
    tm = a.shape[0]
    halo = halo.astype(F32) * jnp.where(first, 0.0, 1.0)
    if shifts is None:
        a = a.astype(F32)
        return a, _shift_down(a, halo, 1), _shift_down(a, halo, 2)
    row8 = lax.broadcasted_iota(jnp.int32, halo.shape, 0)
    taps = [a.astype(F32)]
    for k, shift in zip((1, 2), shifts):
        down = jnp.dot(shift, a, preferred_element_type=F32)
        top = down[0:SUBLANES] + jnp.where(row8 < k, pltpu.roll(halo, k, 0), 0.0)
        taps.append(jnp.concatenate([top, down[SUBLANES:tm]], axis=0))
    return taps


def _conv_gate_val(refs, shifts, cols, first):
    ag_ref, av_ref, hg_ref, hv_ref, wg_ref, wv_ref, bg_ref, bv_ref = refs
    g0, g1, g2 = _conv_taps(ag_ref[:, cols], hg_ref[:, cols], first, shifts)
    gate = wg_ref[2:3, cols] * g0 + wg_ref[1:2, cols] * g1 + wg_ref[0:1, cols] * g2 + bg_ref[:, cols]
    v0, v1, v2 = _conv_taps(av_ref[:, cols], hv_ref[:, cols], first, shifts)
    val = wv_ref[2:3, cols] * v0 + wv_ref[1:2, cols] * v1 + wv_ref[0:1, cols] * v2 + bv_ref[:, cols]
    return gate, val, (g2, g1, g0), (v2, v1, v0)


_FF_CHUNKS = [slice(j * FF_CHUNK, (j + 1) * FF_CHUNK) for j in range(D_FF // FF_CHUNK)]


def _conv_specs(tm):
    step = tm // SUBLANES
    prev = lambda i: jnp.maximum(i * step - 1, 0)
    return [pl.BlockSpec((tm, D_FF), lambda i: (i, 0)), pl.BlockSpec((tm, D_FF), lambda i: (i, 1)),
            pl.BlockSpec((SUBLANES, D_FF), lambda i: (prev(i), 0)), pl.BlockSpec((SUBLANES, D_FF), lambda i: (prev(i), 1))]


def _ffn_fwd_loss(a, w_conv, b_conv, w_down, x1, g3, target, tm):
    S = x1.shape[0]

    def body(ag_ref, av_ref, hg_ref, hv_ref, wg_ref, wv_ref, bg_ref, bv_ref, wd_ref, x1_ref, g_ref, t_ref,
             dx2_ref, loss_ref, dg_ref):
        i = pl.program_id(0)

        @pl.when(i == 0)
        def _():
            loss_ref[...] = jnp.zeros_like(loss_ref)
            dg_ref[...] = jnp.zeros_like(dg_ref)

        x2 = x1_ref[...]
        for cols in _FF_CHUNKS:
            gate, val, _, _ = _conv_gate_val((ag_ref, av_ref, hg_ref, hv_ref, wg_ref, wv_ref, bg_ref, bv_ref), None, cols, i == 0)
            y = (gate * _sigmoid(gate) * val).astype(BF16)
            x2 = x2 + jnp.dot(y, wd_ref[cols, :], preferred_element_type=F32)
        r = lax.rsqrt(jnp.mean(x2 * x2, axis=-1, keepdims=True) + EPS)
        xhat = x2 * r
        gg = g_ref[...]
        err = xhat * gg - t_ref[...]
        loss_ref[...] += jnp.sum(err * err, axis=0, keepdims=True)
        dy = err * (1.0 / D_MODEL)
        dg_ref[...] += jnp.sum(dy * xhat, axis=0, keepdims=True)
        dxhat = dy * gg
        dx2_ref[...] = r * (dxhat - xhat * jnp.mean(dxhat * xhat, axis=-1, keepdims=True))

    row = lambda w: pl.BlockSpec((tm, w), lambda i: (i, 0))
    half = lambda r: [pl.BlockSpec((r, D_FF), lambda i: (0, 0)), pl.BlockSpec((r, D_FF), lambda i: (0, 1))]
    return pl.pallas_call(
        body, name="ffn_fwd_loss", grid=(S // tm,),
        in_specs=_conv_specs(tm) + half(3) + half(1) + [_full((D_FF, D_MODEL)), row(D_MODEL), _full((1, D_MODEL)), row(D_MODEL)],
        out_specs=[row(D_MODEL), _full((1, D_MODEL)), _full((1, D_MODEL))],
        out_shape=[jax.ShapeDtypeStruct((S, D_MODEL), F32), jax.ShapeDtypeStruct((1, D_MODEL), F32),
                   jax.ShapeDtypeStruct((1, D_MODEL), F32)],
        compiler_params=_params(("arbitrary",)),
    )(a, a, a, a, w_conv, w_conv, b_conv, b_conv, w_down, x1, g3, target)


def _ffn_bwd_gate(dx2, a, w_conv, b_conv, w_down, tm):
    S = dx2.shape[0]

    def body(dx_ref, ag_ref, av_ref, hg_ref, hv_ref, wg_ref, wv_ref, bg_ref, bv_ref, wd_ref,
             dc_ref, y_ref, dw_ref, db_ref):
        i = pl.program_id(0)

        @pl.when(i == 0)
        def _():
            dw_ref[...] = jnp.zeros_like(dw_ref)
            db_ref[...] = jnp.zeros_like(db_ref)

        dx = dx_ref[...].astype(BF16)
        shifts = _shift_matrices(tm)
        for cols in _FF_CHUNKS:
            gate, val, gtaps, vtaps = _conv_gate_val((ag_ref, av_ref, hg_ref, hv_ref, wg_ref, wv_ref, bg_ref, bv_ref), shifts, cols, i == 0)
            sg = _sigmoid(gate)
            act = gate * sg
            y_ref[:, cols] = (act * val).astype(BF16)
            dy = lax.dot_general(dx, wd_ref[cols, :], NT, preferred_element_type=F32)
            dgate = dy * val * (sg + act - act * sg)
            dval = dy * act
            for d, taps, out in ((dgate, gtaps, cols), (dval, vtaps, slice(D_FF + cols.start, D_FF + cols.stop))):
                dc_ref[:, out] = d.astype(BF16)
                db_ref[0:1, out] += jnp.sum(d, axis=0, keepdims=True)
                for j in range(3):
                    dw_ref[j:j + 1, out] += jnp.sum(d * taps[j], axis=0, keepdims=True)

    row = lambda w: pl.BlockSpec((tm, w), lambda i: (i, 0))
    half = lambda r: [pl.BlockSpec((r, D_FF), lambda i: (0, 0)), pl.BlockSpec((r, D_FF), lambda i: (0, 1))]
    return pl.pallas_call(
        body, name="ffn_bwd_gate", grid=(S // tm,),
        in_specs=[row(D_MODEL)] + _conv_specs(tm) + half(3) + half(1) + [_full((D_FF, D_MODEL))],
        out_specs=[row(2 * D_FF), row(D_FF), _full((SUBLANES, 2 * D_FF)), _full((1, 2 * D_FF))],
        out_shape=[jax.ShapeDtypeStruct((S, 2 * D_FF), BF16), jax.ShapeDtypeStruct((S, D_FF), BF16),
                   jax.ShapeDtypeStruct((SUBLANES, 2 * D_FF), F32), jax.ShapeDtypeStruct((1, 2 * D_FF), F32)],
        compiler_params=_params(("arbitrary",)),
    )(dx2, a, a, a, a, w_conv, w_conv, b_conv, b_conv, w_down)


def _conv_bwd(dc, w_conv, tm, tn):
    S, C = dc.shape
    step = tm // SUBLANES
    last_blk = S // SUBLANES - 1

    def body(d_ref, nx_ref, w_ref, o_ref):
        last = pl.program_id(0) == pl.num_programs(0) - 1
        d = d_ref[...].astype(F32)
        nx = nx_ref[...].astype(F32) * jnp.where(last, 0.0, 1.0)
        out = w_ref[2:3, :] * d + w_ref[1:2, :] * _shift_up(d, nx, 1) + w_ref[0:1, :] * _shift_up(d, nx, 2)
        o_ref[...] = out.astype(BF16)

    return pl.pallas_call(
        body, name="conv_bwd", grid=(S // tm, C // tn),
        in_specs=[pl.BlockSpec((tm, tn), lambda i, j: (i, j)),
                  pl.BlockSpec((SUBLANES, tn), lambda i, j: (jnp.minimum((i + 1) * step, last_blk), j)),
                  pl.BlockSpec((3, tn), lambda i, j: (0, j))],
        out_specs=pl.BlockSpec((tm, tn), lambda i, j: (i, j)),
        out_shape=jax.ShapeDtypeStruct((S, C), BF16),
        compiler_params=_params(("parallel", "parallel")),
    )(dc, dc, w_conv)


def _matmul_tn(a, b, name, bm, bn, tk, col_a=0, col_b=0, quarters=None):
    S = a.shape[0]
    gm, gn = quarters if quarters else (1, 1)
    nk = S // tk

    def body(a_ref, b_ref, o_ref):
        @pl.when(pl.program_id(2) == 0)
        def _():
            o_ref[...] = jnp.zeros_like(o_ref)

        o_ref[...] += lax.dot_general(a_ref[...].astype(BF16), b_ref[...].astype(BF16), TN, preferred_element_type=F32)

    if quarters and gn > 1:
        out_spec = pl.BlockSpec((None, bm, bn), lambda i, j, k: (j, i, 0))
        out_shape = jax.ShapeDtypeStruct((gn, gm * bm, bn), F32)
    else:
        out_spec = pl.BlockSpec((bm, bn), lambda i, j, k: (i, j))
        out_shape = jax.ShapeDtypeStruct((gm * bm, gn * bn), F32)
    return pl.pallas_call(
        body, name=name, grid=(gm, gn, nk),
        in_specs=[pl.BlockSpec((tk, bm), lambda i, j, k: (k, col_a * gm + i)),
                  pl.BlockSpec((tk, bn), lambda i, j, k: (k, col_b * gn + j))],
        out_specs=out_spec, out_shape=out_shape,
        compiler_params=_params(("parallel", "parallel", "arbitrary")),
    )(a, b)


def _up_bwd(dact, w_up_q, x1, g2, dx2, tm):
    S = x1.shape[0]
    nq, _, wq = w_up_q.shape

    def body(d_ref, w_ref, x_ref, g_ref, dx2_ref, dx1_ref, dg_ref):
        @pl.when(pl.program_id(0) == 0)
        def _():
            dg_ref[...] = jnp.zeros_like(dg_ref)

        dh = jnp.zeros((tm, D_MODEL), F32)
        for j in range(nq):
            dh = dh + lax.dot_general(d_ref[:, j * wq:(j + 1) * wq], w_ref[j], NT, preferred_element_type=F32)
        dx, dg = _rms_bwd(dh, x_ref[...], g_ref[...])
        dg_ref[...] += dg
        dx1_ref[...] = dx2_ref[...] + dx

    row = lambda w: pl.BlockSpec((tm, w), lambda i: (i, 0))
    return pl.pallas_call(
        body, name="up_bwd", grid=(S // tm,),
        in_specs=[row(nq * wq), pl.BlockSpec((nq, D_MODEL, wq), lambda i: (0, 0, 0), pipeline_mode=pl.Buffered(1)),
                  row(D_MODEL), _full((1, D_MODEL)), row(D_MODEL)],
        out_specs=[row(D_MODEL), _full((1, D_MODEL))],
        out_shape=[jax.ShapeDtypeStruct((S, D_MODEL), F32), jax.ShapeDtypeStruct((1, D_MODEL), F32)],
        compiler_params=_params(("arbitrary",)),
    )(dact, w_up_q, x1, g2, dx2)


def _out_bwd(dx1, w_out, tm):
    S = dx1.shape[0]

    def body(d_ref, w_ref, o_ref):
        o_ref[...] = lax.dot_general(d_ref[...].astype(BF16), w_ref[...], NT, preferred_element_type=F32).astype(BF16)

    return pl.pallas_call(
        body, name="out_bwd", grid=(S // tm,),
        in_specs=[pl.BlockSpec((tm, D_MODEL), lambda i: (i, 0)), _full((D_MODEL, D_MODEL))],
        out_specs=pl.BlockSpec((tm, D_MODEL), lambda i: (i, 0)),
        out_shape=jax.ShapeDtypeStruct((S, D_MODEL), BF16),
        compiler_params=_params(("parallel",)),
    )(dx1, w_out)


def _gate_bwd(z, dcat, w_mask, w_mask_t, ln_row, b_full, seg_avg, head_ind, tm):
    S = z.shape[0]
    nb = tm // SG_BLOCK

    def body(zu_ref, zv_ref, do_ref, w_ref, wt_ref, ln_ref, b_ref, avg_ref, ind_ref,
             dzu_ref, dzv_ref, dw_ref, db_ref, dln_ref, dvn_s, dbf_s):
        i = pl.program_id(0)

        @pl.when(i == 0)
        def _():
            dw_ref[...] = jnp.zeros_like(dw_ref)
            dln_ref[...] = jnp.zeros_like(dln_ref)
            dbf_s[...] = jnp.zeros_like(dbf_s)

        zu = zu_ref[...].astype(F32)
        zv = zv_ref[...].astype(F32)
        u = _gelu(zu)
        v = _gelu(zv)
        avg = avg_ref[...]
        vhat, rstd = _layer_norm_heads(v, avg)
        ln = ln_ref[...]
        vn = vhat * ln
        for b in range(nb):
            rows = slice(b * SG_BLOCK, (b + 1) * SG_BLOCK)
            vn_b = vn[rows]
            mixed = _gate_mix(vn_b, w_ref, b_ref[...])
            do = do_ref[rows, :].astype(F32)
            dzu_ref[rows, :] = (do * mixed * _gelu_grad(zu[rows])).astype(BF16)
            dmix = do * u[rows]
            dbf_s[...] += dmix
            vn_bf = vn_b.astype(BF16)
            dvn = jnp.zeros((SG_BLOCK, D_HEADS), F32)
            for h in range(N_HEADS):
                dmh = jnp.where(_head_mask(h, SG_BLOCK), dmix, 0.0).astype(BF16)
                dw_ref[h] += lax.dot_general(dmh, vn_bf, NT, preferred_element_type=F32)
                dvn = dvn + jnp.dot(wt_ref[h], dmh, preferred_element_type=F32)
            dvn_s[rows, :] = dvn
        dvn = dvn_s[...]
        dln_ref[...] += jnp.sum(dvn * vhat, axis=0, keepdims=True)
        dvhat = dvn * ln
        dv = rstd * (dvhat - _split_dot(dvhat, avg) - vhat * _split_dot(dvhat * vhat, avg))
        dzv_ref[...] = (dv * _gelu_grad(zv)).astype(BF16)

        @pl.when(i == pl.num_programs(0) - 1)
        def _():
            r = lax.broadcasted_iota(jnp.int32, (SG_BLOCK, SG_BLOCK), 0) // CHUNK
            s = lax.broadcasted_iota(jnp.int32, (SG_BLOCK, SG_BLOCK), 1) // CHUNK
            for h in range(N_HEADS):
                dw_ref[h] = jnp.where(r >= s, dw_ref[h], 0.0)
            db_ref[...] = _split_dot(dbf_s[...], ind_ref[...])

    row = lambda col: pl.BlockSpec((tm, D_HEADS), lambda i: (i, col))
    wspec = _full((N_HEADS, SG_BLOCK, SG_BLOCK))
    return pl.pallas_call(
        body, name="gate_bwd", grid=(S // tm,),
        in_specs=[row(0), row(1), row(0), wspec, wspec, _full((1, D_HEADS)), _full((SG_BLOCK, D_HEADS)),
                  _full((D_HEADS, D_HEADS)), _full((D_HEADS, LANES))],
        out_specs=[row(0), row(0), wspec, _full((SG_BLOCK, LANES)), _full((1, D_HEADS))],
        out_shape=[jax.ShapeDtypeStruct((S, D_HEADS), BF16), jax.ShapeDtypeStruct((S, D_HEADS), BF16),
                   jax.ShapeDtypeStruct((N_HEADS, SG_BLOCK, SG_BLOCK), F32), jax.ShapeDtypeStruct((SG_BLOCK, LANES), F32),
                   jax.ShapeDtypeStruct((1, D_HEADS), F32)],
        scratch_shapes=[pltpu.VMEM((tm, D_HEADS), F32), pltpu.VMEM((SG_BLOCK, D_HEADS), F32)],
        compiler_params=_params(("arbitrary",)),
    )(z, z, dcat, w_mask, w_mask_t, ln_row, b_full, seg_avg, head_ind)


def _attn_pack_grad(o, dcat, qa, lse, head_ind, k, tm):
    S = o.shape[0]

    def body(o_ref, do_ref, qa_ref, lse_ref, ind_ref, pl_ref, pt_ref, eye_ref, ds_ref, dst_ref, ls_ref, lst_ref,
             dop_ref, qb_ref, dot_ref, qbt_ref):
        do = do_ref[...]
        delta = _split_dot(o_ref[...].astype(F32) * do.astype(F32), ind_ref[...])
        hi = delta.astype(BF16)
        lo = (delta - hi.astype(F32)).astype(BF16)
        dop = jnp.dot(do, pl_ref[...], preferred_element_type=F32)
        dop = dop - jnp.dot(hi, ds_ref[0], preferred_element_type=F32) - jnp.dot(lo, ds_ref[1], preferred_element_type=F32)
        dop_ref[...] = dop.astype(BF16)
        dot = lax.dot_general(pt_ref[...], do, NT, preferred_element_type=F32)
        dot = dot - lax.dot_general(dst_ref[0], hi, NT, preferred_element_type=F32)
        dot = dot - lax.dot_general(dst_ref[1], lo, NT, preferred_element_type=F32)
        dot_ref[...] = dot.astype(BF16)
        qa = qa_ref[...]
        qb = qa.astype(F32)
        qbt = lax.dot_general(eye_ref[...], qa, NT, preferred_element_type=F32)
        for j, part in enumerate(_split3(lse_ref[...])):
            qb = qb - lax.dot_general(part, ls_ref[j], TN, preferred_element_type=F32)
            qbt = qbt - jnp.dot(lst_ref[j], part, preferred_element_type=F32)
        qb_ref[...] = qb.astype(BF16)
        qbt_ref[...] = qbt.astype(BF16)

    pad = pl.BlockSpec((tm, D_PAD), lambda i: (i, 0))
    padt = pl.BlockSpec((D_PAD, tm), lambda i: (0, i))
    return pl.pallas_call(
        body, name="attn_pack_grad", grid=(S // tm,),
        in_specs=[pl.BlockSpec((tm, D_HEADS), lambda i: (i, 0)), pl.BlockSpec((tm, D_HEADS), lambda i: (i, 1)), pad,
                  pl.BlockSpec((STAT_ROWS, tm), lambda i: (0, i)), _full((D_HEADS, LANES)), _full((D_HEADS, D_PAD)),
                  _full((D_PAD, D_HEADS)), _full((D_PAD, D_PAD)), _full((2, LANES, D_PAD)), _full((2, D_PAD, LANES)),
                  _full((3, STAT_ROWS, D_PAD)), _full((3, D_PAD, STAT_ROWS))],
        out_specs=[pad, pad, padt, padt],
        out_shape=[jax.ShapeDtypeStruct((S, D_PAD), BF16)] * 2 + [jax.ShapeDtypeStruct((D_PAD, S), BF16)] * 2,
        compiler_params=_params(("parallel",)),
    )(o, dcat, qa, lse, head_ind, k["place"], k["place_t"], jnp.eye(D_PAD, dtype=BF16), k["d_stat"],
      jnp.swapaxes(k["d_stat"], 1, 2), k["l_stat"], jnp.swapaxes(k["l_stat"], 1, 2))


def _attn_bwd(qb, qbt, ka, va, dop, dopt, k, tq, sums16):
    S = qb.shape[0]
    n = S // tq
    ns = len(sums16)

    pairs = [(kb, q) for kb in range(n) for q in range(kb, n)]
    k_of = jnp.asarray([kb for kb, _ in pairs], jnp.int32)
    q_of = jnp.asarray([q for _, q in pairs], jnp.int32)

    def body(k_of_ref, q_of_ref, q_ref, qt_ref, k_ref, v_ref, do_ref, dot_ref, pt_ref, pick_ref, *rest):
        dq_hbm, dk_ref, dv_ref, dcc_ref = rest[ns:ns + 4]
        dq_s, dk_s, dv_s, s_s, d_s, sem = rest[2 * ns + 4:2 * ns + 10]
        scatter_start, scatter_finish = _scatter_ops(rest[:ns], rest[ns + 4:2 * ns + 4], *rest[2 * ns + 10:])
        g = pl.program_id(0)
        ki, qi = k_of_ref[pl.program_id(1)], q_of_ref[pl.program_id(1)]

        @pl.when((g == 0) & (ki == 0) & (qi == 0))
        def _():
            scatter_start()

        @pl.when((ki == 0) & (qi == 0))
        def _():
            dq_s[...] = jnp.zeros_like(dq_s)

        @pl.when(qi == ki)
        def _():
            dk_s[...] = jnp.zeros_like(dk_s)
            dv_s[...] = jnp.zeros_like(dv_s)

        def step(diagonal):
            chunks = [slice(c * KEY_CHUNK, (c + 1) * KEY_CHUNK) for c in range(tq // KEY_CHUNK)]

            def scores(hh, rows, slot):
                sl = slice(hh * HEAD_PAD, (hh + 1) * HEAD_PAD)
                s_s[slot, rows, :] = lax.dot_general(q_ref[rows, sl], k_ref[:, sl], NT, preferred_element_type=F32)
                d_s[slot, rows, :] = lax.dot_general(do_ref[rows, sl], v_ref[:, sl], NT, preferred_element_type=F32)

            for rows in chunks:
                scores(0, rows, 0)
            for hh in range(GROUP_HEADS):
                sl = slice(hh * HEAD_PAD, (hh + 1) * HEAD_PAD)
                slot = hh % 2
                dv, dk = dv_s[sl, :], dk_s[sl, :]
                for rows in chunks:
                    if hh + 1 < GROUP_HEADS:
                        scores(hh + 1, rows, 1 - slot)
                    p = jnp.exp2(s_s[slot, rows, :])
                    if diagonal:
                        row = rows.start + lax.broadcasted_iota(jnp.int32, (KEY_CHUNK, tq), 0)
                        col = lax.broadcasted_iota(jnp.int32, (KEY_CHUNK, tq), 1)
                        p = jnp.where(row >= col, p, 0.0)
                    ds = (p * d_s[slot, rows, :]).astype(BF16)
                    dv = dv + jnp.dot(dot_ref[sl, rows], p.astype(BF16), preferred_element_type=F32)
                    dk = dk + jnp.dot(qt_ref[sl, rows], ds, preferred_element_type=F32)
                    qrows = pl.ds(pl.multiple_of(qi * tq + rows.start, KEY_CHUNK), KEY_CHUNK)
                    dq_s[qrows, sl] += jnp.dot(ds, k_ref[:, sl], preferred_element_type=F32)
                dv_s[sl, :] = dv
                dk_s[sl, :] = dk

        @pl.when(qi > ki)
        def _():
            step(False)

        @pl.when(qi == ki)
        def _():
            step(True)

        @pl.when(qi == n - 1)
        def _():
            dk = dk_s[...]
            pt = pt_ref[...]
            dk_ref[...] = lax.dot_general((dk * (1.0 / LOG2E)).astype(BF16), pt, TN, preferred_element_type=F32).astype(BF16)
            dv_ref[...] = lax.dot_general(dv_s[...].astype(BF16), pt, TN, preferred_element_type=F32).astype(BF16)
            dcc_ref[...] = sum(lax.dot_general(part, pick_ref[...], TN, preferred_element_type=F32) for part in _split3(dk))

        @pl.when((ki == n - 1) & (qi == n - 1))
        def _():
            cp = pltpu.make_async_copy(dq_s, dq_hbm.at[g], sem)
            cp.start()
            cp.wait()

        @pl.when((g == GROUPS - 1) & (ki == n - 1) & (qi == n - 1))
        def _():
            scatter_finish()

    gw = GROUP_HEADS * HEAD_DIM
    qspec = pl.BlockSpec((tq, GROUP_PAD), lambda g, i, ks, qs: (qs[i], g))
    qtspec = pl.BlockSpec((GROUP_PAD, tq), lambda g, i, ks, qs: (g, qs[i]))
    kspec = pl.BlockSpec((tq, GROUP_PAD), lambda g, i, ks, qs: (ks[i], g))
    kout = pl.BlockSpec((tq, gw), lambda g, i, ks, qs: (ks[i], g))
    out = pl.pallas_call(
        body, name="attn_bwd",
        grid_spec=pltpu.PrefetchScalarGridSpec(
            num_scalar_prefetch=2, grid=(GROUPS, len(pairs)),
            in_specs=[qspec, qtspec, kspec, kspec, qspec, qtspec, pl.BlockSpec((GROUP_PAD, gw), lambda g, i, ks, qs: (0, 0)),
                      pl.BlockSpec((None, GROUP_PAD, LANES), lambda g, i, ks, qs: (g, 0, 0))] + [_ANY] * ns,
            out_specs=[_ANY, kout, kout, pl.BlockSpec((None, tq, LANES), lambda g, i, ks, qs: (g, ks[i], 0))] + [_ANY] * ns,
            scratch_shapes=[pltpu.VMEM((S, GROUP_PAD), F32), pltpu.VMEM((GROUP_PAD, tq), F32), pltpu.VMEM((GROUP_PAD, tq), F32),
                            pltpu.VMEM((2, tq, tq), F32), pltpu.VMEM((2, tq, tq), F32), pltpu.SemaphoreType.DMA]
            + _scatter_sems(ns)),
        out_shape=[jax.ShapeDtypeStruct((GROUPS, S, GROUP_PAD), F32), jax.ShapeDtypeStruct((S, D_HEADS), BF16),
                   jax.ShapeDtypeStruct((S, D_HEADS), BF16), jax.ShapeDtypeStruct((GROUPS, S, LANES), F32)]
        + _scatter_shapes(sums16),
        compiler_params=_params(("arbitrary", "arbitrary")),
    )(k_of, q_of, qb, qbt, ka, va, dop, dopt, k["place_t_group"], k["pick_cols"], *sums16)
    return out[0], out[1], out[2], out[3], out[4:]


def _attn_unpack(dqp, dcc, k, tm):
    S = dqp.shape[1]
    gw = GROUP_HEADS * HEAD_DIM

    def body(dqp_ref, dcc_ref, pt_ref, pick_ref, dq_ref, dc_ref):
        dc = jnp.zeros((tm, LANES), F32)
        for g in range(GROUPS):
            x = dqp_ref[g]
            dq_ref[:, g * gw:(g + 1) * gw] = jnp.dot((x * SCALE).astype(BF16), pt_ref[...], preferred_element_type=F32).astype(BF16)
            dc = dc + _split3_dot(x, pick_ref[g]) - dcc_ref[g]
        dc_ref[...] = dc

    return pl.pallas_call(
        body, name="attn_unpack", grid=(S // tm,),
        in_specs=[pl.BlockSpec((GROUPS, tm, GROUP_PAD), lambda i: (0, i, 0)), pl.BlockSpec((GROUPS, tm, LANES), lambda i: (0, i, 0)),
                  _full((GROUP_PAD, gw)), _full((GROUPS, GROUP_PAD, LANES))],
        out_specs=[pl.BlockSpec((tm, D_HEADS), lambda i: (i, 0)), pl.BlockSpec((tm, LANES), lambda i: (i, 0))],
        out_shape=[jax.ShapeDtypeStruct((S, D_HEADS), BF16), jax.ShapeDtypeStruct((S, LANES), F32)],
        compiler_params=_params(("parallel",)),
    )(dqp, dcc, k["place_t_group"], k["pick_rows"])


def _fox_bwd(dc, f, bias_row, tb):
    S = f.shape[0]
    nb = S // tb

    def body(dc_ref, f_ref, b_ref, df_ref, dbias_ref, carry):
        @pl.when(pl.program_id(0) == 0)
        def _():
            carry[...] = jnp.zeros_like(carry)
            dbias_ref[...] = jnp.zeros_like(dbias_ref)

        r = lax.broadcasted_iota(jnp.int32, (tb, tb), 0)
        s = lax.broadcasted_iota(jnp.int32, (tb, tb), 1)
        tri = (s >= r).astype(F32)
        rc = jnp.dot(tri, dc_ref[...], precision=lax.Precision.HIGHEST, preferred_element_type=F32) + carry[0:1, :]
        carry[...] = jnp.broadcast_to(rc[0:1, :], carry.shape)
        lane = lax.broadcasted_iota(jnp.int32, (tb, LANES), 1)
        df = jnp.where(lane < N_HEADS, rc * jax.nn.sigmoid(-(f_ref[...] + b_ref[...])), 0.0)
        df_ref[...] = df.astype(BF16)
        dbias_ref[...] += jnp.sum(df, axis=0, keepdims=True)

    rev = pl.BlockSpec((tb, LANES), lambda i: (nb - 1 - i, 0))
    return pl.pallas_call(
        body, name="fox_bwd", grid=(nb,),
        in_specs=[rev, rev, _full((1, LANES))],
        out_specs=[rev, _full((1, LANES))],
        out_shape=[jax.ShapeDtypeStruct((S, LANES), BF16), jax.ShapeDtypeStruct((1, LANES), F32)],
        scratch_shapes=[pltpu.VMEM((SUBLANES, LANES), F32)],
        compiler_params=_params(("arbitrary",)),
    )(dc, f, bias_row)


_DZ_WIDTHS = (D_HEADS,) * 5 + (LANES,)


def _in_bwd(pieces, w_in, x, g1, dx1, tm):
    S = x.shape[0]

    def body(*refs):
        p_refs, (w_ref, x_ref, g_ref, dx1_ref, dx_ref, dg_ref) = refs[:6], refs[6:]

        @pl.when(pl.program_id(0) == 0)
        def _():
            dg_ref[...] = jnp.zeros_like(dg_ref)

        dh = jnp.zeros((tm, D_MODEL), F32)
        off = 0
        for p_ref, w in zip(p_refs, _DZ_WIDTHS):
            dh = dh + lax.dot_general(p_ref[...].astype(BF16), w_ref[:, off:off + w], NT, preferred_element_type=F32)
            off += w
        dx, dg = _rms_bwd(dh, x_ref[...], g_ref[...])
        dg_ref[...] += dg
        dx_ref[...] = dx1_ref[...] + dx

    row = lambda w: pl.BlockSpec((tm, w), lambda i: (i, 0))
    return pl.pallas_call(
        body, name="in_bwd", grid=(S // tm,),
        in_specs=[row(w) for w in _DZ_WIDTHS] + [_full((D_MODEL, D_IN_PAD)), row(D_MODEL), _full((1, D_MODEL)), row(D_MODEL)],
        out_specs=[row(D_MODEL), _full((1, D_MODEL))],
        out_shape=[jax.ShapeDtypeStruct((S, D_MODEL), F32), jax.ShapeDtypeStruct((1, D_MODEL), F32)],
        compiler_params=_params(("arbitrary",)),
    )(*pieces, w_in, x, g1, dx1)


def _dw_in(h1, pieces, tk):
    S = h1.shape[0]

    def body(*refs):
        h_ref, p_refs, o_ref = refs[0], refs[1:7], refs[7]

        @pl.when(pl.program_id(0) == 0)
        def _():
            o_ref[...] = jnp.zeros_like(o_ref)

        off = 0
        for p_ref, w in zip(p_refs, _DZ_WIDTHS):
            o_ref[:, off:off + w] += lax.dot_general(h_ref[...], p_ref[...].astype(BF16), TN, preferred_element_type=F32)
            off += w

    row = lambda w: pl.BlockSpec((tk, w), lambda k: (k, 0))
    return pl.pallas_call(
        body, name="dw_in", grid=(S // tk,),
        in_specs=[row(D_MODEL)] + [row(w) for w in _DZ_WIDTHS],
        out_specs=_full((D_MODEL, D_IN_PAD)),
        out_shape=jax.ShapeDtypeStruct((D_MODEL, D_IN_PAD), F32),
        compiler_params=_params(("arbitrary",)),
    )(h1, *pieces)


def _adamw_math(w, g, m, v):
    m = ADAM_B1 * m + (1.0 - ADAM_B1) * g
    v = ADAM_B2 * v + (1.0 - ADAM_B2) * (g * g)
    m_hat = m / (1.0 - ADAM_B1 ** ADAM_STEP)
    v_hat = v / (1.0 - ADAM_B2 ** ADAM_STEP)
    delta = -ADAM_LR * (m_hat / (jnp.sqrt(v_hat) + ADAM_EPS) + ADAM_WD * w)
    return delta, m, v


def _adamw(name, w, g, m, v):
    R, C = w.shape
    tr = _row_tile(R, 256)

    def body(w_ref, g_ref, m_ref, v_ref, go_ref, d_ref, nm_ref, nv_ref):
        g = g_ref[...]
        d, nm, nv = _adamw_math(w_ref[...], g, m_ref[...], v_ref[...])
        go_ref[...] = g
        d_ref[...] = d
        nm_ref[...] = nm
        nv_ref[...] = nv

    spec = pl.BlockSpec((tr, C), lambda i: (i, 0))
    return pl.pallas_call(
        body, name=name, grid=(R // tr,), in_specs=[spec] * 4, out_specs=[spec] * 4,
        out_shape=[jax.ShapeDtypeStruct((R, C), F32)] * 4,
        compiler_params=_params(("parallel",)),
    )(w, g, m, v)


def _pair_sum(name, grad, theirs, ids):
    q, half, C = theirs.shape
    tr = _row_tile(half, 256)
    nb = half // tr

    def body(ids_ref, a_ref, b_ref, s_ref, sb_ref):
        s = a_ref[...] + b_ref[...]
        s_ref[...] = s
        sb_ref[...] = s.astype(BF16)

    here = pl.BlockSpec((None, tr, C), lambda j, i, ids: (j, i, 0))
    return pl.pallas_call(
        body, name=name,
        grid_spec=pltpu.PrefetchScalarGridSpec(
            num_scalar_prefetch=1, grid=(q, nb),
            in_specs=[pl.BlockSpec((None, tr, C), lambda j, i, ids: (j, ids[1] * nb + i, 0)), here],
            out_specs=[here, here]),
        out_shape=[jax.ShapeDtypeStruct((q, half, C), F32), jax.ShapeDtypeStruct((q, half, C), BF16)],
        compiler_params=_params(("parallel", "parallel")),
    )(ids, grad, theirs)


def _chip_sum(name, sums32, others, ids):
    _, half, C = sums32.shape
    tr = _row_tile(half, 256)
    nb = half // tr

    def body(ids_ref, a_ref, o_ref, s_ref):
        s = a_ref[...]
        for j in range(3):
            s = s + o_ref[j].astype(F32)
        s_ref[...] = s

    return pl.pallas_call(
        body, name=name,
        grid_spec=pltpu.PrefetchScalarGridSpec(
            num_scalar_prefetch=1, grid=(nb,),
            in_specs=[pl.BlockSpec((None, tr, C), lambda i, ids: (ids[0], i, 0)),
                      pl.BlockSpec((3, tr, C), lambda i, ids: (0, i, 0))],
            out_specs=pl.BlockSpec((tr, C), lambda i, ids: (ids[1] * nb + i, 0))),
        out_shape=jax.ShapeDtypeStruct((2 * half, C), F32),
        compiler_params=_params(("parallel",)),
    )(ids, sums32, others)


def _place():
    return lax.axis_index("x"), lax.axis_index("y"), lax.axis_index("c")


def _other_chips(x, y):
    return [(1 - x, y), (x, 1 - y), (1 - x, 1 - y)]


_ANY = pl.BlockSpec(memory_space=pl.ANY)


def _gather_quarters(shards):
    n = len(shards)

    def body(*refs):
        start, hand_on, finish = _gather_ops(refs[:n], refs[n:2 * n], *refs[2 * n:])
        start()
        hand_on()
        finish()

    return pl.pallas_call(
        body, name="gather_weights",
        in_specs=[_ANY] * n, out_specs=[_ANY] * n,
        out_shape=_gather_shapes(shards), scratch_shapes=_gather_sems(n),
    )(*shards)


def _gather_shapes(shards):
    return [jax.ShapeDtypeStruct((4,) + s.shape, s.dtype) for s in shards]


def _gather_sems(n):
    return [pltpu.SemaphoreType.DMA((n, 3))] * 4 + [pltpu.SemaphoreType.DMA((n,))]


def _gather_ops(ins, outs, send_sems, recv_sems, pass_send_sems, pass_recv_sems, own_sems):
    n = len(ins)
    halved = [r.shape[0] % 32 == 0 for r in ins]

    def part(a, quarter, core):
        if not halved[a]:
            return outs[a].at[quarter]
        half = ins[a].shape[0] // 2
        return outs[a].at[quarter, pl.ds(core * half, half), :]

    def ici(a, j, quarter):
        x, y, c = _place()
        px, py = _other_chips(x, y)[j]
        src = ins[a]
        if halved[a]:
            half = src.shape[0] // 2
            src = src.at[pl.ds(c * half, half), :]
        return pltpu.make_async_remote_copy(src_ref=src, dst_ref=part(a, quarter, c), send_sem=send_sems.at[a, j],
                                            recv_sem=recv_sems.at[a, j], device_id=(px, py, c), device_id_type=MESH)

    def passed(a, j, core):
        x, y, c = _place()
        px, py = _other_chips(x, y)[j]
        half = part(a, 2 * px + py, core)
        return pltpu.make_async_remote_copy(src_ref=half, dst_ref=half, send_sem=pass_send_sems.at[a, j],
                                            recv_sem=pass_recv_sems.at[a, j], device_id=(x, y, 1 - c), device_id_type=MESH)

    def own(a):
        x, y, _ = _place()
        return pltpu.make_async_copy(ins[a], outs[a].at[2 * x + y], own_sems.at[a])

    def start():
        x, y, _ = _place()
        for a in range(n):
            for j in range(3):
                ici(a, j, 2 * x + y).start()
            own(a).start()

    def hand_on():
        x, y, c = _place()
        for a in range(n):
            for j, (px, py) in enumerate(_other_chips(x, y)):
                ici(a, j, 2 * px + py).wait_recv()
                if halved[a]:
                    passed(a, j, c).start()

    def finish():
        x, y, c = _place()
        for a in range(n):
            for j in range(3):
                if halved[a]:
                    passed(a, j, 1 - c).wait_recv()
                    passed(a, j, c).wait_send()
                ici(a, j, 2 * x + y).wait_send()
            own(a).wait()

    return start, hand_on, finish


def _swap_halves(grads, name):
    n = len(grads)

    def body(*refs):
        ins, outs = refs[:n], refs[n:2 * n]
        send_sems, recv_sems = refs[2 * n:]
        x, y, c = _place()
        started = []
        for a in range(n):
            half = ins[a].shape[1] // 2
            cp = pltpu.make_async_remote_copy(src_ref=ins[a].at[:, pl.ds((1 - c) * half, half), :], dst_ref=outs[a],
                                              send_sem=send_sems.at[a], recv_sem=recv_sems.at[a],
                                              device_id=(x, y, 1 - c), device_id_type=MESH)
            cp.start()
            started.append(cp)
        for cp in started:
            cp.wait()

    return pl.pallas_call(
        body, name=name,
        in_specs=[_ANY] * n, out_specs=[_ANY] * n,
        out_shape=[jax.ShapeDtypeStruct((4, g.shape[1] // 2, g.shape[2]), F32) for g in grads],
        scratch_shapes=[pltpu.SemaphoreType.DMA((n,)), pltpu.SemaphoreType.DMA((n,))],
    )(*grads)


def _scatter_quarters(sums16):
    n = len(sums16)

    def body(*refs):
        start, finish = _scatter_ops(refs[:n], refs[n:2 * n], *refs[2 * n:])
        start()
        finish()

    return pl.pallas_call(
        body, name="scatter_quarters",
        in_specs=[_ANY] * n, out_specs=[_ANY] * n,
        out_shape=_scatter_shapes(sums16), scratch_shapes=_scatter_sems(n),
    )(*sums16)


def _scatter_shapes(sums16):
    return [jax.ShapeDtypeStruct((3,) + s.shape[1:], BF16) for s in sums16]


def _scatter_sems(n):
    return [pltpu.SemaphoreType.DMA((n, 3))] * 2


def _scatter_ops(ins, outs, send_sems, recv_sems):
    n = len(ins)

    def copy(a, j):
        x, y, c = _place()
        px, py = _other_chips(x, y)[j]
        return pltpu.make_async_remote_copy(src_ref=ins[a].at[2 * px + py], dst_ref=outs[a].at[j], send_sem=send_sems.at[a, j],
                                            recv_sem=recv_sems.at[a, j], device_id=(px, py, c), device_id_type=MESH)

    def start():
        for a in range(n):
            for j in range(3):
                copy(a, j).start()

    def finish():
        for a in range(n):
            for j in range(3):
                copy(a, j).wait()

    return start, finish


def _join_halves(fulls):
    n = len(fulls)

    def body(*refs):
        ins, outs = refs[:n], refs[n:2 * n]
        send_sems, recv_sems = refs[2 * n:]
        x, y, c = _place()
        started = []
        for a in range(n):
            half = ins[a].shape[0] // 2
            rows = pl.ds(c * half, half)
            cp = pltpu.make_async_remote_copy(src_ref=ins[a].at[rows, :], dst_ref=outs[a].at[rows, :], send_sem=send_sems.at[a],
                                              recv_sem=recv_sems.at[a], device_id=(x, y, 1 - c), device_id_type=MESH)
            cp.start()
            started.append(cp)
        for cp in started:
            cp.wait()

    return pl.pallas_call(
        body, name="join_halves",
        in_specs=[_ANY] * n, out_specs=[_ANY] * n,
        out_shape=[jax.ShapeDtypeStruct(f.shape, F32) for f in fulls],
        input_output_aliases={a: a for a in range(n)},
        scratch_shapes=[pltpu.SemaphoreType.DMA((n,)), pltpu.SemaphoreType.DMA((n,))],
    )(*fulls)


def _small_allreduce(g):
    R = g.shape[0]
    half = R // 2

    def body(g_ref, out_ref, other_s, chip_s, parts_s, send_sems, recv_sems):
        x, y, c = _place()
        mine = 2 * x + y
        rows = pl.ds(pl.multiple_of(c * half, SUBLANES), half)

        def to_other_core(src, dst, k):
            return pltpu.make_async_remote_copy(src_ref=src, dst_ref=dst, send_sem=send_sems.at[k], recv_sem=recv_sems.at[k],
                                                device_id=(x, y, 1 - c), device_id_type=MESH)

        swap = to_other_core(g_ref, other_s, 0)
        swap.start()
        swap.wait()
        chip_s[...] = g_ref[...] + other_s[...]
        parts_s[mine] = chip_s[rows, :]
        sends = []
        for j, (px, py) in enumerate(_other_chips(x, y)):
            cp = pltpu.make_async_remote_copy(src_ref=chip_s.at[rows, :], dst_ref=parts_s.at[mine], send_sem=send_sems.at[1 + j],
                                              recv_sem=recv_sems.at[1 + j], device_id=(px, py, c), device_id_type=MESH)
            cp.start()
            sends.append(cp)
        for cp in sends:
            cp.wait()
        out_ref[rows, :] = (parts_s[0] + parts_s[1]) + (parts_s[2] + parts_s[3])
        join = to_other_core(out_ref.at[rows, :], out_ref.at[rows, :], 4)
        join.start()
        join.wait()

    vm = pl.BlockSpec(memory_space=pltpu.VMEM)
    return pl.pallas_call(
        body, name="small_allreduce",
        in_specs=[vm], out_specs=vm, out_shape=jax.ShapeDtypeStruct((R, LANES), F32),
        scratch_shapes=[pltpu.VMEM((R, LANES), F32), pltpu.VMEM((R, LANES), F32), pltpu.VMEM((4, half, LANES), F32),
                        pltpu.SemaphoreType.DMA((5,)), pltpu.SemaphoreType.DMA((5,))],
        compiler_params=pltpu.CompilerParams(vmem_limit_bytes=VMEM_LIMIT),
    )(g)


def _adamw_small(ws, gs, ms, vs):
    n = len(ws)

    def body(*refs):
        for k in range(n):
            w_ref, g_ref, m_ref, v_ref = (refs[j * n + k] for j in range(4))
            d, nm, nv = _adamw_math(w_ref[...], g_ref[...], m_ref[...], v_ref[...])
            refs[4 * n + k][...] = d
            refs[5 * n + k][...] = nm
            refs[6 * n + k][...] = nv

    vm = pl.BlockSpec(memory_space=pltpu.VMEM)
    out = pl.pallas_call(
        body, name="adamw_small",
        in_specs=[vm] * (4 * n), out_specs=[vm] * (3 * n),
        out_shape=[jax.ShapeDtypeStruct(w.shape, F32) for w in ws] * 3,
        compiler_params=pltpu.CompilerParams(vmem_limit_bytes=VMEM_LIMIT),
    )(*ws, *gs, *ms, *vs)
    return out[:n], out[n:2 * n], out[2 * n:]


_SMALL = (("norm_mix_g", D_MODEL), ("f_bias", N_HEADS), ("sg_ln_g", D_HEADS), ("sg_w", N_HEADS * SG_BLOCK * SG_BLOCK),
          ("sg_b", N_HEADS * SG_BLOCK), ("norm_ffn_g", D_MODEL), ("w_conv", 3 * 2 * D_FF), ("b_conv", 2 * D_FF),
          ("norm_final_g", D_MODEL))


def _pack_small(parts):
    rows = []
    for name, size in _SMALL:
        flat = parts[name].reshape(-1).astype(F32)
        pad = (-size) % (SUBLANES * LANES)
        rows.append(jnp.pad(flat, (0, pad)).reshape(-1, LANES))
    packed = jnp.concatenate(rows, axis=0)
    return jnp.pad(packed, ((0, (-packed.shape[0]) % (2 * SUBLANES)), (0, 0)))


def _unpack_small(packed, shapes):
    out, r = {}, 0
    for name, size in _SMALL:
        nrows = (size + SUBLANES * LANES - 1) // (SUBLANES * LANES) * SUBLANES
        out[name] = packed[r:r + nrows].reshape(-1)[:size].reshape(shapes[name])
        r += nrows
    return out


def _local_step(x, target, g1, w_in, f_bias, sg_ln_g, sg_w, sg_b, g2, b_conv, g3, late_shards, ids):
    S = x.shape[0]
    tm = _row_tile(S, 512)
    tms = _row_tile(S, 256)
    tq = _row_tile(S, 512)

    lane = jnp.arange(D_HEADS)
    seg_avg = jnp.where(lane[:, None] // HEAD_DIM == lane[None, :] // HEAD_DIM, 1.0 / HEAD_DIM, 0.0).astype(BF16)
    head_ind = (lane[:, None] // HEAD_DIM == jnp.arange(LANES)[None, :]).astype(BF16)
    pos_chunk = jnp.arange(SG_BLOCK) // CHUNK
    w_mask32 = jnp.where(pos_chunk[:, None] >= pos_chunk[None, :], sg_w, 0.0)
    w_mask = w_mask32.astype(BF16)
    w_mask_t = jnp.swapaxes(w_mask32, 1, 2).astype(BF16)
    ln_row = sg_ln_g.reshape(1, D_HEADS)
    b_full = jnp.repeat(sg_b.T, HEAD_DIM, axis=1)
    bias_row = jnp.pad(f_bias.reshape(1, N_HEADS), ((0, 0), (0, LANES - N_HEADS)))
    b_conv_row = b_conv.reshape(1, 2 * D_FF)

    z, f, h1 = _in_proj(x, g1, w_in, tm)
    c = _fox_prep(f, bias_row, _row_tile(S, 256))
    consts = _attn_consts()
    qa, ka, va, vat = _attn_pack(z, c, consts, tm)
    out_b, lse, gathered = _attn_fwd(qa, ka, vat, consts["place_t"], tq, late_shards)
    g_out, w_up_q, g_down, g_conv = gathered
    w_out = g_out.reshape(D_MODEL, D_MODEL)
    w_down = g_down.reshape(D_FF, D_MODEL)
    w_conv = jnp.concatenate([g_conv[q] for q in range(4)], axis=1)
    out_a = _gate_fwd(z, w_mask, ln_row, b_full, seg_avg, tm)
    x1, h2 = _mix_out(x, out_a, out_b, w_out, g2, tm)
    a = _up_proj(h2, w_up_q, tm)
    dx2, sq_err, dg3 = _ffn_fwd_loss(a, w_conv, b_conv_row, w_down, x1, g3, target, tms)

    dconv, y, dw_conv8, db_conv = _ffn_bwd_gate(dx2, a, w_conv, b_conv_row, w_down, tms)
    dact = _conv_bwd(dconv, w_conv, tm, 2 * D_FF // 4)
    dw_down = _matmul_tn(y, dx2, "dw_down", D_FF // 2, D_MODEL, tm, quarters=(2, 1))
    dx1, dg2 = _up_bwd(dact, w_up_q, x1, g2, dx2, tms)
    dw_up_q = _matmul_tn(h2, dact, "dw_up", D_MODEL, 2 * D_FF // 4, tm, quarters=(1, 4))
    dcat = _out_bwd(dx1, w_out, tm)
    dw_out_a = _matmul_tn(out_a, dx1, "dw_out_a", D_HEADS, D_MODEL, tm)
    dw_out_b = _matmul_tn(out_b, dx1, "dw_out_b", D_HEADS, D_MODEL, tm)
    early = {"w_down": dw_down.reshape(4, D_FF // 4, D_MODEL), "w_up": dw_up_q,
             "w_out": jnp.concatenate([dw_out_a, dw_out_b], axis=0).reshape(4, D_MODEL // 4, D_MODEL)}
    early_sums = _chip_sums(early, ids, "early")
    dzu, dzv, dsg_w, dsg_b_t, dln = _gate_bwd(z, dcat, w_mask, w_mask_t, ln_row, b_full, seg_avg, head_ind, tm)
    dop, qb, dopt, qbt = _attn_pack_grad(out_b, dcat, qa, lse, head_ind, consts, tm)
    dqp, dk, dv, dcc, landed = _attn_bwd(qb, qbt, ka, va, dop, dopt, consts, tq, [s16 for _, s16 in early_sums.values()])
    early_parts = {k: (s32, got) for (k, (s32, _)), got in zip(early_sums.items(), landed)}
    dq, dc = _attn_unpack(dqp, dcc, consts, tm)
    df, dbias = _fox_bwd(dc, f, bias_row, _row_tile(S, 256))
    pieces = (dzu, dzv, dq, dk, dv, df)
    dx, dg1 = _in_bwd(pieces, w_in, x, g1, dx1, tms)
    dw_in = _dw_in(h1, pieces, tm)

    grads = {
        "norm_mix_g": dg1, "f_bias": dbias[:, :N_HEADS], "sg_ln_g": dln, "sg_w": dsg_w, "sg_b": dsg_b_t[:, :N_HEADS].T,
        "norm_ffn_g": dg2, "w_conv": dw_conv8[:3], "b_conv": db_conv, "norm_final_g": dg3,
        "w_in": dw_in,
    }
    return sq_err, dx, grads, early_parts


def _chip_sums(grads_q, ids, tag):
    names = list(grads_q)
    theirs = _swap_halves([grads_q[k] for k in names], "swap_halves_" + tag)
    return {k: _pair_sum("pair_sum_" + k, grads_q[k], t, ids) for k, t in zip(names, theirs)}


def _finish_reduction(parts, ids):
    names = list(parts)
    fulls = [_chip_sum("chip_sum_" + k, s32, got, ids) for k, (s32, got) in parts.items()]
    return dict(zip(names, _join_halves(fulls)))


def kernel(x, norm_mix_g, w_in, f_bias, sg_ln_g, sg_w, sg_b, w_out, norm_ffn_g, w_up, w_conv, b_conv, w_down, norm_final_g, loss_target, m_norm_mix_g, m_w_in, m_f_bias, m_sg_ln_g, m_sg_w, m_sg_b, m_w_out, m_norm_ffn_g, m_w_up, m_w_conv, m_b_conv, m_w_down, m_norm_final_g, v_norm_mix_g, v_w_in, v_f_bias, v_sg_ln_g, v_sg_w, v_sg_b, v_w_out, v_norm_ffn_g, v_w_up, v_w_conv, v_b_conv, v_w_down, v_norm_final_g):
    args = dict(locals())
    quarter = 2 * lax.axis_index("x") + lax.axis_index("y")
    ids = jnp.stack([quarter, lax.axis_index("c")]).astype(jnp.int32)
    wq_conv = w_conv.shape[-1]

    g_in = _gather_quarters([w_in[0].astype(BF16)])[0]
    w_in_full = jnp.pad(jnp.concatenate([g_in[q] for q in range(4)], axis=1), ((0, 0), (0, D_IN_PAD - D_IN)))
    late_shards = [w_out[0].astype(BF16), w_up[0].astype(BF16), w_down[0].astype(BF16), w_conv[0]]

    sq_err, dx, grads, early_parts = _local_step(
        x[0], loss_target[0], norm_mix_g, w_in_full, f_bias[0], sg_ln_g[0], sg_w[0], sg_b[0], norm_ffn_g, b_conv[0],
        norm_final_g.reshape(1, D_MODEL), late_shards, ids)
    loss = lax.psum(0.5 * jnp.sum(sq_err) / D_MODEL, ("x", "y", "c"))

    dw_in = grads["w_in"][:, :D_IN].reshape(D_MODEL, 4, D_IN // 4).transpose(1, 0, 2)
    late_sums = _chip_sums({"w_in": dw_in}, ids, "late")
    landed = _scatter_quarters([s16 for _, s16 in late_sums.values()])
    late_parts = {k: (s32, got) for (k, (s32, _)), got in zip(late_sums.items(), landed)}
    big = _finish_reduction({**early_parts, **late_parts}, ids)

    out = {"loss": loss, "grad_x": dx[None]}
    for k in ("w_in", "w_out", "w_up", "w_down"):
        g, d, nm, nv = _adamw("adamw_" + k, args[k][0], big[k], args["m_" + k][0], args["v_" + k][0])
        out["grad_" + k], out["delta_" + k], out["new_m_" + k], out["new_v_" + k] = g[None], d[None], nm[None], nv[None]

    small_names = [n for n, _ in _SMALL]
    shapes = {n: (3, 4 * wq_conv) if n == "w_conv" else args[n].shape for n in small_names}
    g_small = _unpack_small(_small_allreduce(_pack_small({n: grads[n] for n in small_names})), shapes)
    g_small["w_conv"] = lax.dynamic_slice(g_small["w_conv"], (0, quarter * wq_conv), (3, wq_conv))[None]
    flat2d = lambda t: t.reshape(-1, t.shape[-1])
    updated = _adamw_small(*[[flat2d(src[p + n]) for n in small_names] for src, p in
                             ((args, ""), (g_small, ""), (args, "m_"), (args, "v_"))])
    for n, g in g_small.items():
        out["grad_" + n] = g
    for prefix, arrs in zip(("delta_", "new_m_", "new_v_"), updated):
        for n, t in zip(small_names, arrs):
            out[prefix + n] = t.reshape(args[n].shape)

    weights = ["norm_mix_g", "w_in", "f_bias", "sg_ln_g", "sg_w", "sg_b", "w_out", "norm_ffn_g", "w_up", "w_conv", "b_conv",
               "w_down", "norm_final_g"]
    return (out["loss"], out["grad_x"], *[out[p + n] for p in ("grad_", "delta_", "new_m_", "new_v_") for n in weights])
```

```python
import functools
import math

import jax
import jax.numpy as jnp
from jax import lax
from jax.experimental import pallas as pl
from jax.experimental.pallas import tpu as pltpu

F32 = jnp.float32
BF16 = jnp.bfloat16
MESH = pl.DeviceIdType.MESH

D_MODEL = 1024
N_HEADS = 8
HEAD_DIM = 64
D_HEADS = N_HEADS * HEAD_DIM
SG_BLOCK = 128
CHUNK = 64
D_FF = 2816
D_IN = 2 * D_HEADS + 3 * D_HEADS + N_HEADS
LANES = 128
SUBLANES = 8
D_IN_PAD = 5 * D_HEADS + LANES
EPS = 1e-6
SCALE = HEAD_DIM ** -0.5
NEG = -1e30
LOG2E = 1.4426950408889634
HEAD_PAD = LANES
D_PAD = N_HEADS * HEAD_PAD
Q_STAT = HEAD_DIM
K_STAT = HEAD_DIM + 3
L_STAT = HEAD_DIM + 6
GROUPS = 2
GROUP_HEADS = N_HEADS // GROUPS
GROUP_PAD = GROUP_HEADS * HEAD_PAD
KEY_CHUNK = 256
STAT_ROWS = 16
FF_CHUNK = 256

ADAM_LR = 0.001
ADAM_B1 = 0.9
ADAM_B2 = 0.999
ADAM_EPS = 1e-08
ADAM_WD = 0.01
ADAM_STEP = 10

VMEM_LIMIT = 56 * 1024 * 1024

NT = (((1,), (1,)), ((), ()))
TN = (((0,), (0,)), ((), ()))


def _params(sem):
    return pltpu.CompilerParams(dimension_semantics=sem, vmem_limit_bytes=VMEM_LIMIT)


def _full(shape):
    nd = len(shape)
    return pl.BlockSpec(shape, lambda *_: (0,) * nd)


def _row_tile(rows, target):
    best = None
    for t in range(SUBLANES, min(rows, target) + 1, SUBLANES):
        if rows % t == 0:
            best = t
    assert best is not None, rows
    return best


def _sigmoid(x):
    return 0.5 * jnp.tanh(0.5 * x) + 0.5


def _gelu(z):
    return 0.5 * z * (1.0 + lax.erf(z * (2.0 ** -0.5)))


def _gelu_grad(z):
    cdf = 0.5 * (1.0 + lax.erf(z * (2.0 ** -0.5)))
    pdf = jnp.exp(-0.5 * z * z) * (1.0 / math.sqrt(2.0 * math.pi))
    return cdf + z * pdf


def _split_dot(x, m):
    hi = x.astype(BF16)
    lo = (x - hi.astype(F32)).astype(BF16)
    return jnp.dot(hi, m, preferred_element_type=F32) + jnp.dot(lo, m, preferred_element_type=F32)


def _head_mask(h, rows):
    lane = lax.broadcasted_iota(jnp.int32, (rows, D_HEADS), 1)
    return (lane >= h * HEAD_DIM) & (lane < (h + 1) * HEAD_DIM)


def _rms_bwd(dh, x, g):
    r = lax.rsqrt(jnp.mean(x * x, axis=-1, keepdims=True) + EPS)
    xhat = x * r
    dg = jnp.sum(dh * xhat, axis=0, keepdims=True)
    dxhat = dh * g
    dx = r * (dxhat - xhat * jnp.mean(dxhat * xhat, axis=-1, keepdims=True))
    return dx, dg


def _in_proj(x, g1, w_in, tm):
    S = x.shape[0]
    nz = D_IN_PAD - LANES

    def body(x_ref, g_ref, w_ref, z_ref, f_ref, h_ref):
        xf = x_ref[...]
        r = lax.rsqrt(jnp.mean(xf * xf, axis=-1, keepdims=True) + EPS)
        h = (xf * r * g_ref[...]).astype(BF16)
        h_ref[...] = h
        zz = jnp.dot(h, w_ref[...], preferred_element_type=F32)
        z_ref[...] = zz[:, :nz].astype(BF16)
        f_ref[...] = zz[:, nz:]

    return pl.pallas_call(
        body, name="in_proj", grid=(S // tm,),
        in_specs=[pl.BlockSpec((tm, D_MODEL), lambda i: (i, 0)), _full((1, D_MODEL)), _full((D_MODEL, D_IN_PAD))],
        out_specs=[pl.BlockSpec((tm, nz), lambda i: (i, 0)), pl.BlockSpec((tm, LANES), lambda i: (i, 0)),
                   pl.BlockSpec((tm, D_MODEL), lambda i: (i, 0))],
        out_shape=[jax.ShapeDtypeStruct((S, nz), BF16), jax.ShapeDtypeStruct((S, LANES), F32),
                   jax.ShapeDtypeStruct((S, D_MODEL), BF16)],
        compiler_params=_params(("parallel",)),
    )(x, g1, w_in)


def _fox_prep(f, bias_row, tb):
    S = f.shape[0]

    def body(f_ref, b_ref, c_ref, carry):
        @pl.when(pl.program_id(0) == 0)
        def _():
            carry[...] = jnp.zeros_like(carry)

        xv = f_ref[...] + b_ref[...]
        lf = jnp.minimum(xv, 0.0) - jnp.log(1.0 + jnp.exp(-jnp.abs(xv)))
        r = lax.broadcasted_iota(jnp.int32, (tb, tb), 0)
        s = lax.broadcasted_iota(jnp.int32, (tb, tb), 1)
        tri = (r >= s).astype(F32)
        cs = jnp.dot(tri, lf, precision=lax.Precision.HIGHEST, preferred_element_type=F32) + carry[0:1, :]
        c_ref[...] = cs
        carry[...] = jnp.broadcast_to(cs[tb - 1:tb, :], carry.shape)

    return pl.pallas_call(
        body, name="fox_prep", grid=(S // tb,),
        in_specs=[pl.BlockSpec((tb, LANES), lambda i: (i, 0)), _full((1, LANES))],
        out_specs=pl.BlockSpec((tb, LANES), lambda i: (i, 0)),
        out_shape=jax.ShapeDtypeStruct((S, LANES), F32),
        scratch_shapes=[pltpu.VMEM((SUBLANES, LANES), F32)],
        compiler_params=_params(("arbitrary",)),
    )(f, bias_row)


def _attn_consts():
    col = jnp.arange(D_PAD)
    row = jnp.arange(D_HEADS)
    head = jnp.arange(LANES)
    place = (row[:, None] // HEAD_DIM == col[None, :] // HEAD_PAD) & (row[:, None] % HEAD_DIM == col[None, :] % HEAD_PAD)

    def stat(offset):
        return ((head[:, None] < N_HEADS) & (col[None, :] == head[:, None] * HEAD_PAD + offset)).astype(BF16)

    def ones(offsets):
        return sum((col % HEAD_PAD == o) for o in offsets).astype(F32).reshape(1, D_PAD)

    def pick(offset):
        gcol = jnp.arange(GROUP_PAD)
        return jnp.stack([((gcol[:, None] % HEAD_PAD == offset) & (head[None, :] == g * GROUP_HEADS + gcol[:, None] // HEAD_PAD))
                          for g in range(GROUPS)]).astype(BF16)

    place = place.astype(BF16)
    return {
        "place": place, "place_t": place.T, "place_t_group": place.T[:GROUP_PAD, :GROUP_HEADS * HEAD_DIM],
        "q_stat": jnp.stack([stat(Q_STAT + j) for j in range(3)]), "k_stat": jnp.stack([stat(K_STAT + j) for j in range(3)]),
        "d_stat": jnp.stack([stat(Q_STAT + j) for j in range(2)]), "l_stat": jnp.stack([stat(L_STAT + j)[:STAT_ROWS] for j in range(3)]),
        "q_ones": ones(range(K_STAT, K_STAT + 3)), "k_ones": ones(list(range(Q_STAT, Q_STAT + 3)) + list(range(L_STAT, L_STAT + 3))),
        "v_ones": ones(range(Q_STAT, Q_STAT + 2)),
        "pick_rows": pick(Q_STAT), "pick_cols": pick(K_STAT),
    }


def _split3(x):
    hi = x.astype(BF16)
    r = x - hi.astype(F32)
    mid = r.astype(BF16)
    return hi, mid, (r - mid.astype(F32)).astype(BF16)


def _split3_dot(x, m):
    return sum(jnp.dot(part, m, preferred_element_type=F32) for part in _split3(x))


def _attn_pack(z, c, k, tm):
    S = z.shape[0]

    def body(q_ref, k_ref, v_ref, c_ref, pl_ref, pt_ref, qs_ref, ks_ref, qo_ref, ko_ref, vo_ref, voc_ref,
             qa_ref, ka_ref, va_ref, vt_ref):
        place = pl_ref[...]
        q = (q_ref[...].astype(F32) * (SCALE * LOG2E)).astype(BF16)
        qa = jnp.dot(q, place, preferred_element_type=F32) + qo_ref[...]
        ka = jnp.dot(k_ref[...], place, preferred_element_type=F32) + ko_ref[...]
        for j, part in enumerate(_split3(c_ref[...] * LOG2E)):
            qa = qa + jnp.dot(part, qs_ref[j], preferred_element_type=F32)
            ka = ka - jnp.dot(part, ks_ref[j], preferred_element_type=F32)
        qa_ref[...] = qa.astype(BF16)
        ka_ref[...] = ka.astype(BF16)
        v = v_ref[...]
        va_ref[...] = (jnp.dot(v, place, preferred_element_type=F32) + vo_ref[...]).astype(BF16)
        vt_ref[...] = (lax.dot_general(pt_ref[...], v, NT, preferred_element_type=F32) + voc_ref[...]).astype(BF16)

    blk = lambda col: pl.BlockSpec((tm, D_HEADS), lambda i: (i, col))
    out = pl.BlockSpec((tm, D_PAD), lambda i: (i, 0))
    pad = jax.ShapeDtypeStruct((S, D_PAD), BF16)
    return pl.pallas_call(
        body, name="attn_pack", grid=(S // tm,),
        in_specs=[blk(2), blk(3), blk(4), pl.BlockSpec((tm, LANES), lambda i: (i, 0)), _full((D_HEADS, D_PAD)), _full((D_PAD, D_HEADS)),
                  _full((3, LANES, D_PAD)), _full((3, LANES, D_PAD)), _full((1, D_PAD)), _full((1, D_PAD)), _full((1, D_PAD)),
                  _full((D_PAD, 1))],
        out_specs=[out, out, out, pl.BlockSpec((None, D_PAD, tm), lambda i: (i, 0, 0))],
        out_shape=[pad, pad, pad, jax.ShapeDtypeStruct((S // tm, D_PAD, tm), BF16)],
        compiler_params=_params(("parallel",)),
    )(z, z, z, c, k["place"], k["place_t"], k["q_stat"], k["k_stat"], k["q_ones"], k["k_ones"], k["v_ones"], k["v_ones"].T)


def _attn_fwd(qa, ka, vat, place_t, tq, shards):
    S = qa.shape[0]
    n = S // tq
    ns = len(shards)
    hand_on_at = (2 * n) // 3

    pairs = [(q, k) for q in range(n) for k in range(q + 1)]
    q_of = jnp.asarray([q for q, _ in pairs], jnp.int32)
    k_of = jnp.asarray([k for _, k in pairs], jnp.int32)

    def body(q_of_ref, k_of_ref, q_ref, k_ref, vt_ref, pt_ref, *rest):
        o_ref, lse_ref = rest[ns:ns + 2]
        m_s, acc_s, ot_s, s_s = rest[2 * ns + 2:2 * ns + 6]
        start, hand_on, finish = _gather_ops(rest[:ns], rest[ns + 2:2 * ns + 2], *rest[2 * ns + 6:])
        qi, ki = q_of_ref[pl.program_id(0)], k_of_ref[pl.program_id(0)]

        @pl.when((qi == 0) & (ki == 0))
        def _():
            start()

        @pl.when((qi == hand_on_at) & (ki == 0))
        def _():
            hand_on()

        @pl.when(ki == 0)
        def _():
            m_s[...] = jnp.full_like(m_s, NEG)
            acc_s[...] = jnp.zeros_like(acc_s)

        def step(diagonal):
            chunks = [slice(c * KEY_CHUNK, (c + 1) * KEY_CHUNK) for c in range(tq // KEY_CHUNK)]

            def scores(h, rows, slot):
                sl = slice(h * HEAD_PAD, (h + 1) * HEAD_PAD)
                st = lax.dot_general(k_ref[rows, sl], q_ref[:, sl], NT, preferred_element_type=F32)
                if diagonal:
                    key = rows.start + lax.broadcasted_iota(jnp.int32, (KEY_CHUNK, tq), 0)
                    query = lax.broadcasted_iota(jnp.int32, (KEY_CHUNK, tq), 1)
                    st = jnp.where(query >= key, st, NEG)
                s_s[slot, rows, :] = st
                return jnp.max(st, axis=0, keepdims=True)

            m_cur = functools.reduce(jnp.maximum, [scores(0, rows, 0) for rows in chunks])
            for h in range(N_HEADS):
                sl = slice(h * HEAD_PAD, (h + 1) * HEAD_PAD)
                slot = h % 2
                m_prev = m_s[h][0:1, :]
                m_new = jnp.maximum(m_prev, m_cur)
                acc = jnp.exp2(m_prev - m_new) * acc_s[h]
                m_next = []
                for rows in chunks:
                    if h + 1 < N_HEADS:
                        m_next.append(scores(h + 1, rows, 1 - slot))
                    pt = jnp.exp2(s_s[slot, rows, :] - m_new).astype(BF16)
                    acc = acc + jnp.dot(vt_ref[sl, rows], pt, preferred_element_type=F32)
                acc_s[h] = acc
                m_s[h] = jnp.broadcast_to(m_new, (SUBLANES, tq))
                if m_next:
                    m_cur = functools.reduce(jnp.maximum, m_next)

        @pl.when(ki < qi)
        def _():
            step(False)

        @pl.when(ki == qi)
        def _():
            step(True)
            lse_ref[...] = jnp.zeros_like(lse_ref)
            for h in range(N_HEADS):
                acc = acc_s[h]
                denom = acc[Q_STAT:Q_STAT + 1, :]
                ot_s[h * HEAD_PAD:(h + 1) * HEAD_PAD, :] = (acc / denom).astype(BF16)
                lse_ref[h:h + 1, :] = m_s[h][0:1, :] + jnp.log(denom) * LOG2E
            o_ref[...] = lax.dot_general(ot_s[...], pt_ref[...], TN, preferred_element_type=F32).astype(BF16)

        @pl.when((qi == n - 1) & (ki == n - 1))
        def _():
            finish()

    out = pl.pallas_call(
        body, name="attn_fwd",
        grid_spec=pltpu.PrefetchScalarGridSpec(
            num_scalar_prefetch=2, grid=(len(pairs),),
            in_specs=[pl.BlockSpec((tq, D_PAD), lambda i, qs, ks: (qs[i], 0)), pl.BlockSpec((tq, D_PAD), lambda i, qs, ks: (ks[i], 0)),
                      pl.BlockSpec((None, D_PAD, tq), lambda i, qs, ks: (ks[i], 0, 0)),
                      pl.BlockSpec((D_PAD, D_HEADS), lambda i, qs, ks: (0, 0))]
            + [_ANY] * ns,
            out_specs=[pl.BlockSpec((tq, D_HEADS), lambda i, qs, ks: (qs[i], 0)),
                       pl.BlockSpec((STAT_ROWS, tq), lambda i, qs, ks: (0, qs[i]))] + [_ANY] * ns,
            scratch_shapes=[pltpu.VMEM((N_HEADS, SUBLANES, tq), F32), pltpu.VMEM((N_HEADS, HEAD_PAD, tq), F32),
                            pltpu.VMEM((D_PAD, tq), BF16), pltpu.VMEM((2, tq, tq), F32)] + _gather_sems(ns)),
        out_shape=[jax.ShapeDtypeStruct((S, D_HEADS), BF16), jax.ShapeDtypeStruct((STAT_ROWS, S), F32)] + _gather_shapes(shards),
        compiler_params=_params(("arbitrary",)),
    )(q_of, k_of, qa, ka, vat, place_t, *shards)
    return out[0], out[1], out[2:]


def _layer_norm_heads(v, seg_avg):
    mu = _split_dot(v, seg_avg)
    d = v - mu
    var = _split_dot(d * d, seg_avg)
    rstd = lax.rsqrt(var + EPS)
    return d * rstd, rstd


def _gate_mix(vn_blk, w_ref, bias):
    acc = bias
    for h in range(N_HEADS):
        vh = jnp.where(_head_mask(h, SG_BLOCK), vn_blk, 0.0).astype(BF16)
        acc = acc + jnp.dot(w_ref[h], vh, preferred_element_type=F32)
    return acc


def _gate_fwd(z, w_mask, ln_row, b_full, seg_avg, tm):
    S = z.shape[0]

    def body(zu_ref, zv_ref, w_ref, ln_ref, b_ref, avg_ref, o_ref):
        u = _gelu(zu_ref[...].astype(F32))
        v = _gelu(zv_ref[...].astype(F32))
        vhat, _ = _layer_norm_heads(v, avg_ref[...])
        vn = vhat * ln_ref[...]
        for b in range(tm // SG_BLOCK):
            rows = slice(b * SG_BLOCK, (b + 1) * SG_BLOCK)
            mixed = _gate_mix(vn[rows], w_ref, b_ref[...])
            o_ref[rows, :] = (u[rows] * mixed).astype(BF16)

    return pl.pallas_call(
        body, name="gate_fwd", grid=(S // tm,),
        in_specs=[pl.BlockSpec((tm, D_HEADS), lambda i: (i, 0)), pl.BlockSpec((tm, D_HEADS), lambda i: (i, 1)),
                  _full((N_HEADS, SG_BLOCK, SG_BLOCK)), _full((1, D_HEADS)), _full((SG_BLOCK, D_HEADS)),
                  _full((D_HEADS, D_HEADS))],
        out_specs=pl.BlockSpec((tm, D_HEADS), lambda i: (i, 0)),
        out_shape=jax.ShapeDtypeStruct((S, D_HEADS), BF16),
        compiler_params=_params(("parallel",)),
    )(z, z, w_mask, ln_row, b_full, seg_avg)


def _mix_out(x, out_a, out_b, w_out, g2, tm):
    S = x.shape[0]

    def body(x_ref, a_ref, b_ref, w_ref, g_ref, x1_ref, h_ref):
        y = jnp.dot(a_ref[...], w_ref[:D_HEADS, :], preferred_element_type=F32)
        y = y + jnp.dot(b_ref[...], w_ref[D_HEADS:, :], preferred_element_type=F32)
        x1 = x_ref[...] + y
        x1_ref[...] = x1
        r = lax.rsqrt(jnp.mean(x1 * x1, axis=-1, keepdims=True) + EPS)
        h_ref[...] = (x1 * r * g_ref[...]).astype(BF16)

    row = lambda w: pl.BlockSpec((tm, w), lambda i: (i, 0))
    return pl.pallas_call(
        body, name="mix_out", grid=(S // tm,),
        in_specs=[row(D_MODEL), row(D_HEADS), row(D_HEADS), _full((D_MODEL, D_MODEL)), _full((1, D_MODEL))],
        out_specs=[row(D_MODEL), row(D_MODEL)],
        out_shape=[jax.ShapeDtypeStruct((S, D_MODEL), F32), jax.ShapeDtypeStruct((S, D_MODEL), BF16)],
        compiler_params=_params(("parallel",)),
    )(x, out_a, out_b, w_out, g2)


def _up_proj(h2, w_up_q, tm):
    S = h2.shape[0]
    nq, _, wq = w_up_q.shape

    def body(h_ref, w_ref, a_ref):
        a_ref[...] = jnp.dot(h_ref[...], w_ref[...], preferred_element_type=F32).astype(BF16)

    return pl.pallas_call(
        body, name="up_proj", grid=(nq, S // tm),
        in_specs=[pl.BlockSpec((tm, D_MODEL), lambda j, i: (i, 0)), pl.BlockSpec((None, D_MODEL, wq), lambda j, i: (j, 0, 0))],
        out_specs=pl.BlockSpec((tm, wq), lambda j, i: (i, j)),
        out_shape=jax.ShapeDtypeStruct((S, nq * wq), BF16),
        compiler_params=_params(("parallel", "parallel")),
    )(h2, w_up_q)


def _shift_down(a, halo, k):
    tm = a.shape[0]
    ra = pltpu.roll(a, k, 0)
    rh = pltpu.roll(halo, k, 0)
    row = lax.broadcasted_iota(jnp.int32, halo.shape, 0)
    top = jnp.where(row < k, rh, ra[0:SUBLANES])
    return jnp.concatenate([top, ra[SUBLANES:tm]], axis=0)


def _shift_up(a, halo, k):
    tm = a.shape[0]
    ra = pltpu.roll(a, tm - k, 0)
    rh = pltpu.roll(halo, SUBLANES - k, 0)
    row = lax.broadcasted_iota(jnp.int32, halo.shape, 0)
    bottom = jnp.where(row >= SUBLANES - k, rh, ra[tm - SUBLANES:tm])
    return jnp.concatenate([ra[0:tm - SUBLANES], bottom], axis=0)


def _shift_matrices(tm):
    row = lax.broadcasted_iota(jnp.int32, (tm, tm), 0)
    col = lax.broadcasted_iota(jnp.int32, (tm, tm), 1)
    return [(row == col + k).astype(BF16) for k in (1, 2)]


def _conv_taps(a, halo, first, shifts):
    tm = a.shape[0]
    halo = halo.astype(F32) * jnp.where(first, 0.0, 1.0)
    if shifts is None:
        a = a.astype(F32)
        return a, _shift_down(a, halo, 1), _shift_down(a, halo, 2)
    row8 = lax.broadcasted_iota(jnp.int32, halo.shape, 0)
    taps = [a.astype(F32)]
    for k, shift in zip((1, 2), shifts):
        down = jnp.dot(shift, a, preferred_element_type=F32)
        top = down[0:SUBLANES] + jnp.where(row8 < k, pltpu.roll(halo, k, 0), 0.0)
        taps.append(jnp.concatenate([top, down[SUBLANES:tm]], axis=0))
    return taps


def _conv_gate_val(refs, shifts, cols, first):
    ag_ref, av_ref, hg_ref, hv_ref, wg_ref, wv_ref, bg_ref, bv_ref = refs
    g0, g1, g2 = _conv_taps(ag_ref[:, cols], hg_ref[:, cols], first, shifts)
    gate = wg_ref[2:3, cols] * g0 + wg_ref[1:2, cols] * g1 + wg_ref[0:1, cols] * g2 + bg_ref[:, cols]
    v0, v1, v2 = _conv_taps(av_ref[:, cols], hv_ref[:, cols], first, shifts)
    val = wv_ref[2:3, cols] * v0 + wv_ref[1:2, cols] * v1 + wv_ref[0:1, cols] * v2 + bv_ref[:, cols]
    return gate, val, (g2, g1, g0), (v2, v1, v0)


_FF_CHUNKS = [slice(j * FF_CHUNK, (j + 1) * FF_CHUNK) for j in range(D_FF // FF_CHUNK)]


def _conv_specs(tm):
    step = tm // SUBLANES
    prev = lambda i: jnp.maximum(i * step - 1, 0)
    return [pl.BlockSpec((tm, D_FF), lambda i: (i, 0)), pl.BlockSpec((tm, D_FF), lambda i: (i, 1)),
            pl.BlockSpec((SUBLANES, D_FF), lambda i: (prev(i), 0)), pl.BlockSpec((SUBLANES, D_FF), lambda i: (prev(i), 1))]


def _ffn_fwd_loss(a, w_conv, b_conv, w_down, x1, g3, target, tm):
    S = x1.shape[0]

    def body(ag_ref, av_ref, hg_ref, hv_ref, wg_ref, wv_ref, bg_ref, bv_ref, wd_ref, x1_ref, g_ref, t_ref,
             dx2_ref, loss_ref, dg_ref):
        i = pl.program_id(0)

        @pl.when(i == 0)
        def _():
            loss_ref[...] = jnp.zeros_like(loss_ref)
            dg_ref[...] = jnp.zeros_like(dg_ref)

        x2 = x1_ref[...]
        for cols in _FF_CHUNKS:
            gate, val, _, _ = _conv_gate_val((ag_ref, av_ref, hg_ref, hv_ref, wg_ref, wv_ref, bg_ref, bv_ref), None, cols, i == 0)
            y = (gate * _sigmoid(gate) * val).astype(BF16)
            x2 = x2 + jnp.dot(y, wd_ref[cols, :], preferred_element_type=F32)
        r = lax.rsqrt(jnp.mean(x2 * x2, axis=-1, keepdims=True) + EPS)
        xhat = x2 * r
        gg = g_ref[...]
        err = xhat * gg - t_ref[...]
        loss_ref[...] += jnp.sum(err * err, axis=0, keepdims=True)
        dy = err * (1.0 / D_MODEL)
        dg_ref[...] += jnp.sum(dy * xhat, axis=0, keepdims=True)
        dxhat = dy * gg
        dx2_ref[...] = r * (dxhat - xhat * jnp.mean(dxhat * xhat, axis=-1, keepdims=True))

    row = lambda w: pl.BlockSpec((tm, w), lambda i: (i, 0))
    half = lambda r: [pl.BlockSpec((r, D_FF), lambda i: (0, 0)), pl.BlockSpec((r, D_FF), lambda i: (0, 1))]
    return pl.pallas_call(
        body, name="ffn_fwd_loss", grid=(S // tm,),
        in_specs=_conv_specs(tm) + half(3) + half(1) + [_full((D_FF, D_MODEL)), row(D_MODEL), _full((1, D_MODEL)), row(D_MODEL)],
        out_specs=[row(D_MODEL), _full((1, D_MODEL)), _full((1, D_MODEL))],
        out_shape=[jax.ShapeDtypeStruct((S, D_MODEL), F32), jax.ShapeDtypeStruct((1, D_MODEL), F32),
                   jax.ShapeDtypeStruct((1, D_MODEL), F32)],
        compiler_params=_params(("arbitrary",)),
    )(a, a, a, a, w_conv, w_conv, b_conv, b_conv, w_down, x1, g3, target)


def _ffn_bwd_gate(dx2, a, w_conv, b_conv, w_down, tm):
    S = dx2.shape[0]

    def body(dx_ref, ag_ref, av_ref, hg_ref, hv_ref, wg_ref, wv_ref, bg_ref, bv_ref, wd_ref,
             dc_ref, y_ref, dw_ref, db_ref):
        i = pl.program_id(0)

        @pl.when(i == 0)
        def _():
            dw_ref[...] = jnp.zeros_like(dw_ref)
            db_ref[...] = jnp.zeros_like(db_ref)

        dx = dx_ref[...].astype(BF16)
        shifts = _shift_matrices(tm)
        for cols in _FF_CHUNKS:
            gate, val, gtaps, vtaps = _conv_gate_val((ag_ref, av_ref, hg_ref, hv_ref, wg_ref, wv_ref, bg_ref, bv_ref), shifts, cols, i == 0)
            sg = _sigmoid(gate)
            act = gate * sg
            y_ref[:, cols] = (act * val).astype(BF16)
            dy = lax.dot_general(dx, wd_ref[cols, :], NT, preferred_element_type=F32)
            dgate = dy * val * (sg + act - act * sg)
            dval = dy * act
            for d, taps, out in ((dgate, gtaps, cols), (dval, vtaps, slice(D_FF + cols.start, D_FF + cols.stop))):
                dc_ref[:, out] = d.astype(BF16)
                db_ref[0:1, out] += jnp.sum(d, axis=0, keepdims=True)
                for j in range(3):
                    dw_ref[j:j + 1, out] += jnp.sum(d * taps[j], axis=0, keepdims=True)

    row = lambda w: pl.BlockSpec((tm, w), lambda i: (i, 0))
    half = lambda r: [pl.BlockSpec((r, D_FF), lambda i: (0, 0)), pl.BlockSpec((r, D_FF), lambda i: (0, 1))]
    return pl.pallas_call(
        body, name="ffn_bwd_gate", grid=(S // tm,),
        in_specs=[row(D_MODEL)] + _conv_specs(tm) + half(3) + half(1) + [_full((D_FF, D_MODEL))],
        out_specs=[row(2 * D_FF), row(D_FF), _full((SUBLANES, 2 * D_FF)), _full((1, 2 * D_FF))],
        out_shape=[jax.ShapeDtypeStruct((S, 2 * D_FF), BF16), jax.ShapeDtypeStruct((S, D_FF), BF16),
                   jax.ShapeDtypeStruct((SUBLANES, 2 * D_FF), F32), jax.ShapeDtypeStruct((1, 2 * D_FF), F32)],
        compiler_params=_params(("arbitrary",)),
    )(dx2, a, a, a, a, w_conv, w_conv, b_conv, b_conv, w_down)


def _conv_bwd(dc, w_conv, tm, tn):
    S, C = dc.shape
    step = tm // SUBLANES
    last_blk = S // SUBLANES - 1

    def body(d_ref, nx_ref, w_ref, o_ref):
        last = pl.program_id(0) == pl.num_programs(0) - 1
        row = lax.broadcasted_iota(jnp.int32, (tm, tm), 0)
        col = lax.broadcasted_iota(jnp.int32, (tm, tm), 1)
        row8 = lax.broadcasted_iota(jnp.int32, (SUBLANES, FF_CHUNK), 0)
        ups = [(row + k == col).astype(BF16) for k in (1, 2)]
        for c0 in range(0, tn, FF_CHUNK):
            cols = slice(c0, c0 + FF_CHUNK)
            d = d_ref[:, cols]
            nx = nx_ref[:, cols].astype(F32) * jnp.where(last, 0.0, 1.0)
            out = w_ref[2:3, cols] * d.astype(F32)
            for k, up in zip((1, 2), ups):
                moved = jnp.dot(up, d, preferred_element_type=F32)
                bottom = moved[tm - SUBLANES:tm] + jnp.where(row8 >= SUBLANES - k, pltpu.roll(nx, SUBLANES - k, 0), 0.0)
                out = out + w_ref[2 - k:3 - k, cols] * jnp.concatenate([moved[0:tm - SUBLANES], bottom], axis=0)
            o_ref[:, cols] = out.astype(BF16)

    return pl.pallas_call(
        body, name="conv_bwd", grid=(S // tm, C // tn),
        in_specs=[pl.BlockSpec((tm, tn), lambda i, j: (i, j)),
                  pl.BlockSpec((SUBLANES, tn), lambda i, j: (jnp.minimum((i + 1) * step, last_blk), j)),
                  pl.BlockSpec((3, tn), lambda i, j: (0, j))],
        out_specs=pl.BlockSpec((tm, tn), lambda i, j: (i, j)),
        out_shape=jax.ShapeDtypeStruct((S, C), BF16),
        compiler_params=_params(("parallel", "parallel")),
    )(dc, dc, w_conv)


def _matmul_tn(a, b, name, bm, bn, tk, col_a=0, col_b=0, quarters=None):
    S = a.shape[0]
    gm, gn = quarters if quarters else (1, 1)
    nk = S // tk

    def body(a_ref, b_ref, o_ref):
        @pl.when(pl.program_id(2) == 0)
        def _():
            o_ref[...] = jnp.zeros_like(o_ref)

        o_ref[...] += lax.dot_general(a_ref[...].astype(BF16), b_ref[...].astype(BF16), TN, preferred_element_type=F32)

    if quarters and gn > 1:
        out_spec = pl.BlockSpec((None, bm, bn), lambda i, j, k: (j, i, 0))
        out_shape = jax.ShapeDtypeStruct((gn, gm * bm, bn), F32)
    else:
        out_spec = pl.BlockSpec((bm, bn), lambda i, j, k: (i, j))
        out_shape = jax.ShapeDtypeStruct((gm * bm, gn * bn), F32)
    return pl.pallas_call(
        body, name=name, grid=(gm, gn, nk),
        in_specs=[pl.BlockSpec((tk, bm), lambda i, j, k: (k, col_a * gm + i)),
                  pl.BlockSpec((tk, bn), lambda i, j, k: (k, col_b * gn + j))],
        out_specs=out_spec, out_shape=out_shape,
        compiler_params=_params(("parallel", "parallel", "arbitrary")),
    )(a, b)


def _up_bwd(dact, w_up_q, x1, g2, dx2, tm):
    S = x1.shape[0]
    nq, _, wq = w_up_q.shape

    def body(d_ref, w_ref, x_ref, g_ref, dx2_ref, dx1_ref, dg_ref):
        @pl.when(pl.program_id(0) == 0)
        def _():
            dg_ref[...] = jnp.zeros_like(dg_ref)

        dh = jnp.zeros((tm, D_MODEL), F32)
        for j in range(nq):
            dh = dh + lax.dot_general(d_ref[:, j * wq:(j + 1) * wq], w_ref[j], NT, preferred_element_type=F32)
        dx, dg = _rms_bwd(dh, x_ref[...], g_ref[...])
        dg_ref[...] += dg
        dx1_ref[...] = dx2_ref[...] + dx

    row = lambda w: pl.BlockSpec((tm, w), lambda i: (i, 0))
    return pl.pallas_call(
        body, name="up_bwd", grid=(S // tm,),
        in_specs=[row(nq * wq), pl.BlockSpec((nq, D_MODEL, wq), lambda i: (0, 0, 0), pipeline_mode=pl.Buffered(1)),
                  row(D_MODEL), _full((1, D_MODEL)), row(D_MODEL)],
        out_specs=[row(D_MODEL), _full((1, D_MODEL))],
        out_shape=[jax.ShapeDtypeStruct((S, D_MODEL), F32), jax.ShapeDtypeStruct((1, D_MODEL), F32)],
        compiler_params=_params(("arbitrary",)),
    )(dact, w_up_q, x1, g2, dx2)


def _out_bwd(dx1, w_out, tm):
    S = dx1.shape[0]

    def body(d_ref, w_ref, o_ref):
        o_ref[...] = lax.dot_general(d_ref[...].astype(BF16), w_ref[...], NT, preferred_element_type=F32).astype(BF16)

    return pl.pallas_call(
        body, name="out_bwd", grid=(S // tm,),
        in_specs=[pl.BlockSpec((tm, D_MODEL), lambda i: (i, 0)), _full((D_MODEL, D_MODEL))],
        out_specs=pl.BlockSpec((tm, D_MODEL), lambda i: (i, 0)),
        out_shape=jax.ShapeDtypeStruct((S, D_MODEL), BF16),
        compiler_params=_params(("parallel",)),
    )(dx1, w_out)


def _gate_bwd(z, dcat, w_mask, w_mask_t, ln_row, b_full, seg_avg, head_ind, tm):
    S = z.shape[0]
    nb = tm // SG_BLOCK

    def body(zu_ref, zv_ref, do_ref, w_ref, wt_ref, ln_ref, b_ref, avg_ref, ind_ref,
             dzu_ref, dzv_ref, dw_ref, db_ref, dln_ref, dvn_s, dbf_s):
        i = pl.program_id(0)

        @pl.when(i == 0)
        def _():
            dw_ref[...] = jnp.zeros_like(dw_ref)
            dln_ref[...] = jnp.zeros_like(dln_ref)
            dbf_s[...] = jnp.zeros_like(dbf_s)

        zu = zu_ref[...].astype(F32)
        zv = zv_ref[...].astype(F32)
        u = _gelu(zu)
        v = _gelu(zv)
        avg = avg_ref[...]
        vhat, rstd = _layer_norm_heads(v, avg)
        ln = ln_ref[...]
        vn = vhat * ln
        for b in range(nb):
            rows = slice(b * SG_BLOCK, (b + 1) * SG_BLOCK)
            vn_b = vn[rows]
            mixed = _gate_mix(vn_b, w_ref, b_ref[...])
            do = do_ref[rows, :].astype(F32)
            dzu_ref[rows, :] = (do * mixed * _gelu_grad(zu[rows])).astype(BF16)
            dmix = do * u[rows]
            dbf_s[...] += dmix
            vn_bf = vn_b.astype(BF16)
            dvn = jnp.zeros((SG_BLOCK, D_HEADS), F32)
            for h in range(N_HEADS):
                dmh = jnp.where(_head_mask(h, SG_BLOCK), dmix, 0.0).astype(BF16)
                dw_ref[h] += lax.dot_general(dmh, vn_bf, NT, preferred_element_type=F32)
                dvn = dvn + jnp.dot(wt_ref[h], dmh, preferred_element_type=F32)
            dvn_s[rows, :] = dvn
        dvn = dvn_s[...]
        dln_ref[...] += jnp.sum(dvn * vhat, axis=0, keepdims=True)
        dvhat = dvn * ln
        dv = rstd * (dvhat - _split_dot(dvhat, avg) - vhat * _split_dot(dvhat * vhat, avg))
        dzv_ref[...] = (dv * _gelu_grad(zv)).astype(BF16)

        @pl.when(i == pl.num_programs(0) - 1)
        def _():
            r = lax.broadcasted_iota(jnp.int32, (SG_BLOCK, SG_BLOCK), 0) // CHUNK
            s = lax.broadcasted_iota(jnp.int32, (SG_BLOCK, SG_BLOCK), 1) // CHUNK
            for h in range(N_HEADS):
                dw_ref[h] = jnp.where(r >= s, dw_ref[h], 0.0)
            db_ref[...] = _split_dot(dbf_s[...], ind_ref[...])

    row = lambda col: pl.BlockSpec((tm, D_HEADS), lambda i: (i, col))
    wspec = _full((N_HEADS, SG_BLOCK, SG_BLOCK))
    return pl.pallas_call(
        body, name="gate_bwd", grid=(S // tm,),
        in_specs=[row(0), row(1), row(0), wspec, wspec, _full((1, D_HEADS)), _full((SG_BLOCK, D_HEADS)),
                  _full((D_HEADS, D_HEADS)), _full((D_HEADS, LANES))],
        out_specs=[row(0), row(0), wspec, _full((SG_BLOCK, LANES)), _full((1, D_HEADS))],
        out_shape=[jax.ShapeDtypeStruct((S, D_HEADS), BF16), jax.ShapeDtypeStruct((S, D_HEADS), BF16),
                   jax.ShapeDtypeStruct((N_HEADS, SG_BLOCK, SG_BLOCK), F32), jax.ShapeDtypeStruct((SG_BLOCK, LANES), F32),
                   jax.ShapeDtypeStruct((1, D_HEADS), F32)],
        scratch_shapes=[pltpu.VMEM((tm, D_HEADS), F32), pltpu.VMEM((SG_BLOCK, D_HEADS), F32)],
        compiler_params=_params(("arbitrary",)),
    )(z, z, dcat, w_mask, w_mask_t, ln_row, b_full, seg_avg, head_ind)


def _attn_pack_grad(o, dcat, qa, lse, head_ind, k, tm):
    S = o.shape[0]

    def body(o_ref, do_ref, qa_ref, lse_ref, ind_ref, pl_ref, pt_ref, eye_ref, ds_ref, dst_ref, ls_ref, lst_ref,
             dop_ref, qb_ref, dot_ref, qbt_ref):
        do = do_ref[...]
        delta = _split_dot(o_ref[...].astype(F32) * do.astype(F32), ind_ref[...])
        hi = delta.astype(BF16)
        lo = (delta - hi.astype(F32)).astype(BF16)
        dop = jnp.dot(do, pl_ref[...], preferred_element_type=F32)
        dop = dop - jnp.dot(hi, ds_ref[0], preferred_element_type=F32) - jnp.dot(lo, ds_ref[1], preferred_element_type=F32)
        dop_ref[...] = dop.astype(BF16)
        dot = lax.dot_general(pt_ref[...], do, NT, preferred_element_type=F32)
        dot = dot - lax.dot_general(dst_ref[0], hi, NT, preferred_element_type=F32)
        dot = dot - lax.dot_general(dst_ref[1], lo, NT, preferred_element_type=F32)
        dot_ref[...] = dot.astype(BF16)
        qa = qa_ref[...]
        qb = qa.astype(F32)
        qbt = lax.dot_general(eye_ref[...], qa, NT, preferred_element_type=F32)
        for j, part in enumerate(_split3(lse_ref[...])):
            qb = qb - lax.dot_general(part, ls_ref[j], TN, preferred_element_type=F32)
            qbt = qbt - jnp.dot(lst_ref[j], part, preferred_element_type=F32)
        qb_ref[...] = qb.astype(BF16)
        qbt_ref[...] = qbt.astype(BF16)

    pad = pl.BlockSpec((tm, D_PAD), lambda i: (i, 0))
    padt = pl.BlockSpec((None, D_PAD, tm), lambda i: (i, 0, 0))
    return pl.pallas_call(
        body, name="attn_pack_grad", grid=(S // tm,),
        in_specs=[pl.BlockSpec((tm, D_HEADS), lambda i: (i, 0)), pl.BlockSpec((tm, D_HEADS), lambda i: (i, 1)), pad,
                  pl.BlockSpec((STAT_ROWS, tm), lambda i: (0, i)), _full((D_HEADS, LANES)), _full((D_HEADS, D_PAD)),
                  _full((D_PAD, D_HEADS)), _full((D_PAD, D_PAD)), _full((2, LANES, D_PAD)), _full((2, D_PAD, LANES)),
                  _full((3, STAT_ROWS, D_PAD)), _full((3, D_PAD, STAT_ROWS))],
        out_specs=[pad, pad, padt, padt],
        out_shape=[jax.ShapeDtypeStruct((S, D_PAD), BF16)] * 2 + [jax.ShapeDtypeStruct((S // tm, D_PAD, tm), BF16)] * 2,
        compiler_params=_params(("parallel",)),
    )(o, dcat, qa, lse, head_ind, k["place"], k["place_t"], jnp.eye(D_PAD, dtype=BF16), k["d_stat"],
      jnp.swapaxes(k["d_stat"], 1, 2), k["l_stat"], jnp.swapaxes(k["l_stat"], 1, 2))


def _attn_bwd(qb, qbt, ka, va, dop, dopt, k, tq, sums16):
    S = qb.shape[0]
    n = S // tq
    ns = len(sums16)

    pairs = [(kb, q) for kb in range(n) for q in range(kb, n)]
    k_of = jnp.asarray([kb for kb, _ in pairs], jnp.int32)
    q_of = jnp.asarray([q for _, q in pairs], jnp.int32)

    def body(k_of_ref, q_of_ref, q_ref, qt_ref, k_ref, v_ref, do_ref, dot_ref, pt_ref, pick_ref, *rest):
        dq_hbm, dk_ref, dv_ref, dcc_ref = rest[ns:ns + 4]
        dq_s, dk_s, dv_s, s_s, d_s, sem = rest[2 * ns + 4:2 * ns + 10]
        scatter_start, scatter_finish = _scatter_ops(rest[:ns], rest[ns + 4:2 * ns + 4], *rest[2 * ns + 10:])
        g = pl.program_id(0)
        ki, qi = k_of_ref[pl.program_id(1)], q_of_ref[pl.program_id(1)]

        @pl.when((g == 0) & (ki == 0) & (qi == 0))
        def _():
            scatter_start()

        @pl.when((ki == 0) & (qi == 0))
        def _():
            dq_s[...] = jnp.zeros_like(dq_s)

        @pl.when(qi == ki)
        def _():
            dk_s[...] = jnp.zeros_like(dk_s)
            dv_s[...] = jnp.zeros_like(dv_s)

        def step(diagonal):
            chunks = [slice(c * KEY_CHUNK, (c + 1) * KEY_CHUNK) for c in range(tq // KEY_CHUNK)]

            def scores(hh, rows, slot):
                sl = slice(hh * HEAD_PAD, (hh + 1) * HEAD_PAD)
                s_s[slot, rows, :] = lax.dot_general(q_ref[rows, sl], k_ref[:, sl], NT, preferred_element_type=F32)
                d_s[slot, rows, :] = lax.dot_general(do_ref[rows, sl], v_ref[:, sl], NT, preferred_element_type=F32)

            for rows in chunks:
                scores(0, rows, 0)
            for hh in range(GROUP_HEADS):
                sl = slice(hh * HEAD_PAD, (hh + 1) * HEAD_PAD)
                slot = hh % 2
                dv, dk = dv_s[sl, :], dk_s[sl, :]
                for rows in chunks:
                    if hh + 1 < GROUP_HEADS:
                        scores(hh + 1, rows, 1 - slot)
                    p = jnp.exp2(s_s[slot, rows, :])
                    if diagonal:
                        row = rows.start + lax.broadcasted_iota(jnp.int32, (KEY_CHUNK, tq), 0)
                        col = lax.broadcasted_iota(jnp.int32, (KEY_CHUNK, tq), 1)
                        p = jnp.where(row >= col, p, 0.0)
                    ds = (p * d_s[slot, rows, :]).astype(BF16)
                    dv = dv + jnp.dot(dot_ref[sl, rows], p.astype(BF16), preferred_element_type=F32)
                    dk = dk + jnp.dot(qt_ref[sl, rows], ds, preferred_element_type=F32)
                    qrows = pl.ds(pl.multiple_of(qi * tq + rows.start, KEY_CHUNK), KEY_CHUNK)
                    dq_s[qrows, sl] += jnp.dot(ds, k_ref[:, sl], preferred_element_type=F32)
                dv_s[sl, :] = dv
                dk_s[sl, :] = dk

        @pl.when(qi > ki)
        def _():
            step(False)

        @pl.when(qi == ki)
        def _():
            step(True)

        @pl.when(qi == n - 1)
        def _():
            dk = dk_s[...]
            pt = pt_ref[...]
            dk_ref[...] = lax.dot_general((dk * (1.0 / LOG2E)).astype(BF16), pt, TN, preferred_element_type=F32).astype(BF16)
            dv_ref[...] = lax.dot_general(dv_s[...].astype(BF16), pt, TN, preferred_element_type=F32).astype(BF16)
            dcc_ref[...] = sum(lax.dot_general(part, pick_ref[...], TN, preferred_element_type=F32) for part in _split3(dk))

        @pl.when((ki == n - 1) & (qi == n - 1))
        def _():
            cp = pltpu.make_async_copy(dq_s, dq_hbm.at[g], sem)
            cp.start()
            cp.wait()

        @pl.when((g == GROUPS - 1) & (ki == n - 1) & (qi == n - 1))
        def _():
            scatter_finish()

    gw = GROUP_HEADS * HEAD_DIM
    qspec = pl.BlockSpec((tq, GROUP_PAD), lambda g, i, ks, qs: (qs[i], g))
    qtspec = pl.BlockSpec((None, GROUP_PAD, tq), lambda g, i, ks, qs: (qs[i], g, 0))
    kspec = pl.BlockSpec((tq, GROUP_PAD), lambda g, i, ks, qs: (ks[i], g))
    kout = pl.BlockSpec((tq, gw), lambda g, i, ks, qs: (ks[i], g))
    out = pl.pallas_call(
        body, name="attn_bwd",
        grid_spec=pltpu.PrefetchScalarGridSpec(
            num_scalar_prefetch=2, grid=(GROUPS, len(pairs)),
            in_specs=[qspec, qtspec, kspec, kspec, qspec, qtspec, pl.BlockSpec((GROUP_PAD, gw), lambda g, i, ks, qs: (0, 0)),
                      pl.BlockSpec((None, GROUP_PAD, LANES), lambda g, i, ks, qs: (g, 0, 0))] + [_ANY] * ns,
            out_specs=[_ANY, kout, kout, pl.BlockSpec((None, tq, LANES), lambda g, i, ks, qs: (g, ks[i], 0))] + [_ANY] * ns,
            scratch_shapes=[pltpu.VMEM((S, GROUP_PAD), F32), pltpu.VMEM((GROUP_PAD, tq), F32), pltpu.VMEM((GROUP_PAD, tq), F32),
                            pltpu.VMEM((2, tq, tq), F32), pltpu.VMEM((2, tq, tq), F32), pltpu.SemaphoreType.DMA]
            + _scatter_sems(ns)),
        out_shape=[jax.ShapeDtypeStruct((GROUPS, S, GROUP_PAD), F32), jax.ShapeDtypeStruct((S, D_HEADS), BF16),
                   jax.ShapeDtypeStruct((S, D_HEADS), BF16), jax.ShapeDtypeStruct((GROUPS, S, LANES), F32)]
        + _scatter_shapes(sums16),
        compiler_params=_params(("arbitrary", "arbitrary")),
    )(k_of, q_of, qb, qbt, ka, va, dop, dopt, k["place_t_group"], k["pick_cols"], *sums16)
    return out[0], out[1], out[2], out[3], out[4:]


def _attn_unpack(dqp, dcc, k, tm):
    S = dqp.shape[1]
    gw = GROUP_HEADS * HEAD_DIM

    def body(dqp_ref, dcc_ref, pt_ref, pick_ref, dq_ref, dc_ref):
        dc = jnp.zeros((tm, LANES), F32)
        for g in range(GROUPS):
            x = dqp_ref[g]
            dq_ref[:, g * gw:(g + 1) * gw] = jnp.dot((x * SCALE).astype(BF16), pt_ref[...], preferred_element_type=F32).astype(BF16)
            dc = dc + _split3_dot(x, pick_ref[g]) - dcc_ref[g]
        dc_ref[...] = dc

    return pl.pallas_call(
        body, name="attn_unpack", grid=(S // tm,),
        in_specs=[pl.BlockSpec((GROUPS, tm, GROUP_PAD), lambda i: (0, i, 0)), pl.BlockSpec((GROUPS, tm, LANES), lambda i: (0, i, 0)),
                  _full((GROUP_PAD, gw)), _full((GROUPS, GROUP_PAD, LANES))],
        out_specs=[pl.BlockSpec((tm, D_HEADS), lambda i: (i, 0)), pl.BlockSpec((tm, LANES), lambda i: (i, 0))],
        out_shape=[jax.ShapeDtypeStruct((S, D_HEADS), BF16), jax.ShapeDtypeStruct((S, LANES), F32)],
        compiler_params=_params(("parallel",)),
    )(dqp, dcc, k["place_t_group"], k["pick_rows"])


def _fox_bwd(dc, f, bias_row, tb):
    S = f.shape[0]
    nb = S // tb

    def body(dc_ref, f_ref, b_ref, df_ref, dbias_ref, carry):
        @pl.when(pl.program_id(0) == 0)
        def _():
            carry[...] = jnp.zeros_like(carry)
            dbias_ref[...] = jnp.zeros_like(dbias_ref)

        r = lax.broadcasted_iota(jnp.int32, (tb, tb), 0)
        s = lax.broadcasted_iota(jnp.int32, (tb, tb), 1)
        tri = (s >= r).astype(F32)
        rc = jnp.dot(tri, dc_ref[...], precision=lax.Precision.HIGHEST, preferred_element_type=F32) + carry[0:1, :]
        carry[...] = jnp.broadcast_to(rc[0:1, :], carry.shape)
        lane = lax.broadcasted_iota(jnp.int32, (tb, LANES), 1)
        df = jnp.where(lane < N_HEADS, rc * jax.nn.sigmoid(-(f_ref[...] + b_ref[...])), 0.0)
        df_ref[...] = df.astype(BF16)
        dbias_ref[...] += jnp.sum(df, axis=0, keepdims=True)

    rev = pl.BlockSpec((tb, LANES), lambda i: (nb - 1 - i, 0))
    return pl.pallas_call(
        body, name="fox_bwd", grid=(nb,),
        in_specs=[rev, rev, _full((1, LANES))],
        out_specs=[rev, _full((1, LANES))],
        out_shape=[jax.ShapeDtypeStruct((S, LANES), BF16), jax.ShapeDtypeStruct((1, LANES), F32)],
        scratch_shapes=[pltpu.VMEM((SUBLANES, LANES), F32)],
        compiler_params=_params(("arbitrary",)),
    )(dc, f, bias_row)


_DZ_WIDTHS = (D_HEADS,) * 5 + (LANES,)


def _in_bwd(pieces, w_in, x, g1, dx1, tm):
    S = x.shape[0]

    def body(*refs):
        p_refs, (w_ref, x_ref, g_ref, dx1_ref, dx_ref, dg_ref) = refs[:6], refs[6:]

        @pl.when(pl.program_id(0) == 0)
        def _():
            dg_ref[...] = jnp.zeros_like(dg_ref)

        dh = jnp.zeros((tm, D_MODEL), F32)
        off = 0
        for p_ref, w in zip(p_refs, _DZ_WIDTHS):
            dh = dh + lax.dot_general(p_ref[...].astype(BF16), w_ref[:, off:off + w], NT, preferred_element_type=F32)
            off += w
        dx, dg = _rms_bwd(dh, x_ref[...], g_ref[...])
        dg_ref[...] += dg
        dx_ref[...] = dx1_ref[...] + dx

    row = lambda w: pl.BlockSpec((tm, w), lambda i: (i, 0))
    return pl.pallas_call(
        body, name="in_bwd", grid=(S // tm,),
        in_specs=[row(w) for w in _DZ_WIDTHS] + [_full((D_MODEL, D_IN_PAD)), row(D_MODEL), _full((1, D_MODEL)), row(D_MODEL)],
        out_specs=[row(D_MODEL), _full((1, D_MODEL))],
        out_shape=[jax.ShapeDtypeStruct((S, D_MODEL), F32), jax.ShapeDtypeStruct((1, D_MODEL), F32)],
        compiler_params=_params(("arbitrary",)),
    )(*pieces, w_in, x, g1, dx1)


def _dw_in(h1, pieces, tk):
    S = h1.shape[0]

    def body(*refs):
        h_ref, p_refs, o_ref = refs[0], refs[1:7], refs[7]

        @pl.when(pl.program_id(0) == 0)
        def _():
            o_ref[...] = jnp.zeros_like(o_ref)

        off = 0
        for p_ref, w in zip(p_refs, _DZ_WIDTHS):
            o_ref[:, off:off + w] += lax.dot_general(h_ref[...], p_ref[...].astype(BF16), TN, preferred_element_type=F32)
            off += w

    row = lambda w: pl.BlockSpec((tk, w), lambda k: (k, 0))
    return pl.pallas_call(
        body, name="dw_in", grid=(S // tk,),
        in_specs=[row(D_MODEL)] + [row(w) for w in _DZ_WIDTHS],
        out_specs=_full((D_MODEL, D_IN_PAD)),
        out_shape=jax.ShapeDtypeStruct((D_MODEL, D_IN_PAD), F32),
        compiler_params=_params(("arbitrary",)),
    )(h1, *pieces)


def _adamw_math(w, g, m, v):
    m = ADAM_B1 * m + (1.0 - ADAM_B1) * g
    v = ADAM_B2 * v + (1.0 - ADAM_B2) * (g * g)
    m_hat = m / (1.0 - ADAM_B1 ** ADAM_STEP)
    v_hat = v / (1.0 - ADAM_B2 ** ADAM_STEP)
    delta = -ADAM_LR * (m_hat / (jnp.sqrt(v_hat) + ADAM_EPS) + ADAM_WD * w)
    return delta, m, v


def _adamw(name, w, g, m, v):
    R, C = w.shape
    tr = _row_tile(R, 256)

    def body(w_ref, g_ref, m_ref, v_ref, go_ref, d_ref, nm_ref, nv_ref):
        g = g_ref[...]
        d, nm, nv = _adamw_math(w_ref[...], g, m_ref[...], v_ref[...])
        go_ref[...] = g
        d_ref[...] = d
        nm_ref[...] = nm
        nv_ref[...] = nv

    spec = pl.BlockSpec((tr, C), lambda i: (i, 0))
    return pl.pallas_call(
        body, name=name, grid=(R // tr,), in_specs=[spec] * 4, out_specs=[spec] * 4,
        out_shape=[jax.ShapeDtypeStruct((R, C), F32)] * 4,
        compiler_params=_params(("parallel",)),
    )(w, g, m, v)


def _pair_sum(name, grad, theirs, ids):
    q, half, C = theirs.shape
    tr = _row_tile(half, 256)
    nb = half // tr

    def body(ids_ref, a_ref, b_ref, s_ref, sb_ref):
        s = a_ref[...] + b_ref[...]
        s_ref[...] = s
        sb_ref[...] = s.astype(BF16)

    here = pl.BlockSpec((None, tr, C), lambda j, i, ids: (j, i, 0))
    return pl.pallas_call(
        body, name=name,
        grid_spec=pltpu.PrefetchScalarGridSpec(
            num_scalar_prefetch=1, grid=(q, nb),
            in_specs=[pl.BlockSpec((None, tr, C), lambda j, i, ids: (j, ids[1] * nb + i, 0)), here],
            out_specs=[here, here]),
        out_shape=[jax.ShapeDtypeStruct((q, half, C), F32), jax.ShapeDtypeStruct((q, half, C), BF16)],
        compiler_params=_params(("parallel", "parallel")),
    )(ids, grad, theirs)


def _chip_sum(name, sums32, others, ids):
    _, half, C = sums32.shape
    tr = _row_tile(half, 256)
    nb = half // tr

    def body(ids_ref, a_ref, o_ref, s_ref):
        s = a_ref[...]
        for j in range(3):
            s = s + o_ref[j].astype(F32)
        s_ref[...] = s

    return pl.pallas_call(
        body, name=name,
        grid_spec=pltpu.PrefetchScalarGridSpec(
            num_scalar_prefetch=1, grid=(nb,),
            in_specs=[pl.BlockSpec((None, tr, C), lambda i, ids: (ids[0], i, 0)),
                      pl.BlockSpec((3, tr, C), lambda i, ids: (0, i, 0))],
            out_specs=pl.BlockSpec((tr, C), lambda i, ids: (ids[1] * nb + i, 0))),
        out_shape=jax.ShapeDtypeStruct((2 * half, C), F32),
        compiler_params=_params(("parallel",)),
    )(ids, sums32, others)


def _place():
    return lax.axis_index("x"), lax.axis_index("y"), lax.axis_index("c")


def _other_chips(x, y):
    return [(1 - x, y), (x, 1 - y), (1 - x, 1 - y)]


_ANY = pl.BlockSpec(memory_space=pl.ANY)


def _gather_quarters(shards):
    n = len(shards)

    def body(*refs):
        start, hand_on, finish = _gather_ops(refs[:n], refs[n:2 * n], *refs[2 * n:])
        start()
        hand_on()
        finish()

    return pl.pallas_call(
        body, name="gather_weights",
        in_specs=[_ANY] * n, out_specs=[_ANY] * n,
        out_shape=_gather_shapes(shards), scratch_shapes=_gather_sems(n),
    )(*shards)


def _gather_shapes(shards):
    return [jax.ShapeDtypeStruct((4,) + s.shape, s.dtype) for s in shards]


def _gather_sems(n):
    return [pltpu.SemaphoreType.DMA((n, 3))] * 4 + [pltpu.SemaphoreType.DMA((n,))]


def _gather_ops(ins, outs, send_sems, recv_sems, pass_send_sems, pass_recv_sems, own_sems):
    n = len(ins)
    halved = [r.shape[0] % 32 == 0 for r in ins]

    def part(a, quarter, core):
        if not halved[a]:
            return outs[a].at[quarter]
        half = ins[a].shape[0] // 2
        return outs[a].at[quarter, pl.ds(core * half, half), :]

    def ici(a, j, quarter):
        x, y, c = _place()
        px, py = _other_chips(x, y)[j]
        src = ins[a]
        if halved[a]:
            half = src.shape[0] // 2
            src = src.at[pl.ds(c * half, half), :]
        return pltpu.make_async_remote_copy(src_ref=src, dst_ref=part(a, quarter, c), send_sem=send_sems.at[a, j],
                                            recv_sem=recv_sems.at[a, j], device_id=(px, py, c), device_id_type=MESH)

    def passed(a, j, core):
        x, y, c = _place()
        px, py = _other_chips(x, y)[j]
        half = part(a, 2 * px + py, core)
        return pltpu.make_async_remote_copy(src_ref=half, dst_ref=half, send_sem=pass_send_sems.at[a, j],
                                            recv_sem=pass_recv_sems.at[a, j], device_id=(x, y, 1 - c), device_id_type=MESH)

    def own(a):
        x, y, _ = _place()
        return pltpu.make_async_copy(ins[a], outs[a].at[2 * x + y], own_sems.at[a])

    def start():
        x, y, _ = _place()
        for a in range(n):
            for j in range(3):
                ici(a, j, 2 * x + y).start()
            own(a).start()

    def hand_on():
        x, y, c = _place()
        for a in range(n):
            for j, (px, py) in enumerate(_other_chips(x, y)):
                ici(a, j, 2 * px + py).wait_recv()
                if halved[a]:
                    passed(a, j, c).start()

    def finish():
        x, y, c = _place()
        for a in range(n):
            for j in range(3):
                if halved[a]:
                    passed(a, j, 1 - c).wait_recv()
                    passed(a, j, c).wait_send()
                ici(a, j, 2 * x + y).wait_send()
            own(a).wait()

    return start, hand_on, finish


def _swap_halves(grads, name):
    n = len(grads)

    def body(*refs):
        ins, outs = refs[:n], refs[n:2 * n]
        send_sems, recv_sems = refs[2 * n:]
        x, y, c = _place()
        started = []
        for a in range(n):
            half = ins[a].shape[1] // 2
            cp = pltpu.make_async_remote_copy(src_ref=ins[a].at[:, pl.ds((1 - c) * half, half), :], dst_ref=outs[a],
                                              send_sem=send_sems.at[a], recv_sem=recv_sems.at[a],
                                              device_id=(x, y, 1 - c), device_id_type=MESH)
            cp.start()
            started.append(cp)
        for cp in started:
            cp.wait()

    return pl.pallas_call(
        body, name=name,
        in_specs=[_ANY] * n, out_specs=[_ANY] * n,
        out_shape=[jax.ShapeDtypeStruct((4, g.shape[1] // 2, g.shape[2]), F32) for g in grads],
        scratch_shapes=[pltpu.SemaphoreType.DMA((n,)), pltpu.SemaphoreType.DMA((n,))],
    )(*grads)


def _scatter_quarters(sums16):
    n = len(sums16)

    def body(*refs):
        start, finish = _scatter_ops(refs[:n], refs[n:2 * n], *refs[2 * n:])
        start()
        finish()

    return pl.pallas_call(
        body, name="scatter_quarters",
        in_specs=[_ANY] * n, out_specs=[_ANY] * n,
        out_shape=_scatter_shapes(sums16), scratch_shapes=_scatter_sems(n),
    )(*sums16)


def _scatter_shapes(sums16):
    return [jax.ShapeDtypeStruct((3,) + s.shape[1:], BF16) for s in sums16]


def _scatter_sems(n):
    return [pltpu.SemaphoreType.DMA((n, 3))] * 2


def _scatter_ops(ins, outs, send_sems, recv_sems):
    n = len(ins)

    def copy(a, j):
        x, y, c = _place()
        px, py = _other_chips(x, y)[j]
        return pltpu.make_async_remote_copy(src_ref=ins[a].at[2 * px + py], dst_ref=outs[a].at[j], send_sem=send_sems.at[a, j],
                                            recv_sem=recv_sems.at[a, j], device_id=(px, py, c), device_id_type=MESH)

    def start():
        for a in range(n):
            for j in range(3):
                copy(a, j).start()

    def finish():
        for a in range(n):
            for j in range(3):
                copy(a, j).wait()

    return start, finish


def _join_halves(fulls):
    n = len(fulls)

    def body(*refs):
        ins, outs = refs[:n], refs[n:2 * n]
        send_sems, recv_sems = refs[2 * n:]
        x, y, c = _place()
        started = []
        for a in range(n):
            half = ins[a].shape[0] // 2
            rows = pl.ds(c * half, half)
            cp = pltpu.make_async_remote_copy(src_ref=ins[a].at[rows, :], dst_ref=outs[a].at[rows, :], send_sem=send_sems.at[a],
                                              recv_sem=recv_sems.at[a], device_id=(x, y, 1 - c), device_id_type=MESH)
            cp.start()
            started.append(cp)
        for cp in started:
            cp.wait()

    return pl.pallas_call(
        body, name="join_halves",
        in_specs=[_ANY] * n, out_specs=[_ANY] * n,
        out_shape=[jax.ShapeDtypeStruct(f.shape, F32) for f in fulls],
        input_output_aliases={a: a for a in range(n)},
        scratch_shapes=[pltpu.SemaphoreType.DMA((n,)), pltpu.SemaphoreType.DMA((n,))],
    )(*fulls)


def _small_allreduce(g):
    R = g.shape[0]
    half = R // 2

    def body(g_ref, out_ref, other_s, chip_s, parts_s, send_sems, recv_sems):
        x, y, c = _place()
        mine = 2 * x + y
        rows = pl.ds(pl.multiple_of(c * half, SUBLANES), half)

        def to_other_core(src, dst, k):
            return pltpu.make_async_remote_copy(src_ref=src, dst_ref=dst, send_sem=send_sems.at[k], recv_sem=recv_sems.at[k],
                                                device_id=(x, y, 1 - c), device_id_type=MESH)

        swap = to_other_core(g_ref, other_s, 0)
        swap.start()
        swap.wait()
        chip_s[...] = g_ref[...] + other_s[...]
        parts_s[mine] = chip_s[rows, :]
        sends = []
        for j, (px, py) in enumerate(_other_chips(x, y)):
            cp = pltpu.make_async_remote_copy(src_ref=chip_s.at[rows, :], dst_ref=parts_s.at[mine], send_sem=send_sems.at[1 + j],
                                              recv_sem=recv_sems.at[1 + j], device_id=(px, py, c), device_id_type=MESH)
            cp.start()
            sends.append(cp)
        for cp in sends:
            cp.wait()
        out_ref[rows, :] = (parts_s[0] + parts_s[1]) + (parts_s[2] + parts_s[3])
        join = to_other_core(out_ref.at[rows, :], out_ref.at[rows, :], 4)
        join.start()
        join.wait()

    vm = pl.BlockSpec(memory_space=pltpu.VMEM)
    return pl.pallas_call(
        body, name="small_allreduce",
        in_specs=[vm], out_specs=vm, out_shape=jax.ShapeDtypeStruct((R, LANES), F32),
        scratch_shapes=[pltpu.VMEM((R, LANES), F32), pltpu.VMEM((R, LANES), F32), pltpu.VMEM((4, half, LANES), F32),
                        pltpu.SemaphoreType.DMA((5,)), pltpu.SemaphoreType.DMA((5,))],
        compiler_params=pltpu.CompilerParams(vmem_limit_bytes=VMEM_LIMIT),
    )(g)


def _adamw_small(ws, gs, ms, vs):
    n = len(ws)

    def body(*refs):
        for k in range(n):
            w_ref, g_ref, m_ref, v_ref = (refs[j * n + k] for j in range(4))
            d, nm, nv = _adamw_math(w_ref[...], g_ref[...], m_ref[...], v_ref[...])
            refs[4 * n + k][...] = d
            refs[5 * n + k][...] = nm
            refs[6 * n + k][...] = nv

    vm = pl.BlockSpec(memory_space=pltpu.VMEM)
    out = pl.pallas_call(
        body, name="adamw_small",
        in_specs=[vm] * (4 * n), out_specs=[vm] * (3 * n),
        out_shape=[jax.ShapeDtypeStruct(w.shape, F32) for w in ws] * 3,
        compiler_params=pltpu.CompilerParams(vmem_limit_bytes=VMEM_LIMIT),
    )(*ws, *gs, *ms, *vs)
    return out[:n], out[n:2 * n], out[2 * n:]


_SMALL = (("norm_mix_g", D_MODEL), ("f_bias", N_HEADS), ("sg_ln_g", D_HEADS), ("sg_w", N_HEADS * SG_BLOCK * SG_BLOCK),
          ("sg_b", N_HEADS * SG_BLOCK), ("norm_ffn_g", D_MODEL), ("w_conv", 3 * 2 * D_FF), ("b_conv", 2 * D_FF),
          ("norm_final_g", D_MODEL))


def _pack_small(parts):
    rows = []
    for name, size in _SMALL:
        flat = parts[name].reshape(-1).astype(F32)
        pad = (-size) % (SUBLANES * LANES)
        rows.append(jnp.pad(flat, (0, pad)).reshape(-1, LANES))
    packed = jnp.concatenate(rows, axis=0)
    return jnp.pad(packed, ((0, (-packed.shape[0]) % (2 * SUBLANES)), (0, 0)))


def _unpack_small(packed, shapes):
    out, r = {}, 0
    for name, size in _SMALL:
        nrows = (size + SUBLANES * LANES - 1) // (SUBLANES * LANES) * SUBLANES
        out[name] = packed[r:r + nrows].reshape(-1)[:size].reshape(shapes[name])
        r += nrows
    return out


def _local_step(x, target, g1, w_in, f_bias, sg_ln_g, sg_w, sg_b, g2, b_conv, g3, late_shards, ids):
    S = x.shape[0]
    tm = _row_tile(S, 512)
    tms = _row_tile(S, 256)
    tq = tm

    lane = jnp.arange(D_HEADS)
    seg_avg = jnp.where(lane[:, None] // HEAD_DIM == lane[None, :] // HEAD_DIM, 1.0 / HEAD_DIM, 0.0).astype(BF16)
    head_ind = (lane[:, None] // HEAD_DIM == jnp.arange(LANES)[None, :]).astype(BF16)
    pos_chunk = jnp.arange(SG_BLOCK) // CHUNK
    w_mask32 = jnp.where(pos_chunk[:, None] >= pos_chunk[None, :], sg_w, 0.0)
    w_mask = w_mask32.astype(BF16)
    w_mask_t = jnp.swapaxes(w_mask32, 1, 2).astype(BF16)
    ln_row = sg_ln_g.reshape(1, D_HEADS)
    b_full = jnp.repeat(sg_b.T, HEAD_DIM, axis=1)
    bias_row = jnp.pad(f_bias.reshape(1, N_HEADS), ((0, 0), (0, LANES - N_HEADS)))
    b_conv_row = b_conv.reshape(1, 2 * D_FF)

    z, f, h1 = _in_proj(x, g1, w_in, tm)
    c = _fox_prep(f, bias_row, _row_tile(S, 256))
    consts = _attn_consts()
    qa, ka, va, vat = _attn_pack(z, c, consts, tm)
    out_b, lse, gathered = _attn_fwd(qa, ka, vat, consts["place_t"], tq, late_shards)
    g_out, w_up_q, g_down, g_conv = gathered
    w_out = g_out.reshape(D_MODEL, D_MODEL)
    w_down = g_down.reshape(D_FF, D_MODEL)
    w_conv = jnp.concatenate([g_conv[q] for q in range(4)], axis=1)
    out_a = _gate_fwd(z, w_mask, ln_row, b_full, seg_avg, tm)
    x1, h2 = _mix_out(x, out_a, out_b, w_out, g2, tm)
    a = _up_proj(h2, w_up_q, tm)
    dx2, sq_err, dg3 = _ffn_fwd_loss(a, w_conv, b_conv_row, w_down, x1, g3, target, tms)

    dconv, y, dw_conv8, db_conv = _ffn_bwd_gate(dx2, a, w_conv, b_conv_row, w_down, tms)
    dact = _conv_bwd(dconv, w_conv, tms, D_FF)
    dw_down = _matmul_tn(y, dx2, "dw_down", D_FF // 2, D_MODEL, tm, quarters=(2, 1))
    dx1, dg2 = _up_bwd(dact, w_up_q, x1, g2, dx2, tms)
    dw_up_q = _matmul_tn(h2, dact, "dw_up", D_MODEL, 2 * D_FF // 4, tm, quarters=(1, 4))
    dcat = _out_bwd(dx1, w_out, tm)
    dw_out_a = _matmul_tn(out_a, dx1, "dw_out_a", D_HEADS, D_MODEL, tm)
    dw_out_b = _matmul_tn(out_b, dx1, "dw_out_b", D_HEADS, D_MODEL, tm)
    early = {"w_down": dw_down.reshape(4, D_FF // 4, D_MODEL), "w_up": dw_up_q,
             "w_out": jnp.concatenate([dw_out_a, dw_out_b], axis=0).reshape(4, D_MODEL // 4, D_MODEL)}
    early_sums = _chip_sums(early, ids, "early")
    dzu, dzv, dsg_w, dsg_b_t, dln = _gate_bwd(z, dcat, w_mask, w_mask_t, ln_row, b_full, seg_avg, head_ind, tm)
    dop, qb, dopt, qbt = _attn_pack_grad(out_b, dcat, qa, lse, head_ind, consts, tm)
    dqp, dk, dv, dcc, landed = _attn_bwd(qb, qbt, ka, va, dop, dopt, consts, tq, [s16 for _, s16 in early_sums.values()])
    early_parts = {k: (s32, got) for (k, (s32, _)), got in zip(early_sums.items(), landed)}
    dq, dc = _attn_unpack(dqp, dcc, consts, tm)
    df, dbias = _fox_bwd(dc, f, bias_row, _row_tile(S, 256))
    pieces = (dzu, dzv, dq, dk, dv, df)
    dx, dg1 = _in_bwd(pieces, w_in, x, g1, dx1, tms)
    dw_in = _dw_in(h1, pieces, tm)

    grads = {
        "norm_mix_g": dg1, "f_bias": dbias[:, :N_HEADS], "sg_ln_g": dln, "sg_w": dsg_w, "sg_b": dsg_b_t[:, :N_HEADS].T,
        "norm_ffn_g": dg2, "w_conv": dw_conv8[:3], "b_conv": db_conv, "norm_final_g": dg3,
        "w_in": dw_in,
    }
    return sq_err, dx, grads, early_parts


def _chip_sums(grads_q, ids, tag):
    names = list(grads_q)
    theirs = _swap_halves([grads_q[k] for k in names], "swap_halves_" + tag)
    return {k: _pair_sum("pair_sum_" + k, grads_q[k], t, ids) for k, t in zip(names, theirs)}


def _finish_reduction(parts, ids):
    names = list(parts)
    fulls = [_chip_sum("chip_sum_" + k, s32, got, ids) for k, (s32, got) in parts.items()]
    return dict(zip(names, _join_halves(fulls)))


def kernel(x, norm_mix_g, w_in, f_bias, sg_ln_g, sg_w, sg_b, w_out, norm_ffn_g, w_up, w_conv, b_conv, w_down, norm_final_g, loss_target, m_norm_mix_g, m_w_in, m_f_bias, m_sg_ln_g, m_sg_w, m_sg_b, m_w_out, m_norm_ffn_g, m_w_up, m_w_conv, m_b_conv, m_w_down, m_norm_final_g, v_norm_mix_g, v_w_in, v_f_bias, v_sg_ln_g, v_sg_w, v_sg_b, v_w_out, v_norm_ffn_g, v_w_up, v_w_conv, v_b_conv, v_w_down, v_norm_final_g):
    args = dict(locals())
    quarter = 2 * lax.axis_index("x") + lax.axis_index("y")
    ids = jnp.stack([quarter, lax.axis_index("c")]).astype(jnp.int32)
    wq_conv = w_conv.shape[-1]

    g_in = _gather_quarters([w_in[0].astype(BF16)])[0]
    w_in_full = jnp.pad(jnp.concatenate([g_in[q] for q in range(4)], axis=1), ((0, 0), (0, D_IN_PAD - D_IN)))
    late_shards = [w_out[0].astype(BF16), w_up[0].astype(BF16), w_down[0].astype(BF16), w_conv[0]]

    sq_err, dx, grads, early_parts = _local_step(
        x[0], loss_target[0], norm_mix_g, w_in_full, f_bias[0], sg_ln_g[0], sg_w[0], sg_b[0], norm_ffn_g, b_conv[0],
        norm_final_g.reshape(1, D_MODEL), late_shards, ids)
    loss = lax.psum(0.5 * jnp.sum(sq_err) / D_MODEL, ("x", "y", "c"))

    dw_in = grads["w_in"][:, :D_IN].reshape(D_MODEL, 4, D_IN // 4).transpose(1, 0, 2)
    late_sums = _chip_sums({"w_in": dw_in}, ids, "late")
    landed = _scatter_quarters([s16 for _, s16 in late_sums.values()])
    late_parts = {k: (s32, got) for (k, (s32, _)), got in zip(late_sums.items(), landed)}
    big = _finish_reduction({**early_parts, **late_parts}, ids)

    out = {"loss": loss, "grad_x": dx[None]}
    for k in ("w_in", "w_out", "w_up", "w_down"):
        g, d, nm, nv = _adamw("adamw_" + k, args[k][0], big[k], args["m_" + k][0], args["v_" + k][0])
        out["grad_" + k], out["delta_" + k], out["new_m_" + k], out["new_v_" + k] = g[None], d[None], nm[None], nv[None]

    small_names = [n for n, _ in _SMALL]
    shapes = {n: (3, 4 * wq_conv) if n == "w_conv" else args[n].shape for n in small_names}
    g_small = _unpack_small(_small_allreduce(_pack_small({n: grads[n] for n in small_names})), shapes)
    g_small["w_conv"] = lax.dynamic_slice(g_small["w_conv"], (0, quarter * wq_conv), (3, wq_conv))[None]
    flat2d = lambda t: t.reshape(-1, t.shape[-1])
    updated = _adamw_small(*[[flat2d(src[p + n]) for n in small_names] for src, p in
                             ((args, ""), (g_small, ""), (args, "m_"), (args, "v_"))])
    for n, g in g_small.items():
        out["grad_" + n] = g
    for prefix, arrs in zip(("delta_", "new_m_", "new_v_"), updated):
        for n, t in zip(small_names, arrs):
            out[prefix + n] = t.reshape(args[n].shape)

    weights = ["norm_mix_g", "w_in", "f_bias", "sg_ln_g", "sg_w", "sg_b", "w_out", "norm_ffn_g", "w_up", "w_conv", "b_conv",
               "w_down", "norm_final_g"]
    return (out["loss"], out["grad_x"], *[out[p + n] for p in ("grad_", "delta_", "new_m_", "new_v_") for n in weights])
```

```python
import functools
import math

import jax
import jax.numpy as jnp
from jax import lax
from jax.experimental import pallas as pl
from jax.experimental.pallas import tpu as pltpu

F32 = jnp.float32
BF16 = jnp.bfloat16
MESH = pl.DeviceIdType.MESH

D_MODEL = 1024
N_HEADS = 8
HEAD_DIM = 64
D_HEADS = N_HEADS * HEAD_DIM
SG_BLOCK = 128
CHUNK = 64
D_FF = 2816
D_IN = 2 * D_HEADS + 3 * D_HEADS + N_HEADS
LANES = 128
SUBLANES = 8
D_IN_PAD = 5 * D_HEADS + LANES
EPS = 1e-6
SCALE = HEAD_DIM ** -0.5
NEG = -1e30
LOG2E = 1.4426950408889634
HEAD_PAD = LANES
D_PAD = N_HEADS * HEAD_PAD
Q_STAT = HEAD_DIM
K_STAT = HEAD_DIM + 3
L_STAT = HEAD_DIM + 6
GROUPS = 2
GROUP_HEADS = N_HEADS // GROUPS
GROUP_PAD = GROUP_HEADS * HEAD_PAD
KEY_CHUNK = 256
STAT_ROWS = 16
FF_CHUNK = 256

ADAM_LR = 0.001
ADAM_B1 = 0.9
ADAM_B2 = 0.999
ADAM_EPS = 1e-08
ADAM_WD = 0.01
ADAM_STEP = 10

VMEM_LIMIT = 56 * 1024 * 1024

NT = (((1,), (1,)), ((), ()))
TN = (((0,), (0,)), ((), ()))


def _params(sem):
    return pltpu.CompilerParams(dimension_semantics=sem, vmem_limit_bytes=VMEM_LIMIT)


def _full(shape):
    nd = len(shape)
    return pl.BlockSpec(shape, lambda *_: (0,) * nd)


def _row_tile(rows, target):
    best = None
    for t in range(SUBLANES, min(rows, target) + 1, SUBLANES):
        if rows % t == 0:
            best = t
    assert best is not None, rows
    return best


def _sigmoid(x):
    return 0.5 * jnp.tanh(0.5 * x) + 0.5


def _gelu(z):
    return 0.5 * z * (1.0 + lax.erf(z * (2.0 ** -0.5)))


def _gelu_grad(z):
    cdf = 0.5 * (1.0 + lax.erf(z * (2.0 ** -0.5)))
    pdf = jnp.exp(-0.5 * z * z) * (1.0 / math.sqrt(2.0 * math.pi))
    return cdf + z * pdf


def _split_dot(x, m):
    hi = x.astype(BF16)
    lo = (x - hi.astype(F32)).astype(BF16)
    return jnp.dot(hi, m, preferred_element_type=F32) + jnp.dot(lo, m, preferred_element_type=F32)


def _head_mask(h, rows):
    lane = lax.broadcasted_iota(jnp.int32, (rows, D_HEADS), 1)
    return (lane >= h * HEAD_DIM) & (lane < (h + 1) * HEAD_DIM)


def _rms_bwd(dh, x, g):
    r = lax.rsqrt(jnp.mean(x * x, axis=-1, keepdims=True) + EPS)
    xhat = x * r
    dg = jnp.sum(dh * xhat, axis=0, keepdims=True)
    dxhat = dh * g
    dx = r * (dxhat - xhat * jnp.mean(dxhat * xhat, axis=-1, keepdims=True))
    return dx, dg


def _in_proj(x, g1, w_in, tm):
    S = x.shape[0]
    nz = D_IN_PAD - LANES

    def body(x_ref, g_ref, w_ref, z_ref, f_ref, h_ref):
        xf = x_ref[...]
        r = lax.rsqrt(jnp.mean(xf * xf, axis=-1, keepdims=True) + EPS)
        h = (xf * r * g_ref[...]).astype(BF16)
        h_ref[...] = h
        zz = jnp.dot(h, w_ref[...], preferred_element_type=F32)
        z_ref[...] = zz[:, :nz].astype(BF16)
        f_ref[...] = zz[:, nz:]

    return pl.pallas_call(
        body, name="in_proj", grid=(S // tm,),
        in_specs=[pl.BlockSpec((tm, D_MODEL), lambda i: (i, 0)), _full((1, D_MODEL)), _full((D_MODEL, D_IN_PAD))],
        out_specs=[pl.BlockSpec((tm, nz), lambda i: (i, 0)), pl.BlockSpec((tm, LANES), lambda i: (i, 0)),
                   pl.BlockSpec((tm, D_MODEL), lambda i: (i, 0))],
        out_shape=[jax.ShapeDtypeStruct((S, nz), BF16), jax.ShapeDtypeStruct((S, LANES), F32),
                   jax.ShapeDtypeStruct((S, D_MODEL), BF16)],
        compiler_params=_params(("parallel",)),
    )(x, g1, w_in)


def _fox_prep(f, bias_row, tb):
    S = f.shape[0]

    def body(f_ref, b_ref, c_ref, carry):
        @pl.when(pl.program_id(0) == 0)
        def _():
            carry[...] = jnp.zeros_like(carry)

        xv = f_ref[...] + b_ref[...]
        lf = jnp.minimum(xv, 0.0) - jnp.log(1.0 + jnp.exp(-jnp.abs(xv)))
        r = lax.broadcasted_iota(jnp.int32, (tb, tb), 0)
        s = lax.broadcasted_iota(jnp.int32, (tb, tb), 1)
        tri = (r >= s).astype(F32)
        cs = jnp.dot(tri, lf, precision=lax.Precision.HIGHEST, preferred_element_type=F32) + carry[0:1, :]
        c_ref[...] = cs
        carry[...] = jnp.broadcast_to(cs[tb - 1:tb, :], carry.shape)

    return pl.pallas_call(
        body, name="fox_prep", grid=(S // tb,),
        in_specs=[pl.BlockSpec((tb, LANES), lambda i: (i, 0)), _full((1, LANES))],
        out_specs=pl.BlockSpec((tb, LANES), lambda i: (i, 0)),
        out_shape=jax.ShapeDtypeStruct((S, LANES), F32),
        scratch_shapes=[pltpu.VMEM((SUBLANES, LANES), F32)],
        compiler_params=_params(("arbitrary",)),
    )(f, bias_row)


def _attn_consts():
    col = jnp.arange(D_PAD)
    row = jnp.arange(D_HEADS)
    head = jnp.arange(LANES)
    place = (row[:, None] // HEAD_DIM == col[None, :] // HEAD_PAD) & (row[:, None] % HEAD_DIM == col[None, :] % HEAD_PAD)

    def stat(offset):
        return ((head[:, None] < N_HEADS) & (col[None, :] == head[:, None] * HEAD_PAD + offset)).astype(BF16)

    def ones(offsets):
        return sum((col % HEAD_PAD == o) for o in offsets).astype(F32).reshape(1, D_PAD)

    def pick(offset):
        gcol = jnp.arange(GROUP_PAD)
        return jnp.stack([((gcol[:, None] % HEAD_PAD == offset) & (head[None, :] == g * GROUP_HEADS + gcol[:, None] // HEAD_PAD))
                          for g in range(GROUPS)]).astype(BF16)

    place = place.astype(BF16)
    return {
        "place": place, "place_t": place.T, "place_t_group": place.T[:GROUP_PAD, :GROUP_HEADS * HEAD_DIM],
        "q_stat": jnp.stack([stat(Q_STAT + j) for j in range(3)]), "k_stat": jnp.stack([stat(K_STAT + j) for j in range(3)]),
        "d_stat": jnp.stack([stat(Q_STAT + j) for j in range(2)]), "l_stat": jnp.stack([stat(L_STAT + j)[:STAT_ROWS] for j in range(3)]),
        "q_ones": ones(range(K_STAT, K_STAT + 3)), "k_ones": ones(list(range(Q_STAT, Q_STAT + 3)) + list(range(L_STAT, L_STAT + 3))),
        "v_ones": ones(range(Q_STAT, Q_STAT + 2)),
        "pick_rows": pick(Q_STAT), "pick_cols": pick(K_STAT),
    }


def _split3(x):
    hi = x.astype(BF16)
    r = x - hi.astype(F32)
    mid = r.astype(BF16)
    return hi, mid, (r - mid.astype(F32)).astype(BF16)


def _split3_dot(x, m):
    return sum(jnp.dot(part, m, preferred_element_type=F32) for part in _split3(x))


def _attn_pack(z, c, k, tm):
    S = z.shape[0]

    def body(q_ref, k_ref, v_ref, c_ref, pl_ref, pt_ref, qs_ref, ks_ref, qo_ref, ko_ref, vo_ref, voc_ref,
             qa_ref, ka_ref, va_ref, vt_ref):
        place = pl_ref[...]
        q = (q_ref[...].astype(F32) * (SCALE * LOG2E)).astype(BF16)
        qa = jnp.dot(q, place, preferred_element_type=F32) + qo_ref[...]
        ka = jnp.dot(k_ref[...], place, preferred_element_type=F32) + ko_ref[...]
        for j, part in enumerate(_split3(c_ref[...] * LOG2E)):
            qa = qa + jnp.dot(part, qs_ref[j], preferred_element_type=F32)
            ka = ka - jnp.dot(part, ks_ref[j], preferred_element_type=F32)
        qa_ref[...] = qa.astype(BF16)
        ka_ref[...] = ka.astype(BF16)
        v = v_ref[...]
        va_ref[...] = (jnp.dot(v, place, preferred_element_type=F32) + vo_ref[...]).astype(BF16)
        vt_ref[...] = (lax.dot_general(pt_ref[...], v, NT, preferred_element_type=F32) + voc_ref[...]).astype(BF16)

    blk = lambda col: pl.BlockSpec((tm, D_HEADS), lambda i: (i, col))
    out = pl.BlockSpec((tm, D_PAD), lambda i: (i, 0))
    pad = jax.ShapeDtypeStruct((S, D_PAD), BF16)
    return pl.pallas_call(
        body, name="attn_pack", grid=(S // tm,),
        in_specs=[blk(2), blk(3), blk(4), pl.BlockSpec((tm, LANES), lambda i: (i, 0)), _full((D_HEADS, D_PAD)), _full((D_PAD, D_HEADS)),
                  _full((3, LANES, D_PAD)), _full((3, LANES, D_PAD)), _full((1, D_PAD)), _full((1, D_PAD)), _full((1, D_PAD)),
                  _full((D_PAD, 1))],
        out_specs=[out, out, out, pl.BlockSpec((None, D_PAD, tm), lambda i: (i, 0, 0))],
        out_shape=[pad, pad, pad, jax.ShapeDtypeStruct((S // tm, D_PAD, tm), BF16)],
        compiler_params=_params(("parallel",)),
    )(z, z, z, c, k["place"], k["place_t"], k["q_stat"], k["k_stat"], k["q_ones"], k["k_ones"], k["v_ones"], k["v_ones"].T)


def _attn_fwd(qa, ka, vat, place_t, tq, shards):
    S = qa.shape[0]
    n = S // tq
    ns = len(shards)
    hand_on_at = (2 * n) // 3

    pairs = [(q, k) for q in range(n) for k in range(q + 1)]
    q_of = jnp.asarray([q for q, _ in pairs], jnp.int32)
    k_of = jnp.asarray([k for _, k in pairs], jnp.int32)

    def body(q_of_ref, k_of_ref, q_ref, k_ref, vt_ref, pt_ref, *rest):
        o_ref, lse_ref = rest[ns:ns + 2]
        m_s, acc_s, ot_s, s_s = rest[2 * ns + 2:2 * ns + 6]
        start, hand_on, finish = _gather_ops(rest[:ns], rest[ns + 2:2 * ns + 2], *rest[2 * ns + 6:])
        qi, ki = q_of_ref[pl.program_id(0)], k_of_ref[pl.program_id(0)]

        @pl.when((qi == 0) & (ki == 0))
        def _():
            start()

        @pl.when((qi == hand_on_at) & (ki == 0))
        def _():
            hand_on()

        @pl.when(ki == 0)
        def _():
            m_s[...] = jnp.full_like(m_s, NEG)
            acc_s[...] = jnp.zeros_like(acc_s)

        def step(diagonal):
            chunks = [slice(c * KEY_CHUNK, (c + 1) * KEY_CHUNK) for c in range(tq // KEY_CHUNK)]

            def scores(h, rows, slot):
                sl = slice(h * HEAD_PAD, (h + 1) * HEAD_PAD)
                st = lax.dot_general(k_ref[rows, sl], q_ref[:, sl], NT, preferred_element_type=F32)
                if diagonal:
                    key = rows.start + lax.broadcasted_iota(jnp.int32, (KEY_CHUNK, tq), 0)
                    query = lax.broadcasted_iota(jnp.int32, (KEY_CHUNK, tq), 1)
                    st = jnp.where(query >= key, st, NEG)
                s_s[slot, rows, :] = st
                return jnp.max(st, axis=0, keepdims=True)

            m_cur = functools.reduce(jnp.maximum, [scores(0, rows, 0) for rows in chunks])
            for h in range(N_HEADS):
                sl = slice(h * HEAD_PAD, (h + 1) * HEAD_PAD)
                slot = h % 2
                m_prev = m_s[h][0:1, :]
                m_new = jnp.maximum(m_prev, m_cur)
                acc = jnp.exp2(m_prev - m_new) * acc_s[h]
                m_next = []
                for rows in chunks:
                    if h + 1 < N_HEADS:
                        m_next.append(scores(h + 1, rows, 1 - slot))
                    pt = jnp.exp2(s_s[slot, rows, :] - m_new).astype(BF16)
                    acc = acc + jnp.dot(vt_ref[sl, rows], pt, preferred_element_type=F32)
                acc_s[h] = acc
                m_s[h] = jnp.broadcast_to(m_new, (SUBLANES, tq))
                if m_next:
                    m_cur = functools.reduce(jnp.maximum, m_next)

        @pl.when(ki < qi)
        def _():
            step(False)

        @pl.when(ki == qi)
        def _():
            step(True)
            lse_ref[...] = jnp.zeros_like(lse_ref)
            for h in range(N_HEADS):
                acc = acc_s[h]
                denom = acc[Q_STAT:Q_STAT + 1, :]
                ot_s[h * HEAD_PAD:(h + 1) * HEAD_PAD, :] = (acc / denom).astype(BF16)
                lse_ref[h:h + 1, :] = m_s[h][0:1, :] + jnp.log(denom) * LOG2E
            o_ref[...] = lax.dot_general(ot_s[...], pt_ref[...], TN, preferred_element_type=F32).astype(BF16)

        @pl.when((qi == n - 1) & (ki == n - 1))
        def _():
            finish()

    out = pl.pallas_call(
        body, name="attn_fwd",
        grid_spec=pltpu.PrefetchScalarGridSpec(
            num_scalar_prefetch=2, grid=(len(pairs),),
            in_specs=[pl.BlockSpec((tq, D_PAD), lambda i, qs, ks: (qs[i], 0)),
                      pl.BlockSpec((tq, D_PAD), lambda i, qs, ks: (ks[i], 0)),
                      pl.BlockSpec((None, D_PAD, tq), lambda i, qs, ks: (ks[i], 0, 0)),
                      pl.BlockSpec((D_PAD, D_HEADS), lambda i, qs, ks: (0, 0))]
            + [_ANY] * ns,
            out_specs=[pl.BlockSpec((tq, D_HEADS), lambda i, qs, ks: (qs[i], 0)),
                       pl.BlockSpec((STAT_ROWS, tq), lambda i, qs, ks: (0, qs[i]))] + [_ANY] * ns,
            scratch_shapes=[pltpu.VMEM((N_HEADS, SUBLANES, tq), F32), pltpu.VMEM((N_HEADS, HEAD_PAD, tq), F32),
                            pltpu.VMEM((D_PAD, tq), BF16), pltpu.VMEM((2, tq, tq), F32)] + _gather_sems(ns)),
        out_shape=[jax.ShapeDtypeStruct((S, D_HEADS), BF16), jax.ShapeDtypeStruct((STAT_ROWS, S), F32)] + _gather_shapes(shards),
        compiler_params=_params(("arbitrary",)),
    )(q_of, k_of, qa, ka, vat, place_t, *shards)
    return out[0], out[1], out[2:]


def _layer_norm_heads(v, seg_avg):
    mu = _split_dot(v, seg_avg)
    d = v - mu
    var = _split_dot(d * d, seg_avg)
    rstd = lax.rsqrt(var + EPS)
    return d * rstd, rstd


def _gate_mix(vn_blk, w_ref, bias):
    acc = bias
    for h in range(N_HEADS):
        vh = jnp.where(_head_mask(h, SG_BLOCK), vn_blk, 0.0).astype(BF16)
        acc = acc + jnp.dot(w_ref[h], vh, preferred_element_type=F32)
    return acc


def _gate_fwd(z, w_mask, ln_row, b_full, seg_avg, tm):
    S = z.shape[0]

    def body(zu_ref, zv_ref, w_ref, ln_ref, b_ref, avg_ref, o_ref):
        u = _gelu(zu_ref[...].astype(F32))
        v = _gelu(zv_ref[...].astype(F32))
        vhat, _ = _layer_norm_heads(v, avg_ref[...])
        vn = vhat * ln_ref[...]
        for b in range(tm // SG_BLOCK):
            rows = slice(b * SG_BLOCK, (b + 1) * SG_BLOCK)
            mixed = _gate_mix(vn[rows], w_ref, b_ref[...])
            o_ref[rows, :] = (u[rows] * mixed).astype(BF16)

    return pl.pallas_call(
        body, name="gate_fwd", grid=(S // tm,),
        in_specs=[pl.BlockSpec((tm, D_HEADS), lambda i: (i, 0)), pl.BlockSpec((tm, D_HEADS), lambda i: (i, 1)),
                  _full((N_HEADS, SG_BLOCK, SG_BLOCK)), _full((1, D_HEADS)), _full((SG_BLOCK, D_HEADS)),
                  _full((D_HEADS, D_HEADS))],
        out_specs=pl.BlockSpec((tm, D_HEADS), lambda i: (i, 0)),
        out_shape=jax.ShapeDtypeStruct((S, D_HEADS), BF16),
        compiler_params=_params(("parallel",)),
    )(z, z, w_mask, ln_row, b_full, seg_avg)


def _mix_out(x, out_a, out_b, w_out, g2, tm):
    S = x.shape[0]

    def body(x_ref, a_ref, b_ref, w_ref, g_ref, x1_ref, h_ref):
        y = jnp.dot(a_ref[...], w_ref[:D_HEADS, :], preferred_element_type=F32)
        y = y + jnp.dot(b_ref[...], w_ref[D_HEADS:, :], preferred_element_type=F32)
        x1 = x_ref[...] + y
        x1_ref[...] = x1
        r = lax.rsqrt(jnp.mean(x1 * x1, axis=-1, keepdims=True) + EPS)
        h_ref[...] = (x1 * r * g_ref[...]).astype(BF16)

    row = lambda w: pl.BlockSpec((tm, w), lambda i: (i, 0))
    return pl.pallas_call(
        body, name="mix_out", grid=(S // tm,),
        in_specs=[row(D_MODEL), row(D_HEADS), row(D_HEADS), _full((D_MODEL, D_MODEL)), _full((1, D_MODEL))],
        out_specs=[row(D_MODEL), row(D_MODEL)],
        out_shape=[jax.ShapeDtypeStruct((S, D_MODEL), F32), jax.ShapeDtypeStruct((S, D_MODEL), BF16)],
        compiler_params=_params(("parallel",)),
    )(x, out_a, out_b, w_out, g2)


def _up_proj(h2, w_up_q, tm):
    S = h2.shape[0]
    nq, _, wq = w_up_q.shape

    def body(h_ref, w_ref, a_ref):
        a_ref[...] = jnp.dot(h_ref[...], w_ref[...], preferred_element_type=F32).astype(BF16)

    return pl.pallas_call(
        body, name="up_proj", grid=(nq, S // tm),
        in_specs=[pl.BlockSpec((tm, D_MODEL), lambda j, i: (i, 0)), pl.BlockSpec((None, D_MODEL, wq), lambda j, i: (j, 0, 0))],
        out_specs=pl.BlockSpec((tm, wq), lambda j, i: (i, j)),
        out_shape=jax.ShapeDtypeStruct((S, nq * wq), BF16),
        compiler_params=_params(("parallel", "parallel")),
    )(h2, w_up_q)


def _shift_down(a, halo, k):
    tm = a.shape[0]
    ra = pltpu.roll(a, k, 0)
    rh = pltpu.roll(halo, k, 0)
    row = lax.broadcasted_iota(jnp.int32, halo.shape, 0)
    top = jnp.where(row < k, rh, ra[0:SUBLANES])
    return jnp.concatenate([top, ra[SUBLANES:tm]], axis=0)


def _shift_up(a, halo, k):
    tm = a.shape[0]
    ra = pltpu.roll(a, tm - k, 0)
    rh = pltpu.roll(halo, SUBLANES - k, 0)
    row = lax.broadcasted_iota(jnp.int32, halo.shape, 0)
    bottom = jnp.where(row >= SUBLANES - k, rh, ra[tm - SUBLANES:tm])
    return jnp.concatenate([ra[0:tm - SUBLANES], bottom], axis=0)


def _shift_matrices(tm):
    row = lax.broadcasted_iota(jnp.int32, (tm, tm), 0)
    col = lax.broadcasted_iota(jnp.int32, (tm, tm), 1)
    return [(row == col + k).astype(BF16) for k in (1, 2)]


def _conv_taps(a, halo, first, shifts):
    tm = a.shape[0]
    halo = halo.astype(F32) * jnp.where(first, 0.0, 1.0)
    if shifts is None:
        a = a.astype(F32)
        return a, _shift_down(a, halo, 1), _shift_down(a, halo, 2)
    row8 = lax.broadcasted_iota(jnp.int32, halo.shape, 0)
    taps = [a.astype(F32)]
    for k, shift in zip((1, 2), shifts):
        down = jnp.dot(shift, a, preferred_element_type=F32)
        top = down[0:SUBLANES] + jnp.where(row8 < k, pltpu.roll(halo, k, 0), 0.0)
        taps.append(jnp.concatenate([top, down[SUBLANES:tm]], axis=0))
    return taps


def _conv_gate_val(refs, shifts, cols, first):
    ag_ref, av_ref, hg_ref, hv_ref, wg_ref, wv_ref, bg_ref, bv_ref = refs
    g0, g1, g2 = _conv_taps(ag_ref[:, cols], hg_ref[:, cols], first, shifts)
    gate = wg_ref[2:3, cols] * g0 + wg_ref[1:2, cols] * g1 + wg_ref[0:1, cols] * g2 + bg_ref[:, cols]
    v0, v1, v2 = _conv_taps(av_ref[:, cols], hv_ref[:, cols], first, shifts)
    val = wv_ref[2:3, cols] * v0 + wv_ref[1:2, cols] * v1 + wv_ref[0:1, cols] * v2 + bv_ref[:, cols]
    return gate, val, (g2, g1, g0), (v2, v1, v0)


_FF_CHUNKS = [slice(j * FF_CHUNK, (j + 1) * FF_CHUNK) for j in range(D_FF // FF_CHUNK)]


def _conv_specs(tm):
    step = tm // SUBLANES
    prev = lambda i: jnp.maximum(i * step - 1, 0)
    return [pl.BlockSpec((tm, D_FF), lambda i: (i, 0)), pl.BlockSpec((tm, D_FF), lambda i: (i, 1)),
            pl.BlockSpec((SUBLANES, D_FF), lambda i: (prev(i), 0)), pl.BlockSpec((SUBLANES, D_FF), lambda i: (prev(i), 1))]


def _ffn_fwd_loss(a, w_conv, b_conv, w_down, x1, g3, target, tm):
    S = x1.shape[0]

    def body(ag_ref, av_ref, hg_ref, hv_ref, wg_ref, wv_ref, bg_ref, bv_ref, wd_ref, x1_ref, g_ref, t_ref,
             dx2_ref, loss_ref, dg_ref):
        i = pl.program_id(0)

        @pl.when(i == 0)
        def _():
            loss_ref[...] = jnp.zeros_like(loss_ref)
            dg_ref[...] = jnp.zeros_like(dg_ref)

        x2 = x1_ref[...]
        for cols in _FF_CHUNKS:
            gate, val, _, _ = _conv_gate_val((ag_ref, av_ref, hg_ref, hv_ref, wg_ref, wv_ref, bg_ref, bv_ref), None, cols, i == 0)
            y = (gate * _sigmoid(gate) * val).astype(BF16)
            x2 = x2 + jnp.dot(y, wd_ref[cols, :], preferred_element_type=F32)
        r = lax.rsqrt(jnp.mean(x2 * x2, axis=-1, keepdims=True) + EPS)
        xhat = x2 * r
        gg = g_ref[...]
        err = xhat * gg - t_ref[...]
        loss_ref[...] += jnp.sum(err * err, axis=0, keepdims=True)
        dy = err * (1.0 / D_MODEL)
        dg_ref[...] += jnp.sum(dy * xhat, axis=0, keepdims=True)
        dxhat = dy * gg
        dx2_ref[...] = r * (dxhat - xhat * jnp.mean(dxhat * xhat, axis=-1, keepdims=True))

    row = lambda w: pl.BlockSpec((tm, w), lambda i: (i, 0))
    half = lambda r: [pl.BlockSpec((r, D_FF), lambda i: (0, 0)), pl.BlockSpec((r, D_FF), lambda i: (0, 1))]
    return pl.pallas_call(
        body, name="ffn_fwd_loss", grid=(S // tm,),
        in_specs=_conv_specs(tm) + half(3) + half(1) + [_full((D_FF, D_MODEL)), row(D_MODEL), _full((1, D_MODEL)), row(D_MODEL)],
        out_specs=[row(D_MODEL), _full((1, D_MODEL)), _full((1, D_MODEL))],
        out_shape=[jax.ShapeDtypeStruct((S, D_MODEL), F32), jax.ShapeDtypeStruct((1, D_MODEL), F32),
                   jax.ShapeDtypeStruct((1, D_MODEL), F32)],
        compiler_params=_params(("arbitrary",)),
    )(a, a, a, a, w_conv, w_conv, b_conv, b_conv, w_down, x1, g3, target)


def _ffn_bwd_gate(dx2, a, w_conv, b_conv, w_down, tm):
    S = dx2.shape[0]

    def body(dx_ref, ag_ref, av_ref, hg_ref, hv_ref, wg_ref, wv_ref, bg_ref, bv_ref, wd_ref,
             dc_ref, y_ref, dw_ref, db_ref):
        i = pl.program_id(0)

        @pl.when(i == 0)
        def _():
            dw_ref[...] = jnp.zeros_like(dw_ref)
            db_ref[...] = jnp.zeros_like(db_ref)

        dx = dx_ref[...].astype(BF16)
        shifts = _shift_matrices(tm)
        for cols in _FF_CHUNKS:
            gate, val, gtaps, vtaps = _conv_gate_val((ag_ref, av_ref, hg_ref, hv_ref, wg_ref, wv_ref, bg_ref, bv_ref), shifts, cols, i == 0)
            sg = _sigmoid(gate)
            act = gate * sg
            y_ref[:, cols] = (act * val).astype(BF16)
            dy = lax.dot_general(dx, wd_ref[cols, :], NT, preferred_element_type=F32)
            dgate = dy * val * (sg + act - act * sg)
            dval = dy * act
            for d, taps, out in ((dgate, gtaps, cols), (dval, vtaps, slice(D_FF + cols.start, D_FF + cols.stop))):
                dc_ref[:, out] = d.astype(BF16)
                db_ref[0:1, out] += jnp.sum(d, axis=0, keepdims=True)
                for j in range(3):
                    dw_ref[j:j + 1, out] += jnp.sum(d * taps[j], axis=0, keepdims=True)

    row = lambda w: pl.BlockSpec((tm, w), lambda i: (i, 0))
    half = lambda r: [pl.BlockSpec((r, D_FF), lambda i: (0, 0)), pl.BlockSpec((r, D_FF), lambda i: (0, 1))]
    return pl.pallas_call(
        body, name="ffn_bwd_gate", grid=(S // tm,),
        in_specs=[row(D_MODEL)] + _conv_specs(tm) + half(3) + half(1) + [_full((D_FF, D_MODEL))],
        out_specs=[row(2 * D_FF), row(D_FF), _full((SUBLANES, 2 * D_FF)), _full((1, 2 * D_FF))],
        out_shape=[jax.ShapeDtypeStruct((S, 2 * D_FF), BF16), jax.ShapeDtypeStruct((S, D_FF), BF16),
                   jax.ShapeDtypeStruct((SUBLANES, 2 * D_FF), F32), jax.ShapeDtypeStruct((1, 2 * D_FF), F32)],
        compiler_params=_params(("arbitrary",)),
    )(dx2, a, a, a, a, w_conv, w_conv, b_conv, b_conv, w_down)


def _conv_bwd(dc, w_conv, tm, tn):
    S, C = dc.shape
    step = tm // SUBLANES
    last_blk = S // SUBLANES - 1

    def body(d_ref, nx_ref, w_ref, o_ref):
        last = pl.program_id(0) == pl.num_programs(0) - 1
        row = lax.broadcasted_iota(jnp.int32, (tm, tm), 0)
        col = lax.broadcasted_iota(jnp.int32, (tm, tm), 1)
        row8 = lax.broadcasted_iota(jnp.int32, (SUBLANES, FF_CHUNK), 0)
        ups = [(row + k == col).astype(BF16) for k in (1, 2)]
        for c0 in range(0, tn, FF_CHUNK):
            cols = slice(c0, c0 + FF_CHUNK)
            d = d_ref[:, cols]
            nx = nx_ref[:, cols].astype(F32) * jnp.where(last, 0.0, 1.0)
            out = w_ref[2:3, cols] * d.astype(F32)
            for k, up in zip((1, 2), ups):
                moved = jnp.dot(up, d, preferred_element_type=F32)
                bottom = moved[tm - SUBLANES:tm] + jnp.where(row8 >= SUBLANES - k, pltpu.roll(nx, SUBLANES - k, 0), 0.0)
                out = out + w_ref[2 - k:3 - k, cols] * jnp.concatenate([moved[0:tm - SUBLANES], bottom], axis=0)
            o_ref[:, cols] = out.astype(BF16)

    return pl.pallas_call(
        body, name="conv_bwd", grid=(S // tm, C // tn),
        in_specs=[pl.BlockSpec((tm, tn), lambda i, j: (i, j)),
                  pl.BlockSpec((SUBLANES, tn), lambda i, j: (jnp.minimum((i + 1) * step, last_blk), j)),
                  pl.BlockSpec((3, tn), lambda i, j: (0, j))],
        out_specs=pl.BlockSpec((tm, tn), lambda i, j: (i, j)),
        out_shape=jax.ShapeDtypeStruct((S, C), BF16),
        compiler_params=_params(("parallel", "parallel")),
    )(dc, dc, w_conv)


def _matmul_tn(a, b, name, bm, bn, tk, col_a=0, col_b=0, quarters=None):
    S = a.shape[0]
    gm, gn = quarters if quarters else (1, 1)
    nk = S // tk

    def body(a_ref, b_ref, o_ref):
        @pl.when(pl.program_id(2) == 0)
        def _():
            o_ref[...] = jnp.zeros_like(o_ref)

        o_ref[...] += lax.dot_general(a_ref[...].astype(BF16), b_ref[...].astype(BF16), TN, preferred_element_type=F32)

    if quarters and gn > 1:
        out_spec = pl.BlockSpec((None, bm, bn), lambda i, j, k: (j, i, 0))
        out_shape = jax.ShapeDtypeStruct((gn, gm * bm, bn), F32)
    else:
        out_spec = pl.BlockSpec((bm, bn), lambda i, j, k: (i, j))
        out_shape = jax.ShapeDtypeStruct((gm * bm, gn * bn), F32)
    return pl.pallas_call(
        body, name=name, grid=(gm, gn, nk),
        in_specs=[pl.BlockSpec((tk, bm), lambda i, j, k: (k, col_a * gm + i)),
                  pl.BlockSpec((tk, bn), lambda i, j, k: (k, col_b * gn + j))],
        out_specs=out_spec, out_shape=out_shape,
        compiler_params=_params(("parallel", "parallel", "arbitrary")),
    )(a, b)


def _up_bwd(dact, w_up_q, x1, g2, dx2, tm):
    S = x1.shape[0]
    nq, _, wq = w_up_q.shape

    def body(d_ref, w_ref, x_ref, g_ref, dx2_ref, dx1_ref, dg_ref):
        @pl.when(pl.program_id(0) == 0)
        def _():
            dg_ref[...] = jnp.zeros_like(dg_ref)

        dh = jnp.zeros((tm, D_MODEL), F32)
        for j in range(nq):
            dh = dh + lax.dot_general(d_ref[:, j * wq:(j + 1) * wq], w_ref[j], NT, preferred_element_type=F32)
        dx, dg = _rms_bwd(dh, x_ref[...], g_ref[...])
        dg_ref[...] += dg
        dx1_ref[...] = dx2_ref[...] + dx

    row = lambda w: pl.BlockSpec((tm, w), lambda i: (i, 0))
    return pl.pallas_call(
        body, name="up_bwd", grid=(S // tm,),
        in_specs=[row(nq * wq), pl.BlockSpec((nq, D_MODEL, wq), lambda i: (0, 0, 0), pipeline_mode=pl.Buffered(1)),
                  row(D_MODEL), _full((1, D_MODEL)), row(D_MODEL)],
        out_specs=[row(D_MODEL), _full((1, D_MODEL))],
        out_shape=[jax.ShapeDtypeStruct((S, D_MODEL), F32), jax.ShapeDtypeStruct((1, D_MODEL), F32)],
        compiler_params=_params(("arbitrary",)),
    )(dact, w_up_q, x1, g2, dx2)


def _out_bwd(dx1, w_out, tm):
    S = dx1.shape[0]

    def body(d_ref, w_ref, o_ref):
        o_ref[...] = lax.dot_general(d_ref[...].astype(BF16), w_ref[...], NT, preferred_element_type=F32).astype(BF16)

    return pl.pallas_call(
        body, name="out_bwd", grid=(S // tm,),
        in_specs=[pl.BlockSpec((tm, D_MODEL), lambda i: (i, 0)), _full((D_MODEL, D_MODEL))],
        out_specs=pl.BlockSpec((tm, D_MODEL), lambda i: (i, 0)),
        out_shape=jax.ShapeDtypeStruct((S, D_MODEL), BF16),
        compiler_params=_params(("parallel",)),
    )(dx1, w_out)


def _gate_bwd(z, dcat, w_mask, w_mask_t, ln_row, b_full, seg_avg, head_ind, tm):
    S = z.shape[0]
    nb = tm // SG_BLOCK

    def body(zu_ref, zv_ref, do_ref, w_ref, wt_ref, ln_ref, b_ref, avg_ref, ind_ref,
             dzu_ref, dzv_ref, dw_ref, db_ref, dln_ref, dvn_s, dbf_s):
        i = pl.program_id(0)

        @pl.when(i == 0)
        def _():
            dw_ref[...] = jnp.zeros_like(dw_ref)
            dln_ref[...] = jnp.zeros_like(dln_ref)
            dbf_s[...] = jnp.zeros_like(dbf_s)

        zu = zu_ref[...].astype(F32)
        zv = zv_ref[...].astype(F32)
        u = _gelu(zu)
        v = _gelu(zv)
        avg = avg_ref[...]
        vhat, rstd = _layer_norm_heads(v, avg)
        ln = ln_ref[...]
        vn = vhat * ln
        for b in range(nb):
            rows = slice(b * SG_BLOCK, (b + 1) * SG_BLOCK)
            vn_b = vn[rows]
            mixed = _gate_mix(vn_b, w_ref, b_ref[...])
            do = do_ref[rows, :].astype(F32)
            dzu_ref[rows, :] = (do * mixed * _gelu_grad(zu[rows])).astype(BF16)
            dmix = do * u[rows]
            dbf_s[...] += dmix
            vn_bf = vn_b.astype(BF16)
            dvn = jnp.zeros((SG_BLOCK, D_HEADS), F32)
            for h in range(N_HEADS):
                dmh = jnp.where(_head_mask(h, SG_BLOCK), dmix, 0.0).astype(BF16)
                dw_ref[h] += lax.dot_general(dmh, vn_bf, NT, preferred_element_type=F32)
                dvn = dvn + jnp.dot(wt_ref[h], dmh, preferred_element_type=F32)
            dvn_s[rows, :] = dvn
        dvn = dvn_s[...]
        dln_ref[...] += jnp.sum(dvn * vhat, axis=0, keepdims=True)
        dvhat = dvn * ln
        dv = rstd * (dvhat - _split_dot(dvhat, avg) - vhat * _split_dot(dvhat * vhat, avg))
        dzv_ref[...] = (dv * _gelu_grad(zv)).astype(BF16)

        @pl.when(i == pl.num_programs(0) - 1)
        def _():
            r = lax.broadcasted_iota(jnp.int32, (SG_BLOCK, SG_BLOCK), 0) // CHUNK
            s = lax.broadcasted_iota(jnp.int32, (SG_BLOCK, SG_BLOCK), 1) // CHUNK
            for h in range(N_HEADS):
                dw_ref[h] = jnp.where(r >= s, dw_ref[h], 0.0)
            db_ref[...] = _split_dot(dbf_s[...], ind_ref[...])

    row = lambda col: pl.BlockSpec((tm, D_HEADS), lambda i: (i, col))
    wspec = _full((N_HEADS, SG_BLOCK, SG_BLOCK))
    return pl.pallas_call(
        body, name="gate_bwd", grid=(S // tm,),
        in_specs=[row(0), row(1), row(0), wspec, wspec, _full((1, D_HEADS)), _full((SG_BLOCK, D_HEADS)),
                  _full((D_HEADS, D_HEADS)), _full((D_HEADS, LANES))],
        out_specs=[row(0), row(0), wspec, _full((SG_BLOCK, LANES)), _full((1, D_HEADS))],
        out_shape=[jax.ShapeDtypeStruct((S, D_HEADS), BF16), jax.ShapeDtypeStruct((S, D_HEADS), BF16),
                   jax.ShapeDtypeStruct((N_HEADS, SG_BLOCK, SG_BLOCK), F32), jax.ShapeDtypeStruct((SG_BLOCK, LANES), F32),
                   jax.ShapeDtypeStruct((1, D_HEADS), F32)],
        scratch_shapes=[pltpu.VMEM((tm, D_HEADS), F32), pltpu.VMEM((SG_BLOCK, D_HEADS), F32)],
        compiler_params=_params(("arbitrary",)),
    )(z, z, dcat, w_mask, w_mask_t, ln_row, b_full, seg_avg, head_ind)


def _attn_pack_grad(o, dcat, qa, lse, head_ind, k, tm):
    S = o.shape[0]

    def body(o_ref, do_ref, qa_ref, lse_ref, ind_ref, pl_ref, pt_ref, eye_ref, ds_ref, dst_ref, ls_ref, lst_ref,
             dop_ref, qb_ref, dot_ref, qbt_ref):
        do = do_ref[...]
        delta = _split_dot(o_ref[...].astype(F32) * do.astype(F32), ind_ref[...])
        hi = delta.astype(BF16)
        lo = (delta - hi.astype(F32)).astype(BF16)
        dop = jnp.dot(do, pl_ref[...], preferred_element_type=F32)
        dop = dop - jnp.dot(hi, ds_ref[0], preferred_element_type=F32) - jnp.dot(lo, ds_ref[1], preferred_element_type=F32)
        for g in range(GROUPS):
            dop_ref[g] = dop[:, g * GROUP_PAD:(g + 1) * GROUP_PAD].astype(BF16)
        dot = lax.dot_general(pt_ref[...], do, NT, preferred_element_type=F32)
        dot = dot - lax.dot_general(dst_ref[0], hi, NT, preferred_element_type=F32)
        dot = dot - lax.dot_general(dst_ref[1], lo, NT, preferred_element_type=F32)
        dot_ref[...] = dot.astype(BF16)
        qa = qa_ref[...]
        qb = qa.astype(F32)
        qbt = lax.dot_general(eye_ref[...], qa, NT, preferred_element_type=F32)
        for j, part in enumerate(_split3(lse_ref[...])):
            qb = qb - lax.dot_general(part, ls_ref[j], TN, preferred_element_type=F32)
            qbt = qbt - jnp.dot(lst_ref[j], part, preferred_element_type=F32)
        for g in range(GROUPS):
            qb_ref[g] = qb[:, g * GROUP_PAD:(g + 1) * GROUP_PAD].astype(BF16)
        qbt_ref[...] = qbt.astype(BF16)

    pad = pl.BlockSpec((tm, D_PAD), lambda i: (i, 0))
    padt = pl.BlockSpec((None, D_PAD, tm), lambda i: (i, 0, 0))
    return pl.pallas_call(
        body, name="attn_pack_grad", grid=(S // tm,),
        in_specs=[pl.BlockSpec((tm, D_HEADS), lambda i: (i, 0)), pl.BlockSpec((tm, D_HEADS), lambda i: (i, 1)), pad,
                  pl.BlockSpec((STAT_ROWS, tm), lambda i: (0, i)), _full((D_HEADS, LANES)), _full((D_HEADS, D_PAD)),
                  _full((D_PAD, D_HEADS)), _full((D_PAD, D_PAD)), _full((2, LANES, D_PAD)), _full((2, D_PAD, LANES)),
                  _full((3, STAT_ROWS, D_PAD)), _full((3, D_PAD, STAT_ROWS))],
        out_specs=[pl.BlockSpec((GROUPS, tm, GROUP_PAD), lambda i: (0, i, 0))] * 2 + [padt, padt],
        out_shape=[jax.ShapeDtypeStruct((GROUPS, S, GROUP_PAD), BF16)] * 2 + [jax.ShapeDtypeStruct((S // tm, D_PAD, tm), BF16)] * 2,
        compiler_params=_params(("parallel",)),
    )(o, dcat, qa, lse, head_ind, k["place"], k["place_t"], jnp.eye(D_PAD, dtype=BF16), k["d_stat"],
      jnp.swapaxes(k["d_stat"], 1, 2), k["l_stat"], jnp.swapaxes(k["l_stat"], 1, 2))


def _attn_bwd(qb, qbt, ka, va, dop, dopt, k, tq, sums16):
    S = ka.shape[0]
    n = S // tq
    ns = len(sums16)

    pairs = [(kb, q) for kb in range(n) for q in range(kb, n)]
    k_of = jnp.asarray([kb for kb, _ in pairs], jnp.int32)
    q_of = jnp.asarray([q for _, q in pairs], jnp.int32)

    def body(k_of_ref, q_of_ref, q_ref, qt_ref, k_ref, v_ref, do_ref, dot_ref, pt_ref, pick_ref, *rest):
        dq_hbm, dk_ref, dv_ref, dcc_ref = rest[ns:ns + 4]
        dq_s, dk_s, dv_s, s_s, d_s, sem = rest[2 * ns + 4:2 * ns + 10]
        scatter_start, scatter_finish = _scatter_ops(rest[:ns], rest[ns + 4:2 * ns + 4], *rest[2 * ns + 10:])
        g = pl.program_id(0)
        ki, qi = k_of_ref[pl.program_id(1)], q_of_ref[pl.program_id(1)]

        @pl.when((g == 0) & (ki == 0) & (qi == 0))
        def _():
            scatter_start()

        @pl.when((ki == 0) & (qi == 0))
        def _():
            dq_s[...] = jnp.zeros_like(dq_s)

        @pl.when(qi == ki)
        def _():
            dk_s[...] = jnp.zeros_like(dk_s)
            dv_s[...] = jnp.zeros_like(dv_s)

        def step(diagonal):
            chunks = [slice(c * KEY_CHUNK, (c + 1) * KEY_CHUNK) for c in range(tq // KEY_CHUNK)]

            def scores(hh, rows, slot):
                sl = slice(hh * HEAD_PAD, (hh + 1) * HEAD_PAD)
                s_s[slot, rows, :] = lax.dot_general(q_ref[rows, sl], k_ref[:, sl], NT, preferred_element_type=F32)
                d_s[slot, rows, :] = lax.dot_general(do_ref[rows, sl], v_ref[:, sl], NT, preferred_element_type=F32)

            for rows in chunks:
                scores(0, rows, 0)
            for hh in range(GROUP_HEADS):
                sl = slice(hh * HEAD_PAD, (hh + 1) * HEAD_PAD)
                slot = hh % 2
                dv, dk = dv_s[sl, :], dk_s[sl, :]
                for rows in chunks:
                    if hh + 1 < GROUP_HEADS:
                        scores(hh + 1, rows, 1 - slot)
                    p = jnp.exp2(s_s[slot, rows, :])
                    if diagonal:
                        row = rows.start + lax.broadcasted_iota(jnp.int32, (KEY_CHUNK, tq), 0)
                        col = lax.broadcasted_iota(jnp.int32, (KEY_CHUNK, tq), 1)
                        p = jnp.where(row >= col, p, 0.0)
                    ds = (p * d_s[slot, rows, :]).astype(BF16)
                    dv = dv + jnp.dot(dot_ref[sl, rows], p.astype(BF16), preferred_element_type=F32)
                    dk = dk + jnp.dot(qt_ref[sl, rows], ds, preferred_element_type=F32)
                    qrows = pl.ds(pl.multiple_of(qi * tq + rows.start, KEY_CHUNK), KEY_CHUNK)
                    dq_s[qrows, sl] += jnp.dot(ds, k_ref[:, sl], preferred_element_type=F32)
                dv_s[sl, :] = dv
                dk_s[sl, :] = dk

        @pl.when(qi > ki)
        def _():
            step(False)

        @pl.when(qi == ki)
        def _():
            step(True)

        @pl.when(qi == n - 1)
        def _():
            dk = dk_s[...]
            pt = pt_ref[...]
            dk_ref[...] = lax.dot_general((dk * (1.0 / LOG2E)).astype(BF16), pt, TN, preferred_element_type=F32).astype(BF16)
            dv_ref[...] = lax.dot_general(dv_s[...].astype(BF16), pt, TN, preferred_element_type=F32).astype(BF16)
            dcc_ref[...] = sum(lax.dot_general(part, pick_ref[...], TN, preferred_element_type=F32) for part in _split3(dk))

        @pl.when((ki == n - 1) & (qi == n - 1))
        def _():
            cp = pltpu.make_async_copy(dq_s, dq_hbm.at[g], sem)
            cp.start()
            cp.wait()

        @pl.when((g == GROUPS - 1) & (ki == n - 1) & (qi == n - 1))
        def _():
            scatter_finish()

    gw = GROUP_HEADS * HEAD_DIM
    qspec = pl.BlockSpec((None, tq, GROUP_PAD), lambda g, i, ks, qs: (g, qs[i], 0))
    qtspec = pl.BlockSpec((None, GROUP_PAD, tq), lambda g, i, ks, qs: (qs[i], g, 0))
    kspec = pl.BlockSpec((tq, GROUP_PAD), lambda g, i, ks, qs: (ks[i], g))
    kout = pl.BlockSpec((tq, gw), lambda g, i, ks, qs: (ks[i], g))
    out = pl.pallas_call(
        body, name="attn_bwd",
        grid_spec=pltpu.PrefetchScalarGridSpec(
            num_scalar_prefetch=2, grid=(GROUPS, len(pairs)),
            in_specs=[qspec, qtspec, kspec, kspec, qspec, qtspec, pl.BlockSpec((GROUP_PAD, gw), lambda g, i, ks, qs: (0, 0)),
                      pl.BlockSpec((None, GROUP_PAD, LANES), lambda g, i, ks, qs: (g, 0, 0))] + [_ANY] * ns,
            out_specs=[_ANY, kout, kout, pl.BlockSpec((None, tq, LANES), lambda g, i, ks, qs: (g, ks[i], 0))] + [_ANY] * ns,
            scratch_shapes=[pltpu.VMEM((S, GROUP_PAD), F32), pltpu.VMEM((GROUP_PAD, tq), F32), pltpu.VMEM((GROUP_PAD, tq), F32),
                            pltpu.VMEM((2, tq, tq), F32), pltpu.VMEM((2, tq, tq), F32), pltpu.SemaphoreType.DMA]
            + _scatter_sems(ns)),
        out_shape=[jax.ShapeDtypeStruct((GROUPS, S, GROUP_PAD), F32), jax.ShapeDtypeStruct((S, D_HEADS), BF16),
                   jax.ShapeDtypeStruct((S, D_HEADS), BF16), jax.ShapeDtypeStruct((GROUPS, S, LANES), F32)]
        + _scatter_shapes(sums16),
        compiler_params=_params(("arbitrary", "arbitrary")),
    )(k_of, q_of, qb, qbt, ka, va, dop, dopt, k["place_t_group"], k["pick_cols"], *sums16)
    return out[0], out[1], out[2], out[3], out[4:]


def _attn_unpack(dqp, dcc, k, tm):
    S = dqp.shape[1]
    gw = GROUP_HEADS * HEAD_DIM

    def body(dqp_ref, dcc_ref, pt_ref, pick_ref, dq_ref, dc_ref):
        dc = jnp.zeros((tm, LANES), F32)
        for g in range(GROUPS):
            x = dqp_ref[g]
            dq_ref[:, g * gw:(g + 1) * gw] = jnp.dot((x * SCALE).astype(BF16), pt_ref[...], preferred_element_type=F32).astype(BF16)
            dc = dc + _split3_dot(x, pick_ref[g]) - dcc_ref[g]
        dc_ref[...] = dc

    return pl.pallas_call(
        body, name="attn_unpack", grid=(S // tm,),
        in_specs=[pl.BlockSpec((GROUPS, tm, GROUP_PAD), lambda i: (0, i, 0)), pl.BlockSpec((GROUPS, tm, LANES), lambda i: (0, i, 0)),
                  _full((GROUP_PAD, gw)), _full((GROUPS, GROUP_PAD, LANES))],
        out_specs=[pl.BlockSpec((tm, D_HEADS), lambda i: (i, 0)), pl.BlockSpec((tm, LANES), lambda i: (i, 0))],
        out_shape=[jax.ShapeDtypeStruct((S, D_HEADS), BF16), jax.ShapeDtypeStruct((S, LANES), F32)],
        compiler_params=_params(("parallel",)),
    )(dqp, dcc, k["place_t_group"], k["pick_rows"])


def _fox_bwd(dc, f, bias_row, tb):
    S = f.shape[0]
    nb = S // tb

    def body(dc_ref, f_ref, b_ref, df_ref, dbias_ref, carry):
        @pl.when(pl.program_id(0) == 0)
        def _():
            carry[...] = jnp.zeros_like(carry)
            dbias_ref[...] = jnp.zeros_like(dbias_ref)

        r = lax.broadcasted_iota(jnp.int32, (tb, tb), 0)
        s = lax.broadcasted_iota(jnp.int32, (tb, tb), 1)
        tri = (s >= r).astype(F32)
        rc = jnp.dot(tri, dc_ref[...], precision=lax.Precision.HIGHEST, preferred_element_type=F32) + carry[0:1, :]
        carry[...] = jnp.broadcast_to(rc[0:1, :], carry.shape)
        lane = lax.broadcasted_iota(jnp.int32, (tb, LANES), 1)
        df = jnp.where(lane < N_HEADS, rc * jax.nn.sigmoid(-(f_ref[...] + b_ref[...])), 0.0)
        df_ref[...] = df.astype(BF16)
        dbias_ref[...] += jnp.sum(df, axis=0, keepdims=True)

    rev = pl.BlockSpec((tb, LANES), lambda i: (nb - 1 - i, 0))
    return pl.pallas_call(
        body, name="fox_bwd", grid=(nb,),
        in_specs=[rev, rev, _full((1, LANES))],
        out_specs=[rev, _full((1, LANES))],
        out_shape=[jax.ShapeDtypeStruct((S, LANES), BF16), jax.ShapeDtypeStruct((1, LANES), F32)],
        scratch_shapes=[pltpu.VMEM((SUBLANES, LANES), F32)],
        compiler_params=_params(("arbitrary",)),
    )(dc, f, bias_row)


_DZ_WIDTHS = (D_HEADS,) * 5 + (LANES,)


def _in_bwd(pieces, w_in, x, g1, dx1, tm):
    S = x.shape[0]

    def body(*refs):
        p_refs, (w_ref, x_ref, g_ref, dx1_ref, dx_ref, dg_ref) = refs[:6], refs[6:]

        @pl.when(pl.program_id(0) == 0)
        def _():
            dg_ref[...] = jnp.zeros_like(dg_ref)

        dh = jnp.zeros((tm, D_MODEL), F32)
        off = 0
        for p_ref, w in zip(p_refs, _DZ_WIDTHS):
            dh = dh + lax.dot_general(p_ref[...].astype(BF16), w_ref[:, off:off + w], NT, preferred_element_type=F32)
            off += w
        dx, dg = _rms_bwd(dh, x_ref[...], g_ref[...])
        dg_ref[...] += dg
        dx_ref[...] = dx1_ref[...] + dx

    row = lambda w: pl.BlockSpec((tm, w), lambda i: (i, 0))
    return pl.pallas_call(
        body, name="in_bwd", grid=(S // tm,),
        in_specs=[row(w) for w in _DZ_WIDTHS] + [_full((D_MODEL, D_IN_PAD)), row(D_MODEL), _full((1, D_MODEL)), row(D_MODEL)],
        out_specs=[row(D_MODEL), _full((1, D_MODEL))],
        out_shape=[jax.ShapeDtypeStruct((S, D_MODEL), F32), jax.ShapeDtypeStruct((1, D_MODEL), F32)],
        compiler_params=_params(("arbitrary",)),
    )(*pieces, w_in, x, g1, dx1)


def _dw_in(h1, pieces, tk):
    S = h1.shape[0]

    def body(*refs):
        h_ref, p_refs, o_ref = refs[0], refs[1:7], refs[7]

        @pl.when(pl.program_id(0) == 0)
        def _():
            o_ref[...] = jnp.zeros_like(o_ref)

        off = 0
        for p_ref, w in zip(p_refs, _DZ_WIDTHS):
            o_ref[:, off:off + w] += lax.dot_general(h_ref[...], p_ref[...].astype(BF16), TN, preferred_element_type=F32)
            off += w

    row = lambda w: pl.BlockSpec((tk, w), lambda k: (k, 0))
    return pl.pallas_call(
        body, name="dw_in", grid=(S // tk,),
        in_specs=[row(D_MODEL)] + [row(w) for w in _DZ_WIDTHS],
        out_specs=_full((D_MODEL, D_IN_PAD)),
        out_shape=jax.ShapeDtypeStruct((D_MODEL, D_IN_PAD), F32),
        compiler_params=_params(("arbitrary",)),
    )(h1, *pieces)


def _adamw_math(w, g, m, v):
    m = ADAM_B1 * m + (1.0 - ADAM_B1) * g
    v = ADAM_B2 * v + (1.0 - ADAM_B2) * (g * g)
    m_hat = m / (1.0 - ADAM_B1 ** ADAM_STEP)
    v_hat = v / (1.0 - ADAM_B2 ** ADAM_STEP)
    delta = -ADAM_LR * (m_hat / (jnp.sqrt(v_hat) + ADAM_EPS) + ADAM_WD * w)
    return delta, m, v


def _adamw(name, w, g, m, v):
    R, C = w.shape
    tr = _row_tile(R, 256)

    def body(w_ref, g_ref, m_ref, v_ref, go_ref, d_ref, nm_ref, nv_ref):
        g = g_ref[...]
        d, nm, nv = _adamw_math(w_ref[...], g, m_ref[...], v_ref[...])
        go_ref[...] = g
        d_ref[...] = d
        nm_ref[...] = nm
        nv_ref[...] = nv

    spec = pl.BlockSpec((tr, C), lambda i: (i, 0))
    return pl.pallas_call(
        body, name=name, grid=(R // tr,), in_specs=[spec] * 4, out_specs=[spec] * 4,
        out_shape=[jax.ShapeDtypeStruct((R, C), F32)] * 4,
        compiler_params=_params(("parallel",)),
    )(w, g, m, v)


def _pair_sum(name, grad, theirs, ids):
    q, half, C = theirs.shape
    tr = _row_tile(half, 256)
    nb = half // tr

    def body(ids_ref, a_ref, b_ref, s_ref, sb_ref):
        s = a_ref[...] + b_ref[...]
        s_ref[...] = s
        sb_ref[...] = s.astype(BF16)

    here = pl.BlockSpec((None, tr, C), lambda j, i, ids: (j, i, 0))
    return pl.pallas_call(
        body, name=name,
        grid_spec=pltpu.PrefetchScalarGridSpec(
            num_scalar_prefetch=1, grid=(q, nb),
            in_specs=[pl.BlockSpec((None, tr, C), lambda j, i, ids: (j, ids[1] * nb + i, 0)), here],
            out_specs=[here, here]),
        out_shape=[jax.ShapeDtypeStruct((q, half, C), F32), jax.ShapeDtypeStruct((q, half, C), BF16)],
        compiler_params=_params(("parallel", "parallel")),
    )(ids, grad, theirs)


def _chip_sum(name, sums32, others, ids):
    _, half, C = sums32.shape
    tr = _row_tile(half, 256)
    nb = half // tr

    def body(ids_ref, a_ref, o_ref, s_ref):
        s = a_ref[...]
        for j in range(3):
            s = s + o_ref[j].astype(F32)
        s_ref[...] = s

    return pl.pallas_call(
        body, name=name,
        grid_spec=pltpu.PrefetchScalarGridSpec(
            num_scalar_prefetch=1, grid=(nb,),
            in_specs=[pl.BlockSpec((None, tr, C), lambda i, ids: (ids[0], i, 0)),
                      pl.BlockSpec((3, tr, C), lambda i, ids: (0, i, 0))],
            out_specs=pl.BlockSpec((tr, C), lambda i, ids: (ids[1] * nb + i, 0))),
        out_shape=jax.ShapeDtypeStruct((2 * half, C), F32),
        compiler_params=_params(("parallel",)),
    )(ids, sums32, others)


def _place():
    return lax.axis_index("x"), lax.axis_index("y"), lax.axis_index("c")


def _other_chips(x, y):
    return [(1 - x, y), (x, 1 - y), (1 - x, 1 - y)]


_ANY = pl.BlockSpec(memory_space=pl.ANY)


def _gather_quarters(shards):
    n = len(shards)

    def body(*refs):
        start, hand_on, finish = _gather_ops(refs[:n], refs[n:2 * n], *refs[2 * n:])
        start()
        hand_on()
        finish()

    return pl.pallas_call(
        body, name="gather_weights",
        in_specs=[_ANY] * n, out_specs=[_ANY] * n,
        out_shape=_gather_shapes(shards), scratch_shapes=_gather_sems(n),
    )(*shards)


def _gather_shapes(shards):
    return [jax.ShapeDtypeStruct((4,) + s.shape, s.dtype) for s in shards]


def _gather_sems(n):
    return [pltpu.SemaphoreType.DMA((n, 3))] * 4 + [pltpu.SemaphoreType.DMA((n,))]


def _gather_ops(ins, outs, send_sems, recv_sems, pass_send_sems, pass_recv_sems, own_sems):
    n = len(ins)
    halved = [r.shape[0] % 32 == 0 for r in ins]

    def part(a, quarter, core):
        if not halved[a]:
            return outs[a].at[quarter]
        half = ins[a].shape[0] // 2
        return outs[a].at[quarter, pl.ds(core * half, half), :]

    def ici(a, j, quarter):
        x, y, c = _place()
        px, py = _other_chips(x, y)[j]
        src = ins[a]
        if halved[a]:
            half = src.shape[0] // 2
            src = src.at[pl.ds(c * half, half), :]
        return pltpu.make_async_remote_copy(src_ref=src, dst_ref=part(a, quarter, c), send_sem=send_sems.at[a, j],
                                            recv_sem=recv_sems.at[a, j], device_id=(px, py, c), device_id_type=MESH)

    def passed(a, j, core):
        x, y, c = _place()
        px, py = _other_chips(x, y)[j]
        half = part(a, 2 * px + py, core)
        return pltpu.make_async_remote_copy(src_ref=half, dst_ref=half, send_sem=pass_send_sems.at[a, j],
                                            recv_sem=pass_recv_sems.at[a, j], device_id=(x, y, 1 - c), device_id_type=MESH)

    def own(a):
        x, y, _ = _place()
        return pltpu.make_async_copy(ins[a], outs[a].at[2 * x + y], own_sems.at[a])

    def start():
        x, y, _ = _place()
        for a in range(n):
            for j in range(3):
                ici(a, j, 2 * x + y).start()
            own(a).start()

    def hand_on():
        x, y, c = _place()
        for a in range(n):
            for j, (px, py) in enumerate(_other_chips(x, y)):
                ici(a, j, 2 * px + py).wait_recv()
                if halved[a]:
                    passed(a, j, c).start()

    def finish():
        x, y, c = _place()
        for a in range(n):
            for j in range(3):
                if halved[a]:
                    passed(a, j, 1 - c).wait_recv()
                    passed(a, j, c).wait_send()
                ici(a, j, 2 * x + y).wait_send()
            own(a).wait()

    return start, hand_on, finish


def _swap_halves(grads, name):
    n = len(grads)

    def body(*refs):
        ins, outs = refs[:n], refs[n:2 * n]
        send_sems, recv_sems = refs[2 * n:]
        x, y, c = _place()
        started = []
        for a in range(n):
            half = ins[a].shape[1] // 2
            cp = pltpu.make_async_remote_copy(src_ref=ins[a].at[:, pl.ds((1 - c) * half, half), :], dst_ref=outs[a],
                                              send_sem=send_sems.at[a], recv_sem=recv_sems.at[a],
                                              device_id=(x, y, 1 - c), device_id_type=MESH)
            cp.start()
            started.append(cp)
        for cp in started:
            cp.wait()

    return pl.pallas_call(
        body, name=name,
        in_specs=[_ANY] * n, out_specs=[_ANY] * n,
        out_shape=[jax.ShapeDtypeStruct((4, g.shape[1] // 2, g.shape[2]), F32) for g in grads],
        scratch_shapes=[pltpu.SemaphoreType.DMA((n,)), pltpu.SemaphoreType.DMA((n,))],
    )(*grads)


def _scatter_quarters(sums16):
    n = len(sums16)

    def body(*refs):
        start, finish = _scatter_ops(refs[:n], refs[n:2 * n], *refs[2 * n:])
        start()
        finish()

    return pl.pallas_call(
        body, name="scatter_quarters",
        in_specs=[_ANY] * n, out_specs=[_ANY] * n,
        out_shape=_scatter_shapes(sums16), scratch_shapes=_scatter_sems(n),
    )(*sums16)


def _scatter_shapes(sums16):
    return [jax.ShapeDtypeStruct((3,) + s.shape[1:], BF16) for s in sums16]


def _scatter_sems(n):
    return [pltpu.SemaphoreType.DMA((n, 3))] * 2


def _scatter_ops(ins, outs, send_sems, recv_sems):
    n = len(ins)

    def copy(a, j):
        x, y, c = _place()
        px, py = _other_chips(x, y)[j]
        return pltpu.make_async_remote_copy(src_ref=ins[a].at[2 * px + py], dst_ref=outs[a].at[j], send_sem=send_sems.at[a, j],
                                            recv_sem=recv_sems.at[a, j], device_id=(px, py, c), device_id_type=MESH)

    def start():
        for a in range(n):
            for j in range(3):
                copy(a, j).start()

    def finish():
        for a in range(n):
            for j in range(3):
                copy(a, j).wait()

    return start, finish


def _join_halves(fulls):
    n = len(fulls)

    def body(*refs):
        ins, outs = refs[:n], refs[n:2 * n]
        send_sems, recv_sems = refs[2 * n:]
        x, y, c = _place()
        started = []
        for a in range(n):
            half = ins[a].shape[0] // 2
            rows = pl.ds(c * half, half)
            cp = pltpu.make_async_remote_copy(src_ref=ins[a].at[rows, :], dst_ref=outs[a].at[rows, :], send_sem=send_sems.at[a],
                                              recv_sem=recv_sems.at[a], device_id=(x, y, 1 - c), device_id_type=MESH)
            cp.start()
            started.append(cp)
        for cp in started:
            cp.wait()

    return pl.pallas_call(
        body, name="join_halves",
        in_specs=[_ANY] * n, out_specs=[_ANY] * n,
        out_shape=[jax.ShapeDtypeStruct(f.shape, F32) for f in fulls],
        input_output_aliases={a: a for a in range(n)},
        scratch_shapes=[pltpu.SemaphoreType.DMA((n,)), pltpu.SemaphoreType.DMA((n,))],
    )(*fulls)


def _small_allreduce(g):
    R = g.shape[0]
    half = R // 2

    def body(g_ref, out_ref, other_s, chip_s, parts_s, send_sems, recv_sems):
        x, y, c = _place()
        mine = 2 * x + y
        rows = pl.ds(pl.multiple_of(c * half, SUBLANES), half)

        def to_other_core(src, dst, k):
            return pltpu.make_async_remote_copy(src_ref=src, dst_ref=dst, send_sem=send_sems.at[k], recv_sem=recv_sems.at[k],
                                                device_id=(x, y, 1 - c), device_id_type=MESH)

        swap = to_other_core(g_ref, other_s, 0)
        swap.start()
        swap.wait()
        chip_s[...] = g_ref[...] + other_s[...]
        parts_s[mine] = chip_s[rows, :]
        sends = []
        for j, (px, py) in enumerate(_other_chips(x, y)):
            cp = pltpu.make_async_remote_copy(src_ref=chip_s.at[rows, :], dst_ref=parts_s.at[mine], send_sem=send_sems.at[1 + j],
                                              recv_sem=recv_sems.at[1 + j], device_id=(px, py, c), device_id_type=MESH)
            cp.start()
            sends.append(cp)
        for cp in sends:
            cp.wait()
        out_ref[rows, :] = (parts_s[0] + parts_s[1]) + (parts_s[2] + parts_s[3])
        join = to_other_core(out_ref.at[rows, :], out_ref.at[rows, :], 4)
        join.start()
        join.wait()

    vm = pl.BlockSpec(memory_space=pltpu.VMEM)
    return pl.pallas_call(
        body, name="small_allreduce",
        in_specs=[vm], out_specs=vm, out_shape=jax.ShapeDtypeStruct((R, LANES), F32),
        scratch_shapes=[pltpu.VMEM((R, LANES), F32), pltpu.VMEM((R, LANES), F32), pltpu.VMEM((4, half, LANES), F32),
                        pltpu.SemaphoreType.DMA((5,)), pltpu.SemaphoreType.DMA((5,))],
        compiler_params=pltpu.CompilerParams(vmem_limit_bytes=VMEM_LIMIT),
    )(g)


def _adamw_small(ws, gs, ms, vs):
    n = len(ws)

    def body(*refs):
        for k in range(n):
            w_ref, g_ref, m_ref, v_ref = (refs[j * n + k] for j in range(4))
            d, nm, nv = _adamw_math(w_ref[...], g_ref[...], m_ref[...], v_ref[...])
            refs[4 * n + k][...] = d
            refs[5 * n + k][...] = nm
            refs[6 * n + k][...] = nv

    vm = pl.BlockSpec(memory_space=pltpu.VMEM)
    out = pl.pallas_call(
        body, name="adamw_small",
        in_specs=[vm] * (4 * n), out_specs=[vm] * (3 * n),
        out_shape=[jax.ShapeDtypeStruct(w.shape, F32) for w in ws] * 3,
        compiler_params=pltpu.CompilerParams(vmem_limit_bytes=VMEM_LIMIT),
    )(*ws, *gs, *ms, *vs)
    return out[:n], out[n:2 * n], out[2 * n:]


_SMALL = (("norm_mix_g", D_MODEL), ("f_bias", N_HEADS), ("sg_ln_g", D_HEADS), ("sg_w", N_HEADS * SG_BLOCK * SG_BLOCK),
          ("sg_b", N_HEADS * SG_BLOCK), ("norm_ffn_g", D_MODEL), ("w_conv", 3 * 2 * D_FF), ("b_conv", 2 * D_FF),
          ("norm_final_g", D_MODEL))


def _pack_small(parts):
    rows = []
    for name, size in _SMALL:
        flat = parts[name].reshape(-1).astype(F32)
        pad = (-size) % (SUBLANES * LANES)
        rows.append(jnp.pad(flat, (0, pad)).reshape(-1, LANES))
    packed = jnp.concatenate(rows, axis=0)
    return jnp.pad(packed, ((0, (-packed.shape[0]) % (2 * SUBLANES)), (0, 0)))


def _unpack_small(packed, shapes):
    out, r = {}, 0
    for name, size in _SMALL:
        nrows = (size + SUBLANES * LANES - 1) // (SUBLANES * LANES) * SUBLANES
        out[name] = packed[r:r + nrows].reshape(-1)[:size].reshape(shapes[name])
        r += nrows
    return out


def _local_step(x, target, g1, w_in, f_bias, sg_ln_g, sg_w, sg_b, g2, b_conv, g3, late_shards, ids):
    S = x.shape[0]
    tm = _row_tile(S, 512)
    tms = _row_tile(S, 256)
    tq = tm

    lane = jnp.arange(D_HEADS)
    seg_avg = jnp.where(lane[:, None] // HEAD_DIM == lane[None, :] // HEAD_DIM, 1.0 / HEAD_DIM, 0.0).astype(BF16)
    head_ind = (lane[:, None] // HEAD_DIM == jnp.arange(LANES)[None, :]).astype(BF16)
    pos_chunk = jnp.arange(SG_BLOCK) // CHUNK
    w_mask32 = jnp.where(pos_chunk[:, None] >= pos_chunk[None, :], sg_w, 0.0)
    w_mask = w_mask32.astype(BF16)
    w_mask_t = jnp.swapaxes(w_mask32, 1, 2).astype(BF16)
    ln_row = sg_ln_g.reshape(1, D_HEADS)
    b_full = jnp.repeat(sg_b.T, HEAD_DIM, axis=1)
    bias_row = jnp.pad(f_bias.reshape(1, N_HEADS), ((0, 0), (0, LANES - N_HEADS)))
    b_conv_row = b_conv.reshape(1, 2 * D_FF)

    z, f, h1 = _in_proj(x, g1, w_in, tm)
    c = _fox_prep(f, bias_row, _row_tile(S, 256))
    consts = _attn_consts()
    qa, ka, va, vat = _attn_pack(z, c, consts, tm)
    out_b, lse, gathered = _attn_fwd(qa, ka, vat, consts["place_t"], tq, late_shards)
    g_out, w_up_q, g_down, g_conv = gathered
    w_out = g_out.reshape(D_MODEL, D_MODEL)
    w_down = g_down.reshape(D_FF, D_MODEL)
    w_conv = jnp.concatenate([g_conv[q] for q in range(4)], axis=1)
    out_a = _gate_fwd(z, w_mask, ln_row, b_full, seg_avg, tm)
    x1, h2 = _mix_out(x, out_a, out_b, w_out, g2, tm)
    a = _up_proj(h2, w_up_q, tm)
    dx2, sq_err, dg3 = _ffn_fwd_loss(a, w_conv, b_conv_row, w_down, x1, g3, target, tms)

    dconv, y, dw_conv8, db_conv = _ffn_bwd_gate(dx2, a, w_conv, b_conv_row, w_down, tms)
    dact = _conv_bwd(dconv, w_conv, tms, D_FF)
    dw_down = _matmul_tn(y, dx2, "dw_down", D_FF // 2, D_MODEL, tm, quarters=(2, 1))
    dx1, dg2 = _up_bwd(dact, w_up_q, x1, g2, dx2, tms)
    dw_up_q = _matmul_tn(h2, dact, "dw_up", D_MODEL, 2 * D_FF // 4, tm, quarters=(1, 4))
    dcat = _out_bwd(dx1, w_out, tm)
    dw_out_a = _matmul_tn(out_a, dx1, "dw_out_a", D_HEADS, D_MODEL, tm)
    dw_out_b = _matmul_tn(out_b, dx1, "dw_out_b", D_HEADS, D_MODEL, tm)
    early = {"w_down": dw_down.reshape(4, D_FF // 4, D_MODEL), "w_up": dw_up_q,
             "w_out": jnp.concatenate([dw_out_a, dw_out_b], axis=0).reshape(4, D_MODEL // 4, D_MODEL)}
    early_sums = _chip_sums(early, ids, "early")
    dzu, dzv, dsg_w, dsg_b_t, dln = _gate_bwd(z, dcat, w_mask, w_mask_t, ln_row, b_full, seg_avg, head_ind, tm)
    dop, qb, dopt, qbt = _attn_pack_grad(out_b, dcat, qa, lse, head_ind, consts, tm)
    dqp, dk, dv, dcc, landed = _attn_bwd(qb, qbt, ka, va, dop, dopt, consts, tq, [s16 for _, s16 in early_sums.values()])
    early_parts = {k: (s32, got) for (k, (s32, _)), got in zip(early_sums.items(), landed)}
    dq, dc = _attn_unpack(dqp, dcc, consts, tm)
    df, dbias = _fox_bwd(dc, f, bias_row, _row_tile(S, 256))
    pieces = (dzu, dzv, dq, dk, dv, df)
    dx, dg1 = _in_bwd(pieces, w_in, x, g1, dx1, tms)
    dw_in = _dw_in(h1, pieces, tm)

    grads = {
        "norm_mix_g": dg1, "f_bias": dbias[:, :N_HEADS], "sg_ln_g": dln, "sg_w": dsg_w, "sg_b": dsg_b_t[:, :N_HEADS].T,
        "norm_ffn_g": dg2, "w_conv": dw_conv8[:3], "b_conv": db_conv, "norm_final_g": dg3,
        "w_in": dw_in,
    }
    return sq_err, dx, grads, early_parts


def _chip_sums(grads_q, ids, tag):
    names = list(grads_q)
    theirs = _swap_halves([grads_q[k] for k in names], "swap_halves_" + tag)
    return {k: _pair_sum("pair_sum_" + k, grads_q[k], t, ids) for k, t in zip(names, theirs)}


def _finish_reduction(parts, ids):
    names = list(parts)
    fulls = [_chip_sum("chip_sum_" + k, s32, got, ids) for k, (s32, got) in parts.items()]
    return dict(zip(names, _join_halves(fulls)))


def kernel(x, norm_mix_g, w_in, f_bias, sg_ln_g, sg_w, sg_b, w_out, norm_ffn_g, w_up, w_conv, b_conv, w_down, norm_final_g, loss_target, m_norm_mix_g, m_w_in, m_f_bias, m_sg_ln_g, m_sg_w, m_sg_b, m_w_out, m_norm_ffn_g, m_w_up, m_w_conv, m_b_conv, m_w_down, m_norm_final_g, v_norm_mix_g, v_w_in, v_f_bias, v_sg_ln_g, v_sg_w, v_sg_b, v_w_out, v_norm_ffn_g, v_w_up, v_w_conv, v_b_conv, v_w_down, v_norm_final_g):
    args = dict(locals())
    quarter = 2 * lax.axis_index("x") + lax.axis_index("y")
    ids = jnp.stack([quarter, lax.axis_index("c")]).astype(jnp.int32)
    wq_conv = w_conv.shape[-1]

    g_in = _gather_quarters([w_in[0].astype(BF16)])[0]
    w_in_full = jnp.pad(jnp.concatenate([g_in[q] for q in range(4)], axis=1), ((0, 0), (0, D_IN_PAD - D_IN)))
    late_shards = [w_out[0].astype(BF16), w_up[0].astype(BF16), w_down[0].astype(BF16), w_conv[0]]

    sq_err, dx, grads, early_parts = _local_step(
        x[0], loss_target[0], norm_mix_g, w_in_full, f_bias[0], sg_ln_g[0], sg_w[0], sg_b[0], norm_ffn_g, b_conv[0],
        norm_final_g.reshape(1, D_MODEL), late_shards, ids)
    loss = lax.psum(0.5 * jnp.sum(sq_err) / D_MODEL, ("x", "y", "c"))

    dw_in = grads["w_in"][:, :D_IN].reshape(D_MODEL, 4, D_IN // 4).transpose(1, 0, 2)
    late_sums = _chip_sums({"w_in": dw_in}, ids, "late")
    landed = _scatter_quarters([s16 for _, s16 in late_sums.values()])
    late_parts = {k: (s32, got) for (k, (s32, _)), got in zip(late_sums.items(), landed)}
    big = _finish_reduction({**early_parts, **late_parts}, ids)

    out = {"loss": loss, "grad_x": dx[None]}
    for k in ("w_in", "w_out", "w_up", "w_down"):
        g, d, nm, nv = _adamw("adamw_" + k, args[k][0], big[k], args["m_" + k][0], args["v_" + k][0])
        out["grad_" + k], out["delta_" + k], out["new_m_" + k], out["new_v_" + k] = g[None], d[None], nm[None], nv[None]

    small_names = [n for n, _ in _SMALL]
    shapes = {n: (3, 4 * wq_conv) if n == "w_conv" else args[n].shape for n in small_names}
    g_small = _unpack_small(_small_allreduce(_pack_small({n: grads[n] for n in small_names})), shapes)
    g_small["w_conv"] = lax.dynamic_slice(g_small["w_conv"], (0, quarter * wq_conv), (3, wq_conv))[None]
    flat2d = lambda t: t.reshape(-1, t.shape[-1])
    updated = _adamw_small(*[[flat2d(src[p + n]) for n in small_names] for src, p in
                             ((args, ""), (g_small, ""), (args, "m_"), (args, "v_"))])
    for n, g in g_small.items():
        out["grad_" + n] = g
    for prefix, arrs in zip(("delta_", "new_m_", "new_v_"), updated):
        for n, t in zip(small_names, arrs):
            out[prefix + n] = t.reshape(args[n].shape)

    weights = ["norm_mix_g", "w_in", "f_bias", "sg_ln_g", "sg_w", "sg_b", "w_out", "norm_ffn_g", "w_up", "w_conv", "b_conv",
               "w_down", "norm_final_g"]
    return (out["loss"], out["grad_x"], *[out[p + n] for p in ("grad_", "delta_", "new_m_", "new_v_") for n in weights])
```

```python
import functools
import math

import jax
import jax.numpy as jnp
from jax import lax
from jax.experimental import pallas as pl
from jax.experimental.pallas import tpu as pltpu

F32 = jnp.float32
BF16 = jnp.bfloat16
MESH = pl.DeviceIdType.MESH

D_MODEL = 1024
N_HEADS = 8
HEAD_DIM = 64
D_HEADS = N_HEADS * HEAD_DIM
SG_BLOCK = 128
CHUNK = 64
D_FF = 2816
D_IN = 2 * D_HEADS + 3 * D_HEADS + N_HEADS
LANES = 128
SUBLANES = 8
D_IN_PAD = 5 * D_HEADS + LANES
EPS = 1e-6
SCALE = HEAD_DIM ** -0.5
NEG = -1e30
LOG2E = 1.4426950408889634
HEAD_PAD = LANES
D_PAD = N_HEADS * HEAD_PAD
Q_STAT = HEAD_DIM
K_STAT = HEAD_DIM + 3
L_STAT = HEAD_DIM + 6
GROUPS = 2
GROUP_HEADS = N_HEADS // GROUPS
GROUP_PAD = GROUP_HEADS * HEAD_PAD
KEY_CHUNK = 256
STAT_ROWS = 16
FF_CHUNK = 256

ADAM_LR = 0.001
ADAM_B1 = 0.9
ADAM_B2 = 0.999
ADAM_EPS = 1e-08
ADAM_WD = 0.01
ADAM_STEP = 10

VMEM_LIMIT = 56 * 1024 * 1024

NT = (((1,), (1,)), ((), ()))
TN = (((0,), (0,)), ((), ()))


def _params(sem):
    return pltpu.CompilerParams(dimension_semantics=sem, vmem_limit_bytes=VMEM_LIMIT)


def _full(shape):
    nd = len(shape)
    return pl.BlockSpec(shape, lambda *_: (0,) * nd)


def _row_tile(rows, target):
    best = None
    for t in range(SUBLANES, min(rows, target) + 1, SUBLANES):
        if rows % t == 0:
            best = t
    assert best is not None, rows
    return best


def _sigmoid(x):
    return 0.5 * jnp.tanh(0.5 * x) + 0.5


def _gelu(z):
    return 0.5 * z * (1.0 + lax.erf(z * (2.0 ** -0.5)))


def _gelu_grad(z):
    cdf = 0.5 * (1.0 + lax.erf(z * (2.0 ** -0.5)))
    pdf = jnp.exp(-0.5 * z * z) * (1.0 / math.sqrt(2.0 * math.pi))
    return cdf + z * pdf


def _split_dot(x, m):
    hi = x.astype(BF16)
    lo = (x - hi.astype(F32)).astype(BF16)
    return jnp.dot(hi, m, preferred_element_type=F32) + jnp.dot(lo, m, preferred_element_type=F32)


def _head_mask(h, rows):
    lane = lax.broadcasted_iota(jnp.int32, (rows, D_HEADS), 1)
    return (lane >= h * HEAD_DIM) & (lane < (h + 1) * HEAD_DIM)


def _rms_bwd(dh, x, g):
    r = lax.rsqrt(jnp.mean(x * x, axis=-1, keepdims=True) + EPS)
    xhat = x * r
    dg = jnp.sum(dh * xhat, axis=0, keepdims=True)
    dxhat = dh * g
    dx = r * (dxhat - xhat * jnp.mean(dxhat * xhat, axis=-1, keepdims=True))
    return dx, dg


def _in_proj(x, g1, w_in, tm):
    S = x.shape[0]
    nz = D_IN_PAD - LANES

    def body(x_ref, g_ref, w_ref, z_ref, f_ref, h_ref):
        xf = x_ref[...]
        r = lax.rsqrt(jnp.mean(xf * xf, axis=-1, keepdims=True) + EPS)
        h = (xf * r * g_ref[...]).astype(BF16)
        h_ref[...] = h
        zz = jnp.dot(h, w_ref[...], preferred_element_type=F32)
        z_ref[...] = zz[:, :nz].astype(BF16)
        f_ref[...] = zz[:, nz:]

    return pl.pallas_call(
        body, name="in_proj", grid=(S // tm,),
        in_specs=[pl.BlockSpec((tm, D_MODEL), lambda i: (i, 0)), _full((1, D_MODEL)), _full((D_MODEL, D_IN_PAD))],
        out_specs=[pl.BlockSpec((tm, nz), lambda i: (i, 0)), pl.BlockSpec((tm, LANES), lambda i: (i, 0)),
                   pl.BlockSpec((tm, D_MODEL), lambda i: (i, 0))],
        out_shape=[jax.ShapeDtypeStruct((S, nz), BF16), jax.ShapeDtypeStruct((S, LANES), F32),
                   jax.ShapeDtypeStruct((S, D_MODEL), BF16)],
        compiler_params=_params(("parallel",)),
    )(x, g1, w_in)


def _fox_prep(f, bias_row, tb):
    S = f.shape[0]

    def body(f_ref, b_ref, c_ref, carry):
        @pl.when(pl.program_id(0) == 0)
        def _():
            carry[...] = jnp.zeros_like(carry)

        xv = f_ref[...] + b_ref[...]
        lf = jnp.minimum(xv, 0.0) - jnp.log(1.0 + jnp.exp(-jnp.abs(xv)))
        r = lax.broadcasted_iota(jnp.int32, (tb, tb), 0)
        s = lax.broadcasted_iota(jnp.int32, (tb, tb), 1)
        tri = (r >= s).astype(F32)
        cs = jnp.dot(tri, lf, precision=lax.Precision.HIGHEST, preferred_element_type=F32) + carry[0:1, :]
        c_ref[...] = cs
        carry[...] = jnp.broadcast_to(cs[tb - 1:tb, :], carry.shape)

    return pl.pallas_call(
        body, name="fox_prep", grid=(S // tb,),
        in_specs=[pl.BlockSpec((tb, LANES), lambda i: (i, 0)), _full((1, LANES))],
        out_specs=pl.BlockSpec((tb, LANES), lambda i: (i, 0)),
        out_shape=jax.ShapeDtypeStruct((S, LANES), F32),
        scratch_shapes=[pltpu.VMEM((SUBLANES, LANES), F32)],
        compiler_params=_params(("arbitrary",)),
    )(f, bias_row)


def _attn_consts():
    col = jnp.arange(D_PAD)
    row = jnp.arange(D_HEADS)
    head = jnp.arange(LANES)
    place = (row[:, None] // HEAD_DIM == col[None, :] // HEAD_PAD) & (row[:, None] % HEAD_DIM == col[None, :] % HEAD_PAD)

    def stat(offset):
        return ((head[:, None] < N_HEADS) & (col[None, :] == head[:, None] * HEAD_PAD + offset)).astype(BF16)

    def ones(offsets):
        return sum((col % HEAD_PAD == o) for o in offsets).astype(F32).reshape(1, D_PAD)

    def pick(offset):
        gcol = jnp.arange(GROUP_PAD)
        return jnp.stack([((gcol[:, None] % HEAD_PAD == offset) & (head[None, :] == g * GROUP_HEADS + gcol[:, None] // HEAD_PAD))
                          for g in range(GROUPS)]).astype(BF16)

    place = place.astype(BF16)
    return {
        "place": place, "place_t": place.T, "place_t_group": place.T[:GROUP_PAD, :GROUP_HEADS * HEAD_DIM],
        "q_stat": jnp.stack([stat(Q_STAT + j) for j in range(3)]), "k_stat": jnp.stack([stat(K_STAT + j) for j in range(3)]),
        "d_stat": jnp.stack([stat(Q_STAT + j) for j in range(2)]), "l_stat": jnp.stack([stat(L_STAT + j)[:STAT_ROWS] for j in range(3)]),
        "q_ones": ones(range(K_STAT, K_STAT + 3)), "k_ones": ones(list(range(Q_STAT, Q_STAT + 3)) + list(range(L_STAT, L_STAT + 3))),
        "v_ones": ones(range(Q_STAT, Q_STAT + 2)),
        "pick_rows": pick(Q_STAT), "pick_cols": pick(K_STAT),
    }


def _split3(x):
    hi = x.astype(BF16)
    r = x - hi.astype(F32)
    mid = r.astype(BF16)
    return hi, mid, (r - mid.astype(F32)).astype(BF16)


def _split3_dot(x, m):
    return sum(jnp.dot(part, m, preferred_element_type=F32) for part in _split3(x))


def _attn_pack(z, c, k, tm):
    S = z.shape[0]

    def body(q_ref, k_ref, v_ref, c_ref, pl_ref, pt_ref, qs_ref, ks_ref, qo_ref, ko_ref, vo_ref, voc_ref,
             qa_ref, ka_ref, va_ref, vt_ref):
        place = pl_ref[...]
        q = (q_ref[...].astype(F32) * (SCALE * LOG2E)).astype(BF16)
        qa = jnp.dot(q, place, preferred_element_type=F32) + qo_ref[...]
        ka = jnp.dot(k_ref[...], place, preferred_element_type=F32) + ko_ref[...]
        for j, part in enumerate(_split3(c_ref[...] * LOG2E)):
            qa = qa + jnp.dot(part, qs_ref[j], preferred_element_type=F32)
            ka = ka - jnp.dot(part, ks_ref[j], preferred_element_type=F32)
        qa_ref[...] = qa.astype(BF16)
        ka_ref[...] = ka.astype(BF16)
        v = v_ref[...]
        va_ref[...] = (jnp.dot(v, place, preferred_element_type=F32) + vo_ref[...]).astype(BF16)
        vt_ref[...] = (lax.dot_general(pt_ref[...], v, NT, preferred_element_type=F32) + voc_ref[...]).astype(BF16)

    blk = lambda col: pl.BlockSpec((tm, D_HEADS), lambda i: (i, col))
    out = pl.BlockSpec((tm, D_PAD), lambda i: (i, 0))
    pad = jax.ShapeDtypeStruct((S, D_PAD), BF16)
    return pl.pallas_call(
        body, name="attn_pack", grid=(S // tm,),
        in_specs=[blk(2), blk(3), blk(4), pl.BlockSpec((tm, LANES), lambda i: (i, 0)), _full((D_HEADS, D_PAD)), _full((D_PAD, D_HEADS)),
                  _full((3, LANES, D_PAD)), _full((3, LANES, D_PAD)), _full((1, D_PAD)), _full((1, D_PAD)), _full((1, D_PAD)),
                  _full((D_PAD, 1))],
        out_specs=[out, out, out, pl.BlockSpec((None, D_PAD, tm), lambda i: (i, 0, 0))],
        out_shape=[pad, pad, pad, jax.ShapeDtypeStruct((S // tm, D_PAD, tm), BF16)],
        compiler_params=_params(("parallel",)),
    )(z, z, z, c, k["place"], k["place_t"], k["q_stat"], k["k_stat"], k["q_ones"], k["k_ones"], k["v_ones"], k["v_ones"].T)


def _attn_fwd(qa, ka, vat, place_t, tq, shards):
    S = qa.shape[0]
    n = S // tq
    ns = len(shards)
    hand_on_at = (2 * n) // 3

    pairs = [(q, k) for q in range(n) for k in range(q + 1)]
    q_of = jnp.asarray([q for q, _ in pairs], jnp.int32)
    k_of = jnp.asarray([k for _, k in pairs], jnp.int32)

    def body(q_of_ref, k_of_ref, q_ref, k_ref, vt_ref, pt_ref, *rest):
        o_ref, lse_ref = rest[ns:ns + 2]
        m_s, acc_s, ot_s = rest[2 * ns + 2:2 * ns + 5]
        s_s = rest[2 * ns + 5:2 * ns + 7]
        start, hand_on, finish = _gather_ops(rest[:ns], rest[ns + 2:2 * ns + 2], *rest[2 * ns + 7:])
        qi, ki = q_of_ref[pl.program_id(0)], k_of_ref[pl.program_id(0)]

        @pl.when((qi == 0) & (ki == 0))
        def _():
            start()

        @pl.when((qi == hand_on_at) & (ki == 0))
        def _():
            hand_on()

        @pl.when(ki == 0)
        def _():
            m_s[...] = jnp.full_like(m_s, NEG)
            acc_s[...] = jnp.zeros_like(acc_s)

        def step(diagonal):
            chunks = [slice(c * KEY_CHUNK, (c + 1) * KEY_CHUNK) for c in range(tq // KEY_CHUNK)]

            def scores(h, rows, slot):
                sl = slice(h * HEAD_PAD, (h + 1) * HEAD_PAD)
                st = lax.dot_general(k_ref[rows, sl], q_ref[:, sl], NT, preferred_element_type=F32)
                if diagonal:
                    key = rows.start + lax.broadcasted_iota(jnp.int32, (KEY_CHUNK, tq), 0)
                    query = lax.broadcasted_iota(jnp.int32, (KEY_CHUNK, tq), 1)
                    st = jnp.where(query >= key, st, NEG)
                s_s[slot][rows, :] = st
                return jnp.max(st, axis=0, keepdims=True)

            m_cur = functools.reduce(jnp.maximum, [scores(0, rows, 0) for rows in chunks])
            for h in range(N_HEADS):
                sl = slice(h * HEAD_PAD, (h + 1) * HEAD_PAD)
                slot = h % 2
                m_prev = m_s[h][0:1, :]
                m_new = jnp.maximum(m_prev, m_cur)
                acc = jnp.exp2(m_prev - m_new) * acc_s[h]
                m_next = []
                for rows in chunks:
                    if h + 1 < N_HEADS:
                        m_next.append(scores(h + 1, rows, 1 - slot))
                    pt = jnp.exp2(s_s[slot][rows, :] - m_new).astype(BF16)
                    acc = acc + jnp.dot(vt_ref[sl, rows], pt, preferred_element_type=F32)
                acc_s[h] = acc
                m_s[h] = jnp.broadcast_to(m_new, (SUBLANES, tq))
                if m_next:
                    m_cur = functools.reduce(jnp.maximum, m_next)

        @pl.when(ki < qi)
        def _():
            step(False)

        @pl.when(ki == qi)
        def _():
            step(True)
            lse_ref[...] = jnp.zeros_like(lse_ref)
            for h in range(N_HEADS):
                acc = acc_s[h]
                denom = acc[Q_STAT:Q_STAT + 1, :]
                ot_s[h * HEAD_PAD:(h + 1) * HEAD_PAD, :] = (acc / denom).astype(BF16)
                lse_ref[h:h + 1, :] = m_s[h][0:1, :] + jnp.log(denom) * LOG2E
            o_ref[...] = lax.dot_general(ot_s[...], pt_ref[...], TN, preferred_element_type=F32).astype(BF16)

        @pl.when((qi == n - 1) & (ki == n - 1))
        def _():
            finish()

    out = pl.pallas_call(
        body, name="attn_fwd",
        grid_spec=pltpu.PrefetchScalarGridSpec(
            num_scalar_prefetch=2, grid=(len(pairs),),
            in_specs=[pl.BlockSpec((tq, D_PAD), lambda i, qs, ks: (qs[i], 0)),
                      pl.BlockSpec((tq, D_PAD), lambda i, qs, ks: (ks[i], 0)),
                      pl.BlockSpec((None, D_PAD, tq), lambda i, qs, ks: (ks[i], 0, 0)),
                      pl.BlockSpec((D_PAD, D_HEADS), lambda i, qs, ks: (0, 0))]
            + [_ANY] * ns,
            out_specs=[pl.BlockSpec((tq, D_HEADS), lambda i, qs, ks: (qs[i], 0)),
                       pl.BlockSpec((STAT_ROWS, tq), lambda i, qs, ks: (0, qs[i]))] + [_ANY] * ns,
            scratch_shapes=[pltpu.VMEM((N_HEADS, SUBLANES, tq), F32), pltpu.VMEM((N_HEADS, HEAD_PAD, tq), F32),
                            pltpu.VMEM((D_PAD, tq), BF16), pltpu.VMEM((tq, tq), F32), pltpu.VMEM((tq, tq), F32)] + _gather_sems(ns)),
        out_shape=[jax.ShapeDtypeStruct((S, D_HEADS), BF16), jax.ShapeDtypeStruct((STAT_ROWS, S), F32)] + _gather_shapes(shards),
        compiler_params=_params(("arbitrary",)),
    )(q_of, k_of, qa, ka, vat, place_t, *shards)
    return out[0], out[1], out[2:]


def _layer_norm_heads(v, seg_avg):
    mu = _split_dot(v, seg_avg)
    d = v - mu
    var = _split_dot(d * d, seg_avg)
    rstd = lax.rsqrt(var + EPS)
    return d * rstd, rstd


def _gate_mix(vn_blk, w_ref, bias):
    acc = bias
    for h in range(N_HEADS):
        vh = jnp.where(_head_mask(h, SG_BLOCK), vn_blk, 0.0).astype(BF16)
        acc = acc + jnp.dot(w_ref[h], vh, preferred_element_type=F32)
    return acc


def _gate_fwd(z, w_mask, ln_row, b_full, seg_avg, tm):
    S = z.shape[0]

    def body(zu_ref, zv_ref, w_ref, ln_ref, b_ref, avg_ref, o_ref):
        u = _gelu(zu_ref[...].astype(F32))
        v = _gelu(zv_ref[...].astype(F32))
        vhat, _ = _layer_norm_heads(v, avg_ref[...])
        vn = vhat * ln_ref[...]
        for b in range(tm // SG_BLOCK):
            rows = slice(b * SG_BLOCK, (b + 1) * SG_BLOCK)
            mixed = _gate_mix(vn[rows], w_ref, b_ref[...])
            o_ref[rows, :] = (u[rows] * mixed).astype(BF16)

    return pl.pallas_call(
        body, name="gate_fwd", grid=(S // tm,),
        in_specs=[pl.BlockSpec((tm, D_HEADS), lambda i: (i, 0)), pl.BlockSpec((tm, D_HEADS), lambda i: (i, 1)),
                  _full((N_HEADS, SG_BLOCK, SG_BLOCK)), _full((1, D_HEADS)), _full((SG_BLOCK, D_HEADS)),
                  _full((D_HEADS, D_HEADS))],
        out_specs=pl.BlockSpec((tm, D_HEADS), lambda i: (i, 0)),
        out_shape=jax.ShapeDtypeStruct((S, D_HEADS), BF16),
        compiler_params=_params(("parallel",)),
    )(z, z, w_mask, ln_row, b_full, seg_avg)


def _mix_out(x, out_a, out_b, w_out, g2, tm):
    S = x.shape[0]

    def body(x_ref, a_ref, b_ref, w_ref, g_ref, x1_ref, h_ref):
        y = jnp.dot(a_ref[...], w_ref[:D_HEADS, :], preferred_element_type=F32)
        y = y + jnp.dot(b_ref[...], w_ref[D_HEADS:, :], preferred_element_type=F32)
        x1 = x_ref[...] + y
        x1_ref[...] = x1
        r = lax.rsqrt(jnp.mean(x1 * x1, axis=-1, keepdims=True) + EPS)
        h_ref[...] = (x1 * r * g_ref[...]).astype(BF16)

    row = lambda w: pl.BlockSpec((tm, w), lambda i: (i, 0))
    return pl.pallas_call(
        body, name="mix_out", grid=(S // tm,),
        in_specs=[row(D_MODEL), row(D_HEADS), row(D_HEADS), _full((D_MODEL, D_MODEL)), _full((1, D_MODEL))],
        out_specs=[row(D_MODEL), row(D_MODEL)],
        out_shape=[jax.ShapeDtypeStruct((S, D_MODEL), F32), jax.ShapeDtypeStruct((S, D_MODEL), BF16)],
        compiler_params=_params(("parallel",)),
    )(x, out_a, out_b, w_out, g2)


def _up_proj(h2, w_up_q, tm):
    S = h2.shape[0]
    nq, _, wq = w_up_q.shape

    def body(h_ref, w_ref, a_ref):
        a_ref[...] = jnp.dot(h_ref[...], w_ref[...], preferred_element_type=F32).astype(BF16)

    return pl.pallas_call(
        body, name="up_proj", grid=(nq, S // tm),
        in_specs=[pl.BlockSpec((tm, D_MODEL), lambda j, i: (i, 0)), pl.BlockSpec((None, D_MODEL, wq), lambda j, i: (j, 0, 0))],
        out_specs=pl.BlockSpec((tm, wq), lambda j, i: (i, j)),
        out_shape=jax.ShapeDtypeStruct((S, nq * wq), BF16),
        compiler_params=_params(("parallel", "parallel")),
    )(h2, w_up_q)


def _shift_down(a, halo, k):
    tm = a.shape[0]
    ra = pltpu.roll(a, k, 0)
    rh = pltpu.roll(halo, k, 0)
    row = lax.broadcasted_iota(jnp.int32, halo.shape, 0)
    top = jnp.where(row < k, rh, ra[0:SUBLANES])
    return jnp.concatenate([top, ra[SUBLANES:tm]], axis=0)


def _shift_up(a, halo, k):
    tm = a.shape[0]
    ra = pltpu.roll(a, tm - k, 0)
    rh = pltpu.roll(halo, SUBLANES - k, 0)
    row = lax.broadcasted_iota(jnp.int32, halo.shape, 0)
    bottom = jnp.where(row >= SUBLANES - k, rh, ra[tm - SUBLANES:tm])
    return jnp.concatenate([ra[0:tm - SUBLANES], bottom], axis=0)


def _shift_matrices(tm):
    row = lax.broadcasted_iota(jnp.int32, (tm, tm), 0)
    col = lax.broadcasted_iota(jnp.int32, (tm, tm), 1)
    return [(row == col + k).astype(BF16) for k in (1, 2)]


def _conv_taps(a, halo, first, shifts):
    tm = a.shape[0]
    halo = halo.astype(F32) * jnp.where(first, 0.0, 1.0)
    if shifts is None:
        a = a.astype(F32)
        return a, _shift_down(a, halo, 1), _shift_down(a, halo, 2)
    row8 = lax.broadcasted_iota(jnp.int32, halo.shape, 0)
    taps = [a.astype(F32)]
    for k, shift in zip((1, 2), shifts):
        down = jnp.dot(shift, a, preferred_element_type=F32)
        top = down[0:SUBLANES] + jnp.where(row8 < k, pltpu.roll(halo, k, 0), 0.0)
        taps.append(jnp.concatenate([top, down[SUBLANES:tm]], axis=0))
    return taps


def _conv_gate_val(refs, shifts, cols, first):
    ag_ref, av_ref, hg_ref, hv_ref, wg_ref, wv_ref, bg_ref, bv_ref = refs
    g0, g1, g2 = _conv_taps(ag_ref[:, cols], hg_ref[:, cols], first, shifts)
    gate = wg_ref[2:3, cols] * g0 + wg_ref[1:2, cols] * g1 + wg_ref[0:1, cols] * g2 + bg_ref[:, cols]
    v0, v1, v2 = _conv_taps(av_ref[:, cols], hv_ref[:, cols], first, shifts)
    val = wv_ref[2:3, cols] * v0 + wv_ref[1:2, cols] * v1 + wv_ref[0:1, cols] * v2 + bv_ref[:, cols]
    return gate, val, (g2, g1, g0), (v2, v1, v0)


_FF_CHUNKS = [slice(j * FF_CHUNK, (j + 1) * FF_CHUNK) for j in range(D_FF // FF_CHUNK)]


def _conv_specs(tm):
    step = tm // SUBLANES
    prev = lambda i: jnp.maximum(i * step - 1, 0)
    return [pl.BlockSpec((tm, D_FF), lambda i: (i, 0)), pl.BlockSpec((tm, D_FF), lambda i: (i, 1)),
            pl.BlockSpec((SUBLANES, D_FF), lambda i: (prev(i), 0)), pl.BlockSpec((SUBLANES, D_FF), lambda i: (prev(i), 1))]


def _ffn_fwd_loss(a, w_conv, b_conv, w_down, x1, g3, target, tm):
    S = x1.shape[0]

    def body(ag_ref, av_ref, hg_ref, hv_ref, wg_ref, wv_ref, bg_ref, bv_ref, wd_ref, x1_ref, g_ref, t_ref,
             dx2_ref, loss_ref, dg_ref):
        i = pl.program_id(0)

        @pl.when(i == 0)
        def _():
            loss_ref[...] = jnp.zeros_like(loss_ref)
            dg_ref[...] = jnp.zeros_like(dg_ref)

        x2 = x1_ref[...]
        for cols in _FF_CHUNKS:
            gate, val, _, _ = _conv_gate_val((ag_ref, av_ref, hg_ref, hv_ref, wg_ref, wv_ref, bg_ref, bv_ref), None, cols, i == 0)
            y = (gate * _sigmoid(gate) * val).astype(BF16)
            x2 = x2 + jnp.dot(y, wd_ref[cols, :], preferred_element_type=F32)
        r = lax.rsqrt(jnp.mean(x2 * x2, axis=-1, keepdims=True) + EPS)
        xhat = x2 * r
        gg = g_ref[...]
        err = xhat * gg - t_ref[...]
        loss_ref[...] += jnp.sum(err * err, axis=0, keepdims=True)
        dy = err * (1.0 / D_MODEL)
        dg_ref[...] += jnp.sum(dy * xhat, axis=0, keepdims=True)
        dxhat = dy * gg
        dx2_ref[...] = r * (dxhat - xhat * jnp.mean(dxhat * xhat, axis=-1, keepdims=True))

    row = lambda w: pl.BlockSpec((tm, w), lambda i: (i, 0))
    half = lambda r: [pl.BlockSpec((r, D_FF), lambda i: (0, 0)), pl.BlockSpec((r, D_FF), lambda i: (0, 1))]
    return pl.pallas_call(
        body, name="ffn_fwd_loss", grid=(S // tm,),
        in_specs=_conv_specs(tm) + half(3) + half(1) + [_full((D_FF, D_MODEL)), row(D_MODEL), _full((1, D_MODEL)), row(D_MODEL)],
        out_specs=[row(D_MODEL), _full((1, D_MODEL)), _full((1, D_MODEL))],
        out_shape=[jax.ShapeDtypeStruct((S, D_MODEL), F32), jax.ShapeDtypeStruct((1, D_MODEL), F32),
                   jax.ShapeDtypeStruct((1, D_MODEL), F32)],
        compiler_params=_params(("arbitrary",)),
    )(a, a, a, a, w_conv, w_conv, b_conv, b_conv, w_down, x1, g3, target)


def _ffn_bwd_gate(dx2, a, w_conv, b_conv, w_down, tm):
    S = dx2.shape[0]

    def body(dx_ref, ag_ref, av_ref, hg_ref, hv_ref, wg_ref, wv_ref, bg_ref, bv_ref, wd_ref,
             dc_ref, y_ref, dw_ref, db_ref):
        i = pl.program_id(0)

        @pl.when(i == 0)
        def _():
            dw_ref[...] = jnp.zeros_like(dw_ref)
            db_ref[...] = jnp.zeros_like(db_ref)

        dx = dx_ref[...].astype(BF16)
        shifts = _shift_matrices(tm)
        for cols in _FF_CHUNKS:
            gate, val, gtaps, vtaps = _conv_gate_val((ag_ref, av_ref, hg_ref, hv_ref, wg_ref, wv_ref, bg_ref, bv_ref), shifts, cols, i == 0)
            sg = _sigmoid(gate)
            act = gate * sg
            y_ref[:, cols] = (act * val).astype(BF16)
            dy = lax.dot_general(dx, wd_ref[cols, :], NT, preferred_element_type=F32)
            dgate = dy * val * (sg + act - act * sg)
            dval = dy * act
            for d, taps, out in ((dgate, gtaps, cols), (dval, vtaps, slice(D_FF + cols.start, D_FF + cols.stop))):
                dc_ref[:, out] = d.astype(BF16)
                db_ref[0:1, out] += jnp.sum(d, axis=0, keepdims=True)
                for j in range(3):
                    dw_ref[j:j + 1, out] += jnp.sum(d * taps[j], axis=0, keepdims=True)

    row = lambda w: pl.BlockSpec((tm, w), lambda i: (i, 0))
    half = lambda r: [pl.BlockSpec((r, D_FF), lambda i: (0, 0)), pl.BlockSpec((r, D_FF), lambda i: (0, 1))]
    return pl.pallas_call(
        body, name="ffn_bwd_gate", grid=(S // tm,),
        in_specs=[row(D_MODEL)] + _conv_specs(tm) + half(3) + half(1) + [_full((D_FF, D_MODEL))],
        out_specs=[row(2 * D_FF), row(D_FF), _full((SUBLANES, 2 * D_FF)), _full((1, 2 * D_FF))],
        out_shape=[jax.ShapeDtypeStruct((S, 2 * D_FF), BF16), jax.ShapeDtypeStruct((S, D_FF), BF16),
                   jax.ShapeDtypeStruct((SUBLANES, 2 * D_FF), F32), jax.ShapeDtypeStruct((1, 2 * D_FF), F32)],
        compiler_params=_params(("arbitrary",)),
    )(dx2, a, a, a, a, w_conv, w_conv, b_conv, b_conv, w_down)


def _conv_bwd(dc, w_conv, tm, tn):
    S, C = dc.shape
    step = tm // SUBLANES
    last_blk = S // SUBLANES - 1

    def body(d_ref, nx_ref, w_ref, o_ref):
        last = pl.program_id(0) == pl.num_programs(0) - 1
        row = lax.broadcasted_iota(jnp.int32, (tm, tm), 0)
        col = lax.broadcasted_iota(jnp.int32, (tm, tm), 1)
        row8 = lax.broadcasted_iota(jnp.int32, (SUBLANES, FF_CHUNK), 0)
        ups = [(row + k == col).astype(BF16) for k in (1, 2)]
        for c0 in range(0, tn, FF_CHUNK):
            cols = slice(c0, c0 + FF_CHUNK)
            d = d_ref[:, cols]
            nx = nx_ref[:, cols].astype(F32) * jnp.where(last, 0.0, 1.0)
            out = w_ref[2:3, cols] * d.astype(F32)
            for k, up in zip((1, 2), ups):
                moved = jnp.dot(up, d, preferred_element_type=F32)
                bottom = moved[tm - SUBLANES:tm] + jnp.where(row8 >= SUBLANES - k, pltpu.roll(nx, SUBLANES - k, 0), 0.0)
                out = out + w_ref[2 - k:3 - k, cols] * jnp.concatenate([moved[0:tm - SUBLANES], bottom], axis=0)
            o_ref[:, cols] = out.astype(BF16)

    return pl.pallas_call(
        body, name="conv_bwd", grid=(S // tm, C // tn),
        in_specs=[pl.BlockSpec((tm, tn), lambda i, j: (i, j)),
                  pl.BlockSpec((SUBLANES, tn), lambda i, j: (jnp.minimum((i + 1) * step, last_blk), j)),
                  pl.BlockSpec((3, tn), lambda i, j: (0, j))],
        out_specs=pl.BlockSpec((tm, tn), lambda i, j: (i, j)),
        out_shape=jax.ShapeDtypeStruct((S, C), BF16),
        compiler_params=_params(("parallel", "parallel")),
    )(dc, dc, w_conv)


def _matmul_tn(a, b, name, bm, bn, tk, col_a=0, col_b=0, quarters=None):
    S = a.shape[0]
    gm, gn = quarters if quarters else (1, 1)
    nk = S // tk

    def body(a_ref, b_ref, o_ref):
        @pl.when(pl.program_id(2) == 0)
        def _():
            o_ref[...] = jnp.zeros_like(o_ref)

        o_ref[...] += lax.dot_general(a_ref[...].astype(BF16), b_ref[...].astype(BF16), TN, preferred_element_type=F32)

    if quarters and gn > 1:
        out_spec = pl.BlockSpec((None, bm, bn), lambda i, j, k: (j, i, 0))
        out_shape = jax.ShapeDtypeStruct((gn, gm * bm, bn), F32)
    else:
        out_spec = pl.BlockSpec((bm, bn), lambda i, j, k: (i, j))
        out_shape = jax.ShapeDtypeStruct((gm * bm, gn * bn), F32)
    return pl.pallas_call(
        body, name=name, grid=(gm, gn, nk),
        in_specs=[pl.BlockSpec((tk, bm), lambda i, j, k: (k, col_a * gm + i)),
                  pl.BlockSpec((tk, bn), lambda i, j, k: (k, col_b * gn + j))],
        out_specs=out_spec, out_shape=out_shape,
        compiler_params=_params(("parallel", "parallel", "arbitrary")),
    )(a, b)


def _up_bwd(dact, w_up_q, x1, g2, dx2, tm):
    S = x1.shape[0]
    nq, _, wq = w_up_q.shape

    def body(d_ref, w_ref, x_ref, g_ref, dx2_ref, dx1_ref, dg_ref):
        @pl.when(pl.program_id(0) == 0)
        def _():
            dg_ref[...] = jnp.zeros_like(dg_ref)

        dh = jnp.zeros((tm, D_MODEL), F32)
        for j in range(nq):
            dh = dh + lax.dot_general(d_ref[:, j * wq:(j + 1) * wq], w_ref[j], NT, preferred_element_type=F32)
        dx, dg = _rms_bwd(dh, x_ref[...], g_ref[...])
        dg_ref[...] += dg
        dx1_ref[...] = dx2_ref[...] + dx

    row = lambda w: pl.BlockSpec((tm, w), lambda i: (i, 0))
    return pl.pallas_call(
        body, name="up_bwd", grid=(S // tm,),
        in_specs=[row(nq * wq), pl.BlockSpec((nq, D_MODEL, wq), lambda i: (0, 0, 0), pipeline_mode=pl.Buffered(1)),
                  row(D_MODEL), _full((1, D_MODEL)), row(D_MODEL)],
        out_specs=[row(D_MODEL), _full((1, D_MODEL))],
        out_shape=[jax.ShapeDtypeStruct((S, D_MODEL), F32), jax.ShapeDtypeStruct((1, D_MODEL), F32)],
        compiler_params=_params(("arbitrary",)),
    )(dact, w_up_q, x1, g2, dx2)


def _out_bwd(dx1, w_out, tm):
    S = dx1.shape[0]

    def body(d_ref, w_ref, o_ref):
        o_ref[...] = lax.dot_general(d_ref[...].astype(BF16), w_ref[...], NT, preferred_element_type=F32).astype(BF16)

    return pl.pallas_call(
        body, name="out_bwd", grid=(S // tm,),
        in_specs=[pl.BlockSpec((tm, D_MODEL), lambda i: (i, 0)), _full((D_MODEL, D_MODEL))],
        out_specs=pl.BlockSpec((tm, D_MODEL), lambda i: (i, 0)),
        out_shape=jax.ShapeDtypeStruct((S, D_MODEL), BF16),
        compiler_params=_params(("parallel",)),
    )(dx1, w_out)


def _gate_bwd(z, dcat, w_mask, w_mask_t, ln_row, b_full, seg_avg, head_ind, tm):
    S = z.shape[0]
    nb = tm // SG_BLOCK

    def body(zu_ref, zv_ref, do_ref, w_ref, wt_ref, ln_ref, b_ref, avg_ref, ind_ref,
             dzu_ref, dzv_ref, dw_ref, db_ref, dln_ref, dvn_s, dbf_s):
        i = pl.program_id(0)

        @pl.when(i == 0)
        def _():
            dw_ref[...] = jnp.zeros_like(dw_ref)
            dln_ref[...] = jnp.zeros_like(dln_ref)
            dbf_s[...] = jnp.zeros_like(dbf_s)

        zu = zu_ref[...].astype(F32)
        zv = zv_ref[...].astype(F32)
        u = _gelu(zu)
        v = _gelu(zv)
        avg = avg_ref[...]
        vhat, rstd = _layer_norm_heads(v, avg)
        ln = ln_ref[...]
        vn = vhat * ln
        for b in range(nb):
            rows = slice(b * SG_BLOCK, (b + 1) * SG_BLOCK)
            vn_b = vn[rows]
            mixed = _gate_mix(vn_b, w_ref, b_ref[...])
            do = do_ref[rows, :].astype(F32)
            dzu_ref[rows, :] = (do * mixed * _gelu_grad(zu[rows])).astype(BF16)
            dmix = do * u[rows]
            dbf_s[...] += dmix
            vn_bf = vn_b.astype(BF16)
            dvn = jnp.zeros((SG_BLOCK, D_HEADS), F32)
            for h in range(N_HEADS):
                dmh = jnp.where(_head_mask(h, SG_BLOCK), dmix, 0.0).astype(BF16)
                dw_ref[h] += lax.dot_general(dmh, vn_bf, NT, preferred_element_type=F32)
                dvn = dvn + jnp.dot(wt_ref[h], dmh, preferred_element_type=F32)
            dvn_s[rows, :] = dvn
        dvn = dvn_s[...]
        dln_ref[...] += jnp.sum(dvn * vhat, axis=0, keepdims=True)
        dvhat = dvn * ln
        dv = rstd * (dvhat - _split_dot(dvhat, avg) - vhat * _split_dot(dvhat * vhat, avg))
        dzv_ref[...] = (dv * _gelu_grad(zv)).astype(BF16)

        @pl.when(i == pl.num_programs(0) - 1)
        def _():
            r = lax.broadcasted_iota(jnp.int32, (SG_BLOCK, SG_BLOCK), 0) // CHUNK
            s = lax.broadcasted_iota(jnp.int32, (SG_BLOCK, SG_BLOCK), 1) // CHUNK
            for h in range(N_HEADS):
                dw_ref[h] = jnp.where(r >= s, dw_ref[h], 0.0)
            db_ref[...] = _split_dot(dbf_s[...], ind_ref[...])

    row = lambda col: pl.BlockSpec((tm, D_HEADS), lambda i: (i, col))
    wspec = _full((N_HEADS, SG_BLOCK, SG_BLOCK))
    return pl.pallas_call(
        body, name="gate_bwd", grid=(S // tm,),
        in_specs=[row(0), row(1), row(0), wspec, wspec, _full((1, D_HEADS)), _full((SG_BLOCK, D_HEADS)),
                  _full((D_HEADS, D_HEADS)), _full((D_HEADS, LANES))],
        out_specs=[row(0), row(0), wspec, _full((SG_BLOCK, LANES)), _full((1, D_HEADS))],
        out_shape=[jax.ShapeDtypeStruct((S, D_HEADS), BF16), jax.ShapeDtypeStruct((S, D_HEADS), BF16),
                   jax.ShapeDtypeStruct((N_HEADS, SG_BLOCK, SG_BLOCK), F32), jax.ShapeDtypeStruct((SG_BLOCK, LANES), F32),
                   jax.ShapeDtypeStruct((1, D_HEADS), F32)],
        scratch_shapes=[pltpu.VMEM((tm, D_HEADS), F32), pltpu.VMEM((SG_BLOCK, D_HEADS), F32)],
        compiler_params=_params(("arbitrary",)),
    )(z, z, dcat, w_mask, w_mask_t, ln_row, b_full, seg_avg, head_ind)


def _attn_pack_grad(o, dcat, qa, lse, head_ind, k, tm):
    S = o.shape[0]

    def body(o_ref, do_ref, qa_ref, lse_ref, ind_ref, pl_ref, pt_ref, eye_ref, ds_ref, dst_ref, ls_ref, lst_ref,
             dop_ref, qb_ref, dot_ref, qbt_ref):
        do = do_ref[...]
        delta = _split_dot(o_ref[...].astype(F32) * do.astype(F32), ind_ref[...])
        hi = delta.astype(BF16)
        lo = (delta - hi.astype(F32)).astype(BF16)
        dop = jnp.dot(do, pl_ref[...], preferred_element_type=F32)
        dop = dop - jnp.dot(hi, ds_ref[0], preferred_element_type=F32) - jnp.dot(lo, ds_ref[1], preferred_element_type=F32)
        for g in range(GROUPS):
            dop_ref[g] = dop[:, g * GROUP_PAD:(g + 1) * GROUP_PAD].astype(BF16)
        dot = lax.dot_general(pt_ref[...], do, NT, preferred_element_type=F32)
        dot = dot - lax.dot_general(dst_ref[0], hi, NT, preferred_element_type=F32)
        dot = dot - lax.dot_general(dst_ref[1], lo, NT, preferred_element_type=F32)
        dot_ref[...] = dot.astype(BF16)
        qa = qa_ref[...]
        qb = qa.astype(F32)
        qbt = lax.dot_general(eye_ref[...], qa, NT, preferred_element_type=F32)
        for j, part in enumerate(_split3(lse_ref[...])):
            qb = qb - lax.dot_general(part, ls_ref[j], TN, preferred_element_type=F32)
            qbt = qbt - jnp.dot(lst_ref[j], part, preferred_element_type=F32)
        for g in range(GROUPS):
            qb_ref[g] = qb[:, g * GROUP_PAD:(g + 1) * GROUP_PAD].astype(BF16)
        qbt_ref[...] = qbt.astype(BF16)

    pad = pl.BlockSpec((tm, D_PAD), lambda i: (i, 0))
    padt = pl.BlockSpec((None, D_PAD, tm), lambda i: (i, 0, 0))
    return pl.pallas_call(
        body, name="attn_pack_grad", grid=(S // tm,),
        in_specs=[pl.BlockSpec((tm, D_HEADS), lambda i: (i, 0)), pl.BlockSpec((tm, D_HEADS), lambda i: (i, 1)), pad,
                  pl.BlockSpec((STAT_ROWS, tm), lambda i: (0, i)), _full((D_HEADS, LANES)), _full((D_HEADS, D_PAD)),
                  _full((D_PAD, D_HEADS)), _full((D_PAD, D_PAD)), _full((2, LANES, D_PAD)), _full((2, D_PAD, LANES)),
                  _full((3, STAT_ROWS, D_PAD)), _full((3, D_PAD, STAT_ROWS))],
        out_specs=[pl.BlockSpec((GROUPS, tm, GROUP_PAD), lambda i: (0, i, 0))] * 2 + [padt, padt],
        out_shape=[jax.ShapeDtypeStruct((GROUPS, S, GROUP_PAD), BF16)] * 2 + [jax.ShapeDtypeStruct((S // tm, D_PAD, tm), BF16)] * 2,
        compiler_params=_params(("parallel",)),
    )(o, dcat, qa, lse, head_ind, k["place"], k["place_t"], jnp.eye(D_PAD, dtype=BF16), k["d_stat"],
      jnp.swapaxes(k["d_stat"], 1, 2), k["l_stat"], jnp.swapaxes(k["l_stat"], 1, 2))


def _attn_bwd(qb, qbt, ka, va, dop, dopt, k, tq, sums16):
    S = ka.shape[0]
    n = S // tq
    ns = len(sums16)

    pairs = [(kb, q) for kb in range(n) for q in range(kb, n)]
    k_of = jnp.asarray([kb for kb, _ in pairs], jnp.int32)
    q_of = jnp.asarray([q for _, q in pairs], jnp.int32)

    def body(k_of_ref, q_of_ref, q_ref, qt_ref, k_ref, v_ref, do_ref, dot_ref, pt_ref, pick_ref, *rest):
        dq_hbm, dk_ref, dv_ref, dcc_ref = rest[ns:ns + 4]
        dq_s, dk_s, dv_s = rest[2 * ns + 4:2 * ns + 7]
        s_s, d_s = rest[2 * ns + 7:2 * ns + 9], rest[2 * ns + 9:2 * ns + 11]
        sem = rest[2 * ns + 11]
        scatter_start, scatter_finish = _scatter_ops(rest[:ns], rest[ns + 4:2 * ns + 4], *rest[2 * ns + 12:])
        g = pl.program_id(0)
        ki, qi = k_of_ref[pl.program_id(1)], q_of_ref[pl.program_id(1)]

        @pl.when((g == 0) & (ki == 0) & (qi == 0))
        def _():
            scatter_start()

        @pl.when((ki == 0) & (qi == 0))
        def _():
            dq_s[...] = jnp.zeros_like(dq_s)

        @pl.when(qi == ki)
        def _():
            dk_s[...] = jnp.zeros_like(dk_s)
            dv_s[...] = jnp.zeros_like(dv_s)

        def step(diagonal):
            chunks = [slice(c * KEY_CHUNK, (c + 1) * KEY_CHUNK) for c in range(tq // KEY_CHUNK)]

            def scores(hh, rows, slot):
                sl = slice(hh * HEAD_PAD, (hh + 1) * HEAD_PAD)
                s_s[slot][rows, :] = lax.dot_general(q_ref[rows, sl], k_ref[:, sl], NT, preferred_element_type=F32)
                d_s[slot][rows, :] = lax.dot_general(do_ref[rows, sl], v_ref[:, sl], NT, preferred_element_type=F32)

            for rows in chunks:
                scores(0, rows, 0)
            for hh in range(GROUP_HEADS):
                sl = slice(hh * HEAD_PAD, (hh + 1) * HEAD_PAD)
                slot = hh % 2
                dv, dk = dv_s[sl, :], dk_s[sl, :]
                for rows in chunks:
                    if hh + 1 < GROUP_HEADS:
                        scores(hh + 1, rows, 1 - slot)
                    p = jnp.exp2(s_s[slot][rows, :])
                    if diagonal:
                        row = rows.start + lax.broadcasted_iota(jnp.int32, (KEY_CHUNK, tq), 0)
                        col = lax.broadcasted_iota(jnp.int32, (KEY_CHUNK, tq), 1)
                        p = jnp.where(row >= col, p, 0.0)
                    ds = (p * d_s[slot][rows, :]).astype(BF16)
                    dv = dv + jnp.dot(dot_ref[sl, rows], p.astype(BF16), preferred_element_type=F32)
                    dk = dk + jnp.dot(qt_ref[sl, rows], ds, preferred_element_type=F32)
                    qrows = pl.ds(pl.multiple_of(qi * tq + rows.start, KEY_CHUNK), KEY_CHUNK)
                    dq_s[qrows, sl] += jnp.dot(ds, k_ref[:, sl], preferred_element_type=F32)
                dv_s[sl, :] = dv
                dk_s[sl, :] = dk

        @pl.when(qi > ki)
        def _():
            step(False)

        @pl.when(qi == ki)
        def _():
            step(True)

        @pl.when(qi == n - 1)
        def _():
            dk = dk_s[...]
            pt = pt_ref[...]
            dk_ref[...] = lax.dot_general((dk * (1.0 / LOG2E)).astype(BF16), pt, TN, preferred_element_type=F32).astype(BF16)
            dv_ref[...] = lax.dot_general(dv_s[...].astype(BF16), pt, TN, preferred_element_type=F32).astype(BF16)
            dcc_ref[...] = sum(lax.dot_general(part, pick_ref[...], TN, preferred_element_type=F32) for part in _split3(dk))

        @pl.when((ki == n - 1) & (qi == n - 1))
        def _():
            cp = pltpu.make_async_copy(dq_s, dq_hbm.at[g], sem)
            cp.start()
            cp.wait()

        @pl.when((g == GROUPS - 1) & (ki == n - 1) & (qi == n - 1))
        def _():
            scatter_finish()

    gw = GROUP_HEADS * HEAD_DIM
    qspec = pl.BlockSpec((None, tq, GROUP_PAD), lambda g, i, ks, qs: (g, qs[i], 0))
    qtspec = pl.BlockSpec((None, GROUP_PAD, tq), lambda g, i, ks, qs: (qs[i], g, 0))
    kspec = pl.BlockSpec((tq, GROUP_PAD), lambda g, i, ks, qs: (ks[i], g))
    kout = pl.BlockSpec((tq, gw), lambda g, i, ks, qs: (ks[i], g))
    out = pl.pallas_call(
        body, name="attn_bwd",
        grid_spec=pltpu.PrefetchScalarGridSpec(
            num_scalar_prefetch=2, grid=(GROUPS, len(pairs)),
            in_specs=[qspec, qtspec, kspec, kspec, qspec, qtspec, pl.BlockSpec((GROUP_PAD, gw), lambda g, i, ks, qs: (0, 0)),
                      pl.BlockSpec((None, GROUP_PAD, LANES), lambda g, i, ks, qs: (g, 0, 0))] + [_ANY] * ns,
            out_specs=[_ANY, kout, kout, pl.BlockSpec((None, tq, LANES), lambda g, i, ks, qs: (g, ks[i], 0))] + [_ANY] * ns,
            scratch_shapes=[pltpu.VMEM((S, GROUP_PAD), F32), pltpu.VMEM((GROUP_PAD, tq), F32), pltpu.VMEM((GROUP_PAD, tq), F32),
                            pltpu.VMEM((tq, tq), F32), pltpu.VMEM((tq, tq), F32), pltpu.VMEM((tq, tq), F32),
                            pltpu.VMEM((tq, tq), F32), pltpu.SemaphoreType.DMA]
            + _scatter_sems(ns)),
        out_shape=[jax.ShapeDtypeStruct((GROUPS, S, GROUP_PAD), F32), jax.ShapeDtypeStruct((S, D_HEADS), BF16),
                   jax.ShapeDtypeStruct((S, D_HEADS), BF16), jax.ShapeDtypeStruct((GROUPS, S, LANES), F32)]
        + _scatter_shapes(sums16),
        compiler_params=_params(("arbitrary", "arbitrary")),
    )(k_of, q_of, qb, qbt, ka, va, dop, dopt, k["place_t_group"], k["pick_cols"], *sums16)
    return out[0], out[1], out[2], out[3], out[4:]


def _attn_unpack(dqp, dcc, k, tm):
    S = dqp.shape[1]
    gw = GROUP_HEADS * HEAD_DIM

    def body(dqp_ref, dcc_ref, pt_ref, pick_ref, dq_ref, dc_ref):
        dc = jnp.zeros((tm, LANES), F32)
        for g in range(GROUPS):
            x = dqp_ref[g]
            dq_ref[:, g * gw:(g + 1) * gw] = jnp.dot((x * SCALE).astype(BF16), pt_ref[...], preferred_element_type=F32).astype(BF16)
            dc = dc + _split3_dot(x, pick_ref[g]) - dcc_ref[g]
        dc_ref[...] = dc

    return pl.pallas_call(
        body, name="attn_unpack", grid=(S // tm,),
        in_specs=[pl.BlockSpec((GROUPS, tm, GROUP_PAD), lambda i: (0, i, 0)), pl.BlockSpec((GROUPS, tm, LANES), lambda i: (0, i, 0)),
                  _full((GROUP_PAD, gw)), _full((GROUPS, GROUP_PAD, LANES))],
        out_specs=[pl.BlockSpec((tm, D_HEADS), lambda i: (i, 0)), pl.BlockSpec((tm, LANES), lambda i: (i, 0))],
        out_shape=[jax.ShapeDtypeStruct((S, D_HEADS), BF16), jax.ShapeDtypeStruct((S, LANES), F32)],
        compiler_params=_params(("parallel",)),
    )(dqp, dcc, k["place_t_group"], k["pick_rows"])


def _fox_bwd(dc, f, bias_row, tb):
    S = f.shape[0]
    nb = S // tb

    def body(dc_ref, f_ref, b_ref, df_ref, dbias_ref, carry):
        @pl.when(pl.program_id(0) == 0)
        def _():
            carry[...] = jnp.zeros_like(carry)
            dbias_ref[...] = jnp.zeros_like(dbias_ref)

        r = lax.broadcasted_iota(jnp.int32, (tb, tb), 0)
        s = lax.broadcasted_iota(jnp.int32, (tb, tb), 1)
        tri = (s >= r).astype(F32)
        rc = jnp.dot(tri, dc_ref[...], precision=lax.Precision.HIGHEST, preferred_element_type=F32) + carry[0:1, :]
        carry[...] = jnp.broadcast_to(rc[0:1, :], carry.shape)
        lane = lax.broadcasted_iota(jnp.int32, (tb, LANES), 1)
        df = jnp.where(lane < N_HEADS, rc * jax.nn.sigmoid(-(f_ref[...] + b_ref[...])), 0.0)
        df_ref[...] = df.astype(BF16)
        dbias_ref[...] += jnp.sum(df, axis=0, keepdims=True)

    rev = pl.BlockSpec((tb, LANES), lambda i: (nb - 1 - i, 0))
    return pl.pallas_call(
        body, name="fox_bwd", grid=(nb,),
        in_specs=[rev, rev, _full((1, LANES))],
        out_specs=[rev, _full((1, LANES))],
        out_shape=[jax.ShapeDtypeStruct((S, LANES), BF16), jax.ShapeDtypeStruct((1, LANES), F32)],
        scratch_shapes=[pltpu.VMEM((SUBLANES, LANES), F32)],
        compiler_params=_params(("arbitrary",)),
    )(dc, f, bias_row)


_DZ_WIDTHS = (D_HEADS,) * 5 + (LANES,)


def _in_bwd(pieces, w_in, x, g1, dx1, tm, sums16):
    S = x.shape[0]
    ns = len(sums16)

    def body(*refs):
        p_refs, (w_ref, x_ref, g_ref, dx1_ref) = refs[:6], refs[6:10]
        dx_ref, dg_ref = refs[10 + ns:12 + ns]
        scatter_start, scatter_finish = _scatter_ops(refs[10:10 + ns], refs[12 + ns:12 + 2 * ns], *refs[12 + 2 * ns:])

        @pl.when(pl.program_id(0) == 0)
        def _():
            dg_ref[...] = jnp.zeros_like(dg_ref)
            scatter_start()

        dh = jnp.zeros((tm, D_MODEL), F32)
        off = 0
        for p_ref, w in zip(p_refs, _DZ_WIDTHS):
            dh = dh + lax.dot_general(p_ref[...].astype(BF16), w_ref[:, off:off + w], NT, preferred_element_type=F32)
            off += w
        dx, dg = _rms_bwd(dh, x_ref[...], g_ref[...])
        dg_ref[...] += dg
        dx_ref[...] = dx1_ref[...] + dx

        @pl.when(pl.program_id(0) == pl.num_programs(0) - 1)
        def _():
            scatter_finish()

    row = lambda w: pl.BlockSpec((tm, w), lambda i: (i, 0))
    out = pl.pallas_call(
        body, name="in_bwd", grid=(S // tm,),
        in_specs=[row(w) for w in _DZ_WIDTHS] + [_full((D_MODEL, D_IN_PAD)), row(D_MODEL), _full((1, D_MODEL)), row(D_MODEL)]
        + [_ANY] * ns,
        out_specs=[row(D_MODEL), _full((1, D_MODEL))] + [_ANY] * ns,
        out_shape=[jax.ShapeDtypeStruct((S, D_MODEL), F32), jax.ShapeDtypeStruct((1, D_MODEL), F32)] + _scatter_shapes(sums16),
        scratch_shapes=_scatter_sems(ns),
        compiler_params=_params(("arbitrary",)),
    )(*pieces, w_in, x, g1, dx1, *sums16)
    return out[0], out[1], out[2:]


def _dw_in(h1, pieces, tk):
    S = h1.shape[0]

    def body(*refs):
        h_ref, p_refs, o_ref = refs[0], refs[1:7], refs[7]

        @pl.when(pl.program_id(0) == 0)
        def _():
            o_ref[...] = jnp.zeros_like(o_ref)

        off = 0
        for p_ref, w in zip(p_refs, _DZ_WIDTHS):
            o_ref[:, off:off + w] += lax.dot_general(h_ref[...], p_ref[...].astype(BF16), TN, preferred_element_type=F32)
            off += w

    row = lambda w: pl.BlockSpec((tk, w), lambda k: (k, 0))
    return pl.pallas_call(
        body, name="dw_in", grid=(S // tk,),
        in_specs=[row(D_MODEL)] + [row(w) for w in _DZ_WIDTHS],
        out_specs=_full((D_MODEL, D_IN_PAD)),
        out_shape=jax.ShapeDtypeStruct((D_MODEL, D_IN_PAD), F32),
        compiler_params=_params(("arbitrary",)),
    )(h1, *pieces)


def _adamw_math(w, g, m, v):
    m = ADAM_B1 * m + (1.0 - ADAM_B1) * g
    v = ADAM_B2 * v + (1.0 - ADAM_B2) * (g * g)
    m_hat = m / (1.0 - ADAM_B1 ** ADAM_STEP)
    v_hat = v / (1.0 - ADAM_B2 ** ADAM_STEP)
    delta = -ADAM_LR * (m_hat / (jnp.sqrt(v_hat) + ADAM_EPS) + ADAM_WD * w)
    return delta, m, v


def _adamw(name, w, g, m, v):
    R, C = w.shape
    tr = _row_tile(R, 256)

    def body(w_ref, g_ref, m_ref, v_ref, go_ref, d_ref, nm_ref, nv_ref):
        g = g_ref[...]
        d, nm, nv = _adamw_math(w_ref[...], g, m_ref[...], v_ref[...])
        go_ref[...] = g
        d_ref[...] = d
        nm_ref[...] = nm
        nv_ref[...] = nv

    spec = pl.BlockSpec((tr, C), lambda i: (i, 0))
    return pl.pallas_call(
        body, name=name, grid=(R // tr,), in_specs=[spec] * 4, out_specs=[spec] * 4,
        out_shape=[jax.ShapeDtypeStruct((R, C), F32)] * 4,
        compiler_params=_params(("parallel",)),
    )(w, g, m, v)


def _pair_sum(name, grad, theirs, ids):
    q, half, C = theirs.shape
    tr = _row_tile(half, 256)
    nb = half // tr

    def body(ids_ref, a_ref, b_ref, s_ref, sb_ref):
        s = a_ref[...] + b_ref[...]
        s_ref[...] = s
        sb_ref[...] = s.astype(BF16)

    here = pl.BlockSpec((None, tr, C), lambda j, i, ids: (j, i, 0))
    return pl.pallas_call(
        body, name=name,
        grid_spec=pltpu.PrefetchScalarGridSpec(
            num_scalar_prefetch=1, grid=(q, nb),
            in_specs=[pl.BlockSpec((None, tr, C), lambda j, i, ids: (j, ids[1] * nb + i, 0)), here],
            out_specs=[here, here]),
        out_shape=[jax.ShapeDtypeStruct((q, half, C), F32), jax.ShapeDtypeStruct((q, half, C), BF16)],
        compiler_params=_params(("parallel", "parallel")),
    )(ids, grad, theirs)


def _chip_sum(name, sums32, others, ids):
    _, half, C = sums32.shape
    tr = _row_tile(half, 256)
    nb = half // tr

    def body(ids_ref, a_ref, o_ref, s_ref):
        s = a_ref[...]
        for j in range(3):
            s = s + o_ref[j].astype(F32)
        s_ref[...] = s

    return pl.pallas_call(
        body, name=name,
        grid_spec=pltpu.PrefetchScalarGridSpec(
            num_scalar_prefetch=1, grid=(nb,),
            in_specs=[pl.BlockSpec((None, tr, C), lambda i, ids: (ids[0], i, 0)),
                      pl.BlockSpec((3, tr, C), lambda i, ids: (0, i, 0))],
            out_specs=pl.BlockSpec((tr, C), lambda i, ids: (ids[1] * nb + i, 0))),
        out_shape=jax.ShapeDtypeStruct((2 * half, C), F32),
        compiler_params=_params(("parallel",)),
    )(ids, sums32, others)


def _place():
    return lax.axis_index("x"), lax.axis_index("y"), lax.axis_index("c")


def _other_chips(x, y):
    return [(1 - x, y), (x, 1 - y), (1 - x, 1 - y)]


_ANY = pl.BlockSpec(memory_space=pl.ANY)


def _gather_quarters(shards):
    n = len(shards)

    def body(*refs):
        start, hand_on, finish = _gather_ops(refs[:n], refs[n:2 * n], *refs[2 * n:])
        start()
        hand_on()
        finish()

    return pl.pallas_call(
        body, name="gather_weights",
        in_specs=[_ANY] * n, out_specs=[_ANY] * n,
        out_shape=_gather_shapes(shards), scratch_shapes=_gather_sems(n),
    )(*shards)


def _gather_shapes(shards):
    return [jax.ShapeDtypeStruct((4,) + s.shape, s.dtype) for s in shards]


def _gather_sems(n):
    return [pltpu.SemaphoreType.DMA((n, 3))] * 4 + [pltpu.SemaphoreType.DMA((n,))]


def _gather_ops(ins, outs, send_sems, recv_sems, pass_send_sems, pass_recv_sems, own_sems):
    n = len(ins)
    halved = [r.shape[0] % 32 == 0 for r in ins]

    def part(a, quarter, core):
        if not halved[a]:
            return outs[a].at[quarter]
        half = ins[a].shape[0] // 2
        return outs[a].at[quarter, pl.ds(core * half, half), :]

    def ici(a, j, quarter):
        x, y, c = _place()
        px, py = _other_chips(x, y)[j]
        src = ins[a]
        if halved[a]:
            half = src.shape[0] // 2
            src = src.at[pl.ds(c * half, half), :]
        return pltpu.make_async_remote_copy(src_ref=src, dst_ref=part(a, quarter, c), send_sem=send_sems.at[a, j],
                                            recv_sem=recv_sems.at[a, j], device_id=(px, py, c), device_id_type=MESH)

    def passed(a, j, core):
        x, y, c = _place()
        px, py = _other_chips(x, y)[j]
        half = part(a, 2 * px + py, core)
        return pltpu.make_async_remote_copy(src_ref=half, dst_ref=half, send_sem=pass_send_sems.at[a, j],
                                            recv_sem=pass_recv_sems.at[a, j], device_id=(x, y, 1 - c), device_id_type=MESH)

    def own(a):
        x, y, _ = _place()
        return pltpu.make_async_copy(ins[a], outs[a].at[2 * x + y], own_sems.at[a])

    def start():
        x, y, _ = _place()
        for a in range(n):
            for j in range(3):
                ici(a, j, 2 * x + y).start()
            own(a).start()

    def hand_on():
        x, y, c = _place()
        for a in range(n):
            for j, (px, py) in enumerate(_other_chips(x, y)):
                ici(a, j, 2 * px + py).wait_recv()
                if halved[a]:
                    passed(a, j, c).start()

    def finish():
        x, y, c = _place()
        for a in range(n):
            for j in range(3):
                if halved[a]:
                    passed(a, j, 1 - c).wait_recv()
                    passed(a, j, c).wait_send()
                ici(a, j, 2 * x + y).wait_send()
            own(a).wait()

    return start, hand_on, finish


def _swap_halves(grads, name):
    n = len(grads)

    def body(*refs):
        ins, outs = refs[:n], refs[n:2 * n]
        send_sems, recv_sems = refs[2 * n:]
        x, y, c = _place()
        started = []
        for a in range(n):
            half = ins[a].shape[1] // 2
            cp = pltpu.make_async_remote_copy(src_ref=ins[a].at[:, pl.ds((1 - c) * half, half), :], dst_ref=outs[a],
                                              send_sem=send_sems.at[a], recv_sem=recv_sems.at[a],
                                              device_id=(x, y, 1 - c), device_id_type=MESH)
            cp.start()
            started.append(cp)
        for cp in started:
            cp.wait()

    return pl.pallas_call(
        body, name=name,
        in_specs=[_ANY] * n, out_specs=[_ANY] * n,
        out_shape=[jax.ShapeDtypeStruct((4, g.shape[1] // 2, g.shape[2]), F32) for g in grads],
        scratch_shapes=[pltpu.SemaphoreType.DMA((n,)), pltpu.SemaphoreType.DMA((n,))],
    )(*grads)


def _scatter_shapes(sums16):
    return [jax.ShapeDtypeStruct((3,) + s.shape[1:], BF16) for s in sums16]


def _scatter_sems(n):
    return [pltpu.SemaphoreType.DMA((n, 3))] * 2


def _scatter_ops(ins, outs, send_sems, recv_sems):
    n = len(ins)

    def copy(a, j):
        x, y, c = _place()
        px, py = _other_chips(x, y)[j]
        return pltpu.make_async_remote_copy(src_ref=ins[a].at[2 * px + py], dst_ref=outs[a].at[j], send_sem=send_sems.at[a, j],
                                            recv_sem=recv_sems.at[a, j], device_id=(px, py, c), device_id_type=MESH)

    def start():
        for a in range(n):
            for j in range(3):
                copy(a, j).start()

    def finish():
        for a in range(n):
            for j in range(3):
                copy(a, j).wait()

    return start, finish


def _join_halves(fulls):
    n = len(fulls)

    def body(*refs):
        ins, outs = refs[:n], refs[n:2 * n]
        send_sems, recv_sems = refs[2 * n:]
        x, y, c = _place()
        started = []
        for a in range(n):
            half = ins[a].shape[0] // 2
            rows = pl.ds(c * half, half)
            cp = pltpu.make_async_remote_copy(src_ref=ins[a].at[rows, :], dst_ref=outs[a].at[rows, :], send_sem=send_sems.at[a],
                                              recv_sem=recv_sems.at[a], device_id=(x, y, 1 - c), device_id_type=MESH)
            cp.start()
            started.append(cp)
        for cp in started:
            cp.wait()

    return pl.pallas_call(
        body, name="join_halves",
        in_specs=[_ANY] * n, out_specs=[_ANY] * n,
        out_shape=[jax.ShapeDtypeStruct(f.shape, F32) for f in fulls],
        input_output_aliases={a: a for a in range(n)},
        scratch_shapes=[pltpu.SemaphoreType.DMA((n,)), pltpu.SemaphoreType.DMA((n,))],
    )(*fulls)


def _small_allreduce(g):
    R = g.shape[0]
    half = R // 2

    def body(g_ref, out_ref, other_s, chip_s, parts_s, send_sems, recv_sems):
        x, y, c = _place()
        mine = 2 * x + y
        rows = pl.ds(pl.multiple_of(c * half, SUBLANES), half)

        def to_other_core(src, dst, k):
            return pltpu.make_async_remote_copy(src_ref=src, dst_ref=dst, send_sem=send_sems.at[k], recv_sem=recv_sems.at[k],
                                                device_id=(x, y, 1 - c), device_id_type=MESH)

        swap = to_other_core(g_ref, other_s, 0)
        swap.start()
        swap.wait()
        chip_s[...] = g_ref[...] + other_s[...]
        parts_s[mine] = chip_s[rows, :]
        sends = []
        for j, (px, py) in enumerate(_other_chips(x, y)):
            cp = pltpu.make_async_remote_copy(src_ref=chip_s.at[rows, :], dst_ref=parts_s.at[mine], send_sem=send_sems.at[1 + j],
                                              recv_sem=recv_sems.at[1 + j], device_id=(px, py, c), device_id_type=MESH)
            cp.start()
            sends.append(cp)
        for cp in sends:
            cp.wait()
        out_ref[rows, :] = (parts_s[0] + parts_s[1]) + (parts_s[2] + parts_s[3])
        join = to_other_core(out_ref.at[rows, :], out_ref.at[rows, :], 4)
        join.start()
        join.wait()

    vm = pl.BlockSpec(memory_space=pltpu.VMEM)
    return pl.pallas_call(
        body, name="small_allreduce",
        in_specs=[vm], out_specs=vm, out_shape=jax.ShapeDtypeStruct((R, LANES), F32),
        scratch_shapes=[pltpu.VMEM((R, LANES), F32), pltpu.VMEM((R, LANES), F32), pltpu.VMEM((4, half, LANES), F32),
                        pltpu.SemaphoreType.DMA((5,)), pltpu.SemaphoreType.DMA((5,))],
        compiler_params=pltpu.CompilerParams(vmem_limit_bytes=VMEM_LIMIT),
    )(g)


def _adamw_small(ws, gs, ms, vs):
    n = len(ws)

    def body(*refs):
        for k in range(n):
            w_ref, g_ref, m_ref, v_ref = (refs[j * n + k] for j in range(4))
            d, nm, nv = _adamw_math(w_ref[...], g_ref[...], m_ref[...], v_ref[...])
            refs[4 * n + k][...] = d
            refs[5 * n + k][...] = nm
            refs[6 * n + k][...] = nv

    vm = pl.BlockSpec(memory_space=pltpu.VMEM)
    out = pl.pallas_call(
        body, name="adamw_small",
        in_specs=[vm] * (4 * n), out_specs=[vm] * (3 * n),
        out_shape=[jax.ShapeDtypeStruct(w.shape, F32) for w in ws] * 3,
        compiler_params=pltpu.CompilerParams(vmem_limit_bytes=VMEM_LIMIT),
    )(*ws, *gs, *ms, *vs)
    return out[:n], out[n:2 * n], out[2 * n:]


_SMALL = (("norm_mix_g", D_MODEL), ("f_bias", N_HEADS), ("sg_ln_g", D_HEADS), ("sg_w", N_HEADS * SG_BLOCK * SG_BLOCK),
          ("sg_b", N_HEADS * SG_BLOCK), ("norm_ffn_g", D_MODEL), ("w_conv", 3 * 2 * D_FF), ("b_conv", 2 * D_FF),
          ("norm_final_g", D_MODEL))


def _pack_small(parts):
    rows = []
    for name, size in _SMALL:
        flat = parts[name].reshape(-1).astype(F32)
        pad = (-size) % (SUBLANES * LANES)
        rows.append(jnp.pad(flat, (0, pad)).reshape(-1, LANES))
    packed = jnp.concatenate(rows, axis=0)
    return jnp.pad(packed, ((0, (-packed.shape[0]) % (2 * SUBLANES)), (0, 0)))


def _unpack_small(packed, shapes):
    out, r = {}, 0
    for name, size in _SMALL:
        nrows = (size + SUBLANES * LANES - 1) // (SUBLANES * LANES) * SUBLANES
        out[name] = packed[r:r + nrows].reshape(-1)[:size].reshape(shapes[name])
        r += nrows
    return out


def _local_step(x, target, g1, w_in, f_bias, sg_ln_g, sg_w, sg_b, g2, b_conv, g3, late_shards, ids):
    S = x.shape[0]
    tm = _row_tile(S, 512)
    tms = _row_tile(S, 256)
    tq = tm

    lane = jnp.arange(D_HEADS)
    seg_avg = jnp.where(lane[:, None] // HEAD_DIM == lane[None, :] // HEAD_DIM, 1.0 / HEAD_DIM, 0.0).astype(BF16)
    head_ind = (lane[:, None] // HEAD_DIM == jnp.arange(LANES)[None, :]).astype(BF16)
    pos_chunk = jnp.arange(SG_BLOCK) // CHUNK
    w_mask32 = jnp.where(pos_chunk[:, None] >= pos_chunk[None, :], sg_w, 0.0)
    w_mask = w_mask32.astype(BF16)
    w_mask_t = jnp.swapaxes(w_mask32, 1, 2).astype(BF16)
    ln_row = sg_ln_g.reshape(1, D_HEADS)
    b_full = jnp.repeat(sg_b.T, HEAD_DIM, axis=1)
    bias_row = jnp.pad(f_bias.reshape(1, N_HEADS), ((0, 0), (0, LANES - N_HEADS)))
    b_conv_row = b_conv.reshape(1, 2 * D_FF)

    z, f, h1 = _in_proj(x, g1, w_in, tm)
    c = _fox_prep(f, bias_row, _row_tile(S, 256))
    consts = _attn_consts()
    qa, ka, va, vat = _attn_pack(z, c, consts, tm)
    out_b, lse, gathered = _attn_fwd(qa, ka, vat, consts["place_t"], tq, late_shards)
    g_out, w_up_q, g_down, g_conv = gathered
    w_out = g_out.reshape(D_MODEL, D_MODEL)
    w_down = g_down.reshape(D_FF, D_MODEL)
    w_conv = jnp.concatenate([g_conv[q] for q in range(4)], axis=1)
    out_a = _gate_fwd(z, w_mask, ln_row, b_full, seg_avg, tm)
    x1, h2 = _mix_out(x, out_a, out_b, w_out, g2, tm)
    a = _up_proj(h2, w_up_q, tm)
    dx2, sq_err, dg3 = _ffn_fwd_loss(a, w_conv, b_conv_row, w_down, x1, g3, target, tms)

    dconv, y, dw_conv8, db_conv = _ffn_bwd_gate(dx2, a, w_conv, b_conv_row, w_down, tms)
    dact = _conv_bwd(dconv, w_conv, tms, D_FF)
    dw_down = _matmul_tn(y, dx2, "dw_down", D_FF // 2, D_MODEL, tm, quarters=(2, 1))
    dx1, dg2 = _up_bwd(dact, w_up_q, x1, g2, dx2, tm)
    dw_up_q = _matmul_tn(h2, dact, "dw_up", D_MODEL, 2 * D_FF // 4, tm, quarters=(1, 4))
    dcat = _out_bwd(dx1, w_out, tm)
    dw_out_a = _matmul_tn(out_a, dx1, "dw_out_a", D_HEADS, D_MODEL, tm)
    dw_out_b = _matmul_tn(out_b, dx1, "dw_out_b", D_HEADS, D_MODEL, tm)
    early = {"w_down": dw_down.reshape(4, D_FF // 4, D_MODEL), "w_up": dw_up_q,
             "w_out": jnp.concatenate([dw_out_a, dw_out_b], axis=0).reshape(4, D_MODEL // 4, D_MODEL)}
    early_sums = _chip_sums(early, ids, "early")
    dzu, dzv, dsg_w, dsg_b_t, dln = _gate_bwd(z, dcat, w_mask, w_mask_t, ln_row, b_full, seg_avg, head_ind, tm)
    dop, qb, dopt, qbt = _attn_pack_grad(out_b, dcat, qa, lse, head_ind, consts, tm)
    dqp, dk, dv, dcc, landed = _attn_bwd(qb, qbt, ka, va, dop, dopt, consts, tq, [s16 for _, s16 in early_sums.values()])
    early_parts = {k: (s32, got) for (k, (s32, _)), got in zip(early_sums.items(), landed)}
    dq, dc = _attn_unpack(dqp, dcc, consts, tm)
    df, dbias = _fox_bwd(dc, f, bias_row, _row_tile(S, 256))
    pieces = (dzu, dzv, dq, dk, dv, df)
    dw_in = _dw_in(h1, pieces, tm)[:, :D_IN].reshape(D_MODEL, 4, D_IN // 4).transpose(1, 0, 2)
    (w_in_sum, w_in_sum16), = _chip_sums({"w_in": dw_in}, ids, "late").values()
    dx, dg1, (w_in_landed,) = _in_bwd(pieces, w_in, x, g1, dx1, tm, [w_in_sum16])

    grads = {
        "norm_mix_g": dg1, "f_bias": dbias[:, :N_HEADS], "sg_ln_g": dln, "sg_w": dsg_w, "sg_b": dsg_b_t[:, :N_HEADS].T,
        "norm_ffn_g": dg2, "w_conv": dw_conv8[:3], "b_conv": db_conv, "norm_final_g": dg3,
    }
    return sq_err, dx, grads, {**early_parts, "w_in": (w_in_sum, w_in_landed)}


def _chip_sums(grads_q, ids, tag):
    names = list(grads_q)
    theirs = _swap_halves([grads_q[k] for k in names], "swap_halves_" + tag)
    return {k: _pair_sum("pair_sum_" + k, grads_q[k], t, ids) for k, t in zip(names, theirs)}


def _finish_reduction(parts, ids):
    names = list(parts)
    fulls = [_chip_sum("chip_sum_" + k, s32, got, ids) for k, (s32, got) in parts.items()]
    return dict(zip(names, _join_halves(fulls)))


def kernel(x, norm_mix_g, w_in, f_bias, sg_ln_g, sg_w, sg_b, w_out, norm_ffn_g, w_up, w_conv, b_conv, w_down, norm_final_g, loss_target, m_norm_mix_g, m_w_in, m_f_bias, m_sg_ln_g, m_sg_w, m_sg_b, m_w_out, m_norm_ffn_g, m_w_up, m_w_conv, m_b_conv, m_w_down, m_norm_final_g, v_norm_mix_g, v_w_in, v_f_bias, v_sg_ln_g, v_sg_w, v_sg_b, v_w_out, v_norm_ffn_g, v_w_up, v_w_conv, v_b_conv, v_w_down, v_norm_final_g):
    args = dict(locals())
    quarter = 2 * lax.axis_index("x") + lax.axis_index("y")
    ids = jnp.stack([quarter, lax.axis_index("c")]).astype(jnp.int32)
    wq_conv = w_conv.shape[-1]

    g_in = _gather_quarters([w_in[0].astype(BF16)])[0]
    w_in_full = jnp.pad(jnp.concatenate([g_in[q] for q in range(4)], axis=1), ((0, 0), (0, D_IN_PAD - D_IN)))
    late_shards = [w_out[0].astype(BF16), w_up[0].astype(BF16), w_down[0].astype(BF16), w_conv[0]]

    sq_err, dx, grads, parts = _local_step(
        x[0], loss_target[0], norm_mix_g, w_in_full, f_bias[0], sg_ln_g[0], sg_w[0], sg_b[0], norm_ffn_g, b_conv[0],
        norm_final_g.reshape(1, D_MODEL), late_shards, ids)
    loss = lax.psum(0.5 * jnp.sum(sq_err) / D_MODEL, ("x", "y", "c"))

    big = _finish_reduction(parts, ids)

    out = {"loss": loss, "grad_x": dx[None]}
    for k in ("w_in", "w_out", "w_up", "w_down"):
        g, d, nm, nv = _adamw("adamw_" + k, args[k][0], big[k], args["m_" + k][0], args["v_" + k][0])
        out["grad_" + k], out["delta_" + k], out["new_m_" + k], out["new_v_" + k] = g[None], d[None], nm[None], nv[None]

    small_names = [n for n, _ in _SMALL]
    shapes = {n: (3, 4 * wq_conv) if n == "w_conv" else args[n].shape for n in small_names}
    g_small = _unpack_small(_small_allreduce(_pack_small({n: grads[n] for n in small_names})), shapes)
    g_small["w_conv"] = lax.dynamic_slice(g_small["w_conv"], (0, quarter * wq_conv), (3, wq_conv))[None]
    flat2d = lambda t: t.reshape(-1, t.shape[-1])
    updated = _adamw_small(*[[flat2d(src[p + n]) for n in small_names] for src, p in
                             ((args, ""), (g_small, ""), (args, "m_"), (args, "v_"))])
    for n, g in g_small.items():
        out["grad_" + n] = g
    for prefix, arrs in zip(("delta_", "new_m_", "new_v_"), updated):
        for n, t in zip(small_names, arrs):
            out[prefix + n] = t.reshape(args[n].shape)

    weights = ["norm_mix_g", "w_in", "f_bias", "sg_ln_g", "sg_w", "sg_b", "w_out", "norm_ffn_g", "w_up", "w_conv", "b_conv",
               "w_down", "norm_final_g"]
    return (out["loss"], out["grad_x"], *[out[p + n] for p in ("grad_", "delta_", "new_m_", "new_v_") for n in weights])
```

```python
import functools
import math

import jax
import jax.numpy as jnp
from jax import lax
from jax.experimental import pallas as pl
from jax.experimental.pallas import tpu as pltpu

F32 = jnp.float32
BF16 = jnp.bfloat16
MESH = pl.DeviceIdType.MESH

D_MODEL = 1024
N_HEADS = 8
HEAD_DIM = 64
D_HEADS = N_HEADS * HEAD_DIM
SG_BLOCK = 128
CHUNK = 64
D_FF = 2816
D_IN = 2 * D_HEADS + 3 * D_HEADS + N_HEADS
LANES = 128
SUBLANES = 8
D_IN_PAD = 5 * D_HEADS + LANES
EPS = 1e-6
SCALE = HEAD_DIM ** -0.5
NEG = -1e30
LOG2E = 1.4426950408889634
HEAD_PAD = LANES
D_PAD = N_HEADS * HEAD_PAD
Q_STAT = HEAD_DIM
K_STAT = HEAD_DIM + 3
L_STAT = HEAD_DIM + 6
GROUPS = 2
GROUP_HEADS = N_HEADS // GROUPS
GROUP_PAD = GROUP_HEADS * HEAD_PAD
KEY_CHUNK = 256
STAT_ROWS = 16
FF_CHUNK = 256

ADAM_LR = 0.001
ADAM_B1 = 0.9
ADAM_B2 = 0.999
ADAM_EPS = 1e-08
ADAM_WD = 0.01
ADAM_STEP = 10

VMEM_LIMIT = 56 * 1024 * 1024

NT = (((1,), (1,)), ((), ()))
TN = (((0,), (0,)), ((), ()))


def _params(sem):
    return pltpu.CompilerParams(dimension_semantics=sem, vmem_limit_bytes=VMEM_LIMIT)


def _full(shape):
    nd = len(shape)
    return pl.BlockSpec(shape, lambda *_: (0,) * nd)


def _row_tile(rows, target):
    best = None
    for t in range(SUBLANES, min(rows, target) + 1, SUBLANES):
        if rows % t == 0:
            best = t
    assert best is not None, rows
    return best


def _sigmoid(x):
    return 0.5 * jnp.tanh(0.5 * x) + 0.5


def _gelu(z):
    return 0.5 * z * (1.0 + lax.erf(z * (2.0 ** -0.5)))


def _gelu_grad(z):
    cdf = 0.5 * (1.0 + lax.erf(z * (2.0 ** -0.5)))
    pdf = jnp.exp(-0.5 * z * z) * (1.0 / math.sqrt(2.0 * math.pi))
    return cdf + z * pdf


def _split_dot(x, m):
    hi = x.astype(BF16)
    lo = (x - hi.astype(F32)).astype(BF16)
    return jnp.dot(hi, m, preferred_element_type=F32) + jnp.dot(lo, m, preferred_element_type=F32)


def _head_mask(h, rows):
    lane = lax.broadcasted_iota(jnp.int32, (rows, D_HEADS), 1)
    return (lane >= h * HEAD_DIM) & (lane < (h + 1) * HEAD_DIM)


def _rms_bwd(dh, x, g):
    r = lax.rsqrt(jnp.mean(x * x, axis=-1, keepdims=True) + EPS)
    xhat = x * r
    dg = jnp.sum(dh * xhat, axis=0, keepdims=True)
    dxhat = dh * g
    dx = r * (dxhat - xhat * jnp.mean(dxhat * xhat, axis=-1, keepdims=True))
    return dx, dg


def _in_proj(x, g1, w_in, tm):
    S = x.shape[0]
    nz = D_IN_PAD - LANES

    def body(x_ref, g_ref, w_ref, z_ref, f_ref, h_ref):
        xf = x_ref[...]
        r = lax.rsqrt(jnp.mean(xf * xf, axis=-1, keepdims=True) + EPS)
        h = (xf * r * g_ref[...]).astype(BF16)
        h_ref[...] = h
        zz = jnp.dot(h, w_ref[...], preferred_element_type=F32)
        z_ref[...] = zz[:, :nz].astype(BF16)
        f_ref[...] = zz[:, nz:]

    return pl.pallas_call(
        body, name="in_proj", grid=(S // tm,),
        in_specs=[pl.BlockSpec((tm, D_MODEL), lambda i: (i, 0)), _full((1, D_MODEL)), _full((D_MODEL, D_IN_PAD))],
        out_specs=[pl.BlockSpec((tm, nz), lambda i: (i, 0)), pl.BlockSpec((tm, LANES), lambda i: (i, 0)),
                   pl.BlockSpec((tm, D_MODEL), lambda i: (i, 0))],
        out_shape=[jax.ShapeDtypeStruct((S, nz), BF16), jax.ShapeDtypeStruct((S, LANES), F32),
                   jax.ShapeDtypeStruct((S, D_MODEL), BF16)],
        compiler_params=_params(("parallel",)),
    )(x, g1, w_in)


def _fox_prep(f, bias_row, tb):
    S = f.shape[0]

    def body(f_ref, b_ref, c_ref, carry):
        @pl.when(pl.program_id(0) == 0)
        def _():
            carry[...] = jnp.zeros_like(carry)

        xv = f_ref[...] + b_ref[...]
        lf = jnp.minimum(xv, 0.0) - jnp.log(1.0 + jnp.exp(-jnp.abs(xv)))
        r = lax.broadcasted_iota(jnp.int32, (tb, tb), 0)
        s = lax.broadcasted_iota(jnp.int32, (tb, tb), 1)
        tri = (r >= s).astype(F32)
        cs = jnp.dot(tri, lf, precision=lax.Precision.HIGHEST, preferred_element_type=F32) + carry[0:1, :]
        c_ref[...] = cs
        carry[...] = jnp.broadcast_to(cs[tb - 1:tb, :], carry.shape)

    return pl.pallas_call(
        body, name="fox_prep", grid=(S // tb,),
        in_specs=[pl.BlockSpec((tb, LANES), lambda i: (i, 0)), _full((1, LANES))],
        out_specs=pl.BlockSpec((tb, LANES), lambda i: (i, 0)),
        out_shape=jax.ShapeDtypeStruct((S, LANES), F32),
        scratch_shapes=[pltpu.VMEM((SUBLANES, LANES), F32)],
        compiler_params=_params(("arbitrary",)),
    )(f, bias_row)


def _attn_consts():
    col = jnp.arange(D_PAD)
    row = jnp.arange(D_HEADS)
    head = jnp.arange(LANES)
    place = (row[:, None] // HEAD_DIM == col[None, :] // HEAD_PAD) & (row[:, None] % HEAD_DIM == col[None, :] % HEAD_PAD)

    def stat(offset):
        return ((head[:, None] < N_HEADS) & (col[None, :] == head[:, None] * HEAD_PAD + offset)).astype(BF16)

    def ones(offsets):
        return sum((col % HEAD_PAD == o) for o in offsets).astype(F32).reshape(1, D_PAD)

    def pick(offset):
        gcol = jnp.arange(GROUP_PAD)
        return jnp.stack([((gcol[:, None] % HEAD_PAD == offset) & (head[None, :] == g * GROUP_HEADS + gcol[:, None] // HEAD_PAD))
                          for g in range(GROUPS)]).astype(BF16)

    place = place.astype(BF16)
    return {
        "place": place, "place_t": place.T, "place_t_group": place.T[:GROUP_PAD, :GROUP_HEADS * HEAD_DIM],
        "q_stat": jnp.stack([stat(Q_STAT + j) for j in range(3)]), "k_stat": jnp.stack([stat(K_STAT + j) for j in range(3)]),
        "d_stat": jnp.stack([stat(Q_STAT + j) for j in range(2)]), "l_stat": jnp.stack([stat(L_STAT + j)[:STAT_ROWS] for j in range(3)]),
        "q_ones": ones(range(K_STAT, K_STAT + 3)), "k_ones": ones(list(range(Q_STAT, Q_STAT + 3)) + list(range(L_STAT, L_STAT + 3))),
        "v_ones": ones(range(Q_STAT, Q_STAT + 2)),
        "pick_rows": pick(Q_STAT), "pick_cols": pick(K_STAT),
    }


def _split3(x):
    hi = x.astype(BF16)
    r = x - hi.astype(F32)
    mid = r.astype(BF16)
    return hi, mid, (r - mid.astype(F32)).astype(BF16)


def _split3_dot(x, m):
    return sum(jnp.dot(part, m, preferred_element_type=F32) for part in _split3(x))


def _attn_pack(z, c, k, tm):
    S = z.shape[0]

    def body(q_ref, k_ref, v_ref, c_ref, pl_ref, pt_ref, qs_ref, ks_ref, qo_ref, ko_ref, vo_ref, voc_ref,
             qa_ref, ka_ref, va_ref, vt_ref):
        place = pl_ref[...]
        q = (q_ref[...].astype(F32) * (SCALE * LOG2E)).astype(BF16)
        qa = jnp.dot(q, place, preferred_element_type=F32) + qo_ref[...]
        ka = jnp.dot(k_ref[...], place, preferred_element_type=F32) + ko_ref[...]
        for j, part in enumerate(_split3(c_ref[...] * LOG2E)):
            qa = qa + jnp.dot(part, qs_ref[j], preferred_element_type=F32)
            ka = ka - jnp.dot(part, ks_ref[j], preferred_element_type=F32)
        qa_ref[...] = qa.astype(BF16)
        ka_ref[...] = ka.astype(BF16)
        v = v_ref[...]
        va_ref[...] = (jnp.dot(v, place, preferred_element_type=F32) + vo_ref[...]).astype(BF16)
        vt_ref[...] = (lax.dot_general(pt_ref[...], v, NT, preferred_element_type=F32) + voc_ref[...]).astype(BF16)

    blk = lambda col: pl.BlockSpec((tm, D_HEADS), lambda i: (i, col))
    out = pl.BlockSpec((tm, D_PAD), lambda i: (i, 0))
    pad = jax.ShapeDtypeStruct((S, D_PAD), BF16)
    return pl.pallas_call(
        body, name="attn_pack", grid=(S // tm,),
        in_specs=[blk(2), blk(3), blk(4), pl.BlockSpec((tm, LANES), lambda i: (i, 0)), _full((D_HEADS, D_PAD)), _full((D_PAD, D_HEADS)),
                  _full((3, LANES, D_PAD)), _full((3, LANES, D_PAD)), _full((1, D_PAD)), _full((1, D_PAD)), _full((1, D_PAD)),
                  _full((D_PAD, 1))],
        out_specs=[out, out, out, pl.BlockSpec((None, D_PAD, tm), lambda i: (i, 0, 0))],
        out_shape=[pad, pad, pad, jax.ShapeDtypeStruct((S // tm, D_PAD, tm), BF16)],
        compiler_params=_params(("parallel",)),
    )(z, z, z, c, k["place"], k["place_t"], k["q_stat"], k["k_stat"], k["q_ones"], k["k_ones"], k["v_ones"], k["v_ones"].T)


def _attn_fwd(qa, ka, vat, place_t, tq, shards):
    S = qa.shape[0]
    n = S // tq
    ns = len(shards)
    hand_on_at = (2 * n) // 3

    pairs = [(q, k) for q in range(n) for k in range(q + 1)]
    q_of = jnp.asarray([q for q, _ in pairs], jnp.int32)
    k_of = jnp.asarray([k for _, k in pairs], jnp.int32)

    def body(q_of_ref, k_of_ref, q_ref, k_ref, vt_ref, pt_ref, *rest):
        o_ref, lse_ref = rest[ns:ns + 2]
        m_s, acc_s, ot_s = rest[2 * ns + 2:2 * ns + 5]
        s_s = rest[2 * ns + 5:2 * ns + 7]
        start, hand_on, finish = _gather_ops(rest[:ns], rest[ns + 2:2 * ns + 2], *rest[2 * ns + 7:])
        qi, ki = q_of_ref[pl.program_id(0)], k_of_ref[pl.program_id(0)]

        @pl.when((qi == 0) & (ki == 0))
        def _():
            start()

        @pl.when((qi == hand_on_at) & (ki == 0))
        def _():
            hand_on()

        @pl.when(ki == 0)
        def _():
            m_s[...] = jnp.full_like(m_s, NEG)
            acc_s[...] = jnp.zeros_like(acc_s)

        def step(diagonal):
            chunks = [slice(c * KEY_CHUNK, (c + 1) * KEY_CHUNK) for c in range(tq // KEY_CHUNK)]

            def scores(h, rows, slot):
                sl = slice(h * HEAD_PAD, (h + 1) * HEAD_PAD)
                st = lax.dot_general(k_ref[rows, sl], q_ref[:, sl], NT, preferred_element_type=F32)
                if diagonal:
                    key = rows.start + lax.broadcasted_iota(jnp.int32, (KEY_CHUNK, tq), 0)
                    query = lax.broadcasted_iota(jnp.int32, (KEY_CHUNK, tq), 1)
                    st = jnp.where(query >= key, st, NEG)
                s_s[slot][rows, :] = st
                return jnp.max(st, axis=0, keepdims=True)

            m_cur = functools.reduce(jnp.maximum, [scores(0, rows, 0) for rows in chunks])
            for h in range(N_HEADS):
                sl = slice(h * HEAD_PAD, (h + 1) * HEAD_PAD)
                slot = h % 2
                m_prev = m_s[h][0:1, :]
                m_new = jnp.maximum(m_prev, m_cur)
                acc = jnp.exp2(m_prev - m_new) * acc_s[h]
                m_next = []
                for rows in chunks:
                    if h + 1 < N_HEADS:
                        m_next.append(scores(h + 1, rows, 1 - slot))
                    pt = jnp.exp2(s_s[slot][rows, :] - m_new).astype(BF16)
                    acc = acc + jnp.dot(vt_ref[sl, rows], pt, preferred_element_type=F32)
                acc_s[h] = acc
                m_s[h] = jnp.broadcast_to(m_new, (SUBLANES, tq))
                if m_next:
                    m_cur = functools.reduce(jnp.maximum, m_next)

        @pl.when(ki < qi)
        def _():
            step(False)

        @pl.when(ki == qi)
        def _():
            step(True)
            lse_ref[...] = jnp.zeros_like(lse_ref)
            for h in range(N_HEADS):
                acc = acc_s[h]
                denom = acc[Q_STAT:Q_STAT + 1, :]
                ot_s[h * HEAD_PAD:(h + 1) * HEAD_PAD, :] = (acc / denom).astype(BF16)
                lse_ref[h:h + 1, :] = m_s[h][0:1, :] + jnp.log(denom) * LOG2E
            o_ref[...] = lax.dot_general(ot_s[...], pt_ref[...], TN, preferred_element_type=F32).astype(BF16)

        @pl.when((qi == n - 1) & (ki == n - 1))
        def _():
            finish()

    out = pl.pallas_call(
        body, name="attn_fwd",
        grid_spec=pltpu.PrefetchScalarGridSpec(
            num_scalar_prefetch=2, grid=(len(pairs),),
            in_specs=[pl.BlockSpec((tq, D_PAD), lambda i, qs, ks: (qs[i], 0)),
                      pl.BlockSpec((tq, D_PAD), lambda i, qs, ks: (ks[i], 0)),
                      pl.BlockSpec((None, D_PAD, tq), lambda i, qs, ks: (ks[i], 0, 0)),
                      pl.BlockSpec((D_PAD, D_HEADS), lambda i, qs, ks: (0, 0))]
            + [_ANY] * ns,
            out_specs=[pl.BlockSpec((tq, D_HEADS), lambda i, qs, ks: (qs[i], 0)),
                       pl.BlockSpec((STAT_ROWS, tq), lambda i, qs, ks: (0, qs[i]))] + [_ANY] * ns,
            scratch_shapes=[pltpu.VMEM((N_HEADS, SUBLANES, tq), F32), pltpu.VMEM((N_HEADS, HEAD_PAD, tq), F32),
                            pltpu.VMEM((D_PAD, tq), BF16), pltpu.VMEM((tq, tq), F32), pltpu.VMEM((tq, tq), F32)] + _gather_sems(ns)),
        out_shape=[jax.ShapeDtypeStruct((S, D_HEADS), BF16), jax.ShapeDtypeStruct((STAT_ROWS, S), F32)] + _gather_shapes(shards),
        compiler_params=_params(("arbitrary",)),
    )(q_of, k_of, qa, ka, vat, place_t, *shards)
    return out[0], out[1], out[2:]


def _layer_norm_heads(v, seg_avg):
    mu = _split_dot(v, seg_avg)
    d = v - mu
    var = _split_dot(d * d, seg_avg)
    rstd = lax.rsqrt(var + EPS)
    return d * rstd, rstd


def _gate_mix(vn_blk, w_ref, bias):
    acc = bias
    for h in range(N_HEADS):
        vh = jnp.where(_head_mask(h, SG_BLOCK), vn_blk, 0.0).astype(BF16)
        acc = acc + jnp.dot(w_ref[h], vh, preferred_element_type=F32)
    return acc


def _gate_fwd(z, w_mask, ln_row, b_full, seg_avg, tm):
    S = z.shape[0]

    def body(zu_ref, zv_ref, w_ref, ln_ref, b_ref, avg_ref, o_ref):
        u = _gelu(zu_ref[...].astype(F32))
        v = _gelu(zv_ref[...].astype(F32))
        vhat, _ = _layer_norm_heads(v, avg_ref[...])
        vn = vhat * ln_ref[...]
        for b in range(tm // SG_BLOCK):
            rows = slice(b * SG_BLOCK, (b + 1) * SG_BLOCK)
            mixed = _gate_mix(vn[rows], w_ref, b_ref[...])
            o_ref[rows, :] = (u[rows] * mixed).astype(BF16)

    return pl.pallas_call(
        body, name="gate_fwd", grid=(S // tm,),
        in_specs=[pl.BlockSpec((tm, D_HEADS), lambda i: (i, 0)), pl.BlockSpec((tm, D_HEADS), lambda i: (i, 1)),
                  _full((N_HEADS, SG_BLOCK, SG_BLOCK)), _full((1, D_HEADS)), _full((SG_BLOCK, D_HEADS)),
                  _full((D_HEADS, D_HEADS))],
        out_specs=pl.BlockSpec((tm, D_HEADS), lambda i: (i, 0)),
        out_shape=jax.ShapeDtypeStruct((S, D_HEADS), BF16),
        compiler_params=_params(("parallel",)),
    )(z, z, w_mask, ln_row, b_full, seg_avg)


def _mix_out(x, out_a, out_b, w_out, g2, tm):
    S = x.shape[0]

    def body(x_ref, a_ref, b_ref, w_ref, g_ref, x1_ref, h_ref):
        y = jnp.dot(a_ref[...], w_ref[:D_HEADS, :], preferred_element_type=F32)
        y = y + jnp.dot(b_ref[...], w_ref[D_HEADS:, :], preferred_element_type=F32)
        x1 = x_ref[...] + y
        x1_ref[...] = x1
        r = lax.rsqrt(jnp.mean(x1 * x1, axis=-1, keepdims=True) + EPS)
        h_ref[...] = (x1 * r * g_ref[...]).astype(BF16)

    row = lambda w: pl.BlockSpec((tm, w), lambda i: (i, 0))
    return pl.pallas_call(
        body, name="mix_out", grid=(S // tm,),
        in_specs=[row(D_MODEL), row(D_HEADS), row(D_HEADS), _full((D_MODEL, D_MODEL)), _full((1, D_MODEL))],
        out_specs=[row(D_MODEL), row(D_MODEL)],
        out_shape=[jax.ShapeDtypeStruct((S, D_MODEL), F32), jax.ShapeDtypeStruct((S, D_MODEL), BF16)],
        compiler_params=_params(("parallel",)),
    )(x, out_a, out_b, w_out, g2)


def _up_proj(h2, w_up_q, tm):
    S = h2.shape[0]
    nq, _, wq = w_up_q.shape

    def body(h_ref, w_ref, a_ref):
        a_ref[...] = jnp.dot(h_ref[...], w_ref[...], preferred_element_type=F32).astype(BF16)

    return pl.pallas_call(
        body, name="up_proj", grid=(nq, S // tm),
        in_specs=[pl.BlockSpec((tm, D_MODEL), lambda j, i: (i, 0)), pl.BlockSpec((None, D_MODEL, wq), lambda j, i: (j, 0, 0))],
        out_specs=pl.BlockSpec((tm, wq), lambda j, i: (i, j)),
        out_shape=jax.ShapeDtypeStruct((S, nq * wq), BF16),
        compiler_params=_params(("parallel", "parallel")),
    )(h2, w_up_q)


def _shift_down(a, halo, k):
    tm = a.shape[0]
    ra = pltpu.roll(a, k, 0)
    rh = pltpu.roll(halo, k, 0)
    row = lax.broadcasted_iota(jnp.int32, halo.shape, 0)
    top = jnp.where(row < k, rh, ra[0:SUBLANES])
    return jnp.concatenate([top, ra[SUBLANES:tm]], axis=0)


def _shift_up(a, halo, k):
    tm = a.shape[0]
    ra = pltpu.roll(a, tm - k, 0)
    rh = pltpu.roll(halo, SUBLANES - k, 0)
    row = lax.broadcasted_iota(jnp.int32, halo.shape, 0)
    bottom = jnp.where(row >= SUBLANES - k, rh, ra[tm - SUBLANES:tm])
    return jnp.concatenate([ra[0:tm - SUBLANES], bottom], axis=0)


def _shift_matrices(tm):
    row = lax.broadcasted_iota(jnp.int32, (tm, tm), 0)
    col = lax.broadcasted_iota(jnp.int32, (tm, tm), 1)
    return [(row == col + k).astype(BF16) for k in (1, 2)]


def _conv_taps(a, halo, first, shifts):
    tm = a.shape[0]
    halo = halo.astype(F32) * jnp.where(first, 0.0, 1.0)
    if shifts is None:
        a = a.astype(F32)
        return a, _shift_down(a, halo, 1), _shift_down(a, halo, 2)
    row8 = lax.broadcasted_iota(jnp.int32, halo.shape, 0)
    taps = [a.astype(F32)]
    for k, shift in zip((1, 2), shifts):
        down = jnp.dot(shift, a, preferred_element_type=F32)
        top = down[0:SUBLANES] + jnp.where(row8 < k, pltpu.roll(halo, k, 0), 0.0)
        taps.append(jnp.concatenate([top, down[SUBLANES:tm]], axis=0))
    return taps


def _conv_gate_val(refs, shifts, cols, first):
    ag_ref, av_ref, hg_ref, hv_ref, wg_ref, wv_ref, bg_ref, bv_ref = refs
    g0, g1, g2 = _conv_taps(ag_ref[:, cols], hg_ref[:, cols], first, shifts)
    gate = wg_ref[2:3, cols] * g0 + wg_ref[1:2, cols] * g1 + wg_ref[0:1, cols] * g2 + bg_ref[:, cols]
    v0, v1, v2 = _conv_taps(av_ref[:, cols], hv_ref[:, cols], first, shifts)
    val = wv_ref[2:3, cols] * v0 + wv_ref[1:2, cols] * v1 + wv_ref[0:1, cols] * v2 + bv_ref[:, cols]
    return gate, val, (g2, g1, g0), (v2, v1, v0)


_FF_CHUNKS = [slice(j * FF_CHUNK, (j + 1) * FF_CHUNK) for j in range(D_FF // FF_CHUNK)]


def _conv_specs(tm):
    step = tm // SUBLANES
    prev = lambda i: jnp.maximum(i * step - 1, 0)
    return [pl.BlockSpec((tm, D_FF), lambda i: (i, 0)), pl.BlockSpec((tm, D_FF), lambda i: (i, 1)),
            pl.BlockSpec((SUBLANES, D_FF), lambda i: (prev(i), 0)), pl.BlockSpec((SUBLANES, D_FF), lambda i: (prev(i), 1))]


def _ffn_fwd_loss(a, w_conv, b_conv, w_down, x1, g3, target, tm):
    S = x1.shape[0]

    def body(ag_ref, av_ref, hg_ref, hv_ref, wg_ref, wv_ref, bg_ref, bv_ref, wd_ref, x1_ref, g_ref, t_ref,
             dx2_ref, loss_ref, dg_ref):
        i = pl.program_id(0)

        @pl.when(i == 0)
        def _():
            loss_ref[...] = jnp.zeros_like(loss_ref)
            dg_ref[...] = jnp.zeros_like(dg_ref)

        x2 = x1_ref[...]
        for cols in _FF_CHUNKS:
            gate, val, _, _ = _conv_gate_val((ag_ref, av_ref, hg_ref, hv_ref, wg_ref, wv_ref, bg_ref, bv_ref), None, cols, i == 0)
            y = (gate * _sigmoid(gate) * val).astype(BF16)
            x2 = x2 + jnp.dot(y, wd_ref[cols, :], preferred_element_type=F32)
        r = lax.rsqrt(jnp.mean(x2 * x2, axis=-1, keepdims=True) + EPS)
        xhat = x2 * r
        gg = g_ref[...]
        err = xhat * gg - t_ref[...]
        loss_ref[...] += jnp.sum(err * err, axis=0, keepdims=True)
        dy = err * (1.0 / D_MODEL)
        dg_ref[...] += jnp.sum(dy * xhat, axis=0, keepdims=True)
        dxhat = dy * gg
        dx2_ref[...] = r * (dxhat - xhat * jnp.mean(dxhat * xhat, axis=-1, keepdims=True))

    row = lambda w: pl.BlockSpec((tm, w), lambda i: (i, 0))
    half = lambda r: [pl.BlockSpec((r, D_FF), lambda i: (0, 0)), pl.BlockSpec((r, D_FF), lambda i: (0, 1))]
    return pl.pallas_call(
        body, name="ffn_fwd_loss", grid=(S // tm,),
        in_specs=_conv_specs(tm) + half(3) + half(1) + [_full((D_FF, D_MODEL)), row(D_MODEL), _full((1, D_MODEL)), row(D_MODEL)],
        out_specs=[row(D_MODEL), _full((1, D_MODEL)), _full((1, D_MODEL))],
        out_shape=[jax.ShapeDtypeStruct((S, D_MODEL), F32), jax.ShapeDtypeStruct((1, D_MODEL), F32),
                   jax.ShapeDtypeStruct((1, D_MODEL), F32)],
        compiler_params=_params(("arbitrary",)),
    )(a, a, a, a, w_conv, w_conv, b_conv, b_conv, w_down, x1, g3, target)


def _ffn_bwd_gate(dx2, a, w_conv, b_conv, w_down, tm):
    S = dx2.shape[0]

    def body(dx_ref, ag_ref, av_ref, hg_ref, hv_ref, wg_ref, wv_ref, bg_ref, bv_ref, wd_ref,
             dc_ref, y_ref, dw_ref, db_ref):
        i = pl.program_id(0)

        @pl.when(i == 0)
        def _():
            dw_ref[...] = jnp.zeros_like(dw_ref)
            db_ref[...] = jnp.zeros_like(db_ref)

        dx = dx_ref[...].astype(BF16)
        shifts = _shift_matrices(tm)
        for cols in _FF_CHUNKS:
            gate, val, gtaps, vtaps = _conv_gate_val((ag_ref, av_ref, hg_ref, hv_ref, wg_ref, wv_ref, bg_ref, bv_ref), shifts, cols, i == 0)
            sg = _sigmoid(gate)
            act = gate * sg
            y_ref[:, cols] = (act * val).astype(BF16)
            dy = lax.dot_general(dx, wd_ref[cols, :], NT, preferred_element_type=F32)
            dgate = dy * val * (sg + act - act * sg)
            dval = dy * act
            for d, taps, out in ((dgate, gtaps, cols), (dval, vtaps, slice(D_FF + cols.start, D_FF + cols.stop))):
                dc_ref[:, out] = d.astype(BF16)
                db_ref[0:1, out] += jnp.sum(d, axis=0, keepdims=True)
                for j in range(3):
                    dw_ref[j:j + 1, out] += jnp.sum(d * taps[j], axis=0, keepdims=True)

    row = lambda w: pl.BlockSpec((tm, w), lambda i: (i, 0))
    half = lambda r: [pl.BlockSpec((r, D_FF), lambda i: (0, 0)), pl.BlockSpec((r, D_FF), lambda i: (0, 1))]
    return pl.pallas_call(
        body, name="ffn_bwd_gate", grid=(S // tm,),
        in_specs=[row(D_MODEL)] + _conv_specs(tm) + half(3) + half(1) + [_full((D_FF, D_MODEL))],
        out_specs=[row(2 * D_FF), row(D_FF), _full((SUBLANES, 2 * D_FF)), _full((1, 2 * D_FF))],
        out_shape=[jax.ShapeDtypeStruct((S, 2 * D_FF), BF16), jax.ShapeDtypeStruct((S, D_FF), BF16),
                   jax.ShapeDtypeStruct((SUBLANES, 2 * D_FF), F32), jax.ShapeDtypeStruct((1, 2 * D_FF), F32)],
        compiler_params=_params(("arbitrary",)),
    )(dx2, a, a, a, a, w_conv, w_conv, b_conv, b_conv, w_down)


def _conv_bwd(dc, w_conv, tm, tn):
    S, C = dc.shape
    step = tm // SUBLANES
    last_blk = S // SUBLANES - 1

    def body(d_ref, nx_ref, w_ref, o_ref):
        last = pl.program_id(0) == pl.num_programs(0) - 1
        row = lax.broadcasted_iota(jnp.int32, (tm, tm), 0)
        col = lax.broadcasted_iota(jnp.int32, (tm, tm), 1)
        row8 = lax.broadcasted_iota(jnp.int32, (SUBLANES, FF_CHUNK), 0)
        ups = [(row + k == col).astype(BF16) for k in (1, 2)]
        for c0 in range(0, tn, FF_CHUNK):
            cols = slice(c0, c0 + FF_CHUNK)
            d = d_ref[:, cols]
            nx = nx_ref[:, cols].astype(F32) * jnp.where(last, 0.0, 1.0)
            out = w_ref[2:3, cols] * d.astype(F32)
            for k, up in zip((1, 2), ups):
                moved = jnp.dot(up, d, preferred_element_type=F32)
                bottom = moved[tm - SUBLANES:tm] + jnp.where(row8 >= SUBLANES - k, pltpu.roll(nx, SUBLANES - k, 0), 0.0)
                out = out + w_ref[2 - k:3 - k, cols] * jnp.concatenate([moved[0:tm - SUBLANES], bottom], axis=0)
            o_ref[:, cols] = out.astype(BF16)

    return pl.pallas_call(
        body, name="conv_bwd", grid=(S // tm, C // tn),
        in_specs=[pl.BlockSpec((tm, tn), lambda i, j: (i, j)),
                  pl.BlockSpec((SUBLANES, tn), lambda i, j: (jnp.minimum((i + 1) * step, last_blk), j)),
                  pl.BlockSpec((3, tn), lambda i, j: (0, j))],
        out_specs=pl.BlockSpec((tm, tn), lambda i, j: (i, j)),
        out_shape=jax.ShapeDtypeStruct((S, C), BF16),
        compiler_params=_params(("parallel", "parallel")),
    )(dc, dc, w_conv)


def _matmul_tn(a, b, name, bm, bn, tk, col_a=0, col_b=0, quarters=None):
    S = a.shape[0]
    gm, gn = quarters if quarters else (1, 1)
    nk = S // tk

    def body(a_ref, b_ref, o_ref):
        @pl.when(pl.program_id(2) == 0)
        def _():
            o_ref[...] = jnp.zeros_like(o_ref)

        o_ref[...] += lax.dot_general(a_ref[...].astype(BF16), b_ref[...].astype(BF16), TN, preferred_element_type=F32)

    if quarters and gn > 1:
        out_spec = pl.BlockSpec((None, bm, bn), lambda i, j, k: (j, i, 0))
        out_shape = jax.ShapeDtypeStruct((gn, gm * bm, bn), F32)
    else:
        out_spec = pl.BlockSpec((bm, bn), lambda i, j, k: (i, j))
        out_shape = jax.ShapeDtypeStruct((gm * bm, gn * bn), F32)
    return pl.pallas_call(
        body, name=name, grid=(gm, gn, nk),
        in_specs=[pl.BlockSpec((tk, bm), lambda i, j, k: (k, col_a * gm + i)),
                  pl.BlockSpec((tk, bn), lambda i, j, k: (k, col_b * gn + j))],
        out_specs=out_spec, out_shape=out_shape,
        compiler_params=_params(("parallel", "parallel", "arbitrary")),
    )(a, b)


def _up_bwd(dact, w_up_q, x1, g2, dx2, tm):
    S = x1.shape[0]
    nq, _, wq = w_up_q.shape

    def body(d_ref, w_ref, x_ref, g_ref, dx2_ref, dx1_ref, dg_ref):
        @pl.when(pl.program_id(0) == 0)
        def _():
            dg_ref[...] = jnp.zeros_like(dg_ref)

        dh = jnp.zeros((tm, D_MODEL), F32)
        for j in range(nq):
            dh = dh + lax.dot_general(d_ref[:, j * wq:(j + 1) * wq], w_ref[j], NT, preferred_element_type=F32)
        dx, dg = _rms_bwd(dh, x_ref[...], g_ref[...])
        dg_ref[...] += dg
        dx1_ref[...] = dx2_ref[...] + dx

    row = lambda w: pl.BlockSpec((tm, w), lambda i: (i, 0))
    return pl.pallas_call(
        body, name="up_bwd", grid=(S // tm,),
        in_specs=[row(nq * wq), pl.BlockSpec((nq, D_MODEL, wq), lambda i: (0, 0, 0), pipeline_mode=pl.Buffered(1)),
                  row(D_MODEL), _full((1, D_MODEL)), row(D_MODEL)],
        out_specs=[row(D_MODEL), _full((1, D_MODEL))],
        out_shape=[jax.ShapeDtypeStruct((S, D_MODEL), F32), jax.ShapeDtypeStruct((1, D_MODEL), F32)],
        compiler_params=_params(("arbitrary",)),
    )(dact, w_up_q, x1, g2, dx2)


def _out_bwd(dx1, w_out, tm):
    S = dx1.shape[0]

    def body(d_ref, w_ref, o_ref):
        o_ref[...] = lax.dot_general(d_ref[...].astype(BF16), w_ref[...], NT, preferred_element_type=F32).astype(BF16)

    return pl.pallas_call(
        body, name="out_bwd", grid=(S // tm,),
        in_specs=[pl.BlockSpec((tm, D_MODEL), lambda i: (i, 0)), _full((D_MODEL, D_MODEL))],
        out_specs=pl.BlockSpec((tm, D_MODEL), lambda i: (i, 0)),
        out_shape=jax.ShapeDtypeStruct((S, D_MODEL), BF16),
        compiler_params=_params(("parallel",)),
    )(dx1, w_out)


def _gate_bwd(z, dcat, w_mask, w_mask_t, ln_row, b_full, seg_avg, head_ind, tm, swap):
    S = z.shape[0]
    nb = tm // SG_BLOCK
    ns = len(swap)

    def body(zu_ref, zv_ref, do_ref, w_ref, wt_ref, ln_ref, b_ref, avg_ref, ind_ref, *rest):
        dzu_ref, dzv_ref, dw_ref, db_ref, dln_ref = rest[ns:ns + 5]
        dvn_s, dbf_s = rest[2 * ns + 5:2 * ns + 7]
        swap_start, swap_finish = _swap_ops(rest[:ns], rest[ns + 5:2 * ns + 5], *rest[2 * ns + 7:])
        i = pl.program_id(0)

        @pl.when(i == 0)
        def _():
            swap_start()
            dw_ref[...] = jnp.zeros_like(dw_ref)
            dln_ref[...] = jnp.zeros_like(dln_ref)
            dbf_s[...] = jnp.zeros_like(dbf_s)

        zu = zu_ref[...].astype(F32)
        zv = zv_ref[...].astype(F32)
        u = _gelu(zu)
        v = _gelu(zv)
        avg = avg_ref[...]
        vhat, rstd = _layer_norm_heads(v, avg)
        ln = ln_ref[...]
        vn = vhat * ln
        for b in range(nb):
            rows = slice(b * SG_BLOCK, (b + 1) * SG_BLOCK)
            vn_b = vn[rows]
            mixed = _gate_mix(vn_b, w_ref, b_ref[...])
            do = do_ref[rows, :].astype(F32)
            dzu_ref[rows, :] = (do * mixed * _gelu_grad(zu[rows])).astype(BF16)
            dmix = do * u[rows]
            dbf_s[...] += dmix
            vn_bf = vn_b.astype(BF16)
            dvn = jnp.zeros((SG_BLOCK, D_HEADS), F32)
            for h in range(N_HEADS):
                dmh = jnp.where(_head_mask(h, SG_BLOCK), dmix, 0.0).astype(BF16)
                dw_ref[h] += lax.dot_general(dmh, vn_bf, NT, preferred_element_type=F32)
                dvn = dvn + jnp.dot(wt_ref[h], dmh, preferred_element_type=F32)
            dvn_s[rows, :] = dvn
        dvn = dvn_s[...]
        dln_ref[...] += jnp.sum(dvn * vhat, axis=0, keepdims=True)
        dvhat = dvn * ln
        dv = rstd * (dvhat - _split_dot(dvhat, avg) - vhat * _split_dot(dvhat * vhat, avg))
        dzv_ref[...] = (dv * _gelu_grad(zv)).astype(BF16)

        @pl.when(i == pl.num_programs(0) - 1)
        def _():
            r = lax.broadcasted_iota(jnp.int32, (SG_BLOCK, SG_BLOCK), 0) // CHUNK
            s = lax.broadcasted_iota(jnp.int32, (SG_BLOCK, SG_BLOCK), 1) // CHUNK
            for h in range(N_HEADS):
                dw_ref[h] = jnp.where(r >= s, dw_ref[h], 0.0)
            db_ref[...] = _split_dot(dbf_s[...], ind_ref[...])
            swap_finish()

    row = lambda col: pl.BlockSpec((tm, D_HEADS), lambda i: (i, col))
    wspec = _full((N_HEADS, SG_BLOCK, SG_BLOCK))
    out = pl.pallas_call(
        body, name="gate_bwd", grid=(S // tm,),
        in_specs=[row(0), row(1), row(0), wspec, wspec, _full((1, D_HEADS)), _full((SG_BLOCK, D_HEADS)),
                  _full((D_HEADS, D_HEADS)), _full((D_HEADS, LANES))] + [_ANY] * ns,
        out_specs=[row(0), row(0), wspec, _full((SG_BLOCK, LANES)), _full((1, D_HEADS))] + [_ANY] * ns,
        out_shape=[jax.ShapeDtypeStruct((S, D_HEADS), BF16), jax.ShapeDtypeStruct((S, D_HEADS), BF16),
                   jax.ShapeDtypeStruct((N_HEADS, SG_BLOCK, SG_BLOCK), F32), jax.ShapeDtypeStruct((SG_BLOCK, LANES), F32),
                   jax.ShapeDtypeStruct((1, D_HEADS), F32)] + _swap_shapes(swap),
        scratch_shapes=[pltpu.VMEM((tm, D_HEADS), F32), pltpu.VMEM((SG_BLOCK, D_HEADS), F32)] + _swap_sems(ns),
        compiler_params=_params(("arbitrary",)),
    )(z, z, dcat, w_mask, w_mask_t, ln_row, b_full, seg_avg, head_ind, *swap)
    return out[:5], out[5:]


def _attn_pack_grad(o, dcat, qa, lse, head_ind, k, tm):
    S = o.shape[0]

    def body(o_ref, do_ref, qa_ref, lse_ref, ind_ref, pl_ref, pt_ref, eye_ref, ds_ref, dst_ref, ls_ref, lst_ref,
             dop_ref, qb_ref, dot_ref, qbt_ref):
        do = do_ref[...]
        delta = _split_dot(o_ref[...].astype(F32) * do.astype(F32), ind_ref[...])
        hi = delta.astype(BF16)
        lo = (delta - hi.astype(F32)).astype(BF16)
        dop = jnp.dot(do, pl_ref[...], preferred_element_type=F32)
        dop = dop - jnp.dot(hi, ds_ref[0], preferred_element_type=F32) - jnp.dot(lo, ds_ref[1], preferred_element_type=F32)
        for g in range(GROUPS):
            dop_ref[g] = dop[:, g * GROUP_PAD:(g + 1) * GROUP_PAD].astype(BF16)
        dot = lax.dot_general(pt_ref[...], do, NT, preferred_element_type=F32)
        dot = dot - lax.dot_general(dst_ref[0], hi, NT, preferred_element_type=F32)
        dot = dot - lax.dot_general(dst_ref[1], lo, NT, preferred_element_type=F32)
        dot_ref[...] = dot.astype(BF16)
        qa = qa_ref[...]
        qb = qa.astype(F32)
        qbt = lax.dot_general(eye_ref[...], qa, NT, preferred_element_type=F32)
        for j, part in enumerate(_split3(lse_ref[...])):
            qb = qb - lax.dot_general(part, ls_ref[j], TN, preferred_element_type=F32)
            qbt = qbt - jnp.dot(lst_ref[j], part, preferred_element_type=F32)
        for g in range(GROUPS):
            qb_ref[g] = qb[:, g * GROUP_PAD:(g + 1) * GROUP_PAD].astype(BF16)
        qbt_ref[...] = qbt.astype(BF16)

    pad = pl.BlockSpec((tm, D_PAD), lambda i: (i, 0))
    padt = pl.BlockSpec((None, D_PAD, tm), lambda i: (i, 0, 0))
    return pl.pallas_call(
        body, name="attn_pack_grad", grid=(S // tm,),
        in_specs=[pl.BlockSpec((tm, D_HEADS), lambda i: (i, 0)), pl.BlockSpec((tm, D_HEADS), lambda i: (i, 1)), pad,
                  pl.BlockSpec((STAT_ROWS, tm), lambda i: (0, i)), _full((D_HEADS, LANES)), _full((D_HEADS, D_PAD)),
                  _full((D_PAD, D_HEADS)), _full((D_PAD, D_PAD)), _full((2, LANES, D_PAD)), _full((2, D_PAD, LANES)),
                  _full((3, STAT_ROWS, D_PAD)), _full((3, D_PAD, STAT_ROWS))],
        out_specs=[pl.BlockSpec((GROUPS, tm, GROUP_PAD), lambda i: (0, i, 0))] * 2 + [padt, padt],
        out_shape=[jax.ShapeDtypeStruct((GROUPS, S, GROUP_PAD), BF16)] * 2 + [jax.ShapeDtypeStruct((S // tm, D_PAD, tm), BF16)] * 2,
        compiler_params=_params(("parallel",)),
    )(o, dcat, qa, lse, head_ind, k["place"], k["place_t"], jnp.eye(D_PAD, dtype=BF16), k["d_stat"],
      jnp.swapaxes(k["d_stat"], 1, 2), k["l_stat"], jnp.swapaxes(k["l_stat"], 1, 2))


def _attn_bwd(qb, qbt, ka, va, dop, dopt, k, tq, sums16):
    S = ka.shape[0]
    n = S // tq
    ns = len(sums16)

    pairs = [(kb, q) for kb in range(n) for q in range(kb, n)]
    k_of = jnp.asarray([kb for kb, _ in pairs], jnp.int32)
    q_of = jnp.asarray([q for _, q in pairs], jnp.int32)

    def body(k_of_ref, q_of_ref, q_ref, qt_ref, k_ref, v_ref, do_ref, dot_ref, pt_ref, pick_ref, *rest):
        dq_hbm, dk_ref, dv_ref, dcc_ref = rest[ns:ns + 4]
        dq_s, dk_s, dv_s = rest[2 * ns + 4:2 * ns + 7]
        s_s, d_s = rest[2 * ns + 7:2 * ns + 9], rest[2 * ns + 9:2 * ns + 11]
        sem = rest[2 * ns + 11]
        scatter_start, scatter_finish = _scatter_ops(rest[:ns], rest[ns + 4:2 * ns + 4], *rest[2 * ns + 12:])
        g = pl.program_id(0)
        ki, qi = k_of_ref[pl.program_id(1)], q_of_ref[pl.program_id(1)]

        @pl.when((g == 0) & (ki == 0) & (qi == 0))
        def _():
            scatter_start()

        @pl.when((ki == 0) & (qi == 0))
        def _():
            dq_s[...] = jnp.zeros_like(dq_s)

        @pl.when(qi == ki)
        def _():
            dk_s[...] = jnp.zeros_like(dk_s)
            dv_s[...] = jnp.zeros_like(dv_s)

        def step(diagonal):
            chunks = [slice(c * KEY_CHUNK, (c + 1) * KEY_CHUNK) for c in range(tq // KEY_CHUNK)]

            def scores(hh, rows, slot):
                sl = slice(hh * HEAD_PAD, (hh + 1) * HEAD_PAD)
                s_s[slot][rows, :] = lax.dot_general(q_ref[rows, sl], k_ref[:, sl], NT, preferred_element_type=F32)
                d_s[slot][rows, :] = lax.dot_general(do_ref[rows, sl], v_ref[:, sl], NT, preferred_element_type=F32)

            for rows in chunks:
                scores(0, rows, 0)
            for hh in range(GROUP_HEADS):
                sl = slice(hh * HEAD_PAD, (hh + 1) * HEAD_PAD)
                slot = hh % 2
                dv, dk = dv_s[sl, :], dk_s[sl, :]
                for rows in chunks:
                    if hh + 1 < GROUP_HEADS:
                        scores(hh + 1, rows, 1 - slot)
                    p = jnp.exp2(s_s[slot][rows, :])
                    if diagonal:
                        row = rows.start + lax.broadcasted_iota(jnp.int32, (KEY_CHUNK, tq), 0)
                        col = lax.broadcasted_iota(jnp.int32, (KEY_CHUNK, tq), 1)
                        p = jnp.where(row >= col, p, 0.0)
                    ds = (p * d_s[slot][rows, :]).astype(BF16)
                    dv = dv + jnp.dot(dot_ref[sl, rows], p.astype(BF16), preferred_element_type=F32)
                    dk = dk + jnp.dot(qt_ref[sl, rows], ds, preferred_element_type=F32)
                    qrows = pl.ds(pl.multiple_of(qi * tq + rows.start, KEY_CHUNK), KEY_CHUNK)
                    dq_s[qrows, sl] += jnp.dot(ds, k_ref[:, sl], preferred_element_type=F32)
                dv_s[sl, :] = dv
                dk_s[sl, :] = dk

        @pl.when(qi > ki)
        def _():
            step(False)

        @pl.when(qi == ki)
        def _():
            step(True)

        @pl.when(qi == n - 1)
        def _():
            dk = dk_s[...]
            pt = pt_ref[...]
            dk_ref[...] = lax.dot_general((dk * (1.0 / LOG2E)).astype(BF16), pt, TN, preferred_element_type=F32).astype(BF16)
            dv_ref[...] = lax.dot_general(dv_s[...].astype(BF16), pt, TN, preferred_element_type=F32).astype(BF16)
            dcc_ref[...] = sum(lax.dot_general(part, pick_ref[...], TN, preferred_element_type=F32) for part in _split3(dk))

        @pl.when((ki == n - 1) & (qi == n - 1))
        def _():
            cp = pltpu.make_async_copy(dq_s, dq_hbm.at[g], sem)
            cp.start()
            cp.wait()

        @pl.when((g == GROUPS - 1) & (ki == n - 1) & (qi == n - 1))
        def _():
            scatter_finish()

    gw = GROUP_HEADS * HEAD_DIM
    qspec = pl.BlockSpec((None, tq, GROUP_PAD), lambda g, i, ks, qs: (g, qs[i], 0))
    qtspec = pl.BlockSpec((None, GROUP_PAD, tq), lambda g, i, ks, qs: (qs[i], g, 0))
    kspec = pl.BlockSpec((tq, GROUP_PAD), lambda g, i, ks, qs: (ks[i], g))
    kout = pl.BlockSpec((tq, gw), lambda g, i, ks, qs: (ks[i], g))
    out = pl.pallas_call(
        body, name="attn_bwd",
        grid_spec=pltpu.PrefetchScalarGridSpec(
            num_scalar_prefetch=2, grid=(GROUPS, len(pairs)),
            in_specs=[qspec, qtspec, kspec, kspec, qspec, qtspec, pl.BlockSpec((GROUP_PAD, gw), lambda g, i, ks, qs: (0, 0)),
                      pl.BlockSpec((None, GROUP_PAD, LANES), lambda g, i, ks, qs: (g, 0, 0))] + [_ANY] * ns,
            out_specs=[_ANY, kout, kout, pl.BlockSpec((None, tq, LANES), lambda g, i, ks, qs: (g, ks[i], 0))] + [_ANY] * ns,
            scratch_shapes=[pltpu.VMEM((S, GROUP_PAD), F32), pltpu.VMEM((GROUP_PAD, tq), F32), pltpu.VMEM((GROUP_PAD, tq), F32),
                            pltpu.VMEM((tq, tq), F32), pltpu.VMEM((tq, tq), F32), pltpu.VMEM((tq, tq), F32),
                            pltpu.VMEM((tq, tq), F32), pltpu.SemaphoreType.DMA]
            + _scatter_sems(ns)),
        out_shape=[jax.ShapeDtypeStruct((GROUPS, S, GROUP_PAD), F32), jax.ShapeDtypeStruct((S, D_HEADS), BF16),
                   jax.ShapeDtypeStruct((S, D_HEADS), BF16), jax.ShapeDtypeStruct((GROUPS, S, LANES), F32)]
        + _scatter_shapes(sums16),
        compiler_params=_params(("arbitrary", "arbitrary")),
    )(k_of, q_of, qb, qbt, ka, va, dop, dopt, k["place_t_group"], k["pick_cols"], *sums16)
    return out[0], out[1], out[2], out[3], out[4:]


def _attn_unpack(dqp, dcc, k, tm):
    S = dqp.shape[1]
    gw = GROUP_HEADS * HEAD_DIM

    def body(dqp_ref, dcc_ref, pt_ref, pick_ref, dq_ref, dc_ref):
        dc = jnp.zeros((tm, LANES), F32)
        for g in range(GROUPS):
            x = dqp_ref[g]
            dq_ref[:, g * gw:(g + 1) * gw] = jnp.dot((x * SCALE).astype(BF16), pt_ref[...], preferred_element_type=F32).astype(BF16)
            dc = dc + _split3_dot(x, pick_ref[g]) - dcc_ref[g]
        dc_ref[...] = dc

    return pl.pallas_call(
        body, name="attn_unpack", grid=(S // tm,),
        in_specs=[pl.BlockSpec((GROUPS, tm, GROUP_PAD), lambda i: (0, i, 0)), pl.BlockSpec((GROUPS, tm, LANES), lambda i: (0, i, 0)),
                  _full((GROUP_PAD, gw)), _full((GROUPS, GROUP_PAD, LANES))],
        out_specs=[pl.BlockSpec((tm, D_HEADS), lambda i: (i, 0)), pl.BlockSpec((tm, LANES), lambda i: (i, 0))],
        out_shape=[jax.ShapeDtypeStruct((S, D_HEADS), BF16), jax.ShapeDtypeStruct((S, LANES), F32)],
        compiler_params=_params(("parallel",)),
    )(dqp, dcc, k["place_t_group"], k["pick_rows"])


def _fox_bwd(dc, f, bias_row, tb):
    S = f.shape[0]
    nb = S // tb

    def body(dc_ref, f_ref, b_ref, df_ref, dbias_ref, carry):
        @pl.when(pl.program_id(0) == 0)
        def _():
            carry[...] = jnp.zeros_like(carry)
            dbias_ref[...] = jnp.zeros_like(dbias_ref)

        r = lax.broadcasted_iota(jnp.int32, (tb, tb), 0)
        s = lax.broadcasted_iota(jnp.int32, (tb, tb), 1)
        tri = (s >= r).astype(F32)
        rc = jnp.dot(tri, dc_ref[...], precision=lax.Precision.HIGHEST, preferred_element_type=F32) + carry[0:1, :]
        carry[...] = jnp.broadcast_to(rc[0:1, :], carry.shape)
        lane = lax.broadcasted_iota(jnp.int32, (tb, LANES), 1)
        df = jnp.where(lane < N_HEADS, rc * jax.nn.sigmoid(-(f_ref[...] + b_ref[...])), 0.0)
        df_ref[...] = df.astype(BF16)
        dbias_ref[...] += jnp.sum(df, axis=0, keepdims=True)

    rev = pl.BlockSpec((tb, LANES), lambda i: (nb - 1 - i, 0))
    return pl.pallas_call(
        body, name="fox_bwd", grid=(nb,),
        in_specs=[rev, rev, _full((1, LANES))],
        out_specs=[rev, _full((1, LANES))],
        out_shape=[jax.ShapeDtypeStruct((S, LANES), BF16), jax.ShapeDtypeStruct((1, LANES), F32)],
        scratch_shapes=[pltpu.VMEM((SUBLANES, LANES), F32)],
        compiler_params=_params(("arbitrary",)),
    )(dc, f, bias_row)


_DZ_WIDTHS = (D_HEADS,) * 5 + (LANES,)


def _in_bwd(pieces, w_in, x, g1, dx1, tm, sums16):
    S = x.shape[0]
    ns = len(sums16)

    def body(*refs):
        p_refs, (w_ref, x_ref, g_ref, dx1_ref) = refs[:6], refs[6:10]
        dx_ref, dg_ref = refs[10 + ns:12 + ns]
        scatter_start, scatter_finish = _scatter_ops(refs[10:10 + ns], refs[12 + ns:12 + 2 * ns], *refs[12 + 2 * ns:])

        @pl.when(pl.program_id(0) == 0)
        def _():
            dg_ref[...] = jnp.zeros_like(dg_ref)
            scatter_start()

        dh = jnp.zeros((tm, D_MODEL), F32)
        off = 0
        for p_ref, w in zip(p_refs, _DZ_WIDTHS):
            dh = dh + lax.dot_general(p_ref[...].astype(BF16), w_ref[:, off:off + w], NT, preferred_element_type=F32)
            off += w
        dx, dg = _rms_bwd(dh, x_ref[...], g_ref[...])
        dg_ref[...] += dg
        dx_ref[...] = dx1_ref[...] + dx

        @pl.when(pl.program_id(0) == pl.num_programs(0) - 1)
        def _():
            scatter_finish()

    row = lambda w: pl.BlockSpec((tm, w), lambda i: (i, 0))
    out = pl.pallas_call(
        body, name="in_bwd", grid=(S // tm,),
        in_specs=[row(w) for w in _DZ_WIDTHS] + [_full((D_MODEL, D_IN_PAD)), row(D_MODEL), _full((1, D_MODEL)), row(D_MODEL)]
        + [_ANY] * ns,
        out_specs=[row(D_MODEL), _full((1, D_MODEL))] + [_ANY] * ns,
        out_shape=[jax.ShapeDtypeStruct((S, D_MODEL), F32), jax.ShapeDtypeStruct((1, D_MODEL), F32)] + _scatter_shapes(sums16),
        scratch_shapes=_scatter_sems(ns),
        compiler_params=_params(("arbitrary",)),
    )(*pieces, w_in, x, g1, dx1, *sums16)
    return out[0], out[1], out[2:]


def _dw_in(h1, pieces, tk):
    S = h1.shape[0]

    def body(*refs):
        h_ref, p_refs, o_ref = refs[0], refs[1:7], refs[7]

        @pl.when(pl.program_id(0) == 0)
        def _():
            o_ref[...] = jnp.zeros_like(o_ref)

        off = 0
        for p_ref, w in zip(p_refs, _DZ_WIDTHS):
            o_ref[:, off:off + w] += lax.dot_general(h_ref[...], p_ref[...].astype(BF16), TN, preferred_element_type=F32)
            off += w

    row = lambda w: pl.BlockSpec((tk, w), lambda k: (k, 0))
    return pl.pallas_call(
        body, name="dw_in", grid=(S // tk,),
        in_specs=[row(D_MODEL)] + [row(w) for w in _DZ_WIDTHS],
        out_specs=_full((D_MODEL, D_IN_PAD)),
        out_shape=jax.ShapeDtypeStruct((D_MODEL, D_IN_PAD), F32),
        compiler_params=_params(("arbitrary",)),
    )(h1, *pieces)


def _adamw_math(w, g, m, v):
    m = ADAM_B1 * m + (1.0 - ADAM_B1) * g
    v = ADAM_B2 * v + (1.0 - ADAM_B2) * (g * g)
    m_hat = m / (1.0 - ADAM_B1 ** ADAM_STEP)
    v_hat = v / (1.0 - ADAM_B2 ** ADAM_STEP)
    delta = -ADAM_LR * (m_hat / (jnp.sqrt(v_hat) + ADAM_EPS) + ADAM_WD * w)
    return delta, m, v


def _adamw(name, w, g, m, v):
    R, C = w.shape
    tr = _row_tile(R, 256)

    def body(w_ref, g_ref, m_ref, v_ref, go_ref, d_ref, nm_ref, nv_ref):
        g = g_ref[...]
        d, nm, nv = _adamw_math(w_ref[...], g, m_ref[...], v_ref[...])
        go_ref[...] = g
        d_ref[...] = d
        nm_ref[...] = nm
        nv_ref[...] = nv

    spec = pl.BlockSpec((tr, C), lambda i: (i, 0))
    return pl.pallas_call(
        body, name=name, grid=(R // tr,), in_specs=[spec] * 4, out_specs=[spec] * 4,
        out_shape=[jax.ShapeDtypeStruct((R, C), F32)] * 4,
        compiler_params=_params(("parallel",)),
    )(w, g, m, v)


def _pair_sum(name, grad, theirs, ids):
    q, half, C = theirs.shape
    tr = _row_tile(half, 256)
    nb = half // tr

    def body(ids_ref, a_ref, b_ref, s_ref, sb_ref):
        s = a_ref[...] + b_ref[...]
        s_ref[...] = s
        sb_ref[...] = s.astype(BF16)

    here = pl.BlockSpec((None, tr, C), lambda j, i, ids: (j, i, 0))
    return pl.pallas_call(
        body, name=name,
        grid_spec=pltpu.PrefetchScalarGridSpec(
            num_scalar_prefetch=1, grid=(q, nb),
            in_specs=[pl.BlockSpec((None, tr, C), lambda j, i, ids: (j, ids[1] * nb + i, 0)), here],
            out_specs=[here, here]),
        out_shape=[jax.ShapeDtypeStruct((q, half, C), F32), jax.ShapeDtypeStruct((q, half, C), BF16)],
        compiler_params=_params(("parallel", "parallel")),
    )(ids, grad, theirs)


def _chip_sum(name, sums32, others, ids):
    _, half, C = sums32.shape
    tr = _row_tile(half, 256)
    nb = half // tr

    def body(ids_ref, a_ref, o_ref, s_ref):
        s = a_ref[...]
        for j in range(3):
            s = s + o_ref[j].astype(F32)
        s_ref[...] = s

    return pl.pallas_call(
        body, name=name,
        grid_spec=pltpu.PrefetchScalarGridSpec(
            num_scalar_prefetch=1, grid=(nb,),
            in_specs=[pl.BlockSpec((None, tr, C), lambda i, ids: (ids[0], i, 0)),
                      pl.BlockSpec((3, tr, C), lambda i, ids: (0, i, 0))],
            out_specs=pl.BlockSpec((tr, C), lambda i, ids: (ids[1] * nb + i, 0))),
        out_shape=jax.ShapeDtypeStruct((2 * half, C), F32),
        compiler_params=_params(("parallel",)),
    )(ids, sums32, others)


def _place():
    return lax.axis_index("x"), lax.axis_index("y"), lax.axis_index("c")


def _other_chips(x, y):
    return [(1 - x, y), (x, 1 - y), (1 - x, 1 - y)]


_ANY = pl.BlockSpec(memory_space=pl.ANY)


def _gather_quarters(shards):
    n = len(shards)

    def body(*refs):
        start, hand_on, finish = _gather_ops(refs[:n], refs[n:2 * n], *refs[2 * n:])
        start()
        hand_on()
        finish()

    return pl.pallas_call(
        body, name="gather_weights",
        in_specs=[_ANY] * n, out_specs=[_ANY] * n,
        out_shape=_gather_shapes(shards), scratch_shapes=_gather_sems(n),
    )(*shards)


def _gather_shapes(shards):
    return [jax.ShapeDtypeStruct((4,) + s.shape, s.dtype) for s in shards]


def _gather_sems(n):
    return [pltpu.SemaphoreType.DMA((n, 3))] * 4 + [pltpu.SemaphoreType.DMA((n,))]


def _gather_ops(ins, outs, send_sems, recv_sems, pass_send_sems, pass_recv_sems, own_sems):
    n = len(ins)
    halved = [r.shape[0] % 32 == 0 for r in ins]

    def part(a, quarter, core):
        if not halved[a]:
            return outs[a].at[quarter]
        half = ins[a].shape[0] // 2
        return outs[a].at[quarter, pl.ds(core * half, half), :]

    def ici(a, j, quarter):
        x, y, c = _place()
        px, py = _other_chips(x, y)[j]
        src = ins[a]
        if halved[a]:
            half = src.shape[0] // 2
            src = src.at[pl.ds(c * half, half), :]
        return pltpu.make_async_remote_copy(src_ref=src, dst_ref=part(a, quarter, c), send_sem=send_sems.at[a, j],
                                            recv_sem=recv_sems.at[a, j], device_id=(px, py, c), device_id_type=MESH)

    def passed(a, j, core):
        x, y, c = _place()
        px, py = _other_chips(x, y)[j]
        half = part(a, 2 * px + py, core)
        return pltpu.make_async_remote_copy(src_ref=half, dst_ref=half, send_sem=pass_send_sems.at[a, j],
                                            recv_sem=pass_recv_sems.at[a, j], device_id=(x, y, 1 - c), device_id_type=MESH)

    def own(a):
        x, y, _ = _place()
        return pltpu.make_async_copy(ins[a], outs[a].at[2 * x + y], own_sems.at[a])

    def start():
        x, y, _ = _place()
        for a in range(n):
            for j in range(3):
                ici(a, j, 2 * x + y).start()
            own(a).start()

    def hand_on():
        x, y, c = _place()
        for a in range(n):
            for j, (px, py) in enumerate(_other_chips(x, y)):
                ici(a, j, 2 * px + py).wait_recv()
                if halved[a]:
                    passed(a, j, c).start()

    def finish():
        x, y, c = _place()
        for a in range(n):
            for j in range(3):
                if halved[a]:
                    passed(a, j, 1 - c).wait_recv()
                    passed(a, j, c).wait_send()
                ici(a, j, 2 * x + y).wait_send()
            own(a).wait()

    return start, hand_on, finish


def _swap_halves(grads, name):
    n = len(grads)

    def body(*refs):
        start, finish = _swap_ops(refs[:n], refs[n:2 * n], *refs[2 * n:])
        start()
        finish()

    return pl.pallas_call(
        body, name=name,
        in_specs=[_ANY] * n, out_specs=[_ANY] * n, out_shape=_swap_shapes(grads), scratch_shapes=_swap_sems(n),
    )(*grads)


def _swap_shapes(grads):
    return [jax.ShapeDtypeStruct((4, g.shape[1] // 2, g.shape[2]), F32) for g in grads]


def _swap_sems(n):
    return [pltpu.SemaphoreType.DMA((n,))] * 2


def _swap_ops(ins, outs, send_sems, recv_sems):
    def copy(a):
        x, y, c = _place()
        half = ins[a].shape[1] // 2
        return pltpu.make_async_remote_copy(src_ref=ins[a].at[:, pl.ds((1 - c) * half, half), :], dst_ref=outs[a],
                                            send_sem=send_sems.at[a], recv_sem=recv_sems.at[a],
                                            device_id=(x, y, 1 - c), device_id_type=MESH)

    def start():
        for a in range(len(ins)):
            copy(a).start()

    def finish():
        for a in range(len(ins)):
            copy(a).wait()

    return start, finish


def _scatter_shapes(sums16):
    return [jax.ShapeDtypeStruct((3,) + s.shape[1:], BF16) for s in sums16]


def _scatter_sems(n):
    return [pltpu.SemaphoreType.DMA((n, 3))] * 2


def _scatter_ops(ins, outs, send_sems, recv_sems):
    n = len(ins)

    def copy(a, j):
        x, y, c = _place()
        px, py = _other_chips(x, y)[j]
        return pltpu.make_async_remote_copy(src_ref=ins[a].at[2 * px + py], dst_ref=outs[a].at[j], send_sem=send_sems.at[a, j],
                                            recv_sem=recv_sems.at[a, j], device_id=(px, py, c), device_id_type=MESH)

    def start():
        for a in range(n):
            for j in range(3):
                copy(a, j).start()

    def finish():
        for a in range(n):
            for j in range(3):
                copy(a, j).wait()

    return start, finish


def _join_halves(fulls):
    n = len(fulls)

    def body(*refs):
        ins, outs = refs[:n], refs[n:2 * n]
        send_sems, recv_sems = refs[2 * n:]
        x, y, c = _place()
        started = []
        for a in range(n):
            half = ins[a].shape[0] // 2
            rows = pl.ds(c * half, half)
            cp = pltpu.make_async_remote_copy(src_ref=ins[a].at[rows, :], dst_ref=outs[a].at[rows, :], send_sem=send_sems.at[a],
                                              recv_sem=recv_sems.at[a], device_id=(x, y, 1 - c), device_id_type=MESH)
            cp.start()
            started.append(cp)
        for cp in started:
            cp.wait()

    return pl.pallas_call(
        body, name="join_halves",
        in_specs=[_ANY] * n, out_specs=[_ANY] * n,
        out_shape=[jax.ShapeDtypeStruct(f.shape, F32) for f in fulls],
        input_output_aliases={a: a for a in range(n)},
        scratch_shapes=[pltpu.SemaphoreType.DMA((n,)), pltpu.SemaphoreType.DMA((n,))],
    )(*fulls)


def _small_allreduce(g):
    R = g.shape[0]
    half = R // 2

    def body(g_ref, out_ref, other_s, chip_s, parts_s, send_sems, recv_sems):
        x, y, c = _place()
        mine = 2 * x + y
        rows = pl.ds(pl.multiple_of(c * half, SUBLANES), half)

        def to_other_core(src, dst, k):
            return pltpu.make_async_remote_copy(src_ref=src, dst_ref=dst, send_sem=send_sems.at[k], recv_sem=recv_sems.at[k],
                                                device_id=(x, y, 1 - c), device_id_type=MESH)

        swap = to_other_core(g_ref, other_s, 0)
        swap.start()
        swap.wait()
        chip_s[...] = g_ref[...] + other_s[...]
        parts_s[mine] = chip_s[rows, :]
        sends = []
        for j, (px, py) in enumerate(_other_chips(x, y)):
            cp = pltpu.make_async_remote_copy(src_ref=chip_s.at[rows, :], dst_ref=parts_s.at[mine], send_sem=send_sems.at[1 + j],
                                              recv_sem=recv_sems.at[1 + j], device_id=(px, py, c), device_id_type=MESH)
            cp.start()
            sends.append(cp)
        for cp in sends:
            cp.wait()
        out_ref[rows, :] = (parts_s[0] + parts_s[1]) + (parts_s[2] + parts_s[3])
        join = to_other_core(out_ref.at[rows, :], out_ref.at[rows, :], 4)
        join.start()
        join.wait()

    vm = pl.BlockSpec(memory_space=pltpu.VMEM)
    return pl.pallas_call(
        body, name="small_allreduce",
        in_specs=[vm], out_specs=vm, out_shape=jax.ShapeDtypeStruct((R, LANES), F32),
        scratch_shapes=[pltpu.VMEM((R, LANES), F32), pltpu.VMEM((R, LANES), F32), pltpu.VMEM((4, half, LANES), F32),
                        pltpu.SemaphoreType.DMA((5,)), pltpu.SemaphoreType.DMA((5,))],
        compiler_params=pltpu.CompilerParams(vmem_limit_bytes=VMEM_LIMIT),
    )(g)


def _adamw_small(ws, gs, ms, vs):
    n = len(ws)

    def body(*refs):
        for k in range(n):
            w_ref, g_ref, m_ref, v_ref = (refs[j * n + k] for j in range(4))
            d, nm, nv = _adamw_math(w_ref[...], g_ref[...], m_ref[...], v_ref[...])
            refs[4 * n + k][...] = d
            refs[5 * n + k][...] = nm
            refs[6 * n + k][...] = nv

    vm = pl.BlockSpec(memory_space=pltpu.VMEM)
    out = pl.pallas_call(
        body, name="adamw_small",
        in_specs=[vm] * (4 * n), out_specs=[vm] * (3 * n),
        out_shape=[jax.ShapeDtypeStruct(w.shape, F32) for w in ws] * 3,
        compiler_params=pltpu.CompilerParams(vmem_limit_bytes=VMEM_LIMIT),
    )(*ws, *gs, *ms, *vs)
    return out[:n], out[n:2 * n], out[2 * n:]


_SMALL = (("norm_mix_g", D_MODEL), ("f_bias", N_HEADS), ("sg_ln_g", D_HEADS), ("sg_w", N_HEADS * SG_BLOCK * SG_BLOCK),
          ("sg_b", N_HEADS * SG_BLOCK), ("norm_ffn_g", D_MODEL), ("w_conv", 3 * 2 * D_FF), ("b_conv", 2 * D_FF),
          ("norm_final_g", D_MODEL))


def _pack_small(parts):
    rows = []
    for name, size in _SMALL:
        flat = parts[name].reshape(-1).astype(F32)
        pad = (-size) % (SUBLANES * LANES)
        rows.append(jnp.pad(flat, (0, pad)).reshape(-1, LANES))
    packed = jnp.concatenate(rows, axis=0)
    return jnp.pad(packed, ((0, (-packed.shape[0]) % (2 * SUBLANES)), (0, 0)))


def _unpack_small(packed, shapes):
    out, r = {}, 0
    for name, size in _SMALL:
        nrows = (size + SUBLANES * LANES - 1) // (SUBLANES * LANES) * SUBLANES
        out[name] = packed[r:r + nrows].reshape(-1)[:size].reshape(shapes[name])
        r += nrows
    return out


def _local_step(x, target, g1, w_in, f_bias, sg_ln_g, sg_w, sg_b, g2, b_conv, g3, late_shards, ids):
    S = x.shape[0]
    tm = _row_tile(S, 512)
    tms = _row_tile(S, 256)
    tq = tm

    lane = jnp.arange(D_HEADS)
    seg_avg = jnp.where(lane[:, None] // HEAD_DIM == lane[None, :] // HEAD_DIM, 1.0 / HEAD_DIM, 0.0).astype(BF16)
    head_ind = (lane[:, None] // HEAD_DIM == jnp.arange(LANES)[None, :]).astype(BF16)
    pos_chunk = jnp.arange(SG_BLOCK) // CHUNK
    w_mask32 = jnp.where(pos_chunk[:, None] >= pos_chunk[None, :], sg_w, 0.0)
    w_mask = w_mask32.astype(BF16)
    w_mask_t = jnp.swapaxes(w_mask32, 1, 2).astype(BF16)
    ln_row = sg_ln_g.reshape(1, D_HEADS)
    b_full = jnp.repeat(sg_b.T, HEAD_DIM, axis=1)
    bias_row = jnp.pad(f_bias.reshape(1, N_HEADS), ((0, 0), (0, LANES - N_HEADS)))
    b_conv_row = b_conv.reshape(1, 2 * D_FF)

    z, f, h1 = _in_proj(x, g1, w_in, tm)
    c = _fox_prep(f, bias_row, _row_tile(S, 256))
    consts = _attn_consts()
    qa, ka, va, vat = _attn_pack(z, c, consts, tm)
    out_b, lse, gathered = _attn_fwd(qa, ka, vat, consts["place_t"], tq, late_shards)
    g_out, w_up_q, g_down, g_conv = gathered
    w_out = g_out.reshape(D_MODEL, D_MODEL)
    w_down = g_down.reshape(D_FF, D_MODEL)
    w_conv = jnp.concatenate([g_conv[q] for q in range(4)], axis=1)
    out_a = _gate_fwd(z, w_mask, ln_row, b_full, seg_avg, tm)
    x1, h2 = _mix_out(x, out_a, out_b, w_out, g2, tm)
    a = _up_proj(h2, w_up_q, tm)
    dx2, sq_err, dg3 = _ffn_fwd_loss(a, w_conv, b_conv_row, w_down, x1, g3, target, tms)

    dconv, y, dw_conv8, db_conv = _ffn_bwd_gate(dx2, a, w_conv, b_conv_row, w_down, tms)
    dact = _conv_bwd(dconv, w_conv, tms, D_FF)
    dw_down = _matmul_tn(y, dx2, "dw_down", D_FF // 2, D_MODEL, tm, quarters=(2, 1))
    dx1, dg2 = _up_bwd(dact, w_up_q, x1, g2, dx2, tm)
    dw_up_q = _matmul_tn(h2, dact, "dw_up", D_MODEL, 2 * D_FF // 4, tm, quarters=(1, 4))
    dcat = _out_bwd(dx1, w_out, tm)
    dw_out_a = _matmul_tn(out_a, dx1, "dw_out_a", D_HEADS, D_MODEL, tm)
    dw_out_b = _matmul_tn(out_b, dx1, "dw_out_b", D_HEADS, D_MODEL, tm)
    early = {"w_down": dw_down.reshape(4, D_FF // 4, D_MODEL), "w_up": dw_up_q,
             "w_out": jnp.concatenate([dw_out_a, dw_out_b], axis=0).reshape(4, D_MODEL // 4, D_MODEL)}
    (dzu, dzv, dsg_w, dsg_b_t, dln), theirs = _gate_bwd(z, dcat, w_mask, w_mask_t, ln_row, b_full, seg_avg, head_ind, tm,
                                                        list(early.values()))
    early_sums = _chip_sums(early, theirs, ids)
    dop, qb, dopt, qbt = _attn_pack_grad(out_b, dcat, qa, lse, head_ind, consts, tm)
    dqp, dk, dv, dcc, landed = _attn_bwd(qb, qbt, ka, va, dop, dopt, consts, tq, [s16 for _, s16 in early_sums.values()])
    early_parts = {k: (s32, got) for (k, (s32, _)), got in zip(early_sums.items(), landed)}
    dq, dc = _attn_unpack(dqp, dcc, consts, tm)
    df, dbias = _fox_bwd(dc, f, bias_row, _row_tile(S, 256))
    pieces = (dzu, dzv, dq, dk, dv, df)
    dw_in = _dw_in(h1, pieces, tm)[:, :D_IN].reshape(D_MODEL, 4, D_IN // 4).transpose(1, 0, 2)
    (w_in_sum, w_in_sum16), = _chip_sums({"w_in": dw_in}, _swap_halves([dw_in], "swap_halves"), ids).values()
    dx, dg1, (w_in_landed,) = _in_bwd(pieces, w_in, x, g1, dx1, tm, [w_in_sum16])

    grads = {
        "norm_mix_g": dg1, "f_bias": dbias[:, :N_HEADS], "sg_ln_g": dln, "sg_w": dsg_w, "sg_b": dsg_b_t[:, :N_HEADS].T,
        "norm_ffn_g": dg2, "w_conv": dw_conv8[:3], "b_conv": db_conv, "norm_final_g": dg3,
    }
    return sq_err, dx, grads, {**early_parts, "w_in": (w_in_sum, w_in_landed)}


def _chip_sums(grads_q, theirs, ids):
    return {k: _pair_sum("pair_sum_" + k, g, t, ids) for (k, g), t in zip(grads_q.items(), theirs)}


def _finish_reduction(parts, ids):
    names = list(parts)
    fulls = [_chip_sum("chip_sum_" + k, s32, got, ids) for k, (s32, got) in parts.items()]
    return dict(zip(names, _join_halves(fulls)))


def kernel(x, norm_mix_g, w_in, f_bias, sg_ln_g, sg_w, sg_b, w_out, norm_ffn_g, w_up, w_conv, b_conv, w_down, norm_final_g, loss_target, m_norm_mix_g, m_w_in, m_f_bias, m_sg_ln_g, m_sg_w, m_sg_b, m_w_out, m_norm_ffn_g, m_w_up, m_w_conv, m_b_conv, m_w_down, m_norm_final_g, v_norm_mix_g, v_w_in, v_f_bias, v_sg_ln_g, v_sg_w, v_sg_b, v_w_out, v_norm_ffn_g, v_w_up, v_w_conv, v_b_conv, v_w_down, v_norm_final_g):
    args = dict(locals())
    quarter = 2 * lax.axis_index("x") + lax.axis_index("y")
    ids = jnp.stack([quarter, lax.axis_index("c")]).astype(jnp.int32)
    wq_conv = w_conv.shape[-1]

    g_in = _gather_quarters([w_in[0].astype(BF16)])[0]
    w_in_full = jnp.pad(jnp.concatenate([g_in[q] for q in range(4)], axis=1), ((0, 0), (0, D_IN_PAD - D_IN)))
    late_shards = [w_out[0].astype(BF16), w_up[0].astype(BF16), w_down[0].astype(BF16), w_conv[0]]

    sq_err, dx, grads, parts = _local_step(
        x[0], loss_target[0], norm_mix_g, w_in_full, f_bias[0], sg_ln_g[0], sg_w[0], sg_b[0], norm_ffn_g, b_conv[0],
        norm_final_g.reshape(1, D_MODEL), late_shards, ids)
    loss = lax.psum(0.5 * jnp.sum(sq_err) / D_MODEL, ("x", "y", "c"))

    big = _finish_reduction(parts, ids)

    out = {"loss": loss, "grad_x": dx[None]}
    for k in ("w_in", "w_out", "w_up", "w_down"):
        g, d, nm, nv = _adamw("adamw_" + k, args[k][0], big[k], args["m_" + k][0], args["v_" + k][0])
        out["grad_" + k], out["delta_" + k], out["new_m_" + k], out["new_v_" + k] = g[None], d[None], nm[None], nv[None]

    small_names = [n for n, _ in _SMALL]
    shapes = {n: (3, 4 * wq_conv) if n == "w_conv" else args[n].shape for n in small_names}
    g_small = _unpack_small(_small_allreduce(_pack_small({n: grads[n] for n in small_names})), shapes)
    g_small["w_conv"] = lax.dynamic_slice(g_small["w_conv"], (0, quarter * wq_conv), (3, wq_conv))[None]
    flat2d = lambda t: t.reshape(-1, t.shape[-1])
    updated = _adamw_small(*[[flat2d(src[p + n]) for n in small_names] for src, p in
                             ((args, ""), (g_small, ""), (args, "m_"), (args, "v_"))])
    for n, g in g_small.items():
        out["grad_" + n] = g
    for prefix, arrs in zip(("delta_", "new_m_", "new_v_"), updated):
        for n, t in zip(small_names, arrs):
            out[prefix + n] = t.reshape(args[n].shape)

    weights = ["norm_mix_g", "w_in", "f_bias", "sg_ln_g", "sg_w", "sg_b", "w_out", "norm_ffn_g", "w_up", "w_conv", "b_conv",
               "w_down", "norm_final_g"]
    return (out["loss"], out["grad_x"], *[out[p + n] for p in ("grad_", "delta_", "new_m_", "new_v_") for n in weights])
```

```python
import functools
import math

import jax
import jax.numpy as jnp
from jax import lax
from jax.experimental import pallas as pl
from jax.experimental.pallas import tpu as pltpu

F32 = jnp.float32
BF16 = jnp.bfloat16
MESH = pl.DeviceIdType.MESH

D_MODEL = 1024
N_HEADS = 8
HEAD_DIM = 64
D_HEADS = N_HEADS * HEAD_DIM
SG_BLOCK = 128
CHUNK = 64
D_FF = 2816
D_IN = 2 * D_HEADS + 3 * D_HEADS + N_HEADS
LANES = 128
SUBLANES = 8
D_IN_PAD = 5 * D_HEADS + LANES
EPS = 1e-6
SCALE = HEAD_DIM ** -0.5
NEG = -1e30
LOG2E = 1.4426950408889634
HEAD_PAD = LANES
D_PAD = N_HEADS * HEAD_PAD
Q_STAT = HEAD_DIM
K_STAT = HEAD_DIM + 3
L_STAT = HEAD_DIM + 6
GROUPS = 2
GROUP_HEADS = N_HEADS // GROUPS
GROUP_PAD = GROUP_HEADS * HEAD_PAD
KEY_CHUNK = 256
STAT_ROWS = 16
FF_CHUNK = 256

ADAM_LR = 0.001
ADAM_B1 = 0.9
ADAM_B2 = 0.999
ADAM_EPS = 1e-08
ADAM_WD = 0.01
ADAM_STEP = 10

VMEM_LIMIT = 56 * 1024 * 1024

NT = (((1,), (1,)), ((), ()))
TN = (((0,), (0,)), ((), ()))


def _params(sem):
    return pltpu.CompilerParams(dimension_semantics=sem, vmem_limit_bytes=VMEM_LIMIT)


def _full(shape):
    nd = len(shape)
    return pl.BlockSpec(shape, lambda *_: (0,) * nd)


def _row_tile(rows, target):
    best = None
    for t in range(SUBLANES, min(rows, target) + 1, SUBLANES):
        if rows % t == 0:
            best = t
    assert best is not None, rows
    return best


def _sigmoid(x):
    return 0.5 * jnp.tanh(0.5 * x) + 0.5


def _gelu(z):
    return 0.5 * z * (1.0 + lax.erf(z * (2.0 ** -0.5)))


def _gelu_grad(z):
    cdf = 0.5 * (1.0 + lax.erf(z * (2.0 ** -0.5)))
    pdf = jnp.exp(-0.5 * z * z) * (1.0 / math.sqrt(2.0 * math.pi))
    return cdf + z * pdf


def _split_dot(x, m):
    hi = x.astype(BF16)
    lo = (x - hi.astype(F32)).astype(BF16)
    return jnp.dot(hi, m, preferred_element_type=F32) + jnp.dot(lo, m, preferred_element_type=F32)


def _head_mask(h, rows):
    lane = lax.broadcasted_iota(jnp.int32, (rows, D_HEADS), 1)
    return (lane >= h * HEAD_DIM) & (lane < (h + 1) * HEAD_DIM)


def _rms_bwd(dh, x, g):
    r = lax.rsqrt(jnp.mean(x * x, axis=-1, keepdims=True) + EPS)
    xhat = x * r
    dg = jnp.sum(dh * xhat, axis=0, keepdims=True)
    dxhat = dh * g
    dx = r * (dxhat - xhat * jnp.mean(dxhat * xhat, axis=-1, keepdims=True))
    return dx, dg


def _in_proj(x, g1, w_in, tm):
    S = x.shape[0]
    nz = D_IN_PAD - LANES

    def body(x_ref, g_ref, w_ref, z_ref, f_ref, h_ref):
        xf = x_ref[...]
        r = lax.rsqrt(jnp.mean(xf * xf, axis=-1, keepdims=True) + EPS)
        h = (xf * r * g_ref[...]).astype(BF16)
        h_ref[...] = h
        zz = jnp.dot(h, w_ref[...], preferred_element_type=F32)
        z_ref[...] = zz[:, :nz].astype(BF16)
        f_ref[...] = zz[:, nz:]

    return pl.pallas_call(
        body, name="in_proj", grid=(S // tm,),
        in_specs=[pl.BlockSpec((tm, D_MODEL), lambda i: (i, 0)), _full((1, D_MODEL)), _full((D_MODEL, D_IN_PAD))],
        out_specs=[pl.BlockSpec((tm, nz), lambda i: (i, 0)), pl.BlockSpec((tm, LANES), lambda i: (i, 0)),
                   pl.BlockSpec((tm, D_MODEL), lambda i: (i, 0))],
        out_shape=[jax.ShapeDtypeStruct((S, nz), BF16), jax.ShapeDtypeStruct((S, LANES), F32),
                   jax.ShapeDtypeStruct((S, D_MODEL), BF16)],
        compiler_params=_params(("parallel",)),
    )(x, g1, w_in)


def _fox_prep(f, bias_row, tb):
    S = f.shape[0]

    def body(f_ref, b_ref, c_ref, carry):
        @pl.when(pl.program_id(0) == 0)
        def _():
            carry[...] = jnp.zeros_like(carry)

        xv = f_ref[...] + b_ref[...]
        lf = jnp.minimum(xv, 0.0) - jnp.log(1.0 + jnp.exp(-jnp.abs(xv)))
        r = lax.broadcasted_iota(jnp.int32, (tb, tb), 0)
        s = lax.broadcasted_iota(jnp.int32, (tb, tb), 1)
        tri = (r >= s).astype(F32)
        cs = jnp.dot(tri, lf, precision=lax.Precision.HIGHEST, preferred_element_type=F32) + carry[0:1, :]
        c_ref[...] = cs
        carry[...] = jnp.broadcast_to(cs[tb - 1:tb, :], carry.shape)

    return pl.pallas_call(
        body, name="fox_prep", grid=(S // tb,),
        in_specs=[pl.BlockSpec((tb, LANES), lambda i: (i, 0)), _full((1, LANES))],
        out_specs=pl.BlockSpec((tb, LANES), lambda i: (i, 0)),
        out_shape=jax.ShapeDtypeStruct((S, LANES), F32),
        scratch_shapes=[pltpu.VMEM((SUBLANES, LANES), F32)],
        compiler_params=_params(("arbitrary",)),
    )(f, bias_row)


def _attn_consts():
    col = jnp.arange(D_PAD)
    row = jnp.arange(D_HEADS)
    head = jnp.arange(LANES)
    place = (row[:, None] // HEAD_DIM == col[None, :] // HEAD_PAD) & (row[:, None] % HEAD_DIM == col[None, :] % HEAD_PAD)

    def stat(offset):
        return ((head[:, None] < N_HEADS) & (col[None, :] == head[:, None] * HEAD_PAD + offset)).astype(BF16)

    def ones(offsets):
        return sum((col % HEAD_PAD == o) for o in offsets).astype(F32).reshape(1, D_PAD)

    def pick(offset):
        gcol = jnp.arange(GROUP_PAD)
        return jnp.stack([((gcol[:, None] % HEAD_PAD == offset) & (head[None, :] == g * GROUP_HEADS + gcol[:, None] // HEAD_PAD))
                          for g in range(GROUPS)]).astype(BF16)

    place = place.astype(BF16)
    return {
        "place": place, "place_t": place.T, "place_t_group": place.T[:GROUP_PAD, :GROUP_HEADS * HEAD_DIM],
        "q_stat": jnp.stack([stat(Q_STAT + j) for j in range(3)]), "k_stat": jnp.stack([stat(K_STAT + j) for j in range(3)]),
        "d_stat": jnp.stack([stat(Q_STAT + j) for j in range(2)]), "l_stat": jnp.stack([stat(L_STAT + j)[:STAT_ROWS] for j in range(3)]),
        "q_ones": ones(range(K_STAT, K_STAT + 3)), "k_ones": ones(list(range(Q_STAT, Q_STAT + 3)) + list(range(L_STAT, L_STAT + 3))),
        "v_ones": ones(range(Q_STAT, Q_STAT + 2)),
        "pick_rows": pick(Q_STAT), "pick_cols": pick(K_STAT),
    }


def _split3(x):
    hi = x.astype(BF16)
    r = x - hi.astype(F32)
    mid = r.astype(BF16)
    return hi, mid, (r - mid.astype(F32)).astype(BF16)


def _split3_dot(x, m):
    return sum(jnp.dot(part, m, preferred_element_type=F32) for part in _split3(x))


def _attn_pack(z, c, k, tm):
    S = z.shape[0]

    def body(q_ref, k_ref, v_ref, c_ref, pl_ref, pt_ref, qs_ref, ks_ref, qo_ref, ko_ref, vo_ref, voc_ref,
             qa_ref, ka_ref, va_ref, vt_ref):
        place = pl_ref[...]
        q = (q_ref[...].astype(F32) * (SCALE * LOG2E)).astype(BF16)
        qa = jnp.dot(q, place, preferred_element_type=F32) + qo_ref[...]
        ka = jnp.dot(k_ref[...], place, preferred_element_type=F32) + ko_ref[...]
        for j, part in enumerate(_split3(c_ref[...] * LOG2E)):
            qa = qa + jnp.dot(part, qs_ref[j], preferred_element_type=F32)
            ka = ka - jnp.dot(part, ks_ref[j], preferred_element_type=F32)
        qa_ref[...] = qa.astype(BF16)
        ka_ref[...] = ka.astype(BF16)
        v = v_ref[...]
        va_ref[...] = (jnp.dot(v, place, preferred_element_type=F32) + vo_ref[...]).astype(BF16)
        vt_ref[...] = (lax.dot_general(pt_ref[...], v, NT, preferred_element_type=F32) + voc_ref[...]).astype(BF16)

    blk = lambda col: pl.BlockSpec((tm, D_HEADS), lambda i: (i, col))
    out = pl.BlockSpec((tm, D_PAD), lambda i: (i, 0))
    pad = jax.ShapeDtypeStruct((S, D_PAD), BF16)
    return pl.pallas_call(
        body, name="attn_pack", grid=(S // tm,),
        in_specs=[blk(2), blk(3), blk(4), pl.BlockSpec((tm, LANES), lambda i: (i, 0)), _full((D_HEADS, D_PAD)), _full((D_PAD, D_HEADS)),
                  _full((3, LANES, D_PAD)), _full((3, LANES, D_PAD)), _full((1, D_PAD)), _full((1, D_PAD)), _full((1, D_PAD)),
                  _full((D_PAD, 1))],
        out_specs=[out, out, out, pl.BlockSpec((None, D_PAD, tm), lambda i: (i, 0, 0))],
        out_shape=[pad, pad, pad, jax.ShapeDtypeStruct((S // tm, D_PAD, tm), BF16)],
        compiler_params=_params(("parallel",)),
    )(z, z, z, c, k["place"], k["place_t"], k["q_stat"], k["k_stat"], k["q_ones"], k["k_ones"], k["v_ones"], k["v_ones"].T)


def _attn_fwd(qa, ka, vat, place_t, tq, shards):
    S = qa.shape[0]
    n = S // tq
    ns = len(shards)
    hand_on_at = (2 * n) // 3

    pairs = [(q, k) for q in range(n) for k in range(q + 1)]
    q_of = jnp.asarray([q for q, _ in pairs], jnp.int32)
    k_of = jnp.asarray([k for _, k in pairs], jnp.int32)

    def body(q_of_ref, k_of_ref, q_ref, k_ref, vt_ref, pt_ref, *rest):
        o_ref, lse_ref = rest[ns:ns + 2]
        m_s, acc_s, ot_s = rest[2 * ns + 2:2 * ns + 5]
        s_s = rest[2 * ns + 5:2 * ns + 7]
        start, hand_on, finish = _gather_ops(rest[:ns], rest[ns + 2:2 * ns + 2], *rest[2 * ns + 7:])
        qi, ki = q_of_ref[pl.program_id(0)], k_of_ref[pl.program_id(0)]

        @pl.when((qi == 0) & (ki == 0))
        def _():
            start()

        @pl.when((qi == hand_on_at) & (ki == 0))
        def _():
            hand_on()

        @pl.when(ki == 0)
        def _():
            m_s[...] = jnp.full_like(m_s, NEG)
            acc_s[...] = jnp.zeros_like(acc_s)

        def step(diagonal):
            chunks = [slice(c * KEY_CHUNK, (c + 1) * KEY_CHUNK) for c in range(tq // KEY_CHUNK)]

            def scores(h, rows, slot):
                sl = slice(h * HEAD_PAD, (h + 1) * HEAD_PAD)
                st = lax.dot_general(k_ref[rows, sl], q_ref[:, sl], NT, preferred_element_type=F32)
                if diagonal:
                    key = rows.start + lax.broadcasted_iota(jnp.int32, (KEY_CHUNK, tq), 0)
                    query = lax.broadcasted_iota(jnp.int32, (KEY_CHUNK, tq), 1)
                    st = jnp.where(query >= key, st, NEG)
                s_s[slot][rows, :] = st
                return jnp.max(st, axis=0, keepdims=True)

            m_cur = functools.reduce(jnp.maximum, [scores(0, rows, 0) for rows in chunks])
            for h in range(N_HEADS):
                sl = slice(h * HEAD_PAD, (h + 1) * HEAD_PAD)
                slot = h % 2
                m_prev = m_s[h][0:1, :]
                m_new = jnp.maximum(m_prev, m_cur)
                acc = jnp.exp2(m_prev - m_new) * acc_s[h]
                m_next = []
                for rows in chunks:
                    if h + 1 < N_HEADS:
                        m_next.append(scores(h + 1, rows, 1 - slot))
                    pt = jnp.exp2(s_s[slot][rows, :] - m_new).astype(BF16)
                    acc = acc + jnp.dot(vt_ref[sl, rows], pt, preferred_element_type=F32)
                acc_s[h] = acc
                m_s[h] = jnp.broadcast_to(m_new, (SUBLANES, tq))
                if m_next:
                    m_cur = functools.reduce(jnp.maximum, m_next)

        @pl.when(ki < qi)
        def _():
            step(False)

        @pl.when(ki == qi)
        def _():
            step(True)
            lse_ref[...] = jnp.zeros_like(lse_ref)
            for h in range(N_HEADS):
                acc = acc_s[h]
                denom = acc[Q_STAT:Q_STAT + 1, :]
                ot_s[h * HEAD_PAD:(h + 1) * HEAD_PAD, :] = (acc / denom).astype(BF16)
                lse_ref[h:h + 1, :] = m_s[h][0:1, :] + jnp.log(denom) * LOG2E
            o_ref[...] = lax.dot_general(ot_s[...], pt_ref[...], TN, preferred_element_type=F32).astype(BF16)

        @pl.when((qi == n - 1) & (ki == n - 1))
        def _():
            finish()

    out = pl.pallas_call(
        body, name="attn_fwd",
        grid_spec=pltpu.PrefetchScalarGridSpec(
            num_scalar_prefetch=2, grid=(len(pairs),),
            in_specs=[pl.BlockSpec((tq, D_PAD), lambda i, qs, ks: (qs[i], 0)),
                      pl.BlockSpec((tq, D_PAD), lambda i, qs, ks: (ks[i], 0)),
                      pl.BlockSpec((None, D_PAD, tq), lambda i, qs, ks: (ks[i], 0, 0)),
                      pl.BlockSpec((D_PAD, D_HEADS), lambda i, qs, ks: (0, 0))]
            + [_ANY] * ns,
            out_specs=[pl.BlockSpec((tq, D_HEADS), lambda i, qs, ks: (qs[i], 0)),
                       pl.BlockSpec((STAT_ROWS, tq), lambda i, qs, ks: (0, qs[i]))] + [_ANY] * ns,
            scratch_shapes=[pltpu.VMEM((N_HEADS, SUBLANES, tq), F32), pltpu.VMEM((N_HEADS, HEAD_PAD, tq), F32),
                            pltpu.VMEM((D_PAD, tq), BF16), pltpu.VMEM((tq, tq), F32), pltpu.VMEM((tq, tq), F32)] + _gather_sems(ns)),
        out_shape=[jax.ShapeDtypeStruct((S, D_HEADS), BF16), jax.ShapeDtypeStruct((STAT_ROWS, S), F32)] + _gather_shapes(shards),
        compiler_params=_params(("arbitrary",)),
    )(q_of, k_of, qa, ka, vat, place_t, *shards)
    return out[0], out[1], out[2:]


def _layer_norm_heads(v, seg_avg):
    mu = _split_dot(v, seg_avg)
    d = v - mu
    var = _split_dot(d * d, seg_avg)
    rstd = lax.rsqrt(var + EPS)
    return d * rstd, rstd


def _gate_mix(vn_blk, w_ref, bias):
    acc = bias
    for h in range(N_HEADS):
        vh = jnp.where(_head_mask(h, SG_BLOCK), vn_blk, 0.0).astype(BF16)
        acc = acc + jnp.dot(w_ref[h], vh, preferred_element_type=F32)
    return acc


def _gate_fwd(z, w_mask, ln_row, b_full, seg_avg, tm):
    S = z.shape[0]

    def body(zu_ref, zv_ref, w_ref, ln_ref, b_ref, avg_ref, o_ref):
        u = _gelu(zu_ref[...].astype(F32))
        v = _gelu(zv_ref[...].astype(F32))
        vhat, _ = _layer_norm_heads(v, avg_ref[...])
        vn = vhat * ln_ref[...]
        for b in range(tm // SG_BLOCK):
            rows = slice(b * SG_BLOCK, (b + 1) * SG_BLOCK)
            mixed = _gate_mix(vn[rows], w_ref, b_ref[...])
            o_ref[rows, :] = (u[rows] * mixed).astype(BF16)

    return pl.pallas_call(
        body, name="gate_fwd", grid=(S // tm,),
        in_specs=[pl.BlockSpec((tm, D_HEADS), lambda i: (i, 0)), pl.BlockSpec((tm, D_HEADS), lambda i: (i, 1)),
                  _full((N_HEADS, SG_BLOCK, SG_BLOCK)), _full((1, D_HEADS)), _full((SG_BLOCK, D_HEADS)),
                  _full((D_HEADS, D_HEADS))],
        out_specs=pl.BlockSpec((tm, D_HEADS), lambda i: (i, 0)),
        out_shape=jax.ShapeDtypeStruct((S, D_HEADS), BF16),
        compiler_params=_params(("parallel",)),
    )(z, z, w_mask, ln_row, b_full, seg_avg)


def _mix_out(x, out_a, out_b, w_out, g2, tm):
    S = x.shape[0]

    def body(x_ref, a_ref, b_ref, w_ref, g_ref, x1_ref, h_ref):
        y = jnp.dot(a_ref[...], w_ref[:D_HEADS, :], preferred_element_type=F32)
        y = y + jnp.dot(b_ref[...], w_ref[D_HEADS:, :], preferred_element_type=F32)
        x1 = x_ref[...] + y
        x1_ref[...] = x1
        r = lax.rsqrt(jnp.mean(x1 * x1, axis=-1, keepdims=True) + EPS)
        h_ref[...] = (x1 * r * g_ref[...]).astype(BF16)

    row = lambda w: pl.BlockSpec((tm, w), lambda i: (i, 0))
    return pl.pallas_call(
        body, name="mix_out", grid=(S // tm,),
        in_specs=[row(D_MODEL), row(D_HEADS), row(D_HEADS), _full((D_MODEL, D_MODEL)), _full((1, D_MODEL))],
        out_specs=[row(D_MODEL), row(D_MODEL)],
        out_shape=[jax.ShapeDtypeStruct((S, D_MODEL), F32), jax.ShapeDtypeStruct((S, D_MODEL), BF16)],
        compiler_params=_params(("parallel",)),
    )(x, out_a, out_b, w_out, g2)


def _up_proj(h2, w_up_q, tm):
    S = h2.shape[0]
    nq, _, wq = w_up_q.shape

    def body(h_ref, w_ref, a_ref):
        a_ref[...] = jnp.dot(h_ref[...], w_ref[...], preferred_element_type=F32).astype(BF16)

    return pl.pallas_call(
        body, name="up_proj", grid=(nq, S // tm),
        in_specs=[pl.BlockSpec((tm, D_MODEL), lambda j, i: (i, 0)), pl.BlockSpec((None, D_MODEL, wq), lambda j, i: (j, 0, 0))],
        out_specs=pl.BlockSpec((tm, wq), lambda j, i: (i, j)),
        out_shape=jax.ShapeDtypeStruct((S, nq * wq), BF16),
        compiler_params=_params(("parallel", "parallel")),
    )(h2, w_up_q)


def _shift_down(a, halo, k):
    tm = a.shape[0]
    ra = pltpu.roll(a, k, 0)
    rh = pltpu.roll(halo, k, 0)
    row = lax.broadcasted_iota(jnp.int32, halo.shape, 0)
    top = jnp.where(row < k, rh, ra[0:SUBLANES])
    return jnp.concatenate([top, ra[SUBLANES:tm]], axis=0)


def _shift_up(a, halo, k):
    tm = a.shape[0]
    ra = pltpu.roll(a, tm - k, 0)
    rh = pltpu.roll(halo, SUBLANES - k, 0)
    row = lax.broadcasted_iota(jnp.int32, halo.shape, 0)
    bottom = jnp.where(row >= SUBLANES - k, rh, ra[tm - SUBLANES:tm])
    return jnp.concatenate([ra[0:tm - SUBLANES], bottom], axis=0)


def _shift_matrices(tm):
    row = lax.broadcasted_iota(jnp.int32, (tm, tm), 0)
    col = lax.broadcasted_iota(jnp.int32, (tm, tm), 1)
    return [(row == col + k).astype(BF16) for k in (1, 2)]


def _conv_taps(a, halo, first, shifts):
    tm = a.shape[0]
    halo = halo.astype(F32) * jnp.where(first, 0.0, 1.0)
    if shifts is None:
        a = a.astype(F32)
        return a, _shift_down(a, halo, 1), _shift_down(a, halo, 2)
    row8 = lax.broadcasted_iota(jnp.int32, halo.shape, 0)
    taps = [a.astype(F32)]
    for k, shift in zip((1, 2), shifts):
        down = jnp.dot(shift, a, preferred_element_type=F32)
        top = down[0:SUBLANES] + jnp.where(row8 < k, pltpu.roll(halo, k, 0), 0.0)
        taps.append(jnp.concatenate([top, down[SUBLANES:tm]], axis=0))
    return taps


def _conv_gate_val(refs, shifts, cols, first):
    ag_ref, av_ref, hg_ref, hv_ref, wg_ref, wv_ref, bg_ref, bv_ref = refs
    g0, g1, g2 = _conv_taps(ag_ref[:, cols], hg_ref[:, cols], first, shifts)
    gate = wg_ref[2:3, cols] * g0 + wg_ref[1:2, cols] * g1 + wg_ref[0:1, cols] * g2 + bg_ref[:, cols]
    v0, v1, v2 = _conv_taps(av_ref[:, cols], hv_ref[:, cols], first, shifts)
    val = wv_ref[2:3, cols] * v0 + wv_ref[1:2, cols] * v1 + wv_ref[0:1, cols] * v2 + bv_ref[:, cols]
    return gate, val, (g2, g1, g0), (v2, v1, v0)


_FF_CHUNKS = [slice(j * FF_CHUNK, (j + 1) * FF_CHUNK) for j in range(D_FF // FF_CHUNK)]


def _conv_specs(tm):
    step = tm // SUBLANES
    prev = lambda i: jnp.maximum(i * step - 1, 0)
    return [pl.BlockSpec((tm, D_FF), lambda i: (i, 0)), pl.BlockSpec((tm, D_FF), lambda i: (i, 1)),
            pl.BlockSpec((SUBLANES, D_FF), lambda i: (prev(i), 0)), pl.BlockSpec((SUBLANES, D_FF), lambda i: (prev(i), 1))]


def _ffn_fwd_loss(a, w_conv, b_conv, w_down, x1, g3, target, tm):
    S = x1.shape[0]

    def body(ag_ref, av_ref, hg_ref, hv_ref, wg_ref, wv_ref, bg_ref, bv_ref, wd_ref, x1_ref, g_ref, t_ref,
             dx2_ref, loss_ref, dg_ref):
        i = pl.program_id(0)

        @pl.when(i == 0)
        def _():
            loss_ref[...] = jnp.zeros_like(loss_ref)
            dg_ref[...] = jnp.zeros_like(dg_ref)

        x2 = x1_ref[...]
        for cols in _FF_CHUNKS:
            gate, val, _, _ = _conv_gate_val((ag_ref, av_ref, hg_ref, hv_ref, wg_ref, wv_ref, bg_ref, bv_ref), None, cols, i == 0)
            half = 0.5 * gate
            y = ((half + half * jnp.tanh(half)) * val).astype(BF16)
            x2 = x2 + jnp.dot(y, wd_ref[cols, :], preferred_element_type=F32)
        r = lax.rsqrt(jnp.mean(x2 * x2, axis=-1, keepdims=True) + EPS)
        xhat = x2 * r
        gg = g_ref[...]
        err = xhat * gg - t_ref[...]
        loss_ref[...] += jnp.sum(err * err, axis=0, keepdims=True)
        dy = err * (1.0 / D_MODEL)
        dg_ref[...] += jnp.sum(dy * xhat, axis=0, keepdims=True)
        dxhat = dy * gg
        dx2_ref[...] = r * (dxhat - xhat * jnp.mean(dxhat * xhat, axis=-1, keepdims=True))

    row = lambda w: pl.BlockSpec((tm, w), lambda i: (i, 0))
    half = lambda r: [pl.BlockSpec((r, D_FF), lambda i: (0, 0)), pl.BlockSpec((r, D_FF), lambda i: (0, 1))]
    return pl.pallas_call(
        body, name="ffn_fwd_loss", grid=(S // tm,),
        in_specs=_conv_specs(tm) + half(3) + half(1) + [_full((D_FF, D_MODEL)), row(D_MODEL), _full((1, D_MODEL)), row(D_MODEL)],
        out_specs=[row(D_MODEL), _full((1, D_MODEL)), _full((1, D_MODEL))],
        out_shape=[jax.ShapeDtypeStruct((S, D_MODEL), F32), jax.ShapeDtypeStruct((1, D_MODEL), F32),
                   jax.ShapeDtypeStruct((1, D_MODEL), F32)],
        compiler_params=_params(("arbitrary",)),
    )(a, a, a, a, w_conv, w_conv, b_conv, b_conv, w_down, x1, g3, target)


def _ffn_bwd_gate(dx2, a, w_conv, b_conv, w_down, tm):
    S = dx2.shape[0]

    def body(dx_ref, ag_ref, av_ref, hg_ref, hv_ref, wg_ref, wv_ref, bg_ref, bv_ref, wd_ref,
             dc_ref, y_ref, dw_ref, db_ref):
        i = pl.program_id(0)

        @pl.when(i == 0)
        def _():
            dw_ref[...] = jnp.zeros_like(dw_ref)
            db_ref[...] = jnp.zeros_like(db_ref)

        dx = dx_ref[...].astype(BF16)
        shifts = _shift_matrices(tm)
        for cols in _FF_CHUNKS:
            gate, val, gtaps, vtaps = _conv_gate_val((ag_ref, av_ref, hg_ref, hv_ref, wg_ref, wv_ref, bg_ref, bv_ref), shifts, cols, i == 0)
            sg = _sigmoid(gate)
            act = gate * sg
            y_ref[:, cols] = (act * val).astype(BF16)
            dy = lax.dot_general(dx, wd_ref[cols, :], NT, preferred_element_type=F32)
            dgate = dy * val * (sg + act - act * sg)
            dval = dy * act
            for d, taps, out in ((dgate, gtaps, cols), (dval, vtaps, slice(D_FF + cols.start, D_FF + cols.stop))):
                dc_ref[:, out] = d.astype(BF16)
                db_ref[0:1, out] += jnp.sum(d, axis=0, keepdims=True)
                for j in range(3):
                    dw_ref[j:j + 1, out] += jnp.sum(d * taps[j], axis=0, keepdims=True)

    row = lambda w: pl.BlockSpec((tm, w), lambda i: (i, 0))
    half = lambda r: [pl.BlockSpec((r, D_FF), lambda i: (0, 0)), pl.BlockSpec((r, D_FF), lambda i: (0, 1))]
    return pl.pallas_call(
        body, name="ffn_bwd_gate", grid=(S // tm,),
        in_specs=[row(D_MODEL)] + _conv_specs(tm) + half(3) + half(1) + [_full((D_FF, D_MODEL))],
        out_specs=[row(2 * D_FF), row(D_FF), _full((SUBLANES, 2 * D_FF)), _full((1, 2 * D_FF))],
        out_shape=[jax.ShapeDtypeStruct((S, 2 * D_FF), BF16), jax.ShapeDtypeStruct((S, D_FF), BF16),
                   jax.ShapeDtypeStruct((SUBLANES, 2 * D_FF), F32), jax.ShapeDtypeStruct((1, 2 * D_FF), F32)],
        compiler_params=_params(("arbitrary",)),
    )(dx2, a, a, a, a, w_conv, w_conv, b_conv, b_conv, w_down)


def _conv_bwd(dc, w_conv, tm, tn):
    S, C = dc.shape
    step = tm // SUBLANES
    last_blk = S // SUBLANES - 1

    def body(d_ref, nx_ref, w_ref, o_ref):
        last = pl.program_id(0) == pl.num_programs(0) - 1
        row = lax.broadcasted_iota(jnp.int32, (tm, tm), 0)
        col = lax.broadcasted_iota(jnp.int32, (tm, tm), 1)
        row8 = lax.broadcasted_iota(jnp.int32, (SUBLANES, FF_CHUNK), 0)
        ups = [(row + k == col).astype(BF16) for k in (1, 2)]
        for c0 in range(0, tn, FF_CHUNK):
            cols = slice(c0, c0 + FF_CHUNK)
            d = d_ref[:, cols]
            nx = nx_ref[:, cols].astype(F32) * jnp.where(last, 0.0, 1.0)
            out = w_ref[2:3, cols] * d.astype(F32)
            for k, up in zip((1, 2), ups):
                moved = jnp.dot(up, d, preferred_element_type=F32)
                bottom = moved[tm - SUBLANES:tm] + jnp.where(row8 >= SUBLANES - k, pltpu.roll(nx, SUBLANES - k, 0), 0.0)
                out = out + w_ref[2 - k:3 - k, cols] * jnp.concatenate([moved[0:tm - SUBLANES], bottom], axis=0)
            o_ref[:, cols] = out.astype(BF16)

    return pl.pallas_call(
        body, name="conv_bwd", grid=(S // tm, C // tn),
        in_specs=[pl.BlockSpec((tm, tn), lambda i, j: (i, j)),
                  pl.BlockSpec((SUBLANES, tn), lambda i, j: (jnp.minimum((i + 1) * step, last_blk), j)),
                  pl.BlockSpec((3, tn), lambda i, j: (0, j))],
        out_specs=pl.BlockSpec((tm, tn), lambda i, j: (i, j)),
        out_shape=jax.ShapeDtypeStruct((S, C), BF16),
        compiler_params=_params(("parallel", "parallel")),
    )(dc, dc, w_conv)


def _matmul_tn(a, b, name, bm, bn, tk, col_a=0, col_b=0, quarters=None):
    S = a.shape[0]
    gm, gn = quarters if quarters else (1, 1)
    nk = S // tk

    def body(a_ref, b_ref, o_ref):
        @pl.when(pl.program_id(2) == 0)
        def _():
            o_ref[...] = jnp.zeros_like(o_ref)

        o_ref[...] += lax.dot_general(a_ref[...].astype(BF16), b_ref[...].astype(BF16), TN, preferred_element_type=F32)

    if quarters and gn > 1:
        out_spec = pl.BlockSpec((None, bm, bn), lambda i, j, k: (j, i, 0))
        out_shape = jax.ShapeDtypeStruct((gn, gm * bm, bn), F32)
    else:
        out_spec = pl.BlockSpec((bm, bn), lambda i, j, k: (i, j))
        out_shape = jax.ShapeDtypeStruct((gm * bm, gn * bn), F32)
    return pl.pallas_call(
        body, name=name, grid=(gm, gn, nk),
        in_specs=[pl.BlockSpec((tk, bm), lambda i, j, k: (k, col_a * gm + i)),
                  pl.BlockSpec((tk, bn), lambda i, j, k: (k, col_b * gn + j))],
        out_specs=out_spec, out_shape=out_shape,
        compiler_params=_params(("parallel", "parallel", "arbitrary")),
    )(a, b)


def _up_bwd(dact, w_up_q, x1, g2, dx2, tm):
    S = x1.shape[0]
    nq, _, wq = w_up_q.shape

    def body(d_ref, w_ref, x_ref, g_ref, dx2_ref, dx1_ref, dg_ref):
        @pl.when(pl.program_id(0) == 0)
        def _():
            dg_ref[...] = jnp.zeros_like(dg_ref)

        dh = jnp.zeros((tm, D_MODEL), F32)
        for j in range(nq):
            dh = dh + lax.dot_general(d_ref[:, j * wq:(j + 1) * wq], w_ref[j], NT, preferred_element_type=F32)
        dx, dg = _rms_bwd(dh, x_ref[...], g_ref[...])
        dg_ref[...] += dg
        dx1_ref[...] = dx2_ref[...] + dx

    row = lambda w: pl.BlockSpec((tm, w), lambda i: (i, 0))
    return pl.pallas_call(
        body, name="up_bwd", grid=(S // tm,),
        in_specs=[row(nq * wq), pl.BlockSpec((nq, D_MODEL, wq), lambda i: (0, 0, 0), pipeline_mode=pl.Buffered(1)),
                  row(D_MODEL), _full((1, D_MODEL)), row(D_MODEL)],
        out_specs=[row(D_MODEL), _full((1, D_MODEL))],
        out_shape=[jax.ShapeDtypeStruct((S, D_MODEL), F32), jax.ShapeDtypeStruct((1, D_MODEL), F32)],
        compiler_params=_params(("arbitrary",)),
    )(dact, w_up_q, x1, g2, dx2)


def _out_bwd(dx1, w_out, tm):
    S = dx1.shape[0]

    def body(d_ref, w_ref, o_ref):
        o_ref[...] = lax.dot_general(d_ref[...].astype(BF16), w_ref[...], NT, preferred_element_type=F32).astype(BF16)

    return pl.pallas_call(
        body, name="out_bwd", grid=(S // tm,),
        in_specs=[pl.BlockSpec((tm, D_MODEL), lambda i: (i, 0)), _full((D_MODEL, D_MODEL))],
        out_specs=pl.BlockSpec((tm, D_MODEL), lambda i: (i, 0)),
        out_shape=jax.ShapeDtypeStruct((S, D_MODEL), BF16),
        compiler_params=_params(("parallel",)),
    )(dx1, w_out)


def _gate_bwd(z, dcat, w_mask, w_mask_t, ln_row, b_full, seg_avg, head_ind, tm, swap):
    S = z.shape[0]
    nb = tm // SG_BLOCK
    ns = len(swap)

    def body(zu_ref, zv_ref, do_ref, w_ref, wt_ref, ln_ref, b_ref, avg_ref, ind_ref, *rest):
        dzu_ref, dzv_ref, dw_ref, db_ref, dln_ref = rest[ns:ns + 5]
        dvn_s, dbf_s = rest[2 * ns + 5:2 * ns + 7]
        swap_start, swap_finish = _swap_ops(rest[:ns], rest[ns + 5:2 * ns + 5], *rest[2 * ns + 7:])
        i = pl.program_id(0)

        @pl.when(i == 0)
        def _():
            swap_start()
            dw_ref[...] = jnp.zeros_like(dw_ref)
            dln_ref[...] = jnp.zeros_like(dln_ref)
            dbf_s[...] = jnp.zeros_like(dbf_s)

        zu = zu_ref[...].astype(F32)
        zv = zv_ref[...].astype(F32)
        u = _gelu(zu)
        v = _gelu(zv)
        avg = avg_ref[...]
        vhat, rstd = _layer_norm_heads(v, avg)
        ln = ln_ref[...]
        vn = vhat * ln
        for b in range(nb):
            rows = slice(b * SG_BLOCK, (b + 1) * SG_BLOCK)
            vn_b = vn[rows]
            mixed = _gate_mix(vn_b, w_ref, b_ref[...])
            do = do_ref[rows, :].astype(F32)
            dzu_ref[rows, :] = (do * mixed * _gelu_grad(zu[rows])).astype(BF16)
            dmix = do * u[rows]
            dbf_s[...] += dmix
            vn_bf = vn_b.astype(BF16)
            dvn = jnp.zeros((SG_BLOCK, D_HEADS), F32)
            for h in range(N_HEADS):
                dmh = jnp.where(_head_mask(h, SG_BLOCK), dmix, 0.0).astype(BF16)
                dw_ref[h] += lax.dot_general(dmh, vn_bf, NT, preferred_element_type=F32)
                dvn = dvn + jnp.dot(wt_ref[h], dmh, preferred_element_type=F32)
            dvn_s[rows, :] = dvn
        dvn = dvn_s[...]
        dln_ref[...] += jnp.sum(dvn * vhat, axis=0, keepdims=True)
        dvhat = dvn * ln
        dv = rstd * (dvhat - _split_dot(dvhat, avg) - vhat * _split_dot(dvhat * vhat, avg))
        dzv_ref[...] = (dv * _gelu_grad(zv)).astype(BF16)

        @pl.when(i == pl.num_programs(0) - 1)
        def _():
            r = lax.broadcasted_iota(jnp.int32, (SG_BLOCK, SG_BLOCK), 0) // CHUNK
            s = lax.broadcasted_iota(jnp.int32, (SG_BLOCK, SG_BLOCK), 1) // CHUNK
            for h in range(N_HEADS):
                dw_ref[h] = jnp.where(r >= s, dw_ref[h], 0.0)
            db_ref[...] = _split_dot(dbf_s[...], ind_ref[...])
            swap_finish()

    row = lambda col: pl.BlockSpec((tm, D_HEADS), lambda i: (i, col))
    wspec = _full((N_HEADS, SG_BLOCK, SG_BLOCK))
    out = pl.pallas_call(
        body, name="gate_bwd", grid=(S // tm,),
        in_specs=[row(0), row(1), row(0), wspec, wspec, _full((1, D_HEADS)), _full((SG_BLOCK, D_HEADS)),
                  _full((D_HEADS, D_HEADS)), _full((D_HEADS, LANES))] + [_ANY] * ns,
        out_specs=[row(0), row(0), wspec, _full((SG_BLOCK, LANES)), _full((1, D_HEADS))] + [_ANY] * ns,
        out_shape=[jax.ShapeDtypeStruct((S, D_HEADS), BF16), jax.ShapeDtypeStruct((S, D_HEADS), BF16),
                   jax.ShapeDtypeStruct((N_HEADS, SG_BLOCK, SG_BLOCK), F32), jax.ShapeDtypeStruct((SG_BLOCK, LANES), F32),
                   jax.ShapeDtypeStruct((1, D_HEADS), F32)] + _swap_shapes(swap),
        scratch_shapes=[pltpu.VMEM((tm, D_HEADS), F32), pltpu.VMEM((SG_BLOCK, D_HEADS), F32)] + _swap_sems(ns),
        compiler_params=_params(("arbitrary",)),
    )(z, z, dcat, w_mask, w_mask_t, ln_row, b_full, seg_avg, head_ind, *swap)
    return out[:5], out[5:]


def _attn_pack_grad(o, dcat, qa, lse, head_ind, k, tm):
    S = o.shape[0]

    def body(o_ref, do_ref, qa_ref, lse_ref, ind_ref, pl_ref, pt_ref, eye_ref, ds_ref, dst_ref, ls_ref, lst_ref,
             dop_ref, qb_ref, dot_ref, qbt_ref):
        do = do_ref[...]
        delta = _split_dot(o_ref[...].astype(F32) * do.astype(F32), ind_ref[...])
        hi = delta.astype(BF16)
        lo = (delta - hi.astype(F32)).astype(BF16)
        dop = jnp.dot(do, pl_ref[...], preferred_element_type=F32)
        dop = dop - jnp.dot(hi, ds_ref[0], preferred_element_type=F32) - jnp.dot(lo, ds_ref[1], preferred_element_type=F32)
        for g in range(GROUPS):
            dop_ref[g] = dop[:, g * GROUP_PAD:(g + 1) * GROUP_PAD].astype(BF16)
        dot = lax.dot_general(pt_ref[...], do, NT, preferred_element_type=F32)
        dot = dot - lax.dot_general(dst_ref[0], hi, NT, preferred_element_type=F32)
        dot = dot - lax.dot_general(dst_ref[1], lo, NT, preferred_element_type=F32)
        dot_ref[...] = dot.astype(BF16)
        qa = qa_ref[...]
        qb = qa.astype(F32)
        qbt = lax.dot_general(eye_ref[...], qa, NT, preferred_element_type=F32)
        for j, part in enumerate(_split3(lse_ref[...])):
            qb = qb - lax.dot_general(part, ls_ref[j], TN, preferred_element_type=F32)
            qbt = qbt - jnp.dot(lst_ref[j], part, preferred_element_type=F32)
        for g in range(GROUPS):
            qb_ref[g] = qb[:, g * GROUP_PAD:(g + 1) * GROUP_PAD].astype(BF16)
        qbt_ref[...] = qbt.astype(BF16)

    pad = pl.BlockSpec((tm, D_PAD), lambda i: (i, 0))
    padt = pl.BlockSpec((None, D_PAD, tm), lambda i: (i, 0, 0))
    return pl.pallas_call(
        body, name="attn_pack_grad", grid=(S // tm,),
        in_specs=[pl.BlockSpec((tm, D_HEADS), lambda i: (i, 0)), pl.BlockSpec((tm, D_HEADS), lambda i: (i, 1)), pad,
                  pl.BlockSpec((STAT_ROWS, tm), lambda i: (0, i)), _full((D_HEADS, LANES)), _full((D_HEADS, D_PAD)),
                  _full((D_PAD, D_HEADS)), _full((D_PAD, D_PAD)), _full((2, LANES, D_PAD)), _full((2, D_PAD, LANES)),
                  _full((3, STAT_ROWS, D_PAD)), _full((3, D_PAD, STAT_ROWS))],
        out_specs=[pl.BlockSpec((GROUPS, tm, GROUP_PAD), lambda i: (0, i, 0))] * 2 + [padt, padt],
        out_shape=[jax.ShapeDtypeStruct((GROUPS, S, GROUP_PAD), BF16)] * 2 + [jax.ShapeDtypeStruct((S // tm, D_PAD, tm), BF16)] * 2,
        compiler_params=_params(("parallel",)),
    )(o, dcat, qa, lse, head_ind, k["place"], k["place_t"], jnp.eye(D_PAD, dtype=BF16), k["d_stat"],
      jnp.swapaxes(k["d_stat"], 1, 2), k["l_stat"], jnp.swapaxes(k["l_stat"], 1, 2))


def _attn_bwd(qb, qbt, ka, va, dop, dopt, k, tq, sums16):
    S = ka.shape[0]
    n = S // tq
    ns = len(sums16)

    pairs = [(kb, q) for kb in range(n) for q in range(kb, n)]
    k_of = jnp.asarray([kb for kb, _ in pairs], jnp.int32)
    q_of = jnp.asarray([q for _, q in pairs], jnp.int32)

    def body(k_of_ref, q_of_ref, q_ref, qt_ref, k_ref, v_ref, do_ref, dot_ref, pt_ref, pick_ref, *rest):
        dq_hbm, dk_ref, dv_ref, dcc_ref = rest[ns:ns + 4]
        dq_s, dk_s, dv_s = rest[2 * ns + 4:2 * ns + 7]
        s_s, d_s = rest[2 * ns + 7:2 * ns + 9], rest[2 * ns + 9:2 * ns + 11]
        sem = rest[2 * ns + 11]
        scatter_start, scatter_finish = _scatter_ops(rest[:ns], rest[ns + 4:2 * ns + 4], *rest[2 * ns + 12:])
        g = pl.program_id(0)
        ki, qi = k_of_ref[pl.program_id(1)], q_of_ref[pl.program_id(1)]

        @pl.when((g == 0) & (ki == 0) & (qi == 0))
        def _():
            scatter_start()

        @pl.when((ki == 0) & (qi == 0))
        def _():
            dq_s[...] = jnp.zeros_like(dq_s)

        @pl.when(qi == ki)
        def _():
            dk_s[...] = jnp.zeros_like(dk_s)
            dv_s[...] = jnp.zeros_like(dv_s)

        def step(diagonal):
            chunks = [slice(c * KEY_CHUNK, (c + 1) * KEY_CHUNK) for c in range(tq // KEY_CHUNK)]

            def scores(hh, rows, slot):
                sl = slice(hh * HEAD_PAD, (hh + 1) * HEAD_PAD)
                s_s[slot][rows, :] = lax.dot_general(q_ref[rows, sl], k_ref[:, sl], NT, preferred_element_type=F32)
                d_s[slot][rows, :] = lax.dot_general(do_ref[rows, sl], v_ref[:, sl], NT, preferred_element_type=F32)

            for rows in chunks:
                scores(0, rows, 0)
            for hh in range(GROUP_HEADS):
                sl = slice(hh * HEAD_PAD, (hh + 1) * HEAD_PAD)
                slot = hh % 2
                dv, dk = dv_s[sl, :], dk_s[sl, :]
                for rows in chunks:
                    if hh + 1 < GROUP_HEADS:
                        scores(hh + 1, rows, 1 - slot)
                    p = jnp.exp2(s_s[slot][rows, :])
                    if diagonal:
                        row = rows.start + lax.broadcasted_iota(jnp.int32, (KEY_CHUNK, tq), 0)
                        col = lax.broadcasted_iota(jnp.int32, (KEY_CHUNK, tq), 1)
                        p = jnp.where(row >= col, p, 0.0)
                    ds = (p * d_s[slot][rows, :]).astype(BF16)
                    dv = dv + jnp.dot(dot_ref[sl, rows], p.astype(BF16), preferred_element_type=F32)
                    dk = dk + jnp.dot(qt_ref[sl, rows], ds, preferred_element_type=F32)
                    qrows = pl.ds(pl.multiple_of(qi * tq + rows.start, KEY_CHUNK), KEY_CHUNK)
                    dq_s[qrows, sl] += jnp.dot(ds, k_ref[:, sl], preferred_element_type=F32)
                dv_s[sl, :] = dv
                dk_s[sl, :] = dk

        @pl.when(qi > ki)
        def _():
            step(False)

        @pl.when(qi == ki)
        def _():
            step(True)

        @pl.when(qi == n - 1)
        def _():
            dk = dk_s[...]
            pt = pt_ref[...]
            dk_ref[...] = lax.dot_general((dk * (1.0 / LOG2E)).astype(BF16), pt, TN, preferred_element_type=F32).astype(BF16)
            dv_ref[...] = lax.dot_general(dv_s[...].astype(BF16), pt, TN, preferred_element_type=F32).astype(BF16)
            dcc_ref[...] = sum(lax.dot_general(part, pick_ref[...], TN, preferred_element_type=F32) for part in _split3(dk))

        @pl.when((ki == n - 1) & (qi == n - 1))
        def _():
            cp = pltpu.make_async_copy(dq_s, dq_hbm.at[g], sem)
            cp.start()
            cp.wait()

        @pl.when((g == GROUPS - 1) & (ki == n - 1) & (qi == n - 1))
        def _():
            scatter_finish()

    gw = GROUP_HEADS * HEAD_DIM
    qspec = pl.BlockSpec((None, tq, GROUP_PAD), lambda g, i, ks, qs: (g, qs[i], 0))
    qtspec = pl.BlockSpec((None, GROUP_PAD, tq), lambda g, i, ks, qs: (qs[i], g, 0))
    kspec = pl.BlockSpec((tq, GROUP_PAD), lambda g, i, ks, qs: (ks[i], g))
    kout = pl.BlockSpec((tq, gw), lambda g, i, ks, qs: (ks[i], g))
    out = pl.pallas_call(
        body, name="attn_bwd",
        grid_spec=pltpu.PrefetchScalarGridSpec(
            num_scalar_prefetch=2, grid=(GROUPS, len(pairs)),
            in_specs=[qspec, qtspec, kspec, kspec, qspec, qtspec, pl.BlockSpec((GROUP_PAD, gw), lambda g, i, ks, qs: (0, 0)),
                      pl.BlockSpec((None, GROUP_PAD, LANES), lambda g, i, ks, qs: (g, 0, 0))] + [_ANY] * ns,
            out_specs=[_ANY, kout, kout, pl.BlockSpec((None, tq, LANES), lambda g, i, ks, qs: (g, ks[i], 0))] + [_ANY] * ns,
            scratch_shapes=[pltpu.VMEM((S, GROUP_PAD), F32), pltpu.VMEM((GROUP_PAD, tq), F32), pltpu.VMEM((GROUP_PAD, tq), F32),
                            pltpu.VMEM((tq, tq), F32), pltpu.VMEM((tq, tq), F32), pltpu.VMEM((tq, tq), F32),
                            pltpu.VMEM((tq, tq), F32), pltpu.SemaphoreType.DMA]
            + _scatter_sems(ns)),
        out_shape=[jax.ShapeDtypeStruct((GROUPS, S, GROUP_PAD), F32), jax.ShapeDtypeStruct((S, D_HEADS), BF16),
                   jax.ShapeDtypeStruct((S, D_HEADS), BF16), jax.ShapeDtypeStruct((GROUPS, S, LANES), F32)]
        + _scatter_shapes(sums16),
        compiler_params=_params(("arbitrary", "arbitrary")),
    )(k_of, q_of, qb, qbt, ka, va, dop, dopt, k["place_t_group"], k["pick_cols"], *sums16)
    return out[0], out[1], out[2], out[3], out[4:]


def _attn_unpack(dqp, dcc, k, tm):
    S = dqp.shape[1]
    gw = GROUP_HEADS * HEAD_DIM

    def body(dqp_ref, dcc_ref, pt_ref, pick_ref, dq_ref, dc_ref):
        dc = jnp.zeros((tm, LANES), F32)
        for g in range(GROUPS):
            x = dqp_ref[g]
            dq_ref[:, g * gw:(g + 1) * gw] = jnp.dot((x * SCALE).astype(BF16), pt_ref[...], preferred_element_type=F32).astype(BF16)
            dc = dc + _split3_dot(x, pick_ref[g]) - dcc_ref[g]
        dc_ref[...] = dc

    return pl.pallas_call(
        body, name="attn_unpack", grid=(S // tm,),
        in_specs=[pl.BlockSpec((GROUPS, tm, GROUP_PAD), lambda i: (0, i, 0)), pl.BlockSpec((GROUPS, tm, LANES), lambda i: (0, i, 0)),
                  _full((GROUP_PAD, gw)), _full((GROUPS, GROUP_PAD, LANES))],
        out_specs=[pl.BlockSpec((tm, D_HEADS), lambda i: (i, 0)), pl.BlockSpec((tm, LANES), lambda i: (i, 0))],
        out_shape=[jax.ShapeDtypeStruct((S, D_HEADS), BF16), jax.ShapeDtypeStruct((S, LANES), F32)],
        compiler_params=_params(("parallel",)),
    )(dqp, dcc, k["place_t_group"], k["pick_rows"])


def _fox_bwd(dc, f, bias_row, tb):
    S = f.shape[0]
    nb = S // tb

    def body(dc_ref, f_ref, b_ref, df_ref, dbias_ref, carry):
        @pl.when(pl.program_id(0) == 0)
        def _():
            carry[...] = jnp.zeros_like(carry)
            dbias_ref[...] = jnp.zeros_like(dbias_ref)

        r = lax.broadcasted_iota(jnp.int32, (tb, tb), 0)
        s = lax.broadcasted_iota(jnp.int32, (tb, tb), 1)
        tri = (s >= r).astype(F32)
        rc = jnp.dot(tri, dc_ref[...], precision=lax.Precision.HIGHEST, preferred_element_type=F32) + carry[0:1, :]
        carry[...] = jnp.broadcast_to(rc[0:1, :], carry.shape)
        lane = lax.broadcasted_iota(jnp.int32, (tb, LANES), 1)
        df = jnp.where(lane < N_HEADS, rc * jax.nn.sigmoid(-(f_ref[...] + b_ref[...])), 0.0)
        df_ref[...] = df.astype(BF16)
        dbias_ref[...] += jnp.sum(df, axis=0, keepdims=True)

    rev = pl.BlockSpec((tb, LANES), lambda i: (nb - 1 - i, 0))
    return pl.pallas_call(
        body, name="fox_bwd", grid=(nb,),
        in_specs=[rev, rev, _full((1, LANES))],
        out_specs=[rev, _full((1, LANES))],
        out_shape=[jax.ShapeDtypeStruct((S, LANES), BF16), jax.ShapeDtypeStruct((1, LANES), F32)],
        scratch_shapes=[pltpu.VMEM((SUBLANES, LANES), F32)],
        compiler_params=_params(("arbitrary",)),
    )(dc, f, bias_row)


_DZ_WIDTHS = (D_HEADS,) * 5 + (LANES,)


def _in_bwd(pieces, w_in, x, g1, dx1, tm, sums16):
    S = x.shape[0]
    ns = len(sums16)

    def body(*refs):
        p_refs, (w_ref, x_ref, g_ref, dx1_ref) = refs[:6], refs[6:10]
        dx_ref, dg_ref = refs[10 + ns:12 + ns]
        scatter_start, scatter_finish = _scatter_ops(refs[10:10 + ns], refs[12 + ns:12 + 2 * ns], *refs[12 + 2 * ns:])

        @pl.when(pl.program_id(0) == 0)
        def _():
            dg_ref[...] = jnp.zeros_like(dg_ref)
            scatter_start()

        dh = jnp.zeros((tm, D_MODEL), F32)
        off = 0
        for p_ref, w in zip(p_refs, _DZ_WIDTHS):
            dh = dh + lax.dot_general(p_ref[...].astype(BF16), w_ref[:, off:off + w], NT, preferred_element_type=F32)
            off += w
        dx, dg = _rms_bwd(dh, x_ref[...], g_ref[...])
        dg_ref[...] += dg
        dx_ref[...] = dx1_ref[...] + dx

        @pl.when(pl.program_id(0) == pl.num_programs(0) - 1)
        def _():
            scatter_finish()

    row = lambda w: pl.BlockSpec((tm, w), lambda i: (i, 0))
    out = pl.pallas_call(
        body, name="in_bwd", grid=(S // tm,),
        in_specs=[row(w) for w in _DZ_WIDTHS] + [_full((D_MODEL, D_IN_PAD)), row(D_MODEL), _full((1, D_MODEL)), row(D_MODEL)]
        + [_ANY] * ns,
        out_specs=[row(D_MODEL), _full((1, D_MODEL))] + [_ANY] * ns,
        out_shape=[jax.ShapeDtypeStruct((S, D_MODEL), F32), jax.ShapeDtypeStruct((1, D_MODEL), F32)] + _scatter_shapes(sums16),
        scratch_shapes=_scatter_sems(ns),
        compiler_params=_params(("arbitrary",)),
    )(*pieces, w_in, x, g1, dx1, *sums16)
    return out[0], out[1], out[2:]


def _dw_in(h1, pieces, tk):
    S = h1.shape[0]

    def body(*refs):
        h_ref, p_refs, o_ref = refs[0], refs[1:7], refs[7]

        @pl.when(pl.program_id(0) == 0)
        def _():
            o_ref[...] = jnp.zeros_like(o_ref)

        off = 0
        for p_ref, w in zip(p_refs, _DZ_WIDTHS):
            o_ref[:, off:off + w] += lax.dot_general(h_ref[...], p_ref[...].astype(BF16), TN, preferred_element_type=F32)
            off += w

    row = lambda w: pl.BlockSpec((tk, w), lambda k: (k, 0))
    return pl.pallas_call(
        body, name="dw_in", grid=(S // tk,),
        in_specs=[row(D_MODEL)] + [row(w) for w in _DZ_WIDTHS],
        out_specs=_full((D_MODEL, D_IN_PAD)),
        out_shape=jax.ShapeDtypeStruct((D_MODEL, D_IN_PAD), F32),
        compiler_params=_params(("arbitrary",)),
    )(h1, *pieces)


def _adamw_math(w, g, m, v):
    m = ADAM_B1 * m + (1.0 - ADAM_B1) * g
    v = ADAM_B2 * v + (1.0 - ADAM_B2) * (g * g)
    m_hat = m / (1.0 - ADAM_B1 ** ADAM_STEP)
    v_hat = v / (1.0 - ADAM_B2 ** ADAM_STEP)
    delta = -ADAM_LR * (m_hat / (jnp.sqrt(v_hat) + ADAM_EPS) + ADAM_WD * w)
    return delta, m, v


def _adamw(name, w, g, m, v):
    R, C = w.shape
    tr = _row_tile(R, 256)

    def body(w_ref, g_ref, m_ref, v_ref, go_ref, d_ref, nm_ref, nv_ref):
        g = g_ref[...]
        d, nm, nv = _adamw_math(w_ref[...], g, m_ref[...], v_ref[...])
        go_ref[...] = g
        d_ref[...] = d
        nm_ref[...] = nm
        nv_ref[...] = nv

    spec = pl.BlockSpec((tr, C), lambda i: (i, 0))
    return pl.pallas_call(
        body, name=name, grid=(R // tr,), in_specs=[spec] * 4, out_specs=[spec] * 4,
        out_shape=[jax.ShapeDtypeStruct((R, C), F32)] * 4,
        compiler_params=_params(("parallel",)),
    )(w, g, m, v)


def _pair_sum(name, grad, theirs, ids):
    q, half, C = theirs.shape
    tr = _row_tile(half, 256)
    nb = half // tr

    def body(ids_ref, a_ref, b_ref, s_ref, sb_ref):
        s = a_ref[...] + b_ref[...]
        s_ref[...] = s
        sb_ref[...] = s.astype(BF16)

    here = pl.BlockSpec((None, tr, C), lambda j, i, ids: (j, i, 0))
    return pl.pallas_call(
        body, name=name,
        grid_spec=pltpu.PrefetchScalarGridSpec(
            num_scalar_prefetch=1, grid=(q, nb),
            in_specs=[pl.BlockSpec((None, tr, C), lambda j, i, ids: (j, ids[1] * nb + i, 0)), here],
            out_specs=[here, here]),
        out_shape=[jax.ShapeDtypeStruct((q, half, C), F32), jax.ShapeDtypeStruct((q, half, C), BF16)],
        compiler_params=_params(("parallel", "parallel")),
    )(ids, grad, theirs)


def _chip_sum(name, sums32, others, ids):
    _, half, C = sums32.shape
    tr = _row_tile(half, 256)
    nb = half // tr

    def body(ids_ref, a_ref, o_ref, s_ref):
        s = a_ref[...]
        for j in range(3):
            s = s + o_ref[j].astype(F32)
        s_ref[...] = s

    return pl.pallas_call(
        body, name=name,
        grid_spec=pltpu.PrefetchScalarGridSpec(
            num_scalar_prefetch=1, grid=(nb,),
            in_specs=[pl.BlockSpec((None, tr, C), lambda i, ids: (ids[0], i, 0)),
                      pl.BlockSpec((3, tr, C), lambda i, ids: (0, i, 0))],
            out_specs=pl.BlockSpec((tr, C), lambda i, ids: (ids[1] * nb + i, 0))),
        out_shape=jax.ShapeDtypeStruct((2 * half, C), F32),
        compiler_params=_params(("parallel",)),
    )(ids, sums32, others)


def _place():
    return lax.axis_index("x"), lax.axis_index("y"), lax.axis_index("c")


def _other_chips(x, y):
    return [(1 - x, y), (x, 1 - y), (1 - x, 1 - y)]


_ANY = pl.BlockSpec(memory_space=pl.ANY)


def _gather_quarters(shards):
    n = len(shards)

    def body(*refs):
        start, hand_on, finish = _gather_ops(refs[:n], refs[n:2 * n], *refs[2 * n:])
        start()
        hand_on()
        finish()

    return pl.pallas_call(
        body, name="gather_weights",
        in_specs=[_ANY] * n, out_specs=[_ANY] * n,
        out_shape=_gather_shapes(shards), scratch_shapes=_gather_sems(n),
    )(*shards)


def _gather_shapes(shards):
    return [jax.ShapeDtypeStruct((4,) + s.shape, s.dtype) for s in shards]


def _gather_sems(n):
    return [pltpu.SemaphoreType.DMA((n, 3))] * 4 + [pltpu.SemaphoreType.DMA((n,))]


def _gather_ops(ins, outs, send_sems, recv_sems, pass_send_sems, pass_recv_sems, own_sems):
    n = len(ins)
    halved = [r.shape[0] % 32 == 0 for r in ins]

    def part(a, quarter, core):
        if not halved[a]:
            return outs[a].at[quarter]
        half = ins[a].shape[0] // 2
        return outs[a].at[quarter, pl.ds(core * half, half), :]

    def ici(a, j, quarter):
        x, y, c = _place()
        px, py = _other_chips(x, y)[j]
        src = ins[a]
        if halved[a]:
            half = src.shape[0] // 2
            src = src.at[pl.ds(c * half, half), :]
        return pltpu.make_async_remote_copy(src_ref=src, dst_ref=part(a, quarter, c), send_sem=send_sems.at[a, j],
                                            recv_sem=recv_sems.at[a, j], device_id=(px, py, c), device_id_type=MESH)

    def passed(a, j, core):
        x, y, c = _place()
        px, py = _other_chips(x, y)[j]
        half = part(a, 2 * px + py, core)
        return pltpu.make_async_remote_copy(src_ref=half, dst_ref=half, send_sem=pass_send_sems.at[a, j],
                                            recv_sem=pass_recv_sems.at[a, j], device_id=(x, y, 1 - c), device_id_type=MESH)

    def own(a):
        x, y, _ = _place()
        return pltpu.make_async_copy(ins[a], outs[a].at[2 * x + y], own_sems.at[a])

    def start():
        x, y, _ = _place()
        for a in range(n):
            for j in range(3):
                ici(a, j, 2 * x + y).start()
            own(a).start()

    def hand_on():
        x, y, c = _place()
        for a in range(n):
            for j, (px, py) in enumerate(_other_chips(x, y)):
                ici(a, j, 2 * px + py).wait_recv()
                if halved[a]:
                    passed(a, j, c).start()

    def finish():
        x, y, c = _place()
        for a in range(n):
            for j in range(3):
                if halved[a]:
                    passed(a, j, 1 - c).wait_recv()
                    passed(a, j, c).wait_send()
                ici(a, j, 2 * x + y).wait_send()
            own(a).wait()

    return start, hand_on, finish


def _swap_halves(grads, name):
    n = len(grads)

    def body(*refs):
        start, finish = _swap_ops(refs[:n], refs[n:2 * n], *refs[2 * n:])
        start()
        finish()

    return pl.pallas_call(
        body, name=name,
        in_specs=[_ANY] * n, out_specs=[_ANY] * n, out_shape=_swap_shapes(grads), scratch_shapes=_swap_sems(n),
    )(*grads)


def _swap_shapes(grads):
    return [jax.ShapeDtypeStruct((4, g.shape[1] // 2, g.shape[2]), F32) for g in grads]


def _swap_sems(n):
    return [pltpu.SemaphoreType.DMA((n,))] * 2


def _swap_ops(ins, outs, send_sems, recv_sems):
    def copy(a):
        x, y, c = _place()
        half = ins[a].shape[1] // 2
        return pltpu.make_async_remote_copy(src_ref=ins[a].at[:, pl.ds((1 - c) * half, half), :], dst_ref=outs[a],
                                            send_sem=send_sems.at[a], recv_sem=recv_sems.at[a],
                                            device_id=(x, y, 1 - c), device_id_type=MESH)

    def start():
        for a in range(len(ins)):
            copy(a).start()

    def finish():
        for a in range(len(ins)):
            copy(a).wait()

    return start, finish


def _scatter_shapes(sums16):
    return [jax.ShapeDtypeStruct((3,) + s.shape[1:], BF16) for s in sums16]


def _scatter_sems(n):
    return [pltpu.SemaphoreType.DMA((n, 3))] * 2


def _scatter_ops(ins, outs, send_sems, recv_sems):
    n = len(ins)

    def copy(a, j):
        x, y, c = _place()
        px, py = _other_chips(x, y)[j]
        return pltpu.make_async_remote_copy(src_ref=ins[a].at[2 * px + py], dst_ref=outs[a].at[j], send_sem=send_sems.at[a, j],
                                            recv_sem=recv_sems.at[a, j], device_id=(px, py, c), device_id_type=MESH)

    def start():
        for a in range(n):
            for j in range(3):
                copy(a, j).start()

    def finish():
        for a in range(n):
            for j in range(3):
                copy(a, j).wait()

    return start, finish


def _join_halves(fulls):
    n = len(fulls)

    def body(*refs):
        ins, outs = refs[:n], refs[n:2 * n]
        send_sems, recv_sems = refs[2 * n:]
        x, y, c = _place()
        started = []
        for a in range(n):
            half = ins[a].shape[0] // 2
            rows = pl.ds(c * half, half)
            cp = pltpu.make_async_remote_copy(src_ref=ins[a].at[rows, :], dst_ref=outs[a].at[rows, :], send_sem=send_sems.at[a],
                                              recv_sem=recv_sems.at[a], device_id=(x, y, 1 - c), device_id_type=MESH)
            cp.start()
            started.append(cp)
        for cp in started:
            cp.wait()

    return pl.pallas_call(
        body, name="join_halves",
        in_specs=[_ANY] * n, out_specs=[_ANY] * n,
        out_shape=[jax.ShapeDtypeStruct(f.shape, F32) for f in fulls],
        input_output_aliases={a: a for a in range(n)},
        scratch_shapes=[pltpu.SemaphoreType.DMA((n,)), pltpu.SemaphoreType.DMA((n,))],
    )(*fulls)


def _small_allreduce(g):
    R = g.shape[0]
    half = R // 2

    def body(g_ref, out_ref, other_s, chip_s, parts_s, send_sems, recv_sems):
        x, y, c = _place()
        mine = 2 * x + y
        rows = pl.ds(pl.multiple_of(c * half, SUBLANES), half)

        def to_other_core(src, dst, k):
            return pltpu.make_async_remote_copy(src_ref=src, dst_ref=dst, send_sem=send_sems.at[k], recv_sem=recv_sems.at[k],
                                                device_id=(x, y, 1 - c), device_id_type=MESH)

        swap = to_other_core(g_ref, other_s, 0)
        swap.start()
        swap.wait()
        chip_s[...] = g_ref[...] + other_s[...]
        parts_s[mine] = chip_s[rows, :]
        sends = []
        for j, (px, py) in enumerate(_other_chips(x, y)):
            cp = pltpu.make_async_remote_copy(src_ref=chip_s.at[rows, :], dst_ref=parts_s.at[mine], send_sem=send_sems.at[1 + j],
                                              recv_sem=recv_sems.at[1 + j], device_id=(px, py, c), device_id_type=MESH)
            cp.start()
            sends.append(cp)
        for cp in sends:
            cp.wait()
        out_ref[rows, :] = (parts_s[0] + parts_s[1]) + (parts_s[2] + parts_s[3])
        join = to_other_core(out_ref.at[rows, :], out_ref.at[rows, :], 4)
        join.start()
        join.wait()

    vm = pl.BlockSpec(memory_space=pltpu.VMEM)
    return pl.pallas_call(
        body, name="small_allreduce",
        in_specs=[vm], out_specs=vm, out_shape=jax.ShapeDtypeStruct((R, LANES), F32),
        scratch_shapes=[pltpu.VMEM((R, LANES), F32), pltpu.VMEM((R, LANES), F32), pltpu.VMEM((4, half, LANES), F32),
                        pltpu.SemaphoreType.DMA((5,)), pltpu.SemaphoreType.DMA((5,))],
        compiler_params=pltpu.CompilerParams(vmem_limit_bytes=VMEM_LIMIT),
    )(g)


def _adamw_small(ws, gs, ms, vs):
    n = len(ws)

    def body(*refs):
        for k in range(n):
            w_ref, g_ref, m_ref, v_ref = (refs[j * n + k] for j in range(4))
            d, nm, nv = _adamw_math(w_ref[...], g_ref[...], m_ref[...], v_ref[...])
            refs[4 * n + k][...] = d
            refs[5 * n + k][...] = nm
            refs[6 * n + k][...] = nv

    vm = pl.BlockSpec(memory_space=pltpu.VMEM)
    out = pl.pallas_call(
        body, name="adamw_small",
        in_specs=[vm] * (4 * n), out_specs=[vm] * (3 * n),
        out_shape=[jax.ShapeDtypeStruct(w.shape, F32) for w in ws] * 3,
        compiler_params=pltpu.CompilerParams(vmem_limit_bytes=VMEM_LIMIT),
    )(*ws, *gs, *ms, *vs)
    return out[:n], out[n:2 * n], out[2 * n:]


_SMALL = (("norm_mix_g", D_MODEL), ("f_bias", N_HEADS), ("sg_ln_g", D_HEADS), ("sg_w", N_HEADS * SG_BLOCK * SG_BLOCK),
          ("sg_b", N_HEADS * SG_BLOCK), ("norm_ffn_g", D_MODEL), ("w_conv", 3 * 2 * D_FF), ("b_conv", 2 * D_FF),
          ("norm_final_g", D_MODEL))
_PACKED = _SMALL + (("sq_err", D_MODEL),)


def _pack_small(parts):
    rows = []
    for name, size in _PACKED:
        flat = parts[name].reshape(-1).astype(F32)
        pad = (-size) % (SUBLANES * LANES)
        rows.append(jnp.pad(flat, (0, pad)).reshape(-1, LANES))
    packed = jnp.concatenate(rows, axis=0)
    return jnp.pad(packed, ((0, (-packed.shape[0]) % (2 * SUBLANES)), (0, 0)))


def _unpack_small(packed, shapes):
    out, r = {}, 0
    for name, size in _PACKED:
        nrows = (size + SUBLANES * LANES - 1) // (SUBLANES * LANES) * SUBLANES
        out[name] = packed[r:r + nrows].reshape(-1)[:size].reshape(shapes[name])
        r += nrows
    return out


def _local_step(x, target, g1, w_in, f_bias, sg_ln_g, sg_w, sg_b, g2, b_conv, g3, late_shards, ids):
    S = x.shape[0]
    tm = _row_tile(S, 512)
    tms = _row_tile(S, 256)
    tq = tm

    lane = jnp.arange(D_HEADS)
    seg_avg = jnp.where(lane[:, None] // HEAD_DIM == lane[None, :] // HEAD_DIM, 1.0 / HEAD_DIM, 0.0).astype(BF16)
    head_ind = (lane[:, None] // HEAD_DIM == jnp.arange(LANES)[None, :]).astype(BF16)
    pos_chunk = jnp.arange(SG_BLOCK) // CHUNK
    w_mask32 = jnp.where(pos_chunk[:, None] >= pos_chunk[None, :], sg_w, 0.0)
    w_mask = w_mask32.astype(BF16)
    w_mask_t = jnp.swapaxes(w_mask32, 1, 2).astype(BF16)
    ln_row = sg_ln_g.reshape(1, D_HEADS)
    b_full = jnp.repeat(sg_b.T, HEAD_DIM, axis=1)
    bias_row = jnp.pad(f_bias.reshape(1, N_HEADS), ((0, 0), (0, LANES - N_HEADS)))
    b_conv_row = b_conv.reshape(1, 2 * D_FF)

    z, f, h1 = _in_proj(x, g1, w_in, tm)
    c = _fox_prep(f, bias_row, _row_tile(S, 256))
    consts = _attn_consts()
    qa, ka, va, vat = _attn_pack(z, c, consts, tm)
    out_b, lse, gathered = _attn_fwd(qa, ka, vat, consts["place_t"], tq, late_shards)
    g_out, w_up_q, g_down, g_conv = gathered
    w_out = g_out.reshape(D_MODEL, D_MODEL)
    w_down = g_down.reshape(D_FF, D_MODEL)
    w_conv = jnp.concatenate([g_conv[q] for q in range(4)], axis=1)
    out_a = _gate_fwd(z, w_mask, ln_row, b_full, seg_avg, tm)
    x1, h2 = _mix_out(x, out_a, out_b, w_out, g2, tm)
    a = _up_proj(h2, w_up_q, tm)
    dx2, sq_err, dg3 = _ffn_fwd_loss(a, w_conv, b_conv_row, w_down, x1, g3, target, tms)

    dconv, y, dw_conv8, db_conv = _ffn_bwd_gate(dx2, a, w_conv, b_conv_row, w_down, tms)
    dact = _conv_bwd(dconv, w_conv, tms, D_FF)
    dw_down = _matmul_tn(y, dx2, "dw_down", D_FF // 2, D_MODEL, tm, quarters=(2, 1))
    dx1, dg2 = _up_bwd(dact, w_up_q, x1, g2, dx2, tm)
    dw_up_q = _matmul_tn(h2, dact, "dw_up", D_MODEL, 2 * D_FF // 4, tm, quarters=(1, 4))
    dcat = _out_bwd(dx1, w_out, tm)
    dw_out_a = _matmul_tn(out_a, dx1, "dw_out_a", D_HEADS, D_MODEL, tm)
    dw_out_b = _matmul_tn(out_b, dx1, "dw_out_b", D_HEADS, D_MODEL, tm)
    early = {"w_down": dw_down.reshape(4, D_FF // 4, D_MODEL), "w_up": dw_up_q,
             "w_out": jnp.concatenate([dw_out_a, dw_out_b], axis=0).reshape(4, D_MODEL // 4, D_MODEL)}
    (dzu, dzv, dsg_w, dsg_b_t, dln), theirs = _gate_bwd(z, dcat, w_mask, w_mask_t, ln_row, b_full, seg_avg, head_ind, tm,
                                                        list(early.values()))
    early_sums = _chip_sums(early, theirs, ids)
    dop, qb, dopt, qbt = _attn_pack_grad(out_b, dcat, qa, lse, head_ind, consts, tm)
    dqp, dk, dv, dcc, landed = _attn_bwd(qb, qbt, ka, va, dop, dopt, consts, tq, [s16 for _, s16 in early_sums.values()])
    early_parts = {k: (s32, got) for (k, (s32, _)), got in zip(early_sums.items(), landed)}
    dq, dc = _attn_unpack(dqp, dcc, consts, tm)
    df, dbias = _fox_bwd(dc, f, bias_row, _row_tile(S, 256))
    pieces = (dzu, dzv, dq, dk, dv, df)
    dw_in = _dw_in(h1, pieces, tm)[:, :D_IN].reshape(D_MODEL, 4, D_IN // 4).transpose(1, 0, 2)
    (w_in_sum, w_in_sum16), = _chip_sums({"w_in": dw_in}, _swap_halves([dw_in], "swap_halves"), ids).values()
    dx, dg1, (w_in_landed,) = _in_bwd(pieces, w_in, x, g1, dx1, tm, [w_in_sum16])

    grads = {
        "norm_mix_g": dg1, "f_bias": dbias[:, :N_HEADS], "sg_ln_g": dln, "sg_w": dsg_w, "sg_b": dsg_b_t[:, :N_HEADS].T,
        "norm_ffn_g": dg2, "w_conv": dw_conv8[:3], "b_conv": db_conv, "norm_final_g": dg3,
    }
    return sq_err, dx, grads, {**early_parts, "w_in": (w_in_sum, w_in_landed)}


def _chip_sums(grads_q, theirs, ids):
    return {k: _pair_sum("pair_sum_" + k, g, t, ids) for (k, g), t in zip(grads_q.items(), theirs)}


def _finish_reduction(parts, ids):
    names = list(parts)
    fulls = [_chip_sum("chip_sum_" + k, s32, got, ids) for k, (s32, got) in parts.items()]
    return dict(zip(names, _join_halves(fulls)))


def kernel(x, norm_mix_g, w_in, f_bias, sg_ln_g, sg_w, sg_b, w_out, norm_ffn_g, w_up, w_conv, b_conv, w_down, norm_final_g, loss_target, m_norm_mix_g, m_w_in, m_f_bias, m_sg_ln_g, m_sg_w, m_sg_b, m_w_out, m_norm_ffn_g, m_w_up, m_w_conv, m_b_conv, m_w_down, m_norm_final_g, v_norm_mix_g, v_w_in, v_f_bias, v_sg_ln_g, v_sg_w, v_sg_b, v_w_out, v_norm_ffn_g, v_w_up, v_w_conv, v_b_conv, v_w_down, v_norm_final_g):
    args = dict(locals())
    quarter = 2 * lax.axis_index("x") + lax.axis_index("y")
    ids = jnp.stack([quarter, lax.axis_index("c")]).astype(jnp.int32)
    wq_conv = w_conv.shape[-1]

    g_in = _gather_quarters([w_in[0].astype(BF16)])[0]
    w_in_full = jnp.pad(jnp.concatenate([g_in[q] for q in range(4)], axis=1), ((0, 0), (0, D_IN_PAD - D_IN)))
    late_shards = [w_out[0].astype(BF16), w_up[0].astype(BF16), w_down[0].astype(BF16), w_conv[0]]

    sq_err, dx, grads, parts = _local_step(
        x[0], loss_target[0], norm_mix_g, w_in_full, f_bias[0], sg_ln_g[0], sg_w[0], sg_b[0], norm_ffn_g, b_conv[0],
        norm_final_g.reshape(1, D_MODEL), late_shards, ids)
    big = _finish_reduction(parts, ids)

    out = {"grad_x": dx[None]}
    for k in ("w_in", "w_out", "w_up", "w_down"):
        g, d, nm, nv = _adamw("adamw_" + k, args[k][0], big[k], args["m_" + k][0], args["v_" + k][0])
        out["grad_" + k], out["delta_" + k], out["new_m_" + k], out["new_v_" + k] = g[None], d[None], nm[None], nv[None]

    small_names = [n for n, _ in _SMALL]
    shapes = {n: (3, 4 * wq_conv) if n == "w_conv" else args[n].shape for n in small_names}
    shapes["sq_err"] = sq_err.shape
    g_small = _unpack_small(_small_allreduce(_pack_small({**{n: grads[n] for n in small_names}, "sq_err": sq_err})), shapes)
    out["loss"] = 0.5 * jnp.sum(g_small.pop("sq_err")) / D_MODEL
    g_small["w_conv"] = lax.dynamic_slice(g_small["w_conv"], (0, quarter * wq_conv), (3, wq_conv))[None]
    flat2d = lambda t: t.reshape(-1, t.shape[-1])
    updated = _adamw_small(*[[flat2d(src[p + n]) for n in small_names] for src, p in
                             ((args, ""), (g_small, ""), (args, "m_"), (args, "v_"))])
    for n, g in g_small.items():
        out["grad_" + n] = g
    for prefix, arrs in zip(("delta_", "new_m_", "new_v_"), updated):
        for n, t in zip(small_names, arrs):
            out[prefix + n] = t.reshape(args[n].shape)

    weights = ["norm_mix_g", "w_in", "f_bias", "sg_ln_g", "sg_w", "sg_b", "w_out", "norm_ffn_g", "w_up", "w_conv", "b_conv",
               "w_down", "norm_final_g"]
    return (out["loss"], out["grad_x"], *[out[p + n] for p in ("grad_", "delta_", "new_m_", "new_v_") for n in weights])
```

```python
import functools
import math

import jax
import jax.numpy as jnp
from jax import lax
from jax.experimental import pallas as pl
from jax.experimental.pallas import tpu as pltpu

F32 = jnp.float32
BF16 = jnp.bfloat16
MESH = pl.DeviceIdType.MESH

D_MODEL = 1024
N_HEADS = 8
HEAD_DIM = 64
D_HEADS = N_HEADS * HEAD_DIM
SG_BLOCK = 128
CHUNK = 64
D_FF = 2816
D_IN = 2 * D_HEADS + 3 * D_HEADS + N_HEADS
LANES = 128
SUBLANES = 8
D_IN_PAD = 5 * D_HEADS + LANES
EPS = 1e-6
SCALE = HEAD_DIM ** -0.5
NEG = -1e30
LOG2E = 1.4426950408889634
HEAD_PAD = LANES
D_PAD = N_HEADS * HEAD_PAD
Q_STAT = HEAD_DIM
K_STAT = HEAD_DIM + 3
L_STAT = HEAD_DIM + 6
GROUPS = 2
GROUP_HEADS = N_HEADS // GROUPS
GROUP_PAD = GROUP_HEADS * HEAD_PAD
KEY_CHUNK = 256
STAT_ROWS = 16
FF_CHUNK = 256

ADAM_LR = 0.001
ADAM_B1 = 0.9
ADAM_B2 = 0.999
ADAM_EPS = 1e-08
ADAM_WD = 0.01
ADAM_STEP = 10

VMEM_LIMIT = 56 * 1024 * 1024

NT = (((1,), (1,)), ((), ()))
TN = (((0,), (0,)), ((), ()))


def _params(sem):
    return pltpu.CompilerParams(dimension_semantics=sem, vmem_limit_bytes=VMEM_LIMIT)


def _full(shape):
    nd = len(shape)
    return pl.BlockSpec(shape, lambda *_: (0,) * nd)


def _row_tile(rows, target):
    best = None
    for t in range(SUBLANES, min(rows, target) + 1, SUBLANES):
        if rows % t == 0:
            best = t
    assert best is not None, rows
    return best


def _sigmoid(x):
    return 0.5 * jnp.tanh(0.5 * x) + 0.5


def _gelu(z):
    return 0.5 * z * (1.0 + lax.erf(z * (2.0 ** -0.5)))


def _gelu_grad(z):
    cdf = 0.5 * (1.0 + lax.erf(z * (2.0 ** -0.5)))
    pdf = jnp.exp(-0.5 * z * z) * (1.0 / math.sqrt(2.0 * math.pi))
    return cdf + z * pdf


def _split_dot(x, m):
    hi = x.astype(BF16)
    lo = (x - hi.astype(F32)).astype(BF16)
    return jnp.dot(hi, m, preferred_element_type=F32) + jnp.dot(lo, m, preferred_element_type=F32)


def _head_mask(h, rows):
    lane = lax.broadcasted_iota(jnp.int32, (rows, D_HEADS), 1)
    return (lane >= h * HEAD_DIM) & (lane < (h + 1) * HEAD_DIM)


def _rms_bwd(dh, x, g):
    r = lax.rsqrt(jnp.mean(x * x, axis=-1, keepdims=True) + EPS)
    xhat = x * r
    dg = jnp.sum(dh * xhat, axis=0, keepdims=True)
    dxhat = dh * g
    dx = r * (dxhat - xhat * jnp.mean(dxhat * xhat, axis=-1, keepdims=True))
    return dx, dg


def _in_proj(x, g1, w_in, tm):
    S = x.shape[0]
    nz = D_IN_PAD - LANES

    def body(x_ref, g_ref, w_ref, z_ref, f_ref, h_ref):
        xf = x_ref[...]
        r = lax.rsqrt(jnp.mean(xf * xf, axis=-1, keepdims=True) + EPS)
        h = (xf * r * g_ref[...]).astype(BF16)
        h_ref[...] = h
        zz = jnp.dot(h, w_ref[...], preferred_element_type=F32)
        z_ref[...] = zz[:, :nz].astype(BF16)
        f_ref[...] = zz[:, nz:]

    return pl.pallas_call(
        body, name="in_proj", grid=(S // tm,),
        in_specs=[pl.BlockSpec((tm, D_MODEL), lambda i: (i, 0)), _full((1, D_MODEL)), _full((D_MODEL, D_IN_PAD))],
        out_specs=[pl.BlockSpec((tm, nz), lambda i: (i, 0)), pl.BlockSpec((tm, LANES), lambda i: (i, 0)),
                   pl.BlockSpec((tm, D_MODEL), lambda i: (i, 0))],
        out_shape=[jax.ShapeDtypeStruct((S, nz), BF16), jax.ShapeDtypeStruct((S, LANES), F32),
                   jax.ShapeDtypeStruct((S, D_MODEL), BF16)],
        compiler_params=_params(("parallel",)),
    )(x, g1, w_in)


def _fox_prep(f, bias_row, tb):
    S = f.shape[0]

    def body(f_ref, b_ref, c_ref, carry):
        @pl.when(pl.program_id(0) == 0)
        def _():
            carry[...] = jnp.zeros_like(carry)

        xv = f_ref[...] + b_ref[...]
        lf = jnp.minimum(xv, 0.0) - jnp.log(1.0 + jnp.exp(-jnp.abs(xv)))
        r = lax.broadcasted_iota(jnp.int32, (tb, tb), 0)
        s = lax.broadcasted_iota(jnp.int32, (tb, tb), 1)
        tri = (r >= s).astype(F32)
        cs = jnp.dot(tri, lf, precision=lax.Precision.HIGHEST, preferred_element_type=F32) + carry[0:1, :]
        c_ref[...] = cs
        carry[...] = jnp.broadcast_to(cs[tb - 1:tb, :], carry.shape)

    return pl.pallas_call(
        body, name="fox_prep", grid=(S // tb,),
        in_specs=[pl.BlockSpec((tb, LANES), lambda i: (i, 0)), _full((1, LANES))],
        out_specs=pl.BlockSpec((tb, LANES), lambda i: (i, 0)),
        out_shape=jax.ShapeDtypeStruct((S, LANES), F32),
        scratch_shapes=[pltpu.VMEM((SUBLANES, LANES), F32)],
        compiler_params=_params(("arbitrary",)),
    )(f, bias_row)


def _attn_consts():
    col = jnp.arange(D_PAD)
    row = jnp.arange(D_HEADS)
    head = jnp.arange(LANES)
    place = (row[:, None] // HEAD_DIM == col[None, :] // HEAD_PAD) & (row[:, None] % HEAD_DIM == col[None, :] % HEAD_PAD)

    def stat(offset):
        return ((head[:, None] < N_HEADS) & (col[None, :] == head[:, None] * HEAD_PAD + offset)).astype(BF16)

    def ones(offsets):
        return sum((col % HEAD_PAD == o) for o in offsets).astype(F32).reshape(1, D_PAD)

    place = place.astype(BF16)
    return {
        "place": place, "place_t": place.T, "place_t_group": place.T[:GROUP_PAD, :GROUP_HEADS * HEAD_DIM],
        "q_stat": jnp.stack([stat(Q_STAT + j) for j in range(3)]), "k_stat": jnp.stack([stat(K_STAT + j) for j in range(3)]),
        "d_stat": jnp.stack([stat(Q_STAT + j) for j in range(2)]), "l_stat": jnp.stack([stat(L_STAT + j)[:STAT_ROWS] for j in range(3)]),
        "q_ones": ones(range(K_STAT, K_STAT + 3)), "k_ones": ones(list(range(Q_STAT, Q_STAT + 3)) + list(range(L_STAT, L_STAT + 3))),
        "v_ones": ones(range(Q_STAT, Q_STAT + 2)),
    }


def _split3(x):
    hi = x.astype(BF16)
    r = x - hi.astype(F32)
    mid = r.astype(BF16)
    return hi, mid, (r - mid.astype(F32)).astype(BF16)


def _attn_pack(z, c, k, tm):
    S = z.shape[0]

    def body(q_ref, k_ref, v_ref, c_ref, pl_ref, pt_ref, qs_ref, ks_ref, qo_ref, ko_ref, vo_ref, voc_ref,
             qa_ref, ka_ref, va_ref, vt_ref):
        place = pl_ref[...]
        q = (q_ref[...].astype(F32) * (SCALE * LOG2E)).astype(BF16)
        qa = jnp.dot(q, place, preferred_element_type=F32) + qo_ref[...]
        ka = jnp.dot(k_ref[...], place, preferred_element_type=F32) + ko_ref[...]
        for j, part in enumerate(_split3(c_ref[...] * LOG2E)):
            qa = qa + jnp.dot(part, qs_ref[j], preferred_element_type=F32)
            ka = ka - jnp.dot(part, ks_ref[j], preferred_element_type=F32)
        qa_ref[...] = qa.astype(BF16)
        ka_ref[...] = ka.astype(BF16)
        v = v_ref[...]
        va_ref[...] = (jnp.dot(v, place, preferred_element_type=F32) + vo_ref[...]).astype(BF16)
        vt_ref[...] = (lax.dot_general(pt_ref[...], v, NT, preferred_element_type=F32) + voc_ref[...]).astype(BF16)

    blk = lambda col: pl.BlockSpec((tm, D_HEADS), lambda i: (i, col))
    out = pl.BlockSpec((tm, D_PAD), lambda i: (i, 0))
    pad = jax.ShapeDtypeStruct((S, D_PAD), BF16)
    return pl.pallas_call(
        body, name="attn_pack", grid=(S // tm,),
        in_specs=[blk(2), blk(3), blk(4), pl.BlockSpec((tm, LANES), lambda i: (i, 0)), _full((D_HEADS, D_PAD)), _full((D_PAD, D_HEADS)),
                  _full((3, LANES, D_PAD)), _full((3, LANES, D_PAD)), _full((1, D_PAD)), _full((1, D_PAD)), _full((1, D_PAD)),
                  _full((D_PAD, 1))],
        out_specs=[out, out, out, pl.BlockSpec((None, D_PAD, tm), lambda i: (i, 0, 0))],
        out_shape=[pad, pad, pad, jax.ShapeDtypeStruct((S // tm, D_PAD, tm), BF16)],
        compiler_params=_params(("parallel",)),
    )(z, z, z, c, k["place"], k["place_t"], k["q_stat"], k["k_stat"], k["q_ones"], k["k_ones"], k["v_ones"], k["v_ones"].T)


def _attn_fwd(qa, ka, vat, place_t, tq, shards):
    S = qa.shape[0]
    n = S // tq
    ns = len(shards)
    hand_on_at = (2 * n) // 3

    pairs = [(q, k) for q in range(n) for k in range(q + 1)]
    q_of = jnp.asarray([q for q, _ in pairs], jnp.int32)
    k_of = jnp.asarray([k for _, k in pairs], jnp.int32)

    def body(q_of_ref, k_of_ref, q_ref, k_ref, vt_ref, pt_ref, *rest):
        o_ref, lse_ref = rest[ns:ns + 2]
        m_s, acc_s, ot_s = rest[2 * ns + 2:2 * ns + 5]
        s_s = rest[2 * ns + 5:2 * ns + 7]
        start, hand_on, finish = _gather_ops(rest[:ns], rest[ns + 2:2 * ns + 2], *rest[2 * ns + 7:])
        qi, ki = q_of_ref[pl.program_id(0)], k_of_ref[pl.program_id(0)]

        @pl.when((qi == 0) & (ki == 0))
        def _():
            start()

        @pl.when((qi == hand_on_at) & (ki == 0))
        def _():
            hand_on()

        @pl.when(ki == 0)
        def _():
            m_s[...] = jnp.full_like(m_s, NEG)
            acc_s[...] = jnp.zeros_like(acc_s)

        def step(diagonal):
            chunks = [slice(c * KEY_CHUNK, (c + 1) * KEY_CHUNK) for c in range(tq // KEY_CHUNK)]

            def scores(h, rows, slot):
                sl = slice(h * HEAD_PAD, (h + 1) * HEAD_PAD)
                st = lax.dot_general(k_ref[rows, sl], q_ref[:, sl], NT, preferred_element_type=F32)
                if diagonal:
                    key = rows.start + lax.broadcasted_iota(jnp.int32, (KEY_CHUNK, tq), 0)
                    query = lax.broadcasted_iota(jnp.int32, (KEY_CHUNK, tq), 1)
                    st = jnp.where(query >= key, st, NEG)
                s_s[slot][rows, :] = st
                return jnp.max(st, axis=0, keepdims=True)

            m_cur = functools.reduce(jnp.maximum, [scores(0, rows, 0) for rows in chunks])
            for h in range(N_HEADS):
                sl = slice(h * HEAD_PAD, (h + 1) * HEAD_PAD)
                slot = h % 2
                m_prev = m_s[h][0:1, :]
                m_new = jnp.maximum(m_prev, m_cur)
                acc = jnp.exp2(m_prev - m_new) * acc_s[h]
                m_next = []
                for rows in chunks:
                    if h + 1 < N_HEADS:
                        m_next.append(scores(h + 1, rows, 1 - slot))
                    pt = jnp.exp2(s_s[slot][rows, :] - m_new).astype(BF16)
                    acc = acc + jnp.dot(vt_ref[sl, rows], pt, preferred_element_type=F32)
                acc_s[h] = acc
                m_s[h] = jnp.broadcast_to(m_new, (SUBLANES, tq))
                if m_next:
                    m_cur = functools.reduce(jnp.maximum, m_next)

        @pl.when(ki < qi)
        def _():
            step(False)

        @pl.when(ki == qi)
        def _():
            step(True)
            lse_ref[...] = jnp.zeros_like(lse_ref)
            for h in range(N_HEADS):
                acc = acc_s[h]
                denom = acc[Q_STAT:Q_STAT + 1, :]
                ot_s[h * HEAD_PAD:(h + 1) * HEAD_PAD, :] = (acc / denom).astype(BF16)
                lse_ref[h:h + 1, :] = m_s[h][0:1, :] + jnp.log(denom) * LOG2E
            o_ref[...] = lax.dot_general(ot_s[...], pt_ref[...], TN, preferred_element_type=F32).astype(BF16)

        @pl.when((qi == n - 1) & (ki == n - 1))
        def _():
            finish()

    out = pl.pallas_call(
        body, name="attn_fwd",
        grid_spec=pltpu.PrefetchScalarGridSpec(
            num_scalar_prefetch=2, grid=(len(pairs),),
            in_specs=[pl.BlockSpec((tq, D_PAD), lambda i, qs, ks: (qs[i], 0)),
                      pl.BlockSpec((tq, D_PAD), lambda i, qs, ks: (ks[i], 0)),
                      pl.BlockSpec((None, D_PAD, tq), lambda i, qs, ks: (ks[i], 0, 0)),
                      pl.BlockSpec((D_PAD, D_HEADS), lambda i, qs, ks: (0, 0))]
            + [_ANY] * ns,
            out_specs=[pl.BlockSpec((tq, D_HEADS), lambda i, qs, ks: (qs[i], 0)),
                       pl.BlockSpec((STAT_ROWS, tq), lambda i, qs, ks: (0, qs[i]))] + [_ANY] * ns,
            scratch_shapes=[pltpu.VMEM((N_HEADS, SUBLANES, tq), F32), pltpu.VMEM((N_HEADS, HEAD_PAD, tq), F32),
                            pltpu.VMEM((D_PAD, tq), BF16), pltpu.VMEM((tq, tq), F32), pltpu.VMEM((tq, tq), F32)] + _gather_sems(ns)),
        out_shape=[jax.ShapeDtypeStruct((S, D_HEADS), BF16), jax.ShapeDtypeStruct((STAT_ROWS, S), F32)] + _gather_shapes(shards),
        compiler_params=_params(("arbitrary",)),
    )(q_of, k_of, qa, ka, vat, place_t, *shards)
    return out[0], out[1], out[2:]


def _layer_norm_heads(v, seg_avg):
    mu = _split_dot(v, seg_avg)
    d = v - mu
    var = _split_dot(d * d, seg_avg)
    rstd = lax.rsqrt(var + EPS)
    return d * rstd, rstd


def _gate_mix(vn_blk, w_ref, bias):
    acc = bias
    for h in range(N_HEADS):
        vh = jnp.where(_head_mask(h, SG_BLOCK), vn_blk, 0.0).astype(BF16)
        acc = acc + jnp.dot(w_ref[h], vh, preferred_element_type=F32)
    return acc


def _gate_fwd(z, w_mask, ln_row, b_full, seg_avg, tm):
    S = z.shape[0]

    def body(zu_ref, zv_ref, w_ref, ln_ref, b_ref, avg_ref, o_ref):
        u = _gelu(zu_ref[...].astype(F32))
        v = _gelu(zv_ref[...].astype(F32))
        vhat, _ = _layer_norm_heads(v, avg_ref[...])
        vn = vhat * ln_ref[...]
        for b in range(tm // SG_BLOCK):
            rows = slice(b * SG_BLOCK, (b + 1) * SG_BLOCK)
            mixed = _gate_mix(vn[rows], w_ref, b_ref[...])
            o_ref[rows, :] = (u[rows] * mixed).astype(BF16)

    return pl.pallas_call(
        body, name="gate_fwd", grid=(S // tm,),
        in_specs=[pl.BlockSpec((tm, D_HEADS), lambda i: (i, 0)), pl.BlockSpec((tm, D_HEADS), lambda i: (i, 1)),
                  _full((N_HEADS, SG_BLOCK, SG_BLOCK)), _full((1, D_HEADS)), _full((SG_BLOCK, D_HEADS)),
                  _full((D_HEADS, D_HEADS))],
        out_specs=pl.BlockSpec((tm, D_HEADS), lambda i: (i, 0)),
        out_shape=jax.ShapeDtypeStruct((S, D_HEADS), BF16),
        compiler_params=_params(("parallel",)),
    )(z, z, w_mask, ln_row, b_full, seg_avg)


def _mix_out(x, out_a, out_b, w_out, g2, tm):
    S = x.shape[0]

    def body(x_ref, a_ref, b_ref, w_ref, g_ref, x1_ref, h_ref):
        y = jnp.dot(a_ref[...], w_ref[:D_HEADS, :], preferred_element_type=F32)
        y = y + jnp.dot(b_ref[...], w_ref[D_HEADS:, :], preferred_element_type=F32)
        x1 = x_ref[...] + y
        x1_ref[...] = x1
        r = lax.rsqrt(jnp.mean(x1 * x1, axis=-1, keepdims=True) + EPS)
        h_ref[...] = (x1 * r * g_ref[...]).astype(BF16)

    row = lambda w: pl.BlockSpec((tm, w), lambda i: (i, 0))
    return pl.pallas_call(
        body, name="mix_out", grid=(S // tm,),
        in_specs=[row(D_MODEL), row(D_HEADS), row(D_HEADS), _full((D_MODEL, D_MODEL)), _full((1, D_MODEL))],
        out_specs=[row(D_MODEL), row(D_MODEL)],
        out_shape=[jax.ShapeDtypeStruct((S, D_MODEL), F32), jax.ShapeDtypeStruct((S, D_MODEL), BF16)],
        compiler_params=_params(("parallel",)),
    )(x, out_a, out_b, w_out, g2)


def _up_proj(h2, w_up_q, tm):
    S = h2.shape[0]
    nq, _, wq = w_up_q.shape

    def body(h_ref, w_ref, a_ref):
        a_ref[...] = jnp.dot(h_ref[...], w_ref[...], preferred_element_type=F32).astype(BF16)

    return pl.pallas_call(
        body, name="up_proj", grid=(nq, S // tm),
        in_specs=[pl.BlockSpec((tm, D_MODEL), lambda j, i: (i, 0)), pl.BlockSpec((None, D_MODEL, wq), lambda j, i: (j, 0, 0))],
        out_specs=pl.BlockSpec((tm, wq), lambda j, i: (i, j)),
        out_shape=jax.ShapeDtypeStruct((S, nq * wq), BF16),
        compiler_params=_params(("parallel", "parallel")),
    )(h2, w_up_q)


def _shift_down(a, halo, k):
    tm = a.shape[0]
    ra = pltpu.roll(a, k, 0)
    rh = pltpu.roll(halo, k, 0)
    row = lax.broadcasted_iota(jnp.int32, halo.shape, 0)
    top = jnp.where(row < k, rh, ra[0:SUBLANES])
    return jnp.concatenate([top, ra[SUBLANES:tm]], axis=0)


def _shift_up(a, halo, k):
    tm = a.shape[0]
    ra = pltpu.roll(a, tm - k, 0)
    rh = pltpu.roll(halo, SUBLANES - k, 0)
    row = lax.broadcasted_iota(jnp.int32, halo.shape, 0)
    bottom = jnp.where(row >= SUBLANES - k, rh, ra[tm - SUBLANES:tm])
    return jnp.concatenate([ra[0:tm - SUBLANES], bottom], axis=0)


def _shift_matrices(tm):
    row = lax.broadcasted_iota(jnp.int32, (tm, tm), 0)
    col = lax.broadcasted_iota(jnp.int32, (tm, tm), 1)
    return [(row == col + k).astype(BF16) for k in (1, 2)]


def _conv_taps(a, halo, first, shifts):
    tm = a.shape[0]
    halo = halo.astype(F32) * jnp.where(first, 0.0, 1.0)
    if shifts is None:
        a = a.astype(F32)
        return a, _shift_down(a, halo, 1), _shift_down(a, halo, 2)
    row8 = lax.broadcasted_iota(jnp.int32, halo.shape, 0)
    taps = [a.astype(F32)]
    for k, shift in zip((1, 2), shifts):
        down = jnp.dot(shift, a, preferred_element_type=F32)
        top = down[0:SUBLANES] + jnp.where(row8 < k, pltpu.roll(halo, k, 0), 0.0)
        taps.append(jnp.concatenate([top, down[SUBLANES:tm]], axis=0))
    return taps


def _conv_gate_val(refs, shifts, cols, first):
    ag_ref, av_ref, hg_ref, hv_ref, wg_ref, wv_ref, bg_ref, bv_ref = refs
    g0, g1, g2 = _conv_taps(ag_ref[:, cols], hg_ref[:, cols], first, shifts)
    gate = wg_ref[2:3, cols] * g0 + wg_ref[1:2, cols] * g1 + wg_ref[0:1, cols] * g2 + bg_ref[:, cols]
    v0, v1, v2 = _conv_taps(av_ref[:, cols], hv_ref[:, cols], first, shifts)
    val = wv_ref[2:3, cols] * v0 + wv_ref[1:2, cols] * v1 + wv_ref[0:1, cols] * v2 + bv_ref[:, cols]
    return gate, val, (g2, g1, g0), (v2, v1, v0)


_FF_CHUNKS = [slice(j * FF_CHUNK, (j + 1) * FF_CHUNK) for j in range(D_FF // FF_CHUNK)]


def _conv_specs(tm):
    step = tm // SUBLANES
    prev = lambda i: jnp.maximum(i * step - 1, 0)
    return [pl.BlockSpec((tm, D_FF), lambda i: (i, 0)), pl.BlockSpec((tm, D_FF), lambda i: (i, 1)),
            pl.BlockSpec((SUBLANES, D_FF), lambda i: (prev(i), 0)), pl.BlockSpec((SUBLANES, D_FF), lambda i: (prev(i), 1))]


def _ffn_fwd_loss(a, w_conv, b_conv, w_down, x1, g3, target, tm):
    S = x1.shape[0]

    def body(ag_ref, av_ref, hg_ref, hv_ref, wg_ref, wv_ref, bg_ref, bv_ref, wd_ref, x1_ref, g_ref, t_ref,
             dx2_ref, loss_ref, dg_ref):
        i = pl.program_id(0)

        @pl.when(i == 0)
        def _():
            loss_ref[...] = jnp.zeros_like(loss_ref)
            dg_ref[...] = jnp.zeros_like(dg_ref)

        x2 = x1_ref[...]
        for cols in _FF_CHUNKS:
            gate, val, _, _ = _conv_gate_val((ag_ref, av_ref, hg_ref, hv_ref, wg_ref, wv_ref, bg_ref, bv_ref), None, cols, i == 0)
            half = 0.5 * gate
            y = ((half + half * jnp.tanh(half)) * val).astype(BF16)
            x2 = x2 + jnp.dot(y, wd_ref[cols, :], preferred_element_type=F32)
        r = lax.rsqrt(jnp.mean(x2 * x2, axis=-1, keepdims=True) + EPS)
        xhat = x2 * r
        gg = g_ref[...]
        err = xhat * gg - t_ref[...]
        loss_ref[...] += jnp.sum(err * err, axis=0, keepdims=True)
        dy = err * (1.0 / D_MODEL)
        dg_ref[...] += jnp.sum(dy * xhat, axis=0, keepdims=True)
        dxhat = dy * gg
        dx2_ref[...] = r * (dxhat - xhat * jnp.mean(dxhat * xhat, axis=-1, keepdims=True))

    row = lambda w: pl.BlockSpec((tm, w), lambda i: (i, 0))
    half = lambda r: [pl.BlockSpec((r, D_FF), lambda i: (0, 0)), pl.BlockSpec((r, D_FF), lambda i: (0, 1))]
    return pl.pallas_call(
        body, name="ffn_fwd_loss", grid=(S // tm,),
        in_specs=_conv_specs(tm) + half(3) + half(1) + [_full((D_FF, D_MODEL)), row(D_MODEL), _full((1, D_MODEL)), row(D_MODEL)],
        out_specs=[row(D_MODEL), _full((1, D_MODEL)), _full((1, D_MODEL))],
        out_shape=[jax.ShapeDtypeStruct((S, D_MODEL), F32), jax.ShapeDtypeStruct((1, D_MODEL), F32),
                   jax.ShapeDtypeStruct((1, D_MODEL), F32)],
        compiler_params=_params(("arbitrary",)),
    )(a, a, a, a, w_conv, w_conv, b_conv, b_conv, w_down, x1, g3, target)


def _ffn_bwd_gate(dx2, a, w_conv, b_conv, w_down, tm):
    S = dx2.shape[0]

    def body(dx_ref, ag_ref, av_ref, hg_ref, hv_ref, wg_ref, wv_ref, bg_ref, bv_ref, wd_ref,
             dc_ref, y_ref, dw_ref, db_ref):
        i = pl.program_id(0)

        @pl.when(i == 0)
        def _():
            dw_ref[...] = jnp.zeros_like(dw_ref)
            db_ref[...] = jnp.zeros_like(db_ref)

        dx = dx_ref[...].astype(BF16)
        shifts = _shift_matrices(tm)
        for cols in _FF_CHUNKS:
            gate, val, gtaps, vtaps = _conv_gate_val((ag_ref, av_ref, hg_ref, hv_ref, wg_ref, wv_ref, bg_ref, bv_ref), shifts, cols, i == 0)
            sg = _sigmoid(gate)
            act = gate * sg
            y_ref[:, cols] = (act * val).astype(BF16)
            dy = lax.dot_general(dx, wd_ref[cols, :], NT, preferred_element_type=F32)
            dgate = dy * val * (sg + act - act * sg)
            dval = dy * act
            for d, taps, out in ((dgate, gtaps, cols), (dval, vtaps, slice(D_FF + cols.start, D_FF + cols.stop))):
                dc_ref[:, out] = d.astype(BF16)
                db_ref[0:1, out] += jnp.sum(d, axis=0, keepdims=True)
                for j in range(3):
                    dw_ref[j:j + 1, out] += jnp.sum(d * taps[j], axis=0, keepdims=True)

    row = lambda w: pl.BlockSpec((tm, w), lambda i: (i, 0))
    half = lambda r: [pl.BlockSpec((r, D_FF), lambda i: (0, 0)), pl.BlockSpec((r, D_FF), lambda i: (0, 1))]
    return pl.pallas_call(
        body, name="ffn_bwd_gate", grid=(S // tm,),
        in_specs=[row(D_MODEL)] + _conv_specs(tm) + half(3) + half(1) + [_full((D_FF, D_MODEL))],
        out_specs=[row(2 * D_FF), row(D_FF), _full((SUBLANES, 2 * D_FF)), _full((1, 2 * D_FF))],
        out_shape=[jax.ShapeDtypeStruct((S, 2 * D_FF), BF16), jax.ShapeDtypeStruct((S, D_FF), BF16),
                   jax.ShapeDtypeStruct((SUBLANES, 2 * D_FF), F32), jax.ShapeDtypeStruct((1, 2 * D_FF), F32)],
        compiler_params=_params(("arbitrary",)),
    )(dx2, a, a, a, a, w_conv, w_conv, b_conv, b_conv, w_down)


def _conv_bwd(dc, w_conv, tm, tn):
    S, C = dc.shape
    step = tm // SUBLANES
    last_blk = S // SUBLANES - 1

    def body(d_ref, nx_ref, w_ref, o_ref):
        last = pl.program_id(0) == pl.num_programs(0) - 1
        row = lax.broadcasted_iota(jnp.int32, (tm, tm), 0)
        col = lax.broadcasted_iota(jnp.int32, (tm, tm), 1)
        row8 = lax.broadcasted_iota(jnp.int32, (SUBLANES, FF_CHUNK), 0)
        ups = [(row + k == col).astype(BF16) for k in (1, 2)]
        for c0 in range(0, tn, FF_CHUNK):
            cols = slice(c0, c0 + FF_CHUNK)
            d = d_ref[:, cols]
            nx = nx_ref[:, cols].astype(F32) * jnp.where(last, 0.0, 1.0)
            out = w_ref[2:3, cols] * d.astype(F32)
            for k, up in zip((1, 2), ups):
                moved = jnp.dot(up, d, preferred_element_type=F32)
                bottom = moved[tm - SUBLANES:tm] + jnp.where(row8 >= SUBLANES - k, pltpu.roll(nx, SUBLANES - k, 0), 0.0)
                out = out + w_ref[2 - k:3 - k, cols] * jnp.concatenate([moved[0:tm - SUBLANES], bottom], axis=0)
            o_ref[:, cols] = out.astype(BF16)

    return pl.pallas_call(
        body, name="conv_bwd", grid=(S // tm, C // tn),
        in_specs=[pl.BlockSpec((tm, tn), lambda i, j: (i, j)),
                  pl.BlockSpec((SUBLANES, tn), lambda i, j: (jnp.minimum((i + 1) * step, last_blk), j)),
                  pl.BlockSpec((3, tn), lambda i, j: (0, j))],
        out_specs=pl.BlockSpec((tm, tn), lambda i, j: (i, j)),
        out_shape=jax.ShapeDtypeStruct((S, C), BF16),
        compiler_params=_params(("parallel", "parallel")),
    )(dc, dc, w_conv)


def _matmul_tn(a, b, name, bm, bn, tk, col_a=0, col_b=0, quarters=None):
    S = a.shape[0]
    gm, gn = quarters if quarters else (1, 1)
    nk = S // tk

    def body(a_ref, b_ref, o_ref):
        @pl.when(pl.program_id(2) == 0)
        def _():
            o_ref[...] = jnp.zeros_like(o_ref)

        o_ref[...] += lax.dot_general(a_ref[...].astype(BF16), b_ref[...].astype(BF16), TN, preferred_element_type=F32)

    if quarters and gn > 1:
        out_spec = pl.BlockSpec((None, bm, bn), lambda i, j, k: (j, i, 0))
        out_shape = jax.ShapeDtypeStruct((gn, gm * bm, bn), F32)
    else:
        out_spec = pl.BlockSpec((bm, bn), lambda i, j, k: (i, j))
        out_shape = jax.ShapeDtypeStruct((gm * bm, gn * bn), F32)
    return pl.pallas_call(
        body, name=name, grid=(gm, gn, nk),
        in_specs=[pl.BlockSpec((tk, bm), lambda i, j, k: (k, col_a * gm + i)),
                  pl.BlockSpec((tk, bn), lambda i, j, k: (k, col_b * gn + j))],
        out_specs=out_spec, out_shape=out_shape,
        compiler_params=_params(("parallel", "parallel", "arbitrary")),
    )(a, b)


def _up_bwd(dact, w_up_q, x1, g2, dx2, tm):
    S = x1.shape[0]
    nq, _, wq = w_up_q.shape

    def body(d_ref, w_ref, x_ref, g_ref, dx2_ref, dx1_ref, dg_ref):
        @pl.when(pl.program_id(0) == 0)
        def _():
            dg_ref[...] = jnp.zeros_like(dg_ref)

        dh = jnp.zeros((tm, D_MODEL), F32)
        for j in range(nq):
            dh = dh + lax.dot_general(d_ref[:, j * wq:(j + 1) * wq], w_ref[j], NT, preferred_element_type=F32)
        dx, dg = _rms_bwd(dh, x_ref[...], g_ref[...])
        dg_ref[...] += dg
        dx1_ref[...] = dx2_ref[...] + dx

    row = lambda w: pl.BlockSpec((tm, w), lambda i: (i, 0))
    return pl.pallas_call(
        body, name="up_bwd", grid=(S // tm,),
        in_specs=[row(nq * wq), pl.BlockSpec((nq, D_MODEL, wq), lambda i: (0, 0, 0), pipeline_mode=pl.Buffered(1)),
                  row(D_MODEL), _full((1, D_MODEL)), row(D_MODEL)],
        out_specs=[row(D_MODEL), _full((1, D_MODEL))],
        out_shape=[jax.ShapeDtypeStruct((S, D_MODEL), F32), jax.ShapeDtypeStruct((1, D_MODEL), F32)],
        compiler_params=_params(("arbitrary",)),
    )(dact, w_up_q, x1, g2, dx2)


def _out_bwd(dx1, w_out, tm):
    S = dx1.shape[0]

    def body(d_ref, w_ref, o_ref):
        o_ref[...] = lax.dot_general(d_ref[...].astype(BF16), w_ref[...], NT, preferred_element_type=F32).astype(BF16)

    return pl.pallas_call(
        body, name="out_bwd", grid=(S // tm,),
        in_specs=[pl.BlockSpec((tm, D_MODEL), lambda i: (i, 0)), _full((D_MODEL, D_MODEL))],
        out_specs=pl.BlockSpec((tm, D_MODEL), lambda i: (i, 0)),
        out_shape=jax.ShapeDtypeStruct((S, D_MODEL), BF16),
        compiler_params=_params(("parallel",)),
    )(dx1, w_out)


def _gate_bwd(z, dcat, w_mask, w_mask_t, ln_row, b_full, seg_avg, head_ind, tm, swap):
    S = z.shape[0]
    nb = tm // SG_BLOCK
    ns = len(swap)

    def body(zu_ref, zv_ref, do_ref, w_ref, wt_ref, ln_ref, b_ref, avg_ref, ind_ref, *rest):
        dzu_ref, dzv_ref, dw_ref, db_ref, dln_ref = rest[ns:ns + 5]
        dvn_s, dbf_s = rest[2 * ns + 5:2 * ns + 7]
        swap_start, swap_finish = _swap_ops(rest[:ns], rest[ns + 5:2 * ns + 5], *rest[2 * ns + 7:])
        i = pl.program_id(0)

        @pl.when(i == 0)
        def _():
            swap_start()
            dw_ref[...] = jnp.zeros_like(dw_ref)
            dln_ref[...] = jnp.zeros_like(dln_ref)
            dbf_s[...] = jnp.zeros_like(dbf_s)

        zu = zu_ref[...].astype(F32)
        zv = zv_ref[...].astype(F32)
        u = _gelu(zu)
        v = _gelu(zv)
        avg = avg_ref[...]
        vhat, rstd = _layer_norm_heads(v, avg)
        ln = ln_ref[...]
        vn = vhat * ln
        for b in range(nb):
            rows = slice(b * SG_BLOCK, (b + 1) * SG_BLOCK)
            vn_b = vn[rows]
            mixed = _gate_mix(vn_b, w_ref, b_ref[...])
            do = do_ref[rows, :].astype(F32)
            dzu_ref[rows, :] = (do * mixed * _gelu_grad(zu[rows])).astype(BF16)
            dmix = do * u[rows]
            dbf_s[...] += dmix
            vn_bf = vn_b.astype(BF16)
            dvn = jnp.zeros((SG_BLOCK, D_HEADS), F32)
            for h in range(N_HEADS):
                dmh = jnp.where(_head_mask(h, SG_BLOCK), dmix, 0.0).astype(BF16)
                dw_ref[h] += lax.dot_general(dmh, vn_bf, NT, preferred_element_type=F32)
                dvn = dvn + jnp.dot(wt_ref[h], dmh, preferred_element_type=F32)
            dvn_s[rows, :] = dvn
        dvn = dvn_s[...]
        dln_ref[...] += jnp.sum(dvn * vhat, axis=0, keepdims=True)
        dvhat = dvn * ln
        dv = rstd * (dvhat - _split_dot(dvhat, avg) - vhat * _split_dot(dvhat * vhat, avg))
        dzv_ref[...] = (dv * _gelu_grad(zv)).astype(BF16)

        @pl.when(i == pl.num_programs(0) - 1)
        def _():
            r = lax.broadcasted_iota(jnp.int32, (SG_BLOCK, SG_BLOCK), 0) // CHUNK
            s = lax.broadcasted_iota(jnp.int32, (SG_BLOCK, SG_BLOCK), 1) // CHUNK
            for h in range(N_HEADS):
                dw_ref[h] = jnp.where(r >= s, dw_ref[h], 0.0)
            db_ref[...] = _split_dot(dbf_s[...], ind_ref[...])
            swap_finish()

    row = lambda col: pl.BlockSpec((tm, D_HEADS), lambda i: (i, col))
    wspec = _full((N_HEADS, SG_BLOCK, SG_BLOCK))
    out = pl.pallas_call(
        body, name="gate_bwd", grid=(S // tm,),
        in_specs=[row(0), row(1), row(0), wspec, wspec, _full((1, D_HEADS)), _full((SG_BLOCK, D_HEADS)),
                  _full((D_HEADS, D_HEADS)), _full((D_HEADS, LANES))] + [_ANY] * ns,
        out_specs=[row(0), row(0), wspec, _full((SG_BLOCK, LANES)), _full((1, D_HEADS))] + [_ANY] * ns,
        out_shape=[jax.ShapeDtypeStruct((S, D_HEADS), BF16), jax.ShapeDtypeStruct((S, D_HEADS), BF16),
                   jax.ShapeDtypeStruct((N_HEADS, SG_BLOCK, SG_BLOCK), F32), jax.ShapeDtypeStruct((SG_BLOCK, LANES), F32),
                   jax.ShapeDtypeStruct((1, D_HEADS), F32)] + _swap_shapes(swap),
        scratch_shapes=[pltpu.VMEM((tm, D_HEADS), F32), pltpu.VMEM((SG_BLOCK, D_HEADS), F32)] + _swap_sems(ns),
        compiler_params=_params(("arbitrary",)),
    )(z, z, dcat, w_mask, w_mask_t, ln_row, b_full, seg_avg, head_ind, *swap)
    return out[:5], out[5:]


def _attn_pack_grad(o, dcat, qa, lse, head_ind, k, tm):
    S = o.shape[0]

    def body(o_ref, do_ref, qa_ref, lse_ref, ind_ref, pl_ref, pt_ref, eye_ref, ds_ref, dst_ref, ls_ref, lst_ref,
             dop_ref, qb_ref, dot_ref, qbt_ref):
        do = do_ref[...]
        delta = _split_dot(o_ref[...].astype(F32) * do.astype(F32), ind_ref[...])
        hi = delta.astype(BF16)
        lo = (delta - hi.astype(F32)).astype(BF16)
        dop = jnp.dot(do, pl_ref[...], preferred_element_type=F32)
        dop = dop - jnp.dot(hi, ds_ref[0], preferred_element_type=F32) - jnp.dot(lo, ds_ref[1], preferred_element_type=F32)
        for g in range(GROUPS):
            dop_ref[g] = dop[:, g * GROUP_PAD:(g + 1) * GROUP_PAD].astype(BF16)
        dot = lax.dot_general(pt_ref[...], do, NT, preferred_element_type=F32)
        dot = dot - lax.dot_general(dst_ref[0], hi, NT, preferred_element_type=F32)
        dot = dot - lax.dot_general(dst_ref[1], lo, NT, preferred_element_type=F32)
        dot_ref[...] = dot.astype(BF16)
        qa = qa_ref[...]
        qb = qa.astype(F32)
        qbt = lax.dot_general(eye_ref[...], qa, NT, preferred_element_type=F32)
        for j, part in enumerate(_split3(lse_ref[...])):
            qb = qb - lax.dot_general(part, ls_ref[j], TN, preferred_element_type=F32)
            qbt = qbt - jnp.dot(lst_ref[j], part, preferred_element_type=F32)
        for g in range(GROUPS):
            qb_ref[g] = qb[:, g * GROUP_PAD:(g + 1) * GROUP_PAD].astype(BF16)
        qbt_ref[...] = qbt.astype(BF16)

    pad = pl.BlockSpec((tm, D_PAD), lambda i: (i, 0))
    padt = pl.BlockSpec((None, D_PAD, tm), lambda i: (i, 0, 0))
    return pl.pallas_call(
        body, name="attn_pack_grad", grid=(S // tm,),
        in_specs=[pl.BlockSpec((tm, D_HEADS), lambda i: (i, 0)), pl.BlockSpec((tm, D_HEADS), lambda i: (i, 1)), pad,
                  pl.BlockSpec((STAT_ROWS, tm), lambda i: (0, i)), _full((D_HEADS, LANES)), _full((D_HEADS, D_PAD)),
                  _full((D_PAD, D_HEADS)), _full((D_PAD, D_PAD)), _full((2, LANES, D_PAD)), _full((2, D_PAD, LANES)),
                  _full((3, STAT_ROWS, D_PAD)), _full((3, D_PAD, STAT_ROWS))],
        out_specs=[pl.BlockSpec((GROUPS, tm, GROUP_PAD), lambda i: (0, i, 0))] * 2 + [padt, padt],
        out_shape=[jax.ShapeDtypeStruct((GROUPS, S, GROUP_PAD), BF16)] * 2 + [jax.ShapeDtypeStruct((S // tm, D_PAD, tm), BF16)] * 2,
        compiler_params=_params(("parallel",)),
    )(o, dcat, qa, lse, head_ind, k["place"], k["place_t"], jnp.eye(D_PAD, dtype=BF16), k["d_stat"],
      jnp.swapaxes(k["d_stat"], 1, 2), k["l_stat"], jnp.swapaxes(k["l_stat"], 1, 2))


def _attn_bwd(qb, qbt, ka, va, dop, dopt, k, tq, sums16):
    S = ka.shape[0]
    n = S // tq
    ns = len(sums16)

    pairs = [(kb, q) for kb in range(n) for q in range(kb, n)]
    k_of = jnp.asarray([kb for kb, _ in pairs], jnp.int32)
    q_of = jnp.asarray([q for _, q in pairs], jnp.int32)

    def body(k_of_ref, q_of_ref, q_ref, qt_ref, k_ref, v_ref, do_ref, dot_ref, pt_ref, *rest):
        dq_hbm, dcr_hbm, dk_ref, dv_ref, dcc_ref = rest[ns:ns + 5]
        dq_s, dcr_s, dk_s, dv_s, dcc_s = rest[2 * ns + 5:2 * ns + 10]
        s_s, d_s = rest[2 * ns + 10:2 * ns + 12], rest[2 * ns + 12:2 * ns + 14]
        sems = rest[2 * ns + 14]
        scatter_start, scatter_finish = _scatter_ops(rest[:ns], rest[ns + 5:2 * ns + 5], *rest[2 * ns + 15:])
        g = pl.program_id(0)
        ki, qi = k_of_ref[pl.program_id(1)], q_of_ref[pl.program_id(1)]

        @pl.when((g == 0) & (ki == 0) & (qi == 0))
        def _():
            scatter_start()

        @pl.when((ki == 0) & (qi == 0))
        def _():
            dq_s[...] = jnp.zeros_like(dq_s)
            dcr_s[...] = jnp.zeros_like(dcr_s)

        @pl.when(qi == ki)
        def _():
            dk_s[...] = jnp.zeros_like(dk_s)
            dv_s[...] = jnp.zeros_like(dv_s)
            dcc_s[...] = jnp.zeros_like(dcc_s)

        def step(diagonal):
            chunks = [slice(c * KEY_CHUNK, (c + 1) * KEY_CHUNK) for c in range(tq // KEY_CHUNK)]

            def scores(hh, rows, slot):
                sl = slice(hh * HEAD_PAD, (hh + 1) * HEAD_PAD)
                s_s[slot][rows, :] = lax.dot_general(q_ref[rows, sl], k_ref[:, sl], NT, preferred_element_type=F32)
                d_s[slot][rows, :] = lax.dot_general(do_ref[rows, sl], v_ref[:, sl], NT, preferred_element_type=F32)

            for rows in chunks:
                scores(0, rows, 0)
            for hh in range(GROUP_HEADS):
                sl = slice(hh * HEAD_PAD, (hh + 1) * HEAD_PAD)
                slot = hh % 2
                dv, dk = dv_s[sl, :], dk_s[sl, :]
                for rows in chunks:
                    if hh + 1 < GROUP_HEADS:
                        scores(hh + 1, rows, 1 - slot)
                    p = jnp.exp2(s_s[slot][rows, :])
                    if diagonal:
                        row = rows.start + lax.broadcasted_iota(jnp.int32, (KEY_CHUNK, tq), 0)
                        col = lax.broadcasted_iota(jnp.int32, (KEY_CHUNK, tq), 1)
                        p = jnp.where(row >= col, p, 0.0)
                    ds = p * d_s[slot][rows, :]
                    qrows = pl.ds(pl.multiple_of(qi * tq + rows.start, KEY_CHUNK), KEY_CHUNK)
                    dcc_s[hh:hh + 1, :] += jnp.sum(ds, axis=0, keepdims=True)
                    dcr_s[qrows, hh:hh + 1] += jnp.sum(ds, axis=1, keepdims=True)
                    ds = ds.astype(BF16)
                    dv = dv + jnp.dot(dot_ref[sl, rows], p.astype(BF16), preferred_element_type=F32)
                    dk = dk + jnp.dot(qt_ref[sl, rows], ds, preferred_element_type=F32)
                    dq_s[qrows, sl] += jnp.dot(ds, k_ref[:, sl], preferred_element_type=F32)
                dv_s[sl, :] = dv
                dk_s[sl, :] = dk

        @pl.when(qi > ki)
        def _():
            step(False)

        @pl.when(qi == ki)
        def _():
            step(True)

        @pl.when(qi == n - 1)
        def _():
            dk = dk_s[...]
            pt = pt_ref[...]
            dk_ref[...] = lax.dot_general((dk * (1.0 / LOG2E)).astype(BF16), pt, TN, preferred_element_type=F32).astype(BF16)
            dv_ref[...] = lax.dot_general(dv_s[...].astype(BF16), pt, TN, preferred_element_type=F32).astype(BF16)
            dcc_ref[...] = dcc_s[...]

        @pl.when((ki == n - 1) & (qi == n - 1))
        def _():
            copies = [pltpu.make_async_copy(dq_s, dq_hbm.at[g], sems.at[0]), pltpu.make_async_copy(dcr_s, dcr_hbm.at[g], sems.at[1])]
            for cp in copies:
                cp.start()
            for cp in copies:
                cp.wait()

        @pl.when((g == GROUPS - 1) & (ki == n - 1) & (qi == n - 1))
        def _():
            scatter_finish()

    gw = GROUP_HEADS * HEAD_DIM
    qspec = pl.BlockSpec((None, tq, GROUP_PAD), lambda g, i, ks, qs: (g, qs[i], 0))
    qtspec = pl.BlockSpec((None, GROUP_PAD, tq), lambda g, i, ks, qs: (qs[i], g, 0))
    kspec = pl.BlockSpec((tq, GROUP_PAD), lambda g, i, ks, qs: (ks[i], g))
    kout = pl.BlockSpec((tq, gw), lambda g, i, ks, qs: (ks[i], g))
    out = pl.pallas_call(
        body, name="attn_bwd",
        grid_spec=pltpu.PrefetchScalarGridSpec(
            num_scalar_prefetch=2, grid=(GROUPS, len(pairs)),
            in_specs=[qspec, qtspec, kspec, kspec, qspec, qtspec, pl.BlockSpec((GROUP_PAD, gw), lambda g, i, ks, qs: (0, 0))]
            + [_ANY] * ns,
            out_specs=[_ANY, _ANY, kout, kout, pl.BlockSpec((None, SUBLANES, tq), lambda g, i, ks, qs: (g, 0, ks[i]))] + [_ANY] * ns,
            scratch_shapes=[pltpu.VMEM((S, GROUP_PAD), F32), pltpu.VMEM((S, LANES), F32), pltpu.VMEM((GROUP_PAD, tq), F32),
                            pltpu.VMEM((GROUP_PAD, tq), F32), pltpu.VMEM((SUBLANES, tq), F32),
                            pltpu.VMEM((tq, tq), F32), pltpu.VMEM((tq, tq), F32), pltpu.VMEM((tq, tq), F32),
                            pltpu.VMEM((tq, tq), F32), pltpu.SemaphoreType.DMA((2,))]
            + _scatter_sems(ns)),
        out_shape=[jax.ShapeDtypeStruct((GROUPS, S, GROUP_PAD), F32), jax.ShapeDtypeStruct((GROUPS, S, LANES), F32),
                   jax.ShapeDtypeStruct((S, D_HEADS), BF16), jax.ShapeDtypeStruct((S, D_HEADS), BF16),
                   jax.ShapeDtypeStruct((GROUPS, SUBLANES, S), F32)]
        + _scatter_shapes(sums16),
        compiler_params=_params(("arbitrary", "arbitrary")),
    )(k_of, q_of, qb, qbt, ka, va, dop, dopt, k["place_t_group"], *sums16)
    return out[0], out[1], out[2], out[3], out[4], out[5:]


def _attn_unpack(dqp, k, tm):
    S = dqp.shape[1]
    gw = GROUP_HEADS * HEAD_DIM

    def body(dqp_ref, pt_ref, dq_ref):
        for g in range(GROUPS):
            dq_ref[:, g * gw:(g + 1) * gw] = jnp.dot((dqp_ref[g] * SCALE).astype(BF16), pt_ref[...],
                                                     preferred_element_type=F32).astype(BF16)

    return pl.pallas_call(
        body, name="attn_unpack", grid=(S // tm,),
        in_specs=[pl.BlockSpec((GROUPS, tm, GROUP_PAD), lambda i: (0, i, 0)), _full((GROUP_PAD, gw))],
        out_specs=pl.BlockSpec((tm, D_HEADS), lambda i: (i, 0)),
        out_shape=jax.ShapeDtypeStruct((S, D_HEADS), BF16),
        compiler_params=_params(("parallel",)),
    )(dqp, k["place_t_group"])


def _fox_bwd(dc, f, bias_row, tb):
    S = f.shape[0]
    nb = S // tb

    def body(dc_ref, f_ref, b_ref, df_ref, dbias_ref, carry):
        @pl.when(pl.program_id(0) == 0)
        def _():
            carry[...] = jnp.zeros_like(carry)
            dbias_ref[...] = jnp.zeros_like(dbias_ref)

        r = lax.broadcasted_iota(jnp.int32, (tb, tb), 0)
        s = lax.broadcasted_iota(jnp.int32, (tb, tb), 1)
        tri = (s >= r).astype(F32)
        rc = jnp.dot(tri, dc_ref[...], precision=lax.Precision.HIGHEST, preferred_element_type=F32) + carry[0:1, :]
        carry[...] = jnp.broadcast_to(rc[0:1, :], carry.shape)
        lane = lax.broadcasted_iota(jnp.int32, (tb, LANES), 1)
        df = jnp.where(lane < N_HEADS, rc * jax.nn.sigmoid(-(f_ref[...] + b_ref[...])), 0.0)
        df_ref[...] = df.astype(BF16)
        dbias_ref[...] += jnp.sum(df, axis=0, keepdims=True)

    rev = pl.BlockSpec((tb, LANES), lambda i: (nb - 1 - i, 0))
    return pl.pallas_call(
        body, name="fox_bwd", grid=(nb,),
        in_specs=[rev, rev, _full((1, LANES))],
        out_specs=[rev, _full((1, LANES))],
        out_shape=[jax.ShapeDtypeStruct((S, LANES), BF16), jax.ShapeDtypeStruct((1, LANES), F32)],
        scratch_shapes=[pltpu.VMEM((SUBLANES, LANES), F32)],
        compiler_params=_params(("arbitrary",)),
    )(dc, f, bias_row)


_DZ_WIDTHS = (D_HEADS,) * 5 + (LANES,)


def _in_bwd(pieces, w_in, x, g1, dx1, tm, sums16):
    S = x.shape[0]
    ns = len(sums16)

    def body(*refs):
        p_refs, (w_ref, x_ref, g_ref, dx1_ref) = refs[:6], refs[6:10]
        dx_ref, dg_ref = refs[10 + ns:12 + ns]
        scatter_start, scatter_finish = _scatter_ops(refs[10:10 + ns], refs[12 + ns:12 + 2 * ns], *refs[12 + 2 * ns:])

        @pl.when(pl.program_id(0) == 0)
        def _():
            dg_ref[...] = jnp.zeros_like(dg_ref)
            scatter_start()

        dh = jnp.zeros((tm, D_MODEL), F32)
        off = 0
        for p_ref, w in zip(p_refs, _DZ_WIDTHS):
            dh = dh + lax.dot_general(p_ref[...].astype(BF16), w_ref[:, off:off + w], NT, preferred_element_type=F32)
            off += w
        dx, dg = _rms_bwd(dh, x_ref[...], g_ref[...])
        dg_ref[...] += dg
        dx_ref[...] = dx1_ref[...] + dx

        @pl.when(pl.program_id(0) == pl.num_programs(0) - 1)
        def _():
            scatter_finish()

    row = lambda w: pl.BlockSpec((tm, w), lambda i: (i, 0))
    out = pl.pallas_call(
        body, name="in_bwd", grid=(S // tm,),
        in_specs=[row(w) for w in _DZ_WIDTHS] + [_full((D_MODEL, D_IN_PAD)), row(D_MODEL), _full((1, D_MODEL)), row(D_MODEL)]
        + [_ANY] * ns,
        out_specs=[row(D_MODEL), _full((1, D_MODEL))] + [_ANY] * ns,
        out_shape=[jax.ShapeDtypeStruct((S, D_MODEL), F32), jax.ShapeDtypeStruct((1, D_MODEL), F32)] + _scatter_shapes(sums16),
        scratch_shapes=_scatter_sems(ns),
        compiler_params=_params(("arbitrary",)),
    )(*pieces, w_in, x, g1, dx1, *sums16)
    return out[0], out[1], out[2:]


def _dw_in(h1, pieces, tk):
    S = h1.shape[0]

    def body(*refs):
        h_ref, p_refs, o_ref = refs[0], refs[1:7], refs[7]

        @pl.when(pl.program_id(0) == 0)
        def _():
            o_ref[...] = jnp.zeros_like(o_ref)

        off = 0
        for p_ref, w in zip(p_refs, _DZ_WIDTHS):
            o_ref[:, off:off + w] += lax.dot_general(h_ref[...], p_ref[...].astype(BF16), TN, preferred_element_type=F32)
            off += w

    row = lambda w: pl.BlockSpec((tk, w), lambda k: (k, 0))
    return pl.pallas_call(
        body, name="dw_in", grid=(S // tk,),
        in_specs=[row(D_MODEL)] + [row(w) for w in _DZ_WIDTHS],
        out_specs=_full((D_MODEL, D_IN_PAD)),
        out_shape=jax.ShapeDtypeStruct((D_MODEL, D_IN_PAD), F32),
        compiler_params=_params(("arbitrary",)),
    )(h1, *pieces)


def _adamw_math(w, g, m, v):
    m = ADAM_B1 * m + (1.0 - ADAM_B1) * g
    v = ADAM_B2 * v + (1.0 - ADAM_B2) * (g * g)
    m_hat = m / (1.0 - ADAM_B1 ** ADAM_STEP)
    v_hat = v / (1.0 - ADAM_B2 ** ADAM_STEP)
    delta = -ADAM_LR * (m_hat / (jnp.sqrt(v_hat) + ADAM_EPS) + ADAM_WD * w)
    return delta, m, v


def _adamw(name, w, g, m, v):
    R, C = w.shape
    tr = _row_tile(R, 256)

    def body(w_ref, g_ref, m_ref, v_ref, go_ref, d_ref, nm_ref, nv_ref):
        g = g_ref[...]
        d, nm, nv = _adamw_math(w_ref[...], g, m_ref[...], v_ref[...])
        go_ref[...] = g
        d_ref[...] = d
        nm_ref[...] = nm
        nv_ref[...] = nv

    spec = pl.BlockSpec((tr, C), lambda i: (i, 0))
    return pl.pallas_call(
        body, name=name, grid=(R // tr,), in_specs=[spec] * 4, out_specs=[spec] * 4,
        out_shape=[jax.ShapeDtypeStruct((R, C), F32)] * 4,
        compiler_params=_params(("parallel",)),
    )(w, g, m, v)


def _pair_sum(name, grad, theirs, ids):
    q, half, C = theirs.shape
    tr = _row_tile(half, 256)
    nb = half // tr

    def body(ids_ref, a_ref, b_ref, s_ref, sb_ref):
        s = a_ref[...] + b_ref[...]
        s_ref[...] = s
        sb_ref[...] = s.astype(BF16)

    here = pl.BlockSpec((None, tr, C), lambda j, i, ids: (j, i, 0))
    return pl.pallas_call(
        body, name=name,
        grid_spec=pltpu.PrefetchScalarGridSpec(
            num_scalar_prefetch=1, grid=(q, nb),
            in_specs=[pl.BlockSpec((None, tr, C), lambda j, i, ids: (j, ids[1] * nb + i, 0)), here],
            out_specs=[here, here]),
        out_shape=[jax.ShapeDtypeStruct((q, half, C), F32), jax.ShapeDtypeStruct((q, half, C), BF16)],
        compiler_params=_params(("parallel", "parallel")),
    )(ids, grad, theirs)


def _chip_sum(name, sums32, others, ids):
    _, half, C = sums32.shape
    tr = _row_tile(half, 256)
    nb = half // tr

    def body(ids_ref, a_ref, o_ref, s_ref):
        s = a_ref[...]
        for j in range(3):
            s = s + o_ref[j].astype(F32)
        s_ref[...] = s

    return pl.pallas_call(
        body, name=name,
        grid_spec=pltpu.PrefetchScalarGridSpec(
            num_scalar_prefetch=1, grid=(nb,),
            in_specs=[pl.BlockSpec((None, tr, C), lambda i, ids: (ids[0], i, 0)),
                      pl.BlockSpec((3, tr, C), lambda i, ids: (0, i, 0))],
            out_specs=pl.BlockSpec((tr, C), lambda i, ids: (ids[1] * nb + i, 0))),
        out_shape=jax.ShapeDtypeStruct((2 * half, C), F32),
        compiler_params=_params(("parallel",)),
    )(ids, sums32, others)


def _place():
    return lax.axis_index("x"), lax.axis_index("y"), lax.axis_index("c")


def _other_chips(x, y):
    return [(1 - x, y), (x, 1 - y), (1 - x, 1 - y)]


_ANY = pl.BlockSpec(memory_space=pl.ANY)


def _gather_quarters(shards):
    n = len(shards)

    def body(*refs):
        start, hand_on, finish = _gather_ops(refs[:n], refs[n:2 * n], *refs[2 * n:])
        start()
        hand_on()
        finish()

    return pl.pallas_call(
        body, name="gather_weights",
        in_specs=[_ANY] * n, out_specs=[_ANY] * n,
        out_shape=_gather_shapes(shards), scratch_shapes=_gather_sems(n),
    )(*shards)


def _gather_shapes(shards):
    return [jax.ShapeDtypeStruct((4,) + s.shape, s.dtype) for s in shards]


def _gather_sems(n):
    return [pltpu.SemaphoreType.DMA((n, 3))] * 4 + [pltpu.SemaphoreType.DMA((n,))]


def _gather_ops(ins, outs, send_sems, recv_sems, pass_send_sems, pass_recv_sems, own_sems):
    n = len(ins)
    halved = [r.shape[0] % 32 == 0 for r in ins]

    def part(a, quarter, core):
        if not halved[a]:
            return outs[a].at[quarter]
        half = ins[a].shape[0] // 2
        return outs[a].at[quarter, pl.ds(core * half, half), :]

    def ici(a, j, quarter):
        x, y, c = _place()
        px, py = _other_chips(x, y)[j]
        src = ins[a]
        if halved[a]:
            half = src.shape[0] // 2
            src = src.at[pl.ds(c * half, half), :]
        return pltpu.make_async_remote_copy(src_ref=src, dst_ref=part(a, quarter, c), send_sem=send_sems.at[a, j],
                                            recv_sem=recv_sems.at[a, j], device_id=(px, py, c), device_id_type=MESH)

    def passed(a, j, core):
        x, y, c = _place()
        px, py = _other_chips(x, y)[j]
        half = part(a, 2 * px + py, core)
        return pltpu.make_async_remote_copy(src_ref=half, dst_ref=half, send_sem=pass_send_sems.at[a, j],
                                            recv_sem=pass_recv_sems.at[a, j], device_id=(x, y, 1 - c), device_id_type=MESH)

    def own(a):
        x, y, _ = _place()
        return pltpu.make_async_copy(ins[a], outs[a].at[2 * x + y], own_sems.at[a])

    def start():
        x, y, _ = _place()
        for a in range(n):
            for j in range(3):
                ici(a, j, 2 * x + y).start()
            own(a).start()

    def hand_on():
        x, y, c = _place()
        for a in range(n):
            for j, (px, py) in enumerate(_other_chips(x, y)):
                ici(a, j, 2 * px + py).wait_recv()
                if halved[a]:
                    passed(a, j, c).start()

    def finish():
        x, y, c = _place()
        for a in range(n):
            for j in range(3):
                if halved[a]:
                    passed(a, j, 1 - c).wait_recv()
                    passed(a, j, c).wait_send()
                ici(a, j, 2 * x + y).wait_send()
            own(a).wait()

    return start, hand_on, finish


def _swap_halves(grads, name):
    n = len(grads)

    def body(*refs):
        start, finish = _swap_ops(refs[:n], refs[n:2 * n], *refs[2 * n:])
        start()
        finish()

    return pl.pallas_call(
        body, name=name,
        in_specs=[_ANY] * n, out_specs=[_ANY] * n, out_shape=_swap_shapes(grads), scratch_shapes=_swap_sems(n),
    )(*grads)


def _swap_shapes(grads):
    return [jax.ShapeDtypeStruct((4, g.shape[1] // 2, g.shape[2]), F32) for g in grads]


def _swap_sems(n):
    return [pltpu.SemaphoreType.DMA((n,))] * 2


def _swap_ops(ins, outs, send_sems, recv_sems):
    def copy(a):
        x, y, c = _place()
        half = ins[a].shape[1] // 2
        return pltpu.make_async_remote_copy(src_ref=ins[a].at[:, pl.ds((1 - c) * half, half), :], dst_ref=outs[a],
                                            send_sem=send_sems.at[a], recv_sem=recv_sems.at[a],
                                            device_id=(x, y, 1 - c), device_id_type=MESH)

    def start():
        for a in range(len(ins)):
            copy(a).start()

    def finish():
        for a in range(len(ins)):
            copy(a).wait()

    return start, finish


def _scatter_shapes(sums16):
    return [jax.ShapeDtypeStruct((3,) + s.shape[1:], BF16) for s in sums16]


def _scatter_sems(n):
    return [pltpu.SemaphoreType.DMA((n, 3))] * 2


def _scatter_ops(ins, outs, send_sems, recv_sems):
    n = len(ins)

    def copy(a, j):
        x, y, c = _place()
        px, py = _other_chips(x, y)[j]
        return pltpu.make_async_remote_copy(src_ref=ins[a].at[2 * px + py], dst_ref=outs[a].at[j], send_sem=send_sems.at[a, j],
                                            recv_sem=recv_sems.at[a, j], device_id=(px, py, c), device_id_type=MESH)

    def start():
        for a in range(n):
            for j in range(3):
                copy(a, j).start()

    def finish():
        for a in range(n):
            for j in range(3):
                copy(a, j).wait()

    return start, finish


def _join_halves(fulls):
    n = len(fulls)

    def body(*refs):
        ins, outs = refs[:n], refs[n:2 * n]
        send_sems, recv_sems = refs[2 * n:]
        x, y, c = _place()
        started = []
        for a in range(n):
            half = ins[a].shape[0] // 2
            rows = pl.ds(c * half, half)
            cp = pltpu.make_async_remote_copy(src_ref=ins[a].at[rows, :], dst_ref=outs[a].at[rows, :], send_sem=send_sems.at[a],
                                              recv_sem=recv_sems.at[a], device_id=(x, y, 1 - c), device_id_type=MESH)
            cp.start()
            started.append(cp)
        for cp in started:
            cp.wait()

    return pl.pallas_call(
        body, name="join_halves",
        in_specs=[_ANY] * n, out_specs=[_ANY] * n,
        out_shape=[jax.ShapeDtypeStruct(f.shape, F32) for f in fulls],
        input_output_aliases={a: a for a in range(n)},
        scratch_shapes=[pltpu.SemaphoreType.DMA((n,)), pltpu.SemaphoreType.DMA((n,))],
    )(*fulls)


def _small_allreduce(g):
    R = g.shape[0]
    half = R // 2

    def body(g_ref, out_ref, other_s, chip_s, parts_s, send_sems, recv_sems):
        x, y, c = _place()
        mine = 2 * x + y
        rows = pl.ds(pl.multiple_of(c * half, SUBLANES), half)

        def to_other_core(src, dst, k):
            return pltpu.make_async_remote_copy(src_ref=src, dst_ref=dst, send_sem=send_sems.at[k], recv_sem=recv_sems.at[k],
                                                device_id=(x, y, 1 - c), device_id_type=MESH)

        swap = to_other_core(g_ref, other_s, 0)
        swap.start()
        swap.wait()
        chip_s[...] = g_ref[...] + other_s[...]
        parts_s[mine] = chip_s[rows, :]
        sends = []
        for j, (px, py) in enumerate(_other_chips(x, y)):
            cp = pltpu.make_async_remote_copy(src_ref=chip_s.at[rows, :], dst_ref=parts_s.at[mine], send_sem=send_sems.at[1 + j],
                                              recv_sem=recv_sems.at[1 + j], device_id=(px, py, c), device_id_type=MESH)
            cp.start()
            sends.append(cp)
        for cp in sends:
            cp.wait()
        out_ref[rows, :] = (parts_s[0] + parts_s[1]) + (parts_s[2] + parts_s[3])
        join = to_other_core(out_ref.at[rows, :], out_ref.at[rows, :], 4)
        join.start()
        join.wait()

    vm = pl.BlockSpec(memory_space=pltpu.VMEM)
    return pl.pallas_call(
        body, name="small_allreduce",
        in_specs=[vm], out_specs=vm, out_shape=jax.ShapeDtypeStruct((R, LANES), F32),
        scratch_shapes=[pltpu.VMEM((R, LANES), F32), pltpu.VMEM((R, LANES), F32), pltpu.VMEM((4, half, LANES), F32),
                        pltpu.SemaphoreType.DMA((5,)), pltpu.SemaphoreType.DMA((5,))],
        compiler_params=pltpu.CompilerParams(vmem_limit_bytes=VMEM_LIMIT),
    )(g)


def _adamw_small(ws, gs, ms, vs):
    n = len(ws)

    def body(*refs):
        for k in range(n):
            w_ref, g_ref, m_ref, v_ref = (refs[j * n + k] for j in range(4))
            d, nm, nv = _adamw_math(w_ref[...], g_ref[...], m_ref[...], v_ref[...])
            refs[4 * n + k][...] = d
            refs[5 * n + k][...] = nm
            refs[6 * n + k][...] = nv

    vm = pl.BlockSpec(memory_space=pltpu.VMEM)
    out = pl.pallas_call(
        body, name="adamw_small",
        in_specs=[vm] * (4 * n), out_specs=[vm] * (3 * n),
        out_shape=[jax.ShapeDtypeStruct(w.shape, F32) for w in ws] * 3,
        compiler_params=pltpu.CompilerParams(vmem_limit_bytes=VMEM_LIMIT),
    )(*ws, *gs, *ms, *vs)
    return out[:n], out[n:2 * n], out[2 * n:]


_SMALL = (("norm_mix_g", D_MODEL), ("f_bias", N_HEADS), ("sg_ln_g", D_HEADS), ("sg_w", N_HEADS * SG_BLOCK * SG_BLOCK),
          ("sg_b", N_HEADS * SG_BLOCK), ("norm_ffn_g", D_MODEL), ("w_conv", 3 * 2 * D_FF), ("b_conv", 2 * D_FF),
          ("norm_final_g", D_MODEL))
_PACKED = _SMALL + (("sq_err", D_MODEL),)


def _pack_small(parts):
    rows = []
    for name, size in _PACKED:
        flat = parts[name].reshape(-1).astype(F32)
        pad = (-size) % (SUBLANES * LANES)
        rows.append(jnp.pad(flat, (0, pad)).reshape(-1, LANES))
    packed = jnp.concatenate(rows, axis=0)
    return jnp.pad(packed, ((0, (-packed.shape[0]) % (2 * SUBLANES)), (0, 0)))


def _unpack_small(packed, shapes):
    out, r = {}, 0
    for name, size in _PACKED:
        nrows = (size + SUBLANES * LANES - 1) // (SUBLANES * LANES) * SUBLANES
        out[name] = packed[r:r + nrows].reshape(-1)[:size].reshape(shapes[name])
        r += nrows
    return out


def _local_step(x, target, g1, w_in, f_bias, sg_ln_g, sg_w, sg_b, g2, b_conv, g3, late_shards, ids):
    S = x.shape[0]
    tm = _row_tile(S, 512)
    tms = _row_tile(S, 256)
    tq = tm

    lane = jnp.arange(D_HEADS)
    seg_avg = jnp.where(lane[:, None] // HEAD_DIM == lane[None, :] // HEAD_DIM, 1.0 / HEAD_DIM, 0.0).astype(BF16)
    head_ind = (lane[:, None] // HEAD_DIM == jnp.arange(LANES)[None, :]).astype(BF16)
    pos_chunk = jnp.arange(SG_BLOCK) // CHUNK
    w_mask32 = jnp.where(pos_chunk[:, None] >= pos_chunk[None, :], sg_w, 0.0)
    w_mask = w_mask32.astype(BF16)
    w_mask_t = jnp.swapaxes(w_mask32, 1, 2).astype(BF16)
    ln_row = sg_ln_g.reshape(1, D_HEADS)
    b_full = jnp.repeat(sg_b.T, HEAD_DIM, axis=1)
    bias_row = jnp.pad(f_bias.reshape(1, N_HEADS), ((0, 0), (0, LANES - N_HEADS)))
    b_conv_row = b_conv.reshape(1, 2 * D_FF)

    z, f, h1 = _in_proj(x, g1, w_in, tm)
    c = _fox_prep(f, bias_row, _row_tile(S, 256))
    consts = _attn_consts()
    qa, ka, va, vat = _attn_pack(z, c, consts, tm)
    out_b, lse, gathered = _attn_fwd(qa, ka, vat, consts["place_t"], tq, late_shards)
    g_out, w_up_q, g_down, g_conv = gathered
    w_out = g_out.reshape(D_MODEL, D_MODEL)
    w_down = g_down.reshape(D_FF, D_MODEL)
    w_conv = jnp.concatenate([g_conv[q] for q in range(4)], axis=1)
    out_a = _gate_fwd(z, w_mask, ln_row, b_full, seg_avg, tm)
    x1, h2 = _mix_out(x, out_a, out_b, w_out, g2, tm)
    a = _up_proj(h2, w_up_q, tm)
    dx2, sq_err, dg3 = _ffn_fwd_loss(a, w_conv, b_conv_row, w_down, x1, g3, target, tms)

    dconv, y, dw_conv8, db_conv = _ffn_bwd_gate(dx2, a, w_conv, b_conv_row, w_down, tms)
    dact = _conv_bwd(dconv, w_conv, tms, D_FF)
    dw_down = _matmul_tn(y, dx2, "dw_down", D_FF // 2, D_MODEL, tm, quarters=(2, 1))
    dx1, dg2 = _up_bwd(dact, w_up_q, x1, g2, dx2, tm)
    dw_up_q = _matmul_tn(h2, dact, "dw_up", D_MODEL, 2 * D_FF // 4, tm, quarters=(1, 4))
    dcat = _out_bwd(dx1, w_out, tm)
    dw_out_a = _matmul_tn(out_a, dx1, "dw_out_a", D_HEADS, D_MODEL, tm)
    dw_out_b = _matmul_tn(out_b, dx1, "dw_out_b", D_HEADS, D_MODEL, tm)
    early = {"w_down": dw_down.reshape(4, D_FF // 4, D_MODEL), "w_up": dw_up_q,
             "w_out": jnp.concatenate([dw_out_a, dw_out_b], axis=0).reshape(4, D_MODEL // 4, D_MODEL)}
    (dzu, dzv, dsg_w, dsg_b_t, dln), theirs = _gate_bwd(z, dcat, w_mask, w_mask_t, ln_row, b_full, seg_avg, head_ind, tm,
                                                        list(early.values()))
    early_sums = _chip_sums(early, theirs, ids)
    dop, qb, dopt, qbt = _attn_pack_grad(out_b, dcat, qa, lse, head_ind, consts, tm)
    dqp, dc_rows, dk, dv, dc_cols, landed = _attn_bwd(qb, qbt, ka, va, dop, dopt, consts, tq,
                                                      [s16 for _, s16 in early_sums.values()])
    early_parts = {k: (s32, got) for (k, (s32, _)), got in zip(early_sums.items(), landed)}
    dq = _attn_unpack(dqp, consts, tm)
    dc_rows = jnp.concatenate([dc_rows[g][:, :GROUP_HEADS] for g in range(GROUPS)], axis=1)
    dc_cols = jnp.concatenate([dc_cols[g][:GROUP_HEADS] for g in range(GROUPS)], axis=0).T
    dc = jnp.pad(dc_rows - dc_cols, ((0, 0), (0, LANES - N_HEADS)))
    df, dbias = _fox_bwd(dc, f, bias_row, _row_tile(S, 256))
    pieces = (dzu, dzv, dq, dk, dv, df)
    dw_in = _dw_in(h1, pieces, tm)[:, :D_IN].reshape(D_MODEL, 4, D_IN // 4).transpose(1, 0, 2)
    (w_in_sum, w_in_sum16), = _chip_sums({"w_in": dw_in}, _swap_halves([dw_in], "swap_halves"), ids).values()
    dx, dg1, (w_in_landed,) = _in_bwd(pieces, w_in, x, g1, dx1, tm, [w_in_sum16])

    grads = {
        "norm_mix_g": dg1, "f_bias": dbias[:, :N_HEADS], "sg_ln_g": dln, "sg_w": dsg_w, "sg_b": dsg_b_t[:, :N_HEADS].T,
        "norm_ffn_g": dg2, "w_conv": dw_conv8[:3], "b_conv": db_conv, "norm_final_g": dg3,
    }
    return sq_err, dx, grads, {**early_parts, "w_in": (w_in_sum, w_in_landed)}


def _chip_sums(grads_q, theirs, ids):
    return {k: _pair_sum("pair_sum_" + k, g, t, ids) for (k, g), t in zip(grads_q.items(), theirs)}


def _finish_reduction(parts, ids):
    names = list(parts)
    fulls = [_chip_sum("chip_sum_" + k, s32, got, ids) for k, (s32, got) in parts.items()]
    return dict(zip(names, _join_halves(fulls)))


def kernel(x, norm_mix_g, w_in, f_bias, sg_ln_g, sg_w, sg_b, w_out, norm_ffn_g, w_up, w_conv, b_conv, w_down, norm_final_g, loss_target, m_norm_mix_g, m_w_in, m_f_bias, m_sg_ln_g, m_sg_w, m_sg_b, m_w_out, m_norm_ffn_g, m_w_up, m_w_conv, m_b_conv, m_w_down, m_norm_final_g, v_norm_mix_g, v_w_in, v_f_bias, v_sg_ln_g, v_sg_w, v_sg_b, v_w_out, v_norm_ffn_g, v_w_up, v_w_conv, v_b_conv, v_w_down, v_norm_final_g):
    args = dict(locals())
    quarter = 2 * lax.axis_index("x") + lax.axis_index("y")
    ids = jnp.stack([quarter, lax.axis_index("c")]).astype(jnp.int32)
    wq_conv = w_conv.shape[-1]

    g_in = _gather_quarters([w_in[0].astype(BF16)])[0]
    w_in_full = jnp.pad(jnp.concatenate([g_in[q] for q in range(4)], axis=1), ((0, 0), (0, D_IN_PAD - D_IN)))
    late_shards = [w_out[0].astype(BF16), w_up[0].astype(BF16), w_down[0].astype(BF16), w_conv[0]]

    sq_err, dx, grads, parts = _local_step(
        x[0], loss_target[0], norm_mix_g, w_in_full, f_bias[0], sg_ln_g[0], sg_w[0], sg_b[0], norm_ffn_g, b_conv[0],
        norm_final_g.reshape(1, D_MODEL), late_shards, ids)
    big = _finish_reduction(parts, ids)

    out = {"grad_x": dx[None]}
    for k in ("w_in", "w_out", "w_up", "w_down"):
        g, d, nm, nv = _adamw("adamw_" + k, args[k][0], big[k], args["m_" + k][0], args["v_" + k][0])
        out["grad_" + k], out["delta_" + k], out["new_m_" + k], out["new_v_" + k] = g[None], d[None], nm[None], nv[None]

    small_names = [n for n, _ in _SMALL]
    shapes = {n: (3, 4 * wq_conv) if n == "w_conv" else args[n].shape for n in small_names}
    shapes["sq_err"] = sq_err.shape
    g_small = _unpack_small(_small_allreduce(_pack_small({**{n: grads[n] for n in small_names}, "sq_err": sq_err})), shapes)
    out["loss"] = 0.5 * jnp.sum(g_small.pop("sq_err")) / D_MODEL
    g_small["w_conv"] = lax.dynamic_slice(g_small["w_conv"], (0, quarter * wq_conv), (3, wq_conv))[None]
    flat2d = lambda t: t.reshape(-1, t.shape[-1])
    updated = _adamw_small(*[[flat2d(src[p + n]) for n in small_names] for src, p in
                             ((args, ""), (g_small, ""), (args, "m_"), (args, "v_"))])
    for n, g in g_small.items():
        out["grad_" + n] = g
    for prefix, arrs in zip(("delta_", "new_m_", "new_v_"), updated):
        for n, t in zip(small_names, arrs):
            out[prefix + n] = t.reshape(args[n].shape)

    weights = ["norm_mix_g", "w_in", "f_bias", "sg_ln_g", "sg_w", "sg_b", "w_out", "norm_ffn_g", "w_up", "w_conv", "b_conv",
               "w_down", "norm_final_g"]
    return (out["loss"], out["grad_x"], *[out[p + n] for p in ("grad_", "delta_", "new_m_", "new_v_") for n in weights])
```

```python
import functools
import math

import jax
import jax.numpy as jnp
from jax import lax
from jax.experimental import pallas as pl
from jax.experimental.pallas import tpu as pltpu

F32 = jnp.float32
BF16 = jnp.bfloat16
MESH = pl.DeviceIdType.MESH

D_MODEL = 1024
N_HEADS = 8
HEAD_DIM = 64
D_HEADS = N_HEADS * HEAD_DIM
SG_BLOCK = 128
CHUNK = 64
D_FF = 2816
D_IN = 2 * D_HEADS + 3 * D_HEADS + N_HEADS
LANES = 128
SUBLANES = 8
D_IN_PAD = 5 * D_HEADS + LANES
EPS = 1e-6
SCALE = HEAD_DIM ** -0.5
NEG = -1e30
LOG2E = 1.4426950408889634
HEAD_PAD = LANES
D_PAD = N_HEADS * HEAD_PAD
Q_STAT = HEAD_DIM
K_STAT = HEAD_DIM + 3
L_STAT = HEAD_DIM + 6
GROUPS = 2
GROUP_HEADS = N_HEADS // GROUPS
GROUP_PAD = GROUP_HEADS * HEAD_PAD
KEY_CHUNK = 256
STAT_ROWS = 16
FF_CHUNK = 256

ADAM_LR = 0.001
ADAM_B1 = 0.9
ADAM_B2 = 0.999
ADAM_EPS = 1e-08
ADAM_WD = 0.01
ADAM_STEP = 10

VMEM_LIMIT = 56 * 1024 * 1024

NT = (((1,), (1,)), ((), ()))
TN = (((0,), (0,)), ((), ()))


def _params(sem):
    return pltpu.CompilerParams(dimension_semantics=sem, vmem_limit_bytes=VMEM_LIMIT)


def _full(shape):
    nd = len(shape)
    return pl.BlockSpec(shape, lambda *_: (0,) * nd)


def _row_tile(rows, target):
    best = None
    for t in range(SUBLANES, min(rows, target) + 1, SUBLANES):
        if rows % t == 0:
            best = t
    assert best is not None, rows
    return best


def _sigmoid(x):
    return 0.5 * jnp.tanh(0.5 * x) + 0.5


def _gelu(z):
    return 0.5 * z * (1.0 + lax.erf(z * (2.0 ** -0.5)))


def _gelu_grad(z):
    cdf = 0.5 * (1.0 + lax.erf(z * (2.0 ** -0.5)))
    pdf = jnp.exp(-0.5 * z * z) * (1.0 / math.sqrt(2.0 * math.pi))
    return cdf + z * pdf


def _split_dot(x, m):
    hi = x.astype(BF16)
    lo = (x - hi.astype(F32)).astype(BF16)
    return jnp.dot(hi, m, preferred_element_type=F32) + jnp.dot(lo, m, preferred_element_type=F32)


def _head_mask(h, rows):
    lane = lax.broadcasted_iota(jnp.int32, (rows, D_HEADS), 1)
    return (lane >= h * HEAD_DIM) & (lane < (h + 1) * HEAD_DIM)


def _rms_bwd(dh, x, g):
    r = lax.rsqrt(jnp.mean(x * x, axis=-1, keepdims=True) + EPS)
    xhat = x * r
    dg = jnp.sum(dh * xhat, axis=0, keepdims=True)
    dxhat = dh * g
    dx = r * (dxhat - xhat * jnp.mean(dxhat * xhat, axis=-1, keepdims=True))
    return dx, dg


def _in_proj(x, g1, w_in, tm):
    S = x.shape[0]
    nz = D_IN_PAD - LANES

    def body(x_ref, g_ref, w_ref, z_ref, f_ref, h_ref):
        xf = x_ref[...]
        r = lax.rsqrt(jnp.mean(xf * xf, axis=-1, keepdims=True) + EPS)
        h = (xf * r * g_ref[...]).astype(BF16)
        h_ref[...] = h
        zz = jnp.dot(h, w_ref[...], preferred_element_type=F32)
        z_ref[...] = zz[:, :nz].astype(BF16)
        f_ref[...] = zz[:, nz:]

    return pl.pallas_call(
        body, name="in_proj", grid=(S // tm,),
        in_specs=[pl.BlockSpec((tm, D_MODEL), lambda i: (i, 0)), _full((1, D_MODEL)), _full((D_MODEL, D_IN_PAD))],
        out_specs=[pl.BlockSpec((tm, nz), lambda i: (i, 0)), pl.BlockSpec((tm, LANES), lambda i: (i, 0)),
                   pl.BlockSpec((tm, D_MODEL), lambda i: (i, 0))],
        out_shape=[jax.ShapeDtypeStruct((S, nz), BF16), jax.ShapeDtypeStruct((S, LANES), F32),
                   jax.ShapeDtypeStruct((S, D_MODEL), BF16)],
        compiler_params=_params(("parallel",)),
    )(x, g1, w_in)


def _fox_prep(f, bias_row, tb):
    S = f.shape[0]

    def body(f_ref, b_ref, c_ref, carry):
        @pl.when(pl.program_id(0) == 0)
        def _():
            carry[...] = jnp.zeros_like(carry)

        xv = f_ref[...] + b_ref[...]
        lf = jnp.minimum(xv, 0.0) - jnp.log(1.0 + jnp.exp(-jnp.abs(xv)))
        r = lax.broadcasted_iota(jnp.int32, (tb, tb), 0)
        s = lax.broadcasted_iota(jnp.int32, (tb, tb), 1)
        tri = (r >= s).astype(F32)
        cs = jnp.dot(tri, lf, precision=lax.Precision.HIGHEST, preferred_element_type=F32) + carry[0:1, :]
        c_ref[...] = cs
        carry[...] = jnp.broadcast_to(cs[tb - 1:tb, :], carry.shape)

    return pl.pallas_call(
        body, name="fox_prep", grid=(S // tb,),
        in_specs=[pl.BlockSpec((tb, LANES), lambda i: (i, 0)), _full((1, LANES))],
        out_specs=pl.BlockSpec((tb, LANES), lambda i: (i, 0)),
        out_shape=jax.ShapeDtypeStruct((S, LANES), F32),
        scratch_shapes=[pltpu.VMEM((SUBLANES, LANES), F32)],
        compiler_params=_params(("arbitrary",)),
    )(f, bias_row)


def _attn_consts():
    col = jnp.arange(D_PAD)
    row = jnp.arange(D_HEADS)
    head = jnp.arange(LANES)
    place = (row[:, None] // HEAD_DIM == col[None, :] // HEAD_PAD) & (row[:, None] % HEAD_DIM == col[None, :] % HEAD_PAD)

    def stat(offset):
        return ((head[:, None] < N_HEADS) & (col[None, :] == head[:, None] * HEAD_PAD + offset)).astype(BF16)

    def stat3(base):
        part, h = head // N_HEADS, head % N_HEADS
        return ((part[:, None] < 3) & (col[None, :] == h[:, None] * HEAD_PAD + base + part[:, None])).astype(BF16)

    def ones(offsets):
        return sum((col % HEAD_PAD == o) for o in offsets).astype(F32).reshape(1, D_PAD)

    place = place.astype(BF16)
    return {
        "place": place, "place_t": place.T, "place_t_group": place.T[:GROUP_PAD, :GROUP_HEADS * HEAD_DIM],
        "q_stat": stat3(Q_STAT), "k_stat": stat3(K_STAT),
        "d_stat": stat3(Q_STAT) * (head[:, None] < 2 * N_HEADS).astype(BF16),
        "l_stat": jnp.concatenate([stat(L_STAT + j)[:STAT_ROWS] for j in range(3)], axis=0),
        "q_ones": ones(range(K_STAT, K_STAT + 3)), "k_ones": ones(list(range(Q_STAT, Q_STAT + 3)) + list(range(L_STAT, L_STAT + 3))),
        "v_ones": ones(range(Q_STAT, Q_STAT + 2)),
    }


def _split3(x):
    hi = x.astype(BF16)
    r = x - hi.astype(F32)
    mid = r.astype(BF16)
    return hi, mid, (r - mid.astype(F32)).astype(BF16)


def _attn_pack(z, c, k, tm):
    S = z.shape[0]

    def body(q_ref, k_ref, v_ref, c_ref, pl_ref, pt_ref, qs_ref, ks_ref, qo_ref, ko_ref, vo_ref, voc_ref,
             qa_ref, ka_ref, va_ref, vt_ref):
        place = pl_ref[...]
        q = (q_ref[...].astype(F32) * (SCALE * LOG2E)).astype(BF16)
        qa = jnp.dot(q, place, preferred_element_type=F32) + qo_ref[...]
        ka = jnp.dot(k_ref[...], place, preferred_element_type=F32) + ko_ref[...]
        lane = lax.broadcasted_iota(jnp.int32, (tm, LANES), 1)
        hi, mid, lo = _split3(jnp.where(lane < N_HEADS, c_ref[...] * LOG2E, 0.0))
        parts = hi.astype(F32) + pltpu.roll(mid.astype(F32), N_HEADS, 1) + pltpu.roll(lo.astype(F32), 2 * N_HEADS, 1)
        parts = parts.astype(BF16)
        qa = qa + jnp.dot(parts, qs_ref[...], preferred_element_type=F32)
        ka = ka - jnp.dot(parts, ks_ref[...], preferred_element_type=F32)
        qa_ref[...] = qa.astype(BF16)
        ka_ref[...] = ka.astype(BF16)
        v = v_ref[...]
        va_ref[...] = (jnp.dot(v, place, preferred_element_type=F32) + vo_ref[...]).astype(BF16)
        vt_ref[...] = (lax.dot_general(pt_ref[...], v, NT, preferred_element_type=F32) + voc_ref[...]).astype(BF16)

    blk = lambda col: pl.BlockSpec((tm, D_HEADS), lambda i: (i, col))
    out = pl.BlockSpec((tm, D_PAD), lambda i: (i, 0))
    pad = jax.ShapeDtypeStruct((S, D_PAD), BF16)
    return pl.pallas_call(
        body, name="attn_pack", grid=(S // tm,),
        in_specs=[blk(2), blk(3), blk(4), pl.BlockSpec((tm, LANES), lambda i: (i, 0)), _full((D_HEADS, D_PAD)), _full((D_PAD, D_HEADS)),
                  _full((LANES, D_PAD)), _full((LANES, D_PAD)), _full((1, D_PAD)), _full((1, D_PAD)), _full((1, D_PAD)),
                  _full((D_PAD, 1))],
        out_specs=[out, out, out, pl.BlockSpec((None, D_PAD, tm), lambda i: (i, 0, 0))],
        out_shape=[pad, pad, pad, jax.ShapeDtypeStruct((S // tm, D_PAD, tm), BF16)],
        compiler_params=_params(("parallel",)),
    )(z, z, z, c, k["place"], k["place_t"], k["q_stat"], k["k_stat"], k["q_ones"], k["k_ones"], k["v_ones"], k["v_ones"].T)


def _attn_fwd(qa, ka, vat, place_t, tq, shards):
    S = qa.shape[0]
    n = S // tq
    ns = len(shards)
    hand_on_at = (2 * n) // 3

    pairs = [(q, k) for q in range(n) for k in range(q + 1)]
    q_of = jnp.asarray([q for q, _ in pairs], jnp.int32)
    k_of = jnp.asarray([k for _, k in pairs], jnp.int32)

    def body(q_of_ref, k_of_ref, q_ref, k_ref, vt_ref, pt_ref, *rest):
        o_ref, lse_ref = rest[ns:ns + 2]
        m_s, acc_s, ot_s = rest[2 * ns + 2:2 * ns + 5]
        s_s = rest[2 * ns + 5:2 * ns + 7]
        start, hand_on, finish = _gather_ops(rest[:ns], rest[ns + 2:2 * ns + 2], *rest[2 * ns + 7:])
        qi, ki = q_of_ref[pl.program_id(0)], k_of_ref[pl.program_id(0)]

        @pl.when((qi == 0) & (ki == 0))
        def _():
            start()

        @pl.when((qi == hand_on_at) & (ki == 0))
        def _():
            hand_on()

        @pl.when(ki == 0)
        def _():
            m_s[...] = jnp.full_like(m_s, NEG)
            acc_s[...] = jnp.zeros_like(acc_s)

        def step(diagonal):
            chunks = [slice(c * KEY_CHUNK, (c + 1) * KEY_CHUNK) for c in range(tq // KEY_CHUNK)]

            def scores(h, rows, slot):
                sl = slice(h * HEAD_PAD, (h + 1) * HEAD_PAD)
                st = lax.dot_general(k_ref[rows, sl], q_ref[:, sl], NT, preferred_element_type=F32)
                if diagonal:
                    key = rows.start + lax.broadcasted_iota(jnp.int32, (KEY_CHUNK, tq), 0)
                    query = lax.broadcasted_iota(jnp.int32, (KEY_CHUNK, tq), 1)
                    st = jnp.where(query >= key, st, NEG)
                s_s[slot][rows, :] = st
                return jnp.max(st, axis=0, keepdims=True)

            m_cur = functools.reduce(jnp.maximum, [scores(0, rows, 0) for rows in chunks])
            for h in range(N_HEADS):
                sl = slice(h * HEAD_PAD, (h + 1) * HEAD_PAD)
                slot = h % 2
                m_prev = m_s[h][0:1, :]
                m_new = jnp.maximum(m_prev, m_cur)
                acc = jnp.exp2(m_prev - m_new) * acc_s[h]
                m_next = []
                for rows in chunks:
                    if h + 1 < N_HEADS:
                        m_next.append(scores(h + 1, rows, 1 - slot))
                    pt = jnp.exp2(s_s[slot][rows, :] - m_new).astype(BF16)
                    acc = acc + jnp.dot(vt_ref[sl, rows], pt, preferred_element_type=F32)
                acc_s[h] = acc
                m_s[h] = jnp.broadcast_to(m_new, (SUBLANES, tq))
                if m_next:
                    m_cur = functools.reduce(jnp.maximum, m_next)

        @pl.when(ki < qi)
        def _():
            step(False)

        @pl.when(ki == qi)
        def _():
            step(True)
            lse_ref[...] = jnp.zeros_like(lse_ref)
            for h in range(N_HEADS):
                acc = acc_s[h]
                denom = acc[Q_STAT:Q_STAT + 1, :]
                ot_s[h * HEAD_PAD:(h + 1) * HEAD_PAD, :] = (acc / denom).astype(BF16)
                lse_ref[h:h + 1, :] = m_s[h][0:1, :] + jnp.log(denom) * LOG2E
            o_ref[...] = lax.dot_general(ot_s[...], pt_ref[...], TN, preferred_element_type=F32).astype(BF16)

        @pl.when((qi == n - 1) & (ki == n - 1))
        def _():
            finish()

    out = pl.pallas_call(
        body, name="attn_fwd",
        grid_spec=pltpu.PrefetchScalarGridSpec(
            num_scalar_prefetch=2, grid=(len(pairs),),
            in_specs=[pl.BlockSpec((tq, D_PAD), lambda i, qs, ks: (qs[i], 0)),
                      pl.BlockSpec((tq, D_PAD), lambda i, qs, ks: (ks[i], 0)),
                      pl.BlockSpec((None, D_PAD, tq), lambda i, qs, ks: (ks[i], 0, 0)),
                      pl.BlockSpec((D_PAD, D_HEADS), lambda i, qs, ks: (0, 0))]
            + [_ANY] * ns,
            out_specs=[pl.BlockSpec((tq, D_HEADS), lambda i, qs, ks: (qs[i], 0)),
                       pl.BlockSpec((STAT_ROWS, tq), lambda i, qs, ks: (0, qs[i]))] + [_ANY] * ns,
            scratch_shapes=[pltpu.VMEM((N_HEADS, SUBLANES, tq), F32), pltpu.VMEM((N_HEADS, HEAD_PAD, tq), F32),
                            pltpu.VMEM((D_PAD, tq), BF16), pltpu.VMEM((tq, tq), F32), pltpu.VMEM((tq, tq), F32)] + _gather_sems(ns)),
        out_shape=[jax.ShapeDtypeStruct((S, D_HEADS), BF16), jax.ShapeDtypeStruct((STAT_ROWS, S), F32)] + _gather_shapes(shards),
        compiler_params=_params(("arbitrary",)),
    )(q_of, k_of, qa, ka, vat, place_t, *shards)
    return out[0], out[1], out[2:]


def _layer_norm_heads(v, seg_avg):
    mu = _split_dot(v, seg_avg)
    d = v - mu
    var = _split_dot(d * d, seg_avg)
    rstd = lax.rsqrt(var + EPS)
    return d * rstd, rstd


def _gate_mix(vn_blk, w_ref, bias):
    acc = bias
    for h in range(N_HEADS):
        vh = jnp.where(_head_mask(h, SG_BLOCK), vn_blk, 0.0).astype(BF16)
        acc = acc + jnp.dot(w_ref[h], vh, preferred_element_type=F32)
    return acc


def _gate_fwd(z, w_mask, ln_row, b_full, seg_avg, tm):
    S = z.shape[0]

    def body(zu_ref, zv_ref, w_ref, ln_ref, b_ref, avg_ref, o_ref):
        u = _gelu(zu_ref[...].astype(F32))
        v = _gelu(zv_ref[...].astype(F32))
        vhat, _ = _layer_norm_heads(v, avg_ref[...])
        vn = vhat * ln_ref[...]
        for b in range(tm // SG_BLOCK):
            rows = slice(b * SG_BLOCK, (b + 1) * SG_BLOCK)
            mixed = _gate_mix(vn[rows], w_ref, b_ref[...])
            o_ref[rows, :] = (u[rows] * mixed).astype(BF16)

    return pl.pallas_call(
        body, name="gate_fwd", grid=(S // tm,),
        in_specs=[pl.BlockSpec((tm, D_HEADS), lambda i: (i, 0)), pl.BlockSpec((tm, D_HEADS), lambda i: (i, 1)),
                  _full((N_HEADS, SG_BLOCK, SG_BLOCK)), _full((1, D_HEADS)), _full((SG_BLOCK, D_HEADS)),
                  _full((D_HEADS, D_HEADS))],
        out_specs=pl.BlockSpec((tm, D_HEADS), lambda i: (i, 0)),
        out_shape=jax.ShapeDtypeStruct((S, D_HEADS), BF16),
        compiler_params=_params(("parallel",)),
    )(z, z, w_mask, ln_row, b_full, seg_avg)


def _mix_out(x, out_a, out_b, w_out, g2, tm):
    S = x.shape[0]

    def body(x_ref, a_ref, b_ref, w_ref, g_ref, x1_ref, h_ref):
        y = jnp.dot(a_ref[...], w_ref[:D_HEADS, :], preferred_element_type=F32)
        y = y + jnp.dot(b_ref[...], w_ref[D_HEADS:, :], preferred_element_type=F32)
        x1 = x_ref[...] + y
        x1_ref[...] = x1
        r = lax.rsqrt(jnp.mean(x1 * x1, axis=-1, keepdims=True) + EPS)
        h_ref[...] = (x1 * r * g_ref[...]).astype(BF16)

    row = lambda w: pl.BlockSpec((tm, w), lambda i: (i, 0))
    return pl.pallas_call(
        body, name="mix_out", grid=(S // tm,),
        in_specs=[row(D_MODEL), row(D_HEADS), row(D_HEADS), _full((D_MODEL, D_MODEL)), _full((1, D_MODEL))],
        out_specs=[row(D_MODEL), row(D_MODEL)],
        out_shape=[jax.ShapeDtypeStruct((S, D_MODEL), F32), jax.ShapeDtypeStruct((S, D_MODEL), BF16)],
        compiler_params=_params(("parallel",)),
    )(x, out_a, out_b, w_out, g2)


def _up_proj(h2, w_up_q, tm):
    S = h2.shape[0]
    nq, _, wq = w_up_q.shape

    def body(h_ref, w_ref, a_ref):
        a_ref[...] = jnp.dot(h_ref[...], w_ref[...], preferred_element_type=F32).astype(BF16)

    return pl.pallas_call(
        body, name="up_proj", grid=(nq, S // tm),
        in_specs=[pl.BlockSpec((tm, D_MODEL), lambda j, i: (i, 0)), pl.BlockSpec((None, D_MODEL, wq), lambda j, i: (j, 0, 0))],
        out_specs=pl.BlockSpec((tm, wq), lambda j, i: (i, j)),
        out_shape=jax.ShapeDtypeStruct((S, nq * wq), BF16),
        compiler_params=_params(("parallel", "parallel")),
    )(h2, w_up_q)


def _shift_down(a, halo, k):
    tm = a.shape[0]
    ra = pltpu.roll(a, k, 0)
    rh = pltpu.roll(halo, k, 0)
    row = lax.broadcasted_iota(jnp.int32, halo.shape, 0)
    top = jnp.where(row < k, rh, ra[0:SUBLANES])
    return jnp.concatenate([top, ra[SUBLANES:tm]], axis=0)


def _shift_up(a, halo, k):
    tm = a.shape[0]
    ra = pltpu.roll(a, tm - k, 0)
    rh = pltpu.roll(halo, SUBLANES - k, 0)
    row = lax.broadcasted_iota(jnp.int32, halo.shape, 0)
    bottom = jnp.where(row >= SUBLANES - k, rh, ra[tm - SUBLANES:tm])
    return jnp.concatenate([ra[0:tm - SUBLANES], bottom], axis=0)


def _shift_matrices(tm):
    row = lax.broadcasted_iota(jnp.int32, (tm, tm), 0)
    col = lax.broadcasted_iota(jnp.int32, (tm, tm), 1)
    return [(row == col + k).astype(BF16) for k in (1, 2)]


def _conv_taps(a, halo, first, shifts):
    tm = a.shape[0]
    halo = halo.astype(F32) * jnp.where(first, 0.0, 1.0)
    if shifts is None:
        a = a.astype(F32)
        return a, _shift_down(a, halo, 1), _shift_down(a, halo, 2)
    row8 = lax.broadcasted_iota(jnp.int32, halo.shape, 0)
    taps = [a.astype(F32)]
    for k, shift in zip((1, 2), shifts):
        down = jnp.dot(shift, a, preferred_element_type=F32)
        top = down[0:SUBLANES] + jnp.where(row8 < k, pltpu.roll(halo, k, 0), 0.0)
        taps.append(jnp.concatenate([top, down[SUBLANES:tm]], axis=0))
    return taps


def _conv_gate_val(refs, shifts, cols, first):
    ag_ref, av_ref, hg_ref, hv_ref, wg_ref, wv_ref, bg_ref, bv_ref = refs
    g0, g1, g2 = _conv_taps(ag_ref[:, cols], hg_ref[:, cols], first, shifts)
    gate = wg_ref[2:3, cols] * g0 + wg_ref[1:2, cols] * g1 + wg_ref[0:1, cols] * g2 + bg_ref[:, cols]
    v0, v1, v2 = _conv_taps(av_ref[:, cols], hv_ref[:, cols], first, shifts)
    val = wv_ref[2:3, cols] * v0 + wv_ref[1:2, cols] * v1 + wv_ref[0:1, cols] * v2 + bv_ref[:, cols]
    return gate, val, (g2, g1, g0), (v2, v1, v0)


_FF_CHUNKS = [slice(j * FF_CHUNK, (j + 1) * FF_CHUNK) for j in range(D_FF // FF_CHUNK)]


def _conv_specs(tm):
    step = tm // SUBLANES
    prev = lambda i: jnp.maximum(i * step - 1, 0)
    return [pl.BlockSpec((tm, D_FF), lambda i: (i, 0)), pl.BlockSpec((tm, D_FF), lambda i: (i, 1)),
            pl.BlockSpec((SUBLANES, D_FF), lambda i: (prev(i), 0)), pl.BlockSpec((SUBLANES, D_FF), lambda i: (prev(i), 1))]


def _ffn_fwd_loss(a, w_conv, b_conv, w_down, x1, g3, target, tm):
    S = x1.shape[0]

    def body(ag_ref, av_ref, hg_ref, hv_ref, wg_ref, wv_ref, bg_ref, bv_ref, wd_ref, x1_ref, g_ref, t_ref,
             dx2_ref, loss_ref, dg_ref):
        i = pl.program_id(0)

        @pl.when(i == 0)
        def _():
            loss_ref[...] = jnp.zeros_like(loss_ref)
            dg_ref[...] = jnp.zeros_like(dg_ref)

        x2 = x1_ref[...]
        for cols in _FF_CHUNKS:
            gate, val, _, _ = _conv_gate_val((ag_ref, av_ref, hg_ref, hv_ref, wg_ref, wv_ref, bg_ref, bv_ref), None, cols, i == 0)
            half = 0.5 * gate
            y = ((half + half * jnp.tanh(half)) * val).astype(BF16)
            x2 = x2 + jnp.dot(y, wd_ref[cols, :], preferred_element_type=F32)
        r = lax.rsqrt(jnp.mean(x2 * x2, axis=-1, keepdims=True) + EPS)
        xhat = x2 * r
        gg = g_ref[...]
        err = xhat * gg - t_ref[...]
        loss_ref[...] += jnp.sum(err * err, axis=0, keepdims=True)
        dy = err * (1.0 / D_MODEL)
        dg_ref[...] += jnp.sum(dy * xhat, axis=0, keepdims=True)
        dxhat = dy * gg
        dx2_ref[...] = r * (dxhat - xhat * jnp.mean(dxhat * xhat, axis=-1, keepdims=True))

    row = lambda w: pl.BlockSpec((tm, w), lambda i: (i, 0))
    half = lambda r: [pl.BlockSpec((r, D_FF), lambda i: (0, 0)), pl.BlockSpec((r, D_FF), lambda i: (0, 1))]
    return pl.pallas_call(
        body, name="ffn_fwd_loss", grid=(S // tm,),
        in_specs=_conv_specs(tm) + half(3) + half(1) + [_full((D_FF, D_MODEL)), row(D_MODEL), _full((1, D_MODEL)), row(D_MODEL)],
        out_specs=[row(D_MODEL), _full((1, D_MODEL)), _full((1, D_MODEL))],
        out_shape=[jax.ShapeDtypeStruct((S, D_MODEL), F32), jax.ShapeDtypeStruct((1, D_MODEL), F32),
                   jax.ShapeDtypeStruct((1, D_MODEL), F32)],
        compiler_params=_params(("arbitrary",)),
    )(a, a, a, a, w_conv, w_conv, b_conv, b_conv, w_down, x1, g3, target)


def _ffn_bwd_gate(dx2, a, w_conv, b_conv, w_down, tm):
    S = dx2.shape[0]

    def body(dx_ref, ag_ref, av_ref, hg_ref, hv_ref, wg_ref, wv_ref, bg_ref, bv_ref, wd_ref,
             dc_ref, y_ref, dw_ref, db_ref):
        i = pl.program_id(0)

        @pl.when(i == 0)
        def _():
            dw_ref[...] = jnp.zeros_like(dw_ref)
            db_ref[...] = jnp.zeros_like(db_ref)

        dx = dx_ref[...].astype(BF16)
        shifts = _shift_matrices(tm)
        for cols in _FF_CHUNKS:
            gate, val, gtaps, vtaps = _conv_gate_val((ag_ref, av_ref, hg_ref, hv_ref, wg_ref, wv_ref, bg_ref, bv_ref), shifts, cols, i == 0)
            sg = _sigmoid(gate)
            act = gate * sg
            y_ref[:, cols] = (act * val).astype(BF16)
            dy = lax.dot_general(dx, wd_ref[cols, :], NT, preferred_element_type=F32)
            dgate = dy * val * (sg + act - act * sg)
            dval = dy * act
            for d, taps, out in ((dgate, gtaps, cols), (dval, vtaps, slice(D_FF + cols.start, D_FF + cols.stop))):
                dc_ref[:, out] = d.astype(BF16)
                db_ref[0:1, out] += jnp.sum(d, axis=0, keepdims=True)
                for j in range(3):
                    dw_ref[j:j + 1, out] += jnp.sum(d * taps[j], axis=0, keepdims=True)

    row = lambda w: pl.BlockSpec((tm, w), lambda i: (i, 0))
    half = lambda r: [pl.BlockSpec((r, D_FF), lambda i: (0, 0)), pl.BlockSpec((r, D_FF), lambda i: (0, 1))]
    return pl.pallas_call(
        body, name="ffn_bwd_gate", grid=(S // tm,),
        in_specs=[row(D_MODEL)] + _conv_specs(tm) + half(3) + half(1) + [_full((D_FF, D_MODEL))],
        out_specs=[row(2 * D_FF), row(D_FF), _full((SUBLANES, 2 * D_FF)), _full((1, 2 * D_FF))],
        out_shape=[jax.ShapeDtypeStruct((S, 2 * D_FF), BF16), jax.ShapeDtypeStruct((S, D_FF), BF16),
                   jax.ShapeDtypeStruct((SUBLANES, 2 * D_FF), F32), jax.ShapeDtypeStruct((1, 2 * D_FF), F32)],
        compiler_params=_params(("arbitrary",)),
    )(dx2, a, a, a, a, w_conv, w_conv, b_conv, b_conv, w_down)


def _conv_bwd(dc, w_conv, tm, tn):
    S, C = dc.shape
    step = tm // SUBLANES
    last_blk = S // SUBLANES - 1

    def body(d_ref, nx_ref, w_ref, o_ref):
        last = pl.program_id(0) == pl.num_programs(0) - 1
        row = lax.broadcasted_iota(jnp.int32, (tm, tm), 0)
        col = lax.broadcasted_iota(jnp.int32, (tm, tm), 1)
        row8 = lax.broadcasted_iota(jnp.int32, (SUBLANES, FF_CHUNK), 0)
        ups = [(row + k == col).astype(BF16) for k in (1, 2)]
        for c0 in range(0, tn, FF_CHUNK):
            cols = slice(c0, c0 + FF_CHUNK)
            d = d_ref[:, cols]
            nx = nx_ref[:, cols].astype(F32) * jnp.where(last, 0.0, 1.0)
            out = w_ref[2:3, cols] * d.astype(F32)
            for k, up in zip((1, 2), ups):
                moved = jnp.dot(up, d, preferred_element_type=F32)
                bottom = moved[tm - SUBLANES:tm] + jnp.where(row8 >= SUBLANES - k, pltpu.roll(nx, SUBLANES - k, 0), 0.0)
                out = out + w_ref[2 - k:3 - k, cols] * jnp.concatenate([moved[0:tm - SUBLANES], bottom], axis=0)
            o_ref[:, cols] = out.astype(BF16)

    return pl.pallas_call(
        body, name="conv_bwd", grid=(S // tm, C // tn),
        in_specs=[pl.BlockSpec((tm, tn), lambda i, j: (i, j)),
                  pl.BlockSpec((SUBLANES, tn), lambda i, j: (jnp.minimum((i + 1) * step, last_blk), j)),
                  pl.BlockSpec((3, tn), lambda i, j: (0, j))],
        out_specs=pl.BlockSpec((tm, tn), lambda i, j: (i, j)),
        out_shape=jax.ShapeDtypeStruct((S, C), BF16),
        compiler_params=_params(("parallel", "parallel")),
    )(dc, dc, w_conv)


def _matmul_tn(a, b, name, bm, bn, tk, col_a=0, col_b=0, quarters=None):
    S = a.shape[0]
    gm, gn = quarters if quarters else (1, 1)
    nk = S // tk

    def body(a_ref, b_ref, o_ref):
        @pl.when(pl.program_id(2) == 0)
        def _():
            o_ref[...] = jnp.zeros_like(o_ref)

        o_ref[...] += lax.dot_general(a_ref[...].astype(BF16), b_ref[...].astype(BF16), TN, preferred_element_type=F32)

    if quarters and gn > 1:
        out_spec = pl.BlockSpec((None, bm, bn), lambda i, j, k: (j, i, 0))
        out_shape = jax.ShapeDtypeStruct((gn, gm * bm, bn), F32)
    else:
        out_spec = pl.BlockSpec((bm, bn), lambda i, j, k: (i, j))
        out_shape = jax.ShapeDtypeStruct((gm * bm, gn * bn), F32)
    return pl.pallas_call(
        body, name=name, grid=(gm, gn, nk),
        in_specs=[pl.BlockSpec((tk, bm), lambda i, j, k: (k, col_a * gm + i)),
                  pl.BlockSpec((tk, bn), lambda i, j, k: (k, col_b * gn + j))],
        out_specs=out_spec, out_shape=out_shape,
        compiler_params=_params(("parallel", "parallel", "arbitrary")),
    )(a, b)


def _up_bwd(dact, w_up_q, x1, g2, dx2, tm):
    S = x1.shape[0]
    nq, _, wq = w_up_q.shape

    def body(d_ref, w_ref, x_ref, g_ref, dx2_ref, dx1_ref, dg_ref):
        @pl.when(pl.program_id(0) == 0)
        def _():
            dg_ref[...] = jnp.zeros_like(dg_ref)

        dh = jnp.zeros((tm, D_MODEL), F32)
        for j in range(nq):
            dh = dh + lax.dot_general(d_ref[:, j * wq:(j + 1) * wq], w_ref[j], NT, preferred_element_type=F32)
        dx, dg = _rms_bwd(dh, x_ref[...], g_ref[...])
        dg_ref[...] += dg
        dx1_ref[...] = dx2_ref[...] + dx

    row = lambda w: pl.BlockSpec((tm, w), lambda i: (i, 0))
    return pl.pallas_call(
        body, name="up_bwd", grid=(S // tm,),
        in_specs=[row(nq * wq), pl.BlockSpec((nq, D_MODEL, wq), lambda i: (0, 0, 0), pipeline_mode=pl.Buffered(1)),
                  row(D_MODEL), _full((1, D_MODEL)), row(D_MODEL)],
        out_specs=[row(D_MODEL), _full((1, D_MODEL))],
        out_shape=[jax.ShapeDtypeStruct((S, D_MODEL), F32), jax.ShapeDtypeStruct((1, D_MODEL), F32)],
        compiler_params=_params(("arbitrary",)),
    )(dact, w_up_q, x1, g2, dx2)


def _out_bwd(dx1, w_out, tm):
    S = dx1.shape[0]

    def body(d_ref, w_ref, o_ref):
        o_ref[...] = lax.dot_general(d_ref[...].astype(BF16), w_ref[...], NT, preferred_element_type=F32).astype(BF16)

    return pl.pallas_call(
        body, name="out_bwd", grid=(S // tm,),
        in_specs=[pl.BlockSpec((tm, D_MODEL), lambda i: (i, 0)), _full((D_MODEL, D_MODEL))],
        out_specs=pl.BlockSpec((tm, D_MODEL), lambda i: (i, 0)),
        out_shape=jax.ShapeDtypeStruct((S, D_MODEL), BF16),
        compiler_params=_params(("parallel",)),
    )(dx1, w_out)


def _gate_bwd(z, dcat, w_mask, w_mask_t, ln_row, b_full, seg_avg, head_ind, tm, swap):
    S = z.shape[0]
    nb = tm // SG_BLOCK
    ns = len(swap)

    def body(zu_ref, zv_ref, do_ref, w_ref, wt_ref, ln_ref, b_ref, avg_ref, ind_ref, *rest):
        dzu_ref, dzv_ref, dw_ref, db_ref, dln_ref = rest[ns:ns + 5]
        dvn_s, dbf_s = rest[2 * ns + 5:2 * ns + 7]
        swap_start, swap_finish = _swap_ops(rest[:ns], rest[ns + 5:2 * ns + 5], *rest[2 * ns + 7:])
        i = pl.program_id(0)

        @pl.when(i == 0)
        def _():
            swap_start()
            dw_ref[...] = jnp.zeros_like(dw_ref)
            dln_ref[...] = jnp.zeros_like(dln_ref)
            dbf_s[...] = jnp.zeros_like(dbf_s)

        zu = zu_ref[...].astype(F32)
        zv = zv_ref[...].astype(F32)
        u = _gelu(zu)
        v = _gelu(zv)
        avg = avg_ref[...]
        vhat, rstd = _layer_norm_heads(v, avg)
        ln = ln_ref[...]
        vn = vhat * ln
        for b in range(nb):
            rows = slice(b * SG_BLOCK, (b + 1) * SG_BLOCK)
            vn_b = vn[rows]
            mixed = _gate_mix(vn_b, w_ref, b_ref[...])
            do = do_ref[rows, :].astype(F32)
            dzu_ref[rows, :] = (do * mixed * _gelu_grad(zu[rows])).astype(BF16)
            dmix = do * u[rows]
            dbf_s[...] += dmix
            vn_bf = vn_b.astype(BF16)
            dvn = jnp.zeros((SG_BLOCK, D_HEADS), F32)
            for h in range(N_HEADS):
                dmh = jnp.where(_head_mask(h, SG_BLOCK), dmix, 0.0).astype(BF16)
                dw_ref[h] += lax.dot_general(dmh, vn_bf, NT, preferred_element_type=F32)
                dvn = dvn + jnp.dot(wt_ref[h], dmh, preferred_element_type=F32)
            dvn_s[rows, :] = dvn
        dvn = dvn_s[...]
        dln_ref[...] += jnp.sum(dvn * vhat, axis=0, keepdims=True)
        dvhat = dvn * ln
        dv = rstd * (dvhat - _split_dot(dvhat, avg) - vhat * _split_dot(dvhat * vhat, avg))
        dzv_ref[...] = (dv * _gelu_grad(zv)).astype(BF16)

        @pl.when(i == pl.num_programs(0) - 1)
        def _():
            r = lax.broadcasted_iota(jnp.int32, (SG_BLOCK, SG_BLOCK), 0) // CHUNK
            s = lax.broadcasted_iota(jnp.int32, (SG_BLOCK, SG_BLOCK), 1) // CHUNK
            for h in range(N_HEADS):
                dw_ref[h] = jnp.where(r >= s, dw_ref[h], 0.0)
            db_ref[...] = _split_dot(dbf_s[...], ind_ref[...])
            swap_finish()

    row = lambda col: pl.BlockSpec((tm, D_HEADS), lambda i: (i, col))
    wspec = _full((N_HEADS, SG_BLOCK, SG_BLOCK))
    out = pl.pallas_call(
        body, name="gate_bwd", grid=(S // tm,),
        in_specs=[row(0), row(1), row(0), wspec, wspec, _full((1, D_HEADS)), _full((SG_BLOCK, D_HEADS)),
                  _full((D_HEADS, D_HEADS)), _full((D_HEADS, LANES))] + [_ANY] * ns,
        out_specs=[row(0), row(0), wspec, _full((SG_BLOCK, LANES)), _full((1, D_HEADS))] + [_ANY] * ns,
        out_shape=[jax.ShapeDtypeStruct((S, D_HEADS), BF16), jax.ShapeDtypeStruct((S, D_HEADS), BF16),
                   jax.ShapeDtypeStruct((N_HEADS, SG_BLOCK, SG_BLOCK), F32), jax.ShapeDtypeStruct((SG_BLOCK, LANES), F32),
                   jax.ShapeDtypeStruct((1, D_HEADS), F32)] + _swap_shapes(swap),
        scratch_shapes=[pltpu.VMEM((tm, D_HEADS), F32), pltpu.VMEM((SG_BLOCK, D_HEADS), F32)] + _swap_sems(ns),
        compiler_params=_params(("arbitrary",)),
    )(z, z, dcat, w_mask, w_mask_t, ln_row, b_full, seg_avg, head_ind, *swap)
    return out[:5], out[5:]


def _attn_pack_grad(o, dcat, qa, lse, head_ind, k, tm):
    S = o.shape[0]

    def body(o_ref, do_ref, qa_ref, lse_ref, ind_ref, pl_ref, pt_ref, eye_ref, ds_ref, dst_ref, ls_ref, lst_ref,
             dop_ref, qb_ref, dot_ref, qbt_ref):
        do = do_ref[...]
        delta = _split_dot(o_ref[...].astype(F32) * do.astype(F32), ind_ref[...])
        hi = delta.astype(BF16).astype(F32)
        parts = (hi + pltpu.roll((delta - hi).astype(BF16).astype(F32), N_HEADS, 1)).astype(BF16)
        dop = jnp.dot(do, pl_ref[...], preferred_element_type=F32) - jnp.dot(parts, ds_ref[...], preferred_element_type=F32)
        for g in range(GROUPS):
            dop_ref[g] = dop[:, g * GROUP_PAD:(g + 1) * GROUP_PAD].astype(BF16)
        dot = lax.dot_general(pt_ref[...], do, NT, preferred_element_type=F32)
        dot_ref[...] = (dot - lax.dot_general(dst_ref[...], parts, NT, preferred_element_type=F32)).astype(BF16)
        qa = qa_ref[...]
        stack = jnp.concatenate(_split3(lse_ref[...]), axis=0)
        qb = qa.astype(F32) - lax.dot_general(stack, ls_ref[...], TN, preferred_element_type=F32)
        qbt = lax.dot_general(eye_ref[...], qa, NT, preferred_element_type=F32)
        qbt = qbt - jnp.dot(lst_ref[...], stack, preferred_element_type=F32)
        for g in range(GROUPS):
            qb_ref[g] = qb[:, g * GROUP_PAD:(g + 1) * GROUP_PAD].astype(BF16)
        qbt_ref[...] = qbt.astype(BF16)

    pad = pl.BlockSpec((tm, D_PAD), lambda i: (i, 0))
    padt = pl.BlockSpec((None, D_PAD, tm), lambda i: (i, 0, 0))
    return pl.pallas_call(
        body, name="attn_pack_grad", grid=(S // tm,),
        in_specs=[pl.BlockSpec((tm, D_HEADS), lambda i: (i, 0)), pl.BlockSpec((tm, D_HEADS), lambda i: (i, 1)), pad,
                  pl.BlockSpec((STAT_ROWS, tm), lambda i: (0, i)), _full((D_HEADS, LANES)), _full((D_HEADS, D_PAD)),
                  _full((D_PAD, D_HEADS)), _full((D_PAD, D_PAD)), _full((LANES, D_PAD)), _full((D_PAD, LANES)),
                  _full((3 * STAT_ROWS, D_PAD)), _full((D_PAD, 3 * STAT_ROWS))],
        out_specs=[pl.BlockSpec((GROUPS, tm, GROUP_PAD), lambda i: (0, i, 0))] * 2 + [padt, padt],
        out_shape=[jax.ShapeDtypeStruct((GROUPS, S, GROUP_PAD), BF16)] * 2 + [jax.ShapeDtypeStruct((S // tm, D_PAD, tm), BF16)] * 2,
        compiler_params=_params(("parallel",)),
    )(o, dcat, qa, lse, head_ind, k["place"], k["place_t"], jnp.eye(D_PAD, dtype=BF16), k["d_stat"], k["d_stat"].T,
      k["l_stat"], k["l_stat"].T)


def _attn_bwd(qb, qbt, ka, va, dop, dopt, k, tq, sums16):
    S = ka.shape[0]
    n = S // tq
    ns = len(sums16)

    pairs = [(kb, q) for kb in range(n) for q in range(kb, n)]
    k_of = jnp.asarray([kb for kb, _ in pairs], jnp.int32)
    q_of = jnp.asarray([q for _, q in pairs], jnp.int32)

    def body(k_of_ref, q_of_ref, q_ref, qt_ref, k_ref, v_ref, do_ref, dot_ref, pt_ref, *rest):
        dq_hbm, dcr_hbm, dk_ref, dv_ref, dcc_ref = rest[ns:ns + 5]
        dq_s, dcr_s, dk_s, dv_s, dcc_s = rest[2 * ns + 5:2 * ns + 10]
        s_s, d_s = rest[2 * ns + 10:2 * ns + 12], rest[2 * ns + 12:2 * ns + 14]
        sems = rest[2 * ns + 14]
        scatter_start, scatter_finish = _scatter_ops(rest[:ns], rest[ns + 5:2 * ns + 5], *rest[2 * ns + 15:])
        g = pl.program_id(0)
        ki, qi = k_of_ref[pl.program_id(1)], q_of_ref[pl.program_id(1)]

        @pl.when((g == 0) & (ki == 0) & (qi == 0))
        def _():
            scatter_start()

        @pl.when((ki == 0) & (qi == 0))
        def _():
            dq_s[...] = jnp.zeros_like(dq_s)
            dcr_s[...] = jnp.zeros_like(dcr_s)

        @pl.when(qi == ki)
        def _():
            dk_s[...] = jnp.zeros_like(dk_s)
            dv_s[...] = jnp.zeros_like(dv_s)
            dcc_s[...] = jnp.zeros_like(dcc_s)

        def step(diagonal):
            chunks = [slice(c * KEY_CHUNK, (c + 1) * KEY_CHUNK) for c in range(tq // KEY_CHUNK)]

            def scores(hh, rows, slot):
                sl = slice(hh * HEAD_PAD, (hh + 1) * HEAD_PAD)
                s_s[slot][rows, :] = lax.dot_general(q_ref[rows, sl], k_ref[:, sl], NT, preferred_element_type=F32)
                d_s[slot][rows, :] = lax.dot_general(do_ref[rows, sl], v_ref[:, sl], NT, preferred_element_type=F32)

            for rows in chunks:
                scores(0, rows, 0)
            for hh in range(GROUP_HEADS):
                sl = slice(hh * HEAD_PAD, (hh + 1) * HEAD_PAD)
                slot = hh % 2
                dv, dk = dv_s[sl, :], dk_s[sl, :]
                for rows in chunks:
                    if hh + 1 < GROUP_HEADS:
                        scores(hh + 1, rows, 1 - slot)
                    p = jnp.exp2(s_s[slot][rows, :])
                    if diagonal:
                        row = rows.start + lax.broadcasted_iota(jnp.int32, (KEY_CHUNK, tq), 0)
                        col = lax.broadcasted_iota(jnp.int32, (KEY_CHUNK, tq), 1)
                        p = jnp.where(row >= col, p, 0.0)
                    ds = p * d_s[slot][rows, :]
                    qrows = pl.ds(pl.multiple_of(qi * tq + rows.start, KEY_CHUNK), KEY_CHUNK)
                    dcc_s[hh:hh + 1, :] += jnp.sum(ds, axis=0, keepdims=True)
                    dcr_s[qrows, hh:hh + 1] += jnp.sum(ds, axis=1, keepdims=True)
                    ds = ds.astype(BF16)
                    dv = dv + jnp.dot(dot_ref[sl, rows], p.astype(BF16), preferred_element_type=F32)
                    dk = dk + jnp.dot(qt_ref[sl, rows], ds, preferred_element_type=F32)
                    dq_s[qrows, sl] += jnp.dot(ds, k_ref[:, sl], preferred_element_type=F32)
                dv_s[sl, :] = dv
                dk_s[sl, :] = dk

        @pl.when(qi > ki)
        def _():
            step(False)

        @pl.when(qi == ki)
        def _():
            step(True)

        @pl.when(qi == n - 1)
        def _():
            dk = dk_s[...]
            pt = pt_ref[...]
            dk_ref[...] = lax.dot_general((dk * (1.0 / LOG2E)).astype(BF16), pt, TN, preferred_element_type=F32).astype(BF16)
            dv_ref[...] = lax.dot_general(dv_s[...].astype(BF16), pt, TN, preferred_element_type=F32).astype(BF16)
            dcc_ref[...] = dcc_s[...]

        @pl.when((ki == n - 1) & (qi == n - 1))
        def _():
            copies = [pltpu.make_async_copy(dq_s, dq_hbm.at[g], sems.at[0]), pltpu.make_async_copy(dcr_s, dcr_hbm.at[g], sems.at[1])]
            for cp in copies:
                cp.start()
            for cp in copies:
                cp.wait()

        @pl.when((g == GROUPS - 1) & (ki == n - 1) & (qi == n - 1))
        def _():
            scatter_finish()

    gw = GROUP_HEADS * HEAD_DIM
    qspec = pl.BlockSpec((None, tq, GROUP_PAD), lambda g, i, ks, qs: (g, qs[i], 0))
    qtspec = pl.BlockSpec((None, GROUP_PAD, tq), lambda g, i, ks, qs: (qs[i], g, 0))
    kspec = pl.BlockSpec((tq, GROUP_PAD), lambda g, i, ks, qs: (ks[i], g))
    kout = pl.BlockSpec((tq, gw), lambda g, i, ks, qs: (ks[i], g))
    out = pl.pallas_call(
        body, name="attn_bwd",
        grid_spec=pltpu.PrefetchScalarGridSpec(
            num_scalar_prefetch=2, grid=(GROUPS, len(pairs)),
            in_specs=[qspec, qtspec, kspec, kspec, qspec, qtspec, pl.BlockSpec((GROUP_PAD, gw), lambda g, i, ks, qs: (0, 0))]
            + [_ANY] * ns,
            out_specs=[_ANY, _ANY, kout, kout, pl.BlockSpec((None, SUBLANES, tq), lambda g, i, ks, qs: (g, 0, ks[i]))] + [_ANY] * ns,
            scratch_shapes=[pltpu.VMEM((S, GROUP_PAD), F32), pltpu.VMEM((S, LANES), F32), pltpu.VMEM((GROUP_PAD, tq), F32),
                            pltpu.VMEM((GROUP_PAD, tq), F32), pltpu.VMEM((SUBLANES, tq), F32),
                            pltpu.VMEM((tq, tq), F32), pltpu.VMEM((tq, tq), F32), pltpu.VMEM((tq, tq), F32),
                            pltpu.VMEM((tq, tq), F32), pltpu.SemaphoreType.DMA((2,))]
            + _scatter_sems(ns)),
        out_shape=[jax.ShapeDtypeStruct((GROUPS, S, GROUP_PAD), F32), jax.ShapeDtypeStruct((GROUPS, S, LANES), F32),
                   jax.ShapeDtypeStruct((S, D_HEADS), BF16), jax.ShapeDtypeStruct((S, D_HEADS), BF16),
                   jax.ShapeDtypeStruct((GROUPS, SUBLANES, S), F32)]
        + _scatter_shapes(sums16),
        compiler_params=_params(("arbitrary", "arbitrary")),
    )(k_of, q_of, qb, qbt, ka, va, dop, dopt, k["place_t_group"], *sums16)
    return out[0], out[1], out[2], out[3], out[4], out[5:]


def _attn_unpack(dqp, k, tm):
    S = dqp.shape[1]
    gw = GROUP_HEADS * HEAD_DIM

    def body(dqp_ref, pt_ref, dq_ref):
        for g in range(GROUPS):
            dq_ref[:, g * gw:(g + 1) * gw] = jnp.dot((dqp_ref[g] * SCALE).astype(BF16), pt_ref[...],
                                                     preferred_element_type=F32).astype(BF16)

    return pl.pallas_call(
        body, name="attn_unpack", grid=(S // tm,),
        in_specs=[pl.BlockSpec((GROUPS, tm, GROUP_PAD), lambda i: (0, i, 0)), _full((GROUP_PAD, gw))],
        out_specs=pl.BlockSpec((tm, D_HEADS), lambda i: (i, 0)),
        out_shape=jax.ShapeDtypeStruct((S, D_HEADS), BF16),
        compiler_params=_params(("parallel",)),
    )(dqp, k["place_t_group"])


def _fox_bwd(dc, f, bias_row, tb):
    S = f.shape[0]
    nb = S // tb

    def body(dc_ref, f_ref, b_ref, df_ref, dbias_ref, carry):
        @pl.when(pl.program_id(0) == 0)
        def _():
            carry[...] = jnp.zeros_like(carry)
            dbias_ref[...] = jnp.zeros_like(dbias_ref)

        r = lax.broadcasted_iota(jnp.int32, (tb, tb), 0)
        s = lax.broadcasted_iota(jnp.int32, (tb, tb), 1)
        tri = (s >= r).astype(F32)
        rc = jnp.dot(tri, dc_ref[...], precision=lax.Precision.HIGHEST, preferred_element_type=F32) + carry[0:1, :]
        carry[...] = jnp.broadcast_to(rc[0:1, :], carry.shape)
        lane = lax.broadcasted_iota(jnp.int32, (tb, LANES), 1)
        df = jnp.where(lane < N_HEADS, rc * jax.nn.sigmoid(-(f_ref[...] + b_ref[...])), 0.0)
        df_ref[...] = df.astype(BF16)
        dbias_ref[...] += jnp.sum(df, axis=0, keepdims=True)

    rev = pl.BlockSpec((tb, LANES), lambda i: (nb - 1 - i, 0))
    return pl.pallas_call(
        body, name="fox_bwd", grid=(nb,),
        in_specs=[rev, rev, _full((1, LANES))],
        out_specs=[rev, _full((1, LANES))],
        out_shape=[jax.ShapeDtypeStruct((S, LANES), BF16), jax.ShapeDtypeStruct((1, LANES), F32)],
        scratch_shapes=[pltpu.VMEM((SUBLANES, LANES), F32)],
        compiler_params=_params(("arbitrary",)),
    )(dc, f, bias_row)


_DZ_WIDTHS = (D_HEADS,) * 5 + (LANES,)


def _in_bwd(pieces, w_in, x, g1, dx1, tm, sums16):
    S = x.shape[0]
    ns = len(sums16)

    def body(*refs):
        p_refs, (w_ref, x_ref, g_ref, dx1_ref) = refs[:6], refs[6:10]
        dx_ref, dg_ref = refs[10 + ns:12 + ns]
        scatter_start, scatter_finish = _scatter_ops(refs[10:10 + ns], refs[12 + ns:12 + 2 * ns], *refs[12 + 2 * ns:])

        @pl.when(pl.program_id(0) == 0)
        def _():
            dg_ref[...] = jnp.zeros_like(dg_ref)
            scatter_start()

        dh = jnp.zeros((tm, D_MODEL), F32)
        off = 0
        for p_ref, w in zip(p_refs, _DZ_WIDTHS):
            dh = dh + lax.dot_general(p_ref[...].astype(BF16), w_ref[:, off:off + w], NT, preferred_element_type=F32)
            off += w
        dx, dg = _rms_bwd(dh, x_ref[...], g_ref[...])
        dg_ref[...] += dg
        dx_ref[...] = dx1_ref[...] + dx

        @pl.when(pl.program_id(0) == pl.num_programs(0) - 1)
        def _():
            scatter_finish()

    row = lambda w: pl.BlockSpec((tm, w), lambda i: (i, 0))
    out = pl.pallas_call(
        body, name="in_bwd", grid=(S // tm,),
        in_specs=[row(w) for w in _DZ_WIDTHS] + [_full((D_MODEL, D_IN_PAD)), row(D_MODEL), _full((1, D_MODEL)), row(D_MODEL)]
        + [_ANY] * ns,
        out_specs=[row(D_MODEL), _full((1, D_MODEL))] + [_ANY] * ns,
        out_shape=[jax.ShapeDtypeStruct((S, D_MODEL), F32), jax.ShapeDtypeStruct((1, D_MODEL), F32)] + _scatter_shapes(sums16),
        scratch_shapes=_scatter_sems(ns),
        compiler_params=_params(("arbitrary",)),
    )(*pieces, w_in, x, g1, dx1, *sums16)
    return out[0], out[1], out[2:]


def _dw_in(h1, pieces, tk):
    S = h1.shape[0]

    def body(*refs):
        h_ref, p_refs, o_ref = refs[0], refs[1:7], refs[7]

        @pl.when(pl.program_id(0) == 0)
        def _():
            o_ref[...] = jnp.zeros_like(o_ref)

        off = 0
        for p_ref, w in zip(p_refs, _DZ_WIDTHS):
            o_ref[:, off:off + w] += lax.dot_general(h_ref[...], p_ref[...].astype(BF16), TN, preferred_element_type=F32)
            off += w

    row = lambda w: pl.BlockSpec((tk, w), lambda k: (k, 0))
    return pl.pallas_call(
        body, name="dw_in", grid=(S // tk,),
        in_specs=[row(D_MODEL)] + [row(w) for w in _DZ_WIDTHS],
        out_specs=_full((D_MODEL, D_IN_PAD)),
        out_shape=jax.ShapeDtypeStruct((D_MODEL, D_IN_PAD), F32),
        compiler_params=_params(("arbitrary",)),
    )(h1, *pieces)


def _adamw_math(w, g, m, v):
    m = ADAM_B1 * m + (1.0 - ADAM_B1) * g
    v = ADAM_B2 * v + (1.0 - ADAM_B2) * (g * g)
    m_hat = m / (1.0 - ADAM_B1 ** ADAM_STEP)
    v_hat = v / (1.0 - ADAM_B2 ** ADAM_STEP)
    delta = -ADAM_LR * (m_hat / (jnp.sqrt(v_hat) + ADAM_EPS) + ADAM_WD * w)
    return delta, m, v


def _adamw(name, w, g, m, v):
    R, C = w.shape
    tr = _row_tile(R, 256)

    def body(w_ref, g_ref, m_ref, v_ref, go_ref, d_ref, nm_ref, nv_ref):
        g = g_ref[...]
        d, nm, nv = _adamw_math(w_ref[...], g, m_ref[...], v_ref[...])
        go_ref[...] = g
        d_ref[...] = d
        nm_ref[...] = nm
        nv_ref[...] = nv

    spec = pl.BlockSpec((tr, C), lambda i: (i, 0))
    return pl.pallas_call(
        body, name=name, grid=(R // tr,), in_specs=[spec] * 4, out_specs=[spec] * 4,
        out_shape=[jax.ShapeDtypeStruct((R, C), F32)] * 4,
        compiler_params=_params(("parallel",)),
    )(w, g, m, v)


def _pair_sum(name, grad, theirs, ids):
    q, half, C = theirs.shape
    tr = _row_tile(half, 256)
    nb = half // tr

    def body(ids_ref, a_ref, b_ref, s_ref, sb_ref):
        s = a_ref[...] + b_ref[...]
        s_ref[...] = s
        sb_ref[...] = s.astype(BF16)

    here = pl.BlockSpec((None, tr, C), lambda j, i, ids: (j, i, 0))
    return pl.pallas_call(
        body, name=name,
        grid_spec=pltpu.PrefetchScalarGridSpec(
            num_scalar_prefetch=1, grid=(q, nb),
            in_specs=[pl.BlockSpec((None, tr, C), lambda j, i, ids: (j, ids[1] * nb + i, 0)), here],
            out_specs=[here, here]),
        out_shape=[jax.ShapeDtypeStruct((q, half, C), F32), jax.ShapeDtypeStruct((q, half, C), BF16)],
        compiler_params=_params(("parallel", "parallel")),
    )(ids, grad, theirs)


def _chip_sum(name, sums32, others, ids):
    _, half, C = sums32.shape
    tr = _row_tile(half, 256)
    nb = half // tr

    def body(ids_ref, a_ref, o_ref, s_ref):
        s = a_ref[...]
        for j in range(3):
            s = s + o_ref[j].astype(F32)
        s_ref[...] = s

    return pl.pallas_call(
        body, name=name,
        grid_spec=pltpu.PrefetchScalarGridSpec(
            num_scalar_prefetch=1, grid=(nb,),
            in_specs=[pl.BlockSpec((None, tr, C), lambda i, ids: (ids[0], i, 0)),
                      pl.BlockSpec((3, tr, C), lambda i, ids: (0, i, 0))],
            out_specs=pl.BlockSpec((tr, C), lambda i, ids: (ids[1] * nb + i, 0))),
        out_shape=jax.ShapeDtypeStruct((2 * half, C), F32),
        compiler_params=_params(("parallel",)),
    )(ids, sums32, others)


def _place():
    return lax.axis_index("x"), lax.axis_index("y"), lax.axis_index("c")


def _other_chips(x, y):
    return [(1 - x, y), (x, 1 - y), (1 - x, 1 - y)]


_ANY = pl.BlockSpec(memory_space=pl.ANY)


def _gather_quarters(shards):
    n = len(shards)

    def body(*refs):
        start, hand_on, finish = _gather_ops(refs[:n], refs[n:2 * n], *refs[2 * n:])
        start()
        hand_on()
        finish()

    return pl.pallas_call(
        body, name="gather_weights",
        in_specs=[_ANY] * n, out_specs=[_ANY] * n,
        out_shape=_gather_shapes(shards), scratch_shapes=_gather_sems(n),
    )(*shards)


def _gather_shapes(shards):
    return [jax.ShapeDtypeStruct((4,) + s.shape, s.dtype) for s in shards]


def _gather_sems(n):
    return [pltpu.SemaphoreType.DMA((n, 3))] * 4 + [pltpu.SemaphoreType.DMA((n,))]


def _gather_ops(ins, outs, send_sems, recv_sems, pass_send_sems, pass_recv_sems, own_sems):
    n = len(ins)
    halved = [r.shape[0] % 32 == 0 for r in ins]

    def part(a, quarter, core):
        if not halved[a]:
            return outs[a].at[quarter]
        half = ins[a].shape[0] // 2
        return outs[a].at[quarter, pl.ds(core * half, half), :]

    def ici(a, j, quarter):
        x, y, c = _place()
        px, py = _other_chips(x, y)[j]
        src = ins[a]
        if halved[a]:
            half = src.shape[0] // 2
            src = src.at[pl.ds(c * half, half), :]
        return pltpu.make_async_remote_copy(src_ref=src, dst_ref=part(a, quarter, c), send_sem=send_sems.at[a, j],
                                            recv_sem=recv_sems.at[a, j], device_id=(px, py, c), device_id_type=MESH)

    def passed(a, j, core):
        x, y, c = _place()
        px, py = _other_chips(x, y)[j]
        half = part(a, 2 * px + py, core)
        return pltpu.make_async_remote_copy(src_ref=half, dst_ref=half, send_sem=pass_send_sems.at[a, j],
                                            recv_sem=pass_recv_sems.at[a, j], device_id=(x, y, 1 - c), device_id_type=MESH)

    def own(a):
        x, y, _ = _place()
        return pltpu.make_async_copy(ins[a], outs[a].at[2 * x + y], own_sems.at[a])

    def start():
        x, y, _ = _place()
        for a in range(n):
            for j in range(3):
                ici(a, j, 2 * x + y).start()
            own(a).start()

    def hand_on():
        x, y, c = _place()
        for a in range(n):
            for j, (px, py) in enumerate(_other_chips(x, y)):
                ici(a, j, 2 * px + py).wait_recv()
                if halved[a]:
                    passed(a, j, c).start()

    def finish():
        x, y, c = _place()
        for a in range(n):
            for j in range(3):
                if halved[a]:
                    passed(a, j, 1 - c).wait_recv()
                    passed(a, j, c).wait_send()
                ici(a, j, 2 * x + y).wait_send()
            own(a).wait()

    return start, hand_on, finish


def _swap_halves(grads, name):
    n = len(grads)

    def body(*refs):
        start, finish = _swap_ops(refs[:n], refs[n:2 * n], *refs[2 * n:])
        start()
        finish()

    return pl.pallas_call(
        body, name=name,
        in_specs=[_ANY] * n, out_specs=[_ANY] * n, out_shape=_swap_shapes(grads), scratch_shapes=_swap_sems(n),
    )(*grads)


def _swap_shapes(grads):
    return [jax.ShapeDtypeStruct((4, g.shape[1] // 2, g.shape[2]), F32) for g in grads]


def _swap_sems(n):
    return [pltpu.SemaphoreType.DMA((n,))] * 2


def _swap_ops(ins, outs, send_sems, recv_sems):
    def copy(a):
        x, y, c = _place()
        half = ins[a].shape[1] // 2
        return pltpu.make_async_remote_copy(src_ref=ins[a].at[:, pl.ds((1 - c) * half, half), :], dst_ref=outs[a],
                                            send_sem=send_sems.at[a], recv_sem=recv_sems.at[a],
                                            device_id=(x, y, 1 - c), device_id_type=MESH)

    def start():
        for a in range(len(ins)):
            copy(a).start()

    def finish():
        for a in range(len(ins)):
            copy(a).wait()

    return start, finish


def _scatter_shapes(sums16):
    return [jax.ShapeDtypeStruct((3,) + s.shape[1:], BF16) for s in sums16]


def _scatter_sems(n):
    return [pltpu.SemaphoreType.DMA((n, 3))] * 2


def _scatter_ops(ins, outs, send_sems, recv_sems):
    n = len(ins)

    def copy(a, j):
        x, y, c = _place()
        px, py = _other_chips(x, y)[j]
        return pltpu.make_async_remote_copy(src_ref=ins[a].at[2 * px + py], dst_ref=outs[a].at[j], send_sem=send_sems.at[a, j],
                                            recv_sem=recv_sems.at[a, j], device_id=(px, py, c), device_id_type=MESH)

    def start():
        for a in range(n):
            for j in range(3):
                copy(a, j).start()

    def finish():
        for a in range(n):
            for j in range(3):
                copy(a, j).wait()

    return start, finish


def _join_halves(fulls):
    n = len(fulls)

    def body(*refs):
        ins, outs = refs[:n], refs[n:2 * n]
        send_sems, recv_sems = refs[2 * n:]
        x, y, c = _place()
        started = []
        for a in range(n):
            half = ins[a].shape[0] // 2
            rows = pl.ds(c * half, half)
            cp = pltpu.make_async_remote_copy(src_ref=ins[a].at[rows, :], dst_ref=outs[a].at[rows, :], send_sem=send_sems.at[a],
                                              recv_sem=recv_sems.at[a], device_id=(x, y, 1 - c), device_id_type=MESH)
            cp.start()
            started.append(cp)
        for cp in started:
            cp.wait()

    return pl.pallas_call(
        body, name="join_halves",
        in_specs=[_ANY] * n, out_specs=[_ANY] * n,
        out_shape=[jax.ShapeDtypeStruct(f.shape, F32) for f in fulls],
        input_output_aliases={a: a for a in range(n)},
        scratch_shapes=[pltpu.SemaphoreType.DMA((n,)), pltpu.SemaphoreType.DMA((n,))],
    )(*fulls)


def _small_allreduce(g):
    R = g.shape[0]
    half = R // 2

    def body(g_ref, out_ref, other_s, chip_s, parts_s, send_sems, recv_sems):
        x, y, c = _place()
        mine = 2 * x + y
        rows = pl.ds(pl.multiple_of(c * half, SUBLANES), half)

        def to_other_core(src, dst, k):
            return pltpu.make_async_remote_copy(src_ref=src, dst_ref=dst, send_sem=send_sems.at[k], recv_sem=recv_sems.at[k],
                                                device_id=(x, y, 1 - c), device_id_type=MESH)

        swap = to_other_core(g_ref, other_s, 0)
        swap.start()
        swap.wait()
        chip_s[...] = g_ref[...] + other_s[...]
        parts_s[mine] = chip_s[rows, :]
        sends = []
        for j, (px, py) in enumerate(_other_chips(x, y)):
            cp = pltpu.make_async_remote_copy(src_ref=chip_s.at[rows, :], dst_ref=parts_s.at[mine], send_sem=send_sems.at[1 + j],
                                              recv_sem=recv_sems.at[1 + j], device_id=(px, py, c), device_id_type=MESH)
            cp.start()
            sends.append(cp)
        for cp in sends:
            cp.wait()
        out_ref[rows, :] = (parts_s[0] + parts_s[1]) + (parts_s[2] + parts_s[3])
        join = to_other_core(out_ref.at[rows, :], out_ref.at[rows, :], 4)
        join.start()
        join.wait()

    vm = pl.BlockSpec(memory_space=pltpu.VMEM)
    return pl.pallas_call(
        body, name="small_allreduce",
        in_specs=[vm], out_specs=vm, out_shape=jax.ShapeDtypeStruct((R, LANES), F32),
        scratch_shapes=[pltpu.VMEM((R, LANES), F32), pltpu.VMEM((R, LANES), F32), pltpu.VMEM((4, half, LANES), F32),
                        pltpu.SemaphoreType.DMA((5,)), pltpu.SemaphoreType.DMA((5,))],
        compiler_params=pltpu.CompilerParams(vmem_limit_bytes=VMEM_LIMIT),
    )(g)


def _adamw_small(ws, gs, ms, vs):
    n = len(ws)

    def body(*refs):
        for k in range(n):
            w_ref, g_ref, m_ref, v_ref = (refs[j * n + k] for j in range(4))
            d, nm, nv = _adamw_math(w_ref[...], g_ref[...], m_ref[...], v_ref[...])
            refs[4 * n + k][...] = d
            refs[5 * n + k][...] = nm
            refs[6 * n + k][...] = nv

    vm = pl.BlockSpec(memory_space=pltpu.VMEM)
    out = pl.pallas_call(
        body, name="adamw_small",
        in_specs=[vm] * (4 * n), out_specs=[vm] * (3 * n),
        out_shape=[jax.ShapeDtypeStruct(w.shape, F32) for w in ws] * 3,
        compiler_params=pltpu.CompilerParams(vmem_limit_bytes=VMEM_LIMIT),
    )(*ws, *gs, *ms, *vs)
    return out[:n], out[n:2 * n], out[2 * n:]


_SMALL = (("norm_mix_g", D_MODEL), ("f_bias", N_HEADS), ("sg_ln_g", D_HEADS), ("sg_w", N_HEADS * SG_BLOCK * SG_BLOCK),
          ("sg_b", N_HEADS * SG_BLOCK), ("norm_ffn_g", D_MODEL), ("w_conv", 3 * 2 * D_FF), ("b_conv", 2 * D_FF),
          ("norm_final_g", D_MODEL))
_PACKED = _SMALL + (("sq_err", D_MODEL),)


def _pack_small(parts):
    rows = []
    for name, size in _PACKED:
        flat = parts[name].reshape(-1).astype(F32)
        pad = (-size) % (SUBLANES * LANES)
        rows.append(jnp.pad(flat, (0, pad)).reshape(-1, LANES))
    packed = jnp.concatenate(rows, axis=0)
    return jnp.pad(packed, ((0, (-packed.shape[0]) % (2 * SUBLANES)), (0, 0)))


def _unpack_small(packed, shapes):
    out, r = {}, 0
    for name, size in _PACKED:
        nrows = (size + SUBLANES * LANES - 1) // (SUBLANES * LANES) * SUBLANES
        out[name] = packed[r:r + nrows].reshape(-1)[:size].reshape(shapes[name])
        r += nrows
    return out


def _local_step(x, target, g1, w_in, f_bias, sg_ln_g, sg_w, sg_b, g2, b_conv, g3, late_shards, ids):
    S = x.shape[0]
    tm = _row_tile(S, 512)
    tms = _row_tile(S, 256)
    tq = tm

    lane = jnp.arange(D_HEADS)
    seg_avg = jnp.where(lane[:, None] // HEAD_DIM == lane[None, :] // HEAD_DIM, 1.0 / HEAD_DIM, 0.0).astype(BF16)
    head_ind = (lane[:, None] // HEAD_DIM == jnp.arange(LANES)[None, :]).astype(BF16)
    pos_chunk = jnp.arange(SG_BLOCK) // CHUNK
    w_mask32 = jnp.where(pos_chunk[:, None] >= pos_chunk[None, :], sg_w, 0.0)
    w_mask = w_mask32.astype(BF16)
    w_mask_t = jnp.swapaxes(w_mask32, 1, 2).astype(BF16)
    ln_row = sg_ln_g.reshape(1, D_HEADS)
    b_full = jnp.repeat(sg_b.T, HEAD_DIM, axis=1)
    bias_row = jnp.pad(f_bias.reshape(1, N_HEADS), ((0, 0), (0, LANES - N_HEADS)))
    b_conv_row = b_conv.reshape(1, 2 * D_FF)

    z, f, h1 = _in_proj(x, g1, w_in, tm)
    c = _fox_prep(f, bias_row, _row_tile(S, 256))
    consts = _attn_consts()
    qa, ka, va, vat = _attn_pack(z, c, consts, tm)
    out_b, lse, gathered = _attn_fwd(qa, ka, vat, consts["place_t"], tq, late_shards)
    g_out, w_up_q, g_down, g_conv = gathered
    w_out = g_out.reshape(D_MODEL, D_MODEL)
    w_down = g_down.reshape(D_FF, D_MODEL)
    w_conv = jnp.concatenate([g_conv[q] for q in range(4)], axis=1)
    out_a = _gate_fwd(z, w_mask, ln_row, b_full, seg_avg, tm)
    x1, h2 = _mix_out(x, out_a, out_b, w_out, g2, tm)
    a = _up_proj(h2, w_up_q, tm)
    dx2, sq_err, dg3 = _ffn_fwd_loss(a, w_conv, b_conv_row, w_down, x1, g3, target, tms)

    dconv, y, dw_conv8, db_conv = _ffn_bwd_gate(dx2, a, w_conv, b_conv_row, w_down, tms)
    dact = _conv_bwd(dconv, w_conv, tms, D_FF)
    dw_down = _matmul_tn(y, dx2, "dw_down", D_FF // 2, D_MODEL, tm, quarters=(2, 1))
    dx1, dg2 = _up_bwd(dact, w_up_q, x1, g2, dx2, tm)
    dw_up_q = _matmul_tn(h2, dact, "dw_up", D_MODEL, 2 * D_FF // 4, tm, quarters=(1, 4))
    dcat = _out_bwd(dx1, w_out, tm)
    dw_out_a = _matmul_tn(out_a, dx1, "dw_out_a", D_HEADS, D_MODEL, tm)
    dw_out_b = _matmul_tn(out_b, dx1, "dw_out_b", D_HEADS, D_MODEL, tm)
    early = {"w_down": dw_down.reshape(4, D_FF // 4, D_MODEL), "w_up": dw_up_q,
             "w_out": jnp.concatenate([dw_out_a, dw_out_b], axis=0).reshape(4, D_MODEL // 4, D_MODEL)}
    (dzu, dzv, dsg_w, dsg_b_t, dln), theirs = _gate_bwd(z, dcat, w_mask, w_mask_t, ln_row, b_full, seg_avg, head_ind, tm,
                                                        list(early.values()))
    early_sums = _chip_sums(early, theirs, ids)
    dop, qb, dopt, qbt = _attn_pack_grad(out_b, dcat, qa, lse, head_ind, consts, tm)
    dqp, dc_rows, dk, dv, dc_cols, landed = _attn_bwd(qb, qbt, ka, va, dop, dopt, consts, tq,
                                                      [s16 for _, s16 in early_sums.values()])
    early_parts = {k: (s32, got) for (k, (s32, _)), got in zip(early_sums.items(), landed)}
    dq = _attn_unpack(dqp, consts, tm)
    dc_rows = jnp.concatenate([dc_rows[g][:, :GROUP_HEADS] for g in range(GROUPS)], axis=1)
    dc_cols = jnp.concatenate([dc_cols[g][:GROUP_HEADS] for g in range(GROUPS)], axis=0).T
    dc = jnp.pad(dc_rows - dc_cols, ((0, 0), (0, LANES - N_HEADS)))
    df, dbias = _fox_bwd(dc, f, bias_row, _row_tile(S, 256))
    pieces = (dzu, dzv, dq, dk, dv, df)
    dw_in = _dw_in(h1, pieces, tm)[:, :D_IN].reshape(D_MODEL, 4, D_IN // 4).transpose(1, 0, 2)
    (w_in_sum, w_in_sum16), = _chip_sums({"w_in": dw_in}, _swap_halves([dw_in], "swap_halves"), ids).values()
    dx, dg1, (w_in_landed,) = _in_bwd(pieces, w_in, x, g1, dx1, tm, [w_in_sum16])

    grads = {
        "norm_mix_g": dg1, "f_bias": dbias[:, :N_HEADS], "sg_ln_g": dln, "sg_w": dsg_w, "sg_b": dsg_b_t[:, :N_HEADS].T,
        "norm_ffn_g": dg2, "w_conv": dw_conv8[:3], "b_conv": db_conv, "norm_final_g": dg3,
    }
    return sq_err, dx, grads, {**early_parts, "w_in": (w_in_sum, w_in_landed)}


def _chip_sums(grads_q, theirs, ids):
    return {k: _pair_sum("pair_sum_" + k, g, t, ids) for (k, g), t in zip(grads_q.items(), theirs)}


def _finish_reduction(parts, ids):
    names = list(parts)
    fulls = [_chip_sum("chip_sum_" + k, s32, got, ids) for k, (s32, got) in parts.items()]
    return dict(zip(names, _join_halves(fulls)))


def kernel(x, norm_mix_g, w_in, f_bias, sg_ln_g, sg_w, sg_b, w_out, norm_ffn_g, w_up, w_conv, b_conv, w_down, norm_final_g, loss_target, m_norm_mix_g, m_w_in, m_f_bias, m_sg_ln_g, m_sg_w, m_sg_b, m_w_out, m_norm_ffn_g, m_w_up, m_w_conv, m_b_conv, m_w_down, m_norm_final_g, v_norm_mix_g, v_w_in, v_f_bias, v_sg_ln_g, v_sg_w, v_sg_b, v_w_out, v_norm_ffn_g, v_w_up, v_w_conv, v_b_conv, v_w_down, v_norm_final_g):
    args = dict(locals())
    quarter = 2 * lax.axis_index("x") + lax.axis_index("y")
    ids = jnp.stack([quarter, lax.axis_index("c")]).astype(jnp.int32)
    wq_conv = w_conv.shape[-1]

    g_in = _gather_quarters([w_in[0].astype(BF16)])[0]
    w_in_full = jnp.pad(jnp.concatenate([g_in[q] for q in range(4)], axis=1), ((0, 0), (0, D_IN_PAD - D_IN)))
    late_shards = [w_out[0].astype(BF16), w_up[0].astype(BF16), w_down[0].astype(BF16), w_conv[0]]

    sq_err, dx, grads, parts = _local_step(
        x[0], loss_target[0], norm_mix_g, w_in_full, f_bias[0], sg_ln_g[0], sg_w[0], sg_b[0], norm_ffn_g, b_conv[0],
        norm_final_g.reshape(1, D_MODEL), late_shards, ids)
    big = _finish_reduction(parts, ids)

    out = {"grad_x": dx[None]}
    for k in ("w_in", "w_out", "w_up", "w_down"):
        g, d, nm, nv = _adamw("adamw_" + k, args[k][0], big[k], args["m_" + k][0], args["v_" + k][0])
        out["grad_" + k], out["delta_" + k], out["new_m_" + k], out["new_v_" + k] = g[None], d[None], nm[None], nv[None]

    small_names = [n for n, _ in _SMALL]
    shapes = {n: (3, 4 * wq_conv) if n == "w_conv" else args[n].shape for n in small_names}
    shapes["sq_err"] = sq_err.shape
    g_small = _unpack_small(_small_allreduce(_pack_small({**{n: grads[n] for n in small_names}, "sq_err": sq_err})), shapes)
    out["loss"] = 0.5 * jnp.sum(g_small.pop("sq_err")) / D_MODEL
    g_small["w_conv"] = lax.dynamic_slice(g_small["w_conv"], (0, quarter * wq_conv), (3, wq_conv))[None]
    flat2d = lambda t: t.reshape(-1, t.shape[-1])
    updated = _adamw_small(*[[flat2d(src[p + n]) for n in small_names] for src, p in
                             ((args, ""), (g_small, ""), (args, "m_"), (args, "v_"))])
    for n, g in g_small.items():
        out["grad_" + n] = g
    for prefix, arrs in zip(("delta_", "new_m_", "new_v_"), updated):
        for n, t in zip(small_names, arrs):
            out[prefix + n] = t.reshape(args[n].shape)

    weights = ["norm_mix_g", "w_in", "f_bias", "sg_ln_g", "sg_w", "sg_b", "w_out", "norm_ffn_g", "w_up", "w_conv", "b_conv",
               "w_down", "norm_final_g"]
    return (out["loss"], out["grad_x"], *[out[p + n] for p in ("grad_", "delta_", "new_m_", "new_v_") for n in weights])
```

```python
import functools
import math

import jax
import jax.numpy as jnp
from jax import lax
from jax.experimental import pallas as pl
from jax.experimental.pallas import tpu as pltpu

F32 = jnp.float32
BF16 = jnp.bfloat16
MESH = pl.DeviceIdType.MESH

D_MODEL = 1024
N_HEADS = 8
HEAD_DIM = 64
D_HEADS = N_HEADS * HEAD_DIM
SG_BLOCK = 128
CHUNK = 64
D_FF = 2816
D_IN = 2 * D_HEADS + 3 * D_HEADS + N_HEADS
LANES = 128
SUBLANES = 8
D_IN_PAD = 5 * D_HEADS + LANES
EPS = 1e-6
SCALE = HEAD_DIM ** -0.5
NEG = -1e30
LOG2E = 1.4426950408889634
HEAD_PAD = LANES
D_PAD = N_HEADS * HEAD_PAD
Q_STAT = HEAD_DIM
K_STAT = HEAD_DIM + 3
L_STAT = HEAD_DIM + 6
GROUPS = 2
GROUP_HEADS = N_HEADS // GROUPS
GROUP_PAD = GROUP_HEADS * HEAD_PAD
KEY_CHUNK = 256
STAT_ROWS = 16
FF_CHUNK = 256

ADAM_LR = 0.001
ADAM_B1 = 0.9
ADAM_B2 = 0.999
ADAM_EPS = 1e-08
ADAM_WD = 0.01
ADAM_STEP = 10

VMEM_LIMIT = 56 * 1024 * 1024

NT = (((1,), (1,)), ((), ()))
TN = (((0,), (0,)), ((), ()))


def _params(sem):
    return pltpu.CompilerParams(dimension_semantics=sem, vmem_limit_bytes=VMEM_LIMIT)


def _full(shape):
    nd = len(shape)
    return pl.BlockSpec(shape, lambda *_: (0,) * nd)


def _row_tile(rows, target):
    best = None
    for t in range(SUBLANES, min(rows, target) + 1, SUBLANES):
        if rows % t == 0:
            best = t
    assert best is not None, rows
    return best


def _sigmoid(x):
    return 0.5 * jnp.tanh(0.5 * x) + 0.5


def _gelu(z):
    return 0.5 * z * (1.0 + lax.erf(z * (2.0 ** -0.5)))


def _gelu_grad(z):
    cdf = 0.5 * (1.0 + lax.erf(z * (2.0 ** -0.5)))
    pdf = jnp.exp(-0.5 * z * z) * (1.0 / math.sqrt(2.0 * math.pi))
    return cdf + z * pdf


def _split_dot(x, m):
    hi = x.astype(BF16)
    lo = (x - hi.astype(F32)).astype(BF16)
    return jnp.dot(hi, m, preferred_element_type=F32) + jnp.dot(lo, m, preferred_element_type=F32)


def _head_mask(h, rows):
    lane = lax.broadcasted_iota(jnp.int32, (rows, D_HEADS), 1)
    return (lane >= h * HEAD_DIM) & (lane < (h + 1) * HEAD_DIM)


def _rms_bwd(dh, x, g):
    r = lax.rsqrt(jnp.mean(x * x, axis=-1, keepdims=True) + EPS)
    xhat = x * r
    dg = jnp.sum(dh * xhat, axis=0, keepdims=True)
    dxhat = dh * g
    dx = r * (dxhat - xhat * jnp.mean(dxhat * xhat, axis=-1, keepdims=True))
    return dx, dg


def _in_proj(x, g1, w_in, tm):
    S = x.shape[0]
    nz = D_IN_PAD - LANES

    def body(x_ref, g_ref, w_ref, z_ref, f_ref, h_ref):
        xf = x_ref[...]
        r = lax.rsqrt(jnp.mean(xf * xf, axis=-1, keepdims=True) + EPS)
        h = (xf * r * g_ref[...]).astype(BF16)
        h_ref[...] = h
        zz = jnp.dot(h, w_ref[...], preferred_element_type=F32)
        z_ref[...] = zz[:, :nz].astype(BF16)
        f_ref[...] = zz[:, nz:]

    return pl.pallas_call(
        body, name="in_proj", grid=(S // tm,),
        in_specs=[pl.BlockSpec((tm, D_MODEL), lambda i: (i, 0)), _full((1, D_MODEL)), _full((D_MODEL, D_IN_PAD))],
        out_specs=[pl.BlockSpec((tm, nz), lambda i: (i, 0)), pl.BlockSpec((tm, LANES), lambda i: (i, 0)),
                   pl.BlockSpec((tm, D_MODEL), lambda i: (i, 0))],
        out_shape=[jax.ShapeDtypeStruct((S, nz), BF16), jax.ShapeDtypeStruct((S, LANES), F32),
                   jax.ShapeDtypeStruct((S, D_MODEL), BF16)],
        compiler_params=_params(("parallel",)),
    )(x, g1, w_in)


def _fox_prep(f, bias_row, tb):
    S = f.shape[0]

    def body(f_ref, b_ref, c_ref, carry):
        @pl.when(pl.program_id(0) == 0)
        def _():
            carry[...] = jnp.zeros_like(carry)

        xv = f_ref[...] + b_ref[...]
        lf = jnp.minimum(xv, 0.0) - jnp.log(1.0 + jnp.exp(-jnp.abs(xv)))
        r = lax.broadcasted_iota(jnp.int32, (tb, tb), 0)
        s = lax.broadcasted_iota(jnp.int32, (tb, tb), 1)
        tri = (r >= s).astype(F32)
        cs = jnp.dot(tri, lf, precision=lax.Precision.HIGHEST, preferred_element_type=F32) + carry[0:1, :]
        c_ref[...] = cs
        carry[...] = jnp.broadcast_to(cs[tb - 1:tb, :], carry.shape)

    return pl.pallas_call(
        body, name="fox_prep", grid=(S // tb,),
        in_specs=[pl.BlockSpec((tb, LANES), lambda i: (i, 0)), _full((1, LANES))],
        out_specs=pl.BlockSpec((tb, LANES), lambda i: (i, 0)),
        out_shape=jax.ShapeDtypeStruct((S, LANES), F32),
        scratch_shapes=[pltpu.VMEM((SUBLANES, LANES), F32)],
        compiler_params=_params(("arbitrary",)),
    )(f, bias_row)


def _attn_consts():
    col = jnp.arange(D_PAD)
    row = jnp.arange(D_HEADS)
    head = jnp.arange(LANES)
    place = (row[:, None] // HEAD_DIM == col[None, :] // HEAD_PAD) & (row[:, None] % HEAD_DIM == col[None, :] % HEAD_PAD)

    def stat(offset):
        return ((head[:, None] < N_HEADS) & (col[None, :] == head[:, None] * HEAD_PAD + offset)).astype(BF16)

    def stat3(base):
        part, h = head // N_HEADS, head % N_HEADS
        return ((part[:, None] < 3) & (col[None, :] == h[:, None] * HEAD_PAD + base + part[:, None])).astype(BF16)

    def ones(offsets):
        return sum((col % HEAD_PAD == o) for o in offsets).astype(F32).reshape(1, D_PAD)

    place = place.astype(BF16)
    return {
        "place": place, "place_t": place.T, "place_t_group": place.T[:GROUP_PAD, :GROUP_HEADS * HEAD_DIM],
        "q_stat": stat3(Q_STAT), "k_stat": stat3(K_STAT),
        "d_stat": stat3(Q_STAT) * (head[:, None] < 2 * N_HEADS).astype(BF16),
        "l_stat": jnp.concatenate([stat(L_STAT + j)[:STAT_ROWS] for j in range(3)], axis=0),
        "q_ones": ones(range(K_STAT, K_STAT + 3)), "k_ones": ones(list(range(Q_STAT, Q_STAT + 3)) + list(range(L_STAT, L_STAT + 3))),
        "v_ones": ones(range(Q_STAT, Q_STAT + 2)),
    }


def _split3(x):
    hi = x.astype(BF16)
    r = x - hi.astype(F32)
    mid = r.astype(BF16)
    return hi, mid, (r - mid.astype(F32)).astype(BF16)


def _attn_pack(z, c, k, tm):
    S = z.shape[0]

    def body(q_ref, k_ref, v_ref, c_ref, pl_ref, pt_ref, qs_ref, ks_ref, qo_ref, ko_ref, vo_ref, voc_ref,
             qa_ref, ka_ref, va_ref, vt_ref):
        place = pl_ref[...]
        q = (q_ref[...].astype(F32) * (SCALE * LOG2E)).astype(BF16)
        qa = jnp.dot(q, place, preferred_element_type=F32) + qo_ref[...]
        ka = jnp.dot(k_ref[...], place, preferred_element_type=F32) + ko_ref[...]
        lane = lax.broadcasted_iota(jnp.int32, (tm, LANES), 1)
        hi, mid, lo = _split3(jnp.where(lane < N_HEADS, c_ref[...] * LOG2E, 0.0))
        parts = hi.astype(F32) + pltpu.roll(mid.astype(F32), N_HEADS, 1) + pltpu.roll(lo.astype(F32), 2 * N_HEADS, 1)
        parts = parts.astype(BF16)
        qa = qa + jnp.dot(parts, qs_ref[...], preferred_element_type=F32)
        ka = ka - jnp.dot(parts, ks_ref[...], preferred_element_type=F32)
        qa_ref[...] = qa.astype(BF16)
        ka_ref[...] = ka.astype(BF16)
        v = v_ref[...]
        va_ref[...] = (jnp.dot(v, place, preferred_element_type=F32) + vo_ref[...]).astype(BF16)
        vt_ref[...] = (lax.dot_general(pt_ref[...], v, NT, preferred_element_type=F32) + voc_ref[...]).astype(BF16)

    blk = lambda col: pl.BlockSpec((tm, D_HEADS), lambda i: (i, col))
    out = pl.BlockSpec((tm, D_PAD), lambda i: (i, 0))
    pad = jax.ShapeDtypeStruct((S, D_PAD), BF16)
    return pl.pallas_call(
        body, name="attn_pack", grid=(S // tm,),
        in_specs=[blk(2), blk(3), blk(4), pl.BlockSpec((tm, LANES), lambda i: (i, 0)), _full((D_HEADS, D_PAD)), _full((D_PAD, D_HEADS)),
                  _full((LANES, D_PAD)), _full((LANES, D_PAD)), _full((1, D_PAD)), _full((1, D_PAD)), _full((1, D_PAD)),
                  _full((D_PAD, 1))],
        out_specs=[out, out, out, pl.BlockSpec((None, D_PAD, tm), lambda i: (i, 0, 0))],
        out_shape=[pad, pad, pad, jax.ShapeDtypeStruct((S // tm, D_PAD, tm), BF16)],
        compiler_params=_params(("parallel",)),
    )(z, z, z, c, k["place"], k["place_t"], k["q_stat"], k["k_stat"], k["q_ones"], k["k_ones"], k["v_ones"], k["v_ones"].T)


def _attn_fwd(qa, ka, vat, place_t, tq, shards):
    S = qa.shape[0]
    n = S // tq
    ns = len(shards)
    hand_on_at = (2 * n) // 3

    pairs = [(q, k) for q in range(n) for k in range(q + 1)]
    q_of = jnp.asarray([q for q, _ in pairs], jnp.int32)
    k_of = jnp.asarray([k for _, k in pairs], jnp.int32)

    def body(q_of_ref, k_of_ref, q_ref, k_ref, vt_ref, pt_ref, *rest):
        o_ref, lse_ref = rest[ns:ns + 2]
        m_s, acc_s, ot_s = rest[2 * ns + 2:2 * ns + 5]
        s_s = rest[2 * ns + 5:2 * ns + 7]
        start, hand_on, finish = _gather_ops(rest[:ns], rest[ns + 2:2 * ns + 2], *rest[2 * ns + 7:])
        qi, ki = q_of_ref[pl.program_id(0)], k_of_ref[pl.program_id(0)]

        @pl.when((qi == 0) & (ki == 0))
        def _():
            start()

        @pl.when((qi == hand_on_at) & (ki == 0))
        def _():
            hand_on()

        @pl.when(ki == 0)
        def _():
            m_s[...] = jnp.full_like(m_s, NEG)
            acc_s[...] = jnp.zeros_like(acc_s)

        def step(diagonal):
            chunks = [slice(c * KEY_CHUNK, (c + 1) * KEY_CHUNK) for c in range(tq // KEY_CHUNK)]

            def scores(h, rows, slot):
                sl = slice(h * HEAD_PAD, (h + 1) * HEAD_PAD)
                st = lax.dot_general(k_ref[rows, sl], q_ref[:, sl], NT, preferred_element_type=F32)
                if diagonal:
                    key = rows.start + lax.broadcasted_iota(jnp.int32, (KEY_CHUNK, tq), 0)
                    query = lax.broadcasted_iota(jnp.int32, (KEY_CHUNK, tq), 1)
                    st = jnp.where(query >= key, st, NEG)
                s_s[slot][rows, :] = st
                return jnp.max(st, axis=0, keepdims=True)

            m_cur = functools.reduce(jnp.maximum, [scores(0, rows, 0) for rows in chunks])
            for h in range(N_HEADS):
                sl = slice(h * HEAD_PAD, (h + 1) * HEAD_PAD)
                slot = h % 2
                m_prev = m_s[h][0:1, :]
                m_new = jnp.maximum(m_prev, m_cur)
                acc = jnp.exp2(m_prev - m_new) * acc_s[h]
                m_next = []
                for rows in chunks:
                    if h + 1 < N_HEADS:
                        m_next.append(scores(h + 1, rows, 1 - slot))
                    pt = jnp.exp2(s_s[slot][rows, :] - m_new).astype(BF16)
                    acc = acc + jnp.dot(vt_ref[sl, rows], pt, preferred_element_type=F32)
                acc_s[h] = acc
                m_s[h] = jnp.broadcast_to(m_new, (SUBLANES, tq))
                if m_next:
                    m_cur = functools.reduce(jnp.maximum, m_next)

        @pl.when(ki < qi)
        def _():
            step(False)

        @pl.when(ki == qi)
        def _():
            step(True)
            lse_ref[...] = jnp.zeros_like(lse_ref)
            for h in range(N_HEADS):
                acc = acc_s[h]
                denom = acc[Q_STAT:Q_STAT + 1, :]
                ot_s[h * HEAD_PAD:(h + 1) * HEAD_PAD, :] = (acc / denom).astype(BF16)
                lse_ref[h:h + 1, :] = m_s[h][0:1, :] + jnp.log(denom) * LOG2E
            o_ref[...] = lax.dot_general(ot_s[...], pt_ref[...], TN, preferred_element_type=F32).astype(BF16)

        @pl.when((qi == n - 1) & (ki == n - 1))
        def _():
            finish()

    out = pl.pallas_call(
        body, name="attn_fwd",
        grid_spec=pltpu.PrefetchScalarGridSpec(
            num_scalar_prefetch=2, grid=(len(pairs),),
            in_specs=[pl.BlockSpec((tq, D_PAD), lambda i, qs, ks: (qs[i], 0)),
                      pl.BlockSpec((tq, D_PAD), lambda i, qs, ks: (ks[i], 0)),
                      pl.BlockSpec((None, D_PAD, tq), lambda i, qs, ks: (ks[i], 0, 0)),
                      pl.BlockSpec((D_PAD, D_HEADS), lambda i, qs, ks: (0, 0))]
            + [_ANY] * ns,
            out_specs=[pl.BlockSpec((tq, D_HEADS), lambda i, qs, ks: (qs[i], 0)),
                       pl.BlockSpec((STAT_ROWS, tq), lambda i, qs, ks: (0, qs[i]))] + [_ANY] * ns,
            scratch_shapes=[pltpu.VMEM((N_HEADS, SUBLANES, tq), F32), pltpu.VMEM((N_HEADS, HEAD_PAD, tq), F32),
                            pltpu.VMEM((D_PAD, tq), BF16), pltpu.VMEM((tq, tq), F32), pltpu.VMEM((tq, tq), F32)] + _gather_sems(ns)),
        out_shape=[jax.ShapeDtypeStruct((S, D_HEADS), BF16), jax.ShapeDtypeStruct((STAT_ROWS, S), F32)] + _gather_shapes(shards),
        compiler_params=_params(("arbitrary",)),
    )(q_of, k_of, qa, ka, vat, place_t, *shards)
    return out[0], out[1], out[2:]


def _layer_norm_heads(v, seg_avg):
    mu = _split_dot(v, seg_avg)
    d = v - mu
    var = _split_dot(d * d, seg_avg)
    rstd = lax.rsqrt(var + EPS)
    return d * rstd, rstd


def _gate_mix(vn_blk, w_ref, bias):
    acc = bias
    for h in range(N_HEADS):
        vh = jnp.where(_head_mask(h, SG_BLOCK), vn_blk, 0.0).astype(BF16)
        acc = acc + jnp.dot(w_ref[h], vh, preferred_element_type=F32)
    return acc


def _gate_fwd(z, w_mask, ln_row, b_full, seg_avg, tm):
    S = z.shape[0]

    def body(zu_ref, zv_ref, w_ref, ln_ref, b_ref, avg_ref, o_ref):
        u = _gelu(zu_ref[...].astype(F32))
        v = _gelu(zv_ref[...].astype(F32))
        vhat, _ = _layer_norm_heads(v, avg_ref[...])
        vn = vhat * ln_ref[...]
        for b in range(tm // SG_BLOCK):
            rows = slice(b * SG_BLOCK, (b + 1) * SG_BLOCK)
            mixed = _gate_mix(vn[rows], w_ref, b_ref[...])
            o_ref[rows, :] = (u[rows] * mixed).astype(BF16)

    return pl.pallas_call(
        body, name="gate_fwd", grid=(S // tm,),
        in_specs=[pl.BlockSpec((tm, D_HEADS), lambda i: (i, 0)), pl.BlockSpec((tm, D_HEADS), lambda i: (i, 1)),
                  _full((N_HEADS, SG_BLOCK, SG_BLOCK)), _full((1, D_HEADS)), _full((SG_BLOCK, D_HEADS)),
                  _full((D_HEADS, D_HEADS))],
        out_specs=pl.BlockSpec((tm, D_HEADS), lambda i: (i, 0)),
        out_shape=jax.ShapeDtypeStruct((S, D_HEADS), BF16),
        compiler_params=_params(("parallel",)),
    )(z, z, w_mask, ln_row, b_full, seg_avg)


def _mix_out(x, out_a, out_b, w_out, g2, tm):
    S = x.shape[0]

    def body(x_ref, a_ref, b_ref, w_ref, g_ref, x1_ref, h_ref):
        y = jnp.dot(a_ref[...], w_ref[:D_HEADS, :], preferred_element_type=F32)
        y = y + jnp.dot(b_ref[...], w_ref[D_HEADS:, :], preferred_element_type=F32)
        x1 = x_ref[...] + y
        x1_ref[...] = x1
        r = lax.rsqrt(jnp.mean(x1 * x1, axis=-1, keepdims=True) + EPS)
        h_ref[...] = (x1 * r * g_ref[...]).astype(BF16)

    row = lambda w: pl.BlockSpec((tm, w), lambda i: (i, 0))
    return pl.pallas_call(
        body, name="mix_out", grid=(S // tm,),
        in_specs=[row(D_MODEL), row(D_HEADS), row(D_HEADS), _full((D_MODEL, D_MODEL)), _full((1, D_MODEL))],
        out_specs=[row(D_MODEL), row(D_MODEL)],
        out_shape=[jax.ShapeDtypeStruct((S, D_MODEL), F32), jax.ShapeDtypeStruct((S, D_MODEL), BF16)],
        compiler_params=_params(("parallel",)),
    )(x, out_a, out_b, w_out, g2)


def _up_proj(h2, w_up_q, tm):
    S = h2.shape[0]
    nq, _, wq = w_up_q.shape

    def body(h_ref, w_ref, a_ref):
        a_ref[...] = jnp.dot(h_ref[...], w_ref[...], preferred_element_type=F32).astype(BF16)

    return pl.pallas_call(
        body, name="up_proj", grid=(nq, S // tm),
        in_specs=[pl.BlockSpec((tm, D_MODEL), lambda j, i: (i, 0)), pl.BlockSpec((None, D_MODEL, wq), lambda j, i: (j, 0, 0))],
        out_specs=pl.BlockSpec((tm, wq), lambda j, i: (i, j)),
        out_shape=jax.ShapeDtypeStruct((S, nq * wq), BF16),
        compiler_params=_params(("parallel", "parallel")),
    )(h2, w_up_q)


def _shift_down(a, halo, k):
    tm = a.shape[0]
    ra = pltpu.roll(a, k, 0)
    rh = pltpu.roll(halo, k, 0)
    row = lax.broadcasted_iota(jnp.int32, halo.shape, 0)
    top = jnp.where(row < k, rh, ra[0:SUBLANES])
    return jnp.concatenate([top, ra[SUBLANES:tm]], axis=0)


def _shift_up(a, halo, k):
    tm = a.shape[0]
    ra = pltpu.roll(a, tm - k, 0)
    rh = pltpu.roll(halo, SUBLANES - k, 0)
    row = lax.broadcasted_iota(jnp.int32, halo.shape, 0)
    bottom = jnp.where(row >= SUBLANES - k, rh, ra[tm - SUBLANES:tm])
    return jnp.concatenate([ra[0:tm - SUBLANES], bottom], axis=0)


def _shift_matrices(tm):
    row = lax.broadcasted_iota(jnp.int32, (tm, tm), 0)
    col = lax.broadcasted_iota(jnp.int32, (tm, tm), 1)
    return [(row == col + k).astype(BF16) for k in (1, 2)]


def _conv_taps(a, halo, first, shifts):
    tm = a.shape[0]
    halo = halo.astype(F32) * jnp.where(first, 0.0, 1.0)
    if shifts is None:
        a = a.astype(F32)
        return a, _shift_down(a, halo, 1), _shift_down(a, halo, 2)
    row8 = lax.broadcasted_iota(jnp.int32, halo.shape, 0)
    taps = [a.astype(F32)]
    for k, shift in zip((1, 2), shifts):
        down = jnp.dot(shift, a, preferred_element_type=F32)
        top = down[0:SUBLANES] + jnp.where(row8 < k, pltpu.roll(halo, k, 0), 0.0)
        taps.append(jnp.concatenate([top, down[SUBLANES:tm]], axis=0))
    return taps


def _conv_gate_val(refs, shifts, cols, first):
    ag_ref, av_ref, hg_ref, hv_ref, wg_ref, wv_ref, bg_ref, bv_ref = refs
    g0, g1, g2 = _conv_taps(ag_ref[:, cols], hg_ref[:, cols], first, shifts)
    gate = wg_ref[2:3, cols] * g0 + wg_ref[1:2, cols] * g1 + wg_ref[0:1, cols] * g2 + bg_ref[:, cols]
    v0, v1, v2 = _conv_taps(av_ref[:, cols], hv_ref[:, cols], first, shifts)
    val = wv_ref[2:3, cols] * v0 + wv_ref[1:2, cols] * v1 + wv_ref[0:1, cols] * v2 + bv_ref[:, cols]
    return gate, val, (g2, g1, g0), (v2, v1, v0)


_FF_CHUNKS = [slice(j * FF_CHUNK, (j + 1) * FF_CHUNK) for j in range(D_FF // FF_CHUNK)]


def _conv_specs(tm):
    step = tm // SUBLANES
    prev = lambda i: jnp.maximum(i * step - 1, 0)
    return [pl.BlockSpec((tm, D_FF), lambda i: (i, 0)), pl.BlockSpec((tm, D_FF), lambda i: (i, 1)),
            pl.BlockSpec((SUBLANES, D_FF), lambda i: (prev(i), 0)), pl.BlockSpec((SUBLANES, D_FF), lambda i: (prev(i), 1))]


def _ffn_fwd_loss(a, w_conv, b_conv, w_down, x1, g3, target, tm):
    S = x1.shape[0]

    def body(ag_ref, av_ref, hg_ref, hv_ref, wg_ref, wv_ref, bg_ref, bv_ref, wd_ref, x1_ref, g_ref, t_ref,
             dx2_ref, loss_ref, dg_ref):
        i = pl.program_id(0)

        @pl.when(i == 0)
        def _():
            loss_ref[...] = jnp.zeros_like(loss_ref)
            dg_ref[...] = jnp.zeros_like(dg_ref)

        x2 = x1_ref[...]
        for cols in _FF_CHUNKS:
            gate, val, _, _ = _conv_gate_val((ag_ref, av_ref, hg_ref, hv_ref, wg_ref, wv_ref, bg_ref, bv_ref), None, cols, i == 0)
            half = 0.5 * gate
            y = ((half + half * jnp.tanh(half)) * val).astype(BF16)
            x2 = x2 + jnp.dot(y, wd_ref[cols, :], preferred_element_type=F32)
        r = lax.rsqrt(jnp.mean(x2 * x2, axis=-1, keepdims=True) + EPS)
        xhat = x2 * r
        gg = g_ref[...]
        err = xhat * gg - t_ref[...]
        loss_ref[...] += jnp.sum(err * err, axis=0, keepdims=True)
        dy = err * (1.0 / D_MODEL)
        dg_ref[...] += jnp.sum(dy * xhat, axis=0, keepdims=True)
        dxhat = dy * gg
        dx2_ref[...] = r * (dxhat - xhat * jnp.mean(dxhat * xhat, axis=-1, keepdims=True))

    row = lambda w: pl.BlockSpec((tm, w), lambda i: (i, 0))
    half = lambda r: [pl.BlockSpec((r, D_FF), lambda i: (0, 0)), pl.BlockSpec((r, D_FF), lambda i: (0, 1))]
    return pl.pallas_call(
        body, name="ffn_fwd_loss", grid=(S // tm,),
        in_specs=_conv_specs(tm) + half(3) + half(1) + [_full((D_FF, D_MODEL)), row(D_MODEL), _full((1, D_MODEL)), row(D_MODEL)],
        out_specs=[row(D_MODEL), _full((1, D_MODEL)), _full((1, D_MODEL))],
        out_shape=[jax.ShapeDtypeStruct((S, D_MODEL), F32), jax.ShapeDtypeStruct((1, D_MODEL), F32),
                   jax.ShapeDtypeStruct((1, D_MODEL), F32)],
        compiler_params=_params(("arbitrary",)),
    )(a, a, a, a, w_conv, w_conv, b_conv, b_conv, w_down, x1, g3, target)


def _ffn_bwd_gate(dx2, a, w_conv, b_conv, w_down, tm):
    S = dx2.shape[0]

    def body(dx_ref, ag_ref, av_ref, hg_ref, hv_ref, wg_ref, wv_ref, bg_ref, bv_ref, wd_ref,
             dc_ref, y_ref, dw_ref, db_ref):
        i = pl.program_id(0)

        @pl.when(i == 0)
        def _():
            dw_ref[...] = jnp.zeros_like(dw_ref)
            db_ref[...] = jnp.zeros_like(db_ref)

        dx = dx_ref[...].astype(BF16)
        shifts = _shift_matrices(tm)
        for cols in _FF_CHUNKS:
            gate, val, gtaps, vtaps = _conv_gate_val((ag_ref, av_ref, hg_ref, hv_ref, wg_ref, wv_ref, bg_ref, bv_ref), shifts, cols, i == 0)
            sg = _sigmoid(gate)
            act = gate * sg
            y_ref[:, cols] = (act * val).astype(BF16)
            dy = lax.dot_general(dx, wd_ref[cols, :], NT, preferred_element_type=F32)
            dgate = dy * val * (sg + act - act * sg)
            dval = dy * act
            for d, taps, out in ((dgate, gtaps, cols), (dval, vtaps, slice(D_FF + cols.start, D_FF + cols.stop))):
                dc_ref[:, out] = d.astype(BF16)
                db_ref[0:1, out] += jnp.sum(d, axis=0, keepdims=True)
                for j in range(3):
                    dw_ref[j:j + 1, out] += jnp.sum(d * taps[j], axis=0, keepdims=True)

    row = lambda w: pl.BlockSpec((tm, w), lambda i: (i, 0))
    half = lambda r: [pl.BlockSpec((r, D_FF), lambda i: (0, 0)), pl.BlockSpec((r, D_FF), lambda i: (0, 1))]
    return pl.pallas_call(
        body, name="ffn_bwd_gate", grid=(S // tm,),
        in_specs=[row(D_MODEL)] + _conv_specs(tm) + half(3) + half(1) + [_full((D_FF, D_MODEL))],
        out_specs=[row(2 * D_FF), row(D_FF), _full((SUBLANES, 2 * D_FF)), _full((1, 2 * D_FF))],
        out_shape=[jax.ShapeDtypeStruct((S, 2 * D_FF), BF16), jax.ShapeDtypeStruct((S, D_FF), BF16),
                   jax.ShapeDtypeStruct((SUBLANES, 2 * D_FF), F32), jax.ShapeDtypeStruct((1, 2 * D_FF), F32)],
        compiler_params=_params(("arbitrary",)),
    )(dx2, a, a, a, a, w_conv, w_conv, b_conv, b_conv, w_down)


def _conv_bwd(dc, w_conv, tm, tn):
    S, C = dc.shape
    step = tm // SUBLANES
    last_blk = S // SUBLANES - 1

    def body(d_ref, nx_ref, w_ref, o_ref):
        last = pl.program_id(0) == pl.num_programs(0) - 1
        row = lax.broadcasted_iota(jnp.int32, (tm, tm), 0)
        col = lax.broadcasted_iota(jnp.int32, (tm, tm), 1)
        row8 = lax.broadcasted_iota(jnp.int32, (SUBLANES, FF_CHUNK), 0)
        ups = [(row + k == col).astype(BF16) for k in (1, 2)]
        for c0 in range(0, tn, FF_CHUNK):
            cols = slice(c0, c0 + FF_CHUNK)
            d = d_ref[:, cols]
            nx = nx_ref[:, cols].astype(F32) * jnp.where(last, 0.0, 1.0)
            out = w_ref[2:3, cols] * d.astype(F32)
            for k, up in zip((1, 2), ups):
                moved = jnp.dot(up, d, preferred_element_type=F32)
                bottom = moved[tm - SUBLANES:tm] + jnp.where(row8 >= SUBLANES - k, pltpu.roll(nx, SUBLANES - k, 0), 0.0)
                out = out + w_ref[2 - k:3 - k, cols] * jnp.concatenate([moved[0:tm - SUBLANES], bottom], axis=0)
            o_ref[:, cols] = out.astype(BF16)

    return pl.pallas_call(
        body, name="conv_bwd", grid=(S // tm, C // tn),
        in_specs=[pl.BlockSpec((tm, tn), lambda i, j: (i, j)),
                  pl.BlockSpec((SUBLANES, tn), lambda i, j: (jnp.minimum((i + 1) * step, last_blk), j)),
                  pl.BlockSpec((3, tn), lambda i, j: (0, j))],
        out_specs=pl.BlockSpec((tm, tn), lambda i, j: (i, j)),
        out_shape=jax.ShapeDtypeStruct((S, C), BF16),
        compiler_params=_params(("parallel", "parallel")),
    )(dc, dc, w_conv)


def _matmul_tn(a, b, name, bm, bn, tk, col_a=0, col_b=0, quarters=None):
    S = a.shape[0]
    gm, gn = quarters if quarters else (1, 1)
    nk = S // tk

    def body(a_ref, b_ref, o_ref):
        @pl.when(pl.program_id(2) == 0)
        def _():
            o_ref[...] = jnp.zeros_like(o_ref)

        o_ref[...] += lax.dot_general(a_ref[...].astype(BF16), b_ref[...].astype(BF16), TN, preferred_element_type=F32)

    if quarters and gn > 1:
        out_spec = pl.BlockSpec((None, bm, bn), lambda i, j, k: (j, i, 0))
        out_shape = jax.ShapeDtypeStruct((gn, gm * bm, bn), F32)
    else:
        out_spec = pl.BlockSpec((bm, bn), lambda i, j, k: (i, j))
        out_shape = jax.ShapeDtypeStruct((gm * bm, gn * bn), F32)
    return pl.pallas_call(
        body, name=name, grid=(gm, gn, nk),
        in_specs=[pl.BlockSpec((tk, bm), lambda i, j, k: (k, col_a * gm + i)),
                  pl.BlockSpec((tk, bn), lambda i, j, k: (k, col_b * gn + j))],
        out_specs=out_spec, out_shape=out_shape,
        compiler_params=_params(("parallel", "parallel", "arbitrary")),
    )(a, b)


def _dw_out(out_a, out_b, dx1, tk):
    S = dx1.shape[0]

    def body(a_ref, b_ref, d_ref, o_ref):
        @pl.when(pl.program_id(0) == 0)
        def _():
            o_ref[...] = jnp.zeros_like(o_ref)

        d = d_ref[...].astype(BF16)
        o_ref[:D_HEADS, :] += lax.dot_general(a_ref[...], d, TN, preferred_element_type=F32)
        o_ref[D_HEADS:, :] += lax.dot_general(b_ref[...], d, TN, preferred_element_type=F32)

    row = lambda w: pl.BlockSpec((tk, w), lambda k: (k, 0))
    return pl.pallas_call(
        body, name="dw_out", grid=(S // tk,),
        in_specs=[row(D_HEADS), row(D_HEADS), row(D_MODEL)], out_specs=_full((D_MODEL, D_MODEL)),
        out_shape=jax.ShapeDtypeStruct((D_MODEL, D_MODEL), F32),
        compiler_params=_params(("arbitrary",)),
    )(out_a, out_b, dx1)


def _up_bwd(dact, w_up_q, x1, g2, dx2, tm):
    S = x1.shape[0]
    nq, _, wq = w_up_q.shape

    def body(d_ref, w_ref, x_ref, g_ref, dx2_ref, dx1_ref, dg_ref):
        @pl.when(pl.program_id(0) == 0)
        def _():
            dg_ref[...] = jnp.zeros_like(dg_ref)

        dh = jnp.zeros((tm, D_MODEL), F32)
        for j in range(nq):
            dh = dh + lax.dot_general(d_ref[:, j * wq:(j + 1) * wq], w_ref[j], NT, preferred_element_type=F32)
        dx, dg = _rms_bwd(dh, x_ref[...], g_ref[...])
        dg_ref[...] += dg
        dx1_ref[...] = dx2_ref[...] + dx

    row = lambda w: pl.BlockSpec((tm, w), lambda i: (i, 0))
    return pl.pallas_call(
        body, name="up_bwd", grid=(S // tm,),
        in_specs=[row(nq * wq), pl.BlockSpec((nq, D_MODEL, wq), lambda i: (0, 0, 0), pipeline_mode=pl.Buffered(1)),
                  row(D_MODEL), _full((1, D_MODEL)), row(D_MODEL)],
        out_specs=[row(D_MODEL), _full((1, D_MODEL))],
        out_shape=[jax.ShapeDtypeStruct((S, D_MODEL), F32), jax.ShapeDtypeStruct((1, D_MODEL), F32)],
        compiler_params=_params(("arbitrary",)),
    )(dact, w_up_q, x1, g2, dx2)


def _out_bwd(dx1, w_out, tm):
    S = dx1.shape[0]

    def body(d_ref, w_ref, o_ref):
        o_ref[...] = lax.dot_general(d_ref[...].astype(BF16), w_ref[...], NT, preferred_element_type=F32).astype(BF16)

    return pl.pallas_call(
        body, name="out_bwd", grid=(S // tm,),
        in_specs=[pl.BlockSpec((tm, D_MODEL), lambda i: (i, 0)), _full((D_MODEL, D_MODEL))],
        out_specs=pl.BlockSpec((tm, D_MODEL), lambda i: (i, 0)),
        out_shape=jax.ShapeDtypeStruct((S, D_MODEL), BF16),
        compiler_params=_params(("parallel",)),
    )(dx1, w_out)


def _gate_bwd(z, dcat, w_mask, w_mask_t, ln_row, b_full, seg_avg, head_ind, tm, swap):
    S = z.shape[0]
    nb = tm // SG_BLOCK
    ns = len(swap)

    def body(zu_ref, zv_ref, do_ref, w_ref, wt_ref, ln_ref, b_ref, avg_ref, ind_ref, *rest):
        dzu_ref, dzv_ref, dw_ref, db_ref, dln_ref = rest[ns:ns + 5]
        dvn_s, dbf_s = rest[2 * ns + 5:2 * ns + 7]
        swap_start, swap_finish = _swap_ops(rest[:ns], rest[ns + 5:2 * ns + 5], *rest[2 * ns + 7:])
        i = pl.program_id(0)

        @pl.when(i == 0)
        def _():
            swap_start()
            dw_ref[...] = jnp.zeros_like(dw_ref)
            dln_ref[...] = jnp.zeros_like(dln_ref)
            dbf_s[...] = jnp.zeros_like(dbf_s)

        zu = zu_ref[...].astype(F32)
        zv = zv_ref[...].astype(F32)
        u = _gelu(zu)
        v = _gelu(zv)
        avg = avg_ref[...]
        vhat, rstd = _layer_norm_heads(v, avg)
        ln = ln_ref[...]
        vn = vhat * ln
        for b in range(nb):
            rows = slice(b * SG_BLOCK, (b + 1) * SG_BLOCK)
            vn_b = vn[rows]
            mixed = _gate_mix(vn_b, w_ref, b_ref[...])
            do = do_ref[rows, :].astype(F32)
            dzu_ref[rows, :] = (do * mixed * _gelu_grad(zu[rows])).astype(BF16)
            dmix = do * u[rows]
            dbf_s[...] += dmix
            vn_bf = vn_b.astype(BF16)
            dvn = jnp.zeros((SG_BLOCK, D_HEADS), F32)
            for h in range(N_HEADS):
                dmh = jnp.where(_head_mask(h, SG_BLOCK), dmix, 0.0).astype(BF16)
                dw_ref[h] += lax.dot_general(dmh, vn_bf, NT, preferred_element_type=F32)
                dvn = dvn + jnp.dot(wt_ref[h], dmh, preferred_element_type=F32)
            dvn_s[rows, :] = dvn
        dvn = dvn_s[...]
        dln_ref[...] += jnp.sum(dvn * vhat, axis=0, keepdims=True)
        dvhat = dvn * ln
        dv = rstd * (dvhat - _split_dot(dvhat, avg) - vhat * _split_dot(dvhat * vhat, avg))
        dzv_ref[...] = (dv * _gelu_grad(zv)).astype(BF16)

        @pl.when(i == pl.num_programs(0) - 1)
        def _():
            r = lax.broadcasted_iota(jnp.int32, (SG_BLOCK, SG_BLOCK), 0) // CHUNK
            s = lax.broadcasted_iota(jnp.int32, (SG_BLOCK, SG_BLOCK), 1) // CHUNK
            for h in range(N_HEADS):
                dw_ref[h] = jnp.where(r >= s, dw_ref[h], 0.0)
            db_ref[...] = _split_dot(dbf_s[...], ind_ref[...])
            swap_finish()

    row = lambda col: pl.BlockSpec((tm, D_HEADS), lambda i: (i, col))
    wspec = _full((N_HEADS, SG_BLOCK, SG_BLOCK))
    out = pl.pallas_call(
        body, name="gate_bwd", grid=(S // tm,),
        in_specs=[row(0), row(1), row(0), wspec, wspec, _full((1, D_HEADS)), _full((SG_BLOCK, D_HEADS)),
                  _full((D_HEADS, D_HEADS)), _full((D_HEADS, LANES))] + [_ANY] * ns,
        out_specs=[row(0), row(0), wspec, _full((SG_BLOCK, LANES)), _full((1, D_HEADS))] + [_ANY] * ns,
        out_shape=[jax.ShapeDtypeStruct((S, D_HEADS), BF16), jax.ShapeDtypeStruct((S, D_HEADS), BF16),
                   jax.ShapeDtypeStruct((N_HEADS, SG_BLOCK, SG_BLOCK), F32), jax.ShapeDtypeStruct((SG_BLOCK, LANES), F32),
                   jax.ShapeDtypeStruct((1, D_HEADS), F32)] + _swap_shapes(swap),
        scratch_shapes=[pltpu.VMEM((tm, D_HEADS), F32), pltpu.VMEM((SG_BLOCK, D_HEADS), F32)] + _swap_sems(ns),
        compiler_params=_params(("arbitrary",)),
    )(z, z, dcat, w_mask, w_mask_t, ln_row, b_full, seg_avg, head_ind, *swap)
    return out[:5], out[5:]


def _attn_pack_grad(o, dcat, qa, lse, head_ind, k, tm):
    S = o.shape[0]

    def body(o_ref, do_ref, qa_ref, lse_ref, ind_ref, pl_ref, pt_ref, eye_ref, ds_ref, dst_ref, ls_ref, lst_ref,
             dop_ref, qb_ref, dot_ref, qbt_ref):
        do = do_ref[...]
        delta = _split_dot(o_ref[...].astype(F32) * do.astype(F32), ind_ref[...])
        hi = delta.astype(BF16).astype(F32)
        parts = (hi + pltpu.roll((delta - hi).astype(BF16).astype(F32), N_HEADS, 1)).astype(BF16)
        dop = jnp.dot(do, pl_ref[...], preferred_element_type=F32) - jnp.dot(parts, ds_ref[...], preferred_element_type=F32)
        for g in range(GROUPS):
            dop_ref[g] = dop[:, g * GROUP_PAD:(g + 1) * GROUP_PAD].astype(BF16)
        dot = lax.dot_general(pt_ref[...], do, NT, preferred_element_type=F32)
        dot_ref[...] = (dot - lax.dot_general(dst_ref[...], parts, NT, preferred_element_type=F32)).astype(BF16)
        qa = qa_ref[...]
        stack = jnp.concatenate(_split3(lse_ref[...]), axis=0)
        qb = qa.astype(F32) - lax.dot_general(stack, ls_ref[...], TN, preferred_element_type=F32)
        qbt = lax.dot_general(eye_ref[...], qa, NT, preferred_element_type=F32)
        qbt = qbt - jnp.dot(lst_ref[...], stack, preferred_element_type=F32)
        for g in range(GROUPS):
            qb_ref[g] = qb[:, g * GROUP_PAD:(g + 1) * GROUP_PAD].astype(BF16)
        qbt_ref[...] = qbt.astype(BF16)

    pad = pl.BlockSpec((tm, D_PAD), lambda i: (i, 0))
    padt = pl.BlockSpec((None, D_PAD, tm), lambda i: (i, 0, 0))
    return pl.pallas_call(
        body, name="attn_pack_grad", grid=(S // tm,),
        in_specs=[pl.BlockSpec((tm, D_HEADS), lambda i: (i, 0)), pl.BlockSpec((tm, D_HEADS), lambda i: (i, 1)), pad,
                  pl.BlockSpec((STAT_ROWS, tm), lambda i: (0, i)), _full((D_HEADS, LANES)), _full((D_HEADS, D_PAD)),
                  _full((D_PAD, D_HEADS)), _full((D_PAD, D_PAD)), _full((LANES, D_PAD)), _full((D_PAD, LANES)),
                  _full((3 * STAT_ROWS, D_PAD)), _full((D_PAD, 3 * STAT_ROWS))],
        out_specs=[pl.BlockSpec((GROUPS, tm, GROUP_PAD), lambda i: (0, i, 0))] * 2 + [padt, padt],
        out_shape=[jax.ShapeDtypeStruct((GROUPS, S, GROUP_PAD), BF16)] * 2 + [jax.ShapeDtypeStruct((S // tm, D_PAD, tm), BF16)] * 2,
        compiler_params=_params(("parallel",)),
    )(o, dcat, qa, lse, head_ind, k["place"], k["place_t"], jnp.eye(D_PAD, dtype=BF16), k["d_stat"], k["d_stat"].T,
      k["l_stat"], k["l_stat"].T)


def _attn_bwd(qb, qbt, ka, va, dop, dopt, k, tq, sums16):
    S = ka.shape[0]
    n = S // tq
    ns = len(sums16)

    pairs = [(kb, q) for kb in range(n) for q in range(kb, n)]
    k_of = jnp.asarray([kb for kb, _ in pairs], jnp.int32)
    q_of = jnp.asarray([q for _, q in pairs], jnp.int32)

    def body(k_of_ref, q_of_ref, q_ref, qt_ref, k_ref, v_ref, do_ref, dot_ref, pt_ref, *rest):
        dq_hbm, dcr_hbm, dk_ref, dv_ref, dcc_ref = rest[ns:ns + 5]
        dq_s, dcr_s, dk_s, dv_s, dcc_s = rest[2 * ns + 5:2 * ns + 10]
        s_s, d_s = rest[2 * ns + 10:2 * ns + 12], rest[2 * ns + 12:2 * ns + 14]
        sems = rest[2 * ns + 14]
        scatter_start, scatter_finish = _scatter_ops(rest[:ns], rest[ns + 5:2 * ns + 5], *rest[2 * ns + 15:])
        g = pl.program_id(0)
        ki, qi = k_of_ref[pl.program_id(1)], q_of_ref[pl.program_id(1)]

        @pl.when((g == 0) & (ki == 0) & (qi == 0))
        def _():
            scatter_start()

        @pl.when((ki == 0) & (qi == 0))
        def _():
            dq_s[...] = jnp.zeros_like(dq_s)
            dcr_s[...] = jnp.zeros_like(dcr_s)

        @pl.when(qi == ki)
        def _():
            dk_s[...] = jnp.zeros_like(dk_s)
            dv_s[...] = jnp.zeros_like(dv_s)
            dcc_s[...] = jnp.zeros_like(dcc_s)

        def step(diagonal):
            chunks = [slice(c * KEY_CHUNK, (c + 1) * KEY_CHUNK) for c in range(tq // KEY_CHUNK)]

            def scores(hh, rows, slot):
                sl = slice(hh * HEAD_PAD, (hh + 1) * HEAD_PAD)
                s_s[slot][rows, :] = lax.dot_general(q_ref[rows, sl], k_ref[:, sl], NT, preferred_element_type=F32)
                d_s[slot][rows, :] = lax.dot_general(do_ref[rows, sl], v_ref[:, sl], NT, preferred_element_type=F32)

            for rows in chunks:
                scores(0, rows, 0)
            for hh in range(GROUP_HEADS):
                sl = slice(hh * HEAD_PAD, (hh + 1) * HEAD_PAD)
                slot = hh % 2
                dv, dk = dv_s[sl, :], dk_s[sl, :]
                for rows in chunks:
                    if hh + 1 < GROUP_HEADS:
                        scores(hh + 1, rows, 1 - slot)
                    p = jnp.exp2(s_s[slot][rows, :])
                    if diagonal:
                        row = rows.start + lax.broadcasted_iota(jnp.int32, (KEY_CHUNK, tq), 0)
                        col = lax.broadcasted_iota(jnp.int32, (KEY_CHUNK, tq), 1)
                        p = jnp.where(row >= col, p, 0.0)
                    ds = p * d_s[slot][rows, :]
                    qrows = pl.ds(pl.multiple_of(qi * tq + rows.start, KEY_CHUNK), KEY_CHUNK)
                    dcc_s[hh:hh + 1, :] += jnp.sum(ds, axis=0, keepdims=True)
                    dcr_s[qrows, hh:hh + 1] += jnp.sum(ds, axis=1, keepdims=True)
                    ds = ds.astype(BF16)
                    dv = dv + jnp.dot(dot_ref[sl, rows], p.astype(BF16), preferred_element_type=F32)
                    dk = dk + jnp.dot(qt_ref[sl, rows], ds, preferred_element_type=F32)
                    dq_s[qrows, sl] += jnp.dot(ds, k_ref[:, sl], preferred_element_type=F32)
                dv_s[sl, :] = dv
                dk_s[sl, :] = dk

        @pl.when(qi > ki)
        def _():
            step(False)

        @pl.when(qi == ki)
        def _():
            step(True)

        @pl.when(qi == n - 1)
        def _():
            dk = dk_s[...]
            pt = pt_ref[...]
            dk_ref[...] = lax.dot_general((dk * (1.0 / LOG2E)).astype(BF16), pt, TN, preferred_element_type=F32).astype(BF16)
            dv_ref[...] = lax.dot_general(dv_s[...].astype(BF16), pt, TN, preferred_element_type=F32).astype(BF16)
            dcc_ref[...] = dcc_s[...]

        @pl.when((ki == n - 1) & (qi == n - 1))
        def _():
            copies = [pltpu.make_async_copy(dq_s, dq_hbm.at[g], sems.at[0]), pltpu.make_async_copy(dcr_s, dcr_hbm.at[g], sems.at[1])]
            for cp in copies:
                cp.start()
            for cp in copies:
                cp.wait()

        @pl.when((g == GROUPS - 1) & (ki == n - 1) & (qi == n - 1))
        def _():
            scatter_finish()

    gw = GROUP_HEADS * HEAD_DIM
    qspec = pl.BlockSpec((None, tq, GROUP_PAD), lambda g, i, ks, qs: (g, qs[i], 0))
    qtspec = pl.BlockSpec((None, GROUP_PAD, tq), lambda g, i, ks, qs: (qs[i], g, 0))
    kspec = pl.BlockSpec((tq, GROUP_PAD), lambda g, i, ks, qs: (ks[i], g))
    kout = pl.BlockSpec((tq, gw), lambda g, i, ks, qs: (ks[i], g))
    out = pl.pallas_call(
        body, name="attn_bwd",
        grid_spec=pltpu.PrefetchScalarGridSpec(
            num_scalar_prefetch=2, grid=(GROUPS, len(pairs)),
            in_specs=[qspec, qtspec, kspec, kspec, qspec, qtspec, pl.BlockSpec((GROUP_PAD, gw), lambda g, i, ks, qs: (0, 0))]
            + [_ANY] * ns,
            out_specs=[_ANY, _ANY, kout, kout, pl.BlockSpec((None, SUBLANES, tq), lambda g, i, ks, qs: (g, 0, ks[i]))] + [_ANY] * ns,
            scratch_shapes=[pltpu.VMEM((S, GROUP_PAD), F32), pltpu.VMEM((S, LANES), F32), pltpu.VMEM((GROUP_PAD, tq), F32),
                            pltpu.VMEM((GROUP_PAD, tq), F32), pltpu.VMEM((SUBLANES, tq), F32),
                            pltpu.VMEM((tq, tq), F32), pltpu.VMEM((tq, tq), F32), pltpu.VMEM((tq, tq), F32),
                            pltpu.VMEM((tq, tq), F32), pltpu.SemaphoreType.DMA((2,))]
            + _scatter_sems(ns)),
        out_shape=[jax.ShapeDtypeStruct((GROUPS, S, GROUP_PAD), F32), jax.ShapeDtypeStruct((GROUPS, S, LANES), F32),
                   jax.ShapeDtypeStruct((S, D_HEADS), BF16), jax.ShapeDtypeStruct((S, D_HEADS), BF16),
                   jax.ShapeDtypeStruct((GROUPS, SUBLANES, S), F32)]
        + _scatter_shapes(sums16),
        compiler_params=_params(("arbitrary", "arbitrary")),
    )(k_of, q_of, qb, qbt, ka, va, dop, dopt, k["place_t_group"], *sums16)
    return out[0], out[1], out[2], out[3], out[4], out[5:]


def _attn_unpack(dqp, k, tm):
    S = dqp.shape[1]
    gw = GROUP_HEADS * HEAD_DIM

    def body(dqp_ref, pt_ref, dq_ref):
        for g in range(GROUPS):
            dq_ref[:, g * gw:(g + 1) * gw] = jnp.dot((dqp_ref[g] * SCALE).astype(BF16), pt_ref[...],
                                                     preferred_element_type=F32).astype(BF16)

    return pl.pallas_call(
        body, name="attn_unpack", grid=(S // tm,),
        in_specs=[pl.BlockSpec((GROUPS, tm, GROUP_PAD), lambda i: (0, i, 0)), _full((GROUP_PAD, gw))],
        out_specs=pl.BlockSpec((tm, D_HEADS), lambda i: (i, 0)),
        out_shape=jax.ShapeDtypeStruct((S, D_HEADS), BF16),
        compiler_params=_params(("parallel",)),
    )(dqp, k["place_t_group"])


def _fox_bwd(dc, f, bias_row, tb):
    S = f.shape[0]
    nb = S // tb

    def body(dc_ref, f_ref, b_ref, df_ref, dbias_ref, carry):
        @pl.when(pl.program_id(0) == 0)
        def _():
            carry[...] = jnp.zeros_like(carry)
            dbias_ref[...] = jnp.zeros_like(dbias_ref)

        r = lax.broadcasted_iota(jnp.int32, (tb, tb), 0)
        s = lax.broadcasted_iota(jnp.int32, (tb, tb), 1)
        tri = (s >= r).astype(F32)
        rc = jnp.dot(tri, dc_ref[...], precision=lax.Precision.HIGHEST, preferred_element_type=F32) + carry[0:1, :]
        carry[...] = jnp.broadcast_to(rc[0:1, :], carry.shape)
        lane = lax.broadcasted_iota(jnp.int32, (tb, LANES), 1)
        df = jnp.where(lane < N_HEADS, rc * jax.nn.sigmoid(-(f_ref[...] + b_ref[...])), 0.0)
        df_ref[...] = df.astype(BF16)
        dbias_ref[...] += jnp.sum(df, axis=0, keepdims=True)

    rev = pl.BlockSpec((tb, LANES), lambda i: (nb - 1 - i, 0))
    return pl.pallas_call(
        body, name="fox_bwd", grid=(nb,),
        in_specs=[rev, rev, _full((1, LANES))],
        out_specs=[rev, _full((1, LANES))],
        out_shape=[jax.ShapeDtypeStruct((S, LANES), BF16), jax.ShapeDtypeStruct((1, LANES), F32)],
        scratch_shapes=[pltpu.VMEM((SUBLANES, LANES), F32)],
        compiler_params=_params(("arbitrary",)),
    )(dc, f, bias_row)


_DZ_WIDTHS = (D_HEADS,) * 5 + (LANES,)


def _in_bwd(pieces, w_in, x, g1, dx1, tm, sums16):
    S = x.shape[0]
    ns = len(sums16)

    def body(*refs):
        p_refs, (w_ref, x_ref, g_ref, dx1_ref) = refs[:6], refs[6:10]
        dx_ref, dg_ref = refs[10 + ns:12 + ns]
        scatter_start, scatter_finish = _scatter_ops(refs[10:10 + ns], refs[12 + ns:12 + 2 * ns], *refs[12 + 2 * ns:])

        @pl.when(pl.program_id(0) == 0)
        def _():
            dg_ref[...] = jnp.zeros_like(dg_ref)
            scatter_start()

        dh = jnp.zeros((tm, D_MODEL), F32)
        off = 0
        for p_ref, w in zip(p_refs, _DZ_WIDTHS):
            dh = dh + lax.dot_general(p_ref[...].astype(BF16), w_ref[:, off:off + w], NT, preferred_element_type=F32)
            off += w
        dx, dg = _rms_bwd(dh, x_ref[...], g_ref[...])
        dg_ref[...] += dg
        dx_ref[...] = dx1_ref[...] + dx

        @pl.when(pl.program_id(0) == pl.num_programs(0) - 1)
        def _():
            scatter_finish()

    row = lambda w: pl.BlockSpec((tm, w), lambda i: (i, 0))
    out = pl.pallas_call(
        body, name="in_bwd", grid=(S // tm,),
        in_specs=[row(w) for w in _DZ_WIDTHS] + [_full((D_MODEL, D_IN_PAD)), row(D_MODEL), _full((1, D_MODEL)), row(D_MODEL)]
        + [_ANY] * ns,
        out_specs=[row(D_MODEL), _full((1, D_MODEL))] + [_ANY] * ns,
        out_shape=[jax.ShapeDtypeStruct((S, D_MODEL), F32), jax.ShapeDtypeStruct((1, D_MODEL), F32)] + _scatter_shapes(sums16),
        scratch_shapes=_scatter_sems(ns),
        compiler_params=_params(("arbitrary",)),
    )(*pieces, w_in, x, g1, dx1, *sums16)
    return out[0], out[1], out[2:]


def _dw_in(h1, pieces, tk):
    S = h1.shape[0]

    def body(*refs):
        h_ref, p_refs, o_ref = refs[0], refs[1:7], refs[7]

        @pl.when(pl.program_id(0) == 0)
        def _():
            o_ref[...] = jnp.zeros_like(o_ref)

        off = 0
        for p_ref, w in zip(p_refs, _DZ_WIDTHS):
            o_ref[:, off:off + w] += lax.dot_general(h_ref[...], p_ref[...].astype(BF16), TN, preferred_element_type=F32)
            off += w

    row = lambda w: pl.BlockSpec((tk, w), lambda k: (k, 0))
    return pl.pallas_call(
        body, name="dw_in", grid=(S // tk,),
        in_specs=[row(D_MODEL)] + [row(w) for w in _DZ_WIDTHS],
        out_specs=_full((D_MODEL, D_IN_PAD)),
        out_shape=jax.ShapeDtypeStruct((D_MODEL, D_IN_PAD), F32),
        compiler_params=_params(("arbitrary",)),
    )(h1, *pieces)


def _adamw_math(w, g, m, v):
    m = ADAM_B1 * m + (1.0 - ADAM_B1) * g
    v = ADAM_B2 * v + (1.0 - ADAM_B2) * (g * g)
    m_hat = m / (1.0 - ADAM_B1 ** ADAM_STEP)
    v_hat = v / (1.0 - ADAM_B2 ** ADAM_STEP)
    delta = -ADAM_LR * (m_hat / (jnp.sqrt(v_hat) + ADAM_EPS) + ADAM_WD * w)
    return delta, m, v


def _adamw(name, w, g, m, v):
    R, C = w.shape
    tr = _row_tile(R, 256)

    def body(w_ref, g_ref, m_ref, v_ref, go_ref, d_ref, nm_ref, nv_ref):
        g = g_ref[...]
        d, nm, nv = _adamw_math(w_ref[...], g, m_ref[...], v_ref[...])
        go_ref[...] = g
        d_ref[...] = d
        nm_ref[...] = nm
        nv_ref[...] = nv

    spec = pl.BlockSpec((tr, C), lambda i: (i, 0))
    return pl.pallas_call(
        body, name=name, grid=(R // tr,), in_specs=[spec] * 4, out_specs=[spec] * 4,
        out_shape=[jax.ShapeDtypeStruct((R, C), F32)] * 4,
        compiler_params=_params(("parallel",)),
    )(w, g, m, v)


def _pair_sum(name, grad, theirs, ids):
    q, half, C = theirs.shape
    tr = _row_tile(half, 256)
    nb = half // tr

    def body(ids_ref, a_ref, b_ref, s_ref, sb_ref):
        s = a_ref[...] + b_ref[...]
        s_ref[...] = s
        sb_ref[...] = s.astype(BF16)

    here = pl.BlockSpec((None, tr, C), lambda j, i, ids: (j, i, 0))
    return pl.pallas_call(
        body, name=name,
        grid_spec=pltpu.PrefetchScalarGridSpec(
            num_scalar_prefetch=1, grid=(q, nb),
            in_specs=[pl.BlockSpec((None, tr, C), lambda j, i, ids: (j, ids[1] * nb + i, 0)), here],
            out_specs=[here, here]),
        out_shape=[jax.ShapeDtypeStruct((q, half, C), F32), jax.ShapeDtypeStruct((q, half, C), BF16)],
        compiler_params=_params(("parallel", "parallel")),
    )(ids, grad, theirs)


def _chip_sum(name, sums32, others, ids):
    _, half, C = sums32.shape
    tr = _row_tile(half, 256)
    nb = half // tr

    def body(ids_ref, a_ref, o_ref, s_ref):
        s = a_ref[...]
        for j in range(3):
            s = s + o_ref[j].astype(F32)
        s_ref[...] = s

    return pl.pallas_call(
        body, name=name,
        grid_spec=pltpu.PrefetchScalarGridSpec(
            num_scalar_prefetch=1, grid=(nb,),
            in_specs=[pl.BlockSpec((None, tr, C), lambda i, ids: (ids[0], i, 0)),
                      pl.BlockSpec((3, tr, C), lambda i, ids: (0, i, 0))],
            out_specs=pl.BlockSpec((tr, C), lambda i, ids: (ids[1] * nb + i, 0))),
        out_shape=jax.ShapeDtypeStruct((2 * half, C), F32),
        compiler_params=_params(("parallel",)),
    )(ids, sums32, others)


def _place():
    return lax.axis_index("x"), lax.axis_index("y"), lax.axis_index("c")


def _other_chips(x, y):
    return [(1 - x, y), (x, 1 - y), (1 - x, 1 - y)]


_ANY = pl.BlockSpec(memory_space=pl.ANY)


def _gather_quarters(shards):
    n = len(shards)

    def body(*refs):
        start, hand_on, finish = _gather_ops(refs[:n], refs[n:2 * n], *refs[2 * n:])
        start()
        hand_on()
        finish()

    return pl.pallas_call(
        body, name="gather_weights",
        in_specs=[_ANY] * n, out_specs=[_ANY] * n,
        out_shape=_gather_shapes(shards), scratch_shapes=_gather_sems(n),
    )(*shards)


def _gather_shapes(shards):
    return [jax.ShapeDtypeStruct((4,) + s.shape, s.dtype) for s in shards]


def _gather_sems(n):
    return [pltpu.SemaphoreType.DMA((n, 3))] * 4 + [pltpu.SemaphoreType.DMA((n,))]


def _gather_ops(ins, outs, send_sems, recv_sems, pass_send_sems, pass_recv_sems, own_sems):
    n = len(ins)
    halved = [r.shape[0] % 32 == 0 for r in ins]

    def part(a, quarter, core):
        if not halved[a]:
            return outs[a].at[quarter]
        half = ins[a].shape[0] // 2
        return outs[a].at[quarter, pl.ds(core * half, half), :]

    def ici(a, j, quarter):
        x, y, c = _place()
        px, py = _other_chips(x, y)[j]
        src = ins[a]
        if halved[a]:
            half = src.shape[0] // 2
            src = src.at[pl.ds(c * half, half), :]
        return pltpu.make_async_remote_copy(src_ref=src, dst_ref=part(a, quarter, c), send_sem=send_sems.at[a, j],
                                            recv_sem=recv_sems.at[a, j], device_id=(px, py, c), device_id_type=MESH)

    def passed(a, j, core):
        x, y, c = _place()
        px, py = _other_chips(x, y)[j]
        half = part(a, 2 * px + py, core)
        return pltpu.make_async_remote_copy(src_ref=half, dst_ref=half, send_sem=pass_send_sems.at[a, j],
                                            recv_sem=pass_recv_sems.at[a, j], device_id=(x, y, 1 - c), device_id_type=MESH)

    def own(a):
        x, y, _ = _place()
        return pltpu.make_async_copy(ins[a], outs[a].at[2 * x + y], own_sems.at[a])

    def start():
        x, y, _ = _place()
        for a in range(n):
            for j in range(3):
                ici(a, j, 2 * x + y).start()
            own(a).start()

    def hand_on():
        x, y, c = _place()
        for a in range(n):
            for j, (px, py) in enumerate(_other_chips(x, y)):
                ici(a, j, 2 * px + py).wait_recv()
                if halved[a]:
                    passed(a, j, c).start()

    def finish():
        x, y, c = _place()
        for a in range(n):
            for j in range(3):
                if halved[a]:
                    passed(a, j, 1 - c).wait_recv()
                    passed(a, j, c).wait_send()
                ici(a, j, 2 * x + y).wait_send()
            own(a).wait()

    return start, hand_on, finish


def _swap_halves(grads, name):
    n = len(grads)

    def body(*refs):
        start, finish = _swap_ops(refs[:n], refs[n:2 * n], *refs[2 * n:])
        start()
        finish()

    return pl.pallas_call(
        body, name=name,
        in_specs=[_ANY] * n, out_specs=[_ANY] * n, out_shape=_swap_shapes(grads), scratch_shapes=_swap_sems(n),
    )(*grads)


def _swap_shapes(grads):
    return [jax.ShapeDtypeStruct((4, g.shape[1] // 2, g.shape[2]), F32) for g in grads]


def _swap_sems(n):
    return [pltpu.SemaphoreType.DMA((n,))] * 2


def _swap_ops(ins, outs, send_sems, recv_sems):
    def copy(a):
        x, y, c = _place()
        half = ins[a].shape[1] // 2
        return pltpu.make_async_remote_copy(src_ref=ins[a].at[:, pl.ds((1 - c) * half, half), :], dst_ref=outs[a],
                                            send_sem=send_sems.at[a], recv_sem=recv_sems.at[a],
                                            device_id=(x, y, 1 - c), device_id_type=MESH)

    def start():
        for a in range(len(ins)):
            copy(a).start()

    def finish():
        for a in range(len(ins)):
            copy(a).wait()

    return start, finish


def _scatter_shapes(sums16):
    return [jax.ShapeDtypeStruct((3,) + s.shape[1:], BF16) for s in sums16]


def _scatter_sems(n):
    return [pltpu.SemaphoreType.DMA((n, 3))] * 2


def _scatter_ops(ins, outs, send_sems, recv_sems):
    n = len(ins)

    def copy(a, j):
        x, y, c = _place()
        px, py = _other_chips(x, y)[j]
        return pltpu.make_async_remote_copy(src_ref=ins[a].at[2 * px + py], dst_ref=outs[a].at[j], send_sem=send_sems.at[a, j],
                                            recv_sem=recv_sems.at[a, j], device_id=(px, py, c), device_id_type=MESH)

    def start():
        for a in range(n):
            for j in range(3):
                copy(a, j).start()

    def finish():
        for a in range(n):
            for j in range(3):
                copy(a, j).wait()

    return start, finish


def _join_halves(fulls):
    n = len(fulls)

    def body(*refs):
        ins, outs = refs[:n], refs[n:2 * n]
        send_sems, recv_sems = refs[2 * n:]
        x, y, c = _place()
        started = []
        for a in range(n):
            half = ins[a].shape[0] // 2
            rows = pl.ds(c * half, half)
            cp = pltpu.make_async_remote_copy(src_ref=ins[a].at[rows, :], dst_ref=outs[a].at[rows, :], send_sem=send_sems.at[a],
                                              recv_sem=recv_sems.at[a], device_id=(x, y, 1 - c), device_id_type=MESH)
            cp.start()
            started.append(cp)
        for cp in started:
            cp.wait()

    return pl.pallas_call(
        body, name="join_halves",
        in_specs=[_ANY] * n, out_specs=[_ANY] * n,
        out_shape=[jax.ShapeDtypeStruct(f.shape, F32) for f in fulls],
        input_output_aliases={a: a for a in range(n)},
        scratch_shapes=[pltpu.SemaphoreType.DMA((n,)), pltpu.SemaphoreType.DMA((n,))],
    )(*fulls)


def _small_allreduce(g):
    R = g.shape[0]
    half = R // 2

    def body(g_ref, out_ref, other_s, chip_s, parts_s, send_sems, recv_sems):
        x, y, c = _place()
        mine = 2 * x + y
        rows = pl.ds(pl.multiple_of(c * half, SUBLANES), half)

        def to_other_core(src, dst, k):
            return pltpu.make_async_remote_copy(src_ref=src, dst_ref=dst, send_sem=send_sems.at[k], recv_sem=recv_sems.at[k],
                                                device_id=(x, y, 1 - c), device_id_type=MESH)

        swap = to_other_core(g_ref, other_s, 0)
        swap.start()
        swap.wait()
        chip_s[...] = g_ref[...] + other_s[...]
        parts_s[mine] = chip_s[rows, :]
        sends = []
        for j, (px, py) in enumerate(_other_chips(x, y)):
            cp = pltpu.make_async_remote_copy(src_ref=chip_s.at[rows, :], dst_ref=parts_s.at[mine], send_sem=send_sems.at[1 + j],
                                              recv_sem=recv_sems.at[1 + j], device_id=(px, py, c), device_id_type=MESH)
            cp.start()
            sends.append(cp)
        for cp in sends:
            cp.wait()
        out_ref[rows, :] = (parts_s[0] + parts_s[1]) + (parts_s[2] + parts_s[3])
        join = to_other_core(out_ref.at[rows, :], out_ref.at[rows, :], 4)
        join.start()
        join.wait()

    vm = pl.BlockSpec(memory_space=pltpu.VMEM)
    return pl.pallas_call(
        body, name="small_allreduce",
        in_specs=[vm], out_specs=vm, out_shape=jax.ShapeDtypeStruct((R, LANES), F32),
        scratch_shapes=[pltpu.VMEM((R, LANES), F32), pltpu.VMEM((R, LANES), F32), pltpu.VMEM((4, half, LANES), F32),
                        pltpu.SemaphoreType.DMA((5,)), pltpu.SemaphoreType.DMA((5,))],
        compiler_params=pltpu.CompilerParams(vmem_limit_bytes=VMEM_LIMIT),
    )(g)


def _adamw_small(ws, gs, ms, vs):
    n = len(ws)

    def body(*refs):
        for k in range(n):
            w_ref, g_ref, m_ref, v_ref = (refs[j * n + k] for j in range(4))
            d, nm, nv = _adamw_math(w_ref[...], g_ref[...], m_ref[...], v_ref[...])
            refs[4 * n + k][...] = d
            refs[5 * n + k][...] = nm
            refs[6 * n + k][...] = nv

    vm = pl.BlockSpec(memory_space=pltpu.VMEM)
    out = pl.pallas_call(
        body, name="adamw_small",
        in_specs=[vm] * (4 * n), out_specs=[vm] * (3 * n),
        out_shape=[jax.ShapeDtypeStruct(w.shape, F32) for w in ws] * 3,
        compiler_params=pltpu.CompilerParams(vmem_limit_bytes=VMEM_LIMIT),
    )(*ws, *gs, *ms, *vs)
    return out[:n], out[n:2 * n], out[2 * n:]


_SMALL = (("norm_mix_g", D_MODEL), ("f_bias", N_HEADS), ("sg_ln_g", D_HEADS), ("sg_w", N_HEADS * SG_BLOCK * SG_BLOCK),
          ("sg_b", N_HEADS * SG_BLOCK), ("norm_ffn_g", D_MODEL), ("w_conv", 3 * 2 * D_FF), ("b_conv", 2 * D_FF),
          ("norm_final_g", D_MODEL))
_PACKED = _SMALL + (("sq_err", D_MODEL),)


def _pack_small(parts):
    rows = []
    for name, size in _PACKED:
        flat = parts[name].reshape(-1).astype(F32)
        pad = (-size) % (SUBLANES * LANES)
        rows.append(jnp.pad(flat, (0, pad)).reshape(-1, LANES))
    packed = jnp.concatenate(rows, axis=0)
    return jnp.pad(packed, ((0, (-packed.shape[0]) % (2 * SUBLANES)), (0, 0)))


def _unpack_small(packed, shapes):
    out, r = {}, 0
    for name, size in _PACKED:
        nrows = (size + SUBLANES * LANES - 1) // (SUBLANES * LANES) * SUBLANES
        out[name] = packed[r:r + nrows].reshape(-1)[:size].reshape(shapes[name])
        r += nrows
    return out


def _local_step(x, target, g1, w_in, f_bias, sg_ln_g, sg_w, sg_b, g2, b_conv, g3, late_shards, ids):
    S = x.shape[0]
    tm = _row_tile(S, 512)
    tms = _row_tile(S, 256)
    tq = tm

    lane = jnp.arange(D_HEADS)
    seg_avg = jnp.where(lane[:, None] // HEAD_DIM == lane[None, :] // HEAD_DIM, 1.0 / HEAD_DIM, 0.0).astype(BF16)
    head_ind = (lane[:, None] // HEAD_DIM == jnp.arange(LANES)[None, :]).astype(BF16)
    pos_chunk = jnp.arange(SG_BLOCK) // CHUNK
    w_mask32 = jnp.where(pos_chunk[:, None] >= pos_chunk[None, :], sg_w, 0.0)
    w_mask = w_mask32.astype(BF16)
    w_mask_t = jnp.swapaxes(w_mask32, 1, 2).astype(BF16)
    ln_row = sg_ln_g.reshape(1, D_HEADS)
    b_full = jnp.repeat(sg_b.T, HEAD_DIM, axis=1)
    bias_row = jnp.pad(f_bias.reshape(1, N_HEADS), ((0, 0), (0, LANES - N_HEADS)))
    b_conv_row = b_conv.reshape(1, 2 * D_FF)

    z, f, h1 = _in_proj(x, g1, w_in, tm)
    c = _fox_prep(f, bias_row, tm)
    consts = _attn_consts()
    qa, ka, va, vat = _attn_pack(z, c, consts, tm)
    out_b, lse, gathered = _attn_fwd(qa, ka, vat, consts["place_t"], tq, late_shards)
    g_out, w_up_q, g_down, g_conv = gathered
    w_out = g_out.reshape(D_MODEL, D_MODEL)
    w_down = g_down.reshape(D_FF, D_MODEL)
    w_conv = jnp.concatenate([g_conv[q] for q in range(4)], axis=1)
    out_a = _gate_fwd(z, w_mask, ln_row, b_full, seg_avg, tm)
    x1, h2 = _mix_out(x, out_a, out_b, w_out, g2, tm)
    a = _up_proj(h2, w_up_q, tm)
    dx2, sq_err, dg3 = _ffn_fwd_loss(a, w_conv, b_conv_row, w_down, x1, g3, target, tms)

    dconv, y, dw_conv8, db_conv = _ffn_bwd_gate(dx2, a, w_conv, b_conv_row, w_down, tms)
    dact = _conv_bwd(dconv, w_conv, tms, D_FF)
    dw_down = _matmul_tn(y, dx2, "dw_down", D_FF // 2, D_MODEL, tm, quarters=(2, 1))
    dx1, dg2 = _up_bwd(dact, w_up_q, x1, g2, dx2, tm)
    dw_up_q = _matmul_tn(h2, dact, "dw_up", D_MODEL, 2 * D_FF // 4, tm, quarters=(1, 4))
    dcat = _out_bwd(dx1, w_out, tm)
    dw_out = _dw_out(out_a, out_b, dx1, tm)
    early = {"w_down": dw_down.reshape(4, D_FF // 4, D_MODEL), "w_up": dw_up_q,
             "w_out": dw_out.reshape(4, D_MODEL // 4, D_MODEL)}
    (dzu, dzv, dsg_w, dsg_b_t, dln), theirs = _gate_bwd(z, dcat, w_mask, w_mask_t, ln_row, b_full, seg_avg, head_ind, tm,
                                                        list(early.values()))
    early_sums = _chip_sums(early, theirs, ids)
    dop, qb, dopt, qbt = _attn_pack_grad(out_b, dcat, qa, lse, head_ind, consts, tm)
    dqp, dc_rows, dk, dv, dc_cols, landed = _attn_bwd(qb, qbt, ka, va, dop, dopt, consts, tq,
                                                      [s16 for _, s16 in early_sums.values()])
    early_parts = {k: (s32, got) for (k, (s32, _)), got in zip(early_sums.items(), landed)}
    dq = _attn_unpack(dqp, consts, tm)
    dc_rows = jnp.concatenate([dc_rows[g][:, :GROUP_HEADS] for g in range(GROUPS)], axis=1)
    dc_cols = jnp.concatenate([dc_cols[g][:GROUP_HEADS] for g in range(GROUPS)], axis=0).T
    dc = jnp.pad(dc_rows - dc_cols, ((0, 0), (0, LANES - N_HEADS)))
    df, dbias = _fox_bwd(dc, f, bias_row, tm)
    pieces = (dzu, dzv, dq, dk, dv, df)
    dw_in = _dw_in(h1, pieces, tm)[:, :D_IN].reshape(D_MODEL, 4, D_IN // 4).transpose(1, 0, 2)
    (w_in_sum, w_in_sum16), = _chip_sums({"w_in": dw_in}, _swap_halves([dw_in], "swap_halves"), ids).values()
    dx, dg1, (w_in_landed,) = _in_bwd(pieces, w_in, x, g1, dx1, tm, [w_in_sum16])

    grads = {
        "norm_mix_g": dg1, "f_bias": dbias[:, :N_HEADS], "sg_ln_g": dln, "sg_w": dsg_w, "sg_b": dsg_b_t[:, :N_HEADS].T,
        "norm_ffn_g": dg2, "w_conv": dw_conv8[:3], "b_conv": db_conv, "norm_final_g": dg3,
    }
    return sq_err, dx, grads, {**early_parts, "w_in": (w_in_sum, w_in_landed)}


def _chip_sums(grads_q, theirs, ids):
    return {k: _pair_sum("pair_sum_" + k, g, t, ids) for (k, g), t in zip(grads_q.items(), theirs)}


def _finish_reduction(parts, ids):
    names = list(parts)
    fulls = [_chip_sum("chip_sum_" + k, s32, got, ids) for k, (s32, got) in parts.items()]
    return dict(zip(names, _join_halves(fulls)))


def kernel(x, norm_mix_g, w_in, f_bias, sg_ln_g, sg_w, sg_b, w_out, norm_ffn_g, w_up, w_conv, b_conv, w_down, norm_final_g, loss_target, m_norm_mix_g, m_w_in, m_f_bias, m_sg_ln_g, m_sg_w, m_sg_b, m_w_out, m_norm_ffn_g, m_w_up, m_w_conv, m_b_conv, m_w_down, m_norm_final_g, v_norm_mix_g, v_w_in, v_f_bias, v_sg_ln_g, v_sg_w, v_sg_b, v_w_out, v_norm_ffn_g, v_w_up, v_w_conv, v_b_conv, v_w_down, v_norm_final_g):
    args = dict(locals())
    quarter = 2 * lax.axis_index("x") + lax.axis_index("y")
    ids = jnp.stack([quarter, lax.axis_index("c")]).astype(jnp.int32)
    wq_conv = w_conv.shape[-1]

    g_in = _gather_quarters([w_in[0].astype(BF16)])[0]
    w_in_full = jnp.pad(jnp.concatenate([g_in[q] for q in range(4)], axis=1), ((0, 0), (0, D_IN_PAD - D_IN)))
    late_shards = [w_out[0].astype(BF16), w_up[0].astype(BF16), w_down[0].astype(BF16), w_conv[0]]

    sq_err, dx, grads, parts = _local_step(
        x[0], loss_target[0], norm_mix_g, w_in_full, f_bias[0], sg_ln_g[0], sg_w[0], sg_b[0], norm_ffn_g, b_conv[0],
        norm_final_g.reshape(1, D_MODEL), late_shards, ids)
    big = _finish_reduction(parts, ids)

    out = {"grad_x": dx[None]}
    for k in ("w_in", "w_out", "w_up", "w_down"):
        g, d, nm, nv = _adamw("adamw_" + k, args[k][0], big[k], args["m_" + k][0], args["v_" + k][0])
        out["grad_" + k], out["delta_" + k], out["new_m_" + k], out["new_v_" + k] = g[None], d[None], nm[None], nv[None]

    small_names = [n for n, _ in _SMALL]
    shapes = {n: (3, 4 * wq_conv) if n == "w_conv" else args[n].shape for n in small_names}
    shapes["sq_err"] = sq_err.shape
    g_small = _unpack_small(_small_allreduce(_pack_small({**{n: grads[n] for n in small_names}, "sq_err": sq_err})), shapes)
    out["loss"] = 0.5 * jnp.sum(g_small.pop("sq_err")) / D_MODEL
    g_small["w_conv"] = lax.dynamic_slice(g_small["w_conv"], (0, quarter * wq_conv), (3, wq_conv))[None]
    flat2d = lambda t: t.reshape(-1, t.shape[-1])
    updated = _adamw_small(*[[flat2d(src[p + n]) for n in small_names] for src, p in
                             ((args, ""), (g_small, ""), (args, "m_"), (args, "v_"))])
    for n, g in g_small.items():
        out["grad_" + n] = g
    for prefix, arrs in zip(("delta_", "new_m_", "new_v_"), updated):
        for n, t in zip(small_names, arrs):
            out[prefix + n] = t.reshape(args[n].shape)

    weights = ["norm_mix_g", "w_in", "f_bias", "sg_ln_g", "sg_w", "sg_b", "w_out", "norm_ffn_g", "w_up", "w_conv", "b_conv",
               "w_down", "norm_final_g"]
    return (out["loss"], out["grad_x"], *[out[p + n] for p in ("grad_", "delta_", "new_m_", "new_v_") for n in weights])
```

```python
import functools
import math

import jax
import jax.numpy as jnp
from jax import lax
from jax.experimental import pallas as pl
from jax.experimental.pallas import tpu as pltpu

F32 = jnp.float32
BF16 = jnp.bfloat16
MESH = pl.DeviceIdType.MESH

D_MODEL = 1024
N_HEADS = 8
HEAD_DIM = 64
D_HEADS = N_HEADS * HEAD_DIM
SG_BLOCK = 128
CHUNK = 64
D_FF = 2816
D_IN = 2 * D_HEADS + 3 * D_HEADS + N_HEADS
LANES = 128
SUBLANES = 8
D_IN_PAD = 5 * D_HEADS + LANES
EPS = 1e-6
SCALE = HEAD_DIM ** -0.5
NEG = -1e30
LOG2E = 1.4426950408889634
HEAD_PAD = LANES
D_PAD = N_HEADS * HEAD_PAD
Q_STAT = HEAD_DIM
K_STAT = HEAD_DIM + 3
L_STAT = HEAD_DIM + 6
GROUPS = 2
GROUP_HEADS = N_HEADS // GROUPS
GROUP_PAD = GROUP_HEADS * HEAD_PAD
KEY_CHUNK = 256
FWD_KEY_CHUNK = 512
STAT_ROWS = 16
FF_CHUNK = 256

ADAM_LR = 0.001
ADAM_B1 = 0.9
ADAM_B2 = 0.999
ADAM_EPS = 1e-08
ADAM_WD = 0.01
ADAM_STEP = 10

VMEM_LIMIT = 56 * 1024 * 1024

NT = (((1,), (1,)), ((), ()))
TN = (((0,), (0,)), ((), ()))


def _params(sem):
    return pltpu.CompilerParams(dimension_semantics=sem, vmem_limit_bytes=VMEM_LIMIT)


def _full(shape):
    nd = len(shape)
    return pl.BlockSpec(shape, lambda *_: (0,) * nd)


def _row_tile(rows, target):
    best = None
    for t in range(SUBLANES, min(rows, target) + 1, SUBLANES):
        if rows % t == 0:
            best = t
    assert best is not None, rows
    return best


def _sigmoid(x):
    return 0.5 * jnp.tanh(0.5 * x) + 0.5


def _gelu(z):
    return 0.5 * z * (1.0 + lax.erf(z * (2.0 ** -0.5)))


def _gelu_grad(z):
    cdf = 0.5 * (1.0 + lax.erf(z * (2.0 ** -0.5)))
    pdf = jnp.exp(-0.5 * z * z) * (1.0 / math.sqrt(2.0 * math.pi))
    return cdf + z * pdf


def _split_dot(x, m):
    hi = x.astype(BF16)
    lo = (x - hi.astype(F32)).astype(BF16)
    return jnp.dot(hi, m, preferred_element_type=F32) + jnp.dot(lo, m, preferred_element_type=F32)


def _head_mask(h, rows):
    lane = lax.broadcasted_iota(jnp.int32, (rows, D_HEADS), 1)
    return (lane >= h * HEAD_DIM) & (lane < (h + 1) * HEAD_DIM)


def _rms_bwd(dh, x, g):
    r = lax.rsqrt(jnp.mean(x * x, axis=-1, keepdims=True) + EPS)
    xhat = x * r
    dg = jnp.sum(dh * xhat, axis=0, keepdims=True)
    dxhat = dh * g
    dx = r * (dxhat - xhat * jnp.mean(dxhat * xhat, axis=-1, keepdims=True))
    return dx, dg


def _in_proj(x, g1, w_in, tm):
    S = x.shape[0]
    nz = D_IN_PAD - LANES

    def body(x_ref, g_ref, w_ref, z_ref, f_ref, h_ref):
        xf = x_ref[...]
        r = lax.rsqrt(jnp.mean(xf * xf, axis=-1, keepdims=True) + EPS)
        h = (xf * r * g_ref[...]).astype(BF16)
        h_ref[...] = h
        zz = jnp.dot(h, w_ref[...], preferred_element_type=F32)
        z_ref[...] = zz[:, :nz].astype(BF16)
        f_ref[...] = zz[:, nz:]

    return pl.pallas_call(
        body, name="in_proj", grid=(S // tm,),
        in_specs=[pl.BlockSpec((tm, D_MODEL), lambda i: (i, 0)), _full((1, D_MODEL)), _full((D_MODEL, D_IN_PAD))],
        out_specs=[pl.BlockSpec((tm, nz), lambda i: (i, 0)), pl.BlockSpec((tm, LANES), lambda i: (i, 0)),
                   pl.BlockSpec((tm, D_MODEL), lambda i: (i, 0))],
        out_shape=[jax.ShapeDtypeStruct((S, nz), BF16), jax.ShapeDtypeStruct((S, LANES), F32),
                   jax.ShapeDtypeStruct((S, D_MODEL), BF16)],
        compiler_params=_params(("parallel",)),
    )(x, g1, w_in)


def _fox_prep(f, bias_row, tb):
    S = f.shape[0]

    def body(f_ref, b_ref, c_ref, carry):
        @pl.when(pl.program_id(0) == 0)
        def _():
            carry[...] = jnp.zeros_like(carry)

        xv = f_ref[...] + b_ref[...]
        lf = jnp.minimum(xv, 0.0) - jnp.log(1.0 + jnp.exp(-jnp.abs(xv)))
        r = lax.broadcasted_iota(jnp.int32, (tb, tb), 0)
        s = lax.broadcasted_iota(jnp.int32, (tb, tb), 1)
        tri = (r >= s).astype(F32)
        cs = jnp.dot(tri, lf, precision=lax.Precision.HIGHEST, preferred_element_type=F32) + carry[0:1, :]
        c_ref[...] = cs
        carry[...] = jnp.broadcast_to(cs[tb - 1:tb, :], carry.shape)

    return pl.pallas_call(
        body, name="fox_prep", grid=(S // tb,),
        in_specs=[pl.BlockSpec((tb, LANES), lambda i: (i, 0)), _full((1, LANES))],
        out_specs=pl.BlockSpec((tb, LANES), lambda i: (i, 0)),
        out_shape=jax.ShapeDtypeStruct((S, LANES), F32),
        scratch_shapes=[pltpu.VMEM((SUBLANES, LANES), F32)],
        compiler_params=_params(("arbitrary",)),
    )(f, bias_row)


def _attn_consts():
    col = jnp.arange(D_PAD)
    row = jnp.arange(D_HEADS)
    head = jnp.arange(LANES)
    place = (row[:, None] // HEAD_DIM == col[None, :] // HEAD_PAD) & (row[:, None] % HEAD_DIM == col[None, :] % HEAD_PAD)

    def stat(offset):
        return ((head[:, None] < N_HEADS) & (col[None, :] == head[:, None] * HEAD_PAD + offset)).astype(BF16)

    def stat3(base):
        part, h = head // N_HEADS, head % N_HEADS
        return ((part[:, None] < 3) & (col[None, :] == h[:, None] * HEAD_PAD + base + part[:, None])).astype(BF16)

    def ones(offsets):
        return sum((col % HEAD_PAD == o) for o in offsets).astype(F32).reshape(1, D_PAD)

    place = place.astype(BF16)
    return {
        "place": place, "place_t": place.T, "place_t_group": place.T[:GROUP_PAD, :GROUP_HEADS * HEAD_DIM],
        "q_stat": stat3(Q_STAT), "k_stat": stat3(K_STAT),
        "d_stat": stat3(Q_STAT) * (head[:, None] < 2 * N_HEADS).astype(BF16),
        "l_stat": jnp.concatenate([stat(L_STAT + j)[:STAT_ROWS] for j in range(3)], axis=0),
        "q_ones": ones(range(K_STAT, K_STAT + 3)), "k_ones": ones(list(range(Q_STAT, Q_STAT + 3)) + list(range(L_STAT, L_STAT + 3))),
        "v_ones": ones(range(Q_STAT, Q_STAT + 2)),
    }


def _split3(x):
    hi = x.astype(BF16)
    r = x - hi.astype(F32)
    mid = r.astype(BF16)
    return hi, mid, (r - mid.astype(F32)).astype(BF16)


def _attn_pack(z, c, k, tm):
    S = z.shape[0]

    def body(q_ref, k_ref, v_ref, c_ref, pl_ref, pt_ref, qs_ref, ks_ref, qo_ref, ko_ref, vo_ref, voc_ref,
             qa_ref, ka_ref, va_ref, vt_ref):
        place = pl_ref[...]
        q = (q_ref[...].astype(F32) * (SCALE * LOG2E)).astype(BF16)
        qa = jnp.dot(q, place, preferred_element_type=F32) + qo_ref[...]
        ka = jnp.dot(k_ref[...], place, preferred_element_type=F32) + ko_ref[...]
        lane = lax.broadcasted_iota(jnp.int32, (tm, LANES), 1)
        hi, mid, lo = _split3(jnp.where(lane < N_HEADS, c_ref[...] * LOG2E, 0.0))
        parts = hi.astype(F32) + pltpu.roll(mid.astype(F32), N_HEADS, 1) + pltpu.roll(lo.astype(F32), 2 * N_HEADS, 1)
        parts = parts.astype(BF16)
        qa = qa + jnp.dot(parts, qs_ref[...], preferred_element_type=F32)
        ka = ka - jnp.dot(parts, ks_ref[...], preferred_element_type=F32)
        qa_ref[...] = qa.astype(BF16)
        ka_ref[...] = ka.astype(BF16)
        v = v_ref[...]
        va_ref[...] = (jnp.dot(v, place, preferred_element_type=F32) + vo_ref[...]).astype(BF16)
        vt_ref[...] = (lax.dot_general(pt_ref[...], v, NT, preferred_element_type=F32) + voc_ref[...]).astype(BF16)

    blk = lambda col: pl.BlockSpec((tm, D_HEADS), lambda i: (i, col))
    out = pl.BlockSpec((tm, D_PAD), lambda i: (i, 0))
    pad = jax.ShapeDtypeStruct((S, D_PAD), BF16)
    return pl.pallas_call(
        body, name="attn_pack", grid=(S // tm,),
        in_specs=[blk(2), blk(3), blk(4), pl.BlockSpec((tm, LANES), lambda i: (i, 0)), _full((D_HEADS, D_PAD)), _full((D_PAD, D_HEADS)),
                  _full((LANES, D_PAD)), _full((LANES, D_PAD)), _full((1, D_PAD)), _full((1, D_PAD)), _full((1, D_PAD)),
                  _full((D_PAD, 1))],
        out_specs=[out, out, out, pl.BlockSpec((None, D_PAD, tm), lambda i: (i, 0, 0))],
        out_shape=[pad, pad, pad, jax.ShapeDtypeStruct((S // tm, D_PAD, tm), BF16)],
        compiler_params=_params(("parallel",)),
    )(z, z, z, c, k["place"], k["place_t"], k["q_stat"], k["k_stat"], k["q_ones"], k["k_ones"], k["v_ones"], k["v_ones"].T)


def _attn_fwd(qa, ka, vat, place_t, tq, shards):
    S = qa.shape[0]
    n = S // tq
    ns = len(shards)
    hand_on_at = (2 * n) // 3

    pairs = [(q, k) for q in range(n) for k in range(q + 1)]
    q_of = jnp.asarray([q for q, _ in pairs], jnp.int32)
    k_of = jnp.asarray([k for _, k in pairs], jnp.int32)

    def body(q_of_ref, k_of_ref, q_ref, k_ref, vt_ref, pt_ref, *rest):
        o_ref, lse_ref = rest[ns:ns + 2]
        m_s, acc_s, ot_s = rest[2 * ns + 2:2 * ns + 5]
        s_s = rest[2 * ns + 5:2 * ns + 7]
        start, hand_on, finish = _gather_ops(rest[:ns], rest[ns + 2:2 * ns + 2], *rest[2 * ns + 7:])
        qi, ki = q_of_ref[pl.program_id(0)], k_of_ref[pl.program_id(0)]

        @pl.when((qi == 0) & (ki == 0))
        def _():
            start()

        @pl.when((qi == hand_on_at) & (ki == 0))
        def _():
            hand_on()

        @pl.when(ki == 0)
        def _():
            m_s[...] = jnp.full_like(m_s, NEG)
            acc_s[...] = jnp.zeros_like(acc_s)

        def step(diagonal):
            kc = FWD_KEY_CHUNK
            chunks = [slice(c * kc, (c + 1) * kc) for c in range(tq // kc)]

            def scores(h, rows, slot):
                sl = slice(h * HEAD_PAD, (h + 1) * HEAD_PAD)
                st = lax.dot_general(k_ref[rows, sl], q_ref[:, sl], NT, preferred_element_type=F32)
                if diagonal:
                    key = rows.start + lax.broadcasted_iota(jnp.int32, (kc, tq), 0)
                    query = lax.broadcasted_iota(jnp.int32, (kc, tq), 1)
                    st = jnp.where(query >= key, st, NEG)
                s_s[slot][rows, :] = st
                return jnp.max(st, axis=0, keepdims=True)

            m_cur = functools.reduce(jnp.maximum, [scores(0, rows, 0) for rows in chunks])
            for h in range(N_HEADS):
                sl = slice(h * HEAD_PAD, (h + 1) * HEAD_PAD)
                slot = h % 2
                m_prev = m_s[h][0:1, :]
                m_new = jnp.maximum(m_prev, m_cur)
                acc = jnp.exp2(m_prev - m_new) * acc_s[h]
                m_next = []
                for rows in chunks:
                    if h + 1 < N_HEADS:
                        m_next.append(scores(h + 1, rows, 1 - slot))
                    pt = jnp.exp2(s_s[slot][rows, :] - m_new).astype(BF16)
                    acc = acc + jnp.dot(vt_ref[sl, rows], pt, preferred_element_type=F32)
                acc_s[h] = acc
                m_s[h] = jnp.broadcast_to(m_new, (SUBLANES, tq))
                if m_next:
                    m_cur = functools.reduce(jnp.maximum, m_next)

        @pl.when(ki < qi)
        def _():
            step(False)

        @pl.when(ki == qi)
        def _():
            step(True)
            lse_ref[...] = jnp.zeros_like(lse_ref)
            for h in range(N_HEADS):
                acc = acc_s[h]
                denom = acc[Q_STAT:Q_STAT + 1, :]
                ot_s[h * HEAD_PAD:(h + 1) * HEAD_PAD, :] = (acc / denom).astype(BF16)
                lse_ref[h:h + 1, :] = m_s[h][0:1, :] + jnp.log(denom) * LOG2E
            o_ref[...] = lax.dot_general(ot_s[...], pt_ref[...], TN, preferred_element_type=F32).astype(BF16)

        @pl.when((qi == n - 1) & (ki == n - 1))
        def _():
            finish()

    out = pl.pallas_call(
        body, name="attn_fwd",
        grid_spec=pltpu.PrefetchScalarGridSpec(
            num_scalar_prefetch=2, grid=(len(pairs),),
            in_specs=[pl.BlockSpec((tq, D_PAD), lambda i, qs, ks: (qs[i], 0)),
                      pl.BlockSpec((tq, D_PAD), lambda i, qs, ks: (ks[i], 0)),
                      pl.BlockSpec((None, D_PAD, tq), lambda i, qs, ks: (ks[i], 0, 0)),
                      pl.BlockSpec((D_PAD, D_HEADS), lambda i, qs, ks: (0, 0))]
            + [_ANY] * ns,
            out_specs=[pl.BlockSpec((tq, D_HEADS), lambda i, qs, ks: (qs[i], 0)),
                       pl.BlockSpec((STAT_ROWS, tq), lambda i, qs, ks: (0, qs[i]))] + [_ANY] * ns,
            scratch_shapes=[pltpu.VMEM((N_HEADS, SUBLANES, tq), F32), pltpu.VMEM((N_HEADS, HEAD_PAD, tq), F32),
                            pltpu.VMEM((D_PAD, tq), BF16), pltpu.VMEM((tq, tq), F32), pltpu.VMEM((tq, tq), F32)] + _gather_sems(ns)),
        out_shape=[jax.ShapeDtypeStruct((S, D_HEADS), BF16), jax.ShapeDtypeStruct((STAT_ROWS, S), F32)] + _gather_shapes(shards),
        compiler_params=_params(("arbitrary",)),
    )(q_of, k_of, qa, ka, vat, place_t, *shards)
    return out[0], out[1], out[2:]


def _layer_norm_heads(v, seg_avg):
    mu = _split_dot(v, seg_avg)
    d = v - mu
    var = _split_dot(d * d, seg_avg)
    rstd = lax.rsqrt(var + EPS)
    return d * rstd, rstd


def _gate_mix(vn_blk, w_ref, bias):
    acc = bias
    for h in range(N_HEADS):
        vh = jnp.where(_head_mask(h, SG_BLOCK), vn_blk, 0.0).astype(BF16)
        acc = acc + jnp.dot(w_ref[h], vh, preferred_element_type=F32)
    return acc


def _gate_fwd(z, w_mask, ln_row, b_full, seg_avg, tm):
    S = z.shape[0]

    def body(zu_ref, zv_ref, w_ref, ln_ref, b_ref, avg_ref, o_ref):
        u = _gelu(zu_ref[...].astype(F32))
        v = _gelu(zv_ref[...].astype(F32))
        vhat, _ = _layer_norm_heads(v, avg_ref[...])
        vn = vhat * ln_ref[...]
        for b in range(tm // SG_BLOCK):
            rows = slice(b * SG_BLOCK, (b + 1) * SG_BLOCK)
            mixed = _gate_mix(vn[rows], w_ref, b_ref[...])
            o_ref[rows, :] = (u[rows] * mixed).astype(BF16)

    return pl.pallas_call(
        body, name="gate_fwd", grid=(S // tm,),
        in_specs=[pl.BlockSpec((tm, D_HEADS), lambda i: (i, 0)), pl.BlockSpec((tm, D_HEADS), lambda i: (i, 1)),
                  _full((N_HEADS, SG_BLOCK, SG_BLOCK)), _full((1, D_HEADS)), _full((SG_BLOCK, D_HEADS)),
                  _full((D_HEADS, D_HEADS))],
        out_specs=pl.BlockSpec((tm, D_HEADS), lambda i: (i, 0)),
        out_shape=jax.ShapeDtypeStruct((S, D_HEADS), BF16),
        compiler_params=_params(("parallel",)),
    )(z, z, w_mask, ln_row, b_full, seg_avg)


def _mix_out(x, out_a, out_b, w_out, g2, tm):
    S = x.shape[0]

    def body(x_ref, a_ref, b_ref, w_ref, g_ref, x1_ref, h_ref):
        y = jnp.dot(a_ref[...], w_ref[:D_HEADS, :], preferred_element_type=F32)
        y = y + jnp.dot(b_ref[...], w_ref[D_HEADS:, :], preferred_element_type=F32)
        x1 = x_ref[...] + y
        x1_ref[...] = x1
        r = lax.rsqrt(jnp.mean(x1 * x1, axis=-1, keepdims=True) + EPS)
        h_ref[...] = (x1 * r * g_ref[...]).astype(BF16)

    row = lambda w: pl.BlockSpec((tm, w), lambda i: (i, 0))
    return pl.pallas_call(
        body, name="mix_out", grid=(S // tm,),
        in_specs=[row(D_MODEL), row(D_HEADS), row(D_HEADS), _full((D_MODEL, D_MODEL)), _full((1, D_MODEL))],
        out_specs=[row(D_MODEL), row(D_MODEL)],
        out_shape=[jax.ShapeDtypeStruct((S, D_MODEL), F32), jax.ShapeDtypeStruct((S, D_MODEL), BF16)],
        compiler_params=_params(("parallel",)),
    )(x, out_a, out_b, w_out, g2)


def _up_proj(h2, w_up_q, tm):
    S = h2.shape[0]
    nq, _, wq = w_up_q.shape

    def body(h_ref, w_ref, a_ref):
        a_ref[...] = jnp.dot(h_ref[...], w_ref[...], preferred_element_type=F32).astype(BF16)

    return pl.pallas_call(
        body, name="up_proj", grid=(nq, S // tm),
        in_specs=[pl.BlockSpec((tm, D_MODEL), lambda j, i: (i, 0)), pl.BlockSpec((None, D_MODEL, wq), lambda j, i: (j, 0, 0))],
        out_specs=pl.BlockSpec((tm, wq), lambda j, i: (i, j)),
        out_shape=jax.ShapeDtypeStruct((S, nq * wq), BF16),
        compiler_params=_params(("parallel", "parallel")),
    )(h2, w_up_q)


def _shift_down(a, halo, k):
    tm = a.shape[0]
    ra = pltpu.roll(a, k, 0)
    rh = pltpu.roll(halo, k, 0)
    row = lax.broadcasted_iota(jnp.int32, halo.shape, 0)
    top = jnp.where(row < k, rh, ra[0:SUBLANES])
    return jnp.concatenate([top, ra[SUBLANES:tm]], axis=0)


def _shift_up(a, halo, k):
    tm = a.shape[0]
    ra = pltpu.roll(a, tm - k, 0)
    rh = pltpu.roll(halo, SUBLANES - k, 0)
    row = lax.broadcasted_iota(jnp.int32, halo.shape, 0)
    bottom = jnp.where(row >= SUBLANES - k, rh, ra[tm - SUBLANES:tm])
    return jnp.concatenate([ra[0:tm - SUBLANES], bottom], axis=0)


def _shift_matrices(tm):
    row = lax.broadcasted_iota(jnp.int32, (tm, tm), 0)
    col = lax.broadcasted_iota(jnp.int32, (tm, tm), 1)
    return [(row == col + k).astype(BF16) for k in (1, 2)]


def _conv_taps(a, halo, first, shifts):
    tm = a.shape[0]
    halo = halo.astype(F32) * jnp.where(first, 0.0, 1.0)
    if shifts is None:
        a = a.astype(F32)
        return a, _shift_down(a, halo, 1), _shift_down(a, halo, 2)
    row8 = lax.broadcasted_iota(jnp.int32, halo.shape, 0)
    taps = [a.astype(F32)]
    for k, shift in zip((1, 2), shifts):
        down = jnp.dot(shift, a, preferred_element_type=F32)
        top = down[0:SUBLANES] + jnp.where(row8 < k, pltpu.roll(halo, k, 0), 0.0)
        taps.append(jnp.concatenate([top, down[SUBLANES:tm]], axis=0))
    return taps


def _conv_gate_val(refs, shifts, cols, first):
    ag_ref, av_ref, hg_ref, hv_ref, wg_ref, wv_ref, bg_ref, bv_ref = refs
    g0, g1, g2 = _conv_taps(ag_ref[:, cols], hg_ref[:, cols], first, shifts)
    gate = wg_ref[2:3, cols] * g0 + wg_ref[1:2, cols] * g1 + wg_ref[0:1, cols] * g2 + bg_ref[:, cols]
    v0, v1, v2 = _conv_taps(av_ref[:, cols], hv_ref[:, cols], first, shifts)
    val = wv_ref[2:3, cols] * v0 + wv_ref[1:2, cols] * v1 + wv_ref[0:1, cols] * v2 + bv_ref[:, cols]
    return gate, val, (g2, g1, g0), (v2, v1, v0)


_FF_CHUNKS = [slice(j * FF_CHUNK, (j + 1) * FF_CHUNK) for j in range(D_FF // FF_CHUNK)]


def _conv_specs(tm):
    step = tm // SUBLANES
    prev = lambda i: jnp.maximum(i * step - 1, 0)
    return [pl.BlockSpec((tm, D_FF), lambda i: (i, 0)), pl.BlockSpec((tm, D_FF), lambda i: (i, 1)),
            pl.BlockSpec((SUBLANES, D_FF), lambda i: (prev(i), 0)), pl.BlockSpec((SUBLANES, D_FF), lambda i: (prev(i), 1))]


def _ffn_fwd_loss(a, w_conv, b_conv, w_down, x1, g3, target, tm):
    S = x1.shape[0]

    def body(ag_ref, av_ref, hg_ref, hv_ref, wg_ref, wv_ref, bg_ref, bv_ref, wd_ref, x1_ref, g_ref, t_ref,
             dx2_ref, loss_ref, dg_ref):
        i = pl.program_id(0)

        @pl.when(i == 0)
        def _():
            loss_ref[...] = jnp.zeros_like(loss_ref)
            dg_ref[...] = jnp.zeros_like(dg_ref)

        x2 = x1_ref[...]
        for cols in _FF_CHUNKS:
            gate, val, _, _ = _conv_gate_val((ag_ref, av_ref, hg_ref, hv_ref, wg_ref, wv_ref, bg_ref, bv_ref), None, cols, i == 0)
            half = 0.5 * gate
            y = ((half + half * jnp.tanh(half)) * val).astype(BF16)
            x2 = x2 + jnp.dot(y, wd_ref[cols, :], preferred_element_type=F32)
        r = lax.rsqrt(jnp.mean(x2 * x2, axis=-1, keepdims=True) + EPS)
        xhat = x2 * r
        gg = g_ref[...]
        err = xhat * gg - t_ref[...]
        loss_ref[...] += jnp.sum(err * err, axis=0, keepdims=True)
        dy = err * (1.0 / D_MODEL)
        dg_ref[...] += jnp.sum(dy * xhat, axis=0, keepdims=True)
        dxhat = dy * gg
        dx2_ref[...] = r * (dxhat - xhat * jnp.mean(dxhat * xhat, axis=-1, keepdims=True))

    row = lambda w: pl.BlockSpec((tm, w), lambda i: (i, 0))
    half = lambda r: [pl.BlockSpec((r, D_FF), lambda i: (0, 0)), pl.BlockSpec((r, D_FF), lambda i: (0, 1))]
    return pl.pallas_call(
        body, name="ffn_fwd_loss", grid=(S // tm,),
        in_specs=_conv_specs(tm) + half(3) + half(1) + [_full((D_FF, D_MODEL)), row(D_MODEL), _full((1, D_MODEL)), row(D_MODEL)],
        out_specs=[row(D_MODEL), _full((1, D_MODEL)), _full((1, D_MODEL))],
        out_shape=[jax.ShapeDtypeStruct((S, D_MODEL), F32), jax.ShapeDtypeStruct((1, D_MODEL), F32),
                   jax.ShapeDtypeStruct((1, D_MODEL), F32)],
        compiler_params=_params(("arbitrary",)),
    )(a, a, a, a, w_conv, w_conv, b_conv, b_conv, w_down, x1, g3, target)


def _ffn_bwd_gate(dx2, a, w_conv, b_conv, w_down, tm):
    S = dx2.shape[0]

    def body(dx_ref, ag_ref, av_ref, hg_ref, hv_ref, wg_ref, wv_ref, bg_ref, bv_ref, wd_ref,
             dc_ref, y_ref, dw_ref, db_ref):
        i = pl.program_id(0)

        @pl.when(i == 0)
        def _():
            dw_ref[...] = jnp.zeros_like(dw_ref)
            db_ref[...] = jnp.zeros_like(db_ref)

        dx = dx_ref[...].astype(BF16)
        shifts = _shift_matrices(tm)
        for cols in _FF_CHUNKS:
            gate, val, gtaps, vtaps = _conv_gate_val((ag_ref, av_ref, hg_ref, hv_ref, wg_ref, wv_ref, bg_ref, bv_ref), shifts, cols, i == 0)
            sg = _sigmoid(gate)
            act = gate * sg
            y_ref[:, cols] = (act * val).astype(BF16)
            dy = lax.dot_general(dx, wd_ref[cols, :], NT, preferred_element_type=F32)
            dgate = dy * val * (sg + act - act * sg)
            dval = dy * act
            for d, taps, out in ((dgate, gtaps, cols), (dval, vtaps, slice(D_FF + cols.start, D_FF + cols.stop))):
                dc_ref[:, out] = d.astype(BF16)
                db_ref[0:1, out] += jnp.sum(d, axis=0, keepdims=True)
                for j in range(3):
                    dw_ref[j:j + 1, out] += jnp.sum(d * taps[j], axis=0, keepdims=True)

    row = lambda w: pl.BlockSpec((tm, w), lambda i: (i, 0))
    half = lambda r: [pl.BlockSpec((r, D_FF), lambda i: (0, 0)), pl.BlockSpec((r, D_FF), lambda i: (0, 1))]
    return pl.pallas_call(
        body, name="ffn_bwd_gate", grid=(S // tm,),
        in_specs=[row(D_MODEL)] + _conv_specs(tm) + half(3) + half(1) + [_full((D_FF, D_MODEL))],
        out_specs=[row(2 * D_FF), row(D_FF), _full((SUBLANES, 2 * D_FF)), _full((1, 2 * D_FF))],
        out_shape=[jax.ShapeDtypeStruct((S, 2 * D_FF), BF16), jax.ShapeDtypeStruct((S, D_FF), BF16),
                   jax.ShapeDtypeStruct((SUBLANES, 2 * D_FF), F32), jax.ShapeDtypeStruct((1, 2 * D_FF), F32)],
        compiler_params=_params(("arbitrary",)),
    )(dx2, a, a, a, a, w_conv, w_conv, b_conv, b_conv, w_down)


def _conv_bwd(dc, w_conv, tm, tn):
    S, C = dc.shape
    step = tm // SUBLANES
    last_blk = S // SUBLANES - 1

    def body(d_ref, nx_ref, w_ref, o_ref):
        last = pl.program_id(0) == pl.num_programs(0) - 1
        row = lax.broadcasted_iota(jnp.int32, (tm, tm), 0)
        col = lax.broadcasted_iota(jnp.int32, (tm, tm), 1)
        row8 = lax.broadcasted_iota(jnp.int32, (SUBLANES, FF_CHUNK), 0)
        ups = [(row + k == col).astype(BF16) for k in (1, 2)]
        for c0 in range(0, tn, FF_CHUNK):
            cols = slice(c0, c0 + FF_CHUNK)
            d = d_ref[:, cols]
            nx = nx_ref[:, cols].astype(F32) * jnp.where(last, 0.0, 1.0)
            out = w_ref[2:3, cols] * d.astype(F32)
            for k, up in zip((1, 2), ups):
                moved = jnp.dot(up, d, preferred_element_type=F32)
                bottom = moved[tm - SUBLANES:tm] + jnp.where(row8 >= SUBLANES - k, pltpu.roll(nx, SUBLANES - k, 0), 0.0)
                out = out + w_ref[2 - k:3 - k, cols] * jnp.concatenate([moved[0:tm - SUBLANES], bottom], axis=0)
            o_ref[:, cols] = out.astype(BF16)

    return pl.pallas_call(
        body, name="conv_bwd", grid=(S // tm, C // tn),
        in_specs=[pl.BlockSpec((tm, tn), lambda i, j: (i, j)),
                  pl.BlockSpec((SUBLANES, tn), lambda i, j: (jnp.minimum((i + 1) * step, last_blk), j)),
                  pl.BlockSpec((3, tn), lambda i, j: (0, j))],
        out_specs=pl.BlockSpec((tm, tn), lambda i, j: (i, j)),
        out_shape=jax.ShapeDtypeStruct((S, C), BF16),
        compiler_params=_params(("parallel", "parallel")),
    )(dc, dc, w_conv)


def _matmul_tn(a, b, name, bm, bn, tk, col_a=0, col_b=0, quarters=None):
    S = a.shape[0]
    gm, gn = quarters if quarters else (1, 1)
    nk = S // tk

    def body(a_ref, b_ref, o_ref):
        @pl.when(pl.program_id(2) == 0)
        def _():
            o_ref[...] = jnp.zeros_like(o_ref)

        o_ref[...] += lax.dot_general(a_ref[...].astype(BF16), b_ref[...].astype(BF16), TN, preferred_element_type=F32)

    if quarters and gn > 1:
        out_spec = pl.BlockSpec((None, bm, bn), lambda i, j, k: (j, i, 0))
        out_shape = jax.ShapeDtypeStruct((gn, gm * bm, bn), F32)
    else:
        out_spec = pl.BlockSpec((bm, bn), lambda i, j, k: (i, j))
        out_shape = jax.ShapeDtypeStruct((gm * bm, gn * bn), F32)
    return pl.pallas_call(
        body, name=name, grid=(gm, gn, nk),
        in_specs=[pl.BlockSpec((tk, bm), lambda i, j, k: (k, col_a * gm + i)),
                  pl.BlockSpec((tk, bn), lambda i, j, k: (k, col_b * gn + j))],
        out_specs=out_spec, out_shape=out_shape,
        compiler_params=_params(("parallel", "parallel", "arbitrary")),
    )(a, b)


def _dw_out(out_a, out_b, dx1, tk):
    S = dx1.shape[0]

    def body(a_ref, b_ref, d_ref, o_ref):
        @pl.when(pl.program_id(0) == 0)
        def _():
            o_ref[...] = jnp.zeros_like(o_ref)

        d = d_ref[...].astype(BF16)
        o_ref[:D_HEADS, :] += lax.dot_general(a_ref[...], d, TN, preferred_element_type=F32)
        o_ref[D_HEADS:, :] += lax.dot_general(b_ref[...], d, TN, preferred_element_type=F32)

    row = lambda w: pl.BlockSpec((tk, w), lambda k: (k, 0))
    return pl.pallas_call(
        body, name="dw_out", grid=(S // tk,),
        in_specs=[row(D_HEADS), row(D_HEADS), row(D_MODEL)], out_specs=_full((D_MODEL, D_MODEL)),
        out_shape=jax.ShapeDtypeStruct((D_MODEL, D_MODEL), F32),
        compiler_params=_params(("arbitrary",)),
    )(out_a, out_b, dx1)


def _up_bwd(dact, w_up_q, x1, g2, dx2, tm):
    S = x1.shape[0]
    nq, _, wq = w_up_q.shape

    def body(d_ref, w_ref, x_ref, g_ref, dx2_ref, dx1_ref, dg_ref):
        @pl.when(pl.program_id(0) == 0)
        def _():
            dg_ref[...] = jnp.zeros_like(dg_ref)

        dh = jnp.zeros((tm, D_MODEL), F32)
        for j in range(nq):
            dh = dh + lax.dot_general(d_ref[:, j * wq:(j + 1) * wq], w_ref[j], NT, preferred_element_type=F32)
        dx, dg = _rms_bwd(dh, x_ref[...], g_ref[...])
        dg_ref[...] += dg
        dx1_ref[...] = dx2_ref[...] + dx

    row = lambda w: pl.BlockSpec((tm, w), lambda i: (i, 0))
    return pl.pallas_call(
        body, name="up_bwd", grid=(S // tm,),
        in_specs=[row(nq * wq), pl.BlockSpec((nq, D_MODEL, wq), lambda i: (0, 0, 0), pipeline_mode=pl.Buffered(1)),
                  row(D_MODEL), _full((1, D_MODEL)), row(D_MODEL)],
        out_specs=[row(D_MODEL), _full((1, D_MODEL))],
        out_shape=[jax.ShapeDtypeStruct((S, D_MODEL), F32), jax.ShapeDtypeStruct((1, D_MODEL), F32)],
        compiler_params=_params(("arbitrary",)),
    )(dact, w_up_q, x1, g2, dx2)


def _out_bwd(dx1, w_out, tm):
    S = dx1.shape[0]

    def body(d_ref, w_ref, o_ref):
        o_ref[...] = lax.dot_general(d_ref[...].astype(BF16), w_ref[...], NT, preferred_element_type=F32).astype(BF16)

    return pl.pallas_call(
        body, name="out_bwd", grid=(S // tm,),
        in_specs=[pl.BlockSpec((tm, D_MODEL), lambda i: (i, 0)), _full((D_MODEL, D_MODEL))],
        out_specs=pl.BlockSpec((tm, D_MODEL), lambda i: (i, 0)),
        out_shape=jax.ShapeDtypeStruct((S, D_MODEL), BF16),
        compiler_params=_params(("parallel",)),
    )(dx1, w_out)


def _gate_bwd(z, dcat, w_mask, w_mask_t, ln_row, b_full, seg_avg, head_ind, tm, swap):
    S = z.shape[0]
    nb = tm // SG_BLOCK
    ns = len(swap)

    def body(zu_ref, zv_ref, do_ref, w_ref, wt_ref, ln_ref, b_ref, avg_ref, ind_ref, *rest):
        dzu_ref, dzv_ref, dw_ref, db_ref, dln_ref = rest[ns:ns + 5]
        dvn_s, dbf_s = rest[2 * ns + 5:2 * ns + 7]
        swap_start, swap_finish = _swap_ops(rest[:ns], rest[ns + 5:2 * ns + 5], *rest[2 * ns + 7:])
        i = pl.program_id(0)

        @pl.when(i == 0)
        def _():
            swap_start()
            dw_ref[...] = jnp.zeros_like(dw_ref)
            dln_ref[...] = jnp.zeros_like(dln_ref)
            dbf_s[...] = jnp.zeros_like(dbf_s)

        zu = zu_ref[...].astype(F32)
        zv = zv_ref[...].astype(F32)
        u = _gelu(zu)
        v = _gelu(zv)
        avg = avg_ref[...]
        vhat, rstd = _layer_norm_heads(v, avg)
        ln = ln_ref[...]
        vn = vhat * ln
        for b in range(nb):
            rows = slice(b * SG_BLOCK, (b + 1) * SG_BLOCK)
            vn_b = vn[rows]
            mixed = _gate_mix(vn_b, w_ref, b_ref[...])
            do = do_ref[rows, :].astype(F32)
            dzu_ref[rows, :] = (do * mixed * _gelu_grad(zu[rows])).astype(BF16)
            dmix = do * u[rows]
            dbf_s[...] += dmix
            vn_bf = vn_b.astype(BF16)
            dvn = jnp.zeros((SG_BLOCK, D_HEADS), F32)
            for h in range(N_HEADS):
                dmh = jnp.where(_head_mask(h, SG_BLOCK), dmix, 0.0).astype(BF16)
                dw_ref[h] += lax.dot_general(dmh, vn_bf, NT, preferred_element_type=F32)
                dvn = dvn + jnp.dot(wt_ref[h], dmh, preferred_element_type=F32)
            dvn_s[rows, :] = dvn
        dvn = dvn_s[...]
        dln_ref[...] += jnp.sum(dvn * vhat, axis=0, keepdims=True)
        dvhat = dvn * ln
        dv = rstd * (dvhat - _split_dot(dvhat, avg) - vhat * _split_dot(dvhat * vhat, avg))
        dzv_ref[...] = (dv * _gelu_grad(zv)).astype(BF16)

        @pl.when(i == pl.num_programs(0) - 1)
        def _():
            r = lax.broadcasted_iota(jnp.int32, (SG_BLOCK, SG_BLOCK), 0) // CHUNK
            s = lax.broadcasted_iota(jnp.int32, (SG_BLOCK, SG_BLOCK), 1) // CHUNK
            for h in range(N_HEADS):
                dw_ref[h] = jnp.where(r >= s, dw_ref[h], 0.0)
            db_ref[...] = _split_dot(dbf_s[...], ind_ref[...])
            swap_finish()

    row = lambda col: pl.BlockSpec((tm, D_HEADS), lambda i: (i, col))
    wspec = _full((N_HEADS, SG_BLOCK, SG_BLOCK))
    out = pl.pallas_call(
        body, name="gate_bwd", grid=(S // tm,),
        in_specs=[row(0), row(1), row(0), wspec, wspec, _full((1, D_HEADS)), _full((SG_BLOCK, D_HEADS)),
                  _full((D_HEADS, D_HEADS)), _full((D_HEADS, LANES))] + [_ANY] * ns,
        out_specs=[row(0), row(0), wspec, _full((SG_BLOCK, LANES)), _full((1, D_HEADS))] + [_ANY] * ns,
        out_shape=[jax.ShapeDtypeStruct((S, D_HEADS), BF16), jax.ShapeDtypeStruct((S, D_HEADS), BF16),
                   jax.ShapeDtypeStruct((N_HEADS, SG_BLOCK, SG_BLOCK), F32), jax.ShapeDtypeStruct((SG_BLOCK, LANES), F32),
                   jax.ShapeDtypeStruct((1, D_HEADS), F32)] + _swap_shapes(swap),
        scratch_shapes=[pltpu.VMEM((tm, D_HEADS), F32), pltpu.VMEM((SG_BLOCK, D_HEADS), F32)] + _swap_sems(ns),
        compiler_params=_params(("arbitrary",)),
    )(z, z, dcat, w_mask, w_mask_t, ln_row, b_full, seg_avg, head_ind, *swap)
    return out[:5], out[5:]


def _attn_pack_grad(o, dcat, qa, lse, head_ind, k, tm):
    S = o.shape[0]

    def body(o_ref, do_ref, qa_ref, lse_ref, ind_ref, pl_ref, pt_ref, eye_ref, ds_ref, dst_ref, ls_ref, lst_ref,
             dop_ref, qb_ref, dot_ref, qbt_ref):
        do = do_ref[...]
        delta = _split_dot(o_ref[...].astype(F32) * do.astype(F32), ind_ref[...])
        hi = delta.astype(BF16).astype(F32)
        parts = (hi + pltpu.roll((delta - hi).astype(BF16).astype(F32), N_HEADS, 1)).astype(BF16)
        dop = jnp.dot(do, pl_ref[...], preferred_element_type=F32) - jnp.dot(parts, ds_ref[...], preferred_element_type=F32)
        for g in range(GROUPS):
            dop_ref[g] = dop[:, g * GROUP_PAD:(g + 1) * GROUP_PAD].astype(BF16)
        dot = lax.dot_general(pt_ref[...], do, NT, preferred_element_type=F32)
        dot_ref[...] = (dot - lax.dot_general(dst_ref[...], parts, NT, preferred_element_type=F32)).astype(BF16)
        qa = qa_ref[...]
        stack = jnp.concatenate(_split3(lse_ref[...]), axis=0)
        qb = qa.astype(F32) - lax.dot_general(stack, ls_ref[...], TN, preferred_element_type=F32)
        qbt = lax.dot_general(eye_ref[...], qa, NT, preferred_element_type=F32)
        qbt = qbt - jnp.dot(lst_ref[...], stack, preferred_element_type=F32)
        for g in range(GROUPS):
            qb_ref[g] = qb[:, g * GROUP_PAD:(g + 1) * GROUP_PAD].astype(BF16)
        qbt_ref[...] = qbt.astype(BF16)

    pad = pl.BlockSpec((tm, D_PAD), lambda i: (i, 0))
    padt = pl.BlockSpec((None, D_PAD, tm), lambda i: (i, 0, 0))
    return pl.pallas_call(
        body, name="attn_pack_grad", grid=(S // tm,),
        in_specs=[pl.BlockSpec((tm, D_HEADS), lambda i: (i, 0)), pl.BlockSpec((tm, D_HEADS), lambda i: (i, 1)), pad,
                  pl.BlockSpec((STAT_ROWS, tm), lambda i: (0, i)), _full((D_HEADS, LANES)), _full((D_HEADS, D_PAD)),
                  _full((D_PAD, D_HEADS)), _full((D_PAD, D_PAD)), _full((LANES, D_PAD)), _full((D_PAD, LANES)),
                  _full((3 * STAT_ROWS, D_PAD)), _full((D_PAD, 3 * STAT_ROWS))],
        out_specs=[pl.BlockSpec((GROUPS, tm, GROUP_PAD), lambda i: (0, i, 0))] * 2 + [padt, padt],
        out_shape=[jax.ShapeDtypeStruct((GROUPS, S, GROUP_PAD), BF16)] * 2 + [jax.ShapeDtypeStruct((S // tm, D_PAD, tm), BF16)] * 2,
        compiler_params=_params(("parallel",)),
    )(o, dcat, qa, lse, head_ind, k["place"], k["place_t"], jnp.eye(D_PAD, dtype=BF16), k["d_stat"], k["d_stat"].T,
      k["l_stat"], k["l_stat"].T)


def _attn_bwd(qb, qbt, ka, va, dop, dopt, k, tq, sums16):
    S = ka.shape[0]
    n = S // tq
    ns = len(sums16)

    pairs = [(kb, q) for kb in range(n) for q in range(kb, n)]
    k_of = jnp.asarray([kb for kb, _ in pairs], jnp.int32)
    q_of = jnp.asarray([q for _, q in pairs], jnp.int32)

    def body(k_of_ref, q_of_ref, q_ref, qt_ref, k_ref, v_ref, do_ref, dot_ref, pt_ref, *rest):
        dq_hbm, dcr_hbm, dk_ref, dv_ref, dcc_ref = rest[ns:ns + 5]
        dq_s, dcr_s, dk_s, dv_s, dcc_s = rest[2 * ns + 5:2 * ns + 10]
        s_s, d_s = rest[2 * ns + 10:2 * ns + 12], rest[2 * ns + 12:2 * ns + 14]
        sems = rest[2 * ns + 14]
        scatter_start, scatter_finish = _scatter_ops(rest[:ns], rest[ns + 5:2 * ns + 5], *rest[2 * ns + 15:])
        g = pl.program_id(0)
        ki, qi = k_of_ref[pl.program_id(1)], q_of_ref[pl.program_id(1)]

        @pl.when((g == 0) & (ki == 0) & (qi == 0))
        def _():
            scatter_start()

        @pl.when((ki == 0) & (qi == 0))
        def _():
            dq_s[...] = jnp.zeros_like(dq_s)
            dcr_s[...] = jnp.zeros_like(dcr_s)

        @pl.when(qi == ki)
        def _():
            dk_s[...] = jnp.zeros_like(dk_s)
            dv_s[...] = jnp.zeros_like(dv_s)
            dcc_s[...] = jnp.zeros_like(dcc_s)

        def step(diagonal):
            chunks = [slice(c * KEY_CHUNK, (c + 1) * KEY_CHUNK) for c in range(tq // KEY_CHUNK)]

            def scores(hh, rows, slot):
                sl = slice(hh * HEAD_PAD, (hh + 1) * HEAD_PAD)
                s_s[slot][rows, :] = lax.dot_general(q_ref[rows, sl], k_ref[:, sl], NT, preferred_element_type=F32)
                d_s[slot][rows, :] = lax.dot_general(do_ref[rows, sl], v_ref[:, sl], NT, preferred_element_type=F32)

            for rows in chunks:
                scores(0, rows, 0)
            for hh in range(GROUP_HEADS):
                sl = slice(hh * HEAD_PAD, (hh + 1) * HEAD_PAD)
                slot = hh % 2
                dv, dk = dv_s[sl, :], dk_s[sl, :]
                for rows in chunks:
                    if hh + 1 < GROUP_HEADS:
                        scores(hh + 1, rows, 1 - slot)
                    p = jnp.exp2(s_s[slot][rows, :])
                    if diagonal:
                        row = rows.start + lax.broadcasted_iota(jnp.int32, (KEY_CHUNK, tq), 0)
                        col = lax.broadcasted_iota(jnp.int32, (KEY_CHUNK, tq), 1)
                        p = jnp.where(row >= col, p, 0.0)
                    ds = p * d_s[slot][rows, :]
                    qrows = pl.ds(pl.multiple_of(qi * tq + rows.start, KEY_CHUNK), KEY_CHUNK)
                    dcc_s[hh:hh + 1, :] += jnp.sum(ds, axis=0, keepdims=True)
                    dcr_s[qrows, hh:hh + 1] += jnp.sum(ds, axis=1, keepdims=True)
                    ds = ds.astype(BF16)
                    dv = dv + jnp.dot(dot_ref[sl, rows], p.astype(BF16), preferred_element_type=F32)
                    dk = dk + jnp.dot(qt_ref[sl, rows], ds, preferred_element_type=F32)
                    dq_s[qrows, sl] += jnp.dot(ds, k_ref[:, sl], preferred_element_type=F32)
                dv_s[sl, :] = dv
                dk_s[sl, :] = dk

        @pl.when(qi > ki)
        def _():
            step(False)

        @pl.when(qi == ki)
        def _():
            step(True)

        @pl.when(qi == n - 1)
        def _():
            dk = dk_s[...]
            pt = pt_ref[...]
            dk_ref[...] = lax.dot_general((dk * (1.0 / LOG2E)).astype(BF16), pt, TN, preferred_element_type=F32).astype(BF16)
            dv_ref[...] = lax.dot_general(dv_s[...].astype(BF16), pt, TN, preferred_element_type=F32).astype(BF16)
            dcc_ref[...] = dcc_s[...]

        @pl.when((ki == n - 1) & (qi == n - 1))
        def _():
            copies = [pltpu.make_async_copy(dq_s, dq_hbm.at[g], sems.at[0]), pltpu.make_async_copy(dcr_s, dcr_hbm.at[g], sems.at[1])]
            for cp in copies:
                cp.start()
            for cp in copies:
                cp.wait()

        @pl.when((g == GROUPS - 1) & (ki == n - 1) & (qi == n - 1))
        def _():
            scatter_finish()

    gw = GROUP_HEADS * HEAD_DIM
    qspec = pl.BlockSpec((None, tq, GROUP_PAD), lambda g, i, ks, qs: (g, qs[i], 0))
    qtspec = pl.BlockSpec((None, GROUP_PAD, tq), lambda g, i, ks, qs: (qs[i], g, 0))
    kspec = pl.BlockSpec((tq, GROUP_PAD), lambda g, i, ks, qs: (ks[i], g))
    kout = pl.BlockSpec((tq, gw), lambda g, i, ks, qs: (ks[i], g))
    out = pl.pallas_call(
        body, name="attn_bwd",
        grid_spec=pltpu.PrefetchScalarGridSpec(
            num_scalar_prefetch=2, grid=(GROUPS, len(pairs)),
            in_specs=[qspec, qtspec, kspec, kspec, qspec, qtspec, pl.BlockSpec((GROUP_PAD, gw), lambda g, i, ks, qs: (0, 0))]
            + [_ANY] * ns,
            out_specs=[_ANY, _ANY, kout, kout, pl.BlockSpec((None, SUBLANES, tq), lambda g, i, ks, qs: (g, 0, ks[i]))] + [_ANY] * ns,
            scratch_shapes=[pltpu.VMEM((S, GROUP_PAD), F32), pltpu.VMEM((S, LANES), F32), pltpu.VMEM((GROUP_PAD, tq), F32),
                            pltpu.VMEM((GROUP_PAD, tq), F32), pltpu.VMEM((SUBLANES, tq), F32),
                            pltpu.VMEM((tq, tq), F32), pltpu.VMEM((tq, tq), F32), pltpu.VMEM((tq, tq), F32),
                            pltpu.VMEM((tq, tq), F32), pltpu.SemaphoreType.DMA((2,))]
            + _scatter_sems(ns)),
        out_shape=[jax.ShapeDtypeStruct((GROUPS, S, GROUP_PAD), F32), jax.ShapeDtypeStruct((GROUPS, S, LANES), F32),
                   jax.ShapeDtypeStruct((S, D_HEADS), BF16), jax.ShapeDtypeStruct((S, D_HEADS), BF16),
                   jax.ShapeDtypeStruct((GROUPS, SUBLANES, S), F32)]
        + _scatter_shapes(sums16),
        compiler_params=_params(("arbitrary", "arbitrary")),
    )(k_of, q_of, qb, qbt, ka, va, dop, dopt, k["place_t_group"], *sums16)
    return out[0], out[1], out[2], out[3], out[4], out[5:]


def _attn_unpack(dqp, k, tm):
    S = dqp.shape[1]
    gw = GROUP_HEADS * HEAD_DIM

    def body(dqp_ref, pt_ref, dq_ref):
        for g in range(GROUPS):
            dq_ref[:, g * gw:(g + 1) * gw] = jnp.dot((dqp_ref[g] * SCALE).astype(BF16), pt_ref[...],
                                                     preferred_element_type=F32).astype(BF16)

    return pl.pallas_call(
        body, name="attn_unpack", grid=(S // tm,),
        in_specs=[pl.BlockSpec((GROUPS, tm, GROUP_PAD), lambda i: (0, i, 0)), _full((GROUP_PAD, gw))],
        out_specs=pl.BlockSpec((tm, D_HEADS), lambda i: (i, 0)),
        out_shape=jax.ShapeDtypeStruct((S, D_HEADS), BF16),
        compiler_params=_params(("parallel",)),
    )(dqp, k["place_t_group"])


def _fox_bwd(dc, f, bias_row, tb):
    S = f.shape[0]
    nb = S // tb

    def body(dc_ref, f_ref, b_ref, df_ref, dbias_ref, carry):
        @pl.when(pl.program_id(0) == 0)
        def _():
            carry[...] = jnp.zeros_like(carry)
            dbias_ref[...] = jnp.zeros_like(dbias_ref)

        r = lax.broadcasted_iota(jnp.int32, (tb, tb), 0)
        s = lax.broadcasted_iota(jnp.int32, (tb, tb), 1)
        tri = (s >= r).astype(F32)
        rc = jnp.dot(tri, dc_ref[...], precision=lax.Precision.HIGHEST, preferred_element_type=F32) + carry[0:1, :]
        carry[...] = jnp.broadcast_to(rc[0:1, :], carry.shape)
        lane = lax.broadcasted_iota(jnp.int32, (tb, LANES), 1)
        df = jnp.where(lane < N_HEADS, rc * jax.nn.sigmoid(-(f_ref[...] + b_ref[...])), 0.0)
        df_ref[...] = df.astype(BF16)
        dbias_ref[...] += jnp.sum(df, axis=0, keepdims=True)

    rev = pl.BlockSpec((tb, LANES), lambda i: (nb - 1 - i, 0))
    return pl.pallas_call(
        body, name="fox_bwd", grid=(nb,),
        in_specs=[rev, rev, _full((1, LANES))],
        out_specs=[rev, _full((1, LANES))],
        out_shape=[jax.ShapeDtypeStruct((S, LANES), BF16), jax.ShapeDtypeStruct((1, LANES), F32)],
        scratch_shapes=[pltpu.VMEM((SUBLANES, LANES), F32)],
        compiler_params=_params(("arbitrary",)),
    )(dc, f, bias_row)


_DZ_WIDTHS = (D_HEADS,) * 5 + (LANES,)


def _in_bwd(pieces, w_in, x, g1, dx1, tm, sums16):
    S = x.shape[0]
    ns = len(sums16)

    def body(*refs):
        p_refs, (w_ref, x_ref, g_ref, dx1_ref) = refs[:6], refs[6:10]
        dx_ref, dg_ref = refs[10 + ns:12 + ns]
        scatter_start, scatter_finish = _scatter_ops(refs[10:10 + ns], refs[12 + ns:12 + 2 * ns], *refs[12 + 2 * ns:])

        @pl.when(pl.program_id(0) == 0)
        def _():
            dg_ref[...] = jnp.zeros_like(dg_ref)
            scatter_start()

        dh = jnp.zeros((tm, D_MODEL), F32)
        off = 0
        for p_ref, w in zip(p_refs, _DZ_WIDTHS):
            dh = dh + lax.dot_general(p_ref[...].astype(BF16), w_ref[:, off:off + w], NT, preferred_element_type=F32)
            off += w
        dx, dg = _rms_bwd(dh, x_ref[...], g_ref[...])
        dg_ref[...] += dg
        dx_ref[...] = dx1_ref[...] + dx

        @pl.when(pl.program_id(0) == pl.num_programs(0) - 1)
        def _():
            scatter_finish()

    row = lambda w: pl.BlockSpec((tm, w), lambda i: (i, 0))
    out = pl.pallas_call(
        body, name="in_bwd", grid=(S // tm,),
        in_specs=[row(w) for w in _DZ_WIDTHS] + [_full((D_MODEL, D_IN_PAD)), row(D_MODEL), _full((1, D_MODEL)), row(D_MODEL)]
        + [_ANY] * ns,
        out_specs=[row(D_MODEL), _full((1, D_MODEL))] + [_ANY] * ns,
        out_shape=[jax.ShapeDtypeStruct((S, D_MODEL), F32), jax.ShapeDtypeStruct((1, D_MODEL), F32)] + _scatter_shapes(sums16),
        scratch_shapes=_scatter_sems(ns),
        compiler_params=_params(("arbitrary",)),
    )(*pieces, w_in, x, g1, dx1, *sums16)
    return out[0], out[1], out[2:]


def _dw_in(h1, pieces, tk):
    S = h1.shape[0]

    def body(*refs):
        h_ref, p_refs, o_ref = refs[0], refs[1:7], refs[7]

        @pl.when(pl.program_id(0) == 0)
        def _():
            o_ref[...] = jnp.zeros_like(o_ref)

        off = 0
        for p_ref, w in zip(p_refs, _DZ_WIDTHS):
            o_ref[:, off:off + w] += lax.dot_general(h_ref[...], p_ref[...].astype(BF16), TN, preferred_element_type=F32)
            off += w

    row = lambda w: pl.BlockSpec((tk, w), lambda k: (k, 0))
    return pl.pallas_call(
        body, name="dw_in", grid=(S // tk,),
        in_specs=[row(D_MODEL)] + [row(w) for w in _DZ_WIDTHS],
        out_specs=_full((D_MODEL, D_IN_PAD)),
        out_shape=jax.ShapeDtypeStruct((D_MODEL, D_IN_PAD), F32),
        compiler_params=_params(("arbitrary",)),
    )(h1, *pieces)


def _adamw_math(w, g, m, v):
    m = ADAM_B1 * m + (1.0 - ADAM_B1) * g
    v = ADAM_B2 * v + (1.0 - ADAM_B2) * (g * g)
    m_hat = m / (1.0 - ADAM_B1 ** ADAM_STEP)
    v_hat = v / (1.0 - ADAM_B2 ** ADAM_STEP)
    delta = -ADAM_LR * (m_hat / (jnp.sqrt(v_hat) + ADAM_EPS) + ADAM_WD * w)
    return delta, m, v


def _adamw(name, w, g, m, v):
    R, C = w.shape
    tr = _row_tile(R, 256)

    def body(w_ref, g_ref, m_ref, v_ref, go_ref, d_ref, nm_ref, nv_ref):
        g = g_ref[...]
        d, nm, nv = _adamw_math(w_ref[...], g, m_ref[...], v_ref[...])
        go_ref[...] = g
        d_ref[...] = d
        nm_ref[...] = nm
        nv_ref[...] = nv

    spec = pl.BlockSpec((tr, C), lambda i: (i, 0))
    return pl.pallas_call(
        body, name=name, grid=(R // tr,), in_specs=[spec] * 4, out_specs=[spec] * 4,
        out_shape=[jax.ShapeDtypeStruct((R, C), F32)] * 4,
        compiler_params=_params(("parallel",)),
    )(w, g, m, v)


def _pair_sum(name, grad, theirs, ids):
    q, half, C = theirs.shape
    tr = _row_tile(half, 256)
    nb = half // tr

    def body(ids_ref, a_ref, b_ref, s_ref, sb_ref):
        s = a_ref[...] + b_ref[...]
        s_ref[...] = s
        sb_ref[...] = s.astype(BF16)

    here = pl.BlockSpec((None, tr, C), lambda j, i, ids: (j, i, 0))
    return pl.pallas_call(
        body, name=name,
        grid_spec=pltpu.PrefetchScalarGridSpec(
            num_scalar_prefetch=1, grid=(q, nb),
            in_specs=[pl.BlockSpec((None, tr, C), lambda j, i, ids: (j, ids[1] * nb + i, 0)), here],
            out_specs=[here, here]),
        out_shape=[jax.ShapeDtypeStruct((q, half, C), F32), jax.ShapeDtypeStruct((q, half, C), BF16)],
        compiler_params=_params(("parallel", "parallel")),
    )(ids, grad, theirs)


def _chip_sum(name, sums32, others, ids):
    _, half, C = sums32.shape
    tr = _row_tile(half, 256)
    nb = half // tr

    def body(ids_ref, a_ref, o_ref, s_ref):
        s = a_ref[...]
        for j in range(3):
            s = s + o_ref[j].astype(F32)
        s_ref[...] = s

    return pl.pallas_call(
        body, name=name,
        grid_spec=pltpu.PrefetchScalarGridSpec(
            num_scalar_prefetch=1, grid=(nb,),
            in_specs=[pl.BlockSpec((None, tr, C), lambda i, ids: (ids[0], i, 0)),
                      pl.BlockSpec((3, tr, C), lambda i, ids: (0, i, 0))],
            out_specs=pl.BlockSpec((tr, C), lambda i, ids: (ids[1] * nb + i, 0))),
        out_shape=jax.ShapeDtypeStruct((2 * half, C), F32),
        compiler_params=_params(("parallel",)),
    )(ids, sums32, others)


def _place():
    return lax.axis_index("x"), lax.axis_index("y"), lax.axis_index("c")


def _other_chips(x, y):
    return [(1 - x, y), (x, 1 - y), (1 - x, 1 - y)]


_ANY = pl.BlockSpec(memory_space=pl.ANY)


def _gather_quarters(shards):
    n = len(shards)

    def body(*refs):
        start, hand_on, finish = _gather_ops(refs[:n], refs[n:2 * n], *refs[2 * n:])
        start()
        hand_on()
        finish()

    return pl.pallas_call(
        body, name="gather_weights",
        in_specs=[_ANY] * n, out_specs=[_ANY] * n,
        out_shape=_gather_shapes(shards), scratch_shapes=_gather_sems(n),
    )(*shards)


def _gather_shapes(shards):
    return [jax.ShapeDtypeStruct((4,) + s.shape, s.dtype) for s in shards]


def _gather_sems(n):
    return [pltpu.SemaphoreType.DMA((n, 3))] * 4 + [pltpu.SemaphoreType.DMA((n,))]


def _gather_ops(ins, outs, send_sems, recv_sems, pass_send_sems, pass_recv_sems, own_sems):
    n = len(ins)
    halved = [r.shape[0] % 32 == 0 for r in ins]

    def part(a, quarter, core):
        if not halved[a]:
            return outs[a].at[quarter]
        half = ins[a].shape[0] // 2
        return outs[a].at[quarter, pl.ds(core * half, half), :]

    def ici(a, j, quarter):
        x, y, c = _place()
        px, py = _other_chips(x, y)[j]
        src = ins[a]
        if halved[a]:
            half = src.shape[0] // 2
            src = src.at[pl.ds(c * half, half), :]
        return pltpu.make_async_remote_copy(src_ref=src, dst_ref=part(a, quarter, c), send_sem=send_sems.at[a, j],
                                            recv_sem=recv_sems.at[a, j], device_id=(px, py, c), device_id_type=MESH)

    def passed(a, j, core):
        x, y, c = _place()
        px, py = _other_chips(x, y)[j]
        half = part(a, 2 * px + py, core)
        return pltpu.make_async_remote_copy(src_ref=half, dst_ref=half, send_sem=pass_send_sems.at[a, j],
                                            recv_sem=pass_recv_sems.at[a, j], device_id=(x, y, 1 - c), device_id_type=MESH)

    def own(a):
        x, y, _ = _place()
        return pltpu.make_async_copy(ins[a], outs[a].at[2 * x + y], own_sems.at[a])

    def start():
        x, y, _ = _place()
        for a in range(n):
            for j in range(3):
                ici(a, j, 2 * x + y).start()
            own(a).start()

    def hand_on():
        x, y, c = _place()
        for a in range(n):
            for j, (px, py) in enumerate(_other_chips(x, y)):
                ici(a, j, 2 * px + py).wait_recv()
                if halved[a]:
                    passed(a, j, c).start()

    def finish():
        x, y, c = _place()
        for a in range(n):
            for j in range(3):
                if halved[a]:
                    passed(a, j, 1 - c).wait_recv()
                    passed(a, j, c).wait_send()
                ici(a, j, 2 * x + y).wait_send()
            own(a).wait()

    return start, hand_on, finish


def _swap_halves(grads, name):
    n = len(grads)

    def body(*refs):
        start, finish = _swap_ops(refs[:n], refs[n:2 * n], *refs[2 * n:])
        start()
        finish()

    return pl.pallas_call(
        body, name=name,
        in_specs=[_ANY] * n, out_specs=[_ANY] * n, out_shape=_swap_shapes(grads), scratch_shapes=_swap_sems(n),
    )(*grads)


def _swap_shapes(grads):
    return [jax.ShapeDtypeStruct((4, g.shape[1] // 2, g.shape[2]), F32) for g in grads]


def _swap_sems(n):
    return [pltpu.SemaphoreType.DMA((n,))] * 2


def _swap_ops(ins, outs, send_sems, recv_sems):
    def copy(a):
        x, y, c = _place()
        half = ins[a].shape[1] // 2
        return pltpu.make_async_remote_copy(src_ref=ins[a].at[:, pl.ds((1 - c) * half, half), :], dst_ref=outs[a],
                                            send_sem=send_sems.at[a], recv_sem=recv_sems.at[a],
                                            device_id=(x, y, 1 - c), device_id_type=MESH)

    def start():
        for a in range(len(ins)):
            copy(a).start()

    def finish():
        for a in range(len(ins)):
            copy(a).wait()

    return start, finish


def _scatter_shapes(sums16):
    return [jax.ShapeDtypeStruct((3,) + s.shape[1:], BF16) for s in sums16]


def _scatter_sems(n):
    return [pltpu.SemaphoreType.DMA((n, 3))] * 2


def _scatter_ops(ins, outs, send_sems, recv_sems):
    n = len(ins)

    def copy(a, j):
        x, y, c = _place()
        px, py = _other_chips(x, y)[j]
        return pltpu.make_async_remote_copy(src_ref=ins[a].at[2 * px + py], dst_ref=outs[a].at[j], send_sem=send_sems.at[a, j],
                                            recv_sem=recv_sems.at[a, j], device_id=(px, py, c), device_id_type=MESH)

    def start():
        for a in range(n):
            for j in range(3):
                copy(a, j).start()

    def finish():
        for a in range(n):
            for j in range(3):
                copy(a, j).wait()

    return start, finish


def _join_halves(fulls):
    n = len(fulls)

    def body(*refs):
        ins, outs = refs[:n], refs[n:2 * n]
        send_sems, recv_sems = refs[2 * n:]
        x, y, c = _place()
        started = []
        for a in range(n):
            half = ins[a].shape[0] // 2
            rows = pl.ds(c * half, half)
            cp = pltpu.make_async_remote_copy(src_ref=ins[a].at[rows, :], dst_ref=outs[a].at[rows, :], send_sem=send_sems.at[a],
                                              recv_sem=recv_sems.at[a], device_id=(x, y, 1 - c), device_id_type=MESH)
            cp.start()
            started.append(cp)
        for cp in started:
            cp.wait()

    return pl.pallas_call(
        body, name="join_halves",
        in_specs=[_ANY] * n, out_specs=[_ANY] * n,
        out_shape=[jax.ShapeDtypeStruct(f.shape, F32) for f in fulls],
        input_output_aliases={a: a for a in range(n)},
        scratch_shapes=[pltpu.SemaphoreType.DMA((n,)), pltpu.SemaphoreType.DMA((n,))],
    )(*fulls)


def _small_allreduce(g):
    R = g.shape[0]
    half = R // 2

    def body(g_ref, out_ref, other_s, chip_s, parts_s, send_sems, recv_sems):
        x, y, c = _place()
        mine = 2 * x + y
        rows = pl.ds(pl.multiple_of(c * half, SUBLANES), half)

        def to_other_core(src, dst, k):
            return pltpu.make_async_remote_copy(src_ref=src, dst_ref=dst, send_sem=send_sems.at[k], recv_sem=recv_sems.at[k],
                                                device_id=(x, y, 1 - c), device_id_type=MESH)

        swap = to_other_core(g_ref, other_s, 0)
        swap.start()
        swap.wait()
        chip_s[...] = g_ref[...] + other_s[...]
        parts_s[mine] = chip_s[rows, :]
        sends = []
        for j, (px, py) in enumerate(_other_chips(x, y)):
            cp = pltpu.make_async_remote_copy(src_ref=chip_s.at[rows, :], dst_ref=parts_s.at[mine], send_sem=send_sems.at[1 + j],
                                              recv_sem=recv_sems.at[1 + j], device_id=(px, py, c), device_id_type=MESH)
            cp.start()
            sends.append(cp)
        for cp in sends:
            cp.wait()
        out_ref[rows, :] = (parts_s[0] + parts_s[1]) + (parts_s[2] + parts_s[3])
        join = to_other_core(out_ref.at[rows, :], out_ref.at[rows, :], 4)
        join.start()
        join.wait()

    vm = pl.BlockSpec(memory_space=pltpu.VMEM)
    return pl.pallas_call(
        body, name="small_allreduce",
        in_specs=[vm], out_specs=vm, out_shape=jax.ShapeDtypeStruct((R, LANES), F32),
        scratch_shapes=[pltpu.VMEM((R, LANES), F32), pltpu.VMEM((R, LANES), F32), pltpu.VMEM((4, half, LANES), F32),
                        pltpu.SemaphoreType.DMA((5,)), pltpu.SemaphoreType.DMA((5,))],
        compiler_params=pltpu.CompilerParams(vmem_limit_bytes=VMEM_LIMIT),
    )(g)


def _adamw_small(ws, gs, ms, vs):
    n = len(ws)

    def body(*refs):
        for k in range(n):
            w_ref, g_ref, m_ref, v_ref = (refs[j * n + k] for j in range(4))
            d, nm, nv = _adamw_math(w_ref[...], g_ref[...], m_ref[...], v_ref[...])
            refs[4 * n + k][...] = d
            refs[5 * n + k][...] = nm
            refs[6 * n + k][...] = nv

    vm = pl.BlockSpec(memory_space=pltpu.VMEM)
    out = pl.pallas_call(
        body, name="adamw_small",
        in_specs=[vm] * (4 * n), out_specs=[vm] * (3 * n),
        out_shape=[jax.ShapeDtypeStruct(w.shape, F32) for w in ws] * 3,
        compiler_params=pltpu.CompilerParams(vmem_limit_bytes=VMEM_LIMIT),
    )(*ws, *gs, *ms, *vs)
    return out[:n], out[n:2 * n], out[2 * n:]


_SMALL = (("norm_mix_g", D_MODEL), ("f_bias", N_HEADS), ("sg_ln_g", D_HEADS), ("sg_w", N_HEADS * SG_BLOCK * SG_BLOCK),
          ("sg_b", N_HEADS * SG_BLOCK), ("norm_ffn_g", D_MODEL), ("w_conv", 3 * 2 * D_FF), ("b_conv", 2 * D_FF),
          ("norm_final_g", D_MODEL))
_PACKED = _SMALL + (("sq_err", D_MODEL),)


def _pack_small(parts):
    rows = []
    for name, size in _PACKED:
        flat = parts[name].reshape(-1).astype(F32)
        pad = (-size) % (SUBLANES * LANES)
        rows.append(jnp.pad(flat, (0, pad)).reshape(-1, LANES))
    packed = jnp.concatenate(rows, axis=0)
    return jnp.pad(packed, ((0, (-packed.shape[0]) % (2 * SUBLANES)), (0, 0)))


def _unpack_small(packed, shapes):
    out, r = {}, 0
    for name, size in _PACKED:
        nrows = (size + SUBLANES * LANES - 1) // (SUBLANES * LANES) * SUBLANES
        out[name] = packed[r:r + nrows].reshape(-1)[:size].reshape(shapes[name])
        r += nrows
    return out


def _local_step(x, target, g1, w_in, f_bias, sg_ln_g, sg_w, sg_b, g2, b_conv, g3, late_shards, ids):
    S = x.shape[0]
    tm = _row_tile(S, 512)
    tms = _row_tile(S, 256)
    tq = tm

    lane = jnp.arange(D_HEADS)
    seg_avg = jnp.where(lane[:, None] // HEAD_DIM == lane[None, :] // HEAD_DIM, 1.0 / HEAD_DIM, 0.0).astype(BF16)
    head_ind = (lane[:, None] // HEAD_DIM == jnp.arange(LANES)[None, :]).astype(BF16)
    pos_chunk = jnp.arange(SG_BLOCK) // CHUNK
    w_mask32 = jnp.where(pos_chunk[:, None] >= pos_chunk[None, :], sg_w, 0.0)
    w_mask = w_mask32.astype(BF16)
    w_mask_t = jnp.swapaxes(w_mask32, 1, 2).astype(BF16)
    ln_row = sg_ln_g.reshape(1, D_HEADS)
    b_full = jnp.repeat(sg_b.T, HEAD_DIM, axis=1)
    bias_row = jnp.pad(f_bias.reshape(1, N_HEADS), ((0, 0), (0, LANES - N_HEADS)))
    b_conv_row = b_conv.reshape(1, 2 * D_FF)

    z, f, h1 = _in_proj(x, g1, w_in, tm)
    c = _fox_prep(f, bias_row, tm)
    consts = _attn_consts()
    qa, ka, va, vat = _attn_pack(z, c, consts, tm)
    out_b, lse, gathered = _attn_fwd(qa, ka, vat, consts["place_t"], tq, late_shards)
    g_out, w_up_q, g_down, g_conv = gathered
    w_out = g_out.reshape(D_MODEL, D_MODEL)
    w_down = g_down.reshape(D_FF, D_MODEL)
    w_conv = jnp.concatenate([g_conv[q] for q in range(4)], axis=1)
    out_a = _gate_fwd(z, w_mask, ln_row, b_full, seg_avg, tm)
    x1, h2 = _mix_out(x, out_a, out_b, w_out, g2, tm)
    a = _up_proj(h2, w_up_q, tm)
    dx2, sq_err, dg3 = _ffn_fwd_loss(a, w_conv, b_conv_row, w_down, x1, g3, target, tms)

    dconv, y, dw_conv8, db_conv = _ffn_bwd_gate(dx2, a, w_conv, b_conv_row, w_down, tms)
    dact = _conv_bwd(dconv, w_conv, tms, D_FF)
    dw_down = _matmul_tn(y, dx2, "dw_down", D_FF // 2, D_MODEL, tm, quarters=(2, 1))
    dx1, dg2 = _up_bwd(dact, w_up_q, x1, g2, dx2, tm)
    dw_up_q = _matmul_tn(h2, dact, "dw_up", D_MODEL, 2 * D_FF // 4, tm, quarters=(1, 4))
    dcat = _out_bwd(dx1, w_out, tm)
    dw_out = _dw_out(out_a, out_b, dx1, tm)
    early = {"w_down": dw_down.reshape(4, D_FF // 4, D_MODEL), "w_up": dw_up_q,
             "w_out": dw_out.reshape(4, D_MODEL // 4, D_MODEL)}
    (dzu, dzv, dsg_w, dsg_b_t, dln), theirs = _gate_bwd(z, dcat, w_mask, w_mask_t, ln_row, b_full, seg_avg, head_ind, tm,
                                                        list(early.values()))
    early_sums = _chip_sums(early, theirs, ids)
    dop, qb, dopt, qbt = _attn_pack_grad(out_b, dcat, qa, lse, head_ind, consts, tm)
    dqp, dc_rows, dk, dv, dc_cols, landed = _attn_bwd(qb, qbt, ka, va, dop, dopt, consts, tq,
                                                      [s16 for _, s16 in early_sums.values()])
    early_parts = {k: (s32, got) for (k, (s32, _)), got in zip(early_sums.items(), landed)}
    dq = _attn_unpack(dqp, consts, tm)
    dc_rows = jnp.concatenate([dc_rows[g][:, :GROUP_HEADS] for g in range(GROUPS)], axis=1)
    dc_cols = jnp.concatenate([dc_cols[g][:GROUP_HEADS] for g in range(GROUPS)], axis=0).T
    dc = jnp.pad(dc_rows - dc_cols, ((0, 0), (0, LANES - N_HEADS)))
    df, dbias = _fox_bwd(dc, f, bias_row, tm)
    pieces = (dzu, dzv, dq, dk, dv, df)
    dw_in = _dw_in(h1, pieces, tm)[:, :D_IN].reshape(D_MODEL, 4, D_IN // 4).transpose(1, 0, 2)
    (w_in_sum, w_in_sum16), = _chip_sums({"w_in": dw_in}, _swap_halves([dw_in], "swap_halves"), ids).values()
    dx, dg1, (w_in_landed,) = _in_bwd(pieces, w_in, x, g1, dx1, tm, [w_in_sum16])

    grads = {
        "norm_mix_g": dg1, "f_bias": dbias[:, :N_HEADS], "sg_ln_g": dln, "sg_w": dsg_w, "sg_b": dsg_b_t[:, :N_HEADS].T,
        "norm_ffn_g": dg2, "w_conv": dw_conv8[:3], "b_conv": db_conv, "norm_final_g": dg3,
    }
    return sq_err, dx, grads, {**early_parts, "w_in": (w_in_sum, w_in_landed)}


def _chip_sums(grads_q, theirs, ids):
    return {k: _pair_sum("pair_sum_" + k, g, t, ids) for (k, g), t in zip(grads_q.items(), theirs)}


def _finish_reduction(parts, ids):
    names = list(parts)
    fulls = [_chip_sum("chip_sum_" + k, s32, got, ids) for k, (s32, got) in parts.items()]
    return dict(zip(names, _join_halves(fulls)))


def kernel(x, norm_mix_g, w_in, f_bias, sg_ln_g, sg_w, sg_b, w_out, norm_ffn_g, w_up, w_conv, b_conv, w_down, norm_final_g, loss_target, m_norm_mix_g, m_w_in, m_f_bias, m_sg_ln_g, m_sg_w, m_sg_b, m_w_out, m_norm_ffn_g, m_w_up, m_w_conv, m_b_conv, m_w_down, m_norm_final_g, v_norm_mix_g, v_w_in, v_f_bias, v_sg_ln_g, v_sg_w, v_sg_b, v_w_out, v_norm_ffn_g, v_w_up, v_w_conv, v_b_conv, v_w_down, v_norm_final_g):
    args = dict(locals())
    quarter = 2 * lax.axis_index("x") + lax.axis_index("y")
    ids = jnp.stack([quarter, lax.axis_index("c")]).astype(jnp.int32)
    wq_conv = w_conv.shape[-1]

    g_in = _gather_quarters([w_in[0].astype(BF16)])[0]
    w_in_full = jnp.pad(jnp.concatenate([g_in[q] for q in range(4)], axis=1), ((0, 0), (0, D_IN_PAD - D_IN)))
    late_shards = [w_out[0].astype(BF16), w_up[0].astype(BF16), w_down[0].astype(BF16), w_conv[0]]

    sq_err, dx, grads, parts = _local_step(
        x[0], loss_target[0], norm_mix_g, w_in_full, f_bias[0], sg_ln_g[0], sg_w[0], sg_b[0], norm_ffn_g, b_conv[0],
        norm_final_g.reshape(1, D_MODEL), late_shards, ids)
    big = _finish_reduction(parts, ids)

    out = {"grad_x": dx[None]}
    for k in ("w_in", "w_out", "w_up", "w_down"):
        g, d, nm, nv = _adamw("adamw_" + k, args[k][0], big[k], args["m_" + k][0], args["v_" + k][0])
        out["grad_" + k], out["delta_" + k], out["new_m_" + k], out["new_v_" + k] = g[None], d[None], nm[None], nv[None]

    small_names = [n for n, _ in _SMALL]
    shapes = {n: (3, 4 * wq_conv) if n == "w_conv" else args[n].shape for n in small_names}
    shapes["sq_err"] = sq_err.shape
    g_small = _unpack_small(_small_allreduce(_pack_small({**{n: grads[n] for n in small_names}, "sq_err": sq_err})), shapes)
    out["loss"] = 0.5 * jnp.sum(g_small.pop("sq_err")) / D_MODEL
    g_small["w_conv"] = lax.dynamic_slice(g_small["w_conv"], (0, quarter * wq_conv), (3, wq_conv))[None]
    flat2d = lambda t: t.reshape(-1, t.shape[-1])
    updated = _adamw_small(*[[flat2d(src[p + n]) for n in small_names] for src, p in
                             ((args, ""), (g_small, ""), (args, "m_"), (args, "v_"))])
    for n, g in g_small.items():
        out["grad_" + n] = g
    for prefix, arrs in zip(("delta_", "new_m_", "new_v_"), updated):
        for n, t in zip(small_names, arrs):
            out[prefix + n] = t.reshape(args[n].shape)

    weights = ["norm_mix_g", "w_in", "f_bias", "sg_ln_g", "sg_w", "sg_b", "w_out", "norm_ffn_g", "w_up", "w_conv", "b_conv",
               "w_down", "norm_final_g"]
    return (out["loss"], out["grad_x"], *[out[p + n] for p in ("grad_", "delta_", "new_m_", "new_v_") for n in weights])
```

```python
import functools
import math

import jax
import jax.numpy as jnp
from jax import lax
from jax.experimental import pallas as pl
from jax.experimental.pallas import tpu as pltpu

F32 = jnp.float32
BF16 = jnp.bfloat16
MESH = pl.DeviceIdType.MESH

D_MODEL = 1024
N_HEADS = 8
HEAD_DIM = 64
D_HEADS = N_HEADS * HEAD_DIM
SG_BLOCK = 128
CHUNK = 64
D_FF = 2816
D_IN = 2 * D_HEADS + 3 * D_HEADS + N_HEADS
LANES = 128
SUBLANES = 8
D_IN_PAD = 5 * D_HEADS + LANES
EPS = 1e-6
SCALE = HEAD_DIM ** -0.5
NEG = -1e30
LOG2E = 1.4426950408889634
HEAD_PAD = LANES
D_PAD = N_HEADS * HEAD_PAD
Q_STAT = HEAD_DIM
K_STAT = HEAD_DIM + 3
L_STAT = HEAD_DIM + 6
GROUPS = 2
GROUP_HEADS = N_HEADS // GROUPS
GROUP_PAD = GROUP_HEADS * HEAD_PAD
KEY_CHUNK = 256
FWD_KEY_CHUNK = 512
STAT_ROWS = 16
FF_CHUNK = 256

ADAM_LR = 0.001
ADAM_B1 = 0.9
ADAM_B2 = 0.999
ADAM_EPS = 1e-08
ADAM_WD = 0.01
ADAM_STEP = 10

VMEM_LIMIT = 56 * 1024 * 1024

NT = (((1,), (1,)), ((), ()))
TN = (((0,), (0,)), ((), ()))


def _params(sem):
    return pltpu.CompilerParams(dimension_semantics=sem, vmem_limit_bytes=VMEM_LIMIT)


def _full(shape):
    nd = len(shape)
    return pl.BlockSpec(shape, lambda *_: (0,) * nd)


def _row_tile(rows, target):
    best = None
    for t in range(SUBLANES, min(rows, target) + 1, SUBLANES):
        if rows % t == 0:
            best = t
    assert best is not None, rows
    return best


def _sigmoid(x):
    return 0.5 * jnp.tanh(0.5 * x) + 0.5


def _gelu(z):
    return 0.5 * z * (1.0 + lax.erf(z * (2.0 ** -0.5)))


def _gelu_grad(z):
    cdf = 0.5 * (1.0 + lax.erf(z * (2.0 ** -0.5)))
    pdf = jnp.exp(-0.5 * z * z) * (1.0 / math.sqrt(2.0 * math.pi))
    return cdf + z * pdf


def _split_dot(x, m):
    hi = x.astype(BF16)
    lo = (x - hi.astype(F32)).astype(BF16)
    return jnp.dot(hi, m, preferred_element_type=F32) + jnp.dot(lo, m, preferred_element_type=F32)


def _head_mask(h, rows):
    lane = lax.broadcasted_iota(jnp.int32, (rows, D_HEADS), 1)
    return (lane >= h * HEAD_DIM) & (lane < (h + 1) * HEAD_DIM)


def _rms_bwd(dh, x, g):
    r = lax.rsqrt(jnp.mean(x * x, axis=-1, keepdims=True) + EPS)
    xhat = x * r
    dg = jnp.sum(dh * xhat, axis=0, keepdims=True)
    dxhat = dh * g
    dx = r * (dxhat - xhat * jnp.mean(dxhat * xhat, axis=-1, keepdims=True))
    return dx, dg


def _in_proj(x, g1, w_in, tm):
    S = x.shape[0]
    nz = D_IN_PAD - LANES

    def body(x_ref, g_ref, w_ref, z_ref, f_ref, h_ref):
        xf = x_ref[...]
        r = lax.rsqrt(jnp.mean(xf * xf, axis=-1, keepdims=True) + EPS)
        h = (xf * r * g_ref[...]).astype(BF16)
        h_ref[...] = h
        zz = jnp.dot(h, w_ref[...], preferred_element_type=F32)
        z_ref[...] = zz[:, :nz].astype(BF16)
        f_ref[...] = zz[:, nz:]

    return pl.pallas_call(
        body, name="in_proj", grid=(S // tm,),
        in_specs=[pl.BlockSpec((tm, D_MODEL), lambda i: (i, 0)), _full((1, D_MODEL)), _full((D_MODEL, D_IN_PAD))],
        out_specs=[pl.BlockSpec((tm, nz), lambda i: (i, 0)), pl.BlockSpec((tm, LANES), lambda i: (i, 0)),
                   pl.BlockSpec((tm, D_MODEL), lambda i: (i, 0))],
        out_shape=[jax.ShapeDtypeStruct((S, nz), BF16), jax.ShapeDtypeStruct((S, LANES), F32),
                   jax.ShapeDtypeStruct((S, D_MODEL), BF16)],
        compiler_params=_params(("parallel",)),
    )(x, g1, w_in)


def _fox_prep(f, bias_row, tb):
    S = f.shape[0]

    def body(f_ref, b_ref, c_ref, carry):
        @pl.when(pl.program_id(0) == 0)
        def _():
            carry[...] = jnp.zeros_like(carry)

        xv = f_ref[...] + b_ref[...]
        lf = jnp.minimum(xv, 0.0) - jnp.log(1.0 + jnp.exp(-jnp.abs(xv)))
        r = lax.broadcasted_iota(jnp.int32, (tb, tb), 0)
        s = lax.broadcasted_iota(jnp.int32, (tb, tb), 1)
        tri = (r >= s).astype(F32)
        cs = jnp.dot(tri, lf, precision=lax.Precision.HIGHEST, preferred_element_type=F32) + carry[0:1, :]
        c_ref[...] = cs
        carry[...] = jnp.broadcast_to(cs[tb - 1:tb, :], carry.shape)

    return pl.pallas_call(
        body, name="fox_prep", grid=(S // tb,),
        in_specs=[pl.BlockSpec((tb, LANES), lambda i: (i, 0)), _full((1, LANES))],
        out_specs=pl.BlockSpec((tb, LANES), lambda i: (i, 0)),
        out_shape=jax.ShapeDtypeStruct((S, LANES), F32),
        scratch_shapes=[pltpu.VMEM((SUBLANES, LANES), F32)],
        compiler_params=_params(("arbitrary",)),
    )(f, bias_row)


def _attn_consts():
    col = jnp.arange(D_PAD)
    row = jnp.arange(D_HEADS)
    head = jnp.arange(LANES)
    place = (row[:, None] // HEAD_DIM == col[None, :] // HEAD_PAD) & (row[:, None] % HEAD_DIM == col[None, :] % HEAD_PAD)

    def stat(offset):
        return ((head[:, None] < N_HEADS) & (col[None, :] == head[:, None] * HEAD_PAD + offset)).astype(BF16)

    def stat3(base):
        part, h = head // N_HEADS, head % N_HEADS
        return ((part[:, None] < 3) & (col[None, :] == h[:, None] * HEAD_PAD + base + part[:, None])).astype(BF16)

    def ones(offsets):
        return sum((col % HEAD_PAD == o) for o in offsets).astype(F32).reshape(1, D_PAD)

    place = place.astype(BF16)
    return {
        "place": place, "place_t": place.T, "place_t_group": place.T[:GROUP_PAD, :GROUP_HEADS * HEAD_DIM],
        "q_stat": stat3(Q_STAT), "k_stat": stat3(K_STAT),
        "d_stat": stat3(Q_STAT) * (head[:, None] < 2 * N_HEADS).astype(BF16),
        "l_stat": jnp.concatenate([stat(L_STAT + j)[:STAT_ROWS] for j in range(3)], axis=0),
        "q_ones": ones(range(K_STAT, K_STAT + 3)), "k_ones": ones(list(range(Q_STAT, Q_STAT + 3)) + list(range(L_STAT, L_STAT + 3))),
        "v_ones": ones(range(Q_STAT, Q_STAT + 2)),
    }


def _split3(x):
    hi = x.astype(BF16)
    r = x - hi.astype(F32)
    mid = r.astype(BF16)
    return hi, mid, (r - mid.astype(F32)).astype(BF16)


def _attn_pack(z, c, k, tm):
    S = z.shape[0]

    def body(q_ref, k_ref, v_ref, c_ref, pl_ref, pt_ref, qs_ref, ks_ref, qo_ref, ko_ref, vo_ref, voc_ref,
             qa_ref, ka_ref, va_ref, vt_ref):
        place = pl_ref[...]
        q = (q_ref[...].astype(F32) * (SCALE * LOG2E)).astype(BF16)
        qa = jnp.dot(q, place, preferred_element_type=F32) + qo_ref[...]
        ka = jnp.dot(k_ref[...], place, preferred_element_type=F32) + ko_ref[...]
        lane = lax.broadcasted_iota(jnp.int32, (tm, LANES), 1)
        hi, mid, lo = _split3(jnp.where(lane < N_HEADS, c_ref[...] * LOG2E, 0.0))
        parts = hi.astype(F32) + pltpu.roll(mid.astype(F32), N_HEADS, 1) + pltpu.roll(lo.astype(F32), 2 * N_HEADS, 1)
        parts = parts.astype(BF16)
        qa = qa + jnp.dot(parts, qs_ref[...], preferred_element_type=F32)
        ka = ka - jnp.dot(parts, ks_ref[...], preferred_element_type=F32)
        qa_ref[...] = qa.astype(BF16)
        ka_ref[...] = ka.astype(BF16)
        v = v_ref[...]
        va_ref[...] = (jnp.dot(v, place, preferred_element_type=F32) + vo_ref[...]).astype(BF16)
        vt_ref[...] = (lax.dot_general(pt_ref[...], v, NT, preferred_element_type=F32) + voc_ref[...]).astype(BF16)

    blk = lambda col: pl.BlockSpec((tm, D_HEADS), lambda i: (i, col))
    out = pl.BlockSpec((tm, D_PAD), lambda i: (i, 0))
    pad = jax.ShapeDtypeStruct((S, D_PAD), BF16)
    return pl.pallas_call(
        body, name="attn_pack", grid=(S // tm,),
        in_specs=[blk(2), blk(3), blk(4), pl.BlockSpec((tm, LANES), lambda i: (i, 0)), _full((D_HEADS, D_PAD)), _full((D_PAD, D_HEADS)),
                  _full((LANES, D_PAD)), _full((LANES, D_PAD)), _full((1, D_PAD)), _full((1, D_PAD)), _full((1, D_PAD)),
                  _full((D_PAD, 1))],
        out_specs=[out, out, out, pl.BlockSpec((None, D_PAD, tm), lambda i: (i, 0, 0))],
        out_shape=[pad, pad, pad, jax.ShapeDtypeStruct((S // tm, D_PAD, tm), BF16)],
        compiler_params=_params(("parallel",)),
    )(z, z, z, c, k["place"], k["place_t"], k["q_stat"], k["k_stat"], k["q_ones"], k["k_ones"], k["v_ones"], k["v_ones"].T)


def _attn_fwd(qa, ka, vat, place_t, tq, shards):
    S = qa.shape[0]
    n = S // tq
    ns = len(shards)
    hand_on_at = (2 * n) // 3

    pairs = [(q, k) for q in range(n) for k in range(q + 1)]
    q_of = jnp.asarray([q for q, _ in pairs], jnp.int32)
    k_of = jnp.asarray([k for _, k in pairs], jnp.int32)

    def body(q_of_ref, k_of_ref, q_ref, k_ref, vt_ref, pt_ref, *rest):
        o_ref, lse_ref = rest[ns:ns + 2]
        m_s, acc_s, ot_s = rest[2 * ns + 2:2 * ns + 5]
        s_s = rest[2 * ns + 5:2 * ns + 7]
        start, hand_on, finish = _gather_ops(rest[:ns], rest[ns + 2:2 * ns + 2], *rest[2 * ns + 7:])
        qi, ki = q_of_ref[pl.program_id(0)], k_of_ref[pl.program_id(0)]

        @pl.when((qi == 0) & (ki == 0))
        def _():
            start()

        @pl.when((qi == hand_on_at) & (ki == 0))
        def _():
            hand_on()

        @pl.when(ki == 0)
        def _():
            m_s[...] = jnp.full_like(m_s, NEG)
            acc_s[...] = jnp.zeros_like(acc_s)

        def step(diagonal):
            kc = FWD_KEY_CHUNK
            chunks = [slice(c * kc, (c + 1) * kc) for c in range(tq // kc)]

            def scores(h, rows, slot):
                sl = slice(h * HEAD_PAD, (h + 1) * HEAD_PAD)
                st = lax.dot_general(k_ref[rows, sl], q_ref[:, sl], NT, preferred_element_type=F32)
                if diagonal:
                    key = rows.start + lax.broadcasted_iota(jnp.int32, (kc, tq), 0)
                    query = lax.broadcasted_iota(jnp.int32, (kc, tq), 1)
                    st = jnp.where(query >= key, st, NEG)
                s_s[slot][rows, :] = st
                return jnp.max(st, axis=0, keepdims=True)

            m_cur = functools.reduce(jnp.maximum, [scores(0, rows, 0) for rows in chunks])
            for h in range(N_HEADS):
                sl = slice(h * HEAD_PAD, (h + 1) * HEAD_PAD)
                slot = h % 2
                m_prev = m_s[h][0:1, :]
                m_new = jnp.maximum(m_prev, m_cur)
                acc = jnp.exp2(m_prev - m_new) * acc_s[h]
                m_next = []
                for rows in chunks:
                    if h + 1 < N_HEADS:
                        m_next.append(scores(h + 1, rows, 1 - slot))
                    pt = jnp.exp2(s_s[slot][rows, :] - m_new).astype(BF16)
                    acc = acc + jnp.dot(vt_ref[sl, rows], pt, preferred_element_type=F32)
                acc_s[h] = acc
                m_s[h] = jnp.broadcast_to(m_new, (SUBLANES, tq))
                if m_next:
                    m_cur = functools.reduce(jnp.maximum, m_next)

        @pl.when(ki < qi)
        def _():
            step(False)

        @pl.when(ki == qi)
        def _():
            step(True)
            lse_ref[...] = jnp.zeros_like(lse_ref)
            for h in range(N_HEADS):
                acc = acc_s[h]
                denom = acc[Q_STAT:Q_STAT + 1, :]
                ot_s[h * HEAD_PAD:(h + 1) * HEAD_PAD, :] = (acc / denom).astype(BF16)
                lse_ref[h:h + 1, :] = m_s[h][0:1, :] + jnp.log(denom) * LOG2E
            o_ref[...] = lax.dot_general(ot_s[...], pt_ref[...], TN, preferred_element_type=F32).astype(BF16)

        @pl.when((qi == n - 1) & (ki == n - 1))
        def _():
            finish()

    out = pl.pallas_call(
        body, name="attn_fwd",
        grid_spec=pltpu.PrefetchScalarGridSpec(
            num_scalar_prefetch=2, grid=(len(pairs),),
            in_specs=[pl.BlockSpec((tq, D_PAD), lambda i, qs, ks: (qs[i], 0)),
                      pl.BlockSpec((tq, D_PAD), lambda i, qs, ks: (ks[i], 0)),
                      pl.BlockSpec((None, D_PAD, tq), lambda i, qs, ks: (ks[i], 0, 0)),
                      pl.BlockSpec((D_PAD, D_HEADS), lambda i, qs, ks: (0, 0))]
            + [_ANY] * ns,
            out_specs=[pl.BlockSpec((tq, D_HEADS), lambda i, qs, ks: (qs[i], 0)),
                       pl.BlockSpec((STAT_ROWS, tq), lambda i, qs, ks: (0, qs[i]))] + [_ANY] * ns,
            scratch_shapes=[pltpu.VMEM((N_HEADS, SUBLANES, tq), F32), pltpu.VMEM((N_HEADS, HEAD_PAD, tq), F32),
                            pltpu.VMEM((D_PAD, tq), BF16), pltpu.VMEM((tq, tq), F32), pltpu.VMEM((tq, tq), F32)] + _gather_sems(ns)),
        out_shape=[jax.ShapeDtypeStruct((S, D_HEADS), BF16), jax.ShapeDtypeStruct((STAT_ROWS, S), F32)] + _gather_shapes(shards),
        compiler_params=_params(("arbitrary",)),
    )(q_of, k_of, qa, ka, vat, place_t, *shards)
    return out[0], out[1], out[2:]


def _layer_norm_heads(v, seg_avg):
    mu = _split_dot(v, seg_avg)
    d = v - mu
    var = _split_dot(d * d, seg_avg)
    rstd = lax.rsqrt(var + EPS)
    return d * rstd, rstd


def _gate_mix(vn_blk, w_ref, bias):
    acc = bias
    for h in range(N_HEADS):
        vh = jnp.where(_head_mask(h, SG_BLOCK), vn_blk, 0.0).astype(BF16)
        acc = acc + jnp.dot(w_ref[h], vh, preferred_element_type=F32)
    return acc


def _gate_fwd(z, w_mask, ln_row, b_full, seg_avg, tm):
    S = z.shape[0]

    def body(zu_ref, zv_ref, w_ref, ln_ref, b_ref, avg_ref, o_ref):
        u = _gelu(zu_ref[...].astype(F32))
        v = _gelu(zv_ref[...].astype(F32))
        vhat, _ = _layer_norm_heads(v, avg_ref[...])
        vn = vhat * ln_ref[...]
        for b in range(tm // SG_BLOCK):
            rows = slice(b * SG_BLOCK, (b + 1) * SG_BLOCK)
            mixed = _gate_mix(vn[rows], w_ref, b_ref[...])
            o_ref[rows, :] = (u[rows] * mixed).astype(BF16)

    return pl.pallas_call(
        body, name="gate_fwd", grid=(S // tm,),
        in_specs=[pl.BlockSpec((tm, D_HEADS), lambda i: (i, 0)), pl.BlockSpec((tm, D_HEADS), lambda i: (i, 1)),
                  _full((N_HEADS, SG_BLOCK, SG_BLOCK)), _full((1, D_HEADS)), _full((SG_BLOCK, D_HEADS)),
                  _full((D_HEADS, D_HEADS))],
        out_specs=pl.BlockSpec((tm, D_HEADS), lambda i: (i, 0)),
        out_shape=jax.ShapeDtypeStruct((S, D_HEADS), BF16),
        compiler_params=_params(("parallel",)),
    )(z, z, w_mask, ln_row, b_full, seg_avg)


def _mix_out(x, out_a, out_b, w_out, g2, tm):
    S = x.shape[0]

    def body(x_ref, a_ref, b_ref, w_ref, g_ref, x1_ref, h_ref):
        y = jnp.dot(a_ref[...], w_ref[:D_HEADS, :], preferred_element_type=F32)
        y = y + jnp.dot(b_ref[...], w_ref[D_HEADS:, :], preferred_element_type=F32)
        x1 = x_ref[...] + y
        x1_ref[...] = x1
        r = lax.rsqrt(jnp.mean(x1 * x1, axis=-1, keepdims=True) + EPS)
        h_ref[...] = (x1 * r * g_ref[...]).astype(BF16)

    row = lambda w: pl.BlockSpec((tm, w), lambda i: (i, 0))
    return pl.pallas_call(
        body, name="mix_out", grid=(S // tm,),
        in_specs=[row(D_MODEL), row(D_HEADS), row(D_HEADS), _full((D_MODEL, D_MODEL)), _full((1, D_MODEL))],
        out_specs=[row(D_MODEL), row(D_MODEL)],
        out_shape=[jax.ShapeDtypeStruct((S, D_MODEL), F32), jax.ShapeDtypeStruct((S, D_MODEL), BF16)],
        compiler_params=_params(("parallel",)),
    )(x, out_a, out_b, w_out, g2)


def _up_proj(h2, w_up_q, tm):
    S = h2.shape[0]
    nq, _, wq = w_up_q.shape

    def body(h_ref, w_ref, a_ref):
        a_ref[...] = jnp.dot(h_ref[...], w_ref[...], preferred_element_type=F32).astype(BF16)

    return pl.pallas_call(
        body, name="up_proj", grid=(nq, S // tm),
        in_specs=[pl.BlockSpec((tm, D_MODEL), lambda j, i: (i, 0)), pl.BlockSpec((None, D_MODEL, wq), lambda j, i: (j, 0, 0))],
        out_specs=pl.BlockSpec((tm, wq), lambda j, i: (i, j)),
        out_shape=jax.ShapeDtypeStruct((S, nq * wq), BF16),
        compiler_params=_params(("parallel", "parallel")),
    )(h2, w_up_q)


def _shift_down(a, halo, k):
    tm = a.shape[0]
    ra = pltpu.roll(a, k, 0)
    rh = pltpu.roll(halo, k, 0)
    row = lax.broadcasted_iota(jnp.int32, halo.shape, 0)
    top = jnp.where(row < k, rh, ra[0:SUBLANES])
    return jnp.concatenate([top, ra[SUBLANES:tm]], axis=0)


def _shift_up(a, halo, k):
    tm = a.shape[0]
    ra = pltpu.roll(a, tm - k, 0)
    rh = pltpu.roll(halo, SUBLANES - k, 0)
    row = lax.broadcasted_iota(jnp.int32, halo.shape, 0)
    bottom = jnp.where(row >= SUBLANES - k, rh, ra[tm - SUBLANES:tm])
    return jnp.concatenate([ra[0:tm - SUBLANES], bottom], axis=0)


def _shift_matrices(tm):
    row = lax.broadcasted_iota(jnp.int32, (tm, tm), 0)
    col = lax.broadcasted_iota(jnp.int32, (tm, tm), 1)
    return [(row == col + k).astype(BF16) for k in (1, 2)]


def _conv_taps(a, halo, first, shifts):
    tm = a.shape[0]
    halo = halo.astype(F32) * jnp.where(first, 0.0, 1.0)
    if shifts is None:
        a = a.astype(F32)
        return a, _shift_down(a, halo, 1), _shift_down(a, halo, 2)
    row8 = lax.broadcasted_iota(jnp.int32, halo.shape, 0)
    taps = [a.astype(F32)]
    for k, shift in zip((1, 2), shifts):
        down = jnp.dot(shift, a, preferred_element_type=F32)
        top = down[0:SUBLANES] + jnp.where(row8 < k, pltpu.roll(halo, k, 0), 0.0)
        taps.append(jnp.concatenate([top, down[SUBLANES:tm]], axis=0))
    return taps


def _conv_gate_val(refs, shifts, cols, first):
    ag_ref, av_ref, hg_ref, hv_ref, wg_ref, wv_ref, bg_ref, bv_ref = refs
    g0, g1, g2 = _conv_taps(ag_ref[:, cols], hg_ref[:, cols], first, shifts)
    gate = wg_ref[2:3, cols] * g0 + wg_ref[1:2, cols] * g1 + wg_ref[0:1, cols] * g2 + bg_ref[:, cols]
    v0, v1, v2 = _conv_taps(av_ref[:, cols], hv_ref[:, cols], first, shifts)
    val = wv_ref[2:3, cols] * v0 + wv_ref[1:2, cols] * v1 + wv_ref[0:1, cols] * v2 + bv_ref[:, cols]
    return gate, val, (g2, g1, g0), (v2, v1, v0)


_FF_CHUNKS = [slice(j * FF_CHUNK, (j + 1) * FF_CHUNK) for j in range(D_FF // FF_CHUNK)]


def _conv_specs(tm):
    step = tm // SUBLANES
    prev = lambda i: jnp.maximum(i * step - 1, 0)
    return [pl.BlockSpec((tm, D_FF), lambda i: (i, 0)), pl.BlockSpec((tm, D_FF), lambda i: (i, 1)),
            pl.BlockSpec((SUBLANES, D_FF), lambda i: (prev(i), 0)), pl.BlockSpec((SUBLANES, D_FF), lambda i: (prev(i), 1))]


def _ffn_fwd_loss(a, w_conv, b_conv, w_down, x1, g3, target, tm):
    S = x1.shape[0]

    def body(ag_ref, av_ref, hg_ref, hv_ref, wg_ref, wv_ref, bg_ref, bv_ref, wd_ref, x1_ref, g_ref, t_ref,
             dx2_ref, loss_ref, dg_ref):
        i = pl.program_id(0)

        @pl.when(i == 0)
        def _():
            loss_ref[...] = jnp.zeros_like(loss_ref)
            dg_ref[...] = jnp.zeros_like(dg_ref)

        x2 = x1_ref[...]
        for cols in _FF_CHUNKS:
            gate, val, _, _ = _conv_gate_val((ag_ref, av_ref, hg_ref, hv_ref, wg_ref, wv_ref, bg_ref, bv_ref), None, cols, i == 0)
            half = 0.5 * gate
            y = ((half + half * jnp.tanh(half)) * val).astype(BF16)
            x2 = x2 + jnp.dot(y, wd_ref[cols, :], preferred_element_type=F32)
        r = lax.rsqrt(jnp.mean(x2 * x2, axis=-1, keepdims=True) + EPS)
        xhat = x2 * r
        gg = g_ref[...]
        err = xhat * gg - t_ref[...]
        loss_ref[...] += jnp.sum(err * err, axis=0, keepdims=True)
        dy = err * (1.0 / D_MODEL)
        dg_ref[...] += jnp.sum(dy * xhat, axis=0, keepdims=True)
        dxhat = dy * gg
        dx2_ref[...] = r * (dxhat - xhat * jnp.mean(dxhat * xhat, axis=-1, keepdims=True))

    row = lambda w: pl.BlockSpec((tm, w), lambda i: (i, 0))
    half = lambda r: [pl.BlockSpec((r, D_FF), lambda i: (0, 0)), pl.BlockSpec((r, D_FF), lambda i: (0, 1))]
    return pl.pallas_call(
        body, name="ffn_fwd_loss", grid=(S // tm,),
        in_specs=_conv_specs(tm) + half(3) + half(1) + [_full((D_FF, D_MODEL)), row(D_MODEL), _full((1, D_MODEL)), row(D_MODEL)],
        out_specs=[row(D_MODEL), _full((1, D_MODEL)), _full((1, D_MODEL))],
        out_shape=[jax.ShapeDtypeStruct((S, D_MODEL), F32), jax.ShapeDtypeStruct((1, D_MODEL), F32),
                   jax.ShapeDtypeStruct((1, D_MODEL), F32)],
        compiler_params=_params(("arbitrary",)),
    )(a, a, a, a, w_conv, w_conv, b_conv, b_conv, w_down, x1, g3, target)


def _ffn_bwd_gate(dx2, a, w_conv, b_conv, w_down, tm):
    S = dx2.shape[0]

    def body(dx_ref, ag_ref, av_ref, hg_ref, hv_ref, wg_ref, wv_ref, bg_ref, bv_ref, wd_ref,
             dc_ref, y_ref, dw_ref, db_ref):
        i = pl.program_id(0)

        @pl.when(i == 0)
        def _():
            dw_ref[...] = jnp.zeros_like(dw_ref)
            db_ref[...] = jnp.zeros_like(db_ref)

        dx = dx_ref[...].astype(BF16)
        shifts = _shift_matrices(tm)
        for cols in _FF_CHUNKS:
            gate, val, gtaps, vtaps = _conv_gate_val((ag_ref, av_ref, hg_ref, hv_ref, wg_ref, wv_ref, bg_ref, bv_ref), shifts, cols, i == 0)
            sg = _sigmoid(gate)
            act = gate * sg
            y_ref[:, cols] = (act * val).astype(BF16)
            dy = lax.dot_general(dx, wd_ref[cols, :], NT, preferred_element_type=F32)
            dgate = dy * val * (sg + act - act * sg)
            dval = dy * act
            for d, taps, out in ((dgate, gtaps, cols), (dval, vtaps, slice(D_FF + cols.start, D_FF + cols.stop))):
                dc_ref[:, out] = d.astype(BF16)
                db_ref[0:1, out] += jnp.sum(d, axis=0, keepdims=True)
                for j in range(3):
                    dw_ref[j:j + 1, out] += jnp.sum(d * taps[j], axis=0, keepdims=True)

    row = lambda w: pl.BlockSpec((tm, w), lambda i: (i, 0))
    half = lambda r: [pl.BlockSpec((r, D_FF), lambda i: (0, 0)), pl.BlockSpec((r, D_FF), lambda i: (0, 1))]
    return pl.pallas_call(
        body, name="ffn_bwd_gate", grid=(S // tm,),
        in_specs=[row(D_MODEL)] + _conv_specs(tm) + half(3) + half(1) + [_full((D_FF, D_MODEL))],
        out_specs=[row(2 * D_FF), row(D_FF), _full((SUBLANES, 2 * D_FF)), _full((1, 2 * D_FF))],
        out_shape=[jax.ShapeDtypeStruct((S, 2 * D_FF), BF16), jax.ShapeDtypeStruct((S, D_FF), BF16),
                   jax.ShapeDtypeStruct((SUBLANES, 2 * D_FF), F32), jax.ShapeDtypeStruct((1, 2 * D_FF), F32)],
        compiler_params=_params(("arbitrary",)),
    )(dx2, a, a, a, a, w_conv, w_conv, b_conv, b_conv, w_down)


def _conv_bwd(dc, w_conv, tm, tn):
    S, C = dc.shape
    step = tm // SUBLANES
    last_blk = S // SUBLANES - 1

    def body(d_ref, nx_ref, w_ref, o_ref):
        last = pl.program_id(0) == pl.num_programs(0) - 1
        row = lax.broadcasted_iota(jnp.int32, (tm, tm), 0)
        col = lax.broadcasted_iota(jnp.int32, (tm, tm), 1)
        row8 = lax.broadcasted_iota(jnp.int32, (SUBLANES, FF_CHUNK), 0)
        ups = [(row + k == col).astype(BF16) for k in (1, 2)]
        for c0 in range(0, tn, FF_CHUNK):
            cols = slice(c0, c0 + FF_CHUNK)
            d = d_ref[:, cols]
            nx = nx_ref[:, cols].astype(F32) * jnp.where(last, 0.0, 1.0)
            out = w_ref[2:3, cols] * d.astype(F32)
            for k, up in zip((1, 2), ups):
                moved = jnp.dot(up, d, preferred_element_type=F32)
                bottom = moved[tm - SUBLANES:tm] + jnp.where(row8 >= SUBLANES - k, pltpu.roll(nx, SUBLANES - k, 0), 0.0)
                out = out + w_ref[2 - k:3 - k, cols] * jnp.concatenate([moved[0:tm - SUBLANES], bottom], axis=0)
            o_ref[:, cols] = out.astype(BF16)

    return pl.pallas_call(
        body, name="conv_bwd", grid=(S // tm, C // tn),
        in_specs=[pl.BlockSpec((tm, tn), lambda i, j: (i, j)),
                  pl.BlockSpec((SUBLANES, tn), lambda i, j: (jnp.minimum((i + 1) * step, last_blk), j)),
                  pl.BlockSpec((3, tn), lambda i, j: (0, j))],
        out_specs=pl.BlockSpec((tm, tn), lambda i, j: (i, j)),
        out_shape=jax.ShapeDtypeStruct((S, C), BF16),
        compiler_params=_params(("parallel", "parallel")),
    )(dc, dc, w_conv)


def _matmul_tn(a, b, name, bm, bn, tk, col_a=0, col_b=0, quarters=None):
    S = a.shape[0]
    gm, gn = quarters if quarters else (1, 1)
    nk = S // tk

    def body(a_ref, b_ref, o_ref):
        @pl.when(pl.program_id(2) == 0)
        def _():
            o_ref[...] = jnp.zeros_like(o_ref)

        o_ref[...] += lax.dot_general(a_ref[...].astype(BF16), b_ref[...].astype(BF16), TN, preferred_element_type=F32)

    if quarters and gn > 1:
        out_spec = pl.BlockSpec((None, bm, bn), lambda i, j, k: (j, i, 0))
        out_shape = jax.ShapeDtypeStruct((gn, gm * bm, bn), F32)
    else:
        out_spec = pl.BlockSpec((bm, bn), lambda i, j, k: (i, j))
        out_shape = jax.ShapeDtypeStruct((gm * bm, gn * bn), F32)
    return pl.pallas_call(
        body, name=name, grid=(gm, gn, nk),
        in_specs=[pl.BlockSpec((tk, bm), lambda i, j, k: (k, col_a * gm + i)),
                  pl.BlockSpec((tk, bn), lambda i, j, k: (k, col_b * gn + j))],
        out_specs=out_spec, out_shape=out_shape,
        compiler_params=_params(("parallel", "parallel", "arbitrary")),
    )(a, b)


def _dw_out(out_a, out_b, dx1, tk):
    S = dx1.shape[0]

    def body(a_ref, b_ref, d_ref, o_ref):
        @pl.when(pl.program_id(0) == 0)
        def _():
            o_ref[...] = jnp.zeros_like(o_ref)

        d = d_ref[...].astype(BF16)
        o_ref[:D_HEADS, :] += lax.dot_general(a_ref[...], d, TN, preferred_element_type=F32)
        o_ref[D_HEADS:, :] += lax.dot_general(b_ref[...], d, TN, preferred_element_type=F32)

    row = lambda w: pl.BlockSpec((tk, w), lambda k: (k, 0))
    return pl.pallas_call(
        body, name="dw_out", grid=(S // tk,),
        in_specs=[row(D_HEADS), row(D_HEADS), row(D_MODEL)], out_specs=_full((D_MODEL, D_MODEL)),
        out_shape=jax.ShapeDtypeStruct((D_MODEL, D_MODEL), F32),
        compiler_params=_params(("arbitrary",)),
    )(out_a, out_b, dx1)


def _up_bwd(dact, w_up_q, x1, g2, dx2, tm):
    S = x1.shape[0]
    nq, _, wq = w_up_q.shape

    def body(d_ref, w_ref, x_ref, g_ref, dx2_ref, dx1_ref, dg_ref):
        @pl.when(pl.program_id(0) == 0)
        def _():
            dg_ref[...] = jnp.zeros_like(dg_ref)

        dh = jnp.zeros((tm, D_MODEL), F32)
        for j in range(nq):
            dh = dh + lax.dot_general(d_ref[:, j * wq:(j + 1) * wq], w_ref[j], NT, preferred_element_type=F32)
        dx, dg = _rms_bwd(dh, x_ref[...], g_ref[...])
        dg_ref[...] += dg
        dx1_ref[...] = dx2_ref[...] + dx

    row = lambda w: pl.BlockSpec((tm, w), lambda i: (i, 0))
    return pl.pallas_call(
        body, name="up_bwd", grid=(S // tm,),
        in_specs=[row(nq * wq), pl.BlockSpec((nq, D_MODEL, wq), lambda i: (0, 0, 0), pipeline_mode=pl.Buffered(1)),
                  row(D_MODEL), _full((1, D_MODEL)), row(D_MODEL)],
        out_specs=[row(D_MODEL), _full((1, D_MODEL))],
        out_shape=[jax.ShapeDtypeStruct((S, D_MODEL), F32), jax.ShapeDtypeStruct((1, D_MODEL), F32)],
        compiler_params=_params(("arbitrary",)),
    )(dact, w_up_q, x1, g2, dx2)


def _out_bwd(dx1, w_out, tm):
    S = dx1.shape[0]

    def body(d_ref, w_ref, o_ref):
        o_ref[...] = lax.dot_general(d_ref[...].astype(BF16), w_ref[...], NT, preferred_element_type=F32).astype(BF16)

    return pl.pallas_call(
        body, name="out_bwd", grid=(S // tm,),
        in_specs=[pl.BlockSpec((tm, D_MODEL), lambda i: (i, 0)), _full((D_MODEL, D_MODEL))],
        out_specs=pl.BlockSpec((tm, D_MODEL), lambda i: (i, 0)),
        out_shape=jax.ShapeDtypeStruct((S, D_MODEL), BF16),
        compiler_params=_params(("parallel",)),
    )(dx1, w_out)


def _gate_bwd(z, dcat, w_mask, w_mask_t, ln_row, b_full, seg_avg, head_ind, tm, swap):
    S = z.shape[0]
    nb = tm // SG_BLOCK
    ns = len(swap)

    def body(zu_ref, zv_ref, do_ref, w_ref, wt_ref, ln_ref, b_ref, avg_ref, ind_ref, *rest):
        dzu_ref, dzv_ref, dw_ref, db_ref, dln_ref = rest[ns:ns + 5]
        dvn_s, dbf_s = rest[2 * ns + 5:2 * ns + 7]
        swap_start, swap_finish = _swap_ops(rest[:ns], rest[ns + 5:2 * ns + 5], *rest[2 * ns + 7:])
        i = pl.program_id(0)

        @pl.when(i == 0)
        def _():
            swap_start()
            dw_ref[...] = jnp.zeros_like(dw_ref)
            dln_ref[...] = jnp.zeros_like(dln_ref)
            dbf_s[...] = jnp.zeros_like(dbf_s)

        zu = zu_ref[...].astype(F32)
        zv = zv_ref[...].astype(F32)
        u = _gelu(zu)
        v = _gelu(zv)
        avg = avg_ref[...]
        vhat, rstd = _layer_norm_heads(v, avg)
        ln = ln_ref[...]
        vn = vhat * ln
        for b in range(nb):
            rows = slice(b * SG_BLOCK, (b + 1) * SG_BLOCK)
            vn_b = vn[rows]
            mixed = _gate_mix(vn_b, w_ref, b_ref[...])
            do = do_ref[rows, :].astype(F32)
            dzu_ref[rows, :] = (do * mixed * _gelu_grad(zu[rows])).astype(BF16)
            dmix = do * u[rows]
            dbf_s[...] += dmix
            vn_bf = vn_b.astype(BF16)
            dvn = jnp.zeros((SG_BLOCK, D_HEADS), F32)
            for h in range(N_HEADS):
                dmh = jnp.where(_head_mask(h, SG_BLOCK), dmix, 0.0).astype(BF16)
                dw_ref[h] += lax.dot_general(dmh, vn_bf, NT, preferred_element_type=F32)
                dvn = dvn + jnp.dot(wt_ref[h], dmh, preferred_element_type=F32)
            dvn_s[rows, :] = dvn
        dvn = dvn_s[...]
        dln_ref[...] += jnp.sum(dvn * vhat, axis=0, keepdims=True)
        dvhat = dvn * ln
        dv = rstd * (dvhat - _split_dot(dvhat, avg) - vhat * _split_dot(dvhat * vhat, avg))
        dzv_ref[...] = (dv * _gelu_grad(zv)).astype(BF16)

        @pl.when(i == pl.num_programs(0) - 1)
        def _():
            r = lax.broadcasted_iota(jnp.int32, (SG_BLOCK, SG_BLOCK), 0) // CHUNK
            s = lax.broadcasted_iota(jnp.int32, (SG_BLOCK, SG_BLOCK), 1) // CHUNK
            for h in range(N_HEADS):
                dw_ref[h] = jnp.where(r >= s, dw_ref[h], 0.0)
            db_ref[...] = _split_dot(dbf_s[...], ind_ref[...])
            swap_finish()

    row = lambda col: pl.BlockSpec((tm, D_HEADS), lambda i: (i, col))
    wspec = _full((N_HEADS, SG_BLOCK, SG_BLOCK))
    out = pl.pallas_call(
        body, name="gate_bwd", grid=(S // tm,),
        in_specs=[row(0), row(1), row(0), wspec, wspec, _full((1, D_HEADS)), _full((SG_BLOCK, D_HEADS)),
                  _full((D_HEADS, D_HEADS)), _full((D_HEADS, LANES))] + [_ANY] * ns,
        out_specs=[row(0), row(0), wspec, _full((SG_BLOCK, LANES)), _full((1, D_HEADS))] + [_ANY] * ns,
        out_shape=[jax.ShapeDtypeStruct((S, D_HEADS), BF16), jax.ShapeDtypeStruct((S, D_HEADS), BF16),
                   jax.ShapeDtypeStruct((N_HEADS, SG_BLOCK, SG_BLOCK), F32), jax.ShapeDtypeStruct((SG_BLOCK, LANES), F32),
                   jax.ShapeDtypeStruct((1, D_HEADS), F32)] + _swap_shapes(swap),
        scratch_shapes=[pltpu.VMEM((tm, D_HEADS), F32), pltpu.VMEM((SG_BLOCK, D_HEADS), F32)] + _swap_sems(ns),
        compiler_params=_params(("arbitrary",)),
    )(z, z, dcat, w_mask, w_mask_t, ln_row, b_full, seg_avg, head_ind, *swap)
    return out[:5], out[5:]


def _attn_pack_grad(o, dcat, qa, lse, head_ind, k, tm):
    S = o.shape[0]

    def body(o_ref, do_ref, qa_ref, lse_ref, ind_ref, pl_ref, pt_ref, eye_ref, ds_ref, dst_ref, ls_ref, lst_ref,
             dop_ref, qb_ref, dot_ref, qbt_ref):
        do = do_ref[...]
        delta = _split_dot(o_ref[...].astype(F32) * do.astype(F32), ind_ref[...])
        hi = delta.astype(BF16).astype(F32)
        parts = (hi + pltpu.roll((delta - hi).astype(BF16).astype(F32), N_HEADS, 1)).astype(BF16)
        dop = jnp.dot(do, pl_ref[...], preferred_element_type=F32) - jnp.dot(parts, ds_ref[...], preferred_element_type=F32)
        for g in range(GROUPS):
            dop_ref[g] = dop[:, g * GROUP_PAD:(g + 1) * GROUP_PAD].astype(BF16)
        dot = lax.dot_general(pt_ref[...], do, NT, preferred_element_type=F32)
        dot_ref[...] = (dot - lax.dot_general(dst_ref[...], parts, NT, preferred_element_type=F32)).astype(BF16)
        qa = qa_ref[...]
        stack = jnp.concatenate(_split3(lse_ref[...]), axis=0)
        qb = qa.astype(F32) - lax.dot_general(stack, ls_ref[...], TN, preferred_element_type=F32)
        qbt = lax.dot_general(eye_ref[...], qa, NT, preferred_element_type=F32)
        qbt = qbt - jnp.dot(lst_ref[...], stack, preferred_element_type=F32)
        for g in range(GROUPS):
            qb_ref[g] = qb[:, g * GROUP_PAD:(g + 1) * GROUP_PAD].astype(BF16)
        qbt_ref[...] = qbt.astype(BF16)

    pad = pl.BlockSpec((tm, D_PAD), lambda i: (i, 0))
    padt = pl.BlockSpec((None, D_PAD, tm), lambda i: (i, 0, 0))
    return pl.pallas_call(
        body, name="attn_pack_grad", grid=(S // tm,),
        in_specs=[pl.BlockSpec((tm, D_HEADS), lambda i: (i, 0)), pl.BlockSpec((tm, D_HEADS), lambda i: (i, 1)), pad,
                  pl.BlockSpec((STAT_ROWS, tm), lambda i: (0, i)), _full((D_HEADS, LANES)), _full((D_HEADS, D_PAD)),
                  _full((D_PAD, D_HEADS)), _full((D_PAD, D_PAD)), _full((LANES, D_PAD)), _full((D_PAD, LANES)),
                  _full((3 * STAT_ROWS, D_PAD)), _full((D_PAD, 3 * STAT_ROWS))],
        out_specs=[pl.BlockSpec((GROUPS, tm, GROUP_PAD), lambda i: (0, i, 0))] * 2 + [padt, padt],
        out_shape=[jax.ShapeDtypeStruct((GROUPS, S, GROUP_PAD), BF16)] * 2 + [jax.ShapeDtypeStruct((S // tm, D_PAD, tm), BF16)] * 2,
        compiler_params=_params(("parallel",)),
    )(o, dcat, qa, lse, head_ind, k["place"], k["place_t"], jnp.eye(D_PAD, dtype=BF16), k["d_stat"], k["d_stat"].T,
      k["l_stat"], k["l_stat"].T)


def _attn_bwd(qb, qbt, ka, va, dop, dopt, k, tq, sums16):
    S = ka.shape[0]
    n = S // tq
    ns = len(sums16)

    pairs = [(kb, q) for kb in range(n) for q in range(kb, n)]
    k_of = jnp.asarray([kb for kb, _ in pairs], jnp.int32)
    q_of = jnp.asarray([q for _, q in pairs], jnp.int32)

    def body(k_of_ref, q_of_ref, q_ref, qt_ref, k_ref, v_ref, do_ref, dot_ref, pt_ref, *rest):
        dq_hbm, dcr_hbm, dk_ref, dv_ref, dcc_ref = rest[ns:ns + 5]
        dq_s, dcr_s, dk_s, dv_s, dcc_s = rest[2 * ns + 5:2 * ns + 10]
        s_s, d_s = rest[2 * ns + 10:2 * ns + 12], rest[2 * ns + 12:2 * ns + 14]
        sems = rest[2 * ns + 14]
        scatter_start, scatter_finish = _scatter_ops(rest[:ns], rest[ns + 5:2 * ns + 5], *rest[2 * ns + 15:])
        g = pl.program_id(0)
        ki, qi = k_of_ref[pl.program_id(1)], q_of_ref[pl.program_id(1)]

        @pl.when((g == 0) & (ki == 0) & (qi == 0))
        def _():
            scatter_start()

        @pl.when((ki == 0) & (qi == 0))
        def _():
            dq_s[...] = jnp.zeros_like(dq_s)
            dcr_s[...] = jnp.zeros_like(dcr_s)

        @pl.when(qi == ki)
        def _():
            dk_s[...] = jnp.zeros_like(dk_s)
            dv_s[...] = jnp.zeros_like(dv_s)
            dcc_s[...] = jnp.zeros_like(dcc_s)

        def step(diagonal):
            chunks = [slice(c * KEY_CHUNK, (c + 1) * KEY_CHUNK) for c in range(tq // KEY_CHUNK)]

            def scores(hh, rows, slot):
                sl = slice(hh * HEAD_PAD, (hh + 1) * HEAD_PAD)
                s_s[slot][rows, :] = lax.dot_general(q_ref[rows, sl], k_ref[:, sl], NT, preferred_element_type=F32)
                d_s[slot][rows, :] = lax.dot_general(do_ref[rows, sl], v_ref[:, sl], NT, preferred_element_type=F32)

            for rows in chunks:
                scores(0, rows, 0)
            for hh in range(GROUP_HEADS):
                sl = slice(hh * HEAD_PAD, (hh + 1) * HEAD_PAD)
                slot = hh % 2
                dv, dk = dv_s[sl, :], dk_s[sl, :]
                for rows in chunks:
                    if hh + 1 < GROUP_HEADS:
                        scores(hh + 1, rows, 1 - slot)
                    p = jnp.exp2(s_s[slot][rows, :])
                    if diagonal:
                        row = rows.start + lax.broadcasted_iota(jnp.int32, (KEY_CHUNK, tq), 0)
                        col = lax.broadcasted_iota(jnp.int32, (KEY_CHUNK, tq), 1)
                        p = jnp.where(row >= col, p, 0.0)
                    ds = p * d_s[slot][rows, :]
                    qrows = pl.ds(pl.multiple_of(qi * tq + rows.start, KEY_CHUNK), KEY_CHUNK)
                    dcc_s[hh:hh + 1, :] += jnp.sum(ds, axis=0, keepdims=True)
                    dcr_s[qrows, hh:hh + 1] += jnp.sum(ds, axis=1, keepdims=True)
                    ds = ds.astype(BF16)
                    dv = dv + jnp.dot(dot_ref[sl, rows], p.astype(BF16), preferred_element_type=F32)
                    dk = dk + jnp.dot(qt_ref[sl, rows], ds, preferred_element_type=F32)
                    dq_s[qrows, sl] += jnp.dot(ds, k_ref[:, sl], preferred_element_type=F32)
                dv_s[sl, :] = dv
                dk_s[sl, :] = dk

        @pl.when(qi > ki)
        def _():
            step(False)

        @pl.when(qi == ki)
        def _():
            step(True)

        @pl.when(qi == n - 1)
        def _():
            dk = dk_s[...]
            pt = pt_ref[...]
            dk_ref[...] = lax.dot_general((dk * (1.0 / LOG2E)).astype(BF16), pt, TN, preferred_element_type=F32).astype(BF16)
            dv_ref[...] = lax.dot_general(dv_s[...].astype(BF16), pt, TN, preferred_element_type=F32).astype(BF16)
            dcc_ref[...] = dcc_s[...]

        @pl.when((ki == n - 1) & (qi == n - 1))
        def _():
            copies = [pltpu.make_async_copy(dq_s, dq_hbm.at[g], sems.at[0]), pltpu.make_async_copy(dcr_s, dcr_hbm.at[g], sems.at[1])]
            for cp in copies:
                cp.start()
            for cp in copies:
                cp.wait()

        @pl.when((g == GROUPS - 1) & (ki == n - 1) & (qi == n - 1))
        def _():
            scatter_finish()

    gw = GROUP_HEADS * HEAD_DIM
    qspec = pl.BlockSpec((None, tq, GROUP_PAD), lambda g, i, ks, qs: (g, qs[i], 0))
    qtspec = pl.BlockSpec((None, GROUP_PAD, tq), lambda g, i, ks, qs: (qs[i], g, 0))
    kspec = pl.BlockSpec((tq, GROUP_PAD), lambda g, i, ks, qs: (ks[i], g))
    kout = pl.BlockSpec((tq, gw), lambda g, i, ks, qs: (ks[i], g))
    out = pl.pallas_call(
        body, name="attn_bwd",
        grid_spec=pltpu.PrefetchScalarGridSpec(
            num_scalar_prefetch=2, grid=(GROUPS, len(pairs)),
            in_specs=[qspec, qtspec, kspec, kspec, qspec, qtspec, pl.BlockSpec((GROUP_PAD, gw), lambda g, i, ks, qs: (0, 0))]
            + [_ANY] * ns,
            out_specs=[_ANY, _ANY, kout, kout, pl.BlockSpec((None, SUBLANES, tq), lambda g, i, ks, qs: (g, 0, ks[i]))] + [_ANY] * ns,
            scratch_shapes=[pltpu.VMEM((S, GROUP_PAD), F32), pltpu.VMEM((S, LANES), F32), pltpu.VMEM((GROUP_PAD, tq), F32),
                            pltpu.VMEM((GROUP_PAD, tq), F32), pltpu.VMEM((SUBLANES, tq), F32),
                            pltpu.VMEM((tq, tq), F32), pltpu.VMEM((tq, tq), F32), pltpu.VMEM((tq, tq), F32),
                            pltpu.VMEM((tq, tq), F32), pltpu.SemaphoreType.DMA((2,))]
            + _scatter_sems(ns)),
        out_shape=[jax.ShapeDtypeStruct((GROUPS, S, GROUP_PAD), F32), jax.ShapeDtypeStruct((GROUPS, S, LANES), F32),
                   jax.ShapeDtypeStruct((S, D_HEADS), BF16), jax.ShapeDtypeStruct((S, D_HEADS), BF16),
                   jax.ShapeDtypeStruct((GROUPS, SUBLANES, S), F32)]
        + _scatter_shapes(sums16),
        compiler_params=_params(("arbitrary", "arbitrary")),
    )(k_of, q_of, qb, qbt, ka, va, dop, dopt, k["place_t_group"], *sums16)
    return out[0], out[1], out[2], out[3], out[4], out[5:]


def _attn_unpack(dqp, k, tm):
    S = dqp.shape[1]
    gw = GROUP_HEADS * HEAD_DIM

    def body(dqp_ref, pt_ref, dq_ref):
        for g in range(GROUPS):
            dq_ref[:, g * gw:(g + 1) * gw] = jnp.dot((dqp_ref[g] * SCALE).astype(BF16), pt_ref[...],
                                                     preferred_element_type=F32).astype(BF16)

    return pl.pallas_call(
        body, name="attn_unpack", grid=(S // tm,),
        in_specs=[pl.BlockSpec((GROUPS, tm, GROUP_PAD), lambda i: (0, i, 0)), _full((GROUP_PAD, gw))],
        out_specs=pl.BlockSpec((tm, D_HEADS), lambda i: (i, 0)),
        out_shape=jax.ShapeDtypeStruct((S, D_HEADS), BF16),
        compiler_params=_params(("parallel",)),
    )(dqp, k["place_t_group"])


def _fox_bwd(dc, f, bias_row, tb):
    S = f.shape[0]
    nb = S // tb

    def body(dc_ref, f_ref, b_ref, df_ref, dbias_ref, carry):
        @pl.when(pl.program_id(0) == 0)
        def _():
            carry[...] = jnp.zeros_like(carry)
            dbias_ref[...] = jnp.zeros_like(dbias_ref)

        r = lax.broadcasted_iota(jnp.int32, (tb, tb), 0)
        s = lax.broadcasted_iota(jnp.int32, (tb, tb), 1)
        tri = (s >= r).astype(F32)
        rc = jnp.dot(tri, dc_ref[...], precision=lax.Precision.HIGHEST, preferred_element_type=F32) + carry[0:1, :]
        carry[...] = jnp.broadcast_to(rc[0:1, :], carry.shape)
        lane = lax.broadcasted_iota(jnp.int32, (tb, LANES), 1)
        df = jnp.where(lane < N_HEADS, rc * jax.nn.sigmoid(-(f_ref[...] + b_ref[...])), 0.0)
        df_ref[...] = df.astype(BF16)
        dbias_ref[...] += jnp.sum(df, axis=0, keepdims=True)

    rev = pl.BlockSpec((tb, LANES), lambda i: (nb - 1 - i, 0))
    return pl.pallas_call(
        body, name="fox_bwd", grid=(nb,),
        in_specs=[rev, rev, _full((1, LANES))],
        out_specs=[rev, _full((1, LANES))],
        out_shape=[jax.ShapeDtypeStruct((S, LANES), BF16), jax.ShapeDtypeStruct((1, LANES), F32)],
        scratch_shapes=[pltpu.VMEM((SUBLANES, LANES), F32)],
        compiler_params=_params(("arbitrary",)),
    )(dc, f, bias_row)


_DZ_WIDTHS = (D_HEADS,) * 5 + (LANES,)


def _in_bwd(pieces, w_in, x, g1, dx1, tm, sums16):
    S = x.shape[0]
    ns = len(sums16)

    def body(*refs):
        p_refs, (w_ref, x_ref, g_ref, dx1_ref) = refs[:6], refs[6:10]
        dx_ref, dg_ref = refs[10 + ns:12 + ns]
        scatter_start, scatter_finish = _scatter_ops(refs[10:10 + ns], refs[12 + ns:12 + 2 * ns], *refs[12 + 2 * ns:])

        @pl.when(pl.program_id(0) == 0)
        def _():
            dg_ref[...] = jnp.zeros_like(dg_ref)
            scatter_start()

        dh = jnp.zeros((tm, D_MODEL), F32)
        off = 0
        for p_ref, w in zip(p_refs, _DZ_WIDTHS):
            dh = dh + lax.dot_general(p_ref[...].astype(BF16), w_ref[:, off:off + w], NT, preferred_element_type=F32)
            off += w
        dx, dg = _rms_bwd(dh, x_ref[...], g_ref[...])
        dg_ref[...] += dg
        dx_ref[...] = dx1_ref[...] + dx

        @pl.when(pl.program_id(0) == pl.num_programs(0) - 1)
        def _():
            scatter_finish()

    row = lambda w: pl.BlockSpec((tm, w), lambda i: (i, 0))
    out = pl.pallas_call(
        body, name="in_bwd", grid=(S // tm,),
        in_specs=[row(w) for w in _DZ_WIDTHS] + [_full((D_MODEL, D_IN_PAD)), row(D_MODEL), _full((1, D_MODEL)), row(D_MODEL)]
        + [_ANY] * ns,
        out_specs=[row(D_MODEL), _full((1, D_MODEL))] + [_ANY] * ns,
        out_shape=[jax.ShapeDtypeStruct((S, D_MODEL), F32), jax.ShapeDtypeStruct((1, D_MODEL), F32)] + _scatter_shapes(sums16),
        scratch_shapes=_scatter_sems(ns),
        compiler_params=_params(("arbitrary",)),
    )(*pieces, w_in, x, g1, dx1, *sums16)
    return out[0], out[1], out[2:]


def _dw_in(h1, pieces, tk):
    S = h1.shape[0]

    def body(*refs):
        h_ref, p_refs, o_ref = refs[0], refs[1:7], refs[7]

        @pl.when(pl.program_id(0) == 0)
        def _():
            o_ref[...] = jnp.zeros_like(o_ref)

        off = 0
        for p_ref, w in zip(p_refs, _DZ_WIDTHS):
            o_ref[:, off:off + w] += lax.dot_general(h_ref[...], p_ref[...].astype(BF16), TN, preferred_element_type=F32)
            off += w

    row = lambda w: pl.BlockSpec((tk, w), lambda k: (k, 0))
    return pl.pallas_call(
        body, name="dw_in", grid=(S // tk,),
        in_specs=[row(D_MODEL)] + [row(w) for w in _DZ_WIDTHS],
        out_specs=_full((D_MODEL, D_IN_PAD)),
        out_shape=jax.ShapeDtypeStruct((D_MODEL, D_IN_PAD), F32),
        compiler_params=_params(("arbitrary",)),
    )(h1, *pieces)


def _adamw_math(w, g, m, v):
    m = ADAM_B1 * m + (1.0 - ADAM_B1) * g
    v = ADAM_B2 * v + (1.0 - ADAM_B2) * (g * g)
    m_hat = m / (1.0 - ADAM_B1 ** ADAM_STEP)
    v_hat = v / (1.0 - ADAM_B2 ** ADAM_STEP)
    delta = -ADAM_LR * (m_hat / (jnp.sqrt(v_hat) + ADAM_EPS) + ADAM_WD * w)
    return delta, m, v


def _adamw(name, w, g, m, v):
    R, C = w.shape
    tr = _row_tile(R, 256)

    def body(w_ref, g_ref, m_ref, v_ref, go_ref, d_ref, nm_ref, nv_ref):
        g = g_ref[...]
        d, nm, nv = _adamw_math(w_ref[...], g, m_ref[...], v_ref[...])
        go_ref[...] = g
        d_ref[...] = d
        nm_ref[...] = nm
        nv_ref[...] = nv

    spec = pl.BlockSpec((tr, C), lambda i: (i, 0))
    return pl.pallas_call(
        body, name=name, grid=(R // tr,), in_specs=[spec] * 4, out_specs=[spec] * 4,
        out_shape=[jax.ShapeDtypeStruct((R, C), F32)] * 4,
        compiler_params=_params(("parallel",)),
    )(w, g, m, v)


def _pair_sum(name, grad, theirs, ids):
    q, half, C = theirs.shape
    tr = _row_tile(half, 256)
    nb = half // tr

    def body(ids_ref, a_ref, b_ref, sb_ref):
        sb_ref[...] = (a_ref[...] + b_ref[...]).astype(BF16)

    here = pl.BlockSpec((None, tr, C), lambda j, i, ids: (j, i, 0))
    return pl.pallas_call(
        body, name=name,
        grid_spec=pltpu.PrefetchScalarGridSpec(
            num_scalar_prefetch=1, grid=(q, nb),
            in_specs=[pl.BlockSpec((None, tr, C), lambda j, i, ids: (j, ids[1] * nb + i, 0)), here],
            out_specs=here),
        out_shape=jax.ShapeDtypeStruct((q, half, C), BF16),
        compiler_params=_params(("parallel", "parallel")),
    )(ids, grad, theirs)


def _chip_sum(name, grad, theirs, others, ids):
    _, half, C = theirs.shape
    tr = _row_tile(half, 256)
    nb = half // tr

    def body(ids_ref, a_ref, b_ref, o_ref, s_ref):
        s = a_ref[...] + b_ref[...]
        for j in range(3):
            s = s + o_ref[j].astype(F32)
        s_ref[...] = s

    return pl.pallas_call(
        body, name=name,
        grid_spec=pltpu.PrefetchScalarGridSpec(
            num_scalar_prefetch=1, grid=(nb,),
            in_specs=[pl.BlockSpec((None, tr, C), lambda i, ids: (ids[0], ids[1] * nb + i, 0)),
                      pl.BlockSpec((None, tr, C), lambda i, ids: (ids[0], i, 0)),
                      pl.BlockSpec((3, tr, C), lambda i, ids: (0, i, 0))],
            out_specs=pl.BlockSpec((tr, C), lambda i, ids: (ids[1] * nb + i, 0))),
        out_shape=jax.ShapeDtypeStruct((2 * half, C), F32),
        compiler_params=_params(("parallel",)),
    )(ids, grad, theirs, others)


def _place():
    return lax.axis_index("x"), lax.axis_index("y"), lax.axis_index("c")


def _other_chips(x, y):
    return [(1 - x, y), (x, 1 - y), (1 - x, 1 - y)]


_ANY = pl.BlockSpec(memory_space=pl.ANY)


def _gather_quarters(shards):
    n = len(shards)

    def body(*refs):
        start, hand_on, finish = _gather_ops(refs[:n], refs[n:2 * n], *refs[2 * n:])
        start()
        hand_on()
        finish()

    return pl.pallas_call(
        body, name="gather_weights",
        in_specs=[_ANY] * n, out_specs=[_ANY] * n,
        out_shape=_gather_shapes(shards), scratch_shapes=_gather_sems(n),
    )(*shards)


def _gather_shapes(shards):
    return [jax.ShapeDtypeStruct((4,) + s.shape, s.dtype) for s in shards]


def _gather_sems(n):
    return [pltpu.SemaphoreType.DMA((n, 3))] * 4 + [pltpu.SemaphoreType.DMA((n,))]


def _gather_ops(ins, outs, send_sems, recv_sems, pass_send_sems, pass_recv_sems, own_sems):
    n = len(ins)
    halved = [r.shape[0] % 32 == 0 for r in ins]

    def part(a, quarter, core):
        if not halved[a]:
            return outs[a].at[quarter]
        half = ins[a].shape[0] // 2
        return outs[a].at[quarter, pl.ds(core * half, half), :]

    def ici(a, j, quarter):
        x, y, c = _place()
        px, py = _other_chips(x, y)[j]
        src = ins[a]
        if halved[a]:
            half = src.shape[0] // 2
            src = src.at[pl.ds(c * half, half), :]
        return pltpu.make_async_remote_copy(src_ref=src, dst_ref=part(a, quarter, c), send_sem=send_sems.at[a, j],
                                            recv_sem=recv_sems.at[a, j], device_id=(px, py, c), device_id_type=MESH)

    def passed(a, j, core):
        x, y, c = _place()
        px, py = _other_chips(x, y)[j]
        half = part(a, 2 * px + py, core)
        return pltpu.make_async_remote_copy(src_ref=half, dst_ref=half, send_sem=pass_send_sems.at[a, j],
                                            recv_sem=pass_recv_sems.at[a, j], device_id=(x, y, 1 - c), device_id_type=MESH)

    def own(a):
        x, y, _ = _place()
        return pltpu.make_async_copy(ins[a], outs[a].at[2 * x + y], own_sems.at[a])

    def start():
        x, y, _ = _place()
        for a in range(n):
            for j in range(3):
                ici(a, j, 2 * x + y).start()
            own(a).start()

    def hand_on():
        x, y, c = _place()
        for a in range(n):
            for j, (px, py) in enumerate(_other_chips(x, y)):
                ici(a, j, 2 * px + py).wait_recv()
                if halved[a]:
                    passed(a, j, c).start()

    def finish():
        x, y, c = _place()
        for a in range(n):
            for j in range(3):
                if halved[a]:
                    passed(a, j, 1 - c).wait_recv()
                    passed(a, j, c).wait_send()
                ici(a, j, 2 * x + y).wait_send()
            own(a).wait()

    return start, hand_on, finish


def _swap_halves(grads, name):
    n = len(grads)

    def body(*refs):
        start, finish = _swap_ops(refs[:n], refs[n:2 * n], *refs[2 * n:])
        start()
        finish()

    return pl.pallas_call(
        body, name=name,
        in_specs=[_ANY] * n, out_specs=[_ANY] * n, out_shape=_swap_shapes(grads), scratch_shapes=_swap_sems(n),
    )(*grads)


def _swap_shapes(grads):
    return [jax.ShapeDtypeStruct((4, g.shape[1] // 2, g.shape[2]), F32) for g in grads]


def _swap_sems(n):
    return [pltpu.SemaphoreType.DMA((n,))] * 2


def _swap_ops(ins, outs, send_sems, recv_sems):
    def copy(a):
        x, y, c = _place()
        half = ins[a].shape[1] // 2
        return pltpu.make_async_remote_copy(src_ref=ins[a].at[:, pl.ds((1 - c) * half, half), :], dst_ref=outs[a],
                                            send_sem=send_sems.at[a], recv_sem=recv_sems.at[a],
                                            device_id=(x, y, 1 - c), device_id_type=MESH)

    def start():
        for a in range(len(ins)):
            copy(a).start()

    def finish():
        for a in range(len(ins)):
            copy(a).wait()

    return start, finish


def _scatter_shapes(sums16):
    return [jax.ShapeDtypeStruct((3,) + s.shape[1:], BF16) for s in sums16]


def _scatter_sems(n):
    return [pltpu.SemaphoreType.DMA((n, 3))] * 2


def _scatter_ops(ins, outs, send_sems, recv_sems):
    n = len(ins)

    def copy(a, j):
        x, y, c = _place()
        px, py = _other_chips(x, y)[j]
        return pltpu.make_async_remote_copy(src_ref=ins[a].at[2 * px + py], dst_ref=outs[a].at[j], send_sem=send_sems.at[a, j],
                                            recv_sem=recv_sems.at[a, j], device_id=(px, py, c), device_id_type=MESH)

    def start():
        for a in range(n):
            for j in range(3):
                copy(a, j).start()

    def finish():
        for a in range(n):
            for j in range(3):
                copy(a, j).wait()

    return start, finish


def _join_halves(fulls):
    n = len(fulls)

    def body(*refs):
        ins, outs = refs[:n], refs[n:2 * n]
        send_sems, recv_sems = refs[2 * n:]
        x, y, c = _place()
        started = []
        for a in range(n):
            half = ins[a].shape[0] // 2
            rows = pl.ds(c * half, half)
            cp = pltpu.make_async_remote_copy(src_ref=ins[a].at[rows, :], dst_ref=outs[a].at[rows, :], send_sem=send_sems.at[a],
                                              recv_sem=recv_sems.at[a], device_id=(x, y, 1 - c), device_id_type=MESH)
            cp.start()
            started.append(cp)
        for cp in started:
            cp.wait()

    return pl.pallas_call(
        body, name="join_halves",
        in_specs=[_ANY] * n, out_specs=[_ANY] * n,
        out_shape=[jax.ShapeDtypeStruct(f.shape, F32) for f in fulls],
        input_output_aliases={a: a for a in range(n)},
        scratch_shapes=[pltpu.SemaphoreType.DMA((n,)), pltpu.SemaphoreType.DMA((n,))],
    )(*fulls)


def _small_allreduce(g):
    R = g.shape[0]
    half = R // 2

    def body(g_ref, out_ref, other_s, chip_s, parts_s, send_sems, recv_sems):
        x, y, c = _place()
        mine = 2 * x + y
        rows = pl.ds(pl.multiple_of(c * half, SUBLANES), half)

        def to_other_core(src, dst, k):
            return pltpu.make_async_remote_copy(src_ref=src, dst_ref=dst, send_sem=send_sems.at[k], recv_sem=recv_sems.at[k],
                                                device_id=(x, y, 1 - c), device_id_type=MESH)

        swap = to_other_core(g_ref, other_s, 0)
        swap.start()
        swap.wait()
        chip_s[...] = g_ref[...] + other_s[...]
        parts_s[mine] = chip_s[rows, :]
        sends = []
        for j, (px, py) in enumerate(_other_chips(x, y)):
            cp = pltpu.make_async_remote_copy(src_ref=chip_s.at[rows, :], dst_ref=parts_s.at[mine], send_sem=send_sems.at[1 + j],
                                              recv_sem=recv_sems.at[1 + j], device_id=(px, py, c), device_id_type=MESH)
            cp.start()
            sends.append(cp)
        for cp in sends:
            cp.wait()
        out_ref[rows, :] = (parts_s[0] + parts_s[1]) + (parts_s[2] + parts_s[3])
        join = to_other_core(out_ref.at[rows, :], out_ref.at[rows, :], 4)
        join.start()
        join.wait()

    vm = pl.BlockSpec(memory_space=pltpu.VMEM)
    return pl.pallas_call(
        body, name="small_allreduce",
        in_specs=[vm], out_specs=vm, out_shape=jax.ShapeDtypeStruct((R, LANES), F32),
        scratch_shapes=[pltpu.VMEM((R, LANES), F32), pltpu.VMEM((R, LANES), F32), pltpu.VMEM((4, half, LANES), F32),
                        pltpu.SemaphoreType.DMA((5,)), pltpu.SemaphoreType.DMA((5,))],
        compiler_params=pltpu.CompilerParams(vmem_limit_bytes=VMEM_LIMIT),
    )(g)


def _adamw_small(ws, gs, ms, vs):
    n = len(ws)

    def body(*refs):
        for k in range(n):
            w_ref, g_ref, m_ref, v_ref = (refs[j * n + k] for j in range(4))
            d, nm, nv = _adamw_math(w_ref[...], g_ref[...], m_ref[...], v_ref[...])
            refs[4 * n + k][...] = d
            refs[5 * n + k][...] = nm
            refs[6 * n + k][...] = nv

    vm = pl.BlockSpec(memory_space=pltpu.VMEM)
    out = pl.pallas_call(
        body, name="adamw_small",
        in_specs=[vm] * (4 * n), out_specs=[vm] * (3 * n),
        out_shape=[jax.ShapeDtypeStruct(w.shape, F32) for w in ws] * 3,
        compiler_params=pltpu.CompilerParams(vmem_limit_bytes=VMEM_LIMIT),
    )(*ws, *gs, *ms, *vs)
    return out[:n], out[n:2 * n], out[2 * n:]


_SMALL = (("norm_mix_g", D_MODEL), ("f_bias", N_HEADS), ("sg_ln_g", D_HEADS), ("sg_w", N_HEADS * SG_BLOCK * SG_BLOCK),
          ("sg_b", N_HEADS * SG_BLOCK), ("norm_ffn_g", D_MODEL), ("w_conv", 3 * 2 * D_FF), ("b_conv", 2 * D_FF),
          ("norm_final_g", D_MODEL))
_PACKED = _SMALL + (("sq_err", D_MODEL),)


def _pack_small(parts):
    rows = []
    for name, size in _PACKED:
        flat = parts[name].reshape(-1).astype(F32)
        pad = (-size) % (SUBLANES * LANES)
        rows.append(jnp.pad(flat, (0, pad)).reshape(-1, LANES))
    packed = jnp.concatenate(rows, axis=0)
    return jnp.pad(packed, ((0, (-packed.shape[0]) % (2 * SUBLANES)), (0, 0)))


def _unpack_small(packed, shapes):
    out, r = {}, 0
    for name, size in _PACKED:
        nrows = (size + SUBLANES * LANES - 1) // (SUBLANES * LANES) * SUBLANES
        out[name] = packed[r:r + nrows].reshape(-1)[:size].reshape(shapes[name])
        r += nrows
    return out


def _local_step(x, target, g1, w_in, f_bias, sg_ln_g, sg_w, sg_b, g2, b_conv, g3, late_shards, ids):
    S = x.shape[0]
    tm = _row_tile(S, 512)
    tms = _row_tile(S, 256)
    tq = tm

    lane = jnp.arange(D_HEADS)
    seg_avg = jnp.where(lane[:, None] // HEAD_DIM == lane[None, :] // HEAD_DIM, 1.0 / HEAD_DIM, 0.0).astype(BF16)
    head_ind = (lane[:, None] // HEAD_DIM == jnp.arange(LANES)[None, :]).astype(BF16)
    pos_chunk = jnp.arange(SG_BLOCK) // CHUNK
    w_mask32 = jnp.where(pos_chunk[:, None] >= pos_chunk[None, :], sg_w, 0.0)
    w_mask = w_mask32.astype(BF16)
    w_mask_t = jnp.swapaxes(w_mask32, 1, 2).astype(BF16)
    ln_row = sg_ln_g.reshape(1, D_HEADS)
    b_full = jnp.repeat(sg_b.T, HEAD_DIM, axis=1)
    bias_row = jnp.pad(f_bias.reshape(1, N_HEADS), ((0, 0), (0, LANES - N_HEADS)))
    b_conv_row = b_conv.reshape(1, 2 * D_FF)

    z, f, h1 = _in_proj(x, g1, w_in, tm)
    c = _fox_prep(f, bias_row, tm)
    consts = _attn_consts()
    qa, ka, va, vat = _attn_pack(z, c, consts, tm)
    out_b, lse, gathered = _attn_fwd(qa, ka, vat, consts["place_t"], tq, late_shards)
    g_out, w_up_q, g_down, g_conv = gathered
    w_out = g_out.reshape(D_MODEL, D_MODEL)
    w_down = g_down.reshape(D_FF, D_MODEL)
    w_conv = jnp.concatenate([g_conv[q] for q in range(4)], axis=1)
    out_a = _gate_fwd(z, w_mask, ln_row, b_full, seg_avg, tm)
    x1, h2 = _mix_out(x, out_a, out_b, w_out, g2, tm)
    a = _up_proj(h2, w_up_q, tm)
    dx2, sq_err, dg3 = _ffn_fwd_loss(a, w_conv, b_conv_row, w_down, x1, g3, target, tm)

    dconv, y, dw_conv8, db_conv = _ffn_bwd_gate(dx2, a, w_conv, b_conv_row, w_down, tms)
    dact = _conv_bwd(dconv, w_conv, tms, D_FF)
    dw_down = _matmul_tn(y, dx2, "dw_down", D_FF // 2, D_MODEL, tm, quarters=(2, 1))
    dx1, dg2 = _up_bwd(dact, w_up_q, x1, g2, dx2, tm)
    dw_up_q = _matmul_tn(h2, dact, "dw_up", D_MODEL, 2 * D_FF // 4, tm, quarters=(1, 4))
    dcat = _out_bwd(dx1, w_out, tm)
    dw_out = _dw_out(out_a, out_b, dx1, tm)
    early = {"w_down": dw_down.reshape(4, D_FF // 4, D_MODEL), "w_up": dw_up_q,
             "w_out": dw_out.reshape(4, D_MODEL // 4, D_MODEL)}
    (dzu, dzv, dsg_w, dsg_b_t, dln), theirs = _gate_bwd(z, dcat, w_mask, w_mask_t, ln_row, b_full, seg_avg, head_ind, tm,
                                                        list(early.values()))
    early_sums = _chip_sums(early, theirs, ids)
    dop, qb, dopt, qbt = _attn_pack_grad(out_b, dcat, qa, lse, head_ind, consts, tm)
    dqp, dc_rows, dk, dv, dc_cols, landed = _attn_bwd(qb, qbt, ka, va, dop, dopt, consts, tq,
                                                      [s16 for _, s16 in early_sums.values()])
    early_parts = {k: (s32, got) for (k, (s32, _)), got in zip(early_sums.items(), landed)}
    dq = _attn_unpack(dqp, consts, tm)
    dc_rows = jnp.concatenate([dc_rows[g][:, :GROUP_HEADS] for g in range(GROUPS)], axis=1)
    dc_cols = jnp.concatenate([dc_cols[g][:GROUP_HEADS] for g in range(GROUPS)], axis=0).T
    dc = jnp.pad(dc_rows - dc_cols, ((0, 0), (0, LANES - N_HEADS)))
    df, dbias = _fox_bwd(dc, f, bias_row, tm)
    pieces = (dzu, dzv, dq, dk, dv, df)
    dw_in = _dw_in(h1, pieces, tm)[:, :D_IN].reshape(D_MODEL, 4, D_IN // 4).transpose(1, 0, 2)
    (w_in_sum, w_in_sum16), = _chip_sums({"w_in": dw_in}, _swap_halves([dw_in], "swap_halves"), ids).values()
    dx, dg1, (w_in_landed,) = _in_bwd(pieces, w_in, x, g1, dx1, tm, [w_in_sum16])

    grads = {
        "norm_mix_g": dg1, "f_bias": dbias[:, :N_HEADS], "sg_ln_g": dln, "sg_w": dsg_w, "sg_b": dsg_b_t[:, :N_HEADS].T,
        "norm_ffn_g": dg2, "w_conv": dw_conv8[:3], "b_conv": db_conv, "norm_final_g": dg3,
    }
    return sq_err, dx, grads, {**early_parts, "w_in": (w_in_sum, w_in_landed)}


def _chip_sums(grads_q, theirs, ids):
    return {k: ((g, t), _pair_sum("pair_sum_" + k, g, t, ids)) for (k, g), t in zip(grads_q.items(), theirs)}


def _finish_reduction(parts, ids):
    names = list(parts)
    fulls = [_chip_sum("chip_sum_" + k, g, t, got, ids) for k, ((g, t), got) in parts.items()]
    return dict(zip(names, _join_halves(fulls)))


def kernel(x, norm_mix_g, w_in, f_bias, sg_ln_g, sg_w, sg_b, w_out, norm_ffn_g, w_up, w_conv, b_conv, w_down, norm_final_g, loss_target, m_norm_mix_g, m_w_in, m_f_bias, m_sg_ln_g, m_sg_w, m_sg_b, m_w_out, m_norm_ffn_g, m_w_up, m_w_conv, m_b_conv, m_w_down, m_norm_final_g, v_norm_mix_g, v_w_in, v_f_bias, v_sg_ln_g, v_sg_w, v_sg_b, v_w_out, v_norm_ffn_g, v_w_up, v_w_conv, v_b_conv, v_w_down, v_norm_final_g):
    args = dict(locals())
    quarter = 2 * lax.axis_index("x") + lax.axis_index("y")
    ids = jnp.stack([quarter, lax.axis_index("c")]).astype(jnp.int32)
    wq_conv = w_conv.shape[-1]

    g_in = _gather_quarters([w_in[0].astype(BF16)])[0]
    w_in_full = jnp.pad(jnp.concatenate([g_in[q] for q in range(4)], axis=1), ((0, 0), (0, D_IN_PAD - D_IN)))
    late_shards = [w_out[0].astype(BF16), w_up[0].astype(BF16), w_down[0].astype(BF16), w_conv[0]]

    sq_err, dx, grads, parts = _local_step(
        x[0], loss_target[0], norm_mix_g, w_in_full, f_bias[0], sg_ln_g[0], sg_w[0], sg_b[0], norm_ffn_g, b_conv[0],
        norm_final_g.reshape(1, D_MODEL), late_shards, ids)
    big = _finish_reduction(parts, ids)

    out = {"grad_x": dx[None]}
    for k in ("w_in", "w_out", "w_up", "w_down"):
        g, d, nm, nv = _adamw("adamw_" + k, args[k][0], big[k], args["m_" + k][0], args["v_" + k][0])
        out["grad_" + k], out["delta_" + k], out["new_m_" + k], out["new_v_" + k] = g[None], d[None], nm[None], nv[None]

    small_names = [n for n, _ in _SMALL]
    shapes = {n: (3, 4 * wq_conv) if n == "w_conv" else args[n].shape for n in small_names}
    shapes["sq_err"] = sq_err.shape
    g_small = _unpack_small(_small_allreduce(_pack_small({**{n: grads[n] for n in small_names}, "sq_err": sq_err})), shapes)
    out["loss"] = 0.5 * jnp.sum(g_small.pop("sq_err")) / D_MODEL
    g_small["w_conv"] = lax.dynamic_slice(g_small["w_conv"], (0, quarter * wq_conv), (3, wq_conv))[None]
    flat2d = lambda t: t.reshape(-1, t.shape[-1])
    updated = _adamw_small(*[[flat2d(src[p + n]) for n in small_names] for src, p in
                             ((args, ""), (g_small, ""), (args, "m_"), (args, "v_"))])
    for n, g in g_small.items():
        out["grad_" + n] = g
    for prefix, arrs in zip(("delta_", "new_m_", "new_v_"), updated):
        for n, t in zip(small_names, arrs):
            out[prefix + n] = t.reshape(args[n].shape)

    weights = ["norm_mix_g", "w_in", "f_bias", "sg_ln_g", "sg_w", "sg_b", "w_out", "norm_ffn_g", "w_up", "w_conv", "b_conv",
               "w_down", "norm_final_g"]
    return (out["loss"], out["grad_x"], *[out[p + n] for p in ("grad_", "delta_", "new_m_", "new_v_") for n in weights])
```

```python
import functools
import math

import jax
import jax.numpy as jnp
from jax import lax
from jax.experimental import pallas as pl
from jax.experimental.pallas import tpu as pltpu

F32 = jnp.float32
BF16 = jnp.bfloat16
MESH = pl.DeviceIdType.MESH

D_MODEL = 1024
N_HEADS = 8
HEAD_DIM = 64
D_HEADS = N_HEADS * HEAD_DIM
SG_BLOCK = 128
CHUNK = 64
D_FF = 2816
D_IN = 2 * D_HEADS + 3 * D_HEADS + N_HEADS
LANES = 128
SUBLANES = 8
D_IN_PAD = 5 * D_HEADS + LANES
EPS = 1e-6
SCALE = HEAD_DIM ** -0.5
NEG = -1e30
LOG2E = 1.4426950408889634
HEAD_PAD = LANES
D_PAD = N_HEADS * HEAD_PAD
Q_STAT = HEAD_DIM
K_STAT = HEAD_DIM + 3
L_STAT = HEAD_DIM + 6
GROUPS = 2
GROUP_HEADS = N_HEADS // GROUPS
GROUP_PAD = GROUP_HEADS * HEAD_PAD
KEY_CHUNK = 256
FWD_KEY_CHUNK = 512
STAT_ROWS = 16
FF_CHUNK = 256

ADAM_LR = 0.001
ADAM_B1 = 0.9
ADAM_B2 = 0.999
ADAM_EPS = 1e-08
ADAM_WD = 0.01
ADAM_STEP = 10

VMEM_LIMIT = 56 * 1024 * 1024

NT = (((1,), (1,)), ((), ()))
TN = (((0,), (0,)), ((), ()))


def _params(sem):
    return pltpu.CompilerParams(dimension_semantics=sem, vmem_limit_bytes=VMEM_LIMIT)


def _full(shape):
    nd = len(shape)
    return pl.BlockSpec(shape, lambda *_: (0,) * nd)


def _row_tile(rows, target):
    best = None
    for t in range(SUBLANES, min(rows, target) + 1, SUBLANES):
        if rows % t == 0:
            best = t
    assert best is not None, rows
    return best


def _sigmoid(x):
    return 0.5 * jnp.tanh(0.5 * x) + 0.5


def _gelu(z):
    return 0.5 * z * (1.0 + lax.erf(z * (2.0 ** -0.5)))


def _gelu_grad(z):
    cdf = 0.5 * (1.0 + lax.erf(z * (2.0 ** -0.5)))
    pdf = jnp.exp(-0.5 * z * z) * (1.0 / math.sqrt(2.0 * math.pi))
    return cdf + z * pdf


def _split_dot(x, m):
    hi = x.astype(BF16)
    lo = (x - hi.astype(F32)).astype(BF16)
    return jnp.dot(hi, m, preferred_element_type=F32) + jnp.dot(lo, m, preferred_element_type=F32)


def _head_mask(h, rows):
    lane = lax.broadcasted_iota(jnp.int32, (rows, D_HEADS), 1)
    return (lane >= h * HEAD_DIM) & (lane < (h + 1) * HEAD_DIM)


def _rms_bwd(dh, x, g):
    r = lax.rsqrt(jnp.mean(x * x, axis=-1, keepdims=True) + EPS)
    xhat = x * r
    dg = jnp.sum(dh * xhat, axis=0, keepdims=True)
    dxhat = dh * g
    dx = r * (dxhat - xhat * jnp.mean(dxhat * xhat, axis=-1, keepdims=True))
    return dx, dg


def _in_proj(x, g1, w_in, tm):
    S = x.shape[0]
    nz = D_IN_PAD - LANES

    def body(x_ref, g_ref, w_ref, z_ref, f_ref, h_ref):
        xf = x_ref[...]
        r = lax.rsqrt(jnp.mean(xf * xf, axis=-1, keepdims=True) + EPS)
        h = (xf * r * g_ref[...]).astype(BF16)
        h_ref[...] = h
        zz = jnp.dot(h, w_ref[...], preferred_element_type=F32)
        z_ref[...] = zz[:, :nz].astype(BF16)
        f_ref[...] = zz[:, nz:]

    return pl.pallas_call(
        body, name="in_proj", grid=(S // tm,),
        in_specs=[pl.BlockSpec((tm, D_MODEL), lambda i: (i, 0)), _full((1, D_MODEL)), _full((D_MODEL, D_IN_PAD))],
        out_specs=[pl.BlockSpec((tm, nz), lambda i: (i, 0)), pl.BlockSpec((tm, LANES), lambda i: (i, 0)),
                   pl.BlockSpec((tm, D_MODEL), lambda i: (i, 0))],
        out_shape=[jax.ShapeDtypeStruct((S, nz), BF16), jax.ShapeDtypeStruct((S, LANES), F32),
                   jax.ShapeDtypeStruct((S, D_MODEL), BF16)],
        compiler_params=_params(("parallel",)),
    )(x, g1, w_in)


def _fox_prep(f, bias_row, tb):
    S = f.shape[0]

    def body(f_ref, b_ref, c_ref, carry):
        @pl.when(pl.program_id(0) == 0)
        def _():
            carry[...] = jnp.zeros_like(carry)

        xv = f_ref[...] + b_ref[...]
        lf = jnp.minimum(xv, 0.0) - jnp.log(1.0 + jnp.exp(-jnp.abs(xv)))
        r = lax.broadcasted_iota(jnp.int32, (tb, tb), 0)
        s = lax.broadcasted_iota(jnp.int32, (tb, tb), 1)
        tri = (r >= s).astype(F32)
        cs = jnp.dot(tri, lf, precision=lax.Precision.HIGHEST, preferred_element_type=F32) + carry[0:1, :]
        c_ref[...] = cs
        carry[...] = jnp.broadcast_to(cs[tb - 1:tb, :], carry.shape)

    return pl.pallas_call(
        body, name="fox_prep", grid=(S // tb,),
        in_specs=[pl.BlockSpec((tb, LANES), lambda i: (i, 0)), _full((1, LANES))],
        out_specs=pl.BlockSpec((tb, LANES), lambda i: (i, 0)),
        out_shape=jax.ShapeDtypeStruct((S, LANES), F32),
        scratch_shapes=[pltpu.VMEM((SUBLANES, LANES), F32)],
        compiler_params=_params(("arbitrary",)),
    )(f, bias_row)


def _attn_consts():
    col = jnp.arange(D_PAD)
    row = jnp.arange(D_HEADS)
    head = jnp.arange(LANES)
    place = (row[:, None] // HEAD_DIM == col[None, :] // HEAD_PAD) & (row[:, None] % HEAD_DIM == col[None, :] % HEAD_PAD)

    def stat(offset):
        return ((head[:, None] < N_HEADS) & (col[None, :] == head[:, None] * HEAD_PAD + offset)).astype(BF16)

    def stat3(base):
        part, h = head // N_HEADS, head % N_HEADS
        return ((part[:, None] < 3) & (col[None, :] == h[:, None] * HEAD_PAD + base + part[:, None])).astype(BF16)

    def ones(offsets):
        return sum((col % HEAD_PAD == o) for o in offsets).astype(F32).reshape(1, D_PAD)

    place = place.astype(BF16)
    return {
        "place": place, "place_t": place.T, "place_t_group": place.T[:GROUP_PAD, :GROUP_HEADS * HEAD_DIM],
        "q_stat": stat3(Q_STAT), "k_stat": stat3(K_STAT),
        "d_stat": stat3(Q_STAT) * (head[:, None] < 2 * N_HEADS).astype(BF16),
        "l_stat": jnp.concatenate([stat(L_STAT + j)[:STAT_ROWS] for j in range(3)], axis=0),
        "q_ones": ones(range(K_STAT, K_STAT + 3)), "k_ones": ones(list(range(Q_STAT, Q_STAT + 3)) + list(range(L_STAT, L_STAT + 3))),
        "v_ones": ones(range(Q_STAT, Q_STAT + 2)),
    }


def _split3(x):
    hi = x.astype(BF16)
    r = x - hi.astype(F32)
    mid = r.astype(BF16)
    return hi, mid, (r - mid.astype(F32)).astype(BF16)


def _attn_pack(z, c, k, tm):
    S = z.shape[0]

    def body(q_ref, k_ref, v_ref, c_ref, pl_ref, pt_ref, qs_ref, ks_ref, qo_ref, ko_ref, vo_ref, voc_ref,
             qa_ref, ka_ref, va_ref, vt_ref):
        place = pl_ref[...]
        q = (q_ref[...].astype(F32) * (SCALE * LOG2E)).astype(BF16)
        qa = jnp.dot(q, place, preferred_element_type=F32) + qo_ref[...]
        ka = jnp.dot(k_ref[...], place, preferred_element_type=F32) + ko_ref[...]
        lane = lax.broadcasted_iota(jnp.int32, (tm, LANES), 1)
        hi, mid, lo = _split3(jnp.where(lane < N_HEADS, c_ref[...] * LOG2E, 0.0))
        parts = hi.astype(F32) + pltpu.roll(mid.astype(F32), N_HEADS, 1) + pltpu.roll(lo.astype(F32), 2 * N_HEADS, 1)
        parts = parts.astype(BF16)
        qa = qa + jnp.dot(parts, qs_ref[...], preferred_element_type=F32)
        ka = ka - jnp.dot(parts, ks_ref[...], preferred_element_type=F32)
        qa_ref[...] = qa.astype(BF16)
        ka_ref[...] = ka.astype(BF16)
        v = v_ref[...]
        va_ref[...] = (jnp.dot(v, place, preferred_element_type=F32) + vo_ref[...]).astype(BF16)
        vt_ref[...] = (lax.dot_general(pt_ref[...], v, NT, preferred_element_type=F32) + voc_ref[...]).astype(BF16)

    blk = lambda col: pl.BlockSpec((tm, D_HEADS), lambda i: (i, col))
    out = pl.BlockSpec((tm, D_PAD), lambda i: (i, 0))
    pad = jax.ShapeDtypeStruct((S, D_PAD), BF16)
    return pl.pallas_call(
        body, name="attn_pack", grid=(S // tm,),
        in_specs=[blk(2), blk(3), blk(4), pl.BlockSpec((tm, LANES), lambda i: (i, 0)), _full((D_HEADS, D_PAD)), _full((D_PAD, D_HEADS)),
                  _full((LANES, D_PAD)), _full((LANES, D_PAD)), _full((1, D_PAD)), _full((1, D_PAD)), _full((1, D_PAD)),
                  _full((D_PAD, 1))],
        out_specs=[out, out, out, pl.BlockSpec((None, D_PAD, tm), lambda i: (i, 0, 0))],
        out_shape=[pad, pad, pad, jax.ShapeDtypeStruct((S // tm, D_PAD, tm), BF16)],
        compiler_params=_params(("parallel",)),
    )(z, z, z, c, k["place"], k["place_t"], k["q_stat"], k["k_stat"], k["q_ones"], k["k_ones"], k["v_ones"], k["v_ones"].T)


def _attn_fwd(qa, ka, vat, place_t, tq, shards):
    S = qa.shape[0]
    n = S // tq
    ns = len(shards)
    hand_on_at = (2 * n) // 3

    pairs = [(q, k) for q in range(n) for k in range(q + 1)]
    q_of = jnp.asarray([q for q, _ in pairs], jnp.int32)
    k_of = jnp.asarray([k for _, k in pairs], jnp.int32)

    def body(q_of_ref, k_of_ref, q_ref, k_ref, vt_ref, pt_ref, *rest):
        o_ref, lse_ref = rest[ns:ns + 2]
        m_s, acc_s, ot_s = rest[2 * ns + 2:2 * ns + 5]
        s_s = rest[2 * ns + 5:2 * ns + 7]
        start, hand_on, finish = _gather_ops(rest[:ns], rest[ns + 2:2 * ns + 2], *rest[2 * ns + 7:])
        qi, ki = q_of_ref[pl.program_id(0)], k_of_ref[pl.program_id(0)]

        @pl.when((qi == 0) & (ki == 0))
        def _():
            start()

        @pl.when((qi == hand_on_at) & (ki == 0))
        def _():
            hand_on()

        @pl.when(ki == 0)
        def _():
            m_s[...] = jnp.full_like(m_s, NEG)
            acc_s[...] = jnp.zeros_like(acc_s)

        def step(diagonal):
            kc = FWD_KEY_CHUNK
            chunks = [slice(c * kc, (c + 1) * kc) for c in range(tq // kc)]

            def scores(h, rows, slot):
                sl = slice(h * HEAD_PAD, (h + 1) * HEAD_PAD)
                st = lax.dot_general(k_ref[rows, sl], q_ref[:, sl], NT, preferred_element_type=F32)
                if diagonal:
                    key = rows.start + lax.broadcasted_iota(jnp.int32, (kc, tq), 0)
                    query = lax.broadcasted_iota(jnp.int32, (kc, tq), 1)
                    st = jnp.where(query >= key, st, NEG)
                s_s[slot][rows, :] = st
                return jnp.max(st, axis=0, keepdims=True)

            m_cur = functools.reduce(jnp.maximum, [scores(0, rows, 0) for rows in chunks])
            for h in range(N_HEADS):
                sl = slice(h * HEAD_PAD, (h + 1) * HEAD_PAD)
                slot = h % 2
                m_prev = m_s[h][0:1, :]
                m_new = jnp.maximum(m_prev, m_cur)
                acc = jnp.exp2(m_prev - m_new) * acc_s[h]
                m_next = []
                for rows in chunks:
                    if h + 1 < N_HEADS:
                        m_next.append(scores(h + 1, rows, 1 - slot))
                    pt = jnp.exp2(s_s[slot][rows, :] - m_new).astype(BF16)
                    acc = acc + jnp.dot(vt_ref[sl, rows], pt, preferred_element_type=F32)
                acc_s[h] = acc
                m_s[h] = jnp.broadcast_to(m_new, (SUBLANES, tq))
                if m_next:
                    m_cur = functools.reduce(jnp.maximum, m_next)

        @pl.when(ki < qi)
        def _():
            step(False)

        @pl.when(ki == qi)
        def _():
            step(True)
            lse_ref[...] = jnp.zeros_like(lse_ref)
            for h in range(N_HEADS):
                acc = acc_s[h]
                denom = acc[Q_STAT:Q_STAT + 1, :]
                ot_s[h * HEAD_PAD:(h + 1) * HEAD_PAD, :] = (acc / denom).astype(BF16)
                lse_ref[h:h + 1, :] = m_s[h][0:1, :] + jnp.log(denom) * LOG2E
            o_ref[...] = lax.dot_general(ot_s[...], pt_ref[...], TN, preferred_element_type=F32).astype(BF16)

        @pl.when((qi == n - 1) & (ki == n - 1))
        def _():
            finish()

    out = pl.pallas_call(
        body, name="attn_fwd",
        grid_spec=pltpu.PrefetchScalarGridSpec(
            num_scalar_prefetch=2, grid=(len(pairs),),
            in_specs=[pl.BlockSpec((tq, D_PAD), lambda i, qs, ks: (qs[i], 0)),
                      pl.BlockSpec((tq, D_PAD), lambda i, qs, ks: (ks[i], 0)),
                      pl.BlockSpec((None, D_PAD, tq), lambda i, qs, ks: (ks[i], 0, 0)),
                      pl.BlockSpec((D_PAD, D_HEADS), lambda i, qs, ks: (0, 0))]
            + [_ANY] * ns,
            out_specs=[pl.BlockSpec((tq, D_HEADS), lambda i, qs, ks: (qs[i], 0)),
                       pl.BlockSpec((STAT_ROWS, tq), lambda i, qs, ks: (0, qs[i]))] + [_ANY] * ns,
            scratch_shapes=[pltpu.VMEM((N_HEADS, SUBLANES, tq), F32), pltpu.VMEM((N_HEADS, HEAD_PAD, tq), F32),
                            pltpu.VMEM((D_PAD, tq), BF16), pltpu.VMEM((tq, tq), F32), pltpu.VMEM((tq, tq), F32)] + _gather_sems(ns)),
        out_shape=[jax.ShapeDtypeStruct((S, D_HEADS), BF16), jax.ShapeDtypeStruct((STAT_ROWS, S), F32)] + _gather_shapes(shards),
        compiler_params=_params(("arbitrary",)),
    )(q_of, k_of, qa, ka, vat, place_t, *shards)
    return out[0], out[1], out[2:]


def _layer_norm_heads(v, seg_avg):
    mu = _split_dot(v, seg_avg)
    d = v - mu
    var = jnp.dot((d * d).astype(BF16), seg_avg, preferred_element_type=F32)
    rstd = lax.rsqrt(var + EPS)
    return d * rstd, rstd


def _gate_mix(vn_blk, w_ref, bias):
    acc = bias
    for h in range(N_HEADS):
        vh = jnp.where(_head_mask(h, SG_BLOCK), vn_blk, 0.0).astype(BF16)
        acc = acc + jnp.dot(w_ref[h], vh, preferred_element_type=F32)
    return acc


def _gate_fwd(z, w_mask, ln_row, b_full, seg_avg, tm):
    S = z.shape[0]

    def body(zu_ref, zv_ref, w_ref, ln_ref, b_ref, avg_ref, o_ref):
        u = _gelu(zu_ref[...].astype(F32))
        v = _gelu(zv_ref[...].astype(F32))
        vhat, _ = _layer_norm_heads(v, avg_ref[...])
        vn = vhat * ln_ref[...]
        for b in range(tm // SG_BLOCK):
            rows = slice(b * SG_BLOCK, (b + 1) * SG_BLOCK)
            mixed = _gate_mix(vn[rows], w_ref, b_ref[...])
            o_ref[rows, :] = (u[rows] * mixed).astype(BF16)

    return pl.pallas_call(
        body, name="gate_fwd", grid=(S // tm,),
        in_specs=[pl.BlockSpec((tm, D_HEADS), lambda i: (i, 0)), pl.BlockSpec((tm, D_HEADS), lambda i: (i, 1)),
                  _full((N_HEADS, SG_BLOCK, SG_BLOCK)), _full((1, D_HEADS)), _full((SG_BLOCK, D_HEADS)),
                  _full((D_HEADS, D_HEADS))],
        out_specs=pl.BlockSpec((tm, D_HEADS), lambda i: (i, 0)),
        out_shape=jax.ShapeDtypeStruct((S, D_HEADS), BF16),
        compiler_params=_params(("parallel",)),
    )(z, z, w_mask, ln_row, b_full, seg_avg)


def _mix_out(x, out_a, out_b, w_out, g2, tm):
    S = x.shape[0]

    def body(x_ref, a_ref, b_ref, w_ref, g_ref, x1_ref, h_ref):
        y = jnp.dot(a_ref[...], w_ref[:D_HEADS, :], preferred_element_type=F32)
        y = y + jnp.dot(b_ref[...], w_ref[D_HEADS:, :], preferred_element_type=F32)
        x1 = x_ref[...] + y
        x1_ref[...] = x1
        r = lax.rsqrt(jnp.mean(x1 * x1, axis=-1, keepdims=True) + EPS)
        h_ref[...] = (x1 * r * g_ref[...]).astype(BF16)

    row = lambda w: pl.BlockSpec((tm, w), lambda i: (i, 0))
    return pl.pallas_call(
        body, name="mix_out", grid=(S // tm,),
        in_specs=[row(D_MODEL), row(D_HEADS), row(D_HEADS), _full((D_MODEL, D_MODEL)), _full((1, D_MODEL))],
        out_specs=[row(D_MODEL), row(D_MODEL)],
        out_shape=[jax.ShapeDtypeStruct((S, D_MODEL), F32), jax.ShapeDtypeStruct((S, D_MODEL), BF16)],
        compiler_params=_params(("parallel",)),
    )(x, out_a, out_b, w_out, g2)


def _up_proj(h2, w_up_q, tm):
    S = h2.shape[0]
    nq, _, wq = w_up_q.shape

    def body(h_ref, w_ref, a_ref):
        a_ref[...] = jnp.dot(h_ref[...], w_ref[...], preferred_element_type=F32).astype(BF16)

    return pl.pallas_call(
        body, name="up_proj", grid=(nq, S // tm),
        in_specs=[pl.BlockSpec((tm, D_MODEL), lambda j, i: (i, 0)), pl.BlockSpec((None, D_MODEL, wq), lambda j, i: (j, 0, 0))],
        out_specs=pl.BlockSpec((tm, wq), lambda j, i: (i, j)),
        out_shape=jax.ShapeDtypeStruct((S, nq * wq), BF16),
        compiler_params=_params(("parallel", "parallel")),
    )(h2, w_up_q)


def _shift_down(a, halo, k):
    tm = a.shape[0]
    ra = pltpu.roll(a, k, 0)
    rh = pltpu.roll(halo, k, 0)
    row = lax.broadcasted_iota(jnp.int32, halo.shape, 0)
    top = jnp.where(row < k, rh, ra[0:SUBLANES])
    return jnp.concatenate([top, ra[SUBLANES:tm]], axis=0)


def _shift_matrices(tm):
    row = lax.broadcasted_iota(jnp.int32, (tm, tm), 0)
    col = lax.broadcasted_iota(jnp.int32, (tm, tm), 1)
    return [(row == col + k).astype(BF16) for k in (1, 2)]


def _conv_taps(a, halo, first, shifts):
    tm = a.shape[0]
    halo = halo.astype(F32) * jnp.where(first, 0.0, 1.0)
    if shifts is None:
        a = a.astype(F32)
        return a, _shift_down(a, halo, 1), _shift_down(a, halo, 2)
    row8 = lax.broadcasted_iota(jnp.int32, halo.shape, 0)
    taps = [a.astype(F32)]
    for k, shift in zip((1, 2), shifts):
        down = jnp.dot(shift, a, preferred_element_type=F32)
        top = down[0:SUBLANES] + jnp.where(row8 < k, pltpu.roll(halo, k, 0), 0.0)
        taps.append(jnp.concatenate([top, down[SUBLANES:tm]], axis=0))
    return taps


def _conv_gate_val(refs, shifts, cols, first):
    ag_ref, av_ref, hg_ref, hv_ref, wg_ref, wv_ref, bg_ref, bv_ref = refs
    g0, g1, g2 = _conv_taps(ag_ref[:, cols], hg_ref[:, cols], first, shifts)
    gate = wg_ref[2:3, cols] * g0 + wg_ref[1:2, cols] * g1 + wg_ref[0:1, cols] * g2 + bg_ref[:, cols]
    v0, v1, v2 = _conv_taps(av_ref[:, cols], hv_ref[:, cols], first, shifts)
    val = wv_ref[2:3, cols] * v0 + wv_ref[1:2, cols] * v1 + wv_ref[0:1, cols] * v2 + bv_ref[:, cols]
    return gate, val, (g2, g1, g0), (v2, v1, v0)


_FF_CHUNKS = [slice(j * FF_CHUNK, (j + 1) * FF_CHUNK) for j in range(D_FF // FF_CHUNK)]


def _conv_specs(tm):
    step = tm // SUBLANES
    prev = lambda i: jnp.maximum(i * step - 1, 0)
    return [pl.BlockSpec((tm, D_FF), lambda i: (i, 0)), pl.BlockSpec((tm, D_FF), lambda i: (i, 1)),
            pl.BlockSpec((SUBLANES, D_FF), lambda i: (prev(i), 0)), pl.BlockSpec((SUBLANES, D_FF), lambda i: (prev(i), 1))]


def _ffn_fwd_loss(a, w_conv, b_conv, w_down, x1, g3, target, tm):
    S = x1.shape[0]

    def body(ag_ref, av_ref, hg_ref, hv_ref, wg_ref, wv_ref, bg_ref, bv_ref, wd_ref, x1_ref, g_ref, t_ref,
             dx2_ref, loss_ref, dg_ref):
        i = pl.program_id(0)

        @pl.when(i == 0)
        def _():
            loss_ref[...] = jnp.zeros_like(loss_ref)
            dg_ref[...] = jnp.zeros_like(dg_ref)

        x2 = x1_ref[...]
        for cols in _FF_CHUNKS:
            gate, val, _, _ = _conv_gate_val((ag_ref, av_ref, hg_ref, hv_ref, wg_ref, wv_ref, bg_ref, bv_ref), None, cols, i == 0)
            half = 0.5 * gate
            y = ((half + half * jnp.tanh(half)) * val).astype(BF16)
            x2 = x2 + jnp.dot(y, wd_ref[cols, :], preferred_element_type=F32)
        r = lax.rsqrt(jnp.mean(x2 * x2, axis=-1, keepdims=True) + EPS)
        xhat = x2 * r
        gg = g_ref[...]
        err = xhat * gg - t_ref[...]
        loss_ref[...] += jnp.sum(err * err, axis=0, keepdims=True)
        dy = err * (1.0 / D_MODEL)
        dg_ref[...] += jnp.sum(dy * xhat, axis=0, keepdims=True)
        dxhat = dy * gg
        dx2_ref[...] = r * (dxhat - xhat * jnp.mean(dxhat * xhat, axis=-1, keepdims=True))

    row = lambda w: pl.BlockSpec((tm, w), lambda i: (i, 0))
    half = lambda r: [pl.BlockSpec((r, D_FF), lambda i: (0, 0)), pl.BlockSpec((r, D_FF), lambda i: (0, 1))]
    return pl.pallas_call(
        body, name="ffn_fwd_loss", grid=(S // tm,),
        in_specs=_conv_specs(tm) + half(3) + half(1) + [_full((D_FF, D_MODEL)), row(D_MODEL), _full((1, D_MODEL)), row(D_MODEL)],
        out_specs=[row(D_MODEL), _full((1, D_MODEL)), _full((1, D_MODEL))],
        out_shape=[jax.ShapeDtypeStruct((S, D_MODEL), F32), jax.ShapeDtypeStruct((1, D_MODEL), F32),
                   jax.ShapeDtypeStruct((1, D_MODEL), F32)],
        compiler_params=_params(("arbitrary",)),
    )(a, a, a, a, w_conv, w_conv, b_conv, b_conv, w_down, x1, g3, target)


def _ffn_bwd_gate(dx2, a, w_conv, b_conv, w_down, tm):
    S = dx2.shape[0]

    def body(dx_ref, ag_ref, av_ref, hg_ref, hv_ref, wg_ref, wv_ref, bg_ref, bv_ref, wd_ref,
             dc_ref, y_ref, dw_ref, db_ref):
        i = pl.program_id(0)

        @pl.when(i == 0)
        def _():
            dw_ref[...] = jnp.zeros_like(dw_ref)
            db_ref[...] = jnp.zeros_like(db_ref)

        dx = dx_ref[...].astype(BF16)
        shifts = _shift_matrices(tm)
        for cols in _FF_CHUNKS:
            gate, val, gtaps, vtaps = _conv_gate_val((ag_ref, av_ref, hg_ref, hv_ref, wg_ref, wv_ref, bg_ref, bv_ref), shifts, cols, i == 0)
            sg = _sigmoid(gate)
            act = gate * sg
            y_ref[:, cols] = (act * val).astype(BF16)
            dy = lax.dot_general(dx, wd_ref[cols, :], NT, preferred_element_type=F32)
            dgate = dy * val * (sg + act - act * sg)
            dval = dy * act
            for d, taps, out in ((dgate, gtaps, cols), (dval, vtaps, slice(D_FF + cols.start, D_FF + cols.stop))):
                dc_ref[:, out] = d.astype(BF16)
                db_ref[0:1, out] += jnp.sum(d, axis=0, keepdims=True)
                for j in range(3):
                    dw_ref[j:j + 1, out] += jnp.sum(d * taps[j], axis=0, keepdims=True)

    row = lambda w: pl.BlockSpec((tm, w), lambda i: (i, 0))
    half = lambda r: [pl.BlockSpec((r, D_FF), lambda i: (0, 0)), pl.BlockSpec((r, D_FF), lambda i: (0, 1))]
    return pl.pallas_call(
        body, name="ffn_bwd_gate", grid=(S // tm,),
        in_specs=[row(D_MODEL)] + _conv_specs(tm) + half(3) + half(1) + [_full((D_FF, D_MODEL))],
        out_specs=[row(2 * D_FF), row(D_FF), _full((SUBLANES, 2 * D_FF)), _full((1, 2 * D_FF))],
        out_shape=[jax.ShapeDtypeStruct((S, 2 * D_FF), BF16), jax.ShapeDtypeStruct((S, D_FF), BF16),
                   jax.ShapeDtypeStruct((SUBLANES, 2 * D_FF), F32), jax.ShapeDtypeStruct((1, 2 * D_FF), F32)],
        compiler_params=_params(("arbitrary",)),
    )(dx2, a, a, a, a, w_conv, w_conv, b_conv, b_conv, w_down)


def _conv_bwd(dc, w_conv, tm, tn):
    S, C = dc.shape
    step = tm // SUBLANES
    last_blk = S // SUBLANES - 1

    def body(d_ref, nx_ref, w_ref, o_ref):
        last = pl.program_id(0) == pl.num_programs(0) - 1
        row = lax.broadcasted_iota(jnp.int32, (tm, tm), 0)
        col = lax.broadcasted_iota(jnp.int32, (tm, tm), 1)
        row8 = lax.broadcasted_iota(jnp.int32, (SUBLANES, FF_CHUNK), 0)
        ups = [(row + k == col).astype(BF16) for k in (1, 2)]
        for c0 in range(0, tn, FF_CHUNK):
            cols = slice(c0, c0 + FF_CHUNK)
            d = d_ref[:, cols]
            nx = nx_ref[:, cols].astype(F32) * jnp.where(last, 0.0, 1.0)
            out = w_ref[2:3, cols] * d.astype(F32)
            for k, up in zip((1, 2), ups):
                moved = jnp.dot(up, d, preferred_element_type=F32)
                bottom = moved[tm - SUBLANES:tm] + jnp.where(row8 >= SUBLANES - k, pltpu.roll(nx, SUBLANES - k, 0), 0.0)
                out = out + w_ref[2 - k:3 - k, cols] * jnp.concatenate([moved[0:tm - SUBLANES], bottom], axis=0)
            o_ref[:, cols] = out.astype(BF16)

    return pl.pallas_call(
        body, name="conv_bwd", grid=(S // tm, C // tn),
        in_specs=[pl.BlockSpec((tm, tn), lambda i, j: (i, j)),
                  pl.BlockSpec((SUBLANES, tn), lambda i, j: (jnp.minimum((i + 1) * step, last_blk), j)),
                  pl.BlockSpec((3, tn), lambda i, j: (0, j))],
        out_specs=pl.BlockSpec((tm, tn), lambda i, j: (i, j)),
        out_shape=jax.ShapeDtypeStruct((S, C), BF16),
        compiler_params=_params(("parallel", "parallel")),
    )(dc, dc, w_conv)


def _matmul_tn(a, b, name, bm, bn, tk, quarters):
    S = a.shape[0]
    gm, gn = quarters
    nk = S // tk

    def body(a_ref, b_ref, o_ref):
        @pl.when(pl.program_id(2) == 0)
        def _():
            o_ref[...] = jnp.zeros_like(o_ref)

        o_ref[...] += lax.dot_general(a_ref[...].astype(BF16), b_ref[...].astype(BF16), TN, preferred_element_type=F32)

    if gn > 1:
        out_spec = pl.BlockSpec((None, bm, bn), lambda i, j, k: (j, i, 0))
        out_shape = jax.ShapeDtypeStruct((gn, gm * bm, bn), F32)
    else:
        out_spec = pl.BlockSpec((bm, bn), lambda i, j, k: (i, j))
        out_shape = jax.ShapeDtypeStruct((gm * bm, gn * bn), F32)
    return pl.pallas_call(
        body, name=name, grid=(gm, gn, nk),
        in_specs=[pl.BlockSpec((tk, bm), lambda i, j, k: (k, i)), pl.BlockSpec((tk, bn), lambda i, j, k: (k, j))],
        out_specs=out_spec, out_shape=out_shape,
        compiler_params=_params(("parallel", "parallel", "arbitrary")),
    )(a, b)


def _dw_out(out_a, out_b, dx1, tk):
    S = dx1.shape[0]

    def body(a_ref, b_ref, d_ref, o_ref):
        @pl.when(pl.program_id(0) == 0)
        def _():
            o_ref[...] = jnp.zeros_like(o_ref)

        d = d_ref[...].astype(BF16)
        o_ref[:D_HEADS, :] += lax.dot_general(a_ref[...], d, TN, preferred_element_type=F32)
        o_ref[D_HEADS:, :] += lax.dot_general(b_ref[...], d, TN, preferred_element_type=F32)

    row = lambda w: pl.BlockSpec((tk, w), lambda k: (k, 0))
    return pl.pallas_call(
        body, name="dw_out", grid=(S // tk,),
        in_specs=[row(D_HEADS), row(D_HEADS), row(D_MODEL)], out_specs=_full((D_MODEL, D_MODEL)),
        out_shape=jax.ShapeDtypeStruct((D_MODEL, D_MODEL), F32),
        compiler_params=_params(("arbitrary",)),
    )(out_a, out_b, dx1)


def _up_bwd(dact, w_up_q, x1, g2, dx2, tm):
    S = x1.shape[0]
    nq, _, wq = w_up_q.shape

    def body(d_ref, w_ref, x_ref, g_ref, dx2_ref, dx1_ref, dg_ref):
        @pl.when(pl.program_id(0) == 0)
        def _():
            dg_ref[...] = jnp.zeros_like(dg_ref)

        dh = jnp.zeros((tm, D_MODEL), F32)
        for j in range(nq):
            dh = dh + lax.dot_general(d_ref[:, j * wq:(j + 1) * wq], w_ref[j], NT, preferred_element_type=F32)
        dx, dg = _rms_bwd(dh, x_ref[...], g_ref[...])
        dg_ref[...] += dg
        dx1_ref[...] = dx2_ref[...] + dx

    row = lambda w: pl.BlockSpec((tm, w), lambda i: (i, 0))
    return pl.pallas_call(
        body, name="up_bwd", grid=(S // tm,),
        in_specs=[row(nq * wq), pl.BlockSpec((nq, D_MODEL, wq), lambda i: (0, 0, 0), pipeline_mode=pl.Buffered(1)),
                  row(D_MODEL), _full((1, D_MODEL)), row(D_MODEL)],
        out_specs=[row(D_MODEL), _full((1, D_MODEL))],
        out_shape=[jax.ShapeDtypeStruct((S, D_MODEL), F32), jax.ShapeDtypeStruct((1, D_MODEL), F32)],
        compiler_params=_params(("arbitrary",)),
    )(dact, w_up_q, x1, g2, dx2)


def _out_bwd(dx1, w_out, tm):
    S = dx1.shape[0]

    def body(d_ref, w_ref, o_ref):
        o_ref[...] = lax.dot_general(d_ref[...].astype(BF16), w_ref[...], NT, preferred_element_type=F32).astype(BF16)

    return pl.pallas_call(
        body, name="out_bwd", grid=(S // tm,),
        in_specs=[pl.BlockSpec((tm, D_MODEL), lambda i: (i, 0)), _full((D_MODEL, D_MODEL))],
        out_specs=pl.BlockSpec((tm, D_MODEL), lambda i: (i, 0)),
        out_shape=jax.ShapeDtypeStruct((S, D_MODEL), BF16),
        compiler_params=_params(("parallel",)),
    )(dx1, w_out)


def _gate_bwd(z, dcat, w_mask, w_mask_t, ln_row, b_full, seg_avg, head_ind, tm, swap):
    S = z.shape[0]
    nb = tm // SG_BLOCK
    ns = len(swap)

    def body(zu_ref, zv_ref, do_ref, w_ref, wt_ref, ln_ref, b_ref, avg_ref, ind_ref, *rest):
        dzu_ref, dzv_ref, dw_ref, db_ref, dln_ref = rest[ns:ns + 5]
        dvn_s, dbf_s = rest[2 * ns + 5:2 * ns + 7]
        swap_start, swap_finish = _swap_ops(rest[:ns], rest[ns + 5:2 * ns + 5], *rest[2 * ns + 7:])
        i = pl.program_id(0)

        @pl.when(i == 0)
        def _():
            swap_start()
            dw_ref[...] = jnp.zeros_like(dw_ref)
            dln_ref[...] = jnp.zeros_like(dln_ref)
            dbf_s[...] = jnp.zeros_like(dbf_s)

        zu = zu_ref[...].astype(F32)
        zv = zv_ref[...].astype(F32)
        u = _gelu(zu)
        v = _gelu(zv)
        avg = avg_ref[...]
        vhat, rstd = _layer_norm_heads(v, avg)
        ln = ln_ref[...]
        vn = vhat * ln
        for b in range(nb):
            rows = slice(b * SG_BLOCK, (b + 1) * SG_BLOCK)
            vn_b = vn[rows]
            mixed = _gate_mix(vn_b, w_ref, b_ref[...])
            do = do_ref[rows, :].astype(F32)
            dzu_ref[rows, :] = (do * mixed * _gelu_grad(zu[rows])).astype(BF16)
            dmix = do * u[rows]
            dbf_s[...] += dmix
            vn_bf = vn_b.astype(BF16)
            dvn = jnp.zeros((SG_BLOCK, D_HEADS), F32)
            for h in range(N_HEADS):
                dmh = jnp.where(_head_mask(h, SG_BLOCK), dmix, 0.0).astype(BF16)
                dw_ref[h] += lax.dot_general(dmh, vn_bf, NT, preferred_element_type=F32)
                dvn = dvn + jnp.dot(wt_ref[h], dmh, preferred_element_type=F32)
            dvn_s[rows, :] = dvn
        dvn = dvn_s[...]
        dln_ref[...] += jnp.sum(dvn * vhat, axis=0, keepdims=True)
        dvhat = dvn * ln
        dv = rstd * (dvhat - _split_dot(dvhat, avg) - vhat * _split_dot(dvhat * vhat, avg))
        dzv_ref[...] = (dv * _gelu_grad(zv)).astype(BF16)

        @pl.when(i == pl.num_programs(0) - 1)
        def _():
            r = lax.broadcasted_iota(jnp.int32, (SG_BLOCK, SG_BLOCK), 0) // CHUNK
            s = lax.broadcasted_iota(jnp.int32, (SG_BLOCK, SG_BLOCK), 1) // CHUNK
            for h in range(N_HEADS):
                dw_ref[h] = jnp.where(r >= s, dw_ref[h], 0.0)
            db_ref[...] = _split_dot(dbf_s[...], ind_ref[...])
            swap_finish()

    row = lambda col: pl.BlockSpec((tm, D_HEADS), lambda i: (i, col))
    wspec = _full((N_HEADS, SG_BLOCK, SG_BLOCK))
    out = pl.pallas_call(
        body, name="gate_bwd", grid=(S // tm,),
        in_specs=[row(0), row(1), row(0), wspec, wspec, _full((1, D_HEADS)), _full((SG_BLOCK, D_HEADS)),
                  _full((D_HEADS, D_HEADS)), _full((D_HEADS, LANES))] + [_ANY] * ns,
        out_specs=[row(0), row(0), wspec, _full((SG_BLOCK, LANES)), _full((1, D_HEADS))] + [_ANY] * ns,
        out_shape=[jax.ShapeDtypeStruct((S, D_HEADS), BF16), jax.ShapeDtypeStruct((S, D_HEADS), BF16),
                   jax.ShapeDtypeStruct((N_HEADS, SG_BLOCK, SG_BLOCK), F32), jax.ShapeDtypeStruct((SG_BLOCK, LANES), F32),
                   jax.ShapeDtypeStruct((1, D_HEADS), F32)] + _swap_shapes(swap),
        scratch_shapes=[pltpu.VMEM((tm, D_HEADS), F32), pltpu.VMEM((SG_BLOCK, D_HEADS), F32)] + _swap_sems(ns),
        compiler_params=_params(("arbitrary",)),
    )(z, z, dcat, w_mask, w_mask_t, ln_row, b_full, seg_avg, head_ind, *swap)
    return out[:5], out[5:]


def _attn_pack_grad(o, dcat, qa, lse, head_ind, k, tm):
    S = o.shape[0]

    def body(o_ref, do_ref, qa_ref, lse_ref, ind_ref, pl_ref, pt_ref, eye_ref, ds_ref, dst_ref, ls_ref, lst_ref,
             dop_ref, qb_ref, dot_ref, qbt_ref):
        do = do_ref[...]
        delta = _split_dot(o_ref[...].astype(F32) * do.astype(F32), ind_ref[...])
        hi = delta.astype(BF16).astype(F32)
        parts = (hi + pltpu.roll((delta - hi).astype(BF16).astype(F32), N_HEADS, 1)).astype(BF16)
        dop = jnp.dot(do, pl_ref[...], preferred_element_type=F32) - jnp.dot(parts, ds_ref[...], preferred_element_type=F32)
        for g in range(GROUPS):
            dop_ref[g] = dop[:, g * GROUP_PAD:(g + 1) * GROUP_PAD].astype(BF16)
        dot = lax.dot_general(pt_ref[...], do, NT, preferred_element_type=F32)
        dot_ref[...] = (dot - lax.dot_general(dst_ref[...], parts, NT, preferred_element_type=F32)).astype(BF16)
        qa = qa_ref[...]
        stack = jnp.concatenate(_split3(lse_ref[...]), axis=0)
        qb = qa.astype(F32) - lax.dot_general(stack, ls_ref[...], TN, preferred_element_type=F32)
        qbt = lax.dot_general(eye_ref[...], qa, NT, preferred_element_type=F32)
        qbt = qbt - jnp.dot(lst_ref[...], stack, preferred_element_type=F32)
        for g in range(GROUPS):
            qb_ref[g] = qb[:, g * GROUP_PAD:(g + 1) * GROUP_PAD].astype(BF16)
        qbt_ref[...] = qbt.astype(BF16)

    pad = pl.BlockSpec((tm, D_PAD), lambda i: (i, 0))
    padt = pl.BlockSpec((None, D_PAD, tm), lambda i: (i, 0, 0))
    return pl.pallas_call(
        body, name="attn_pack_grad", grid=(S // tm,),
        in_specs=[pl.BlockSpec((tm, D_HEADS), lambda i: (i, 0)), pl.BlockSpec((tm, D_HEADS), lambda i: (i, 1)), pad,
                  pl.BlockSpec((STAT_ROWS, tm), lambda i: (0, i)), _full((D_HEADS, LANES)), _full((D_HEADS, D_PAD)),
                  _full((D_PAD, D_HEADS)), _full((D_PAD, D_PAD)), _full((LANES, D_PAD)), _full((D_PAD, LANES)),
                  _full((3 * STAT_ROWS, D_PAD)), _full((D_PAD, 3 * STAT_ROWS))],
        out_specs=[pl.BlockSpec((GROUPS, tm, GROUP_PAD), lambda i: (0, i, 0))] * 2 + [padt, padt],
        out_shape=[jax.ShapeDtypeStruct((GROUPS, S, GROUP_PAD), BF16)] * 2 + [jax.ShapeDtypeStruct((S // tm, D_PAD, tm), BF16)] * 2,
        compiler_params=_params(("parallel",)),
    )(o, dcat, qa, lse, head_ind, k["place"], k["place_t"], jnp.eye(D_PAD, dtype=BF16), k["d_stat"], k["d_stat"].T,
      k["l_stat"], k["l_stat"].T)


def _attn_bwd(qb, qbt, ka, va, dop, dopt, k, tq, sums16):
    S = ka.shape[0]
    n = S // tq
    ns = len(sums16)

    pairs = [(kb, q) for kb in range(n) for q in range(kb, n)]
    k_of = jnp.asarray([kb for kb, _ in pairs], jnp.int32)
    q_of = jnp.asarray([q for _, q in pairs], jnp.int32)

    def body(k_of_ref, q_of_ref, q_ref, qt_ref, k_ref, v_ref, do_ref, dot_ref, pt_ref, *rest):
        dq_hbm, dcr_hbm, dk_ref, dv_ref, dcc_ref = rest[ns:ns + 5]
        dq_s, dcr_s, dk_s, dv_s, dcc_s = rest[2 * ns + 5:2 * ns + 10]
        s_s, d_s = rest[2 * ns + 10:2 * ns + 12], rest[2 * ns + 12:2 * ns + 14]
        sems = rest[2 * ns + 14]
        scatter_start, scatter_finish = _scatter_ops(rest[:ns], rest[ns + 5:2 * ns + 5], *rest[2 * ns + 15:])
        g = pl.program_id(0)
        ki, qi = k_of_ref[pl.program_id(1)], q_of_ref[pl.program_id(1)]

        @pl.when((g == 0) & (ki == 0) & (qi == 0))
        def _():
            scatter_start()

        @pl.when((ki == 0) & (qi == 0))
        def _():
            dq_s[...] = jnp.zeros_like(dq_s)
            dcr_s[...] = jnp.zeros_like(dcr_s)

        @pl.when(qi == ki)
        def _():
            dk_s[...] = jnp.zeros_like(dk_s)
            dv_s[...] = jnp.zeros_like(dv_s)
            dcc_s[...] = jnp.zeros_like(dcc_s)

        def step(diagonal):
            chunks = [slice(c * KEY_CHUNK, (c + 1) * KEY_CHUNK) for c in range(tq // KEY_CHUNK)]

            def scores(hh, rows, slot):
                sl = slice(hh * HEAD_PAD, (hh + 1) * HEAD_PAD)
                s_s[slot][rows, :] = lax.dot_general(q_ref[rows, sl], k_ref[:, sl], NT, preferred_element_type=F32)
                d_s[slot][rows, :] = lax.dot_general(do_ref[rows, sl], v_ref[:, sl], NT, preferred_element_type=F32)

            for rows in chunks:
                scores(0, rows, 0)
            for hh in range(GROUP_HEADS):
                sl = slice(hh * HEAD_PAD, (hh + 1) * HEAD_PAD)
                slot = hh % 2
                dv, dk = dv_s[sl, :], dk_s[sl, :]
                for rows in chunks:
                    if hh + 1 < GROUP_HEADS:
                        scores(hh + 1, rows, 1 - slot)
                    p = jnp.exp2(s_s[slot][rows, :])
                    if diagonal:
                        row = rows.start + lax.broadcasted_iota(jnp.int32, (KEY_CHUNK, tq), 0)
                        col = lax.broadcasted_iota(jnp.int32, (KEY_CHUNK, tq), 1)
                        p = jnp.where(row >= col, p, 0.0)
                    ds = p * d_s[slot][rows, :]
                    qrows = pl.ds(pl.multiple_of(qi * tq + rows.start, KEY_CHUNK), KEY_CHUNK)
                    dcc_s[hh:hh + 1, :] += jnp.sum(ds, axis=0, keepdims=True)
                    dcr_s[qrows, hh:hh + 1] += jnp.sum(ds, axis=1, keepdims=True)
                    ds = ds.astype(BF16)
                    dv = dv + jnp.dot(dot_ref[sl, rows], p.astype(BF16), preferred_element_type=F32)
                    dk = dk + jnp.dot(qt_ref[sl, rows], ds, preferred_element_type=F32)
                    dq_s[qrows, sl] += jnp.dot(ds, k_ref[:, sl], preferred_element_type=F32)
                dv_s[sl, :] = dv
                dk_s[sl, :] = dk

        @pl.when(qi > ki)
        def _():
            step(False)

        @pl.when(qi == ki)
        def _():
            step(True)

        @pl.when(qi == n - 1)
        def _():
            dk = dk_s[...]
            pt = pt_ref[...]
            dk_ref[...] = lax.dot_general((dk * (1.0 / LOG2E)).astype(BF16), pt, TN, preferred_element_type=F32).astype(BF16)
            dv_ref[...] = lax.dot_general(dv_s[...].astype(BF16), pt, TN, preferred_element_type=F32).astype(BF16)
            dcc_ref[...] = dcc_s[...]

        @pl.when((ki == n - 1) & (qi == n - 1))
        def _():
            copies = [pltpu.make_async_copy(dq_s, dq_hbm.at[g], sems.at[0]), pltpu.make_async_copy(dcr_s, dcr_hbm.at[g], sems.at[1])]
            for cp in copies:
                cp.start()
            for cp in copies:
                cp.wait()

        @pl.when((g == GROUPS - 1) & (ki == n - 1) & (qi == n - 1))
        def _():
            scatter_finish()

    gw = GROUP_HEADS * HEAD_DIM
    qspec = pl.BlockSpec((None, tq, GROUP_PAD), lambda g, i, ks, qs: (g, qs[i], 0))
    qtspec = pl.BlockSpec((None, GROUP_PAD, tq), lambda g, i, ks, qs: (qs[i], g, 0))
    kspec = pl.BlockSpec((tq, GROUP_PAD), lambda g, i, ks, qs: (ks[i], g))
    kout = pl.BlockSpec((tq, gw), lambda g, i, ks, qs: (ks[i], g))
    out = pl.pallas_call(
        body, name="attn_bwd",
        grid_spec=pltpu.PrefetchScalarGridSpec(
            num_scalar_prefetch=2, grid=(GROUPS, len(pairs)),
            in_specs=[qspec, qtspec, kspec, kspec, qspec, qtspec, pl.BlockSpec((GROUP_PAD, gw), lambda g, i, ks, qs: (0, 0))]
            + [_ANY] * ns,
            out_specs=[_ANY, _ANY, kout, kout, pl.BlockSpec((None, SUBLANES, tq), lambda g, i, ks, qs: (g, 0, ks[i]))] + [_ANY] * ns,
            scratch_shapes=[pltpu.VMEM((S, GROUP_PAD), F32), pltpu.VMEM((S, LANES), F32), pltpu.VMEM((GROUP_PAD, tq), F32),
                            pltpu.VMEM((GROUP_PAD, tq), F32), pltpu.VMEM((SUBLANES, tq), F32),
                            pltpu.VMEM((tq, tq), F32), pltpu.VMEM((tq, tq), F32), pltpu.VMEM((tq, tq), F32),
                            pltpu.VMEM((tq, tq), F32), pltpu.SemaphoreType.DMA((2,))]
            + _scatter_sems(ns)),
        out_shape=[jax.ShapeDtypeStruct((GROUPS, S, GROUP_PAD), F32), jax.ShapeDtypeStruct((GROUPS, S, LANES), F32),
                   jax.ShapeDtypeStruct((S, D_HEADS), BF16), jax.ShapeDtypeStruct((S, D_HEADS), BF16),
                   jax.ShapeDtypeStruct((GROUPS, SUBLANES, S), F32)]
        + _scatter_shapes(sums16),
        compiler_params=_params(("arbitrary", "arbitrary")),
    )(k_of, q_of, qb, qbt, ka, va, dop, dopt, k["place_t_group"], *sums16)
    return out[0], out[1], out[2], out[3], out[4], out[5:]


def _attn_unpack(dqp, k, tm):
    S = dqp.shape[1]
    gw = GROUP_HEADS * HEAD_DIM

    def body(dqp_ref, pt_ref, dq_ref):
        for g in range(GROUPS):
            dq_ref[:, g * gw:(g + 1) * gw] = jnp.dot((dqp_ref[g] * SCALE).astype(BF16), pt_ref[...],
                                                     preferred_element_type=F32).astype(BF16)

    return pl.pallas_call(
        body, name="attn_unpack", grid=(S // tm,),
        in_specs=[pl.BlockSpec((GROUPS, tm, GROUP_PAD), lambda i: (0, i, 0)), _full((GROUP_PAD, gw))],
        out_specs=pl.BlockSpec((tm, D_HEADS), lambda i: (i, 0)),
        out_shape=jax.ShapeDtypeStruct((S, D_HEADS), BF16),
        compiler_params=_params(("parallel",)),
    )(dqp, k["place_t_group"])


def _fox_bwd(dc, f, bias_row, tb):
    S = f.shape[0]
    nb = S // tb

    def body(dc_ref, f_ref, b_ref, df_ref, dbias_ref, carry):
        @pl.when(pl.program_id(0) == 0)
        def _():
            carry[...] = jnp.zeros_like(carry)
            dbias_ref[...] = jnp.zeros_like(dbias_ref)

        r = lax.broadcasted_iota(jnp.int32, (tb, tb), 0)
        s = lax.broadcasted_iota(jnp.int32, (tb, tb), 1)
        tri = (s >= r).astype(F32)
        rc = jnp.dot(tri, dc_ref[...], precision=lax.Precision.HIGHEST, preferred_element_type=F32) + carry[0:1, :]
        carry[...] = jnp.broadcast_to(rc[0:1, :], carry.shape)
        lane = lax.broadcasted_iota(jnp.int32, (tb, LANES), 1)
        df = jnp.where(lane < N_HEADS, rc * jax.nn.sigmoid(-(f_ref[...] + b_ref[...])), 0.0)
        df_ref[...] = df.astype(BF16)
        dbias_ref[...] += jnp.sum(df, axis=0, keepdims=True)

    rev = pl.BlockSpec((tb, LANES), lambda i: (nb - 1 - i, 0))
    return pl.pallas_call(
        body, name="fox_bwd", grid=(nb,),
        in_specs=[rev, rev, _full((1, LANES))],
        out_specs=[rev, _full((1, LANES))],
        out_shape=[jax.ShapeDtypeStruct((S, LANES), BF16), jax.ShapeDtypeStruct((1, LANES), F32)],
        scratch_shapes=[pltpu.VMEM((SUBLANES, LANES), F32)],
        compiler_params=_params(("arbitrary",)),
    )(dc, f, bias_row)


_DZ_WIDTHS = (D_HEADS,) * 5 + (LANES,)


def _in_bwd(pieces, w_in, x, g1, dx1, tm, sums16):
    S = x.shape[0]
    ns = len(sums16)

    def body(*refs):
        p_refs, (w_ref, x_ref, g_ref, dx1_ref) = refs[:6], refs[6:10]
        dx_ref, dg_ref = refs[10 + ns:12 + ns]
        scatter_start, scatter_finish = _scatter_ops(refs[10:10 + ns], refs[12 + ns:12 + 2 * ns], *refs[12 + 2 * ns:])

        @pl.when(pl.program_id(0) == 0)
        def _():
            dg_ref[...] = jnp.zeros_like(dg_ref)
            scatter_start()

        dh = jnp.zeros((tm, D_MODEL), F32)
        off = 0
        for p_ref, w in zip(p_refs, _DZ_WIDTHS):
            dh = dh + lax.dot_general(p_ref[...].astype(BF16), w_ref[:, off:off + w], NT, preferred_element_type=F32)
            off += w
        dx, dg = _rms_bwd(dh, x_ref[...], g_ref[...])
        dg_ref[...] += dg
        dx_ref[...] = dx1_ref[...] + dx

        @pl.when(pl.program_id(0) == pl.num_programs(0) - 1)
        def _():
            scatter_finish()

    row = lambda w: pl.BlockSpec((tm, w), lambda i: (i, 0))
    out = pl.pallas_call(
        body, name="in_bwd", grid=(S // tm,),
        in_specs=[row(w) for w in _DZ_WIDTHS] + [_full((D_MODEL, D_IN_PAD)), row(D_MODEL), _full((1, D_MODEL)), row(D_MODEL)]
        + [_ANY] * ns,
        out_specs=[row(D_MODEL), _full((1, D_MODEL))] + [_ANY] * ns,
        out_shape=[jax.ShapeDtypeStruct((S, D_MODEL), F32), jax.ShapeDtypeStruct((1, D_MODEL), F32)] + _scatter_shapes(sums16),
        scratch_shapes=_scatter_sems(ns),
        compiler_params=_params(("arbitrary",)),
    )(*pieces, w_in, x, g1, dx1, *sums16)
    return out[0], out[1], out[2:]


def _dw_in(h1, pieces, tk):
    S = h1.shape[0]

    def body(*refs):
        h_ref, p_refs, o_ref = refs[0], refs[1:7], refs[7]

        @pl.when(pl.program_id(0) == 0)
        def _():
            o_ref[...] = jnp.zeros_like(o_ref)

        off = 0
        for p_ref, w in zip(p_refs, _DZ_WIDTHS):
            o_ref[:, off:off + w] += lax.dot_general(h_ref[...], p_ref[...].astype(BF16), TN, preferred_element_type=F32)
            off += w

    row = lambda w: pl.BlockSpec((tk, w), lambda k: (k, 0))
    return pl.pallas_call(
        body, name="dw_in", grid=(S // tk,),
        in_specs=[row(D_MODEL)] + [row(w) for w in _DZ_WIDTHS],
        out_specs=_full((D_MODEL, D_IN_PAD)),
        out_shape=jax.ShapeDtypeStruct((D_MODEL, D_IN_PAD), F32),
        compiler_params=_params(("arbitrary",)),
    )(h1, *pieces)


def _adamw_math(w, g, m, v):
    m = ADAM_B1 * m + (1.0 - ADAM_B1) * g
    v = ADAM_B2 * v + (1.0 - ADAM_B2) * (g * g)
    m_hat = m / (1.0 - ADAM_B1 ** ADAM_STEP)
    v_hat = v / (1.0 - ADAM_B2 ** ADAM_STEP)
    delta = -ADAM_LR * (m_hat / (jnp.sqrt(v_hat) + ADAM_EPS) + ADAM_WD * w)
    return delta, m, v


def _adamw(name, w, g, m, v):
    R, C = w.shape
    tr = _row_tile(R, 256)

    def body(w_ref, g_ref, m_ref, v_ref, go_ref, d_ref, nm_ref, nv_ref):
        g = g_ref[...]
        d, nm, nv = _adamw_math(w_ref[...], g, m_ref[...], v_ref[...])
        go_ref[...] = g
        d_ref[...] = d
        nm_ref[...] = nm
        nv_ref[...] = nv

    spec = pl.BlockSpec((tr, C), lambda i: (i, 0))
    return pl.pallas_call(
        body, name=name, grid=(R // tr,), in_specs=[spec] * 4, out_specs=[spec] * 4,
        out_shape=[jax.ShapeDtypeStruct((R, C), F32)] * 4,
        compiler_params=_params(("parallel",)),
    )(w, g, m, v)


def _pair_sum(name, grad, theirs, ids):
    q, half, C = theirs.shape
    tr = _row_tile(half, 256)
    nb = half // tr

    def body(ids_ref, a_ref, b_ref, sb_ref):
        sb_ref[...] = (a_ref[...] + b_ref[...]).astype(BF16)

    here = pl.BlockSpec((None, tr, C), lambda j, i, ids: (j, i, 0))
    return pl.pallas_call(
        body, name=name,
        grid_spec=pltpu.PrefetchScalarGridSpec(
            num_scalar_prefetch=1, grid=(q, nb),
            in_specs=[pl.BlockSpec((None, tr, C), lambda j, i, ids: (j, ids[1] * nb + i, 0)), here],
            out_specs=here),
        out_shape=jax.ShapeDtypeStruct((q, half, C), BF16),
        compiler_params=_params(("parallel", "parallel")),
    )(ids, grad, theirs)


def _chip_sum(name, grad, theirs, others, ids):
    _, half, C = theirs.shape
    tr = _row_tile(half, 256)
    nb = half // tr

    def body(ids_ref, a_ref, b_ref, o_ref, s_ref):
        s = a_ref[...] + b_ref[...]
        for j in range(3):
            s = s + o_ref[j].astype(F32)
        s_ref[...] = s

    return pl.pallas_call(
        body, name=name,
        grid_spec=pltpu.PrefetchScalarGridSpec(
            num_scalar_prefetch=1, grid=(nb,),
            in_specs=[pl.BlockSpec((None, tr, C), lambda i, ids: (ids[0], ids[1] * nb + i, 0)),
                      pl.BlockSpec((None, tr, C), lambda i, ids: (ids[0], i, 0)),
                      pl.BlockSpec((3, tr, C), lambda i, ids: (0, i, 0))],
            out_specs=pl.BlockSpec((tr, C), lambda i, ids: (ids[1] * nb + i, 0))),
        out_shape=jax.ShapeDtypeStruct((2 * half, C), F32),
        compiler_params=_params(("parallel",)),
    )(ids, grad, theirs, others)


def _place():
    return lax.axis_index("x"), lax.axis_index("y"), lax.axis_index("c")


def _other_chips(x, y):
    return [(1 - x, y), (x, 1 - y), (1 - x, 1 - y)]


_ANY = pl.BlockSpec(memory_space=pl.ANY)


def _gather_quarters(shards):
    n = len(shards)

    def body(*refs):
        start, hand_on, finish = _gather_ops(refs[:n], refs[n:2 * n], *refs[2 * n:])
        start()
        hand_on()
        finish()

    return pl.pallas_call(
        body, name="gather_weights",
        in_specs=[_ANY] * n, out_specs=[_ANY] * n,
        out_shape=_gather_shapes(shards), scratch_shapes=_gather_sems(n),
    )(*shards)


def _gather_shapes(shards):
    return [jax.ShapeDtypeStruct((4,) + s.shape, s.dtype) for s in shards]


def _gather_sems(n):
    return [pltpu.SemaphoreType.DMA((n, 3))] * 4 + [pltpu.SemaphoreType.DMA((n,))]


def _gather_ops(ins, outs, send_sems, recv_sems, pass_send_sems, pass_recv_sems, own_sems):
    n = len(ins)
    halved = [r.shape[0] % 32 == 0 for r in ins]

    def part(a, quarter, core):
        if not halved[a]:
            return outs[a].at[quarter]
        half = ins[a].shape[0] // 2
        return outs[a].at[quarter, pl.ds(core * half, half), :]

    def ici(a, j, quarter):
        x, y, c = _place()
        px, py = _other_chips(x, y)[j]
        src = ins[a]
        if halved[a]:
            half = src.shape[0] // 2
            src = src.at[pl.ds(c * half, half), :]
        return pltpu.make_async_remote_copy(src_ref=src, dst_ref=part(a, quarter, c), send_sem=send_sems.at[a, j],
                                            recv_sem=recv_sems.at[a, j], device_id=(px, py, c), device_id_type=MESH)

    def passed(a, j, core):
        x, y, c = _place()
        px, py = _other_chips(x, y)[j]
        half = part(a, 2 * px + py, core)
        return pltpu.make_async_remote_copy(src_ref=half, dst_ref=half, send_sem=pass_send_sems.at[a, j],
                                            recv_sem=pass_recv_sems.at[a, j], device_id=(x, y, 1 - c), device_id_type=MESH)

    def own(a):
        x, y, _ = _place()
        return pltpu.make_async_copy(ins[a], outs[a].at[2 * x + y], own_sems.at[a])

    def start():
        x, y, _ = _place()
        for a in range(n):
            for j in range(3):
                ici(a, j, 2 * x + y).start()
            own(a).start()

    def hand_on():
        x, y, c = _place()
        for a in range(n):
            for j, (px, py) in enumerate(_other_chips(x, y)):
                ici(a, j, 2 * px + py).wait_recv()
                if halved[a]:
                    passed(a, j, c).start()

    def finish():
        x, y, c = _place()
        for a in range(n):
            for j in range(3):
                if halved[a]:
                    passed(a, j, 1 - c).wait_recv()
                    passed(a, j, c).wait_send()
                ici(a, j, 2 * x + y).wait_send()
            own(a).wait()

    return start, hand_on, finish


def _swap_halves(grads, name):
    n = len(grads)

    def body(*refs):
        start, finish = _swap_ops(refs[:n], refs[n:2 * n], *refs[2 * n:])
        start()
        finish()

    return pl.pallas_call(
        body, name=name,
        in_specs=[_ANY] * n, out_specs=[_ANY] * n, out_shape=_swap_shapes(grads), scratch_shapes=_swap_sems(n),
    )(*grads)


def _swap_shapes(grads):
    return [jax.ShapeDtypeStruct((4, g.shape[1] // 2, g.shape[2]), F32) for g in grads]


def _swap_sems(n):
    return [pltpu.SemaphoreType.DMA((n,))] * 2


def _swap_ops(ins, outs, send_sems, recv_sems):
    def copy(a):
        x, y, c = _place()
        half = ins[a].shape[1] // 2
        return pltpu.make_async_remote_copy(src_ref=ins[a].at[:, pl.ds((1 - c) * half, half), :], dst_ref=outs[a],
                                            send_sem=send_sems.at[a], recv_sem=recv_sems.at[a],
                                            device_id=(x, y, 1 - c), device_id_type=MESH)

    def start():
        for a in range(len(ins)):
            copy(a).start()

    def finish():
        for a in range(len(ins)):
            copy(a).wait()

    return start, finish


def _scatter_shapes(sums16):
    return [jax.ShapeDtypeStruct((3,) + s.shape[1:], BF16) for s in sums16]


def _scatter_sems(n):
    return [pltpu.SemaphoreType.DMA((n, 3))] * 2


def _scatter_ops(ins, outs, send_sems, recv_sems):
    n = len(ins)

    def copy(a, j):
        x, y, c = _place()
        px, py = _other_chips(x, y)[j]
        return pltpu.make_async_remote_copy(src_ref=ins[a].at[2 * px + py], dst_ref=outs[a].at[j], send_sem=send_sems.at[a, j],
                                            recv_sem=recv_sems.at[a, j], device_id=(px, py, c), device_id_type=MESH)

    def start():
        for a in range(n):
            for j in range(3):
                copy(a, j).start()

    def finish():
        for a in range(n):
            for j in range(3):
                copy(a, j).wait()

    return start, finish


def _join_halves(fulls):
    n = len(fulls)

    def body(*refs):
        ins, outs = refs[:n], refs[n:2 * n]
        send_sems, recv_sems = refs[2 * n:]
        x, y, c = _place()
        started = []
        for a in range(n):
            half = ins[a].shape[0] // 2
            rows = pl.ds(c * half, half)
            cp = pltpu.make_async_remote_copy(src_ref=ins[a].at[rows, :], dst_ref=outs[a].at[rows, :], send_sem=send_sems.at[a],
                                              recv_sem=recv_sems.at[a], device_id=(x, y, 1 - c), device_id_type=MESH)
            cp.start()
            started.append(cp)
        for cp in started:
            cp.wait()

    return pl.pallas_call(
        body, name="join_halves",
        in_specs=[_ANY] * n, out_specs=[_ANY] * n,
        out_shape=[jax.ShapeDtypeStruct(f.shape, F32) for f in fulls],
        input_output_aliases={a: a for a in range(n)},
        scratch_shapes=[pltpu.SemaphoreType.DMA((n,)), pltpu.SemaphoreType.DMA((n,))],
    )(*fulls)


def _small_allreduce(g):
    R = g.shape[0]
    half = R // 2

    def body(g_ref, out_ref, other_s, chip_s, parts_s, send_sems, recv_sems):
        x, y, c = _place()
        mine = 2 * x + y
        rows = pl.ds(pl.multiple_of(c * half, SUBLANES), half)

        def to_other_core(src, dst, k):
            return pltpu.make_async_remote_copy(src_ref=src, dst_ref=dst, send_sem=send_sems.at[k], recv_sem=recv_sems.at[k],
                                                device_id=(x, y, 1 - c), device_id_type=MESH)

        swap = to_other_core(g_ref, other_s, 0)
        swap.start()
        swap.wait()
        chip_s[...] = g_ref[...] + other_s[...]
        parts_s[mine] = chip_s[rows, :]
        sends = []
        for j, (px, py) in enumerate(_other_chips(x, y)):
            cp = pltpu.make_async_remote_copy(src_ref=chip_s.at[rows, :], dst_ref=parts_s.at[mine], send_sem=send_sems.at[1 + j],
                                              recv_sem=recv_sems.at[1 + j], device_id=(px, py, c), device_id_type=MESH)
            cp.start()
            sends.append(cp)
        for cp in sends:
            cp.wait()
        out_ref[rows, :] = (parts_s[0] + parts_s[1]) + (parts_s[2] + parts_s[3])
        join = to_other_core(out_ref.at[rows, :], out_ref.at[rows, :], 4)
        join.start()
        join.wait()

    vm = pl.BlockSpec(memory_space=pltpu.VMEM)
    return pl.pallas_call(
        body, name="small_allreduce",
        in_specs=[vm], out_specs=vm, out_shape=jax.ShapeDtypeStruct((R, LANES), F32),
        scratch_shapes=[pltpu.VMEM((R, LANES), F32), pltpu.VMEM((R, LANES), F32), pltpu.VMEM((4, half, LANES), F32),
                        pltpu.SemaphoreType.DMA((5,)), pltpu.SemaphoreType.DMA((5,))],
        compiler_params=pltpu.CompilerParams(vmem_limit_bytes=VMEM_LIMIT),
    )(g)


def _adamw_small(ws, gs, ms, vs):
    n = len(ws)

    def body(*refs):
        for k in range(n):
            w_ref, g_ref, m_ref, v_ref = (refs[j * n + k] for j in range(4))
            d, nm, nv = _adamw_math(w_ref[...], g_ref[...], m_ref[...], v_ref[...])
            refs[4 * n + k][...] = d
            refs[5 * n + k][...] = nm
            refs[6 * n + k][...] = nv

    vm = pl.BlockSpec(memory_space=pltpu.VMEM)
    out = pl.pallas_call(
        body, name="adamw_small",
        in_specs=[vm] * (4 * n), out_specs=[vm] * (3 * n),
        out_shape=[jax.ShapeDtypeStruct(w.shape, F32) for w in ws] * 3,
        compiler_params=pltpu.CompilerParams(vmem_limit_bytes=VMEM_LIMIT),
    )(*ws, *gs, *ms, *vs)
    return out[:n], out[n:2 * n], out[2 * n:]


_SMALL = (("norm_mix_g", D_MODEL), ("f_bias", N_HEADS), ("sg_ln_g", D_HEADS), ("sg_w", N_HEADS * SG_BLOCK * SG_BLOCK),
          ("sg_b", N_HEADS * SG_BLOCK), ("norm_ffn_g", D_MODEL), ("w_conv", 3 * 2 * D_FF), ("b_conv", 2 * D_FF),
          ("norm_final_g", D_MODEL))
_PACKED = _SMALL + (("sq_err", D_MODEL),)


def _pack_small(parts):
    rows = []
    for name, size in _PACKED:
        flat = parts[name].reshape(-1).astype(F32)
        pad = (-size) % (SUBLANES * LANES)
        rows.append(jnp.pad(flat, (0, pad)).reshape(-1, LANES))
    packed = jnp.concatenate(rows, axis=0)
    return jnp.pad(packed, ((0, (-packed.shape[0]) % (2 * SUBLANES)), (0, 0)))


def _unpack_small(packed, shapes):
    out, r = {}, 0
    for name, size in _PACKED:
        nrows = (size + SUBLANES * LANES - 1) // (SUBLANES * LANES) * SUBLANES
        out[name] = packed[r:r + nrows].reshape(-1)[:size].reshape(shapes[name])
        r += nrows
    return out


def _local_step(x, target, g1, w_in, f_bias, sg_ln_g, sg_w, sg_b, g2, b_conv, g3, late_shards, ids):
    S = x.shape[0]
    tm = _row_tile(S, 512)
    tms = _row_tile(S, 256)
    tq = tm

    lane = jnp.arange(D_HEADS)
    seg_avg = jnp.where(lane[:, None] // HEAD_DIM == lane[None, :] // HEAD_DIM, 1.0 / HEAD_DIM, 0.0).astype(BF16)
    head_ind = (lane[:, None] // HEAD_DIM == jnp.arange(LANES)[None, :]).astype(BF16)
    pos_chunk = jnp.arange(SG_BLOCK) // CHUNK
    w_mask32 = jnp.where(pos_chunk[:, None] >= pos_chunk[None, :], sg_w, 0.0)
    w_mask = w_mask32.astype(BF16)
    w_mask_t = jnp.swapaxes(w_mask32, 1, 2).astype(BF16)
    ln_row = sg_ln_g.reshape(1, D_HEADS)
    b_full = jnp.repeat(sg_b.T, HEAD_DIM, axis=1)
    bias_row = jnp.pad(f_bias.reshape(1, N_HEADS), ((0, 0), (0, LANES - N_HEADS)))
    b_conv_row = b_conv.reshape(1, 2 * D_FF)

    z, f, h1 = _in_proj(x, g1, w_in, tm)
    c = _fox_prep(f, bias_row, tm)
    consts = _attn_consts()
    qa, ka, va, vat = _attn_pack(z, c, consts, tm)
    out_b, lse, gathered = _attn_fwd(qa, ka, vat, consts["place_t"], tq, late_shards)
    g_out, w_up_q, g_down, g_conv = gathered
    w_out = g_out.reshape(D_MODEL, D_MODEL)
    w_down = g_down.reshape(D_FF, D_MODEL)
    w_conv = jnp.concatenate([g_conv[q] for q in range(4)], axis=1)
    out_a = _gate_fwd(z, w_mask, ln_row, b_full, seg_avg, tm)
    x1, h2 = _mix_out(x, out_a, out_b, w_out, g2, tm)
    a = _up_proj(h2, w_up_q, tm)
    dx2, sq_err, dg3 = _ffn_fwd_loss(a, w_conv, b_conv_row, w_down, x1, g3, target, tm)

    dconv, y, dw_conv8, db_conv = _ffn_bwd_gate(dx2, a, w_conv, b_conv_row, w_down, tms)
    dact = _conv_bwd(dconv, w_conv, tms, D_FF)
    dw_down = _matmul_tn(y, dx2, "dw_down", D_FF // 2, D_MODEL, tm, quarters=(2, 1))
    dx1, dg2 = _up_bwd(dact, w_up_q, x1, g2, dx2, tm)
    dw_up_q = _matmul_tn(h2, dact, "dw_up", D_MODEL, 2 * D_FF // 4, tm, quarters=(1, 4))
    dcat = _out_bwd(dx1, w_out, tm)
    dw_out = _dw_out(out_a, out_b, dx1, tm)
    early = {"w_down": dw_down.reshape(4, D_FF // 4, D_MODEL), "w_up": dw_up_q,
             "w_out": dw_out.reshape(4, D_MODEL // 4, D_MODEL)}
    (dzu, dzv, dsg_w, dsg_b_t, dln), theirs = _gate_bwd(z, dcat, w_mask, w_mask_t, ln_row, b_full, seg_avg, head_ind, tm,
                                                        list(early.values()))
    early_sums = _chip_sums(early, theirs, ids)
    dop, qb, dopt, qbt = _attn_pack_grad(out_b, dcat, qa, lse, head_ind, consts, tm)
    dqp, dc_rows, dk, dv, dc_cols, landed = _attn_bwd(qb, qbt, ka, va, dop, dopt, consts, tq,
                                                      [s16 for _, s16 in early_sums.values()])
    early_parts = {k: (s32, got) for (k, (s32, _)), got in zip(early_sums.items(), landed)}
    dq = _attn_unpack(dqp, consts, tm)
    dc_rows = jnp.concatenate([dc_rows[g][:, :GROUP_HEADS] for g in range(GROUPS)], axis=1)
    dc_cols = jnp.concatenate([dc_cols[g][:GROUP_HEADS] for g in range(GROUPS)], axis=0).T
    dc = jnp.pad(dc_rows - dc_cols, ((0, 0), (0, LANES - N_HEADS)))
    df, dbias = _fox_bwd(dc, f, bias_row, tm)
    pieces = (dzu, dzv, dq, dk, dv, df)
    dw_in = _dw_in(h1, pieces, tm)[:, :D_IN].reshape(D_MODEL, 4, D_IN // 4).transpose(1, 0, 2)
    (w_in_sum, w_in_sum16), = _chip_sums({"w_in": dw_in}, _swap_halves([dw_in], "swap_halves"), ids).values()
    dx, dg1, (w_in_landed,) = _in_bwd(pieces, w_in, x, g1, dx1, tm, [w_in_sum16])

    grads = {
        "norm_mix_g": dg1, "f_bias": dbias[:, :N_HEADS], "sg_ln_g": dln, "sg_w": dsg_w, "sg_b": dsg_b_t[:, :N_HEADS].T,
        "norm_ffn_g": dg2, "w_conv": dw_conv8[:3], "b_conv": db_conv, "norm_final_g": dg3,
    }
    return sq_err, dx, grads, {**early_parts, "w_in": (w_in_sum, w_in_landed)}


def _chip_sums(grads_q, theirs, ids):
    return {k: ((g, t), _pair_sum("pair_sum_" + k, g, t, ids)) for (k, g), t in zip(grads_q.items(), theirs)}


def _finish_reduction(parts, ids):
    names = list(parts)
    fulls = [_chip_sum("chip_sum_" + k, g, t, got, ids) for k, ((g, t), got) in parts.items()]
    return dict(zip(names, _join_halves(fulls)))


def kernel(x, norm_mix_g, w_in, f_bias, sg_ln_g, sg_w, sg_b, w_out, norm_ffn_g, w_up, w_conv, b_conv, w_down, norm_final_g, loss_target, m_norm_mix_g, m_w_in, m_f_bias, m_sg_ln_g, m_sg_w, m_sg_b, m_w_out, m_norm_ffn_g, m_w_up, m_w_conv, m_b_conv, m_w_down, m_norm_final_g, v_norm_mix_g, v_w_in, v_f_bias, v_sg_ln_g, v_sg_w, v_sg_b, v_w_out, v_norm_ffn_g, v_w_up, v_w_conv, v_b_conv, v_w_down, v_norm_final_g):
    args = dict(locals())
    quarter = 2 * lax.axis_index("x") + lax.axis_index("y")
    ids = jnp.stack([quarter, lax.axis_index("c")]).astype(jnp.int32)
    wq_conv = w_conv.shape[-1]

    g_in = _gather_quarters([w_in[0].astype(BF16)])[0]
    w_in_full = jnp.pad(jnp.concatenate([g_in[q] for q in range(4)], axis=1), ((0, 0), (0, D_IN_PAD - D_IN)))
    late_shards = [w_out[0].astype(BF16), w_up[0].astype(BF16), w_down[0].astype(BF16), w_conv[0]]

    sq_err, dx, grads, parts = _local_step(
        x[0], loss_target[0], norm_mix_g, w_in_full, f_bias[0], sg_ln_g[0], sg_w[0], sg_b[0], norm_ffn_g, b_conv[0],
        norm_final_g.reshape(1, D_MODEL), late_shards, ids)
    big = _finish_reduction(parts, ids)

    out = {"grad_x": dx[None]}
    for k in ("w_in", "w_out", "w_up", "w_down"):
        g, d, nm, nv = _adamw("adamw_" + k, args[k][0], big[k], args["m_" + k][0], args["v_" + k][0])
        out["grad_" + k], out["delta_" + k], out["new_m_" + k], out["new_v_" + k] = g[None], d[None], nm[None], nv[None]

    small_names = [n for n, _ in _SMALL]
    shapes = {n: (3, 4 * wq_conv) if n == "w_conv" else args[n].shape for n in small_names}
    shapes["sq_err"] = sq_err.shape
    g_small = _unpack_small(_small_allreduce(_pack_small({**{n: grads[n] for n in small_names}, "sq_err": sq_err})), shapes)
    out["loss"] = 0.5 * jnp.sum(g_small.pop("sq_err")) / D_MODEL
    g_small["w_conv"] = lax.dynamic_slice(g_small["w_conv"], (0, quarter * wq_conv), (3, wq_conv))[None]
    flat2d = lambda t: t.reshape(-1, t.shape[-1])
    updated = _adamw_small(*[[flat2d(src[p + n]) for n in small_names] for src, p in
                             ((args, ""), (g_small, ""), (args, "m_"), (args, "v_"))])
    for n, g in g_small.items():
        out["grad_" + n] = g
    for prefix, arrs in zip(("delta_", "new_m_", "new_v_"), updated):
        for n, t in zip(small_names, arrs):
            out[prefix + n] = t.reshape(args[n].shape)

    weights = ["norm_mix_g", "w_in", "f_bias", "sg_ln_g", "sg_w", "sg_b", "w_out", "norm_ffn_g", "w_up", "w_conv", "b_conv",
               "w_down", "norm_final_g"]
    return (out["loss"], out["grad_x"], *[out[p + n] for p in ("grad_", "delta_", "new_m_", "new_v_") for n in weights])
```

```python
import functools
import math

import jax
import jax.numpy as jnp
from jax import lax
from jax.experimental import pallas as pl
from jax.experimental.pallas import tpu as pltpu

F32 = jnp.float32
BF16 = jnp.bfloat16
MESH = pl.DeviceIdType.MESH

D_MODEL = 1024
N_HEADS = 8
HEAD_DIM = 64
D_HEADS = N_HEADS * HEAD_DIM
SG_BLOCK = 128
CHUNK = 64
D_FF = 2816
D_IN = 2 * D_HEADS + 3 * D_HEADS + N_HEADS
LANES = 128
SUBLANES = 8
D_IN_PAD = 5 * D_HEADS + LANES
EPS = 1e-6
SCALE = HEAD_DIM ** -0.5
NEG = -1e30
LOG2E = 1.4426950408889634
HEAD_PAD = LANES
D_PAD = N_HEADS * HEAD_PAD
Q_STAT = HEAD_DIM
K_STAT = HEAD_DIM + 3
L_STAT = HEAD_DIM + 6
GROUPS = 2
GROUP_HEADS = N_HEADS // GROUPS
GROUP_PAD = GROUP_HEADS * HEAD_PAD
KEY_CHUNK = 256
FWD_KEY_CHUNK = 512
STAT_ROWS = 16
FF_CHUNK = 256

ADAM_LR = 0.001
ADAM_B1 = 0.9
ADAM_B2 = 0.999
ADAM_EPS = 1e-08
ADAM_WD = 0.01
ADAM_STEP = 10

VMEM_LIMIT = 56 * 1024 * 1024

NT = (((1,), (1,)), ((), ()))
TN = (((0,), (0,)), ((), ()))


def _params(sem):
    return pltpu.CompilerParams(dimension_semantics=sem, vmem_limit_bytes=VMEM_LIMIT)


def _full(shape):
    nd = len(shape)
    return pl.BlockSpec(shape, lambda *_: (0,) * nd)


def _row_tile(rows, target):
    best = None
    for t in range(SUBLANES, min(rows, target) + 1, SUBLANES):
        if rows % t == 0:
            best = t
    assert best is not None, rows
    return best


def _sigmoid(x):
    return 0.5 * jnp.tanh(0.5 * x) + 0.5


def _gelu(z):
    return 0.5 * z * (1.0 + lax.erf(z * (2.0 ** -0.5)))


def _gelu_grad(z):
    cdf = 0.5 * (1.0 + lax.erf(z * (2.0 ** -0.5)))
    pdf = jnp.exp(-0.5 * z * z) * (1.0 / math.sqrt(2.0 * math.pi))
    return cdf + z * pdf


def _split_dot(x, m):
    hi = x.astype(BF16)
    lo = (x - hi.astype(F32)).astype(BF16)
    return jnp.dot(hi, m, preferred_element_type=F32) + jnp.dot(lo, m, preferred_element_type=F32)


def _head_mask(h, rows):
    lane = lax.broadcasted_iota(jnp.int32, (rows, D_HEADS), 1)
    return (lane >= h * HEAD_DIM) & (lane < (h + 1) * HEAD_DIM)


def _rms_bwd(dh, x, g):
    r = lax.rsqrt(jnp.mean(x * x, axis=-1, keepdims=True) + EPS)
    xhat = x * r
    dg = jnp.sum(dh * xhat, axis=0, keepdims=True)
    dxhat = dh * g
    dx = r * (dxhat - xhat * jnp.mean(dxhat * xhat, axis=-1, keepdims=True))
    return dx, dg


def _in_proj(x, g1, w_in, tm):
    S = x.shape[0]
    nz = D_IN_PAD - LANES

    def body(x_ref, g_ref, w_ref, z_ref, f_ref, h_ref):
        xf = x_ref[...]
        r = lax.rsqrt(jnp.mean(xf * xf, axis=-1, keepdims=True) + EPS)
        h = (xf * r * g_ref[...]).astype(BF16)
        h_ref[...] = h
        zz = jnp.dot(h, w_ref[...], preferred_element_type=F32)
        z_ref[...] = zz[:, :nz].astype(BF16)
        f_ref[...] = zz[:, nz:]

    return pl.pallas_call(
        body, name="in_proj", grid=(S // tm,),
        in_specs=[pl.BlockSpec((tm, D_MODEL), lambda i: (i, 0)), _full((1, D_MODEL)), _full((D_MODEL, D_IN_PAD))],
        out_specs=[pl.BlockSpec((tm, nz), lambda i: (i, 0)), pl.BlockSpec((tm, LANES), lambda i: (i, 0)),
                   pl.BlockSpec((tm, D_MODEL), lambda i: (i, 0))],
        out_shape=[jax.ShapeDtypeStruct((S, nz), BF16), jax.ShapeDtypeStruct((S, LANES), F32),
                   jax.ShapeDtypeStruct((S, D_MODEL), BF16)],
        compiler_params=_params(("parallel",)),
    )(x, g1, w_in)


def _fox_prep(f, bias_row, tb):
    S = f.shape[0]

    def body(f_ref, b_ref, c_ref, carry):
        @pl.when(pl.program_id(0) == 0)
        def _():
            carry[...] = jnp.zeros_like(carry)

        xv = f_ref[...] + b_ref[...]
        lf = jnp.minimum(xv, 0.0) - jnp.log(1.0 + jnp.exp(-jnp.abs(xv)))
        r = lax.broadcasted_iota(jnp.int32, (tb, tb), 0)
        s = lax.broadcasted_iota(jnp.int32, (tb, tb), 1)
        tri = (r >= s).astype(F32)
        cs = jnp.dot(tri, lf, precision=lax.Precision.HIGHEST, preferred_element_type=F32) + carry[0:1, :]
        c_ref[...] = cs
        carry[...] = jnp.broadcast_to(cs[tb - 1:tb, :], carry.shape)

    return pl.pallas_call(
        body, name="fox_prep", grid=(S // tb,),
        in_specs=[pl.BlockSpec((tb, LANES), lambda i: (i, 0)), _full((1, LANES))],
        out_specs=pl.BlockSpec((tb, LANES), lambda i: (i, 0)),
        out_shape=jax.ShapeDtypeStruct((S, LANES), F32),
        scratch_shapes=[pltpu.VMEM((SUBLANES, LANES), F32)],
        compiler_params=_params(("arbitrary",)),
    )(f, bias_row)


def _attn_consts():
    col = jnp.arange(D_PAD)
    row = jnp.arange(D_HEADS)
    head = jnp.arange(LANES)
    place = (row[:, None] // HEAD_DIM == col[None, :] // HEAD_PAD) & (row[:, None] % HEAD_DIM == col[None, :] % HEAD_PAD)

    def stat(offset):
        return ((head[:, None] < N_HEADS) & (col[None, :] == head[:, None] * HEAD_PAD + offset)).astype(BF16)

    def stat3(base):
        part, h = head // N_HEADS, head % N_HEADS
        return ((part[:, None] < 3) & (col[None, :] == h[:, None] * HEAD_PAD + base + part[:, None])).astype(BF16)

    def ones(offsets):
        return sum((col % HEAD_PAD == o) for o in offsets).astype(F32).reshape(1, D_PAD)

    place = place.astype(BF16)
    return {
        "place": place, "place_t": place.T, "place_t_group": place.T[:GROUP_PAD, :GROUP_HEADS * HEAD_DIM],
        "q_stat": stat3(Q_STAT), "k_stat": stat3(K_STAT),
        "d_stat": stat3(Q_STAT) * (head[:, None] < 2 * N_HEADS).astype(BF16),
        "l_stat": jnp.concatenate([stat(L_STAT + j)[:STAT_ROWS] for j in range(3)], axis=0),
        "q_ones": ones(range(K_STAT, K_STAT + 3)), "k_ones": ones(list(range(Q_STAT, Q_STAT + 3)) + list(range(L_STAT, L_STAT + 3))),
        "v_ones": ones(range(Q_STAT, Q_STAT + 2)),
    }


def _split3(x):
    hi = x.astype(BF16)
    r = x - hi.astype(F32)
    mid = r.astype(BF16)
    return hi, mid, (r - mid.astype(F32)).astype(BF16)


def _attn_pack(z, c, k, tm):
    S = z.shape[0]

    def body(q_ref, k_ref, v_ref, c_ref, pl_ref, pt_ref, qs_ref, ks_ref, qo_ref, ko_ref, vo_ref, voc_ref,
             qa_ref, ka_ref, va_ref, vt_ref):
        place = pl_ref[...]
        q = (q_ref[...].astype(F32) * (SCALE * LOG2E)).astype(BF16)
        qa = jnp.dot(q, place, preferred_element_type=F32) + qo_ref[...]
        ka = jnp.dot(k_ref[...], place, preferred_element_type=F32) + ko_ref[...]
        lane = lax.broadcasted_iota(jnp.int32, (tm, LANES), 1)
        hi, mid, lo = _split3(jnp.where(lane < N_HEADS, c_ref[...] * LOG2E, 0.0))
        parts = hi.astype(F32) + pltpu.roll(mid.astype(F32), N_HEADS, 1) + pltpu.roll(lo.astype(F32), 2 * N_HEADS, 1)
        parts = parts.astype(BF16)
        qa = qa + jnp.dot(parts, qs_ref[...], preferred_element_type=F32)
        ka = ka - jnp.dot(parts, ks_ref[...], preferred_element_type=F32)
        qa_ref[...] = qa.astype(BF16)
        ka_ref[...] = ka.astype(BF16)
        v = v_ref[...]
        va_ref[...] = (jnp.dot(v, place, preferred_element_type=F32) + vo_ref[...]).astype(BF16)
        vt_ref[...] = (lax.dot_general(pt_ref[...], v, NT, preferred_element_type=F32) + voc_ref[...]).astype(BF16)

    blk = lambda col: pl.BlockSpec((tm, D_HEADS), lambda i: (i, col))
    out = pl.BlockSpec((tm, D_PAD), lambda i: (i, 0))
    pad = jax.ShapeDtypeStruct((S, D_PAD), BF16)
    return pl.pallas_call(
        body, name="attn_pack", grid=(S // tm,),
        in_specs=[blk(2), blk(3), blk(4), pl.BlockSpec((tm, LANES), lambda i: (i, 0)), _full((D_HEADS, D_PAD)), _full((D_PAD, D_HEADS)),
                  _full((LANES, D_PAD)), _full((LANES, D_PAD)), _full((1, D_PAD)), _full((1, D_PAD)), _full((1, D_PAD)),
                  _full((D_PAD, 1))],
        out_specs=[out, out, out, pl.BlockSpec((None, D_PAD, tm), lambda i: (i, 0, 0))],
        out_shape=[pad, pad, pad, jax.ShapeDtypeStruct((S // tm, D_PAD, tm), BF16)],
        compiler_params=_params(("parallel",)),
    )(z, z, z, c, k["place"], k["place_t"], k["q_stat"], k["k_stat"], k["q_ones"], k["k_ones"], k["v_ones"], k["v_ones"].T)


def _attn_fwd(qa, ka, vat, place_t, tq, shards):
    S = qa.shape[0]
    n = S // tq
    ns = len(shards)
    hand_on_at = (2 * n) // 3

    pairs = [(q, k) for q in range(n) for k in range(q + 1)]
    q_of = jnp.asarray([q for q, _ in pairs], jnp.int32)
    k_of = jnp.asarray([k for _, k in pairs], jnp.int32)

    def body(q_of_ref, k_of_ref, q_ref, k_ref, vt_ref, pt_ref, *rest):
        o_ref, lse_ref = rest[ns:ns + 2]
        m_s, acc_s, ot_s = rest[2 * ns + 2:2 * ns + 5]
        s_s = rest[2 * ns + 5:2 * ns + 7]
        start, hand_on, finish = _gather_ops(rest[:ns], rest[ns + 2:2 * ns + 2], *rest[2 * ns + 7:])
        qi, ki = q_of_ref[pl.program_id(0)], k_of_ref[pl.program_id(0)]

        @pl.when((qi == 0) & (ki == 0))
        def _():
            start()

        @pl.when((qi == hand_on_at) & (ki == 0))
        def _():
            hand_on()

        @pl.when(ki == 0)
        def _():
            m_s[...] = jnp.full_like(m_s, NEG)
            acc_s[...] = jnp.zeros_like(acc_s)

        def step(diagonal):
            kc = FWD_KEY_CHUNK
            chunks = [slice(c * kc, (c + 1) * kc) for c in range(tq // kc)]

            def scores(h, rows, slot):
                sl = slice(h * HEAD_PAD, (h + 1) * HEAD_PAD)
                st = lax.dot_general(k_ref[rows, sl], q_ref[:, sl], NT, preferred_element_type=F32)
                if diagonal:
                    key = rows.start + lax.broadcasted_iota(jnp.int32, (kc, tq), 0)
                    query = lax.broadcasted_iota(jnp.int32, (kc, tq), 1)
                    st = jnp.where(query >= key, st, NEG)
                s_s[slot][rows, :] = st
                return jnp.max(st, axis=0, keepdims=True)

            m_cur = functools.reduce(jnp.maximum, [scores(0, rows, 0) for rows in chunks])
            for h in range(N_HEADS):
                sl = slice(h * HEAD_PAD, (h + 1) * HEAD_PAD)
                slot = h % 2
                m_prev = m_s[h][0:1, :]
                m_new = jnp.maximum(m_prev, m_cur)
                acc = jnp.exp2(m_prev - m_new) * acc_s[h]
                m_next = []
                for rows in chunks:
                    if h + 1 < N_HEADS:
                        m_next.append(scores(h + 1, rows, 1 - slot))
                    pt = jnp.exp2(s_s[slot][rows, :] - m_new).astype(BF16)
                    acc = acc + jnp.dot(vt_ref[sl, rows], pt, preferred_element_type=F32)
                acc_s[h] = acc
                m_s[h] = jnp.broadcast_to(m_new, (SUBLANES, tq))
                if m_next:
                    m_cur = functools.reduce(jnp.maximum, m_next)

        @pl.when(ki < qi)
        def _():
            step(False)

        @pl.when(ki == qi)
        def _():
            step(True)
            lse_ref[...] = jnp.zeros_like(lse_ref)
            for h in range(N_HEADS):
                acc = acc_s[h]
                denom = acc[Q_STAT:Q_STAT + 1, :]
                ot_s[h * HEAD_PAD:(h + 1) * HEAD_PAD, :] = (acc / denom).astype(BF16)
                lse_ref[h:h + 1, :] = m_s[h][0:1, :] + jnp.log(denom) * LOG2E
            o_ref[...] = lax.dot_general(ot_s[...], pt_ref[...], TN, preferred_element_type=F32).astype(BF16)

        @pl.when((qi == n - 1) & (ki == n - 1))
        def _():
            finish()

    out = pl.pallas_call(
        body, name="attn_fwd",
        grid_spec=pltpu.PrefetchScalarGridSpec(
            num_scalar_prefetch=2, grid=(len(pairs),),
            in_specs=[pl.BlockSpec((tq, D_PAD), lambda i, qs, ks: (qs[i], 0)),
                      pl.BlockSpec((tq, D_PAD), lambda i, qs, ks: (ks[i], 0)),
                      pl.BlockSpec((None, D_PAD, tq), lambda i, qs, ks: (ks[i], 0, 0)),
                      pl.BlockSpec((D_PAD, D_HEADS), lambda i, qs, ks: (0, 0))]
            + [_ANY] * ns,
            out_specs=[pl.BlockSpec((tq, D_HEADS), lambda i, qs, ks: (qs[i], 0)),
                       pl.BlockSpec((STAT_ROWS, tq), lambda i, qs, ks: (0, qs[i]))] + [_ANY] * ns,
            scratch_shapes=[pltpu.VMEM((N_HEADS, SUBLANES, tq), F32), pltpu.VMEM((N_HEADS, HEAD_PAD, tq), F32),
                            pltpu.VMEM((D_PAD, tq), BF16), pltpu.VMEM((tq, tq), F32), pltpu.VMEM((tq, tq), F32)] + _gather_sems(ns)),
        out_shape=[jax.ShapeDtypeStruct((S, D_HEADS), BF16), jax.ShapeDtypeStruct((STAT_ROWS, S), F32)] + _gather_shapes(shards),
        compiler_params=_params(("arbitrary",)),
    )(q_of, k_of, qa, ka, vat, place_t, *shards)
    return out[0], out[1], out[2:]


def _layer_norm_heads(v, seg_avg):
    mu = _split_dot(v, seg_avg)
    d = v - mu
    var = _split_dot(d * d, seg_avg)
    rstd = lax.rsqrt(var + EPS)
    return d * rstd, rstd


def _gate_mix(vn_blk, w_ref, bias):
    acc = bias
    for h in range(N_HEADS):
        vh = jnp.where(_head_mask(h, SG_BLOCK), vn_blk, 0.0).astype(BF16)
        acc = acc + jnp.dot(w_ref[h], vh, preferred_element_type=F32)
    return acc


def _gate_fwd(z, w_mask, ln_row, b_full, seg_avg, tm):
    S = z.shape[0]

    def body(zu_ref, zv_ref, w_ref, ln_ref, b_ref, avg_ref, o_ref):
        u = _gelu(zu_ref[...].astype(F32))
        v = _gelu(zv_ref[...].astype(F32))
        vhat, _ = _layer_norm_heads(v, avg_ref[...])
        vn = vhat * ln_ref[...]
        for b in range(tm // SG_BLOCK):
            rows = slice(b * SG_BLOCK, (b + 1) * SG_BLOCK)
            mixed = _gate_mix(vn[rows], w_ref, b_ref[...])
            o_ref[rows, :] = (u[rows] * mixed).astype(BF16)

    return pl.pallas_call(
        body, name="gate_fwd", grid=(S // tm,),
        in_specs=[pl.BlockSpec((tm, D_HEADS), lambda i: (i, 0)), pl.BlockSpec((tm, D_HEADS), lambda i: (i, 1)),
                  _full((N_HEADS, SG_BLOCK, SG_BLOCK)), _full((1, D_HEADS)), _full((SG_BLOCK, D_HEADS)),
                  _full((D_HEADS, D_HEADS))],
        out_specs=pl.BlockSpec((tm, D_HEADS), lambda i: (i, 0)),
        out_shape=jax.ShapeDtypeStruct((S, D_HEADS), BF16),
        compiler_params=_params(("parallel",)),
    )(z, z, w_mask, ln_row, b_full, seg_avg)


def _mix_out(x, out_a, out_b, w_out, g2, tm):
    S = x.shape[0]

    def body(x_ref, a_ref, b_ref, w_ref, g_ref, x1_ref, h_ref):
        y = jnp.dot(a_ref[...], w_ref[:D_HEADS, :], preferred_element_type=F32)
        y = y + jnp.dot(b_ref[...], w_ref[D_HEADS:, :], preferred_element_type=F32)
        x1 = x_ref[...] + y
        x1_ref[...] = x1
        r = lax.rsqrt(jnp.mean(x1 * x1, axis=-1, keepdims=True) + EPS)
        h_ref[...] = (x1 * r * g_ref[...]).astype(BF16)

    row = lambda w: pl.BlockSpec((tm, w), lambda i: (i, 0))
    return pl.pallas_call(
        body, name="mix_out", grid=(S // tm,),
        in_specs=[row(D_MODEL), row(D_HEADS), row(D_HEADS), _full((D_MODEL, D_MODEL)), _full((1, D_MODEL))],
        out_specs=[row(D_MODEL), row(D_MODEL)],
        out_shape=[jax.ShapeDtypeStruct((S, D_MODEL), F32), jax.ShapeDtypeStruct((S, D_MODEL), BF16)],
        compiler_params=_params(("parallel",)),
    )(x, out_a, out_b, w_out, g2)


def _up_proj(h2, w_up_q, tm):
    S = h2.shape[0]
    nq, _, wq = w_up_q.shape

    def body(h_ref, w_ref, a_ref):
        a_ref[...] = jnp.dot(h_ref[...], w_ref[...], preferred_element_type=F32).astype(BF16)

    return pl.pallas_call(
        body, name="up_proj", grid=(nq, S // tm),
        in_specs=[pl.BlockSpec((tm, D_MODEL), lambda j, i: (i, 0)), pl.BlockSpec((None, D_MODEL, wq), lambda j, i: (j, 0, 0))],
        out_specs=pl.BlockSpec((tm, wq), lambda j, i: (i, j)),
        out_shape=jax.ShapeDtypeStruct((S, nq * wq), BF16),
        compiler_params=_params(("parallel", "parallel")),
    )(h2, w_up_q)


def _shift_down(a, halo, k):
    tm = a.shape[0]
    ra = pltpu.roll(a, k, 0)
    rh = pltpu.roll(halo, k, 0)
    row = lax.broadcasted_iota(jnp.int32, halo.shape, 0)
    top = jnp.where(row < k, rh, ra[0:SUBLANES])
    return jnp.concatenate([top, ra[SUBLANES:tm]], axis=0)


def _shift_matrices(tm):
    row = lax.broadcasted_iota(jnp.int32, (tm, tm), 0)
    col = lax.broadcasted_iota(jnp.int32, (tm, tm), 1)
    return [(row == col + k).astype(BF16) for k in (1, 2)]


def _conv_taps(a, halo, first, shifts):
    tm = a.shape[0]
    halo = halo.astype(F32) * jnp.where(first, 0.0, 1.0)
    if shifts is None:
        a = a.astype(F32)
        return a, _shift_down(a, halo, 1), _shift_down(a, halo, 2)
    row8 = lax.broadcasted_iota(jnp.int32, halo.shape, 0)
    taps = [a.astype(F32)]
    for k, shift in zip((1, 2), shifts):
        down = jnp.dot(shift, a, preferred_element_type=F32)
        top = down[0:SUBLANES] + jnp.where(row8 < k, pltpu.roll(halo, k, 0), 0.0)
        taps.append(jnp.concatenate([top, down[SUBLANES:tm]], axis=0))
    return taps


def _conv_gate_val(refs, shifts, cols, first):
    ag_ref, av_ref, hg_ref, hv_ref, wg_ref, wv_ref, bg_ref, bv_ref = refs
    g0, g1, g2 = _conv_taps(ag_ref[:, cols], hg_ref[:, cols], first, shifts)
    gate = wg_ref[2:3, cols] * g0 + wg_ref[1:2, cols] * g1 + wg_ref[0:1, cols] * g2 + bg_ref[:, cols]
    v0, v1, v2 = _conv_taps(av_ref[:, cols], hv_ref[:, cols], first, shifts)
    val = wv_ref[2:3, cols] * v0 + wv_ref[1:2, cols] * v1 + wv_ref[0:1, cols] * v2 + bv_ref[:, cols]
    return gate, val, (g2, g1, g0), (v2, v1, v0)


_FF_CHUNKS = [slice(j * FF_CHUNK, (j + 1) * FF_CHUNK) for j in range(D_FF // FF_CHUNK)]


def _conv_specs(tm):
    step = tm // SUBLANES
    prev = lambda i: jnp.maximum(i * step - 1, 0)
    return [pl.BlockSpec((tm, D_FF), lambda i: (i, 0)), pl.BlockSpec((tm, D_FF), lambda i: (i, 1)),
            pl.BlockSpec((SUBLANES, D_FF), lambda i: (prev(i), 0)), pl.BlockSpec((SUBLANES, D_FF), lambda i: (prev(i), 1))]


def _ffn_fwd_loss(a, w_conv, b_conv, w_down, x1, g3, target, tm):
    S = x1.shape[0]

    def body(ag_ref, av_ref, hg_ref, hv_ref, wg_ref, wv_ref, bg_ref, bv_ref, wd_ref, x1_ref, g_ref, t_ref,
             dx2_ref, loss_ref, dg_ref):
        i = pl.program_id(0)

        @pl.when(i == 0)
        def _():
            loss_ref[...] = jnp.zeros_like(loss_ref)
            dg_ref[...] = jnp.zeros_like(dg_ref)

        x2 = x1_ref[...]
        for cols in _FF_CHUNKS:
            gate, val, _, _ = _conv_gate_val((ag_ref, av_ref, hg_ref, hv_ref, wg_ref, wv_ref, bg_ref, bv_ref), None, cols, i == 0)
            half = 0.5 * gate
            y = ((half + half * jnp.tanh(half)) * val).astype(BF16)
            x2 = x2 + jnp.dot(y, wd_ref[cols, :], preferred_element_type=F32)
        r = lax.rsqrt(jnp.mean(x2 * x2, axis=-1, keepdims=True) + EPS)
        xhat = x2 * r
        gg = g_ref[...]
        err = xhat * gg - t_ref[...]
        loss_ref[...] += jnp.sum(err * err, axis=0, keepdims=True)
        dy = err * (1.0 / D_MODEL)
        dg_ref[...] += jnp.sum(dy * xhat, axis=0, keepdims=True)
        dxhat = dy * gg
        dx2_ref[...] = r * (dxhat - xhat * jnp.mean(dxhat * xhat, axis=-1, keepdims=True))

    row = lambda w: pl.BlockSpec((tm, w), lambda i: (i, 0))
    half = lambda r: [pl.BlockSpec((r, D_FF), lambda i: (0, 0)), pl.BlockSpec((r, D_FF), lambda i: (0, 1))]
    return pl.pallas_call(
        body, name="ffn_fwd_loss", grid=(S // tm,),
        in_specs=_conv_specs(tm) + half(3) + half(1) + [_full((D_FF, D_MODEL)), row(D_MODEL), _full((1, D_MODEL)), row(D_MODEL)],
        out_specs=[row(D_MODEL), _full((1, D_MODEL)), _full((1, D_MODEL))],
        out_shape=[jax.ShapeDtypeStruct((S, D_MODEL), F32), jax.ShapeDtypeStruct((1, D_MODEL), F32),
                   jax.ShapeDtypeStruct((1, D_MODEL), F32)],
        compiler_params=_params(("arbitrary",)),
    )(a, a, a, a, w_conv, w_conv, b_conv, b_conv, w_down, x1, g3, target)


def _ffn_bwd_gate(dx2, a, w_conv, b_conv, w_down, tm):
    S = dx2.shape[0]

    def body(dx_ref, ag_ref, av_ref, hg_ref, hv_ref, wg_ref, wv_ref, bg_ref, bv_ref, wd_ref,
             dc_ref, y_ref, dw_ref, db_ref):
        i = pl.program_id(0)

        @pl.when(i == 0)
        def _():
            dw_ref[...] = jnp.zeros_like(dw_ref)
            db_ref[...] = jnp.zeros_like(db_ref)

        dx = dx_ref[...].astype(BF16)
        shifts = _shift_matrices(tm)
        for cols in _FF_CHUNKS:
            gate, val, gtaps, vtaps = _conv_gate_val((ag_ref, av_ref, hg_ref, hv_ref, wg_ref, wv_ref, bg_ref, bv_ref), shifts, cols, i == 0)
            sg = _sigmoid(gate)
            act = gate * sg
            y_ref[:, cols] = (act * val).astype(BF16)
            dy = lax.dot_general(dx, wd_ref[cols, :], NT, preferred_element_type=F32)
            dgate = dy * val * (sg + act - act * sg)
            dval = dy * act
            for d, taps, out in ((dgate, gtaps, cols), (dval, vtaps, slice(D_FF + cols.start, D_FF + cols.stop))):
                dc_ref[:, out] = d.astype(BF16)
                db_ref[0:1, out] += jnp.sum(d, axis=0, keepdims=True)
                for j in range(3):
                    dw_ref[j:j + 1, out] += jnp.sum(d * taps[j], axis=0, keepdims=True)

    row = lambda w: pl.BlockSpec((tm, w), lambda i: (i, 0))
    half = lambda r: [pl.BlockSpec((r, D_FF), lambda i: (0, 0)), pl.BlockSpec((r, D_FF), lambda i: (0, 1))]
    return pl.pallas_call(
        body, name="ffn_bwd_gate", grid=(S // tm,),
        in_specs=[row(D_MODEL)] + _conv_specs(tm) + half(3) + half(1) + [_full((D_FF, D_MODEL))],
        out_specs=[row(2 * D_FF), row(D_FF), _full((SUBLANES, 2 * D_FF)), _full((1, 2 * D_FF))],
        out_shape=[jax.ShapeDtypeStruct((S, 2 * D_FF), BF16), jax.ShapeDtypeStruct((S, D_FF), BF16),
                   jax.ShapeDtypeStruct((SUBLANES, 2 * D_FF), F32), jax.ShapeDtypeStruct((1, 2 * D_FF), F32)],
        compiler_params=_params(("arbitrary",)),
    )(dx2, a, a, a, a, w_conv, w_conv, b_conv, b_conv, w_down)


def _conv_bwd(dc, w_conv, tm, tn):
    S, C = dc.shape
    step = tm // SUBLANES
    last_blk = S // SUBLANES - 1

    def body(d_ref, nx_ref, w_ref, o_ref):
        last = pl.program_id(0) == pl.num_programs(0) - 1
        row = lax.broadcasted_iota(jnp.int32, (tm, tm), 0)
        col = lax.broadcasted_iota(jnp.int32, (tm, tm), 1)
        row8 = lax.broadcasted_iota(jnp.int32, (SUBLANES, FF_CHUNK), 0)
        ups = [(row + k == col).astype(BF16) for k in (1, 2)]
        for c0 in range(0, tn, FF_CHUNK):
            cols = slice(c0, c0 + FF_CHUNK)
            d = d_ref[:, cols]
            nx = nx_ref[:, cols].astype(F32) * jnp.where(last, 0.0, 1.0)
            out = w_ref[2:3, cols] * d.astype(F32)
            for k, up in zip((1, 2), ups):
                moved = jnp.dot(up, d, preferred_element_type=F32)
                bottom = moved[tm - SUBLANES:tm] + jnp.where(row8 >= SUBLANES - k, pltpu.roll(nx, SUBLANES - k, 0), 0.0)
                out = out + w_ref[2 - k:3 - k, cols] * jnp.concatenate([moved[0:tm - SUBLANES], bottom], axis=0)
            o_ref[:, cols] = out.astype(BF16)

    return pl.pallas_call(
        body, name="conv_bwd", grid=(S // tm, C // tn),
        in_specs=[pl.BlockSpec((tm, tn), lambda i, j: (i, j)),
                  pl.BlockSpec((SUBLANES, tn), lambda i, j: (jnp.minimum((i + 1) * step, last_blk), j)),
                  pl.BlockSpec((3, tn), lambda i, j: (0, j))],
        out_specs=pl.BlockSpec((tm, tn), lambda i, j: (i, j)),
        out_shape=jax.ShapeDtypeStruct((S, C), BF16),
        compiler_params=_params(("parallel", "parallel")),
    )(dc, dc, w_conv)


def _matmul_tn(a, b, name, bm, bn, tk, quarters):
    S = a.shape[0]
    gm, gn = quarters
    nk = S // tk

    def body(a_ref, b_ref, o_ref):
        @pl.when(pl.program_id(2) == 0)
        def _():
            o_ref[...] = jnp.zeros_like(o_ref)

        o_ref[...] += lax.dot_general(a_ref[...].astype(BF16), b_ref[...].astype(BF16), TN, preferred_element_type=F32)

    if gn > 1:
        out_spec = pl.BlockSpec((None, bm, bn), lambda i, j, k: (j, i, 0))
        out_shape = jax.ShapeDtypeStruct((gn, gm * bm, bn), F32)
    else:
        out_spec = pl.BlockSpec((bm, bn), lambda i, j, k: (i, j))
        out_shape = jax.ShapeDtypeStruct((gm * bm, gn * bn), F32)
    return pl.pallas_call(
        body, name=name, grid=(gm, gn, nk),
        in_specs=[pl.BlockSpec((tk, bm), lambda i, j, k: (k, i)), pl.BlockSpec((tk, bn), lambda i, j, k: (k, j))],
        out_specs=out_spec, out_shape=out_shape,
        compiler_params=_params(("parallel", "parallel", "arbitrary")),
    )(a, b)


def _dw_out(out_a, out_b, dx1, tk):
    S = dx1.shape[0]

    def body(a_ref, b_ref, d_ref, o_ref):
        @pl.when(pl.program_id(0) == 0)
        def _():
            o_ref[...] = jnp.zeros_like(o_ref)

        d = d_ref[...].astype(BF16)
        o_ref[:D_HEADS, :] += lax.dot_general(a_ref[...], d, TN, preferred_element_type=F32)
        o_ref[D_HEADS:, :] += lax.dot_general(b_ref[...], d, TN, preferred_element_type=F32)

    row = lambda w: pl.BlockSpec((tk, w), lambda k: (k, 0))
    return pl.pallas_call(
        body, name="dw_out", grid=(S // tk,),
        in_specs=[row(D_HEADS), row(D_HEADS), row(D_MODEL)], out_specs=_full((D_MODEL, D_MODEL)),
        out_shape=jax.ShapeDtypeStruct((D_MODEL, D_MODEL), F32),
        compiler_params=_params(("arbitrary",)),
    )(out_a, out_b, dx1)


def _up_bwd(dact, w_up_q, x1, g2, dx2, tm):
    S = x1.shape[0]
    nq, _, wq = w_up_q.shape

    def body(d_ref, w_ref, x_ref, g_ref, dx2_ref, dx1_ref, dg_ref):
        @pl.when(pl.program_id(0) == 0)
        def _():
            dg_ref[...] = jnp.zeros_like(dg_ref)

        dh = jnp.zeros((tm, D_MODEL), F32)
        for j in range(nq):
            dh = dh + lax.dot_general(d_ref[:, j * wq:(j + 1) * wq], w_ref[j], NT, preferred_element_type=F32)
        dx, dg = _rms_bwd(dh, x_ref[...], g_ref[...])
        dg_ref[...] += dg
        dx1_ref[...] = dx2_ref[...] + dx

    row = lambda w: pl.BlockSpec((tm, w), lambda i: (i, 0))
    return pl.pallas_call(
        body, name="up_bwd", grid=(S // tm,),
        in_specs=[row(nq * wq), pl.BlockSpec((nq, D_MODEL, wq), lambda i: (0, 0, 0), pipeline_mode=pl.Buffered(1)),
                  row(D_MODEL), _full((1, D_MODEL)), row(D_MODEL)],
        out_specs=[row(D_MODEL), _full((1, D_MODEL))],
        out_shape=[jax.ShapeDtypeStruct((S, D_MODEL), F32), jax.ShapeDtypeStruct((1, D_MODEL), F32)],
        compiler_params=_params(("arbitrary",)),
    )(dact, w_up_q, x1, g2, dx2)


def _out_bwd(dx1, w_out, tm):
    S = dx1.shape[0]

    def body(d_ref, w_ref, o_ref):
        o_ref[...] = lax.dot_general(d_ref[...].astype(BF16), w_ref[...], NT, preferred_element_type=F32).astype(BF16)

    return pl.pallas_call(
        body, name="out_bwd", grid=(S // tm,),
        in_specs=[pl.BlockSpec((tm, D_MODEL), lambda i: (i, 0)), _full((D_MODEL, D_MODEL))],
        out_specs=pl.BlockSpec((tm, D_MODEL), lambda i: (i, 0)),
        out_shape=jax.ShapeDtypeStruct((S, D_MODEL), BF16),
        compiler_params=_params(("parallel",)),
    )(dx1, w_out)


def _gate_bwd(z, dcat, w_mask, w_mask_t, ln_row, b_full, seg_avg, head_ind, tm, swap):
    S = z.shape[0]
    nb = tm // SG_BLOCK
    ns = len(swap)

    def body(zu_ref, zv_ref, do_ref, w_ref, wt_ref, ln_ref, b_ref, avg_ref, ind_ref, *rest):
        dzu_ref, dzv_ref, dw_ref, db_ref, dln_ref = rest[ns:ns + 5]
        dvn_s, dbf_s = rest[2 * ns + 5:2 * ns + 7]
        swap_start, swap_finish = _swap_ops(rest[:ns], rest[ns + 5:2 * ns + 5], *rest[2 * ns + 7:])
        i = pl.program_id(0)

        @pl.when(i == 0)
        def _():
            swap_start()
            dw_ref[...] = jnp.zeros_like(dw_ref)
            dln_ref[...] = jnp.zeros_like(dln_ref)
            dbf_s[...] = jnp.zeros_like(dbf_s)

        zu = zu_ref[...].astype(F32)
        zv = zv_ref[...].astype(F32)
        u = _gelu(zu)
        v = _gelu(zv)
        avg = avg_ref[...]
        vhat, rstd = _layer_norm_heads(v, avg)
        ln = ln_ref[...]
        vn = vhat * ln
        for b in range(nb):
            rows = slice(b * SG_BLOCK, (b + 1) * SG_BLOCK)
            vn_b = vn[rows]
            mixed = _gate_mix(vn_b, w_ref, b_ref[...])
            do = do_ref[rows, :].astype(F32)
            dzu_ref[rows, :] = (do * mixed * _gelu_grad(zu[rows])).astype(BF16)
            dmix = do * u[rows]
            dbf_s[...] += dmix
            vn_bf = vn_b.astype(BF16)
            dvn = jnp.zeros((SG_BLOCK, D_HEADS), F32)
            for h in range(N_HEADS):
                dmh = jnp.where(_head_mask(h, SG_BLOCK), dmix, 0.0).astype(BF16)
                dw_ref[h] += lax.dot_general(dmh, vn_bf, NT, preferred_element_type=F32)
                dvn = dvn + jnp.dot(wt_ref[h], dmh, preferred_element_type=F32)
            dvn_s[rows, :] = dvn
        dvn = dvn_s[...]
        dln_ref[...] += jnp.sum(dvn * vhat, axis=0, keepdims=True)
        dvhat = dvn * ln
        dv = rstd * (dvhat - _split_dot(dvhat, avg) - vhat * _split_dot(dvhat * vhat, avg))
        dzv_ref[...] = (dv * _gelu_grad(zv)).astype(BF16)

        @pl.when(i == pl.num_programs(0) - 1)
        def _():
            r = lax.broadcasted_iota(jnp.int32, (SG_BLOCK, SG_BLOCK), 0) // CHUNK
            s = lax.broadcasted_iota(jnp.int32, (SG_BLOCK, SG_BLOCK), 1) // CHUNK
            for h in range(N_HEADS):
                dw_ref[h] = jnp.where(r >= s, dw_ref[h], 0.0)
            db_ref[...] = _split_dot(dbf_s[...], ind_ref[...])
            swap_finish()

    row = lambda col: pl.BlockSpec((tm, D_HEADS), lambda i: (i, col))
    wspec = _full((N_HEADS, SG_BLOCK, SG_BLOCK))
    out = pl.pallas_call(
        body, name="gate_bwd", grid=(S // tm,),
        in_specs=[row(0), row(1), row(0), wspec, wspec, _full((1, D_HEADS)), _full((SG_BLOCK, D_HEADS)),
                  _full((D_HEADS, D_HEADS)), _full((D_HEADS, LANES))] + [_ANY] * ns,
        out_specs=[row(0), row(0), wspec, _full((SG_BLOCK, LANES)), _full((1, D_HEADS))] + [_ANY] * ns,
        out_shape=[jax.ShapeDtypeStruct((S, D_HEADS), BF16), jax.ShapeDtypeStruct((S, D_HEADS), BF16),
                   jax.ShapeDtypeStruct((N_HEADS, SG_BLOCK, SG_BLOCK), F32), jax.ShapeDtypeStruct((SG_BLOCK, LANES), F32),
                   jax.ShapeDtypeStruct((1, D_HEADS), F32)] + _swap_shapes(swap),
        scratch_shapes=[pltpu.VMEM((tm, D_HEADS), F32), pltpu.VMEM((SG_BLOCK, D_HEADS), F32)] + _swap_sems(ns),
        compiler_params=_params(("arbitrary",)),
    )(z, z, dcat, w_mask, w_mask_t, ln_row, b_full, seg_avg, head_ind, *swap)
    return out[:5], out[5:]


def _attn_pack_grad(o, dcat, qa, lse, head_ind, k, tm):
    S = o.shape[0]

    def body(o_ref, do_ref, qa_ref, lse_ref, ind_ref, pl_ref, pt_ref, eye_ref, ds_ref, dst_ref, ls_ref, lst_ref,
             dop_ref, qb_ref, dot_ref, qbt_ref):
        do = do_ref[...]
        delta = _split_dot(o_ref[...].astype(F32) * do.astype(F32), ind_ref[...])
        hi = delta.astype(BF16).astype(F32)
        parts = (hi + pltpu.roll((delta - hi).astype(BF16).astype(F32), N_HEADS, 1)).astype(BF16)
        dop = jnp.dot(do, pl_ref[...], preferred_element_type=F32) - jnp.dot(parts, ds_ref[...], preferred_element_type=F32)
        for g in range(GROUPS):
            dop_ref[g] = dop[:, g * GROUP_PAD:(g + 1) * GROUP_PAD].astype(BF16)
        dot = lax.dot_general(pt_ref[...], do, NT, preferred_element_type=F32)
        dot_ref[...] = (dot - lax.dot_general(dst_ref[...], parts, NT, preferred_element_type=F32)).astype(BF16)
        qa = qa_ref[...]
        stack = jnp.concatenate(_split3(lse_ref[...]), axis=0)
        qb = qa.astype(F32) - lax.dot_general(stack, ls_ref[...], TN, preferred_element_type=F32)
        qbt = lax.dot_general(eye_ref[...], qa, NT, preferred_element_type=F32)
        qbt = qbt - jnp.dot(lst_ref[...], stack, preferred_element_type=F32)
        for g in range(GROUPS):
            qb_ref[g] = qb[:, g * GROUP_PAD:(g + 1) * GROUP_PAD].astype(BF16)
        qbt_ref[...] = qbt.astype(BF16)

    pad = pl.BlockSpec((tm, D_PAD), lambda i: (i, 0))
    padt = pl.BlockSpec((None, D_PAD, tm), lambda i: (i, 0, 0))
    return pl.pallas_call(
        body, name="attn_pack_grad", grid=(S // tm,),
        in_specs=[pl.BlockSpec((tm, D_HEADS), lambda i: (i, 0)), pl.BlockSpec((tm, D_HEADS), lambda i: (i, 1)), pad,
                  pl.BlockSpec((STAT_ROWS, tm), lambda i: (0, i)), _full((D_HEADS, LANES)), _full((D_HEADS, D_PAD)),
                  _full((D_PAD, D_HEADS)), _full((D_PAD, D_PAD)), _full((LANES, D_PAD)), _full((D_PAD, LANES)),
                  _full((3 * STAT_ROWS, D_PAD)), _full((D_PAD, 3 * STAT_ROWS))],
        out_specs=[pl.BlockSpec((GROUPS, tm, GROUP_PAD), lambda i: (0, i, 0))] * 2 + [padt, padt],
        out_shape=[jax.ShapeDtypeStruct((GROUPS, S, GROUP_PAD), BF16)] * 2 + [jax.ShapeDtypeStruct((S // tm, D_PAD, tm), BF16)] * 2,
        compiler_params=_params(("parallel",)),
    )(o, dcat, qa, lse, head_ind, k["place"], k["place_t"], jnp.eye(D_PAD, dtype=BF16), k["d_stat"], k["d_stat"].T,
      k["l_stat"], k["l_stat"].T)


def _attn_bwd(qb, qbt, ka, va, dop, dopt, k, tq, sums16):
    S = ka.shape[0]
    n = S // tq
    ns = len(sums16)

    pairs = [(kb, q) for kb in range(n) for q in range(kb, n)]
    k_of = jnp.asarray([kb for kb, _ in pairs], jnp.int32)
    q_of = jnp.asarray([q for _, q in pairs], jnp.int32)

    def body(k_of_ref, q_of_ref, q_ref, qt_ref, k_ref, v_ref, do_ref, dot_ref, pt_ref, *rest):
        dq_hbm, dcr_hbm, dk_ref, dv_ref, dcc_ref = rest[ns:ns + 5]
        dq_s, dcr_s, dk_s, dv_s, dcc_s = rest[2 * ns + 5:2 * ns + 10]
        s_s, d_s = rest[2 * ns + 10:2 * ns + 12], rest[2 * ns + 12:2 * ns + 14]
        sems = rest[2 * ns + 14]
        scatter_start, scatter_finish = _scatter_ops(rest[:ns], rest[ns + 5:2 * ns + 5], *rest[2 * ns + 15:])
        g = pl.program_id(0)
        ki, qi = k_of_ref[pl.program_id(1)], q_of_ref[pl.program_id(1)]

        @pl.when((g == 0) & (ki == 0) & (qi == 0))
        def _():
            scatter_start()

        @pl.when((ki == 0) & (qi == 0))
        def _():
            dq_s[...] = jnp.zeros_like(dq_s)
            dcr_s[...] = jnp.zeros_like(dcr_s)

        @pl.when(qi == ki)
        def _():
            dk_s[...] = jnp.zeros_like(dk_s)
            dv_s[...] = jnp.zeros_like(dv_s)
            dcc_s[...] = jnp.zeros_like(dcc_s)

        def step(diagonal):
            chunks = [slice(c * KEY_CHUNK, (c + 1) * KEY_CHUNK) for c in range(tq // KEY_CHUNK)]

            def scores(hh, rows, slot):
                sl = slice(hh * HEAD_PAD, (hh + 1) * HEAD_PAD)
                s_s[slot][rows, :] = lax.dot_general(q_ref[rows, sl], k_ref[:, sl], NT, preferred_element_type=F32)
                d_s[slot][rows, :] = lax.dot_general(do_ref[rows, sl], v_ref[:, sl], NT, preferred_element_type=F32)

            for rows in chunks:
                scores(0, rows, 0)
            for hh in range(GROUP_HEADS):
                sl = slice(hh * HEAD_PAD, (hh + 1) * HEAD_PAD)
                slot = hh % 2
                dv, dk = dv_s[sl, :], dk_s[sl, :]
                for rows in chunks:
                    if hh + 1 < GROUP_HEADS:
                        scores(hh + 1, rows, 1 - slot)
                    p = jnp.exp2(s_s[slot][rows, :])
                    if diagonal:
                        row = rows.start + lax.broadcasted_iota(jnp.int32, (KEY_CHUNK, tq), 0)
                        col = lax.broadcasted_iota(jnp.int32, (KEY_CHUNK, tq), 1)
                        p = jnp.where(row >= col, p, 0.0)
                    ds = p * d_s[slot][rows, :]
                    qrows = pl.ds(pl.multiple_of(qi * tq + rows.start, KEY_CHUNK), KEY_CHUNK)
                    dcc_s[hh:hh + 1, :] += jnp.sum(ds, axis=0, keepdims=True)
                    dcr_s[qrows, hh:hh + 1] += jnp.sum(ds, axis=1, keepdims=True)
                    ds = ds.astype(BF16)
                    dv = dv + jnp.dot(dot_ref[sl, rows], p.astype(BF16), preferred_element_type=F32)
                    dk = dk + jnp.dot(qt_ref[sl, rows], ds, preferred_element_type=F32)
                    dq_s[qrows, sl] += jnp.dot(ds, k_ref[:, sl], preferred_element_type=F32)
                dv_s[sl, :] = dv
                dk_s[sl, :] = dk

        @pl.when(qi > ki)
        def _():
            step(False)

        @pl.when(qi == ki)
        def _():
            step(True)

        @pl.when(qi == n - 1)
        def _():
            dk = dk_s[...]
            pt = pt_ref[...]
            dk_ref[...] = lax.dot_general((dk * (1.0 / LOG2E)).astype(BF16), pt, TN, preferred_element_type=F32).astype(BF16)
            dv_ref[...] = lax.dot_general(dv_s[...].astype(BF16), pt, TN, preferred_element_type=F32).astype(BF16)
            dcc_ref[...] = dcc_s[...]

        @pl.when((ki == n - 1) & (qi == n - 1))
        def _():
            copies = [pltpu.make_async_copy(dq_s, dq_hbm.at[g], sems.at[0]), pltpu.make_async_copy(dcr_s, dcr_hbm.at[g], sems.at[1])]
            for cp in copies:
                cp.start()
            for cp in copies:
                cp.wait()

        @pl.when((g == GROUPS - 1) & (ki == n - 1) & (qi == n - 1))
        def _():
            scatter_finish()

    gw = GROUP_HEADS * HEAD_DIM
    qspec = pl.BlockSpec((None, tq, GROUP_PAD), lambda g, i, ks, qs: (g, qs[i], 0))
    qtspec = pl.BlockSpec((None, GROUP_PAD, tq), lambda g, i, ks, qs: (qs[i], g, 0))
    kspec = pl.BlockSpec((tq, GROUP_PAD), lambda g, i, ks, qs: (ks[i], g))
    kout = pl.BlockSpec((tq, gw), lambda g, i, ks, qs: (ks[i], g))
    out = pl.pallas_call(
        body, name="attn_bwd",
        grid_spec=pltpu.PrefetchScalarGridSpec(
            num_scalar_prefetch=2, grid=(GROUPS, len(pairs)),
            in_specs=[qspec, qtspec, kspec, kspec, qspec, qtspec, pl.BlockSpec((GROUP_PAD, gw), lambda g, i, ks, qs: (0, 0))]
            + [_ANY] * ns,
            out_specs=[_ANY, _ANY, kout, kout, pl.BlockSpec((None, SUBLANES, tq), lambda g, i, ks, qs: (g, 0, ks[i]))] + [_ANY] * ns,
            scratch_shapes=[pltpu.VMEM((S, GROUP_PAD), F32), pltpu.VMEM((S, LANES), F32), pltpu.VMEM((GROUP_PAD, tq), F32),
                            pltpu.VMEM((GROUP_PAD, tq), F32), pltpu.VMEM((SUBLANES, tq), F32),
                            pltpu.VMEM((tq, tq), F32), pltpu.VMEM((tq, tq), F32), pltpu.VMEM((tq, tq), F32),
                            pltpu.VMEM((tq, tq), F32), pltpu.SemaphoreType.DMA((2,))]
            + _scatter_sems(ns)),
        out_shape=[jax.ShapeDtypeStruct((GROUPS, S, GROUP_PAD), F32), jax.ShapeDtypeStruct((GROUPS, S, LANES), F32),
                   jax.ShapeDtypeStruct((S, D_HEADS), BF16), jax.ShapeDtypeStruct((S, D_HEADS), BF16),
                   jax.ShapeDtypeStruct((GROUPS, SUBLANES, S), F32)]
        + _scatter_shapes(sums16),
        compiler_params=_params(("arbitrary", "arbitrary")),
    )(k_of, q_of, qb, qbt, ka, va, dop, dopt, k["place_t_group"], *sums16)
    return out[0], out[1], out[2], out[3], out[4], out[5:]


def _attn_unpack(dqp, k, tm):
    S = dqp.shape[1]
    gw = GROUP_HEADS * HEAD_DIM

    def body(dqp_ref, pt_ref, dq_ref):
        for g in range(GROUPS):
            dq_ref[:, g * gw:(g + 1) * gw] = jnp.dot((dqp_ref[g] * SCALE).astype(BF16), pt_ref[...],
                                                     preferred_element_type=F32).astype(BF16)

    return pl.pallas_call(
        body, name="attn_unpack", grid=(S // tm,),
        in_specs=[pl.BlockSpec((GROUPS, tm, GROUP_PAD), lambda i: (0, i, 0)), _full((GROUP_PAD, gw))],
        out_specs=pl.BlockSpec((tm, D_HEADS), lambda i: (i, 0)),
        out_shape=jax.ShapeDtypeStruct((S, D_HEADS), BF16),
        compiler_params=_params(("parallel",)),
    )(dqp, k["place_t_group"])


def _fox_bwd(dc, f, bias_row, tb):
    S = f.shape[0]
    nb = S // tb

    def body(dc_ref, f_ref, b_ref, df_ref, dbias_ref, carry):
        @pl.when(pl.program_id(0) == 0)
        def _():
            carry[...] = jnp.zeros_like(carry)
            dbias_ref[...] = jnp.zeros_like(dbias_ref)

        r = lax.broadcasted_iota(jnp.int32, (tb, tb), 0)
        s = lax.broadcasted_iota(jnp.int32, (tb, tb), 1)
        tri = (s >= r).astype(F32)
        rc = jnp.dot(tri, dc_ref[...], precision=lax.Precision.HIGHEST, preferred_element_type=F32) + carry[0:1, :]
        carry[...] = jnp.broadcast_to(rc[0:1, :], carry.shape)
        lane = lax.broadcasted_iota(jnp.int32, (tb, LANES), 1)
        df = jnp.where(lane < N_HEADS, rc * jax.nn.sigmoid(-(f_ref[...] + b_ref[...])), 0.0)
        df_ref[...] = df.astype(BF16)
        dbias_ref[...] += jnp.sum(df, axis=0, keepdims=True)

    rev = pl.BlockSpec((tb, LANES), lambda i: (nb - 1 - i, 0))
    return pl.pallas_call(
        body, name="fox_bwd", grid=(nb,),
        in_specs=[rev, rev, _full((1, LANES))],
        out_specs=[rev, _full((1, LANES))],
        out_shape=[jax.ShapeDtypeStruct((S, LANES), BF16), jax.ShapeDtypeStruct((1, LANES), F32)],
        scratch_shapes=[pltpu.VMEM((SUBLANES, LANES), F32)],
        compiler_params=_params(("arbitrary",)),
    )(dc, f, bias_row)


_DZ_WIDTHS = (D_HEADS,) * 5 + (LANES,)


def _in_bwd(pieces, w_in, x, g1, dx1, tm, sums16):
    S = x.shape[0]
    ns = len(sums16)

    def body(*refs):
        p_refs, (w_ref, x_ref, g_ref, dx1_ref) = refs[:6], refs[6:10]
        dx_ref, dg_ref = refs[10 + ns:12 + ns]
        scatter_start, scatter_finish = _scatter_ops(refs[10:10 + ns], refs[12 + ns:12 + 2 * ns], *refs[12 + 2 * ns:])

        @pl.when(pl.program_id(0) == 0)
        def _():
            dg_ref[...] = jnp.zeros_like(dg_ref)
            scatter_start()

        dh = jnp.zeros((tm, D_MODEL), F32)
        off = 0
        for p_ref, w in zip(p_refs, _DZ_WIDTHS):
            dh = dh + lax.dot_general(p_ref[...].astype(BF16), w_ref[:, off:off + w], NT, preferred_element_type=F32)
            off += w
        dx, dg = _rms_bwd(dh, x_ref[...], g_ref[...])
        dg_ref[...] += dg
        dx_ref[...] = dx1_ref[...] + dx

        @pl.when(pl.program_id(0) == pl.num_programs(0) - 1)
        def _():
            scatter_finish()

    row = lambda w: pl.BlockSpec((tm, w), lambda i: (i, 0))
    out = pl.pallas_call(
        body, name="in_bwd", grid=(S // tm,),
        in_specs=[row(w) for w in _DZ_WIDTHS] + [_full((D_MODEL, D_IN_PAD)), row(D_MODEL), _full((1, D_MODEL)), row(D_MODEL)]
        + [_ANY] * ns,
        out_specs=[row(D_MODEL), _full((1, D_MODEL))] + [_ANY] * ns,
        out_shape=[jax.ShapeDtypeStruct((S, D_MODEL), F32), jax.ShapeDtypeStruct((1, D_MODEL), F32)] + _scatter_shapes(sums16),
        scratch_shapes=_scatter_sems(ns),
        compiler_params=_params(("arbitrary",)),
    )(*pieces, w_in, x, g1, dx1, *sums16)
    return out[0], out[1], out[2:]


def _dw_in(h1, pieces, tk):
    S = h1.shape[0]

    def body(*refs):
        h_ref, p_refs, o_ref = refs[0], refs[1:7], refs[7]

        @pl.when(pl.program_id(0) == 0)
        def _():
            o_ref[...] = jnp.zeros_like(o_ref)

        off = 0
        for p_ref, w in zip(p_refs, _DZ_WIDTHS):
            o_ref[:, off:off + w] += lax.dot_general(h_ref[...], p_ref[...].astype(BF16), TN, preferred_element_type=F32)
            off += w

    row = lambda w: pl.BlockSpec((tk, w), lambda k: (k, 0))
    return pl.pallas_call(
        body, name="dw_in", grid=(S // tk,),
        in_specs=[row(D_MODEL)] + [row(w) for w in _DZ_WIDTHS],
        out_specs=_full((D_MODEL, D_IN_PAD)),
        out_shape=jax.ShapeDtypeStruct((D_MODEL, D_IN_PAD), F32),
        compiler_params=_params(("arbitrary",)),
    )(h1, *pieces)


def _adamw_math(w, g, m, v):
    m = ADAM_B1 * m + (1.0 - ADAM_B1) * g
    v = ADAM_B2 * v + (1.0 - ADAM_B2) * (g * g)
    m_hat = m / (1.0 - ADAM_B1 ** ADAM_STEP)
    v_hat = v / (1.0 - ADAM_B2 ** ADAM_STEP)
    delta = -ADAM_LR * (m_hat / (jnp.sqrt(v_hat) + ADAM_EPS) + ADAM_WD * w)
    return delta, m, v


def _adamw(name, w, g, m, v):
    R, C = w.shape
    tr = _row_tile(R, 256)

    def body(w_ref, g_ref, m_ref, v_ref, go_ref, d_ref, nm_ref, nv_ref):
        g = g_ref[...]
        d, nm, nv = _adamw_math(w_ref[...], g, m_ref[...], v_ref[...])
        go_ref[...] = g
        d_ref[...] = d
        nm_ref[...] = nm
        nv_ref[...] = nv

    spec = pl.BlockSpec((tr, C), lambda i: (i, 0))
    return pl.pallas_call(
        body, name=name, grid=(R // tr,), in_specs=[spec] * 4, out_specs=[spec] * 4,
        out_shape=[jax.ShapeDtypeStruct((R, C), F32)] * 4,
        compiler_params=_params(("parallel",)),
    )(w, g, m, v)


def _pair_sum(name, grad, theirs, ids):
    q, half, C = theirs.shape
    tr = _row_tile(half, 256)
    nb = half // tr

    def body(ids_ref, a_ref, b_ref, sb_ref):
        sb_ref[...] = (a_ref[...] + b_ref[...]).astype(BF16)

    here = pl.BlockSpec((None, tr, C), lambda j, i, ids: (j, i, 0))
    return pl.pallas_call(
        body, name=name,
        grid_spec=pltpu.PrefetchScalarGridSpec(
            num_scalar_prefetch=1, grid=(q, nb),
            in_specs=[pl.BlockSpec((None, tr, C), lambda j, i, ids: (j, ids[1] * nb + i, 0)), here],
            out_specs=here),
        out_shape=jax.ShapeDtypeStruct((q, half, C), BF16),
        compiler_params=_params(("parallel", "parallel")),
    )(ids, grad, theirs)


def _chip_sum(name, grad, theirs, others, ids):
    _, half, C = theirs.shape
    tr = _row_tile(half, 256)
    nb = half // tr

    def body(ids_ref, a_ref, b_ref, o_ref, s_ref):
        s = a_ref[...] + b_ref[...]
        for j in range(3):
            s = s + o_ref[j].astype(F32)
        s_ref[...] = s

    return pl.pallas_call(
        body, name=name,
        grid_spec=pltpu.PrefetchScalarGridSpec(
            num_scalar_prefetch=1, grid=(nb,),
            in_specs=[pl.BlockSpec((None, tr, C), lambda i, ids: (ids[0], ids[1] * nb + i, 0)),
                      pl.BlockSpec((None, tr, C), lambda i, ids: (ids[0], i, 0)),
                      pl.BlockSpec((3, tr, C), lambda i, ids: (0, i, 0))],
            out_specs=pl.BlockSpec((tr, C), lambda i, ids: (ids[1] * nb + i, 0))),
        out_shape=jax.ShapeDtypeStruct((2 * half, C), F32),
        compiler_params=_params(("parallel",)),
    )(ids, grad, theirs, others)


def _place():
    return lax.axis_index("x"), lax.axis_index("y"), lax.axis_index("c")


def _other_chips(x, y):
    return [(1 - x, y), (x, 1 - y), (1 - x, 1 - y)]


_ANY = pl.BlockSpec(memory_space=pl.ANY)


def _gather_quarters(shards):
    n = len(shards)

    def body(*refs):
        start, hand_on, finish = _gather_ops(refs[:n], refs[n:2 * n], *refs[2 * n:])
        start()
        hand_on()
        finish()

    return pl.pallas_call(
        body, name="gather_weights",
        in_specs=[_ANY] * n, out_specs=[_ANY] * n,
        out_shape=_gather_shapes(shards), scratch_shapes=_gather_sems(n),
    )(*shards)


def _gather_shapes(shards):
    return [jax.ShapeDtypeStruct((4,) + s.shape, s.dtype) for s in shards]


def _gather_sems(n):
    return [pltpu.SemaphoreType.DMA((n, 3))] * 4 + [pltpu.SemaphoreType.DMA((n,))]


def _gather_ops(ins, outs, send_sems, recv_sems, pass_send_sems, pass_recv_sems, own_sems):
    n = len(ins)
    halved = [r.shape[0] % 32 == 0 for r in ins]

    def part(a, quarter, core):
        if not halved[a]:
            return outs[a].at[quarter]
        half = ins[a].shape[0] // 2
        return outs[a].at[quarter, pl.ds(core * half, half), :]

    def ici(a, j, quarter):
        x, y, c = _place()
        px, py = _other_chips(x, y)[j]
        src = ins[a]
        if halved[a]:
            half = src.shape[0] // 2
            src = src.at[pl.ds(c * half, half), :]
        return pltpu.make_async_remote_copy(src_ref=src, dst_ref=part(a, quarter, c), send_sem=send_sems.at[a, j],
                                            recv_sem=recv_sems.at[a, j], device_id=(px, py, c), device_id_type=MESH)

    def passed(a, j, core):
        x, y, c = _place()
        px, py = _other_chips(x, y)[j]
        half = part(a, 2 * px + py, core)
        return pltpu.make_async_remote_copy(src_ref=half, dst_ref=half, send_sem=pass_send_sems.at[a, j],
                                            recv_sem=pass_recv_sems.at[a, j], device_id=(x, y, 1 - c), device_id_type=MESH)

    def own(a):
        x, y, _ = _place()
        return pltpu.make_async_copy(ins[a], outs[a].at[2 * x + y], own_sems.at[a])

    def start():
        x, y, _ = _place()
        for a in range(n):
            for j in range(3):
                ici(a, j, 2 * x + y).start()
            own(a).start()

    def hand_on():
        x, y, c = _place()
        for a in range(n):
            for j, (px, py) in enumerate(_other_chips(x, y)):
                ici(a, j, 2 * px + py).wait_recv()
                if halved[a]:
                    passed(a, j, c).start()

    def finish():
        x, y, c = _place()
        for a in range(n):
            for j in range(3):
                if halved[a]:
                    passed(a, j, 1 - c).wait_recv()
                    passed(a, j, c).wait_send()
                ici(a, j, 2 * x + y).wait_send()
            own(a).wait()

    return start, hand_on, finish


def _swap_halves(grads, name):
    n = len(grads)

    def body(*refs):
        start, finish = _swap_ops(refs[:n], refs[n:2 * n], *refs[2 * n:])
        start()
        finish()

    return pl.pallas_call(
        body, name=name,
        in_specs=[_ANY] * n, out_specs=[_ANY] * n, out_shape=_swap_shapes(grads), scratch_shapes=_swap_sems(n),
    )(*grads)


def _swap_shapes(grads):
    return [jax.ShapeDtypeStruct((4, g.shape[1] // 2, g.shape[2]), F32) for g in grads]


def _swap_sems(n):
    return [pltpu.SemaphoreType.DMA((n,))] * 2


def _swap_ops(ins, outs, send_sems, recv_sems):
    def copy(a):
        x, y, c = _place()
        half = ins[a].shape[1] // 2
        return pltpu.make_async_remote_copy(src_ref=ins[a].at[:, pl.ds((1 - c) * half, half), :], dst_ref=outs[a],
                                            send_sem=send_sems.at[a], recv_sem=recv_sems.at[a],
                                            device_id=(x, y, 1 - c), device_id_type=MESH)

    def start():
        for a in range(len(ins)):
            copy(a).start()

    def finish():
        for a in range(len(ins)):
            copy(a).wait()

    return start, finish


def _scatter_shapes(sums16):
    return [jax.ShapeDtypeStruct((3,) + s.shape[1:], BF16) for s in sums16]


def _scatter_sems(n):
    return [pltpu.SemaphoreType.DMA((n, 3))] * 2


def _scatter_ops(ins, outs, send_sems, recv_sems):
    n = len(ins)

    def copy(a, j):
        x, y, c = _place()
        px, py = _other_chips(x, y)[j]
        return pltpu.make_async_remote_copy(src_ref=ins[a].at[2 * px + py], dst_ref=outs[a].at[j], send_sem=send_sems.at[a, j],
                                            recv_sem=recv_sems.at[a, j], device_id=(px, py, c), device_id_type=MESH)

    def start():
        for a in range(n):
            for j in range(3):
                copy(a, j).start()

    def finish():
        for a in range(n):
            for j in range(3):
                copy(a, j).wait()

    return start, finish


def _join_halves(fulls):
    n = len(fulls)

    def body(*refs):
        ins, outs = refs[:n], refs[n:2 * n]
        send_sems, recv_sems = refs[2 * n:]
        x, y, c = _place()
        started = []
        for a in range(n):
            half = ins[a].shape[0] // 2
            rows = pl.ds(c * half, half)
            cp = pltpu.make_async_remote_copy(src_ref=ins[a].at[rows, :], dst_ref=outs[a].at[rows, :], send_sem=send_sems.at[a],
                                              recv_sem=recv_sems.at[a], device_id=(x, y, 1 - c), device_id_type=MESH)
            cp.start()
            started.append(cp)
        for cp in started:
            cp.wait()

    return pl.pallas_call(
        body, name="join_halves",
        in_specs=[_ANY] * n, out_specs=[_ANY] * n,
        out_shape=[jax.ShapeDtypeStruct(f.shape, F32) for f in fulls],
        input_output_aliases={a: a for a in range(n)},
        scratch_shapes=[pltpu.SemaphoreType.DMA((n,)), pltpu.SemaphoreType.DMA((n,))],
    )(*fulls)


def _small_allreduce(g):
    R = g.shape[0]
    half = R // 2

    def body(g_ref, out_ref, other_s, chip_s, parts_s, send_sems, recv_sems):
        x, y, c = _place()
        mine = 2 * x + y
        rows = pl.ds(pl.multiple_of(c * half, SUBLANES), half)

        def to_other_core(src, dst, k):
            return pltpu.make_async_remote_copy(src_ref=src, dst_ref=dst, send_sem=send_sems.at[k], recv_sem=recv_sems.at[k],
                                                device_id=(x, y, 1 - c), device_id_type=MESH)

        swap = to_other_core(g_ref, other_s, 0)
        swap.start()
        swap.wait()
        chip_s[...] = g_ref[...] + other_s[...]
        parts_s[mine] = chip_s[rows, :]
        sends = []
        for j, (px, py) in enumerate(_other_chips(x, y)):
            cp = pltpu.make_async_remote_copy(src_ref=chip_s.at[rows, :], dst_ref=parts_s.at[mine], send_sem=send_sems.at[1 + j],
                                              recv_sem=recv_sems.at[1 + j], device_id=(px, py, c), device_id_type=MESH)
            cp.start()
            sends.append(cp)
        for cp in sends:
            cp.wait()
        out_ref[rows, :] = (parts_s[0] + parts_s[1]) + (parts_s[2] + parts_s[3])
        join = to_other_core(out_ref.at[rows, :], out_ref.at[rows, :], 4)
        join.start()
        join.wait()

    vm = pl.BlockSpec(memory_space=pltpu.VMEM)
    return pl.pallas_call(
        body, name="small_allreduce",
        in_specs=[vm], out_specs=vm, out_shape=jax.ShapeDtypeStruct((R, LANES), F32),
        scratch_shapes=[pltpu.VMEM((R, LANES), F32), pltpu.VMEM((R, LANES), F32), pltpu.VMEM((4, half, LANES), F32),
                        pltpu.SemaphoreType.DMA((5,)), pltpu.SemaphoreType.DMA((5,))],
        compiler_params=pltpu.CompilerParams(vmem_limit_bytes=VMEM_LIMIT),
    )(g)


def _adamw_small(ws, gs, ms, vs):
    n = len(ws)

    def body(*refs):
        for k in range(n):
            w_ref, g_ref, m_ref, v_ref = (refs[j * n + k] for j in range(4))
            d, nm, nv = _adamw_math(w_ref[...], g_ref[...], m_ref[...], v_ref[...])
            refs[4 * n + k][...] = d
            refs[5 * n + k][...] = nm
            refs[6 * n + k][...] = nv

    vm = pl.BlockSpec(memory_space=pltpu.VMEM)
    out = pl.pallas_call(
        body, name="adamw_small",
        in_specs=[vm] * (4 * n), out_specs=[vm] * (3 * n),
        out_shape=[jax.ShapeDtypeStruct(w.shape, F32) for w in ws] * 3,
        compiler_params=pltpu.CompilerParams(vmem_limit_bytes=VMEM_LIMIT),
    )(*ws, *gs, *ms, *vs)
    return out[:n], out[n:2 * n], out[2 * n:]


_SMALL = (("norm_mix_g", D_MODEL), ("f_bias", N_HEADS), ("sg_ln_g", D_HEADS), ("sg_w", N_HEADS * SG_BLOCK * SG_BLOCK),
          ("sg_b", N_HEADS * SG_BLOCK), ("norm_ffn_g", D_MODEL), ("w_conv", 3 * 2 * D_FF), ("b_conv", 2 * D_FF),
          ("norm_final_g", D_MODEL))
_PACKED = _SMALL + (("sq_err", D_MODEL),)


def _pack_small(parts):
    rows = []
    for name, size in _PACKED:
        flat = parts[name].reshape(-1).astype(F32)
        pad = (-size) % (SUBLANES * LANES)
        rows.append(jnp.pad(flat, (0, pad)).reshape(-1, LANES))
    packed = jnp.concatenate(rows, axis=0)
    return jnp.pad(packed, ((0, (-packed.shape[0]) % (2 * SUBLANES)), (0, 0)))


def _unpack_small(packed, shapes):
    out, r = {}, 0
    for name, size in _PACKED:
        nrows = (size + SUBLANES * LANES - 1) // (SUBLANES * LANES) * SUBLANES
        out[name] = packed[r:r + nrows].reshape(-1)[:size].reshape(shapes[name])
        r += nrows
    return out


def _local_step(x, target, g1, w_in, f_bias, sg_ln_g, sg_w, sg_b, g2, b_conv, g3, late_shards, ids):
    S = x.shape[0]
    tm = _row_tile(S, 512)
    tms = _row_tile(S, 256)
    tq = tm

    lane = jnp.arange(D_HEADS)
    seg_avg = jnp.where(lane[:, None] // HEAD_DIM == lane[None, :] // HEAD_DIM, 1.0 / HEAD_DIM, 0.0).astype(BF16)
    head_ind = (lane[:, None] // HEAD_DIM == jnp.arange(LANES)[None, :]).astype(BF16)
    pos_chunk = jnp.arange(SG_BLOCK) // CHUNK
    w_mask32 = jnp.where(pos_chunk[:, None] >= pos_chunk[None, :], sg_w, 0.0)
    w_mask = w_mask32.astype(BF16)
    w_mask_t = jnp.swapaxes(w_mask32, 1, 2).astype(BF16)
    ln_row = sg_ln_g.reshape(1, D_HEADS)
    b_full = jnp.repeat(sg_b.T, HEAD_DIM, axis=1)
    bias_row = jnp.pad(f_bias.reshape(1, N_HEADS), ((0, 0), (0, LANES - N_HEADS)))
    b_conv_row = b_conv.reshape(1, 2 * D_FF)

    z, f, h1 = _in_proj(x, g1, w_in, tm)
    c = _fox_prep(f, bias_row, tm)
    consts = _attn_consts()
    qa, ka, va, vat = _attn_pack(z, c, consts, tm)
    out_b, lse, gathered = _attn_fwd(qa, ka, vat, consts["place_t"], tq, late_shards)
    g_out, w_up_q, g_down, g_conv = gathered
    w_out = g_out.reshape(D_MODEL, D_MODEL)
    w_down = g_down.reshape(D_FF, D_MODEL)
    w_conv = jnp.concatenate([g_conv[q] for q in range(4)], axis=1)
    out_a = _gate_fwd(z, w_mask, ln_row, b_full, seg_avg, tm)
    x1, h2 = _mix_out(x, out_a, out_b, w_out, g2, tm)
    a = _up_proj(h2, w_up_q, tm)
    dx2, sq_err, dg3 = _ffn_fwd_loss(a, w_conv, b_conv_row, w_down, x1, g3, target, tm)

    dconv, y, dw_conv8, db_conv = _ffn_bwd_gate(dx2, a, w_conv, b_conv_row, w_down, tms)
    dact = _conv_bwd(dconv, w_conv, tms, D_FF)
    dw_down = _matmul_tn(y, dx2, "dw_down", D_FF // 2, D_MODEL, tm, quarters=(2, 1))
    dx1, dg2 = _up_bwd(dact, w_up_q, x1, g2, dx2, tm)
    dw_up_q = _matmul_tn(h2, dact, "dw_up", D_MODEL, 2 * D_FF // 4, tm, quarters=(1, 4))
    dcat = _out_bwd(dx1, w_out, tm)
    dw_out = _dw_out(out_a, out_b, dx1, tm)
    early = {"w_down": dw_down.reshape(4, D_FF // 4, D_MODEL), "w_up": dw_up_q,
             "w_out": dw_out.reshape(4, D_MODEL // 4, D_MODEL)}
    (dzu, dzv, dsg_w, dsg_b_t, dln), theirs = _gate_bwd(z, dcat, w_mask, w_mask_t, ln_row, b_full, seg_avg, head_ind, tm,
                                                        list(early.values()))
    early_sums = _chip_sums(early, theirs, ids)
    dop, qb, dopt, qbt = _attn_pack_grad(out_b, dcat, qa, lse, head_ind, consts, tm)
    dqp, dc_rows, dk, dv, dc_cols, landed = _attn_bwd(qb, qbt, ka, va, dop, dopt, consts, tq,
                                                      [s16 for _, s16 in early_sums.values()])
    early_parts = {k: (s32, got) for (k, (s32, _)), got in zip(early_sums.items(), landed)}
    dq = _attn_unpack(dqp, consts, tm)
    dc_rows = jnp.concatenate([dc_rows[g][:, :GROUP_HEADS] for g in range(GROUPS)], axis=1)
    dc_cols = jnp.concatenate([dc_cols[g][:GROUP_HEADS] for g in range(GROUPS)], axis=0).T
    dc = jnp.pad(dc_rows - dc_cols, ((0, 0), (0, LANES - N_HEADS)))
    df, dbias = _fox_bwd(dc, f, bias_row, tm)
    pieces = (dzu, dzv, dq, dk, dv, df)
    dw_in = _dw_in(h1, pieces, tm)[:, :D_IN].reshape(D_MODEL, 4, D_IN // 4).transpose(1, 0, 2)
    (w_in_sum, w_in_sum16), = _chip_sums({"w_in": dw_in}, _swap_halves([dw_in], "swap_halves"), ids).values()
    dx, dg1, (w_in_landed,) = _in_bwd(pieces, w_in, x, g1, dx1, tm, [w_in_sum16])

    grads = {
        "norm_mix_g": dg1, "f_bias": dbias[:, :N_HEADS], "sg_ln_g": dln, "sg_w": dsg_w, "sg_b": dsg_b_t[:, :N_HEADS].T,
        "norm_ffn_g": dg2, "w_conv": dw_conv8[:3], "b_conv": db_conv, "norm_final_g": dg3,
    }
    return sq_err, dx, grads, {**early_parts, "w_in": (w_in_sum, w_in_landed)}


def _chip_sums(grads_q, theirs, ids):
    return {k: ((g, t), _pair_sum("pair_sum_" + k, g, t, ids)) for (k, g), t in zip(grads_q.items(), theirs)}


def _finish_reduction(parts, ids):
    names = list(parts)
    fulls = [_chip_sum("chip_sum_" + k, g, t, got, ids) for k, ((g, t), got) in parts.items()]
    return dict(zip(names, _join_halves(fulls)))


def kernel(x, norm_mix_g, w_in, f_bias, sg_ln_g, sg_w, sg_b, w_out, norm_ffn_g, w_up, w_conv, b_conv, w_down, norm_final_g, loss_target, m_norm_mix_g, m_w_in, m_f_bias, m_sg_ln_g, m_sg_w, m_sg_b, m_w_out, m_norm_ffn_g, m_w_up, m_w_conv, m_b_conv, m_w_down, m_norm_final_g, v_norm_mix_g, v_w_in, v_f_bias, v_sg_ln_g, v_sg_w, v_sg_b, v_w_out, v_norm_ffn_g, v_w_up, v_w_conv, v_b_conv, v_w_down, v_norm_final_g):
    args = dict(locals())
    quarter = 2 * lax.axis_index("x") + lax.axis_index("y")
    ids = jnp.stack([quarter, lax.axis_index("c")]).astype(jnp.int32)
    wq_conv = w_conv.shape[-1]

    g_in = _gather_quarters([w_in[0].astype(BF16)])[0]
    w_in_full = jnp.pad(jnp.concatenate([g_in[q] for q in range(4)], axis=1), ((0, 0), (0, D_IN_PAD - D_IN)))
    late_shards = [w_out[0].astype(BF16), w_up[0].astype(BF16), w_down[0].astype(BF16), w_conv[0]]

    sq_err, dx, grads, parts = _local_step(
        x[0], loss_target[0], norm_mix_g, w_in_full, f_bias[0], sg_ln_g[0], sg_w[0], sg_b[0], norm_ffn_g, b_conv[0],
        norm_final_g.reshape(1, D_MODEL), late_shards, ids)
    big = _finish_reduction(parts, ids)

    out = {"grad_x": dx[None]}
    for k in ("w_in", "w_out", "w_up", "w_down"):
        g, d, nm, nv = _adamw("adamw_" + k, args[k][0], big[k], args["m_" + k][0], args["v_" + k][0])
        out["grad_" + k], out["delta_" + k], out["new_m_" + k], out["new_v_" + k] = g[None], d[None], nm[None], nv[None]

    small_names = [n for n, _ in _SMALL]
    shapes = {n: (3, 4 * wq_conv) if n == "w_conv" else args[n].shape for n in small_names}
    shapes["sq_err"] = sq_err.shape
    g_small = _unpack_small(_small_allreduce(_pack_small({**{n: grads[n] for n in small_names}, "sq_err": sq_err})), shapes)
    out["loss"] = 0.5 * jnp.sum(g_small.pop("sq_err")) / D_MODEL
    g_small["w_conv"] = lax.dynamic_slice(g_small["w_conv"], (0, quarter * wq_conv), (3, wq_conv))[None]
    flat2d = lambda t: t.reshape(-1, t.shape[-1])
    updated = _adamw_small(*[[flat2d(src[p + n]) for n in small_names] for src, p in
                             ((args, ""), (g_small, ""), (args, "m_"), (args, "v_"))])
    for n, g in g_small.items():
        out["grad_" + n] = g
    for prefix, arrs in zip(("delta_", "new_m_", "new_v_"), updated):
        for n, t in zip(small_names, arrs):
            out[prefix + n] = t.reshape(args[n].shape)

    weights = ["norm_mix_g", "w_in", "f_bias", "sg_ln_g", "sg_w", "sg_b", "w_out", "norm_ffn_g", "w_up", "w_conv", "b_conv",
               "w_down", "norm_final_g"]
    return (out["loss"], out["grad_x"], *[out[p + n] for p in ("grad_", "delta_", "new_m_", "new_v_") for n in weights])
```

```python
import functools
import math

import jax
import jax.numpy as jnp
from jax import lax
from jax.experimental import pallas as pl
from jax.experimental.pallas import tpu as pltpu

F32 = jnp.float32
BF16 = jnp.bfloat16
MESH = pl.DeviceIdType.MESH

D_MODEL = 1024
N_HEADS = 8
HEAD_DIM = 64
D_HEADS = N_HEADS * HEAD_DIM
SG_BLOCK = 128
CHUNK = 64
D_FF = 2816
D_IN = 2 * D_HEADS + 3 * D_HEADS + N_HEADS
LANES = 128
SUBLANES = 8
D_IN_PAD = 5 * D_HEADS + LANES
EPS = 1e-6
SCALE = HEAD_DIM ** -0.5
NEG = -1e30
LOG2E = 1.4426950408889634
HEAD_PAD = LANES
D_PAD = N_HEADS * HEAD_PAD
Q_STAT = HEAD_DIM
K_STAT = HEAD_DIM + 3
L_STAT = HEAD_DIM + 6
GROUPS = 2
GROUP_HEADS = N_HEADS // GROUPS
GROUP_PAD = GROUP_HEADS * HEAD_PAD
KEY_CHUNK = 256
FWD_KEY_CHUNK = 512
STAT_ROWS = 16
FF_CHUNK = 256

ADAM_LR = 0.001
ADAM_B1 = 0.9
ADAM_B2 = 0.999
ADAM_EPS = 1e-08
ADAM_WD = 0.01
ADAM_STEP = 10

VMEM_LIMIT = 56 * 1024 * 1024

NT = (((1,), (1,)), ((), ()))
TN = (((0,), (0,)), ((), ()))


def _params(sem):
    return pltpu.CompilerParams(dimension_semantics=sem, vmem_limit_bytes=VMEM_LIMIT)


def _full(shape):
    nd = len(shape)
    return pl.BlockSpec(shape, lambda *_: (0,) * nd)


def _row_tile(rows, target):
    best = None
    for t in range(SUBLANES, min(rows, target) + 1, SUBLANES):
        if rows % t == 0:
            best = t
    assert best is not None, rows
    return best


def _sigmoid(x):
    return 0.5 * jnp.tanh(0.5 * x) + 0.5


def _gelu(z):
    return 0.5 * z * (1.0 + lax.erf(z * (2.0 ** -0.5)))


def _gelu_grad(z):
    cdf = 0.5 * (1.0 + lax.erf(z * (2.0 ** -0.5)))
    pdf = jnp.exp(-0.5 * z * z) * (1.0 / math.sqrt(2.0 * math.pi))
    return cdf + z * pdf


def _split_dot(x, m):
    hi = x.astype(BF16)
    lo = (x - hi.astype(F32)).astype(BF16)
    return jnp.dot(hi, m, preferred_element_type=F32) + jnp.dot(lo, m, preferred_element_type=F32)


def _head_mask(h, rows):
    lane = lax.broadcasted_iota(jnp.int32, (rows, D_HEADS), 1)
    return (lane >= h * HEAD_DIM) & (lane < (h + 1) * HEAD_DIM)


def _rms_bwd(dh, x, g):
    r = lax.rsqrt(jnp.mean(x * x, axis=-1, keepdims=True) + EPS)
    xhat = x * r
    dg = jnp.sum(dh * xhat, axis=0, keepdims=True)
    dxhat = dh * g
    dx = r * (dxhat - xhat * jnp.mean(dxhat * xhat, axis=-1, keepdims=True))
    return dx, dg


def _in_proj(x, g1, w_in, tm):
    S = x.shape[0]
    nz = D_IN_PAD - LANES

    def body(x_ref, g_ref, w_ref, z_ref, f_ref, h_ref):
        xf = x_ref[...]
        r = lax.rsqrt(jnp.mean(xf * xf, axis=-1, keepdims=True) + EPS)
        h = (xf * r * g_ref[...]).astype(BF16)
        h_ref[...] = h
        zz = jnp.dot(h, w_ref[...], preferred_element_type=F32)
        z_ref[...] = zz[:, :nz].astype(BF16)
        f_ref[...] = zz[:, nz:]

    return pl.pallas_call(
        body, name="in_proj", grid=(S // tm,),
        in_specs=[pl.BlockSpec((tm, D_MODEL), lambda i: (i, 0)), _full((1, D_MODEL)), _full((D_MODEL, D_IN_PAD))],
        out_specs=[pl.BlockSpec((tm, nz), lambda i: (i, 0)), pl.BlockSpec((tm, LANES), lambda i: (i, 0)),
                   pl.BlockSpec((tm, D_MODEL), lambda i: (i, 0))],
        out_shape=[jax.ShapeDtypeStruct((S, nz), BF16), jax.ShapeDtypeStruct((S, LANES), F32),
                   jax.ShapeDtypeStruct((S, D_MODEL), BF16)],
        compiler_params=_params(("parallel",)),
    )(x, g1, w_in)


def _fox_prep(f, bias_row, tb):
    S = f.shape[0]

    def body(f_ref, b_ref, c_ref, carry):
        @pl.when(pl.program_id(0) == 0)
        def _():
            carry[...] = jnp.zeros_like(carry)

        xv = f_ref[...] + b_ref[...]
        lf = jnp.minimum(xv, 0.0) - jnp.log(1.0 + jnp.exp(-jnp.abs(xv)))
        r = lax.broadcasted_iota(jnp.int32, (tb, tb), 0)
        s = lax.broadcasted_iota(jnp.int32, (tb, tb), 1)
        tri = (r >= s).astype(F32)
        cs = jnp.dot(tri, lf, precision=lax.Precision.HIGHEST, preferred_element_type=F32) + carry[0:1, :]
        c_ref[...] = cs
        carry[...] = jnp.broadcast_to(cs[tb - 1:tb, :], carry.shape)

    return pl.pallas_call(
        body, name="fox_prep", grid=(S // tb,),
        in_specs=[pl.BlockSpec((tb, LANES), lambda i: (i, 0)), _full((1, LANES))],
        out_specs=pl.BlockSpec((tb, LANES), lambda i: (i, 0)),
        out_shape=jax.ShapeDtypeStruct((S, LANES), F32),
        scratch_shapes=[pltpu.VMEM((SUBLANES, LANES), F32)],
        compiler_params=_params(("arbitrary",)),
    )(f, bias_row)


def _attn_consts():
    col = jnp.arange(D_PAD)
    row = jnp.arange(D_HEADS)
    head = jnp.arange(LANES)
    place = (row[:, None] // HEAD_DIM == col[None, :] // HEAD_PAD) & (row[:, None] % HEAD_DIM == col[None, :] % HEAD_PAD)

    def stat(offset):
        return ((head[:, None] < N_HEADS) & (col[None, :] == head[:, None] * HEAD_PAD + offset)).astype(BF16)

    def stat3(base):
        part, h = head // N_HEADS, head % N_HEADS
        return ((part[:, None] < 3) & (col[None, :] == h[:, None] * HEAD_PAD + base + part[:, None])).astype(BF16)

    def ones(offsets):
        return sum((col % HEAD_PAD == o) for o in offsets).astype(F32).reshape(1, D_PAD)

    place = place.astype(BF16)
    return {
        "place": place, "place_t": place.T, "place_t_group": place.T[:GROUP_PAD, :GROUP_HEADS * HEAD_DIM],
        "q_stat": stat3(Q_STAT), "k_stat": stat3(K_STAT),
        "d_stat": stat3(Q_STAT) * (head[:, None] < 2 * N_HEADS).astype(BF16),
        "l_stat": jnp.concatenate([stat(L_STAT + j)[:STAT_ROWS] for j in range(3)], axis=0),
        "q_ones": ones(range(K_STAT, K_STAT + 3)), "k_ones": ones(list(range(Q_STAT, Q_STAT + 3)) + list(range(L_STAT, L_STAT + 3))),
        "v_ones": ones(range(Q_STAT, Q_STAT + 2)),
    }


def _split3(x):
    hi = x.astype(BF16)
    r = x - hi.astype(F32)
    mid = r.astype(BF16)
    return hi, mid, (r - mid.astype(F32)).astype(BF16)


def _attn_pack(z, c, k, tm):
    S = z.shape[0]

    def body(q_ref, k_ref, v_ref, c_ref, pl_ref, pt_ref, qs_ref, ks_ref, qo_ref, ko_ref, vo_ref, voc_ref,
             qa_ref, ka_ref, va_ref, vt_ref):
        place = pl_ref[...]
        q = (q_ref[...].astype(F32) * (SCALE * LOG2E)).astype(BF16)
        qa = jnp.dot(q, place, preferred_element_type=F32) + qo_ref[...]
        ka = jnp.dot(k_ref[...], place, preferred_element_type=F32) + ko_ref[...]
        lane = lax.broadcasted_iota(jnp.int32, (tm, LANES), 1)
        hi, mid, lo = _split3(jnp.where(lane < N_HEADS, c_ref[...] * LOG2E, 0.0))
        parts = hi.astype(F32) + pltpu.roll(mid.astype(F32), N_HEADS, 1) + pltpu.roll(lo.astype(F32), 2 * N_HEADS, 1)
        parts = parts.astype(BF16)
        qa = qa + jnp.dot(parts, qs_ref[...], preferred_element_type=F32)
        ka = ka - jnp.dot(parts, ks_ref[...], preferred_element_type=F32)
        qa_ref[...] = qa.astype(BF16)
        ka_ref[...] = ka.astype(BF16)
        v = v_ref[...]
        va_ref[...] = (jnp.dot(v, place, preferred_element_type=F32) + vo_ref[...]).astype(BF16)
        vt_ref[...] = (lax.dot_general(pt_ref[...], v, NT, preferred_element_type=F32) + voc_ref[...]).astype(BF16)

    blk = lambda col: pl.BlockSpec((tm, D_HEADS), lambda i: (i, col))
    out = pl.BlockSpec((tm, D_PAD), lambda i: (i, 0))
    pad = jax.ShapeDtypeStruct((S, D_PAD), BF16)
    return pl.pallas_call(
        body, name="attn_pack", grid=(S // tm,),
        in_specs=[blk(2), blk(3), blk(4), pl.BlockSpec((tm, LANES), lambda i: (i, 0)), _full((D_HEADS, D_PAD)), _full((D_PAD, D_HEADS)),
                  _full((LANES, D_PAD)), _full((LANES, D_PAD)), _full((1, D_PAD)), _full((1, D_PAD)), _full((1, D_PAD)),
                  _full((D_PAD, 1))],
        out_specs=[out, out, out, pl.BlockSpec((None, D_PAD, tm), lambda i: (i, 0, 0))],
        out_shape=[pad, pad, pad, jax.ShapeDtypeStruct((S // tm, D_PAD, tm), BF16)],
        compiler_params=_params(("parallel",)),
    )(z, z, z, c, k["place"], k["place_t"], k["q_stat"], k["k_stat"], k["q_ones"], k["k_ones"], k["v_ones"], k["v_ones"].T)


def _attn_fwd(qa, ka, vat, place_t, tq, shards):
    S = qa.shape[0]
    n = S // tq
    ns = len(shards)
    hand_on_at = (2 * n) // 3

    pairs = [(q, k) for q in range(n) for k in range(q + 1)]
    q_of = jnp.asarray([q for q, _ in pairs], jnp.int32)
    k_of = jnp.asarray([k for _, k in pairs], jnp.int32)

    def body(q_of_ref, k_of_ref, q_ref, k_ref, vt_ref, pt_ref, *rest):
        o_ref, lse_ref = rest[ns:ns + 2]
        m_s, acc_s, ot_s = rest[2 * ns + 2:2 * ns + 5]
        s_s = rest[2 * ns + 5:2 * ns + 7]
        start, hand_on, finish = _gather_ops(rest[:ns], rest[ns + 2:2 * ns + 2], *rest[2 * ns + 7:])
        qi, ki = q_of_ref[pl.program_id(0)], k_of_ref[pl.program_id(0)]

        @pl.when((qi == 0) & (ki == 0))
        def _():
            start()

        @pl.when((qi == hand_on_at) & (ki == 0))
        def _():
            hand_on()

        @pl.when(ki == 0)
        def _():
            m_s[...] = jnp.full_like(m_s, NEG)
            acc_s[...] = jnp.zeros_like(acc_s)

        def step(diagonal):
            kc = FWD_KEY_CHUNK
            chunks = [slice(c * kc, (c + 1) * kc) for c in range(tq // kc)]

            def scores(h, rows, slot):
                sl = slice(h * HEAD_PAD, (h + 1) * HEAD_PAD)
                st = lax.dot_general(k_ref[rows, sl], q_ref[:, sl], NT, preferred_element_type=F32)
                if diagonal:
                    key = rows.start + lax.broadcasted_iota(jnp.int32, (kc, tq), 0)
                    query = lax.broadcasted_iota(jnp.int32, (kc, tq), 1)
                    st = jnp.where(query >= key, st, NEG)
                s_s[slot][rows, :] = st
                return jnp.max(st, axis=0, keepdims=True)

            m_cur = functools.reduce(jnp.maximum, [scores(0, rows, 0) for rows in chunks])
            for h in range(N_HEADS):
                sl = slice(h * HEAD_PAD, (h + 1) * HEAD_PAD)
                slot = h % 2
                m_prev = m_s[h][0:1, :]
                m_new = jnp.maximum(m_prev, m_cur)
                acc = jnp.exp2(m_prev - m_new) * acc_s[h]
                m_next = []
                for rows in chunks:
                    if h + 1 < N_HEADS:
                        m_next.append(scores(h + 1, rows, 1 - slot))
                    pt = jnp.exp2(s_s[slot][rows, :] - m_new).astype(BF16)
                    acc = acc + jnp.dot(vt_ref[sl, rows], pt, preferred_element_type=F32)
                acc_s[h] = acc
                m_s[h] = jnp.broadcast_to(m_new, (SUBLANES, tq))
                if m_next:
                    m_cur = functools.reduce(jnp.maximum, m_next)

        @pl.when(ki < qi)
        def _():
            step(False)

        @pl.when(ki == qi)
        def _():
            step(True)
            lse_ref[...] = jnp.zeros_like(lse_ref)
            for h in range(N_HEADS):
                acc = acc_s[h]
                denom = acc[Q_STAT:Q_STAT + 1, :]
                ot_s[h * HEAD_PAD:(h + 1) * HEAD_PAD, :] = (acc / denom).astype(BF16)
                lse_ref[h:h + 1, :] = m_s[h][0:1, :] + jnp.log(denom) * LOG2E
            o_ref[...] = lax.dot_general(ot_s[...], pt_ref[...], TN, preferred_element_type=F32).astype(BF16)

        @pl.when((qi == n - 1) & (ki == n - 1))
        def _():
            finish()

    out = pl.pallas_call(
        body, name="attn_fwd",
        grid_spec=pltpu.PrefetchScalarGridSpec(
            num_scalar_prefetch=2, grid=(len(pairs),),
            in_specs=[pl.BlockSpec((tq, D_PAD), lambda i, qs, ks: (qs[i], 0)),
                      pl.BlockSpec((tq, D_PAD), lambda i, qs, ks: (ks[i], 0)),
                      pl.BlockSpec((None, D_PAD, tq), lambda i, qs, ks: (ks[i], 0, 0)),
                      pl.BlockSpec((D_PAD, D_HEADS), lambda i, qs, ks: (0, 0))]
            + [_ANY] * ns,
            out_specs=[pl.BlockSpec((tq, D_HEADS), lambda i, qs, ks: (qs[i], 0)),
                       pl.BlockSpec((STAT_ROWS, tq), lambda i, qs, ks: (0, qs[i]))] + [_ANY] * ns,
            scratch_shapes=[pltpu.VMEM((N_HEADS, SUBLANES, tq), F32), pltpu.VMEM((N_HEADS, HEAD_PAD, tq), F32),
                            pltpu.VMEM((D_PAD, tq), BF16), pltpu.VMEM((tq, tq), F32), pltpu.VMEM((tq, tq), F32)] + _gather_sems(ns)),
        out_shape=[jax.ShapeDtypeStruct((S, D_HEADS), BF16), jax.ShapeDtypeStruct((STAT_ROWS, S), F32)] + _gather_shapes(shards),
        compiler_params=_params(("arbitrary",)),
    )(q_of, k_of, qa, ka, vat, place_t, *shards)
    return out[0], out[1], out[2:]


def _layer_norm_heads(v, seg_avg):
    mu = _split_dot(v, seg_avg)
    d = v - mu
    var = _split_dot(d * d, seg_avg)
    rstd = lax.rsqrt(var + EPS)
    return d * rstd, rstd


def _gate_mix(vn_blk, w_ref, bias):
    acc = bias
    for h in range(N_HEADS):
        vh = jnp.where(_head_mask(h, SG_BLOCK), vn_blk, 0.0).astype(BF16)
        acc = acc + jnp.dot(w_ref[h], vh, preferred_element_type=F32)
    return acc


def _gate_fwd(z, w_mask, ln_row, b_full, seg_avg, tm):
    S = z.shape[0]

    def body(zu_ref, zv_ref, w_ref, ln_ref, b_ref, avg_ref, o_ref):
        u = _gelu(zu_ref[...].astype(F32))
        v = _gelu(zv_ref[...].astype(F32))
        vhat, _ = _layer_norm_heads(v, avg_ref[...])
        vn = vhat * ln_ref[...]
        for b in range(tm // SG_BLOCK):
            rows = slice(b * SG_BLOCK, (b + 1) * SG_BLOCK)
            mixed = _gate_mix(vn[rows], w_ref, b_ref[...])
            o_ref[rows, :] = (u[rows] * mixed).astype(BF16)

    return pl.pallas_call(
        body, name="gate_fwd", grid=(S // tm,),
        in_specs=[pl.BlockSpec((tm, D_HEADS), lambda i: (i, 0)), pl.BlockSpec((tm, D_HEADS), lambda i: (i, 1)),
                  _full((N_HEADS, SG_BLOCK, SG_BLOCK)), _full((1, D_HEADS)), _full((SG_BLOCK, D_HEADS)),
                  _full((D_HEADS, D_HEADS))],
        out_specs=pl.BlockSpec((tm, D_HEADS), lambda i: (i, 0)),
        out_shape=jax.ShapeDtypeStruct((S, D_HEADS), BF16),
        compiler_params=_params(("parallel",)),
    )(z, z, w_mask, ln_row, b_full, seg_avg)


def _mix_out(x, out_a, out_b, w_out, g2, tm):
    S = x.shape[0]

    def body(x_ref, a_ref, b_ref, w_ref, g_ref, x1_ref, h_ref):
        y = jnp.dot(a_ref[...], w_ref[:D_HEADS, :], preferred_element_type=F32)
        y = y + jnp.dot(b_ref[...], w_ref[D_HEADS:, :], preferred_element_type=F32)
        x1 = x_ref[...] + y
        x1_ref[...] = x1
        r = lax.rsqrt(jnp.mean(x1 * x1, axis=-1, keepdims=True) + EPS)
        h_ref[...] = (x1 * r * g_ref[...]).astype(BF16)

    row = lambda w: pl.BlockSpec((tm, w), lambda i: (i, 0))
    return pl.pallas_call(
        body, name="mix_out", grid=(S // tm,),
        in_specs=[row(D_MODEL), row(D_HEADS), row(D_HEADS), _full((D_MODEL, D_MODEL)), _full((1, D_MODEL))],
        out_specs=[row(D_MODEL), row(D_MODEL)],
        out_shape=[jax.ShapeDtypeStruct((S, D_MODEL), F32), jax.ShapeDtypeStruct((S, D_MODEL), BF16)],
        compiler_params=_params(("parallel",)),
    )(x, out_a, out_b, w_out, g2)


def _up_proj(h2, w_up_q, tm):
    S = h2.shape[0]
    nq, _, wq = w_up_q.shape

    def body(h_ref, w_ref, a_ref):
        a_ref[...] = jnp.dot(h_ref[...], w_ref[...], preferred_element_type=F32).astype(BF16)

    return pl.pallas_call(
        body, name="up_proj", grid=(nq, S // tm),
        in_specs=[pl.BlockSpec((tm, D_MODEL), lambda j, i: (i, 0)), pl.BlockSpec((None, D_MODEL, wq), lambda j, i: (j, 0, 0))],
        out_specs=pl.BlockSpec((tm, wq), lambda j, i: (i, j)),
        out_shape=jax.ShapeDtypeStruct((S, nq * wq), BF16),
        compiler_params=_params(("parallel", "parallel")),
    )(h2, w_up_q)


def _shift_down(a, halo, k):
    tm = a.shape[0]
    ra = pltpu.roll(a, k, 0)
    rh = pltpu.roll(halo, k, 0)
    row = lax.broadcasted_iota(jnp.int32, halo.shape, 0)
    top = jnp.where(row < k, rh, ra[0:SUBLANES])
    return jnp.concatenate([top, ra[SUBLANES:tm]], axis=0)


def _shift_up(a, halo, k):
    tm = a.shape[0]
    ra = pltpu.roll(a, tm - k, 0)
    rh = pltpu.roll(halo, SUBLANES - k, 0)
    row = lax.broadcasted_iota(jnp.int32, halo.shape, 0)
    bottom = jnp.where(row >= SUBLANES - k, rh, ra[tm - SUBLANES:tm])
    return jnp.concatenate([ra[0:tm - SUBLANES], bottom], axis=0)


def _shift_matrices(tm):
    row = lax.broadcasted_iota(jnp.int32, (tm, tm), 0)
    col = lax.broadcasted_iota(jnp.int32, (tm, tm), 1)
    return [(row == col + k).astype(BF16) for k in (1, 2)]


def _conv_taps(a, halo, first, shifts):
    tm = a.shape[0]
    halo = halo.astype(F32) * jnp.where(first, 0.0, 1.0)
    if shifts is None:
        a = a.astype(F32)
        return a, _shift_down(a, halo, 1), _shift_down(a, halo, 2)
    row8 = lax.broadcasted_iota(jnp.int32, halo.shape, 0)
    taps = [a.astype(F32)]
    for k, shift in zip((1, 2), shifts):
        down = jnp.dot(shift, a, preferred_element_type=F32)
        top = down[0:SUBLANES] + jnp.where(row8 < k, pltpu.roll(halo, k, 0), 0.0)
        taps.append(jnp.concatenate([top, down[SUBLANES:tm]], axis=0))
    return taps


def _conv_gate_val(refs, shifts, cols, first):
    ag_ref, av_ref, hg_ref, hv_ref, wg_ref, wv_ref, bg_ref, bv_ref = refs
    g0, g1, g2 = _conv_taps(ag_ref[:, cols], hg_ref[:, cols], first, shifts)
    gate = wg_ref[2:3, cols] * g0 + wg_ref[1:2, cols] * g1 + wg_ref[0:1, cols] * g2 + bg_ref[:, cols]
    v0, v1, v2 = _conv_taps(av_ref[:, cols], hv_ref[:, cols], first, shifts)
    val = wv_ref[2:3, cols] * v0 + wv_ref[1:2, cols] * v1 + wv_ref[0:1, cols] * v2 + bv_ref[:, cols]
    return gate, val, (g2, g1, g0), (v2, v1, v0)


_FF_CHUNKS = [slice(j * FF_CHUNK, (j + 1) * FF_CHUNK) for j in range(D_FF // FF_CHUNK)]


def _conv_specs(tm):
    step = tm // SUBLANES
    prev = lambda i: jnp.maximum(i * step - 1, 0)
    return [pl.BlockSpec((tm, D_FF), lambda i: (i, 0)), pl.BlockSpec((tm, D_FF), lambda i: (i, 1)),
            pl.BlockSpec((SUBLANES, D_FF), lambda i: (prev(i), 0)), pl.BlockSpec((SUBLANES, D_FF), lambda i: (prev(i), 1))]


def _ffn_fwd_loss(a, w_conv, b_conv, w_down, x1, g3, target, tm):
    S = x1.shape[0]

    def body(ag_ref, av_ref, hg_ref, hv_ref, wg_ref, wv_ref, bg_ref, bv_ref, wd_ref, x1_ref, g_ref, t_ref,
             dx2_ref, loss_ref, dg_ref):
        i = pl.program_id(0)

        @pl.when(i == 0)
        def _():
            loss_ref[...] = jnp.zeros_like(loss_ref)
            dg_ref[...] = jnp.zeros_like(dg_ref)

        x2 = x1_ref[...]
        for cols in _FF_CHUNKS:
            gate, val, _, _ = _conv_gate_val((ag_ref, av_ref, hg_ref, hv_ref, wg_ref, wv_ref, bg_ref, bv_ref), None, cols, i == 0)
            half = 0.5 * gate
            y = ((half + half * jnp.tanh(half)) * val).astype(BF16)
            x2 = x2 + jnp.dot(y, wd_ref[cols, :], preferred_element_type=F32)
        r = lax.rsqrt(jnp.mean(x2 * x2, axis=-1, keepdims=True) + EPS)
        xhat = x2 * r
        gg = g_ref[...]
        err = xhat * gg - t_ref[...]
        loss_ref[...] += jnp.sum(err * err, axis=0, keepdims=True)
        dy = err * (1.0 / D_MODEL)
        dg_ref[...] += jnp.sum(dy * xhat, axis=0, keepdims=True)
        dxhat = dy * gg
        dx2_ref[...] = r * (dxhat - xhat * jnp.mean(dxhat * xhat, axis=-1, keepdims=True))

    row = lambda w: pl.BlockSpec((tm, w), lambda i: (i, 0))
    half = lambda r: [pl.BlockSpec((r, D_FF), lambda i: (0, 0)), pl.BlockSpec((r, D_FF), lambda i: (0, 1))]
    return pl.pallas_call(
        body, name="ffn_fwd_loss", grid=(S // tm,),
        in_specs=_conv_specs(tm) + half(3) + half(1) + [_full((D_FF, D_MODEL)), row(D_MODEL), _full((1, D_MODEL)), row(D_MODEL)],
        out_specs=[row(D_MODEL), _full((1, D_MODEL)), _full((1, D_MODEL))],
        out_shape=[jax.ShapeDtypeStruct((S, D_MODEL), F32), jax.ShapeDtypeStruct((1, D_MODEL), F32),
                   jax.ShapeDtypeStruct((1, D_MODEL), F32)],
        compiler_params=_params(("arbitrary",)),
    )(a, a, a, a, w_conv, w_conv, b_conv, b_conv, w_down, x1, g3, target)


def _ffn_bwd_gate(dx2, a, w_conv, b_conv, w_down, tm):
    S = dx2.shape[0]

    def body(dx_ref, ag_ref, av_ref, hg_ref, hv_ref, wg_ref, wv_ref, bg_ref, bv_ref, wd_ref,
             dc_ref, y_ref, dw_ref, db_ref):
        i = pl.program_id(0)

        @pl.when(i == 0)
        def _():
            dw_ref[...] = jnp.zeros_like(dw_ref)
            db_ref[...] = jnp.zeros_like(db_ref)

        dx = dx_ref[...].astype(BF16)
        shifts = _shift_matrices(tm)
        for cols in _FF_CHUNKS:
            gate, val, gtaps, vtaps = _conv_gate_val((ag_ref, av_ref, hg_ref, hv_ref, wg_ref, wv_ref, bg_ref, bv_ref), shifts, cols, i == 0)
            sg = _sigmoid(gate)
            act = gate * sg
            y_ref[:, cols] = (act * val).astype(BF16)
            dy = lax.dot_general(dx, wd_ref[cols, :], NT, preferred_element_type=F32)
            dgate = dy * val * (sg + act - act * sg)
            dval = dy * act
            for d, taps, out in ((dgate, gtaps, cols), (dval, vtaps, slice(D_FF + cols.start, D_FF + cols.stop))):
                dc_ref[:, out] = d.astype(BF16)
                db_ref[0:1, out] += jnp.sum(d, axis=0, keepdims=True)
                for j in range(3):
                    dw_ref[j:j + 1, out] += jnp.sum(d * taps[j], axis=0, keepdims=True)

    row = lambda w: pl.BlockSpec((tm, w), lambda i: (i, 0))
    half = lambda r: [pl.BlockSpec((r, D_FF), lambda i: (0, 0)), pl.BlockSpec((r, D_FF), lambda i: (0, 1))]
    return pl.pallas_call(
        body, name="ffn_bwd_gate", grid=(S // tm,),
        in_specs=[row(D_MODEL)] + _conv_specs(tm) + half(3) + half(1) + [_full((D_FF, D_MODEL))],
        out_specs=[row(2 * D_FF), row(D_FF), _full((SUBLANES, 2 * D_FF)), _full((1, 2 * D_FF))],
        out_shape=[jax.ShapeDtypeStruct((S, 2 * D_FF), BF16), jax.ShapeDtypeStruct((S, D_FF), BF16),
                   jax.ShapeDtypeStruct((SUBLANES, 2 * D_FF), F32), jax.ShapeDtypeStruct((1, 2 * D_FF), F32)],
        compiler_params=_params(("arbitrary",)),
    )(dx2, a, a, a, a, w_conv, w_conv, b_conv, b_conv, w_down)


def _conv_bwd(dc, w_conv, tm, tn):
    S, C = dc.shape
    step = tm // SUBLANES
    last_blk = S // SUBLANES - 1

    def body(d_ref, nx_ref, w_ref, o_ref):
        last = pl.program_id(0) == pl.num_programs(0) - 1
        row = lax.broadcasted_iota(jnp.int32, (tm, tm), 0)
        col = lax.broadcasted_iota(jnp.int32, (tm, tm), 1)
        row8 = lax.broadcasted_iota(jnp.int32, (SUBLANES, FF_CHUNK), 0)
        ups = [(row + k == col).astype(BF16) for k in (1, 2)]
        for c0 in range(0, tn, FF_CHUNK):
            cols = slice(c0, c0 + FF_CHUNK)
            d = d_ref[:, cols]
            nx = nx_ref[:, cols].astype(F32) * jnp.where(last, 0.0, 1.0)
            out = w_ref[2:3, cols] * d.astype(F32)
            for k, up in zip((1, 2), ups):
                moved = jnp.dot(up, d, preferred_element_type=F32)
                bottom = moved[tm - SUBLANES:tm] + jnp.where(row8 >= SUBLANES - k, pltpu.roll(nx, SUBLANES - k, 0), 0.0)
                out = out + w_ref[2 - k:3 - k, cols] * jnp.concatenate([moved[0:tm - SUBLANES], bottom], axis=0)
            o_ref[:, cols] = out.astype(BF16)

    return pl.pallas_call(
        body, name="conv_bwd", grid=(S // tm, C // tn),
        in_specs=[pl.BlockSpec((tm, tn), lambda i, j: (i, j)),
                  pl.BlockSpec((SUBLANES, tn), lambda i, j: (jnp.minimum((i + 1) * step, last_blk), j)),
                  pl.BlockSpec((3, tn), lambda i, j: (0, j))],
        out_specs=pl.BlockSpec((tm, tn), lambda i, j: (i, j)),
        out_shape=jax.ShapeDtypeStruct((S, C), BF16),
        compiler_params=_params(("parallel", "parallel")),
    )(dc, dc, w_conv)


def _matmul_tn(a, b, name, bm, bn, tk, col_a=0, col_b=0, quarters=None):
    S = a.shape[0]
    gm, gn = quarters if quarters else (1, 1)
    nk = S // tk

    def body(a_ref, b_ref, o_ref):
        @pl.when(pl.program_id(2) == 0)
        def _():
            o_ref[...] = jnp.zeros_like(o_ref)

        o_ref[...] += lax.dot_general(a_ref[...].astype(BF16), b_ref[...].astype(BF16), TN, preferred_element_type=F32)

    if quarters and gn > 1:
        out_spec = pl.BlockSpec((None, bm, bn), lambda i, j, k: (j, i, 0))
        out_shape = jax.ShapeDtypeStruct((gn, gm * bm, bn), F32)
    else:
        out_spec = pl.BlockSpec((bm, bn), lambda i, j, k: (i, j))
        out_shape = jax.ShapeDtypeStruct((gm * bm, gn * bn), F32)
    return pl.pallas_call(
        body, name=name, grid=(gm, gn, nk),
        in_specs=[pl.BlockSpec((tk, bm), lambda i, j, k: (k, col_a * gm + i)),
                  pl.BlockSpec((tk, bn), lambda i, j, k: (k, col_b * gn + j))],
        out_specs=out_spec, out_shape=out_shape,
        compiler_params=_params(("parallel", "parallel", "arbitrary")),
    )(a, b)


def _dw_out(out_a, out_b, dx1, tk):
    S = dx1.shape[0]

    def body(a_ref, b_ref, d_ref, o_ref):
        @pl.when(pl.program_id(0) == 0)
        def _():
            o_ref[...] = jnp.zeros_like(o_ref)

        d = d_ref[...].astype(BF16)
        o_ref[:D_HEADS, :] += lax.dot_general(a_ref[...], d, TN, preferred_element_type=F32)
        o_ref[D_HEADS:, :] += lax.dot_general(b_ref[...], d, TN, preferred_element_type=F32)

    row = lambda w: pl.BlockSpec((tk, w), lambda k: (k, 0))
    return pl.pallas_call(
        body, name="dw_out", grid=(S // tk,),
        in_specs=[row(D_HEADS), row(D_HEADS), row(D_MODEL)], out_specs=_full((D_MODEL, D_MODEL)),
        out_shape=jax.ShapeDtypeStruct((D_MODEL, D_MODEL), F32),
        compiler_params=_params(("arbitrary",)),
    )(out_a, out_b, dx1)


def _up_bwd(dact, w_up_q, x1, g2, dx2, tm):
    S = x1.shape[0]
    nq, _, wq = w_up_q.shape

    def body(d_ref, w_ref, x_ref, g_ref, dx2_ref, dx1_ref, dg_ref):
        @pl.when(pl.program_id(0) == 0)
        def _():
            dg_ref[...] = jnp.zeros_like(dg_ref)

        dh = jnp.zeros((tm, D_MODEL), F32)
        for j in range(nq):
            dh = dh + lax.dot_general(d_ref[:, j * wq:(j + 1) * wq], w_ref[j], NT, preferred_element_type=F32)
        dx, dg = _rms_bwd(dh, x_ref[...], g_ref[...])
        dg_ref[...] += dg
        dx1_ref[...] = dx2_ref[...] + dx

    row = lambda w: pl.BlockSpec((tm, w), lambda i: (i, 0))
    return pl.pallas_call(
        body, name="up_bwd", grid=(S // tm,),
        in_specs=[row(nq * wq), pl.BlockSpec((nq, D_MODEL, wq), lambda i: (0, 0, 0), pipeline_mode=pl.Buffered(1)),
                  row(D_MODEL), _full((1, D_MODEL)), row(D_MODEL)],
        out_specs=[row(D_MODEL), _full((1, D_MODEL))],
        out_shape=[jax.ShapeDtypeStruct((S, D_MODEL), F32), jax.ShapeDtypeStruct((1, D_MODEL), F32)],
        compiler_params=_params(("arbitrary",)),
    )(dact, w_up_q, x1, g2, dx2)


def _out_bwd(dx1, w_out, tm):
    S = dx1.shape[0]

    def body(d_ref, w_ref, o_ref):
        o_ref[...] = lax.dot_general(d_ref[...].astype(BF16), w_ref[...], NT, preferred_element_type=F32).astype(BF16)

    return pl.pallas_call(
        body, name="out_bwd", grid=(S // tm,),
        in_specs=[pl.BlockSpec((tm, D_MODEL), lambda i: (i, 0)), _full((D_MODEL, D_MODEL))],
        out_specs=pl.BlockSpec((tm, D_MODEL), lambda i: (i, 0)),
        out_shape=jax.ShapeDtypeStruct((S, D_MODEL), BF16),
        compiler_params=_params(("parallel",)),
    )(dx1, w_out)


def _gate_bwd(z, dcat, w_mask, w_mask_t, ln_row, b_full, seg_avg, head_ind, tm, swap):
    S = z.shape[0]
    nb = tm // SG_BLOCK
    ns = len(swap)

    def body(zu_ref, zv_ref, do_ref, w_ref, wt_ref, ln_ref, b_ref, avg_ref, ind_ref, *rest):
        dzu_ref, dzv_ref, dw_ref, db_ref, dln_ref = rest[ns:ns + 5]
        dvn_s, dbf_s = rest[2 * ns + 5:2 * ns + 7]
        swap_start, swap_finish = _swap_ops(rest[:ns], rest[ns + 5:2 * ns + 5], *rest[2 * ns + 7:])
        i = pl.program_id(0)

        @pl.when(i == 0)
        def _():
            swap_start()
            dw_ref[...] = jnp.zeros_like(dw_ref)
            dln_ref[...] = jnp.zeros_like(dln_ref)
            dbf_s[...] = jnp.zeros_like(dbf_s)

        zu = zu_ref[...].astype(F32)
        zv = zv_ref[...].astype(F32)
        u = _gelu(zu)
        v = _gelu(zv)
        avg = avg_ref[...]
        vhat, rstd = _layer_norm_heads(v, avg)
        ln = ln_ref[...]
        vn = vhat * ln
        for b in range(nb):
            rows = slice(b * SG_BLOCK, (b + 1) * SG_BLOCK)
            vn_b = vn[rows]
            mixed = _gate_mix(vn_b, w_ref, b_ref[...])
            do = do_ref[rows, :].astype(F32)
            dzu_ref[rows, :] = (do * mixed * _gelu_grad(zu[rows])).astype(BF16)
            dmix = do * u[rows]
            dbf_s[...] += dmix
            vn_bf = vn_b.astype(BF16)
            dvn = jnp.zeros((SG_BLOCK, D_HEADS), F32)
            for h in range(N_HEADS):
                dmh = jnp.where(_head_mask(h, SG_BLOCK), dmix, 0.0).astype(BF16)
                dw_ref[h] += lax.dot_general(dmh, vn_bf, NT, preferred_element_type=F32)
                dvn = dvn + jnp.dot(wt_ref[h], dmh, preferred_element_type=F32)
            dvn_s[rows, :] = dvn
        dvn = dvn_s[...]
        dln_ref[...] += jnp.sum(dvn * vhat, axis=0, keepdims=True)
        dvhat = dvn * ln
        dv = rstd * (dvhat - _split_dot(dvhat, avg) - vhat * _split_dot(dvhat * vhat, avg))
        dzv_ref[...] = (dv * _gelu_grad(zv)).astype(BF16)

        @pl.when(i == pl.num_programs(0) - 1)
        def _():
            r = lax.broadcasted_iota(jnp.int32, (SG_BLOCK, SG_BLOCK), 0) // CHUNK
            s = lax.broadcasted_iota(jnp.int32, (SG_BLOCK, SG_BLOCK), 1) // CHUNK
            for h in range(N_HEADS):
                dw_ref[h] = jnp.where(r >= s, dw_ref[h], 0.0)
            db_ref[...] = _split_dot(dbf_s[...], ind_ref[...])
            swap_finish()

    row = lambda col: pl.BlockSpec((tm, D_HEADS), lambda i: (i, col))
    wspec = _full((N_HEADS, SG_BLOCK, SG_BLOCK))
    out = pl.pallas_call(
        body, name="gate_bwd", grid=(S // tm,),
        in_specs=[row(0), row(1), row(0), wspec, wspec, _full((1, D_HEADS)), _full((SG_BLOCK, D_HEADS)),
                  _full((D_HEADS, D_HEADS)), _full((D_HEADS, LANES))] + [_ANY] * ns,
        out_specs=[row(0), row(0), wspec, _full((SG_BLOCK, LANES)), _full((1, D_HEADS))] + [_ANY] * ns,
        out_shape=[jax.ShapeDtypeStruct((S, D_HEADS), BF16), jax.ShapeDtypeStruct((S, D_HEADS), BF16),
                   jax.ShapeDtypeStruct((N_HEADS, SG_BLOCK, SG_BLOCK), F32), jax.ShapeDtypeStruct((SG_BLOCK, LANES), F32),
                   jax.ShapeDtypeStruct((1, D_HEADS), F32)] + _swap_shapes(swap),
        scratch_shapes=[pltpu.VMEM((tm, D_HEADS), F32), pltpu.VMEM((SG_BLOCK, D_HEADS), F32)] + _swap_sems(ns),
        compiler_params=_params(("arbitrary",)),
    )(z, z, dcat, w_mask, w_mask_t, ln_row, b_full, seg_avg, head_ind, *swap)
    return out[:5], out[5:]


def _attn_pack_grad(o, dcat, qa, lse, head_ind, k, tm):
    S = o.shape[0]

    def body(o_ref, do_ref, qa_ref, lse_ref, ind_ref, pl_ref, pt_ref, eye_ref, ds_ref, dst_ref, ls_ref, lst_ref,
             dop_ref, qb_ref, dot_ref, qbt_ref):
        do = do_ref[...]
        delta = _split_dot(o_ref[...].astype(F32) * do.astype(F32), ind_ref[...])
        hi = delta.astype(BF16).astype(F32)
        parts = (hi + pltpu.roll((delta - hi).astype(BF16).astype(F32), N_HEADS, 1)).astype(BF16)
        dop = jnp.dot(do, pl_ref[...], preferred_element_type=F32) - jnp.dot(parts, ds_ref[...], preferred_element_type=F32)
        for g in range(GROUPS):
            dop_ref[g] = dop[:, g * GROUP_PAD:(g + 1) * GROUP_PAD].astype(BF16)
        dot = lax.dot_general(pt_ref[...], do, NT, preferred_element_type=F32)
        dot_ref[...] = (dot - lax.dot_general(dst_ref[...], parts, NT, preferred_element_type=F32)).astype(BF16)
        qa = qa_ref[...]
        stack = jnp.concatenate(_split3(lse_ref[...]), axis=0)
        qb = qa.astype(F32) - lax.dot_general(stack, ls_ref[...], TN, preferred_element_type=F32)
        qbt = lax.dot_general(eye_ref[...], qa, NT, preferred_element_type=F32)
        qbt = qbt - jnp.dot(lst_ref[...], stack, preferred_element_type=F32)
        for g in range(GROUPS):
            qb_ref[g] = qb[:, g * GROUP_PAD:(g + 1) * GROUP_PAD].astype(BF16)
        qbt_ref[...] = qbt.astype(BF16)

    pad = pl.BlockSpec((tm, D_PAD), lambda i: (i, 0))
    padt = pl.BlockSpec((None, D_PAD, tm), lambda i: (i, 0, 0))
    return pl.pallas_call(
        body, name="attn_pack_grad", grid=(S // tm,),
        in_specs=[pl.BlockSpec((tm, D_HEADS), lambda i: (i, 0)), pl.BlockSpec((tm, D_HEADS), lambda i: (i, 1)), pad,
                  pl.BlockSpec((STAT_ROWS, tm), lambda i: (0, i)), _full((D_HEADS, LANES)), _full((D_HEADS, D_PAD)),
                  _full((D_PAD, D_HEADS)), _full((D_PAD, D_PAD)), _full((LANES, D_PAD)), _full((D_PAD, LANES)),
                  _full((3 * STAT_ROWS, D_PAD)), _full((D_PAD, 3 * STAT_ROWS))],
        out_specs=[pl.BlockSpec((GROUPS, tm, GROUP_PAD), lambda i: (0, i, 0))] * 2 + [padt, padt],
        out_shape=[jax.ShapeDtypeStruct((GROUPS, S, GROUP_PAD), BF16)] * 2 + [jax.ShapeDtypeStruct((S // tm, D_PAD, tm), BF16)] * 2,
        compiler_params=_params(("parallel",)),
    )(o, dcat, qa, lse, head_ind, k["place"], k["place_t"], jnp.eye(D_PAD, dtype=BF16), k["d_stat"], k["d_stat"].T,
      k["l_stat"], k["l_stat"].T)


def _attn_bwd(qb, qbt, ka, va, dop, dopt, k, tq, sums16):
    S = ka.shape[0]
    n = S // tq
    ns = len(sums16)

    pairs = [(kb, q) for kb in range(n) for q in range(kb, n)]
    k_of = jnp.asarray([kb for kb, _ in pairs], jnp.int32)
    q_of = jnp.asarray([q for _, q in pairs], jnp.int32)

    def body(k_of_ref, q_of_ref, q_ref, qt_ref, k_ref, v_ref, do_ref, dot_ref, pt_ref, *rest):
        dq_hbm, dcr_hbm, dk_ref, dv_ref, dcc_ref = rest[ns:ns + 5]
        dq_s, dcr_s, dk_s, dv_s, dcc_s = rest[2 * ns + 5:2 * ns + 10]
        s_s, d_s = rest[2 * ns + 10:2 * ns + 12], rest[2 * ns + 12:2 * ns + 14]
        sems = rest[2 * ns + 14]
        scatter_start, scatter_finish = _scatter_ops(rest[:ns], rest[ns + 5:2 * ns + 5], *rest[2 * ns + 15:])
        g = pl.program_id(0)
        ki, qi = k_of_ref[pl.program_id(1)], q_of_ref[pl.program_id(1)]

        @pl.when((g == 0) & (ki == 0) & (qi == 0))
        def _():
            scatter_start()

        @pl.when((ki == 0) & (qi == 0))
        def _():
            dq_s[...] = jnp.zeros_like(dq_s)
            dcr_s[...] = jnp.zeros_like(dcr_s)

        @pl.when(qi == ki)
        def _():
            dk_s[...] = jnp.zeros_like(dk_s)
            dv_s[...] = jnp.zeros_like(dv_s)
            dcc_s[...] = jnp.zeros_like(dcc_s)

        def step(diagonal):
            chunks = [slice(c * KEY_CHUNK, (c + 1) * KEY_CHUNK) for c in range(tq // KEY_CHUNK)]

            def keys_of(rows):
                return slice(0, rows.stop) if diagonal else slice(0, tq)

            def scores(hh, rows, slot):
                sl, keys = slice(hh * HEAD_PAD, (hh + 1) * HEAD_PAD), keys_of(rows)
                s_s[slot][rows, keys] = lax.dot_general(q_ref[rows, sl], k_ref[keys, sl], NT, preferred_element_type=F32)
                d_s[slot][rows, keys] = lax.dot_general(do_ref[rows, sl], v_ref[keys, sl], NT, preferred_element_type=F32)

            for rows in chunks:
                scores(0, rows, 0)
            for hh in range(GROUP_HEADS):
                sl = slice(hh * HEAD_PAD, (hh + 1) * HEAD_PAD)
                slot = hh % 2
                for rows in chunks:
                    if hh + 1 < GROUP_HEADS:
                        scores(hh + 1, rows, 1 - slot)
                    keys = keys_of(rows)
                    p = jnp.exp2(s_s[slot][rows, keys])
                    if diagonal:
                        row = rows.start + lax.broadcasted_iota(jnp.int32, (KEY_CHUNK, keys.stop), 0)
                        col = lax.broadcasted_iota(jnp.int32, (KEY_CHUNK, keys.stop), 1)
                        p = jnp.where(row >= col, p, 0.0)
                    ds = p * d_s[slot][rows, keys]
                    qrows = pl.ds(pl.multiple_of(qi * tq + rows.start, KEY_CHUNK), KEY_CHUNK)
                    dcc_s[hh:hh + 1, keys] += jnp.sum(ds, axis=0, keepdims=True)
                    dcr_s[qrows, hh:hh + 1] += jnp.sum(ds, axis=1, keepdims=True)
                    ds = ds.astype(BF16)
                    dv_s[sl, keys] += jnp.dot(dot_ref[sl, rows], p.astype(BF16), preferred_element_type=F32)
                    dk_s[sl, keys] += jnp.dot(qt_ref[sl, rows], ds, preferred_element_type=F32)
                    dq_s[qrows, sl] += jnp.dot(ds, k_ref[keys, sl], preferred_element_type=F32)

        @pl.when(qi > ki)
        def _():
            step(False)

        @pl.when(qi == ki)
        def _():
            step(True)

        @pl.when(qi == n - 1)
        def _():
            dk = dk_s[...]
            pt = pt_ref[...]
            dk_ref[...] = lax.dot_general((dk * (1.0 / LOG2E)).astype(BF16), pt, TN, preferred_element_type=F32).astype(BF16)
            dv_ref[...] = lax.dot_general(dv_s[...].astype(BF16), pt, TN, preferred_element_type=F32).astype(BF16)
            dcc_ref[...] = dcc_s[...]

        @pl.when((ki == n - 1) & (qi == n - 1))
        def _():
            copies = [pltpu.make_async_copy(dq_s, dq_hbm.at[g], sems.at[0]), pltpu.make_async_copy(dcr_s, dcr_hbm.at[g], sems.at[1])]
            for cp in copies:
                cp.start()
            for cp in copies:
                cp.wait()

        @pl.when((g == GROUPS - 1) & (ki == n - 1) & (qi == n - 1))
        def _():
            scatter_finish()

    gw = GROUP_HEADS * HEAD_DIM
    qspec = pl.BlockSpec((None, tq, GROUP_PAD), lambda g, i, ks, qs: (g, qs[i], 0))
    qtspec = pl.BlockSpec((None, GROUP_PAD, tq), lambda g, i, ks, qs: (qs[i], g, 0))
    kspec = pl.BlockSpec((tq, GROUP_PAD), lambda g, i, ks, qs: (ks[i], g))
    kout = pl.BlockSpec((tq, gw), lambda g, i, ks, qs: (ks[i], g))
    out = pl.pallas_call(
        body, name="attn_bwd",
        grid_spec=pltpu.PrefetchScalarGridSpec(
            num_scalar_prefetch=2, grid=(GROUPS, len(pairs)),
            in_specs=[qspec, qtspec, kspec, kspec, qspec, qtspec, pl.BlockSpec((GROUP_PAD, gw), lambda g, i, ks, qs: (0, 0))]
            + [_ANY] * ns,
            out_specs=[_ANY, _ANY, kout, kout, pl.BlockSpec((None, SUBLANES, tq), lambda g, i, ks, qs: (g, 0, ks[i]))] + [_ANY] * ns,
            scratch_shapes=[pltpu.VMEM((S, GROUP_PAD), F32), pltpu.VMEM((S, LANES), F32), pltpu.VMEM((GROUP_PAD, tq), F32),
                            pltpu.VMEM((GROUP_PAD, tq), F32), pltpu.VMEM((SUBLANES, tq), F32),
                            pltpu.VMEM((tq, tq), F32), pltpu.VMEM((tq, tq), F32), pltpu.VMEM((tq, tq), F32),
                            pltpu.VMEM((tq, tq), F32), pltpu.SemaphoreType.DMA((2,))]
            + _scatter_sems(ns)),
        out_shape=[jax.ShapeDtypeStruct((GROUPS, S, GROUP_PAD), F32), jax.ShapeDtypeStruct((GROUPS, S, LANES), F32),
                   jax.ShapeDtypeStruct((S, D_HEADS), BF16), jax.ShapeDtypeStruct((S, D_HEADS), BF16),
                   jax.ShapeDtypeStruct((GROUPS, SUBLANES, S), F32)]
        + _scatter_shapes(sums16),
        compiler_params=_params(("arbitrary", "arbitrary")),
    )(k_of, q_of, qb, qbt, ka, va, dop, dopt, k["place_t_group"], *sums16)
    return out[0], out[1], out[2], out[3], out[4], out[5:]


def _attn_unpack(dqp, k, tm):
    S = dqp.shape[1]
    gw = GROUP_HEADS * HEAD_DIM

    def body(dqp_ref, pt_ref, dq_ref):
        for g in range(GROUPS):
            dq_ref[:, g * gw:(g + 1) * gw] = jnp.dot((dqp_ref[g] * SCALE).astype(BF16), pt_ref[...],
                                                     preferred_element_type=F32).astype(BF16)

    return pl.pallas_call(
        body, name="attn_unpack", grid=(S // tm,),
        in_specs=[pl.BlockSpec((GROUPS, tm, GROUP_PAD), lambda i: (0, i, 0)), _full((GROUP_PAD, gw))],
        out_specs=pl.BlockSpec((tm, D_HEADS), lambda i: (i, 0)),
        out_shape=jax.ShapeDtypeStruct((S, D_HEADS), BF16),
        compiler_params=_params(("parallel",)),
    )(dqp, k["place_t_group"])


def _fox_bwd(dc, f, bias_row, tb):
    S = f.shape[0]
    nb = S // tb

    def body(dc_ref, f_ref, b_ref, df_ref, dbias_ref, carry):
        @pl.when(pl.program_id(0) == 0)
        def _():
            carry[...] = jnp.zeros_like(carry)
            dbias_ref[...] = jnp.zeros_like(dbias_ref)

        r = lax.broadcasted_iota(jnp.int32, (tb, tb), 0)
        s = lax.broadcasted_iota(jnp.int32, (tb, tb), 1)
        tri = (s >= r).astype(F32)
        rc = jnp.dot(tri, dc_ref[...], precision=lax.Precision.HIGHEST, preferred_element_type=F32) + carry[0:1, :]
        carry[...] = jnp.broadcast_to(rc[0:1, :], carry.shape)
        lane = lax.broadcasted_iota(jnp.int32, (tb, LANES), 1)
        df = jnp.where(lane < N_HEADS, rc * jax.nn.sigmoid(-(f_ref[...] + b_ref[...])), 0.0)
        df_ref[...] = df.astype(BF16)
        dbias_ref[...] += jnp.sum(df, axis=0, keepdims=True)

    rev = pl.BlockSpec((tb, LANES), lambda i: (nb - 1 - i, 0))
    return pl.pallas_call(
        body, name="fox_bwd", grid=(nb,),
        in_specs=[rev, rev, _full((1, LANES))],
        out_specs=[rev, _full((1, LANES))],
        out_shape=[jax.ShapeDtypeStruct((S, LANES), BF16), jax.ShapeDtypeStruct((1, LANES), F32)],
        scratch_shapes=[pltpu.VMEM((SUBLANES, LANES), F32)],
        compiler_params=_params(("arbitrary",)),
    )(dc, f, bias_row)


_DZ_WIDTHS = (D_HEADS,) * 5 + (LANES,)


def _in_bwd(pieces, w_in, x, g1, dx1, tm, sums16):
    S = x.shape[0]
    ns = len(sums16)

    def body(*refs):
        p_refs, (w_ref, x_ref, g_ref, dx1_ref) = refs[:6], refs[6:10]
        dx_ref, dg_ref = refs[10 + ns:12 + ns]
        scatter_start, scatter_finish = _scatter_ops(refs[10:10 + ns], refs[12 + ns:12 + 2 * ns], *refs[12 + 2 * ns:])

        @pl.when(pl.program_id(0) == 0)
        def _():
            dg_ref[...] = jnp.zeros_like(dg_ref)
            scatter_start()

        dh = jnp.zeros((tm, D_MODEL), F32)
        off = 0
        for p_ref, w in zip(p_refs, _DZ_WIDTHS):
            dh = dh + lax.dot_general(p_ref[...].astype(BF16), w_ref[:, off:off + w], NT, preferred_element_type=F32)
            off += w
        dx, dg = _rms_bwd(dh, x_ref[...], g_ref[...])
        dg_ref[...] += dg
        dx_ref[...] = dx1_ref[...] + dx

        @pl.when(pl.program_id(0) == pl.num_programs(0) - 1)
        def _():
            scatter_finish()

    row = lambda w: pl.BlockSpec((tm, w), lambda i: (i, 0))
    out = pl.pallas_call(
        body, name="in_bwd", grid=(S // tm,),
        in_specs=[row(w) for w in _DZ_WIDTHS] + [_full((D_MODEL, D_IN_PAD)), row(D_MODEL), _full((1, D_MODEL)), row(D_MODEL)]
        + [_ANY] * ns,
        out_specs=[row(D_MODEL), _full((1, D_MODEL))] + [_ANY] * ns,
        out_shape=[jax.ShapeDtypeStruct((S, D_MODEL), F32), jax.ShapeDtypeStruct((1, D_MODEL), F32)] + _scatter_shapes(sums16),
        scratch_shapes=_scatter_sems(ns),
        compiler_params=_params(("arbitrary",)),
    )(*pieces, w_in, x, g1, dx1, *sums16)
    return out[0], out[1], out[2:]


def _dw_in(h1, pieces, tk):
    S = h1.shape[0]

    def body(*refs):
        h_ref, p_refs, o_ref = refs[0], refs[1:7], refs[7]

        @pl.when(pl.program_id(0) == 0)
        def _():
            o_ref[...] = jnp.zeros_like(o_ref)

        off = 0
        for p_ref, w in zip(p_refs, _DZ_WIDTHS):
            o_ref[:, off:off + w] += lax.dot_general(h_ref[...], p_ref[...].astype(BF16), TN, preferred_element_type=F32)
            off += w

    row = lambda w: pl.BlockSpec((tk, w), lambda k: (k, 0))
    return pl.pallas_call(
        body, name="dw_in", grid=(S // tk,),
        in_specs=[row(D_MODEL)] + [row(w) for w in _DZ_WIDTHS],
        out_specs=_full((D_MODEL, D_IN_PAD)),
        out_shape=jax.ShapeDtypeStruct((D_MODEL, D_IN_PAD), F32),
        compiler_params=_params(("arbitrary",)),
    )(h1, *pieces)


def _adamw_math(w, g, m, v):
    m = ADAM_B1 * m + (1.0 - ADAM_B1) * g
    v = ADAM_B2 * v + (1.0 - ADAM_B2) * (g * g)
    m_hat = m / (1.0 - ADAM_B1 ** ADAM_STEP)
    v_hat = v / (1.0 - ADAM_B2 ** ADAM_STEP)
    delta = -ADAM_LR * (m_hat / (jnp.sqrt(v_hat) + ADAM_EPS) + ADAM_WD * w)
    return delta, m, v


def _adamw(name, w, g, m, v):
    R, C = w.shape
    tr = _row_tile(R, 256)

    def body(w_ref, g_ref, m_ref, v_ref, go_ref, d_ref, nm_ref, nv_ref):
        g = g_ref[...]
        d, nm, nv = _adamw_math(w_ref[...], g, m_ref[...], v_ref[...])
        go_ref[...] = g
        d_ref[...] = d
        nm_ref[...] = nm
        nv_ref[...] = nv

    spec = pl.BlockSpec((tr, C), lambda i: (i, 0))
    return pl.pallas_call(
        body, name=name, grid=(R // tr,), in_specs=[spec] * 4, out_specs=[spec] * 4,
        out_shape=[jax.ShapeDtypeStruct((R, C), F32)] * 4,
        compiler_params=_params(("parallel",)),
    )(w, g, m, v)


def _pair_sum(name, grad, theirs, ids):
    q, half, C = theirs.shape
    tr = _row_tile(half, 256)
    nb = half // tr

    def body(ids_ref, a_ref, b_ref, sb_ref):
        sb_ref[...] = (a_ref[...] + b_ref[...]).astype(BF16)

    here = pl.BlockSpec((None, tr, C), lambda j, i, ids: (j, i, 0))
    return pl.pallas_call(
        body, name=name,
        grid_spec=pltpu.PrefetchScalarGridSpec(
            num_scalar_prefetch=1, grid=(q, nb),
            in_specs=[pl.BlockSpec((None, tr, C), lambda j, i, ids: (j, ids[1] * nb + i, 0)), here],
            out_specs=here),
        out_shape=jax.ShapeDtypeStruct((q, half, C), BF16),
        compiler_params=_params(("parallel", "parallel")),
    )(ids, grad, theirs)


def _chip_sum(name, grad, theirs, others, ids):
    _, half, C = theirs.shape
    tr = _row_tile(half, 256)
    nb = half // tr

    def body(ids_ref, a_ref, b_ref, o_ref, s_ref):
        s = a_ref[...] + b_ref[...]
        for j in range(3):
            s = s + o_ref[j].astype(F32)
        s_ref[...] = s

    return pl.pallas_call(
        body, name=name,
        grid_spec=pltpu.PrefetchScalarGridSpec(
            num_scalar_prefetch=1, grid=(nb,),
            in_specs=[pl.BlockSpec((None, tr, C), lambda i, ids: (ids[0], ids[1] * nb + i, 0)),
                      pl.BlockSpec((None, tr, C), lambda i, ids: (ids[0], i, 0)),
                      pl.BlockSpec((3, tr, C), lambda i, ids: (0, i, 0))],
            out_specs=pl.BlockSpec((tr, C), lambda i, ids: (ids[1] * nb + i, 0))),
        out_shape=jax.ShapeDtypeStruct((2 * half, C), F32),
        compiler_params=_params(("parallel",)),
    )(ids, grad, theirs, others)


def _place():
    return lax.axis_index("x"), lax.axis_index("y"), lax.axis_index("c")


def _other_chips(x, y):
    return [(1 - x, y), (x, 1 - y), (1 - x, 1 - y)]


_ANY = pl.BlockSpec(memory_space=pl.ANY)


def _gather_quarters(shards):
    n = len(shards)

    def body(*refs):
        start, hand_on, finish = _gather_ops(refs[:n], refs[n:2 * n], *refs[2 * n:])
        start()
        hand_on()
        finish()

    return pl.pallas_call(
        body, name="gather_weights",
        in_specs=[_ANY] * n, out_specs=[_ANY] * n,
        out_shape=_gather_shapes(shards), scratch_shapes=_gather_sems(n),
    )(*shards)


def _gather_shapes(shards):
    return [jax.ShapeDtypeStruct((4,) + s.shape, s.dtype) for s in shards]


def _gather_sems(n):
    return [pltpu.SemaphoreType.DMA((n, 3))] * 4 + [pltpu.SemaphoreType.DMA((n,))]


def _gather_ops(ins, outs, send_sems, recv_sems, pass_send_sems, pass_recv_sems, own_sems):
    n = len(ins)
    halved = [r.shape[0] % 32 == 0 for r in ins]

    def part(a, quarter, core):
        if not halved[a]:
            return outs[a].at[quarter]
        half = ins[a].shape[0] // 2
        return outs[a].at[quarter, pl.ds(core * half, half), :]

    def ici(a, j, quarter):
        x, y, c = _place()
        px, py = _other_chips(x, y)[j]
        src = ins[a]
        if halved[a]:
            half = src.shape[0] // 2
            src = src.at[pl.ds(c * half, half), :]
        return pltpu.make_async_remote_copy(src_ref=src, dst_ref=part(a, quarter, c), send_sem=send_sems.at[a, j],
                                            recv_sem=recv_sems.at[a, j], device_id=(px, py, c), device_id_type=MESH)

    def passed(a, j, core):
        x, y, c = _place()
        px, py = _other_chips(x, y)[j]
        half = part(a, 2 * px + py, core)
        return pltpu.make_async_remote_copy(src_ref=half, dst_ref=half, send_sem=pass_send_sems.at[a, j],
                                            recv_sem=pass_recv_sems.at[a, j], device_id=(x, y, 1 - c), device_id_type=MESH)

    def own(a):
        x, y, _ = _place()
        return pltpu.make_async_copy(ins[a], outs[a].at[2 * x + y], own_sems.at[a])

    def start():
        x, y, _ = _place()
        for a in range(n):
            for j in range(3):
                ici(a, j, 2 * x + y).start()
            own(a).start()

    def hand_on():
        x, y, c = _place()
        for a in range(n):
            for j, (px, py) in enumerate(_other_chips(x, y)):
                ici(a, j, 2 * px + py).wait_recv()
                if halved[a]:
                    passed(a, j, c).start()

    def finish():
        x, y, c = _place()
        for a in range(n):
            for j in range(3):
                if halved[a]:
                    passed(a, j, 1 - c).wait_recv()
                    passed(a, j, c).wait_send()
                ici(a, j, 2 * x + y).wait_send()
            own(a).wait()

    return start, hand_on, finish


def _swap_halves(grads, name):
    n = len(grads)

    def body(*refs):
        start, finish = _swap_ops(refs[:n], refs[n:2 * n], *refs[2 * n:])
        start()
        finish()

    return pl.pallas_call(
        body, name=name,
        in_specs=[_ANY] * n, out_specs=[_ANY] * n, out_shape=_swap_shapes(grads), scratch_shapes=_swap_sems(n),
    )(*grads)


def _swap_shapes(grads):
    return [jax.ShapeDtypeStruct((4, g.shape[1] // 2, g.shape[2]), F32) for g in grads]


def _swap_sems(n):
    return [pltpu.SemaphoreType.DMA((n,))] * 2


def _swap_ops(ins, outs, send_sems, recv_sems):
    def copy(a):
        x, y, c = _place()
        half = ins[a].shape[1] // 2
        return pltpu.make_async_remote_copy(src_ref=ins[a].at[:, pl.ds((1 - c) * half, half), :], dst_ref=outs[a],
                                            send_sem=send_sems.at[a], recv_sem=recv_sems.at[a],
                                            device_id=(x, y, 1 - c), device_id_type=MESH)

    def start():
        for a in range(len(ins)):
            copy(a).start()

    def finish():
        for a in range(len(ins)):
            copy(a).wait()

    return start, finish


def _scatter_shapes(sums16):
    return [jax.ShapeDtypeStruct((3,) + s.shape[1:], BF16) for s in sums16]


def _scatter_sems(n):
    return [pltpu.SemaphoreType.DMA((n, 3))] * 2


def _scatter_ops(ins, outs, send_sems, recv_sems):
    n = len(ins)

    def copy(a, j):
        x, y, c = _place()
        px, py = _other_chips(x, y)[j]
        return pltpu.make_async_remote_copy(src_ref=ins[a].at[2 * px + py], dst_ref=outs[a].at[j], send_sem=send_sems.at[a, j],
                                            recv_sem=recv_sems.at[a, j], device_id=(px, py, c), device_id_type=MESH)

    def start():
        for a in range(n):
            for j in range(3):
                copy(a, j).start()

    def finish():
        for a in range(n):
            for j in range(3):
                copy(a, j).wait()

    return start, finish


def _join_halves(fulls):
    n = len(fulls)

    def body(*refs):
        ins, outs = refs[:n], refs[n:2 * n]
        send_sems, recv_sems = refs[2 * n:]
        x, y, c = _place()
        started = []
        for a in range(n):
            half = ins[a].shape[0] // 2
            rows = pl.ds(c * half, half)
            cp = pltpu.make_async_remote_copy(src_ref=ins[a].at[rows, :], dst_ref=outs[a].at[rows, :], send_sem=send_sems.at[a],
                                              recv_sem=recv_sems.at[a], device_id=(x, y, 1 - c), device_id_type=MESH)
            cp.start()
            started.append(cp)
        for cp in started:
            cp.wait()

    return pl.pallas_call(
        body, name="join_halves",
        in_specs=[_ANY] * n, out_specs=[_ANY] * n,
        out_shape=[jax.ShapeDtypeStruct(f.shape, F32) for f in fulls],
        input_output_aliases={a: a for a in range(n)},
        scratch_shapes=[pltpu.SemaphoreType.DMA((n,)), pltpu.SemaphoreType.DMA((n,))],
    )(*fulls)


def _small_allreduce(g):
    R = g.shape[0]
    half = R // 2

    def body(g_ref, out_ref, other_s, chip_s, parts_s, send_sems, recv_sems):
        x, y, c = _place()
        mine = 2 * x + y
        rows = pl.ds(pl.multiple_of(c * half, SUBLANES), half)

        def to_other_core(src, dst, k):
            return pltpu.make_async_remote_copy(src_ref=src, dst_ref=dst, send_sem=send_sems.at[k], recv_sem=recv_sems.at[k],
                                                device_id=(x, y, 1 - c), device_id_type=MESH)

        swap = to_other_core(g_ref, other_s, 0)
        swap.start()
        swap.wait()
        chip_s[...] = g_ref[...] + other_s[...]
        parts_s[mine] = chip_s[rows, :]
        sends = []
        for j, (px, py) in enumerate(_other_chips(x, y)):
            cp = pltpu.make_async_remote_copy(src_ref=chip_s.at[rows, :], dst_ref=parts_s.at[mine], send_sem=send_sems.at[1 + j],
                                              recv_sem=recv_sems.at[1 + j], device_id=(px, py, c), device_id_type=MESH)
            cp.start()
            sends.append(cp)
        for cp in sends:
            cp.wait()
        out_ref[rows, :] = (parts_s[0] + parts_s[1]) + (parts_s[2] + parts_s[3])
        join = to_other_core(out_ref.at[rows, :], out_ref.at[rows, :], 4)
        join.start()
        join.wait()

    vm = pl.BlockSpec(memory_space=pltpu.VMEM)
    return pl.pallas_call(
        body, name="small_allreduce",
        in_specs=[vm], out_specs=vm, out_shape=jax.ShapeDtypeStruct((R, LANES), F32),
        scratch_shapes=[pltpu.VMEM((R, LANES), F32), pltpu.VMEM((R, LANES), F32), pltpu.VMEM((4, half, LANES), F32),
                        pltpu.SemaphoreType.DMA((5,)), pltpu.SemaphoreType.DMA((5,))],
        compiler_params=pltpu.CompilerParams(vmem_limit_bytes=VMEM_LIMIT),
    )(g)


def _adamw_small(ws, gs, ms, vs):
    n = len(ws)

    def body(*refs):
        for k in range(n):
            w_ref, g_ref, m_ref, v_ref = (refs[j * n + k] for j in range(4))
            d, nm, nv = _adamw_math(w_ref[...], g_ref[...], m_ref[...], v_ref[...])
            refs[4 * n + k][...] = d
            refs[5 * n + k][...] = nm
            refs[6 * n + k][...] = nv

    vm = pl.BlockSpec(memory_space=pltpu.VMEM)
    out = pl.pallas_call(
        body, name="adamw_small",
        in_specs=[vm] * (4 * n), out_specs=[vm] * (3 * n),
        out_shape=[jax.ShapeDtypeStruct(w.shape, F32) for w in ws] * 3,
        compiler_params=pltpu.CompilerParams(vmem_limit_bytes=VMEM_LIMIT),
    )(*ws, *gs, *ms, *vs)
    return out[:n], out[n:2 * n], out[2 * n:]


_SMALL = (("norm_mix_g", D_MODEL), ("f_bias", N_HEADS), ("sg_ln_g", D_HEADS), ("sg_w", N_HEADS * SG_BLOCK * SG_BLOCK),
          ("sg_b", N_HEADS * SG_BLOCK), ("norm_ffn_g", D_MODEL), ("w_conv", 3 * 2 * D_FF), ("b_conv", 2 * D_FF),
          ("norm_final_g", D_MODEL))
_PACKED = _SMALL + (("sq_err", D_MODEL),)


def _pack_small(parts):
    rows = []
    for name, size in _PACKED:
        flat = parts[name].reshape(-1).astype(F32)
        pad = (-size) % (SUBLANES * LANES)
        rows.append(jnp.pad(flat, (0, pad)).reshape(-1, LANES))
    packed = jnp.concatenate(rows, axis=0)
    return jnp.pad(packed, ((0, (-packed.shape[0]) % (2 * SUBLANES)), (0, 0)))


def _unpack_small(packed, shapes):
    out, r = {}, 0
    for name, size in _PACKED:
        nrows = (size + SUBLANES * LANES - 1) // (SUBLANES * LANES) * SUBLANES
        out[name] = packed[r:r + nrows].reshape(-1)[:size].reshape(shapes[name])
        r += nrows
    return out


def _local_step(x, target, g1, w_in, f_bias, sg_ln_g, sg_w, sg_b, g2, b_conv, g3, late_shards, ids):
    S = x.shape[0]
    tm = _row_tile(S, 512)
    tms = _row_tile(S, 256)
    tq = tm

    lane = jnp.arange(D_HEADS)
    seg_avg = jnp.where(lane[:, None] // HEAD_DIM == lane[None, :] // HEAD_DIM, 1.0 / HEAD_DIM, 0.0).astype(BF16)
    head_ind = (lane[:, None] // HEAD_DIM == jnp.arange(LANES)[None, :]).astype(BF16)
    pos_chunk = jnp.arange(SG_BLOCK) // CHUNK
    w_mask32 = jnp.where(pos_chunk[:, None] >= pos_chunk[None, :], sg_w, 0.0)
    w_mask = w_mask32.astype(BF16)
    w_mask_t = jnp.swapaxes(w_mask32, 1, 2).astype(BF16)
    ln_row = sg_ln_g.reshape(1, D_HEADS)
    b_full = jnp.repeat(sg_b.T, HEAD_DIM, axis=1)
    bias_row = jnp.pad(f_bias.reshape(1, N_HEADS), ((0, 0), (0, LANES - N_HEADS)))
    b_conv_row = b_conv.reshape(1, 2 * D_FF)

    z, f, h1 = _in_proj(x, g1, w_in, tm)
    c = _fox_prep(f, bias_row, tm)
    consts = _attn_consts()
    qa, ka, va, vat = _attn_pack(z, c, consts, tm)
    out_b, lse, gathered = _attn_fwd(qa, ka, vat, consts["place_t"], tq, late_shards)
    g_out, w_up_q, g_down, g_conv = gathered
    w_out = g_out.reshape(D_MODEL, D_MODEL)
    w_down = g_down.reshape(D_FF, D_MODEL)
    w_conv = jnp.concatenate([g_conv[q] for q in range(4)], axis=1)
    out_a = _gate_fwd(z, w_mask, ln_row, b_full, seg_avg, tm)
    x1, h2 = _mix_out(x, out_a, out_b, w_out, g2, tm)
    a = _up_proj(h2, w_up_q, tm)
    dx2, sq_err, dg3 = _ffn_fwd_loss(a, w_conv, b_conv_row, w_down, x1, g3, target, tm)

    dconv, y, dw_conv8, db_conv = _ffn_bwd_gate(dx2, a, w_conv, b_conv_row, w_down, tms)
    dact = _conv_bwd(dconv, w_conv, tms, D_FF)
    dw_down = _matmul_tn(y, dx2, "dw_down", D_FF // 2, D_MODEL, tm, quarters=(2, 1))
    dx1, dg2 = _up_bwd(dact, w_up_q, x1, g2, dx2, tm)
    dw_up_q = _matmul_tn(h2, dact, "dw_up", D_MODEL, 2 * D_FF // 4, tm, quarters=(1, 4))
    dcat = _out_bwd(dx1, w_out, tm)
    dw_out = _dw_out(out_a, out_b, dx1, tm)
    early = {"w_down": dw_down.reshape(4, D_FF // 4, D_MODEL), "w_up": dw_up_q,
             "w_out": dw_out.reshape(4, D_MODEL // 4, D_MODEL)}
    (dzu, dzv, dsg_w, dsg_b_t, dln), theirs = _gate_bwd(z, dcat, w_mask, w_mask_t, ln_row, b_full, seg_avg, head_ind, tm,
                                                        list(early.values()))
    early_sums = _chip_sums(early, theirs, ids)
    dop, qb, dopt, qbt = _attn_pack_grad(out_b, dcat, qa, lse, head_ind, consts, tm)
    dqp, dc_rows, dk, dv, dc_cols, landed = _attn_bwd(qb, qbt, ka, va, dop, dopt, consts, tq,
                                                      [s16 for _, s16 in early_sums.values()])
    early_parts = {k: (s32, got) for (k, (s32, _)), got in zip(early_sums.items(), landed)}
    dq = _attn_unpack(dqp, consts, tm)
    dc_rows = jnp.concatenate([dc_rows[g][:, :GROUP_HEADS] for g in range(GROUPS)], axis=1)
    dc_cols = jnp.concatenate([dc_cols[g][:GROUP_HEADS] for g in range(GROUPS)], axis=0).T
    dc = jnp.pad(dc_rows - dc_cols, ((0, 0), (0, LANES - N_HEADS)))
    df, dbias = _fox_bwd(dc, f, bias_row, tm)
    pieces = (dzu, dzv, dq, dk, dv, df)
    dw_in = _dw_in(h1, pieces, tm)[:, :D_IN].reshape(D_MODEL, 4, D_IN // 4).transpose(1, 0, 2)
    (w_in_sum, w_in_sum16), = _chip_sums({"w_in": dw_in}, _swap_halves([dw_in], "swap_halves"), ids).values()
    dx, dg1, (w_in_landed,) = _in_bwd(pieces, w_in, x, g1, dx1, tm, [w_in_sum16])

    grads = {
        "norm_mix_g": dg1, "f_bias": dbias[:, :N_HEADS], "sg_ln_g": dln, "sg_w": dsg_w, "sg_b": dsg_b_t[:, :N_HEADS].T,
        "norm_ffn_g": dg2, "w_conv": dw_conv8[:3], "b_conv": db_conv, "norm_final_g": dg3,
    }
    return sq_err, dx, grads, {**early_parts, "w_in": (w_in_sum, w_in_landed)}


def _chip_sums(grads_q, theirs, ids):
    return {k: ((g, t), _pair_sum("pair_sum_" + k, g, t, ids)) for (k, g), t in zip(grads_q.items(), theirs)}


def _finish_reduction(parts, ids):
    names = list(parts)
    fulls = [_chip_sum("chip_sum_" + k, g, t, got, ids) for k, ((g, t), got) in parts.items()]
    return dict(zip(names, _join_halves(fulls)))


def kernel(x, norm_mix_g, w_in, f_bias, sg_ln_g, sg_w, sg_b, w_out, norm_ffn_g, w_up, w_conv, b_conv, w_down, norm_final_g, loss_target, m_norm_mix_g, m_w_in, m_f_bias, m_sg_ln_g, m_sg_w, m_sg_b, m_w_out, m_norm_ffn_g, m_w_up, m_w_conv, m_b_conv, m_w_down, m_norm_final_g, v_norm_mix_g, v_w_in, v_f_bias, v_sg_ln_g, v_sg_w, v_sg_b, v_w_out, v_norm_ffn_g, v_w_up, v_w_conv, v_b_conv, v_w_down, v_norm_final_g):
    args = dict(locals())
    quarter = 2 * lax.axis_index("x") + lax.axis_index("y")
    ids = jnp.stack([quarter, lax.axis_index("c")]).astype(jnp.int32)
    wq_conv = w_conv.shape[-1]

    g_in = _gather_quarters([w_in[0].astype(BF16)])[0]
    w_in_full = jnp.pad(jnp.concatenate([g_in[q] for q in range(4)], axis=1), ((0, 0), (0, D_IN_PAD - D_IN)))
    late_shards = [w_out[0].astype(BF16), w_up[0].astype(BF16), w_down[0].astype(BF16), w_conv[0]]

    sq_err, dx, grads, parts = _local_step(
        x[0], loss_target[0], norm_mix_g, w_in_full, f_bias[0], sg_ln_g[0], sg_w[0], sg_b[0], norm_ffn_g, b_conv[0],
        norm_final_g.reshape(1, D_MODEL), late_shards, ids)
    big = _finish_reduction(parts, ids)

    out = {"grad_x": dx[None]}
    for k in ("w_in", "w_out", "w_up", "w_down"):
        g, d, nm, nv = _adamw("adamw_" + k, args[k][0], big[k], args["m_" + k][0], args["v_" + k][0])
        out["grad_" + k], out["delta_" + k], out["new_m_" + k], out["new_v_" + k] = g[None], d[None], nm[None], nv[None]

    small_names = [n for n, _ in _SMALL]
    shapes = {n: (3, 4 * wq_conv) if n == "w_conv" else args[n].shape for n in small_names}
    shapes["sq_err"] = sq_err.shape
    g_small = _unpack_small(_small_allreduce(_pack_small({**{n: grads[n] for n in small_names}, "sq_err": sq_err})), shapes)
    out["loss"] = 0.5 * jnp.sum(g_small.pop("sq_err")) / D_MODEL
    g_small["w_conv"] = lax.dynamic_slice(g_small["w_conv"], (0, quarter * wq_conv), (3, wq_conv))[None]
    flat2d = lambda t: t.reshape(-1, t.shape[-1])
    updated = _adamw_small(*[[flat2d(src[p + n]) for n in small_names] for src, p in
                             ((args, ""), (g_small, ""), (args, "m_"), (args, "v_"))])
    for n, g in g_small.items():
        out["grad_" + n] = g
    for prefix, arrs in zip(("delta_", "new_m_", "new_v_"), updated):
        for n, t in zip(small_names, arrs):
            out[prefix + n] = t.reshape(args[n].shape)

    weights = ["norm_mix_g", "w_in", "f_bias", "sg_ln_g", "sg_w", "sg_b", "w_out", "norm_ffn_g", "w_up", "w_conv", "b_conv",
               "w_down", "norm_final_g"]
    return (out["loss"], out["grad_x"], *[out[p + n] for p in ("grad_", "delta_", "new_m_", "new_v_") for n in weights])
```

```python
import functools
import math

import jax
import jax.numpy as jnp
from jax import lax
from jax.experimental import pallas as pl
from jax.experimental.pallas import tpu as pltpu

F32 = jnp.float32
BF16 = jnp.bfloat16
MESH = pl.DeviceIdType.MESH

D_MODEL = 1024
N_HEADS = 8
HEAD_DIM = 64
D_HEADS = N_HEADS * HEAD_DIM
SG_BLOCK = 128
CHUNK = 64
D_FF = 2816
D_IN = 2 * D_HEADS + 3 * D_HEADS + N_HEADS
LANES = 128
SUBLANES = 8
D_IN_PAD = 5 * D_HEADS + LANES
EPS = 1e-6
SCALE = HEAD_DIM ** -0.5
NEG = -1e30
LOG2E = 1.4426950408889634
HEAD_PAD = LANES
D_PAD = N_HEADS * HEAD_PAD
Q_STAT = HEAD_DIM
K_STAT = HEAD_DIM + 3
L_STAT = HEAD_DIM + 6
GROUPS = 2
GROUP_HEADS = N_HEADS // GROUPS
GROUP_PAD = GROUP_HEADS * HEAD_PAD
KEY_CHUNK = 256
FWD_KEY_CHUNK = 512
STAT_ROWS = 16
FF_CHUNK = 256

ADAM_LR = 0.001
ADAM_B1 = 0.9
ADAM_B2 = 0.999
ADAM_EPS = 1e-08
ADAM_WD = 0.01
ADAM_STEP = 10

VMEM_LIMIT = 56 * 1024 * 1024

NT = (((1,), (1,)), ((), ()))
TN = (((0,), (0,)), ((), ()))


def _params(sem):
    return pltpu.CompilerParams(dimension_semantics=sem, vmem_limit_bytes=VMEM_LIMIT)


def _full(shape):
    nd = len(shape)
    return pl.BlockSpec(shape, lambda *_: (0,) * nd)


def _row_tile(rows, target):
    best = None
    for t in range(SUBLANES, min(rows, target) + 1, SUBLANES):
        if rows % t == 0:
            best = t
    assert best is not None, rows
    return best


def _sigmoid(x):
    return 0.5 * jnp.tanh(0.5 * x) + 0.5


def _gelu(z):
    return 0.5 * z * (1.0 + lax.erf(z * (2.0 ** -0.5)))


def _gelu_grad(z):
    cdf = 0.5 * (1.0 + lax.erf(z * (2.0 ** -0.5)))
    pdf = jnp.exp(-0.5 * z * z) * (1.0 / math.sqrt(2.0 * math.pi))
    return cdf + z * pdf


def _split_dot(x, m):
    hi = x.astype(BF16)
    lo = (x - hi.astype(F32)).astype(BF16)
    return jnp.dot(hi, m, preferred_element_type=F32) + jnp.dot(lo, m, preferred_element_type=F32)


def _head_mask(h, rows):
    lane = lax.broadcasted_iota(jnp.int32, (rows, D_HEADS), 1)
    return (lane >= h * HEAD_DIM) & (lane < (h + 1) * HEAD_DIM)


def _rms_bwd(dh, x, g):
    r = lax.rsqrt(jnp.mean(x * x, axis=-1, keepdims=True) + EPS)
    xhat = x * r
    dg = jnp.sum(dh * xhat, axis=0, keepdims=True)
    dxhat = dh * g
    dx = r * (dxhat - xhat * jnp.mean(dxhat * xhat, axis=-1, keepdims=True))
    return dx, dg


def _in_proj(x, g1, w_in, tm):
    S = x.shape[0]
    nz = D_IN_PAD - LANES

    def body(x_ref, g_ref, w_ref, z_ref, f_ref, h_ref):
        xf = x_ref[...]
        r = lax.rsqrt(jnp.mean(xf * xf, axis=-1, keepdims=True) + EPS)
        h = (xf * r * g_ref[...]).astype(BF16)
        h_ref[...] = h
        zz = jnp.dot(h, w_ref[...], preferred_element_type=F32)
        z_ref[...] = zz[:, :nz].astype(BF16)
        f_ref[...] = zz[:, nz:]

    return pl.pallas_call(
        body, name="in_proj", grid=(S // tm,),
        in_specs=[pl.BlockSpec((tm, D_MODEL), lambda i: (i, 0)), _full((1, D_MODEL)), _full((D_MODEL, D_IN_PAD))],
        out_specs=[pl.BlockSpec((tm, nz), lambda i: (i, 0)), pl.BlockSpec((tm, LANES), lambda i: (i, 0)),
                   pl.BlockSpec((tm, D_MODEL), lambda i: (i, 0))],
        out_shape=[jax.ShapeDtypeStruct((S, nz), BF16), jax.ShapeDtypeStruct((S, LANES), F32),
                   jax.ShapeDtypeStruct((S, D_MODEL), BF16)],
        compiler_params=_params(("parallel",)),
    )(x, g1, w_in)


def _fox_prep(f, bias_row, tb):
    S = f.shape[0]

    def body(f_ref, b_ref, c_ref, carry):
        @pl.when(pl.program_id(0) == 0)
        def _():
            carry[...] = jnp.zeros_like(carry)

        xv = f_ref[...] + b_ref[...]
        lf = jnp.minimum(xv, 0.0) - jnp.log(1.0 + jnp.exp(-jnp.abs(xv)))
        r = lax.broadcasted_iota(jnp.int32, (tb, tb), 0)
        s = lax.broadcasted_iota(jnp.int32, (tb, tb), 1)
        tri = (r >= s).astype(F32)
        cs = jnp.dot(tri, lf, precision=lax.Precision.HIGHEST, preferred_element_type=F32) + carry[0:1, :]
        c_ref[...] = cs
        carry[...] = jnp.broadcast_to(cs[tb - 1:tb, :], carry.shape)

    return pl.pallas_call(
        body, name="fox_prep", grid=(S // tb,),
        in_specs=[pl.BlockSpec((tb, LANES), lambda i: (i, 0)), _full((1, LANES))],
        out_specs=pl.BlockSpec((tb, LANES), lambda i: (i, 0)),
        out_shape=jax.ShapeDtypeStruct((S, LANES), F32),
        scratch_shapes=[pltpu.VMEM((SUBLANES, LANES), F32)],
        compiler_params=_params(("arbitrary",)),
    )(f, bias_row)


def _attn_consts():
    col = jnp.arange(D_PAD)
    row = jnp.arange(D_HEADS)
    head = jnp.arange(LANES)
    place = (row[:, None] // HEAD_DIM == col[None, :] // HEAD_PAD) & (row[:, None] % HEAD_DIM == col[None, :] % HEAD_PAD)

    def stat(offset):
        return ((head[:, None] < N_HEADS) & (col[None, :] == head[:, None] * HEAD_PAD + offset)).astype(BF16)

    def stat3(base):
        part, h = head // N_HEADS, head % N_HEADS
        return ((part[:, None] < 3) & (col[None, :] == h[:, None] * HEAD_PAD + base + part[:, None])).astype(BF16)

    def ones(offsets):
        return sum((col % HEAD_PAD == o) for o in offsets).astype(F32).reshape(1, D_PAD)

    place = place.astype(BF16)
    return {
        "place": place, "place_t": place.T, "place_t_group": place.T[:GROUP_PAD, :GROUP_HEADS * HEAD_DIM],
        "q_stat": stat3(Q_STAT), "k_stat": stat3(K_STAT),
        "d_stat": stat3(Q_STAT) * (head[:, None] < 2 * N_HEADS).astype(BF16),
        "l_stat": jnp.concatenate([stat(L_STAT + j)[:STAT_ROWS] for j in range(3)], axis=0),
        "q_ones": ones(range(K_STAT, K_STAT + 3)), "k_ones": ones(list(range(Q_STAT, Q_STAT + 3)) + list(range(L_STAT, L_STAT + 3))),
        "v_ones": ones(range(Q_STAT, Q_STAT + 2)),
    }


def _split3(x):
    hi = x.astype(BF16)
    r = x - hi.astype(F32)
    mid = r.astype(BF16)
    return hi, mid, (r - mid.astype(F32)).astype(BF16)


def _attn_pack(z, c, k, tm):
    S = z.shape[0]

    def body(q_ref, k_ref, v_ref, c_ref, pl_ref, pt_ref, qs_ref, ks_ref, qo_ref, ko_ref, vo_ref, voc_ref,
             qa_ref, ka_ref, va_ref, vt_ref):
        place = pl_ref[...]
        q = (q_ref[...].astype(F32) * (SCALE * LOG2E)).astype(BF16)
        qa = jnp.dot(q, place, preferred_element_type=F32) + qo_ref[...]
        ka = jnp.dot(k_ref[...], place, preferred_element_type=F32) + ko_ref[...]
        lane = lax.broadcasted_iota(jnp.int32, (tm, LANES), 1)
        hi, mid, lo = _split3(jnp.where(lane < N_HEADS, c_ref[...] * LOG2E, 0.0))
        parts = hi.astype(F32) + pltpu.roll(mid.astype(F32), N_HEADS, 1) + pltpu.roll(lo.astype(F32), 2 * N_HEADS, 1)
        parts = parts.astype(BF16)
        qa = qa + jnp.dot(parts, qs_ref[...], preferred_element_type=F32)
        ka = ka - jnp.dot(parts, ks_ref[...], preferred_element_type=F32)
        qa_ref[...] = qa.astype(BF16)
        ka_ref[...] = ka.astype(BF16)
        v = v_ref[...]
        va_ref[...] = (jnp.dot(v, place, preferred_element_type=F32) + vo_ref[...]).astype(BF16)
        vt_ref[...] = (lax.dot_general(pt_ref[...], v, NT, preferred_element_type=F32) + voc_ref[...]).astype(BF16)

    blk = lambda col: pl.BlockSpec((tm, D_HEADS), lambda i: (i, col))
    out = pl.BlockSpec((tm, D_PAD), lambda i: (i, 0))
    pad = jax.ShapeDtypeStruct((S, D_PAD), BF16)
    return pl.pallas_call(
        body, name="attn_pack", grid=(S // tm,),
        in_specs=[blk(2), blk(3), blk(4), pl.BlockSpec((tm, LANES), lambda i: (i, 0)), _full((D_HEADS, D_PAD)), _full((D_PAD, D_HEADS)),
                  _full((LANES, D_PAD)), _full((LANES, D_PAD)), _full((1, D_PAD)), _full((1, D_PAD)), _full((1, D_PAD)),
                  _full((D_PAD, 1))],
        out_specs=[out, out, out, pl.BlockSpec((None, D_PAD, tm), lambda i: (i, 0, 0))],
        out_shape=[pad, pad, pad, jax.ShapeDtypeStruct((S // tm, D_PAD, tm), BF16)],
        compiler_params=_params(("parallel",)),
    )(z, z, z, c, k["place"], k["place_t"], k["q_stat"], k["k_stat"], k["q_ones"], k["k_ones"], k["v_ones"], k["v_ones"].T)


def _attn_fwd(qa, ka, vat, place_t, tq, shards):
    S = qa.shape[0]
    n = S // tq
    ns = len(shards)
    hand_on_at = (2 * n) // 3

    pairs = [(q, k) for q in range(n) for k in range(q + 1)]
    q_of = jnp.asarray([q for q, _ in pairs], jnp.int32)
    k_of = jnp.asarray([k for _, k in pairs], jnp.int32)

    def body(q_of_ref, k_of_ref, q_ref, k_ref, vt_ref, pt_ref, *rest):
        o_ref, lse_ref = rest[ns:ns + 2]
        m_s, acc_s, ot_s = rest[2 * ns + 2:2 * ns + 5]
        s_s = rest[2 * ns + 5:2 * ns + 7]
        start, hand_on, finish = _gather_ops(rest[:ns], rest[ns + 2:2 * ns + 2], *rest[2 * ns + 7:])
        qi, ki = q_of_ref[pl.program_id(0)], k_of_ref[pl.program_id(0)]

        @pl.when((qi == 0) & (ki == 0))
        def _():
            start()

        @pl.when((qi == hand_on_at) & (ki == 0))
        def _():
            hand_on()

        @pl.when(ki == 0)
        def _():
            m_s[...] = jnp.full_like(m_s, NEG)
            acc_s[...] = jnp.zeros_like(acc_s)

        def step(diagonal):
            kc = FWD_KEY_CHUNK
            chunks = [slice(c * kc, (c + 1) * kc) for c in range(tq // kc)]

            def scores(h, rows, slot):
                sl = slice(h * HEAD_PAD, (h + 1) * HEAD_PAD)
                st = lax.dot_general(k_ref[rows, sl], q_ref[:, sl], NT, preferred_element_type=F32)
                if diagonal:
                    key = rows.start + lax.broadcasted_iota(jnp.int32, (kc, tq), 0)
                    query = lax.broadcasted_iota(jnp.int32, (kc, tq), 1)
                    st = jnp.where(query >= key, st, NEG)
                s_s[slot][rows, :] = st
                return jnp.max(st, axis=0, keepdims=True)

            m_cur = functools.reduce(jnp.maximum, [scores(0, rows, 0) for rows in chunks])
            for h in range(N_HEADS):
                sl = slice(h * HEAD_PAD, (h + 1) * HEAD_PAD)
                slot = h % 2
                m_prev = m_s[h][0:1, :]
                m_new = jnp.maximum(m_prev, m_cur)
                acc = jnp.exp2(m_prev - m_new) * acc_s[h]
                m_next = []
                for rows in chunks:
                    if h + 1 < N_HEADS:
                        m_next.append(scores(h + 1, rows, 1 - slot))
                    pt = jnp.exp2(s_s[slot][rows, :] - m_new).astype(BF16)
                    acc = acc + jnp.dot(vt_ref[sl, rows], pt, preferred_element_type=F32)
                acc_s[h] = acc
                m_s[h] = jnp.broadcast_to(m_new, (SUBLANES, tq))
                if m_next:
                    m_cur = functools.reduce(jnp.maximum, m_next)

        @pl.when(ki < qi)
        def _():
            step(False)

        @pl.when(ki == qi)
        def _():
            step(True)
            lse_ref[...] = jnp.zeros_like(lse_ref)
            for h in range(N_HEADS):
                acc = acc_s[h]
                denom = acc[Q_STAT:Q_STAT + 1, :]
                ot_s[h * HEAD_PAD:(h + 1) * HEAD_PAD, :] = (acc / denom).astype(BF16)
                lse_ref[h:h + 1, :] = m_s[h][0:1, :] + jnp.log(denom) * LOG2E
            o_ref[...] = lax.dot_general(ot_s[...], pt_ref[...], TN, preferred_element_type=F32).astype(BF16)

        @pl.when((qi == n - 1) & (ki == n - 1))
        def _():
            finish()

    out = pl.pallas_call(
        body, name="attn_fwd",
        grid_spec=pltpu.PrefetchScalarGridSpec(
            num_scalar_prefetch=2, grid=(len(pairs),),
            in_specs=[pl.BlockSpec((tq, D_PAD), lambda i, qs, ks: (qs[i], 0)),
                      pl.BlockSpec((tq, D_PAD), lambda i, qs, ks: (ks[i], 0)),
                      pl.BlockSpec((None, D_PAD, tq), lambda i, qs, ks: (ks[i], 0, 0)),
                      pl.BlockSpec((D_PAD, D_HEADS), lambda i, qs, ks: (0, 0))]
            + [_ANY] * ns,
            out_specs=[pl.BlockSpec((tq, D_HEADS), lambda i, qs, ks: (qs[i], 0)),
                       pl.BlockSpec((STAT_ROWS, tq), lambda i, qs, ks: (0, qs[i]))] + [_ANY] * ns,
            scratch_shapes=[pltpu.VMEM((N_HEADS, SUBLANES, tq), F32), pltpu.VMEM((N_HEADS, HEAD_PAD, tq), F32),
                            pltpu.VMEM((D_PAD, tq), BF16), pltpu.VMEM((tq, tq), F32), pltpu.VMEM((tq, tq), F32)] + _gather_sems(ns)),
        out_shape=[jax.ShapeDtypeStruct((S, D_HEADS), BF16), jax.ShapeDtypeStruct((STAT_ROWS, S), F32)] + _gather_shapes(shards),
        compiler_params=_params(("arbitrary",)),
    )(q_of, k_of, qa, ka, vat, place_t, *shards)
    return out[0], out[1], out[2:]


def _layer_norm_heads(v, seg_avg):
    mu = _split_dot(v, seg_avg)
    d = v - mu
    var = jnp.dot((d * d).astype(BF16), seg_avg, preferred_element_type=F32)
    rstd = lax.rsqrt(var + EPS)
    return d * rstd, rstd


def _gate_mix(vn_blk, w_ref, bias):
    acc = bias
    for h in range(N_HEADS):
        vh = jnp.where(_head_mask(h, SG_BLOCK), vn_blk, 0.0).astype(BF16)
        acc = acc + jnp.dot(w_ref[h], vh, preferred_element_type=F32)
    return acc


def _gate_fwd(z, w_mask, ln_row, b_full, seg_avg, tm):
    S = z.shape[0]

    def body(zu_ref, zv_ref, w_ref, ln_ref, b_ref, avg_ref, o_ref):
        u = _gelu(zu_ref[...].astype(F32))
        v = _gelu(zv_ref[...].astype(F32))
        vhat, _ = _layer_norm_heads(v, avg_ref[...])
        vn = vhat * ln_ref[...]
        for b in range(tm // SG_BLOCK):
            rows = slice(b * SG_BLOCK, (b + 1) * SG_BLOCK)
            mixed = _gate_mix(vn[rows], w_ref, b_ref[...])
            o_ref[rows, :] = (u[rows] * mixed).astype(BF16)

    return pl.pallas_call(
        body, name="gate_fwd", grid=(S // tm,),
        in_specs=[pl.BlockSpec((tm, D_HEADS), lambda i: (i, 0)), pl.BlockSpec((tm, D_HEADS), lambda i: (i, 1)),
                  _full((N_HEADS, SG_BLOCK, SG_BLOCK)), _full((1, D_HEADS)), _full((SG_BLOCK, D_HEADS)),
                  _full((D_HEADS, D_HEADS))],
        out_specs=pl.BlockSpec((tm, D_HEADS), lambda i: (i, 0)),
        out_shape=jax.ShapeDtypeStruct((S, D_HEADS), BF16),
        compiler_params=_params(("parallel",)),
    )(z, z, w_mask, ln_row, b_full, seg_avg)


def _mix_out(x, out_a, out_b, w_out, g2, tm):
    S = x.shape[0]

    def body(x_ref, a_ref, b_ref, w_ref, g_ref, x1_ref, h_ref):
        y = jnp.dot(a_ref[...], w_ref[:D_HEADS, :], preferred_element_type=F32)
        y = y + jnp.dot(b_ref[...], w_ref[D_HEADS:, :], preferred_element_type=F32)
        x1 = x_ref[...] + y
        x1_ref[...] = x1
        r = lax.rsqrt(jnp.mean(x1 * x1, axis=-1, keepdims=True) + EPS)
        h_ref[...] = (x1 * r * g_ref[...]).astype(BF16)

    row = lambda w: pl.BlockSpec((tm, w), lambda i: (i, 0))
    return pl.pallas_call(
        body, name="mix_out", grid=(S // tm,),
        in_specs=[row(D_MODEL), row(D_HEADS), row(D_HEADS), _full((D_MODEL, D_MODEL)), _full((1, D_MODEL))],
        out_specs=[row(D_MODEL), row(D_MODEL)],
        out_shape=[jax.ShapeDtypeStruct((S, D_MODEL), F32), jax.ShapeDtypeStruct((S, D_MODEL), BF16)],
        compiler_params=_params(("parallel",)),
    )(x, out_a, out_b, w_out, g2)


def _up_proj(h2, w_up_q, tm):
    S = h2.shape[0]
    nq, _, wq = w_up_q.shape

    def body(h_ref, w_ref, a_ref):
        a_ref[...] = jnp.dot(h_ref[...], w_ref[...], preferred_element_type=F32).astype(BF16)

    return pl.pallas_call(
        body, name="up_proj", grid=(nq, S // tm),
        in_specs=[pl.BlockSpec((tm, D_MODEL), lambda j, i: (i, 0)), pl.BlockSpec((None, D_MODEL, wq), lambda j, i: (j, 0, 0))],
        out_specs=pl.BlockSpec((tm, wq), lambda j, i: (i, j)),
        out_shape=jax.ShapeDtypeStruct((S, nq * wq), BF16),
        compiler_params=_params(("parallel", "parallel")),
    )(h2, w_up_q)


def _shift_down(a, halo, k):
    tm = a.shape[0]
    ra = pltpu.roll(a, k, 0)
    rh = pltpu.roll(halo, k, 0)
    row = lax.broadcasted_iota(jnp.int32, halo.shape, 0)
    top = jnp.where(row < k, rh, ra[0:SUBLANES])
    return jnp.concatenate([top, ra[SUBLANES:tm]], axis=0)


def _shift_up(a, halo, k):
    tm = a.shape[0]
    ra = pltpu.roll(a, tm - k, 0)
    rh = pltpu.roll(halo, SUBLANES - k, 0)
    row = lax.broadcasted_iota(jnp.int32, halo.shape, 0)
    bottom = jnp.where(row >= SUBLANES - k, rh, ra[tm - SUBLANES:tm])
    return jnp.concatenate([ra[0:tm - SUBLANES], bottom], axis=0)


def _shift_matrices(tm):
    row = lax.broadcasted_iota(jnp.int32, (tm, tm), 0)
    col = lax.broadcasted_iota(jnp.int32, (tm, tm), 1)
    return [(row == col + k).astype(BF16) for k in (1, 2)]


def _conv_taps(a, halo, first, shifts):
    tm = a.shape[0]
    halo = halo.astype(F32) * jnp.where(first, 0.0, 1.0)
    if shifts is None:
        a = a.astype(F32)
        return a, _shift_down(a, halo, 1), _shift_down(a, halo, 2)
    row8 = lax.broadcasted_iota(jnp.int32, halo.shape, 0)
    taps = [a.astype(F32)]
    for k, shift in zip((1, 2), shifts):
        down = jnp.dot(shift, a, preferred_element_type=F32)
        top = down[0:SUBLANES] + jnp.where(row8 < k, pltpu.roll(halo, k, 0), 0.0)
        taps.append(jnp.concatenate([top, down[SUBLANES:tm]], axis=0))
    return taps


def _conv_gate_val(refs, shifts, cols, first):
    ag_ref, av_ref, hg_ref, hv_ref, wg_ref, wv_ref, bg_ref, bv_ref = refs
    g0, g1, g2 = _conv_taps(ag_ref[:, cols], hg_ref[:, cols], first, shifts)
    gate = wg_ref[2:3, cols] * g0 + wg_ref[1:2, cols] * g1 + wg_ref[0:1, cols] * g2 + bg_ref[:, cols]
    v0, v1, v2 = _conv_taps(av_ref[:, cols], hv_ref[:, cols], first, shifts)
    val = wv_ref[2:3, cols] * v0 + wv_ref[1:2, cols] * v1 + wv_ref[0:1, cols] * v2 + bv_ref[:, cols]
    return gate, val, (g2, g1, g0), (v2, v1, v0)


_FF_CHUNKS = [slice(j * FF_CHUNK, (j + 1) * FF_CHUNK) for j in range(D_FF // FF_CHUNK)]


def _conv_specs(tm):
    step = tm // SUBLANES
    prev = lambda i: jnp.maximum(i * step - 1, 0)
    return [pl.BlockSpec((tm, D_FF), lambda i: (i, 0)), pl.BlockSpec((tm, D_FF), lambda i: (i, 1)),
            pl.BlockSpec((SUBLANES, D_FF), lambda i: (prev(i), 0)), pl.BlockSpec((SUBLANES, D_FF), lambda i: (prev(i), 1))]


def _ffn_fwd_loss(a, w_conv, b_conv, w_down, x1, g3, target, tm):
    S = x1.shape[0]

    def body(ag_ref, av_ref, hg_ref, hv_ref, wg_ref, wv_ref, bg_ref, bv_ref, wd_ref, x1_ref, g_ref, t_ref,
             dx2_ref, loss_ref, dg_ref):
        i = pl.program_id(0)

        @pl.when(i == 0)
        def _():
            loss_ref[...] = jnp.zeros_like(loss_ref)
            dg_ref[...] = jnp.zeros_like(dg_ref)

        x2 = x1_ref[...]
        for cols in _FF_CHUNKS:
            gate, val, _, _ = _conv_gate_val((ag_ref, av_ref, hg_ref, hv_ref, wg_ref, wv_ref, bg_ref, bv_ref), None, cols, i == 0)
            half = 0.5 * gate
            y = ((half + half * jnp.tanh(half)) * val).astype(BF16)
            x2 = x2 + jnp.dot(y, wd_ref[cols, :], preferred_element_type=F32)
        r = lax.rsqrt(jnp.mean(x2 * x2, axis=-1, keepdims=True) + EPS)
        xhat = x2 * r
        gg = g_ref[...]
        err = xhat * gg - t_ref[...]
        loss_ref[...] += jnp.sum(err * err, axis=0, keepdims=True)
        dy = err * (1.0 / D_MODEL)
        dg_ref[...] += jnp.sum(dy * xhat, axis=0, keepdims=True)
        dxhat = dy * gg
        dx2_ref[...] = r * (dxhat - xhat * jnp.mean(dxhat * xhat, axis=-1, keepdims=True))

    row = lambda w: pl.BlockSpec((tm, w), lambda i: (i, 0))
    half = lambda r: [pl.BlockSpec((r, D_FF), lambda i: (0, 0)), pl.BlockSpec((r, D_FF), lambda i: (0, 1))]
    return pl.pallas_call(
        body, name="ffn_fwd_loss", grid=(S // tm,),
        in_specs=_conv_specs(tm) + half(3) + half(1) + [_full((D_FF, D_MODEL)), row(D_MODEL), _full((1, D_MODEL)), row(D_MODEL)],
        out_specs=[row(D_MODEL), _full((1, D_MODEL)), _full((1, D_MODEL))],
        out_shape=[jax.ShapeDtypeStruct((S, D_MODEL), F32), jax.ShapeDtypeStruct((1, D_MODEL), F32),
                   jax.ShapeDtypeStruct((1, D_MODEL), F32)],
        compiler_params=_params(("arbitrary",)),
    )(a, a, a, a, w_conv, w_conv, b_conv, b_conv, w_down, x1, g3, target)


def _ffn_bwd_gate(dx2, a, w_conv, b_conv, w_down, tm):
    S = dx2.shape[0]

    def body(dx_ref, ag_ref, av_ref, hg_ref, hv_ref, wg_ref, wv_ref, bg_ref, bv_ref, wd_ref,
             dc_ref, y_ref, dw_ref, db_ref):
        i = pl.program_id(0)

        @pl.when(i == 0)
        def _():
            dw_ref[...] = jnp.zeros_like(dw_ref)
            db_ref[...] = jnp.zeros_like(db_ref)

        dx = dx_ref[...].astype(BF16)
        shifts = _shift_matrices(tm)
        for cols in _FF_CHUNKS:
            gate, val, gtaps, vtaps = _conv_gate_val((ag_ref, av_ref, hg_ref, hv_ref, wg_ref, wv_ref, bg_ref, bv_ref), shifts, cols, i == 0)
            sg = _sigmoid(gate)
            act = gate * sg
            y_ref[:, cols] = (act * val).astype(BF16)
            dy = lax.dot_general(dx, wd_ref[cols, :], NT, preferred_element_type=F32)
            dgate = dy * val * (sg + act - act * sg)
            dval = dy * act
            for d, taps, out in ((dgate, gtaps, cols), (dval, vtaps, slice(D_FF + cols.start, D_FF + cols.stop))):
                dc_ref[:, out] = d.astype(BF16)
                db_ref[0:1, out] += jnp.sum(d, axis=0, keepdims=True)
                for j in range(3):
                    dw_ref[j:j + 1, out] += jnp.sum(d * taps[j], axis=0, keepdims=True)

    row = lambda w: pl.BlockSpec((tm, w), lambda i: (i, 0))
    half = lambda r: [pl.BlockSpec((r, D_FF), lambda i: (0, 0)), pl.BlockSpec((r, D_FF), lambda i: (0, 1))]
    return pl.pallas_call(
        body, name="ffn_bwd_gate", grid=(S // tm,),
        in_specs=[row(D_MODEL)] + _conv_specs(tm) + half(3) + half(1) + [_full((D_FF, D_MODEL))],
        out_specs=[row(2 * D_FF), row(D_FF), _full((SUBLANES, 2 * D_FF)), _full((1, 2 * D_FF))],
        out_shape=[jax.ShapeDtypeStruct((S, 2 * D_FF), BF16), jax.ShapeDtypeStruct((S, D_FF), BF16),
                   jax.ShapeDtypeStruct((SUBLANES, 2 * D_FF), F32), jax.ShapeDtypeStruct((1, 2 * D_FF), F32)],
        compiler_params=_params(("arbitrary",)),
    )(dx2, a, a, a, a, w_conv, w_conv, b_conv, b_conv, w_down)


def _conv_bwd(dc, w_conv, tm, tn):
    S, C = dc.shape
    step = tm // SUBLANES
    last_blk = S // SUBLANES - 1

    def body(d_ref, nx_ref, w_ref, o_ref):
        last = pl.program_id(0) == pl.num_programs(0) - 1
        row = lax.broadcasted_iota(jnp.int32, (tm, tm), 0)
        col = lax.broadcasted_iota(jnp.int32, (tm, tm), 1)
        row8 = lax.broadcasted_iota(jnp.int32, (SUBLANES, FF_CHUNK), 0)
        ups = [(row + k == col).astype(BF16) for k in (1, 2)]
        for c0 in range(0, tn, FF_CHUNK):
            cols = slice(c0, c0 + FF_CHUNK)
            d = d_ref[:, cols]
            nx = nx_ref[:, cols].astype(F32) * jnp.where(last, 0.0, 1.0)
            out = w_ref[2:3, cols] * d.astype(F32)
            for k, up in zip((1, 2), ups):
                moved = jnp.dot(up, d, preferred_element_type=F32)
                bottom = moved[tm - SUBLANES:tm] + jnp.where(row8 >= SUBLANES - k, pltpu.roll(nx, SUBLANES - k, 0), 0.0)
                out = out + w_ref[2 - k:3 - k, cols] * jnp.concatenate([moved[0:tm - SUBLANES], bottom], axis=0)
            o_ref[:, cols] = out.astype(BF16)

    return pl.pallas_call(
        body, name="conv_bwd", grid=(S // tm, C // tn),
        in_specs=[pl.BlockSpec((tm, tn), lambda i, j: (i, j)),
                  pl.BlockSpec((SUBLANES, tn), lambda i, j: (jnp.minimum((i + 1) * step, last_blk), j)),
                  pl.BlockSpec((3, tn), lambda i, j: (0, j))],
        out_specs=pl.BlockSpec((tm, tn), lambda i, j: (i, j)),
        out_shape=jax.ShapeDtypeStruct((S, C), BF16),
        compiler_params=_params(("parallel", "parallel")),
    )(dc, dc, w_conv)


def _matmul_tn(a, b, name, bm, bn, tk, col_a=0, col_b=0, quarters=None):
    S = a.shape[0]
    gm, gn = quarters if quarters else (1, 1)
    nk = S // tk

    def body(a_ref, b_ref, o_ref):
        @pl.when(pl.program_id(2) == 0)
        def _():
            o_ref[...] = jnp.zeros_like(o_ref)

        o_ref[...] += lax.dot_general(a_ref[...].astype(BF16), b_ref[...].astype(BF16), TN, preferred_element_type=F32)

    if quarters and gn > 1:
        out_spec = pl.BlockSpec((None, bm, bn), lambda i, j, k: (j, i, 0))
        out_shape = jax.ShapeDtypeStruct((gn, gm * bm, bn), F32)
    else:
        out_spec = pl.BlockSpec((bm, bn), lambda i, j, k: (i, j))
        out_shape = jax.ShapeDtypeStruct((gm * bm, gn * bn), F32)
    return pl.pallas_call(
        body, name=name, grid=(gm, gn, nk),
        in_specs=[pl.BlockSpec((tk, bm), lambda i, j, k: (k, col_a * gm + i)),
                  pl.BlockSpec((tk, bn), lambda i, j, k: (k, col_b * gn + j))],
        out_specs=out_spec, out_shape=out_shape,
        compiler_params=_params(("parallel", "parallel", "arbitrary")),
    )(a, b)


def _dw_out(out_a, out_b, dx1, tk):
    S = dx1.shape[0]

    def body(a_ref, b_ref, d_ref, o_ref):
        @pl.when(pl.program_id(0) == 0)
        def _():
            o_ref[...] = jnp.zeros_like(o_ref)

        d = d_ref[...].astype(BF16)
        o_ref[:D_HEADS, :] += lax.dot_general(a_ref[...], d, TN, preferred_element_type=F32)
        o_ref[D_HEADS:, :] += lax.dot_general(b_ref[...], d, TN, preferred_element_type=F32)

    row = lambda w: pl.BlockSpec((tk, w), lambda k: (k, 0))
    return pl.pallas_call(
        body, name="dw_out", grid=(S // tk,),
        in_specs=[row(D_HEADS), row(D_HEADS), row(D_MODEL)], out_specs=_full((D_MODEL, D_MODEL)),
        out_shape=jax.ShapeDtypeStruct((D_MODEL, D_MODEL), F32),
        compiler_params=_params(("arbitrary",)),
    )(out_a, out_b, dx1)


def _up_bwd(dact, w_up_q, x1, g2, dx2, tm):
    S = x1.shape[0]
    nq, _, wq = w_up_q.shape

    def body(d_ref, w_ref, x_ref, g_ref, dx2_ref, dx1_ref, dg_ref):
        @pl.when(pl.program_id(0) == 0)
        def _():
            dg_ref[...] = jnp.zeros_like(dg_ref)

        dh = jnp.zeros((tm, D_MODEL), F32)
        for j in range(nq):
            dh = dh + lax.dot_general(d_ref[:, j * wq:(j + 1) * wq], w_ref[j], NT, preferred_element_type=F32)
        dx, dg = _rms_bwd(dh, x_ref[...], g_ref[...])
        dg_ref[...] += dg
        dx1_ref[...] = dx2_ref[...] + dx

    row = lambda w: pl.BlockSpec((tm, w), lambda i: (i, 0))
    return pl.pallas_call(
        body, name="up_bwd", grid=(S // tm,),
        in_specs=[row(nq * wq), pl.BlockSpec((nq, D_MODEL, wq), lambda i: (0, 0, 0), pipeline_mode=pl.Buffered(1)),
                  row(D_MODEL), _full((1, D_MODEL)), row(D_MODEL)],
        out_specs=[row(D_MODEL), _full((1, D_MODEL))],
        out_shape=[jax.ShapeDtypeStruct((S, D_MODEL), F32), jax.ShapeDtypeStruct((1, D_MODEL), F32)],
        compiler_params=_params(("arbitrary",)),
    )(dact, w_up_q, x1, g2, dx2)


def _out_bwd(dx1, w_out, tm):
    S = dx1.shape[0]

    def body(d_ref, w_ref, o_ref):
        o_ref[...] = lax.dot_general(d_ref[...].astype(BF16), w_ref[...], NT, preferred_element_type=F32).astype(BF16)

    return pl.pallas_call(
        body, name="out_bwd", grid=(S // tm,),
        in_specs=[pl.BlockSpec((tm, D_MODEL), lambda i: (i, 0)), _full((D_MODEL, D_MODEL))],
        out_specs=pl.BlockSpec((tm, D_MODEL), lambda i: (i, 0)),
        out_shape=jax.ShapeDtypeStruct((S, D_MODEL), BF16),
        compiler_params=_params(("parallel",)),
    )(dx1, w_out)


def _gate_bwd(z, dcat, w_mask, w_mask_t, ln_row, b_full, seg_avg, head_ind, tm, swap):
    S = z.shape[0]
    nb = tm // SG_BLOCK
    ns = len(swap)

    def body(zu_ref, zv_ref, do_ref, w_ref, wt_ref, ln_ref, b_ref, avg_ref, ind_ref, *rest):
        dzu_ref, dzv_ref, dw_ref, db_ref, dln_ref = rest[ns:ns + 5]
        dvn_s, dbf_s = rest[2 * ns + 5:2 * ns + 7]
        swap_start, swap_finish = _swap_ops(rest[:ns], rest[ns + 5:2 * ns + 5], *rest[2 * ns + 7:])
        i = pl.program_id(0)

        @pl.when(i == 0)
        def _():
            swap_start()
            dw_ref[...] = jnp.zeros_like(dw_ref)
            dln_ref[...] = jnp.zeros_like(dln_ref)
            dbf_s[...] = jnp.zeros_like(dbf_s)

        zu = zu_ref[...].astype(F32)
        zv = zv_ref[...].astype(F32)
        u = _gelu(zu)
        v = _gelu(zv)
        avg = avg_ref[...]
        vhat, rstd = _layer_norm_heads(v, avg)
        ln = ln_ref[...]
        vn = vhat * ln
        for b in range(nb):
            rows = slice(b * SG_BLOCK, (b + 1) * SG_BLOCK)
            vn_b = vn[rows]
            mixed = _gate_mix(vn_b, w_ref, b_ref[...])
            do = do_ref[rows, :].astype(F32)
            dzu_ref[rows, :] = (do * mixed * _gelu_grad(zu[rows])).astype(BF16)
            dmix = do * u[rows]
            dbf_s[...] += dmix
            vn_bf = vn_b.astype(BF16)
            dvn = jnp.zeros((SG_BLOCK, D_HEADS), F32)
            for h in range(N_HEADS):
                dmh = jnp.where(_head_mask(h, SG_BLOCK), dmix, 0.0).astype(BF16)
                dw_ref[h] += lax.dot_general(dmh, vn_bf, NT, preferred_element_type=F32)
                dvn = dvn + jnp.dot(wt_ref[h], dmh, preferred_element_type=F32)
            dvn_s[rows, :] = dvn
        dvn = dvn_s[...]
        dln_ref[...] += jnp.sum(dvn * vhat, axis=0, keepdims=True)
        dvhat = dvn * ln
        dv = rstd * (dvhat - _split_dot(dvhat, avg) - vhat * _split_dot(dvhat * vhat, avg))
        dzv_ref[...] = (dv * _gelu_grad(zv)).astype(BF16)

        @pl.when(i == pl.num_programs(0) - 1)
        def _():
            r = lax.broadcasted_iota(jnp.int32, (SG_BLOCK, SG_BLOCK), 0) // CHUNK
            s = lax.broadcasted_iota(jnp.int32, (SG_BLOCK, SG_BLOCK), 1) // CHUNK
            for h in range(N_HEADS):
                dw_ref[h] = jnp.where(r >= s, dw_ref[h], 0.0)
            db_ref[...] = _split_dot(dbf_s[...], ind_ref[...])
            swap_finish()

    row = lambda col: pl.BlockSpec((tm, D_HEADS), lambda i: (i, col))
    wspec = _full((N_HEADS, SG_BLOCK, SG_BLOCK))
    out = pl.pallas_call(
        body, name="gate_bwd", grid=(S // tm,),
        in_specs=[row(0), row(1), row(0), wspec, wspec, _full((1, D_HEADS)), _full((SG_BLOCK, D_HEADS)),
                  _full((D_HEADS, D_HEADS)), _full((D_HEADS, LANES))] + [_ANY] * ns,
        out_specs=[row(0), row(0), wspec, _full((SG_BLOCK, LANES)), _full((1, D_HEADS))] + [_ANY] * ns,
        out_shape=[jax.ShapeDtypeStruct((S, D_HEADS), BF16), jax.ShapeDtypeStruct((S, D_HEADS), BF16),
                   jax.ShapeDtypeStruct((N_HEADS, SG_BLOCK, SG_BLOCK), F32), jax.ShapeDtypeStruct((SG_BLOCK, LANES), F32),
                   jax.ShapeDtypeStruct((1, D_HEADS), F32)] + _swap_shapes(swap),
        scratch_shapes=[pltpu.VMEM((tm, D_HEADS), F32), pltpu.VMEM((SG_BLOCK, D_HEADS), F32)] + _swap_sems(ns),
        compiler_params=_params(("arbitrary",)),
    )(z, z, dcat, w_mask, w_mask_t, ln_row, b_full, seg_avg, head_ind, *swap)
    return out[:5], out[5:]


def _attn_pack_grad(o, dcat, qa, lse, head_ind, k, tm):
    S = o.shape[0]

    def body(o_ref, do_ref, qa_ref, lse_ref, ind_ref, pl_ref, pt_ref, eye_ref, ds_ref, dst_ref, ls_ref, lst_ref,
             dop_ref, qb_ref, dot_ref, qbt_ref):
        do = do_ref[...]
        delta = _split_dot(o_ref[...].astype(F32) * do.astype(F32), ind_ref[...])
        hi = delta.astype(BF16).astype(F32)
        parts = (hi + pltpu.roll((delta - hi).astype(BF16).astype(F32), N_HEADS, 1)).astype(BF16)
        dop = jnp.dot(do, pl_ref[...], preferred_element_type=F32) - jnp.dot(parts, ds_ref[...], preferred_element_type=F32)
        for g in range(GROUPS):
            dop_ref[g] = dop[:, g * GROUP_PAD:(g + 1) * GROUP_PAD].astype(BF16)
        dot = lax.dot_general(pt_ref[...], do, NT, preferred_element_type=F32)
        dot_ref[...] = (dot - lax.dot_general(dst_ref[...], parts, NT, preferred_element_type=F32)).astype(BF16)
        qa = qa_ref[...]
        stack = jnp.concatenate(_split3(lse_ref[...]), axis=0)
        qb = qa.astype(F32) - lax.dot_general(stack, ls_ref[...], TN, preferred_element_type=F32)
        qbt = lax.dot_general(eye_ref[...], qa, NT, preferred_element_type=F32)
        qbt = qbt - jnp.dot(lst_ref[...], stack, preferred_element_type=F32)
        for g in range(GROUPS):
            qb_ref[g] = qb[:, g * GROUP_PAD:(g + 1) * GROUP_PAD].astype(BF16)
        qbt_ref[...] = qbt.astype(BF16)

    pad = pl.BlockSpec((tm, D_PAD), lambda i: (i, 0))
    padt = pl.BlockSpec((None, D_PAD, tm), lambda i: (i, 0, 0))
    return pl.pallas_call(
        body, name="attn_pack_grad", grid=(S // tm,),
        in_specs=[pl.BlockSpec((tm, D_HEADS), lambda i: (i, 0)), pl.BlockSpec((tm, D_HEADS), lambda i: (i, 1)), pad,
                  pl.BlockSpec((STAT_ROWS, tm), lambda i: (0, i)), _full((D_HEADS, LANES)), _full((D_HEADS, D_PAD)),
                  _full((D_PAD, D_HEADS)), _full((D_PAD, D_PAD)), _full((LANES, D_PAD)), _full((D_PAD, LANES)),
                  _full((3 * STAT_ROWS, D_PAD)), _full((D_PAD, 3 * STAT_ROWS))],
        out_specs=[pl.BlockSpec((GROUPS, tm, GROUP_PAD), lambda i: (0, i, 0))] * 2 + [padt, padt],
        out_shape=[jax.ShapeDtypeStruct((GROUPS, S, GROUP_PAD), BF16)] * 2 + [jax.ShapeDtypeStruct((S // tm, D_PAD, tm), BF16)] * 2,
        compiler_params=_params(("parallel",)),
    )(o, dcat, qa, lse, head_ind, k["place"], k["place_t"], jnp.eye(D_PAD, dtype=BF16), k["d_stat"], k["d_stat"].T,
      k["l_stat"], k["l_stat"].T)


def _attn_bwd(qb, qbt, ka, va, dop, dopt, k, tq, sums16):
    S = ka.shape[0]
    n = S // tq
    ns = len(sums16)

    pairs = [(kb, q) for kb in range(n) for q in range(kb, n)]
    k_of = jnp.asarray([kb for kb, _ in pairs], jnp.int32)
    q_of = jnp.asarray([q for _, q in pairs], jnp.int32)

    def body(k_of_ref, q_of_ref, q_ref, qt_ref, k_ref, v_ref, do_ref, dot_ref, pt_ref, *rest):
        dq_hbm, dcr_hbm, dk_ref, dv_ref, dcc_ref = rest[ns:ns + 5]
        dq_s, dcr_s, dk_s, dv_s, dcc_s = rest[2 * ns + 5:2 * ns + 10]
        s_s, d_s = rest[2 * ns + 10:2 * ns + 12], rest[2 * ns + 12:2 * ns + 14]
        sems = rest[2 * ns + 14]
        scatter_start, scatter_finish = _scatter_ops(rest[:ns], rest[ns + 5:2 * ns + 5], *rest[2 * ns + 15:])
        g = pl.program_id(0)
        ki, qi = k_of_ref[pl.program_id(1)], q_of_ref[pl.program_id(1)]

        @pl.when((g == 0) & (ki == 0) & (qi == 0))
        def _():
            scatter_start()

        @pl.when((ki == 0) & (qi == 0))
        def _():
            dq_s[...] = jnp.zeros_like(dq_s)
            dcr_s[...] = jnp.zeros_like(dcr_s)

        @pl.when(qi == ki)
        def _():
            dk_s[...] = jnp.zeros_like(dk_s)
            dv_s[...] = jnp.zeros_like(dv_s)
            dcc_s[...] = jnp.zeros_like(dcc_s)

        def step(diagonal):
            chunks = [slice(c * KEY_CHUNK, (c + 1) * KEY_CHUNK) for c in range(tq // KEY_CHUNK)]

            def keys_of(rows):
                return slice(0, rows.stop) if diagonal else slice(0, tq)

            def scores(hh, rows, slot):
                sl, keys = slice(hh * HEAD_PAD, (hh + 1) * HEAD_PAD), keys_of(rows)
                s_s[slot][rows, keys] = lax.dot_general(q_ref[rows, sl], k_ref[keys, sl], NT, preferred_element_type=F32)
                d_s[slot][rows, keys] = lax.dot_general(do_ref[rows, sl], v_ref[keys, sl], NT, preferred_element_type=F32)

            for rows in chunks:
                scores(0, rows, 0)
            for hh in range(GROUP_HEADS):
                sl = slice(hh * HEAD_PAD, (hh + 1) * HEAD_PAD)
                slot = hh % 2
                for rows in chunks:
                    if hh + 1 < GROUP_HEADS:
                        scores(hh + 1, rows, 1 - slot)
                    keys = keys_of(rows)
                    p = jnp.exp2(s_s[slot][rows, keys])
                    if diagonal:
                        row = rows.start + lax.broadcasted_iota(jnp.int32, (KEY_CHUNK, keys.stop), 0)
                        col = lax.broadcasted_iota(jnp.int32, (KEY_CHUNK, keys.stop), 1)
                        p = jnp.where(row >= col, p, 0.0)
                    ds = p * d_s[slot][rows, keys]
                    qrows = pl.ds(pl.multiple_of(qi * tq + rows.start, KEY_CHUNK), KEY_CHUNK)
                    dcc_s[hh:hh + 1, keys] += jnp.sum(ds, axis=0, keepdims=True)
                    dcr_s[qrows, hh:hh + 1] += jnp.sum(ds, axis=1, keepdims=True)
                    ds = ds.astype(BF16)
                    dv_s[sl, keys] += jnp.dot(dot_ref[sl, rows], p.astype(BF16), preferred_element_type=F32)
                    dk_s[sl, keys] += jnp.dot(qt_ref[sl, rows], ds, preferred_element_type=F32)
                    dq_s[qrows, sl] += jnp.dot(ds, k_ref[keys, sl], preferred_element_type=F32)

        @pl.when(qi > ki)
        def _():
            step(False)

        @pl.when(qi == ki)
        def _():
            step(True)

        @pl.when(qi == n - 1)
        def _():
            dk = dk_s[...]
            pt = pt_ref[...]
            dk_ref[...] = lax.dot_general((dk * (1.0 / LOG2E)).astype(BF16), pt, TN, preferred_element_type=F32).astype(BF16)
            dv_ref[...] = lax.dot_general(dv_s[...].astype(BF16), pt, TN, preferred_element_type=F32).astype(BF16)
            dcc_ref[...] = dcc_s[...]

        @pl.when((ki == n - 1) & (qi == n - 1))
        def _():
            copies = [pltpu.make_async_copy(dq_s, dq_hbm.at[g], sems.at[0]), pltpu.make_async_copy(dcr_s, dcr_hbm.at[g], sems.at[1])]
            for cp in copies:
                cp.start()
            for cp in copies:
                cp.wait()

        @pl.when((g == GROUPS - 1) & (ki == n - 1) & (qi == n - 1))
        def _():
            scatter_finish()

    gw = GROUP_HEADS * HEAD_DIM
    qspec = pl.BlockSpec((None, tq, GROUP_PAD), lambda g, i, ks, qs: (g, qs[i], 0))
    qtspec = pl.BlockSpec((None, GROUP_PAD, tq), lambda g, i, ks, qs: (qs[i], g, 0))
    kspec = pl.BlockSpec((tq, GROUP_PAD), lambda g, i, ks, qs: (ks[i], g))
    kout = pl.BlockSpec((tq, gw), lambda g, i, ks, qs: (ks[i], g))
    out = pl.pallas_call(
        body, name="attn_bwd",
        grid_spec=pltpu.PrefetchScalarGridSpec(
            num_scalar_prefetch=2, grid=(GROUPS, len(pairs)),
            in_specs=[qspec, qtspec, kspec, kspec, qspec, qtspec, pl.BlockSpec((GROUP_PAD, gw), lambda g, i, ks, qs: (0, 0))]
            + [_ANY] * ns,
            out_specs=[_ANY, _ANY, kout, kout, pl.BlockSpec((None, SUBLANES, tq), lambda g, i, ks, qs: (g, 0, ks[i]))] + [_ANY] * ns,
            scratch_shapes=[pltpu.VMEM((S, GROUP_PAD), F32), pltpu.VMEM((S, LANES), F32), pltpu.VMEM((GROUP_PAD, tq), F32),
                            pltpu.VMEM((GROUP_PAD, tq), F32), pltpu.VMEM((SUBLANES, tq), F32),
                            pltpu.VMEM((tq, tq), F32), pltpu.VMEM((tq, tq), F32), pltpu.VMEM((tq, tq), F32),
                            pltpu.VMEM((tq, tq), F32), pltpu.SemaphoreType.DMA((2,))]
            + _scatter_sems(ns)),
        out_shape=[jax.ShapeDtypeStruct((GROUPS, S, GROUP_PAD), F32), jax.ShapeDtypeStruct((GROUPS, S, LANES), F32),
                   jax.ShapeDtypeStruct((S, D_HEADS), BF16), jax.ShapeDtypeStruct((S, D_HEADS), BF16),
                   jax.ShapeDtypeStruct((GROUPS, SUBLANES, S), F32)]
        + _scatter_shapes(sums16),
        compiler_params=_params(("arbitrary", "arbitrary")),
    )(k_of, q_of, qb, qbt, ka, va, dop, dopt, k["place_t_group"], *sums16)
    return out[0], out[1], out[2], out[3], out[4], out[5:]


def _attn_unpack(dqp, k, tm):
    S = dqp.shape[1]
    gw = GROUP_HEADS * HEAD_DIM

    def body(dqp_ref, pt_ref, dq_ref):
        for g in range(GROUPS):
            dq_ref[:, g * gw:(g + 1) * gw] = jnp.dot((dqp_ref[g] * SCALE).astype(BF16), pt_ref[...],
                                                     preferred_element_type=F32).astype(BF16)

    return pl.pallas_call(
        body, name="attn_unpack", grid=(S // tm,),
        in_specs=[pl.BlockSpec((GROUPS, tm, GROUP_PAD), lambda i: (0, i, 0)), _full((GROUP_PAD, gw))],
        out_specs=pl.BlockSpec((tm, D_HEADS), lambda i: (i, 0)),
        out_shape=jax.ShapeDtypeStruct((S, D_HEADS), BF16),
        compiler_params=_params(("parallel",)),
    )(dqp, k["place_t_group"])


def _fox_bwd(dc, f, bias_row, tb):
    S = f.shape[0]
    nb = S // tb

    def body(dc_ref, f_ref, b_ref, df_ref, dbias_ref, carry):
        @pl.when(pl.program_id(0) == 0)
        def _():
            carry[...] = jnp.zeros_like(carry)
            dbias_ref[...] = jnp.zeros_like(dbias_ref)

        r = lax.broadcasted_iota(jnp.int32, (tb, tb), 0)
        s = lax.broadcasted_iota(jnp.int32, (tb, tb), 1)
        tri = (s >= r).astype(F32)
        rc = jnp.dot(tri, dc_ref[...], precision=lax.Precision.HIGHEST, preferred_element_type=F32) + carry[0:1, :]
        carry[...] = jnp.broadcast_to(rc[0:1, :], carry.shape)
        lane = lax.broadcasted_iota(jnp.int32, (tb, LANES), 1)
        df = jnp.where(lane < N_HEADS, rc * jax.nn.sigmoid(-(f_ref[...] + b_ref[...])), 0.0)
        df_ref[...] = df.astype(BF16)
        dbias_ref[...] += jnp.sum(df, axis=0, keepdims=True)

    rev = pl.BlockSpec((tb, LANES), lambda i: (nb - 1 - i, 0))
    return pl.pallas_call(
        body, name="fox_bwd", grid=(nb,),
        in_specs=[rev, rev, _full((1, LANES))],
        out_specs=[rev, _full((1, LANES))],
        out_shape=[jax.ShapeDtypeStruct((S, LANES), BF16), jax.ShapeDtypeStruct((1, LANES), F32)],
        scratch_shapes=[pltpu.VMEM((SUBLANES, LANES), F32)],
        compiler_params=_params(("arbitrary",)),
    )(dc, f, bias_row)


_DZ_WIDTHS = (D_HEADS,) * 5 + (LANES,)


def _in_bwd(pieces, w_in, x, g1, dx1, tm, sums16):
    S = x.shape[0]
    ns = len(sums16)

    def body(*refs):
        p_refs, (w_ref, x_ref, g_ref, dx1_ref) = refs[:6], refs[6:10]
        dx_ref, dg_ref = refs[10 + ns:12 + ns]
        scatter_start, scatter_finish = _scatter_ops(refs[10:10 + ns], refs[12 + ns:12 + 2 * ns], *refs[12 + 2 * ns:])

        @pl.when(pl.program_id(0) == 0)
        def _():
            dg_ref[...] = jnp.zeros_like(dg_ref)
            scatter_start()

        dh = jnp.zeros((tm, D_MODEL), F32)
        off = 0
        for p_ref, w in zip(p_refs, _DZ_WIDTHS):
            dh = dh + lax.dot_general(p_ref[...].astype(BF16), w_ref[:, off:off + w], NT, preferred_element_type=F32)
            off += w
        dx, dg = _rms_bwd(dh, x_ref[...], g_ref[...])
        dg_ref[...] += dg
        dx_ref[...] = dx1_ref[...] + dx

        @pl.when(pl.program_id(0) == pl.num_programs(0) - 1)
        def _():
            scatter_finish()

    row = lambda w: pl.BlockSpec((tm, w), lambda i: (i, 0))
    out = pl.pallas_call(
        body, name="in_bwd", grid=(S // tm,),
        in_specs=[row(w) for w in _DZ_WIDTHS] + [_full((D_MODEL, D_IN_PAD)), row(D_MODEL), _full((1, D_MODEL)), row(D_MODEL)]
        + [_ANY] * ns,
        out_specs=[row(D_MODEL), _full((1, D_MODEL))] + [_ANY] * ns,
        out_shape=[jax.ShapeDtypeStruct((S, D_MODEL), F32), jax.ShapeDtypeStruct((1, D_MODEL), F32)] + _scatter_shapes(sums16),
        scratch_shapes=_scatter_sems(ns),
        compiler_params=_params(("arbitrary",)),
    )(*pieces, w_in, x, g1, dx1, *sums16)
    return out[0], out[1], out[2:]


def _dw_in(h1, pieces, tk):
    S = h1.shape[0]

    def body(*refs):
        h_ref, p_refs, o_ref = refs[0], refs[1:7], refs[7]

        @pl.when(pl.program_id(0) == 0)
        def _():
            o_ref[...] = jnp.zeros_like(o_ref)

        off = 0
        for p_ref, w in zip(p_refs, _DZ_WIDTHS):
            o_ref[:, off:off + w] += lax.dot_general(h_ref[...], p_ref[...].astype(BF16), TN, preferred_element_type=F32)
            off += w

    row = lambda w: pl.BlockSpec((tk, w), lambda k: (k, 0))
    return pl.pallas_call(
        body, name="dw_in", grid=(S // tk,),
        in_specs=[row(D_MODEL)] + [row(w) for w in _DZ_WIDTHS],
        out_specs=_full((D_MODEL, D_IN_PAD)),
        out_shape=jax.ShapeDtypeStruct((D_MODEL, D_IN_PAD), F32),
        compiler_params=_params(("arbitrary",)),
    )(h1, *pieces)


def _adamw_math(w, g, m, v):
    m = ADAM_B1 * m + (1.0 - ADAM_B1) * g
    v = ADAM_B2 * v + (1.0 - ADAM_B2) * (g * g)
    m_hat = m / (1.0 - ADAM_B1 ** ADAM_STEP)
    v_hat = v / (1.0 - ADAM_B2 ** ADAM_STEP)
    delta = -ADAM_LR * (m_hat / (jnp.sqrt(v_hat) + ADAM_EPS) + ADAM_WD * w)
    return delta, m, v


def _adamw(name, w, g, m, v):
    R, C = w.shape
    tr = _row_tile(R, 256)

    def body(w_ref, g_ref, m_ref, v_ref, go_ref, d_ref, nm_ref, nv_ref):
        g = g_ref[...]
        d, nm, nv = _adamw_math(w_ref[...], g, m_ref[...], v_ref[...])
        go_ref[...] = g
        d_ref[...] = d
        nm_ref[...] = nm
        nv_ref[...] = nv

    spec = pl.BlockSpec((tr, C), lambda i: (i, 0))
    return pl.pallas_call(
        body, name=name, grid=(R // tr,), in_specs=[spec] * 4, out_specs=[spec] * 4,
        out_shape=[jax.ShapeDtypeStruct((R, C), F32)] * 4,
        compiler_params=_params(("parallel",)),
    )(w, g, m, v)


def _pair_sum(name, grad, theirs, ids):
    q, half, C = theirs.shape
    tr = _row_tile(half, 256)
    nb = half // tr

    def body(ids_ref, a_ref, b_ref, sb_ref):
        sb_ref[...] = (a_ref[...] + b_ref[...]).astype(BF16)

    here = pl.BlockSpec((None, tr, C), lambda j, i, ids: (j, i, 0))
    return pl.pallas_call(
        body, name=name,
        grid_spec=pltpu.PrefetchScalarGridSpec(
            num_scalar_prefetch=1, grid=(q, nb),
            in_specs=[pl.BlockSpec((None, tr, C), lambda j, i, ids: (j, ids[1] * nb + i, 0)), here],
            out_specs=here),
        out_shape=jax.ShapeDtypeStruct((q, half, C), BF16),
        compiler_params=_params(("parallel", "parallel")),
    )(ids, grad, theirs)


def _chip_sum(name, grad, theirs, others, ids):
    _, half, C = theirs.shape
    tr = _row_tile(half, 256)
    nb = half // tr

    def body(ids_ref, a_ref, b_ref, o_ref, s_ref):
        s = a_ref[...] + b_ref[...]
        for j in range(3):
            s = s + o_ref[j].astype(F32)
        s_ref[...] = s

    return pl.pallas_call(
        body, name=name,
        grid_spec=pltpu.PrefetchScalarGridSpec(
            num_scalar_prefetch=1, grid=(nb,),
            in_specs=[pl.BlockSpec((None, tr, C), lambda i, ids: (ids[0], ids[1] * nb + i, 0)),
                      pl.BlockSpec((None, tr, C), lambda i, ids: (ids[0], i, 0)),
                      pl.BlockSpec((3, tr, C), lambda i, ids: (0, i, 0))],
            out_specs=pl.BlockSpec((tr, C), lambda i, ids: (ids[1] * nb + i, 0))),
        out_shape=jax.ShapeDtypeStruct((2 * half, C), F32),
        compiler_params=_params(("parallel",)),
    )(ids, grad, theirs, others)


def _place():
    return lax.axis_index("x"), lax.axis_index("y"), lax.axis_index("c")


def _other_chips(x, y):
    return [(1 - x, y), (x, 1 - y), (1 - x, 1 - y)]


_ANY = pl.BlockSpec(memory_space=pl.ANY)


def _gather_quarters(shards):
    n = len(shards)

    def body(*refs):
        start, hand_on, finish = _gather_ops(refs[:n], refs[n:2 * n], *refs[2 * n:])
        start()
        hand_on()
        finish()

    return pl.pallas_call(
        body, name="gather_weights",
        in_specs=[_ANY] * n, out_specs=[_ANY] * n,
        out_shape=_gather_shapes(shards), scratch_shapes=_gather_sems(n),
    )(*shards)


def _gather_shapes(shards):
    return [jax.ShapeDtypeStruct((4,) + s.shape, s.dtype) for s in shards]


def _gather_sems(n):
    return [pltpu.SemaphoreType.DMA((n, 3))] * 4 + [pltpu.SemaphoreType.DMA((n,))]


def _gather_ops(ins, outs, send_sems, recv_sems, pass_send_sems, pass_recv_sems, own_sems):
    n = len(ins)
    halved = [r.shape[0] % 32 == 0 for r in ins]

    def part(a, quarter, core):
        if not halved[a]:
            return outs[a].at[quarter]
        half = ins[a].shape[0] // 2
        return outs[a].at[quarter, pl.ds(core * half, half), :]

    def ici(a, j, quarter):
        x, y, c = _place()
        px, py = _other_chips(x, y)[j]
        src = ins[a]
        if halved[a]:
            half = src.shape[0] // 2
            src = src.at[pl.ds(c * half, half), :]
        return pltpu.make_async_remote_copy(src_ref=src, dst_ref=part(a, quarter, c), send_sem=send_sems.at[a, j],
                                            recv_sem=recv_sems.at[a, j], device_id=(px, py, c), device_id_type=MESH)

    def passed(a, j, core):
        x, y, c = _place()
        px, py = _other_chips(x, y)[j]
        half = part(a, 2 * px + py, core)
        return pltpu.make_async_remote_copy(src_ref=half, dst_ref=half, send_sem=pass_send_sems.at[a, j],
                                            recv_sem=pass_recv_sems.at[a, j], device_id=(x, y, 1 - c), device_id_type=MESH)

    def own(a):
        x, y, _ = _place()
        return pltpu.make_async_copy(ins[a], outs[a].at[2 * x + y], own_sems.at[a])

    def start():
        x, y, _ = _place()
        for a in range(n):
            for j in range(3):
                ici(a, j, 2 * x + y).start()
            own(a).start()

    def hand_on():
        x, y, c = _place()
        for a in range(n):
            for j, (px, py) in enumerate(_other_chips(x, y)):
                ici(a, j, 2 * px + py).wait_recv()
                if halved[a]:
                    passed(a, j, c).start()

    def finish():
        x, y, c = _place()
        for a in range(n):
            for j in range(3):
                if halved[a]:
                    passed(a, j, 1 - c).wait_recv()
                    passed(a, j, c).wait_send()
                ici(a, j, 2 * x + y).wait_send()
            own(a).wait()

    return start, hand_on, finish


def _swap_halves(grads, name):
    n = len(grads)

    def body(*refs):
        start, finish = _swap_ops(refs[:n], refs[n:2 * n], *refs[2 * n:])
        start()
        finish()

    return pl.pallas_call(
        body, name=name,
        in_specs=[_ANY] * n, out_specs=[_ANY] * n, out_shape=_swap_shapes(grads), scratch_shapes=_swap_sems(n),
    )(*grads)


def _swap_shapes(grads):
    return [jax.ShapeDtypeStruct((4, g.shape[1] // 2, g.shape[2]), F32) for g in grads]


def _swap_sems(n):
    return [pltpu.SemaphoreType.DMA((n,))] * 2


def _swap_ops(ins, outs, send_sems, recv_sems):
    def copy(a):
        x, y, c = _place()
        half = ins[a].shape[1] // 2
        return pltpu.make_async_remote_copy(src_ref=ins[a].at[:, pl.ds((1 - c) * half, half), :], dst_ref=outs[a],
                                            send_sem=send_sems.at[a], recv_sem=recv_sems.at[a],
                                            device_id=(x, y, 1 - c), device_id_type=MESH)

    def start():
        for a in range(len(ins)):
            copy(a).start()

    def finish():
        for a in range(len(ins)):
            copy(a).wait()

    return start, finish


def _scatter_shapes(sums16):
    return [jax.ShapeDtypeStruct((3,) + s.shape[1:], BF16) for s in sums16]


def _scatter_sems(n):
    return [pltpu.SemaphoreType.DMA((n, 3))] * 2


def _scatter_ops(ins, outs, send_sems, recv_sems):
    n = len(ins)

    def copy(a, j):
        x, y, c = _place()
        px, py = _other_chips(x, y)[j]
        return pltpu.make_async_remote_copy(src_ref=ins[a].at[2 * px + py], dst_ref=outs[a].at[j], send_sem=send_sems.at[a, j],
                                            recv_sem=recv_sems.at[a, j], device_id=(px, py, c), device_id_type=MESH)

    def start():
        for a in range(n):
            for j in range(3):
                copy(a, j).start()

    def finish():
        for a in range(n):
            for j in range(3):
                copy(a, j).wait()

    return start, finish


def _join_halves(fulls):
    n = len(fulls)

    def body(*refs):
        ins, outs = refs[:n], refs[n:2 * n]
        send_sems, recv_sems = refs[2 * n:]
        x, y, c = _place()
        started = []
        for a in range(n):
            half = ins[a].shape[0] // 2
            rows = pl.ds(c * half, half)
            cp = pltpu.make_async_remote_copy(src_ref=ins[a].at[rows, :], dst_ref=outs[a].at[rows, :], send_sem=send_sems.at[a],
                                              recv_sem=recv_sems.at[a], device_id=(x, y, 1 - c), device_id_type=MESH)
            cp.start()
            started.append(cp)
        for cp in started:
            cp.wait()

    return pl.pallas_call(
        body, name="join_halves",
        in_specs=[_ANY] * n, out_specs=[_ANY] * n,
        out_shape=[jax.ShapeDtypeStruct(f.shape, F32) for f in fulls],
        input_output_aliases={a: a for a in range(n)},
        scratch_shapes=[pltpu.SemaphoreType.DMA((n,)), pltpu.SemaphoreType.DMA((n,))],
    )(*fulls)


def _small_allreduce(g):
    R = g.shape[0]
    half = R // 2

    def body(g_ref, out_ref, other_s, chip_s, parts_s, send_sems, recv_sems):
        x, y, c = _place()
        mine = 2 * x + y
        rows = pl.ds(pl.multiple_of(c * half, SUBLANES), half)

        def to_other_core(src, dst, k):
            return pltpu.make_async_remote_copy(src_ref=src, dst_ref=dst, send_sem=send_sems.at[k], recv_sem=recv_sems.at[k],
                                                device_id=(x, y, 1 - c), device_id_type=MESH)

        swap = to_other_core(g_ref, other_s, 0)
        swap.start()
        swap.wait()
        chip_s[...] = g_ref[...] + other_s[...]
        parts_s[mine] = chip_s[rows, :]
        sends = []
        for j, (px, py) in enumerate(_other_chips(x, y)):
            cp = pltpu.make_async_remote_copy(src_ref=chip_s.at[rows, :], dst_ref=parts_s.at[mine], send_sem=send_sems.at[1 + j],
                                              recv_sem=recv_sems.at[1 + j], device_id=(px, py, c), device_id_type=MESH)
            cp.start()
            sends.append(cp)
        for cp in sends:
            cp.wait()
        out_ref[rows, :] = (parts_s[0] + parts_s[1]) + (parts_s[2] + parts_s[3])
        join = to_other_core(out_ref.at[rows, :], out_ref.at[rows, :], 4)
        join.start()
        join.wait()

    vm = pl.BlockSpec(memory_space=pltpu.VMEM)
    return pl.pallas_call(
        body, name="small_allreduce",
        in_specs=[vm], out_specs=vm, out_shape=jax.ShapeDtypeStruct((R, LANES), F32),
        scratch_shapes=[pltpu.VMEM((R, LANES), F32), pltpu.VMEM((R, LANES), F32), pltpu.VMEM((4, half, LANES), F32),
                        pltpu.SemaphoreType.DMA((5,)), pltpu.SemaphoreType.DMA((5,))],
        compiler_params=pltpu.CompilerParams(vmem_limit_bytes=VMEM_LIMIT),
    )(g)


def _adamw_small(ws, gs, ms, vs):
    n = len(ws)

    def body(*refs):
        for k in range(n):
            w_ref, g_ref, m_ref, v_ref = (refs[j * n + k] for j in range(4))
            d, nm, nv = _adamw_math(w_ref[...], g_ref[...], m_ref[...], v_ref[...])
            refs[4 * n + k][...] = d
            refs[5 * n + k][...] = nm
            refs[6 * n + k][...] = nv

    vm = pl.BlockSpec(memory_space=pltpu.VMEM)
    out = pl.pallas_call(
        body, name="adamw_small",
        in_specs=[vm] * (4 * n), out_specs=[vm] * (3 * n),
        out_shape=[jax.ShapeDtypeStruct(w.shape, F32) for w in ws] * 3,
        compiler_params=pltpu.CompilerParams(vmem_limit_bytes=VMEM_LIMIT),
    )(*ws, *gs, *ms, *vs)
    return out[:n], out[n:2 * n], out[2 * n:]


_SMALL = (("norm_mix_g", D_MODEL), ("f_bias", N_HEADS), ("sg_ln_g", D_HEADS), ("sg_w", N_HEADS * SG_BLOCK * SG_BLOCK),
          ("sg_b", N_HEADS * SG_BLOCK), ("norm_ffn_g", D_MODEL), ("w_conv", 3 * 2 * D_FF), ("b_conv", 2 * D_FF),
          ("norm_final_g", D_MODEL))
_PACKED = _SMALL + (("sq_err", D_MODEL),)


def _pack_small(parts):
    rows = []
    for name, size in _PACKED:
        flat = parts[name].reshape(-1).astype(F32)
        pad = (-size) % (SUBLANES * LANES)
        rows.append(jnp.pad(flat, (0, pad)).reshape(-1, LANES))
    packed = jnp.concatenate(rows, axis=0)
    return jnp.pad(packed, ((0, (-packed.shape[0]) % (2 * SUBLANES)), (0, 0)))


def _unpack_small(packed, shapes):
    out, r = {}, 0
    for name, size in _PACKED:
        nrows = (size + SUBLANES * LANES - 1) // (SUBLANES * LANES) * SUBLANES
        out[name] = packed[r:r + nrows].reshape(-1)[:size].reshape(shapes[name])
        r += nrows
    return out


def _local_step(x, target, g1, w_in, f_bias, sg_ln_g, sg_w, sg_b, g2, b_conv, g3, late_shards, ids):
    S = x.shape[0]
    tm = _row_tile(S, 512)
    tms = _row_tile(S, 256)
    tq = tm

    lane = jnp.arange(D_HEADS)
    seg_avg = jnp.where(lane[:, None] // HEAD_DIM == lane[None, :] // HEAD_DIM, 1.0 / HEAD_DIM, 0.0).astype(BF16)
    head_ind = (lane[:, None] // HEAD_DIM == jnp.arange(LANES)[None, :]).astype(BF16)
    pos_chunk = jnp.arange(SG_BLOCK) // CHUNK
    w_mask32 = jnp.where(pos_chunk[:, None] >= pos_chunk[None, :], sg_w, 0.0)
    w_mask = w_mask32.astype(BF16)
    w_mask_t = jnp.swapaxes(w_mask32, 1, 2).astype(BF16)
    ln_row = sg_ln_g.reshape(1, D_HEADS)
    b_full = jnp.repeat(sg_b.T, HEAD_DIM, axis=1)
    bias_row = jnp.pad(f_bias.reshape(1, N_HEADS), ((0, 0), (0, LANES - N_HEADS)))
    b_conv_row = b_conv.reshape(1, 2 * D_FF)

    z, f, h1 = _in_proj(x, g1, w_in, tm)
    c = _fox_prep(f, bias_row, tm)
    consts = _attn_consts()
    qa, ka, va, vat = _attn_pack(z, c, consts, tm)
    out_b, lse, gathered = _attn_fwd(qa, ka, vat, consts["place_t"], tq, late_shards)
    g_out, w_up_q, g_down, g_conv = gathered
    w_out = g_out.reshape(D_MODEL, D_MODEL)
    w_down = g_down.reshape(D_FF, D_MODEL)
    w_conv = jnp.concatenate([g_conv[q] for q in range(4)], axis=1)
    out_a = _gate_fwd(z, w_mask, ln_row, b_full, seg_avg, tm)
    x1, h2 = _mix_out(x, out_a, out_b, w_out, g2, tm)
    a = _up_proj(h2, w_up_q, tm)
    dx2, sq_err, dg3 = _ffn_fwd_loss(a, w_conv, b_conv_row, w_down, x1, g3, target, tm)

    dconv, y, dw_conv8, db_conv = _ffn_bwd_gate(dx2, a, w_conv, b_conv_row, w_down, tms)
    dact = _conv_bwd(dconv, w_conv, tms, D_FF)
    dw_down = _matmul_tn(y, dx2, "dw_down", D_FF // 2, D_MODEL, tm, quarters=(2, 1))
    dx1, dg2 = _up_bwd(dact, w_up_q, x1, g2, dx2, tm)
    dw_up_q = _matmul_tn(h2, dact, "dw_up", D_MODEL, 2 * D_FF // 4, tm, quarters=(1, 4))
    dcat = _out_bwd(dx1, w_out, tm)
    dw_out = _dw_out(out_a, out_b, dx1, tm)
    early = {"w_down": dw_down.reshape(4, D_FF // 4, D_MODEL), "w_up": dw_up_q,
             "w_out": dw_out.reshape(4, D_MODEL // 4, D_MODEL)}
    (dzu, dzv, dsg_w, dsg_b_t, dln), theirs = _gate_bwd(z, dcat, w_mask, w_mask_t, ln_row, b_full, seg_avg, head_ind, tm,
                                                        list(early.values()))
    early_sums = _chip_sums(early, theirs, ids)
    dop, qb, dopt, qbt = _attn_pack_grad(out_b, dcat, qa, lse, head_ind, consts, tm)
    dqp, dc_rows, dk, dv, dc_cols, landed = _attn_bwd(qb, qbt, ka, va, dop, dopt, consts, tq,
                                                      [s16 for _, s16 in early_sums.values()])
    early_parts = {k: (s32, got) for (k, (s32, _)), got in zip(early_sums.items(), landed)}
    dq = _attn_unpack(dqp, consts, tm)
    dc_rows = jnp.concatenate([dc_rows[g][:, :GROUP_HEADS] for g in range(GROUPS)], axis=1)
    dc_cols = jnp.concatenate([dc_cols[g][:GROUP_HEADS] for g in range(GROUPS)], axis=0).T
    dc = jnp.pad(dc_rows - dc_cols, ((0, 0), (0, LANES - N_HEADS)))
    df, dbias = _fox_bwd(dc, f, bias_row, tm)
    pieces = (dzu, dzv, dq, dk, dv, df)
    dw_in = _dw_in(h1, pieces, tm)[:, :D_IN].reshape(D_MODEL, 4, D_IN // 4).transpose(1, 0, 2)
    (w_in_sum, w_in_sum16), = _chip_sums({"w_in": dw_in}, _swap_halves([dw_in], "swap_halves"), ids).values()
    dx, dg1, (w_in_landed,) = _in_bwd(pieces, w_in, x, g1, dx1, tm, [w_in_sum16])

    grads = {
        "norm_mix_g": dg1, "f_bias": dbias[:, :N_HEADS], "sg_ln_g": dln, "sg_w": dsg_w, "sg_b": dsg_b_t[:, :N_HEADS].T,
        "norm_ffn_g": dg2, "w_conv": dw_conv8[:3], "b_conv": db_conv, "norm_final_g": dg3,
    }
    return sq_err, dx, grads, {**early_parts, "w_in": (w_in_sum, w_in_landed)}


def _chip_sums(grads_q, theirs, ids):
    return {k: ((g, t), _pair_sum("pair_sum_" + k, g, t, ids)) for (k, g), t in zip(grads_q.items(), theirs)}


def _finish_reduction(parts, ids):
    names = list(parts)
    fulls = [_chip_sum("chip_sum_" + k, g, t, got, ids) for k, ((g, t), got) in parts.items()]
    return dict(zip(names, _join_halves(fulls)))


def kernel(x, norm_mix_g, w_in, f_bias, sg_ln_g, sg_w, sg_b, w_out, norm_ffn_g, w_up, w_conv, b_conv, w_down, norm_final_g, loss_target, m_norm_mix_g, m_w_in, m_f_bias, m_sg_ln_g, m_sg_w, m_sg_b, m_w_out, m_norm_ffn_g, m_w_up, m_w_conv, m_b_conv, m_w_down, m_norm_final_g, v_norm_mix_g, v_w_in, v_f_bias, v_sg_ln_g, v_sg_w, v_sg_b, v_w_out, v_norm_ffn_g, v_w_up, v_w_conv, v_b_conv, v_w_down, v_norm_final_g):
    args = dict(locals())
    quarter = 2 * lax.axis_index("x") + lax.axis_index("y")
    ids = jnp.stack([quarter, lax.axis_index("c")]).astype(jnp.int32)
    wq_conv = w_conv.shape[-1]

    g_in = _gather_quarters([w_in[0].astype(BF16)])[0]
    w_in_full = jnp.pad(jnp.concatenate([g_in[q] for q in range(4)], axis=1), ((0, 0), (0, D_IN_PAD - D_IN)))
    late_shards = [w_out[0].astype(BF16), w_up[0].astype(BF16), w_down[0].astype(BF16), w_conv[0]]

    sq_err, dx, grads, parts = _local_step(
        x[0], loss_target[0], norm_mix_g, w_in_full, f_bias[0], sg_ln_g[0], sg_w[0], sg_b[0], norm_ffn_g, b_conv[0],
        norm_final_g.reshape(1, D_MODEL), late_shards, ids)
    big = _finish_reduction(parts, ids)

    out = {"grad_x": dx[None]}
    for k in ("w_in", "w_out", "w_up", "w_down"):
        g, d, nm, nv = _adamw("adamw_" + k, args[k][0], big[k], args["m_" + k][0], args["v_" + k][0])
        out["grad_" + k], out["delta_" + k], out["new_m_" + k], out["new_v_" + k] = g[None], d[None], nm[None], nv[None]

    small_names = [n for n, _ in _SMALL]
    shapes = {n: (3, 4 * wq_conv) if n == "w_conv" else args[n].shape for n in small_names}
    shapes["sq_err"] = sq_err.shape
    g_small = _unpack_small(_small_allreduce(_pack_small({**{n: grads[n] for n in small_names}, "sq_err": sq_err})), shapes)
    out["loss"] = 0.5 * jnp.sum(g_small.pop("sq_err")) / D_MODEL
    g_small["w_conv"] = lax.dynamic_slice(g_small["w_conv"], (0, quarter * wq_conv), (3, wq_conv))[None]
    flat2d = lambda t: t.reshape(-1, t.shape[-1])
    updated = _adamw_small(*[[flat2d(src[p + n]) for n in small_names] for src, p in
                             ((args, ""), (g_small, ""), (args, "m_"), (args, "v_"))])
    for n, g in g_small.items():
        out["grad_" + n] = g
    for prefix, arrs in zip(("delta_", "new_m_", "new_v_"), updated):
        for n, t in zip(small_names, arrs):
            out[prefix + n] = t.reshape(args[n].shape)

    weights = ["norm_mix_g", "w_in", "f_bias", "sg_ln_g", "sg_w", "sg_b", "w_out", "norm_ffn_g", "w_up", "w_conv", "b_conv",
               "w_down", "norm_final_g"]
    return (out["loss"], out["grad_x"], *[out[p + n] for p in ("grad_", "delta_", "new_m_", "new_v_") for n in weights])
```

```python
import functools
import math

import jax
import jax.numpy as jnp
from jax import lax
from jax.experimental import pallas as pl
from jax.experimental.pallas import tpu as pltpu

F32 = jnp.float32
BF16 = jnp.bfloat16
MESH = pl.DeviceIdType.MESH

D_MODEL = 1024
N_HEADS = 8
HEAD_DIM = 64
D_HEADS = N_HEADS * HEAD_DIM
SG_BLOCK = 128
CHUNK = 64
D_FF = 2816
D_IN = 2 * D_HEADS + 3 * D_HEADS + N_HEADS
LANES = 128
SUBLANES = 8
D_IN_PAD = 5 * D_HEADS + LANES
EPS = 1e-6
SCALE = HEAD_DIM ** -0.5
NEG = -1e30
LOG2E = 1.4426950408889634
HEAD_PAD = LANES
D_PAD = N_HEADS * HEAD_PAD
Q_STAT = HEAD_DIM
K_STAT = HEAD_DIM + 3
L_STAT = HEAD_DIM + 6
GROUPS = 2
GROUP_HEADS = N_HEADS // GROUPS
GROUP_PAD = GROUP_HEADS * HEAD_PAD
KEY_CHUNK = 256
FWD_KEY_CHUNK = 512
STAT_ROWS = 16
FF_CHUNK = 256

ADAM_LR = 0.001
ADAM_B1 = 0.9
ADAM_B2 = 0.999
ADAM_EPS = 1e-08
ADAM_WD = 0.01
ADAM_STEP = 10

VMEM_LIMIT = 56 * 1024 * 1024

NT = (((1,), (1,)), ((), ()))
TN = (((0,), (0,)), ((), ()))


def _params(sem):
    return pltpu.CompilerParams(dimension_semantics=sem, vmem_limit_bytes=VMEM_LIMIT)


def _full(shape):
    nd = len(shape)
    return pl.BlockSpec(shape, lambda *_: (0,) * nd)


def _row_tile(rows, target):
    best = None
    for t in range(SUBLANES, min(rows, target) + 1, SUBLANES):
        if rows % t == 0:
            best = t
    assert best is not None, rows
    return best


def _sigmoid(x):
    return 0.5 * jnp.tanh(0.5 * x) + 0.5


def _gelu(z):
    return 0.5 * z * (1.0 + lax.erf(z * (2.0 ** -0.5)))


def _gelu_grad(z):
    cdf = 0.5 * (1.0 + lax.erf(z * (2.0 ** -0.5)))
    pdf = jnp.exp(-0.5 * z * z) * (1.0 / math.sqrt(2.0 * math.pi))
    return cdf + z * pdf


def _split_dot(x, m):
    hi = x.astype(BF16)
    lo = (x - hi.astype(F32)).astype(BF16)
    return jnp.dot(hi, m, preferred_element_type=F32) + jnp.dot(lo, m, preferred_element_type=F32)


def _head_mask(h, rows):
    lane = lax.broadcasted_iota(jnp.int32, (rows, D_HEADS), 1)
    return (lane >= h * HEAD_DIM) & (lane < (h + 1) * HEAD_DIM)


def _rms_bwd(dh, x, g):
    r = lax.rsqrt(jnp.mean(x * x, axis=-1, keepdims=True) + EPS)
    xhat = x * r
    dg = jnp.sum(dh * xhat, axis=0, keepdims=True)
    dxhat = dh * g
    dx = r * (dxhat - xhat * jnp.mean(dxhat * xhat, axis=-1, keepdims=True))
    return dx, dg


def _in_proj(x, g1, w_in, tm):
    S = x.shape[0]
    nz = D_IN_PAD - LANES

    def body(x_ref, g_ref, w_ref, z_ref, f_ref, h_ref):
        xf = x_ref[...]
        r = lax.rsqrt(jnp.mean(xf * xf, axis=-1, keepdims=True) + EPS)
        h = (xf * r * g_ref[...]).astype(BF16)
        h_ref[...] = h
        zz = jnp.dot(h, w_ref[...], preferred_element_type=F32)
        z_ref[...] = zz[:, :nz].astype(BF16)
        f_ref[...] = zz[:, nz:]

    return pl.pallas_call(
        body, name="in_proj", grid=(S // tm,),
        in_specs=[pl.BlockSpec((tm, D_MODEL), lambda i: (i, 0)), _full((1, D_MODEL)), _full((D_MODEL, D_IN_PAD))],
        out_specs=[pl.BlockSpec((tm, nz), lambda i: (i, 0)), pl.BlockSpec((tm, LANES), lambda i: (i, 0)),
                   pl.BlockSpec((tm, D_MODEL), lambda i: (i, 0))],
        out_shape=[jax.ShapeDtypeStruct((S, nz), BF16), jax.ShapeDtypeStruct((S, LANES), F32),
                   jax.ShapeDtypeStruct((S, D_MODEL), BF16)],
        compiler_params=_params(("parallel",)),
    )(x, g1, w_in)


def _fox_prep(f, bias_row, tb):
    S = f.shape[0]

    def body(f_ref, b_ref, c_ref, carry):
        @pl.when(pl.program_id(0) == 0)
        def _():
            carry[...] = jnp.zeros_like(carry)

        xv = f_ref[...] + b_ref[...]
        lf = jnp.minimum(xv, 0.0) - jnp.log(1.0 + jnp.exp(-jnp.abs(xv)))
        r = lax.broadcasted_iota(jnp.int32, (tb, tb), 0)
        s = lax.broadcasted_iota(jnp.int32, (tb, tb), 1)
        tri = (r >= s).astype(F32)
        cs = jnp.dot(tri, lf, precision=lax.Precision.HIGHEST, preferred_element_type=F32) + carry[0:1, :]
        c_ref[...] = cs
        carry[...] = jnp.broadcast_to(cs[tb - 1:tb, :], carry.shape)

    return pl.pallas_call(
        body, name="fox_prep", grid=(S // tb,),
        in_specs=[pl.BlockSpec((tb, LANES), lambda i: (i, 0)), _full((1, LANES))],
        out_specs=pl.BlockSpec((tb, LANES), lambda i: (i, 0)),
        out_shape=jax.ShapeDtypeStruct((S, LANES), F32),
        scratch_shapes=[pltpu.VMEM((SUBLANES, LANES), F32)],
        compiler_params=_params(("arbitrary",)),
    )(f, bias_row)


def _attn_consts():
    col = jnp.arange(D_PAD)
    row = jnp.arange(D_HEADS)
    head = jnp.arange(LANES)
    place = (row[:, None] // HEAD_DIM == col[None, :] // HEAD_PAD) & (row[:, None] % HEAD_DIM == col[None, :] % HEAD_PAD)

    def stat(offset):
        return ((head[:, None] < N_HEADS) & (col[None, :] == head[:, None] * HEAD_PAD + offset)).astype(BF16)

    def stat3(base):
        part, h = head // N_HEADS, head % N_HEADS
        return ((part[:, None] < 3) & (col[None, :] == h[:, None] * HEAD_PAD + base + part[:, None])).astype(BF16)

    def ones(offsets):
        return sum((col % HEAD_PAD == o) for o in offsets).astype(F32).reshape(1, D_PAD)

    place = place.astype(BF16)
    return {
        "place": place, "place_t": place.T, "place_t_group": place.T[:GROUP_PAD, :GROUP_HEADS * HEAD_DIM],
        "q_stat": stat3(Q_STAT), "k_stat": stat3(K_STAT),
        "d_stat": stat3(Q_STAT) * (head[:, None] < 2 * N_HEADS).astype(BF16),
        "l_stat": jnp.concatenate([stat(L_STAT + j)[:STAT_ROWS] for j in range(3)], axis=0),
        "q_ones": ones(range(K_STAT, K_STAT + 3)), "k_ones": ones(list(range(Q_STAT, Q_STAT + 3)) + list(range(L_STAT, L_STAT + 3))),
        "v_ones": ones(range(Q_STAT, Q_STAT + 2)),
    }


def _split3(x):
    hi = x.astype(BF16)
    r = x - hi.astype(F32)
    mid = r.astype(BF16)
    return hi, mid, (r - mid.astype(F32)).astype(BF16)


def _attn_pack(z, c, k, tm):
    S = z.shape[0]

    def body(q_ref, k_ref, v_ref, c_ref, pl_ref, pt_ref, qs_ref, ks_ref, qo_ref, ko_ref, vo_ref, voc_ref,
             qa_ref, ka_ref, va_ref, vt_ref):
        place = pl_ref[...]
        q = (q_ref[...].astype(F32) * (SCALE * LOG2E)).astype(BF16)
        qa = jnp.dot(q, place, preferred_element_type=F32) + qo_ref[...]
        ka = jnp.dot(k_ref[...], place, preferred_element_type=F32) + ko_ref[...]
        lane = lax.broadcasted_iota(jnp.int32, (tm, LANES), 1)
        hi, mid, lo = _split3(jnp.where(lane < N_HEADS, c_ref[...] * LOG2E, 0.0))
        parts = hi.astype(F32) + pltpu.roll(mid.astype(F32), N_HEADS, 1) + pltpu.roll(lo.astype(F32), 2 * N_HEADS, 1)
        parts = parts.astype(BF16)
        qa = qa + jnp.dot(parts, qs_ref[...], preferred_element_type=F32)
        ka = ka - jnp.dot(parts, ks_ref[...], preferred_element_type=F32)
        qa_ref[...] = qa.astype(BF16)
        ka_ref[...] = ka.astype(BF16)
        v = v_ref[...]
        va_ref[...] = (jnp.dot(v, place, preferred_element_type=F32) + vo_ref[...]).astype(BF16)
        vt_ref[...] = (lax.dot_general(pt_ref[...], v, NT, preferred_element_type=F32) + voc_ref[...]).astype(BF16)

    blk = lambda col: pl.BlockSpec((tm, D_HEADS), lambda i: (i, col))
    out = pl.BlockSpec((tm, D_PAD), lambda i: (i, 0))
    pad = jax.ShapeDtypeStruct((S, D_PAD), BF16)
    return pl.pallas_call(
        body, name="attn_pack", grid=(S // tm,),
        in_specs=[blk(2), blk(3), blk(4), pl.BlockSpec((tm, LANES), lambda i: (i, 0)), _full((D_HEADS, D_PAD)), _full((D_PAD, D_HEADS)),
                  _full((LANES, D_PAD)), _full((LANES, D_PAD)), _full((1, D_PAD)), _full((1, D_PAD)), _full((1, D_PAD)),
                  _full((D_PAD, 1))],
        out_specs=[out, out, out, pl.BlockSpec((None, D_PAD, tm), lambda i: (i, 0, 0))],
        out_shape=[pad, pad, pad, jax.ShapeDtypeStruct((S // tm, D_PAD, tm), BF16)],
        compiler_params=_params(("parallel",)),
    )(z, z, z, c, k["place"], k["place_t"], k["q_stat"], k["k_stat"], k["q_ones"], k["k_ones"], k["v_ones"], k["v_ones"].T)


def _attn_fwd(qa, ka, vat, place_t, tq, shards):
    S = qa.shape[0]
    n = S // tq
    ns = len(shards)
    hand_on_at = (2 * n) // 3

    pairs = [(q, k) for q in range(n) for k in range(q + 1)]
    q_of = jnp.asarray([q for q, _ in pairs], jnp.int32)
    k_of = jnp.asarray([k for _, k in pairs], jnp.int32)

    def body(q_of_ref, k_of_ref, q_ref, k_ref, vt_ref, pt_ref, *rest):
        o_ref, lse_ref = rest[ns:ns + 2]
        m_s, acc_s, ot_s = rest[2 * ns + 2:2 * ns + 5]
        s_s = rest[2 * ns + 5:2 * ns + 7]
        start, hand_on, finish = _gather_ops(rest[:ns], rest[ns + 2:2 * ns + 2], *rest[2 * ns + 7:])
        qi, ki = q_of_ref[pl.program_id(0)], k_of_ref[pl.program_id(0)]

        @pl.when((qi == 0) & (ki == 0))
        def _():
            start()

        @pl.when((qi == hand_on_at) & (ki == 0))
        def _():
            hand_on()

        @pl.when(ki == 0)
        def _():
            m_s[...] = jnp.full_like(m_s, NEG)
            acc_s[...] = jnp.zeros_like(acc_s)

        def step(diagonal):
            kc = FWD_KEY_CHUNK
            chunks = [slice(c * kc, (c + 1) * kc) for c in range(tq // kc)]

            def scores(h, rows, slot):
                sl = slice(h * HEAD_PAD, (h + 1) * HEAD_PAD)
                st = lax.dot_general(k_ref[rows, sl], q_ref[:, sl], NT, preferred_element_type=F32)
                if diagonal:
                    key = rows.start + lax.broadcasted_iota(jnp.int32, (kc, tq), 0)
                    query = lax.broadcasted_iota(jnp.int32, (kc, tq), 1)
                    st = jnp.where(query >= key, st, NEG)
                s_s[slot][rows, :] = st
                return jnp.max(st, axis=0, keepdims=True)

            m_cur = functools.reduce(jnp.maximum, [scores(0, rows, 0) for rows in chunks])
            for h in range(N_HEADS):
                sl = slice(h * HEAD_PAD, (h + 1) * HEAD_PAD)
                slot = h % 2
                m_prev = m_s[h][0:1, :]
                m_new = jnp.maximum(m_prev, m_cur)
                acc = jnp.exp2(m_prev - m_new) * acc_s[h]
                m_next = []
                for rows in chunks:
                    if h + 1 < N_HEADS:
                        m_next.append(scores(h + 1, rows, 1 - slot))
                    pt = jnp.exp2(s_s[slot][rows, :] - m_new).astype(BF16)
                    acc = acc + jnp.dot(vt_ref[sl, rows], pt, preferred_element_type=F32)
                acc_s[h] = acc
                m_s[h] = jnp.broadcast_to(m_new, (SUBLANES, tq))
                if m_next:
                    m_cur = functools.reduce(jnp.maximum, m_next)

        @pl.when(ki < qi)
        def _():
            step(False)

        @pl.when(ki == qi)
        def _():
            step(True)
            lse_ref[...] = jnp.zeros_like(lse_ref)
            for h in range(N_HEADS):
                acc = acc_s[h]
                denom = acc[Q_STAT:Q_STAT + 1, :]
                ot_s[h * HEAD_PAD:(h + 1) * HEAD_PAD, :] = (acc / denom).astype(BF16)
                lse_ref[h:h + 1, :] = m_s[h][0:1, :] + jnp.log(denom) * LOG2E
            o_ref[...] = lax.dot_general(ot_s[...], pt_ref[...], TN, preferred_element_type=F32).astype(BF16)

        @pl.when((qi == n - 1) & (ki == n - 1))
        def _():
            finish()

    out = pl.pallas_call(
        body, name="attn_fwd",
        grid_spec=pltpu.PrefetchScalarGridSpec(
            num_scalar_prefetch=2, grid=(len(pairs),),
            in_specs=[pl.BlockSpec((tq, D_PAD), lambda i, qs, ks: (qs[i], 0)),
                      pl.BlockSpec((tq, D_PAD), lambda i, qs, ks: (ks[i], 0)),
                      pl.BlockSpec((None, D_PAD, tq), lambda i, qs, ks: (ks[i], 0, 0)),
                      pl.BlockSpec((D_PAD, D_HEADS), lambda i, qs, ks: (0, 0))]
            + [_ANY] * ns,
            out_specs=[pl.BlockSpec((tq, D_HEADS), lambda i, qs, ks: (qs[i], 0)),
                       pl.BlockSpec((STAT_ROWS, tq), lambda i, qs, ks: (0, qs[i]))] + [_ANY] * ns,
            scratch_shapes=[pltpu.VMEM((N_HEADS, SUBLANES, tq), F32), pltpu.VMEM((N_HEADS, HEAD_PAD, tq), F32),
                            pltpu.VMEM((D_PAD, tq), BF16), pltpu.VMEM((tq, tq), F32), pltpu.VMEM((tq, tq), F32)] + _gather_sems(ns)),
        out_shape=[jax.ShapeDtypeStruct((S, D_HEADS), BF16), jax.ShapeDtypeStruct((STAT_ROWS, S), F32)] + _gather_shapes(shards),
        compiler_params=_params(("arbitrary",)),
    )(q_of, k_of, qa, ka, vat, place_t, *shards)
    return out[0], out[1], out[2:]


def _layer_norm_heads(v, seg_avg):
    mu = _split_dot(v, seg_avg)
    d = v - mu
    var = jnp.dot((d * d).astype(BF16), seg_avg, preferred_element_type=F32)
    rstd = lax.rsqrt(var + EPS)
    return d * rstd, rstd


def _gate_mix(vn_blk, w_ref, bias):
    acc = bias
    for h in range(N_HEADS):
        vh = jnp.where(_head_mask(h, SG_BLOCK), vn_blk, 0.0).astype(BF16)
        acc = acc + jnp.dot(w_ref[h], vh, preferred_element_type=F32)
    return acc


def _gate_fwd(z, w_mask, ln_row, b_full, seg_avg, tm):
    S = z.shape[0]

    def body(zu_ref, zv_ref, w_ref, ln_ref, b_ref, avg_ref, o_ref):
        u = _gelu(zu_ref[...].astype(F32))
        v = _gelu(zv_ref[...].astype(F32))
        vhat, _ = _layer_norm_heads(v, avg_ref[...])
        vn = vhat * ln_ref[...]
        for b in range(tm // SG_BLOCK):
            rows = slice(b * SG_BLOCK, (b + 1) * SG_BLOCK)
            mixed = _gate_mix(vn[rows], w_ref, b_ref[...])
            o_ref[rows, :] = (u[rows] * mixed).astype(BF16)

    return pl.pallas_call(
        body, name="gate_fwd", grid=(S // tm,),
        in_specs=[pl.BlockSpec((tm, D_HEADS), lambda i: (i, 0)), pl.BlockSpec((tm, D_HEADS), lambda i: (i, 1)),
                  _full((N_HEADS, SG_BLOCK, SG_BLOCK)), _full((1, D_HEADS)), _full((SG_BLOCK, D_HEADS)),
                  _full((D_HEADS, D_HEADS))],
        out_specs=pl.BlockSpec((tm, D_HEADS), lambda i: (i, 0)),
        out_shape=jax.ShapeDtypeStruct((S, D_HEADS), BF16),
        compiler_params=_params(("parallel",)),
    )(z, z, w_mask, ln_row, b_full, seg_avg)


def _mix_out(x, out_a, out_b, w_out, g2, tm):
    S = x.shape[0]

    def body(x_ref, a_ref, b_ref, w_ref, g_ref, x1_ref, h_ref):
        y = jnp.dot(a_ref[...], w_ref[:D_HEADS, :], preferred_element_type=F32)
        y = y + jnp.dot(b_ref[...], w_ref[D_HEADS:, :], preferred_element_type=F32)
        x1 = x_ref[...] + y
        x1_ref[...] = x1
        r = lax.rsqrt(jnp.mean(x1 * x1, axis=-1, keepdims=True) + EPS)
        h_ref[...] = (x1 * r * g_ref[...]).astype(BF16)

    row = lambda w: pl.BlockSpec((tm, w), lambda i: (i, 0))
    return pl.pallas_call(
        body, name="mix_out", grid=(S // tm,),
        in_specs=[row(D_MODEL), row(D_HEADS), row(D_HEADS), _full((D_MODEL, D_MODEL)), _full((1, D_MODEL))],
        out_specs=[row(D_MODEL), row(D_MODEL)],
        out_shape=[jax.ShapeDtypeStruct((S, D_MODEL), F32), jax.ShapeDtypeStruct((S, D_MODEL), BF16)],
        compiler_params=_params(("parallel",)),
    )(x, out_a, out_b, w_out, g2)


def _up_proj(h2, w_up_q, tm):
    S = h2.shape[0]
    nq, _, wq = w_up_q.shape

    def body(h_ref, w_ref, a_ref):
        a_ref[...] = jnp.dot(h_ref[...], w_ref[...], preferred_element_type=F32).astype(BF16)

    return pl.pallas_call(
        body, name="up_proj", grid=(nq, S // tm),
        in_specs=[pl.BlockSpec((tm, D_MODEL), lambda j, i: (i, 0)), pl.BlockSpec((None, D_MODEL, wq), lambda j, i: (j, 0, 0))],
        out_specs=pl.BlockSpec((tm, wq), lambda j, i: (i, j)),
        out_shape=jax.ShapeDtypeStruct((S, nq * wq), BF16),
        compiler_params=_params(("parallel", "parallel")),
    )(h2, w_up_q)


def _shift_down(a, halo, k):
    tm = a.shape[0]
    ra = pltpu.roll(a, k, 0)
    rh = pltpu.roll(halo, k, 0)
    row = lax.broadcasted_iota(jnp.int32, halo.shape, 0)
    top = jnp.where(row < k, rh, ra[0:SUBLANES])
    return jnp.concatenate([top, ra[SUBLANES:tm]], axis=0)


def _shift_up(a, halo, k):
    tm = a.shape[0]
    ra = pltpu.roll(a, tm - k, 0)
    rh = pltpu.roll(halo, SUBLANES - k, 0)
    row = lax.broadcasted_iota(jnp.int32, halo.shape, 0)
    bottom = jnp.where(row >= SUBLANES - k, rh, ra[tm - SUBLANES:tm])
    return jnp.concatenate([ra[0:tm - SUBLANES], bottom], axis=0)


def _shift_matrices(tm):
    row = lax.broadcasted_iota(jnp.int32, (tm, tm), 0)
    col = lax.broadcasted_iota(jnp.int32, (tm, tm), 1)
    return [(row == col + k).astype(BF16) for k in (1, 2)]


def _conv_taps(a, halo, first, shifts):
    tm = a.shape[0]
    halo = halo.astype(F32) * jnp.where(first, 0.0, 1.0)
    if shifts is None:
        a = a.astype(F32)
        return a, _shift_down(a, halo, 1), _shift_down(a, halo, 2)
    row8 = lax.broadcasted_iota(jnp.int32, halo.shape, 0)
    taps = [a.astype(F32)]
    for k, shift in zip((1, 2), shifts):
        down = jnp.dot(shift, a, preferred_element_type=F32)
        top = down[0:SUBLANES] + jnp.where(row8 < k, pltpu.roll(halo, k, 0), 0.0)
        taps.append(jnp.concatenate([top, down[SUBLANES:tm]], axis=0))
    return taps


def _conv_gate_val(refs, shifts, cols, first):
    ag_ref, av_ref, hg_ref, hv_ref, wg_ref, wv_ref, bg_ref, bv_ref = refs
    g0, g1, g2 = _conv_taps(ag_ref[:, cols], hg_ref[:, cols], first, shifts)
    gate = wg_ref[2:3, cols] * g0 + wg_ref[1:2, cols] * g1 + wg_ref[0:1, cols] * g2 + bg_ref[:, cols]
    v0, v1, v2 = _conv_taps(av_ref[:, cols], hv_ref[:, cols], first, shifts)
    val = wv_ref[2:3, cols] * v0 + wv_ref[1:2, cols] * v1 + wv_ref[0:1, cols] * v2 + bv_ref[:, cols]
    return gate, val, (g2, g1, g0), (v2, v1, v0)


_FF_CHUNKS = [slice(j * FF_CHUNK, (j + 1) * FF_CHUNK) for j in range(D_FF // FF_CHUNK)]


def _conv_specs(tm):
    step = tm // SUBLANES
    prev = lambda i: jnp.maximum(i * step - 1, 0)
    return [pl.BlockSpec((tm, D_FF), lambda i: (i, 0)), pl.BlockSpec((tm, D_FF), lambda i: (i, 1)),
            pl.BlockSpec((SUBLANES, D_FF), lambda i: (prev(i), 0)), pl.BlockSpec((SUBLANES, D_FF), lambda i: (prev(i), 1))]


def _ffn_fwd_loss(a, w_conv, b_conv, w_down, x1, g3, target, tm):
    S = x1.shape[0]

    def body(ag_ref, av_ref, hg_ref, hv_ref, wg_ref, wv_ref, bg_ref, bv_ref, wd_ref, x1_ref, g_ref, t_ref,
             dx2_ref, loss_ref, dg_ref):
        i = pl.program_id(0)

        @pl.when(i == 0)
        def _():
            loss_ref[...] = jnp.zeros_like(loss_ref)
            dg_ref[...] = jnp.zeros_like(dg_ref)

        x2 = x1_ref[...]
        for cols in _FF_CHUNKS:
            gate, val, _, _ = _conv_gate_val((ag_ref, av_ref, hg_ref, hv_ref, wg_ref, wv_ref, bg_ref, bv_ref), None, cols, i == 0)
            half = 0.5 * gate
            y = ((half + half * jnp.tanh(half)) * val).astype(BF16)
            x2 = x2 + jnp.dot(y, wd_ref[cols, :], preferred_element_type=F32)
        r = lax.rsqrt(jnp.mean(x2 * x2, axis=-1, keepdims=True) + EPS)
        xhat = x2 * r
        gg = g_ref[...]
        err = xhat * gg - t_ref[...]
        loss_ref[...] += jnp.sum(err * err, axis=0, keepdims=True)
        dy = err * (1.0 / D_MODEL)
        dg_ref[...] += jnp.sum(dy * xhat, axis=0, keepdims=True)
        dxhat = dy * gg
        dx2_ref[...] = r * (dxhat - xhat * jnp.mean(dxhat * xhat, axis=-1, keepdims=True))

    row = lambda w: pl.BlockSpec((tm, w), lambda i: (i, 0))
    half = lambda r: [pl.BlockSpec((r, D_FF), lambda i: (0, 0)), pl.BlockSpec((r, D_FF), lambda i: (0, 1))]
    return pl.pallas_call(
        body, name="ffn_fwd_loss", grid=(S // tm,),
        in_specs=_conv_specs(tm) + half(3) + half(1) + [_full((D_FF, D_MODEL)), row(D_MODEL), _full((1, D_MODEL)), row(D_MODEL)],
        out_specs=[row(D_MODEL), _full((1, D_MODEL)), _full((1, D_MODEL))],
        out_shape=[jax.ShapeDtypeStruct((S, D_MODEL), F32), jax.ShapeDtypeStruct((1, D_MODEL), F32),
                   jax.ShapeDtypeStruct((1, D_MODEL), F32)],
        compiler_params=_params(("arbitrary",)),
    )(a, a, a, a, w_conv, w_conv, b_conv, b_conv, w_down, x1, g3, target)


def _ffn_bwd_gate(dx2, a, w_conv, b_conv, w_down, tm):
    S = dx2.shape[0]

    def body(dx_ref, ag_ref, av_ref, hg_ref, hv_ref, wg_ref, wv_ref, bg_ref, bv_ref, wd_ref,
             dc_ref, y_ref, dw_ref, db_ref):
        i = pl.program_id(0)

        @pl.when(i == 0)
        def _():
            dw_ref[...] = jnp.zeros_like(dw_ref)
            db_ref[...] = jnp.zeros_like(db_ref)

        dx = dx_ref[...].astype(BF16)
        shifts = _shift_matrices(tm)
        for cols in _FF_CHUNKS:
            gate, val, gtaps, vtaps = _conv_gate_val((ag_ref, av_ref, hg_ref, hv_ref, wg_ref, wv_ref, bg_ref, bv_ref), shifts, cols, i == 0)
            sg = _sigmoid(gate)
            act = gate * sg
            y_ref[:, cols] = (act * val).astype(BF16)
            dy = lax.dot_general(dx, wd_ref[cols, :], NT, preferred_element_type=F32)
            dgate = dy * val * (sg + act - act * sg)
            dval = dy * act
            for d, taps, out in ((dgate, gtaps, cols), (dval, vtaps, slice(D_FF + cols.start, D_FF + cols.stop))):
                dc_ref[:, out] = d.astype(BF16)
                db_ref[0:1, out] += jnp.sum(d, axis=0, keepdims=True)
                for j in range(3):
                    dw_ref[j:j + 1, out] += jnp.sum(d * taps[j], axis=0, keepdims=True)

    row = lambda w: pl.BlockSpec((tm, w), lambda i: (i, 0))
    half = lambda r: [pl.BlockSpec((r, D_FF), lambda i: (0, 0)), pl.BlockSpec((r, D_FF), lambda i: (0, 1))]
    return pl.pallas_call(
        body, name="ffn_bwd_gate", grid=(S // tm,),
        in_specs=[row(D_MODEL)] + _conv_specs(tm) + half(3) + half(1) + [_full((D_FF, D_MODEL))],
        out_specs=[row(2 * D_FF), row(D_FF), _full((SUBLANES, 2 * D_FF)), _full((1, 2 * D_FF))],
        out_shape=[jax.ShapeDtypeStruct((S, 2 * D_FF), BF16), jax.ShapeDtypeStruct((S, D_FF), BF16),
                   jax.ShapeDtypeStruct((SUBLANES, 2 * D_FF), F32), jax.ShapeDtypeStruct((1, 2 * D_FF), F32)],
        compiler_params=_params(("arbitrary",)),
    )(dx2, a, a, a, a, w_conv, w_conv, b_conv, b_conv, w_down)


def _conv_bwd(dc, w_conv, tm, tn):
    S, C = dc.shape
    step = tm // SUBLANES
    last_blk = S // SUBLANES - 1

    def body(d_ref, nx_ref, w_ref, o_ref):
        last = pl.program_id(0) == pl.num_programs(0) - 1
        row = lax.broadcasted_iota(jnp.int32, (tm, tm), 0)
        col = lax.broadcasted_iota(jnp.int32, (tm, tm), 1)
        row8 = lax.broadcasted_iota(jnp.int32, (SUBLANES, FF_CHUNK), 0)
        ups = [(row + k == col).astype(BF16) for k in (1, 2)]
        for c0 in range(0, tn, FF_CHUNK):
            cols = slice(c0, c0 + FF_CHUNK)
            d = d_ref[:, cols]
            nx = nx_ref[:, cols].astype(F32) * jnp.where(last, 0.0, 1.0)
            out = w_ref[2:3, cols] * d.astype(F32)
            for k, up in zip((1, 2), ups):
                moved = jnp.dot(up, d, preferred_element_type=F32)
                bottom = moved[tm - SUBLANES:tm] + jnp.where(row8 >= SUBLANES - k, pltpu.roll(nx, SUBLANES - k, 0), 0.0)
                out = out + w_ref[2 - k:3 - k, cols] * jnp.concatenate([moved[0:tm - SUBLANES], bottom], axis=0)
            o_ref[:, cols] = out.astype(BF16)

    return pl.pallas_call(
        body, name="conv_bwd", grid=(S // tm, C // tn),
        in_specs=[pl.BlockSpec((tm, tn), lambda i, j: (i, j)),
                  pl.BlockSpec((SUBLANES, tn), lambda i, j: (jnp.minimum((i + 1) * step, last_blk), j)),
                  pl.BlockSpec((3, tn), lambda i, j: (0, j))],
        out_specs=pl.BlockSpec((tm, tn), lambda i, j: (i, j)),
        out_shape=jax.ShapeDtypeStruct((S, C), BF16),
        compiler_params=_params(("parallel", "parallel")),
    )(dc, dc, w_conv)


def _matmul_tn(a, b, name, bm, bn, tk, col_a=0, col_b=0, quarters=None):
    S = a.shape[0]
    gm, gn = quarters if quarters else (1, 1)
    nk = S // tk

    def body(a_ref, b_ref, o_ref):
        @pl.when(pl.program_id(2) == 0)
        def _():
            o_ref[...] = jnp.zeros_like(o_ref)

        o_ref[...] += lax.dot_general(a_ref[...].astype(BF16), b_ref[...].astype(BF16), TN, preferred_element_type=F32)

    if quarters and gn > 1:
        out_spec = pl.BlockSpec((None, bm, bn), lambda i, j, k: (j, i, 0))
        out_shape = jax.ShapeDtypeStruct((gn, gm * bm, bn), F32)
    else:
        out_spec = pl.BlockSpec((bm, bn), lambda i, j, k: (i, j))
        out_shape = jax.ShapeDtypeStruct((gm * bm, gn * bn), F32)
    return pl.pallas_call(
        body, name=name, grid=(gm, gn, nk),
        in_specs=[pl.BlockSpec((tk, bm), lambda i, j, k: (k, col_a * gm + i)),
                  pl.BlockSpec((tk, bn), lambda i, j, k: (k, col_b * gn + j))],
        out_specs=out_spec, out_shape=out_shape,
        compiler_params=_params(("parallel", "parallel", "arbitrary")),
    )(a, b)


def _dw_out(out_a, out_b, dx1, tk):
    S = dx1.shape[0]

    def body(a_ref, b_ref, d_ref, o_ref):
        @pl.when(pl.program_id(0) == 0)
        def _():
            o_ref[...] = jnp.zeros_like(o_ref)

        d = d_ref[...].astype(BF16)
        o_ref[:D_HEADS, :] += lax.dot_general(a_ref[...], d, TN, preferred_element_type=F32)
        o_ref[D_HEADS:, :] += lax.dot_general(b_ref[...], d, TN, preferred_element_type=F32)

    row = lambda w: pl.BlockSpec((tk, w), lambda k: (k, 0))
    return pl.pallas_call(
        body, name="dw_out", grid=(S // tk,),
        in_specs=[row(D_HEADS), row(D_HEADS), row(D_MODEL)], out_specs=_full((D_MODEL, D_MODEL)),
        out_shape=jax.ShapeDtypeStruct((D_MODEL, D_MODEL), F32),
        compiler_params=_params(("arbitrary",)),
    )(out_a, out_b, dx1)


def _up_bwd(dact, w_up_q, x1, g2, dx2, tm):
    S = x1.shape[0]
    nq, _, wq = w_up_q.shape

    def body(d_ref, w_ref, x_ref, g_ref, dx2_ref, dx1_ref, dg_ref):
        @pl.when(pl.program_id(0) == 0)
        def _():
            dg_ref[...] = jnp.zeros_like(dg_ref)

        dh = jnp.zeros((tm, D_MODEL), F32)
        for j in range(nq):
            dh = dh + lax.dot_general(d_ref[:, j * wq:(j + 1) * wq], w_ref[j], NT, preferred_element_type=F32)
        dx, dg = _rms_bwd(dh, x_ref[...], g_ref[...])
        dg_ref[...] += dg
        dx1_ref[...] = dx2_ref[...] + dx

    row = lambda w: pl.BlockSpec((tm, w), lambda i: (i, 0))
    return pl.pallas_call(
        body, name="up_bwd", grid=(S // tm,),
        in_specs=[row(nq * wq), pl.BlockSpec((nq, D_MODEL, wq), lambda i: (0, 0, 0), pipeline_mode=pl.Buffered(1)),
                  row(D_MODEL), _full((1, D_MODEL)), row(D_MODEL)],
        out_specs=[row(D_MODEL), _full((1, D_MODEL))],
        out_shape=[jax.ShapeDtypeStruct((S, D_MODEL), F32), jax.ShapeDtypeStruct((1, D_MODEL), F32)],
        compiler_params=_params(("arbitrary",)),
    )(dact, w_up_q, x1, g2, dx2)


def _out_bwd(dx1, w_out, tm):
    S = dx1.shape[0]

    def body(d_ref, w_ref, o_ref):
        o_ref[...] = lax.dot_general(d_ref[...].astype(BF16), w_ref[...], NT, preferred_element_type=F32).astype(BF16)

    return pl.pallas_call(
        body, name="out_bwd", grid=(S // tm,),
        in_specs=[pl.BlockSpec((tm, D_MODEL), lambda i: (i, 0)), _full((D_MODEL, D_MODEL))],
        out_specs=pl.BlockSpec((tm, D_MODEL), lambda i: (i, 0)),
        out_shape=jax.ShapeDtypeStruct((S, D_MODEL), BF16),
        compiler_params=_params(("parallel",)),
    )(dx1, w_out)


def _gate_bwd(z, dcat, w_mask, w_mask_t, ln_row, b_full, seg_avg, head_ind, tm, swap):
    S = z.shape[0]
    nb = tm // SG_BLOCK
    ns = len(swap)

    def body(zu_ref, zv_ref, do_ref, w_ref, wt_ref, ln_ref, b_ref, avg_ref, ind_ref, *rest):
        dzu_ref, dzv_ref, dw_ref, db_ref, dln_ref = rest[ns:ns + 5]
        dvn_s, dbf_s = rest[2 * ns + 5:2 * ns + 7]
        swap_start, swap_finish = _swap_ops(rest[:ns], rest[ns + 5:2 * ns + 5], *rest[2 * ns + 7:])
        i = pl.program_id(0)

        @pl.when(i == 0)
        def _():
            swap_start()
            dw_ref[...] = jnp.zeros_like(dw_ref)
            dln_ref[...] = jnp.zeros_like(dln_ref)
            dbf_s[...] = jnp.zeros_like(dbf_s)

        zu = zu_ref[...].astype(F32)
        zv = zv_ref[...].astype(F32)
        u = _gelu(zu)
        v = _gelu(zv)
        avg = avg_ref[...]
        vhat, rstd = _layer_norm_heads(v, avg)
        ln = ln_ref[...]
        vn = vhat * ln
        for b in range(nb):
            rows = slice(b * SG_BLOCK, (b + 1) * SG_BLOCK)
            vn_b = vn[rows]
            mixed = _gate_mix(vn_b, w_ref, b_ref[...])
            do = do_ref[rows, :].astype(F32)
            dzu_ref[rows, :] = (do * mixed * _gelu_grad(zu[rows])).astype(BF16)
            dmix = do * u[rows]
            dbf_s[...] += dmix
            vn_bf = vn_b.astype(BF16)
            dvn = jnp.zeros((SG_BLOCK, D_HEADS), F32)
            for h in range(N_HEADS):
                dmh = jnp.where(_head_mask(h, SG_BLOCK), dmix, 0.0).astype(BF16)
                dw_ref[h] += lax.dot_general(dmh, vn_bf, NT, preferred_element_type=F32)
                dvn = dvn + jnp.dot(wt_ref[h], dmh, preferred_element_type=F32)
            dvn_s[rows, :] = dvn
        dvn = dvn_s[...]
        dln_ref[...] += jnp.sum(dvn * vhat, axis=0, keepdims=True)
        dvhat = dvn * ln
        along = jnp.dot((dvhat * vhat).astype(BF16), avg, preferred_element_type=F32)
        dv = rstd * (dvhat - _split_dot(dvhat, avg) - vhat * along)
        dzv_ref[...] = (dv * _gelu_grad(zv)).astype(BF16)

        @pl.when(i == pl.num_programs(0) - 1)
        def _():
            r = lax.broadcasted_iota(jnp.int32, (SG_BLOCK, SG_BLOCK), 0) // CHUNK
            s = lax.broadcasted_iota(jnp.int32, (SG_BLOCK, SG_BLOCK), 1) // CHUNK
            for h in range(N_HEADS):
                dw_ref[h] = jnp.where(r >= s, dw_ref[h], 0.0)
            db_ref[...] = _split_dot(dbf_s[...], ind_ref[...])
            swap_finish()

    row = lambda col: pl.BlockSpec((tm, D_HEADS), lambda i: (i, col))
    wspec = _full((N_HEADS, SG_BLOCK, SG_BLOCK))
    out = pl.pallas_call(
        body, name="gate_bwd", grid=(S // tm,),
        in_specs=[row(0), row(1), row(0), wspec, wspec, _full((1, D_HEADS)), _full((SG_BLOCK, D_HEADS)),
                  _full((D_HEADS, D_HEADS)), _full((D_HEADS, LANES))] + [_ANY] * ns,
        out_specs=[row(0), row(0), wspec, _full((SG_BLOCK, LANES)), _full((1, D_HEADS))] + [_ANY] * ns,
        out_shape=[jax.ShapeDtypeStruct((S, D_HEADS), BF16), jax.ShapeDtypeStruct((S, D_HEADS), BF16),
                   jax.ShapeDtypeStruct((N_HEADS, SG_BLOCK, SG_BLOCK), F32), jax.ShapeDtypeStruct((SG_BLOCK, LANES), F32),
                   jax.ShapeDtypeStruct((1, D_HEADS), F32)] + _swap_shapes(swap),
        scratch_shapes=[pltpu.VMEM((tm, D_HEADS), F32), pltpu.VMEM((SG_BLOCK, D_HEADS), F32)] + _swap_sems(ns),
        compiler_params=_params(("arbitrary",)),
    )(z, z, dcat, w_mask, w_mask_t, ln_row, b_full, seg_avg, head_ind, *swap)
    return out[:5], out[5:]


def _attn_pack_grad(o, dcat, qa, lse, head_ind, k, tm):
    S = o.shape[0]

    def body(o_ref, do_ref, qa_ref, lse_ref, ind_ref, pl_ref, pt_ref, eye_ref, ds_ref, dst_ref, ls_ref, lst_ref,
             dop_ref, qb_ref, dot_ref, qbt_ref):
        do = do_ref[...]
        delta = _split_dot(o_ref[...].astype(F32) * do.astype(F32), ind_ref[...])
        hi = delta.astype(BF16).astype(F32)
        parts = (hi + pltpu.roll((delta - hi).astype(BF16).astype(F32), N_HEADS, 1)).astype(BF16)
        dop = jnp.dot(do, pl_ref[...], preferred_element_type=F32) - jnp.dot(parts, ds_ref[...], preferred_element_type=F32)
        for g in range(GROUPS):
            dop_ref[g] = dop[:, g * GROUP_PAD:(g + 1) * GROUP_PAD].astype(BF16)
        dot = lax.dot_general(pt_ref[...], do, NT, preferred_element_type=F32)
        dot_ref[...] = (dot - lax.dot_general(dst_ref[...], parts, NT, preferred_element_type=F32)).astype(BF16)
        qa = qa_ref[...]
        stack = jnp.concatenate(_split3(lse_ref[...]), axis=0)
        qb = qa.astype(F32) - lax.dot_general(stack, ls_ref[...], TN, preferred_element_type=F32)
        qbt = lax.dot_general(eye_ref[...], qa, NT, preferred_element_type=F32)
        qbt = qbt - jnp.dot(lst_ref[...], stack, preferred_element_type=F32)
        for g in range(GROUPS):
            qb_ref[g] = qb[:, g * GROUP_PAD:(g + 1) * GROUP_PAD].astype(BF16)
        qbt_ref[...] = qbt.astype(BF16)

    pad = pl.BlockSpec((tm, D_PAD), lambda i: (i, 0))
    padt = pl.BlockSpec((None, D_PAD, tm), lambda i: (i, 0, 0))
    return pl.pallas_call(
        body, name="attn_pack_grad", grid=(S // tm,),
        in_specs=[pl.BlockSpec((tm, D_HEADS), lambda i: (i, 0)), pl.BlockSpec((tm, D_HEADS), lambda i: (i, 1)), pad,
                  pl.BlockSpec((STAT_ROWS, tm), lambda i: (0, i)), _full((D_HEADS, LANES)), _full((D_HEADS, D_PAD)),
                  _full((D_PAD, D_HEADS)), _full((D_PAD, D_PAD)), _full((LANES, D_PAD)), _full((D_PAD, LANES)),
                  _full((3 * STAT_ROWS, D_PAD)), _full((D_PAD, 3 * STAT_ROWS))],
        out_specs=[pl.BlockSpec((GROUPS, tm, GROUP_PAD), lambda i: (0, i, 0))] * 2 + [padt, padt],
        out_shape=[jax.ShapeDtypeStruct((GROUPS, S, GROUP_PAD), BF16)] * 2 + [jax.ShapeDtypeStruct((S // tm, D_PAD, tm), BF16)] * 2,
        compiler_params=_params(("parallel",)),
    )(o, dcat, qa, lse, head_ind, k["place"], k["place_t"], jnp.eye(D_PAD, dtype=BF16), k["d_stat"], k["d_stat"].T,
      k["l_stat"], k["l_stat"].T)


def _attn_bwd(qb, qbt, ka, va, dop, dopt, k, tq, sums16):
    S = ka.shape[0]
    n = S // tq
    ns = len(sums16)

    pairs = [(kb, q) for kb in range(n) for q in range(kb, n)]
    k_of = jnp.asarray([kb for kb, _ in pairs], jnp.int32)
    q_of = jnp.asarray([q for _, q in pairs], jnp.int32)

    def body(k_of_ref, q_of_ref, q_ref, qt_ref, k_ref, v_ref, do_ref, dot_ref, pt_ref, *rest):
        dq_hbm, dcr_hbm, dk_ref, dv_ref, dcc_ref = rest[ns:ns + 5]
        dq_s, dcr_s, dk_s, dv_s, dcc_s = rest[2 * ns + 5:2 * ns + 10]
        s_s, d_s = rest[2 * ns + 10:2 * ns + 12], rest[2 * ns + 12:2 * ns + 14]
        sems = rest[2 * ns + 14]
        scatter_start, scatter_finish = _scatter_ops(rest[:ns], rest[ns + 5:2 * ns + 5], *rest[2 * ns + 15:])
        g = pl.program_id(0)
        ki, qi = k_of_ref[pl.program_id(1)], q_of_ref[pl.program_id(1)]

        @pl.when((g == 0) & (ki == 0) & (qi == 0))
        def _():
            scatter_start()

        @pl.when((ki == 0) & (qi == 0))
        def _():
            dq_s[...] = jnp.zeros_like(dq_s)
            dcr_s[...] = jnp.zeros_like(dcr_s)

        @pl.when(qi == ki)
        def _():
            dk_s[...] = jnp.zeros_like(dk_s)
            dv_s[...] = jnp.zeros_like(dv_s)
            dcc_s[...] = jnp.zeros_like(dcc_s)

        def step(diagonal):
            chunks = [slice(c * KEY_CHUNK, (c + 1) * KEY_CHUNK) for c in range(tq // KEY_CHUNK)]

            def keys_of(rows):
                return slice(0, rows.stop) if diagonal else slice(0, tq)

            def scores(hh, rows, slot):
                sl, keys = slice(hh * HEAD_PAD, (hh + 1) * HEAD_PAD), keys_of(rows)
                s_s[slot][rows, keys] = lax.dot_general(q_ref[rows, sl], k_ref[keys, sl], NT, preferred_element_type=F32)
                d_s[slot][rows, keys] = lax.dot_general(do_ref[rows, sl], v_ref[keys, sl], NT, preferred_element_type=F32)

            for rows in chunks:
                scores(0, rows, 0)
            for hh in range(GROUP_HEADS):
                sl = slice(hh * HEAD_PAD, (hh + 1) * HEAD_PAD)
                slot = hh % 2
                for rows in chunks:
                    if hh + 1 < GROUP_HEADS:
                        scores(hh + 1, rows, 1 - slot)
                    keys = keys_of(rows)
                    p = jnp.exp2(s_s[slot][rows, keys])
                    if diagonal:
                        row = rows.start + lax.broadcasted_iota(jnp.int32, (KEY_CHUNK, keys.stop), 0)
                        col = lax.broadcasted_iota(jnp.int32, (KEY_CHUNK, keys.stop), 1)
                        p = jnp.where(row >= col, p, 0.0)
                    ds = p * d_s[slot][rows, keys]
                    qrows = pl.ds(pl.multiple_of(qi * tq + rows.start, KEY_CHUNK), KEY_CHUNK)
                    dcc_s[hh:hh + 1, keys] += jnp.sum(ds, axis=0, keepdims=True)
                    dcr_s[qrows, hh:hh + 1] += jnp.sum(ds, axis=1, keepdims=True)
                    ds = ds.astype(BF16)
                    dv_s[sl, keys] += jnp.dot(dot_ref[sl, rows], p.astype(BF16), preferred_element_type=F32)
                    dk_s[sl, keys] += jnp.dot(qt_ref[sl, rows], ds, preferred_element_type=F32)
                    dq_s[qrows, sl] += jnp.dot(ds, k_ref[keys, sl], preferred_element_type=F32)

        @pl.when(qi > ki)
        def _():
            step(False)

        @pl.when(qi == ki)
        def _():
            step(True)

        @pl.when(qi == n - 1)
        def _():
            dk = dk_s[...]
            pt = pt_ref[...]
            dk_ref[...] = lax.dot_general((dk * (1.0 / LOG2E)).astype(BF16), pt, TN, preferred_element_type=F32).astype(BF16)
            dv_ref[...] = lax.dot_general(dv_s[...].astype(BF16), pt, TN, preferred_element_type=F32).astype(BF16)
            dcc_ref[...] = dcc_s[...]

        @pl.when((ki == n - 1) & (qi == n - 1))
        def _():
            copies = [pltpu.make_async_copy(dq_s, dq_hbm.at[g], sems.at[0]), pltpu.make_async_copy(dcr_s, dcr_hbm.at[g], sems.at[1])]
            for cp in copies:
                cp.start()
            for cp in copies:
                cp.wait()

        @pl.when((g == GROUPS - 1) & (ki == n - 1) & (qi == n - 1))
        def _():
            scatter_finish()

    gw = GROUP_HEADS * HEAD_DIM
    qspec = pl.BlockSpec((None, tq, GROUP_PAD), lambda g, i, ks, qs: (g, qs[i], 0))
    qtspec = pl.BlockSpec((None, GROUP_PAD, tq), lambda g, i, ks, qs: (qs[i], g, 0))
    kspec = pl.BlockSpec((tq, GROUP_PAD), lambda g, i, ks, qs: (ks[i], g))
    kout = pl.BlockSpec((tq, gw), lambda g, i, ks, qs: (ks[i], g))
    out = pl.pallas_call(
        body, name="attn_bwd",
        grid_spec=pltpu.PrefetchScalarGridSpec(
            num_scalar_prefetch=2, grid=(GROUPS, len(pairs)),
            in_specs=[qspec, qtspec, kspec, kspec, qspec, qtspec, pl.BlockSpec((GROUP_PAD, gw), lambda g, i, ks, qs: (0, 0))]
            + [_ANY] * ns,
            out_specs=[_ANY, _ANY, kout, kout, pl.BlockSpec((None, SUBLANES, tq), lambda g, i, ks, qs: (g, 0, ks[i]))] + [_ANY] * ns,
            scratch_shapes=[pltpu.VMEM((S, GROUP_PAD), F32), pltpu.VMEM((S, LANES), F32), pltpu.VMEM((GROUP_PAD, tq), F32),
                            pltpu.VMEM((GROUP_PAD, tq), F32), pltpu.VMEM((SUBLANES, tq), F32),
                            pltpu.VMEM((tq, tq), F32), pltpu.VMEM((tq, tq), F32), pltpu.VMEM((tq, tq), F32),
                            pltpu.VMEM((tq, tq), F32), pltpu.SemaphoreType.DMA((2,))]
            + _scatter_sems(ns)),
        out_shape=[jax.ShapeDtypeStruct((GROUPS, S, GROUP_PAD), F32), jax.ShapeDtypeStruct((GROUPS, S, LANES), F32),
                   jax.ShapeDtypeStruct((S, D_HEADS), BF16), jax.ShapeDtypeStruct((S, D_HEADS), BF16),
                   jax.ShapeDtypeStruct((GROUPS, SUBLANES, S), F32)]
        + _scatter_shapes(sums16),
        compiler_params=_params(("arbitrary", "arbitrary")),
    )(k_of, q_of, qb, qbt, ka, va, dop, dopt, k["place_t_group"], *sums16)
    return out[0], out[1], out[2], out[3], out[4], out[5:]


def _attn_unpack(dqp, k, tm):
    S = dqp.shape[1]
    gw = GROUP_HEADS * HEAD_DIM

    def body(dqp_ref, pt_ref, dq_ref):
        for g in range(GROUPS):
            dq_ref[:, g * gw:(g + 1) * gw] = jnp.dot((dqp_ref[g] * SCALE).astype(BF16), pt_ref[...],
                                                     preferred_element_type=F32).astype(BF16)

    return pl.pallas_call(
        body, name="attn_unpack", grid=(S // tm,),
        in_specs=[pl.BlockSpec((GROUPS, tm, GROUP_PAD), lambda i: (0, i, 0)), _full((GROUP_PAD, gw))],
        out_specs=pl.BlockSpec((tm, D_HEADS), lambda i: (i, 0)),
        out_shape=jax.ShapeDtypeStruct((S, D_HEADS), BF16),
        compiler_params=_params(("parallel",)),
    )(dqp, k["place_t_group"])


def _fox_bwd(dc, f, bias_row, tb):
    S = f.shape[0]
    nb = S // tb

    def body(dc_ref, f_ref, b_ref, df_ref, dbias_ref, carry):
        @pl.when(pl.program_id(0) == 0)
        def _():
            carry[...] = jnp.zeros_like(carry)
            dbias_ref[...] = jnp.zeros_like(dbias_ref)

        r = lax.broadcasted_iota(jnp.int32, (tb, tb), 0)
        s = lax.broadcasted_iota(jnp.int32, (tb, tb), 1)
        tri = (s >= r).astype(F32)
        rc = jnp.dot(tri, dc_ref[...], precision=lax.Precision.HIGHEST, preferred_element_type=F32) + carry[0:1, :]
        carry[...] = jnp.broadcast_to(rc[0:1, :], carry.shape)
        lane = lax.broadcasted_iota(jnp.int32, (tb, LANES), 1)
        df = jnp.where(lane < N_HEADS, rc * jax.nn.sigmoid(-(f_ref[...] + b_ref[...])), 0.0)
        df_ref[...] = df.astype(BF16)
        dbias_ref[...] += jnp.sum(df, axis=0, keepdims=True)

    rev = pl.BlockSpec((tb, LANES), lambda i: (nb - 1 - i, 0))
    return pl.pallas_call(
        body, name="fox_bwd", grid=(nb,),
        in_specs=[rev, rev, _full((1, LANES))],
        out_specs=[rev, _full((1, LANES))],
        out_shape=[jax.ShapeDtypeStruct((S, LANES), BF16), jax.ShapeDtypeStruct((1, LANES), F32)],
        scratch_shapes=[pltpu.VMEM((SUBLANES, LANES), F32)],
        compiler_params=_params(("arbitrary",)),
    )(dc, f, bias_row)


_DZ_WIDTHS = (D_HEADS,) * 5 + (LANES,)


def _in_bwd(pieces, w_in, x, g1, dx1, tm, sums16):
    S = x.shape[0]
    ns = len(sums16)

    def body(*refs):
        p_refs, (w_ref, x_ref, g_ref, dx1_ref) = refs[:6], refs[6:10]
        dx_ref, dg_ref = refs[10 + ns:12 + ns]
        scatter_start, scatter_finish = _scatter_ops(refs[10:10 + ns], refs[12 + ns:12 + 2 * ns], *refs[12 + 2 * ns:])

        @pl.when(pl.program_id(0) == 0)
        def _():
            dg_ref[...] = jnp.zeros_like(dg_ref)
            scatter_start()

        dh = jnp.zeros((tm, D_MODEL), F32)
        off = 0
        for p_ref, w in zip(p_refs, _DZ_WIDTHS):
            dh = dh + lax.dot_general(p_ref[...].astype(BF16), w_ref[:, off:off + w], NT, preferred_element_type=F32)
            off += w
        dx, dg = _rms_bwd(dh, x_ref[...], g_ref[...])
        dg_ref[...] += dg
        dx_ref[...] = dx1_ref[...] + dx

        @pl.when(pl.program_id(0) == pl.num_programs(0) - 1)
        def _():
            scatter_finish()

    row = lambda w: pl.BlockSpec((tm, w), lambda i: (i, 0))
    out = pl.pallas_call(
        body, name="in_bwd", grid=(S // tm,),
        in_specs=[row(w) for w in _DZ_WIDTHS] + [_full((D_MODEL, D_IN_PAD)), row(D_MODEL), _full((1, D_MODEL)), row(D_MODEL)]
        + [_ANY] * ns,
        out_specs=[row(D_MODEL), _full((1, D_MODEL))] + [_ANY] * ns,
        out_shape=[jax.ShapeDtypeStruct((S, D_MODEL), F32), jax.ShapeDtypeStruct((1, D_MODEL), F32)] + _scatter_shapes(sums16),
        scratch_shapes=_scatter_sems(ns),
        compiler_params=_params(("arbitrary",)),
    )(*pieces, w_in, x, g1, dx1, *sums16)
    return out[0], out[1], out[2:]


def _dw_in(h1, pieces, tk):
    S = h1.shape[0]

    def body(*refs):
        h_ref, p_refs, o_ref = refs[0], refs[1:7], refs[7]

        @pl.when(pl.program_id(0) == 0)
        def _():
            o_ref[...] = jnp.zeros_like(o_ref)

        off = 0
        for p_ref, w in zip(p_refs, _DZ_WIDTHS):
            o_ref[:, off:off + w] += lax.dot_general(h_ref[...], p_ref[...].astype(BF16), TN, preferred_element_type=F32)
            off += w

    row = lambda w: pl.BlockSpec((tk, w), lambda k: (k, 0))
    return pl.pallas_call(
        body, name="dw_in", grid=(S // tk,),
        in_specs=[row(D_MODEL)] + [row(w) for w in _DZ_WIDTHS],
        out_specs=_full((D_MODEL, D_IN_PAD)),
        out_shape=jax.ShapeDtypeStruct((D_MODEL, D_IN_PAD), F32),
        compiler_params=_params(("arbitrary",)),
    )(h1, *pieces)


def _adamw_math(w, g, m, v):
    m = ADAM_B1 * m + (1.0 - ADAM_B1) * g
    v = ADAM_B2 * v + (1.0 - ADAM_B2) * (g * g)
    m_hat = m / (1.0 - ADAM_B1 ** ADAM_STEP)
    v_hat = v / (1.0 - ADAM_B2 ** ADAM_STEP)
    delta = -ADAM_LR * (m_hat / (jnp.sqrt(v_hat) + ADAM_EPS) + ADAM_WD * w)
    return delta, m, v


def _adamw(name, w, g, m, v):
    R, C = w.shape
    tr = _row_tile(R, 256)

    def body(w_ref, g_ref, m_ref, v_ref, go_ref, d_ref, nm_ref, nv_ref):
        g = g_ref[...]
        d, nm, nv = _adamw_math(w_ref[...], g, m_ref[...], v_ref[...])
        go_ref[...] = g
        d_ref[...] = d
        nm_ref[...] = nm
        nv_ref[...] = nv

    spec = pl.BlockSpec((tr, C), lambda i: (i, 0))
    return pl.pallas_call(
        body, name=name, grid=(R // tr,), in_specs=[spec] * 4, out_specs=[spec] * 4,
        out_shape=[jax.ShapeDtypeStruct((R, C), F32)] * 4,
        compiler_params=_params(("parallel",)),
    )(w, g, m, v)


def _pair_sum(name, grad, theirs, ids):
    q, half, C = theirs.shape
    tr = _row_tile(half, 256)
    nb = half // tr

    def body(ids_ref, a_ref, b_ref, sb_ref):
        sb_ref[...] = (a_ref[...] + b_ref[...]).astype(BF16)

    here = pl.BlockSpec((None, tr, C), lambda j, i, ids: (j, i, 0))
    return pl.pallas_call(
        body, name=name,
        grid_spec=pltpu.PrefetchScalarGridSpec(
            num_scalar_prefetch=1, grid=(q, nb),
            in_specs=[pl.BlockSpec((None, tr, C), lambda j, i, ids: (j, ids[1] * nb + i, 0)), here],
            out_specs=here),
        out_shape=jax.ShapeDtypeStruct((q, half, C), BF16),
        compiler_params=_params(("parallel", "parallel")),
    )(ids, grad, theirs)


def _chip_sum(name, grad, theirs, others, ids):
    _, half, C = theirs.shape
    tr = _row_tile(half, 256)
    nb = half // tr

    def body(ids_ref, a_ref, b_ref, o_ref, s_ref):
        s = a_ref[...] + b_ref[...]
        for j in range(3):
            s = s + o_ref[j].astype(F32)
        s_ref[...] = s

    return pl.pallas_call(
        body, name=name,
        grid_spec=pltpu.PrefetchScalarGridSpec(
            num_scalar_prefetch=1, grid=(nb,),
            in_specs=[pl.BlockSpec((None, tr, C), lambda i, ids: (ids[0], ids[1] * nb + i, 0)),
                      pl.BlockSpec((None, tr, C), lambda i, ids: (ids[0], i, 0)),
                      pl.BlockSpec((3, tr, C), lambda i, ids: (0, i, 0))],
            out_specs=pl.BlockSpec((tr, C), lambda i, ids: (ids[1] * nb + i, 0))),
        out_shape=jax.ShapeDtypeStruct((2 * half, C), F32),
        compiler_params=_params(("parallel",)),
    )(ids, grad, theirs, others)


def _place():
    return lax.axis_index("x"), lax.axis_index("y"), lax.axis_index("c")


def _other_chips(x, y):
    return [(1 - x, y), (x, 1 - y), (1 - x, 1 - y)]


_ANY = pl.BlockSpec(memory_space=pl.ANY)


def _gather_quarters(shards):
    n = len(shards)

    def body(*refs):
        start, hand_on, finish = _gather_ops(refs[:n], refs[n:2 * n], *refs[2 * n:])
        start()
        hand_on()
        finish()

    return pl.pallas_call(
        body, name="gather_weights",
        in_specs=[_ANY] * n, out_specs=[_ANY] * n,
        out_shape=_gather_shapes(shards), scratch_shapes=_gather_sems(n),
    )(*shards)


def _gather_shapes(shards):
    return [jax.ShapeDtypeStruct((4,) + s.shape, s.dtype) for s in shards]


def _gather_sems(n):
    return [pltpu.SemaphoreType.DMA((n, 3))] * 4 + [pltpu.SemaphoreType.DMA((n,))]


def _gather_ops(ins, outs, send_sems, recv_sems, pass_send_sems, pass_recv_sems, own_sems):
    n = len(ins)
    halved = [r.shape[0] % 32 == 0 for r in ins]

    def part(a, quarter, core):
        if not halved[a]:
            return outs[a].at[quarter]
        half = ins[a].shape[0] // 2
        return outs[a].at[quarter, pl.ds(core * half, half), :]

    def ici(a, j, quarter):
        x, y, c = _place()
        px, py = _other_chips(x, y)[j]
        src = ins[a]
        if halved[a]:
            half = src.shape[0] // 2
            src = src.at[pl.ds(c * half, half), :]
        return pltpu.make_async_remote_copy(src_ref=src, dst_ref=part(a, quarter, c), send_sem=send_sems.at[a, j],
                                            recv_sem=recv_sems.at[a, j], device_id=(px, py, c), device_id_type=MESH)

    def passed(a, j, core):
        x, y, c = _place()
        px, py = _other_chips(x, y)[j]
        half = part(a, 2 * px + py, core)
        return pltpu.make_async_remote_copy(src_ref=half, dst_ref=half, send_sem=pass_send_sems.at[a, j],
                                            recv_sem=pass_recv_sems.at[a, j], device_id=(x, y, 1 - c), device_id_type=MESH)

    def own(a):
        x, y, _ = _place()
        return pltpu.make_async_copy(ins[a], outs[a].at[2 * x + y], own_sems.at[a])

    def start():
        x, y, _ = _place()
        for a in range(n):
            for j in range(3):
                ici(a, j, 2 * x + y).start()
            own(a).start()

    def hand_on():
        x, y, c = _place()
        for a in range(n):
            for j, (px, py) in enumerate(_other_chips(x, y)):
                ici(a, j, 2 * px + py).wait_recv()
                if halved[a]:
                    passed(a, j, c).start()

    def finish():
        x, y, c = _place()
        for a in range(n):
            for j in range(3):
                if halved[a]:
                    passed(a, j, 1 - c).wait_recv()
                    passed(a, j, c).wait_send()
                ici(a, j, 2 * x + y).wait_send()
            own(a).wait()

    return start, hand_on, finish


def _swap_halves(grads, name):
    n = len(grads)

    def body(*refs):
        start, finish = _swap_ops(refs[:n], refs[n:2 * n], *refs[2 * n:])
        start()
        finish()

    return pl.pallas_call(
        body, name=name,
        in_specs=[_ANY] * n, out_specs=[_ANY] * n, out_shape=_swap_shapes(grads), scratch_shapes=_swap_sems(n),
    )(*grads)


def _swap_shapes(grads):
    return [jax.ShapeDtypeStruct((4, g.shape[1] // 2, g.shape[2]), F32) for g in grads]


def _swap_sems(n):
    return [pltpu.SemaphoreType.DMA((n,))] * 2


def _swap_ops(ins, outs, send_sems, recv_sems):
    def copy(a):
        x, y, c = _place()
        half = ins[a].shape[1] // 2
        return pltpu.make_async_remote_copy(src_ref=ins[a].at[:, pl.ds((1 - c) * half, half), :], dst_ref=outs[a],
                                            send_sem=send_sems.at[a], recv_sem=recv_sems.at[a],
                                            device_id=(x, y, 1 - c), device_id_type=MESH)

    def start():
        for a in range(len(ins)):
            copy(a).start()

    def finish():
        for a in range(len(ins)):
            copy(a).wait()

    return start, finish


def _scatter_shapes(sums16):
    return [jax.ShapeDtypeStruct((3,) + s.shape[1:], BF16) for s in sums16]


def _scatter_sems(n):
    return [pltpu.SemaphoreType.DMA((n, 3))] * 2


def _scatter_ops(ins, outs, send_sems, recv_sems):
    n = len(ins)

    def copy(a, j):
        x, y, c = _place()
        px, py = _other_chips(x, y)[j]
        return pltpu.make_async_remote_copy(src_ref=ins[a].at[2 * px + py], dst_ref=outs[a].at[j], send_sem=send_sems.at[a, j],
                                            recv_sem=recv_sems.at[a, j], device_id=(px, py, c), device_id_type=MESH)

    def start():
        for a in range(n):
            for j in range(3):
                copy(a, j).start()

    def finish():
        for a in range(n):
            for j in range(3):
                copy(a, j).wait()

    return start, finish


def _join_halves(fulls):
    n = len(fulls)

    def body(*refs):
        ins, outs = refs[:n], refs[n:2 * n]
        send_sems, recv_sems = refs[2 * n:]
        x, y, c = _place()
        started = []
        for a in range(n):
            half = ins[a].shape[0] // 2
            rows = pl.ds(c * half, half)
            cp = pltpu.make_async_remote_copy(src_ref=ins[a].at[rows, :], dst_ref=outs[a].at[rows, :], send_sem=send_sems.at[a],
                                              recv_sem=recv_sems.at[a], device_id=(x, y, 1 - c), device_id_type=MESH)
            cp.start()
            started.append(cp)
        for cp in started:
            cp.wait()

    return pl.pallas_call(
        body, name="join_halves",
        in_specs=[_ANY] * n, out_specs=[_ANY] * n,
        out_shape=[jax.ShapeDtypeStruct(f.shape, F32) for f in fulls],
        input_output_aliases={a: a for a in range(n)},
        scratch_shapes=[pltpu.SemaphoreType.DMA((n,)), pltpu.SemaphoreType.DMA((n,))],
    )(*fulls)


def _small_allreduce(g):
    R = g.shape[0]
    half = R // 2

    def body(g_ref, out_ref, other_s, chip_s, parts_s, send_sems, recv_sems):
        x, y, c = _place()
        mine = 2 * x + y
        rows = pl.ds(pl.multiple_of(c * half, SUBLANES), half)

        def to_other_core(src, dst, k):
            return pltpu.make_async_remote_copy(src_ref=src, dst_ref=dst, send_sem=send_sems.at[k], recv_sem=recv_sems.at[k],
                                                device_id=(x, y, 1 - c), device_id_type=MESH)

        swap = to_other_core(g_ref, other_s, 0)
        swap.start()
        swap.wait()
        chip_s[...] = g_ref[...] + other_s[...]
        parts_s[mine] = chip_s[rows, :]
        sends = []
        for j, (px, py) in enumerate(_other_chips(x, y)):
            cp = pltpu.make_async_remote_copy(src_ref=chip_s.at[rows, :], dst_ref=parts_s.at[mine], send_sem=send_sems.at[1 + j],
                                              recv_sem=recv_sems.at[1 + j], device_id=(px, py, c), device_id_type=MESH)
            cp.start()
            sends.append(cp)
        for cp in sends:
            cp.wait()
        out_ref[rows, :] = (parts_s[0] + parts_s[1]) + (parts_s[2] + parts_s[3])
        join = to_other_core(out_ref.at[rows, :], out_ref.at[rows, :], 4)
        join.start()
        join.wait()

    vm = pl.BlockSpec(memory_space=pltpu.VMEM)
    return pl.pallas_call(
        body, name="small_allreduce",
        in_specs=[vm], out_specs=vm, out_shape=jax.ShapeDtypeStruct((R, LANES), F32),
        scratch_shapes=[pltpu.VMEM((R, LANES), F32), pltpu.VMEM((R, LANES), F32), pltpu.VMEM((4, half, LANES), F32),
                        pltpu.SemaphoreType.DMA((5,)), pltpu.SemaphoreType.DMA((5,))],
        compiler_params=pltpu.CompilerParams(vmem_limit_bytes=VMEM_LIMIT),
    )(g)


def _adamw_small(ws, gs, ms, vs):
    n = len(ws)

    def body(*refs):
        for k in range(n):
            w_ref, g_ref, m_ref, v_ref = (refs[j * n + k] for j in range(4))
            d, nm, nv = _adamw_math(w_ref[...], g_ref[...], m_ref[...], v_ref[...])
            refs[4 * n + k][...] = d
            refs[5 * n + k][...] = nm
            refs[6 * n + k][...] = nv

    vm = pl.BlockSpec(memory_space=pltpu.VMEM)
    out = pl.pallas_call(
        body, name="adamw_small",
        in_specs=[vm] * (4 * n), out_specs=[vm] * (3 * n),
        out_shape=[jax.ShapeDtypeStruct(w.shape, F32) for w in ws] * 3,
        compiler_params=pltpu.CompilerParams(vmem_limit_bytes=VMEM_LIMIT),
    )(*ws, *gs, *ms, *vs)
    return out[:n], out[n:2 * n], out[2 * n:]


_SMALL = (("norm_mix_g", D_MODEL), ("f_bias", N_HEADS), ("sg_ln_g", D_HEADS), ("sg_w", N_HEADS * SG_BLOCK * SG_BLOCK),
          ("sg_b", N_HEADS * SG_BLOCK), ("norm_ffn_g", D_MODEL), ("w_conv", 3 * 2 * D_FF), ("b_conv", 2 * D_FF),
          ("norm_final_g", D_MODEL))
_PACKED = _SMALL + (("sq_err", D_MODEL),)


def _pack_small(parts):
    rows = []
    for name, size in _PACKED:
        flat = parts[name].reshape(-1).astype(F32)
        pad = (-size) % (SUBLANES * LANES)
        rows.append(jnp.pad(flat, (0, pad)).reshape(-1, LANES))
    packed = jnp.concatenate(rows, axis=0)
    return jnp.pad(packed, ((0, (-packed.shape[0]) % (2 * SUBLANES)), (0, 0)))


def _unpack_small(packed, shapes):
    out, r = {}, 0
    for name, size in _PACKED:
        nrows = (size + SUBLANES * LANES - 1) // (SUBLANES * LANES) * SUBLANES
        out[name] = packed[r:r + nrows].reshape(-1)[:size].reshape(shapes[name])
        r += nrows
    return out


def _local_step(x, target, g1, w_in, f_bias, sg_ln_g, sg_w, sg_b, g2, b_conv, g3, late_shards, ids):
    S = x.shape[0]
    tm = _row_tile(S, 512)
    tms = _row_tile(S, 256)
    tq = tm

    lane = jnp.arange(D_HEADS)
    seg_avg = jnp.where(lane[:, None] // HEAD_DIM == lane[None, :] // HEAD_DIM, 1.0 / HEAD_DIM, 0.0).astype(BF16)
    head_ind = (lane[:, None] // HEAD_DIM == jnp.arange(LANES)[None, :]).astype(BF16)
    pos_chunk = jnp.arange(SG_BLOCK) // CHUNK
    w_mask32 = jnp.where(pos_chunk[:, None] >= pos_chunk[None, :], sg_w, 0.0)
    w_mask = w_mask32.astype(BF16)
    w_mask_t = jnp.swapaxes(w_mask32, 1, 2).astype(BF16)
    ln_row = sg_ln_g.reshape(1, D_HEADS)
    b_full = jnp.repeat(sg_b.T, HEAD_DIM, axis=1)
    bias_row = jnp.pad(f_bias.reshape(1, N_HEADS), ((0, 0), (0, LANES - N_HEADS)))
    b_conv_row = b_conv.reshape(1, 2 * D_FF)

    z, f, h1 = _in_proj(x, g1, w_in, tm)
    c = _fox_prep(f, bias_row, tm)
    consts = _attn_consts()
    qa, ka, va, vat = _attn_pack(z, c, consts, tm)
    out_b, lse, gathered = _attn_fwd(qa, ka, vat, consts["place_t"], tq, late_shards)
    g_out, w_up_q, g_down, g_conv = gathered
    w_out = g_out.reshape(D_MODEL, D_MODEL)
    w_down = g_down.reshape(D_FF, D_MODEL)
    w_conv = jnp.concatenate([g_conv[q] for q in range(4)], axis=1)
    out_a = _gate_fwd(z, w_mask, ln_row, b_full, seg_avg, tm)
    x1, h2 = _mix_out(x, out_a, out_b, w_out, g2, tm)
    a = _up_proj(h2, w_up_q, tm)
    dx2, sq_err, dg3 = _ffn_fwd_loss(a, w_conv, b_conv_row, w_down, x1, g3, target, tm)

    dconv, y, dw_conv8, db_conv = _ffn_bwd_gate(dx2, a, w_conv, b_conv_row, w_down, tms)
    dact = _conv_bwd(dconv, w_conv, tms, D_FF)
    dw_down = _matmul_tn(y, dx2, "dw_down", D_FF // 2, D_MODEL, tm, quarters=(2, 1))
    dx1, dg2 = _up_bwd(dact, w_up_q, x1, g2, dx2, tm)
    dw_up_q = _matmul_tn(h2, dact, "dw_up", D_MODEL, 2 * D_FF // 4, tm, quarters=(1, 4))
    dcat = _out_bwd(dx1, w_out, tm)
    dw_out = _dw_out(out_a, out_b, dx1, tm)
    early = {"w_down": dw_down.reshape(4, D_FF // 4, D_MODEL), "w_up": dw_up_q,
             "w_out": dw_out.reshape(4, D_MODEL // 4, D_MODEL)}
    (dzu, dzv, dsg_w, dsg_b_t, dln), theirs = _gate_bwd(z, dcat, w_mask, w_mask_t, ln_row, b_full, seg_avg, head_ind, tm,
                                                        list(early.values()))
    early_sums = _chip_sums(early, theirs, ids)
    dop, qb, dopt, qbt = _attn_pack_grad(out_b, dcat, qa, lse, head_ind, consts, tm)
    dqp, dc_rows, dk, dv, dc_cols, landed = _attn_bwd(qb, qbt, ka, va, dop, dopt, consts, tq,
                                                      [s16 for _, s16 in early_sums.values()])
    early_parts = {k: (s32, got) for (k, (s32, _)), got in zip(early_sums.items(), landed)}
    dq = _attn_unpack(dqp, consts, tm)
    dc_rows = jnp.concatenate([dc_rows[g][:, :GROUP_HEADS] for g in range(GROUPS)], axis=1)
    dc_cols = jnp.concatenate([dc_cols[g][:GROUP_HEADS] for g in range(GROUPS)], axis=0).T
    dc = jnp.pad(dc_rows - dc_cols, ((0, 0), (0, LANES - N_HEADS)))
    df, dbias = _fox_bwd(dc, f, bias_row, tm)
    pieces = (dzu, dzv, dq, dk, dv, df)
    dw_in = _dw_in(h1, pieces, tm)[:, :D_IN].reshape(D_MODEL, 4, D_IN // 4).transpose(1, 0, 2)
    (w_in_sum, w_in_sum16), = _chip_sums({"w_in": dw_in}, _swap_halves([dw_in], "swap_halves"), ids).values()
    dx, dg1, (w_in_landed,) = _in_bwd(pieces, w_in, x, g1, dx1, tm, [w_in_sum16])

    grads = {
        "norm_mix_g": dg1, "f_bias": dbias[:, :N_HEADS], "sg_ln_g": dln, "sg_w": dsg_w, "sg_b": dsg_b_t[:, :N_HEADS].T,
        "norm_ffn_g": dg2, "w_conv": dw_conv8[:3], "b_conv": db_conv, "norm_final_g": dg3,
    }
    return sq_err, dx, grads, {**early_parts, "w_in": (w_in_sum, w_in_landed)}


def _chip_sums(grads_q, theirs, ids):
    return {k: ((g, t), _pair_sum("pair_sum_" + k, g, t, ids)) for (k, g), t in zip(grads_q.items(), theirs)}


def _finish_reduction(parts, ids):
    names = list(parts)
    fulls = [_chip_sum("chip_sum_" + k, g, t, got, ids) for k, ((g, t), got) in parts.items()]
    return dict(zip(names, _join_halves(fulls)))


def kernel(x, norm_mix_g, w_in, f_bias, sg_ln_g, sg_w, sg_b, w_out, norm_ffn_g, w_up, w_conv, b_conv, w_down, norm_final_g, loss_target, m_norm_mix_g, m_w_in, m_f_bias, m_sg_ln_g, m_sg_w, m_sg_b, m_w_out, m_norm_ffn_g, m_w_up, m_w_conv, m_b_conv, m_w_down, m_norm_final_g, v_norm_mix_g, v_w_in, v_f_bias, v_sg_ln_g, v_sg_w, v_sg_b, v_w_out, v_norm_ffn_g, v_w_up, v_w_conv, v_b_conv, v_w_down, v_norm_final_g):
    args = dict(locals())
    quarter = 2 * lax.axis_index("x") + lax.axis_index("y")
    ids = jnp.stack([quarter, lax.axis_index("c")]).astype(jnp.int32)
    wq_conv = w_conv.shape[-1]

    g_in = _gather_quarters([w_in[0].astype(BF16)])[0]
    w_in_full = jnp.pad(jnp.concatenate([g_in[q] for q in range(4)], axis=1), ((0, 0), (0, D_IN_PAD - D_IN)))
    late_shards = [w_out[0].astype(BF16), w_up[0].astype(BF16), w_down[0].astype(BF16), w_conv[0]]

    sq_err, dx, grads, parts = _local_step(
        x[0], loss_target[0], norm_mix_g, w_in_full, f_bias[0], sg_ln_g[0], sg_w[0], sg_b[0], norm_ffn_g, b_conv[0],
        norm_final_g.reshape(1, D_MODEL), late_shards, ids)
    big = _finish_reduction(parts, ids)

    out = {"grad_x": dx[None]}
    for k in ("w_in", "w_out", "w_up", "w_down"):
        g, d, nm, nv = _adamw("adamw_" + k, args[k][0], big[k], args["m_" + k][0], args["v_" + k][0])
        out["grad_" + k], out["delta_" + k], out["new_m_" + k], out["new_v_" + k] = g[None], d[None], nm[None], nv[None]

    small_names = [n for n, _ in _SMALL]
    shapes = {n: (3, 4 * wq_conv) if n == "w_conv" else args[n].shape for n in small_names}
    shapes["sq_err"] = sq_err.shape
    g_small = _unpack_small(_small_allreduce(_pack_small({**{n: grads[n] for n in small_names}, "sq_err": sq_err})), shapes)
    out["loss"] = 0.5 * jnp.sum(g_small.pop("sq_err")) / D_MODEL
    g_small["w_conv"] = lax.dynamic_slice(g_small["w_conv"], (0, quarter * wq_conv), (3, wq_conv))[None]
    flat2d = lambda t: t.reshape(-1, t.shape[-1])
    updated = _adamw_small(*[[flat2d(src[p + n]) for n in small_names] for src, p in
                             ((args, ""), (g_small, ""), (args, "m_"), (args, "v_"))])
    for n, g in g_small.items():
        out["grad_" + n] = g
    for prefix, arrs in zip(("delta_", "new_m_", "new_v_"), updated):
        for n, t in zip(small_names, arrs):
            out[prefix + n] = t.reshape(args[n].shape)

    weights = ["norm_mix_g", "w_in", "f_bias", "sg_ln_g", "sg_w", "sg_b", "w_out", "norm_ffn_g", "w_up", "w_conv", "b_conv",
               "w_down", "norm_final_g"]
    return (out["loss"], out["grad_x"], *[out[p + n] for p in ("grad_", "delta_", "new_m_", "new_v_") for n in weights])
```

```python
import functools
import math

import jax
import jax.numpy as jnp
from jax import lax
from jax.experimental import pallas as pl
from jax.experimental.pallas import tpu as pltpu

F32 = jnp.float32
BF16 = jnp.bfloat16
MESH = pl.DeviceIdType.MESH

D_MODEL = 1024
N_HEADS = 8
HEAD_DIM = 64
D_HEADS = N_HEADS * HEAD_DIM
SG_BLOCK = 128
CHUNK = 64
D_FF = 2816
D_IN = 2 * D_HEADS + 3 * D_HEADS + N_HEADS
LANES = 128
SUBLANES = 8
D_IN_PAD = 5 * D_HEADS + LANES
EPS = 1e-6
SCALE = HEAD_DIM ** -0.5
NEG = -1e30
LOG2E = 1.4426950408889634
HEAD_PAD = LANES
D_PAD = N_HEADS * HEAD_PAD
Q_STAT = HEAD_DIM
K_STAT = HEAD_DIM + 3
L_STAT = HEAD_DIM + 6
GROUPS = 2
GROUP_HEADS = N_HEADS // GROUPS
GROUP_PAD = GROUP_HEADS * HEAD_PAD
KEY_CHUNK = 256
FWD_KEY_CHUNK = 512
STAT_ROWS = 16
FF_CHUNK = 256

ADAM_LR = 0.001
ADAM_B1 = 0.9
ADAM_B2 = 0.999
ADAM_EPS = 1e-08
ADAM_WD = 0.01
ADAM_STEP = 10

VMEM_LIMIT = 56 * 1024 * 1024

NT = (((1,), (1,)), ((), ()))
TN = (((0,), (0,)), ((), ()))


def _params(sem):
    return pltpu.CompilerParams(dimension_semantics=sem, vmem_limit_bytes=VMEM_LIMIT)


def _full(shape):
    nd = len(shape)
    return pl.BlockSpec(shape, lambda *_: (0,) * nd)


def _row_tile(rows, target):
    best = None
    for t in range(SUBLANES, min(rows, target) + 1, SUBLANES):
        if rows % t == 0:
            best = t
    assert best is not None, rows
    return best


def _sigmoid(x):
    return 0.5 * jnp.tanh(0.5 * x) + 0.5


def _gelu(z):
    return 0.5 * z * (1.0 + lax.erf(z * (2.0 ** -0.5)))


def _gelu_grad(z):
    cdf = 0.5 * (1.0 + lax.erf(z * (2.0 ** -0.5)))
    pdf = jnp.exp(-0.5 * z * z) * (1.0 / math.sqrt(2.0 * math.pi))
    return cdf + z * pdf


def _split_dot(x, m):
    hi = x.astype(BF16)
    lo = (x - hi.astype(F32)).astype(BF16)
    return jnp.dot(hi, m, preferred_element_type=F32) + jnp.dot(lo, m, preferred_element_type=F32)


def _head_mask(h, rows):
    lane = lax.broadcasted_iota(jnp.int32, (rows, D_HEADS), 1)
    return (lane >= h * HEAD_DIM) & (lane < (h + 1) * HEAD_DIM)


def _rms_bwd(dh, x, g):
    r = lax.rsqrt(jnp.mean(x * x, axis=-1, keepdims=True) + EPS)
    xhat = x * r
    dg = jnp.sum(dh * xhat, axis=0, keepdims=True)
    dxhat = dh * g
    dx = r * (dxhat - xhat * jnp.mean(dxhat * xhat, axis=-1, keepdims=True))
    return dx, dg


def _in_proj(x, g1, w_in, tm):
    S = x.shape[0]
    nz = D_IN_PAD - LANES

    def body(x_ref, g_ref, w_ref, z_ref, f_ref, h_ref):
        xf = x_ref[...]
        r = lax.rsqrt(jnp.mean(xf * xf, axis=-1, keepdims=True) + EPS)
        h = (xf * r * g_ref[...]).astype(BF16)
        h_ref[...] = h
        zz = jnp.dot(h, w_ref[...], preferred_element_type=F32)
        z_ref[...] = zz[:, :nz].astype(BF16)
        f_ref[...] = zz[:, nz:]

    return pl.pallas_call(
        body, name="in_proj", grid=(S // tm,),
        in_specs=[pl.BlockSpec((tm, D_MODEL), lambda i: (i, 0)), _full((1, D_MODEL)), _full((D_MODEL, D_IN_PAD))],
        out_specs=[pl.BlockSpec((tm, nz), lambda i: (i, 0)), pl.BlockSpec((tm, LANES), lambda i: (i, 0)),
                   pl.BlockSpec((tm, D_MODEL), lambda i: (i, 0))],
        out_shape=[jax.ShapeDtypeStruct((S, nz), BF16), jax.ShapeDtypeStruct((S, LANES), F32),
                   jax.ShapeDtypeStruct((S, D_MODEL), BF16)],
        compiler_params=_params(("parallel",)),
    )(x, g1, w_in)


def _fox_prep(f, bias_row, tb):
    S = f.shape[0]

    def body(f_ref, b_ref, c_ref, carry):
        @pl.when(pl.program_id(0) == 0)
        def _():
            carry[...] = jnp.zeros_like(carry)

        xv = f_ref[...] + b_ref[...]
        lf = jnp.minimum(xv, 0.0) - jnp.log(1.0 + jnp.exp(-jnp.abs(xv)))
        r = lax.broadcasted_iota(jnp.int32, (tb, tb), 0)
        s = lax.broadcasted_iota(jnp.int32, (tb, tb), 1)
        tri = (r >= s).astype(BF16)
        cs = sum(jnp.dot(tri, part, preferred_element_type=F32) for part in _split3(lf)) + carry[0:1, :]
        c_ref[...] = cs
        carry[...] = jnp.broadcast_to(cs[tb - 1:tb, :], carry.shape)

    return pl.pallas_call(
        body, name="fox_prep", grid=(S // tb,),
        in_specs=[pl.BlockSpec((tb, LANES), lambda i: (i, 0)), _full((1, LANES))],
        out_specs=pl.BlockSpec((tb, LANES), lambda i: (i, 0)),
        out_shape=jax.ShapeDtypeStruct((S, LANES), F32),
        scratch_shapes=[pltpu.VMEM((SUBLANES, LANES), F32)],
        compiler_params=_params(("arbitrary",)),
    )(f, bias_row)


def _attn_consts():
    col = jnp.arange(D_PAD)
    row = jnp.arange(D_HEADS)
    head = jnp.arange(LANES)
    place = (row[:, None] // HEAD_DIM == col[None, :] // HEAD_PAD) & (row[:, None] % HEAD_DIM == col[None, :] % HEAD_PAD)

    def stat(offset):
        return ((head[:, None] < N_HEADS) & (col[None, :] == head[:, None] * HEAD_PAD + offset)).astype(BF16)

    def stat3(base):
        part, h = head // N_HEADS, head % N_HEADS
        return ((part[:, None] < 3) & (col[None, :] == h[:, None] * HEAD_PAD + base + part[:, None])).astype(BF16)

    def ones(offsets):
        return sum((col % HEAD_PAD == o) for o in offsets).astype(F32).reshape(1, D_PAD)

    place = place.astype(BF16)
    return {
        "place": place, "place_t": place.T, "place_t_group": place.T[:GROUP_PAD, :GROUP_HEADS * HEAD_DIM],
        "q_stat": stat3(Q_STAT), "k_stat": stat3(K_STAT),
        "d_stat": stat3(Q_STAT) * (head[:, None] < 2 * N_HEADS).astype(BF16),
        "l_stat": jnp.concatenate([stat(L_STAT + j)[:STAT_ROWS] for j in range(3)], axis=0),
        "q_ones": ones(range(K_STAT, K_STAT + 3)), "k_ones": ones(list(range(Q_STAT, Q_STAT + 3)) + list(range(L_STAT, L_STAT + 3))),
        "v_ones": ones(range(Q_STAT, Q_STAT + 2)),
    }


def _split3(x):
    hi = x.astype(BF16)
    r = x - hi.astype(F32)
    mid = r.astype(BF16)
    return hi, mid, (r - mid.astype(F32)).astype(BF16)


def _attn_pack(z, c, k, tm):
    S = z.shape[0]

    def body(q_ref, k_ref, v_ref, c_ref, pl_ref, pt_ref, qs_ref, ks_ref, qo_ref, ko_ref, vo_ref, voc_ref,
             qa_ref, ka_ref, va_ref, vt_ref):
        place = pl_ref[...]
        q = (q_ref[...].astype(F32) * (SCALE * LOG2E)).astype(BF16)
        qa = jnp.dot(q, place, preferred_element_type=F32) + qo_ref[...]
        ka = jnp.dot(k_ref[...], place, preferred_element_type=F32) + ko_ref[...]
        lane = lax.broadcasted_iota(jnp.int32, (tm, LANES), 1)
        hi, mid, lo = _split3(jnp.where(lane < N_HEADS, c_ref[...] * LOG2E, 0.0))
        parts = hi.astype(F32) + pltpu.roll(mid.astype(F32), N_HEADS, 1) + pltpu.roll(lo.astype(F32), 2 * N_HEADS, 1)
        parts = parts.astype(BF16)
        qa = qa + jnp.dot(parts, qs_ref[...], preferred_element_type=F32)
        ka = ka - jnp.dot(parts, ks_ref[...], preferred_element_type=F32)
        qa_ref[...] = qa.astype(BF16)
        ka_ref[...] = ka.astype(BF16)
        v = v_ref[...]
        va_ref[...] = (jnp.dot(v, place, preferred_element_type=F32) + vo_ref[...]).astype(BF16)
        vt_ref[...] = (lax.dot_general(pt_ref[...], v, NT, preferred_element_type=F32) + voc_ref[...]).astype(BF16)

    blk = lambda col: pl.BlockSpec((tm, D_HEADS), lambda i: (i, col))
    out = pl.BlockSpec((tm, D_PAD), lambda i: (i, 0))
    pad = jax.ShapeDtypeStruct((S, D_PAD), BF16)
    return pl.pallas_call(
        body, name="attn_pack", grid=(S // tm,),
        in_specs=[blk(2), blk(3), blk(4), pl.BlockSpec((tm, LANES), lambda i: (i, 0)), _full((D_HEADS, D_PAD)), _full((D_PAD, D_HEADS)),
                  _full((LANES, D_PAD)), _full((LANES, D_PAD)), _full((1, D_PAD)), _full((1, D_PAD)), _full((1, D_PAD)),
                  _full((D_PAD, 1))],
        out_specs=[out, out, out, pl.BlockSpec((None, D_PAD, tm), lambda i: (i, 0, 0))],
        out_shape=[pad, pad, pad, jax.ShapeDtypeStruct((S // tm, D_PAD, tm), BF16)],
        compiler_params=_params(("parallel",)),
    )(z, z, z, c, k["place"], k["place_t"], k["q_stat"], k["k_stat"], k["q_ones"], k["k_ones"], k["v_ones"], k["v_ones"].T)


def _attn_fwd(qa, ka, vat, place_t, tq, shards):
    S = qa.shape[0]
    n = S // tq
    ns = len(shards)
    hand_on_at = (2 * n) // 3

    pairs = [(q, k) for q in range(n) for k in range(q + 1)]
    q_of = jnp.asarray([q for q, _ in pairs], jnp.int32)
    k_of = jnp.asarray([k for _, k in pairs], jnp.int32)

    def body(q_of_ref, k_of_ref, q_ref, k_ref, vt_ref, pt_ref, *rest):
        o_ref, lse_ref = rest[ns:ns + 2]
        m_s, acc_s, ot_s = rest[2 * ns + 2:2 * ns + 5]
        s_s = rest[2 * ns + 5:2 * ns + 7]
        start, hand_on, finish = _gather_ops(rest[:ns], rest[ns + 2:2 * ns + 2], *rest[2 * ns + 7:])
        qi, ki = q_of_ref[pl.program_id(0)], k_of_ref[pl.program_id(0)]

        @pl.when((qi == 0) & (ki == 0))
        def _():
            start()

        @pl.when((qi == hand_on_at) & (ki == 0))
        def _():
            hand_on()

        @pl.when(ki == 0)
        def _():
            m_s[...] = jnp.full_like(m_s, NEG)
            acc_s[...] = jnp.zeros_like(acc_s)

        def step(diagonal):
            kc = FWD_KEY_CHUNK
            chunks = [slice(c * kc, (c + 1) * kc) for c in range(tq // kc)]

            def scores(h, rows, slot):
                sl = slice(h * HEAD_PAD, (h + 1) * HEAD_PAD)
                st = lax.dot_general(k_ref[rows, sl], q_ref[:, sl], NT, preferred_element_type=F32)
                if diagonal:
                    key = rows.start + lax.broadcasted_iota(jnp.int32, (kc, tq), 0)
                    query = lax.broadcasted_iota(jnp.int32, (kc, tq), 1)
                    st = jnp.where(query >= key, st, NEG)
                s_s[slot][rows, :] = st
                return jnp.max(st, axis=0, keepdims=True)

            m_cur = functools.reduce(jnp.maximum, [scores(0, rows, 0) for rows in chunks])
            for h in range(N_HEADS):
                sl = slice(h * HEAD_PAD, (h + 1) * HEAD_PAD)
                slot = h % 2
                m_prev = m_s[h][0:1, :]
                m_new = jnp.maximum(m_prev, m_cur)
                acc = jnp.exp2(m_prev - m_new) * acc_s[h]
                m_next = []
                for rows in chunks:
                    if h + 1 < N_HEADS:
                        m_next.append(scores(h + 1, rows, 1 - slot))
                    pt = jnp.exp2(s_s[slot][rows, :] - m_new).astype(BF16)
                    acc = acc + jnp.dot(vt_ref[sl, rows], pt, preferred_element_type=F32)
                acc_s[h] = acc
                m_s[h] = jnp.broadcast_to(m_new, (SUBLANES, tq))
                if m_next:
                    m_cur = functools.reduce(jnp.maximum, m_next)

        @pl.when(ki < qi)
        def _():
            step(False)

        @pl.when(ki == qi)
        def _():
            step(True)
            lse_ref[...] = jnp.zeros_like(lse_ref)
            for h in range(N_HEADS):
                acc = acc_s[h]
                denom = acc[Q_STAT:Q_STAT + 1, :]
                ot_s[h * HEAD_PAD:(h + 1) * HEAD_PAD, :] = (acc / denom).astype(BF16)
                lse_ref[h:h + 1, :] = m_s[h][0:1, :] + jnp.log(denom) * LOG2E
            o_ref[...] = lax.dot_general(ot_s[...], pt_ref[...], TN, preferred_element_type=F32).astype(BF16)

        @pl.when((qi == n - 1) & (ki == n - 1))
        def _():
            finish()

    out = pl.pallas_call(
        body, name="attn_fwd",
        grid_spec=pltpu.PrefetchScalarGridSpec(
            num_scalar_prefetch=2, grid=(len(pairs),),
            in_specs=[pl.BlockSpec((tq, D_PAD), lambda i, qs, ks: (qs[i], 0)),
                      pl.BlockSpec((tq, D_PAD), lambda i, qs, ks: (ks[i], 0)),
                      pl.BlockSpec((None, D_PAD, tq), lambda i, qs, ks: (ks[i], 0, 0)),
                      pl.BlockSpec((D_PAD, D_HEADS), lambda i, qs, ks: (0, 0))]
            + [_ANY] * ns,
            out_specs=[pl.BlockSpec((tq, D_HEADS), lambda i, qs, ks: (qs[i], 0)),
                       pl.BlockSpec((STAT_ROWS, tq), lambda i, qs, ks: (0, qs[i]))] + [_ANY] * ns,
            scratch_shapes=[pltpu.VMEM((N_HEADS, SUBLANES, tq), F32), pltpu.VMEM((N_HEADS, HEAD_PAD, tq), F32),
                            pltpu.VMEM((D_PAD, tq), BF16), pltpu.VMEM((tq, tq), F32), pltpu.VMEM((tq, tq), F32)] + _gather_sems(ns)),
        out_shape=[jax.ShapeDtypeStruct((S, D_HEADS), BF16), jax.ShapeDtypeStruct((STAT_ROWS, S), F32)] + _gather_shapes(shards),
        compiler_params=_params(("arbitrary",)),
    )(q_of, k_of, qa, ka, vat, place_t, *shards)
    return out[0], out[1], out[2:]


def _layer_norm_heads(v, seg_avg):
    mu = _split_dot(v, seg_avg)
    d = v - mu
    var = jnp.dot((d * d).astype(BF16), seg_avg, preferred_element_type=F32)
    rstd = lax.rsqrt(var + EPS)
    return d * rstd, rstd


def _gate_mix(vn_blk, w_ref, bias):
    acc = bias
    for h in range(N_HEADS):
        vh = jnp.where(_head_mask(h, SG_BLOCK), vn_blk, 0.0).astype(BF16)
        acc = acc + jnp.dot(w_ref[h], vh, preferred_element_type=F32)
    return acc


def _gate_fwd(z, w_mask, ln_row, b_full, seg_avg, tm):
    S = z.shape[0]

    def body(zu_ref, zv_ref, w_ref, ln_ref, b_ref, avg_ref, o_ref):
        u = _gelu(zu_ref[...].astype(F32))
        v = _gelu(zv_ref[...].astype(F32))
        vhat, _ = _layer_norm_heads(v, avg_ref[...])
        vn = vhat * ln_ref[...]
        for b in range(tm // SG_BLOCK):
            rows = slice(b * SG_BLOCK, (b + 1) * SG_BLOCK)
            mixed = _gate_mix(vn[rows], w_ref, b_ref[...])
            o_ref[rows, :] = (u[rows] * mixed).astype(BF16)

    return pl.pallas_call(
        body, name="gate_fwd", grid=(S // tm,),
        in_specs=[pl.BlockSpec((tm, D_HEADS), lambda i: (i, 0)), pl.BlockSpec((tm, D_HEADS), lambda i: (i, 1)),
                  _full((N_HEADS, SG_BLOCK, SG_BLOCK)), _full((1, D_HEADS)), _full((SG_BLOCK, D_HEADS)),
                  _full((D_HEADS, D_HEADS))],
        out_specs=pl.BlockSpec((tm, D_HEADS), lambda i: (i, 0)),
        out_shape=jax.ShapeDtypeStruct((S, D_HEADS), BF16),
        compiler_params=_params(("parallel",)),
    )(z, z, w_mask, ln_row, b_full, seg_avg)


def _mix_out(x, out_a, out_b, w_out, g2, tm):
    S = x.shape[0]

    def body(x_ref, a_ref, b_ref, w_ref, g_ref, x1_ref, h_ref):
        y = jnp.dot(a_ref[...], w_ref[:D_HEADS, :], preferred_element_type=F32)
        y = y + jnp.dot(b_ref[...], w_ref[D_HEADS:, :], preferred_element_type=F32)
        x1 = x_ref[...] + y
        x1_ref[...] = x1
        r = lax.rsqrt(jnp.mean(x1 * x1, axis=-1, keepdims=True) + EPS)
        h_ref[...] = (x1 * r * g_ref[...]).astype(BF16)

    row = lambda w: pl.BlockSpec((tm, w), lambda i: (i, 0))
    return pl.pallas_call(
        body, name="mix_out", grid=(S // tm,),
        in_specs=[row(D_MODEL), row(D_HEADS), row(D_HEADS), _full((D_MODEL, D_MODEL)), _full((1, D_MODEL))],
        out_specs=[row(D_MODEL), row(D_MODEL)],
        out_shape=[jax.ShapeDtypeStruct((S, D_MODEL), F32), jax.ShapeDtypeStruct((S, D_MODEL), BF16)],
        compiler_params=_params(("parallel",)),
    )(x, out_a, out_b, w_out, g2)


def _up_proj(h2, w_up_q, tm):
    S = h2.shape[0]
    nq, _, wq = w_up_q.shape

    def body(h_ref, w_ref, a_ref):
        a_ref[...] = jnp.dot(h_ref[...], w_ref[...], preferred_element_type=F32).astype(BF16)

    return pl.pallas_call(
        body, name="up_proj", grid=(nq, S // tm),
        in_specs=[pl.BlockSpec((tm, D_MODEL), lambda j, i: (i, 0)), pl.BlockSpec((None, D_MODEL, wq), lambda j, i: (j, 0, 0))],
        out_specs=pl.BlockSpec((tm, wq), lambda j, i: (i, j)),
        out_shape=jax.ShapeDtypeStruct((S, nq * wq), BF16),
        compiler_params=_params(("parallel", "parallel")),
    )(h2, w_up_q)


def _shift_down(a, halo, k):
    tm = a.shape[0]
    ra = pltpu.roll(a, k, 0)
    rh = pltpu.roll(halo, k, 0)
    row = lax.broadcasted_iota(jnp.int32, halo.shape, 0)
    top = jnp.where(row < k, rh, ra[0:SUBLANES])
    return jnp.concatenate([top, ra[SUBLANES:tm]], axis=0)


def _shift_up(a, halo, k):
    tm = a.shape[0]
    ra = pltpu.roll(a, tm - k, 0)
    rh = pltpu.roll(halo, SUBLANES - k, 0)
    row = lax.broadcasted_iota(jnp.int32, halo.shape, 0)
    bottom = jnp.where(row >= SUBLANES - k, rh, ra[tm - SUBLANES:tm])
    return jnp.concatenate([ra[0:tm - SUBLANES], bottom], axis=0)


def _shift_matrices(tm):
    row = lax.broadcasted_iota(jnp.int32, (tm, tm), 0)
    col = lax.broadcasted_iota(jnp.int32, (tm, tm), 1)
    return [(row == col + k).astype(BF16) for k in (1, 2)]


def _conv_taps(a, halo, first, shifts):
    tm = a.shape[0]
    halo = halo.astype(F32) * jnp.where(first, 0.0, 1.0)
    if shifts is None:
        a = a.astype(F32)
        return a, _shift_down(a, halo, 1), _shift_down(a, halo, 2)
    row8 = lax.broadcasted_iota(jnp.int32, halo.shape, 0)
    taps = [a.astype(F32)]
    for k, shift in zip((1, 2), shifts):
        down = jnp.dot(shift, a, preferred_element_type=F32)
        top = down[0:SUBLANES] + jnp.where(row8 < k, pltpu.roll(halo, k, 0), 0.0)
        taps.append(jnp.concatenate([top, down[SUBLANES:tm]], axis=0))
    return taps


def _conv_gate_val(refs, shifts, cols, first):
    ag_ref, av_ref, hg_ref, hv_ref, wg_ref, wv_ref, bg_ref, bv_ref = refs
    g0, g1, g2 = _conv_taps(ag_ref[:, cols], hg_ref[:, cols], first, shifts)
    gate = wg_ref[2:3, cols] * g0 + wg_ref[1:2, cols] * g1 + wg_ref[0:1, cols] * g2 + bg_ref[:, cols]
    v0, v1, v2 = _conv_taps(av_ref[:, cols], hv_ref[:, cols], first, shifts)
    val = wv_ref[2:3, cols] * v0 + wv_ref[1:2, cols] * v1 + wv_ref[0:1, cols] * v2 + bv_ref[:, cols]
    return gate, val, (g2, g1, g0), (v2, v1, v0)


_FF_CHUNKS = [slice(j * FF_CHUNK, (j + 1) * FF_CHUNK) for j in range(D_FF // FF_CHUNK)]


def _conv_specs(tm):
    step = tm // SUBLANES
    prev = lambda i: jnp.maximum(i * step - 1, 0)
    return [pl.BlockSpec((tm, D_FF), lambda i: (i, 0)), pl.BlockSpec((tm, D_FF), lambda i: (i, 1)),
            pl.BlockSpec((SUBLANES, D_FF), lambda i: (prev(i), 0)), pl.BlockSpec((SUBLANES, D_FF), lambda i: (prev(i), 1))]


def _ffn_fwd_loss(a, w_conv, b_conv, w_down, x1, g3, target, tm):
    S = x1.shape[0]

    def body(ag_ref, av_ref, hg_ref, hv_ref, wg_ref, wv_ref, bg_ref, bv_ref, wd_ref, x1_ref, g_ref, t_ref,
             dx2_ref, loss_ref, dg_ref):
        i = pl.program_id(0)

        @pl.when(i == 0)
        def _():
            loss_ref[...] = jnp.zeros_like(loss_ref)
            dg_ref[...] = jnp.zeros_like(dg_ref)

        x2 = x1_ref[...]
        for cols in _FF_CHUNKS:
            gate, val, _, _ = _conv_gate_val((ag_ref, av_ref, hg_ref, hv_ref, wg_ref, wv_ref, bg_ref, bv_ref), None, cols, i == 0)
            half = 0.5 * gate
            y = ((half + half * jnp.tanh(half)) * val).astype(BF16)
            x2 = x2 + jnp.dot(y, wd_ref[cols, :], preferred_element_type=F32)
        r = lax.rsqrt(jnp.mean(x2 * x2, axis=-1, keepdims=True) + EPS)
        xhat = x2 * r
        gg = g_ref[...]
        err = xhat * gg - t_ref[...]
        loss_ref[...] += jnp.sum(err * err, axis=0, keepdims=True)
        dy = err * (1.0 / D_MODEL)
        dg_ref[...] += jnp.sum(dy * xhat, axis=0, keepdims=True)
        dxhat = dy * gg
        dx2_ref[...] = r * (dxhat - xhat * jnp.mean(dxhat * xhat, axis=-1, keepdims=True))

    row = lambda w: pl.BlockSpec((tm, w), lambda i: (i, 0))
    half = lambda r: [pl.BlockSpec((r, D_FF), lambda i: (0, 0)), pl.BlockSpec((r, D_FF), lambda i: (0, 1))]
    return pl.pallas_call(
        body, name="ffn_fwd_loss", grid=(S // tm,),
        in_specs=_conv_specs(tm) + half(3) + half(1) + [_full((D_FF, D_MODEL)), row(D_MODEL), _full((1, D_MODEL)), row(D_MODEL)],
        out_specs=[row(D_MODEL), _full((1, D_MODEL)), _full((1, D_MODEL))],
        out_shape=[jax.ShapeDtypeStruct((S, D_MODEL), F32), jax.ShapeDtypeStruct((1, D_MODEL), F32),
                   jax.ShapeDtypeStruct((1, D_MODEL), F32)],
        compiler_params=_params(("arbitrary",)),
    )(a, a, a, a, w_conv, w_conv, b_conv, b_conv, w_down, x1, g3, target)


def _ffn_bwd_gate(dx2, a, w_conv, b_conv, w_down, tm):
    S = dx2.shape[0]

    def body(dx_ref, ag_ref, av_ref, hg_ref, hv_ref, wg_ref, wv_ref, bg_ref, bv_ref, wd_ref,
             dc_ref, y_ref, dw_ref, db_ref):
        i = pl.program_id(0)

        @pl.when(i == 0)
        def _():
            dw_ref[...] = jnp.zeros_like(dw_ref)
            db_ref[...] = jnp.zeros_like(db_ref)

        dx = dx_ref[...].astype(BF16)
        shifts = _shift_matrices(tm)
        for cols in _FF_CHUNKS:
            gate, val, gtaps, vtaps = _conv_gate_val((ag_ref, av_ref, hg_ref, hv_ref, wg_ref, wv_ref, bg_ref, bv_ref), shifts, cols, i == 0)
            sg = _sigmoid(gate)
            act = gate * sg
            y_ref[:, cols] = (act * val).astype(BF16)
            dy = lax.dot_general(dx, wd_ref[cols, :], NT, preferred_element_type=F32)
            dgate = dy * val * (sg + act - act * sg)
            dval = dy * act
            for d, taps, out in ((dgate, gtaps, cols), (dval, vtaps, slice(D_FF + cols.start, D_FF + cols.stop))):
                dc_ref[:, out] = d.astype(BF16)
                db_ref[0:1, out] += jnp.sum(d, axis=0, keepdims=True)
                for j in range(3):
                    dw_ref[j:j + 1, out] += jnp.sum(d * taps[j], axis=0, keepdims=True)

    row = lambda w: pl.BlockSpec((tm, w), lambda i: (i, 0))
    half = lambda r: [pl.BlockSpec((r, D_FF), lambda i: (0, 0)), pl.BlockSpec((r, D_FF), lambda i: (0, 1))]
    return pl.pallas_call(
        body, name="ffn_bwd_gate", grid=(S // tm,),
        in_specs=[row(D_MODEL)] + _conv_specs(tm) + half(3) + half(1) + [_full((D_FF, D_MODEL))],
        out_specs=[row(2 * D_FF), row(D_FF), _full((SUBLANES, 2 * D_FF)), _full((1, 2 * D_FF))],
        out_shape=[jax.ShapeDtypeStruct((S, 2 * D_FF), BF16), jax.ShapeDtypeStruct((S, D_FF), BF16),
                   jax.ShapeDtypeStruct((SUBLANES, 2 * D_FF), F32), jax.ShapeDtypeStruct((1, 2 * D_FF), F32)],
        compiler_params=_params(("arbitrary",)),
    )(dx2, a, a, a, a, w_conv, w_conv, b_conv, b_conv, w_down)


def _conv_bwd(dc, w_conv, tm, tn):
    S, C = dc.shape
    step = tm // SUBLANES
    last_blk = S // SUBLANES - 1

    def body(d_ref, nx_ref, w_ref, o_ref):
        last = pl.program_id(0) == pl.num_programs(0) - 1
        row = lax.broadcasted_iota(jnp.int32, (tm, tm), 0)
        col = lax.broadcasted_iota(jnp.int32, (tm, tm), 1)
        row8 = lax.broadcasted_iota(jnp.int32, (SUBLANES, FF_CHUNK), 0)
        ups = [(row + k == col).astype(BF16) for k in (1, 2)]
        for c0 in range(0, tn, FF_CHUNK):
            cols = slice(c0, c0 + FF_CHUNK)
            d = d_ref[:, cols]
            nx = nx_ref[:, cols].astype(F32) * jnp.where(last, 0.0, 1.0)
            out = w_ref[2:3, cols] * d.astype(F32)
            for k, up in zip((1, 2), ups):
                moved = jnp.dot(up, d, preferred_element_type=F32)
                bottom = moved[tm - SUBLANES:tm] + jnp.where(row8 >= SUBLANES - k, pltpu.roll(nx, SUBLANES - k, 0), 0.0)
                out = out + w_ref[2 - k:3 - k, cols] * jnp.concatenate([moved[0:tm - SUBLANES], bottom], axis=0)
            o_ref[:, cols] = out.astype(BF16)

    return pl.pallas_call(
        body, name="conv_bwd", grid=(S // tm, C // tn),
        in_specs=[pl.BlockSpec((tm, tn), lambda i, j: (i, j)),
                  pl.BlockSpec((SUBLANES, tn), lambda i, j: (jnp.minimum((i + 1) * step, last_blk), j)),
                  pl.BlockSpec((3, tn), lambda i, j: (0, j))],
        out_specs=pl.BlockSpec((tm, tn), lambda i, j: (i, j)),
        out_shape=jax.ShapeDtypeStruct((S, C), BF16),
        compiler_params=_params(("parallel", "parallel")),
    )(dc, dc, w_conv)


def _matmul_tn(a, b, name, bm, bn, tk, col_a=0, col_b=0, quarters=None):
    S = a.shape[0]
    gm, gn = quarters if quarters else (1, 1)
    nk = S // tk

    def body(a_ref, b_ref, o_ref):
        @pl.when(pl.program_id(2) == 0)
        def _():
            o_ref[...] = jnp.zeros_like(o_ref)

        o_ref[...] += lax.dot_general(a_ref[...].astype(BF16), b_ref[...].astype(BF16), TN, preferred_element_type=F32)

    if quarters and gn > 1:
        out_spec = pl.BlockSpec((None, bm, bn), lambda i, j, k: (j, i, 0))
        out_shape = jax.ShapeDtypeStruct((gn, gm * bm, bn), F32)
    else:
        out_spec = pl.BlockSpec((bm, bn), lambda i, j, k: (i, j))
        out_shape = jax.ShapeDtypeStruct((gm * bm, gn * bn), F32)
    return pl.pallas_call(
        body, name=name, grid=(gm, gn, nk),
        in_specs=[pl.BlockSpec((tk, bm), lambda i, j, k: (k, col_a * gm + i)),
                  pl.BlockSpec((tk, bn), lambda i, j, k: (k, col_b * gn + j))],
        out_specs=out_spec, out_shape=out_shape,
        compiler_params=_params(("parallel", "parallel", "arbitrary")),
    )(a, b)


def _dw_out(out_a, out_b, dx1, tk):
    S = dx1.shape[0]

    def body(a_ref, b_ref, d_ref, o_ref):
        @pl.when(pl.program_id(0) == 0)
        def _():
            o_ref[...] = jnp.zeros_like(o_ref)

        d = d_ref[...].astype(BF16)
        o_ref[:D_HEADS, :] += lax.dot_general(a_ref[...], d, TN, preferred_element_type=F32)
        o_ref[D_HEADS:, :] += lax.dot_general(b_ref[...], d, TN, preferred_element_type=F32)

    row = lambda w: pl.BlockSpec((tk, w), lambda k: (k, 0))
    return pl.pallas_call(
        body, name="dw_out", grid=(S // tk,),
        in_specs=[row(D_HEADS), row(D_HEADS), row(D_MODEL)], out_specs=_full((D_MODEL, D_MODEL)),
        out_shape=jax.ShapeDtypeStruct((D_MODEL, D_MODEL), F32),
        compiler_params=_params(("arbitrary",)),
    )(out_a, out_b, dx1)


def _up_bwd(dact, w_up_q, x1, g2, dx2, tm):
    S = x1.shape[0]
    nq, _, wq = w_up_q.shape

    def body(d_ref, w_ref, x_ref, g_ref, dx2_ref, dx1_ref, dg_ref):
        @pl.when(pl.program_id(0) == 0)
        def _():
            dg_ref[...] = jnp.zeros_like(dg_ref)

        dh = jnp.zeros((tm, D_MODEL), F32)
        for j in range(nq):
            dh = dh + lax.dot_general(d_ref[:, j * wq:(j + 1) * wq], w_ref[j], NT, preferred_element_type=F32)
        dx, dg = _rms_bwd(dh, x_ref[...], g_ref[...])
        dg_ref[...] += dg
        dx1_ref[...] = dx2_ref[...] + dx

    row = lambda w: pl.BlockSpec((tm, w), lambda i: (i, 0))
    return pl.pallas_call(
        body, name="up_bwd", grid=(S // tm,),
        in_specs=[row(nq * wq), pl.BlockSpec((nq, D_MODEL, wq), lambda i: (0, 0, 0), pipeline_mode=pl.Buffered(1)),
                  row(D_MODEL), _full((1, D_MODEL)), row(D_MODEL)],
        out_specs=[row(D_MODEL), _full((1, D_MODEL))],
        out_shape=[jax.ShapeDtypeStruct((S, D_MODEL), F32), jax.ShapeDtypeStruct((1, D_MODEL), F32)],
        compiler_params=_params(("arbitrary",)),
    )(dact, w_up_q, x1, g2, dx2)


def _out_bwd(dx1, w_out, tm):
    S = dx1.shape[0]

    def body(d_ref, w_ref, o_ref):
        o_ref[...] = lax.dot_general(d_ref[...].astype(BF16), w_ref[...], NT, preferred_element_type=F32).astype(BF16)

    return pl.pallas_call(
        body, name="out_bwd", grid=(S // tm,),
        in_specs=[pl.BlockSpec((tm, D_MODEL), lambda i: (i, 0)), _full((D_MODEL, D_MODEL))],
        out_specs=pl.BlockSpec((tm, D_MODEL), lambda i: (i, 0)),
        out_shape=jax.ShapeDtypeStruct((S, D_MODEL), BF16),
        compiler_params=_params(("parallel",)),
    )(dx1, w_out)


def _gate_bwd(z, dcat, w_mask, w_mask_t, ln_row, b_full, seg_avg, head_ind, tm, swap):
    S = z.shape[0]
    nb = tm // SG_BLOCK
    ns = len(swap)

    def body(zu_ref, zv_ref, do_ref, w_ref, wt_ref, ln_ref, b_ref, avg_ref, ind_ref, *rest):
        dzu_ref, dzv_ref, dw_ref, db_ref, dln_ref = rest[ns:ns + 5]
        dvn_s, dbf_s = rest[2 * ns + 5:2 * ns + 7]
        swap_start, swap_finish = _swap_ops(rest[:ns], rest[ns + 5:2 * ns + 5], *rest[2 * ns + 7:])
        i = pl.program_id(0)

        @pl.when(i == 0)
        def _():
            swap_start()
            dw_ref[...] = jnp.zeros_like(dw_ref)
            dln_ref[...] = jnp.zeros_like(dln_ref)
            dbf_s[...] = jnp.zeros_like(dbf_s)

        zu = zu_ref[...].astype(F32)
        zv = zv_ref[...].astype(F32)
        u = _gelu(zu)
        v = _gelu(zv)
        avg = avg_ref[...]
        vhat, rstd = _layer_norm_heads(v, avg)
        ln = ln_ref[...]
        vn = vhat * ln
        for b in range(nb):
            rows = slice(b * SG_BLOCK, (b + 1) * SG_BLOCK)
            vn_b = vn[rows]
            mixed = _gate_mix(vn_b, w_ref, b_ref[...])
            do = do_ref[rows, :].astype(F32)
            dzu_ref[rows, :] = (do * mixed * _gelu_grad(zu[rows])).astype(BF16)
            dmix = do * u[rows]
            dbf_s[...] += dmix
            vn_bf = vn_b.astype(BF16)
            dvn = jnp.zeros((SG_BLOCK, D_HEADS), F32)
            for h in range(N_HEADS):
                dmh = jnp.where(_head_mask(h, SG_BLOCK), dmix, 0.0).astype(BF16)
                dw_ref[h] += lax.dot_general(dmh, vn_bf, NT, preferred_element_type=F32)
                dvn = dvn + jnp.dot(wt_ref[h], dmh, preferred_element_type=F32)
            dvn_s[rows, :] = dvn
        dvn = dvn_s[...]
        dln_ref[...] += jnp.sum(dvn * vhat, axis=0, keepdims=True)
        dvhat = dvn * ln
        along = jnp.dot((dvhat * vhat).astype(BF16), avg, preferred_element_type=F32)
        dv = rstd * (dvhat - _split_dot(dvhat, avg) - vhat * along)
        dzv_ref[...] = (dv * _gelu_grad(zv)).astype(BF16)

        @pl.when(i == pl.num_programs(0) - 1)
        def _():
            r = lax.broadcasted_iota(jnp.int32, (SG_BLOCK, SG_BLOCK), 0) // CHUNK
            s = lax.broadcasted_iota(jnp.int32, (SG_BLOCK, SG_BLOCK), 1) // CHUNK
            for h in range(N_HEADS):
                dw_ref[h] = jnp.where(r >= s, dw_ref[h], 0.0)
            db_ref[...] = _split_dot(dbf_s[...], ind_ref[...])
            swap_finish()

    row = lambda col: pl.BlockSpec((tm, D_HEADS), lambda i: (i, col))
    wspec = _full((N_HEADS, SG_BLOCK, SG_BLOCK))
    out = pl.pallas_call(
        body, name="gate_bwd", grid=(S // tm,),
        in_specs=[row(0), row(1), row(0), wspec, wspec, _full((1, D_HEADS)), _full((SG_BLOCK, D_HEADS)),
                  _full((D_HEADS, D_HEADS)), _full((D_HEADS, LANES))] + [_ANY] * ns,
        out_specs=[row(0), row(0), wspec, _full((SG_BLOCK, LANES)), _full((1, D_HEADS))] + [_ANY] * ns,
        out_shape=[jax.ShapeDtypeStruct((S, D_HEADS), BF16), jax.ShapeDtypeStruct((S, D_HEADS), BF16),
                   jax.ShapeDtypeStruct((N_HEADS, SG_BLOCK, SG_BLOCK), F32), jax.ShapeDtypeStruct((SG_BLOCK, LANES), F32),
                   jax.ShapeDtypeStruct((1, D_HEADS), F32)] + _swap_shapes(swap),
        scratch_shapes=[pltpu.VMEM((tm, D_HEADS), F32), pltpu.VMEM((SG_BLOCK, D_HEADS), F32)] + _swap_sems(ns),
        compiler_params=_params(("arbitrary",)),
    )(z, z, dcat, w_mask, w_mask_t, ln_row, b_full, seg_avg, head_ind, *swap)
    return out[:5], out[5:]


def _attn_pack_grad(o, dcat, qa, lse, head_ind, k, tm):
    S = o.shape[0]

    def body(o_ref, do_ref, qa_ref, lse_ref, ind_ref, pl_ref, pt_ref, eye_ref, ds_ref, dst_ref, ls_ref, lst_ref,
             dop_ref, qb_ref, dot_ref, qbt_ref):
        do = do_ref[...]
        delta = _split_dot(o_ref[...].astype(F32) * do.astype(F32), ind_ref[...])
        hi = delta.astype(BF16).astype(F32)
        parts = (hi + pltpu.roll((delta - hi).astype(BF16).astype(F32), N_HEADS, 1)).astype(BF16)
        dop = jnp.dot(do, pl_ref[...], preferred_element_type=F32) - jnp.dot(parts, ds_ref[...], preferred_element_type=F32)
        for g in range(GROUPS):
            dop_ref[g] = dop[:, g * GROUP_PAD:(g + 1) * GROUP_PAD].astype(BF16)
        dot = lax.dot_general(pt_ref[...], do, NT, preferred_element_type=F32)
        dot_ref[...] = (dot - lax.dot_general(dst_ref[...], parts, NT, preferred_element_type=F32)).astype(BF16)
        qa = qa_ref[...]
        stack = jnp.concatenate(_split3(lse_ref[...]), axis=0)
        qb = qa.astype(F32) - lax.dot_general(stack, ls_ref[...], TN, preferred_element_type=F32)
        qbt = lax.dot_general(eye_ref[...], qa, NT, preferred_element_type=F32)
        qbt = qbt - jnp.dot(lst_ref[...], stack, preferred_element_type=F32)
        for g in range(GROUPS):
            qb_ref[g] = qb[:, g * GROUP_PAD:(g + 1) * GROUP_PAD].astype(BF16)
        qbt_ref[...] = qbt.astype(BF16)

    pad = pl.BlockSpec((tm, D_PAD), lambda i: (i, 0))
    padt = pl.BlockSpec((None, D_PAD, tm), lambda i: (i, 0, 0))
    return pl.pallas_call(
        body, name="attn_pack_grad", grid=(S // tm,),
        in_specs=[pl.BlockSpec((tm, D_HEADS), lambda i: (i, 0)), pl.BlockSpec((tm, D_HEADS), lambda i: (i, 1)), pad,
                  pl.BlockSpec((STAT_ROWS, tm), lambda i: (0, i)), _full((D_HEADS, LANES)), _full((D_HEADS, D_PAD)),
                  _full((D_PAD, D_HEADS)), _full((D_PAD, D_PAD)), _full((LANES, D_PAD)), _full((D_PAD, LANES)),
                  _full((3 * STAT_ROWS, D_PAD)), _full((D_PAD, 3 * STAT_ROWS))],
        out_specs=[pl.BlockSpec((GROUPS, tm, GROUP_PAD), lambda i: (0, i, 0))] * 2 + [padt, padt],
        out_shape=[jax.ShapeDtypeStruct((GROUPS, S, GROUP_PAD), BF16)] * 2 + [jax.ShapeDtypeStruct((S // tm, D_PAD, tm), BF16)] * 2,
        compiler_params=_params(("parallel",)),
    )(o, dcat, qa, lse, head_ind, k["place"], k["place_t"], jnp.eye(D_PAD, dtype=BF16), k["d_stat"], k["d_stat"].T,
      k["l_stat"], k["l_stat"].T)


def _attn_bwd(qb, qbt, ka, va, dop, dopt, k, tq, sums16):
    S = ka.shape[0]
    n = S // tq
    ns = len(sums16)

    pairs = [(kb, q) for kb in range(n) for q in range(kb, n)]
    k_of = jnp.asarray([kb for kb, _ in pairs], jnp.int32)
    q_of = jnp.asarray([q for _, q in pairs], jnp.int32)

    def body(k_of_ref, q_of_ref, q_ref, qt_ref, k_ref, v_ref, do_ref, dot_ref, pt_ref, *rest):
        dq_hbm, dcr_hbm, dk_ref, dv_ref, dcc_ref = rest[ns:ns + 5]
        dq_s, dcr_s, dk_s, dv_s, dcc_s = rest[2 * ns + 5:2 * ns + 10]
        s_s, d_s = rest[2 * ns + 10:2 * ns + 12], rest[2 * ns + 12:2 * ns + 14]
        sems = rest[2 * ns + 14]
        scatter_start, scatter_finish = _scatter_ops(rest[:ns], rest[ns + 5:2 * ns + 5], *rest[2 * ns + 15:])
        g = pl.program_id(0)
        ki, qi = k_of_ref[pl.program_id(1)], q_of_ref[pl.program_id(1)]

        @pl.when((g == 0) & (ki == 0) & (qi == 0))
        def _():
            scatter_start()

        @pl.when((ki == 0) & (qi == 0))
        def _():
            dq_s[...] = jnp.zeros_like(dq_s)
            dcr_s[...] = jnp.zeros_like(dcr_s)

        @pl.when(qi == ki)
        def _():
            dk_s[...] = jnp.zeros_like(dk_s)
            dv_s[...] = jnp.zeros_like(dv_s)
            dcc_s[...] = jnp.zeros_like(dcc_s)

        def step(diagonal):
            chunks = [slice(c * KEY_CHUNK, (c + 1) * KEY_CHUNK) for c in range(tq // KEY_CHUNK)]

            def keys_of(rows):
                return slice(0, rows.stop) if diagonal else slice(0, tq)

            def scores(hh, rows, slot):
                sl, keys = slice(hh * HEAD_PAD, (hh + 1) * HEAD_PAD), keys_of(rows)
                s_s[slot][rows, keys] = lax.dot_general(q_ref[rows, sl], k_ref[keys, sl], NT, preferred_element_type=F32)
                d_s[slot][rows, keys] = lax.dot_general(do_ref[rows, sl], v_ref[keys, sl], NT, preferred_element_type=F32)

            for rows in chunks:
                scores(0, rows, 0)
            for hh in range(GROUP_HEADS):
                sl = slice(hh * HEAD_PAD, (hh + 1) * HEAD_PAD)
                slot = hh % 2
                for rows in chunks:
                    if hh + 1 < GROUP_HEADS:
                        scores(hh + 1, rows, 1 - slot)
                    keys = keys_of(rows)
                    p = jnp.exp2(s_s[slot][rows, keys])
                    if diagonal:
                        row = rows.start + lax.broadcasted_iota(jnp.int32, (KEY_CHUNK, keys.stop), 0)
                        col = lax.broadcasted_iota(jnp.int32, (KEY_CHUNK, keys.stop), 1)
                        p = jnp.where(row >= col, p, 0.0)
                    ds = p * d_s[slot][rows, keys]
                    qrows = pl.ds(pl.multiple_of(qi * tq + rows.start, KEY_CHUNK), KEY_CHUNK)
                    dcc_s[hh:hh + 1, keys] += jnp.sum(ds, axis=0, keepdims=True)
                    dcr_s[qrows, hh:hh + 1] += jnp.sum(ds, axis=1, keepdims=True)
                    ds = ds.astype(BF16)
                    dv_s[sl, keys] += jnp.dot(dot_ref[sl, rows], p.astype(BF16), preferred_element_type=F32)
                    dk_s[sl, keys] += jnp.dot(qt_ref[sl, rows], ds, preferred_element_type=F32)
                    dq_s[qrows, sl] += jnp.dot(ds, k_ref[keys, sl], preferred_element_type=F32)

        @pl.when(qi > ki)
        def _():
            step(False)

        @pl.when(qi == ki)
        def _():
            step(True)

        @pl.when(qi == n - 1)
        def _():
            dk = dk_s[...]
            pt = pt_ref[...]
            dk_ref[...] = lax.dot_general((dk * (1.0 / LOG2E)).astype(BF16), pt, TN, preferred_element_type=F32).astype(BF16)
            dv_ref[...] = lax.dot_general(dv_s[...].astype(BF16), pt, TN, preferred_element_type=F32).astype(BF16)
            dcc_ref[...] = dcc_s[...]

        @pl.when((ki == n - 1) & (qi == n - 1))
        def _():
            copies = [pltpu.make_async_copy(dq_s, dq_hbm.at[g], sems.at[0]), pltpu.make_async_copy(dcr_s, dcr_hbm.at[g], sems.at[1])]
            for cp in copies:
                cp.start()
            for cp in copies:
                cp.wait()

        @pl.when((g == GROUPS - 1) & (ki == n - 1) & (qi == n - 1))
        def _():
            scatter_finish()

    gw = GROUP_HEADS * HEAD_DIM
    qspec = pl.BlockSpec((None, tq, GROUP_PAD), lambda g, i, ks, qs: (g, qs[i], 0))
    qtspec = pl.BlockSpec((None, GROUP_PAD, tq), lambda g, i, ks, qs: (qs[i], g, 0))
    kspec = pl.BlockSpec((tq, GROUP_PAD), lambda g, i, ks, qs: (ks[i], g))
    kout = pl.BlockSpec((tq, gw), lambda g, i, ks, qs: (ks[i], g))
    out = pl.pallas_call(
        body, name="attn_bwd",
        grid_spec=pltpu.PrefetchScalarGridSpec(
            num_scalar_prefetch=2, grid=(GROUPS, len(pairs)),
            in_specs=[qspec, qtspec, kspec, kspec, qspec, qtspec, pl.BlockSpec((GROUP_PAD, gw), lambda g, i, ks, qs: (0, 0))]
            + [_ANY] * ns,
            out_specs=[_ANY, _ANY, kout, kout, pl.BlockSpec((None, SUBLANES, tq), lambda g, i, ks, qs: (g, 0, ks[i]))] + [_ANY] * ns,
            scratch_shapes=[pltpu.VMEM((S, GROUP_PAD), F32), pltpu.VMEM((S, LANES), F32), pltpu.VMEM((GROUP_PAD, tq), F32),
                            pltpu.VMEM((GROUP_PAD, tq), F32), pltpu.VMEM((SUBLANES, tq), F32),
                            pltpu.VMEM((tq, tq), F32), pltpu.VMEM((tq, tq), F32), pltpu.VMEM((tq, tq), F32),
                            pltpu.VMEM((tq, tq), F32), pltpu.SemaphoreType.DMA((2,))]
            + _scatter_sems(ns)),
        out_shape=[jax.ShapeDtypeStruct((GROUPS, S, GROUP_PAD), F32), jax.ShapeDtypeStruct((GROUPS, S, LANES), F32),
                   jax.ShapeDtypeStruct((S, D_HEADS), BF16), jax.ShapeDtypeStruct((S, D_HEADS), BF16),
                   jax.ShapeDtypeStruct((GROUPS, SUBLANES, S), F32)]
        + _scatter_shapes(sums16),
        compiler_params=_params(("arbitrary", "arbitrary")),
    )(k_of, q_of, qb, qbt, ka, va, dop, dopt, k["place_t_group"], *sums16)
    return out[0], out[1], out[2], out[3], out[4], out[5:]


def _attn_unpack(dqp, k, tm):
    S = dqp.shape[1]
    gw = GROUP_HEADS * HEAD_DIM

    def body(dqp_ref, pt_ref, dq_ref):
        for g in range(GROUPS):
            dq_ref[:, g * gw:(g + 1) * gw] = jnp.dot((dqp_ref[g] * SCALE).astype(BF16), pt_ref[...],
                                                     preferred_element_type=F32).astype(BF16)

    return pl.pallas_call(
        body, name="attn_unpack", grid=(S // tm,),
        in_specs=[pl.BlockSpec((GROUPS, tm, GROUP_PAD), lambda i: (0, i, 0)), _full((GROUP_PAD, gw))],
        out_specs=pl.BlockSpec((tm, D_HEADS), lambda i: (i, 0)),
        out_shape=jax.ShapeDtypeStruct((S, D_HEADS), BF16),
        compiler_params=_params(("parallel",)),
    )(dqp, k["place_t_group"])


def _fox_bwd(dc, f, bias_row, tb):
    S = f.shape[0]
    nb = S // tb

    def body(dc_ref, f_ref, b_ref, df_ref, dbias_ref, carry):
        @pl.when(pl.program_id(0) == 0)
        def _():
            carry[...] = jnp.zeros_like(carry)
            dbias_ref[...] = jnp.zeros_like(dbias_ref)

        r = lax.broadcasted_iota(jnp.int32, (tb, tb), 0)
        s = lax.broadcasted_iota(jnp.int32, (tb, tb), 1)
        tri = (s >= r).astype(BF16)
        rc = sum(jnp.dot(tri, part, preferred_element_type=F32) for part in _split3(dc_ref[...])) + carry[0:1, :]
        carry[...] = jnp.broadcast_to(rc[0:1, :], carry.shape)
        lane = lax.broadcasted_iota(jnp.int32, (tb, LANES), 1)
        df = jnp.where(lane < N_HEADS, rc * jax.nn.sigmoid(-(f_ref[...] + b_ref[...])), 0.0)
        df_ref[...] = df.astype(BF16)
        dbias_ref[...] += jnp.sum(df, axis=0, keepdims=True)

    rev = pl.BlockSpec((tb, LANES), lambda i: (nb - 1 - i, 0))
    return pl.pallas_call(
        body, name="fox_bwd", grid=(nb,),
        in_specs=[rev, rev, _full((1, LANES))],
        out_specs=[rev, _full((1, LANES))],
        out_shape=[jax.ShapeDtypeStruct((S, LANES), BF16), jax.ShapeDtypeStruct((1, LANES), F32)],
        scratch_shapes=[pltpu.VMEM((SUBLANES, LANES), F32)],
        compiler_params=_params(("arbitrary",)),
    )(dc, f, bias_row)


_DZ_WIDTHS = (D_HEADS,) * 5 + (LANES,)


def _in_bwd(pieces, w_in, x, g1, dx1, tm, sums16):
    S = x.shape[0]
    ns = len(sums16)

    def body(*refs):
        p_refs, (w_ref, x_ref, g_ref, dx1_ref) = refs[:6], refs[6:10]
        dx_ref, dg_ref = refs[10 + ns:12 + ns]
        scatter_start, scatter_finish = _scatter_ops(refs[10:10 + ns], refs[12 + ns:12 + 2 * ns], *refs[12 + 2 * ns:])

        @pl.when(pl.program_id(0) == 0)
        def _():
            dg_ref[...] = jnp.zeros_like(dg_ref)
            scatter_start()

        dh = jnp.zeros((tm, D_MODEL), F32)
        off = 0
        for p_ref, w in zip(p_refs, _DZ_WIDTHS):
            dh = dh + lax.dot_general(p_ref[...].astype(BF16), w_ref[:, off:off + w], NT, preferred_element_type=F32)
            off += w
        dx, dg = _rms_bwd(dh, x_ref[...], g_ref[...])
        dg_ref[...] += dg
        dx_ref[...] = dx1_ref[...] + dx

        @pl.when(pl.program_id(0) == pl.num_programs(0) - 1)
        def _():
            scatter_finish()

    row = lambda w: pl.BlockSpec((tm, w), lambda i: (i, 0))
    out = pl.pallas_call(
        body, name="in_bwd", grid=(S // tm,),
        in_specs=[row(w) for w in _DZ_WIDTHS] + [_full((D_MODEL, D_IN_PAD)), row(D_MODEL), _full((1, D_MODEL)), row(D_MODEL)]
        + [_ANY] * ns,
        out_specs=[row(D_MODEL), _full((1, D_MODEL))] + [_ANY] * ns,
        out_shape=[jax.ShapeDtypeStruct((S, D_MODEL), F32), jax.ShapeDtypeStruct((1, D_MODEL), F32)] + _scatter_shapes(sums16),
        scratch_shapes=_scatter_sems(ns),
        compiler_params=_params(("arbitrary",)),
    )(*pieces, w_in, x, g1, dx1, *sums16)
    return out[0], out[1], out[2:]


def _dw_in(h1, pieces, tk):
    S = h1.shape[0]

    def body(*refs):
        h_ref, p_refs, o_ref = refs[0], refs[1:7], refs[7]

        @pl.when(pl.program_id(0) == 0)
        def _():
            o_ref[...] = jnp.zeros_like(o_ref)

        off = 0
        for p_ref, w in zip(p_refs, _DZ_WIDTHS):
            o_ref[:, off:off + w] += lax.dot_general(h_ref[...], p_ref[...].astype(BF16), TN, preferred_element_type=F32)
            off += w

    row = lambda w: pl.BlockSpec((tk, w), lambda k: (k, 0))
    return pl.pallas_call(
        body, name="dw_in", grid=(S // tk,),
        in_specs=[row(D_MODEL)] + [row(w) for w in _DZ_WIDTHS],
        out_specs=_full((D_MODEL, D_IN_PAD)),
        out_shape=jax.ShapeDtypeStruct((D_MODEL, D_IN_PAD), F32),
        compiler_params=_params(("arbitrary",)),
    )(h1, *pieces)


def _adamw_math(w, g, m, v):
    m = ADAM_B1 * m + (1.0 - ADAM_B1) * g
    v = ADAM_B2 * v + (1.0 - ADAM_B2) * (g * g)
    m_hat = m / (1.0 - ADAM_B1 ** ADAM_STEP)
    v_hat = v / (1.0 - ADAM_B2 ** ADAM_STEP)
    delta = -ADAM_LR * (m_hat / (jnp.sqrt(v_hat) + ADAM_EPS) + ADAM_WD * w)
    return delta, m, v


def _adamw(name, w, g, m, v):
    R, C = w.shape
    tr = _row_tile(R, 256)

    def body(w_ref, g_ref, m_ref, v_ref, go_ref, d_ref, nm_ref, nv_ref):
        g = g_ref[...]
        d, nm, nv = _adamw_math(w_ref[...], g, m_ref[...], v_ref[...])
        go_ref[...] = g
        d_ref[...] = d
        nm_ref[...] = nm
        nv_ref[...] = nv

    spec = pl.BlockSpec((tr, C), lambda i: (i, 0))
    return pl.pallas_call(
        body, name=name, grid=(R // tr,), in_specs=[spec] * 4, out_specs=[spec] * 4,
        out_shape=[jax.ShapeDtypeStruct((R, C), F32)] * 4,
        compiler_params=_params(("parallel",)),
    )(w, g, m, v)


def _pair_sum(name, grad, theirs, ids):
    q, half, C = theirs.shape
    tr = _row_tile(half, 256)
    nb = half // tr

    def body(ids_ref, a_ref, b_ref, sb_ref):
        sb_ref[...] = (a_ref[...] + b_ref[...]).astype(BF16)

    here = pl.BlockSpec((None, tr, C), lambda j, i, ids: (j, i, 0))
    return pl.pallas_call(
        body, name=name,
        grid_spec=pltpu.PrefetchScalarGridSpec(
            num_scalar_prefetch=1, grid=(q, nb),
            in_specs=[pl.BlockSpec((None, tr, C), lambda j, i, ids: (j, ids[1] * nb + i, 0)), here],
            out_specs=here),
        out_shape=jax.ShapeDtypeStruct((q, half, C), BF16),
        compiler_params=_params(("parallel", "parallel")),
    )(ids, grad, theirs)


def _chip_sum(name, grad, theirs, others, ids):
    _, half, C = theirs.shape
    tr = _row_tile(half, 256)
    nb = half // tr

    def body(ids_ref, a_ref, b_ref, o_ref, s_ref):
        s = a_ref[...] + b_ref[...]
        for j in range(3):
            s = s + o_ref[j].astype(F32)
        s_ref[...] = s

    return pl.pallas_call(
        body, name=name,
        grid_spec=pltpu.PrefetchScalarGridSpec(
            num_scalar_prefetch=1, grid=(nb,),
            in_specs=[pl.BlockSpec((None, tr, C), lambda i, ids: (ids[0], ids[1] * nb + i, 0)),
                      pl.BlockSpec((None, tr, C), lambda i, ids: (ids[0], i, 0)),
                      pl.BlockSpec((3, tr, C), lambda i, ids: (0, i, 0))],
            out_specs=pl.BlockSpec((tr, C), lambda i, ids: (ids[1] * nb + i, 0))),
        out_shape=jax.ShapeDtypeStruct((2 * half, C), F32),
        compiler_params=_params(("parallel",)),
    )(ids, grad, theirs, others)


def _place():
    return lax.axis_index("x"), lax.axis_index("y"), lax.axis_index("c")


def _other_chips(x, y):
    return [(1 - x, y), (x, 1 - y), (1 - x, 1 - y)]


_ANY = pl.BlockSpec(memory_space=pl.ANY)


def _gather_quarters(shards):
    n = len(shards)

    def body(*refs):
        start, hand_on, finish = _gather_ops(refs[:n], refs[n:2 * n], *refs[2 * n:])
        start()
        hand_on()
        finish()

    return pl.pallas_call(
        body, name="gather_weights",
        in_specs=[_ANY] * n, out_specs=[_ANY] * n,
        out_shape=_gather_shapes(shards), scratch_shapes=_gather_sems(n),
    )(*shards)


def _gather_shapes(shards):
    return [jax.ShapeDtypeStruct((4,) + s.shape, s.dtype) for s in shards]


def _gather_sems(n):
    return [pltpu.SemaphoreType.DMA((n, 3))] * 4 + [pltpu.SemaphoreType.DMA((n,))]


def _gather_ops(ins, outs, send_sems, recv_sems, pass_send_sems, pass_recv_sems, own_sems):
    n = len(ins)
    halved = [r.shape[0] % 32 == 0 for r in ins]

    def part(a, quarter, core):
        if not halved[a]:
            return outs[a].at[quarter]
        half = ins[a].shape[0] // 2
        return outs[a].at[quarter, pl.ds(core * half, half), :]

    def ici(a, j, quarter):
        x, y, c = _place()
        px, py = _other_chips(x, y)[j]
        src = ins[a]
        if halved[a]:
            half = src.shape[0] // 2
            src = src.at[pl.ds(c * half, half), :]
        return pltpu.make_async_remote_copy(src_ref=src, dst_ref=part(a, quarter, c), send_sem=send_sems.at[a, j],
                                            recv_sem=recv_sems.at[a, j], device_id=(px, py, c), device_id_type=MESH)

    def passed(a, j, core):
        x, y, c = _place()
        px, py = _other_chips(x, y)[j]
        half = part(a, 2 * px + py, core)
        return pltpu.make_async_remote_copy(src_ref=half, dst_ref=half, send_sem=pass_send_sems.at[a, j],
                                            recv_sem=pass_recv_sems.at[a, j], device_id=(x, y, 1 - c), device_id_type=MESH)

    def own(a):
        x, y, _ = _place()
        return pltpu.make_async_copy(ins[a], outs[a].at[2 * x + y], own_sems.at[a])

    def start():
        x, y, _ = _place()
        for a in range(n):
            for j in range(3):
                ici(a, j, 2 * x + y).start()
            own(a).start()

    def hand_on():
        x, y, c = _place()
        for a in range(n):
            for j, (px, py) in enumerate(_other_chips(x, y)):
                ici(a, j, 2 * px + py).wait_recv()
                if halved[a]:
                    passed(a, j, c).start()

    def finish():
        x, y, c = _place()
        for a in range(n):
            for j in range(3):
                if halved[a]:
                    passed(a, j, 1 - c).wait_recv()
                    passed(a, j, c).wait_send()
                ici(a, j, 2 * x + y).wait_send()
            own(a).wait()

    return start, hand_on, finish


def _swap_halves(grads, name):
    n = len(grads)

    def body(*refs):
        start, finish = _swap_ops(refs[:n], refs[n:2 * n], *refs[2 * n:])
        start()
        finish()

    return pl.pallas_call(
        body, name=name,
        in_specs=[_ANY] * n, out_specs=[_ANY] * n, out_shape=_swap_shapes(grads), scratch_shapes=_swap_sems(n),
    )(*grads)


def _swap_shapes(grads):
    return [jax.ShapeDtypeStruct((4, g.shape[1] // 2, g.shape[2]), F32) for g in grads]


def _swap_sems(n):
    return [pltpu.SemaphoreType.DMA((n,))] * 2


def _swap_ops(ins, outs, send_sems, recv_sems):
    def copy(a):
        x, y, c = _place()
        half = ins[a].shape[1] // 2
        return pltpu.make_async_remote_copy(src_ref=ins[a].at[:, pl.ds((1 - c) * half, half), :], dst_ref=outs[a],
                                            send_sem=send_sems.at[a], recv_sem=recv_sems.at[a],
                                            device_id=(x, y, 1 - c), device_id_type=MESH)

    def start():
        for a in range(len(ins)):
            copy(a).start()

    def finish():
        for a in range(len(ins)):
            copy(a).wait()

    return start, finish


def _scatter_shapes(sums16):
    return [jax.ShapeDtypeStruct((3,) + s.shape[1:], BF16) for s in sums16]


def _scatter_sems(n):
    return [pltpu.SemaphoreType.DMA((n, 3))] * 2


def _scatter_ops(ins, outs, send_sems, recv_sems):
    n = len(ins)

    def copy(a, j):
        x, y, c = _place()
        px, py = _other_chips(x, y)[j]
        return pltpu.make_async_remote_copy(src_ref=ins[a].at[2 * px + py], dst_ref=outs[a].at[j], send_sem=send_sems.at[a, j],
                                            recv_sem=recv_sems.at[a, j], device_id=(px, py, c), device_id_type=MESH)

    def start():
        for a in range(n):
            for j in range(3):
                copy(a, j).start()

    def finish():
        for a in range(n):
            for j in range(3):
                copy(a, j).wait()

    return start, finish


def _join_halves(fulls):
    n = len(fulls)

    def body(*refs):
        ins, outs = refs[:n], refs[n:2 * n]
        send_sems, recv_sems = refs[2 * n:]
        x, y, c = _place()
        started = []
        for a in range(n):
            half = ins[a].shape[0] // 2
            rows = pl.ds(c * half, half)
            cp = pltpu.make_async_remote_copy(src_ref=ins[a].at[rows, :], dst_ref=outs[a].at[rows, :], send_sem=send_sems.at[a],
                                              recv_sem=recv_sems.at[a], device_id=(x, y, 1 - c), device_id_type=MESH)
            cp.start()
            started.append(cp)
        for cp in started:
            cp.wait()

    return pl.pallas_call(
        body, name="join_halves",
        in_specs=[_ANY] * n, out_specs=[_ANY] * n,
        out_shape=[jax.ShapeDtypeStruct(f.shape, F32) for f in fulls],
        input_output_aliases={a: a for a in range(n)},
        scratch_shapes=[pltpu.SemaphoreType.DMA((n,)), pltpu.SemaphoreType.DMA((n,))],
    )(*fulls)


def _small_allreduce(g):
    R = g.shape[0]
    half = R // 2

    def body(g_ref, out_ref, other_s, chip_s, parts_s, send_sems, recv_sems):
        x, y, c = _place()
        mine = 2 * x + y
        rows = pl.ds(pl.multiple_of(c * half, SUBLANES), half)

        def to_other_core(src, dst, k):
            return pltpu.make_async_remote_copy(src_ref=src, dst_ref=dst, send_sem=send_sems.at[k], recv_sem=recv_sems.at[k],
                                                device_id=(x, y, 1 - c), device_id_type=MESH)

        swap = to_other_core(g_ref, other_s, 0)
        swap.start()
        swap.wait()
        chip_s[...] = g_ref[...] + other_s[...]
        parts_s[mine] = chip_s[rows, :]
        sends = []
        for j, (px, py) in enumerate(_other_chips(x, y)):
            cp = pltpu.make_async_remote_copy(src_ref=chip_s.at[rows, :], dst_ref=parts_s.at[mine], send_sem=send_sems.at[1 + j],
                                              recv_sem=recv_sems.at[1 + j], device_id=(px, py, c), device_id_type=MESH)
            cp.start()
            sends.append(cp)
        for cp in sends:
            cp.wait()
        out_ref[rows, :] = (parts_s[0] + parts_s[1]) + (parts_s[2] + parts_s[3])
        join = to_other_core(out_ref.at[rows, :], out_ref.at[rows, :], 4)
        join.start()
        join.wait()

    vm = pl.BlockSpec(memory_space=pltpu.VMEM)
    return pl.pallas_call(
        body, name="small_allreduce",
        in_specs=[vm], out_specs=vm, out_shape=jax.ShapeDtypeStruct((R, LANES), F32),
        scratch_shapes=[pltpu.VMEM((R, LANES), F32), pltpu.VMEM((R, LANES), F32), pltpu.VMEM((4, half, LANES), F32),
                        pltpu.SemaphoreType.DMA((5,)), pltpu.SemaphoreType.DMA((5,))],
        compiler_params=pltpu.CompilerParams(vmem_limit_bytes=VMEM_LIMIT),
    )(g)


def _adamw_small(ws, gs, ms, vs):
    n = len(ws)

    def body(*refs):
        for k in range(n):
            w_ref, g_ref, m_ref, v_ref = (refs[j * n + k] for j in range(4))
            d, nm, nv = _adamw_math(w_ref[...], g_ref[...], m_ref[...], v_ref[...])
            refs[4 * n + k][...] = d
            refs[5 * n + k][...] = nm
            refs[6 * n + k][...] = nv

    vm = pl.BlockSpec(memory_space=pltpu.VMEM)
    out = pl.pallas_call(
        body, name="adamw_small",
        in_specs=[vm] * (4 * n), out_specs=[vm] * (3 * n),
        out_shape=[jax.ShapeDtypeStruct(w.shape, F32) for w in ws] * 3,
        compiler_params=pltpu.CompilerParams(vmem_limit_bytes=VMEM_LIMIT),
    )(*ws, *gs, *ms, *vs)
    return out[:n], out[n:2 * n], out[2 * n:]


_SMALL = (("norm_mix_g", D_MODEL), ("f_bias", N_HEADS), ("sg_ln_g", D_HEADS), ("sg_w", N_HEADS * SG_BLOCK * SG_BLOCK),
          ("sg_b", N_HEADS * SG_BLOCK), ("norm_ffn_g", D_MODEL), ("w_conv", 3 * 2 * D_FF), ("b_conv", 2 * D_FF),
          ("norm_final_g", D_MODEL))
_PACKED = _SMALL + (("sq_err", D_MODEL),)


def _pack_small(parts):
    rows = []
    for name, size in _PACKED:
        flat = parts[name].reshape(-1).astype(F32)
        pad = (-size) % (SUBLANES * LANES)
        rows.append(jnp.pad(flat, (0, pad)).reshape(-1, LANES))
    packed = jnp.concatenate(rows, axis=0)
    return jnp.pad(packed, ((0, (-packed.shape[0]) % (2 * SUBLANES)), (0, 0)))


def _unpack_small(packed, shapes):
    out, r = {}, 0
    for name, size in _PACKED:
        nrows = (size + SUBLANES * LANES - 1) // (SUBLANES * LANES) * SUBLANES
        out[name] = packed[r:r + nrows].reshape(-1)[:size].reshape(shapes[name])
        r += nrows
    return out


def _local_step(x, target, g1, w_in, f_bias, sg_ln_g, sg_w, sg_b, g2, b_conv, g3, late_shards, ids):
    S = x.shape[0]
    tm = _row_tile(S, 512)
    tms = _row_tile(S, 256)
    tq = tm

    lane = jnp.arange(D_HEADS)
    seg_avg = jnp.where(lane[:, None] // HEAD_DIM == lane[None, :] // HEAD_DIM, 1.0 / HEAD_DIM, 0.0).astype(BF16)
    head_ind = (lane[:, None] // HEAD_DIM == jnp.arange(LANES)[None, :]).astype(BF16)
    pos_chunk = jnp.arange(SG_BLOCK) // CHUNK
    w_mask32 = jnp.where(pos_chunk[:, None] >= pos_chunk[None, :], sg_w, 0.0)
    w_mask = w_mask32.astype(BF16)
    w_mask_t = jnp.swapaxes(w_mask32, 1, 2).astype(BF16)
    ln_row = sg_ln_g.reshape(1, D_HEADS)
    b_full = jnp.repeat(sg_b.T, HEAD_DIM, axis=1)
    bias_row = jnp.pad(f_bias.reshape(1, N_HEADS), ((0, 0), (0, LANES - N_HEADS)))
    b_conv_row = b_conv.reshape(1, 2 * D_FF)

    z, f, h1 = _in_proj(x, g1, w_in, tm)
    c = _fox_prep(f, bias_row, tm)
    consts = _attn_consts()
    qa, ka, va, vat = _attn_pack(z, c, consts, tm)
    out_b, lse, gathered = _attn_fwd(qa, ka, vat, consts["place_t"], tq, late_shards)
    g_out, w_up_q, g_down, g_conv = gathered
    w_out = g_out.reshape(D_MODEL, D_MODEL)
    w_down = g_down.reshape(D_FF, D_MODEL)
    w_conv = jnp.concatenate([g_conv[q] for q in range(4)], axis=1)
    out_a = _gate_fwd(z, w_mask, ln_row, b_full, seg_avg, tm)
    x1, h2 = _mix_out(x, out_a, out_b, w_out, g2, tm)
    a = _up_proj(h2, w_up_q, tm)
    dx2, sq_err, dg3 = _ffn_fwd_loss(a, w_conv, b_conv_row, w_down, x1, g3, target, tm)

    dconv, y, dw_conv8, db_conv = _ffn_bwd_gate(dx2, a, w_conv, b_conv_row, w_down, tms)
    dact = _conv_bwd(dconv, w_conv, tms, D_FF)
    dw_down = _matmul_tn(y, dx2, "dw_down", D_FF // 2, D_MODEL, tm, quarters=(2, 1))
    dx1, dg2 = _up_bwd(dact, w_up_q, x1, g2, dx2, tm)
    dw_up_q = _matmul_tn(h2, dact, "dw_up", D_MODEL, 2 * D_FF // 4, tm, quarters=(1, 4))
    dcat = _out_bwd(dx1, w_out, tm)
    dw_out = _dw_out(out_a, out_b, dx1, tm)
    early = {"w_down": dw_down.reshape(4, D_FF // 4, D_MODEL), "w_up": dw_up_q,
             "w_out": dw_out.reshape(4, D_MODEL // 4, D_MODEL)}
    (dzu, dzv, dsg_w, dsg_b_t, dln), theirs = _gate_bwd(z, dcat, w_mask, w_mask_t, ln_row, b_full, seg_avg, head_ind, tm,
                                                        list(early.values()))
    early_sums = _chip_sums(early, theirs, ids)
    dop, qb, dopt, qbt = _attn_pack_grad(out_b, dcat, qa, lse, head_ind, consts, tm)
    dqp, dc_rows, dk, dv, dc_cols, landed = _attn_bwd(qb, qbt, ka, va, dop, dopt, consts, tq,
                                                      [s16 for _, s16 in early_sums.values()])
    early_parts = {k: (s32, got) for (k, (s32, _)), got in zip(early_sums.items(), landed)}
    dq = _attn_unpack(dqp, consts, tm)
    dc_rows = jnp.concatenate([dc_rows[g][:, :GROUP_HEADS] for g in range(GROUPS)], axis=1)
    dc_cols = jnp.concatenate([dc_cols[g][:GROUP_HEADS] for g in range(GROUPS)], axis=0).T
    dc = jnp.pad(dc_rows - dc_cols, ((0, 0), (0, LANES - N_HEADS)))
    df, dbias = _fox_bwd(dc, f, bias_row, tm)
    pieces = (dzu, dzv, dq, dk, dv, df)
    dw_in = _dw_in(h1, pieces, tm)[:, :D_IN].reshape(D_MODEL, 4, D_IN // 4).transpose(1, 0, 2)
    (w_in_sum, w_in_sum16), = _chip_sums({"w_in": dw_in}, _swap_halves([dw_in], "swap_halves"), ids).values()
    dx, dg1, (w_in_landed,) = _in_bwd(pieces, w_in, x, g1, dx1, tm, [w_in_sum16])

    grads = {
        "norm_mix_g": dg1, "f_bias": dbias[:, :N_HEADS], "sg_ln_g": dln, "sg_w": dsg_w, "sg_b": dsg_b_t[:, :N_HEADS].T,
        "norm_ffn_g": dg2, "w_conv": dw_conv8[:3], "b_conv": db_conv, "norm_final_g": dg3,
    }
    return sq_err, dx, grads, {**early_parts, "w_in": (w_in_sum, w_in_landed)}


def _chip_sums(grads_q, theirs, ids):
    return {k: ((g, t), _pair_sum("pair_sum_" + k, g, t, ids)) for (k, g), t in zip(grads_q.items(), theirs)}


def _finish_reduction(parts, ids):
    names = list(parts)
    fulls = [_chip_sum("chip_sum_" + k, g, t, got, ids) for k, ((g, t), got) in parts.items()]
    return dict(zip(names, _join_halves(fulls)))


def kernel(x, norm_mix_g, w_in, f_bias, sg_ln_g, sg_w, sg_b, w_out, norm_ffn_g, w_up, w_conv, b_conv, w_down, norm_final_g, loss_target, m_norm_mix_g, m_w_in, m_f_bias, m_sg_ln_g, m_sg_w, m_sg_b, m_w_out, m_norm_ffn_g, m_w_up, m_w_conv, m_b_conv, m_w_down, m_norm_final_g, v_norm_mix_g, v_w_in, v_f_bias, v_sg_ln_g, v_sg_w, v_sg_b, v_w_out, v_norm_ffn_g, v_w_up, v_w_conv, v_b_conv, v_w_down, v_norm_final_g):
    args = dict(locals())
    quarter = 2 * lax.axis_index("x") + lax.axis_index("y")
    ids = jnp.stack([quarter, lax.axis_index("c")]).astype(jnp.int32)
    wq_conv = w_conv.shape[-1]

    g_in = _gather_quarters([w_in[0].astype(BF16)])[0]
    w_in_full = jnp.pad(jnp.concatenate([g_in[q] for q in range(4)], axis=1), ((0, 0), (0, D_IN_PAD - D_IN)))
    late_shards = [w_out[0].astype(BF16), w_up[0].astype(BF16), w_down[0].astype(BF16), w_conv[0]]

    sq_err, dx, grads, parts = _local_step(
        x[0], loss_target[0], norm_mix_g, w_in_full, f_bias[0], sg_ln_g[0], sg_w[0], sg_b[0], norm_ffn_g, b_conv[0],
        norm_final_g.reshape(1, D_MODEL), late_shards, ids)
    big = _finish_reduction(parts, ids)

    out = {"grad_x": dx[None]}
    for k in ("w_in", "w_out", "w_up", "w_down"):
        g, d, nm, nv = _adamw("adamw_" + k, args[k][0], big[k], args["m_" + k][0], args["v_" + k][0])
        out["grad_" + k], out["delta_" + k], out["new_m_" + k], out["new_v_" + k] = g[None], d[None], nm[None], nv[None]

    small_names = [n for n, _ in _SMALL]
    shapes = {n: (3, 4 * wq_conv) if n == "w_conv" else args[n].shape for n in small_names}
    shapes["sq_err"] = sq_err.shape
    g_small = _unpack_small(_small_allreduce(_pack_small({**{n: grads[n] for n in small_names}, "sq_err": sq_err})), shapes)
    out["loss"] = 0.5 * jnp.sum(g_small.pop("sq_err")) / D_MODEL
    g_small["w_conv"] = lax.dynamic_slice(g_small["w_conv"], (0, quarter * wq_conv), (3, wq_conv))[None]
    flat2d = lambda t: t.reshape(-1, t.shape[-1])
    updated = _adamw_small(*[[flat2d(src[p + n]) for n in small_names] for src, p in
                             ((args, ""), (g_small, ""), (args, "m_"), (args, "v_"))])
    for n, g in g_small.items():
        out["grad_" + n] = g
    for prefix, arrs in zip(("delta_", "new_m_", "new_v_"), updated):
        for n, t in zip(small_names, arrs):
            out[prefix + n] = t.reshape(args[n].shape)

    weights = ["norm_mix_g", "w_in", "f_bias", "sg_ln_g", "sg_w", "sg_b", "w_out", "norm_ffn_g", "w_up", "w_conv", "b_conv",
               "w_down", "norm_final_g"]
    return (out["loss"], out["grad_x"], *[out[p + n] for p in ("grad_", "delta_", "new_m_", "new_v_") for n in weights])
```
